```python
import math
import jax, jax.numpy as jnp
from jax import lax
import numpy as np

D_MODEL = 1024
BATCH = 8
SEQ = 2048
DEPTH = 1

PLE_DIM = 256
SSD_HEADS = 16
SSD_HEAD_DIM = 64
SSD_INNER = SSD_HEADS * SSD_HEAD_DIM
SSD_GROUPS = 2
SSD_STATE = 128
SSD_CONV = 4
SSD_CHUNK = 128
MLA_HEADS = 16
MLA_Q_RANK = 384
MLA_KV_RANK = 256
MLA_NOPE = 64
MLA_ROPE = 32
MLA_V = 64
MLA_OUT = MLA_HEADS * MLA_V
ROPE_BASE = 10000.0
Q_BLOCK = 128
MIX_WIDTH = SSD_INNER + MLA_OUT
SSD_XBC = SSD_INNER + 2 * SSD_GROUPS * SSD_STATE
IN_WIDTH = SSD_INNER + SSD_XBC + SSD_HEADS + MLA_Q_RANK + MLA_KV_RANK + MLA_ROPE
D_FF = -(-8 * D_MODEL // (3 * 256)) * 256
DEEPNORM_ALPHA = (2 * DEPTH) ** 0.25
DEEPNORM_BETA = (8 * DEPTH) ** -0.25
EPS = 1e-6

kernel_name = "hybrid_ssd_mla_deepnorm_ple_layer"


def rmsnorm(u, g):
    uf = u.astype(jnp.float32)
    out = uf * lax.rsqrt(jnp.mean(uf * uf, axis=-1, keepdims=True) + EPS)
    return (out * g.astype(jnp.float32)).astype(u.dtype)


def layernorm(u, g, b):
    uf = u.astype(jnp.float32)
    mu = jnp.mean(uf, axis=-1, keepdims=True)
    d = uf - mu
    var = jnp.mean(d * d, axis=-1, keepdims=True)
    out = d * lax.rsqrt(var + 1e-5) * g.astype(jnp.float32) + b.astype(jnp.float32)
    return out.astype(u.dtype)


def rope_tables(positions):
    inv_freq = 1.0 / (ROPE_BASE ** (jnp.arange(0, MLA_ROPE, 2, dtype=jnp.float32) / MLA_ROPE))
    ang = positions.astype(jnp.float32)[..., None] * inv_freq
    return jnp.cos(ang), jnp.sin(ang)


def apply_rope(u, cos, sin):
    cos = cos.astype(u.dtype)
    sin = sin.astype(u.dtype)
    u1, u2 = jnp.split(u, 2, axis=-1)
    return jnp.concatenate([u1 * cos - u2 * sin, u2 * cos + u1 * sin], axis=-1)


def causal_depthwise_conv(u, w, bias):
    c = u.shape[-1]
    out = lax.conv_general_dilated(
        u, w[:, None, :].astype(u.dtype), window_strides=(1,),
        padding=((SSD_CONV - 1, 0),), dimension_numbers=("NWC", "WIO", "NWC"),
        feature_group_count=c)
    return out + bias.astype(u.dtype)


def ssd_mixer(z, xBC, dt_raw, conv_w, conv_b, dt_bias, A_log, D_skip, norm_w):
    b, s, _ = xBC.shape
    G, E, P, N, L = SSD_GROUPS, SSD_HEADS // SSD_GROUPS, SSD_HEAD_DIM, SSD_STATE, SSD_CHUNK
    nc = s // L
    f32 = jnp.float32
    xBC = jax.nn.silu(causal_depthwise_conv(xBC, conv_w, conv_b))
    xs, Bm, Cm = jnp.split(xBC, [SSD_INNER, SSD_INNER + G * N], axis=-1)
    xs = xs.astype(f32).reshape(b, nc, L, G, E, P)
    Bm = Bm.astype(f32).reshape(b, nc, L, G, N)
    Cm = Cm.astype(f32).reshape(b, nc, L, G, N)
    dt = jax.nn.softplus(dt_raw.astype(f32) + dt_bias.astype(f32))
    A = -jnp.exp(A_log.astype(f32))
    dt_c = dt.reshape(b, nc, L, G, E)
    dA_cs = jnp.cumsum((dt * A).reshape(b, nc, L, G, E).transpose(0, 1, 3, 4, 2), axis=-1)
    X = xs * dt_c[..., None]
    causal = jnp.tril(jnp.ones((L, L), dtype=bool))
    seg = dA_cs[..., :, None] - dA_cs[..., None, :]
    Lmat = jnp.exp(jnp.where(causal, seg, -jnp.inf))
    CB = jnp.einsum("bclgn,bcsgn->bcgls", Cm, Bm)
    y_diag = jnp.einsum("bcgls,bcgels,bcsgep->bclgep", CB, Lmat, X)
    decay_states = jnp.exp(dA_cs[..., -1:] - dA_cs)
    states = jnp.einsum("bclgn,bcgel,bclgep->bcgepn", Bm, decay_states, X)
    chunk_decay = jnp.exp(dA_cs[..., -1])

    def step(carry, inp):
        dec, st = inp
        return carry * dec[..., None, None] + st, carry

    init = jnp.zeros((b, G, E, P, N), f32)
    _, prev = lax.scan(step, init, (chunk_decay.transpose(1, 0, 2, 3), states.transpose(1, 0, 2, 3, 4, 5)))
    prev = prev.transpose(1, 0, 2, 3, 4, 5)
    y_off = jnp.einsum("bclgn,bcgepn,bcgel->bclgep", Cm, prev, jnp.exp(dA_cs))
    y = y_diag + y_off + D_skip.astype(f32).reshape(G, E)[:, :, None] * xs
    y = y.reshape(b, s, SSD_INNER)
    y = rmsnorm(y * jax.nn.silu(z.astype(f32)), norm_w)
    return y.astype(z.dtype)


def mla_mixer(q_c, kv_c, k_rope, cos, sin, q_norm_w, w_q_b, kv_norm_w, w_kv_b, out_norm_w):
    b, s, _ = q_c.shape
    H = MLA_HEADS
    q = (rmsnorm(q_c, q_norm_w) @ w_q_b).reshape(b, s, H, MLA_NOPE + MLA_ROPE)
    q_nope, q_rope = jnp.split(q, [MLA_NOPE], axis=-1)
    kv = (rmsnorm(kv_c, kv_norm_w) @ w_kv_b).reshape(b, s, H, MLA_NOPE + MLA_V)
    k_nope, v = jnp.split(kv, [MLA_NOPE], axis=-1)
    q_rope = apply_rope(q_rope, cos[:, :, None, :], sin[:, :, None, :])
    k_rope = apply_rope(k_rope, cos, sin)
    scale = 1.0 / math.sqrt(MLA_NOPE + MLA_ROPE)
    nb = s // Q_BLOCK
    qn_blocks = q_nope.reshape(b, nb, Q_BLOCK, H, MLA_NOPE).transpose(1, 0, 2, 3, 4)
    qr_blocks = q_rope.reshape(b, nb, Q_BLOCK, H, MLA_ROPE).transpose(1, 0, 2, 3, 4)
    key_idx = jnp.arange(s)

    def attend(args):
        qn, qr, blk = args
        sc = (jnp.einsum("bqhd,bkhd->bhqk", qn, k_nope).astype(jnp.float32)
              + jnp.einsum("bqhr,bkr->bhqk", qr, k_rope).astype(jnp.float32)) * scale
        q_idx = blk * Q_BLOCK + jnp.arange(Q_BLOCK)
        sc = jnp.where(q_idx[:, None] >= key_idx[None, :], sc, -jnp.inf)
        pr = jax.nn.softmax(sc, axis=-1).astype(v.dtype)
        return jnp.einsum("bhqk,bkhd->bqhd", pr, v)

    out = lax.map(attend, (qn_blocks, qr_blocks, jnp.arange(nb)))
    out = out.transpose(1, 0, 2, 3, 4).reshape(b, s, MLA_OUT)
    return rmsnorm(out, out_norm_w)


def _fwd_setup_inputs(seed: int = 0) -> dict:
    key = jax.random.key(seed)
    ks = iter(jax.random.split(key, 40))
    f32 = jnp.float32

    def w(shape, fan_in, scale=1.0):
        return jax.random.normal(next(ks), shape, f32) * (fan_in ** -0.5) * scale

    def gain(shape):
        return 1.0 + 0.02 * jax.random.normal(next(ks), shape, f32)

    def small(shape):
        return 0.02 * jax.random.normal(next(ks), shape, f32)

    x = jax.random.normal(next(ks), (BATCH, SEQ, D_MODEL), f32)
    p = jax.random.normal(next(ks), (DEPTH, BATCH, SEQ, PLE_DIM), f32)
    offsets = jax.random.randint(next(ks), (BATCH, 1), 0, 1024, dtype=jnp.int32)
    positions = (jnp.arange(SEQ, dtype=jnp.int32)[None, :] + offsets).astype(jnp.int32)

    dt0 = jnp.exp(jax.random.uniform(next(ks), (DEPTH, SSD_HEADS), f32) * (math.log(0.1) - math.log(0.001)) + math.log(0.001))
    ssd_dt_bias = dt0 + jnp.log(-jnp.expm1(-dt0))
    ssd_A_log = jnp.log(jax.random.uniform(next(ks), (DEPTH, SSD_HEADS), f32, 1.0, 16.0))

    return {
        "x": x,
        "p": p,
        "positions": positions,
        "w_in": w((DEPTH, D_MODEL, IN_WIDTH), D_MODEL),
        "ssd_conv_w": w((DEPTH, SSD_CONV, SSD_XBC), SSD_CONV),
        "ssd_conv_b": small((DEPTH, SSD_XBC)),
        "ssd_dt_bias": ssd_dt_bias,
        "ssd_A_log": ssd_A_log,
        "ssd_D": gain((DEPTH, SSD_HEADS)),
        "ssd_norm_w": gain((DEPTH, SSD_INNER)),
        "mla_q_norm_w": gain((DEPTH, MLA_Q_RANK)),
        "mla_w_q_b": w((DEPTH, MLA_Q_RANK, MLA_HEADS * (MLA_NOPE + MLA_ROPE)), MLA_Q_RANK),
        "mla_kv_norm_w": gain((DEPTH, MLA_KV_RANK)),
        "mla_w_kv_b": w((DEPTH, MLA_KV_RANK, MLA_HEADS * (MLA_NOPE + MLA_V)), MLA_KV_RANK),
        "mla_out_norm_w": gain((DEPTH, MLA_OUT)),
        "w_out": w((DEPTH, MIX_WIDTH, D_MODEL), MIX_WIDTH, DEEPNORM_BETA),
        "ln_mix_g": gain((DEPTH, D_MODEL)),
        "ln_mix_b": small((DEPTH, D_MODEL)),
        "w_ffn_gate": w((DEPTH, D_MODEL, D_FF), D_MODEL),
        "w_ffn_up": w((DEPTH, D_MODEL, D_FF), D_MODEL),
        "w_ffn_down": w((DEPTH, D_FF, D_MODEL), D_FF, DEEPNORM_BETA),
        "w_ple_gate": w((DEPTH, D_MODEL, D_MODEL), D_MODEL),
        "w_ple_proj": w((DEPTH, PLE_DIM, D_MODEL), PLE_DIM, DEEPNORM_BETA),
        "ln_ffn_g": gain((DEPTH, D_MODEL)),
        "ln_ffn_b": small((DEPTH, D_MODEL)),
    }


def _fwd_reference(x, p, positions, w_in, ssd_conv_w, ssd_conv_b, ssd_dt_bias, ssd_A_log, ssd_D, ssd_norm_w,
              mla_q_norm_w, mla_w_q_b, mla_kv_norm_w, mla_w_kv_b, mla_out_norm_w, w_out,
              ln_mix_g, ln_mix_b, w_ffn_gate, w_ffn_up, w_ffn_down, w_ple_gate, w_ple_proj,
              ln_ffn_g, ln_ffn_b):
    s0 = SSD_INNER
    s1 = s0 + SSD_XBC
    s2 = s1 + SSD_HEADS
    s3 = s2 + MLA_Q_RANK
    s4 = s3 + MLA_KV_RANK
    splits = [s0, s1, s2, s3, s4]
    cos, sin = rope_tables(positions)
    h = x
    for i in range(DEPTH):
        proj = h @ w_in[i]
        z, xBC, dt_raw, q_c, kv_c, k_rope = jnp.split(proj, splits, axis=-1)
        y_ssd = ssd_mixer(z, xBC, dt_raw, ssd_conv_w[i], ssd_conv_b[i], ssd_dt_bias[i],
                          ssd_A_log[i], ssd_D[i], ssd_norm_w[i])
        y_mla = mla_mixer(q_c, kv_c, k_rope, cos, sin, mla_q_norm_w[i], mla_w_q_b[i],
                          mla_kv_norm_w[i], mla_w_kv_b[i], mla_out_norm_w[i])
        mix = jnp.concatenate([y_ssd, y_mla], axis=-1) @ w_out[i]
        h = layernorm(DEEPNORM_ALPHA * h + mix, ln_mix_g[i], ln_mix_b[i])
        ffn = (jax.nn.silu(h @ w_ffn_gate[i]) * (h @ w_ffn_up[i])) @ w_ffn_down[i]
        ple = jax.nn.sigmoid(h @ w_ple_gate[i]) * (p[i] @ w_ple_proj[i])
        h = layernorm(DEEPNORM_ALPHA * h + ffn + ple, ln_ffn_g[i], ln_ffn_b[i])
    return h


import jax as _jax
import jax.numpy as _jnp

TWIN_FORMAT = 'train_step'
FWD_PARAMS = ['x', 'p', 'positions', 'w_in', 'ssd_conv_w', 'ssd_conv_b', 'ssd_dt_bias', 'ssd_A_log', 'ssd_D', 'ssd_norm_w', 'mla_q_norm_w', 'mla_w_q_b', 'mla_kv_norm_w', 'mla_w_kv_b', 'mla_out_norm_w', 'w_out', 'ln_mix_g', 'ln_mix_b', 'w_ffn_gate', 'w_ffn_up', 'w_ffn_down', 'w_ple_gate', 'w_ple_proj', 'ln_ffn_g', 'ln_ffn_b']
TWIN_WEIGHTS = ['w_in', 'ssd_conv_w', 'ssd_conv_b', 'ssd_dt_bias', 'ssd_A_log', 'ssd_D', 'ssd_norm_w', 'mla_q_norm_w', 'mla_w_q_b', 'mla_kv_norm_w', 'mla_w_kv_b', 'mla_out_norm_w', 'w_out', 'ln_mix_g', 'ln_mix_b', 'w_ffn_gate', 'w_ffn_up', 'w_ffn_down', 'w_ple_gate', 'w_ple_proj', 'ln_ffn_g', 'ln_ffn_b']
TWIN_DIFF_INPUT = 'x'
TWIN_INPUTS = ['x', 'p', 'positions', 'w_in', 'ssd_conv_w', 'ssd_conv_b', 'ssd_dt_bias', 'ssd_A_log', 'ssd_D', 'ssd_norm_w', 'mla_q_norm_w', 'mla_w_q_b', 'mla_kv_norm_w', 'mla_w_kv_b', 'mla_out_norm_w', 'w_out', 'ln_mix_g', 'ln_mix_b', 'w_ffn_gate', 'w_ffn_up', 'w_ffn_down', 'w_ple_gate', 'w_ple_proj', 'ln_ffn_g', 'ln_ffn_b', 'loss_target', 'm_w_in', 'm_ssd_conv_w', 'm_ssd_conv_b', 'm_ssd_dt_bias', 'm_ssd_A_log', 'm_ssd_D', 'm_ssd_norm_w', 'm_mla_q_norm_w', 'm_mla_w_q_b', 'm_mla_kv_norm_w', 'm_mla_w_kv_b', 'm_mla_out_norm_w', 'm_w_out', 'm_ln_mix_g', 'm_ln_mix_b', 'm_w_ffn_gate', 'm_w_ffn_up', 'm_w_ffn_down', 'm_w_ple_gate', 'm_w_ple_proj', 'm_ln_ffn_g', 'm_ln_ffn_b', 'v_w_in', 'v_ssd_conv_w', 'v_ssd_conv_b', 'v_ssd_dt_bias', 'v_ssd_A_log', 'v_ssd_D', 'v_ssd_norm_w', 'v_mla_q_norm_w', 'v_mla_w_q_b', 'v_mla_kv_norm_w', 'v_mla_w_kv_b', 'v_mla_out_norm_w', 'v_w_out', 'v_ln_mix_g', 'v_ln_mix_b', 'v_w_ffn_gate', 'v_w_ffn_up', 'v_w_ffn_down', 'v_w_ple_gate', 'v_w_ple_proj', 'v_ln_ffn_g', 'v_ln_ffn_b']
TWIN_OUTPUTS = ['loss', 'grad_x', 'grad_w_in', 'grad_ssd_conv_w', 'grad_ssd_conv_b', 'grad_ssd_dt_bias', 'grad_ssd_A_log', 'grad_ssd_D', 'grad_ssd_norm_w', 'grad_mla_q_norm_w', 'grad_mla_w_q_b', 'grad_mla_kv_norm_w', 'grad_mla_w_kv_b', 'grad_mla_out_norm_w', 'grad_w_out', 'grad_ln_mix_g', 'grad_ln_mix_b', 'grad_w_ffn_gate', 'grad_w_ffn_up', 'grad_w_ffn_down', 'grad_w_ple_gate', 'grad_w_ple_proj', 'grad_ln_ffn_g', 'grad_ln_ffn_b', 'delta_w_in', 'delta_ssd_conv_w', 'delta_ssd_conv_b', 'delta_ssd_dt_bias', 'delta_ssd_A_log', 'delta_ssd_D', 'delta_ssd_norm_w', 'delta_mla_q_norm_w', 'delta_mla_w_q_b', 'delta_mla_kv_norm_w', 'delta_mla_w_kv_b', 'delta_mla_out_norm_w', 'delta_w_out', 'delta_ln_mix_g', 'delta_ln_mix_b', 'delta_w_ffn_gate', 'delta_w_ffn_up', 'delta_w_ffn_down', 'delta_w_ple_gate', 'delta_w_ple_proj', 'delta_ln_ffn_g', 'delta_ln_ffn_b', 'new_m_w_in', 'new_m_ssd_conv_w', 'new_m_ssd_conv_b', 'new_m_ssd_dt_bias', 'new_m_ssd_A_log', 'new_m_ssd_D', 'new_m_ssd_norm_w', 'new_m_mla_q_norm_w', 'new_m_mla_w_q_b', 'new_m_mla_kv_norm_w', 'new_m_mla_w_kv_b', 'new_m_mla_out_norm_w', 'new_m_w_out', 'new_m_ln_mix_g', 'new_m_ln_mix_b', 'new_m_w_ffn_gate', 'new_m_w_ffn_up', 'new_m_w_ffn_down', 'new_m_w_ple_gate', 'new_m_w_ple_proj', 'new_m_ln_ffn_g', 'new_m_ln_ffn_b', 'new_v_w_in', 'new_v_ssd_conv_w', 'new_v_ssd_conv_b', 'new_v_ssd_dt_bias', 'new_v_ssd_A_log', 'new_v_ssd_D', 'new_v_ssd_norm_w', 'new_v_mla_q_norm_w', 'new_v_mla_w_q_b', 'new_v_mla_kv_norm_w', 'new_v_mla_w_kv_b', 'new_v_mla_out_norm_w', 'new_v_w_out', 'new_v_ln_mix_g', 'new_v_ln_mix_b', 'new_v_w_ffn_gate', 'new_v_w_ffn_up', 'new_v_w_ffn_down', 'new_v_w_ple_gate', 'new_v_w_ple_proj', 'new_v_ln_ffn_g', 'new_v_ln_ffn_b']
TWIN_LEAF_KINDS = {'loss': 'loss', 'grad_x': 'grad_x', 'grad_w_in': 'grad_w', 'grad_ssd_conv_w': 'grad_w', 'grad_ssd_conv_b': 'grad_w', 'grad_ssd_dt_bias': 'grad_w', 'grad_ssd_A_log': 'grad_w', 'grad_ssd_D': 'grad_w', 'grad_ssd_norm_w': 'grad_w', 'grad_mla_q_norm_w': 'grad_w', 'grad_mla_w_q_b': 'grad_w', 'grad_mla_kv_norm_w': 'grad_w', 'grad_mla_w_kv_b': 'grad_w', 'grad_mla_out_norm_w': 'grad_w', 'grad_w_out': 'grad_w', 'grad_ln_mix_g': 'grad_w', 'grad_ln_mix_b': 'grad_w', 'grad_w_ffn_gate': 'grad_w', 'grad_w_ffn_up': 'grad_w', 'grad_w_ffn_down': 'grad_w', 'grad_w_ple_gate': 'grad_w', 'grad_w_ple_proj': 'grad_w', 'grad_ln_ffn_g': 'grad_w', 'grad_ln_ffn_b': 'grad_w', 'delta_w_in': 'delta_w', 'delta_ssd_conv_w': 'delta_w', 'delta_ssd_conv_b': 'delta_w', 'delta_ssd_dt_bias': 'delta_w', 'delta_ssd_A_log': 'delta_w', 'delta_ssd_D': 'delta_w', 'delta_ssd_norm_w': 'delta_w', 'delta_mla_q_norm_w': 'delta_w', 'delta_mla_w_q_b': 'delta_w', 'delta_mla_kv_norm_w': 'delta_w', 'delta_mla_w_kv_b': 'delta_w', 'delta_mla_out_norm_w': 'delta_w', 'delta_w_out': 'delta_w', 'delta_ln_mix_g': 'delta_w', 'delta_ln_mix_b': 'delta_w', 'delta_w_ffn_gate': 'delta_w', 'delta_w_ffn_up': 'delta_w', 'delta_w_ffn_down': 'delta_w', 'delta_w_ple_gate': 'delta_w', 'delta_w_ple_proj': 'delta_w', 'delta_ln_ffn_g': 'delta_w', 'delta_ln_ffn_b': 'delta_w', 'new_m_w_in': 'new_m', 'new_m_ssd_conv_w': 'new_m', 'new_m_ssd_conv_b': 'new_m', 'new_m_ssd_dt_bias': 'new_m', 'new_m_ssd_A_log': 'new_m', 'new_m_ssd_D': 'new_m', 'new_m_ssd_norm_w': 'new_m', 'new_m_mla_q_norm_w': 'new_m', 'new_m_mla_w_q_b': 'new_m', 'new_m_mla_kv_norm_w': 'new_m', 'new_m_mla_w_kv_b': 'new_m', 'new_m_mla_out_norm_w': 'new_m', 'new_m_w_out': 'new_m', 'new_m_ln_mix_g': 'new_m', 'new_m_ln_mix_b': 'new_m', 'new_m_w_ffn_gate': 'new_m', 'new_m_w_ffn_up': 'new_m', 'new_m_w_ffn_down': 'new_m', 'new_m_w_ple_gate': 'new_m', 'new_m_w_ple_proj': 'new_m', 'new_m_ln_ffn_g': 'new_m', 'new_m_ln_ffn_b': 'new_m', 'new_v_w_in': 'new_v', 'new_v_ssd_conv_w': 'new_v', 'new_v_ssd_conv_b': 'new_v', 'new_v_ssd_dt_bias': 'new_v', 'new_v_ssd_A_log': 'new_v', 'new_v_ssd_D': 'new_v', 'new_v_ssd_norm_w': 'new_v', 'new_v_mla_q_norm_w': 'new_v', 'new_v_mla_w_q_b': 'new_v', 'new_v_mla_kv_norm_w': 'new_v', 'new_v_mla_w_kv_b': 'new_v', 'new_v_mla_out_norm_w': 'new_v', 'new_v_w_out': 'new_v', 'new_v_ln_mix_g': 'new_v', 'new_v_ln_mix_b': 'new_v', 'new_v_w_ffn_gate': 'new_v', 'new_v_w_ffn_up': 'new_v', 'new_v_w_ffn_down': 'new_v', 'new_v_w_ple_gate': 'new_v', 'new_v_w_ple_proj': 'new_v', 'new_v_ln_ffn_g': 'new_v', 'new_v_ln_ffn_b': 'new_v'}


def _forward(args):
    return _fwd_reference(*[args[k] for k in FWD_PARAMS])


def _output_shape():
    out = _jax.eval_shape(lambda: _forward(_fwd_setup_inputs(0)))
    return out.shape, out.dtype

N_MICROBATCH = 1
ADAM_LR = 0.001
ADAM_B1 = 0.9
ADAM_B2 = 0.999
ADAM_EPS = 1e-08
ADAM_WD = 0.01
ADAM_STEP = 10
PER_EXAMPLE_BATCH_AXIS = {'x': 0, 'p': 1, 'positions': 0, 'loss_target': 0}
SHARED_INPUTS = []
_WEIGHT_DTYPES = {'w_in': _jnp.float32, 'ssd_conv_w': _jnp.float32, 'ssd_conv_b': _jnp.float32, 'ssd_dt_bias': _jnp.float32, 'ssd_A_log': _jnp.float32, 'ssd_D': _jnp.float32, 'ssd_norm_w': _jnp.float32, 'mla_q_norm_w': _jnp.float32, 'mla_w_q_b': _jnp.float32, 'mla_kv_norm_w': _jnp.float32, 'mla_w_kv_b': _jnp.float32, 'mla_out_norm_w': _jnp.float32, 'w_out': _jnp.float32, 'ln_mix_g': _jnp.float32, 'ln_mix_b': _jnp.float32, 'w_ffn_gate': _jnp.float32, 'w_ffn_up': _jnp.float32, 'w_ffn_down': _jnp.float32, 'w_ple_gate': _jnp.float32, 'w_ple_proj': _jnp.float32, 'ln_ffn_g': _jnp.float32, 'ln_ffn_b': _jnp.float32}
MOMENT_SCALE = {'w_in': 5.011135e-02, 'ssd_conv_w': 3.582686e-02, 'ssd_conv_b': 5.335818e-02, 'ssd_dt_bias': 9.683354e-02, 'ssd_A_log': 2.714500e-01, 'ssd_D': 2.647727e-01, 'ssd_norm_w': 4.689318e-02, 'mla_q_norm_w': 6.946823e-02, 'mla_w_q_b': 3.338329e-02, 'mla_kv_norm_w': 1.492064e-01, 'mla_w_kv_b': 3.999618e-02, 'mla_out_norm_w': 4.332435e-02, 'w_out': 1.002598e-01, 'ln_mix_g': 4.888751e-01, 'ln_mix_b': 2.685733e-01, 'w_ffn_gate': 2.158740e-02, 'w_ffn_up': 2.091699e-02, 'w_ffn_down': 5.836403e-02, 'w_ple_gate': 1.227736e-02, 'w_ple_proj': 5.294165e-02, 'ln_ffn_g': 1.600529e+01, 'ln_ffn_b': 1.144861e+00}


def _to_microbatches(a, axis):
    t = _jnp.moveaxis(a, axis, 0)
    t = t.reshape((N_MICROBATCH, t.shape[0] // N_MICROBATCH) + t.shape[1:])
    return _jnp.moveaxis(t, 1, axis + 1)


def setup_inputs(seed: int = 0) -> dict:
    inp = _fwd_setup_inputs(seed)
    key = _jax.random.fold_in(_jax.random.key(seed), 7919)
    shape, _ = _output_shape()
    out = dict(inp)
    out["loss_target"] = _jax.random.normal(_jax.random.fold_in(key, 0), shape, _jnp.float32)
    for i, name in enumerate(TWIN_WEIGHTS):
        w = inp[name].astype(_jnp.float32)
        if MOMENT_SCALE is None:
            s = _jnp.sqrt(_jnp.mean(_jnp.square(w)) + 1e-30)
        else:
            s = MOMENT_SCALE[name]
        km, kv = _jax.random.split(_jax.random.fold_in(key, i + 1))
        out[name] = w
        out["m_" + name] = s * _jax.random.normal(km, w.shape, _jnp.float32)
        out["v_" + name] = (s * s) * _jax.random.uniform(kv, w.shape, _jnp.float32, 0.5, 1.5)
    if N_MICROBATCH > 1:
        for name, axis in PER_EXAMPLE_BATCH_AXIS.items():
            out[name] = _to_microbatches(out[name], axis)
    return {'x': out['x'], 'p': out['p'], 'positions': out['positions'], 'w_in': out['w_in'], 'ssd_conv_w': out['ssd_conv_w'], 'ssd_conv_b': out['ssd_conv_b'], 'ssd_dt_bias': out['ssd_dt_bias'], 'ssd_A_log': out['ssd_A_log'], 'ssd_D': out['ssd_D'], 'ssd_norm_w': out['ssd_norm_w'], 'mla_q_norm_w': out['mla_q_norm_w'], 'mla_w_q_b': out['mla_w_q_b'], 'mla_kv_norm_w': out['mla_kv_norm_w'], 'mla_w_kv_b': out['mla_w_kv_b'], 'mla_out_norm_w': out['mla_out_norm_w'], 'w_out': out['w_out'], 'ln_mix_g': out['ln_mix_g'], 'ln_mix_b': out['ln_mix_b'], 'w_ffn_gate': out['w_ffn_gate'], 'w_ffn_up': out['w_ffn_up'], 'w_ffn_down': out['w_ffn_down'], 'w_ple_gate': out['w_ple_gate'], 'w_ple_proj': out['w_ple_proj'], 'ln_ffn_g': out['ln_ffn_g'], 'ln_ffn_b': out['ln_ffn_b'], 'loss_target': out['loss_target'], 'm_w_in': out['m_w_in'], 'm_ssd_conv_w': out['m_ssd_conv_w'], 'm_ssd_conv_b': out['m_ssd_conv_b'], 'm_ssd_dt_bias': out['m_ssd_dt_bias'], 'm_ssd_A_log': out['m_ssd_A_log'], 'm_ssd_D': out['m_ssd_D'], 'm_ssd_norm_w': out['m_ssd_norm_w'], 'm_mla_q_norm_w': out['m_mla_q_norm_w'], 'm_mla_w_q_b': out['m_mla_w_q_b'], 'm_mla_kv_norm_w': out['m_mla_kv_norm_w'], 'm_mla_w_kv_b': out['m_mla_w_kv_b'], 'm_mla_out_norm_w': out['m_mla_out_norm_w'], 'm_w_out': out['m_w_out'], 'm_ln_mix_g': out['m_ln_mix_g'], 'm_ln_mix_b': out['m_ln_mix_b'], 'm_w_ffn_gate': out['m_w_ffn_gate'], 'm_w_ffn_up': out['m_w_ffn_up'], 'm_w_ffn_down': out['m_w_ffn_down'], 'm_w_ple_gate': out['m_w_ple_gate'], 'm_w_ple_proj': out['m_w_ple_proj'], 'm_ln_ffn_g': out['m_ln_ffn_g'], 'm_ln_ffn_b': out['m_ln_ffn_b'], 'v_w_in': out['v_w_in'], 'v_ssd_conv_w': out['v_ssd_conv_w'], 'v_ssd_conv_b': out['v_ssd_conv_b'], 'v_ssd_dt_bias': out['v_ssd_dt_bias'], 'v_ssd_A_log': out['v_ssd_A_log'], 'v_ssd_D': out['v_ssd_D'], 'v_ssd_norm_w': out['v_ssd_norm_w'], 'v_mla_q_norm_w': out['v_mla_q_norm_w'], 'v_mla_w_q_b': out['v_mla_w_q_b'], 'v_mla_kv_norm_w': out['v_mla_kv_norm_w'], 'v_mla_w_kv_b': out['v_mla_w_kv_b'], 'v_mla_out_norm_w': out['v_mla_out_norm_w'], 'v_w_out': out['v_w_out'], 'v_ln_mix_g': out['v_ln_mix_g'], 'v_ln_mix_b': out['v_ln_mix_b'], 'v_w_ffn_gate': out['v_w_ffn_gate'], 'v_w_ffn_up': out['v_w_ffn_up'], 'v_w_ffn_down': out['v_w_ffn_down'], 'v_w_ple_gate': out['v_w_ple_gate'], 'v_w_ple_proj': out['v_w_ple_proj'], 'v_ln_ffn_g': out['v_ln_ffn_g'], 'v_ln_ffn_b': out['v_ln_ffn_b']}


def _loss(weights, diff, rest, loss_target):
    with _jax.named_scope("forward"):
        args = {**rest, TWIN_DIFF_INPUT: diff, **{k: w.astype(_WEIGHT_DTYPES[k]) for k, w in weights.items()}}
        y = _forward(args)
    with _jax.named_scope("loss_head"):
        err = _jnp.square(y.astype(_jnp.float32) - loss_target)
        return 0.5 * _jnp.sum(_jnp.mean(err, axis=-1)) if err.ndim else 0.5 * err


def _adamw(w, g, m, v):
    m = ADAM_B1 * m + (1.0 - ADAM_B1) * g
    v = ADAM_B2 * v + (1.0 - ADAM_B2) * _jnp.square(g)
    m_hat = m / (1.0 - ADAM_B1 ** ADAM_STEP)
    v_hat = v / (1.0 - ADAM_B2 ** ADAM_STEP)
    delta = -ADAM_LR * (m_hat / (_jnp.sqrt(v_hat) + ADAM_EPS) + ADAM_WD * w)
    return delta, m, v


def reference(x, p, positions, w_in, ssd_conv_w, ssd_conv_b, ssd_dt_bias, ssd_A_log, ssd_D, ssd_norm_w, mla_q_norm_w, mla_w_q_b, mla_kv_norm_w, mla_w_kv_b, mla_out_norm_w, w_out, ln_mix_g, ln_mix_b, w_ffn_gate, w_ffn_up, w_ffn_down, w_ple_gate, w_ple_proj, ln_ffn_g, ln_ffn_b, loss_target, m_w_in, m_ssd_conv_w, m_ssd_conv_b, m_ssd_dt_bias, m_ssd_A_log, m_ssd_D, m_ssd_norm_w, m_mla_q_norm_w, m_mla_w_q_b, m_mla_kv_norm_w, m_mla_w_kv_b, m_mla_out_norm_w, m_w_out, m_ln_mix_g, m_ln_mix_b, m_w_ffn_gate, m_w_ffn_up, m_w_ffn_down, m_w_ple_gate, m_w_ple_proj, m_ln_ffn_g, m_ln_ffn_b, v_w_in, v_ssd_conv_w, v_ssd_conv_b, v_ssd_dt_bias, v_ssd_A_log, v_ssd_D, v_ssd_norm_w, v_mla_q_norm_w, v_mla_w_q_b, v_mla_kv_norm_w, v_mla_w_kv_b, v_mla_out_norm_w, v_w_out, v_ln_mix_g, v_ln_mix_b, v_w_ffn_gate, v_w_ffn_up, v_w_ffn_down, v_w_ple_gate, v_w_ple_proj, v_ln_ffn_g, v_ln_ffn_b):
    given = dict(x=x, p=p, positions=positions, w_in=w_in, ssd_conv_w=ssd_conv_w, ssd_conv_b=ssd_conv_b, ssd_dt_bias=ssd_dt_bias, ssd_A_log=ssd_A_log, ssd_D=ssd_D, ssd_norm_w=ssd_norm_w, mla_q_norm_w=mla_q_norm_w, mla_w_q_b=mla_w_q_b, mla_kv_norm_w=mla_kv_norm_w, mla_w_kv_b=mla_w_kv_b, mla_out_norm_w=mla_out_norm_w, w_out=w_out, ln_mix_g=ln_mix_g, ln_mix_b=ln_mix_b, w_ffn_gate=w_ffn_gate, w_ffn_up=w_ffn_up, w_ffn_down=w_ffn_down, w_ple_gate=w_ple_gate, w_ple_proj=w_ple_proj, ln_ffn_g=ln_ffn_g, ln_ffn_b=ln_ffn_b, loss_target=loss_target, m_w_in=m_w_in, m_ssd_conv_w=m_ssd_conv_w, m_ssd_conv_b=m_ssd_conv_b, m_ssd_dt_bias=m_ssd_dt_bias, m_ssd_A_log=m_ssd_A_log, m_ssd_D=m_ssd_D, m_ssd_norm_w=m_ssd_norm_w, m_mla_q_norm_w=m_mla_q_norm_w, m_mla_w_q_b=m_mla_w_q_b, m_mla_kv_norm_w=m_mla_kv_norm_w, m_mla_w_kv_b=m_mla_w_kv_b, m_mla_out_norm_w=m_mla_out_norm_w, m_w_out=m_w_out, m_ln_mix_g=m_ln_mix_g, m_ln_mix_b=m_ln_mix_b, m_w_ffn_gate=m_w_ffn_gate, m_w_ffn_up=m_w_ffn_up, m_w_ffn_down=m_w_ffn_down, m_w_ple_gate=m_w_ple_gate, m_w_ple_proj=m_w_ple_proj, m_ln_ffn_g=m_ln_ffn_g, m_ln_ffn_b=m_ln_ffn_b, v_w_in=v_w_in, v_ssd_conv_w=v_ssd_conv_w, v_ssd_conv_b=v_ssd_conv_b, v_ssd_dt_bias=v_ssd_dt_bias, v_ssd_A_log=v_ssd_A_log, v_ssd_D=v_ssd_D, v_ssd_norm_w=v_ssd_norm_w, v_mla_q_norm_w=v_mla_q_norm_w, v_mla_w_q_b=v_mla_w_q_b, v_mla_kv_norm_w=v_mla_kv_norm_w, v_mla_w_kv_b=v_mla_w_kv_b, v_mla_out_norm_w=v_mla_out_norm_w, v_w_out=v_w_out, v_ln_mix_g=v_ln_mix_g, v_ln_mix_b=v_ln_mix_b, v_w_ffn_gate=v_w_ffn_gate, v_w_ffn_up=v_w_ffn_up, v_w_ffn_down=v_w_ffn_down, v_w_ple_gate=v_w_ple_gate, v_w_ple_proj=v_w_ple_proj, v_ln_ffn_g=v_ln_ffn_g, v_ln_ffn_b=v_ln_ffn_b)
    weights = {n: given[n] for n in TWIN_WEIGHTS}
    shared = {n: given[n] for n in SHARED_INPUTS}
    per_example = {n: given[n] for n in ['x', 'p', 'positions']}
    grad_fn = _jax.value_and_grad(_loss, argnums=(0, 1))

    def one_microbatch(ex, loss_target):
        ex = dict(ex)
        diff = ex.pop(TWIN_DIFF_INPUT)
        return grad_fn(weights, diff, {**shared, **ex}, loss_target)

    if N_MICROBATCH == 1:
        loss, (grad_w, grad_x) = one_microbatch(per_example, given["loss_target"])
    else:
        def body(carry, xs):
            loss_sum, grad_sum = carry
            l_k, (gw_k, gx_k) = one_microbatch(xs[0], xs[1])
            with _jax.named_scope("update"):
                return (loss_sum + l_k, _jax.tree.map(_jnp.add, grad_sum, gw_k)), gx_k

        init = (_jnp.zeros((), _jnp.float32), _jax.tree.map(_jnp.zeros_like, weights))
        (loss, grad_w), grad_x = _jax.lax.scan(body, init, (per_example, given["loss_target"]))
    with _jax.named_scope("update"):
        delta_w, new_m, new_v = {}, {}, {}
        for n in TWIN_WEIGHTS:
            delta_w[n], new_m[n], new_v[n] = _adamw(weights[n], grad_w[n], given["m_" + n], given["v_" + n])
    return (loss, grad_x, *[grad_w[n] for n in TWIN_WEIGHTS], *[delta_w[n] for n in TWIN_WEIGHTS],
            *[new_m[n] for n in TWIN_WEIGHTS], *[new_v[n] for n in TWIN_WEIGHTS])
```

```python
import functools
import math

import numpy as np
import jax
import jax.numpy as jnp
from jax import lax
from jax.experimental import pallas as pl
from jax.experimental.pallas import tpu as pltpu

F32 = jnp.float32
BF16 = jnp.bfloat16
HI = lax.Precision.HIGHEST

N_DEV = 8
D_MODEL = 1024
PLE_DIM = 256
SSD_HEADS = 16
SSD_HEAD_DIM = 64
SSD_INNER = 1024
SSD_STATE = 128
SSD_XBC = 1536
SSD_CHUNK = 128
MLA_HEADS = 16
MLA_Q_RANK = 384
MLA_KV_RANK = 256
MLA_NOPE = 64
MLA_ROPE = 32
MLA_V = 64
ROPE_BASE = 10000.0
D_FF = 2816
IN_WIDTH = 3248
IN_PAD = 3456
ALPHA = 2.0 ** 0.25
EPS = 1e-6
LN_EPS = 1e-5
ATT_SCALE = 1.0 / math.sqrt(MLA_NOPE + MLA_ROPE)
ADAM_LR, ADAM_B1, ADAM_B2, ADAM_EPS, ADAM_WD, ADAM_STEP = 0.001, 0.9, 0.999, 1e-08, 0.01, 10

LANE = 128
MM_TILE = 512
ROW_TILE = 256
ATT_TQ = 256

BIG = (("w_in", 3248), ("mla_w_q_b", 576), ("mla_w_kv_b", 512), ("w_out", 2048), ("w_ffn_gate", 2816),
       ("w_ffn_up", 2816), ("w_ffn_down", 2816), ("w_ple_gate", 1024), ("w_ple_proj", 256))
BIG_ROWS = sum(r for _, r in BIG)
CONV_ROW0 = BIG_ROWS
FLAT_ROWS = 16384
FLAT_TILE = 1024

SMALL = (("loss", 1), ("ssd_conv_w", 48), ("ssd_conv_b", 12), ("ssd_dt_bias", 1), ("ssd_A_log", 1), ("ssd_D", 1),
         ("ssd_norm_w", 8), ("mla_q_norm_w", 3), ("mla_kv_norm_w", 2), ("mla_out_norm_w", 8), ("ln_mix_g", 8),
         ("ln_mix_b", 8), ("ln_ffn_g", 8), ("ln_ffn_b", 8))
SMALL_ROWS = 120
REPL = tuple(n for n, _ in SMALL[2:])


def _tile(dim, cap):
    if dim <= cap:
        return dim
    t = (cap // LANE) * LANE
    while dim % t:
        t -= LANE
    return t


def _dot(a, b, dims=(((1,), (0,)), ((), ())), precision=None):
    return lax.dot_general(a, b, dims, preferred_element_type=F32, precision=precision)


_NT = (((1,), (1,)), ((), ()))
_TN = (((0,), (0,)), ((), ()))


def _mm(a, b, *, ta=False, tb=False, add=None, name):
    if ta:
        k_dim, m_dim = a.shape
    else:
        m_dim, k_dim = a.shape
    n_dim = b.shape[0] if tb else b.shape[1]
    assert (b.shape[1] if tb else b.shape[0]) == k_dim
    tm, tn, tk = _tile(m_dim, MM_TILE), _tile(n_dim, MM_TILE), _tile(k_dim, MM_TILE)
    nk = k_dim // tk
    a_spec = pl.BlockSpec((tk, tm), lambda i, j, k: (k, i)) if ta else pl.BlockSpec((tm, tk), lambda i, j, k: (i, k))
    b_spec = pl.BlockSpec((tn, tk), lambda i, j, k: (j, k)) if tb else pl.BlockSpec((tk, tn), lambda i, j, k: (k, j))
    o_spec = pl.BlockSpec((tm, tn), lambda i, j, k: (i, j))
    dims = (((0 if ta else 1,), (1 if tb else 0,)), ((), ()))
    has_add = add is not None

    def body(*refs):
        if has_add:
            a_ref, b_ref, add_ref, o_ref, acc = refs
        else:
            a_ref, b_ref, o_ref, acc = refs
        k = pl.program_id(2)

        @pl.when(k == 0)
        def _():
            acc[...] = add_ref[...] if has_add else jnp.zeros_like(acc)

        acc[...] += _dot(a_ref[...].astype(BF16), b_ref[...].astype(BF16), dims)

        @pl.when(k == nk - 1)
        def _():
            o_ref[...] = acc[...]

    ins = [a, b] + ([add] if has_add else [])
    specs = [a_spec, b_spec] + ([o_spec] if has_add else [])
    return pl.pallas_call(
        body, name=name, grid=(m_dim // tm, n_dim // tn, nk), in_specs=specs, out_specs=o_spec,
        out_shape=jax.ShapeDtypeStruct((m_dim, n_dim), F32), scratch_shapes=[pltpu.VMEM((tm, tn), F32)],
        compiler_params=pltpu.CompilerParams(dimension_semantics=("parallel", "parallel", "arbitrary")),
    )(*ins)


def _rowwise(fn, rows, consts, out_widths, acc_widths=(), *, name, tr=ROW_TILE):
    row_arrays, row_specs = [], []
    first_arr = rows[0][0] if isinstance(rows[0], tuple) else rows[0]
    s_dim = first_arr.shape[0]
    tr = min(tr, s_dim)
    for r in rows:
        arr, width, cb = r if isinstance(r, tuple) else (r, r.shape[1], 0)
        row_arrays.append(arr)
        row_specs.append(pl.BlockSpec((tr, width), functools.partial(lambda i, cb: (i, cb), cb=cb)))
    const_specs = [pl.BlockSpec(c.shape, lambda i: (0, 0)) for c in consts]
    nr, nc, no, na = len(rows), len(consts), len(out_widths), len(acc_widths)

    def body(*refs):
        ins = [r[...] for r in refs[:nr + nc]]
        res = fn(*ins)
        if not isinstance(res, (tuple, list)):
            res = (res,)
        out_refs = refs[nr + nc:nr + nc + no]
        acc_refs = refs[nr + nc + no:]
        for o_ref, val in zip(out_refs, res[:no]):
            o_ref[...] = val.astype(o_ref.dtype)
        first = pl.program_id(0) == 0
        for a_ref, val in zip(acc_refs, res[no:]):
            @pl.when(first)
            def _(a_ref=a_ref, val=val):
                a_ref[...] = val

            @pl.when(jnp.logical_not(first))
            def _(a_ref=a_ref, val=val):
                a_ref[...] += val

    out_shape = [jax.ShapeDtypeStruct((s_dim, w), F32) for w in out_widths]
    out_shape += [jax.ShapeDtypeStruct((1, w), F32) for w in acc_widths]
    out_specs = [pl.BlockSpec((tr, w), lambda i: (i, 0)) for w in out_widths]
    out_specs += [pl.BlockSpec((1, w), lambda i: (0, 0)) for w in acc_widths]
    res = pl.pallas_call(
        body, name=name, grid=(s_dim // tr,), in_specs=row_specs + const_specs, out_specs=out_specs, out_shape=out_shape,
        compiler_params=pltpu.CompilerParams(dimension_semantics=("arbitrary",)),
    )(*row_arrays, *consts)
    return res


def _colsum(v):
    return jnp.sum(v, axis=0, keepdims=True)


def _rms(u, g):
    return u * lax.rsqrt(jnp.mean(u * u, axis=-1, keepdims=True) + EPS) * g


def _ln(u, g, b):
    mu = jnp.mean(u, axis=-1, keepdims=True)
    d = u - mu
    var = jnp.mean(d * d, axis=-1, keepdims=True)
    return d * lax.rsqrt(var + LN_EPS) * g + b


def _sigmoid(v):
    return 1.0 / (1.0 + jnp.exp(-v))


def _silu(v):
    return v * _sigmoid(v)


def _softplus(v):
    y = jnp.exp(-jnp.abs(v))
    w = 1.0 + y
    log1p = jnp.where(w == 1.0, y, jnp.log(w) * y / jnp.where(w == 1.0, 1.0, w - 1.0))
    return jnp.maximum(v, 0.0) + log1p


def _gate_rms(y, z, w):
    return _rms(y * _silu(z), w)


def _vjp_rows(f):
    def fn(*args):
        prim, ct = args[:-1], args[-1]
        _, pull = jax.vjp(f, *prim)
        return pull(ct)
    return fn


def _conv_pre(cur, prev, w, b, first):
    row = lax.broadcasted_iota(jnp.int32, cur.shape, 0)
    acc = cur * w[3:4, :] + b
    for j in (1, 2, 3):
        tail = jnp.where(first, 0.0, pltpu.roll(prev, j, 0))
        acc = acc + jnp.where(row >= j, pltpu.roll(cur, j, 0), tail) * w[3 - j:4 - j, :]
    return acc


def _conv_fwd(u, w, b, name="conv_fwd"):
    s_dim, width = u.shape
    tr = min(ROW_TILE, s_dim)

    def body(cur_ref, prev_ref, w_ref, b_ref, o_ref):
        pre = _conv_pre(cur_ref[...], prev_ref[...], w_ref, b_ref[...], pl.program_id(0) == 0)
        o_ref[...] = _silu(pre)

    return pl.pallas_call(
        body, name=name, grid=(s_dim // tr,),
        in_specs=[pl.BlockSpec((tr, width), lambda i: (i, 0)), pl.BlockSpec((tr, width), lambda i: (jnp.maximum(i - 1, 0), 0)),
                  pl.BlockSpec(w.shape, lambda i: (0, 0)), pl.BlockSpec(b.shape, lambda i: (0, 0))],
        out_specs=pl.BlockSpec((tr, width), lambda i: (i, 0)), out_shape=jax.ShapeDtypeStruct((s_dim, width), F32),
        compiler_params=pltpu.CompilerParams(dimension_semantics=("arbitrary",)),
    )(u, u, w, b)


def _conv_bwd_pre(u, w, b, dact, name="conv_bwd_pre"):
    s_dim, width = u.shape
    tr = min(ROW_TILE, s_dim)

    def body(cur_ref, prev_ref, w_ref, b_ref, d_ref, da_ref, dw_ref, db_ref):
        first = pl.program_id(0) == 0
        cur, prev = cur_ref[...], prev_ref[...]
        pre = _conv_pre(cur, prev, w_ref, b_ref[...], first)
        sg = _sigmoid(pre)
        da = d_ref[...] * (sg * (1.0 + pre * (1.0 - sg)))
        da_ref[...] = da
        row = lax.broadcasted_iota(jnp.int32, cur.shape, 0)

        @pl.when(first)
        def _():
            dw_ref[...] = jnp.zeros_like(dw_ref)
            db_ref[...] = jnp.zeros_like(db_ref)

        db_ref[...] += _colsum(da)
        dw_ref[3:4, :] += _colsum(da * cur)
        for j in (1, 2, 3):
            tail = jnp.where(first, 0.0, pltpu.roll(prev, j, 0))
            sh = jnp.where(row >= j, pltpu.roll(cur, j, 0), tail)
            dw_ref[3 - j:4 - j, :] += _colsum(da * sh)

    return pl.pallas_call(
        body, name=name, grid=(s_dim // tr,),
        in_specs=[pl.BlockSpec((tr, width), lambda i: (i, 0)), pl.BlockSpec((tr, width), lambda i: (jnp.maximum(i - 1, 0), 0)),
                  pl.BlockSpec(w.shape, lambda i: (0, 0)), pl.BlockSpec(b.shape, lambda i: (0, 0)),
                  pl.BlockSpec((tr, width), lambda i: (i, 0))],
        out_specs=[pl.BlockSpec((tr, width), lambda i: (i, 0)), pl.BlockSpec(w.shape, lambda i: (0, 0)),
                   pl.BlockSpec(b.shape, lambda i: (0, 0))],
        out_shape=[jax.ShapeDtypeStruct((s_dim, width), F32), jax.ShapeDtypeStruct(w.shape, F32),
                   jax.ShapeDtypeStruct(b.shape, F32)],
        compiler_params=pltpu.CompilerParams(dimension_semantics=("arbitrary",)),
    )(u, u, w, b, dact)


def _conv_bwd_in(da, w, name="conv_bwd_in"):
    s_dim, width = da.shape
    tr = min(ROW_TILE, s_dim)
    n = s_dim // tr

    def body(cur_ref, nxt_ref, w_ref, o_ref):
        last = pl.program_id(0) == n - 1
        cur, nxt = cur_ref[...], nxt_ref[...]
        row = lax.broadcasted_iota(jnp.int32, cur.shape, 0)
        acc = cur * w_ref[3:4, :]
        for j in (1, 2, 3):
            head = jnp.where(last, 0.0, pltpu.roll(nxt, tr - j, 0))
            acc = acc + jnp.where(row < tr - j, pltpu.roll(cur, tr - j, 0), head) * w_ref[3 - j:4 - j, :]
        o_ref[...] = acc

    return pl.pallas_call(
        body, name=name, grid=(n,),
        in_specs=[pl.BlockSpec((tr, width), lambda i: (i, 0)), pl.BlockSpec((tr, width), lambda i: (jnp.minimum(i + 1, n - 1), 0)),
                  pl.BlockSpec(w.shape, lambda i: (0, 0))],
        out_specs=pl.BlockSpec((tr, width), lambda i: (i, 0)), out_shape=jax.ShapeDtypeStruct((s_dim, width), F32),
        compiler_params=pltpu.CompilerParams(dimension_semantics=("arbitrary",)),
    )(da, da, w)


def _ssd_consts():
    L = SSD_CHUNK
    tri = np.tril(np.ones((L, L), np.float32))
    expand = np.zeros((LANE, SSD_INNER), np.float32)
    expand128 = np.zeros((LANE, SSD_HEADS * LANE), np.float32)
    for h in range(SSD_HEADS):
        expand[h, h * SSD_HEAD_DIM:(h + 1) * SSD_HEAD_DIM] = 1.0
        expand128[h, h * LANE:(h + 1) * LANE] = 1.0
    return jnp.asarray(tri), jnp.asarray(expand), jnp.asarray(expand128), jnp.asarray(expand.T.copy())


def _ssd_prep(dt_ref, bias_ref, alog_ref, tri_ref, exp_ref, exp128_ref, cs_s, cst_s, ex_s, csx_s):
    L = SSD_CHUNK
    dt = _softplus(dt_ref[...] + bias_ref[...])
    a = -jnp.exp(alog_ref[...])
    cs = _dot(tri_ref[...], dt * a, precision=HI)
    cs_s[...] = cs
    cst_s[...] = cs.T
    last = cs_s[L - 1:L, :]
    expand = exp_ref[...]
    ex_s[...] = _dot(jnp.exp(cs), expand, precision=HI)
    f_x = _dot(jnp.exp(last - cs), expand, precision=HI)
    dt_x = _dot(dt, expand, precision=HI)
    csx_s[...] = _dot(cs, exp128_ref[...], precision=HI)
    t_x = ex_s[L - 1:L, :]
    return dt, a, dt_x, f_x, t_x


def _decay_matrix(csx_s, cst_s, h, tril):
    seg = csx_s[:, h * LANE:(h + 1) * LANE] - cst_s[h:h + 1, :]
    return jnp.exp(jnp.where(tril, seg, -jnp.inf))


def _ssd_fwd(xbca, dtr, bias, alog, d_x, name="ssd_fwd"):
    s_dim = xbca.shape[0]
    L = SSD_CHUNK
    nc = s_dim // L
    tri, expand, expand128, _ = _ssd_consts()

    def body(xs_ref, b_ref, c_ref, dt_ref, bias_ref, alog_ref, dx_ref, tri_ref, exp_ref, exp128_ref,
             y_ref, st_ref, st_s, cs_s, cst_s, ex_s, csx_s):
        @pl.when(pl.program_id(0) == 0)
        def _():
            st_s[...] = jnp.zeros_like(st_s)

        dt, a, dt_x, f_x, t_x = _ssd_prep(dt_ref, bias_ref, alog_ref, tri_ref, exp_ref, exp128_ref, cs_s, cst_s, ex_s, csx_s)
        st_ref[0] = st_s[...]
        row = lax.broadcasted_iota(jnp.int32, (L, L), 0)
        col = lax.broadcasted_iota(jnp.int32, (L, L), 1)
        tril = row >= col
        low = col < SSD_HEAD_DIM
        for g in range(2):
            bg = b_ref[:, g * LANE:(g + 1) * LANE]
            cg = c_ref[:, g * LANE:(g + 1) * LANE].astype(BF16)
            gmat = _dot(cg, bg.astype(BF16), _NT)
            bgt = bg.T.astype(BF16)
            for jj in range(4):
                j = 4 * g + jj
                sl = slice(j * LANE, (j + 1) * LANE)
                xp = xs_ref[:, sl]
                x_dt = xp * dt_x[:, sl]
                xb = x_dt.astype(BF16)
                yd = []
                for e in range(2):
                    lm = _decay_matrix(csx_s, cst_s, 2 * j + e, tril)
                    yd.append(_dot((gmat * lm).astype(BF16), xb))
                stp = st_s[j]
                z = _dot(cg, stp.astype(BF16))
                y_ref[:, sl] = jnp.where(low, yd[0], yd[1]) + ex_s[:, sl] * z + dx_ref[:, sl] * xp
                xf = (x_dt * f_x[:, sl]).astype(BF16)
                st_s[j] = t_x[:, sl] * stp + _dot(bgt, xf)

    const = lambda shape: pl.BlockSpec(shape, lambda c: tuple(0 for _ in shape))
    return pl.pallas_call(
        body, name=name, grid=(nc,),
        in_specs=[pl.BlockSpec((L, 1024), lambda c: (c, 0)), pl.BlockSpec((L, 256), lambda c: (c, 4)),
                  pl.BlockSpec((L, 256), lambda c: (c, 5)), pl.BlockSpec((L, LANE), lambda c: (c, 0)),
                  const((1, LANE)), const((1, LANE)), const((1, 1024)), const((L, L)), const((LANE, 1024)),
                  const((LANE, 2048))],
        out_specs=[pl.BlockSpec((L, 1024), lambda c: (c, 0)), pl.BlockSpec((1, 8, LANE, LANE), lambda c: (c, 0, 0, 0))],
        out_shape=[jax.ShapeDtypeStruct((s_dim, 1024), F32), jax.ShapeDtypeStruct((nc, 8, LANE, LANE), F32)],
        scratch_shapes=[pltpu.VMEM((8, LANE, LANE), F32), pltpu.VMEM((L, LANE), F32), pltpu.VMEM((LANE, L), F32),
                        pltpu.VMEM((L, 1024), F32), pltpu.VMEM((L, 2048), F32)],
        compiler_params=pltpu.CompilerParams(dimension_semantics=("arbitrary",)),
    )(xbca, xbca, xbca, dtr, bias, alog, d_x, tri, expand, expand128)


def _ssd_bwd(xbca, dtr, bias, alog, d_x, states, dy, name="ssd_bwd"):
    s_dim = xbca.shape[0]
    L = SSD_CHUNK
    nc = s_dim // L
    tri, expand, expand128, expand_t = _ssd_consts()

    def body(xs_ref, b_ref, c_ref, dt_ref, bias_ref, alog_ref, dx_ref, tri_ref, exp_ref, exp128_ref, expt_ref,
             st_ref, dy_ref, dxbc_ref, ddt_ref, dbias_ref, dalog_ref, dd_ref,
             dst_s, cs_s, cst_s, ex_s, csx_s, dcsx_s, ddtx_s, dcol_s, drow_s, dlast_s):
        @pl.when(pl.program_id(0) == 0)
        def _():
            dst_s[...] = jnp.zeros_like(dst_s)
            dbias_ref[...] = jnp.zeros_like(dbias_ref)
            dalog_ref[...] = jnp.zeros_like(dalog_ref)
            dd_ref[...] = jnp.zeros_like(dd_ref)

        dt, a, dt_x, f_x, t_x = _ssd_prep(dt_ref, bias_ref, alog_ref, tri_ref, exp_ref, exp128_ref, cs_s, cst_s, ex_s, csx_s)
        row = lax.broadcasted_iota(jnp.int32, (L, L), 0)
        col = lax.broadcasted_iota(jnp.int32, (L, L), 1)
        tril = row >= col
        low = col < SSD_HEAD_DIM
        dcol_s[...] = jnp.zeros_like(dcol_s)
        drow_s[...] = jnp.zeros_like(drow_s)
        for g in range(2):
            bg = b_ref[:, g * LANE:(g + 1) * LANE]
            cg = c_ref[:, g * LANE:(g + 1) * LANE]
            bgb, cgb = bg.astype(BF16), cg.astype(BF16)
            gmat = _dot(cgb, bgb, _NT)
            d_g = jnp.zeros((L, L), F32)
            d_b = jnp.zeros((L, LANE), F32)
            d_c = jnp.zeros((L, LANE), F32)
            for jj in range(4):
                j = 4 * g + jj
                sl = slice(j * LANE, (j + 1) * LANE)
                xp = xs_ref[:, sl]
                dtp = dt_x[:, sl]
                x_dt = xp * dtp
                xb = x_dt.astype(BF16)
                dyp = dy_ref[:, sl]
                dd_ref[:, sl] += _colsum(dyp * xp)
                d_xdt = jnp.zeros((L, LANE), F32)
                for e in range(2):
                    h = 2 * j + e
                    lm = _decay_matrix(csx_s, cst_s, h, tril)
                    m = gmat * lm
                    dye = jnp.where(low if e == 0 else jnp.logical_not(low), dyp, 0.0).astype(BF16)
                    d_m = jnp.where(tril, _dot(dye, xb, _NT), 0.0)
                    d_xdt = d_xdt + _dot(m.astype(BF16), dye, _TN)
                    d_g = d_g + d_m * lm
                    w = d_m * m
                    dcol_s[...] += jnp.where(col == h, jnp.sum(w, axis=1, keepdims=True), 0.0)
                    drow_s[...] += jnp.where(row == h, jnp.sum(w, axis=0, keepdims=True), 0.0)
                stp = st_ref[0, j]
                stb = stp.astype(BF16)
                dstn = dst_s[j]
                dstb = dstn.astype(BF16)
                e_p = ex_s[:, sl]
                f_p = f_x[:, sl]
                t_p = t_x[:, sl]
                z = _dot(cgb, stb)
                d_z = (e_p * dyp).astype(BF16)
                d_c = d_c + _dot(d_z, stb, _NT)
                d_xf = _dot(bgb, dstb)
                d_b = d_b + _dot((x_dt * f_p).astype(BF16), dstb, _NT)
                d_xdt = d_xdt + f_p * d_xf
                d_f = x_dt * d_xf * f_p
                dcsx_s[:, sl] = dyp * e_p * z - d_f
                dlast_s[:, sl] = _colsum(d_f) + _colsum(dstn * stp) * t_p
                dst_s[j] = _dot(cgb, d_z, _TN) + t_p * dstn
                dxbc_ref[:, sl] = dx_ref[:, sl] * dyp + d_xdt * dtp
                ddtx_s[:, sl] = d_xdt * xp
            d_gb = d_g.astype(BF16)
            dxbc_ref[:, 1024 + g * LANE:1024 + (g + 1) * LANE] = d_b + _dot(d_gb, cgb, _TN)
            dxbc_ref[:, 1280 + g * LANE:1280 + (g + 1) * LANE] = d_c + _dot(d_gb, bgb)

        expt = expt_ref[...]
        dlast = _dot(jnp.broadcast_to(dlast_s[...], (8, 1024)), expt, precision=HI)
        d_cs = dcol_s[...] - drow_s[...].T + _dot(dcsx_s[...], expt, precision=HI)
        rown = lax.broadcasted_iota(jnp.int32, (L, LANE), 0)
        d_cs = d_cs + jnp.where(rown == L - 1, jnp.sum(dlast, axis=0, keepdims=True) * 0.125, 0.0)
        d_da = _dot(tri_ref[...], d_cs, _TN, precision=HI)
        d_dt = d_da * a + _dot(ddtx_s[...], expt, precision=HI)
        dalog_ref[...] += _colsum(d_da * dt) * a
        d_raw = d_dt * _sigmoid(dt_ref[...] + bias_ref[...])
        ddt_ref[...] = d_raw
        dbias_ref[...] += _colsum(d_raw)

    const = lambda shape: pl.BlockSpec(shape, lambda c: tuple(0 for _ in shape))
    rev = lambda cb: (lambda c: (nc - 1 - c, cb))
    return pl.pallas_call(
        body, name=name, grid=(nc,),
        in_specs=[pl.BlockSpec((L, 1024), rev(0)), pl.BlockSpec((L, 256), rev(4)), pl.BlockSpec((L, 256), rev(5)),
                  pl.BlockSpec((L, LANE), rev(0)), const((1, LANE)), const((1, LANE)), const((1, 1024)), const((L, L)),
                  const((LANE, 1024)), const((LANE, 2048)), const((1024, LANE)),
                  pl.BlockSpec((1, 8, LANE, LANE), lambda c: (nc - 1 - c, 0, 0, 0)), pl.BlockSpec((L, 1024), rev(0))],
        out_specs=[pl.BlockSpec((L, SSD_XBC), rev(0)), pl.BlockSpec((L, LANE), rev(0)), const((1, LANE)), const((1, LANE)),
                   const((1, 1024))],
        out_shape=[jax.ShapeDtypeStruct((s_dim, SSD_XBC), F32), jax.ShapeDtypeStruct((s_dim, LANE), F32),
                   jax.ShapeDtypeStruct((1, LANE), F32), jax.ShapeDtypeStruct((1, LANE), F32),
                   jax.ShapeDtypeStruct((1, 1024), F32)],
        scratch_shapes=[pltpu.VMEM((8, LANE, LANE), F32), pltpu.VMEM((L, LANE), F32), pltpu.VMEM((LANE, L), F32),
                        pltpu.VMEM((L, 1024), F32), pltpu.VMEM((L, 2048), F32), pltpu.VMEM((L, 1024), F32),
                        pltpu.VMEM((L, 1024), F32), pltpu.VMEM((L, LANE), F32), pltpu.VMEM((LANE, L), F32),
                        pltpu.VMEM((1, 1024), F32)],
        compiler_params=pltpu.CompilerParams(dimension_semantics=("arbitrary",)),
    )(xbca, xbca, xbca, dtr, bias, alog, d_x, tri, expand, expand128, expand_t, states, dy)


def _swap_halves(u):
    width = u.shape[1]
    lane = lax.broadcasted_iota(jnp.int32, u.shape, 1)
    return jnp.where(lane % MLA_ROPE < MLA_ROPE // 2, pltpu.roll(u, width - MLA_ROPE // 2, 1), pltpu.roll(u, MLA_ROPE // 2, 1))


def _rope_fwd_fn(u, cos, sin):
    return u * cos + _swap_halves(u) * sin


def _rope_bwd_fn(d, cos, sin):
    return d * cos + _swap_halves(d * sin)


def _spread4(v):
    return v + pltpu.roll(v, 32, 1) + pltpu.roll(v, 64, 1) + pltpu.roll(v, 96, 1)


def _att_masks(tq):
    lane = lax.broadcasted_iota(jnp.int32, (tq, LANE), 1)
    return lane // MLA_NOPE, lane // MLA_ROPE


def _att_probs(qcat, kcat, q0, s_dim):
    tq = qcat.shape[0]
    s = _dot(qcat, kcat, _NT) * ATT_SCALE
    qpos = q0 + lax.broadcasted_iota(jnp.int32, (tq, s_dim), 0)
    kpos = lax.broadcasted_iota(jnp.int32, (tq, s_dim), 1)
    s = jnp.where(qpos >= kpos, s, -jnp.inf)
    p = jnp.exp(s - jnp.max(s, axis=1, keepdims=True))
    return p / jnp.sum(p, axis=1, keepdims=True)


def _att_fwd(q, qr, kv, krt, name="att_fwd"):
    s_dim = q.shape[0]
    tq = min(ATT_TQ, s_dim)

    def body(qn_ref, qr_ref, kn_ref, krt_ref, v_ref, o_ref):
        q0 = pl.program_id(1) * tq
        half_id, grp_id = _att_masks(tq)
        krt = krt_ref[...]
        for half in range(2):
            hs = slice(half * LANE, (half + 1) * LANE)
            kcat = jnp.concatenate([kn_ref[:, hs], krt], axis=1).astype(BF16)
            vb = v_ref[:, hs].astype(BF16)
            outs = []
            for par in range(2):
                e = 2 * half + par
                qcat = jnp.concatenate([jnp.where(half_id == par, qn_ref[:, hs], 0.0),
                                        jnp.where(grp_id == e, qr_ref[...], 0.0)], axis=1).astype(BF16)
                p = _att_probs(qcat, kcat, q0, s_dim)
                outs.append(_dot(p.astype(BF16), vb))
            o_ref[:, hs] = jnp.where(half_id == 0, outs[0], outs[1])

    return pl.pallas_call(
        body, name=name, grid=(4, s_dim // tq),
        in_specs=[pl.BlockSpec((tq, 256), lambda m, i: (i, m)), pl.BlockSpec((tq, LANE), lambda m, i: (i, m)),
                  pl.BlockSpec((s_dim, 256), lambda m, i: (0, m)), pl.BlockSpec((s_dim, LANE), lambda m, i: (0, 0)),
                  pl.BlockSpec((s_dim, 256), lambda m, i: (0, 4 + m))],
        out_specs=pl.BlockSpec((tq, 256), lambda m, i: (i, m)), out_shape=jax.ShapeDtypeStruct((s_dim, 1024), F32),
        compiler_params=pltpu.CompilerParams(dimension_semantics=("parallel", "arbitrary")),
    )(q, qr, kv, krt, kv)


def _att_bwd(q, qr, kv, krt, o, do, name="att_bwd"):
    s_dim = q.shape[0]
    tq = min(ATT_TQ, s_dim)

    def body(qn_ref, qr_ref, kn_ref, krt_ref, v_ref, o_ref, do_ref, dqn_ref, dqr_ref, dkn_ref, dv_ref, dkrt_ref):
        @pl.when(pl.program_id(1) == 0)
        def _():
            dkn_ref[...] = jnp.zeros_like(dkn_ref)
            dv_ref[...] = jnp.zeros_like(dv_ref)
            dkrt_ref[...] = jnp.zeros_like(dkrt_ref)

        q0 = pl.program_id(1) * tq
        half_id, grp_id = _att_masks(tq)
        krt = krt_ref[...]
        dqr = jnp.zeros((tq, LANE), F32)
        for half in range(2):
            hs = slice(half * LANE, (half + 1) * LANE)
            kcat = jnp.concatenate([kn_ref[:, hs], krt], axis=1).astype(BF16)
            vb = v_ref[:, hs].astype(BF16)
            dqn = jnp.zeros((tq, LANE), F32)
            for par in range(2):
                e = 2 * half + par
                qcat = jnp.concatenate([jnp.where(half_id == par, qn_ref[:, hs], 0.0),
                                        jnp.where(grp_id == e, qr_ref[...], 0.0)], axis=1).astype(BF16)
                p = _att_probs(qcat, kcat, q0, s_dim)
                dom = jnp.where(half_id == par, do_ref[:, hs], 0.0)
                d_p = _dot(dom.astype(BF16), vb, _NT)
                d_row = jnp.sum(dom * o_ref[:, hs], axis=1, keepdims=True)
                d_s = (p * (d_p - d_row) * ATT_SCALE).astype(BF16)
                dqcat = _dot(d_s, kcat)
                dqn = dqn + jnp.where(half_id == par, dqcat[:, :LANE], 0.0)
                dqr = dqr + jnp.where(grp_id == e, dqcat[:, LANE:], 0.0)
                dkcat = _dot(d_s, qcat, _TN)
                dkn_ref[:, hs] += dkcat[:, :LANE]
                dkrt_ref[0] += dkcat[:, LANE:]
                dv_ref[:, hs] += _dot(p.astype(BF16), dom.astype(BF16), _TN)
            dqn_ref[:, hs] = dqn
        dqr_ref[...] = dqr

    return pl.pallas_call(
        body, name=name, grid=(4, s_dim // tq),
        in_specs=[pl.BlockSpec((tq, 256), lambda m, i: (i, m)), pl.BlockSpec((tq, LANE), lambda m, i: (i, m)),
                  pl.BlockSpec((s_dim, 256), lambda m, i: (0, m)), pl.BlockSpec((s_dim, LANE), lambda m, i: (0, 0)),
                  pl.BlockSpec((s_dim, 256), lambda m, i: (0, 4 + m)), pl.BlockSpec((tq, 256), lambda m, i: (i, m)),
                  pl.BlockSpec((tq, 256), lambda m, i: (i, m))],
        out_specs=[pl.BlockSpec((tq, 256), lambda m, i: (i, m)), pl.BlockSpec((tq, LANE), lambda m, i: (i, m)),
                   pl.BlockSpec((s_dim, 256), lambda m, i: (0, m)), pl.BlockSpec((s_dim, 256), lambda m, i: (0, m)),
                   pl.BlockSpec((1, s_dim, LANE), lambda m, i: (m, 0, 0))],
        out_shape=[jax.ShapeDtypeStruct((s_dim, 1024), F32), jax.ShapeDtypeStruct((s_dim, 512), F32),
                   jax.ShapeDtypeStruct((s_dim, 1024), F32), jax.ShapeDtypeStruct((s_dim, 1024), F32),
                   jax.ShapeDtypeStruct((4, s_dim, LANE), F32)],
        compiler_params=pltpu.CompilerParams(dimension_semantics=("parallel", "arbitrary")),
    )(q, qr, kv, krt, kv, o, do)


def _all_gather(x, name):
    rows, width = x.shape

    def body(x_ref, out_ref, send_sems, recv_sems, local_sem):
        x_i, y_i, c_i = lax.axis_index("x"), lax.axis_index("y"), lax.axis_index("c")
        me, sibling = (x_i, y_i, c_i), (x_i, y_i, 1 - c_i)
        chips = [(1 - x_i, y_i), (x_i, 1 - y_i), (1 - x_i, 1 - y_i)]

        def slot(px, py, pc):
            return out_ref.at[4 * px + 2 * py + pc]

        def copy(k, block, to, src=None):
            return pltpu.make_async_remote_copy(
                src_ref=slot(*block) if src is None else src, dst_ref=slot(*block), send_sem=send_sems.at[k],
                recv_sem=recv_sems.at[k], device_id=to, device_id_type=pl.DeviceIdType.MESH)

        mine = pltpu.make_async_copy(x_ref, slot(*me), local_sem)
        mine.start()
        first = [copy(0, me, sibling, src=x_ref)]
        first += [copy(1 + j, me, (*chip, c_i), src=x_ref) for j, chip in enumerate(chips)]
        for cp in first:
            cp.start()
        passed = [copy(4 + j, (*chip, c_i), sibling) for j, chip in enumerate(chips)]
        for j, chip in enumerate(chips):
            copy(1 + j, (*chip, c_i), me).wait_recv()
            passed[j].start()
        copy(0, sibling, me).wait_recv()
        for j, chip in enumerate(chips):
            copy(4 + j, (*chip, 1 - c_i), me).wait_recv()
        for cp in first + passed:
            cp.wait_send()
        mine.wait()

    return pl.pallas_call(
        body, name=name, out_shape=jax.ShapeDtypeStruct((N_DEV, rows, width), x.dtype),
        in_specs=[pl.BlockSpec(memory_space=pl.ANY)], out_specs=pl.BlockSpec(memory_space=pl.ANY),
        scratch_shapes=[pltpu.SemaphoreType.DMA((7,)), pltpu.SemaphoreType.DMA((7,)), pltpu.SemaphoreType.DMA],
    )(x)


def _all_to_all(g, name):
    def body(g_ref, out_ref, send_sems, recv_sems, local_sem):
        x_i, y_i, c_i = lax.axis_index("x"), lax.axis_index("y"), lax.axis_index("c")
        me = 4 * x_i + 2 * y_i + c_i
        mine = pltpu.make_async_copy(g_ref.at[me], out_ref.at[me], local_sem)
        mine.start()
        copies = []
        for k in range(1, N_DEV):
            px, py, pc = x_i ^ (k >> 2), y_i ^ ((k >> 1) & 1), c_i ^ (k & 1)
            copies.append(pltpu.make_async_remote_copy(
                src_ref=g_ref.at[4 * px + 2 * py + pc], dst_ref=out_ref.at[me], send_sem=send_sems.at[k - 1],
                recv_sem=recv_sems.at[k - 1], device_id=(px, py, pc), device_id_type=pl.DeviceIdType.MESH))
        for cp in copies:
            cp.start()
        for cp in copies:
            cp.wait_recv()
        for cp in copies:
            cp.wait_send()
        mine.wait()

    return pl.pallas_call(
        body, name=name, out_shape=jax.ShapeDtypeStruct(g.shape, g.dtype),
        in_specs=[pl.BlockSpec(memory_space=pl.ANY)], out_specs=pl.BlockSpec(memory_space=pl.ANY),
        scratch_shapes=[pltpu.SemaphoreType.DMA((7,)), pltpu.SemaphoreType.DMA((7,)), pltpu.SemaphoreType.DMA],
    )(g)


def _adam(slots, w, m, v, name):
    n_slot, rows, _ = slots.shape
    tr = FLAT_TILE if rows % FLAT_TILE == 0 else rows

    def body(s_ref, w_ref, m_ref, v_ref, g_ref, d_ref, mo_ref, vo_ref):
        g = s_ref[0]
        for k in range(1, n_slot):
            g = g + s_ref[k]
        m_new = ADAM_B1 * m_ref[...] + (1.0 - ADAM_B1) * g
        v_new = ADAM_B2 * v_ref[...] + (1.0 - ADAM_B2) * (g * g)
        m_hat = m_new / (1.0 - ADAM_B1 ** ADAM_STEP)
        v_hat = v_new / (1.0 - ADAM_B2 ** ADAM_STEP)
        g_ref[...] = g
        d_ref[...] = -ADAM_LR * (m_hat / (jnp.sqrt(v_hat) + ADAM_EPS) + ADAM_WD * w_ref[...])
        mo_ref[...] = m_new
        vo_ref[...] = v_new

    spec = pl.BlockSpec((tr, LANE), lambda i: (i, 0))
    return pl.pallas_call(
        body, name=name, grid=(rows // tr,), in_specs=[pl.BlockSpec((n_slot, tr, LANE), lambda i: (0, i, 0)), spec, spec, spec],
        out_specs=[spec] * 4, out_shape=[jax.ShapeDtypeStruct((rows, LANE), F32)] * 4,
        compiler_params=pltpu.CompilerParams(dimension_semantics=("parallel",)),
    )(slots, w, m, v)


def _flat(a):
    return a.reshape(-1, LANE)


def _pack_big(parts):
    pieces = [parts[n] for n, _ in BIG]
    pad = FLAT_ROWS - BIG_ROWS
    return jnp.concatenate(pieces + [jnp.zeros((pad, LANE), pieces[0].dtype)], axis=0)


def _unpack_big(flat):
    out, r0 = {}, 0
    for n, r in BIG:
        out[n] = flat[..., r0:r0 + r, :]
        r0 += r
    return out


def _pad_rows(a, rows):
    return jnp.concatenate([a, jnp.zeros((rows - a.shape[0], a.shape[1]), a.dtype)], axis=0)


def _pack_small(vals):
    pieces = []
    for n, r in SMALL:
        v = vals[n].reshape(-1).astype(F32)
        v = jnp.concatenate([v, jnp.zeros((r * LANE - v.shape[0],), F32)])
        pieces.append(v.reshape(r, LANE))
    pieces.append(jnp.zeros((SMALL_ROWS - sum(r for _, r in SMALL), LANE), F32))
    return jnp.concatenate(pieces, axis=0)


def _unpack_small(flat, shapes):
    out, r0 = {}, 0
    for n, r in SMALL:
        size = int(np.prod(shapes[n]))
        out[n] = flat[r0:r0 + r].reshape(-1)[:size].reshape(shapes[n])
        r0 += r
    return out


def _cols_full(g):
    return jnp.transpose(g, (1, 0, 2)).reshape(g.shape[1], -1)


def _cols_split(full):
    k_dim, n_dim = full.shape
    return jnp.transpose(full.reshape(k_dim, N_DEV, n_dim // N_DEV), (1, 0, 2))


def _win_pad(w):
    z = lambda n: jnp.zeros((w.shape[0], n), w.dtype)
    return jnp.concatenate([w[:, :2576], z(112), w[:, 2576:], z(96)], axis=1)


def _win_unpad(w):
    return jnp.concatenate([w[:, :2576], w[:, 2688:3360]], axis=1)


def _heads_split(w, a, b):
    k_dim = w.shape[0]
    w3 = w.reshape(k_dim, MLA_HEADS, a + b)
    return jnp.concatenate([w3[:, :, :a].reshape(k_dim, -1), w3[:, :, a:].reshape(k_dim, -1)], axis=1)


def _heads_merge(w, a, b):
    k_dim = w.shape[0]
    wa = w[:, :MLA_HEADS * a].reshape(k_dim, MLA_HEADS, a)
    wb = w[:, MLA_HEADS * a:].reshape(k_dim, MLA_HEADS, b)
    return jnp.concatenate([wa, wb], axis=2).reshape(k_dim, -1)


def _pad_lanes(v, width=LANE):
    return jnp.concatenate([v, jnp.zeros((v.shape[0], width - v.shape[1]), v.dtype)], axis=1)


def _local_step(x, p, positions, tgt, W, P):
    s_dim = x.shape[0]
    inv_freq = 1.0 / (ROPE_BASE ** (jnp.arange(0, MLA_ROPE, 2, dtype=F32) / MLA_ROPE))
    ang = positions.astype(F32)[:, None] * inv_freq
    cos, sin = jnp.cos(ang), jnp.sin(ang)
    cos32 = jnp.concatenate([cos, cos], axis=1)
    sin32 = jnp.concatenate([-sin, sin], axis=1)
    cos512, sin512 = jnp.tile(cos32, (1, 16)), jnp.tile(sin32, (1, 16))
    cos128, sin128 = jnp.tile(cos32, (1, 4)), jnp.tile(sin32, (1, 4))
    bias_p, alog_p = _pad_lanes(P["ssd_dt_bias"]), _pad_lanes(P["ssd_A_log"])
    d_x = jnp.repeat(P["ssd_D"], SSD_HEAD_DIM, axis=1)

    proj = _mm(x, W["w_in"], name="mm_in")
    z, xbc, dtr = proj[:, :1024], proj[:, 1024:2560], proj[:, 2560:2688]
    qc, kvc, kr = proj[:, 2688:3072], proj[:, 3072:3328], proj[:, 3328:3456]
    xbca = _conv_fwd(xbc, P["ssd_conv_w"], P["ssd_conv_b"])
    y, states = _ssd_fwd(xbca, dtr, bias_p, alog_p, d_x)
    (yssd,) = _rowwise(_gate_rms, [y, z], [P["ssd_norm_w"]], [1024], name="ssd_gate_norm")
    (qn,) = _rowwise(_rms, [qc], [P["mla_q_norm_w"]], [MLA_Q_RANK], name="q_norm")
    (kvn,) = _rowwise(_rms, [kvc], [P["mla_kv_norm_w"]], [MLA_KV_RANK], name="kv_norm")
    q = _mm(qn, W["mla_w_q_b"], name="mm_q")
    kv = _mm(kvn, W["mla_w_kv_b"], name="mm_kv")
    (qr,) = _rowwise(_rope_fwd_fn, [(q, 512, 2), cos512, sin512], [], [512], name="rope_q")
    (krt,) = _rowwise(lambda u, c, s: _spread4(_rope_fwd_fn(u, c, s)), [kr, cos128, sin128], [], [LANE], name="rope_k")
    att = _att_fwd(q, qr, kv, krt)
    (ymla,) = _rowwise(_rms, [att], [P["mla_out_norm_w"]], [1024], name="out_norm")
    ycat = jnp.concatenate([yssd, ymla], axis=1)
    mix = _mm(ycat, W["w_out"], name="mm_out")
    f_h1 = lambda xv, mv, g, b: _ln(ALPHA * xv + mv, g, b)
    (h1,) = _rowwise(f_h1, [x, mix], [P["ln_mix_g"], P["ln_mix_b"]], [1024], name="ln_mix")
    gup = _mm(h1, W["w_gup"], name="mm_gup")
    pg = _mm(h1, W["w_ple_gate"], name="mm_ple_gate")
    pp = _mm(p, W["w_ple_proj"], name="mm_ple")
    (act,) = _rowwise(lambda g, u: _silu(g) * u, [(gup, D_FF, 0), (gup, D_FF, 1)], [], [D_FF], name="swiglu", tr=128)
    ffn = _mm(act, W["w_ffn_down"], name="mm_down")

    f_h2 = lambda hv, fv, pg, ppv, g, b: _ln(ALPHA * hv + fv + _sigmoid(pg) * ppv, g, b)

    def final_fn(hv, fv, pg, ppv, tv, g, b):
        h2, pull = jax.vjp(f_h2, hv, fv, pg, ppv, g, b)
        diff = h2 - tv
        loss = 0.5 * jnp.sum(jnp.mean(diff * diff, axis=-1, keepdims=True), axis=0, keepdims=True)
        d_h, d_f, d_pg, d_pp, d_g, d_b = pull(diff * (1.0 / D_MODEL))
        return d_h, d_f, d_pg, d_pp, d_g, d_b, jnp.broadcast_to(loss, (1, LANE))

    dh1_a, dffn, dpg, dpp, g_ffn_g, g_ffn_b, loss = _rowwise(
        final_fn, [h1, ffn, pg, pp, tgt], [P["ln_ffn_g"], P["ln_ffn_b"]], [1024] * 4, [1024, 1024, LANE], name="final")

    G = {}
    dact = _mm(dffn, W["w_ffn_down"], tb=True, name="mm_down_dx")
    G["w_ffn_down"] = _mm(act, dffn, ta=True, name="mm_down_dw")

    def swiglu_bwd(g, u, d):
        sg = _sigmoid(g)
        return jnp.concatenate([d * u * (sg * (1.0 + g * (1.0 - sg))), d * (g * sg)], axis=1)

    (dgup,) = _rowwise(swiglu_bwd, [(gup, D_FF, 0), (gup, D_FF, 1), dact], [], [2 * D_FF], name="swiglu_bwd", tr=128)
    dh1 = _mm(dgup, W["w_gup"], tb=True, add=dh1_a, name="mm_gup_dx")
    dh1 = _mm(dpg, W["w_ple_gate"], tb=True, add=dh1, name="mm_ple_gate_dx")
    G["w_gup"] = _mm(h1, dgup, ta=True, name="mm_gup_dw")
    G["w_ple_gate"] = _mm(h1, dpg, ta=True, name="mm_ple_gate_dw")
    G["w_ple_proj"] = _mm(p, dpp, ta=True, name="mm_ple_dw")
    dx_a, dmix, g_mix_g, g_mix_b = _rowwise(
        lambda xv, mv, dv, g, b: _vjp_rows(f_h1)(xv, mv, g, b, dv), [x, mix, dh1], [P["ln_mix_g"], P["ln_mix_b"]],
        [1024, 1024], [1024, 1024], name="ln_mix_bwd")
    dycat = _mm(dmix, W["w_out"], tb=True, name="mm_out_dx")
    G["w_out"] = _mm(ycat, dmix, ta=True, name="mm_out_dw")

    datt, g_out_norm = _rowwise(lambda a, dv, w: _vjp_rows(_rms)(a, w, dv), [att, (dycat, 1024, 1)],
                                [P["mla_out_norm_w"]], [1024], [1024], name="out_norm_bwd")
    dqn_nope, dqr, dkn, dv, dkrt = _att_bwd(q, qr, kv, krt, att, datt)
    dkv = jnp.concatenate([dkn, dv], axis=1)
    (dq_rope,) = _rowwise(_rope_bwd_fn, [dqr, cos512, sin512], [], [512], name="rope_q_bwd")

    def rope_k_bwd(d0, d1, d2, d3, c, s):
        d = _spread4(d0 + d1 + d2 + d3)
        lane = lax.broadcasted_iota(jnp.int32, d.shape, 1)
        return _rope_bwd_fn(jnp.where(lane < MLA_ROPE, d, 0.0), c, s)

    (dkr,) = _rowwise(rope_k_bwd, [dkrt[0], dkrt[1], dkrt[2], dkrt[3], cos128, sin128], [], [LANE], name="rope_k_bwd")
    dq = jnp.concatenate([dqn_nope, dq_rope], axis=1)
    dqn = _mm(dq, W["mla_w_q_b"], tb=True, name="mm_q_dx")
    G["mla_w_q_b"] = _mm(qn, dq, ta=True, name="mm_q_dw")
    dkvn = _mm(dkv, W["mla_w_kv_b"], tb=True, name="mm_kv_dx")
    G["mla_w_kv_b"] = _mm(kvn, dkv, ta=True, name="mm_kv_dw")
    dqc, g_q_norm = _rowwise(lambda a, dv, w: _vjp_rows(_rms)(a, w, dv), [qc, dqn], [P["mla_q_norm_w"]],
                             [MLA_Q_RANK], [MLA_Q_RANK], name="q_norm_bwd")
    dkvc, g_kv_norm = _rowwise(lambda a, dv, w: _vjp_rows(_rms)(a, w, dv), [kvc, dkvn], [P["mla_kv_norm_w"]],
                               [MLA_KV_RANK], [MLA_KV_RANK], name="kv_norm_bwd")

    dy, dz, g_ssd_norm = _rowwise(lambda yv, zv, dv, w: _vjp_rows(_gate_rms)(yv, zv, w, dv), [y, z, (dycat, 1024, 0)],
                                  [P["ssd_norm_w"]], [1024, 1024], [1024], name="ssd_gate_norm_bwd")
    dxbca, ddtr, g_dt_bias, g_alog, g_d_x = _ssd_bwd(xbca, dtr, bias_p, alog_p, d_x, states, dy)
    da, g_conv_w, g_conv_b = _conv_bwd_pre(xbc, P["ssd_conv_w"], P["ssd_conv_b"], dxbca)
    dxbc = _conv_bwd_in(da, P["ssd_conv_w"])

    dproj = jnp.concatenate([dz, dxbc, ddtr, dqc, dkvc, dkr], axis=1)
    grad_x = _mm(dproj, W["w_in"], tb=True, add=dx_a, name="mm_in_dx")
    G["w_in"] = _mm(x, dproj, ta=True, name="mm_in_dw")

    small = {
        "loss": loss[:, :1], "ssd_conv_w": g_conv_w, "ssd_conv_b": g_conv_b, "ssd_dt_bias": g_dt_bias[:, :SSD_HEADS],
        "ssd_A_log": g_alog[:, :SSD_HEADS], "ssd_D": jnp.sum(g_d_x.reshape(SSD_HEADS, SSD_HEAD_DIM), axis=1)[None, :],
        "ssd_norm_w": g_ssd_norm, "mla_q_norm_w": g_q_norm, "mla_kv_norm_w": g_kv_norm, "mla_out_norm_w": g_out_norm,
        "ln_mix_g": g_mix_g, "ln_mix_b": g_mix_b, "ln_ffn_g": g_ffn_g, "ln_ffn_b": g_ffn_b,
    }
    return grad_x, G, small


def kernel(x, p, positions, w_in, ssd_conv_w, ssd_conv_b, ssd_dt_bias, ssd_A_log, ssd_D, ssd_norm_w, mla_q_norm_w, mla_w_q_b, mla_kv_norm_w, mla_w_kv_b, mla_out_norm_w, w_out, ln_mix_g, ln_mix_b, w_ffn_gate, w_ffn_up, w_ffn_down, w_ple_gate, w_ple_proj, ln_ffn_g, ln_ffn_b, loss_target, m_w_in, m_ssd_conv_w, m_ssd_conv_b, m_ssd_dt_bias, m_ssd_A_log, m_ssd_D, m_ssd_norm_w, m_mla_q_norm_w, m_mla_w_q_b, m_mla_kv_norm_w, m_mla_w_kv_b, m_mla_out_norm_w, m_w_out, m_ln_mix_g, m_ln_mix_b, m_w_ffn_gate, m_w_ffn_up, m_w_ffn_down, m_w_ple_gate, m_w_ple_proj, m_ln_ffn_g, m_ln_ffn_b, v_w_in, v_ssd_conv_w, v_ssd_conv_b, v_ssd_dt_bias, v_ssd_A_log, v_ssd_D, v_ssd_norm_w, v_mla_q_norm_w, v_mla_w_q_b, v_mla_kv_norm_w, v_mla_w_kv_b, v_mla_out_norm_w, v_w_out, v_ln_mix_g, v_ln_mix_b, v_w_ffn_gate, v_w_ffn_up, v_w_ffn_down, v_w_ple_gate, v_w_ple_proj, v_ln_ffn_g, v_ln_ffn_b):
    args = dict(locals())
    big_names = [n for n, _ in BIG]
    shard = {n: args[n][0] for n in big_names}
    me = 4 * lax.axis_index("x") + 2 * lax.axis_index("y") + lax.axis_index("c")

    conv_sh = ssd_conv_w[0]
    conv_hi = conv_sh.astype(BF16)
    conv_lo = (conv_sh - conv_hi.astype(F32)).astype(BF16)
    conv_rows = jnp.concatenate([_flat(conv_hi), jnp.zeros((2, LANE), BF16), _flat(conv_lo), jnp.zeros((2, LANE), BF16)], axis=0)
    w_flat = jnp.concatenate([_flat(shard[n]).astype(BF16) for n in big_names]
                             + [conv_rows, jnp.zeros((FLAT_ROWS - BIG_ROWS - 16, LANE), BF16)], axis=0)
    gathered = _all_gather(w_flat, "gather_weights")
    gw = _unpack_big(gathered)
    full = {}
    for n in big_names:
        blk = gw[n].reshape((N_DEV,) + shard[n].shape)
        full[n] = blk.reshape(-1, blk.shape[2]) if n in ("w_out", "w_ffn_down", "w_ple_gate") else _cols_full(blk)
    conv_g = gathered[:, CONV_ROW0:CONV_ROW0 + 16, :].astype(F32)
    conv_full = _cols_full((conv_g[:, 0:6] + conv_g[:, 8:14]).reshape(N_DEV, 4, 192))
    W = {
        "w_in": _win_pad(full["w_in"]),
        "mla_w_q_b": _heads_split(full["mla_w_q_b"], MLA_NOPE, MLA_ROPE),
        "mla_w_kv_b": _heads_split(full["mla_w_kv_b"], MLA_NOPE, MLA_V),
        "w_out": full["w_out"],
        "w_gup": jnp.concatenate([full["w_ffn_gate"], full["w_ffn_up"]], axis=1),
        "w_ple_gate": full["w_ple_gate"],
        "w_ffn_down": full["w_ffn_down"],
        "w_ple_proj": full["w_ple_proj"],
    }
    P = {n: args[n] for n in REPL}
    P["ssd_conv_w"] = conv_full

    grad_x, G, small = _local_step(x[0], p[0, 0], positions[0], loss_target[0], W, P)

    gfull = {
        "w_in": _win_unpad(G["w_in"]),
        "mla_w_q_b": _heads_merge(G["mla_w_q_b"], MLA_NOPE, MLA_ROPE),
        "mla_w_kv_b": _heads_merge(G["mla_w_kv_b"], MLA_NOPE, MLA_V),
        "w_out": G["w_out"],
        "w_ffn_gate": G["w_gup"][:, :D_FF],
        "w_ffn_up": G["w_gup"][:, D_FF:2 * D_FF],
        "w_ffn_down": G["w_ffn_down"],
        "w_ple_gate": G["w_ple_gate"],
        "w_ple_proj": G["w_ple_proj"],
    }
    blocks = []
    for n, r in BIG:
        g = gfull[n]
        blk = g.reshape(N_DEV, -1, g.shape[1]) if n in ("w_out", "w_ffn_down", "w_ple_gate") else _cols_split(g)
        blocks.append(blk.reshape(N_DEV, r, LANE))
    blocks.append(jnp.zeros((N_DEV, FLAT_ROWS - BIG_ROWS, LANE), F32))
    g_blocks = jnp.concatenate(blocks, axis=1)
    recv = _all_to_all(g_blocks, "exchange_grads")

    flat3 = lambda pre: jnp.concatenate([_flat(args[pre + n][0]) for n in big_names]
                                        + [jnp.zeros((FLAT_ROWS - BIG_ROWS, LANE), F32)], axis=0)
    g_f, d_f, m_f, v_f = _adam(recv, flat3(""), flat3("m_"), flat3("v_"), "adam_big")

    small_all = _all_gather(_pack_small(small), "gather_small")
    rep_vals = lambda pre: _pack_small({**{n: args[pre + n] for n in REPL}, "loss": jnp.zeros((1,), F32),
                                        "ssd_conv_w": jnp.zeros((1,), F32)})
    g_s, d_s, m_s, v_s = _adam(small_all, rep_vals(""), rep_vals("m_"), rep_vals("v_"), "adam_small")
    shapes = {n: args[n].shape for n in REPL}
    shapes["loss"] = (1,)
    shapes["ssd_conv_w"] = (4, SSD_XBC)
    g_su = _unpack_small(g_s, shapes)
    conv_grad = lax.dynamic_slice_in_dim(g_su["ssd_conv_w"], me * 192, 192, axis=1)
    conv8 = lambda a: _pad_rows(_flat(a), 8)
    g_c, d_c, m_c, v_c = _adam(conv8(conv_grad)[None], conv8(conv_sh), conv8(m_ssd_conv_w[0]), conv8(v_ssd_conv_w[0]),
                               "adam_conv")

    def outputs(big_flat, small_flat, conv_flat):
        big = _unpack_big(big_flat)
        sm = _unpack_small(small_flat, shapes)
        out = []
        for n in ("w_in", "ssd_conv_w", "ssd_conv_b", "ssd_dt_bias", "ssd_A_log", "ssd_D", "ssd_norm_w", "mla_q_norm_w",
                  "mla_w_q_b", "mla_kv_norm_w", "mla_w_kv_b", "mla_out_norm_w", "w_out", "ln_mix_g", "ln_mix_b",
                  "w_ffn_gate", "w_ffn_up", "w_ffn_down", "w_ple_gate", "w_ple_proj", "ln_ffn_g", "ln_ffn_b"):
            if n == "ssd_conv_w":
                out.append(conv_flat[:6].reshape(1, 4, 192))
            elif n in big:
                out.append(big[n].reshape(args[n].shape))
            else:
                out.append(sm[n])
        return out

    loss = g_su["loss"].reshape(())
    return (loss, grad_x[None], *outputs(g_f, g_s, g_c), *outputs(d_f, d_s, d_c), *outputs(m_f, m_s, m_c),
            *outputs(v_f, v_s, v_c))
```

```python
import functools
import math

import numpy as np
import jax
import jax.numpy as jnp
from jax import lax
from jax.experimental import pallas as pl
from jax.experimental.pallas import tpu as pltpu

F32 = jnp.float32
BF16 = jnp.bfloat16
HI = lax.Precision.HIGHEST

N_DEV = 8
D_MODEL = 1024
PLE_DIM = 256
SSD_HEADS = 16
SSD_HEAD_DIM = 64
SSD_INNER = 1024
SSD_STATE = 128
SSD_XBC = 1536
SSD_CHUNK = 128
MLA_HEADS = 16
MLA_Q_RANK = 384
MLA_KV_RANK = 256
MLA_NOPE = 64
MLA_ROPE = 32
MLA_V = 64
ROPE_BASE = 10000.0
D_FF = 2816
IN_WIDTH = 3248
IN_PAD = 3456
ALPHA = 2.0 ** 0.25
EPS = 1e-6
LN_EPS = 1e-5
ATT_SCALE = 1.0 / math.sqrt(MLA_NOPE + MLA_ROPE)
ADAM_LR, ADAM_B1, ADAM_B2, ADAM_EPS, ADAM_WD, ADAM_STEP = 0.001, 0.9, 0.999, 1e-08, 0.01, 10

LANE = 128
MM_TM, MM_TN, MM_TK = 1024, 512, 2048
ROW_TILE = 256
ATT_TQ = 256

GRAD_DT = BF16

BIG = ("w_in", "mla_w_q_b", "mla_w_kv_b", "w_out", "w_ffn_gate", "w_ffn_up", "w_ffn_down", "w_ple_gate", "w_ple_proj")
ROW_SHARDED = ("w_out", "w_ffn_down", "w_ple_gate")
OWNER_BLOCKED = ("w_ffn_gate", "w_ffn_up", "w_ffn_down")
WEIGHT_ORDER = ("w_in", "ssd_conv_w", "ssd_conv_b", "ssd_dt_bias", "ssd_A_log", "ssd_D", "ssd_norm_w", "mla_q_norm_w",
                "mla_w_q_b", "mla_kv_norm_w", "mla_w_kv_b", "mla_out_norm_w", "w_out", "ln_mix_g", "ln_mix_b",
                "w_ffn_gate", "w_ffn_up", "w_ffn_down", "w_ple_gate", "w_ple_proj", "ln_ffn_g", "ln_ffn_b")


def _tile(dim, cap):
    if dim <= cap:
        return dim
    t = (cap // LANE) * LANE
    while dim % t:
        t -= LANE
    return t


def _dot(a, b, dims=(((1,), (0,)), ((), ())), precision=None):
    return lax.dot_general(a, b, dims, preferred_element_type=F32, precision=precision)


_NT = (((1,), (1,)), ((), ()))
_TN = (((0,), (0,)), ((), ()))


def _mm(a, b, *, ta=False, tb=False, a_blk=None, b_blk=None, o_blk=None, add=None, out_dtype=F32, name):
    ka, ma = a.shape[-2:] if ta else a.shape[-2:][::-1]
    nb, kb = b.shape[-2:] if tb else b.shape[-2:][::-1]
    m_dim = N_DEV * ma if a_blk == "m" else ma
    k_dim = N_DEV * ka if a_blk == "k" else ka
    n_dim = N_DEV * nb if b_blk == "n" else nb
    assert k_dim == (N_DEV * kb if b_blk == "k" else kb)
    tm = ma if a_blk == "m" else (m_dim // N_DEV if o_blk == "m" else _tile(m_dim, MM_TM))
    tn = nb if b_blk == "n" else (n_dim // N_DEV if o_blk == "n" else _tile(n_dim, MM_TN))
    tk = ka if a_blk == "k" else (kb if b_blk == "k" else _tile(k_dim, MM_TK))
    nk = k_dim // tk
    dims = (((0 if ta else 1,), (1 if tb else 0,)), ((), ()))
    has_add = add is not None

    def spec(tile, idx, lead):
        if lead is None:
            return pl.BlockSpec(tile, idx)
        return pl.BlockSpec((None,) + tile, lambda i, j, k: (lead(i, j, k),) + idx(i, j, k))

    def a_idx(i, j, k):
        ii, kk = (0 if a_blk == "m" else i), (0 if a_blk == "k" else k)
        return (kk, ii) if ta else (ii, kk)

    def b_idx(i, j, k):
        jj, kk = (0 if b_blk == "n" else j), (0 if b_blk == "k" else k)
        return (jj, kk) if tb else (kk, jj)

    def o_idx(i, j, k):
        return (0 if o_blk == "m" else i, 0 if o_blk == "n" else j)

    pick = {"m": lambda i, j, k: i, "n": lambda i, j, k: j, "k": lambda i, j, k: k, None: None}
    a_spec = spec((tk, tm) if ta else (tm, tk), a_idx, pick[a_blk])
    b_spec = spec((tn, tk) if tb else (tk, tn), b_idx, pick[b_blk])
    o_spec = spec((tm, tn), o_idx, pick[o_blk])

    def body(*refs):
        if has_add:
            a_ref, b_ref, add_ref, o_ref = refs[:4]
        else:
            a_ref, b_ref, o_ref = refs[:3]
        part = _dot(a_ref[...].astype(BF16), b_ref[...].astype(BF16), dims)
        if nk == 1:
            o_ref[...] = ((part + add_ref[...]) if has_add else part).astype(o_ref.dtype)
            return
        acc = refs[-1]
        k = pl.program_id(2)

        @pl.when(k == 0)
        def _():
            acc[...] = (part + add_ref[...]) if has_add else part

        @pl.when(k > 0)
        def _():
            acc[...] += part

        @pl.when(k == nk - 1)
        def _():
            o_ref[...] = acc[...].astype(o_ref.dtype)

    if o_blk == "m":
        out_shape = (N_DEV, tm, n_dim)
    elif o_blk == "n":
        out_shape = (N_DEV, m_dim, tn)
    else:
        out_shape = (m_dim, n_dim)
    ins = [a, b] + ([add] if has_add else [])
    specs = [a_spec, b_spec] + ([pl.BlockSpec((tm, tn), lambda i, j, k: (i, j))] if has_add else [])
    return pl.pallas_call(
        body, name=name, grid=(m_dim // tm, n_dim // tn, nk), in_specs=specs, out_specs=o_spec,
        out_shape=jax.ShapeDtypeStruct(out_shape, out_dtype),
        scratch_shapes=[pltpu.VMEM((tm, tn), F32)] if nk > 1 else [],
        compiler_params=pltpu.CompilerParams(dimension_semantics=("parallel", "parallel", "arbitrary")),
    )(*ins)


def _rowwise(fn, rows, consts, out_widths, acc_widths=(), *, name, tr=ROW_TILE):
    row_arrays, row_specs = [], []
    first_arr = rows[0][0] if isinstance(rows[0], tuple) else rows[0]
    s_dim = first_arr.shape[0]
    tr = min(tr, s_dim)
    for r in rows:
        arr, width, cb = r if isinstance(r, tuple) else (r, r.shape[1], 0)
        row_arrays.append(arr)
        row_specs.append(pl.BlockSpec((tr, width), functools.partial(lambda i, cb: (i, cb), cb=cb)))
    const_specs = [pl.BlockSpec(c.shape, lambda i: (0, 0)) for c in consts]
    nr, nc, no, na = len(rows), len(consts), len(out_widths), len(acc_widths)

    def body(*refs):
        ins = [r[...] for r in refs[:nr + nc]]
        res = fn(*ins)
        if not isinstance(res, (tuple, list)):
            res = (res,)
        out_refs = refs[nr + nc:nr + nc + no]
        acc_refs = refs[nr + nc + no:]
        for o_ref, val in zip(out_refs, res[:no]):
            o_ref[...] = val.astype(o_ref.dtype)
        first = pl.program_id(0) == 0
        for a_ref, val in zip(acc_refs, res[no:]):
            @pl.when(first)
            def _(a_ref=a_ref, val=val):
                a_ref[...] = val

            @pl.when(jnp.logical_not(first))
            def _(a_ref=a_ref, val=val):
                a_ref[...] += val

    out_shape = [jax.ShapeDtypeStruct((s_dim, w), F32) for w in out_widths]
    out_shape += [jax.ShapeDtypeStruct((1, w), F32) for w in acc_widths]
    out_specs = [pl.BlockSpec((tr, w), lambda i: (i, 0)) for w in out_widths]
    out_specs += [pl.BlockSpec((1, w), lambda i: (0, 0)) for w in acc_widths]
    res = pl.pallas_call(
        body, name=name, grid=(s_dim // tr,), in_specs=row_specs + const_specs, out_specs=out_specs, out_shape=out_shape,
        compiler_params=pltpu.CompilerParams(dimension_semantics=("arbitrary",)),
    )(*row_arrays, *consts)
    return res


def _colsum(v):
    return jnp.sum(v, axis=0, keepdims=True)


def _rms(u, g):
    return u * lax.rsqrt(jnp.mean(u * u, axis=-1, keepdims=True) + EPS) * g


def _ln(u, g, b):
    mu = jnp.mean(u, axis=-1, keepdims=True)
    d = u - mu
    var = jnp.mean(d * d, axis=-1, keepdims=True)
    return d * lax.rsqrt(var + LN_EPS) * g + b


def _sigmoid(v):
    return 1.0 / (1.0 + jnp.exp(-v))


def _silu(v):
    return v * _sigmoid(v)


def _softplus(v):
    y = jnp.exp(-jnp.abs(v))
    w = 1.0 + y
    log1p = jnp.where(w == 1.0, y, jnp.log(w) * y / jnp.where(w == 1.0, 1.0, w - 1.0))
    return jnp.maximum(v, 0.0) + log1p


def _gate_rms(y, z, w):
    return _rms(y * _silu(z), w)


def _vjp_rows(f):
    def fn(*args):
        prim, ct = args[:-1], args[-1]
        _, pull = jax.vjp(f, *prim)
        return pull(ct)
    return fn


def _conv_pre(cur, prev, w, b, first):
    row = lax.broadcasted_iota(jnp.int32, cur.shape, 0)
    acc = cur * w[3:4, :] + b
    for j in (1, 2, 3):
        tail = jnp.where(first, 0.0, pltpu.roll(prev, j, 0))
        acc = acc + jnp.where(row >= j, pltpu.roll(cur, j, 0), tail) * w[3 - j:4 - j, :]
    return acc


def _conv_fwd(u, w, b, name="conv_fwd"):
    s_dim, width = u.shape
    tr = min(ROW_TILE, s_dim)

    def body(cur_ref, prev_ref, w_ref, b_ref, o_ref):
        pre = _conv_pre(cur_ref[...], prev_ref[...], w_ref, b_ref[...], pl.program_id(0) == 0)
        o_ref[...] = _silu(pre)

    return pl.pallas_call(
        body, name=name, grid=(s_dim // tr,),
        in_specs=[pl.BlockSpec((tr, width), lambda i: (i, 0)), pl.BlockSpec((tr, width), lambda i: (jnp.maximum(i - 1, 0), 0)),
                  pl.BlockSpec(w.shape, lambda i: (0, 0)), pl.BlockSpec(b.shape, lambda i: (0, 0))],
        out_specs=pl.BlockSpec((tr, width), lambda i: (i, 0)), out_shape=jax.ShapeDtypeStruct((s_dim, width), F32),
        compiler_params=pltpu.CompilerParams(dimension_semantics=("arbitrary",)),
    )(u, u, w, b)


def _conv_bwd_pre(u, w, b, dact, name="conv_bwd_pre"):
    s_dim, width = u.shape
    tr = min(ROW_TILE, s_dim)

    def body(cur_ref, prev_ref, w_ref, b_ref, d_ref, da_ref, dw_ref, db_ref):
        first = pl.program_id(0) == 0
        cur, prev = cur_ref[...], prev_ref[...]
        pre = _conv_pre(cur, prev, w_ref, b_ref[...], first)
        sg = _sigmoid(pre)
        da = d_ref[...] * (sg * (1.0 + pre * (1.0 - sg)))
        da_ref[...] = da
        row = lax.broadcasted_iota(jnp.int32, cur.shape, 0)

        @pl.when(first)
        def _():
            dw_ref[...] = jnp.zeros_like(dw_ref)
            db_ref[...] = jnp.zeros_like(db_ref)

        db_ref[...] += _colsum(da)
        dw_ref[3:4, :] += _colsum(da * cur)
        for j in (1, 2, 3):
            tail = jnp.where(first, 0.0, pltpu.roll(prev, j, 0))
            sh = jnp.where(row >= j, pltpu.roll(cur, j, 0), tail)
            dw_ref[3 - j:4 - j, :] += _colsum(da * sh)

    return pl.pallas_call(
        body, name=name, grid=(s_dim // tr,),
        in_specs=[pl.BlockSpec((tr, width), lambda i: (i, 0)), pl.BlockSpec((tr, width), lambda i: (jnp.maximum(i - 1, 0), 0)),
                  pl.BlockSpec(w.shape, lambda i: (0, 0)), pl.BlockSpec(b.shape, lambda i: (0, 0)),
                  pl.BlockSpec((tr, width), lambda i: (i, 0))],
        out_specs=[pl.BlockSpec((tr, width), lambda i: (i, 0)), pl.BlockSpec(w.shape, lambda i: (0, 0)),
                   pl.BlockSpec(b.shape, lambda i: (0, 0))],
        out_shape=[jax.ShapeDtypeStruct((s_dim, width), F32), jax.ShapeDtypeStruct(w.shape, F32),
                   jax.ShapeDtypeStruct(b.shape, F32)],
        compiler_params=pltpu.CompilerParams(dimension_semantics=("arbitrary",)),
    )(u, u, w, b, dact)


def _conv_bwd_in(da, w, name="conv_bwd_in"):
    s_dim, width = da.shape
    tr = min(ROW_TILE, s_dim)
    n = s_dim // tr

    def body(cur_ref, nxt_ref, w_ref, o_ref):
        last = pl.program_id(0) == n - 1
        cur, nxt = cur_ref[...], nxt_ref[...]
        row = lax.broadcasted_iota(jnp.int32, cur.shape, 0)
        acc = cur * w_ref[3:4, :]
        for j in (1, 2, 3):
            head = jnp.where(last, 0.0, pltpu.roll(nxt, tr - j, 0))
            acc = acc + jnp.where(row < tr - j, pltpu.roll(cur, tr - j, 0), head) * w_ref[3 - j:4 - j, :]
        o_ref[...] = acc

    return pl.pallas_call(
        body, name=name, grid=(n,),
        in_specs=[pl.BlockSpec((tr, width), lambda i: (i, 0)), pl.BlockSpec((tr, width), lambda i: (jnp.minimum(i + 1, n - 1), 0)),
                  pl.BlockSpec(w.shape, lambda i: (0, 0))],
        out_specs=pl.BlockSpec((tr, width), lambda i: (i, 0)), out_shape=jax.ShapeDtypeStruct((s_dim, width), F32),
        compiler_params=pltpu.CompilerParams(dimension_semantics=("arbitrary",)),
    )(da, da, w)


def _ssd_consts():
    L = SSD_CHUNK
    tri = np.tril(np.ones((L, L), np.float32))
    expand = np.zeros((LANE, SSD_INNER), np.float32)
    expand128 = np.zeros((LANE, SSD_HEADS * LANE), np.float32)
    for h in range(SSD_HEADS):
        expand[h, h * SSD_HEAD_DIM:(h + 1) * SSD_HEAD_DIM] = 1.0
        expand128[h, h * LANE:(h + 1) * LANE] = 1.0
    return jnp.asarray(tri), jnp.asarray(expand), jnp.asarray(expand128), jnp.asarray(expand.T.copy())


def _ssd_prep(dt_ref, bias_ref, alog_ref, tri_ref, exp_ref, exp128_ref, cs_s, cst_s, ex_s, csx_s):
    L = SSD_CHUNK
    dt = _softplus(dt_ref[...] + bias_ref[...])
    a = -jnp.exp(alog_ref[...])
    cs = _dot(tri_ref[...], dt * a, precision=HI)
    cs_s[...] = cs
    cst_s[...] = cs.T
    last = cs_s[L - 1:L, :]
    expand = exp_ref[...]
    ex_s[...] = _dot(jnp.exp(cs), expand, precision=HI)
    f_x = _dot(jnp.exp(last - cs), expand, precision=HI)
    dt_x = _dot(dt, expand, precision=HI)
    csx_s[...] = _dot(cs, exp128_ref[...], precision=HI)
    t_x = ex_s[L - 1:L, :]
    return dt, a, dt_x, f_x, t_x


def _decay_matrix(csx_s, cst_s, h, tril):
    seg = csx_s[:, h * LANE:(h + 1) * LANE] - cst_s[h:h + 1, :]
    return jnp.exp(jnp.where(tril, seg, -jnp.inf))


def _ssd_fwd(xbca, dtr, bias, alog, d_x, name="ssd_fwd"):
    s_dim = xbca.shape[0]
    L = SSD_CHUNK
    nc = s_dim // L
    tri, expand, expand128, _ = _ssd_consts()

    def body(xs_ref, b_ref, c_ref, dt_ref, bias_ref, alog_ref, dx_ref, tri_ref, exp_ref, exp128_ref,
             y_ref, st_ref, st_s, cs_s, cst_s, ex_s, csx_s):
        @pl.when(pl.program_id(0) == 0)
        def _():
            st_s[...] = jnp.zeros_like(st_s)

        dt, a, dt_x, f_x, t_x = _ssd_prep(dt_ref, bias_ref, alog_ref, tri_ref, exp_ref, exp128_ref, cs_s, cst_s, ex_s, csx_s)
        st_ref[0] = st_s[...]
        row = lax.broadcasted_iota(jnp.int32, (L, L), 0)
        col = lax.broadcasted_iota(jnp.int32, (L, L), 1)
        tril = row >= col
        low = col < SSD_HEAD_DIM
        for g in range(2):
            bg = b_ref[:, g * LANE:(g + 1) * LANE]
            cg = c_ref[:, g * LANE:(g + 1) * LANE].astype(BF16)
            gmat = _dot(cg, bg.astype(BF16), _NT)
            bgt = bg.T.astype(BF16)
            for jj in range(4):
                j = 4 * g + jj
                sl = slice(j * LANE, (j + 1) * LANE)
                xp = xs_ref[:, sl]
                x_dt = xp * dt_x[:, sl]
                xb = x_dt.astype(BF16)
                yd = []
                for e in range(2):
                    lm = _decay_matrix(csx_s, cst_s, 2 * j + e, tril)
                    yd.append(_dot((gmat * lm).astype(BF16), xb))
                stp = st_s[j]
                z = _dot(cg, stp.astype(BF16))
                y_ref[:, sl] = jnp.where(low, yd[0], yd[1]) + ex_s[:, sl] * z + dx_ref[:, sl] * xp
                xf = (x_dt * f_x[:, sl]).astype(BF16)
                st_s[j] = t_x[:, sl] * stp + _dot(bgt, xf)

    const = lambda shape: pl.BlockSpec(shape, lambda c: tuple(0 for _ in shape))
    return pl.pallas_call(
        body, name=name, grid=(nc,),
        in_specs=[pl.BlockSpec((L, 1024), lambda c: (c, 0)), pl.BlockSpec((L, 256), lambda c: (c, 4)),
                  pl.BlockSpec((L, 256), lambda c: (c, 5)), pl.BlockSpec((L, LANE), lambda c: (c, 0)),
                  const((1, LANE)), const((1, LANE)), const((1, 1024)), const((L, L)), const((LANE, 1024)),
                  const((LANE, 2048))],
        out_specs=[pl.BlockSpec((L, 1024), lambda c: (c, 0)), pl.BlockSpec((1, 8, LANE, LANE), lambda c: (c, 0, 0, 0))],
        out_shape=[jax.ShapeDtypeStruct((s_dim, 1024), F32), jax.ShapeDtypeStruct((nc, 8, LANE, LANE), F32)],
        scratch_shapes=[pltpu.VMEM((8, LANE, LANE), F32), pltpu.VMEM((L, LANE), F32), pltpu.VMEM((LANE, L), F32),
                        pltpu.VMEM((L, 1024), F32), pltpu.VMEM((L, 2048), F32)],
        compiler_params=pltpu.CompilerParams(dimension_semantics=("arbitrary",)),
    )(xbca, xbca, xbca, dtr, bias, alog, d_x, tri, expand, expand128)


def _ssd_bwd(xbca, dtr, bias, alog, d_x, states, dy, name="ssd_bwd"):
    s_dim = xbca.shape[0]
    L = SSD_CHUNK
    nc = s_dim // L
    tri, expand, expand128, expand_t = _ssd_consts()

    def body(xs_ref, b_ref, c_ref, dt_ref, bias_ref, alog_ref, dx_ref, tri_ref, exp_ref, exp128_ref, expt_ref,
             st_ref, dy_ref, dxbc_ref, ddt_ref, dbias_ref, dalog_ref, dd_ref,
             dst_s, cs_s, cst_s, ex_s, csx_s, dcsx_s, ddtx_s, dcol_s, drow_s, dlast_s, dd_s):
        @pl.when(pl.program_id(0) == 0)
        def _():
            dst_s[...] = jnp.zeros_like(dst_s)
            dbias_ref[...] = jnp.zeros_like(dbias_ref)
            dalog_ref[...] = jnp.zeros_like(dalog_ref)
            dd_s[...] = jnp.zeros_like(dd_s)

        dt, a, dt_x, f_x, t_x = _ssd_prep(dt_ref, bias_ref, alog_ref, tri_ref, exp_ref, exp128_ref, cs_s, cst_s, ex_s, csx_s)
        row = lax.broadcasted_iota(jnp.int32, (L, L), 0)
        col = lax.broadcasted_iota(jnp.int32, (L, L), 1)
        tril = row >= col
        low = col < SSD_HEAD_DIM
        dcol_s[...] = jnp.zeros_like(dcol_s)
        drow_s[...] = jnp.zeros_like(drow_s)
        for g in range(2):
            bg = b_ref[:, g * LANE:(g + 1) * LANE]
            cg = c_ref[:, g * LANE:(g + 1) * LANE]
            bgb, cgb = bg.astype(BF16), cg.astype(BF16)
            gmat = _dot(cgb, bgb, _NT)
            d_g = jnp.zeros((L, L), F32)
            d_b = jnp.zeros((L, LANE), F32)
            d_c = jnp.zeros((L, LANE), F32)
            for jj in range(4):
                j = 4 * g + jj
                sl = slice(j * LANE, (j + 1) * LANE)
                xp = xs_ref[:, sl]
                dtp = dt_x[:, sl]
                x_dt = xp * dtp
                xb = x_dt.astype(BF16)
                dyp = dy_ref[:, sl]
                dd_s[:, sl] += _colsum(dyp * xp)
                d_xdt = jnp.zeros((L, LANE), F32)
                for e in range(2):
                    h = 2 * j + e
                    lm = _decay_matrix(csx_s, cst_s, h, tril)
                    m = gmat * lm
                    dye = jnp.where(low if e == 0 else jnp.logical_not(low), dyp, 0.0).astype(BF16)
                    d_m = jnp.where(tril, _dot(dye, xb, _NT), 0.0)
                    d_xdt = d_xdt + _dot(m.astype(BF16), dye, _TN)
                    d_g = d_g + d_m * lm
                    w = d_m * m
                    dcol_s[...] += jnp.where(col == h, jnp.sum(w, axis=1, keepdims=True), 0.0)
                    drow_s[...] += jnp.where(row == h, jnp.sum(w, axis=0, keepdims=True), 0.0)
                stp = st_ref[0, j]
                stb = stp.astype(BF16)
                dstn = dst_s[j]
                dstb = dstn.astype(BF16)
                e_p = ex_s[:, sl]
                f_p = f_x[:, sl]
                t_p = t_x[:, sl]
                z = _dot(cgb, stb)
                d_z = (e_p * dyp).astype(BF16)
                d_c = d_c + _dot(d_z, stb, _NT)
                d_xf = _dot(bgb, dstb)
                d_b = d_b + _dot((x_dt * f_p).astype(BF16), dstb, _NT)
                d_xdt = d_xdt + f_p * d_xf
                d_f = x_dt * d_xf * f_p
                dcsx_s[:, sl] = dyp * e_p * z - d_f
                dlast_s[:, sl] = _colsum(d_f) + _colsum(dstn * stp) * t_p
                dst_s[j] = _dot(cgb, d_z, _TN) + t_p * dstn
                dxbc_ref[:, sl] = dx_ref[:, sl] * dyp + d_xdt * dtp
                ddtx_s[:, sl] = d_xdt * xp
            d_gb = d_g.astype(BF16)
            dxbc_ref[:, 1024 + g * LANE:1024 + (g + 1) * LANE] = d_b + _dot(d_gb, cgb, _TN)
            dxbc_ref[:, 1280 + g * LANE:1280 + (g + 1) * LANE] = d_c + _dot(d_gb, bgb)

        expt = expt_ref[...]
        dlast = _dot(jnp.broadcast_to(dlast_s[...], (8, 1024)), expt, precision=HI)
        d_cs = dcol_s[...] - drow_s[...].T + _dot(dcsx_s[...], expt, precision=HI)
        rown = lax.broadcasted_iota(jnp.int32, (L, LANE), 0)
        d_cs = d_cs + jnp.where(rown == L - 1, jnp.sum(dlast, axis=0, keepdims=True) * 0.125, 0.0)
        d_da = _dot(tri_ref[...], d_cs, _TN, precision=HI)
        d_dt = d_da * a + _dot(ddtx_s[...], expt, precision=HI)
        dalog_ref[...] += _colsum(d_da * dt) * a
        d_raw = d_dt * _sigmoid(dt_ref[...] + bias_ref[...])
        ddt_ref[...] = d_raw
        dbias_ref[...] += _colsum(d_raw)
        dd8 = _dot(jnp.broadcast_to(dd_s[...], (8, 1024)), expt, precision=HI)
        dd_ref[...] = jnp.sum(dd8, axis=0, keepdims=True) * 0.125

    const = lambda shape: pl.BlockSpec(shape, lambda c: tuple(0 for _ in shape))
    rev = lambda cb: (lambda c: (nc - 1 - c, cb))
    return pl.pallas_call(
        body, name=name, grid=(nc,),
        in_specs=[pl.BlockSpec((L, 1024), rev(0)), pl.BlockSpec((L, 256), rev(4)), pl.BlockSpec((L, 256), rev(5)),
                  pl.BlockSpec((L, LANE), rev(0)), const((1, LANE)), const((1, LANE)), const((1, 1024)), const((L, L)),
                  const((LANE, 1024)), const((LANE, 2048)), const((1024, LANE)),
                  pl.BlockSpec((1, 8, LANE, LANE), lambda c: (nc - 1 - c, 0, 0, 0)), pl.BlockSpec((L, 1024), rev(0))],
        out_specs=[pl.BlockSpec((L, SSD_XBC), rev(0)), pl.BlockSpec((L, LANE), rev(0)), const((1, LANE)), const((1, LANE)),
                   const((1, LANE))],
        out_shape=[jax.ShapeDtypeStruct((s_dim, SSD_XBC), F32), jax.ShapeDtypeStruct((s_dim, LANE), F32),
                   jax.ShapeDtypeStruct((1, LANE), F32), jax.ShapeDtypeStruct((1, LANE), F32),
                   jax.ShapeDtypeStruct((1, LANE), F32)],
        scratch_shapes=[pltpu.VMEM((8, LANE, LANE), F32), pltpu.VMEM((L, LANE), F32), pltpu.VMEM((LANE, L), F32),
                        pltpu.VMEM((L, 1024), F32), pltpu.VMEM((L, 2048), F32), pltpu.VMEM((L, 1024), F32),
                        pltpu.VMEM((L, 1024), F32), pltpu.VMEM((L, LANE), F32), pltpu.VMEM((LANE, L), F32),
                        pltpu.VMEM((1, 1024), F32), pltpu.VMEM((1, 1024), F32)],
        compiler_params=pltpu.CompilerParams(dimension_semantics=("arbitrary",)),
    )(xbca, xbca, xbca, dtr, bias, alog, d_x, tri, expand, expand128, expand_t, states, dy)


def _swap_halves(u):
    width = u.shape[1]
    lane = lax.broadcasted_iota(jnp.int32, u.shape, 1)
    return jnp.where(lane % MLA_ROPE < MLA_ROPE // 2, pltpu.roll(u, width - MLA_ROPE // 2, 1), pltpu.roll(u, MLA_ROPE // 2, 1))


def _rope_fwd_fn(u, cos, sin):
    return u * cos + _swap_halves(u) * sin


def _rope_bwd_fn(d, cos, sin):
    return d * cos + _swap_halves(d * sin)


def _spread4(v):
    return v + pltpu.roll(v, 32, 1) + pltpu.roll(v, 64, 1) + pltpu.roll(v, 96, 1)


def _att_masks(tq):
    lane = lax.broadcasted_iota(jnp.int32, (tq, LANE), 1)
    return lane // MLA_NOPE, lane // MLA_ROPE


def _att_probs(qcat, kcat, q0, s_dim):
    tq = qcat.shape[0]
    s = _dot(qcat, kcat, _NT) * ATT_SCALE
    qpos = q0 + lax.broadcasted_iota(jnp.int32, (tq, s_dim), 0)
    kpos = lax.broadcasted_iota(jnp.int32, (tq, s_dim), 1)
    s = jnp.where(qpos >= kpos, s, -jnp.inf)
    p = jnp.exp(s - jnp.max(s, axis=1, keepdims=True))
    return p / jnp.sum(p, axis=1, keepdims=True)


def _att_fwd(q, qr, kv, krt, name="att_fwd"):
    s_dim = q.shape[0]
    tq = min(ATT_TQ, s_dim)

    def body(qn_ref, qr_ref, kn_ref, krt_ref, v_ref, o_ref):
        q0 = pl.program_id(1) * tq
        half_id, grp_id = _att_masks(tq)
        krt = krt_ref[...]
        for half in range(2):
            hs = slice(half * LANE, (half + 1) * LANE)
            kcat = jnp.concatenate([kn_ref[:, hs], krt], axis=1).astype(BF16)
            vb = v_ref[:, hs].astype(BF16)
            outs = []
            for par in range(2):
                e = 2 * half + par
                qcat = jnp.concatenate([jnp.where(half_id == par, qn_ref[:, hs], 0.0),
                                        jnp.where(grp_id == e, qr_ref[...], 0.0)], axis=1).astype(BF16)
                p = _att_probs(qcat, kcat, q0, s_dim)
                outs.append(_dot(p.astype(BF16), vb))
            o_ref[:, hs] = jnp.where(half_id == 0, outs[0], outs[1])

    return pl.pallas_call(
        body, name=name, grid=(4, s_dim // tq),
        in_specs=[pl.BlockSpec((tq, 256), lambda m, i: (i, m)), pl.BlockSpec((tq, LANE), lambda m, i: (i, m)),
                  pl.BlockSpec((s_dim, 256), lambda m, i: (0, m)), pl.BlockSpec((s_dim, LANE), lambda m, i: (0, 0)),
                  pl.BlockSpec((s_dim, 256), lambda m, i: (0, 4 + m))],
        out_specs=pl.BlockSpec((tq, 256), lambda m, i: (i, m)), out_shape=jax.ShapeDtypeStruct((s_dim, 1024), F32),
        compiler_params=pltpu.CompilerParams(dimension_semantics=("parallel", "arbitrary")),
    )(q, qr, kv, krt, kv)


def _att_bwd(q, qr, kv, krt, o, do, name="att_bwd"):
    s_dim = q.shape[0]
    tq = min(ATT_TQ, s_dim)

    def body(qn_ref, qr_ref, kn_ref, krt_ref, v_ref, o_ref, do_ref, dqn_ref, dqr_ref, dkn_ref, dv_ref, dkrt_ref):
        @pl.when(pl.program_id(1) == 0)
        def _():
            dkn_ref[...] = jnp.zeros_like(dkn_ref)
            dv_ref[...] = jnp.zeros_like(dv_ref)
            dkrt_ref[...] = jnp.zeros_like(dkrt_ref)

        q0 = pl.program_id(1) * tq
        half_id, grp_id = _att_masks(tq)
        krt = krt_ref[...]
        dqr = jnp.zeros((tq, LANE), F32)
        for half in range(2):
            hs = slice(half * LANE, (half + 1) * LANE)
            kcat = jnp.concatenate([kn_ref[:, hs], krt], axis=1).astype(BF16)
            vb = v_ref[:, hs].astype(BF16)
            dqn = jnp.zeros((tq, LANE), F32)
            for par in range(2):
                e = 2 * half + par
                qcat = jnp.concatenate([jnp.where(half_id == par, qn_ref[:, hs], 0.0),
                                        jnp.where(grp_id == e, qr_ref[...], 0.0)], axis=1).astype(BF16)
                p = _att_probs(qcat, kcat, q0, s_dim)
                dom = jnp.where(half_id == par, do_ref[:, hs], 0.0)
                d_p = _dot(dom.astype(BF16), vb, _NT)
                d_row = jnp.sum(dom * o_ref[:, hs], axis=1, keepdims=True)
                d_s = (p * (d_p - d_row) * ATT_SCALE).astype(BF16)
                dqcat = _dot(d_s, kcat)
                dqn = dqn + jnp.where(half_id == par, dqcat[:, :LANE], 0.0)
                dqr = dqr + jnp.where(grp_id == e, dqcat[:, LANE:], 0.0)
                dkcat = _dot(d_s, qcat, _TN)
                dkn_ref[:, hs] += dkcat[:, :LANE]
                dkrt_ref[0] += dkcat[:, LANE:]
                dv_ref[:, hs] += _dot(p.astype(BF16), dom.astype(BF16), _TN)
            dqn_ref[:, hs] = dqn
        dqr_ref[...] = dqr

    return pl.pallas_call(
        body, name=name, grid=(4, s_dim // tq),
        in_specs=[pl.BlockSpec((tq, 256), lambda m, i: (i, m)), pl.BlockSpec((tq, LANE), lambda m, i: (i, m)),
                  pl.BlockSpec((s_dim, 256), lambda m, i: (0, m)), pl.BlockSpec((s_dim, LANE), lambda m, i: (0, 0)),
                  pl.BlockSpec((s_dim, 256), lambda m, i: (0, 4 + m)), pl.BlockSpec((tq, 256), lambda m, i: (i, m)),
                  pl.BlockSpec((tq, 256), lambda m, i: (i, m))],
        out_specs=[pl.BlockSpec((tq, 256), lambda m, i: (i, m)), pl.BlockSpec((tq, LANE), lambda m, i: (i, m)),
                   pl.BlockSpec((s_dim, 256), lambda m, i: (0, m)), pl.BlockSpec((s_dim, 256), lambda m, i: (0, m)),
                   pl.BlockSpec((1, s_dim, LANE), lambda m, i: (m, 0, 0))],
        out_shape=[jax.ShapeDtypeStruct((s_dim, 1024), F32), jax.ShapeDtypeStruct((s_dim, 512), F32),
                   jax.ShapeDtypeStruct((s_dim, 1024), F32), jax.ShapeDtypeStruct((s_dim, 1024), F32),
                   jax.ShapeDtypeStruct((4, s_dim, LANE), F32)],
        compiler_params=pltpu.CompilerParams(dimension_semantics=("parallel", "arbitrary")),
    )(q, qr, kv, krt, kv, o, do)


def _all_gather(x, name):
    rows, width = x.shape

    def body(x_ref, out_ref, send_sems, recv_sems, local_sem):
        x_i, y_i, c_i = lax.axis_index("x"), lax.axis_index("y"), lax.axis_index("c")
        me, sibling = (x_i, y_i, c_i), (x_i, y_i, 1 - c_i)
        chips = [(1 - x_i, y_i), (x_i, 1 - y_i), (1 - x_i, 1 - y_i)]

        def slot(px, py, pc):
            return out_ref.at[4 * px + 2 * py + pc]

        def copy(k, block, to, src=None):
            return pltpu.make_async_remote_copy(
                src_ref=slot(*block) if src is None else src, dst_ref=slot(*block), send_sem=send_sems.at[k],
                recv_sem=recv_sems.at[k], device_id=to, device_id_type=pl.DeviceIdType.MESH)

        mine = pltpu.make_async_copy(x_ref, slot(*me), local_sem)
        mine.start()
        first = [copy(0, me, sibling, src=x_ref)]
        first += [copy(1 + j, me, (*chip, c_i), src=x_ref) for j, chip in enumerate(chips)]
        for cp in first:
            cp.start()
        passed = [copy(4 + j, (*chip, c_i), sibling) for j, chip in enumerate(chips)]
        for j, chip in enumerate(chips):
            copy(1 + j, (*chip, c_i), me).wait_recv()
            passed[j].start()
        copy(0, sibling, me).wait_recv()
        for j, chip in enumerate(chips):
            copy(4 + j, (*chip, 1 - c_i), me).wait_recv()
        for cp in first + passed:
            cp.wait_send()
        mine.wait()

    return pl.pallas_call(
        body, name=name, out_shape=jax.ShapeDtypeStruct((N_DEV, rows, width), x.dtype),
        in_specs=[pl.BlockSpec(memory_space=pl.ANY)], out_specs=pl.BlockSpec(memory_space=pl.ANY),
        scratch_shapes=[pltpu.SemaphoreType.DMA((7,)), pltpu.SemaphoreType.DMA((7,)), pltpu.SemaphoreType.DMA],
    )(x)


def _gather_many(shards, name):
    n_arr = len(shards)

    def body(*refs):
        x_refs, out_refs = refs[:n_arr], refs[n_arr:2 * n_arr]
        send_sems, recv_sems, local_sems = refs[2 * n_arr:]
        x_i, y_i, c_i = lax.axis_index("x"), lax.axis_index("y"), lax.axis_index("c")
        me, sibling = (x_i, y_i, c_i), (x_i, y_i, 1 - c_i)
        chips = [(1 - x_i, y_i), (x_i, 1 - y_i), (1 - x_i, 1 - y_i)]

        def copy(a, k, block, to, src=None):
            slot = out_refs[a].at[4 * block[0] + 2 * block[1] + block[2]]
            return pltpu.make_async_remote_copy(
                src_ref=slot if src is None else src, dst_ref=slot, send_sem=send_sems.at[a, k],
                recv_sem=recv_sems.at[a, k], device_id=to, device_id_type=pl.DeviceIdType.MESH)

        mine, first, passed = [], [], []
        for a in range(n_arr):
            mine.append(pltpu.make_async_copy(x_refs[a], out_refs[a].at[4 * x_i + 2 * y_i + c_i], local_sems.at[a]))
            mine[a].start()
            first.append([copy(a, 0, me, sibling, src=x_refs[a])]
                         + [copy(a, 1 + j, me, (*chip, c_i), src=x_refs[a]) for j, chip in enumerate(chips)])
            for cp in first[a]:
                cp.start()
            passed.append([copy(a, 4 + j, (*chip, c_i), sibling) for j, chip in enumerate(chips)])
        for j, chip in enumerate(chips):
            for a in range(n_arr):
                copy(a, 1 + j, (*chip, c_i), me).wait_recv()
                passed[a][j].start()
        for a in range(n_arr):
            copy(a, 0, sibling, me).wait_recv()
            for j, chip in enumerate(chips):
                copy(a, 4 + j, (*chip, 1 - c_i), me).wait_recv()
        for a in range(n_arr):
            for cp in first[a] + passed[a]:
                cp.wait_send()
            mine[a].wait()

    any_spec = pl.BlockSpec(memory_space=pl.ANY)
    return pl.pallas_call(
        body, name=name, out_shape=[jax.ShapeDtypeStruct((N_DEV,) + x.shape, x.dtype) for x in shards],
        in_specs=[any_spec] * n_arr, out_specs=[any_spec] * n_arr,
        scratch_shapes=[pltpu.SemaphoreType.DMA((n_arr, 7)), pltpu.SemaphoreType.DMA((n_arr, 7)),
                        pltpu.SemaphoreType.DMA((n_arr,))],
    )(*shards)


def _pair_exchange(grads, name):
    n_arr = len(grads)

    def body(*refs):
        g_refs, out_refs = refs[:n_arr], refs[n_arr:2 * n_arr]
        send_sems, recv_sems = refs[2 * n_arr:]
        x_i, y_i, c_i = lax.axis_index("x"), lax.axis_index("y"), lax.axis_index("c")
        copies = []
        for a in range(n_arr):
            for chip in range(4):
                copies.append(pltpu.make_async_remote_copy(
                    src_ref=g_refs[a].at[2 * chip + (1 - c_i)], dst_ref=out_refs[a].at[chip], send_sem=send_sems.at[a, chip],
                    recv_sem=recv_sems.at[a, chip], device_id=(x_i, y_i, 1 - c_i), device_id_type=pl.DeviceIdType.MESH))
        for cp in copies:
            cp.start()
        for cp in copies:
            cp.wait_recv()
        for cp in copies:
            cp.wait_send()

    any_spec = pl.BlockSpec(memory_space=pl.ANY)
    return pl.pallas_call(
        body, name=name, out_shape=[jax.ShapeDtypeStruct((4,) + g.shape[1:], g.dtype) for g in grads],
        in_specs=[any_spec] * n_arr, out_specs=[any_spec] * n_arr,
        scratch_shapes=[pltpu.SemaphoreType.DMA((n_arr, 4)), pltpu.SemaphoreType.DMA((n_arr, 4))],
    )(*grads)


def _chip_exchange(sums, name):
    n_arr = len(sums)

    def body(*refs):
        s_refs, out_refs = refs[:n_arr], refs[n_arr:2 * n_arr]
        send_sems, recv_sems, local_sems = refs[2 * n_arr:]
        x_i, y_i, c_i = lax.axis_index("x"), lax.axis_index("y"), lax.axis_index("c")
        my_chip = 2 * x_i + y_i
        copies, local = [], []
        for a in range(n_arr):
            local.append(pltpu.make_async_copy(s_refs[a].at[my_chip], out_refs[a].at[my_chip], local_sems.at[a]))
            local[a].start()
            for k in range(1, 4):
                px, py = x_i ^ (k >> 1), y_i ^ (k & 1)
                copies.append(pltpu.make_async_remote_copy(
                    src_ref=s_refs[a].at[2 * px + py], dst_ref=out_refs[a].at[my_chip], send_sem=send_sems.at[a, k - 1],
                    recv_sem=recv_sems.at[a, k - 1], device_id=(px, py, c_i), device_id_type=pl.DeviceIdType.MESH))
        for cp in copies:
            cp.start()
        for cp in copies:
            cp.wait_recv()
        for cp in copies:
            cp.wait_send()
        for cp in local:
            cp.wait()

    any_spec = pl.BlockSpec(memory_space=pl.ANY)
    return pl.pallas_call(
        body, name=name, out_shape=[jax.ShapeDtypeStruct(s.shape, s.dtype) for s in sums],
        in_specs=[any_spec] * n_arr, out_specs=[any_spec] * n_arr,
        scratch_shapes=[pltpu.SemaphoreType.DMA((n_arr, 3)), pltpu.SemaphoreType.DMA((n_arr, 3)),
                        pltpu.SemaphoreType.DMA((n_arr,))],
    )(*sums)


def _pair_sum(g, recv, core, name):
    _, rows, cols = g.shape
    tr = ROW_TILE if rows % ROW_TILE == 0 else rows

    def body(core_ref, g_ref, r_ref, o_ref):
        o_ref[...] = (g_ref[...].astype(F32) + r_ref[...].astype(F32)).astype(o_ref.dtype)

    grid_spec = pltpu.PrefetchScalarGridSpec(
        num_scalar_prefetch=1, grid=(4, rows // tr),
        in_specs=[pl.BlockSpec((None, tr, cols), lambda k, i, core_ref: (2 * k + core_ref[0], i, 0)),
                  pl.BlockSpec((None, tr, cols), lambda k, i, core_ref: (k, i, 0))],
        out_specs=pl.BlockSpec((None, tr, cols), lambda k, i, core_ref: (k, i, 0)))
    return pl.pallas_call(body, name=name, grid_spec=grid_spec, out_shape=jax.ShapeDtypeStruct((4, rows, cols), g.dtype))(
        core, g, recv)


def _adam_math(g, w, m, v):
    m_new = ADAM_B1 * m + (1.0 - ADAM_B1) * g
    v_new = ADAM_B2 * v + (1.0 - ADAM_B2) * (g * g)
    m_hat = m_new / (1.0 - ADAM_B1 ** ADAM_STEP)
    v_hat = v_new / (1.0 - ADAM_B2 ** ADAM_STEP)
    return -ADAM_LR * (m_hat / (jnp.sqrt(v_hat) + ADAM_EPS) + ADAM_WD * w), m_new, v_new


def _adam(slots, w, m, v, name):
    n_slot, rows, cols = slots.shape
    tr = ROW_TILE if rows % ROW_TILE == 0 else rows

    def body(s_ref, w_ref, m_ref, v_ref, g_ref, d_ref, mo_ref, vo_ref):
        g = s_ref[0].astype(F32)
        for k in range(1, n_slot):
            g = g + s_ref[k].astype(F32)
        g_ref[...] = g
        d_ref[...], mo_ref[...], vo_ref[...] = _adam_math(g, w_ref[...], m_ref[...], v_ref[...])

    spec = pl.BlockSpec((tr, cols), lambda i: (i, 0))
    return pl.pallas_call(
        body, name=name, grid=(rows // tr,), in_specs=[pl.BlockSpec((n_slot, tr, cols), lambda i: (0, i, 0)), spec, spec, spec],
        out_specs=[spec] * 4, out_shape=[jax.ShapeDtypeStruct((rows, cols), F32)] * 4,
        compiler_params=pltpu.CompilerParams(dimension_semantics=("parallel",)),
    )(slots, w, m, v)


PACK_ROWS, PACK_W = 24, 1536
REPL_W = (("ssd_conv_b", 1536), ("ssd_dt_bias", 16), ("ssd_A_log", 16), ("ssd_D", 16), ("ssd_norm_w", 1024),
          ("mla_q_norm_w", 384), ("mla_kv_norm_w", 256), ("mla_out_norm_w", 1024), ("ln_mix_g", 1024),
          ("ln_mix_b", 1024), ("ln_ffn_g", 1024), ("ln_ffn_b", 1024))
LOSS_ROW = 4 + len(REPL_W)


def _pack_small(conv_w_grad, grads, loss, name="pack_small"):
    def body(*refs):
        cw_ref, g_refs, loss_ref, o_ref = refs[0], refs[1:1 + len(REPL_W)], refs[1 + len(REPL_W)], refs[-1]
        o_ref[...] = jnp.zeros_like(o_ref)
        o_ref[0:4, :] = cw_ref[...]
        for i, g_ref in enumerate(g_refs):
            o_ref[4 + i:5 + i, 0:g_ref.shape[1]] = g_ref[...]
        o_ref[LOSS_ROW:LOSS_ROW + 1, 0:LANE] = loss_ref[...]

    return pl.pallas_call(body, name=name, out_shape=jax.ShapeDtypeStruct((PACK_ROWS, PACK_W), F32))(conv_w_grad, *grads, loss)


def _adam_small(gathered, wmv, name="adam_small"):
    def body(*refs):
        s_ref = refs[0]
        in_refs = refs[1:1 + 3 * len(REPL_W)]
        cw_ref, loss_ref = refs[1 + 3 * len(REPL_W)], refs[2 + 3 * len(REPL_W)]
        out_refs = refs[3 + 3 * len(REPL_W):-1]
        tot = refs[-1]
        acc = s_ref[0]
        for k in range(1, N_DEV):
            acc = acc + s_ref[k]
        tot[...] = acc
        cw_ref[...] = tot[0:4, :]
        loss_ref[...] = tot[LOSS_ROW:LOSS_ROW + 1, 0:LANE]
        for i, (_, width) in enumerate(REPL_W):
            g = tot[4 + i:5 + i, 0:width]
            w_ref, m_ref, v_ref = in_refs[3 * i:3 * i + 3]
            g_ref, d_ref, mo_ref, vo_ref = out_refs[4 * i:4 * i + 4]
            g_ref[...] = g
            d_ref[...], mo_ref[...], vo_ref[...] = _adam_math(g, w_ref[...], m_ref[...], v_ref[...])

    flat_in = [a for triple in wmv for a in triple]
    out_shape = [jax.ShapeDtypeStruct((4, PACK_W), F32), jax.ShapeDtypeStruct((1, LANE), F32)]
    for _, width in REPL_W:
        out_shape += [jax.ShapeDtypeStruct((1, width), F32)] * 4
    res = pl.pallas_call(body, name=name, out_shape=out_shape, scratch_shapes=[pltpu.VMEM((PACK_ROWS, PACK_W), F32)])(
        gathered, *flat_in)
    return res[0], res[1], [res[2 + 4 * i:6 + 4 * i] for i in range(len(REPL_W))]


def _cols_full(g):
    return jnp.transpose(g, (1, 0, 2)).reshape(g.shape[1], -1)


def _cols_split(full):
    k_dim, n_dim = full.shape
    return jnp.transpose(full.reshape(k_dim, N_DEV, n_dim // N_DEV), (1, 0, 2))


def _win_pad(w):
    z = lambda n: jnp.zeros((w.shape[0], n), w.dtype)
    return jnp.concatenate([w[:, :2576], z(112), w[:, 2576:], z(96)], axis=1)


def _win_unpad(w):
    return jnp.concatenate([w[:, :2576], w[:, 2688:3360]], axis=1)


def _heads_split(w, a, b):
    k_dim = w.shape[0]
    w3 = w.reshape(k_dim, MLA_HEADS, a + b)
    return jnp.concatenate([w3[:, :, :a].reshape(k_dim, -1), w3[:, :, a:].reshape(k_dim, -1)], axis=1)


def _heads_merge(w, a, b):
    k_dim = w.shape[0]
    wa = w[:, :MLA_HEADS * a].reshape(k_dim, MLA_HEADS, a)
    wb = w[:, MLA_HEADS * a:].reshape(k_dim, MLA_HEADS, b)
    return jnp.concatenate([wa, wb], axis=2).reshape(k_dim, -1)


def _pad_lanes(v, width=LANE):
    return jnp.concatenate([v, jnp.zeros((v.shape[0], width - v.shape[1]), v.dtype)], axis=1)


def _local_step(x, p, positions, tgt, W, P):
    s_dim = x.shape[0]
    inv_freq = 1.0 / (ROPE_BASE ** (jnp.arange(0, MLA_ROPE, 2, dtype=F32) / MLA_ROPE))
    ang = positions.astype(F32)[:, None] * inv_freq
    cos, sin = jnp.cos(ang), jnp.sin(ang)
    cos32 = jnp.concatenate([cos, cos], axis=1)
    sin32 = jnp.concatenate([-sin, sin], axis=1)
    cos512, sin512 = jnp.tile(cos32, (1, 16)), jnp.tile(sin32, (1, 16))
    cos128, sin128 = jnp.tile(cos32, (1, 4)), jnp.tile(sin32, (1, 4))
    bias_p, alog_p = _pad_lanes(P["ssd_dt_bias"]), _pad_lanes(P["ssd_A_log"])
    d_x = jnp.repeat(P["ssd_D"], SSD_HEAD_DIM, axis=1)

    proj = _mm(x, W["w_in"], name="mm_in")
    z, xbc, dtr = proj[:, :1024], proj[:, 1024:2560], proj[:, 2560:2688]
    qc, kvc, kr = proj[:, 2688:3072], proj[:, 3072:3328], proj[:, 3328:3456]
    xbca = _conv_fwd(xbc, P["ssd_conv_w"], P["ssd_conv_b"])
    y, states = _ssd_fwd(xbca, dtr, bias_p, alog_p, d_x)
    (yssd,) = _rowwise(_gate_rms, [y, z], [P["ssd_norm_w"]], [1024], name="ssd_gate_norm")
    (qn,) = _rowwise(_rms, [qc], [P["mla_q_norm_w"]], [MLA_Q_RANK], name="q_norm")
    (kvn,) = _rowwise(_rms, [kvc], [P["mla_kv_norm_w"]], [MLA_KV_RANK], name="kv_norm")
    q = _mm(qn, W["mla_w_q_b"], name="mm_q")
    kv = _mm(kvn, W["mla_w_kv_b"], name="mm_kv")
    (qr,) = _rowwise(_rope_fwd_fn, [(q, 512, 2), cos512, sin512], [], [512], name="rope_q")
    (krt,) = _rowwise(lambda u, c, s: _spread4(_rope_fwd_fn(u, c, s)), [kr, cos128, sin128], [], [LANE], name="rope_k")
    att = _att_fwd(q, qr, kv, krt)
    (ymla,) = _rowwise(_rms, [att], [P["mla_out_norm_w"]], [1024], name="out_norm")
    ycat = jnp.concatenate([yssd, ymla], axis=1)
    mix = _mm(ycat, W["w_out"], name="mm_out")
    f_h1 = lambda xv, mv, g, b: _ln(ALPHA * xv + mv, g, b)
    (h1,) = _rowwise(f_h1, [x, mix], [P["ln_mix_g"], P["ln_mix_b"]], [1024], name="ln_mix")
    fb = D_FF // N_DEV
    hg = _mm(h1, W["w_ffn_gate"], b_blk="n", o_blk="n", name="mm_gate")
    hu = _mm(h1, W["w_ffn_up"], b_blk="n", o_blk="n", name="mm_up")
    pg = _mm(h1, W["w_ple_gate"], name="mm_ple_gate")
    pp = _mm(p, W["w_ple_proj"], name="mm_ple")
    hg2, hu2 = hg.reshape(N_DEV * s_dim, fb), hu.reshape(N_DEV * s_dim, fb)
    (act,) = _rowwise(lambda g, u: _silu(g) * u, [hg2, hu2], [], [fb], name="swiglu", tr=512)
    act3 = act.reshape(N_DEV, s_dim, fb)
    ffn = _mm(act3, W["w_ffn_down"], a_blk="k", b_blk="k", name="mm_down")

    f_h2 = lambda hv, fv, pg, ppv, g, b: _ln(ALPHA * hv + fv + _sigmoid(pg) * ppv, g, b)

    def final_fn(hv, fv, pg, ppv, tv, g, b):
        h2, pull = jax.vjp(f_h2, hv, fv, pg, ppv, g, b)
        diff = h2 - tv
        loss = 0.5 * jnp.sum(jnp.mean(diff * diff, axis=-1, keepdims=True), axis=0, keepdims=True)
        d_h, d_f, d_pg, d_pp, d_g, d_b = pull(diff * (1.0 / D_MODEL))
        return d_h, d_f, d_pg, d_pp, d_g, d_b, jnp.broadcast_to(loss, (1, LANE))

    dh1_a, dffn, dpg, dpp, g_ffn_g, g_ffn_b, loss = _rowwise(
        final_fn, [h1, ffn, pg, pp, tgt], [P["ln_ffn_g"], P["ln_ffn_b"]], [1024] * 4, [1024, 1024, LANE], name="final")

    G = {}
    dact = _mm(dffn, W["w_ffn_down"], tb=True, b_blk="n", o_blk="n", name="mm_down_dx")
    G["w_ffn_down"] = _mm(act3, dffn, ta=True, a_blk="m", o_blk="m", out_dtype=GRAD_DT, name="mm_down_dw")

    def swiglu_bwd(g, u, d):
        sg = _sigmoid(g)
        return d * u * (sg * (1.0 + g * (1.0 - sg))), d * (g * sg)

    dg, du = _rowwise(swiglu_bwd, [hg2, hu2, dact.reshape(N_DEV * s_dim, fb)], [], [fb, fb], name="swiglu_bwd", tr=512)
    dg3, du3 = dg.reshape(N_DEV, s_dim, fb), du.reshape(N_DEV, s_dim, fb)
    dh1 = _mm(dg3, W["w_ffn_gate"], tb=True, a_blk="k", b_blk="k", add=dh1_a, name="mm_gate_dx")
    dh1 = _mm(du3, W["w_ffn_up"], tb=True, a_blk="k", b_blk="k", add=dh1, name="mm_up_dx")
    dh1 = _mm(dpg, W["w_ple_gate"], tb=True, add=dh1, name="mm_ple_gate_dx")
    G["w_ffn_gate"] = _mm(h1, dg3, ta=True, b_blk="n", o_blk="n", out_dtype=GRAD_DT, name="mm_gate_dw")
    G["w_ffn_up"] = _mm(h1, du3, ta=True, b_blk="n", o_blk="n", out_dtype=GRAD_DT, name="mm_up_dw")
    G["w_ple_gate"] = _mm(h1, dpg, ta=True, out_dtype=GRAD_DT, name="mm_ple_gate_dw")
    G["w_ple_proj"] = _mm(p, dpp, ta=True, out_dtype=GRAD_DT, name="mm_ple_dw")
    dx_a, dmix, g_mix_g, g_mix_b = _rowwise(
        lambda xv, mv, dv, g, b: _vjp_rows(f_h1)(xv, mv, g, b, dv), [x, mix, dh1], [P["ln_mix_g"], P["ln_mix_b"]],
        [1024, 1024], [1024, 1024], name="ln_mix_bwd")
    dycat = _mm(dmix, W["w_out"], tb=True, name="mm_out_dx")
    G["w_out"] = _mm(ycat, dmix, ta=True, out_dtype=GRAD_DT, name="mm_out_dw")

    datt, g_out_norm = _rowwise(lambda a, dv, w: _vjp_rows(_rms)(a, w, dv), [att, (dycat, 1024, 1)],
                                [P["mla_out_norm_w"]], [1024], [1024], name="out_norm_bwd")
    dqn_nope, dqr, dkn, dv, dkrt = _att_bwd(q, qr, kv, krt, att, datt)
    dkv = jnp.concatenate([dkn, dv], axis=1)
    (dq_rope,) = _rowwise(_rope_bwd_fn, [dqr, cos512, sin512], [], [512], name="rope_q_bwd")

    def rope_k_bwd(d0, d1, d2, d3, c, s):
        d = _spread4(d0 + d1 + d2 + d3)
        lane = lax.broadcasted_iota(jnp.int32, d.shape, 1)
        return _rope_bwd_fn(jnp.where(lane < MLA_ROPE, d, 0.0), c, s)

    (dkr,) = _rowwise(rope_k_bwd, [dkrt[0], dkrt[1], dkrt[2], dkrt[3], cos128, sin128], [], [LANE], name="rope_k_bwd")
    dq = jnp.concatenate([dqn_nope, dq_rope], axis=1)
    dqn = _mm(dq, W["mla_w_q_b"], tb=True, name="mm_q_dx")
    G["mla_w_q_b"] = _mm(qn, dq, ta=True, out_dtype=GRAD_DT, name="mm_q_dw")
    dkvn = _mm(dkv, W["mla_w_kv_b"], tb=True, name="mm_kv_dx")
    G["mla_w_kv_b"] = _mm(kvn, dkv, ta=True, out_dtype=GRAD_DT, name="mm_kv_dw")
    dqc, g_q_norm = _rowwise(lambda a, dv, w: _vjp_rows(_rms)(a, w, dv), [qc, dqn], [P["mla_q_norm_w"]],
                             [MLA_Q_RANK], [MLA_Q_RANK], name="q_norm_bwd")
    dkvc, g_kv_norm = _rowwise(lambda a, dv, w: _vjp_rows(_rms)(a, w, dv), [kvc, dkvn], [P["mla_kv_norm_w"]],
                               [MLA_KV_RANK], [MLA_KV_RANK], name="kv_norm_bwd")

    dy, dz, g_ssd_norm = _rowwise(lambda yv, zv, dv, w: _vjp_rows(_gate_rms)(yv, zv, w, dv), [y, z, (dycat, 1024, 0)],
                                  [P["ssd_norm_w"]], [1024, 1024], [1024], name="ssd_gate_norm_bwd")
    dxbca, ddtr, g_dt_bias, g_alog, g_d = _ssd_bwd(xbca, dtr, bias_p, alog_p, d_x, states, dy)
    da, g_conv_w, g_conv_b = _conv_bwd_pre(xbc, P["ssd_conv_w"], P["ssd_conv_b"], dxbca)
    dxbc = _conv_bwd_in(da, P["ssd_conv_w"])

    dproj = jnp.concatenate([dz, dxbc, ddtr, dqc, dkvc, dkr], axis=1)
    grad_x = _mm(dproj, W["w_in"], tb=True, add=dx_a, name="mm_in_dx")
    G["w_in"] = _mm(x, dproj, ta=True, out_dtype=GRAD_DT, name="mm_in_dw")

    small = {
        "ssd_conv_b": g_conv_b, "ssd_dt_bias": g_dt_bias, "ssd_A_log": g_alog, "ssd_D": g_d, "ssd_norm_w": g_ssd_norm,
        "mla_q_norm_w": g_q_norm, "mla_kv_norm_w": g_kv_norm, "mla_out_norm_w": g_out_norm, "ln_mix_g": g_mix_g,
        "ln_mix_b": g_mix_b, "ln_ffn_g": g_ffn_g, "ln_ffn_b": g_ffn_b,
    }
    return grad_x, G, _pack_small(g_conv_w, [small[n] for n, _ in REPL_W], loss)


def kernel(x, p, positions, w_in, ssd_conv_w, ssd_conv_b, ssd_dt_bias, ssd_A_log, ssd_D, ssd_norm_w, mla_q_norm_w, mla_w_q_b, mla_kv_norm_w, mla_w_kv_b, mla_out_norm_w, w_out, ln_mix_g, ln_mix_b, w_ffn_gate, w_ffn_up, w_ffn_down, w_ple_gate, w_ple_proj, ln_ffn_g, ln_ffn_b, loss_target, m_w_in, m_ssd_conv_w, m_ssd_conv_b, m_ssd_dt_bias, m_ssd_A_log, m_ssd_D, m_ssd_norm_w, m_mla_q_norm_w, m_mla_w_q_b, m_mla_kv_norm_w, m_mla_w_kv_b, m_mla_out_norm_w, m_w_out, m_ln_mix_g, m_ln_mix_b, m_w_ffn_gate, m_w_ffn_up, m_w_ffn_down, m_w_ple_gate, m_w_ple_proj, m_ln_ffn_g, m_ln_ffn_b, v_w_in, v_ssd_conv_w, v_ssd_conv_b, v_ssd_dt_bias, v_ssd_A_log, v_ssd_D, v_ssd_norm_w, v_mla_q_norm_w, v_mla_w_q_b, v_mla_kv_norm_w, v_mla_w_kv_b, v_mla_out_norm_w, v_w_out, v_ln_mix_g, v_ln_mix_b, v_w_ffn_gate, v_w_ffn_up, v_w_ffn_down, v_w_ple_gate, v_w_ple_proj, v_ln_ffn_g, v_ln_ffn_b):
    args = dict(locals())
    core = lax.axis_index("c")
    me = 4 * lax.axis_index("x") + 2 * lax.axis_index("y") + core

    conv_sh = ssd_conv_w[0]
    conv_hi = conv_sh.astype(BF16)
    conv_lo = (conv_sh - conv_hi.astype(F32)).astype(BF16)
    shards = [args[n][0].astype(BF16) for n in BIG] + [jnp.concatenate([conv_hi, conv_lo], axis=0)]
    gathered = _gather_many(shards, "gather_weights")
    gw = dict(zip(BIG, gathered[:-1]))
    conv_g = gathered[-1].astype(F32)
    rows_full = lambda g: g.reshape(-1, g.shape[2])
    W = {
        "w_in": _win_pad(_cols_full(gw["w_in"])),
        "mla_w_q_b": _heads_split(_cols_full(gw["mla_w_q_b"]), MLA_NOPE, MLA_ROPE),
        "mla_w_kv_b": _heads_split(_cols_full(gw["mla_w_kv_b"]), MLA_NOPE, MLA_V),
        "w_out": rows_full(gw["w_out"]),
        "w_ple_gate": rows_full(gw["w_ple_gate"]),
        "w_ple_proj": _cols_full(gw["w_ple_proj"]),
        "w_ffn_gate": gw["w_ffn_gate"], "w_ffn_up": gw["w_ffn_up"], "w_ffn_down": gw["w_ffn_down"],
    }
    P = {n: args[n] for n, _ in REPL_W}
    P["ssd_conv_w"] = _cols_full(conv_g[:, :4] + conv_g[:, 4:])

    grad_x, G, packed = _local_step(x[0], p[0, 0], positions[0], loss_target[0], W, P)

    blocks = {
        "w_in": _cols_split(_win_unpad(G["w_in"])),
        "mla_w_q_b": _cols_split(_heads_merge(G["mla_w_q_b"], MLA_NOPE, MLA_ROPE)),
        "mla_w_kv_b": _cols_split(_heads_merge(G["mla_w_kv_b"], MLA_NOPE, MLA_V)),
        "w_out": G["w_out"].reshape(N_DEV, -1, D_MODEL),
        "w_ple_gate": G["w_ple_gate"].reshape(N_DEV, -1, D_MODEL),
        "w_ple_proj": _cols_split(G["w_ple_proj"]),
        "w_ffn_gate": G["w_ffn_gate"], "w_ffn_up": G["w_ffn_up"], "w_ffn_down": G["w_ffn_down"],
    }
    glist = [blocks[n] for n in BIG]
    from_sibling = _pair_exchange(glist, "exchange_pairs")
    core_arr = core.astype(jnp.int32).reshape(1)
    sums = [_pair_sum(g, r, core_arr, "pair_sum_" + n) for n, g, r in zip(BIG, glist, from_sibling)]
    recv = _chip_exchange(sums, "exchange_chips")
    big_out = {n: _adam(r, args[n][0], args["m_" + n][0], args["v_" + n][0], "adam_" + n) for n, r in zip(BIG, recv)}

    small_all = _all_gather(packed, "gather_small")
    conv_sum, loss_row, small_out = _adam_small(small_all, [(args[n], args["m_" + n], args["v_" + n]) for n, _ in REPL_W])
    conv_grad = lax.dynamic_slice_in_dim(conv_sum, me * 192, 192, axis=1)
    conv_out = _adam(conv_grad[None], conv_sh, m_ssd_conv_w[0], v_ssd_conv_w[0], "adam_conv")
    small_map = {n: small_out[i] for i, (n, _) in enumerate(REPL_W)}

    def outputs(idx):
        res = []
        for n in WEIGHT_ORDER:
            if n == "ssd_conv_w":
                res.append(conv_out[idx][None])
            elif n in big_out:
                res.append(big_out[n][idx][None])
            else:
                res.append(small_map[n][idx])
        return res

    return (loss_row[0, 0], grad_x[None], *outputs(0), *outputs(1), *outputs(2), *outputs(3))
```

```python
import functools
import math

import numpy as np
import jax
import jax.numpy as jnp
from jax import lax
from jax.experimental import pallas as pl
from jax.experimental.pallas import tpu as pltpu

F32 = jnp.float32
BF16 = jnp.bfloat16
HI = lax.Precision.HIGHEST

N_DEV = 8
D_MODEL = 1024
PLE_DIM = 256
SSD_HEADS = 16
SSD_HEAD_DIM = 64
SSD_INNER = 1024
SSD_STATE = 128
SSD_XBC = 1536
SSD_CHUNK = 128
MLA_HEADS = 16
MLA_Q_RANK = 384
MLA_KV_RANK = 256
MLA_NOPE = 64
MLA_ROPE = 32
MLA_V = 64
ROPE_BASE = 10000.0
D_FF = 2816
IN_WIDTH = 3248
IN_PAD = 3456
ALPHA = 2.0 ** 0.25
EPS = 1e-6
LN_EPS = 1e-5
ATT_SCALE = 1.0 / math.sqrt(MLA_NOPE + MLA_ROPE)
ADAM_LR, ADAM_B1, ADAM_B2, ADAM_EPS, ADAM_WD, ADAM_STEP = 0.001, 0.9, 0.999, 1e-08, 0.01, 10

LANE = 128
MM_TM, MM_TN, MM_TK = 1024, 512, 2048
ROW_TILE = 256
ATT_TQ = 256

GRAD_DT = BF16

BIG = ("w_in", "mla_w_q_b", "mla_w_kv_b", "w_out", "w_ffn_gate", "w_ffn_up", "w_ffn_down", "w_ple_gate", "w_ple_proj")
ROW_SHARDED = ("w_out", "w_ffn_down", "w_ple_gate")
OWNER_BLOCKED = ("w_ffn_gate", "w_ffn_up", "w_ffn_down")
WEIGHT_ORDER = ("w_in", "ssd_conv_w", "ssd_conv_b", "ssd_dt_bias", "ssd_A_log", "ssd_D", "ssd_norm_w", "mla_q_norm_w",
                "mla_w_q_b", "mla_kv_norm_w", "mla_w_kv_b", "mla_out_norm_w", "w_out", "ln_mix_g", "ln_mix_b",
                "w_ffn_gate", "w_ffn_up", "w_ffn_down", "w_ple_gate", "w_ple_proj", "ln_ffn_g", "ln_ffn_b")


def _tile(dim, cap):
    if dim <= cap:
        return dim
    t = (cap // LANE) * LANE
    while dim % t:
        t -= LANE
    return t


def _dot(a, b, dims=(((1,), (0,)), ((), ())), precision=None):
    return lax.dot_general(a, b, dims, preferred_element_type=F32, precision=precision)


_NT = (((1,), (1,)), ((), ()))
_TN = (((0,), (0,)), ((), ()))


def _mm(a, b, *, ta=False, tb=False, a_blk=None, b_blk=None, o_blk=None, add=None, out_dtype=F32, name):
    ka, ma = a.shape[-2:] if ta else a.shape[-2:][::-1]
    nb, kb = b.shape[-2:] if tb else b.shape[-2:][::-1]
    m_dim = N_DEV * ma if a_blk == "m" else ma
    k_dim = N_DEV * ka if a_blk == "k" else ka
    n_dim = N_DEV * nb if b_blk == "n" else nb
    assert k_dim == (N_DEV * kb if b_blk == "k" else kb)
    tm = ma if a_blk == "m" else (m_dim // N_DEV if o_blk == "m" else _tile(m_dim, MM_TM))
    tn = nb if b_blk == "n" else (n_dim // N_DEV if o_blk == "n" else _tile(n_dim, MM_TN))
    tk = ka if a_blk == "k" else (kb if b_blk == "k" else _tile(k_dim, MM_TK))
    nk = k_dim // tk
    dims = (((0 if ta else 1,), (1 if tb else 0,)), ((), ()))
    has_add = add is not None

    def spec(tile, idx, lead):
        if lead is None:
            return pl.BlockSpec(tile, idx)
        return pl.BlockSpec((None,) + tile, lambda i, j, k: (lead(i, j, k),) + idx(i, j, k))

    def a_idx(i, j, k):
        ii, kk = (0 if a_blk == "m" else i), (0 if a_blk == "k" else k)
        return (kk, ii) if ta else (ii, kk)

    def b_idx(i, j, k):
        jj, kk = (0 if b_blk == "n" else j), (0 if b_blk == "k" else k)
        return (jj, kk) if tb else (kk, jj)

    def o_idx(i, j, k):
        return (0 if o_blk == "m" else i, 0 if o_blk == "n" else j)

    pick = {"m": lambda i, j, k: i, "n": lambda i, j, k: j, "k": lambda i, j, k: k, None: None}
    a_spec = spec((tk, tm) if ta else (tm, tk), a_idx, pick[a_blk])
    b_spec = spec((tn, tk) if tb else (tk, tn), b_idx, pick[b_blk])
    o_spec = spec((tm, tn), o_idx, pick[o_blk])

    def body(*refs):
        if has_add:
            a_ref, b_ref, add_ref, o_ref = refs[:4]
        else:
            a_ref, b_ref, o_ref = refs[:3]
        part = _dot(a_ref[...].astype(BF16), b_ref[...].astype(BF16), dims)
        if nk == 1:
            o_ref[...] = ((part + add_ref[...]) if has_add else part).astype(o_ref.dtype)
            return
        acc = refs[-1]
        k = pl.program_id(2)

        @pl.when(k == 0)
        def _():
            acc[...] = (part + add_ref[...]) if has_add else part

        @pl.when(k > 0)
        def _():
            acc[...] += part

        @pl.when(k == nk - 1)
        def _():
            o_ref[...] = acc[...].astype(o_ref.dtype)

    if o_blk == "m":
        out_shape = (N_DEV, tm, n_dim)
    elif o_blk == "n":
        out_shape = (N_DEV, m_dim, tn)
    else:
        out_shape = (m_dim, n_dim)
    ins = [a, b] + ([add] if has_add else [])
    specs = [a_spec, b_spec] + ([pl.BlockSpec((tm, tn), lambda i, j, k: (i, j))] if has_add else [])
    return pl.pallas_call(
        body, name=name, grid=(m_dim // tm, n_dim // tn, nk), in_specs=specs, out_specs=o_spec,
        out_shape=jax.ShapeDtypeStruct(out_shape, out_dtype),
        scratch_shapes=[pltpu.VMEM((tm, tn), F32)] if nk > 1 else [],
        compiler_params=pltpu.CompilerParams(dimension_semantics=("parallel", "parallel", "arbitrary")),
    )(*ins)


def _rowwise(fn, rows, consts, out_widths, acc_widths=(), *, name, tr=ROW_TILE):
    row_arrays, row_specs = [], []
    first_arr = rows[0][0] if isinstance(rows[0], tuple) else rows[0]
    s_dim = first_arr.shape[-2]
    tr = min(tr, s_dim)
    for r in rows:
        arr, width, cb = r if isinstance(r, tuple) else (r, r.shape[-1], 0)
        row_arrays.append(arr)
        if arr.ndim == 3:
            row_specs.append(pl.BlockSpec((None, tr, width), functools.partial(lambda i, k: (k, i, 0), k=cb)))
        else:
            row_specs.append(pl.BlockSpec((tr, width), functools.partial(lambda i, cb: (i, cb), cb=cb)))
    const_specs = [pl.BlockSpec(c.shape, lambda i: (0, 0)) for c in consts]
    nr, nc, no, na = len(rows), len(consts), len(out_widths), len(acc_widths)

    def body(*refs):
        ins = [r[...] for r in refs[:nr + nc]]
        res = fn(*ins)
        if not isinstance(res, (tuple, list)):
            res = (res,)
        out_refs = refs[nr + nc:nr + nc + no]
        acc_refs = refs[nr + nc + no:]
        for o_ref, val in zip(out_refs, res[:no]):
            o_ref[...] = val.astype(o_ref.dtype)
        first = pl.program_id(0) == 0
        for a_ref, val in zip(acc_refs, res[no:]):
            @pl.when(first)
            def _(a_ref=a_ref, val=val):
                a_ref[...] = val

            @pl.when(jnp.logical_not(first))
            def _(a_ref=a_ref, val=val):
                a_ref[...] += val

    outs = [w if isinstance(w, tuple) else (w, F32) for w in out_widths]
    out_shape = [jax.ShapeDtypeStruct((s_dim, w), dt) for w, dt in outs]
    out_shape += [jax.ShapeDtypeStruct((1, w), F32) for w in acc_widths]
    out_specs = [pl.BlockSpec((tr, w), lambda i: (i, 0)) for w, _ in outs]
    out_specs += [pl.BlockSpec((1, w), lambda i: (0, 0)) for w in acc_widths]
    res = pl.pallas_call(
        body, name=name, grid=(s_dim // tr,), in_specs=row_specs + const_specs, out_specs=out_specs, out_shape=out_shape,
        compiler_params=pltpu.CompilerParams(dimension_semantics=("arbitrary",)),
    )(*row_arrays, *consts)
    return res


def _colsum(v):
    return jnp.sum(v, axis=0, keepdims=True)


def _rms(u, g):
    return u * lax.rsqrt(jnp.mean(u * u, axis=-1, keepdims=True) + EPS) * g


def _ln(u, g, b):
    mu = jnp.mean(u, axis=-1, keepdims=True)
    d = u - mu
    var = jnp.mean(d * d, axis=-1, keepdims=True)
    return d * lax.rsqrt(var + LN_EPS) * g + b


def _sigmoid(v):
    return 1.0 / (1.0 + jnp.exp(-v))


def _silu(v):
    return v * _sigmoid(v)


def _softplus(v):
    y = jnp.exp(-jnp.abs(v))
    w = 1.0 + y
    log1p = jnp.where(w == 1.0, y, jnp.log(w) * y / jnp.where(w == 1.0, 1.0, w - 1.0))
    return jnp.maximum(v, 0.0) + log1p


def _gate_rms(y, z, w):
    return _rms(y * _silu(z), w)


def _vjp_rows(f):
    def fn(*args):
        prim, ct = args[:-1], args[-1]
        _, pull = jax.vjp(f, *prim)
        return pull(ct)
    return fn


def _conv_pre(cur, prev, w, b, first):
    row = lax.broadcasted_iota(jnp.int32, cur.shape, 0)
    acc = cur * w[3:4, :] + b
    for j in (1, 2, 3):
        tail = jnp.where(first, 0.0, pltpu.roll(prev, j, 0))
        acc = acc + jnp.where(row >= j, pltpu.roll(cur, j, 0), tail) * w[3 - j:4 - j, :]
    return acc


def _conv_fwd(u, w, b, name="conv_fwd"):
    s_dim, width = u.shape
    tr = min(ROW_TILE, s_dim)

    def body(cur_ref, prev_ref, w_ref, b_ref, o_ref):
        pre = _conv_pre(cur_ref[...], prev_ref[...], w_ref, b_ref[...], pl.program_id(0) == 0)
        o_ref[...] = _silu(pre)

    return pl.pallas_call(
        body, name=name, grid=(s_dim // tr,),
        in_specs=[pl.BlockSpec((tr, width), lambda i: (i, 0)), pl.BlockSpec((tr, width), lambda i: (jnp.maximum(i - 1, 0), 0)),
                  pl.BlockSpec(w.shape, lambda i: (0, 0)), pl.BlockSpec(b.shape, lambda i: (0, 0))],
        out_specs=pl.BlockSpec((tr, width), lambda i: (i, 0)), out_shape=jax.ShapeDtypeStruct((s_dim, width), F32),
        compiler_params=pltpu.CompilerParams(dimension_semantics=("arbitrary",)),
    )(u, u, w, b)


def _conv_bwd_pre(u, w, b, dact, name="conv_bwd_pre"):
    s_dim, width = u.shape
    tr = min(ROW_TILE, s_dim)

    def body(cur_ref, prev_ref, w_ref, b_ref, d_ref, da_ref, dw_ref, db_ref):
        first = pl.program_id(0) == 0
        cur, prev = cur_ref[...], prev_ref[...]
        pre = _conv_pre(cur, prev, w_ref, b_ref[...], first)
        sg = _sigmoid(pre)
        da = d_ref[...] * (sg * (1.0 + pre * (1.0 - sg)))
        da_ref[...] = da
        row = lax.broadcasted_iota(jnp.int32, cur.shape, 0)

        @pl.when(first)
        def _():
            dw_ref[...] = jnp.zeros_like(dw_ref)
            db_ref[...] = jnp.zeros_like(db_ref)

        db_ref[...] += _colsum(da)
        dw_ref[3:4, :] += _colsum(da * cur)
        for j in (1, 2, 3):
            tail = jnp.where(first, 0.0, pltpu.roll(prev, j, 0))
            sh = jnp.where(row >= j, pltpu.roll(cur, j, 0), tail)
            dw_ref[3 - j:4 - j, :] += _colsum(da * sh)

    return pl.pallas_call(
        body, name=name, grid=(s_dim // tr,),
        in_specs=[pl.BlockSpec((tr, width), lambda i: (i, 0)), pl.BlockSpec((tr, width), lambda i: (jnp.maximum(i - 1, 0), 0)),
                  pl.BlockSpec(w.shape, lambda i: (0, 0)), pl.BlockSpec(b.shape, lambda i: (0, 0)),
                  pl.BlockSpec((tr, width), lambda i: (i, 0))],
        out_specs=[pl.BlockSpec((tr, width), lambda i: (i, 0)), pl.BlockSpec(w.shape, lambda i: (0, 0)),
                   pl.BlockSpec(b.shape, lambda i: (0, 0))],
        out_shape=[jax.ShapeDtypeStruct((s_dim, width), F32), jax.ShapeDtypeStruct(w.shape, F32),
                   jax.ShapeDtypeStruct(b.shape, F32)],
        compiler_params=pltpu.CompilerParams(dimension_semantics=("arbitrary",)),
    )(u, u, w, b, dact)


def _conv_bwd_in(da, w, name="conv_bwd_in"):
    s_dim, width = da.shape
    tr = min(ROW_TILE, s_dim)
    n = s_dim // tr

    def body(cur_ref, nxt_ref, w_ref, o_ref):
        last = pl.program_id(0) == n - 1
        cur, nxt = cur_ref[...], nxt_ref[...]
        row = lax.broadcasted_iota(jnp.int32, cur.shape, 0)
        acc = cur * w_ref[3:4, :]
        for j in (1, 2, 3):
            head = jnp.where(last, 0.0, pltpu.roll(nxt, tr - j, 0))
            acc = acc + jnp.where(row < tr - j, pltpu.roll(cur, tr - j, 0), head) * w_ref[3 - j:4 - j, :]
        o_ref[...] = acc.astype(o_ref.dtype)

    return pl.pallas_call(
        body, name=name, grid=(n,),
        in_specs=[pl.BlockSpec((tr, width), lambda i: (i, 0)), pl.BlockSpec((tr, width), lambda i: (jnp.minimum(i + 1, n - 1), 0)),
                  pl.BlockSpec(w.shape, lambda i: (0, 0))],
        out_specs=pl.BlockSpec((tr, width), lambda i: (i, 0)), out_shape=jax.ShapeDtypeStruct((s_dim, width), BF16),
        compiler_params=pltpu.CompilerParams(dimension_semantics=("arbitrary",)),
    )(da, da, w)


def _ssd_consts():
    L = SSD_CHUNK
    tri = np.tril(np.ones((L, L), np.float32))
    expand = np.zeros((LANE, SSD_INNER), np.float32)
    expand128 = np.zeros((LANE, SSD_HEADS * LANE), np.float32)
    for h in range(SSD_HEADS):
        expand[h, h * SSD_HEAD_DIM:(h + 1) * SSD_HEAD_DIM] = 1.0
        expand128[h, h * LANE:(h + 1) * LANE] = 1.0
    return jnp.asarray(tri), jnp.asarray(expand), jnp.asarray(expand128), jnp.asarray(expand.T.copy())


def _ssd_prep(dt_ref, bias_ref, alog_ref, tri_ref, exp_ref, exp128_ref, cs_s, cst_s, ex_s, csx_s):
    L = SSD_CHUNK
    dt = _softplus(dt_ref[...] + bias_ref[...])
    a = -jnp.exp(alog_ref[...])
    cs = _dot(tri_ref[...], dt * a, precision=HI)
    cs_s[...] = cs
    cst_s[...] = cs.T
    last = cs_s[L - 1:L, :]
    expand = exp_ref[...]
    ex_s[...] = _dot(jnp.exp(cs), expand, precision=HI)
    f_x = _dot(jnp.exp(last - cs), expand, precision=HI)
    dt_x = _dot(dt, expand, precision=HI)
    csx_s[...] = _dot(cs, exp128_ref[...], precision=HI)
    t_x = ex_s[L - 1:L, :]
    return dt, a, dt_x, f_x, t_x


def _decay_matrix(csx_s, cst_s, h, tril):
    seg = csx_s[:, h * LANE:(h + 1) * LANE] - cst_s[h:h + 1, :]
    return jnp.exp(jnp.where(tril, seg, -jnp.inf))


def _ssd_fwd(xbca, dtr, bias, alog, d_x, name="ssd_fwd"):
    s_dim = xbca.shape[0]
    L = SSD_CHUNK
    nc = s_dim // L
    tri, expand, expand128, _ = _ssd_consts()

    def body(xs_ref, b_ref, c_ref, dt_ref, bias_ref, alog_ref, dx_ref, tri_ref, exp_ref, exp128_ref,
             y_ref, st_ref, st_s, cs_s, cst_s, ex_s, csx_s):
        @pl.when(pl.program_id(0) == 0)
        def _():
            st_s[...] = jnp.zeros_like(st_s)

        dt, a, dt_x, f_x, t_x = _ssd_prep(dt_ref, bias_ref, alog_ref, tri_ref, exp_ref, exp128_ref, cs_s, cst_s, ex_s, csx_s)
        st_ref[0] = st_s[...]
        row = lax.broadcasted_iota(jnp.int32, (L, L), 0)
        col = lax.broadcasted_iota(jnp.int32, (L, L), 1)
        tril = row >= col
        low = col < SSD_HEAD_DIM
        for g in range(2):
            bg = b_ref[:, g * LANE:(g + 1) * LANE]
            cg = c_ref[:, g * LANE:(g + 1) * LANE].astype(BF16)
            gmat = _dot(cg, bg.astype(BF16), _NT)
            bgt = bg.T.astype(BF16)
            for jj in range(4):
                j = 4 * g + jj
                sl = slice(j * LANE, (j + 1) * LANE)
                xp = xs_ref[:, sl]
                x_dt = xp * dt_x[:, sl]
                xb = x_dt.astype(BF16)
                yd = []
                for e in range(2):
                    lm = _decay_matrix(csx_s, cst_s, 2 * j + e, tril)
                    yd.append(_dot((gmat * lm).astype(BF16), xb))
                stp = st_s[j]
                z = _dot(cg, stp.astype(BF16))
                y_ref[:, sl] = jnp.where(low, yd[0], yd[1]) + ex_s[:, sl] * z + dx_ref[:, sl] * xp
                xf = (x_dt * f_x[:, sl]).astype(BF16)
                st_s[j] = t_x[:, sl] * stp + _dot(bgt, xf)

    const = lambda shape: pl.BlockSpec(shape, lambda c: tuple(0 for _ in shape))
    return pl.pallas_call(
        body, name=name, grid=(nc,),
        in_specs=[pl.BlockSpec((L, 1024), lambda c: (c, 0)), pl.BlockSpec((L, 256), lambda c: (c, 4)),
                  pl.BlockSpec((L, 256), lambda c: (c, 5)), pl.BlockSpec((L, LANE), lambda c: (c, 0)),
                  const((1, LANE)), const((1, LANE)), const((1, 1024)), const((L, L)), const((LANE, 1024)),
                  const((LANE, 2048))],
        out_specs=[pl.BlockSpec((L, 1024), lambda c: (c, 0)), pl.BlockSpec((1, 8, LANE, LANE), lambda c: (c, 0, 0, 0))],
        out_shape=[jax.ShapeDtypeStruct((s_dim, 1024), F32), jax.ShapeDtypeStruct((nc, 8, LANE, LANE), F32)],
        scratch_shapes=[pltpu.VMEM((8, LANE, LANE), F32), pltpu.VMEM((L, LANE), F32), pltpu.VMEM((LANE, L), F32),
                        pltpu.VMEM((L, 1024), F32), pltpu.VMEM((L, 2048), F32)],
        compiler_params=pltpu.CompilerParams(dimension_semantics=("arbitrary",)),
    )(xbca, xbca, xbca, dtr, bias, alog, d_x, tri, expand, expand128)


def _ssd_bwd(xbca, dtr, bias, alog, d_x, states, dy, name="ssd_bwd"):
    s_dim = xbca.shape[0]
    L = SSD_CHUNK
    nc = s_dim // L
    tri, expand, expand128, expand_t = _ssd_consts()

    def body(xs_ref, b_ref, c_ref, dt_ref, bias_ref, alog_ref, dx_ref, tri_ref, exp_ref, exp128_ref, expt_ref,
             st_ref, dy_ref, dxbc_ref, ddt_ref, dbias_ref, dalog_ref, dd_ref,
             dst_s, cs_s, cst_s, ex_s, csx_s, dcsx_s, ddtx_s, dcol_s, drow_s, dlast_s, dd_s):
        @pl.when(pl.program_id(0) == 0)
        def _():
            dst_s[...] = jnp.zeros_like(dst_s)
            dbias_ref[...] = jnp.zeros_like(dbias_ref)
            dalog_ref[...] = jnp.zeros_like(dalog_ref)
            dd_s[...] = jnp.zeros_like(dd_s)

        dt, a, dt_x, f_x, t_x = _ssd_prep(dt_ref, bias_ref, alog_ref, tri_ref, exp_ref, exp128_ref, cs_s, cst_s, ex_s, csx_s)
        row = lax.broadcasted_iota(jnp.int32, (L, L), 0)
        col = lax.broadcasted_iota(jnp.int32, (L, L), 1)
        tril = row >= col
        low = col < SSD_HEAD_DIM
        dcol_s[...] = jnp.zeros_like(dcol_s)
        drow_s[...] = jnp.zeros_like(drow_s)
        for g in range(2):
            bg = b_ref[:, g * LANE:(g + 1) * LANE]
            cg = c_ref[:, g * LANE:(g + 1) * LANE]
            bgb, cgb = bg.astype(BF16), cg.astype(BF16)
            gmat = _dot(cgb, bgb, _NT)
            d_g = jnp.zeros((L, L), F32)
            d_b = jnp.zeros((L, LANE), F32)
            d_c = jnp.zeros((L, LANE), F32)
            for jj in range(4):
                j = 4 * g + jj
                sl = slice(j * LANE, (j + 1) * LANE)
                xp = xs_ref[:, sl]
                dtp = dt_x[:, sl]
                x_dt = xp * dtp
                xb = x_dt.astype(BF16)
                dyp = dy_ref[:, sl]
                dd_s[:, sl] += _colsum(dyp * xp)
                d_xdt = jnp.zeros((L, LANE), F32)
                for e in range(2):
                    h = 2 * j + e
                    lm = _decay_matrix(csx_s, cst_s, h, tril)
                    m = gmat * lm
                    dye = jnp.where(low if e == 0 else jnp.logical_not(low), dyp, 0.0).astype(BF16)
                    d_m = jnp.where(tril, _dot(dye, xb, _NT), 0.0)
                    d_xdt = d_xdt + _dot(m.astype(BF16), dye, _TN)
                    d_g = d_g + d_m * lm
                    w = d_m * m
                    dcol_s[...] += jnp.where(col == h, jnp.sum(w, axis=1, keepdims=True), 0.0)
                    drow_s[...] += jnp.where(row == h, jnp.sum(w, axis=0, keepdims=True), 0.0)
                stp = st_ref[0, j]
                stb = stp.astype(BF16)
                dstn = dst_s[j]
                dstb = dstn.astype(BF16)
                e_p = ex_s[:, sl]
                f_p = f_x[:, sl]
                t_p = t_x[:, sl]
                z = _dot(cgb, stb)
                d_z = (e_p * dyp).astype(BF16)
                d_c = d_c + _dot(d_z, stb, _NT)
                d_xf = _dot(bgb, dstb)
                d_b = d_b + _dot((x_dt * f_p).astype(BF16), dstb, _NT)
                d_xdt = d_xdt + f_p * d_xf
                d_f = x_dt * d_xf * f_p
                dcsx_s[:, sl] = dyp * e_p * z - d_f
                dlast_s[:, sl] = _colsum(d_f) + _colsum(dstn * stp) * t_p
                dst_s[j] = _dot(cgb, d_z, _TN) + t_p * dstn
                dxbc_ref[:, sl] = dx_ref[:, sl] * dyp + d_xdt * dtp
                ddtx_s[:, sl] = d_xdt * xp
            d_gb = d_g.astype(BF16)
            dxbc_ref[:, 1024 + g * LANE:1024 + (g + 1) * LANE] = d_b + _dot(d_gb, cgb, _TN)
            dxbc_ref[:, 1280 + g * LANE:1280 + (g + 1) * LANE] = d_c + _dot(d_gb, bgb)

        expt = expt_ref[...]
        dlast = _dot(jnp.broadcast_to(dlast_s[...], (8, 1024)), expt, precision=HI)
        d_cs = dcol_s[...] - drow_s[...].T + _dot(dcsx_s[...], expt, precision=HI)
        rown = lax.broadcasted_iota(jnp.int32, (L, LANE), 0)
        d_cs = d_cs + jnp.where(rown == L - 1, jnp.sum(dlast, axis=0, keepdims=True) * 0.125, 0.0)
        d_da = _dot(tri_ref[...], d_cs, _TN, precision=HI)
        d_dt = d_da * a + _dot(ddtx_s[...], expt, precision=HI)
        dalog_ref[...] += _colsum(d_da * dt) * a
        d_raw = d_dt * _sigmoid(dt_ref[...] + bias_ref[...])
        ddt_ref[...] = d_raw.astype(ddt_ref.dtype)
        dbias_ref[...] += _colsum(d_raw)
        dd8 = _dot(jnp.broadcast_to(dd_s[...], (8, 1024)), expt, precision=HI)
        dd_ref[...] = jnp.sum(dd8, axis=0, keepdims=True) * 0.125

    const = lambda shape: pl.BlockSpec(shape, lambda c: tuple(0 for _ in shape))
    rev = lambda cb: (lambda c: (nc - 1 - c, cb))
    return pl.pallas_call(
        body, name=name, grid=(nc,),
        in_specs=[pl.BlockSpec((L, 1024), rev(0)), pl.BlockSpec((L, 256), rev(4)), pl.BlockSpec((L, 256), rev(5)),
                  pl.BlockSpec((L, LANE), rev(0)), const((1, LANE)), const((1, LANE)), const((1, 1024)), const((L, L)),
                  const((LANE, 1024)), const((LANE, 2048)), const((1024, LANE)),
                  pl.BlockSpec((1, 8, LANE, LANE), lambda c: (nc - 1 - c, 0, 0, 0)), pl.BlockSpec((L, 1024), rev(0))],
        out_specs=[pl.BlockSpec((L, SSD_XBC), rev(0)), pl.BlockSpec((L, LANE), rev(0)), const((1, LANE)), const((1, LANE)),
                   const((1, LANE))],
        out_shape=[jax.ShapeDtypeStruct((s_dim, SSD_XBC), F32), jax.ShapeDtypeStruct((s_dim, LANE), BF16),
                   jax.ShapeDtypeStruct((1, LANE), F32), jax.ShapeDtypeStruct((1, LANE), F32),
                   jax.ShapeDtypeStruct((1, LANE), F32)],
        scratch_shapes=[pltpu.VMEM((8, LANE, LANE), F32), pltpu.VMEM((L, LANE), F32), pltpu.VMEM((LANE, L), F32),
                        pltpu.VMEM((L, 1024), F32), pltpu.VMEM((L, 2048), F32), pltpu.VMEM((L, 1024), F32),
                        pltpu.VMEM((L, 1024), F32), pltpu.VMEM((L, LANE), F32), pltpu.VMEM((LANE, L), F32),
                        pltpu.VMEM((1, 1024), F32), pltpu.VMEM((1, 1024), F32)],
        compiler_params=pltpu.CompilerParams(dimension_semantics=("arbitrary",)),
    )(xbca, xbca, xbca, dtr, bias, alog, d_x, tri, expand, expand128, expand_t, states, dy)


def _swap_halves(u):
    width = u.shape[1]
    lane = lax.broadcasted_iota(jnp.int32, u.shape, 1)
    return jnp.where(lane % MLA_ROPE < MLA_ROPE // 2, pltpu.roll(u, width - MLA_ROPE // 2, 1), pltpu.roll(u, MLA_ROPE // 2, 1))


def _rope_fwd_fn(u, cos, sin):
    return u * cos + _swap_halves(u) * sin


def _rope_bwd_fn(d, cos, sin):
    return d * cos + _swap_halves(d * sin)


def _spread4(v):
    return v + pltpu.roll(v, 32, 1) + pltpu.roll(v, 64, 1) + pltpu.roll(v, 96, 1)


def _att_masks(tq):
    lane = lax.broadcasted_iota(jnp.int32, (tq, LANE), 1)
    return lane // MLA_NOPE, lane // MLA_ROPE


def _att_tile(i, tq):
    klen = (i + 1) * tq
    qpos = i * tq + lax.broadcasted_iota(jnp.int32, (tq, klen), 0)
    kpos = lax.broadcasted_iota(jnp.int32, (tq, klen), 1)
    return slice(i * tq, (i + 1) * tq), klen, qpos >= kpos


def _att_qcat(qn_t, qr_t, par, e, half_id, grp_id):
    return jnp.concatenate([jnp.where(half_id == par, qn_t, 0.0), jnp.where(grp_id == e, qr_t, 0.0)], axis=1).astype(BF16)


def _att_probs(qcat, kcat, causal):
    s = jnp.where(causal, _dot(qcat, kcat, _NT) * ATT_SCALE, -jnp.inf)
    p = jnp.exp(s - jnp.max(s, axis=1, keepdims=True))
    return p * (1.0 / jnp.sum(p, axis=1, keepdims=True))


def _att_specs(s_dim):
    col = lambda f: pl.BlockSpec((s_dim, LANE), lambda j: (0, f(j)))
    return [col(lambda j: j), col(lambda j: j // 2), col(lambda j: j), col(lambda j: 0), col(lambda j: 8 + j)]


def _att_fwd(q, qr, kv, krt, name="att_fwd"):
    s_dim = q.shape[0]
    tq = min(ATT_TQ, s_dim)

    def body(qn_ref, qr_ref, kn_ref, krt_ref, v_ref, o_ref, kcat_s, vb_s):
        e0 = 2 * (pl.program_id(0) % 2)
        half_id, grp_id = _att_masks(tq)
        kcat_s[...] = jnp.concatenate([kn_ref[...], krt_ref[...]], axis=1).astype(BF16)
        vb_s[...] = v_ref[...].astype(BF16)
        for i in range(s_dim // tq):
            rows, klen, causal = _att_tile(i, tq)
            qn_t, qr_t = qn_ref[rows, :], qr_ref[rows, :]
            outs = []
            for par in range(2):
                qcat = _att_qcat(qn_t, qr_t, par, e0 + par, half_id, grp_id)
                p = _att_probs(qcat, kcat_s[0:klen, :], causal)
                outs.append(_dot(p.astype(BF16), vb_s[0:klen, :]))
            o_ref[rows, :] = jnp.where(half_id == 0, outs[0], outs[1])

    return pl.pallas_call(
        body, name=name, grid=(MLA_HEADS // 2,), in_specs=_att_specs(s_dim),
        out_specs=pl.BlockSpec((s_dim, LANE), lambda j: (0, j)), out_shape=jax.ShapeDtypeStruct((s_dim, 1024), F32),
        scratch_shapes=[pltpu.VMEM((s_dim, 2 * LANE), BF16), pltpu.VMEM((s_dim, LANE), BF16)],
        compiler_params=pltpu.CompilerParams(dimension_semantics=("parallel",)),
    )(q, qr, kv, krt, kv)


def _att_bwd(q, qr, kv, krt, o, do, name="att_bwd"):
    s_dim = q.shape[0]
    tq = min(ATT_TQ, s_dim)

    def body(qn_ref, qr_ref, kn_ref, krt_ref, v_ref, o_ref, do_ref, dqn_ref, dqr_ref, dkn_ref, dv_ref, dkrt_ref,
             kcat_s, vb_s):
        e0 = 2 * (pl.program_id(0) % 2)
        half_id, grp_id = _att_masks(tq)
        kcat_s[...] = jnp.concatenate([kn_ref[...], krt_ref[...]], axis=1).astype(BF16)
        vb_s[...] = v_ref[...].astype(BF16)
        dkn_ref[...] = jnp.zeros_like(dkn_ref)
        dv_ref[...] = jnp.zeros_like(dv_ref)
        dkrt_ref[...] = jnp.zeros_like(dkrt_ref)
        for i in range(s_dim // tq):
            rows, klen, causal = _att_tile(i, tq)
            qn_t, qr_t, o_t, do_t = qn_ref[rows, :], qr_ref[rows, :], o_ref[rows, :], do_ref[rows, :]
            dqn = jnp.zeros((tq, LANE), F32)
            dqr = jnp.zeros((tq, LANE), F32)
            for par in range(2):
                qcat = _att_qcat(qn_t, qr_t, par, e0 + par, half_id, grp_id)
                p = _att_probs(qcat, kcat_s[0:klen, :], causal)
                dom = jnp.where(half_id == par, do_t, 0.0)
                domb = dom.astype(BF16)
                d_p = _dot(domb, vb_s[0:klen, :], _NT)
                d_row = jnp.sum(dom * o_t, axis=1, keepdims=True)
                d_s = (p * (d_p - d_row) * ATT_SCALE).astype(BF16)
                dqcat = _dot(d_s, kcat_s[0:klen, :])
                dqn = dqn + jnp.where(half_id == par, dqcat[:, :LANE], 0.0)
                dqr = dqr + jnp.where(grp_id == e0 + par, dqcat[:, LANE:], 0.0)
                dkcat = _dot(d_s, qcat, _TN)
                dkn_ref[0:klen, :] += dkcat[:, :LANE]
                dkrt_ref[0:klen, :] += dkcat[:, LANE:]
                dv_ref[0:klen, :] += _dot(p.astype(BF16), domb, _TN)
            dqn_ref[rows, :] = dqn.astype(dqn_ref.dtype)
            dqr_ref[rows, :] = dqr

    col = lambda f: pl.BlockSpec((s_dim, LANE), lambda j: (0, f(j)))
    return pl.pallas_call(
        body, name=name, grid=(MLA_HEADS // 2,), in_specs=_att_specs(s_dim) + [col(lambda j: j), col(lambda j: j)],
        out_specs=[col(lambda j: j), pl.BlockSpec((None, s_dim, LANE), lambda j: (j % 2, 0, j // 2)), col(lambda j: j),
                   col(lambda j: j), pl.BlockSpec((None, s_dim, LANE), lambda j: (j, 0, 0))],
        out_shape=[jax.ShapeDtypeStruct((s_dim, 1024), BF16), jax.ShapeDtypeStruct((2, s_dim, 512), F32),
                   jax.ShapeDtypeStruct((s_dim, 1024), F32), jax.ShapeDtypeStruct((s_dim, 1024), F32),
                   jax.ShapeDtypeStruct((MLA_HEADS // 2, s_dim, LANE), F32)],
        scratch_shapes=[pltpu.VMEM((s_dim, 2 * LANE), BF16), pltpu.VMEM((s_dim, LANE), BF16)],
        compiler_params=pltpu.CompilerParams(dimension_semantics=("parallel",)),
    )(q, qr, kv, krt, kv, o, do)


def _all_gather(x, name):
    rows, width = x.shape

    def body(x_ref, out_ref, send_sems, recv_sems, local_sem):
        x_i, y_i, c_i = lax.axis_index("x"), lax.axis_index("y"), lax.axis_index("c")
        me, sibling = (x_i, y_i, c_i), (x_i, y_i, 1 - c_i)
        chips = [(1 - x_i, y_i), (x_i, 1 - y_i), (1 - x_i, 1 - y_i)]

        def slot(px, py, pc):
            return out_ref.at[4 * px + 2 * py + pc]

        def copy(k, block, to, src=None):
            return pltpu.make_async_remote_copy(
                src_ref=slot(*block) if src is None else src, dst_ref=slot(*block), send_sem=send_sems.at[k],
                recv_sem=recv_sems.at[k], device_id=to, device_id_type=pl.DeviceIdType.MESH)

        mine = pltpu.make_async_copy(x_ref, slot(*me), local_sem)
        mine.start()
        first = [copy(0, me, sibling, src=x_ref)]
        first += [copy(1 + j, me, (*chip, c_i), src=x_ref) for j, chip in enumerate(chips)]
        for cp in first:
            cp.start()
        passed = [copy(4 + j, (*chip, c_i), sibling) for j, chip in enumerate(chips)]
        for j, chip in enumerate(chips):
            copy(1 + j, (*chip, c_i), me).wait_recv()
            passed[j].start()
        copy(0, sibling, me).wait_recv()
        for j, chip in enumerate(chips):
            copy(4 + j, (*chip, 1 - c_i), me).wait_recv()
        for cp in first + passed:
            cp.wait_send()
        mine.wait()

    return pl.pallas_call(
        body, name=name, out_shape=jax.ShapeDtypeStruct((N_DEV, rows, width), x.dtype),
        in_specs=[pl.BlockSpec(memory_space=pl.ANY)], out_specs=pl.BlockSpec(memory_space=pl.ANY),
        scratch_shapes=[pltpu.SemaphoreType.DMA((7,)), pltpu.SemaphoreType.DMA((7,)), pltpu.SemaphoreType.DMA],
    )(x)


def _gather_many(shards, name):
    n_arr = len(shards)

    def body(*refs):
        x_refs, out_refs = refs[:n_arr], refs[n_arr:2 * n_arr]
        send_sems, recv_sems, local_sems = refs[2 * n_arr:]
        x_i, y_i, c_i = lax.axis_index("x"), lax.axis_index("y"), lax.axis_index("c")
        me, sibling = (x_i, y_i, c_i), (x_i, y_i, 1 - c_i)
        chips = [(1 - x_i, y_i), (x_i, 1 - y_i), (1 - x_i, 1 - y_i)]

        def copy(a, k, block, to, src=None):
            slot = out_refs[a].at[4 * block[0] + 2 * block[1] + block[2]]
            return pltpu.make_async_remote_copy(
                src_ref=slot if src is None else src, dst_ref=slot, send_sem=send_sems.at[a, k],
                recv_sem=recv_sems.at[a, k], device_id=to, device_id_type=pl.DeviceIdType.MESH)

        mine, first, passed = [], [], []
        for a in range(n_arr):
            mine.append(pltpu.make_async_copy(x_refs[a], out_refs[a].at[4 * x_i + 2 * y_i + c_i], local_sems.at[a]))
            mine[a].start()
            first.append([copy(a, 0, me, sibling, src=x_refs[a])]
                         + [copy(a, 1 + j, me, (*chip, c_i), src=x_refs[a]) for j, chip in enumerate(chips)])
            for cp in first[a]:
                cp.start()
            passed.append([copy(a, 4 + j, (*chip, c_i), sibling) for j, chip in enumerate(chips)])
        for j, chip in enumerate(chips):
            for a in range(n_arr):
                copy(a, 1 + j, (*chip, c_i), me).wait_recv()
                passed[a][j].start()
        for a in range(n_arr):
            copy(a, 0, sibling, me).wait_recv()
            for j, chip in enumerate(chips):
                copy(a, 4 + j, (*chip, 1 - c_i), me).wait_recv()
        for a in range(n_arr):
            for cp in first[a] + passed[a]:
                cp.wait_send()
            mine[a].wait()

    any_spec = pl.BlockSpec(memory_space=pl.ANY)
    return pl.pallas_call(
        body, name=name, out_shape=[jax.ShapeDtypeStruct((N_DEV,) + x.shape, x.dtype) for x in shards],
        in_specs=[any_spec] * n_arr, out_specs=[any_spec] * n_arr,
        scratch_shapes=[pltpu.SemaphoreType.DMA((n_arr, 7)), pltpu.SemaphoreType.DMA((n_arr, 7)),
                        pltpu.SemaphoreType.DMA((n_arr,))],
    )(*shards)


def _pair_exchange(grads, name):
    n_arr = len(grads)

    def body(*refs):
        g_refs, out_refs = refs[:n_arr], refs[n_arr:2 * n_arr]
        send_sems, recv_sems = refs[2 * n_arr:]
        x_i, y_i, c_i = lax.axis_index("x"), lax.axis_index("y"), lax.axis_index("c")
        copies = []
        for a in range(n_arr):
            for chip in range(4):
                copies.append(pltpu.make_async_remote_copy(
                    src_ref=g_refs[a].at[2 * chip + (1 - c_i)], dst_ref=out_refs[a].at[chip], send_sem=send_sems.at[a, chip],
                    recv_sem=recv_sems.at[a, chip], device_id=(x_i, y_i, 1 - c_i), device_id_type=pl.DeviceIdType.MESH))
        for cp in copies:
            cp.start()
        for cp in copies:
            cp.wait_recv()
        for cp in copies:
            cp.wait_send()

    any_spec = pl.BlockSpec(memory_space=pl.ANY)
    return pl.pallas_call(
        body, name=name, out_shape=[jax.ShapeDtypeStruct((4,) + g.shape[1:], g.dtype) for g in grads],
        in_specs=[any_spec] * n_arr, out_specs=[any_spec] * n_arr,
        scratch_shapes=[pltpu.SemaphoreType.DMA((n_arr, 4)), pltpu.SemaphoreType.DMA((n_arr, 4))],
    )(*grads)


def _chip_exchange(sums, name):
    n_arr = len(sums)

    def body(*refs):
        s_refs, out_refs = refs[:n_arr], refs[n_arr:2 * n_arr]
        send_sems, recv_sems, local_sems = refs[2 * n_arr:]
        x_i, y_i, c_i = lax.axis_index("x"), lax.axis_index("y"), lax.axis_index("c")
        my_chip = 2 * x_i + y_i
        copies, local = [], []
        for a in range(n_arr):
            local.append(pltpu.make_async_copy(s_refs[a].at[my_chip], out_refs[a].at[my_chip], local_sems.at[a]))
            local[a].start()
            for k in range(1, 4):
                px, py = x_i ^ (k >> 1), y_i ^ (k & 1)
                copies.append(pltpu.make_async_remote_copy(
                    src_ref=s_refs[a].at[2 * px + py], dst_ref=out_refs[a].at[my_chip], send_sem=send_sems.at[a, k - 1],
                    recv_sem=recv_sems.at[a, k - 1], device_id=(px, py, c_i), device_id_type=pl.DeviceIdType.MESH))
        for cp in copies:
            cp.start()
        for cp in copies:
            cp.wait_recv()
        for cp in copies:
            cp.wait_send()
        for cp in local:
            cp.wait()

    any_spec = pl.BlockSpec(memory_space=pl.ANY)
    return pl.pallas_call(
        body, name=name, out_shape=[jax.ShapeDtypeStruct(s.shape, s.dtype) for s in sums],
        in_specs=[any_spec] * n_arr, out_specs=[any_spec] * n_arr,
        scratch_shapes=[pltpu.SemaphoreType.DMA((n_arr, 3)), pltpu.SemaphoreType.DMA((n_arr, 3)),
                        pltpu.SemaphoreType.DMA((n_arr,))],
    )(*sums)


def _pair_sum(g, recv, core, name):
    _, rows, cols = g.shape
    tr = ROW_TILE if rows % ROW_TILE == 0 else rows

    def body(core_ref, g_ref, r_ref, o_ref):
        o_ref[...] = (g_ref[...].astype(F32) + r_ref[...].astype(F32)).astype(o_ref.dtype)

    grid_spec = pltpu.PrefetchScalarGridSpec(
        num_scalar_prefetch=1, grid=(4, rows // tr),
        in_specs=[pl.BlockSpec((None, tr, cols), lambda k, i, core_ref: (2 * k + core_ref[0], i, 0)),
                  pl.BlockSpec((None, tr, cols), lambda k, i, core_ref: (k, i, 0))],
        out_specs=pl.BlockSpec((None, tr, cols), lambda k, i, core_ref: (k, i, 0)))
    return pl.pallas_call(body, name=name, grid_spec=grid_spec, out_shape=jax.ShapeDtypeStruct((4, rows, cols), g.dtype))(
        core, g, recv)


def _adam_math(g, w, m, v):
    m_new = ADAM_B1 * m + (1.0 - ADAM_B1) * g
    v_new = ADAM_B2 * v + (1.0 - ADAM_B2) * (g * g)
    m_hat = m_new / (1.0 - ADAM_B1 ** ADAM_STEP)
    v_hat = v_new / (1.0 - ADAM_B2 ** ADAM_STEP)
    return -ADAM_LR * (m_hat / (jnp.sqrt(v_hat) + ADAM_EPS) + ADAM_WD * w), m_new, v_new


def _adam(slots, w, m, v, name):
    n_slot, rows, cols = slots.shape
    tr = ROW_TILE if rows % ROW_TILE == 0 else rows

    def body(s_ref, w_ref, m_ref, v_ref, g_ref, d_ref, mo_ref, vo_ref):
        g = s_ref[0].astype(F32)
        for k in range(1, n_slot):
            g = g + s_ref[k].astype(F32)
        g_ref[...] = g
        d_ref[...], mo_ref[...], vo_ref[...] = _adam_math(g, w_ref[...], m_ref[...], v_ref[...])

    spec = pl.BlockSpec((tr, cols), lambda i: (i, 0))
    return pl.pallas_call(
        body, name=name, grid=(rows // tr,), in_specs=[pl.BlockSpec((n_slot, tr, cols), lambda i: (0, i, 0)), spec, spec, spec],
        out_specs=[spec] * 4, out_shape=[jax.ShapeDtypeStruct((rows, cols), F32)] * 4,
        compiler_params=pltpu.CompilerParams(dimension_semantics=("parallel",)),
    )(slots, w, m, v)


PACK_ROWS, PACK_W = 24, 1536
REPL_W = (("ssd_conv_b", 1536), ("ssd_dt_bias", 16), ("ssd_A_log", 16), ("ssd_D", 16), ("ssd_norm_w", 1024),
          ("mla_q_norm_w", 384), ("mla_kv_norm_w", 256), ("mla_out_norm_w", 1024), ("ln_mix_g", 1024),
          ("ln_mix_b", 1024), ("ln_ffn_g", 1024), ("ln_ffn_b", 1024))
LOSS_ROW = 4 + len(REPL_W)


def _pack_small(conv_w_grad, grads, loss, name="pack_small"):
    def body(*refs):
        cw_ref, g_refs, loss_ref, o_ref = refs[0], refs[1:1 + len(REPL_W)], refs[1 + len(REPL_W)], refs[-1]
        o_ref[...] = jnp.zeros_like(o_ref)
        o_ref[0:4, :] = cw_ref[...]
        for i, g_ref in enumerate(g_refs):
            o_ref[4 + i:5 + i, 0:g_ref.shape[1]] = g_ref[...]
        o_ref[LOSS_ROW:LOSS_ROW + 1, 0:LANE] = loss_ref[...]

    return pl.pallas_call(body, name=name, out_shape=jax.ShapeDtypeStruct((PACK_ROWS, PACK_W), F32))(conv_w_grad, *grads, loss)


def _adam_small(gathered, wmv, name="adam_small"):
    def body(*refs):
        s_ref = refs[0]
        in_refs = refs[1:1 + 3 * len(REPL_W)]
        cw_ref, loss_ref = refs[1 + 3 * len(REPL_W)], refs[2 + 3 * len(REPL_W)]
        out_refs = refs[3 + 3 * len(REPL_W):-1]
        tot = refs[-1]
        acc = s_ref[0]
        for k in range(1, N_DEV):
            acc = acc + s_ref[k]
        tot[...] = acc
        cw_ref[...] = tot[0:4, :]
        loss_ref[...] = tot[LOSS_ROW:LOSS_ROW + 1, 0:LANE]
        for i, (_, width) in enumerate(REPL_W):
            g = tot[4 + i:5 + i, 0:width]
            w_ref, m_ref, v_ref = in_refs[3 * i:3 * i + 3]
            g_ref, d_ref, mo_ref, vo_ref = out_refs[4 * i:4 * i + 4]
            g_ref[...] = g
            d_ref[...], mo_ref[...], vo_ref[...] = _adam_math(g, w_ref[...], m_ref[...], v_ref[...])

    flat_in = [a for triple in wmv for a in triple]
    out_shape = [jax.ShapeDtypeStruct((4, PACK_W), F32), jax.ShapeDtypeStruct((1, LANE), F32)]
    for _, width in REPL_W:
        out_shape += [jax.ShapeDtypeStruct((1, width), F32)] * 4
    res = pl.pallas_call(body, name=name, out_shape=out_shape, scratch_shapes=[pltpu.VMEM((PACK_ROWS, PACK_W), F32)])(
        gathered, *flat_in)
    return res[0], res[1], [res[2 + 4 * i:6 + 4 * i] for i in range(len(REPL_W))]


def _cols_full(g):
    return jnp.transpose(g, (1, 0, 2)).reshape(g.shape[1], -1)


def _cols_split(full):
    k_dim, n_dim = full.shape
    return jnp.transpose(full.reshape(k_dim, N_DEV, n_dim // N_DEV), (1, 0, 2))


def _win_pad(w):
    z = lambda n: jnp.zeros((w.shape[0], n), w.dtype)
    return jnp.concatenate([w[:, :2576], z(112), w[:, 2576:], z(96)], axis=1)


def _win_unpad(w):
    return jnp.concatenate([w[:, :2576], w[:, 2688:3360]], axis=1)


def _heads_split(w, a, b):
    k_dim = w.shape[0]
    w3 = w.reshape(k_dim, MLA_HEADS, a + b)
    return jnp.concatenate([w3[:, :, :a].reshape(k_dim, -1), w3[:, :, a:].reshape(k_dim, -1)], axis=1)


def _heads_merge(w, a, b):
    k_dim = w.shape[0]
    wa = w[:, :MLA_HEADS * a].reshape(k_dim, MLA_HEADS, a)
    wb = w[:, MLA_HEADS * a:].reshape(k_dim, MLA_HEADS, b)
    return jnp.concatenate([wa, wb], axis=2).reshape(k_dim, -1)


def _pad_lanes(v, width=LANE):
    return jnp.concatenate([v, jnp.zeros((v.shape[0], width - v.shape[1]), v.dtype)], axis=1)


def _local_step(x, p, positions, tgt, W, P):
    s_dim = x.shape[0]
    inv_freq = 1.0 / (ROPE_BASE ** (jnp.arange(0, MLA_ROPE, 2, dtype=F32) / MLA_ROPE))
    ang = positions.astype(F32)[:, None] * inv_freq
    cos, sin = jnp.cos(ang), jnp.sin(ang)
    cos32 = jnp.concatenate([cos, cos], axis=1)
    sin32 = jnp.concatenate([-sin, sin], axis=1)
    cos512, sin512 = jnp.tile(cos32, (1, 16)), jnp.tile(sin32, (1, 16))
    cos128, sin128 = jnp.tile(cos32, (1, 4)), jnp.tile(sin32, (1, 4))
    bias_p, alog_p = _pad_lanes(P["ssd_dt_bias"]), _pad_lanes(P["ssd_A_log"])
    d_x = jnp.repeat(P["ssd_D"], SSD_HEAD_DIM, axis=1)

    xb, pb = x.astype(BF16), p.astype(BF16)
    proj = _mm(xb, W["w_in"], name="mm_in")
    z, xbc, dtr = proj[:, :1024], proj[:, 1024:2560], proj[:, 2560:2688]
    qc, kvc, kr = proj[:, 2688:3072], proj[:, 3072:3328], proj[:, 3328:3456]
    xbca = _conv_fwd(xbc, P["ssd_conv_w"], P["ssd_conv_b"])
    y, states = _ssd_fwd(xbca, dtr, bias_p, alog_p, d_x)
    (yssd,) = _rowwise(_gate_rms, [y, z], [P["ssd_norm_w"]], [(1024, BF16)], name="ssd_gate_norm")
    (qn,) = _rowwise(_rms, [qc], [P["mla_q_norm_w"]], [(MLA_Q_RANK, BF16)], name="q_norm")
    (kvn,) = _rowwise(_rms, [kvc], [P["mla_kv_norm_w"]], [(MLA_KV_RANK, BF16)], name="kv_norm")
    q = _mm(qn, W["mla_w_q_b"], name="mm_q")
    kv = _mm(kvn, W["mla_w_kv_b"], name="mm_kv")
    (qr,) = _rowwise(_rope_fwd_fn, [(q, 512, 2), cos512, sin512], [], [512], name="rope_q")
    (krt,) = _rowwise(lambda u, c, s: _spread4(_rope_fwd_fn(u, c, s)), [kr, cos128, sin128], [], [LANE], name="rope_k")
    att = _att_fwd(q, qr, kv, krt)
    (ymla,) = _rowwise(_rms, [att], [P["mla_out_norm_w"]], [(1024, BF16)], name="out_norm")
    ycat = jnp.concatenate([yssd, ymla], axis=1)
    mix = _mm(ycat, W["w_out"], name="mm_out")
    f_h1 = lambda xv, mv, g, b: _ln(ALPHA * xv + mv, g, b)
    h1, h1b = _rowwise(lambda *a: (f_h1(*a),) * 2, [x, mix], [P["ln_mix_g"], P["ln_mix_b"]], [1024, (1024, BF16)],
                       name="ln_mix")
    fb = D_FF // N_DEV
    hg = _mm(h1b, W["w_ffn_gate"], b_blk="n", o_blk="n", name="mm_gate")
    hu = _mm(h1b, W["w_ffn_up"], b_blk="n", o_blk="n", name="mm_up")
    pg = _mm(h1b, W["w_ple_gate"], name="mm_ple_gate")
    pp = _mm(pb, W["w_ple_proj"], name="mm_ple")
    hg2, hu2 = hg.reshape(N_DEV * s_dim, fb), hu.reshape(N_DEV * s_dim, fb)
    (act,) = _rowwise(lambda g, u: _silu(g) * u, [hg2, hu2], [], [(fb, BF16)], name="swiglu", tr=512)
    act3 = act.reshape(N_DEV, s_dim, fb)
    ffn = _mm(act3, W["w_ffn_down"], a_blk="k", b_blk="k", name="mm_down")

    f_h2 = lambda hv, fv, pg, ppv, g, b: _ln(ALPHA * hv + fv + _sigmoid(pg) * ppv, g, b)

    def final_fn(hv, fv, pg, ppv, tv, g, b):
        h2, pull = jax.vjp(f_h2, hv, fv, pg, ppv, g, b)
        diff = h2 - tv
        loss = 0.5 * jnp.sum(jnp.mean(diff * diff, axis=-1, keepdims=True), axis=0, keepdims=True)
        d_h, d_f, d_pg, d_pp, d_g, d_b = pull(diff * (1.0 / D_MODEL))
        return d_h, d_f, d_pg, d_pp, d_g, d_b, jnp.broadcast_to(loss, (1, LANE))

    dh1_a, dffn, dpg, dpp, g_ffn_g, g_ffn_b, loss = _rowwise(
        final_fn, [h1, ffn, pg, pp, tgt], [P["ln_ffn_g"], P["ln_ffn_b"]], [1024] + [(1024, BF16)] * 3,
        [1024, 1024, LANE], name="final")

    G = {}
    dact = _mm(dffn, W["w_ffn_down"], tb=True, b_blk="n", o_blk="n", name="mm_down_dx")
    G["w_ffn_down"] = _mm(act3, dffn, ta=True, a_blk="m", o_blk="m", out_dtype=GRAD_DT, name="mm_down_dw")

    def swiglu_bwd(g, u, d):
        sg = _sigmoid(g)
        return d * u * (sg * (1.0 + g * (1.0 - sg))), d * (g * sg)

    dg, du = _rowwise(swiglu_bwd, [hg2, hu2, dact.reshape(N_DEV * s_dim, fb)], [], [(fb, BF16)] * 2, name="swiglu_bwd",
                      tr=512)
    dg3, du3 = dg.reshape(N_DEV, s_dim, fb), du.reshape(N_DEV, s_dim, fb)
    dh1 = _mm(dg3, W["w_ffn_gate"], tb=True, a_blk="k", b_blk="k", add=dh1_a, name="mm_gate_dx")
    dh1 = _mm(du3, W["w_ffn_up"], tb=True, a_blk="k", b_blk="k", add=dh1, name="mm_up_dx")
    dh1 = _mm(dpg, W["w_ple_gate"], tb=True, add=dh1, name="mm_ple_gate_dx")
    G["w_ffn_gate"] = _mm(h1b, dg3, ta=True, b_blk="n", o_blk="n", out_dtype=GRAD_DT, name="mm_gate_dw")
    G["w_ffn_up"] = _mm(h1b, du3, ta=True, b_blk="n", o_blk="n", out_dtype=GRAD_DT, name="mm_up_dw")
    G["w_ple_gate"] = _mm(h1b, dpg, ta=True, out_dtype=GRAD_DT, name="mm_ple_gate_dw")
    G["w_ple_proj"] = _mm(pb, dpp, ta=True, out_dtype=GRAD_DT, name="mm_ple_dw")
    dx_a, dmix, g_mix_g, g_mix_b = _rowwise(
        lambda xv, mv, dv, g, b: _vjp_rows(f_h1)(xv, mv, g, b, dv), [x, mix, dh1], [P["ln_mix_g"], P["ln_mix_b"]],
        [1024, (1024, BF16)], [1024, 1024], name="ln_mix_bwd")
    dycat = _mm(dmix, W["w_out"], tb=True, name="mm_out_dx")
    G["w_out"] = _mm(ycat, dmix, ta=True, out_dtype=GRAD_DT, name="mm_out_dw")

    datt, g_out_norm = _rowwise(lambda a, dv, w: _vjp_rows(_rms)(a, w, dv), [att, (dycat, 1024, 1)],
                                [P["mla_out_norm_w"]], [1024], [1024], name="out_norm_bwd")
    dqn_nope, dqr, dkn, dv, dkrt = _att_bwd(q, qr, kv, krt, att, datt)
    dkv = jnp.concatenate([dkn, dv], axis=1)
    (dq_rope,) = _rowwise(lambda d0, d1, c, s: _rope_bwd_fn(d0 + d1, c, s), [(dqr, 512, 0), (dqr, 512, 1), cos512, sin512],
                          [], [(512, BF16)], name="rope_q_bwd")

    def rope_k_bwd(*a):
        d = _spread4(functools.reduce(lambda u, w: u + w, a[:-2]))
        lane = lax.broadcasted_iota(jnp.int32, d.shape, 1)
        return _rope_bwd_fn(jnp.where(lane < MLA_ROPE, d, 0.0), a[-2], a[-1])

    (dkr,) = _rowwise(rope_k_bwd, [(dkrt, LANE, k) for k in range(MLA_HEADS // 2)] + [cos128, sin128], [], [(LANE, BF16)],
                      name="rope_k_bwd")
    dq = jnp.concatenate([dqn_nope, dq_rope], axis=1)
    dqn = _mm(dq, W["mla_w_q_b"], tb=True, name="mm_q_dx")
    G["mla_w_q_b"] = _mm(qn, dq, ta=True, out_dtype=GRAD_DT, name="mm_q_dw")
    dkvn = _mm(dkv, W["mla_w_kv_b"], tb=True, name="mm_kv_dx")
    G["mla_w_kv_b"] = _mm(kvn, dkv, ta=True, out_dtype=GRAD_DT, name="mm_kv_dw")
    dqc, g_q_norm = _rowwise(lambda a, dv, w: _vjp_rows(_rms)(a, w, dv), [qc, dqn], [P["mla_q_norm_w"]],
                             [(MLA_Q_RANK, BF16)], [MLA_Q_RANK], name="q_norm_bwd")
    dkvc, g_kv_norm = _rowwise(lambda a, dv, w: _vjp_rows(_rms)(a, w, dv), [kvc, dkvn], [P["mla_kv_norm_w"]],
                               [(MLA_KV_RANK, BF16)], [MLA_KV_RANK], name="kv_norm_bwd")

    dy, dz, g_ssd_norm = _rowwise(lambda yv, zv, dv, w: _vjp_rows(_gate_rms)(yv, zv, w, dv), [y, z, (dycat, 1024, 0)],
                                  [P["ssd_norm_w"]], [1024, (1024, BF16)], [1024], name="ssd_gate_norm_bwd")
    dxbca, ddtr, g_dt_bias, g_alog, g_d = _ssd_bwd(xbca, dtr, bias_p, alog_p, d_x, states, dy)
    da, g_conv_w, g_conv_b = _conv_bwd_pre(xbc, P["ssd_conv_w"], P["ssd_conv_b"], dxbca)
    dxbc = _conv_bwd_in(da, P["ssd_conv_w"])

    dproj = jnp.concatenate([dz, dxbc, ddtr, dqc, dkvc, dkr], axis=1)
    grad_x = _mm(dproj, W["w_in"], tb=True, add=dx_a, name="mm_in_dx")
    G["w_in"] = _mm(xb, dproj, ta=True, out_dtype=GRAD_DT, name="mm_in_dw")

    small = {
        "ssd_conv_b": g_conv_b, "ssd_dt_bias": g_dt_bias, "ssd_A_log": g_alog, "ssd_D": g_d, "ssd_norm_w": g_ssd_norm,
        "mla_q_norm_w": g_q_norm, "mla_kv_norm_w": g_kv_norm, "mla_out_norm_w": g_out_norm, "ln_mix_g": g_mix_g,
        "ln_mix_b": g_mix_b, "ln_ffn_g": g_ffn_g, "ln_ffn_b": g_ffn_b,
    }
    return grad_x, G, _pack_small(g_conv_w, [small[n] for n, _ in REPL_W], loss)


def kernel(x, p, positions, w_in, ssd_conv_w, ssd_conv_b, ssd_dt_bias, ssd_A_log, ssd_D, ssd_norm_w, mla_q_norm_w, mla_w_q_b, mla_kv_norm_w, mla_w_kv_b, mla_out_norm_w, w_out, ln_mix_g, ln_mix_b, w_ffn_gate, w_ffn_up, w_ffn_down, w_ple_gate, w_ple_proj, ln_ffn_g, ln_ffn_b, loss_target, m_w_in, m_ssd_conv_w, m_ssd_conv_b, m_ssd_dt_bias, m_ssd_A_log, m_ssd_D, m_ssd_norm_w, m_mla_q_norm_w, m_mla_w_q_b, m_mla_kv_norm_w, m_mla_w_kv_b, m_mla_out_norm_w, m_w_out, m_ln_mix_g, m_ln_mix_b, m_w_ffn_gate, m_w_ffn_up, m_w_ffn_down, m_w_ple_gate, m_w_ple_proj, m_ln_ffn_g, m_ln_ffn_b, v_w_in, v_ssd_conv_w, v_ssd_conv_b, v_ssd_dt_bias, v_ssd_A_log, v_ssd_D, v_ssd_norm_w, v_mla_q_norm_w, v_mla_w_q_b, v_mla_kv_norm_w, v_mla_w_kv_b, v_mla_out_norm_w, v_w_out, v_ln_mix_g, v_ln_mix_b, v_w_ffn_gate, v_w_ffn_up, v_w_ffn_down, v_w_ple_gate, v_w_ple_proj, v_ln_ffn_g, v_ln_ffn_b):
    args = dict(locals())
    core = lax.axis_index("c")
    me = 4 * lax.axis_index("x") + 2 * lax.axis_index("y") + core

    conv_sh = ssd_conv_w[0]
    conv_hi = conv_sh.astype(BF16)
    conv_lo = (conv_sh - conv_hi.astype(F32)).astype(BF16)
    shards = [args[n][0].astype(BF16) for n in BIG] + [jnp.concatenate([conv_hi, conv_lo], axis=0)]
    gathered = _gather_many(shards, "gather_weights")
    gw = dict(zip(BIG, gathered[:-1]))
    conv_g = gathered[-1].astype(F32)
    rows_full = lambda g: g.reshape(-1, g.shape[2])
    W = {
        "w_in": _win_pad(_cols_full(gw["w_in"])),
        "mla_w_q_b": _heads_split(_cols_full(gw["mla_w_q_b"]), MLA_NOPE, MLA_ROPE),
        "mla_w_kv_b": _heads_split(_cols_full(gw["mla_w_kv_b"]), MLA_NOPE, MLA_V),
        "w_out": rows_full(gw["w_out"]),
        "w_ple_gate": rows_full(gw["w_ple_gate"]),
        "w_ple_proj": _cols_full(gw["w_ple_proj"]),
        "w_ffn_gate": gw["w_ffn_gate"], "w_ffn_up": gw["w_ffn_up"], "w_ffn_down": gw["w_ffn_down"],
    }
    P = {n: args[n] for n, _ in REPL_W}
    P["ssd_conv_w"] = _cols_full(conv_g[:, :4] + conv_g[:, 4:])

    grad_x, G, packed = _local_step(x[0], p[0, 0], positions[0], loss_target[0], W, P)

    blocks = {
        "w_in": _cols_split(_win_unpad(G["w_in"])),
        "mla_w_q_b": _cols_split(_heads_merge(G["mla_w_q_b"], MLA_NOPE, MLA_ROPE)),
        "mla_w_kv_b": _cols_split(_heads_merge(G["mla_w_kv_b"], MLA_NOPE, MLA_V)),
        "w_out": G["w_out"].reshape(N_DEV, -1, D_MODEL),
        "w_ple_gate": G["w_ple_gate"].reshape(N_DEV, -1, D_MODEL),
        "w_ple_proj": _cols_split(G["w_ple_proj"]),
        "w_ffn_gate": G["w_ffn_gate"], "w_ffn_up": G["w_ffn_up"], "w_ffn_down": G["w_ffn_down"],
    }
    glist = [blocks[n] for n in BIG]
    from_sibling = _pair_exchange(glist, "exchange_pairs")
    core_arr = core.astype(jnp.int32).reshape(1)
    sums = [_pair_sum(g, r, core_arr, "pair_sum_" + n) for n, g, r in zip(BIG, glist, from_sibling)]
    recv = _chip_exchange(sums, "exchange_chips")
    big_out = {n: _adam(r, args[n][0], args["m_" + n][0], args["v_" + n][0], "adam_" + n) for n, r in zip(BIG, recv)}

    small_all = _all_gather(packed, "gather_small")
    conv_sum, loss_row, small_out = _adam_small(small_all, [(args[n], args["m_" + n], args["v_" + n]) for n, _ in REPL_W])
    conv_grad = lax.dynamic_slice_in_dim(conv_sum, me * 192, 192, axis=1)
    conv_out = _adam(conv_grad[None], conv_sh, m_ssd_conv_w[0], v_ssd_conv_w[0], "adam_conv")
    small_map = {n: small_out[i] for i, (n, _) in enumerate(REPL_W)}

    def outputs(idx):
        res = []
        for n in WEIGHT_ORDER:
            if n == "ssd_conv_w":
                res.append(conv_out[idx][None])
            elif n in big_out:
                res.append(big_out[n][idx][None])
            else:
                res.append(small_map[n][idx])
        return res

    return (loss_row[0, 0], grad_x[None], *outputs(0), *outputs(1), *outputs(2), *outputs(3))
```

```python
import functools
import math

import numpy as np
import jax
import jax.numpy as jnp
from jax import lax
from jax.experimental import pallas as pl
from jax.experimental.pallas import tpu as pltpu

F32 = jnp.float32
BF16 = jnp.bfloat16
HI = lax.Precision.HIGHEST

N_DEV = 8
D_MODEL = 1024
PLE_DIM = 256
SSD_HEADS = 16
SSD_HEAD_DIM = 64
SSD_INNER = 1024
SSD_STATE = 128
SSD_XBC = 1536
SSD_CHUNK = 128
MLA_HEADS = 16
MLA_Q_RANK = 384
MLA_KV_RANK = 256
MLA_NOPE = 64
MLA_ROPE = 32
MLA_V = 64
ROPE_BASE = 10000.0
D_FF = 2816
IN_WIDTH = 3248
IN_PAD = 3456
ALPHA = 2.0 ** 0.25
EPS = 1e-6
LN_EPS = 1e-5
ATT_SCALE = 1.0 / math.sqrt(MLA_NOPE + MLA_ROPE)
ADAM_LR, ADAM_B1, ADAM_B2, ADAM_EPS, ADAM_WD, ADAM_STEP = 0.001, 0.9, 0.999, 1e-08, 0.01, 10

LANE = 128
MM_TM, MM_TN, MM_TK = 1024, 512, 2048
ROW_TILE = 256
ATT_TQ = 256

GRAD_DT = BF16

BIG = ("w_in", "mla_w_q_b", "mla_w_kv_b", "w_out", "w_ffn_gate", "w_ffn_up", "w_ffn_down", "w_ple_gate", "w_ple_proj")
EARLY = ("w_in", "mla_w_q_b", "mla_w_kv_b")
LATE = ("w_out", "w_ffn_gate", "w_ffn_up", "w_ffn_down", "w_ple_gate", "w_ple_proj")
LATE_GRADS = ("w_ffn_gate", "w_ffn_up", "w_ffn_down", "w_ple_gate", "w_ple_proj")
LAST_GRADS = ("w_in", "mla_w_q_b", "mla_w_kv_b", "w_out")
ROW_SHARDED = ("w_out", "w_ffn_down", "w_ple_gate")
OWNER_BLOCKED = ("w_ffn_gate", "w_ffn_up", "w_ffn_down")
WEIGHT_ORDER = ("w_in", "ssd_conv_w", "ssd_conv_b", "ssd_dt_bias", "ssd_A_log", "ssd_D", "ssd_norm_w", "mla_q_norm_w",
                "mla_w_q_b", "mla_kv_norm_w", "mla_w_kv_b", "mla_out_norm_w", "w_out", "ln_mix_g", "ln_mix_b",
                "w_ffn_gate", "w_ffn_up", "w_ffn_down", "w_ple_gate", "w_ple_proj", "ln_ffn_g", "ln_ffn_b")


def _tile(dim, cap):
    if dim <= cap:
        return dim
    t = (cap // LANE) * LANE
    while dim % t:
        t -= LANE
    return t


def _dot(a, b, dims=(((1,), (0,)), ((), ())), precision=None):
    return lax.dot_general(a, b, dims, preferred_element_type=F32, precision=precision)


_NT = (((1,), (1,)), ((), ()))
_TN = (((0,), (0,)), ((), ()))


def _mm(a, b, *, ta=False, tb=False, a_blk=None, b_blk=None, o_blk=None, add=None, out_dtype=F32, name):
    ka, ma = a.shape[-2:] if ta else a.shape[-2:][::-1]
    nb, kb = b.shape[-2:] if tb else b.shape[-2:][::-1]
    m_dim = N_DEV * ma if a_blk == "m" else ma
    k_dim = N_DEV * ka if a_blk == "k" else ka
    n_dim = N_DEV * nb if b_blk == "n" else nb
    assert k_dim == (N_DEV * kb if b_blk == "k" else kb)
    tm = ma if a_blk == "m" else (m_dim // N_DEV if o_blk == "m" else _tile(m_dim, MM_TM))
    tn = nb if b_blk == "n" else (n_dim // N_DEV if o_blk == "n" else _tile(n_dim, MM_TN))
    tk = ka if a_blk == "k" else (kb if b_blk == "k" else _tile(k_dim, MM_TK))
    nk = k_dim // tk
    dims = (((0 if ta else 1,), (1 if tb else 0,)), ((), ()))
    has_add = add is not None

    def spec(tile, idx, lead):
        if lead is None:
            return pl.BlockSpec(tile, idx)
        return pl.BlockSpec((None,) + tile, lambda i, j, k: (lead(i, j, k),) + idx(i, j, k))

    def a_idx(i, j, k):
        ii, kk = (0 if a_blk == "m" else i), (0 if a_blk == "k" else k)
        return (kk, ii) if ta else (ii, kk)

    def b_idx(i, j, k):
        jj, kk = (0 if b_blk == "n" else j), (0 if b_blk == "k" else k)
        return (jj, kk) if tb else (kk, jj)

    def o_idx(i, j, k):
        return (0 if o_blk == "m" else i, 0 if o_blk == "n" else j)

    pick = {"m": lambda i, j, k: i, "n": lambda i, j, k: j, "k": lambda i, j, k: k, None: None}
    a_spec = spec((tk, tm) if ta else (tm, tk), a_idx, pick[a_blk])
    b_spec = spec((tn, tk) if tb else (tk, tn), b_idx, pick[b_blk])
    o_spec = spec((tm, tn), o_idx, pick[o_blk])

    def body(*refs):
        if has_add:
            a_ref, b_ref, add_ref, o_ref = refs[:4]
        else:
            a_ref, b_ref, o_ref = refs[:3]
        part = _dot(a_ref[...].astype(BF16), b_ref[...].astype(BF16), dims)
        if nk == 1:
            o_ref[...] = ((part + add_ref[...]) if has_add else part).astype(o_ref.dtype)
            return
        acc = refs[-1]
        k = pl.program_id(2)

        @pl.when(k == 0)
        def _():
            acc[...] = (part + add_ref[...]) if has_add else part

        @pl.when(k > 0)
        def _():
            acc[...] += part

        @pl.when(k == nk - 1)
        def _():
            o_ref[...] = acc[...].astype(o_ref.dtype)

    if o_blk == "m":
        out_shape = (N_DEV, tm, n_dim)
    elif o_blk == "n":
        out_shape = (N_DEV, m_dim, tn)
    else:
        out_shape = (m_dim, n_dim)
    ins = [a, b] + ([add] if has_add else [])
    specs = [a_spec, b_spec] + ([pl.BlockSpec((tm, tn), lambda i, j, k: (i, j))] if has_add else [])
    return pl.pallas_call(
        body, name=name, grid=(m_dim // tm, n_dim // tn, nk), in_specs=specs, out_specs=o_spec,
        out_shape=jax.ShapeDtypeStruct(out_shape, out_dtype),
        scratch_shapes=[pltpu.VMEM((tm, tn), F32)] if nk > 1 else [],
        compiler_params=pltpu.CompilerParams(dimension_semantics=("parallel", "parallel", "arbitrary")),
    )(*ins)


def _rowwise(fn, rows, consts, out_widths, acc_widths=(), *, name, tr=ROW_TILE):
    row_arrays, row_specs = [], []
    first_arr = rows[0][0] if isinstance(rows[0], tuple) else rows[0]
    s_dim = first_arr.shape[-2]
    tr = min(tr, s_dim)
    for r in rows:
        arr, width, cb = r if isinstance(r, tuple) else (r, r.shape[-1], 0)
        row_arrays.append(arr)
        if arr.ndim == 3:
            row_specs.append(pl.BlockSpec((None, tr, width), functools.partial(lambda i, k: (k, i, 0), k=cb)))
        else:
            row_specs.append(pl.BlockSpec((tr, width), functools.partial(lambda i, cb: (i, cb), cb=cb)))
    const_specs = [pl.BlockSpec(c.shape, lambda i: (0, 0)) for c in consts]
    nr, nc, no, na = len(rows), len(consts), len(out_widths), len(acc_widths)

    def body(*refs):
        ins = [r[...] for r in refs[:nr + nc]]
        res = fn(*ins)
        if not isinstance(res, (tuple, list)):
            res = (res,)
        out_refs = refs[nr + nc:nr + nc + no]
        acc_refs = refs[nr + nc + no:]
        for o_ref, val in zip(out_refs, res[:no]):
            o_ref[...] = val.astype(o_ref.dtype)
        first = pl.program_id(0) == 0
        for a_ref, val in zip(acc_refs, res[no:]):
            @pl.when(first)
            def _(a_ref=a_ref, val=val):
                a_ref[...] = val

            @pl.when(jnp.logical_not(first))
            def _(a_ref=a_ref, val=val):
                a_ref[...] += val

    outs = [w if isinstance(w, tuple) else (w, F32) for w in out_widths]
    out_shape = [jax.ShapeDtypeStruct((s_dim, w), dt) for w, dt in outs]
    out_shape += [jax.ShapeDtypeStruct((1, w), F32) for w in acc_widths]
    out_specs = [pl.BlockSpec((tr, w), lambda i: (i, 0)) for w, _ in outs]
    out_specs += [pl.BlockSpec((1, w), lambda i: (0, 0)) for w in acc_widths]
    res = pl.pallas_call(
        body, name=name, grid=(s_dim // tr,), in_specs=row_specs + const_specs, out_specs=out_specs, out_shape=out_shape,
        compiler_params=pltpu.CompilerParams(dimension_semantics=("arbitrary",)),
    )(*row_arrays, *consts)
    return res


def _colsum(v):
    return jnp.sum(v, axis=0, keepdims=True)


def _rms(u, g):
    return u * lax.rsqrt(jnp.mean(u * u, axis=-1, keepdims=True) + EPS) * g


def _ln(u, g, b):
    mu = jnp.mean(u, axis=-1, keepdims=True)
    d = u - mu
    var = jnp.mean(d * d, axis=-1, keepdims=True)
    return d * lax.rsqrt(var + LN_EPS) * g + b


def _sigmoid(v):
    return 1.0 / (1.0 + jnp.exp(-v))


def _silu(v):
    return v * _sigmoid(v)


def _softplus(v):
    y = jnp.exp(-jnp.abs(v))
    w = 1.0 + y
    log1p = jnp.where(w == 1.0, y, jnp.log(w) * y / jnp.where(w == 1.0, 1.0, w - 1.0))
    return jnp.maximum(v, 0.0) + log1p


def _gate_rms(y, z, w):
    return _rms(y * _silu(z), w)


def _vjp_rows(f):
    def fn(*args):
        prim, ct = args[:-1], args[-1]
        _, pull = jax.vjp(f, *prim)
        return pull(ct)
    return fn


def _conv_pre(cur, prev, w, b, first):
    row = lax.broadcasted_iota(jnp.int32, cur.shape, 0)
    acc = cur * w[3:4, :] + b
    for j in (1, 2, 3):
        tail = jnp.where(first, 0.0, pltpu.roll(prev, j, 0))
        acc = acc + jnp.where(row >= j, pltpu.roll(cur, j, 0), tail) * w[3 - j:4 - j, :]
    return acc


def _conv_fwd(u, w, b, name="conv_fwd"):
    s_dim, width = u.shape
    tr = min(ROW_TILE, s_dim)

    def body(cur_ref, prev_ref, w_ref, b_ref, o_ref):
        pre = _conv_pre(cur_ref[...], prev_ref[...], w_ref, b_ref[...], pl.program_id(0) == 0)
        o_ref[...] = _silu(pre)

    return pl.pallas_call(
        body, name=name, grid=(s_dim // tr,),
        in_specs=[pl.BlockSpec((tr, width), lambda i: (i, 0)), pl.BlockSpec((tr, width), lambda i: (jnp.maximum(i - 1, 0), 0)),
                  pl.BlockSpec(w.shape, lambda i: (0, 0)), pl.BlockSpec(b.shape, lambda i: (0, 0))],
        out_specs=pl.BlockSpec((tr, width), lambda i: (i, 0)), out_shape=jax.ShapeDtypeStruct((s_dim, width), F32),
        compiler_params=pltpu.CompilerParams(dimension_semantics=("arbitrary",)),
    )(u, u, w, b)


def _conv_bwd_pre(u, w, b, dact, name="conv_bwd_pre"):
    s_dim, width = u.shape
    tr = min(ROW_TILE, s_dim)

    def body(cur_ref, prev_ref, w_ref, b_ref, d_ref, da_ref, dw_ref, db_ref):
        first = pl.program_id(0) == 0
        cur, prev = cur_ref[...], prev_ref[...]
        pre = _conv_pre(cur, prev, w_ref, b_ref[...], first)
        sg = _sigmoid(pre)
        da = d_ref[...] * (sg * (1.0 + pre * (1.0 - sg)))
        da_ref[...] = da
        row = lax.broadcasted_iota(jnp.int32, cur.shape, 0)

        @pl.when(first)
        def _():
            dw_ref[...] = jnp.zeros_like(dw_ref)
            db_ref[...] = jnp.zeros_like(db_ref)

        db_ref[...] += _colsum(da)
        dw_ref[3:4, :] += _colsum(da * cur)
        for j in (1, 2, 3):
            tail = jnp.where(first, 0.0, pltpu.roll(prev, j, 0))
            sh = jnp.where(row >= j, pltpu.roll(cur, j, 0), tail)
            dw_ref[3 - j:4 - j, :] += _colsum(da * sh)

    return pl.pallas_call(
        body, name=name, grid=(s_dim // tr,),
        in_specs=[pl.BlockSpec((tr, width), lambda i: (i, 0)), pl.BlockSpec((tr, width), lambda i: (jnp.maximum(i - 1, 0), 0)),
                  pl.BlockSpec(w.shape, lambda i: (0, 0)), pl.BlockSpec(b.shape, lambda i: (0, 0)),
                  pl.BlockSpec((tr, width), lambda i: (i, 0))],
        out_specs=[pl.BlockSpec((tr, width), lambda i: (i, 0)), pl.BlockSpec(w.shape, lambda i: (0, 0)),
                   pl.BlockSpec(b.shape, lambda i: (0, 0))],
        out_shape=[jax.ShapeDtypeStruct((s_dim, width), F32), jax.ShapeDtypeStruct(w.shape, F32),
                   jax.ShapeDtypeStruct(b.shape, F32)],
        compiler_params=pltpu.CompilerParams(dimension_semantics=("arbitrary",)),
    )(u, u, w, b, dact)


def _conv_bwd_in(da, w, name="conv_bwd_in"):
    s_dim, width = da.shape
    tr = min(ROW_TILE, s_dim)
    n = s_dim // tr

    def body(cur_ref, nxt_ref, w_ref, o_ref):
        last = pl.program_id(0) == n - 1
        cur, nxt = cur_ref[...], nxt_ref[...]
        row = lax.broadcasted_iota(jnp.int32, cur.shape, 0)
        acc = cur * w_ref[3:4, :]
        for j in (1, 2, 3):
            head = jnp.where(last, 0.0, pltpu.roll(nxt, tr - j, 0))
            acc = acc + jnp.where(row < tr - j, pltpu.roll(cur, tr - j, 0), head) * w_ref[3 - j:4 - j, :]
        o_ref[...] = acc.astype(o_ref.dtype)

    return pl.pallas_call(
        body, name=name, grid=(n,),
        in_specs=[pl.BlockSpec((tr, width), lambda i: (i, 0)), pl.BlockSpec((tr, width), lambda i: (jnp.minimum(i + 1, n - 1), 0)),
                  pl.BlockSpec(w.shape, lambda i: (0, 0))],
        out_specs=pl.BlockSpec((tr, width), lambda i: (i, 0)), out_shape=jax.ShapeDtypeStruct((s_dim, width), BF16),
        compiler_params=pltpu.CompilerParams(dimension_semantics=("arbitrary",)),
    )(da, da, w)


def _ssd_consts():
    L = SSD_CHUNK
    tri = np.tril(np.ones((L, L), np.float32))
    expand = np.zeros((LANE, SSD_INNER), np.float32)
    expand128 = np.zeros((LANE, SSD_HEADS * LANE), np.float32)
    for h in range(SSD_HEADS):
        expand[h, h * SSD_HEAD_DIM:(h + 1) * SSD_HEAD_DIM] = 1.0
        expand128[h, h * LANE:(h + 1) * LANE] = 1.0
    return jnp.asarray(tri), jnp.asarray(expand), jnp.asarray(expand128), jnp.asarray(expand.T.copy())


def _ssd_prep(dt_ref, bias_ref, alog_ref, tri_ref, exp_ref, exp128_ref, cs_s, cst_s, ex_s, csx_s):
    L = SSD_CHUNK
    dt = _softplus(dt_ref[...] + bias_ref[...])
    a = -jnp.exp(alog_ref[...])
    cs = _dot(tri_ref[...], dt * a, precision=HI)
    cs_s[...] = cs
    cst_s[...] = cs.T
    last = cs_s[L - 1:L, :]
    expand = exp_ref[...]
    ex_s[...] = _dot(jnp.exp(cs), expand, precision=HI)
    f_x = _dot(jnp.exp(last - cs), expand, precision=HI)
    dt_x = _dot(dt, expand, precision=HI)
    csx_s[...] = _dot(cs, exp128_ref[...], precision=HI)
    t_x = ex_s[L - 1:L, :]
    return dt, a, dt_x, f_x, t_x


def _decay_matrix(csx_s, cst_s, h, tril):
    seg = csx_s[:, h * LANE:(h + 1) * LANE] - cst_s[h:h + 1, :]
    return jnp.exp(jnp.where(tril, seg, -jnp.inf))


def _ssd_fwd(xbca, dtr, bias, alog, d_x, name="ssd_fwd"):
    s_dim = xbca.shape[0]
    L = SSD_CHUNK
    nc = s_dim // L
    tri, expand, expand128, _ = _ssd_consts()

    def body(xs_ref, b_ref, c_ref, dt_ref, bias_ref, alog_ref, dx_ref, tri_ref, exp_ref, exp128_ref,
             y_ref, st_ref, st_s, cs_s, cst_s, ex_s, csx_s):
        @pl.when(pl.program_id(0) == 0)
        def _():
            st_s[...] = jnp.zeros_like(st_s)

        dt, a, dt_x, f_x, t_x = _ssd_prep(dt_ref, bias_ref, alog_ref, tri_ref, exp_ref, exp128_ref, cs_s, cst_s, ex_s, csx_s)
        st_ref[0] = st_s[...]
        row = lax.broadcasted_iota(jnp.int32, (L, L), 0)
        col = lax.broadcasted_iota(jnp.int32, (L, L), 1)
        tril = row >= col
        low = col < SSD_HEAD_DIM
        for g in range(2):
            bg = b_ref[:, g * LANE:(g + 1) * LANE]
            cg = c_ref[:, g * LANE:(g + 1) * LANE].astype(BF16)
            gmat = _dot(cg, bg.astype(BF16), _NT)
            bgt = bg.T.astype(BF16)
            for jj in range(4):
                j = 4 * g + jj
                sl = slice(j * LANE, (j + 1) * LANE)
                xp = xs_ref[:, sl]
                x_dt = xp * dt_x[:, sl]
                xb = x_dt.astype(BF16)
                yd = []
                for e in range(2):
                    lm = _decay_matrix(csx_s, cst_s, 2 * j + e, tril)
                    yd.append(_dot((gmat * lm).astype(BF16), xb))
                stp = st_s[j]
                z = _dot(cg, stp.astype(BF16))
                y_ref[:, sl] = jnp.where(low, yd[0], yd[1]) + ex_s[:, sl] * z + dx_ref[:, sl] * xp
                xf = (x_dt * f_x[:, sl]).astype(BF16)
                st_s[j] = t_x[:, sl] * stp + _dot(bgt, xf)

    const = lambda shape: pl.BlockSpec(shape, lambda c: tuple(0 for _ in shape))
    return pl.pallas_call(
        body, name=name, grid=(nc,),
        in_specs=[pl.BlockSpec((L, 1024), lambda c: (c, 0)), pl.BlockSpec((L, 256), lambda c: (c, 4)),
                  pl.BlockSpec((L, 256), lambda c: (c, 5)), pl.BlockSpec((L, LANE), lambda c: (c, 0)),
                  const((1, LANE)), const((1, LANE)), const((1, 1024)), const((L, L)), const((LANE, 1024)),
                  const((LANE, 2048))],
        out_specs=[pl.BlockSpec((L, 1024), lambda c: (c, 0)), pl.BlockSpec((1, 8, LANE, LANE), lambda c: (c, 0, 0, 0))],
        out_shape=[jax.ShapeDtypeStruct((s_dim, 1024), F32), jax.ShapeDtypeStruct((nc, 8, LANE, LANE), F32)],
        scratch_shapes=[pltpu.VMEM((8, LANE, LANE), F32), pltpu.VMEM((L, LANE), F32), pltpu.VMEM((LANE, L), F32),
                        pltpu.VMEM((L, 1024), F32), pltpu.VMEM((L, 2048), F32)],
        compiler_params=pltpu.CompilerParams(dimension_semantics=("arbitrary",)),
    )(xbca, xbca, xbca, dtr, bias, alog, d_x, tri, expand, expand128)


def _ssd_bwd(xbca, dtr, bias, alog, d_x, states, dy, name="ssd_bwd"):
    s_dim = xbca.shape[0]
    L = SSD_CHUNK
    nc = s_dim // L
    tri, expand, expand128, expand_t = _ssd_consts()

    def body(xs_ref, b_ref, c_ref, dt_ref, bias_ref, alog_ref, dx_ref, tri_ref, exp_ref, exp128_ref, expt_ref,
             st_ref, dy_ref, dxbc_ref, ddt_ref, dbias_ref, dalog_ref, dd_ref,
             dst_s, cs_s, cst_s, ex_s, csx_s, dcsx_s, ddtx_s, dcol_s, drow_s, dlast_s, dd_s):
        @pl.when(pl.program_id(0) == 0)
        def _():
            dst_s[...] = jnp.zeros_like(dst_s)
            dbias_ref[...] = jnp.zeros_like(dbias_ref)
            dalog_ref[...] = jnp.zeros_like(dalog_ref)
            dd_s[...] = jnp.zeros_like(dd_s)

        dt, a, dt_x, f_x, t_x = _ssd_prep(dt_ref, bias_ref, alog_ref, tri_ref, exp_ref, exp128_ref, cs_s, cst_s, ex_s, csx_s)
        row = lax.broadcasted_iota(jnp.int32, (L, L), 0)
        col = lax.broadcasted_iota(jnp.int32, (L, L), 1)
        tril = row >= col
        low = col < SSD_HEAD_DIM
        dcol_s[...] = jnp.zeros_like(dcol_s)
        drow_s[...] = jnp.zeros_like(drow_s)
        for g in range(2):
            bg = b_ref[:, g * LANE:(g + 1) * LANE]
            cg = c_ref[:, g * LANE:(g + 1) * LANE]
            bgb, cgb = bg.astype(BF16), cg.astype(BF16)
            gmat = _dot(cgb, bgb, _NT)
            d_g = jnp.zeros((L, L), F32)
            d_b = jnp.zeros((L, LANE), F32)
            d_c = jnp.zeros((L, LANE), F32)
            for jj in range(4):
                j = 4 * g + jj
                sl = slice(j * LANE, (j + 1) * LANE)
                xp = xs_ref[:, sl]
                dtp = dt_x[:, sl]
                x_dt = xp * dtp
                xb = x_dt.astype(BF16)
                dyp = dy_ref[:, sl]
                dd_s[:, sl] += _colsum(dyp * xp)
                d_xdt = jnp.zeros((L, LANE), F32)
                for e in range(2):
                    h = 2 * j + e
                    lm = _decay_matrix(csx_s, cst_s, h, tril)
                    m = gmat * lm
                    dye = jnp.where(low if e == 0 else jnp.logical_not(low), dyp, 0.0).astype(BF16)
                    d_m = jnp.where(tril, _dot(dye, xb, _NT), 0.0)
                    d_xdt = d_xdt + _dot(m.astype(BF16), dye, _TN)
                    d_g = d_g + d_m * lm
                    w = d_m * m
                    dcol_s[...] += jnp.where(col == h, jnp.sum(w, axis=1, keepdims=True), 0.0)
                    drow_s[...] += jnp.where(row == h, jnp.sum(w, axis=0, keepdims=True), 0.0)
                stp = st_ref[0, j]
                stb = stp.astype(BF16)
                dstn = dst_s[j]
                dstb = dstn.astype(BF16)
                e_p = ex_s[:, sl]
                f_p = f_x[:, sl]
                t_p = t_x[:, sl]
                z = _dot(cgb, stb)
                d_z = (e_p * dyp).astype(BF16)
                d_c = d_c + _dot(d_z, stb, _NT)
                d_xf = _dot(bgb, dstb)
                d_b = d_b + _dot((x_dt * f_p).astype(BF16), dstb, _NT)
                d_xdt = d_xdt + f_p * d_xf
                d_f = x_dt * d_xf * f_p
                dcsx_s[:, sl] = dyp * e_p * z - d_f
                dlast_s[:, sl] = _colsum(d_f) + _colsum(dstn * stp) * t_p
                dst_s[j] = _dot(cgb, d_z, _TN) + t_p * dstn
                dxbc_ref[:, sl] = dx_ref[:, sl] * dyp + d_xdt * dtp
                ddtx_s[:, sl] = d_xdt * xp
            d_gb = d_g.astype(BF16)
            dxbc_ref[:, 1024 + g * LANE:1024 + (g + 1) * LANE] = d_b + _dot(d_gb, cgb, _TN)
            dxbc_ref[:, 1280 + g * LANE:1280 + (g + 1) * LANE] = d_c + _dot(d_gb, bgb)

        expt = expt_ref[...]
        dlast = _dot(jnp.broadcast_to(dlast_s[...], (8, 1024)), expt, precision=HI)
        d_cs = dcol_s[...] - drow_s[...].T + _dot(dcsx_s[...], expt, precision=HI)
        rown = lax.broadcasted_iota(jnp.int32, (L, LANE), 0)
        d_cs = d_cs + jnp.where(rown == L - 1, jnp.sum(dlast, axis=0, keepdims=True) * 0.125, 0.0)
        d_da = _dot(tri_ref[...], d_cs, _TN, precision=HI)
        d_dt = d_da * a + _dot(ddtx_s[...], expt, precision=HI)
        dalog_ref[...] += _colsum(d_da * dt) * a
        d_raw = d_dt * _sigmoid(dt_ref[...] + bias_ref[...])
        ddt_ref[...] = d_raw.astype(ddt_ref.dtype)
        dbias_ref[...] += _colsum(d_raw)
        dd8 = _dot(jnp.broadcast_to(dd_s[...], (8, 1024)), expt, precision=HI)
        dd_ref[...] = jnp.sum(dd8, axis=0, keepdims=True) * 0.125

    const = lambda shape: pl.BlockSpec(shape, lambda c: tuple(0 for _ in shape))
    rev = lambda cb: (lambda c: (nc - 1 - c, cb))
    return pl.pallas_call(
        body, name=name, grid=(nc,),
        in_specs=[pl.BlockSpec((L, 1024), rev(0)), pl.BlockSpec((L, 256), rev(4)), pl.BlockSpec((L, 256), rev(5)),
                  pl.BlockSpec((L, LANE), rev(0)), const((1, LANE)), const((1, LANE)), const((1, 1024)), const((L, L)),
                  const((LANE, 1024)), const((LANE, 2048)), const((1024, LANE)),
                  pl.BlockSpec((1, 8, LANE, LANE), lambda c: (nc - 1 - c, 0, 0, 0)), pl.BlockSpec((L, 1024), rev(0))],
        out_specs=[pl.BlockSpec((L, SSD_XBC), rev(0)), pl.BlockSpec((L, LANE), rev(0)), const((1, LANE)), const((1, LANE)),
                   const((1, LANE))],
        out_shape=[jax.ShapeDtypeStruct((s_dim, SSD_XBC), F32), jax.ShapeDtypeStruct((s_dim, LANE), BF16),
                   jax.ShapeDtypeStruct((1, LANE), F32), jax.ShapeDtypeStruct((1, LANE), F32),
                   jax.ShapeDtypeStruct((1, LANE), F32)],
        scratch_shapes=[pltpu.VMEM((8, LANE, LANE), F32), pltpu.VMEM((L, LANE), F32), pltpu.VMEM((LANE, L), F32),
                        pltpu.VMEM((L, 1024), F32), pltpu.VMEM((L, 2048), F32), pltpu.VMEM((L, 1024), F32),
                        pltpu.VMEM((L, 1024), F32), pltpu.VMEM((L, LANE), F32), pltpu.VMEM((LANE, L), F32),
                        pltpu.VMEM((1, 1024), F32), pltpu.VMEM((1, 1024), F32)],
        compiler_params=pltpu.CompilerParams(dimension_semantics=("arbitrary",)),
    )(xbca, xbca, xbca, dtr, bias, alog, d_x, tri, expand, expand128, expand_t, states, dy)


def _swap_halves(u):
    width = u.shape[1]
    lane = lax.broadcasted_iota(jnp.int32, u.shape, 1)
    return jnp.where(lane % MLA_ROPE < MLA_ROPE // 2, pltpu.roll(u, width - MLA_ROPE // 2, 1), pltpu.roll(u, MLA_ROPE // 2, 1))


def _rope_fwd_fn(u, cos, sin):
    return u * cos + _swap_halves(u) * sin


def _rope_bwd_fn(d, cos, sin):
    return d * cos + _swap_halves(d * sin)


def _spread4(v):
    return v + pltpu.roll(v, 32, 1) + pltpu.roll(v, 64, 1) + pltpu.roll(v, 96, 1)


def _att_masks(tq):
    lane = lax.broadcasted_iota(jnp.int32, (tq, LANE), 1)
    return lane // MLA_NOPE, lane // MLA_ROPE


def _att_tile(i, tq):
    klen = (i + 1) * tq
    qpos = i * tq + lax.broadcasted_iota(jnp.int32, (tq, klen), 0)
    kpos = lax.broadcasted_iota(jnp.int32, (tq, klen), 1)
    return slice(i * tq, (i + 1) * tq), klen, qpos >= kpos


def _att_qcat(qn_t, qr_t, par, e, half_id, grp_id):
    return jnp.concatenate([jnp.where(half_id == par, qn_t, 0.0), jnp.where(grp_id == e, qr_t, 0.0)], axis=1).astype(BF16)


def _att_probs(qcat, kcat, causal):
    s = jnp.where(causal, _dot(qcat, kcat, _NT) * ATT_SCALE, -jnp.inf)
    p = jnp.exp(s - jnp.max(s, axis=1, keepdims=True))
    return p * (1.0 / jnp.sum(p, axis=1, keepdims=True))


def _att_specs(s_dim):
    col = lambda f: pl.BlockSpec((s_dim, LANE), lambda j: (0, f(j)))
    return [col(lambda j: j), col(lambda j: j // 2), col(lambda j: j), col(lambda j: 0), col(lambda j: 8 + j)]


def _att_fwd(q, qr, kv, krt, name="att_fwd"):
    s_dim = q.shape[0]
    tq = min(ATT_TQ, s_dim)

    def body(qn_ref, qr_ref, kn_ref, krt_ref, v_ref, o_ref, kcat_s, vb_s):
        e0 = 2 * (pl.program_id(0) % 2)
        half_id, grp_id = _att_masks(tq)
        kcat_s[...] = jnp.concatenate([kn_ref[...], krt_ref[...]], axis=1).astype(BF16)
        vb_s[...] = v_ref[...].astype(BF16)
        for i in range(s_dim // tq):
            rows, klen, causal = _att_tile(i, tq)
            qn_t, qr_t = qn_ref[rows, :], qr_ref[rows, :]
            outs = []
            for par in range(2):
                qcat = _att_qcat(qn_t, qr_t, par, e0 + par, half_id, grp_id)
                p = _att_probs(qcat, kcat_s[0:klen, :], causal)
                outs.append(_dot(p.astype(BF16), vb_s[0:klen, :]))
            o_ref[rows, :] = jnp.where(half_id == 0, outs[0], outs[1])

    return pl.pallas_call(
        body, name=name, grid=(MLA_HEADS // 2,), in_specs=_att_specs(s_dim),
        out_specs=pl.BlockSpec((s_dim, LANE), lambda j: (0, j)), out_shape=jax.ShapeDtypeStruct((s_dim, 1024), F32),
        scratch_shapes=[pltpu.VMEM((s_dim, 2 * LANE), BF16), pltpu.VMEM((s_dim, LANE), BF16)],
        compiler_params=pltpu.CompilerParams(dimension_semantics=("parallel",)),
    )(q, qr, kv, krt, kv)


def _att_bwd(q, qr, kv, krt, o, do, name="att_bwd"):
    s_dim = q.shape[0]
    tq = min(ATT_TQ, s_dim)

    def body(qn_ref, qr_ref, kn_ref, krt_ref, v_ref, o_ref, do_ref, dqn_ref, dqr_ref, dkn_ref, dv_ref, dkrt_ref,
             kcat_s, vb_s):
        e0 = 2 * (pl.program_id(0) % 2)
        half_id, grp_id = _att_masks(tq)
        kcat_s[...] = jnp.concatenate([kn_ref[...], krt_ref[...]], axis=1).astype(BF16)
        vb_s[...] = v_ref[...].astype(BF16)
        dkn_ref[...] = jnp.zeros_like(dkn_ref)
        dv_ref[...] = jnp.zeros_like(dv_ref)
        dkrt_ref[...] = jnp.zeros_like(dkrt_ref)
        for i in range(s_dim // tq):
            rows, klen, causal = _att_tile(i, tq)
            qn_t, qr_t, o_t, do_t = qn_ref[rows, :], qr_ref[rows, :], o_ref[rows, :], do_ref[rows, :]
            dqn = jnp.zeros((tq, LANE), F32)
            dqr = jnp.zeros((tq, LANE), F32)
            for par in range(2):
                qcat = _att_qcat(qn_t, qr_t, par, e0 + par, half_id, grp_id)
                p = _att_probs(qcat, kcat_s[0:klen, :], causal)
                dom = jnp.where(half_id == par, do_t, 0.0)
                domb = dom.astype(BF16)
                d_p = _dot(domb, vb_s[0:klen, :], _NT)
                d_row = jnp.sum(dom * o_t, axis=1, keepdims=True)
                d_s = (p * (d_p - d_row) * ATT_SCALE).astype(BF16)
                dqcat = _dot(d_s, kcat_s[0:klen, :])
                dqn = dqn + jnp.where(half_id == par, dqcat[:, :LANE], 0.0)
                dqr = dqr + jnp.where(grp_id == e0 + par, dqcat[:, LANE:], 0.0)
                dkcat = _dot(d_s, qcat, _TN)
                dkn_ref[0:klen, :] += dkcat[:, :LANE]
                dkrt_ref[0:klen, :] += dkcat[:, LANE:]
                dv_ref[0:klen, :] += _dot(p.astype(BF16), domb, _TN)
            dqn_ref[rows, :] = dqn.astype(dqn_ref.dtype)
            dqr_ref[rows, :] = dqr

    col = lambda f: pl.BlockSpec((s_dim, LANE), lambda j: (0, f(j)))
    return pl.pallas_call(
        body, name=name, grid=(MLA_HEADS // 2,), in_specs=_att_specs(s_dim) + [col(lambda j: j), col(lambda j: j)],
        out_specs=[col(lambda j: j), pl.BlockSpec((None, s_dim, LANE), lambda j: (j % 2, 0, j // 2)), col(lambda j: j),
                   col(lambda j: j), pl.BlockSpec((None, s_dim, LANE), lambda j: (j, 0, 0))],
        out_shape=[jax.ShapeDtypeStruct((s_dim, 1024), BF16), jax.ShapeDtypeStruct((2, s_dim, 512), F32),
                   jax.ShapeDtypeStruct((s_dim, 1024), F32), jax.ShapeDtypeStruct((s_dim, 1024), F32),
                   jax.ShapeDtypeStruct((MLA_HEADS // 2, s_dim, LANE), F32)],
        scratch_shapes=[pltpu.VMEM((s_dim, 2 * LANE), BF16), pltpu.VMEM((s_dim, LANE), BF16)],
        compiler_params=pltpu.CompilerParams(dimension_semantics=("parallel",)),
    )(q, qr, kv, krt, kv, o, do)


def _all_gather(x, name):
    rows, width = x.shape

    def body(x_ref, out_ref, send_sems, recv_sems, local_sem):
        x_i, y_i, c_i = lax.axis_index("x"), lax.axis_index("y"), lax.axis_index("c")
        me, sibling = (x_i, y_i, c_i), (x_i, y_i, 1 - c_i)
        chips = [(1 - x_i, y_i), (x_i, 1 - y_i), (1 - x_i, 1 - y_i)]

        def slot(px, py, pc):
            return out_ref.at[4 * px + 2 * py + pc]

        def copy(k, block, to, src=None):
            return pltpu.make_async_remote_copy(
                src_ref=slot(*block) if src is None else src, dst_ref=slot(*block), send_sem=send_sems.at[k],
                recv_sem=recv_sems.at[k], device_id=to, device_id_type=pl.DeviceIdType.MESH)

        mine = pltpu.make_async_copy(x_ref, slot(*me), local_sem)
        mine.start()
        first = [copy(0, me, sibling, src=x_ref)]
        first += [copy(1 + j, me, (*chip, c_i), src=x_ref) for j, chip in enumerate(chips)]
        for cp in first:
            cp.start()
        passed = [copy(4 + j, (*chip, c_i), sibling) for j, chip in enumerate(chips)]
        for j, chip in enumerate(chips):
            copy(1 + j, (*chip, c_i), me).wait_recv()
            passed[j].start()
        copy(0, sibling, me).wait_recv()
        for j, chip in enumerate(chips):
            copy(4 + j, (*chip, 1 - c_i), me).wait_recv()
        for cp in first + passed:
            cp.wait_send()
        mine.wait()

    return pl.pallas_call(
        body, name=name, out_shape=jax.ShapeDtypeStruct((N_DEV, rows, width), x.dtype),
        in_specs=[pl.BlockSpec(memory_space=pl.ANY)], out_specs=pl.BlockSpec(memory_space=pl.ANY),
        scratch_shapes=[pltpu.SemaphoreType.DMA((7,)), pltpu.SemaphoreType.DMA((7,)), pltpu.SemaphoreType.DMA],
    )(x)


def _gather_many(shards, name):
    n_arr = len(shards)

    def body(*refs):
        x_refs, out_refs = refs[:n_arr], refs[n_arr:2 * n_arr]
        send_sems, recv_sems, local_sems = refs[2 * n_arr:]
        x_i, y_i, c_i = lax.axis_index("x"), lax.axis_index("y"), lax.axis_index("c")
        me, sibling = (x_i, y_i, c_i), (x_i, y_i, 1 - c_i)
        chips = [(1 - x_i, y_i), (x_i, 1 - y_i), (1 - x_i, 1 - y_i)]

        def copy(a, k, block, to, src=None):
            slot = out_refs[a].at[4 * block[0] + 2 * block[1] + block[2]]
            return pltpu.make_async_remote_copy(
                src_ref=slot if src is None else src, dst_ref=slot, send_sem=send_sems.at[a, k],
                recv_sem=recv_sems.at[a, k], device_id=to, device_id_type=pl.DeviceIdType.MESH)

        mine, first, passed = [], [], []
        for a in range(n_arr):
            mine.append(pltpu.make_async_copy(x_refs[a], out_refs[a].at[4 * x_i + 2 * y_i + c_i], local_sems.at[a]))
            mine[a].start()
            first.append([copy(a, 0, me, sibling, src=x_refs[a])]
                         + [copy(a, 1 + j, me, (*chip, c_i), src=x_refs[a]) for j, chip in enumerate(chips)])
            for cp in first[a]:
                cp.start()
            passed.append([copy(a, 4 + j, (*chip, c_i), sibling) for j, chip in enumerate(chips)])
        for j, chip in enumerate(chips):
            for a in range(n_arr):
                copy(a, 1 + j, (*chip, c_i), me).wait_recv()
                passed[a][j].start()
        for a in range(n_arr):
            copy(a, 0, sibling, me).wait_recv()
            for j, chip in enumerate(chips):
                copy(a, 4 + j, (*chip, 1 - c_i), me).wait_recv()
        for a in range(n_arr):
            for cp in first[a] + passed[a]:
                cp.wait_send()
            mine[a].wait()

    any_spec = pl.BlockSpec(memory_space=pl.ANY)
    return pl.pallas_call(
        body, name=name, out_shape=[jax.ShapeDtypeStruct((N_DEV,) + x.shape, x.dtype) for x in shards],
        in_specs=[any_spec] * n_arr, out_specs=[any_spec] * n_arr,
        scratch_shapes=[pltpu.SemaphoreType.DMA((n_arr, 7)), pltpu.SemaphoreType.DMA((n_arr, 7)),
                        pltpu.SemaphoreType.DMA((n_arr,))],
    )(*shards)


def _pair_exchange(grads, name):
    n_arr = len(grads)

    def body(*refs):
        g_refs, out_refs = refs[:n_arr], refs[n_arr:2 * n_arr]
        send_sems, recv_sems = refs[2 * n_arr:]
        x_i, y_i, c_i = lax.axis_index("x"), lax.axis_index("y"), lax.axis_index("c")
        copies = []
        for a in range(n_arr):
            for chip in range(4):
                copies.append(pltpu.make_async_remote_copy(
                    src_ref=g_refs[a].at[2 * chip + (1 - c_i)], dst_ref=out_refs[a].at[chip], send_sem=send_sems.at[a, chip],
                    recv_sem=recv_sems.at[a, chip], device_id=(x_i, y_i, 1 - c_i), device_id_type=pl.DeviceIdType.MESH))
        for cp in copies:
            cp.start()
        for cp in copies:
            cp.wait_recv()
        for cp in copies:
            cp.wait_send()

    any_spec = pl.BlockSpec(memory_space=pl.ANY)
    return pl.pallas_call(
        body, name=name, out_shape=[jax.ShapeDtypeStruct((4,) + g.shape[1:], g.dtype) for g in grads],
        in_specs=[any_spec] * n_arr, out_specs=[any_spec] * n_arr,
        scratch_shapes=[pltpu.SemaphoreType.DMA((n_arr, 4)), pltpu.SemaphoreType.DMA((n_arr, 4))],
    )(*grads)


def _chip_exchange(sums, name):
    n_arr = len(sums)

    def body(*refs):
        s_refs, out_refs = refs[:n_arr], refs[n_arr:2 * n_arr]
        send_sems, recv_sems, local_sems = refs[2 * n_arr:]
        x_i, y_i, c_i = lax.axis_index("x"), lax.axis_index("y"), lax.axis_index("c")
        my_chip = 2 * x_i + y_i
        copies, local = [], []
        for a in range(n_arr):
            local.append(pltpu.make_async_copy(s_refs[a].at[my_chip], out_refs[a].at[my_chip], local_sems.at[a]))
            local[a].start()
            for k in range(1, 4):
                px, py = x_i ^ (k >> 1), y_i ^ (k & 1)
                copies.append(pltpu.make_async_remote_copy(
                    src_ref=s_refs[a].at[2 * px + py], dst_ref=out_refs[a].at[my_chip], send_sem=send_sems.at[a, k - 1],
                    recv_sem=recv_sems.at[a, k - 1], device_id=(px, py, c_i), device_id_type=pl.DeviceIdType.MESH))
        for cp in copies:
            cp.start()
        for cp in copies:
            cp.wait_recv()
        for cp in copies:
            cp.wait_send()
        for cp in local:
            cp.wait()

    any_spec = pl.BlockSpec(memory_space=pl.ANY)
    return pl.pallas_call(
        body, name=name, out_shape=[jax.ShapeDtypeStruct(s.shape, s.dtype) for s in sums],
        in_specs=[any_spec] * n_arr, out_specs=[any_spec] * n_arr,
        scratch_shapes=[pltpu.SemaphoreType.DMA((n_arr, 3)), pltpu.SemaphoreType.DMA((n_arr, 3)),
                        pltpu.SemaphoreType.DMA((n_arr,))],
    )(*sums)


_HBM = pl.BlockSpec(memory_space=pltpu.HBM)
_SEM = pl.BlockSpec(memory_space=pltpu.SEMAPHORE)


def _plan_copies(plan, src_refs, land_refs, send_sems, recv_sems):
    copies = []
    for s_ref, l_ref in zip(src_refs, land_refs):
        for src, dst, peer in plan(s_ref, l_ref):
            k = len(copies)
            copies.append(pltpu.make_async_remote_copy(
                src_ref=src, dst_ref=dst, send_sem=send_sems.at[k], recv_sem=recv_sems.at[k], device_id=peer,
                device_id_type=pl.DeviceIdType.MESH))
    return copies


def _split_start(srcs, lands, plan, n_copy, name):
    n = len(srcs)

    def body(*refs):
        for cp in _plan_copies(plan, refs[:n], refs[n:2 * n], refs[2 * n], refs[2 * n + 1]):
            cp.start()
        refs[-1][...] = jnp.zeros_like(refs[-1])

    sems = pltpu.SemaphoreType.DMA((n * n_copy,))
    res = pl.pallas_call(
        body, name=name,
        out_shape=(sems, sems, *[pltpu.HBM(a.shape, a.dtype) for a in list(srcs) + list(lands)],
                   jax.ShapeDtypeStruct((8, LANE), F32)),
        in_specs=[_HBM] * (2 * n), out_specs=(_SEM, _SEM, *[_HBM] * (2 * n), pl.BlockSpec(memory_space=pltpu.VMEM)),
        input_output_aliases={i: 2 + i for i in range(2 * n)},
        compiler_params=pltpu.CompilerParams(has_side_effects=pltpu.SideEffectType.DATAFLOW_SIDE_EFFECTING),
    )(*[pltpu.with_memory_space_constraint(a, pltpu.HBM) for a in list(srcs) + list(lands)])
    return res[0], res[1], list(res[2:2 + n]), list(res[2 + n:2 + 2 * n]), res[-1]


def _split_wait(send_sems, recv_sems, srcs, lands, after, plan, name):
    n = len(srcs)

    def body(*refs):
        copies = _plan_copies(plan, refs[:n], refs[n:2 * n], refs[2 * n], refs[2 * n + 1])
        for cp in copies:
            cp.wait_send()
        for cp in copies:
            cp.wait_recv()

    res = pl.pallas_call(
        body, name=name, out_shape=tuple(pltpu.HBM(a.shape, a.dtype) for a in list(srcs) + list(lands)),
        in_specs=[_HBM] * (2 * n) + [_SEM, _SEM, pl.BlockSpec(memory_space=pl.ANY)], out_specs=tuple([_HBM] * (2 * n)),
        input_output_aliases={i: i for i in range(2 * n)},
        compiler_params=pltpu.CompilerParams(has_side_effects=pltpu.SideEffectType.DATAFLOW_SIDE_EFFECTING),
    )(*srcs, *lands, send_sems, recv_sems, after)
    return list(res[:n]), list(res[n:])


def _plan_broadcast(src, land):
    x_i, y_i, c_i = lax.axis_index("x"), lax.axis_index("y"), lax.axis_index("c")
    me = 4 * x_i + 2 * y_i + c_i
    return [(src, land.at[me], (x_i ^ (k >> 2), y_i ^ ((k >> 1) & 1), c_i ^ (k & 1))) for k in range(1, N_DEV)]


def _plan_pairs(src, land):
    x_i, y_i, c_i = lax.axis_index("x"), lax.axis_index("y"), lax.axis_index("c")
    return [(src.at[2 * chip + (1 - c_i)], land.at[chip], (x_i, y_i, 1 - c_i)) for chip in range(4)]


def _plan_chips(src, land):
    x_i, y_i, c_i = lax.axis_index("x"), lax.axis_index("y"), lax.axis_index("c")
    plan = []
    for k in range(1, 4):
        px, py = x_i ^ (k >> 1), y_i ^ (k & 1)
        plan.append((src.at[2 * px + py], land.at[2 * x_i + y_i], (px, py, c_i)))
    return plan


def _pair_sum(g, recv, core, name):
    _, rows, cols = g.shape
    tr = ROW_TILE if rows % ROW_TILE == 0 else rows

    def body(core_ref, g_ref, r_ref, o_ref):
        o_ref[...] = (g_ref[...].astype(F32) + r_ref[...].astype(F32)).astype(o_ref.dtype)

    grid_spec = pltpu.PrefetchScalarGridSpec(
        num_scalar_prefetch=1, grid=(4, rows // tr),
        in_specs=[pl.BlockSpec((None, tr, cols), lambda k, i, core_ref: (2 * k + core_ref[0], i, 0)),
                  pl.BlockSpec((None, tr, cols), lambda k, i, core_ref: (k, i, 0))],
        out_specs=pl.BlockSpec((None, tr, cols), lambda k, i, core_ref: (k, i, 0)))
    return pl.pallas_call(body, name=name, grid_spec=grid_spec, out_shape=jax.ShapeDtypeStruct((4, rows, cols), g.dtype))(
        core, g, recv)


def _adam_math(g, w, m, v):
    m_new = ADAM_B1 * m + (1.0 - ADAM_B1) * g
    v_new = ADAM_B2 * v + (1.0 - ADAM_B2) * (g * g)
    m_hat = m_new / (1.0 - ADAM_B1 ** ADAM_STEP)
    v_hat = v_new / (1.0 - ADAM_B2 ** ADAM_STEP)
    return -ADAM_LR * (m_hat / (jnp.sqrt(v_hat) + ADAM_EPS) + ADAM_WD * w), m_new, v_new


def _adam(slots, w, m, v, name, own=None, own_idx=None):
    n_slot, rows, cols = slots.shape
    tr = ROW_TILE if rows % ROW_TILE == 0 else rows
    has_own = own is not None

    def body(*refs):
        refs = refs[1:] if has_own else refs
        if has_own:
            own_ref, refs = refs[0], refs[1:]
        s_ref, w_ref, m_ref, v_ref, g_ref, d_ref, mo_ref, vo_ref = refs
        g = own_ref[...].astype(F32) if has_own else s_ref[0].astype(F32)
        for k in range(0 if has_own else 1, n_slot):
            g = g + s_ref[k].astype(F32)
        g_ref[...] = g
        d_ref[...], mo_ref[...], vo_ref[...] = _adam_math(g, w_ref[...], m_ref[...], v_ref[...])

    spec = pl.BlockSpec((tr, cols), lambda i, *_: (i, 0))
    in_specs = [pl.BlockSpec((n_slot, tr, cols), lambda i, *_: (0, i, 0)), spec, spec, spec]
    if has_own:
        in_specs = [pl.BlockSpec((None, tr, cols), lambda i, idx: (idx[0], i, 0))] + in_specs
    grid_spec = pltpu.PrefetchScalarGridSpec(num_scalar_prefetch=1 if has_own else 0, grid=(rows // tr,), in_specs=in_specs,
                                             out_specs=[spec] * 4)
    ins = ([own_idx, own] if has_own else []) + [slots, w, m, v]
    return pl.pallas_call(
        body, name=name, grid_spec=grid_spec, out_shape=[jax.ShapeDtypeStruct((rows, cols), F32)] * 4,
        compiler_params=pltpu.CompilerParams(dimension_semantics=("parallel",)),
    )(*ins)


PACK_ROWS, PACK_W = 24, 1536
REPL_W = (("ssd_conv_b", 1536), ("ssd_dt_bias", 16), ("ssd_A_log", 16), ("ssd_D", 16), ("ssd_norm_w", 1024),
          ("mla_q_norm_w", 384), ("mla_kv_norm_w", 256), ("mla_out_norm_w", 1024), ("ln_mix_g", 1024),
          ("ln_mix_b", 1024), ("ln_ffn_g", 1024), ("ln_ffn_b", 1024))
LOSS_ROW = 4 + len(REPL_W)


def _pack_small(conv_w_grad, grads, loss, name="pack_small"):
    def body(*refs):
        cw_ref, g_refs, loss_ref, o_ref = refs[0], refs[1:1 + len(REPL_W)], refs[1 + len(REPL_W)], refs[-1]
        o_ref[...] = jnp.zeros_like(o_ref)
        o_ref[0:4, :] = cw_ref[...]
        for i, g_ref in enumerate(g_refs):
            o_ref[4 + i:5 + i, 0:g_ref.shape[1]] = g_ref[...]
        o_ref[LOSS_ROW:LOSS_ROW + 1, 0:LANE] = loss_ref[...]

    return pl.pallas_call(body, name=name, out_shape=jax.ShapeDtypeStruct((PACK_ROWS, PACK_W), F32))(conv_w_grad, *grads, loss)


def _adam_small(gathered, wmv, name="adam_small"):
    def body(*refs):
        s_ref = refs[0]
        in_refs = refs[1:1 + 3 * len(REPL_W)]
        cw_ref, loss_ref = refs[1 + 3 * len(REPL_W)], refs[2 + 3 * len(REPL_W)]
        out_refs = refs[3 + 3 * len(REPL_W):-1]
        tot = refs[-1]
        acc = s_ref[0]
        for k in range(1, N_DEV):
            acc = acc + s_ref[k]
        tot[...] = acc
        cw_ref[...] = tot[0:4, :]
        loss_ref[...] = tot[LOSS_ROW:LOSS_ROW + 1, 0:LANE]
        for i, (_, width) in enumerate(REPL_W):
            g = tot[4 + i:5 + i, 0:width]
            w_ref, m_ref, v_ref = in_refs[3 * i:3 * i + 3]
            g_ref, d_ref, mo_ref, vo_ref = out_refs[4 * i:4 * i + 4]
            g_ref[...] = g
            d_ref[...], mo_ref[...], vo_ref[...] = _adam_math(g, w_ref[...], m_ref[...], v_ref[...])

    flat_in = [a for triple in wmv for a in triple]
    out_shape = [jax.ShapeDtypeStruct((4, PACK_W), F32), jax.ShapeDtypeStruct((1, LANE), F32)]
    for _, width in REPL_W:
        out_shape += [jax.ShapeDtypeStruct((1, width), F32)] * 4
    res = pl.pallas_call(body, name=name, out_shape=out_shape, scratch_shapes=[pltpu.VMEM((PACK_ROWS, PACK_W), F32)])(
        gathered, *flat_in)
    return res[0], res[1], [res[2 + 4 * i:6 + 4 * i] for i in range(len(REPL_W))]


def _cols_full(g):
    return jnp.transpose(g, (1, 0, 2)).reshape(g.shape[1], -1)


def _cols_split(full):
    k_dim, n_dim = full.shape
    return jnp.transpose(full.reshape(k_dim, N_DEV, n_dim // N_DEV), (1, 0, 2))


def _win_pad(w):
    z = lambda n: jnp.zeros((w.shape[0], n), w.dtype)
    return jnp.concatenate([w[:, :2576], z(112), w[:, 2576:], z(96)], axis=1)


def _win_unpad(w):
    return jnp.concatenate([w[:, :2576], w[:, 2688:3360]], axis=1)


def _heads_split(w, a, b):
    k_dim = w.shape[0]
    w3 = w.reshape(k_dim, MLA_HEADS, a + b)
    return jnp.concatenate([w3[:, :, :a].reshape(k_dim, -1), w3[:, :, a:].reshape(k_dim, -1)], axis=1)


def _heads_merge(w, a, b):
    k_dim = w.shape[0]
    wa = w[:, :MLA_HEADS * a].reshape(k_dim, MLA_HEADS, a)
    wb = w[:, MLA_HEADS * a:].reshape(k_dim, MLA_HEADS, b)
    return jnp.concatenate([wa, wb], axis=2).reshape(k_dim, -1)


def _pad_lanes(v, width=LANE):
    return jnp.concatenate([v, jnp.zeros((v.shape[0], width - v.shape[1]), v.dtype)], axis=1)


def _local_step(x, p, positions, tgt, W, P, comm=None):
    comm = comm or {}
    zero_tok = jnp.zeros((8, LANE), F32)
    s_dim = x.shape[0]
    inv_freq = 1.0 / (ROPE_BASE ** (jnp.arange(0, MLA_ROPE, 2, dtype=F32) / MLA_ROPE))
    ang = positions.astype(F32)[:, None] * inv_freq
    cos, sin = jnp.cos(ang), jnp.sin(ang)
    cos32 = jnp.concatenate([cos, cos], axis=1)
    sin32 = jnp.concatenate([-sin, sin], axis=1)
    cos512, sin512 = jnp.tile(cos32, (1, 16)), jnp.tile(sin32, (1, 16))
    cos128, sin128 = jnp.tile(cos32, (1, 4)), jnp.tile(sin32, (1, 4))
    bias_p, alog_p = _pad_lanes(P["ssd_dt_bias"]), _pad_lanes(P["ssd_A_log"])
    d_x = jnp.repeat(P["ssd_D"], SSD_HEAD_DIM, axis=1)

    xb, pb = x.astype(BF16), p.astype(BF16)
    if "token0" in comm:
        xb = (x + comm["token0"][0, 0]).astype(BF16)
    proj = _mm(xb, W["w_in"], name="mm_in")
    z, xbc, dtr = proj[:, :1024], proj[:, 1024:2560], proj[:, 2560:2688]
    qc, kvc, kr = proj[:, 2688:3072], proj[:, 3072:3328], proj[:, 3328:3456]
    xbca = _conv_fwd(xbc, P["ssd_conv_w"], P["ssd_conv_b"])
    y, states = _ssd_fwd(xbca, dtr, bias_p, alog_p, d_x)
    (yssd,) = _rowwise(_gate_rms, [y, z], [P["ssd_norm_w"]], [(1024, BF16)], name="ssd_gate_norm")
    (qn,) = _rowwise(_rms, [qc], [P["mla_q_norm_w"]], [(MLA_Q_RANK, BF16)], name="q_norm")
    (kvn,) = _rowwise(_rms, [kvc], [P["mla_kv_norm_w"]], [(MLA_KV_RANK, BF16)], name="kv_norm")
    q = _mm(qn, W["mla_w_q_b"], name="mm_q")
    kv = _mm(kvn, W["mla_w_kv_b"], name="mm_kv")
    (qr,) = _rowwise(_rope_fwd_fn, [(q, 512, 2), cos512, sin512], [], [512], name="rope_q")
    (krt,) = _rowwise(lambda u, c, s: _spread4(_rope_fwd_fn(u, c, s)), [kr, cos128, sin128], [], [LANE], name="rope_k")
    att = _att_fwd(q, qr, kv, krt)
    (ymla,) = _rowwise(_rms, [att], [P["mla_out_norm_w"]], [(1024, BF16)], name="out_norm")
    ycat = jnp.concatenate([yssd, ymla], axis=1)
    if "late_weights" in comm:
        W = {**W, **comm["late_weights"](ycat)}
    mix = _mm(ycat, W["w_out"], name="mm_out")
    f_h1 = lambda xv, mv, g, b: _ln(ALPHA * xv + mv, g, b)
    h1, h1b = _rowwise(lambda *a: (f_h1(*a),) * 2, [x, mix], [P["ln_mix_g"], P["ln_mix_b"]], [1024, (1024, BF16)],
                       name="ln_mix")
    fb = D_FF // N_DEV
    hg = _mm(h1b, W["w_ffn_gate"], b_blk="n", o_blk="n", name="mm_gate")
    hu = _mm(h1b, W["w_ffn_up"], b_blk="n", o_blk="n", name="mm_up")
    pg = _mm(h1b, W["w_ple_gate"], name="mm_ple_gate")
    pp = _mm(pb, W["w_ple_proj"], name="mm_ple")
    hg2, hu2 = hg.reshape(N_DEV * s_dim, fb), hu.reshape(N_DEV * s_dim, fb)
    (act,) = _rowwise(lambda g, u: _silu(g) * u, [hg2, hu2], [], [(fb, BF16)], name="swiglu", tr=512)
    act3 = act.reshape(N_DEV, s_dim, fb)
    ffn = _mm(act3, W["w_ffn_down"], a_blk="k", b_blk="k", name="mm_down")

    f_h2 = lambda hv, fv, pg, ppv, g, b: _ln(ALPHA * hv + fv + _sigmoid(pg) * ppv, g, b)

    def final_fn(hv, fv, pg, ppv, tv, g, b):
        h2, pull = jax.vjp(f_h2, hv, fv, pg, ppv, g, b)
        diff = h2 - tv
        loss = 0.5 * jnp.sum(jnp.mean(diff * diff, axis=-1, keepdims=True), axis=0, keepdims=True)
        d_h, d_f, d_pg, d_pp, d_g, d_b = pull(diff * (1.0 / D_MODEL))
        return d_h, d_f, d_pg, d_pp, d_g, d_b, jnp.broadcast_to(loss, (1, LANE))

    dh1_a, dffn, dpg, dpp, g_ffn_g, g_ffn_b, loss = _rowwise(
        final_fn, [h1, ffn, pg, pp, tgt], [P["ln_ffn_g"], P["ln_ffn_b"]], [1024] + [(1024, BF16)] * 3,
        [1024, 1024, LANE], name="final")

    G = {}
    dact = _mm(dffn, W["w_ffn_down"], tb=True, b_blk="n", o_blk="n", name="mm_down_dx")
    G["w_ffn_down"] = _mm(act3, dffn, ta=True, a_blk="m", o_blk="m", out_dtype=GRAD_DT, name="mm_down_dw")

    def swiglu_bwd(g, u, d):
        sg = _sigmoid(g)
        return d * u * (sg * (1.0 + g * (1.0 - sg))), d * (g * sg)

    dg, du = _rowwise(swiglu_bwd, [hg2, hu2, dact.reshape(N_DEV * s_dim, fb)], [], [(fb, BF16)] * 2, name="swiglu_bwd",
                      tr=512)
    dg3, du3 = dg.reshape(N_DEV, s_dim, fb), du.reshape(N_DEV, s_dim, fb)
    dh1 = _mm(dg3, W["w_ffn_gate"], tb=True, a_blk="k", b_blk="k", add=dh1_a, name="mm_gate_dx")
    dh1 = _mm(du3, W["w_ffn_up"], tb=True, a_blk="k", b_blk="k", add=dh1, name="mm_up_dx")
    dh1 = _mm(dpg, W["w_ple_gate"], tb=True, add=dh1, name="mm_ple_gate_dx")
    G["w_ffn_gate"] = _mm(h1b, dg3, ta=True, b_blk="n", o_blk="n", out_dtype=GRAD_DT, name="mm_gate_dw")
    G["w_ffn_up"] = _mm(h1b, du3, ta=True, b_blk="n", o_blk="n", out_dtype=GRAD_DT, name="mm_up_dw")
    G["w_ple_gate"] = _mm(h1b, dpg, ta=True, out_dtype=GRAD_DT, name="mm_ple_gate_dw")
    G["w_ple_proj"] = _mm(pb, dpp, ta=True, out_dtype=GRAD_DT, name="mm_ple_dw")
    tok1 = comm["ffn_grads"](G) if "ffn_grads" in comm else zero_tok
    dx_a, dmix, g_mix_g, g_mix_b = _rowwise(
        lambda xv, mv, dv, g, b, t: _vjp_rows(f_h1)(xv, mv, g, b, dv + jnp.min(t)), [x, mix, dh1],
        [P["ln_mix_g"], P["ln_mix_b"], tok1], [1024, (1024, BF16)], [1024, 1024], name="ln_mix_bwd")
    dycat = _mm(dmix, W["w_out"], tb=True, name="mm_out_dx")
    G["w_out"] = _mm(ycat, dmix, ta=True, out_dtype=GRAD_DT, name="mm_out_dw")

    datt, g_out_norm = _rowwise(lambda a, dv, w: _vjp_rows(_rms)(a, w, dv), [att, (dycat, 1024, 1)],
                                [P["mla_out_norm_w"]], [1024], [1024], name="out_norm_bwd")
    dqn_nope, dqr, dkn, dv, dkrt = _att_bwd(q, qr, kv, krt, att, datt)
    dkv = jnp.concatenate([dkn, dv], axis=1)
    tok2 = comm["mid"](dqn_nope) if "mid" in comm else zero_tok
    (dq_rope,) = _rowwise(lambda d0, d1, c, s, t: _rope_bwd_fn(d0 + d1 + jnp.min(t), c, s),
                          [(dqr, 512, 0), (dqr, 512, 1), cos512, sin512], [tok2], [(512, BF16)], name="rope_q_bwd")

    def rope_k_bwd(*a):
        d = _spread4(functools.reduce(lambda u, w: u + w, a[:-3]))
        lane = lax.broadcasted_iota(jnp.int32, d.shape, 1)
        return _rope_bwd_fn(jnp.where(lane < MLA_ROPE, d, 0.0), a[-3], a[-2])

    (dkr,) = _rowwise(rope_k_bwd, [(dkrt, LANE, k) for k in range(MLA_HEADS // 2)] + [cos128, sin128], [tok2],
                      [(LANE, BF16)], name="rope_k_bwd")
    dq = jnp.concatenate([dqn_nope, dq_rope], axis=1)
    dqn = _mm(dq, W["mla_w_q_b"], tb=True, name="mm_q_dx")
    G["mla_w_q_b"] = _mm(qn, dq, ta=True, out_dtype=GRAD_DT, name="mm_q_dw")
    dkvn = _mm(dkv, W["mla_w_kv_b"], tb=True, name="mm_kv_dx")
    G["mla_w_kv_b"] = _mm(kvn, dkv, ta=True, out_dtype=GRAD_DT, name="mm_kv_dw")
    dqc, g_q_norm = _rowwise(lambda a, dv, w: _vjp_rows(_rms)(a, w, dv), [qc, dqn], [P["mla_q_norm_w"]],
                             [(MLA_Q_RANK, BF16)], [MLA_Q_RANK], name="q_norm_bwd")
    dkvc, g_kv_norm = _rowwise(lambda a, dv, w: _vjp_rows(_rms)(a, w, dv), [kvc, dkvn], [P["mla_kv_norm_w"]],
                               [(MLA_KV_RANK, BF16)], [MLA_KV_RANK], name="kv_norm_bwd")

    dy, dz, g_ssd_norm = _rowwise(lambda yv, zv, dv, w, t: _vjp_rows(_gate_rms)(yv, zv, w, dv + jnp.min(t)),
                                  [y, z, (dycat, 1024, 0)], [P["ssd_norm_w"], tok2], [1024, (1024, BF16)], [1024],
                                  name="ssd_gate_norm_bwd")
    dxbca, ddtr, g_dt_bias, g_alog, g_d = _ssd_bwd(xbca, dtr, bias_p, alog_p, d_x, states, dy)
    da, g_conv_w, g_conv_b = _conv_bwd_pre(xbc, P["ssd_conv_w"], P["ssd_conv_b"], dxbca)
    dxbc = _conv_bwd_in(da, P["ssd_conv_w"])

    dproj = jnp.concatenate([dz, dxbc, ddtr, dqc, dkvc, dkr], axis=1)
    grad_x = _mm(dproj, W["w_in"], tb=True, add=dx_a, name="mm_in_dx")
    G["w_in"] = _mm(xb, dproj, ta=True, out_dtype=GRAD_DT, name="mm_in_dw")

    small = {
        "ssd_conv_b": g_conv_b, "ssd_dt_bias": g_dt_bias, "ssd_A_log": g_alog, "ssd_D": g_d, "ssd_norm_w": g_ssd_norm,
        "mla_q_norm_w": g_q_norm, "mla_kv_norm_w": g_kv_norm, "mla_out_norm_w": g_out_norm, "ln_mix_g": g_mix_g,
        "ln_mix_b": g_mix_b, "ln_ffn_g": g_ffn_g, "ln_ffn_b": g_ffn_b,
    }
    return grad_x, G, _pack_small(g_conv_w, [small[n] for n, _ in REPL_W], loss)


def kernel(x, p, positions, w_in, ssd_conv_w, ssd_conv_b, ssd_dt_bias, ssd_A_log, ssd_D, ssd_norm_w, mla_q_norm_w, mla_w_q_b, mla_kv_norm_w, mla_w_kv_b, mla_out_norm_w, w_out, ln_mix_g, ln_mix_b, w_ffn_gate, w_ffn_up, w_ffn_down, w_ple_gate, w_ple_proj, ln_ffn_g, ln_ffn_b, loss_target, m_w_in, m_ssd_conv_w, m_ssd_conv_b, m_ssd_dt_bias, m_ssd_A_log, m_ssd_D, m_ssd_norm_w, m_mla_q_norm_w, m_mla_w_q_b, m_mla_kv_norm_w, m_mla_w_kv_b, m_mla_out_norm_w, m_w_out, m_ln_mix_g, m_ln_mix_b, m_w_ffn_gate, m_w_ffn_up, m_w_ffn_down, m_w_ple_gate, m_w_ple_proj, m_ln_ffn_g, m_ln_ffn_b, v_w_in, v_ssd_conv_w, v_ssd_conv_b, v_ssd_dt_bias, v_ssd_A_log, v_ssd_D, v_ssd_norm_w, v_mla_q_norm_w, v_mla_w_q_b, v_mla_kv_norm_w, v_mla_w_kv_b, v_mla_out_norm_w, v_w_out, v_ln_mix_g, v_ln_mix_b, v_w_ffn_gate, v_w_ffn_up, v_w_ffn_down, v_w_ple_gate, v_w_ple_proj, v_ln_ffn_g, v_ln_ffn_b):
    args = dict(locals())
    core = lax.axis_index("c")
    me = 4 * lax.axis_index("x") + 2 * lax.axis_index("y") + core

    conv_sh = ssd_conv_w[0]
    conv_hi = conv_sh.astype(BF16)
    conv_lo = (conv_sh - conv_hi.astype(F32)).astype(BF16)
    shards = {n: args[n][0].astype(BF16) for n in BIG}
    rows_full = lambda g: g.reshape(-1, g.shape[2])
    core_arr = core.astype(jnp.int32).reshape(1)
    chip_arr = (2 * lax.axis_index("x") + lax.axis_index("y")).astype(jnp.int32).reshape(1)

    early = _gather_many([shards[n] for n in EARLY] + [jnp.concatenate([conv_hi, conv_lo], axis=0)], "gather_early")
    gw = dict(zip(EARLY, early[:-1]))
    conv_g = early[-1].astype(F32)
    W = {
        "w_in": _win_pad(_cols_full(gw["w_in"])),
        "mla_w_q_b": _heads_split(_cols_full(gw["mla_w_q_b"]), MLA_NOPE, MLA_ROPE),
        "mla_w_kv_b": _heads_split(_cols_full(gw["mla_w_kv_b"]), MLA_NOPE, MLA_V),
    }
    P = {n: args[n] for n, _ in REPL_W}
    P["ssd_conv_w"] = _cols_full(conv_g[:, :4] + conv_g[:, 4:])

    lands = [lax.dynamic_update_slice(jnp.zeros((N_DEV,) + shards[n].shape, BF16), shards[n][None], (me, 0, 0)) for n in LATE]
    late_sems = _split_start([shards[n] for n in LATE], lands, _plan_broadcast, N_DEV - 1, "gather_late_start")

    def late_weights(after):
        _, got = _split_wait(*late_sems[:4], after, _plan_broadcast, "gather_late_wait")
        lw = dict(zip(LATE, got))
        return {"w_out": rows_full(lw["w_out"]), "w_ple_gate": rows_full(lw["w_ple_gate"]),
                "w_ple_proj": _cols_full(lw["w_ple_proj"]), "w_ffn_gate": lw["w_ffn_gate"], "w_ffn_up": lw["w_ffn_up"],
                "w_ffn_down": lw["w_ffn_down"]}

    def to_blocks(n, g):
        if n in OWNER_BLOCKED:
            return g
        if n in ROW_SHARDED:
            return g.reshape(N_DEV, -1, g.shape[1])
        if n == "w_in":
            g = _win_unpad(g)
        elif n == "mla_w_q_b":
            g = _heads_merge(g, MLA_NOPE, MLA_ROPE)
        elif n == "mla_w_kv_b":
            g = _heads_merge(g, MLA_NOPE, MLA_V)
        return _cols_split(g)

    flight = {}

    def ffn_grads(G):
        gl = [to_blocks(n, G[n]) for n in LATE_GRADS]
        flight["pairs"] = _split_start(gl, [lax.empty((4,) + g.shape[1:], g.dtype) for g in gl], _plan_pairs, 4, "pairs_start")
        return flight["pairs"][4]

    def mid(after):
        gl, from_sibling = _split_wait(*flight["pairs"][:4], after, _plan_pairs, "pairs_wait")
        sums = [_pair_sum(g, r, core_arr, "pair_sum_" + n) for n, g, r in zip(LATE_GRADS, gl, from_sibling)]
        flight["chips"] = _split_start(sums, [jnp.zeros(s.shape, s.dtype) for s in sums], _plan_chips, 3, "chips_start")
        return flight["chips"][4]

    grad_x, G, packed = _local_step(x[0], p[0, 0], positions[0], loss_target[0], W, P,
                                    comm={"token0": late_sems[4], "late_weights": late_weights, "ffn_grads": ffn_grads, "mid": mid})

    wmv = lambda n: (args[n][0], args["m_" + n][0], args["v_" + n][0])
    sums, recv = _split_wait(*flight["chips"][:4], grad_x, _plan_chips, "chips_wait")
    big_out = {n: _adam(r, *wmv(n), "adam_" + n, own=s, own_idx=chip_arr) for n, s, r in zip(LATE_GRADS, sums, recv)}

    glist = [to_blocks(n, G[n]) for n in LAST_GRADS]
    from_sibling = _pair_exchange(glist, "exchange_pairs")
    sums = [_pair_sum(g, r, core_arr, "pair_sum_" + n) for n, g, r in zip(LAST_GRADS, glist, from_sibling)]
    recv = _chip_exchange(sums, "exchange_chips")
    big_out.update({n: _adam(r, *wmv(n), "adam_" + n) for n, r in zip(LAST_GRADS, recv)})

    small_all = _all_gather(packed, "gather_small")
    conv_sum, loss_row, small_out = _adam_small(small_all, [(args[n], args["m_" + n], args["v_" + n]) for n, _ in REPL_W])
    conv_grad = lax.dynamic_slice_in_dim(conv_sum, me * 192, 192, axis=1)
    conv_out = _adam(conv_grad[None], conv_sh, m_ssd_conv_w[0], v_ssd_conv_w[0], "adam_conv")
    small_map = {n: small_out[i] for i, (n, _) in enumerate(REPL_W)}

    def outputs(idx):
        res = []
        for n in WEIGHT_ORDER:
            if n == "ssd_conv_w":
                res.append(conv_out[idx][None])
            elif n in big_out:
                res.append(big_out[n][idx][None])
            else:
                res.append(small_map[n][idx])
        return res

    return (loss_row[0, 0], grad_x[None], *outputs(0), *outputs(1), *outputs(2), *outputs(3))
```

```python
import functools
import math

import numpy as np
import jax
import jax.numpy as jnp
from jax import lax
from jax.experimental import pallas as pl
from jax.experimental.pallas import tpu as pltpu

F32 = jnp.float32
BF16 = jnp.bfloat16
HI = lax.Precision.HIGHEST

N_DEV = 8
D_MODEL = 1024
PLE_DIM = 256
SSD_HEADS = 16
SSD_HEAD_DIM = 64
SSD_INNER = 1024
SSD_STATE = 128
SSD_XBC = 1536
SSD_CHUNK = 128
MLA_HEADS = 16
MLA_Q_RANK = 384
MLA_KV_RANK = 256
MLA_NOPE = 64
MLA_ROPE = 32
MLA_V = 64
ROPE_BASE = 10000.0
D_FF = 2816
IN_WIDTH = 3248
IN_PAD = 3456
ALPHA = 2.0 ** 0.25
EPS = 1e-6
LN_EPS = 1e-5
ATT_SCALE = 1.0 / math.sqrt(MLA_NOPE + MLA_ROPE)
ADAM_LR, ADAM_B1, ADAM_B2, ADAM_EPS, ADAM_WD, ADAM_STEP = 0.001, 0.9, 0.999, 1e-08, 0.01, 10

LANE = 128
MM_TM, MM_TN, MM_TK = 1024, 512, 2048
ROW_TILE = 256
ATT_TQ = 256

GRAD_DT = BF16

BIG = ("w_in", "mla_w_q_b", "mla_w_kv_b", "w_out", "w_ffn_gate", "w_ffn_up", "w_ffn_down", "w_ple_gate", "w_ple_proj")
EARLY = ("w_in", "mla_w_q_b", "mla_w_kv_b")
LATE = ("w_out", "w_ffn_gate", "w_ffn_up", "w_ffn_down", "w_ple_gate", "w_ple_proj")
LATE_GRADS = ("w_ffn_gate", "w_ffn_up", "w_ffn_down", "w_ple_gate", "w_ple_proj", "w_out")
LAST_GRADS = ("w_in", "mla_w_q_b", "mla_w_kv_b")
ROW_SHARDED = ("w_out", "w_ffn_down", "w_ple_gate")
OWNER_BLOCKED = ("w_ffn_gate", "w_ffn_up", "w_ffn_down")
WEIGHT_ORDER = ("w_in", "ssd_conv_w", "ssd_conv_b", "ssd_dt_bias", "ssd_A_log", "ssd_D", "ssd_norm_w", "mla_q_norm_w",
                "mla_w_q_b", "mla_kv_norm_w", "mla_w_kv_b", "mla_out_norm_w", "w_out", "ln_mix_g", "ln_mix_b",
                "w_ffn_gate", "w_ffn_up", "w_ffn_down", "w_ple_gate", "w_ple_proj", "ln_ffn_g", "ln_ffn_b")


def _tile(dim, cap):
    if dim <= cap:
        return dim
    t = (cap // LANE) * LANE
    while dim % t:
        t -= LANE
    return t


def _dot(a, b, dims=(((1,), (0,)), ((), ())), precision=None):
    return lax.dot_general(a, b, dims, preferred_element_type=F32, precision=precision)


_NT = (((1,), (1,)), ((), ()))
_TN = (((0,), (0,)), ((), ()))


def _mm(a, b, *, ta=False, tb=False, a_blk=None, b_blk=None, o_blk=None, add=None, out_dtype=F32, name):
    ka, ma = a.shape[-2:] if ta else a.shape[-2:][::-1]
    nb, kb = b.shape[-2:] if tb else b.shape[-2:][::-1]
    m_dim = N_DEV * ma if a_blk == "m" else ma
    k_dim = N_DEV * ka if a_blk == "k" else ka
    n_dim = N_DEV * nb if b_blk == "n" else nb
    assert k_dim == (N_DEV * kb if b_blk == "k" else kb)
    tm = ma if a_blk == "m" else (m_dim // N_DEV if o_blk == "m" else _tile(m_dim, MM_TM))
    tn = nb if b_blk == "n" else (n_dim // N_DEV if o_blk == "n" else _tile(n_dim, MM_TN))
    tk = ka if a_blk == "k" else (kb if b_blk == "k" else _tile(k_dim, MM_TK))
    nk = k_dim // tk
    dims = (((0 if ta else 1,), (1 if tb else 0,)), ((), ()))
    has_add = add is not None

    def spec(tile, idx, lead):
        if lead is None:
            return pl.BlockSpec(tile, idx)
        return pl.BlockSpec((None,) + tile, lambda i, j, k: (lead(i, j, k),) + idx(i, j, k))

    def a_idx(i, j, k):
        ii, kk = (0 if a_blk == "m" else i), (0 if a_blk == "k" else k)
        return (kk, ii) if ta else (ii, kk)

    def b_idx(i, j, k):
        jj, kk = (0 if b_blk == "n" else j), (0 if b_blk == "k" else k)
        return (jj, kk) if tb else (kk, jj)

    def o_idx(i, j, k):
        return (0 if o_blk == "m" else i, 0 if o_blk == "n" else j)

    pick = {"m": lambda i, j, k: i, "n": lambda i, j, k: j, "k": lambda i, j, k: k, None: None}
    a_spec = spec((tk, tm) if ta else (tm, tk), a_idx, pick[a_blk])
    b_spec = spec((tn, tk) if tb else (tk, tn), b_idx, pick[b_blk])
    o_spec = spec((tm, tn), o_idx, pick[o_blk])

    def body(*refs):
        if has_add:
            a_ref, b_ref, add_ref, o_ref = refs[:4]
        else:
            a_ref, b_ref, o_ref = refs[:3]
        part = _dot(a_ref[...].astype(BF16), b_ref[...].astype(BF16), dims)
        if nk == 1:
            o_ref[...] = ((part + add_ref[...]) if has_add else part).astype(o_ref.dtype)
            return
        acc = refs[-1]
        k = pl.program_id(2)

        @pl.when(k == 0)
        def _():
            acc[...] = (part + add_ref[...]) if has_add else part

        @pl.when(k > 0)
        def _():
            acc[...] += part

        @pl.when(k == nk - 1)
        def _():
            o_ref[...] = acc[...].astype(o_ref.dtype)

    if o_blk == "m":
        out_shape = (N_DEV, tm, n_dim)
    elif o_blk == "n":
        out_shape = (N_DEV, m_dim, tn)
    else:
        out_shape = (m_dim, n_dim)
    ins = [a, b] + ([add] if has_add else [])
    specs = [a_spec, b_spec] + ([pl.BlockSpec((tm, tn), lambda i, j, k: (i, j))] if has_add else [])
    return pl.pallas_call(
        body, name=name, grid=(m_dim // tm, n_dim // tn, nk), in_specs=specs, out_specs=o_spec,
        out_shape=jax.ShapeDtypeStruct(out_shape, out_dtype),
        scratch_shapes=[pltpu.VMEM((tm, tn), F32)] if nk > 1 else [],
        compiler_params=pltpu.CompilerParams(dimension_semantics=("parallel", "parallel", "arbitrary")),
    )(*ins)


def _rowwise(fn, rows, consts, out_widths, acc_widths=(), *, name, tr=ROW_TILE):
    row_arrays, row_specs = [], []
    first_arr = rows[0][0] if isinstance(rows[0], tuple) else rows[0]
    s_dim = first_arr.shape[-2]
    tr = min(tr, s_dim)
    for r in rows:
        arr, width, cb = r if isinstance(r, tuple) else (r, r.shape[-1], 0)
        row_arrays.append(arr)
        if arr.ndim == 3:
            row_specs.append(pl.BlockSpec((None, tr, width), functools.partial(lambda i, k: (k, i, 0), k=cb)))
        else:
            row_specs.append(pl.BlockSpec((tr, width), functools.partial(lambda i, cb: (i, cb), cb=cb)))
    const_specs = [pl.BlockSpec(c.shape, lambda i: (0, 0)) for c in consts]
    nr, nc, no, na = len(rows), len(consts), len(out_widths), len(acc_widths)

    def body(*refs):
        ins = [r[...] for r in refs[:nr + nc]]
        res = fn(*ins)
        if not isinstance(res, (tuple, list)):
            res = (res,)
        out_refs = refs[nr + nc:nr + nc + no]
        acc_refs = refs[nr + nc + no:]
        for o_ref, val in zip(out_refs, res[:no]):
            o_ref[...] = val.astype(o_ref.dtype)
        first = pl.program_id(0) == 0
        for a_ref, val in zip(acc_refs, res[no:]):
            @pl.when(first)
            def _(a_ref=a_ref, val=val):
                a_ref[...] = val

            @pl.when(jnp.logical_not(first))
            def _(a_ref=a_ref, val=val):
                a_ref[...] += val

    outs = [w if isinstance(w, tuple) else (w, F32) for w in out_widths]
    out_shape = [jax.ShapeDtypeStruct((s_dim, w), dt) for w, dt in outs]
    out_shape += [jax.ShapeDtypeStruct((1, w), F32) for w in acc_widths]
    out_specs = [pl.BlockSpec((tr, w), lambda i: (i, 0)) for w, _ in outs]
    out_specs += [pl.BlockSpec((1, w), lambda i: (0, 0)) for w in acc_widths]
    res = pl.pallas_call(
        body, name=name, grid=(s_dim // tr,), in_specs=row_specs + const_specs, out_specs=out_specs, out_shape=out_shape,
        compiler_params=pltpu.CompilerParams(dimension_semantics=("arbitrary",)),
    )(*row_arrays, *consts)
    return res


def _colsum(v):
    return jnp.sum(v, axis=0, keepdims=True)


def _rms(u, g):
    return u * lax.rsqrt(jnp.mean(u * u, axis=-1, keepdims=True) + EPS) * g


def _ln(u, g, b):
    mu = jnp.mean(u, axis=-1, keepdims=True)
    d = u - mu
    var = jnp.mean(d * d, axis=-1, keepdims=True)
    return d * lax.rsqrt(var + LN_EPS) * g + b


def _sigmoid(v):
    return 1.0 / (1.0 + jnp.exp(-v))


def _silu(v):
    return v * _sigmoid(v)


def _softplus(v):
    y = jnp.exp(-jnp.abs(v))
    w = 1.0 + y
    log1p = jnp.where(w == 1.0, y, jnp.log(w) * y / jnp.where(w == 1.0, 1.0, w - 1.0))
    return jnp.maximum(v, 0.0) + log1p


def _gate_rms(y, z, w):
    return _rms(y * _silu(z), w)


def _vjp_rows(f):
    def fn(*args):
        prim, ct = args[:-1], args[-1]
        _, pull = jax.vjp(f, *prim)
        return pull(ct)
    return fn


def _conv_pre(cur, prev, w, b, first):
    row = lax.broadcasted_iota(jnp.int32, cur.shape, 0)
    acc = cur * w[3:4, :] + b
    for j in (1, 2, 3):
        tail = jnp.where(first, 0.0, pltpu.roll(prev, j, 0))
        acc = acc + jnp.where(row >= j, pltpu.roll(cur, j, 0), tail) * w[3 - j:4 - j, :]
    return acc


def _conv_fwd(u, ucb, w, b, name="conv_fwd"):
    s_dim, width = u.shape[0], w.shape[1]
    tr = min(ROW_TILE, s_dim)

    def body(cur_ref, prev_ref, w_ref, b_ref, o_ref):
        pre = _conv_pre(cur_ref[...], prev_ref[...], w_ref, b_ref[...], pl.program_id(0) == 0)
        o_ref[...] = _silu(pre)

    return pl.pallas_call(
        body, name=name, grid=(s_dim // tr,),
        in_specs=[pl.BlockSpec((tr, width), lambda i: (i, ucb)),
                  pl.BlockSpec((tr, width), lambda i: (jnp.maximum(i - 1, 0), ucb)),
                  pl.BlockSpec(w.shape, lambda i: (0, 0)), pl.BlockSpec(b.shape, lambda i: (0, 0))],
        out_specs=pl.BlockSpec((tr, width), lambda i: (i, 0)), out_shape=jax.ShapeDtypeStruct((s_dim, width), F32),
        compiler_params=pltpu.CompilerParams(dimension_semantics=("arbitrary",)),
    )(u, u, w, b)


def _conv_bwd_pre(u, ucb, w, b, dact, name="conv_bwd_pre"):
    s_dim, width = u.shape[0], w.shape[1]
    tr = min(ROW_TILE, s_dim)

    def body(cur_ref, prev_ref, w_ref, b_ref, d_ref, da_ref, dw_ref, db_ref):
        first = pl.program_id(0) == 0
        cur, prev = cur_ref[...], prev_ref[...]
        pre = _conv_pre(cur, prev, w_ref, b_ref[...], first)
        sg = _sigmoid(pre)
        da = d_ref[...] * (sg * (1.0 + pre * (1.0 - sg)))
        da_ref[...] = da
        row = lax.broadcasted_iota(jnp.int32, cur.shape, 0)

        @pl.when(first)
        def _():
            dw_ref[...] = jnp.zeros_like(dw_ref)
            db_ref[...] = jnp.zeros_like(db_ref)

        db_ref[...] += _colsum(da)
        dw_ref[3:4, :] += _colsum(da * cur)
        for j in (1, 2, 3):
            tail = jnp.where(first, 0.0, pltpu.roll(prev, j, 0))
            sh = jnp.where(row >= j, pltpu.roll(cur, j, 0), tail)
            dw_ref[3 - j:4 - j, :] += _colsum(da * sh)

    return pl.pallas_call(
        body, name=name, grid=(s_dim // tr,),
        in_specs=[pl.BlockSpec((tr, width), lambda i: (i, ucb)),
                  pl.BlockSpec((tr, width), lambda i: (jnp.maximum(i - 1, 0), ucb)),
                  pl.BlockSpec(w.shape, lambda i: (0, 0)), pl.BlockSpec(b.shape, lambda i: (0, 0)),
                  pl.BlockSpec((tr, width), lambda i: (i, 0))],
        out_specs=[pl.BlockSpec((tr, width), lambda i: (i, 0)), pl.BlockSpec(w.shape, lambda i: (0, 0)),
                   pl.BlockSpec(b.shape, lambda i: (0, 0))],
        out_shape=[jax.ShapeDtypeStruct((s_dim, width), F32), jax.ShapeDtypeStruct(w.shape, F32),
                   jax.ShapeDtypeStruct(b.shape, F32)],
        compiler_params=pltpu.CompilerParams(dimension_semantics=("arbitrary",)),
    )(u, u, w, b, dact)


def _conv_bwd_in(da, w, name="conv_bwd_in"):
    s_dim, width = da.shape
    tr = min(ROW_TILE, s_dim)
    n = s_dim // tr

    def body(cur_ref, nxt_ref, w_ref, o_ref):
        last = pl.program_id(0) == n - 1
        cur, nxt = cur_ref[...], nxt_ref[...]
        row = lax.broadcasted_iota(jnp.int32, cur.shape, 0)
        acc = cur * w_ref[3:4, :]
        for j in (1, 2, 3):
            head = jnp.where(last, 0.0, pltpu.roll(nxt, tr - j, 0))
            acc = acc + jnp.where(row < tr - j, pltpu.roll(cur, tr - j, 0), head) * w_ref[3 - j:4 - j, :]
        o_ref[...] = acc.astype(o_ref.dtype)

    return pl.pallas_call(
        body, name=name, grid=(n,),
        in_specs=[pl.BlockSpec((tr, width), lambda i: (i, 0)), pl.BlockSpec((tr, width), lambda i: (jnp.minimum(i + 1, n - 1), 0)),
                  pl.BlockSpec(w.shape, lambda i: (0, 0))],
        out_specs=pl.BlockSpec((tr, width), lambda i: (i, 0)), out_shape=jax.ShapeDtypeStruct((s_dim, width), BF16),
        compiler_params=pltpu.CompilerParams(dimension_semantics=("arbitrary",)),
    )(da, da, w)


def _sel_dot(a, sel, pieces, dims=(((1,), (0,)), ((), ())), sel_left=False):
    sel = sel.astype(BF16)
    acc, rest = None, a
    for _ in range(pieces):
        piece = rest.astype(BF16)
        rest = rest - piece.astype(F32)
        part = _dot(sel, piece, dims) if sel_left else _dot(piece, sel, dims)
        acc = part if acc is None else acc + part
    return acc


def _ssd_consts():
    L = SSD_CHUNK
    tri = np.tril(np.ones((L, L), np.float32))
    expand = np.zeros((LANE, SSD_INNER), np.float32)
    expand128 = np.zeros((LANE, SSD_HEADS * LANE), np.float32)
    for h in range(SSD_HEADS):
        expand[h, h * SSD_HEAD_DIM:(h + 1) * SSD_HEAD_DIM] = 1.0
        expand128[h, h * LANE:(h + 1) * LANE] = 1.0
    return jnp.asarray(tri), jnp.asarray(expand), jnp.asarray(expand128), jnp.asarray(expand.T.copy())


def _ssd_prep(dt_ref, bias_ref, alog_ref, tri_ref, exp_ref, exp128_ref, cs_s, cst_s, ex_s, csx_s):
    L = SSD_CHUNK
    dt = _softplus(dt_ref[...] + bias_ref[...])
    a = -jnp.exp(alog_ref[...])
    cs = _sel_dot(dt * a, tri_ref[...], 3, sel_left=True)
    cs_s[...] = cs
    cst_s[...] = cs.T
    last = cs_s[L - 1:L, :]
    expand = exp_ref[...]
    ex_s[...] = _sel_dot(jnp.exp(cs), expand, 2)
    f_x = _sel_dot(jnp.exp(last - cs), expand, 2)
    dt_x = _sel_dot(dt, expand, 2)
    csx_s[...] = _sel_dot(cs, exp128_ref[...], 3)
    t_x = ex_s[L - 1:L, :]
    return dt, a, dt_x, f_x, t_x


def _decay_matrix(csx_s, cst_s, h, tril):
    seg = csx_s[:, h * LANE:(h + 1) * LANE] - cst_s[h:h + 1, :]
    return jnp.exp(jnp.where(tril, seg, -jnp.inf))


def _ssd_fwd(xbca, dtr, dtcb, bias, alog, d_x, name="ssd_fwd"):
    s_dim = xbca.shape[0]
    L = SSD_CHUNK
    nc = s_dim // L
    tri, expand, expand128, _ = _ssd_consts()

    def body(xs_ref, b_ref, c_ref, dt_ref, bias_ref, alog_ref, dx_ref, tri_ref, exp_ref, exp128_ref,
             y_ref, st_ref, st_s, cs_s, cst_s, ex_s, csx_s):
        @pl.when(pl.program_id(0) == 0)
        def _():
            st_s[...] = jnp.zeros_like(st_s)

        dt, a, dt_x, f_x, t_x = _ssd_prep(dt_ref, bias_ref, alog_ref, tri_ref, exp_ref, exp128_ref, cs_s, cst_s, ex_s, csx_s)
        st_ref[0] = st_s[...]
        row = lax.broadcasted_iota(jnp.int32, (L, L), 0)
        col = lax.broadcasted_iota(jnp.int32, (L, L), 1)
        tril = row >= col
        low = col < SSD_HEAD_DIM
        for g in range(2):
            bg = b_ref[:, g * LANE:(g + 1) * LANE]
            cg = c_ref[:, g * LANE:(g + 1) * LANE].astype(BF16)
            gmat = _dot(cg, bg.astype(BF16), _NT)
            bgt = bg.T.astype(BF16)
            for jj in range(4):
                j = 4 * g + jj
                sl = slice(j * LANE, (j + 1) * LANE)
                xp = xs_ref[:, sl]
                x_dt = xp * dt_x[:, sl]
                xb = x_dt.astype(BF16)
                yd = []
                for e in range(2):
                    lm = _decay_matrix(csx_s, cst_s, 2 * j + e, tril)
                    yd.append(_dot((gmat * lm).astype(BF16), xb))
                stp = st_s[j]
                z = _dot(cg, stp.astype(BF16))
                y_ref[:, sl] = jnp.where(low, yd[0], yd[1]) + ex_s[:, sl] * z + dx_ref[:, sl] * xp
                xf = (x_dt * f_x[:, sl]).astype(BF16)
                st_s[j] = t_x[:, sl] * stp + _dot(bgt, xf)

    const = lambda shape: pl.BlockSpec(shape, lambda c: tuple(0 for _ in shape))
    return pl.pallas_call(
        body, name=name, grid=(nc,),
        in_specs=[pl.BlockSpec((L, 1024), lambda c: (c, 0)), pl.BlockSpec((L, 256), lambda c: (c, 4)),
                  pl.BlockSpec((L, 256), lambda c: (c, 5)), pl.BlockSpec((L, LANE), lambda c: (c, dtcb)),
                  const((1, LANE)), const((1, LANE)), const((1, 1024)), const((L, L)), const((LANE, 1024)),
                  const((LANE, 2048))],
        out_specs=[pl.BlockSpec((L, 1024), lambda c: (c, 0)), pl.BlockSpec((1, 8, LANE, LANE), lambda c: (c, 0, 0, 0))],
        out_shape=[jax.ShapeDtypeStruct((s_dim, 1024), F32), jax.ShapeDtypeStruct((nc, 8, LANE, LANE), F32)],
        scratch_shapes=[pltpu.VMEM((8, LANE, LANE), F32), pltpu.VMEM((L, LANE), F32), pltpu.VMEM((LANE, L), F32),
                        pltpu.VMEM((L, 1024), F32), pltpu.VMEM((L, 2048), F32)],
        compiler_params=pltpu.CompilerParams(dimension_semantics=("arbitrary",)),
    )(xbca, xbca, xbca, dtr, bias, alog, d_x, tri, expand, expand128)


def _ssd_bwd(xbca, dtr, dtcb, bias, alog, d_x, states, dy, name="ssd_bwd"):
    s_dim = xbca.shape[0]
    L = SSD_CHUNK
    nc = s_dim // L
    tri, expand, expand128, expand_t = _ssd_consts()

    def body(xs_ref, b_ref, c_ref, dt_ref, bias_ref, alog_ref, dx_ref, tri_ref, exp_ref, exp128_ref, expt_ref,
             st_ref, dy_ref, dxbc_ref, ddt_ref, dbias_ref, dalog_ref, dd_ref,
             dst_s, cs_s, cst_s, ex_s, csx_s, dcsx_s, ddtx_s, dcol_s, drow_s, dlast_s, dd_s):
        @pl.when(pl.program_id(0) == 0)
        def _():
            dst_s[...] = jnp.zeros_like(dst_s)
            dbias_ref[...] = jnp.zeros_like(dbias_ref)
            dalog_ref[...] = jnp.zeros_like(dalog_ref)
            dd_s[...] = jnp.zeros_like(dd_s)

        dt, a, dt_x, f_x, t_x = _ssd_prep(dt_ref, bias_ref, alog_ref, tri_ref, exp_ref, exp128_ref, cs_s, cst_s, ex_s, csx_s)
        row = lax.broadcasted_iota(jnp.int32, (L, L), 0)
        col = lax.broadcasted_iota(jnp.int32, (L, L), 1)
        tril = row >= col
        low = col < SSD_HEAD_DIM
        dcol_s[...] = jnp.zeros_like(dcol_s)
        drow_s[...] = jnp.zeros_like(drow_s)
        for g in range(2):
            bg = b_ref[:, g * LANE:(g + 1) * LANE]
            cg = c_ref[:, g * LANE:(g + 1) * LANE]
            bgb, cgb = bg.astype(BF16), cg.astype(BF16)
            gmat = _dot(cgb, bgb, _NT)
            d_g = jnp.zeros((L, L), F32)
            d_b = jnp.zeros((L, LANE), F32)
            d_c = jnp.zeros((L, LANE), F32)
            for jj in range(4):
                j = 4 * g + jj
                sl = slice(j * LANE, (j + 1) * LANE)
                xp = xs_ref[:, sl]
                dtp = dt_x[:, sl]
                x_dt = xp * dtp
                xb = x_dt.astype(BF16)
                dyp = dy_ref[:, sl]
                dd_s[:, sl] += _colsum(dyp * xp)
                d_xdt = jnp.zeros((L, LANE), F32)
                for e in range(2):
                    h = 2 * j + e
                    lm = _decay_matrix(csx_s, cst_s, h, tril)
                    m = gmat * lm
                    dye = jnp.where(low if e == 0 else jnp.logical_not(low), dyp, 0.0).astype(BF16)
                    d_m = jnp.where(tril, _dot(dye, xb, _NT), 0.0)
                    d_xdt = d_xdt + _dot(m.astype(BF16), dye, _TN)
                    d_g = d_g + d_m * lm
                    w = d_m * m
                    dcol_s[...] += jnp.where(col == h, jnp.sum(w, axis=1, keepdims=True), 0.0)
                    drow_s[...] += jnp.where(row == h, jnp.sum(w, axis=0, keepdims=True), 0.0)
                stp = st_ref[0, j]
                stb = stp.astype(BF16)
                dstn = dst_s[j]
                dstb = dstn.astype(BF16)
                e_p = ex_s[:, sl]
                f_p = f_x[:, sl]
                t_p = t_x[:, sl]
                z = _dot(cgb, stb)
                d_z = (e_p * dyp).astype(BF16)
                d_c = d_c + _dot(d_z, stb, _NT)
                d_xf = _dot(bgb, dstb)
                d_b = d_b + _dot((x_dt * f_p).astype(BF16), dstb, _NT)
                d_xdt = d_xdt + f_p * d_xf
                d_f = x_dt * d_xf * f_p
                dcsx_s[:, sl] = dyp * e_p * z - d_f
                dlast_s[:, sl] = _colsum(d_f) + _colsum(dstn * stp) * t_p
                dst_s[j] = _dot(cgb, d_z, _TN) + t_p * dstn
                dxbc_ref[:, sl] = dx_ref[:, sl] * dyp + d_xdt * dtp
                ddtx_s[:, sl] = d_xdt * xp
            d_gb = d_g.astype(BF16)
            dxbc_ref[:, 1024 + g * LANE:1024 + (g + 1) * LANE] = d_b + _dot(d_gb, cgb, _TN)
            dxbc_ref[:, 1280 + g * LANE:1280 + (g + 1) * LANE] = d_c + _dot(d_gb, bgb)

        expt = expt_ref[...]
        dlast = _sel_dot(jnp.broadcast_to(dlast_s[...], (8, 1024)), expt, 3)
        d_cs = dcol_s[...] - drow_s[...].T + _sel_dot(dcsx_s[...], expt, 3)
        rown = lax.broadcasted_iota(jnp.int32, (L, LANE), 0)
        d_cs = d_cs + jnp.where(rown == L - 1, jnp.sum(dlast, axis=0, keepdims=True) * 0.125, 0.0)
        d_da = _sel_dot(d_cs, tri_ref[...], 3, _TN, sel_left=True)
        d_dt = d_da * a + _sel_dot(ddtx_s[...], expt, 3)
        dalog_ref[...] += _colsum(d_da * dt) * a
        d_raw = d_dt * _sigmoid(dt_ref[...] + bias_ref[...])
        ddt_ref[...] = d_raw.astype(ddt_ref.dtype)
        dbias_ref[...] += _colsum(d_raw)
        dd8 = _sel_dot(jnp.broadcast_to(dd_s[...], (8, 1024)), expt, 3)
        dd_ref[...] = jnp.sum(dd8, axis=0, keepdims=True) * 0.125

    const = lambda shape: pl.BlockSpec(shape, lambda c: tuple(0 for _ in shape))
    rev = lambda cb: (lambda c: (nc - 1 - c, cb))
    return pl.pallas_call(
        body, name=name, grid=(nc,),
        in_specs=[pl.BlockSpec((L, 1024), rev(0)), pl.BlockSpec((L, 256), rev(4)), pl.BlockSpec((L, 256), rev(5)),
                  pl.BlockSpec((L, LANE), rev(dtcb)), const((1, LANE)), const((1, LANE)), const((1, 1024)), const((L, L)),
                  const((LANE, 1024)), const((LANE, 2048)), const((1024, LANE)),
                  pl.BlockSpec((1, 8, LANE, LANE), lambda c: (nc - 1 - c, 0, 0, 0)), pl.BlockSpec((L, 1024), rev(0))],
        out_specs=[pl.BlockSpec((L, SSD_XBC), rev(0)), pl.BlockSpec((L, LANE), rev(0)), const((1, LANE)), const((1, LANE)),
                   const((1, LANE))],
        out_shape=[jax.ShapeDtypeStruct((s_dim, SSD_XBC), F32), jax.ShapeDtypeStruct((s_dim, LANE), BF16),
                   jax.ShapeDtypeStruct((1, LANE), F32), jax.ShapeDtypeStruct((1, LANE), F32),
                   jax.ShapeDtypeStruct((1, LANE), F32)],
        scratch_shapes=[pltpu.VMEM((8, LANE, LANE), F32), pltpu.VMEM((L, LANE), F32), pltpu.VMEM((LANE, L), F32),
                        pltpu.VMEM((L, 1024), F32), pltpu.VMEM((L, 2048), F32), pltpu.VMEM((L, 1024), F32),
                        pltpu.VMEM((L, 1024), F32), pltpu.VMEM((L, LANE), F32), pltpu.VMEM((LANE, L), F32),
                        pltpu.VMEM((1, 1024), F32), pltpu.VMEM((1, 1024), F32)],
        compiler_params=pltpu.CompilerParams(dimension_semantics=("arbitrary",)),
    )(xbca, xbca, xbca, dtr, bias, alog, d_x, tri, expand, expand128, expand_t, states, dy)


def _swap_halves(u):
    width = u.shape[1]
    lane = lax.broadcasted_iota(jnp.int32, u.shape, 1)
    return jnp.where(lane % MLA_ROPE < MLA_ROPE // 2, pltpu.roll(u, width - MLA_ROPE // 2, 1), pltpu.roll(u, MLA_ROPE // 2, 1))


def _rope_fwd_fn(u, cos, sin):
    return u * cos + _swap_halves(u) * sin


def _rope_bwd_fn(d, cos, sin):
    return d * cos + _swap_halves(d * sin)


def _spread4(v):
    return v + pltpu.roll(v, 32, 1) + pltpu.roll(v, 64, 1) + pltpu.roll(v, 96, 1)


def _att_masks(tq):
    lane = lax.broadcasted_iota(jnp.int32, (tq, LANE), 1)
    return lane // MLA_NOPE, lane // MLA_ROPE


def _att_tile(i, tq):
    klen = (i + 1) * tq
    qpos = i * tq + lax.broadcasted_iota(jnp.int32, (tq, klen), 0)
    kpos = lax.broadcasted_iota(jnp.int32, (tq, klen), 1)
    return slice(i * tq, (i + 1) * tq), klen, qpos >= kpos


def _att_qcat(qn_t, qr_t, par, e, half_id, grp_id):
    return jnp.concatenate([jnp.where(half_id == par, qn_t, 0.0), jnp.where(grp_id == e, qr_t, 0.0)], axis=1).astype(BF16)


def _att_probs(qcat, kcat, causal):
    s = jnp.where(causal, _dot(qcat, kcat, _NT) * ATT_SCALE, -jnp.inf)
    p = jnp.exp(s - jnp.max(s, axis=1, keepdims=True))
    return p * (1.0 / jnp.sum(p, axis=1, keepdims=True))


def _att_specs(s_dim):
    col = lambda f: pl.BlockSpec((s_dim, LANE), lambda j: (0, f(j)))
    return [col(lambda j: j), col(lambda j: j // 2), col(lambda j: j), col(lambda j: 0), col(lambda j: 8 + j)]


def _att_fwd(q, qr, kv, krt, name="att_fwd"):
    s_dim = q.shape[0]
    tq = min(ATT_TQ, s_dim)

    def body(qn_ref, qr_ref, kn_ref, krt_ref, v_ref, o_ref, kcat_s, vb_s):
        e0 = 2 * (pl.program_id(0) % 2)
        half_id, grp_id = _att_masks(tq)
        kcat_s[...] = jnp.concatenate([kn_ref[...], krt_ref[...]], axis=1).astype(BF16)
        vb_s[...] = v_ref[...].astype(BF16)
        for i in range(s_dim // tq):
            rows, klen, causal = _att_tile(i, tq)
            qn_t, qr_t = qn_ref[rows, :], qr_ref[rows, :]
            outs = []
            for par in range(2):
                qcat = _att_qcat(qn_t, qr_t, par, e0 + par, half_id, grp_id)
                p = _att_probs(qcat, kcat_s[0:klen, :], causal)
                outs.append(_dot(p.astype(BF16), vb_s[0:klen, :]))
            o_ref[rows, :] = jnp.where(half_id == 0, outs[0], outs[1])

    return pl.pallas_call(
        body, name=name, grid=(MLA_HEADS // 2,), in_specs=_att_specs(s_dim),
        out_specs=pl.BlockSpec((s_dim, LANE), lambda j: (0, j)), out_shape=jax.ShapeDtypeStruct((s_dim, 1024), F32),
        scratch_shapes=[pltpu.VMEM((s_dim, 2 * LANE), BF16), pltpu.VMEM((s_dim, LANE), BF16)],
        compiler_params=pltpu.CompilerParams(dimension_semantics=("parallel",)),
    )(q, qr, kv, krt, kv)


def _att_bwd(q, qr, kv, krt, o, do, name="att_bwd"):
    s_dim = q.shape[0]
    tq = min(ATT_TQ, s_dim)

    def body(qn_ref, qr_ref, kn_ref, krt_ref, v_ref, o_ref, do_ref, dqn_ref, dqr_ref, dkn_ref, dv_ref, dkrt_ref,
             kcat_s, vb_s):
        e0 = 2 * (pl.program_id(0) % 2)
        half_id, grp_id = _att_masks(tq)
        kcat_s[...] = jnp.concatenate([kn_ref[...], krt_ref[...]], axis=1).astype(BF16)
        vb_s[...] = v_ref[...].astype(BF16)
        dkn_ref[...] = jnp.zeros_like(dkn_ref)
        dv_ref[...] = jnp.zeros_like(dv_ref)
        dkrt_ref[...] = jnp.zeros_like(dkrt_ref)
        for i in range(s_dim // tq):
            rows, klen, causal = _att_tile(i, tq)
            qn_t, qr_t, o_t, do_t = qn_ref[rows, :], qr_ref[rows, :], o_ref[rows, :], do_ref[rows, :]
            dqn = jnp.zeros((tq, LANE), F32)
            dqr = jnp.zeros((tq, LANE), F32)
            for par in range(2):
                qcat = _att_qcat(qn_t, qr_t, par, e0 + par, half_id, grp_id)
                p = _att_probs(qcat, kcat_s[0:klen, :], causal)
                dom = jnp.where(half_id == par, do_t, 0.0)
                domb = dom.astype(BF16)
                d_p = _dot(domb, vb_s[0:klen, :], _NT)
                d_row = jnp.sum(dom * o_t, axis=1, keepdims=True)
                d_s = (p * (d_p - d_row) * ATT_SCALE).astype(BF16)
                dqcat = _dot(d_s, kcat_s[0:klen, :])
                dqn = dqn + jnp.where(half_id == par, dqcat[:, :LANE], 0.0)
                dqr = dqr + jnp.where(grp_id == e0 + par, dqcat[:, LANE:], 0.0)
                dkcat = _dot(d_s, qcat, _TN)
                dkn_ref[0:klen, :] += dkcat[:, :LANE]
                dkrt_ref[0:klen, :] += dkcat[:, LANE:]
                dv_ref[0:klen, :] += _dot(p.astype(BF16), domb, _TN)
            dqn_ref[rows, :] = dqn.astype(dqn_ref.dtype)
            dqr_ref[rows, :] = dqr

    col = lambda f: pl.BlockSpec((s_dim, LANE), lambda j: (0, f(j)))
    return pl.pallas_call(
        body, name=name, grid=(MLA_HEADS // 2,), in_specs=_att_specs(s_dim) + [col(lambda j: j), col(lambda j: j)],
        out_specs=[col(lambda j: j), pl.BlockSpec((None, s_dim, LANE), lambda j: (j % 2, 0, j // 2)), col(lambda j: j),
                   col(lambda j: j), pl.BlockSpec((None, s_dim, LANE), lambda j: (j, 0, 0))],
        out_shape=[jax.ShapeDtypeStruct((s_dim, 1024), BF16), jax.ShapeDtypeStruct((2, s_dim, 512), F32),
                   jax.ShapeDtypeStruct((s_dim, 1024), F32), jax.ShapeDtypeStruct((s_dim, 1024), F32),
                   jax.ShapeDtypeStruct((MLA_HEADS // 2, s_dim, LANE), F32)],
        scratch_shapes=[pltpu.VMEM((s_dim, 2 * LANE), BF16), pltpu.VMEM((s_dim, LANE), BF16)],
        compiler_params=pltpu.CompilerParams(dimension_semantics=("parallel",)),
    )(q, qr, kv, krt, kv, o, do)


def _all_gather(x, name):
    rows, width = x.shape

    def body(x_ref, out_ref, send_sems, recv_sems, local_sem):
        x_i, y_i, c_i = lax.axis_index("x"), lax.axis_index("y"), lax.axis_index("c")
        me, sibling = (x_i, y_i, c_i), (x_i, y_i, 1 - c_i)
        chips = [(1 - x_i, y_i), (x_i, 1 - y_i), (1 - x_i, 1 - y_i)]

        def slot(px, py, pc):
            return out_ref.at[4 * px + 2 * py + pc]

        def copy(k, block, to, src=None):
            return pltpu.make_async_remote_copy(
                src_ref=slot(*block) if src is None else src, dst_ref=slot(*block), send_sem=send_sems.at[k],
                recv_sem=recv_sems.at[k], device_id=to, device_id_type=pl.DeviceIdType.MESH)

        mine = pltpu.make_async_copy(x_ref, slot(*me), local_sem)
        mine.start()
        first = [copy(0, me, sibling, src=x_ref)]
        first += [copy(1 + j, me, (*chip, c_i), src=x_ref) for j, chip in enumerate(chips)]
        for cp in first:
            cp.start()
        passed = [copy(4 + j, (*chip, c_i), sibling) for j, chip in enumerate(chips)]
        for j, chip in enumerate(chips):
            copy(1 + j, (*chip, c_i), me).wait_recv()
            passed[j].start()
        copy(0, sibling, me).wait_recv()
        for j, chip in enumerate(chips):
            copy(4 + j, (*chip, 1 - c_i), me).wait_recv()
        for cp in first + passed:
            cp.wait_send()
        mine.wait()

    return pl.pallas_call(
        body, name=name, out_shape=jax.ShapeDtypeStruct((N_DEV, rows, width), x.dtype),
        in_specs=[pl.BlockSpec(memory_space=pl.ANY)], out_specs=pl.BlockSpec(memory_space=pl.ANY),
        scratch_shapes=[pltpu.SemaphoreType.DMA((7,)), pltpu.SemaphoreType.DMA((7,)), pltpu.SemaphoreType.DMA],
    )(x)


def _gather_many(shards, name):
    n_arr = len(shards)

    def body(*refs):
        x_refs, out_refs = refs[:n_arr], refs[n_arr:2 * n_arr]
        send_sems, recv_sems, local_sems = refs[2 * n_arr:]
        x_i, y_i, c_i = lax.axis_index("x"), lax.axis_index("y"), lax.axis_index("c")
        me, sibling = (x_i, y_i, c_i), (x_i, y_i, 1 - c_i)
        chips = [(1 - x_i, y_i), (x_i, 1 - y_i), (1 - x_i, 1 - y_i)]

        def copy(a, k, block, to, src=None):
            slot = out_refs[a].at[4 * block[0] + 2 * block[1] + block[2]]
            return pltpu.make_async_remote_copy(
                src_ref=slot if src is None else src, dst_ref=slot, send_sem=send_sems.at[a, k],
                recv_sem=recv_sems.at[a, k], device_id=to, device_id_type=pl.DeviceIdType.MESH)

        mine, first, passed = [], [], []
        for a in range(n_arr):
            mine.append(pltpu.make_async_copy(x_refs[a], out_refs[a].at[4 * x_i + 2 * y_i + c_i], local_sems.at[a]))
            mine[a].start()
            first.append([copy(a, 0, me, sibling, src=x_refs[a])]
                         + [copy(a, 1 + j, me, (*chip, c_i), src=x_refs[a]) for j, chip in enumerate(chips)])
            for cp in first[a]:
                cp.start()
            passed.append([copy(a, 4 + j, (*chip, c_i), sibling) for j, chip in enumerate(chips)])
        for j, chip in enumerate(chips):
            for a in range(n_arr):
                copy(a, 1 + j, (*chip, c_i), me).wait_recv()
                passed[a][j].start()
        for a in range(n_arr):
            copy(a, 0, sibling, me).wait_recv()
            for j, chip in enumerate(chips):
                copy(a, 4 + j, (*chip, 1 - c_i), me).wait_recv()
        for a in range(n_arr):
            for cp in first[a] + passed[a]:
                cp.wait_send()
            mine[a].wait()

    any_spec = pl.BlockSpec(memory_space=pl.ANY)
    return pl.pallas_call(
        body, name=name, out_shape=[jax.ShapeDtypeStruct((N_DEV,) + x.shape, x.dtype) for x in shards],
        in_specs=[any_spec] * n_arr, out_specs=[any_spec] * n_arr,
        scratch_shapes=[pltpu.SemaphoreType.DMA((n_arr, 7)), pltpu.SemaphoreType.DMA((n_arr, 7)),
                        pltpu.SemaphoreType.DMA((n_arr,))],
    )(*shards)


def _pair_exchange(grads, name):
    n_arr = len(grads)

    def body(*refs):
        g_refs, out_refs = refs[:n_arr], refs[n_arr:2 * n_arr]
        send_sems, recv_sems = refs[2 * n_arr:]
        x_i, y_i, c_i = lax.axis_index("x"), lax.axis_index("y"), lax.axis_index("c")
        copies = []
        for a in range(n_arr):
            for chip in range(4):
                copies.append(pltpu.make_async_remote_copy(
                    src_ref=g_refs[a].at[2 * chip + (1 - c_i)], dst_ref=out_refs[a].at[chip], send_sem=send_sems.at[a, chip],
                    recv_sem=recv_sems.at[a, chip], device_id=(x_i, y_i, 1 - c_i), device_id_type=pl.DeviceIdType.MESH))
        for cp in copies:
            cp.start()
        for cp in copies:
            cp.wait_recv()
        for cp in copies:
            cp.wait_send()

    any_spec = pl.BlockSpec(memory_space=pl.ANY)
    return pl.pallas_call(
        body, name=name, out_shape=[jax.ShapeDtypeStruct((4,) + g.shape[1:], g.dtype) for g in grads],
        in_specs=[any_spec] * n_arr, out_specs=[any_spec] * n_arr,
        scratch_shapes=[pltpu.SemaphoreType.DMA((n_arr, 4)), pltpu.SemaphoreType.DMA((n_arr, 4))],
    )(*grads)


def _chip_exchange(sums, name):
    n_arr = len(sums)

    def body(*refs):
        s_refs, out_refs = refs[:n_arr], refs[n_arr:2 * n_arr]
        send_sems, recv_sems, local_sems = refs[2 * n_arr:]
        x_i, y_i, c_i = lax.axis_index("x"), lax.axis_index("y"), lax.axis_index("c")
        my_chip = 2 * x_i + y_i
        copies, local = [], []
        for a in range(n_arr):
            local.append(pltpu.make_async_copy(s_refs[a].at[my_chip], out_refs[a].at[my_chip], local_sems.at[a]))
            local[a].start()
            for k in range(1, 4):
                px, py = x_i ^ (k >> 1), y_i ^ (k & 1)
                copies.append(pltpu.make_async_remote_copy(
                    src_ref=s_refs[a].at[2 * px + py], dst_ref=out_refs[a].at[my_chip], send_sem=send_sems.at[a, k - 1],
                    recv_sem=recv_sems.at[a, k - 1], device_id=(px, py, c_i), device_id_type=pl.DeviceIdType.MESH))
        for cp in copies:
            cp.start()
        for cp in copies:
            cp.wait_recv()
        for cp in copies:
            cp.wait_send()
        for cp in local:
            cp.wait()

    any_spec = pl.BlockSpec(memory_space=pl.ANY)
    return pl.pallas_call(
        body, name=name, out_shape=[jax.ShapeDtypeStruct(s.shape, s.dtype) for s in sums],
        in_specs=[any_spec] * n_arr, out_specs=[any_spec] * n_arr,
        scratch_shapes=[pltpu.SemaphoreType.DMA((n_arr, 3)), pltpu.SemaphoreType.DMA((n_arr, 3)),
                        pltpu.SemaphoreType.DMA((n_arr,))],
    )(*sums)


_HBM = pl.BlockSpec(memory_space=pltpu.HBM)
_SEM = pl.BlockSpec(memory_space=pltpu.SEMAPHORE)


def _plan_copies(plan, src_refs, land_refs, send_sems, recv_sems):
    copies = []
    for s_ref, l_ref in zip(src_refs, land_refs):
        for src, dst, peer in plan(s_ref, l_ref):
            k = len(copies)
            copies.append(pltpu.make_async_remote_copy(
                src_ref=src, dst_ref=dst, send_sem=send_sems.at[k], recv_sem=recv_sems.at[k], device_id=peer,
                device_id_type=pl.DeviceIdType.MESH))
    return copies


def _split_start(srcs, lands, plan, n_copy, name):
    n = len(srcs)

    def body(*refs):
        for cp in _plan_copies(plan, refs[:n], refs[n:2 * n], refs[2 * n], refs[2 * n + 1]):
            cp.start()
        refs[-1][...] = jnp.zeros_like(refs[-1])

    sems = pltpu.SemaphoreType.DMA((n * n_copy,))
    res = pl.pallas_call(
        body, name=name,
        out_shape=(sems, sems, *[pltpu.HBM(a.shape, a.dtype) for a in list(srcs) + list(lands)],
                   jax.ShapeDtypeStruct((8, LANE), F32)),
        in_specs=[_HBM] * (2 * n), out_specs=(_SEM, _SEM, *[_HBM] * (2 * n), pl.BlockSpec(memory_space=pltpu.VMEM)),
        input_output_aliases={i: 2 + i for i in range(2 * n)},
        compiler_params=pltpu.CompilerParams(has_side_effects=pltpu.SideEffectType.DATAFLOW_SIDE_EFFECTING),
    )(*[pltpu.with_memory_space_constraint(a, pltpu.HBM) for a in list(srcs) + list(lands)])
    return res[0], res[1], list(res[2:2 + n]), list(res[2 + n:2 + 2 * n]), res[-1]


def _split_wait(send_sems, recv_sems, srcs, lands, after, plan, name):
    n = len(srcs)

    def body(*refs):
        copies = _plan_copies(plan, refs[:n], refs[n:2 * n], refs[2 * n], refs[2 * n + 1])
        for cp in copies:
            cp.wait_send()
        for cp in copies:
            cp.wait_recv()

    res = pl.pallas_call(
        body, name=name, out_shape=tuple(pltpu.HBM(a.shape, a.dtype) for a in list(srcs) + list(lands)),
        in_specs=[_HBM] * (2 * n) + [_SEM, _SEM, pl.BlockSpec(memory_space=pl.ANY)], out_specs=tuple([_HBM] * (2 * n)),
        input_output_aliases={i: i for i in range(2 * n)},
        compiler_params=pltpu.CompilerParams(has_side_effects=pltpu.SideEffectType.DATAFLOW_SIDE_EFFECTING),
    )(*srcs, *lands, send_sems, recv_sems, after)
    return list(res[:n]), list(res[n:])


def _plan_broadcast(src, land):
    x_i, y_i, c_i = lax.axis_index("x"), lax.axis_index("y"), lax.axis_index("c")
    me = 4 * x_i + 2 * y_i + c_i
    return [(src, land.at[me], (x_i ^ (k >> 2), y_i ^ ((k >> 1) & 1), c_i ^ (k & 1))) for k in range(1, N_DEV)]


def _plan_pairs(src, land):
    x_i, y_i, c_i = lax.axis_index("x"), lax.axis_index("y"), lax.axis_index("c")
    return [(src.at[2 * chip + (1 - c_i)], land.at[chip], (x_i, y_i, 1 - c_i)) for chip in range(4)]


def _plan_chips(src, land):
    x_i, y_i, c_i = lax.axis_index("x"), lax.axis_index("y"), lax.axis_index("c")
    plan = []
    for k in range(1, 4):
        px, py = x_i ^ (k >> 1), y_i ^ (k & 1)
        plan.append((src.at[2 * px + py], land.at[2 * x_i + y_i], (px, py, c_i)))
    return plan


def _pair_sum(g, recv, core, name):
    _, rows, cols = g.shape
    tr = ROW_TILE if rows % ROW_TILE == 0 else rows

    def body(core_ref, g_ref, r_ref, o_ref):
        o_ref[...] = (g_ref[...].astype(F32) + r_ref[...].astype(F32)).astype(o_ref.dtype)

    grid_spec = pltpu.PrefetchScalarGridSpec(
        num_scalar_prefetch=1, grid=(4, rows // tr),
        in_specs=[pl.BlockSpec((None, tr, cols), lambda k, i, core_ref: (2 * k + core_ref[0], i, 0)),
                  pl.BlockSpec((None, tr, cols), lambda k, i, core_ref: (k, i, 0))],
        out_specs=pl.BlockSpec((None, tr, cols), lambda k, i, core_ref: (k, i, 0)))
    return pl.pallas_call(body, name=name, grid_spec=grid_spec, out_shape=jax.ShapeDtypeStruct((4, rows, cols), g.dtype))(
        core, g, recv)


def _adam_math(g, w, m, v):
    m_new = ADAM_B1 * m + (1.0 - ADAM_B1) * g
    v_new = ADAM_B2 * v + (1.0 - ADAM_B2) * (g * g)
    m_hat = m_new / (1.0 - ADAM_B1 ** ADAM_STEP)
    v_hat = v_new / (1.0 - ADAM_B2 ** ADAM_STEP)
    return -ADAM_LR * (m_hat / (jnp.sqrt(v_hat) + ADAM_EPS) + ADAM_WD * w), m_new, v_new


def _adam(slots, w, m, v, name, own=None, own_idx=None):
    n_slot, rows, cols = slots.shape
    tr = ROW_TILE if rows % ROW_TILE == 0 else rows
    has_own = own is not None

    def body(*refs):
        refs = refs[1:] if has_own else refs
        if has_own:
            own_ref, refs = refs[0], refs[1:]
        s_ref, w_ref, m_ref, v_ref, g_ref, d_ref, mo_ref, vo_ref = refs
        g = own_ref[...].astype(F32) if has_own else s_ref[0].astype(F32)
        for k in range(0 if has_own else 1, n_slot):
            g = g + s_ref[k].astype(F32)
        g_ref[...] = g
        d_ref[...], mo_ref[...], vo_ref[...] = _adam_math(g, w_ref[...], m_ref[...], v_ref[...])

    spec = pl.BlockSpec((tr, cols), lambda i, *_: (i, 0))
    in_specs = [pl.BlockSpec((n_slot, tr, cols), lambda i, *_: (0, i, 0)), spec, spec, spec]
    if has_own:
        in_specs = [pl.BlockSpec((None, tr, cols), lambda i, idx: (idx[0], i, 0))] + in_specs
    grid_spec = pltpu.PrefetchScalarGridSpec(num_scalar_prefetch=1 if has_own else 0, grid=(rows // tr,), in_specs=in_specs,
                                             out_specs=[spec] * 4)
    ins = ([own_idx, own] if has_own else []) + [slots, w, m, v]
    return pl.pallas_call(
        body, name=name, grid_spec=grid_spec, out_shape=[jax.ShapeDtypeStruct((rows, cols), F32)] * 4,
        compiler_params=pltpu.CompilerParams(dimension_semantics=("parallel",)),
    )(*ins)


PACK_ROWS, PACK_W = 24, 1536
REPL_W = (("ssd_conv_b", 1536), ("ssd_dt_bias", 16), ("ssd_A_log", 16), ("ssd_D", 16), ("ssd_norm_w", 1024),
          ("mla_q_norm_w", 384), ("mla_kv_norm_w", 256), ("mla_out_norm_w", 1024), ("ln_mix_g", 1024),
          ("ln_mix_b", 1024), ("ln_ffn_g", 1024), ("ln_ffn_b", 1024))
LOSS_ROW = 4 + len(REPL_W)


def _pack_small(conv_w_grad, grads, loss, name="pack_small"):
    def body(*refs):
        cw_ref, g_refs, loss_ref, o_ref = refs[0], refs[1:1 + len(REPL_W)], refs[1 + len(REPL_W)], refs[-1]
        o_ref[...] = jnp.zeros_like(o_ref)
        o_ref[0:4, :] = cw_ref[...]
        for i, g_ref in enumerate(g_refs):
            o_ref[4 + i:5 + i, 0:g_ref.shape[1]] = g_ref[...]
        o_ref[LOSS_ROW:LOSS_ROW + 1, 0:LANE] = loss_ref[...]

    return pl.pallas_call(body, name=name, out_shape=jax.ShapeDtypeStruct((PACK_ROWS, PACK_W), F32))(conv_w_grad, *grads, loss)


def _adam_small(gathered, wmv, name="adam_small"):
    def body(*refs):
        s_ref = refs[0]
        in_refs = refs[1:1 + 3 * len(REPL_W)]
        cw_ref, loss_ref = refs[1 + 3 * len(REPL_W)], refs[2 + 3 * len(REPL_W)]
        out_refs = refs[3 + 3 * len(REPL_W):-1]
        tot = refs[-1]
        acc = s_ref[0]
        for k in range(1, N_DEV):
            acc = acc + s_ref[k]
        tot[...] = acc
        cw_ref[...] = tot[0:4, :]
        loss_ref[...] = tot[LOSS_ROW:LOSS_ROW + 1, 0:LANE]
        for i, (_, width) in enumerate(REPL_W):
            g = tot[4 + i:5 + i, 0:width]
            w_ref, m_ref, v_ref = in_refs[3 * i:3 * i + 3]
            g_ref, d_ref, mo_ref, vo_ref = out_refs[4 * i:4 * i + 4]
            g_ref[...] = g
            d_ref[...], mo_ref[...], vo_ref[...] = _adam_math(g, w_ref[...], m_ref[...], v_ref[...])

    flat_in = [a for triple in wmv for a in triple]
    out_shape = [jax.ShapeDtypeStruct((4, PACK_W), F32), jax.ShapeDtypeStruct((1, LANE), F32)]
    for _, width in REPL_W:
        out_shape += [jax.ShapeDtypeStruct((1, width), F32)] * 4
    res = pl.pallas_call(body, name=name, out_shape=out_shape, scratch_shapes=[pltpu.VMEM((PACK_ROWS, PACK_W), F32)])(
        gathered, *flat_in)
    return res[0], res[1], [res[2 + 4 * i:6 + 4 * i] for i in range(len(REPL_W))]


def _cols_full(g):
    return jnp.transpose(g, (1, 0, 2)).reshape(g.shape[1], -1)


def _cols_split(full):
    k_dim, n_dim = full.shape
    return jnp.transpose(full.reshape(k_dim, N_DEV, n_dim // N_DEV), (1, 0, 2))


PROJ_BLOCK = {"z": (1024, 0), "dt": (LANE, 8), "q_c": (MLA_Q_RANK, 3), "xbc": (SSD_XBC, 1), "kv_c": (MLA_KV_RANK, 12),
              "k_rope": (LANE, 26)}


def _win_pad(w):
    z = lambda n: jnp.zeros((w.shape[0], n), w.dtype)
    return jnp.concatenate([w[:, :1024], w[:, 2560:2576], z(112), w[:, 2576:2960], w[:, 1024:2560], w[:, 2960:3216],
                            w[:, 3216:3248], z(96)], axis=1)


def _win_unpad(w):
    return jnp.concatenate([w[:, :1024], w[:, 1536:3072], w[:, 1024:1040], w[:, 1152:1536], w[:, 3072:3328], w[:, 3328:3360]],
                           axis=1)


def _heads_split(w, a, b):
    k_dim = w.shape[0]
    w3 = w.reshape(k_dim, MLA_HEADS, a + b)
    return jnp.concatenate([w3[:, :, :a].reshape(k_dim, -1), w3[:, :, a:].reshape(k_dim, -1)], axis=1)


def _heads_merge(w, a, b):
    k_dim = w.shape[0]
    wa = w[:, :MLA_HEADS * a].reshape(k_dim, MLA_HEADS, a)
    wb = w[:, MLA_HEADS * a:].reshape(k_dim, MLA_HEADS, b)
    return jnp.concatenate([wa, wb], axis=2).reshape(k_dim, -1)


def _pad_lanes(v, width=LANE):
    return jnp.concatenate([v, jnp.zeros((v.shape[0], width - v.shape[1]), v.dtype)], axis=1)


def _local_step(x, p, positions, tgt, W, P, comm=None):
    comm = comm or {}
    zero_tok = jnp.zeros((8, LANE), F32)
    s_dim = x.shape[0]
    inv_freq = 1.0 / (ROPE_BASE ** (jnp.arange(0, MLA_ROPE, 2, dtype=F32) / MLA_ROPE))
    ang = positions.astype(F32)[:, None] * inv_freq
    cos, sin = jnp.cos(ang), jnp.sin(ang)
    cos32 = jnp.concatenate([cos, cos], axis=1)
    sin32 = jnp.concatenate([-sin, sin], axis=1)
    cos512, sin512 = jnp.tile(cos32, (1, 16)), jnp.tile(sin32, (1, 16))
    cos128, sin128 = jnp.tile(cos32, (1, 4)), jnp.tile(sin32, (1, 4))
    bias_p, alog_p = _pad_lanes(P["ssd_dt_bias"]), _pad_lanes(P["ssd_A_log"])
    d_x = jnp.repeat(P["ssd_D"], SSD_HEAD_DIM, axis=1)

    xb, pb = x.astype(BF16), p.astype(BF16)
    if "token0" in comm:
        xb = (x + comm["token0"][0, 0]).astype(BF16)
    proj = _mm(xb, W["w_in"], name="mm_in")
    z, qc, kvc, kr = [(proj,) + PROJ_BLOCK[n] for n in ("z", "q_c", "kv_c", "k_rope")]
    xbca = _conv_fwd(proj, PROJ_BLOCK["xbc"][1], P["ssd_conv_w"], P["ssd_conv_b"])
    y, states = _ssd_fwd(xbca, proj, PROJ_BLOCK["dt"][1], bias_p, alog_p, d_x)
    (yssd,) = _rowwise(_gate_rms, [y, z], [P["ssd_norm_w"]], [(1024, BF16)], name="ssd_gate_norm")
    (qn,) = _rowwise(_rms, [qc], [P["mla_q_norm_w"]], [(MLA_Q_RANK, BF16)], name="q_norm")
    (kvn,) = _rowwise(_rms, [kvc], [P["mla_kv_norm_w"]], [(MLA_KV_RANK, BF16)], name="kv_norm")
    q = _mm(qn, W["mla_w_q_b"], name="mm_q")
    kv = _mm(kvn, W["mla_w_kv_b"], name="mm_kv")
    (qr,) = _rowwise(_rope_fwd_fn, [(q, 512, 2), cos512, sin512], [], [512], name="rope_q")
    (krt,) = _rowwise(lambda u, c, s: _spread4(_rope_fwd_fn(u, c, s)), [kr, cos128, sin128], [], [LANE], name="rope_k")
    att = _att_fwd(q, qr, kv, krt)
    (ymla,) = _rowwise(_rms, [att], [P["mla_out_norm_w"]], [(1024, BF16)], name="out_norm")
    ycat = jnp.concatenate([yssd, ymla], axis=1)
    if "late_weights" in comm:
        W = {**W, **comm["late_weights"](ycat)}
    mix = _mm(ycat, W["w_out"], name="mm_out")
    f_h1 = lambda xv, mv, g, b: _ln(ALPHA * xv + mv, g, b)
    h1, h1b = _rowwise(lambda *a: (f_h1(*a),) * 2, [x, mix], [P["ln_mix_g"], P["ln_mix_b"]], [1024, (1024, BF16)],
                       name="ln_mix")
    fb = D_FF // N_DEV
    hg = _mm(h1b, W["w_ffn_gate"], b_blk="n", o_blk="n", out_dtype=BF16, name="mm_gate")
    hu = _mm(h1b, W["w_ffn_up"], b_blk="n", o_blk="n", out_dtype=BF16, name="mm_up")
    pg = _mm(h1b, W["w_ple_gate"], name="mm_ple_gate")
    pp = _mm(pb, W["w_ple_proj"], name="mm_ple")
    hg2, hu2 = hg.reshape(N_DEV * s_dim, fb), hu.reshape(N_DEV * s_dim, fb)
    (act,) = _rowwise(lambda g, u: _silu(g.astype(F32)) * u.astype(F32), [hg2, hu2], [], [(fb, BF16)], name="swiglu",
                      tr=512)
    act3 = act.reshape(N_DEV, s_dim, fb)
    ffn = _mm(act3, W["w_ffn_down"], a_blk="k", b_blk="k", name="mm_down")

    f_h2 = lambda hv, fv, pg, ppv, g, b: _ln(ALPHA * hv + fv + _sigmoid(pg) * ppv, g, b)

    def final_fn(hv, fv, pg, ppv, tv, g, b):
        h2, pull = jax.vjp(f_h2, hv, fv, pg, ppv, g, b)
        diff = h2 - tv
        loss = 0.5 * jnp.sum(jnp.mean(diff * diff, axis=-1, keepdims=True), axis=0, keepdims=True)
        d_h, d_f, d_pg, d_pp, d_g, d_b = pull(diff * (1.0 / D_MODEL))
        return d_h, d_f, d_pg, d_pp, d_g, d_b, jnp.broadcast_to(loss, (1, LANE))

    dh1_a, dffn, dpg, dpp, g_ffn_g, g_ffn_b, loss = _rowwise(
        final_fn, [h1, ffn, pg, pp, tgt], [P["ln_ffn_g"], P["ln_ffn_b"]], [1024] + [(1024, BF16)] * 3,
        [1024, 1024, LANE], name="final")

    G = {}
    dact = _mm(dffn, W["w_ffn_down"], tb=True, b_blk="n", o_blk="n", name="mm_down_dx")
    G["w_ffn_down"] = _mm(act3, dffn, ta=True, a_blk="m", o_blk="m", out_dtype=GRAD_DT, name="mm_down_dw")

    def swiglu_bwd(g, u, d):
        g, u = g.astype(F32), u.astype(F32)
        sg = _sigmoid(g)
        return d * u * (sg * (1.0 + g * (1.0 - sg))), d * (g * sg)

    dg, du = _rowwise(swiglu_bwd, [hg2, hu2, dact.reshape(N_DEV * s_dim, fb)], [], [(fb, BF16)] * 2, name="swiglu_bwd",
                      tr=512)
    dg3, du3 = dg.reshape(N_DEV, s_dim, fb), du.reshape(N_DEV, s_dim, fb)
    dh1 = _mm(dg3, W["w_ffn_gate"], tb=True, a_blk="k", b_blk="k", add=dh1_a, name="mm_gate_dx")
    dh1 = _mm(du3, W["w_ffn_up"], tb=True, a_blk="k", b_blk="k", add=dh1, name="mm_up_dx")
    dh1 = _mm(dpg, W["w_ple_gate"], tb=True, add=dh1, name="mm_ple_gate_dx")
    G["w_ffn_gate"] = _mm(h1b, dg3, ta=True, b_blk="n", o_blk="n", out_dtype=GRAD_DT, name="mm_gate_dw")
    G["w_ffn_up"] = _mm(h1b, du3, ta=True, b_blk="n", o_blk="n", out_dtype=GRAD_DT, name="mm_up_dw")
    G["w_ple_gate"] = _mm(h1b, dpg, ta=True, out_dtype=GRAD_DT, name="mm_ple_gate_dw")
    G["w_ple_proj"] = _mm(pb, dpp, ta=True, out_dtype=GRAD_DT, name="mm_ple_dw")
    dx_a, dmix, g_mix_g, g_mix_b = _rowwise(
        lambda xv, mv, dv, g, b: _vjp_rows(f_h1)(xv, mv, g, b, dv), [x, mix, dh1], [P["ln_mix_g"], P["ln_mix_b"]],
        [1024, (1024, BF16)], [1024, 1024], name="ln_mix_bwd")
    dycat = _mm(dmix, W["w_out"], tb=True, name="mm_out_dx")
    G["w_out"] = _mm(ycat, dmix, ta=True, out_dtype=GRAD_DT, name="mm_out_dw")

    tok1 = comm["ffn_grads"](G) if "ffn_grads" in comm else zero_tok
    datt, g_out_norm = _rowwise(lambda a, dv, w, t: _vjp_rows(_rms)(a, w, dv + jnp.min(t)), [att, (dycat, 1024, 1)],
                                [P["mla_out_norm_w"], tok1], [1024], [1024], name="out_norm_bwd")
    dqn_nope, dqr, dkn, dv, dkrt = _att_bwd(q, qr, kv, krt, att, datt)
    dkv = jnp.concatenate([dkn, dv], axis=1)
    tok2 = comm["mid"](dqn_nope) if "mid" in comm else zero_tok
    (dq_rope,) = _rowwise(lambda d0, d1, c, s, t: _rope_bwd_fn(d0 + d1 + jnp.min(t), c, s),
                          [(dqr, 512, 0), (dqr, 512, 1), cos512, sin512], [tok2], [(512, BF16)], name="rope_q_bwd")

    def rope_k_bwd(*a):
        d = _spread4(functools.reduce(lambda u, w: u + w, a[:-3]))
        lane = lax.broadcasted_iota(jnp.int32, d.shape, 1)
        return _rope_bwd_fn(jnp.where(lane < MLA_ROPE, d, 0.0), a[-3], a[-2])

    (dkr,) = _rowwise(rope_k_bwd, [(dkrt, LANE, k) for k in range(MLA_HEADS // 2)] + [cos128, sin128], [tok2],
                      [(LANE, BF16)], name="rope_k_bwd")
    dq = jnp.concatenate([dqn_nope, dq_rope], axis=1)
    dqn = _mm(dq, W["mla_w_q_b"], tb=True, name="mm_q_dx")
    G["mla_w_q_b"] = _mm(qn, dq, ta=True, out_dtype=GRAD_DT, name="mm_q_dw")
    dkvn = _mm(dkv, W["mla_w_kv_b"], tb=True, name="mm_kv_dx")
    G["mla_w_kv_b"] = _mm(kvn, dkv, ta=True, out_dtype=GRAD_DT, name="mm_kv_dw")
    dqc, g_q_norm = _rowwise(lambda a, dv, w: _vjp_rows(_rms)(a, w, dv), [qc, dqn], [P["mla_q_norm_w"]],
                             [(MLA_Q_RANK, BF16)], [MLA_Q_RANK], name="q_norm_bwd")
    dkvc, g_kv_norm = _rowwise(lambda a, dv, w: _vjp_rows(_rms)(a, w, dv), [kvc, dkvn], [P["mla_kv_norm_w"]],
                               [(MLA_KV_RANK, BF16)], [MLA_KV_RANK], name="kv_norm_bwd")

    dy, dz, g_ssd_norm = _rowwise(lambda yv, zv, dv, w, t: _vjp_rows(_gate_rms)(yv, zv, w, dv + jnp.min(t)),
                                  [y, z, (dycat, 1024, 0)], [P["ssd_norm_w"], tok2], [1024, (1024, BF16)], [1024],
                                  name="ssd_gate_norm_bwd")
    dxbca, ddtr, g_dt_bias, g_alog, g_d = _ssd_bwd(xbca, proj, PROJ_BLOCK["dt"][1], bias_p, alog_p, d_x, states, dy)
    da, g_conv_w, g_conv_b = _conv_bwd_pre(proj, PROJ_BLOCK["xbc"][1], P["ssd_conv_w"], P["ssd_conv_b"], dxbca)
    dxbc = _conv_bwd_in(da, P["ssd_conv_w"])

    dproj = jnp.concatenate([dz, ddtr, dqc, dxbc, dkvc, dkr], axis=1)
    grad_x = _mm(dproj, W["w_in"], tb=True, add=dx_a, name="mm_in_dx")
    G["w_in"] = _mm(xb, dproj, ta=True, out_dtype=GRAD_DT, name="mm_in_dw")

    small = {
        "ssd_conv_b": g_conv_b, "ssd_dt_bias": g_dt_bias, "ssd_A_log": g_alog, "ssd_D": g_d, "ssd_norm_w": g_ssd_norm,
        "mla_q_norm_w": g_q_norm, "mla_kv_norm_w": g_kv_norm, "mla_out_norm_w": g_out_norm, "ln_mix_g": g_mix_g,
        "ln_mix_b": g_mix_b, "ln_ffn_g": g_ffn_g, "ln_ffn_b": g_ffn_b,
    }
    return grad_x, G, _pack_small(g_conv_w, [small[n] for n, _ in REPL_W], loss)


def kernel(x, p, positions, w_in, ssd_conv_w, ssd_conv_b, ssd_dt_bias, ssd_A_log, ssd_D, ssd_norm_w, mla_q_norm_w, mla_w_q_b, mla_kv_norm_w, mla_w_kv_b, mla_out_norm_w, w_out, ln_mix_g, ln_mix_b, w_ffn_gate, w_ffn_up, w_ffn_down, w_ple_gate, w_ple_proj, ln_ffn_g, ln_ffn_b, loss_target, m_w_in, m_ssd_conv_w, m_ssd_conv_b, m_ssd_dt_bias, m_ssd_A_log, m_ssd_D, m_ssd_norm_w, m_mla_q_norm_w, m_mla_w_q_b, m_mla_kv_norm_w, m_mla_w_kv_b, m_mla_out_norm_w, m_w_out, m_ln_mix_g, m_ln_mix_b, m_w_ffn_gate, m_w_ffn_up, m_w_ffn_down, m_w_ple_gate, m_w_ple_proj, m_ln_ffn_g, m_ln_ffn_b, v_w_in, v_ssd_conv_w, v_ssd_conv_b, v_ssd_dt_bias, v_ssd_A_log, v_ssd_D, v_ssd_norm_w, v_mla_q_norm_w, v_mla_w_q_b, v_mla_kv_norm_w, v_mla_w_kv_b, v_mla_out_norm_w, v_w_out, v_ln_mix_g, v_ln_mix_b, v_w_ffn_gate, v_w_ffn_up, v_w_ffn_down, v_w_ple_gate, v_w_ple_proj, v_ln_ffn_g, v_ln_ffn_b):
    args = dict(locals())
    core = lax.axis_index("c")
    me = 4 * lax.axis_index("x") + 2 * lax.axis_index("y") + core

    conv_sh = ssd_conv_w[0]
    conv_hi = conv_sh.astype(BF16)
    conv_lo = (conv_sh - conv_hi.astype(F32)).astype(BF16)
    shards = {n: args[n][0].astype(BF16) for n in BIG}
    rows_full = lambda g: g.reshape(-1, g.shape[2])
    core_arr = core.astype(jnp.int32).reshape(1)
    chip_arr = (2 * lax.axis_index("x") + lax.axis_index("y")).astype(jnp.int32).reshape(1)

    early = _gather_many([shards[n] for n in EARLY] + [jnp.concatenate([conv_hi, conv_lo], axis=0)], "gather_early")
    gw = dict(zip(EARLY, early[:-1]))
    conv_g = early[-1].astype(F32)
    W = {
        "w_in": _win_pad(_cols_full(gw["w_in"])),
        "mla_w_q_b": _heads_split(_cols_full(gw["mla_w_q_b"]), MLA_NOPE, MLA_ROPE),
        "mla_w_kv_b": _heads_split(_cols_full(gw["mla_w_kv_b"]), MLA_NOPE, MLA_V),
    }
    P = {n: args[n] for n, _ in REPL_W}
    P["ssd_conv_w"] = _cols_full(conv_g[:, :4] + conv_g[:, 4:])

    lands = [lax.dynamic_update_slice(jnp.zeros((N_DEV,) + shards[n].shape, BF16), shards[n][None], (me, 0, 0)) for n in LATE]
    late_sems = _split_start([shards[n] for n in LATE], lands, _plan_broadcast, N_DEV - 1, "gather_late_start")

    def late_weights(after):
        _, got = _split_wait(*late_sems[:4], after, _plan_broadcast, "gather_late_wait")
        lw = dict(zip(LATE, got))
        return {"w_out": rows_full(lw["w_out"]), "w_ple_gate": rows_full(lw["w_ple_gate"]),
                "w_ple_proj": _cols_full(lw["w_ple_proj"]), "w_ffn_gate": lw["w_ffn_gate"], "w_ffn_up": lw["w_ffn_up"],
                "w_ffn_down": lw["w_ffn_down"]}

    def to_blocks(n, g):
        if n in OWNER_BLOCKED:
            return g
        if n in ROW_SHARDED:
            return g.reshape(N_DEV, -1, g.shape[1])
        if n == "w_in":
            g = _win_unpad(g)
        elif n == "mla_w_q_b":
            g = _heads_merge(g, MLA_NOPE, MLA_ROPE)
        elif n == "mla_w_kv_b":
            g = _heads_merge(g, MLA_NOPE, MLA_V)
        return _cols_split(g)

    flight = {}

    def ffn_grads(G):
        gl = [to_blocks(n, G[n]) for n in LATE_GRADS]
        flight["pairs"] = _split_start(gl, [lax.empty((4,) + g.shape[1:], g.dtype) for g in gl], _plan_pairs, 4, "pairs_start")
        return flight["pairs"][4]

    def mid(after):
        gl, from_sibling = _split_wait(*flight["pairs"][:4], after, _plan_pairs, "pairs_wait")
        sums = [_pair_sum(g, r, core_arr, "pair_sum_" + n) for n, g, r in zip(LATE_GRADS, gl, from_sibling)]
        flight["chips"] = _split_start(sums, [jnp.zeros(s.shape, s.dtype) for s in sums], _plan_chips, 3, "chips_start")
        return flight["chips"][4]

    grad_x, G, packed = _local_step(x[0], p[0, 0], positions[0], loss_target[0], W, P,
                                    comm={"token0": late_sems[4], "late_weights": late_weights, "ffn_grads": ffn_grads, "mid": mid})

    wmv = lambda n: (args[n][0], args["m_" + n][0], args["v_" + n][0])
    sums, recv = _split_wait(*flight["chips"][:4], grad_x, _plan_chips, "chips_wait")
    big_out = {n: _adam(r, *wmv(n), "adam_" + n, own=s, own_idx=chip_arr) for n, s, r in zip(LATE_GRADS, sums, recv)}

    glist = [to_blocks(n, G[n]) for n in LAST_GRADS]
    from_sibling = _pair_exchange(glist, "exchange_pairs")
    sums = [_pair_sum(g, r, core_arr, "pair_sum_" + n) for n, g, r in zip(LAST_GRADS, glist, from_sibling)]
    recv = _chip_exchange(sums, "exchange_chips")
    big_out.update({n: _adam(r, *wmv(n), "adam_" + n) for n, r in zip(LAST_GRADS, recv)})

    small_all = _all_gather(packed, "gather_small")
    conv_sum, loss_row, small_out = _adam_small(small_all, [(args[n], args["m_" + n], args["v_" + n]) for n, _ in REPL_W])
    conv_grad = lax.dynamic_slice_in_dim(conv_sum, me * 192, 192, axis=1)
    conv_out = _adam(conv_grad[None], conv_sh, m_ssd_conv_w[0], v_ssd_conv_w[0], "adam_conv")
    small_map = {n: small_out[i] for i, (n, _) in enumerate(REPL_W)}

    def outputs(idx):
        res = []
        for n in WEIGHT_ORDER:
            if n == "ssd_conv_w":
                res.append(conv_out[idx][None])
            elif n in big_out:
                res.append(big_out[n][idx][None])
            else:
                res.append(small_map[n][idx])
        return res

    return (loss_row[0, 0], grad_x[None], *outputs(0), *outputs(1), *outputs(2), *outputs(3))
```

```python
import functools
import math

import numpy as np
import jax
import jax.numpy as jnp
from jax import lax
from jax.experimental import pallas as pl
from jax.experimental.pallas import tpu as pltpu

F32 = jnp.float32
BF16 = jnp.bfloat16
HI = lax.Precision.HIGHEST

N_DEV = 8
D_MODEL = 1024
PLE_DIM = 256
SSD_HEADS = 16
SSD_HEAD_DIM = 64
SSD_INNER = 1024
SSD_STATE = 128
SSD_XBC = 1536
SSD_CHUNK = 128
MLA_HEADS = 16
MLA_Q_RANK = 384
MLA_KV_RANK = 256
MLA_NOPE = 64
MLA_ROPE = 32
MLA_V = 64
ROPE_BASE = 10000.0
D_FF = 2816
IN_WIDTH = 3248
IN_PAD = 3456
ALPHA = 2.0 ** 0.25
EPS = 1e-6
LN_EPS = 1e-5
ATT_SCALE = 1.0 / math.sqrt(MLA_NOPE + MLA_ROPE)
ADAM_LR, ADAM_B1, ADAM_B2, ADAM_EPS, ADAM_WD, ADAM_STEP = 0.001, 0.9, 0.999, 1e-08, 0.01, 10

LANE = 128
MXU_DIM = 256
MM_TM, MM_TN, MM_TK = 1024, 1152, 2048
ROW_TILE = 256
ATT_TQ = 256

GRAD_DT = BF16

BIG = ("w_in", "mla_w_q_b", "mla_w_kv_b", "w_out", "w_ffn_gate", "w_ffn_up", "w_ffn_down", "w_ple_gate", "w_ple_proj")
EARLY = ("w_in", "mla_w_q_b", "mla_w_kv_b")
LATE = ("w_out", "w_ffn_gate", "w_ffn_up", "w_ffn_down", "w_ple_gate", "w_ple_proj")
LATE_GRADS = ("w_ffn_gate", "w_ffn_up", "w_ffn_down", "w_ple_gate", "w_ple_proj", "w_out")
LAST_GRADS = ("w_in", "mla_w_q_b", "mla_w_kv_b")
ROW_SHARDED = ("w_out", "w_ffn_down", "w_ple_gate")
OWNER_BLOCKED = ("w_ffn_gate", "w_ffn_up", "w_ffn_down")
WEIGHT_ORDER = ("w_in", "ssd_conv_w", "ssd_conv_b", "ssd_dt_bias", "ssd_A_log", "ssd_D", "ssd_norm_w", "mla_q_norm_w",
                "mla_w_q_b", "mla_kv_norm_w", "mla_w_kv_b", "mla_out_norm_w", "w_out", "ln_mix_g", "ln_mix_b",
                "w_ffn_gate", "w_ffn_up", "w_ffn_down", "w_ple_gate", "w_ple_proj", "ln_ffn_g", "ln_ffn_b")


def _tile(dim, cap, prefer=None):
    cands = [t for t in range(LANE, min(cap, dim) + 1, LANE) if dim % t == 0]
    if not cands:
        return dim
    if prefer is None:
        return max(cands)
    fill = lambda t: t / (MXU_DIM * -(-t // MXU_DIM))
    best = max(fill(t) for t in cands)
    return min((t for t in cands if fill(t) == best), key=lambda t: abs(t - prefer))


def _dot(a, b, dims=(((1,), (0,)), ((), ())), precision=None):
    return lax.dot_general(a, b, dims, preferred_element_type=F32, precision=precision)


_NT = (((1,), (1,)), ((), ()))
_TN = (((0,), (0,)), ((), ()))


def _mm(a, b, *, ta=False, tb=False, a_blk=None, b_blk=None, o_blk=None, add=None, out_dtype=F32, name):
    ka, ma = a.shape[-2:] if ta else a.shape[-2:][::-1]
    nb, kb = b.shape[-2:] if tb else b.shape[-2:][::-1]
    m_dim = N_DEV * ma if a_blk == "m" else ma
    k_dim = N_DEV * ka if a_blk == "k" else ka
    n_dim = N_DEV * nb if b_blk == "n" else nb
    assert k_dim == (N_DEV * kb if b_blk == "k" else kb)
    tm = ma if a_blk == "m" else (m_dim // N_DEV if o_blk == "m" else _tile(m_dim, MM_TM))
    tn = nb if b_blk == "n" else (n_dim // N_DEV if o_blk == "n" else _tile(n_dim, MM_TN, prefer=512))
    tk = ka if a_blk == "k" else (kb if b_blk == "k" else _tile(k_dim, MM_TK, prefer=MM_TK))
    nk = k_dim // tk
    dims = (((0 if ta else 1,), (1 if tb else 0,)), ((), ()))
    has_add = add is not None

    def spec(tile, idx, lead):
        if lead is None:
            return pl.BlockSpec(tile, idx)
        return pl.BlockSpec((None,) + tile, lambda i, j, k: (lead(i, j, k),) + idx(i, j, k))

    def a_idx(i, j, k):
        ii, kk = (0 if a_blk == "m" else i), (0 if a_blk == "k" else k)
        return (kk, ii) if ta else (ii, kk)

    def b_idx(i, j, k):
        jj, kk = (0 if b_blk == "n" else j), (0 if b_blk == "k" else k)
        return (jj, kk) if tb else (kk, jj)

    def o_idx(i, j, k):
        return (0 if o_blk == "m" else i, 0 if o_blk == "n" else j)

    pick = {"m": lambda i, j, k: i, "n": lambda i, j, k: j, "k": lambda i, j, k: k, None: None}
    a_spec = spec((tk, tm) if ta else (tm, tk), a_idx, pick[a_blk])
    b_spec = spec((tn, tk) if tb else (tk, tn), b_idx, pick[b_blk])
    o_spec = spec((tm, tn), o_idx, pick[o_blk])

    def body(*refs):
        if has_add:
            a_ref, b_ref, add_ref, o_ref = refs[:4]
        else:
            a_ref, b_ref, o_ref = refs[:3]
        part = _dot(a_ref[...].astype(BF16), b_ref[...].astype(BF16), dims)
        if nk == 1:
            o_ref[...] = ((part + add_ref[...]) if has_add else part).astype(o_ref.dtype)
            return
        acc = refs[-1]
        k = pl.program_id(2)

        @pl.when(k == 0)
        def _():
            acc[...] = (part + add_ref[...]) if has_add else part

        @pl.when(k > 0)
        def _():
            acc[...] += part

        @pl.when(k == nk - 1)
        def _():
            o_ref[...] = acc[...].astype(o_ref.dtype)

    if o_blk == "m":
        out_shape = (N_DEV, tm, n_dim)
    elif o_blk == "n":
        out_shape = (N_DEV, m_dim, tn)
    else:
        out_shape = (m_dim, n_dim)
    ins = [a, b] + ([add] if has_add else [])
    specs = [a_spec, b_spec] + ([pl.BlockSpec((tm, tn), lambda i, j, k: (i, j))] if has_add else [])
    return pl.pallas_call(
        body, name=name, grid=(m_dim // tm, n_dim // tn, nk), in_specs=specs, out_specs=o_spec,
        out_shape=jax.ShapeDtypeStruct(out_shape, out_dtype),
        scratch_shapes=[pltpu.VMEM((tm, tn), F32)] if nk > 1 else [],
        compiler_params=pltpu.CompilerParams(dimension_semantics=("parallel", "parallel", "arbitrary")),
    )(*ins)


def _rowwise(fn, rows, consts, out_widths, acc_widths=(), *, name, tr=ROW_TILE):
    row_arrays, row_specs = [], []
    first_arr = rows[0][0] if isinstance(rows[0], tuple) else rows[0]
    s_dim = first_arr.shape[-2]
    tr = min(tr, s_dim)
    for r in rows:
        arr, width, cb = r if isinstance(r, tuple) else (r, r.shape[-1], 0)
        row_arrays.append(arr)
        if arr.ndim == 3:
            row_specs.append(pl.BlockSpec((None, tr, width), functools.partial(lambda i, k: (k, i, 0), k=cb)))
        else:
            row_specs.append(pl.BlockSpec((tr, width), functools.partial(lambda i, cb: (i, cb), cb=cb)))
    const_specs = [pl.BlockSpec(c.shape, lambda i: (0, 0)) for c in consts]
    nr, nc, no, na = len(rows), len(consts), len(out_widths), len(acc_widths)

    def body(*refs):
        ins = [r[...] for r in refs[:nr + nc]]
        res = fn(*ins)
        if not isinstance(res, (tuple, list)):
            res = (res,)
        out_refs = refs[nr + nc:nr + nc + no]
        acc_refs = refs[nr + nc + no:]
        for o_ref, val in zip(out_refs, res[:no]):
            o_ref[...] = val.astype(o_ref.dtype)
        first = pl.program_id(0) == 0
        for a_ref, val in zip(acc_refs, res[no:]):
            @pl.when(first)
            def _(a_ref=a_ref, val=val):
                a_ref[...] = val

            @pl.when(jnp.logical_not(first))
            def _(a_ref=a_ref, val=val):
                a_ref[...] += val

    outs = [w if isinstance(w, tuple) else (w, F32) for w in out_widths]
    out_shape = [jax.ShapeDtypeStruct((s_dim, w), dt) for w, dt in outs]
    out_shape += [jax.ShapeDtypeStruct((1, w), F32) for w in acc_widths]
    out_specs = [pl.BlockSpec((tr, w), lambda i: (i, 0)) for w, _ in outs]
    out_specs += [pl.BlockSpec((1, w), lambda i: (0, 0)) for w in acc_widths]
    res = pl.pallas_call(
        body, name=name, grid=(s_dim // tr,), in_specs=row_specs + const_specs, out_specs=out_specs, out_shape=out_shape,
        compiler_params=pltpu.CompilerParams(dimension_semantics=("arbitrary",)),
    )(*row_arrays, *consts)
    return res


def _colsum(v):
    return jnp.sum(v, axis=0, keepdims=True)


def _rms(u, g):
    return u * lax.rsqrt(jnp.mean(u * u, axis=-1, keepdims=True) + EPS) * g


def _ln(u, g, b):
    mu = jnp.mean(u, axis=-1, keepdims=True)
    d = u - mu
    var = jnp.mean(d * d, axis=-1, keepdims=True)
    return d * lax.rsqrt(var + LN_EPS) * g + b


def _sigmoid(v):
    return 1.0 / (1.0 + jnp.exp(-v))


def _silu(v):
    return v * _sigmoid(v)


def _softplus(v):
    y = jnp.exp(-jnp.abs(v))
    w = 1.0 + y
    log1p = jnp.where(w == 1.0, y, jnp.log(w) * y / jnp.where(w == 1.0, 1.0, w - 1.0))
    return jnp.maximum(v, 0.0) + log1p


def _gate_rms(y, z, w):
    return _rms(y * _silu(z), w)


def _vjp_rows(f):
    def fn(*args):
        prim, ct = args[:-1], args[-1]
        _, pull = jax.vjp(f, *prim)
        return pull(ct)
    return fn


def _conv_pre(cur, prev, w, b, first):
    row = lax.broadcasted_iota(jnp.int32, cur.shape, 0)
    acc = cur * w[3:4, :] + b
    for j in (1, 2, 3):
        tail = jnp.where(first, 0.0, pltpu.roll(prev, j, 0))
        acc = acc + jnp.where(row >= j, pltpu.roll(cur, j, 0), tail) * w[3 - j:4 - j, :]
    return acc


def _conv_fwd(u, ucb, w, b, name="conv_fwd"):
    s_dim, width = u.shape[0], w.shape[1]
    tr = min(ROW_TILE, s_dim)

    def body(cur_ref, prev_ref, w_ref, b_ref, o_ref):
        pre = _conv_pre(cur_ref[...], prev_ref[...], w_ref, b_ref[...], pl.program_id(0) == 0)
        o_ref[...] = _silu(pre)

    return pl.pallas_call(
        body, name=name, grid=(s_dim // tr,),
        in_specs=[pl.BlockSpec((tr, width), lambda i: (i, ucb)),
                  pl.BlockSpec((tr, width), lambda i: (jnp.maximum(i - 1, 0), ucb)),
                  pl.BlockSpec(w.shape, lambda i: (0, 0)), pl.BlockSpec(b.shape, lambda i: (0, 0))],
        out_specs=pl.BlockSpec((tr, width), lambda i: (i, 0)), out_shape=jax.ShapeDtypeStruct((s_dim, width), F32),
        compiler_params=pltpu.CompilerParams(dimension_semantics=("arbitrary",)),
    )(u, u, w, b)


def _conv_bwd_pre(u, ucb, w, b, dact, name="conv_bwd_pre"):
    s_dim, width = u.shape[0], w.shape[1]
    tr = min(ROW_TILE, s_dim)

    def body(cur_ref, prev_ref, w_ref, b_ref, d_ref, da_ref, dw_ref, db_ref):
        first = pl.program_id(0) == 0
        cur, prev = cur_ref[...], prev_ref[...]
        pre = _conv_pre(cur, prev, w_ref, b_ref[...], first)
        sg = _sigmoid(pre)
        da = d_ref[...] * (sg * (1.0 + pre * (1.0 - sg)))
        da_ref[...] = da
        row = lax.broadcasted_iota(jnp.int32, cur.shape, 0)

        @pl.when(first)
        def _():
            dw_ref[...] = jnp.zeros_like(dw_ref)
            db_ref[...] = jnp.zeros_like(db_ref)

        db_ref[...] += _colsum(da)
        dw_ref[3:4, :] += _colsum(da * cur)
        for j in (1, 2, 3):
            tail = jnp.where(first, 0.0, pltpu.roll(prev, j, 0))
            sh = jnp.where(row >= j, pltpu.roll(cur, j, 0), tail)
            dw_ref[3 - j:4 - j, :] += _colsum(da * sh)

    return pl.pallas_call(
        body, name=name, grid=(s_dim // tr,),
        in_specs=[pl.BlockSpec((tr, width), lambda i: (i, ucb)),
                  pl.BlockSpec((tr, width), lambda i: (jnp.maximum(i - 1, 0), ucb)),
                  pl.BlockSpec(w.shape, lambda i: (0, 0)), pl.BlockSpec(b.shape, lambda i: (0, 0)),
                  pl.BlockSpec((tr, width), lambda i: (i, 0))],
        out_specs=[pl.BlockSpec((tr, width), lambda i: (i, 0)), pl.BlockSpec(w.shape, lambda i: (0, 0)),
                   pl.BlockSpec(b.shape, lambda i: (0, 0))],
        out_shape=[jax.ShapeDtypeStruct((s_dim, width), F32), jax.ShapeDtypeStruct(w.shape, F32),
                   jax.ShapeDtypeStruct(b.shape, F32)],
        compiler_params=pltpu.CompilerParams(dimension_semantics=("arbitrary",)),
    )(u, u, w, b, dact)


def _conv_bwd_in(da, w, name="conv_bwd_in"):
    s_dim, width = da.shape
    tr = min(ROW_TILE, s_dim)
    n = s_dim // tr

    def body(cur_ref, nxt_ref, w_ref, o_ref):
        last = pl.program_id(0) == n - 1
        cur, nxt = cur_ref[...], nxt_ref[...]
        row = lax.broadcasted_iota(jnp.int32, cur.shape, 0)
        acc = cur * w_ref[3:4, :]
        for j in (1, 2, 3):
            head = jnp.where(last, 0.0, pltpu.roll(nxt, tr - j, 0))
            acc = acc + jnp.where(row < tr - j, pltpu.roll(cur, tr - j, 0), head) * w_ref[3 - j:4 - j, :]
        o_ref[...] = acc.astype(o_ref.dtype)

    return pl.pallas_call(
        body, name=name, grid=(n,),
        in_specs=[pl.BlockSpec((tr, width), lambda i: (i, 0)), pl.BlockSpec((tr, width), lambda i: (jnp.minimum(i + 1, n - 1), 0)),
                  pl.BlockSpec(w.shape, lambda i: (0, 0))],
        out_specs=pl.BlockSpec((tr, width), lambda i: (i, 0)), out_shape=jax.ShapeDtypeStruct((s_dim, width), BF16),
        compiler_params=pltpu.CompilerParams(dimension_semantics=("arbitrary",)),
    )(da, da, w)


def _sel_dot(a, sel, pieces, dims=(((1,), (0,)), ((), ())), sel_left=False):
    sel = sel.astype(BF16)
    acc, rest = None, a
    for _ in range(pieces):
        piece = rest.astype(BF16)
        rest = rest - piece.astype(F32)
        part = _dot(sel, piece, dims) if sel_left else _dot(piece, sel, dims)
        acc = part if acc is None else acc + part
    return acc


def _ssd_consts():
    L = SSD_CHUNK
    tri = np.tril(np.ones((L, L), np.float32))
    expand = np.zeros((LANE, SSD_INNER), np.float32)
    expand128 = np.zeros((LANE, SSD_HEADS * LANE), np.float32)
    for h in range(SSD_HEADS):
        expand[h, h * SSD_HEAD_DIM:(h + 1) * SSD_HEAD_DIM] = 1.0
        expand128[h, h * LANE:(h + 1) * LANE] = 1.0
    return jnp.asarray(tri), jnp.asarray(expand), jnp.asarray(expand128), jnp.asarray(expand.T.copy())


def _ssd_prep(dt_ref, bias_ref, alog_ref, tri_ref, exp_ref, exp128_ref, cs_s, cst_s, ex_s, csx_s):
    L = SSD_CHUNK
    dt = _softplus(dt_ref[...] + bias_ref[...])
    a = -jnp.exp(alog_ref[...])
    cs = _sel_dot(dt * a, tri_ref[...], 3, sel_left=True)
    cs_s[...] = cs
    cst_s[...] = cs.T
    last = cs_s[L - 1:L, :]
    expand = exp_ref[...]
    ex_s[...] = _sel_dot(jnp.exp(cs), expand, 2)
    f_x = _sel_dot(jnp.exp(last - cs), expand, 2)
    dt_x = _sel_dot(dt, expand, 2)
    csx_s[...] = _sel_dot(cs, exp128_ref[...], 3)
    t_x = ex_s[L - 1:L, :]
    return dt, a, dt_x, f_x, t_x


def _decay_matrix(csx_s, cst_s, h, tril):
    seg = csx_s[:, h * LANE:(h + 1) * LANE] - cst_s[h:h + 1, :]
    return jnp.exp(jnp.where(tril, seg, -jnp.inf))


def _ssd_fwd(xbca, dtr, dtcb, bias, alog, d_x, name="ssd_fwd"):
    s_dim = xbca.shape[0]
    L = SSD_CHUNK
    nc = s_dim // L
    tri, expand, expand128, _ = _ssd_consts()

    def body(xs_ref, b_ref, c_ref, dt_ref, bias_ref, alog_ref, dx_ref, tri_ref, exp_ref, exp128_ref,
             y_ref, st_ref, st_s, cs_s, cst_s, ex_s, csx_s):
        @pl.when(pl.program_id(0) == 0)
        def _():
            st_s[...] = jnp.zeros_like(st_s)

        dt, a, dt_x, f_x, t_x = _ssd_prep(dt_ref, bias_ref, alog_ref, tri_ref, exp_ref, exp128_ref, cs_s, cst_s, ex_s, csx_s)
        st_ref[0] = st_s[...]
        row = lax.broadcasted_iota(jnp.int32, (L, L), 0)
        col = lax.broadcasted_iota(jnp.int32, (L, L), 1)
        tril = row >= col
        low = col < SSD_HEAD_DIM
        for g in range(2):
            bg = b_ref[:, g * LANE:(g + 1) * LANE]
            cg = c_ref[:, g * LANE:(g + 1) * LANE].astype(BF16)
            gmat = _dot(cg, bg.astype(BF16), _NT)
            bgt = bg.T.astype(BF16)
            for jj in range(4):
                j = 4 * g + jj
                sl = slice(j * LANE, (j + 1) * LANE)
                xp = xs_ref[:, sl]
                x_dt = xp * dt_x[:, sl]
                xb = x_dt.astype(BF16)
                yd = []
                for e in range(2):
                    lm = _decay_matrix(csx_s, cst_s, 2 * j + e, tril)
                    yd.append(_dot((gmat * lm).astype(BF16), xb))
                stp = st_s[j]
                z = _dot(cg, stp.astype(BF16))
                y_ref[:, sl] = jnp.where(low, yd[0], yd[1]) + ex_s[:, sl] * z + dx_ref[:, sl] * xp
                xf = (x_dt * f_x[:, sl]).astype(BF16)
                st_s[j] = t_x[:, sl] * stp + _dot(bgt, xf)

    const = lambda shape: pl.BlockSpec(shape, lambda c: tuple(0 for _ in shape))
    return pl.pallas_call(
        body, name=name, grid=(nc,),
        in_specs=[pl.BlockSpec((L, 1024), lambda c: (c, 0)), pl.BlockSpec((L, 256), lambda c: (c, 4)),
                  pl.BlockSpec((L, 256), lambda c: (c, 5)), pl.BlockSpec((L, LANE), lambda c: (c, dtcb)),
                  const((1, LANE)), const((1, LANE)), const((1, 1024)), const((L, L)), const((LANE, 1024)),
                  const((LANE, 2048))],
        out_specs=[pl.BlockSpec((L, 1024), lambda c: (c, 0)), pl.BlockSpec((1, 8, LANE, LANE), lambda c: (c, 0, 0, 0))],
        out_shape=[jax.ShapeDtypeStruct((s_dim, 1024), F32), jax.ShapeDtypeStruct((nc, 8, LANE, LANE), F32)],
        scratch_shapes=[pltpu.VMEM((8, LANE, LANE), F32), pltpu.VMEM((L, LANE), F32), pltpu.VMEM((LANE, L), F32),
                        pltpu.VMEM((L, 1024), F32), pltpu.VMEM((L, 2048), F32)],
        compiler_params=pltpu.CompilerParams(dimension_semantics=("arbitrary",)),
    )(xbca, xbca, xbca, dtr, bias, alog, d_x, tri, expand, expand128)


def _ssd_bwd(xbca, dtr, dtcb, bias, alog, d_x, states, dy, name="ssd_bwd"):
    s_dim = xbca.shape[0]
    L = SSD_CHUNK
    nc = s_dim // L
    tri, expand, expand128, expand_t = _ssd_consts()

    def body(xs_ref, b_ref, c_ref, dt_ref, bias_ref, alog_ref, dx_ref, tri_ref, exp_ref, exp128_ref, expt_ref,
             st_ref, dy_ref, dxbc_ref, ddt_ref, dbias_ref, dalog_ref, dd_ref,
             dst_s, cs_s, cst_s, ex_s, csx_s, dcsx_s, ddtx_s, dcol_s, drow_s, dlast_s, dd_s):
        @pl.when(pl.program_id(0) == 0)
        def _():
            dst_s[...] = jnp.zeros_like(dst_s)
            dbias_ref[...] = jnp.zeros_like(dbias_ref)
            dalog_ref[...] = jnp.zeros_like(dalog_ref)
            dd_s[...] = jnp.zeros_like(dd_s)

        dt, a, dt_x, f_x, t_x = _ssd_prep(dt_ref, bias_ref, alog_ref, tri_ref, exp_ref, exp128_ref, cs_s, cst_s, ex_s, csx_s)
        row = lax.broadcasted_iota(jnp.int32, (L, L), 0)
        col = lax.broadcasted_iota(jnp.int32, (L, L), 1)
        tril = row >= col
        low = col < SSD_HEAD_DIM
        dcol_s[...] = jnp.zeros_like(dcol_s)
        drow_s[...] = jnp.zeros_like(drow_s)
        for g in range(2):
            bg = b_ref[:, g * LANE:(g + 1) * LANE]
            cg = c_ref[:, g * LANE:(g + 1) * LANE]
            bgb, cgb = bg.astype(BF16), cg.astype(BF16)
            gmat = _dot(cgb, bgb, _NT)
            d_g = jnp.zeros((L, L), F32)
            d_b = jnp.zeros((L, LANE), F32)
            d_c = jnp.zeros((L, LANE), F32)
            for jj in range(4):
                j = 4 * g + jj
                sl = slice(j * LANE, (j + 1) * LANE)
                xp = xs_ref[:, sl]
                dtp = dt_x[:, sl]
                x_dt = xp * dtp
                xb = x_dt.astype(BF16)
                dyp = dy_ref[:, sl]
                dd_s[:, sl] += _colsum(dyp * xp)
                d_xdt = jnp.zeros((L, LANE), F32)
                for e in range(2):
                    h = 2 * j + e
                    lm = _decay_matrix(csx_s, cst_s, h, tril)
                    m = gmat * lm
                    dye = jnp.where(low if e == 0 else jnp.logical_not(low), dyp, 0.0).astype(BF16)
                    d_m = jnp.where(tril, _dot(dye, xb, _NT), 0.0)
                    d_xdt = d_xdt + _dot(m.astype(BF16), dye, _TN)
                    d_g = d_g + d_m * lm
                    w = d_m * m
                    dcol_s[...] += jnp.where(col == h, jnp.sum(w, axis=1, keepdims=True), 0.0)
                    drow_s[...] += jnp.where(row == h, jnp.sum(w, axis=0, keepdims=True), 0.0)
                stp = st_ref[0, j]
                stb = stp.astype(BF16)
                dstn = dst_s[j]
                dstb = dstn.astype(BF16)
                e_p = ex_s[:, sl]
                f_p = f_x[:, sl]
                t_p = t_x[:, sl]
                z = _dot(cgb, stb)
                d_z = (e_p * dyp).astype(BF16)
                d_c = d_c + _dot(d_z, stb, _NT)
                d_xf = _dot(bgb, dstb)
                d_b = d_b + _dot((x_dt * f_p).astype(BF16), dstb, _NT)
                d_xdt = d_xdt + f_p * d_xf
                d_f = x_dt * d_xf * f_p
                dcsx_s[:, sl] = dyp * e_p * z - d_f
                dlast_s[:, sl] = _colsum(d_f) + _colsum(dstn * stp) * t_p
                dst_s[j] = _dot(cgb, d_z, _TN) + t_p * dstn
                dxbc_ref[:, sl] = dx_ref[:, sl] * dyp + d_xdt * dtp
                ddtx_s[:, sl] = d_xdt * xp
            d_gb = d_g.astype(BF16)
            dxbc_ref[:, 1024 + g * LANE:1024 + (g + 1) * LANE] = d_b + _dot(d_gb, cgb, _TN)
            dxbc_ref[:, 1280 + g * LANE:1280 + (g + 1) * LANE] = d_c + _dot(d_gb, bgb)

        expt = expt_ref[...]
        dlast = _sel_dot(jnp.broadcast_to(dlast_s[...], (8, 1024)), expt, 3)
        d_cs = dcol_s[...] - drow_s[...].T + _sel_dot(dcsx_s[...], expt, 3)
        rown = lax.broadcasted_iota(jnp.int32, (L, LANE), 0)
        d_cs = d_cs + jnp.where(rown == L - 1, jnp.sum(dlast, axis=0, keepdims=True) * 0.125, 0.0)
        d_da = _sel_dot(d_cs, tri_ref[...], 3, _TN, sel_left=True)
        d_dt = d_da * a + _sel_dot(ddtx_s[...], expt, 3)
        dalog_ref[...] += _colsum(d_da * dt) * a
        d_raw = d_dt * _sigmoid(dt_ref[...] + bias_ref[...])
        ddt_ref[...] = d_raw.astype(ddt_ref.dtype)
        dbias_ref[...] += _colsum(d_raw)
        dd8 = _sel_dot(jnp.broadcast_to(dd_s[...], (8, 1024)), expt, 3)
        dd_ref[...] = jnp.sum(dd8, axis=0, keepdims=True) * 0.125

    const = lambda shape: pl.BlockSpec(shape, lambda c: tuple(0 for _ in shape))
    rev = lambda cb: (lambda c: (nc - 1 - c, cb))
    return pl.pallas_call(
        body, name=name, grid=(nc,),
        in_specs=[pl.BlockSpec((L, 1024), rev(0)), pl.BlockSpec((L, 256), rev(4)), pl.BlockSpec((L, 256), rev(5)),
                  pl.BlockSpec((L, LANE), rev(dtcb)), const((1, LANE)), const((1, LANE)), const((1, 1024)), const((L, L)),
                  const((LANE, 1024)), const((LANE, 2048)), const((1024, LANE)),
                  pl.BlockSpec((1, 8, LANE, LANE), lambda c: (nc - 1 - c, 0, 0, 0)), pl.BlockSpec((L, 1024), rev(0))],
        out_specs=[pl.BlockSpec((L, SSD_XBC), rev(0)), pl.BlockSpec((L, LANE), rev(0)), const((1, LANE)), const((1, LANE)),
                   const((1, LANE))],
        out_shape=[jax.ShapeDtypeStruct((s_dim, SSD_XBC), F32), jax.ShapeDtypeStruct((s_dim, LANE), BF16),
                   jax.ShapeDtypeStruct((1, LANE), F32), jax.ShapeDtypeStruct((1, LANE), F32),
                   jax.ShapeDtypeStruct((1, LANE), F32)],
        scratch_shapes=[pltpu.VMEM((8, LANE, LANE), F32), pltpu.VMEM((L, LANE), F32), pltpu.VMEM((LANE, L), F32),
                        pltpu.VMEM((L, 1024), F32), pltpu.VMEM((L, 2048), F32), pltpu.VMEM((L, 1024), F32),
                        pltpu.VMEM((L, 1024), F32), pltpu.VMEM((L, LANE), F32), pltpu.VMEM((LANE, L), F32),
                        pltpu.VMEM((1, 1024), F32), pltpu.VMEM((1, 1024), F32)],
        compiler_params=pltpu.CompilerParams(dimension_semantics=("arbitrary",)),
    )(xbca, xbca, xbca, dtr, bias, alog, d_x, tri, expand, expand128, expand_t, states, dy)


def _swap_halves(u):
    width = u.shape[1]
    lane = lax.broadcasted_iota(jnp.int32, u.shape, 1)
    return jnp.where(lane % MLA_ROPE < MLA_ROPE // 2, pltpu.roll(u, width - MLA_ROPE // 2, 1), pltpu.roll(u, MLA_ROPE // 2, 1))


def _rope_fwd_fn(u, cos, sin):
    return u * cos + _swap_halves(u) * sin


def _rope_bwd_fn(d, cos, sin):
    return d * cos + _swap_halves(d * sin)


def _spread4(v):
    return v + pltpu.roll(v, 32, 1) + pltpu.roll(v, 64, 1) + pltpu.roll(v, 96, 1)


def _att_masks(tq):
    lane = lax.broadcasted_iota(jnp.int32, (tq, LANE), 1)
    return lane // MLA_NOPE, lane // MLA_ROPE


def _att_tile(i, tq):
    klen = (i + 1) * tq
    qpos = i * tq + lax.broadcasted_iota(jnp.int32, (tq, klen), 0)
    kpos = lax.broadcasted_iota(jnp.int32, (tq, klen), 1)
    return slice(i * tq, (i + 1) * tq), klen, qpos >= kpos


def _att_qcat(qn_t, qr_t, par, e, half_id, grp_id):
    return jnp.concatenate([jnp.where(half_id == par, qn_t * ATT_SCALE, 0.0), jnp.where(grp_id == e, qr_t * ATT_SCALE, 0.0)],
                           axis=1).astype(BF16)


def _att_exp(qcat, kcat, causal):
    s = jnp.where(causal, _dot(qcat, kcat, _NT), -jnp.inf)
    e = jnp.exp(s - jnp.max(s, axis=1, keepdims=True))
    return e, 1.0 / jnp.sum(e, axis=1, keepdims=True)


def _att_specs(s_dim):
    col = lambda f: pl.BlockSpec((s_dim, LANE), lambda j: (0, f(j)))
    return [col(lambda j: j), col(lambda j: j // 2), col(lambda j: j), col(lambda j: 0), col(lambda j: 8 + j)]


def _att_fwd(q, qr, kv, krt, name="att_fwd"):
    s_dim = q.shape[0]
    tq = min(ATT_TQ, s_dim)

    def body(qn_ref, qr_ref, kn_ref, krt_ref, v_ref, o_ref, kcat_s, vb_s):
        e0 = 2 * (pl.program_id(0) % 2)
        half_id, grp_id = _att_masks(tq)
        kcat_s[...] = jnp.concatenate([kn_ref[...], krt_ref[...]], axis=1).astype(BF16)
        vb_s[...] = v_ref[...].astype(BF16)
        for i in range(s_dim // tq):
            rows, klen, causal = _att_tile(i, tq)
            qn_t, qr_t = qn_ref[rows, :], qr_ref[rows, :]
            outs = []
            for par in range(2):
                qcat = _att_qcat(qn_t, qr_t, par, e0 + par, half_id, grp_id)
                e, inv_l = _att_exp(qcat, kcat_s[0:klen, :], causal)
                outs.append(_dot(e.astype(BF16), vb_s[0:klen, :]) * inv_l)
            o_ref[rows, :] = jnp.where(half_id == 0, outs[0], outs[1])

    return pl.pallas_call(
        body, name=name, grid=(MLA_HEADS // 2,), in_specs=_att_specs(s_dim),
        out_specs=pl.BlockSpec((s_dim, LANE), lambda j: (0, j)), out_shape=jax.ShapeDtypeStruct((s_dim, 1024), F32),
        scratch_shapes=[pltpu.VMEM((s_dim, 2 * LANE), BF16), pltpu.VMEM((s_dim, LANE), BF16)],
        compiler_params=pltpu.CompilerParams(dimension_semantics=("parallel",)),
    )(q, qr, kv, krt, kv)


def _att_bwd(q, qr, kv, krt, o, do, name="att_bwd"):
    s_dim = q.shape[0]
    tq = min(ATT_TQ, s_dim)

    def body(qn_ref, qr_ref, kn_ref, krt_ref, v_ref, o_ref, do_ref, dqn_ref, dqr_ref, dkn_ref, dv_ref, dkrt_ref,
             kcat_s, vb_s):
        e0 = 2 * (pl.program_id(0) % 2)
        half_id, grp_id = _att_masks(tq)
        kcat_s[...] = jnp.concatenate([kn_ref[...], krt_ref[...]], axis=1).astype(BF16)
        vb_s[...] = v_ref[...].astype(BF16)
        dkn_ref[...] = jnp.zeros_like(dkn_ref)
        dv_ref[...] = jnp.zeros_like(dv_ref)
        dkrt_ref[...] = jnp.zeros_like(dkrt_ref)
        for i in range(s_dim // tq):
            rows, klen, causal = _att_tile(i, tq)
            qn_t, qr_t, o_t, do_t = qn_ref[rows, :], qr_ref[rows, :], o_ref[rows, :], do_ref[rows, :]
            dqn = jnp.zeros((tq, LANE), F32)
            dqr = jnp.zeros((tq, LANE), F32)
            for par in range(2):
                qcat = _att_qcat(qn_t, qr_t, par, e0 + par, half_id, grp_id)
                e, inv_l = _att_exp(qcat, kcat_s[0:klen, :], causal)
                p = e * inv_l
                dom = jnp.where(half_id == par, do_t, 0.0)
                domb = dom.astype(BF16)
                d_p = _dot(domb, vb_s[0:klen, :], _NT)
                d_row = jnp.sum(dom * o_t, axis=1, keepdims=True)
                d_s = (p * (d_p - d_row)).astype(BF16)
                dqcat = _dot(d_s, kcat_s[0:klen, :]) * ATT_SCALE
                dqn = dqn + jnp.where(half_id == par, dqcat[:, :LANE], 0.0)
                dqr = dqr + jnp.where(grp_id == e0 + par, dqcat[:, LANE:], 0.0)
                dkcat = _dot(d_s, qcat, _TN)
                dkn_ref[0:klen, :] += dkcat[:, :LANE]
                dkrt_ref[0:klen, :] += dkcat[:, LANE:]
                dv_ref[0:klen, :] += _dot(p.astype(BF16), domb, _TN)
            dqn_ref[rows, :] = dqn.astype(dqn_ref.dtype)
            dqr_ref[rows, :] = dqr

    col = lambda f: pl.BlockSpec((s_dim, LANE), lambda j: (0, f(j)))
    return pl.pallas_call(
        body, name=name, grid=(MLA_HEADS // 2,), in_specs=_att_specs(s_dim) + [col(lambda j: j), col(lambda j: j)],
        out_specs=[col(lambda j: j), pl.BlockSpec((None, s_dim, LANE), lambda j: (j % 2, 0, j // 2)), col(lambda j: j),
                   col(lambda j: j), pl.BlockSpec((None, s_dim, LANE), lambda j: (j, 0, 0))],
        out_shape=[jax.ShapeDtypeStruct((s_dim, 1024), BF16), jax.ShapeDtypeStruct((2, s_dim, 512), F32),
                   jax.ShapeDtypeStruct((s_dim, 1024), F32), jax.ShapeDtypeStruct((s_dim, 1024), F32),
                   jax.ShapeDtypeStruct((MLA_HEADS // 2, s_dim, LANE), F32)],
        scratch_shapes=[pltpu.VMEM((s_dim, 2 * LANE), BF16), pltpu.VMEM((s_dim, LANE), BF16)],
        compiler_params=pltpu.CompilerParams(dimension_semantics=("parallel",)),
    )(q, qr, kv, krt, kv, o, do)


def _all_gather(x, name):
    rows, width = x.shape

    def body(x_ref, out_ref, send_sems, recv_sems, local_sem):
        x_i, y_i, c_i = lax.axis_index("x"), lax.axis_index("y"), lax.axis_index("c")
        me, sibling = (x_i, y_i, c_i), (x_i, y_i, 1 - c_i)
        chips = [(1 - x_i, y_i), (x_i, 1 - y_i), (1 - x_i, 1 - y_i)]

        def slot(px, py, pc):
            return out_ref.at[4 * px + 2 * py + pc]

        def copy(k, block, to, src=None):
            return pltpu.make_async_remote_copy(
                src_ref=slot(*block) if src is None else src, dst_ref=slot(*block), send_sem=send_sems.at[k],
                recv_sem=recv_sems.at[k], device_id=to, device_id_type=pl.DeviceIdType.MESH)

        mine = pltpu.make_async_copy(x_ref, slot(*me), local_sem)
        mine.start()
        first = [copy(0, me, sibling, src=x_ref)]
        first += [copy(1 + j, me, (*chip, c_i), src=x_ref) for j, chip in enumerate(chips)]
        for cp in first:
            cp.start()
        passed = [copy(4 + j, (*chip, c_i), sibling) for j, chip in enumerate(chips)]
        for j, chip in enumerate(chips):
            copy(1 + j, (*chip, c_i), me).wait_recv()
            passed[j].start()
        copy(0, sibling, me).wait_recv()
        for j, chip in enumerate(chips):
            copy(4 + j, (*chip, 1 - c_i), me).wait_recv()
        for cp in first + passed:
            cp.wait_send()
        mine.wait()

    return pl.pallas_call(
        body, name=name, out_shape=jax.ShapeDtypeStruct((N_DEV, rows, width), x.dtype),
        in_specs=[pl.BlockSpec(memory_space=pl.ANY)], out_specs=pl.BlockSpec(memory_space=pl.ANY),
        scratch_shapes=[pltpu.SemaphoreType.DMA((7,)), pltpu.SemaphoreType.DMA((7,)), pltpu.SemaphoreType.DMA],
    )(x)


def _gather_many(shards, name):
    n_arr = len(shards)

    def body(*refs):
        x_refs, out_refs = refs[:n_arr], refs[n_arr:2 * n_arr]
        send_sems, recv_sems, local_sems = refs[2 * n_arr:]
        x_i, y_i, c_i = lax.axis_index("x"), lax.axis_index("y"), lax.axis_index("c")
        me, sibling = (x_i, y_i, c_i), (x_i, y_i, 1 - c_i)
        chips = [(1 - x_i, y_i), (x_i, 1 - y_i), (1 - x_i, 1 - y_i)]

        def copy(a, k, block, to, src=None):
            slot = out_refs[a].at[4 * block[0] + 2 * block[1] + block[2]]
            return pltpu.make_async_remote_copy(
                src_ref=slot if src is None else src, dst_ref=slot, send_sem=send_sems.at[a, k],
                recv_sem=recv_sems.at[a, k], device_id=to, device_id_type=pl.DeviceIdType.MESH)

        mine, first, passed = [], [], []
        for a in range(n_arr):
            mine.append(pltpu.make_async_copy(x_refs[a], out_refs[a].at[4 * x_i + 2 * y_i + c_i], local_sems.at[a]))
            mine[a].start()
            first.append([copy(a, 0, me, sibling, src=x_refs[a])]
                         + [copy(a, 1 + j, me, (*chip, c_i), src=x_refs[a]) for j, chip in enumerate(chips)])
            for cp in first[a]:
                cp.start()
            passed.append([copy(a, 4 + j, (*chip, c_i), sibling) for j, chip in enumerate(chips)])
        for j, chip in enumerate(chips):
            for a in range(n_arr):
                copy(a, 1 + j, (*chip, c_i), me).wait_recv()
                passed[a][j].start()
        for a in range(n_arr):
            copy(a, 0, sibling, me).wait_recv()
            for j, chip in enumerate(chips):
                copy(a, 4 + j, (*chip, 1 - c_i), me).wait_recv()
        for a in range(n_arr):
            for cp in first[a] + passed[a]:
                cp.wait_send()
            mine[a].wait()

    any_spec = pl.BlockSpec(memory_space=pl.ANY)
    return pl.pallas_call(
        body, name=name, out_shape=[jax.ShapeDtypeStruct((N_DEV,) + x.shape, x.dtype) for x in shards],
        in_specs=[any_spec] * n_arr, out_specs=[any_spec] * n_arr,
        scratch_shapes=[pltpu.SemaphoreType.DMA((n_arr, 7)), pltpu.SemaphoreType.DMA((n_arr, 7)),
                        pltpu.SemaphoreType.DMA((n_arr,))],
    )(*shards)


def _pair_exchange(grads, name):
    n_arr = len(grads)

    def body(*refs):
        g_refs, out_refs = refs[:n_arr], refs[n_arr:2 * n_arr]
        send_sems, recv_sems = refs[2 * n_arr:]
        x_i, y_i, c_i = lax.axis_index("x"), lax.axis_index("y"), lax.axis_index("c")
        copies = []
        for a in range(n_arr):
            for chip in range(4):
                copies.append(pltpu.make_async_remote_copy(
                    src_ref=g_refs[a].at[2 * chip + (1 - c_i)], dst_ref=out_refs[a].at[chip], send_sem=send_sems.at[a, chip],
                    recv_sem=recv_sems.at[a, chip], device_id=(x_i, y_i, 1 - c_i), device_id_type=pl.DeviceIdType.MESH))
        for cp in copies:
            cp.start()
        for cp in copies:
            cp.wait_recv()
        for cp in copies:
            cp.wait_send()

    any_spec = pl.BlockSpec(memory_space=pl.ANY)
    return pl.pallas_call(
        body, name=name, out_shape=[jax.ShapeDtypeStruct((4,) + g.shape[1:], g.dtype) for g in grads],
        in_specs=[any_spec] * n_arr, out_specs=[any_spec] * n_arr,
        scratch_shapes=[pltpu.SemaphoreType.DMA((n_arr, 4)), pltpu.SemaphoreType.DMA((n_arr, 4))],
    )(*grads)


def _chip_exchange(sums, name):
    n_arr = len(sums)

    def body(*refs):
        s_refs, out_refs = refs[:n_arr], refs[n_arr:2 * n_arr]
        send_sems, recv_sems, local_sems = refs[2 * n_arr:]
        x_i, y_i, c_i = lax.axis_index("x"), lax.axis_index("y"), lax.axis_index("c")
        my_chip = 2 * x_i + y_i
        copies, local = [], []
        for a in range(n_arr):
            local.append(pltpu.make_async_copy(s_refs[a].at[my_chip], out_refs[a].at[my_chip], local_sems.at[a]))
            local[a].start()
            for k in range(1, 4):
                px, py = x_i ^ (k >> 1), y_i ^ (k & 1)
                copies.append(pltpu.make_async_remote_copy(
                    src_ref=s_refs[a].at[2 * px + py], dst_ref=out_refs[a].at[my_chip], send_sem=send_sems.at[a, k - 1],
                    recv_sem=recv_sems.at[a, k - 1], device_id=(px, py, c_i), device_id_type=pl.DeviceIdType.MESH))
        for cp in copies:
            cp.start()
        for cp in copies:
            cp.wait_recv()
        for cp in copies:
            cp.wait_send()
        for cp in local:
            cp.wait()

    any_spec = pl.BlockSpec(memory_space=pl.ANY)
    return pl.pallas_call(
        body, name=name, out_shape=[jax.ShapeDtypeStruct(s.shape, s.dtype) for s in sums],
        in_specs=[any_spec] * n_arr, out_specs=[any_spec] * n_arr,
        scratch_shapes=[pltpu.SemaphoreType.DMA((n_arr, 3)), pltpu.SemaphoreType.DMA((n_arr, 3)),
                        pltpu.SemaphoreType.DMA((n_arr,))],
    )(*sums)


_HBM = pl.BlockSpec(memory_space=pltpu.HBM)
_SEM = pl.BlockSpec(memory_space=pltpu.SEMAPHORE)


def _plan_copies(plan, src_refs, land_refs, send_sems, recv_sems):
    copies = []
    for s_ref, l_ref in zip(src_refs, land_refs):
        for src, dst, peer in plan(s_ref, l_ref):
            k = len(copies)
            copies.append(pltpu.make_async_remote_copy(
                src_ref=src, dst_ref=dst, send_sem=send_sems.at[k], recv_sem=recv_sems.at[k], device_id=peer,
                device_id_type=pl.DeviceIdType.MESH))
    return copies


def _split_start(srcs, lands, plan, n_copy, name):
    n = len(srcs)

    def body(*refs):
        for cp in _plan_copies(plan, refs[:n], refs[n:2 * n], refs[2 * n], refs[2 * n + 1]):
            cp.start()
        refs[-1][...] = jnp.zeros_like(refs[-1])

    sems = pltpu.SemaphoreType.DMA((n * n_copy,))
    res = pl.pallas_call(
        body, name=name,
        out_shape=(sems, sems, *[pltpu.HBM(a.shape, a.dtype) for a in list(srcs) + list(lands)],
                   jax.ShapeDtypeStruct((8, LANE), F32)),
        in_specs=[_HBM] * (2 * n), out_specs=(_SEM, _SEM, *[_HBM] * (2 * n), pl.BlockSpec(memory_space=pltpu.VMEM)),
        input_output_aliases={i: 2 + i for i in range(2 * n)},
        compiler_params=pltpu.CompilerParams(has_side_effects=pltpu.SideEffectType.DATAFLOW_SIDE_EFFECTING),
    )(*[pltpu.with_memory_space_constraint(a, pltpu.HBM) for a in list(srcs) + list(lands)])
    return res[0], res[1], list(res[2:2 + n]), list(res[2 + n:2 + 2 * n]), res[-1]


def _split_wait(send_sems, recv_sems, srcs, lands, after, plan, name):
    n = len(srcs)

    def body(*refs):
        copies = _plan_copies(plan, refs[:n], refs[n:2 * n], refs[2 * n], refs[2 * n + 1])
        for cp in copies:
            cp.wait_send()
        for cp in copies:
            cp.wait_recv()

    res = pl.pallas_call(
        body, name=name, out_shape=tuple(pltpu.HBM(a.shape, a.dtype) for a in list(srcs) + list(lands)),
        in_specs=[_HBM] * (2 * n) + [_SEM, _SEM, pl.BlockSpec(memory_space=pl.ANY)], out_specs=tuple([_HBM] * (2 * n)),
        input_output_aliases={i: i for i in range(2 * n)},
        compiler_params=pltpu.CompilerParams(has_side_effects=pltpu.SideEffectType.DATAFLOW_SIDE_EFFECTING),
    )(*srcs, *lands, send_sems, recv_sems, after)
    return list(res[:n]), list(res[n:])


def _plan_broadcast(src, land):
    x_i, y_i, c_i = lax.axis_index("x"), lax.axis_index("y"), lax.axis_index("c")
    me = 4 * x_i + 2 * y_i + c_i
    return [(src, land.at[me], (x_i ^ (k >> 2), y_i ^ ((k >> 1) & 1), c_i ^ (k & 1))) for k in range(1, N_DEV)]


def _plan_scatter(src, land):
    x_i, y_i, c_i = lax.axis_index("x"), lax.axis_index("y"), lax.axis_index("c")
    me = 4 * x_i + 2 * y_i + c_i
    plan = []
    for k in range(1, N_DEV):
        px, py, pc = x_i ^ (k >> 2), y_i ^ ((k >> 1) & 1), c_i ^ (k & 1)
        plan.append((src.at[4 * px + 2 * py + pc], land.at[me], (px, py, pc)))
    return plan


def _pair_sum(g, recv, core, name):
    _, rows, cols = g.shape
    tr = ROW_TILE if rows % ROW_TILE == 0 else rows

    def body(core_ref, g_ref, r_ref, o_ref):
        o_ref[...] = (g_ref[...].astype(F32) + r_ref[...].astype(F32)).astype(o_ref.dtype)

    grid_spec = pltpu.PrefetchScalarGridSpec(
        num_scalar_prefetch=1, grid=(4, rows // tr),
        in_specs=[pl.BlockSpec((None, tr, cols), lambda k, i, core_ref: (2 * k + core_ref[0], i, 0)),
                  pl.BlockSpec((None, tr, cols), lambda k, i, core_ref: (k, i, 0))],
        out_specs=pl.BlockSpec((None, tr, cols), lambda k, i, core_ref: (k, i, 0)))
    return pl.pallas_call(body, name=name, grid_spec=grid_spec, out_shape=jax.ShapeDtypeStruct((4, rows, cols), g.dtype))(
        core, g, recv)


def _adam_math(g, w, m, v):
    m_new = ADAM_B1 * m + (1.0 - ADAM_B1) * g
    v_new = ADAM_B2 * v + (1.0 - ADAM_B2) * (g * g)
    m_hat = m_new / (1.0 - ADAM_B1 ** ADAM_STEP)
    v_hat = v_new / (1.0 - ADAM_B2 ** ADAM_STEP)
    return -ADAM_LR * (m_hat / (jnp.sqrt(v_hat) + ADAM_EPS) + ADAM_WD * w), m_new, v_new


def _adam(slots, w, m, v, name, own=None, own_idx=None):
    n_slot, rows, cols = slots.shape
    tr = ROW_TILE if rows % ROW_TILE == 0 else rows
    has_own = own is not None

    def body(*refs):
        refs = refs[1:] if has_own else refs
        if has_own:
            own_ref, refs = refs[0], refs[1:]
        s_ref, w_ref, m_ref, v_ref, g_ref, d_ref, mo_ref, vo_ref = refs
        g = own_ref[...].astype(F32) if has_own else s_ref[0].astype(F32)
        for k in range(0 if has_own else 1, n_slot):
            g = g + s_ref[k].astype(F32)
        g_ref[...] = g
        d_ref[...], mo_ref[...], vo_ref[...] = _adam_math(g, w_ref[...], m_ref[...], v_ref[...])

    spec = pl.BlockSpec((tr, cols), lambda i, *_: (i, 0))
    in_specs = [pl.BlockSpec((n_slot, tr, cols), lambda i, *_: (0, i, 0)), spec, spec, spec]
    if has_own:
        in_specs = [pl.BlockSpec((None, tr, cols), lambda i, idx: (idx[0], i, 0))] + in_specs
    grid_spec = pltpu.PrefetchScalarGridSpec(num_scalar_prefetch=1 if has_own else 0, grid=(rows // tr,), in_specs=in_specs,
                                             out_specs=[spec] * 4)
    ins = ([own_idx, own] if has_own else []) + [slots, w, m, v]
    return pl.pallas_call(
        body, name=name, grid_spec=grid_spec, out_shape=[jax.ShapeDtypeStruct((rows, cols), F32)] * 4,
        compiler_params=pltpu.CompilerParams(dimension_semantics=("parallel",)),
    )(*ins)


PACK_ROWS, PACK_W = 24, 1536
REPL_W = (("ssd_conv_b", 1536), ("ssd_dt_bias", 16), ("ssd_A_log", 16), ("ssd_D", 16), ("ssd_norm_w", 1024),
          ("mla_q_norm_w", 384), ("mla_kv_norm_w", 256), ("mla_out_norm_w", 1024), ("ln_mix_g", 1024),
          ("ln_mix_b", 1024), ("ln_ffn_g", 1024), ("ln_ffn_b", 1024))
LOSS_ROW = 4 + len(REPL_W)


def _pack_small(conv_w_grad, grads, loss, name="pack_small"):
    def body(*refs):
        cw_ref, g_refs, loss_ref, o_ref = refs[0], refs[1:1 + len(REPL_W)], refs[1 + len(REPL_W)], refs[-1]
        o_ref[...] = jnp.zeros_like(o_ref)
        o_ref[0:4, :] = cw_ref[...]
        for i, g_ref in enumerate(g_refs):
            o_ref[4 + i:5 + i, 0:g_ref.shape[1]] = g_ref[...]
        o_ref[LOSS_ROW:LOSS_ROW + 1, 0:LANE] = loss_ref[...]

    return pl.pallas_call(body, name=name, out_shape=jax.ShapeDtypeStruct((PACK_ROWS, PACK_W), F32))(conv_w_grad, *grads, loss)


def _adam_small(gathered, wmv, name="adam_small"):
    def body(*refs):
        s_ref = refs[0]
        in_refs = refs[1:1 + 3 * len(REPL_W)]
        cw_ref, loss_ref = refs[1 + 3 * len(REPL_W)], refs[2 + 3 * len(REPL_W)]
        out_refs = refs[3 + 3 * len(REPL_W):-1]
        tot = refs[-1]
        acc = s_ref[0]
        for k in range(1, N_DEV):
            acc = acc + s_ref[k]
        tot[...] = acc
        cw_ref[...] = tot[0:4, :]
        loss_ref[...] = tot[LOSS_ROW:LOSS_ROW + 1, 0:LANE]
        for i, (_, width) in enumerate(REPL_W):
            g = tot[4 + i:5 + i, 0:width]
            w_ref, m_ref, v_ref = in_refs[3 * i:3 * i + 3]
            g_ref, d_ref, mo_ref, vo_ref = out_refs[4 * i:4 * i + 4]
            g_ref[...] = g
            d_ref[...], mo_ref[...], vo_ref[...] = _adam_math(g, w_ref[...], m_ref[...], v_ref[...])

    flat_in = [a for triple in wmv for a in triple]
    out_shape = [jax.ShapeDtypeStruct((4, PACK_W), F32), jax.ShapeDtypeStruct((1, LANE), F32)]
    for _, width in REPL_W:
        out_shape += [jax.ShapeDtypeStruct((1, width), F32)] * 4
    res = pl.pallas_call(body, name=name, out_shape=out_shape, scratch_shapes=[pltpu.VMEM((PACK_ROWS, PACK_W), F32)])(
        gathered, *flat_in)
    return res[0], res[1], [res[2 + 4 * i:6 + 4 * i] for i in range(len(REPL_W))]


def _cols_full(g):
    return jnp.transpose(g, (1, 0, 2)).reshape(g.shape[1], -1)


def _cols_split(full):
    k_dim, n_dim = full.shape
    return jnp.transpose(full.reshape(k_dim, N_DEV, n_dim // N_DEV), (1, 0, 2))


PROJ_BLOCK = {"z": (1024, 0), "dt": (LANE, 8), "q_c": (MLA_Q_RANK, 3), "xbc": (SSD_XBC, 1), "kv_c": (MLA_KV_RANK, 12),
              "k_rope": (LANE, 26)}


def _win_pad(w):
    z = lambda n: jnp.zeros((w.shape[0], n), w.dtype)
    return jnp.concatenate([w[:, :1024], w[:, 2560:2576], z(112), w[:, 2576:2960], w[:, 1024:2560], w[:, 2960:3216],
                            w[:, 3216:3248], z(96)], axis=1)


def _win_unpad(w):
    return jnp.concatenate([w[:, :1024], w[:, 1536:3072], w[:, 1024:1040], w[:, 1152:1536], w[:, 3072:3328], w[:, 3328:3360]],
                           axis=1)


def _heads_split(w, a, b):
    k_dim = w.shape[0]
    w3 = w.reshape(k_dim, MLA_HEADS, a + b)
    return jnp.concatenate([w3[:, :, :a].reshape(k_dim, -1), w3[:, :, a:].reshape(k_dim, -1)], axis=1)


def _heads_merge(w, a, b):
    k_dim = w.shape[0]
    wa = w[:, :MLA_HEADS * a].reshape(k_dim, MLA_HEADS, a)
    wb = w[:, MLA_HEADS * a:].reshape(k_dim, MLA_HEADS, b)
    return jnp.concatenate([wa, wb], axis=2).reshape(k_dim, -1)


def _pad_lanes(v, width=LANE):
    return jnp.concatenate([v, jnp.zeros((v.shape[0], width - v.shape[1]), v.dtype)], axis=1)


def _local_step(x, p, positions, tgt, W, P, comm=None):
    comm = comm or {}
    zero_tok = jnp.zeros((8, LANE), F32)
    s_dim = x.shape[0]
    inv_freq = 1.0 / (ROPE_BASE ** (jnp.arange(0, MLA_ROPE, 2, dtype=F32) / MLA_ROPE))
    ang = positions.astype(F32)[:, None] * inv_freq
    cos, sin = jnp.cos(ang), jnp.sin(ang)
    cos32 = jnp.concatenate([cos, cos], axis=1)
    sin32 = jnp.concatenate([-sin, sin], axis=1)
    cos512, sin512 = jnp.tile(cos32, (1, 16)), jnp.tile(sin32, (1, 16))
    cos128, sin128 = jnp.tile(cos32, (1, 4)), jnp.tile(sin32, (1, 4))
    bias_p, alog_p = _pad_lanes(P["ssd_dt_bias"]), _pad_lanes(P["ssd_A_log"])
    d_x = jnp.repeat(P["ssd_D"], SSD_HEAD_DIM, axis=1)

    xb, pb = x.astype(BF16), p.astype(BF16)
    if "token0" in comm:
        xb = (x + comm["token0"][0, 0]).astype(BF16)
    proj = _mm(xb, W["w_in"], name="mm_in")
    z, qc, kvc, kr = [(proj,) + PROJ_BLOCK[n] for n in ("z", "q_c", "kv_c", "k_rope")]
    xbca = _conv_fwd(proj, PROJ_BLOCK["xbc"][1], P["ssd_conv_w"], P["ssd_conv_b"])
    y, states = _ssd_fwd(xbca, proj, PROJ_BLOCK["dt"][1], bias_p, alog_p, d_x)
    (yssd,) = _rowwise(_gate_rms, [y, z], [P["ssd_norm_w"]], [(1024, BF16)], name="ssd_gate_norm")
    (qn,) = _rowwise(_rms, [qc], [P["mla_q_norm_w"]], [(MLA_Q_RANK, BF16)], name="q_norm")
    (kvn,) = _rowwise(_rms, [kvc], [P["mla_kv_norm_w"]], [(MLA_KV_RANK, BF16)], name="kv_norm")
    q = _mm(qn, W["mla_w_q_b"], name="mm_q")
    kv = _mm(kvn, W["mla_w_kv_b"], name="mm_kv")
    (qr,) = _rowwise(_rope_fwd_fn, [(q, 512, 2), cos512, sin512], [], [512], name="rope_q")
    (krt,) = _rowwise(lambda u, c, s: _spread4(_rope_fwd_fn(u, c, s)), [kr, cos128, sin128], [], [LANE], name="rope_k")
    att = _att_fwd(q, qr, kv, krt)
    (ymla,) = _rowwise(_rms, [att], [P["mla_out_norm_w"]], [(1024, BF16)], name="out_norm")
    ycat = jnp.concatenate([yssd, ymla], axis=1)
    if "late_weights" in comm:
        W = {**W, **comm["late_weights"](ycat)}
    mix = _mm(ycat, W["w_out"], name="mm_out")
    f_h1 = lambda xv, mv, g, b: _ln(ALPHA * xv + mv, g, b)
    h1, h1b = _rowwise(lambda *a: (f_h1(*a),) * 2, [x, mix], [P["ln_mix_g"], P["ln_mix_b"]], [1024, (1024, BF16)],
                       name="ln_mix")
    fb = D_FF // N_DEV
    hg = _mm(h1b, W["w_ffn_gate"], b_blk="n", o_blk="n", out_dtype=BF16, name="mm_gate")
    hu = _mm(h1b, W["w_ffn_up"], b_blk="n", o_blk="n", out_dtype=BF16, name="mm_up")
    pg = _mm(h1b, W["w_ple_gate"], name="mm_ple_gate")
    pp = _mm(pb, W["w_ple_proj"], name="mm_ple")
    hg2, hu2 = hg.reshape(N_DEV * s_dim, fb), hu.reshape(N_DEV * s_dim, fb)
    (act,) = _rowwise(lambda g, u: _silu(g.astype(F32)) * u.astype(F32), [hg2, hu2], [], [(fb, BF16)], name="swiglu",
                      tr=512)
    act3 = act.reshape(N_DEV, s_dim, fb)
    ffn = _mm(act3, W["w_ffn_down"], a_blk="k", b_blk="k", name="mm_down")

    f_h2 = lambda hv, fv, pg, ppv, g, b: _ln(ALPHA * hv + fv + _sigmoid(pg) * ppv, g, b)

    def final_fn(hv, fv, pg, ppv, tv, g, b):
        h2, pull = jax.vjp(f_h2, hv, fv, pg, ppv, g, b)
        diff = h2 - tv
        loss = 0.5 * jnp.sum(jnp.mean(diff * diff, axis=-1, keepdims=True), axis=0, keepdims=True)
        d_h, d_f, d_pg, d_pp, d_g, d_b = pull(diff * (1.0 / D_MODEL))
        return d_h, d_f, d_pg, d_pp, d_g, d_b, jnp.broadcast_to(loss, (1, LANE))

    dh1_a, dffn, dpg, dpp, g_ffn_g, g_ffn_b, loss = _rowwise(
        final_fn, [h1, ffn, pg, pp, tgt], [P["ln_ffn_g"], P["ln_ffn_b"]], [1024] + [(1024, BF16)] * 3,
        [1024, 1024, LANE], name="final")

    G = {}
    dact = _mm(dffn, W["w_ffn_down"], tb=True, b_blk="n", o_blk="n", name="mm_down_dx")
    G["w_ffn_down"] = _mm(act3, dffn, ta=True, a_blk="m", o_blk="m", out_dtype=GRAD_DT, name="mm_down_dw")

    def swiglu_bwd(g, u, d):
        g, u = g.astype(F32), u.astype(F32)
        sg = _sigmoid(g)
        return d * u * (sg * (1.0 + g * (1.0 - sg))), d * (g * sg)

    dg, du = _rowwise(swiglu_bwd, [hg2, hu2, dact.reshape(N_DEV * s_dim, fb)], [], [(fb, BF16)] * 2, name="swiglu_bwd",
                      tr=512)
    dg3, du3 = dg.reshape(N_DEV, s_dim, fb), du.reshape(N_DEV, s_dim, fb)
    dh1 = _mm(dg3, W["w_ffn_gate"], tb=True, a_blk="k", b_blk="k", add=dh1_a, name="mm_gate_dx")
    dh1 = _mm(du3, W["w_ffn_up"], tb=True, a_blk="k", b_blk="k", add=dh1, name="mm_up_dx")
    dh1 = _mm(dpg, W["w_ple_gate"], tb=True, add=dh1, name="mm_ple_gate_dx")
    G["w_ffn_gate"] = _mm(h1b, dg3, ta=True, b_blk="n", o_blk="n", out_dtype=GRAD_DT, name="mm_gate_dw")
    G["w_ffn_up"] = _mm(h1b, du3, ta=True, b_blk="n", o_blk="n", out_dtype=GRAD_DT, name="mm_up_dw")
    G["w_ple_gate"] = _mm(h1b, dpg, ta=True, out_dtype=GRAD_DT, name="mm_ple_gate_dw")
    G["w_ple_proj"] = _mm(pb, dpp, ta=True, out_dtype=GRAD_DT, name="mm_ple_dw")
    dx_a, dmix, g_mix_g, g_mix_b = _rowwise(
        lambda xv, mv, dv, g, b: _vjp_rows(f_h1)(xv, mv, g, b, dv), [x, mix, dh1], [P["ln_mix_g"], P["ln_mix_b"]],
        [1024, (1024, BF16)], [1024, 1024], name="ln_mix_bwd")
    dycat = _mm(dmix, W["w_out"], tb=True, name="mm_out_dx")
    G["w_out"] = _mm(ycat, dmix, ta=True, out_dtype=GRAD_DT, name="mm_out_dw")

    tok1 = comm["ffn_grads"](G) if "ffn_grads" in comm else zero_tok
    datt, g_out_norm = _rowwise(lambda a, dv, w, t: _vjp_rows(_rms)(a, w, dv + jnp.min(t)), [att, (dycat, 1024, 1)],
                                [P["mla_out_norm_w"], tok1], [1024], [1024], name="out_norm_bwd")
    dqn_nope, dqr, dkn, dv, dkrt = _att_bwd(q, qr, kv, krt, att, datt)
    dkv = jnp.concatenate([dkn, dv], axis=1)
    (dq_rope,) = _rowwise(lambda d0, d1, c, s: _rope_bwd_fn(d0 + d1, c, s), [(dqr, 512, 0), (dqr, 512, 1), cos512, sin512],
                          [], [(512, BF16)], name="rope_q_bwd")

    def rope_k_bwd(*a):
        d = _spread4(functools.reduce(lambda u, w: u + w, a[:-2]))
        lane = lax.broadcasted_iota(jnp.int32, d.shape, 1)
        return _rope_bwd_fn(jnp.where(lane < MLA_ROPE, d, 0.0), a[-2], a[-1])

    (dkr,) = _rowwise(rope_k_bwd, [(dkrt, LANE, k) for k in range(MLA_HEADS // 2)] + [cos128, sin128], [], [(LANE, BF16)],
                      name="rope_k_bwd")
    dq = jnp.concatenate([dqn_nope, dq_rope], axis=1)
    dqn = _mm(dq, W["mla_w_q_b"], tb=True, name="mm_q_dx")
    G["mla_w_q_b"] = _mm(qn, dq, ta=True, out_dtype=GRAD_DT, name="mm_q_dw")
    dkvn = _mm(dkv, W["mla_w_kv_b"], tb=True, name="mm_kv_dx")
    G["mla_w_kv_b"] = _mm(kvn, dkv, ta=True, out_dtype=GRAD_DT, name="mm_kv_dw")
    dqc, g_q_norm = _rowwise(lambda a, dv, w: _vjp_rows(_rms)(a, w, dv), [qc, dqn], [P["mla_q_norm_w"]],
                             [(MLA_Q_RANK, BF16)], [MLA_Q_RANK], name="q_norm_bwd")
    dkvc, g_kv_norm = _rowwise(lambda a, dv, w: _vjp_rows(_rms)(a, w, dv), [kvc, dkvn], [P["mla_kv_norm_w"]],
                               [(MLA_KV_RANK, BF16)], [MLA_KV_RANK], name="kv_norm_bwd")

    dy, dz, g_ssd_norm = _rowwise(lambda yv, zv, dv, w, t: _vjp_rows(_gate_rms)(yv, zv, w, dv + jnp.min(t)),
                                  [y, z, (dycat, 1024, 0)], [P["ssd_norm_w"], tok1], [1024, (1024, BF16)], [1024],
                                  name="ssd_gate_norm_bwd")
    dxbca, ddtr, g_dt_bias, g_alog, g_d = _ssd_bwd(xbca, proj, PROJ_BLOCK["dt"][1], bias_p, alog_p, d_x, states, dy)
    da, g_conv_w, g_conv_b = _conv_bwd_pre(proj, PROJ_BLOCK["xbc"][1], P["ssd_conv_w"], P["ssd_conv_b"], dxbca)
    dxbc = _conv_bwd_in(da, P["ssd_conv_w"])

    dproj = jnp.concatenate([dz, ddtr, dqc, dxbc, dkvc, dkr], axis=1)
    grad_x = _mm(dproj, W["w_in"], tb=True, add=dx_a, name="mm_in_dx")
    G["w_in"] = _mm(xb, dproj, ta=True, out_dtype=GRAD_DT, name="mm_in_dw")

    small = {
        "ssd_conv_b": g_conv_b, "ssd_dt_bias": g_dt_bias, "ssd_A_log": g_alog, "ssd_D": g_d, "ssd_norm_w": g_ssd_norm,
        "mla_q_norm_w": g_q_norm, "mla_kv_norm_w": g_kv_norm, "mla_out_norm_w": g_out_norm, "ln_mix_g": g_mix_g,
        "ln_mix_b": g_mix_b, "ln_ffn_g": g_ffn_g, "ln_ffn_b": g_ffn_b,
    }
    return grad_x, G, _pack_small(g_conv_w, [small[n] for n, _ in REPL_W], loss)


def kernel(x, p, positions, w_in, ssd_conv_w, ssd_conv_b, ssd_dt_bias, ssd_A_log, ssd_D, ssd_norm_w, mla_q_norm_w, mla_w_q_b, mla_kv_norm_w, mla_w_kv_b, mla_out_norm_w, w_out, ln_mix_g, ln_mix_b, w_ffn_gate, w_ffn_up, w_ffn_down, w_ple_gate, w_ple_proj, ln_ffn_g, ln_ffn_b, loss_target, m_w_in, m_ssd_conv_w, m_ssd_conv_b, m_ssd_dt_bias, m_ssd_A_log, m_ssd_D, m_ssd_norm_w, m_mla_q_norm_w, m_mla_w_q_b, m_mla_kv_norm_w, m_mla_w_kv_b, m_mla_out_norm_w, m_w_out, m_ln_mix_g, m_ln_mix_b, m_w_ffn_gate, m_w_ffn_up, m_w_ffn_down, m_w_ple_gate, m_w_ple_proj, m_ln_ffn_g, m_ln_ffn_b, v_w_in, v_ssd_conv_w, v_ssd_conv_b, v_ssd_dt_bias, v_ssd_A_log, v_ssd_D, v_ssd_norm_w, v_mla_q_norm_w, v_mla_w_q_b, v_mla_kv_norm_w, v_mla_w_kv_b, v_mla_out_norm_w, v_w_out, v_ln_mix_g, v_ln_mix_b, v_w_ffn_gate, v_w_ffn_up, v_w_ffn_down, v_w_ple_gate, v_w_ple_proj, v_ln_ffn_g, v_ln_ffn_b):
    args = dict(locals())
    core = lax.axis_index("c")
    me = 4 * lax.axis_index("x") + 2 * lax.axis_index("y") + core

    conv_sh = ssd_conv_w[0]
    conv_hi = conv_sh.astype(BF16)
    conv_lo = (conv_sh - conv_hi.astype(F32)).astype(BF16)
    shards = {n: args[n][0].astype(BF16) for n in BIG}
    rows_full = lambda g: g.reshape(-1, g.shape[2])
    core_arr = core.astype(jnp.int32).reshape(1)

    early = _gather_many([shards[n] for n in EARLY] + [jnp.concatenate([conv_hi, conv_lo], axis=0)], "gather_early")
    gw = dict(zip(EARLY, early[:-1]))
    conv_g = early[-1].astype(F32)
    W = {
        "w_in": _win_pad(_cols_full(gw["w_in"])),
        "mla_w_q_b": _heads_split(_cols_full(gw["mla_w_q_b"]), MLA_NOPE, MLA_ROPE),
        "mla_w_kv_b": _heads_split(_cols_full(gw["mla_w_kv_b"]), MLA_NOPE, MLA_V),
    }
    P = {n: args[n] for n, _ in REPL_W}
    P["ssd_conv_w"] = _cols_full(conv_g[:, :4] + conv_g[:, 4:])

    lands = [lax.dynamic_update_slice(jnp.zeros((N_DEV,) + shards[n].shape, BF16), shards[n][None], (me, 0, 0)) for n in LATE]
    late_sems = _split_start([shards[n] for n in LATE], lands, _plan_broadcast, N_DEV - 1, "gather_late_start")

    def late_weights(after):
        _, got = _split_wait(*late_sems[:4], after, _plan_broadcast, "gather_late_wait")
        lw = dict(zip(LATE, got))
        return {"w_out": rows_full(lw["w_out"]), "w_ple_gate": rows_full(lw["w_ple_gate"]),
                "w_ple_proj": _cols_full(lw["w_ple_proj"]), "w_ffn_gate": lw["w_ffn_gate"], "w_ffn_up": lw["w_ffn_up"],
                "w_ffn_down": lw["w_ffn_down"]}

    def to_blocks(n, g):
        if n in OWNER_BLOCKED:
            return g
        if n in ROW_SHARDED:
            return g.reshape(N_DEV, -1, g.shape[1])
        if n == "w_in":
            g = _win_unpad(g)
        elif n == "mla_w_q_b":
            g = _heads_merge(g, MLA_NOPE, MLA_ROPE)
        elif n == "mla_w_kv_b":
            g = _heads_merge(g, MLA_NOPE, MLA_V)
        return _cols_split(g)

    flight = {}

    def ffn_grads(G):
        gl = [to_blocks(n, G[n]) for n in LATE_GRADS]
        flight["grads"] = _split_start(gl, [jnp.zeros(g.shape, g.dtype) for g in gl], _plan_scatter, N_DEV - 1, "grads_start")
        return flight["grads"][4]

    grad_x, G, packed = _local_step(x[0], p[0, 0], positions[0], loss_target[0], W, P,
                                    comm={"token0": late_sems[4], "late_weights": late_weights, "ffn_grads": ffn_grads})

    wmv = lambda n: (args[n][0], args["m_" + n][0], args["v_" + n][0])
    mine, recv = _split_wait(*flight["grads"][:4], grad_x, _plan_scatter, "grads_wait")
    me_arr = me.astype(jnp.int32).reshape(1)
    big_out = {n: _adam(r, *wmv(n), "adam_" + n, own=g, own_idx=me_arr) for n, g, r in zip(LATE_GRADS, mine, recv)}

    glist = [to_blocks(n, G[n]) for n in LAST_GRADS]
    from_sibling = _pair_exchange(glist, "exchange_pairs")
    sums = [_pair_sum(g, r, core_arr, "pair_sum_" + n) for n, g, r in zip(LAST_GRADS, glist, from_sibling)]
    recv = _chip_exchange(sums, "exchange_chips")
    big_out.update({n: _adam(r, *wmv(n), "adam_" + n) for n, r in zip(LAST_GRADS, recv)})

    small_all = _all_gather(packed, "gather_small")
    conv_sum, loss_row, small_out = _adam_small(small_all, [(args[n], args["m_" + n], args["v_" + n]) for n, _ in REPL_W])
    conv_grad = lax.dynamic_slice_in_dim(conv_sum, me * 192, 192, axis=1)
    conv_out = _adam(conv_grad[None], conv_sh, m_ssd_conv_w[0], v_ssd_conv_w[0], "adam_conv")
    small_map = {n: small_out[i] for i, (n, _) in enumerate(REPL_W)}

    def outputs(idx):
        res = []
        for n in WEIGHT_ORDER:
            if n == "ssd_conv_w":
                res.append(conv_out[idx][None])
            elif n in big_out:
                res.append(big_out[n][idx][None])
            else:
                res.append(small_map[n][idx])
        return res

    return (loss_row[0, 0], grad_x[None], *outputs(0), *outputs(1), *outputs(2), *outputs(3))
```

```python
import functools
import math

import numpy as np
import jax
import jax.numpy as jnp
from jax import lax
from jax.experimental import pallas as pl
from jax.experimental.pallas import tpu as pltpu

F32 = jnp.float32
BF16 = jnp.bfloat16
HI = lax.Precision.HIGHEST

N_DEV = 8
D_MODEL = 1024
PLE_DIM = 256
SSD_HEADS = 16
SSD_HEAD_DIM = 64
SSD_INNER = 1024
SSD_STATE = 128
SSD_XBC = 1536
SSD_CHUNK = 128
MLA_HEADS = 16
MLA_Q_RANK = 384
MLA_KV_RANK = 256
MLA_NOPE = 64
MLA_ROPE = 32
MLA_V = 64
ROPE_BASE = 10000.0
D_FF = 2816
IN_WIDTH = 3248
IN_PAD = 3456
ALPHA = 2.0 ** 0.25
EPS = 1e-6
LN_EPS = 1e-5
ATT_SCALE = 1.0 / math.sqrt(MLA_NOPE + MLA_ROPE)
ADAM_LR, ADAM_B1, ADAM_B2, ADAM_EPS, ADAM_WD, ADAM_STEP = 0.001, 0.9, 0.999, 1e-08, 0.01, 10

LANE = 128
MXU_DIM = 256
MM_TM, MM_TN, MM_TK = 1024, 1152, 2048
ROW_TILE = 256
ATT_TQ = 256

GRAD_DT = BF16

BIG = ("w_in", "mla_w_q_b", "mla_w_kv_b", "w_out", "w_ffn_gate", "w_ffn_up", "w_ffn_down", "w_ple_gate", "w_ple_proj")
EARLY = ("w_in", "mla_w_q_b", "mla_w_kv_b")
LATE = ("w_out", "w_ffn_gate", "w_ffn_up", "w_ffn_down", "w_ple_gate", "w_ple_proj")
LATE_GRADS = ("w_ffn_gate", "w_ffn_up", "w_ffn_down", "w_ple_gate", "w_ple_proj", "w_out")
LAST_GRADS = ("w_in", "mla_w_q_b", "mla_w_kv_b")
ROW_SHARDED = ("w_out", "w_ffn_down", "w_ple_gate")
TRANSPOSED = ("w_in", "mla_w_q_b", "w_ffn_gate", "w_ffn_up")
OWNER_BLOCKED = ("w_ffn_gate", "w_ffn_up", "w_ffn_down")
WEIGHT_ORDER = ("w_in", "ssd_conv_w", "ssd_conv_b", "ssd_dt_bias", "ssd_A_log", "ssd_D", "ssd_norm_w", "mla_q_norm_w",
                "mla_w_q_b", "mla_kv_norm_w", "mla_w_kv_b", "mla_out_norm_w", "w_out", "ln_mix_g", "ln_mix_b",
                "w_ffn_gate", "w_ffn_up", "w_ffn_down", "w_ple_gate", "w_ple_proj", "ln_ffn_g", "ln_ffn_b")


def _tile(dim, cap, prefer=None):
    cands = [t for t in range(LANE, min(cap, dim) + 1, LANE) if dim % t == 0]
    if not cands:
        return dim
    if prefer is None:
        return max(cands)
    fill = lambda t: t / (MXU_DIM * -(-t // MXU_DIM))
    best = max(fill(t) for t in cands)
    return min((t for t in cands if fill(t) == best), key=lambda t: abs(t - prefer))


def _dot(a, b, dims=(((1,), (0,)), ((), ())), precision=None):
    return lax.dot_general(a, b, dims, preferred_element_type=F32, precision=precision)


_NT = (((1,), (1,)), ((), ()))
_TN = (((0,), (0,)), ((), ()))


def _mm(a, b, *, ta=False, tb=False, a_blk=None, b_blk=None, o_blk=None, add=None, out_dtype=F32, name):
    ka, ma = a.shape[-2:] if ta else a.shape[-2:][::-1]
    nb, kb = b.shape[-2:] if tb else b.shape[-2:][::-1]
    m_dim = N_DEV * ma if a_blk == "m" else ma
    k_dim = N_DEV * ka if a_blk == "k" else ka
    n_dim = N_DEV * nb if b_blk == "n" else nb
    assert k_dim == (N_DEV * kb if b_blk == "k" else kb)
    tm = ma if a_blk == "m" else (m_dim // N_DEV if o_blk == "m" else _tile(m_dim, MM_TM))
    tn = nb if b_blk == "n" else (n_dim // N_DEV if o_blk == "n" else _tile(n_dim, MM_TN, prefer=512))
    tk = ka if a_blk == "k" else (kb if b_blk == "k" else _tile(k_dim, MM_TK, prefer=MM_TK))
    nk = k_dim // tk
    dims = (((0 if ta else 1,), (1 if tb else 0,)), ((), ()))
    has_add = add is not None

    def spec(tile, idx, lead):
        if lead is None:
            return pl.BlockSpec(tile, idx)
        return pl.BlockSpec((None,) + tile, lambda i, j, k: (lead(i, j, k),) + idx(i, j, k))

    def a_idx(i, j, k):
        ii, kk = (0 if a_blk == "m" else i), (0 if a_blk == "k" else k)
        return (kk, ii) if ta else (ii, kk)

    def b_idx(i, j, k):
        jj, kk = (0 if b_blk == "n" else j), (0 if b_blk == "k" else k)
        return (jj, kk) if tb else (kk, jj)

    def o_idx(i, j, k):
        return (0 if o_blk == "m" else i, 0 if o_blk == "n" else j)

    pick = {"m": lambda i, j, k: i, "n": lambda i, j, k: j, "k": lambda i, j, k: k, None: None}
    a_spec = spec((tk, tm) if ta else (tm, tk), a_idx, pick[a_blk])
    b_spec = spec((tn, tk) if tb else (tk, tn), b_idx, pick[b_blk])
    o_spec = spec((tm, tn), o_idx, pick[o_blk])

    def body(*refs):
        if has_add:
            a_ref, b_ref, add_ref, o_ref = refs[:4]
        else:
            a_ref, b_ref, o_ref = refs[:3]
        part = _dot(a_ref[...].astype(BF16), b_ref[...].astype(BF16), dims)
        if nk == 1:
            o_ref[...] = ((part + add_ref[...]) if has_add else part).astype(o_ref.dtype)
            return
        acc = refs[-1]
        k = pl.program_id(2)

        @pl.when(k == 0)
        def _():
            acc[...] = (part + add_ref[...]) if has_add else part

        @pl.when(k > 0)
        def _():
            acc[...] += part

        @pl.when(k == nk - 1)
        def _():
            o_ref[...] = acc[...].astype(o_ref.dtype)

    if o_blk == "m":
        out_shape = (N_DEV, tm, n_dim)
    elif o_blk == "n":
        out_shape = (N_DEV, m_dim, tn)
    else:
        out_shape = (m_dim, n_dim)
    ins = [a, b] + ([add] if has_add else [])
    specs = [a_spec, b_spec] + ([pl.BlockSpec((tm, tn), lambda i, j, k: (i, j))] if has_add else [])
    return pl.pallas_call(
        body, name=name, grid=(m_dim // tm, n_dim // tn, nk), in_specs=specs, out_specs=o_spec,
        out_shape=jax.ShapeDtypeStruct(out_shape, out_dtype),
        scratch_shapes=[pltpu.VMEM((tm, tn), F32)] if nk > 1 else [],
        compiler_params=pltpu.CompilerParams(dimension_semantics=("parallel", "parallel", "arbitrary")),
    )(*ins)


def _rowwise(fn, rows, consts, out_widths, acc_widths=(), *, name, tr=ROW_TILE):
    row_arrays, row_specs = [], []
    first_arr = rows[0][0] if isinstance(rows[0], tuple) else rows[0]
    s_dim = first_arr.shape[-2]
    tr = min(tr, s_dim)
    for r in rows:
        arr, width, cb = r if isinstance(r, tuple) else (r, r.shape[-1], 0)
        row_arrays.append(arr)
        if arr.ndim == 3:
            row_specs.append(pl.BlockSpec((None, tr, width), functools.partial(lambda i, k: (k, i, 0), k=cb)))
        else:
            row_specs.append(pl.BlockSpec((tr, width), functools.partial(lambda i, cb: (i, cb), cb=cb)))
    const_specs = [pl.BlockSpec(c.shape, lambda i: (0, 0)) for c in consts]
    nr, nc, no, na = len(rows), len(consts), len(out_widths), len(acc_widths)

    def body(*refs):
        ins = [r[...] for r in refs[:nr + nc]]
        res = fn(*ins)
        if not isinstance(res, (tuple, list)):
            res = (res,)
        out_refs = refs[nr + nc:nr + nc + no]
        acc_refs = refs[nr + nc + no:]
        for o_ref, val in zip(out_refs, res[:no]):
            o_ref[...] = val.astype(o_ref.dtype)
        first = pl.program_id(0) == 0
        for a_ref, val in zip(acc_refs, res[no:]):
            @pl.when(first)
            def _(a_ref=a_ref, val=val):
                a_ref[...] = val

            @pl.when(jnp.logical_not(first))
            def _(a_ref=a_ref, val=val):
                a_ref[...] += val

    outs = [w if isinstance(w, tuple) else (w, F32) for w in out_widths]
    out_shape = [jax.ShapeDtypeStruct((s_dim, w), dt) for w, dt in outs]
    out_shape += [jax.ShapeDtypeStruct((1, w), F32) for w in acc_widths]
    out_specs = [pl.BlockSpec((tr, w), lambda i: (i, 0)) for w, _ in outs]
    out_specs += [pl.BlockSpec((1, w), lambda i: (0, 0)) for w in acc_widths]
    res = pl.pallas_call(
        body, name=name, grid=(s_dim // tr,), in_specs=row_specs + const_specs, out_specs=out_specs, out_shape=out_shape,
        compiler_params=pltpu.CompilerParams(dimension_semantics=("arbitrary",)),
    )(*row_arrays, *consts)
    return res


def _colsum(v):
    return jnp.sum(v, axis=0, keepdims=True)


def _rms(u, g):
    return u * lax.rsqrt(jnp.mean(u * u, axis=-1, keepdims=True) + EPS) * g


def _ln(u, g, b):
    mu = jnp.mean(u, axis=-1, keepdims=True)
    d = u - mu
    var = jnp.mean(d * d, axis=-1, keepdims=True)
    return d * lax.rsqrt(var + LN_EPS) * g + b


def _sigmoid(v):
    return 1.0 / (1.0 + jnp.exp(-v))


def _silu(v):
    return v * _sigmoid(v)


def _softplus(v):
    y = jnp.exp(-jnp.abs(v))
    w = 1.0 + y
    log1p = jnp.where(w == 1.0, y, jnp.log(w) * y / jnp.where(w == 1.0, 1.0, w - 1.0))
    return jnp.maximum(v, 0.0) + log1p


def _gate_rms(y, z, w):
    return _rms(y * _silu(z), w)


def _vjp_rows(f):
    def fn(*args):
        prim, ct = args[:-1], args[-1]
        _, pull = jax.vjp(f, *prim)
        return pull(ct)
    return fn


def _conv_pre(cur, prev, w, b, first):
    row = lax.broadcasted_iota(jnp.int32, cur.shape, 0)
    acc = cur * w[3:4, :] + b
    for j in (1, 2, 3):
        tail = jnp.where(first, 0.0, pltpu.roll(prev, j, 0))
        acc = acc + jnp.where(row >= j, pltpu.roll(cur, j, 0), tail) * w[3 - j:4 - j, :]
    return acc


def _conv_fwd(u, ucb, w, b, name="conv_fwd"):
    s_dim, width = u.shape[0], w.shape[1]
    tr = min(ROW_TILE, s_dim)

    def body(cur_ref, prev_ref, w_ref, b_ref, o_ref):
        pre = _conv_pre(cur_ref[...], prev_ref[...], w_ref, b_ref[...], pl.program_id(0) == 0)
        o_ref[...] = _silu(pre)

    return pl.pallas_call(
        body, name=name, grid=(s_dim // tr,),
        in_specs=[pl.BlockSpec((tr, width), lambda i: (i, ucb)),
                  pl.BlockSpec((tr, width), lambda i: (jnp.maximum(i - 1, 0), ucb)),
                  pl.BlockSpec(w.shape, lambda i: (0, 0)), pl.BlockSpec(b.shape, lambda i: (0, 0))],
        out_specs=pl.BlockSpec((tr, width), lambda i: (i, 0)), out_shape=jax.ShapeDtypeStruct((s_dim, width), F32),
        compiler_params=pltpu.CompilerParams(dimension_semantics=("arbitrary",)),
    )(u, u, w, b)


def _conv_bwd_pre(u, ucb, w, b, dact, name="conv_bwd_pre"):
    s_dim, width = u.shape[0], w.shape[1]
    tr = min(ROW_TILE, s_dim)

    def body(cur_ref, prev_ref, w_ref, b_ref, d_ref, da_ref, dw_ref, db_ref):
        first = pl.program_id(0) == 0
        cur, prev = cur_ref[...], prev_ref[...]
        pre = _conv_pre(cur, prev, w_ref, b_ref[...], first)
        sg = _sigmoid(pre)
        da = d_ref[...] * (sg * (1.0 + pre * (1.0 - sg)))
        da_ref[...] = da
        row = lax.broadcasted_iota(jnp.int32, cur.shape, 0)

        @pl.when(first)
        def _():
            dw_ref[...] = jnp.zeros_like(dw_ref)
            db_ref[...] = jnp.zeros_like(db_ref)

        db_ref[...] += _colsum(da)
        dw_ref[3:4, :] += _colsum(da * cur)
        for j in (1, 2, 3):
            tail = jnp.where(first, 0.0, pltpu.roll(prev, j, 0))
            sh = jnp.where(row >= j, pltpu.roll(cur, j, 0), tail)
            dw_ref[3 - j:4 - j, :] += _colsum(da * sh)

    return pl.pallas_call(
        body, name=name, grid=(s_dim // tr,),
        in_specs=[pl.BlockSpec((tr, width), lambda i: (i, ucb)),
                  pl.BlockSpec((tr, width), lambda i: (jnp.maximum(i - 1, 0), ucb)),
                  pl.BlockSpec(w.shape, lambda i: (0, 0)), pl.BlockSpec(b.shape, lambda i: (0, 0)),
                  pl.BlockSpec((tr, width), lambda i: (i, 0))],
        out_specs=[pl.BlockSpec((tr, width), lambda i: (i, 0)), pl.BlockSpec(w.shape, lambda i: (0, 0)),
                   pl.BlockSpec(b.shape, lambda i: (0, 0))],
        out_shape=[jax.ShapeDtypeStruct((s_dim, width), F32), jax.ShapeDtypeStruct(w.shape, F32),
                   jax.ShapeDtypeStruct(b.shape, F32)],
        compiler_params=pltpu.CompilerParams(dimension_semantics=("arbitrary",)),
    )(u, u, w, b, dact)


def _conv_bwd_in(da, w, name="conv_bwd_in"):
    s_dim, width = da.shape
    tr = min(ROW_TILE, s_dim)
    n = s_dim // tr

    def body(cur_ref, nxt_ref, w_ref, o_ref):
        last = pl.program_id(0) == n - 1
        cur, nxt = cur_ref[...], nxt_ref[...]
        row = lax.broadcasted_iota(jnp.int32, cur.shape, 0)
        acc = cur * w_ref[3:4, :]
        for j in (1, 2, 3):
            head = jnp.where(last, 0.0, pltpu.roll(nxt, tr - j, 0))
            acc = acc + jnp.where(row < tr - j, pltpu.roll(cur, tr - j, 0), head) * w_ref[3 - j:4 - j, :]
        o_ref[...] = acc.astype(o_ref.dtype)

    return pl.pallas_call(
        body, name=name, grid=(n,),
        in_specs=[pl.BlockSpec((tr, width), lambda i: (i, 0)), pl.BlockSpec((tr, width), lambda i: (jnp.minimum(i + 1, n - 1), 0)),
                  pl.BlockSpec(w.shape, lambda i: (0, 0))],
        out_specs=pl.BlockSpec((tr, width), lambda i: (i, 0)), out_shape=jax.ShapeDtypeStruct((s_dim, width), BF16),
        compiler_params=pltpu.CompilerParams(dimension_semantics=("arbitrary",)),
    )(da, da, w)


def _sel_dot(a, sel, pieces, dims=(((1,), (0,)), ((), ())), sel_left=False):
    sel = sel.astype(BF16)
    acc, rest = None, a
    for _ in range(pieces):
        piece = rest.astype(BF16)
        rest = rest - piece.astype(F32)
        part = _dot(sel, piece, dims) if sel_left else _dot(piece, sel, dims)
        acc = part if acc is None else acc + part
    return acc


def _ssd_consts():
    L = SSD_CHUNK
    tri = np.tril(np.ones((L, L), np.float32))
    expand = np.zeros((LANE, SSD_INNER), np.float32)
    expand128 = np.zeros((LANE, SSD_HEADS * LANE), np.float32)
    for h in range(SSD_HEADS):
        expand[h, h * SSD_HEAD_DIM:(h + 1) * SSD_HEAD_DIM] = 1.0
        expand128[h, h * LANE:(h + 1) * LANE] = 1.0
    return jnp.asarray(tri), jnp.asarray(expand), jnp.asarray(expand128), jnp.asarray(expand.T.copy())


def _ssd_prep(dt_ref, bias_ref, alog_ref, tri_ref, exp_ref, exp128_ref, cs_s, cst_s, ex_s, csx_s):
    L = SSD_CHUNK
    dt = _softplus(dt_ref[...] + bias_ref[...])
    a = -jnp.exp(alog_ref[...])
    cs = _sel_dot(dt * a, tri_ref[...], 3, sel_left=True)
    cs_s[...] = cs
    cst_s[...] = cs.T
    last = cs_s[L - 1:L, :]
    expand = exp_ref[...]
    ex_s[...] = _sel_dot(jnp.exp(cs), expand, 2)
    f_x = _sel_dot(jnp.exp(last - cs), expand, 2)
    dt_x = _sel_dot(dt, expand, 2)
    csx_s[...] = _sel_dot(cs, exp128_ref[...], 3)
    t_x = ex_s[L - 1:L, :]
    return dt, a, dt_x, f_x, t_x


def _decay_matrix(csx_s, cst_s, h, tril):
    seg = csx_s[:, h * LANE:(h + 1) * LANE] - cst_s[h:h + 1, :]
    return jnp.exp(jnp.where(tril, seg, -jnp.inf))


def _ssd_fwd(xbca, dtr, dtcb, bias, alog, d_x, name="ssd_fwd"):
    s_dim = xbca.shape[0]
    L = SSD_CHUNK
    nc = s_dim // L
    tri, expand, expand128, _ = _ssd_consts()

    def body(xs_ref, b_ref, c_ref, dt_ref, bias_ref, alog_ref, dx_ref, tri_ref, exp_ref, exp128_ref,
             y_ref, st_ref, st_s, cs_s, cst_s, ex_s, csx_s):
        @pl.when(pl.program_id(0) == 0)
        def _():
            st_s[...] = jnp.zeros_like(st_s)

        dt, a, dt_x, f_x, t_x = _ssd_prep(dt_ref, bias_ref, alog_ref, tri_ref, exp_ref, exp128_ref, cs_s, cst_s, ex_s, csx_s)
        st_ref[0] = st_s[...]
        row = lax.broadcasted_iota(jnp.int32, (L, L), 0)
        col = lax.broadcasted_iota(jnp.int32, (L, L), 1)
        tril = row >= col
        low = col < SSD_HEAD_DIM
        for g in range(2):
            bg = b_ref[:, g * LANE:(g + 1) * LANE]
            cg = c_ref[:, g * LANE:(g + 1) * LANE].astype(BF16)
            gmat = _dot(cg, bg.astype(BF16), _NT)
            bgt = bg.T.astype(BF16)
            for jj in range(4):
                j = 4 * g + jj
                sl = slice(j * LANE, (j + 1) * LANE)
                xp = xs_ref[:, sl]
                x_dt = xp * dt_x[:, sl]
                xb = x_dt.astype(BF16)
                yd = []
                for e in range(2):
                    lm = _decay_matrix(csx_s, cst_s, 2 * j + e, tril)
                    yd.append(_dot((gmat * lm).astype(BF16), xb))
                stp = st_s[j]
                z = _dot(cg, stp.astype(BF16))
                y_ref[:, sl] = jnp.where(low, yd[0], yd[1]) + ex_s[:, sl] * z + dx_ref[:, sl] * xp
                xf = (x_dt * f_x[:, sl]).astype(BF16)
                st_s[j] = t_x[:, sl] * stp + _dot(bgt, xf)

    const = lambda shape: pl.BlockSpec(shape, lambda c: tuple(0 for _ in shape))
    return pl.pallas_call(
        body, name=name, grid=(nc,),
        in_specs=[pl.BlockSpec((L, 1024), lambda c: (c, 0)), pl.BlockSpec((L, 256), lambda c: (c, 4)),
                  pl.BlockSpec((L, 256), lambda c: (c, 5)), pl.BlockSpec((L, LANE), lambda c: (c, dtcb)),
                  const((1, LANE)), const((1, LANE)), const((1, 1024)), const((L, L)), const((LANE, 1024)),
                  const((LANE, 2048))],
        out_specs=[pl.BlockSpec((L, 1024), lambda c: (c, 0)), pl.BlockSpec((1, 8, LANE, LANE), lambda c: (c, 0, 0, 0))],
        out_shape=[jax.ShapeDtypeStruct((s_dim, 1024), F32), jax.ShapeDtypeStruct((nc, 8, LANE, LANE), F32)],
        scratch_shapes=[pltpu.VMEM((8, LANE, LANE), F32), pltpu.VMEM((L, LANE), F32), pltpu.VMEM((LANE, L), F32),
                        pltpu.VMEM((L, 1024), F32), pltpu.VMEM((L, 2048), F32)],
        compiler_params=pltpu.CompilerParams(dimension_semantics=("arbitrary",)),
    )(xbca, xbca, xbca, dtr, bias, alog, d_x, tri, expand, expand128)


def _ssd_bwd(xbca, dtr, dtcb, bias, alog, d_x, states, dy, name="ssd_bwd"):
    s_dim = xbca.shape[0]
    L = SSD_CHUNK
    nc = s_dim // L
    tri, expand, expand128, expand_t = _ssd_consts()

    def body(xs_ref, b_ref, c_ref, dt_ref, bias_ref, alog_ref, dx_ref, tri_ref, exp_ref, exp128_ref, expt_ref,
             st_ref, dy_ref, dxbc_ref, ddt_ref, dbias_ref, dalog_ref, dd_ref,
             dst_s, cs_s, cst_s, ex_s, csx_s, dcsx_s, ddtx_s, dcol_s, drow_s, dlast_s, dd_s):
        @pl.when(pl.program_id(0) == 0)
        def _():
            dst_s[...] = jnp.zeros_like(dst_s)
            dbias_ref[...] = jnp.zeros_like(dbias_ref)
            dalog_ref[...] = jnp.zeros_like(dalog_ref)
            dd_s[...] = jnp.zeros_like(dd_s)

        dt, a, dt_x, f_x, t_x = _ssd_prep(dt_ref, bias_ref, alog_ref, tri_ref, exp_ref, exp128_ref, cs_s, cst_s, ex_s, csx_s)
        row = lax.broadcasted_iota(jnp.int32, (L, L), 0)
        col = lax.broadcasted_iota(jnp.int32, (L, L), 1)
        tril = row >= col
        low = col < SSD_HEAD_DIM
        dcol_s[...] = jnp.zeros_like(dcol_s)
        drow_s[...] = jnp.zeros_like(drow_s)
        for g in range(2):
            bg = b_ref[:, g * LANE:(g + 1) * LANE]
            cg = c_ref[:, g * LANE:(g + 1) * LANE]
            bgb, cgb = bg.astype(BF16), cg.astype(BF16)
            gmat = _dot(cgb, bgb, _NT)
            d_g = jnp.zeros((L, L), F32)
            d_b = jnp.zeros((L, LANE), F32)
            d_c = jnp.zeros((L, LANE), F32)
            for jj in range(4):
                j = 4 * g + jj
                sl = slice(j * LANE, (j + 1) * LANE)
                xp = xs_ref[:, sl]
                dtp = dt_x[:, sl]
                x_dt = xp * dtp
                xb = x_dt.astype(BF16)
                dyp = dy_ref[:, sl]
                dd_s[:, sl] += _colsum(dyp * xp)
                d_xdt = jnp.zeros((L, LANE), F32)
                for e in range(2):
                    h = 2 * j + e
                    lm = _decay_matrix(csx_s, cst_s, h, tril)
                    m = gmat * lm
                    dye = jnp.where(low if e == 0 else jnp.logical_not(low), dyp, 0.0).astype(BF16)
                    d_m = jnp.where(tril, _dot(dye, xb, _NT), 0.0)
                    d_xdt = d_xdt + _dot(m.astype(BF16), dye, _TN)
                    d_g = d_g + d_m * lm
                    w = d_m * m
                    dcol_s[...] += jnp.where(col == h, jnp.sum(w, axis=1, keepdims=True), 0.0)
                    drow_s[...] += jnp.where(row == h, jnp.sum(w, axis=0, keepdims=True), 0.0)
                stp = st_ref[0, j]
                stb = stp.astype(BF16)
                dstn = dst_s[j]
                dstb = dstn.astype(BF16)
                e_p = ex_s[:, sl]
                f_p = f_x[:, sl]
                t_p = t_x[:, sl]
                z = _dot(cgb, stb)
                d_z = (e_p * dyp).astype(BF16)
                d_c = d_c + _dot(d_z, stb, _NT)
                d_xf = _dot(bgb, dstb)
                d_b = d_b + _dot((x_dt * f_p).astype(BF16), dstb, _NT)
                d_xdt = d_xdt + f_p * d_xf
                d_f = x_dt * d_xf * f_p
                dcsx_s[:, sl] = dyp * e_p * z - d_f
                dlast_s[:, sl] = _colsum(d_f) + _colsum(dstn * stp) * t_p
                dst_s[j] = _dot(cgb, d_z, _TN) + t_p * dstn
                dxbc_ref[:, sl] = dx_ref[:, sl] * dyp + d_xdt * dtp
                ddtx_s[:, sl] = d_xdt * xp
            d_gb = d_g.astype(BF16)
            dxbc_ref[:, 1024 + g * LANE:1024 + (g + 1) * LANE] = d_b + _dot(d_gb, cgb, _TN)
            dxbc_ref[:, 1280 + g * LANE:1280 + (g + 1) * LANE] = d_c + _dot(d_gb, bgb)

        expt = expt_ref[...]
        dlast = _sel_dot(jnp.broadcast_to(dlast_s[...], (8, 1024)), expt, 3)
        d_cs = dcol_s[...] - drow_s[...].T + _sel_dot(dcsx_s[...], expt, 3)
        rown = lax.broadcasted_iota(jnp.int32, (L, LANE), 0)
        d_cs = d_cs + jnp.where(rown == L - 1, jnp.sum(dlast, axis=0, keepdims=True) * 0.125, 0.0)
        d_da = _sel_dot(d_cs, tri_ref[...], 3, _TN, sel_left=True)
        d_dt = d_da * a + _sel_dot(ddtx_s[...], expt, 3)
        dalog_ref[...] += _colsum(d_da * dt) * a
        d_raw = d_dt * _sigmoid(dt_ref[...] + bias_ref[...])
        ddt_ref[...] = d_raw.astype(ddt_ref.dtype)
        dbias_ref[...] += _colsum(d_raw)
        dd8 = _sel_dot(jnp.broadcast_to(dd_s[...], (8, 1024)), expt, 3)
        dd_ref[...] = jnp.sum(dd8, axis=0, keepdims=True) * 0.125

    const = lambda shape: pl.BlockSpec(shape, lambda c: tuple(0 for _ in shape))
    rev = lambda cb: (lambda c: (nc - 1 - c, cb))
    return pl.pallas_call(
        body, name=name, grid=(nc,),
        in_specs=[pl.BlockSpec((L, 1024), rev(0)), pl.BlockSpec((L, 256), rev(4)), pl.BlockSpec((L, 256), rev(5)),
                  pl.BlockSpec((L, LANE), rev(dtcb)), const((1, LANE)), const((1, LANE)), const((1, 1024)), const((L, L)),
                  const((LANE, 1024)), const((LANE, 2048)), const((1024, LANE)),
                  pl.BlockSpec((1, 8, LANE, LANE), lambda c: (nc - 1 - c, 0, 0, 0)), pl.BlockSpec((L, 1024), rev(0))],
        out_specs=[pl.BlockSpec((L, SSD_XBC), rev(0)), pl.BlockSpec((L, LANE), rev(0)), const((1, LANE)), const((1, LANE)),
                   const((1, LANE))],
        out_shape=[jax.ShapeDtypeStruct((s_dim, SSD_XBC), F32), jax.ShapeDtypeStruct((s_dim, LANE), BF16),
                   jax.ShapeDtypeStruct((1, LANE), F32), jax.ShapeDtypeStruct((1, LANE), F32),
                   jax.ShapeDtypeStruct((1, LANE), F32)],
        scratch_shapes=[pltpu.VMEM((8, LANE, LANE), F32), pltpu.VMEM((L, LANE), F32), pltpu.VMEM((LANE, L), F32),
                        pltpu.VMEM((L, 1024), F32), pltpu.VMEM((L, 2048), F32), pltpu.VMEM((L, 1024), F32),
                        pltpu.VMEM((L, 1024), F32), pltpu.VMEM((L, LANE), F32), pltpu.VMEM((LANE, L), F32),
                        pltpu.VMEM((1, 1024), F32), pltpu.VMEM((1, 1024), F32)],
        compiler_params=pltpu.CompilerParams(dimension_semantics=("arbitrary",)),
    )(xbca, xbca, xbca, dtr, bias, alog, d_x, tri, expand, expand128, expand_t, states, dy)


def _swap_halves(u):
    width = u.shape[1]
    lane = lax.broadcasted_iota(jnp.int32, u.shape, 1)
    return jnp.where(lane % MLA_ROPE < MLA_ROPE // 2, pltpu.roll(u, width - MLA_ROPE // 2, 1), pltpu.roll(u, MLA_ROPE // 2, 1))


def _rope_fwd_fn(u, cos, sin):
    return u * cos + _swap_halves(u) * sin


def _rope_bwd_fn(d, cos, sin):
    return d * cos + _swap_halves(d * sin)


def _spread4(v):
    return v + pltpu.roll(v, 32, 1) + pltpu.roll(v, 64, 1) + pltpu.roll(v, 96, 1)


def _att_masks(tq):
    lane = lax.broadcasted_iota(jnp.int32, (tq, LANE), 1)
    return lane // MLA_NOPE, lane // MLA_ROPE


def _att_tile(i, tq):
    klen = (i + 1) * tq
    qpos = i * tq + lax.broadcasted_iota(jnp.int32, (tq, klen), 0)
    kpos = lax.broadcasted_iota(jnp.int32, (tq, klen), 1)
    return slice(i * tq, (i + 1) * tq), klen, qpos >= kpos


def _att_qcat(qn_t, qr_t, par, e, half_id, grp_id):
    return jnp.concatenate([jnp.where(half_id == par, qn_t * ATT_SCALE, 0.0), jnp.where(grp_id == e, qr_t * ATT_SCALE, 0.0)],
                           axis=1).astype(BF16)


def _att_exp(qcat, kcat, causal):
    s = jnp.where(causal, _dot(qcat, kcat, _NT), -jnp.inf)
    e = jnp.exp(s - jnp.max(s, axis=1, keepdims=True))
    return e, 1.0 / jnp.sum(e, axis=1, keepdims=True)


def _att_specs(s_dim):
    col = lambda f: pl.BlockSpec((s_dim, LANE), lambda j: (0, f(j)))
    return [col(lambda j: j), col(lambda j: j // 2), col(lambda j: j), col(lambda j: 0), col(lambda j: 8 + j)]


def _att_fwd(q, qr, kv, krt, name="att_fwd"):
    s_dim = q.shape[0]
    tq = min(ATT_TQ, s_dim)

    def body(qn_ref, qr_ref, kn_ref, krt_ref, v_ref, o_ref, kcat_s, vb_s):
        e0 = 2 * (pl.program_id(0) % 2)
        half_id, grp_id = _att_masks(tq)
        kcat_s[...] = jnp.concatenate([kn_ref[...], krt_ref[...]], axis=1).astype(BF16)
        vb_s[...] = v_ref[...].astype(BF16)
        for i in range(s_dim // tq):
            rows, klen, causal = _att_tile(i, tq)
            qn_t, qr_t = qn_ref[rows, :], qr_ref[rows, :]
            outs = []
            for par in range(2):
                qcat = _att_qcat(qn_t, qr_t, par, e0 + par, half_id, grp_id)
                e, inv_l = _att_exp(qcat, kcat_s[0:klen, :], causal)
                outs.append(_dot(e.astype(BF16), vb_s[0:klen, :]) * inv_l)
            o_ref[rows, :] = jnp.where(half_id == 0, outs[0], outs[1])

    return pl.pallas_call(
        body, name=name, grid=(MLA_HEADS // 2,), in_specs=_att_specs(s_dim),
        out_specs=pl.BlockSpec((s_dim, LANE), lambda j: (0, j)), out_shape=jax.ShapeDtypeStruct((s_dim, 1024), F32),
        scratch_shapes=[pltpu.VMEM((s_dim, 2 * LANE), BF16), pltpu.VMEM((s_dim, LANE), BF16)],
        compiler_params=pltpu.CompilerParams(dimension_semantics=("parallel",)),
    )(q, qr, kv, krt, kv)


def _att_bwd(q, qr, kv, krt, o, do, name="att_bwd"):
    s_dim = q.shape[0]
    tq = min(ATT_TQ, s_dim)

    def body(qn_ref, qr_ref, kn_ref, krt_ref, v_ref, o_ref, do_ref, dqn_ref, dqr_ref, dkn_ref, dv_ref, dkrt_ref,
             kcat_s, vb_s):
        e0 = 2 * (pl.program_id(0) % 2)
        half_id, grp_id = _att_masks(tq)
        kcat_s[...] = jnp.concatenate([kn_ref[...], krt_ref[...]], axis=1).astype(BF16)
        vb_s[...] = v_ref[...].astype(BF16)
        dkn_ref[...] = jnp.zeros_like(dkn_ref)
        dv_ref[...] = jnp.zeros_like(dv_ref)
        dkrt_ref[...] = jnp.zeros_like(dkrt_ref)
        for i in range(s_dim // tq):
            rows, klen, causal = _att_tile(i, tq)
            qn_t, qr_t, o_t, do_t = qn_ref[rows, :], qr_ref[rows, :], o_ref[rows, :], do_ref[rows, :]
            dqn = jnp.zeros((tq, LANE), F32)
            dqr = jnp.zeros((tq, LANE), F32)
            for par in range(2):
                qcat = _att_qcat(qn_t, qr_t, par, e0 + par, half_id, grp_id)
                e, inv_l = _att_exp(qcat, kcat_s[0:klen, :], causal)
                p = e * inv_l
                dom = jnp.where(half_id == par, do_t, 0.0)
                domb = dom.astype(BF16)
                d_p = _dot(domb, vb_s[0:klen, :], _NT)
                d_row = jnp.sum(dom * o_t, axis=1, keepdims=True)
                d_s = (p * (d_p - d_row)).astype(BF16)
                dqcat = _dot(d_s, kcat_s[0:klen, :]) * ATT_SCALE
                dqn = dqn + jnp.where(half_id == par, dqcat[:, :LANE], 0.0)
                dqr = dqr + jnp.where(grp_id == e0 + par, dqcat[:, LANE:], 0.0)
                dkcat = _dot(d_s, qcat, _TN)
                dkn_ref[0:klen, :] += dkcat[:, :LANE]
                dkrt_ref[0:klen, :] += dkcat[:, LANE:]
                dv_ref[0:klen, :] += _dot(p.astype(BF16), domb, _TN)
            dqn_ref[rows, :] = dqn.astype(dqn_ref.dtype)
            dqr_ref[rows, :] = dqr

    col = lambda f: pl.BlockSpec((s_dim, LANE), lambda j: (0, f(j)))
    return pl.pallas_call(
        body, name=name, grid=(MLA_HEADS // 2,), in_specs=_att_specs(s_dim) + [col(lambda j: j), col(lambda j: j)],
        out_specs=[col(lambda j: j), pl.BlockSpec((None, s_dim, LANE), lambda j: (j % 2, 0, j // 2)), col(lambda j: j),
                   col(lambda j: j), pl.BlockSpec((None, s_dim, LANE), lambda j: (j, 0, 0))],
        out_shape=[jax.ShapeDtypeStruct((s_dim, 1024), BF16), jax.ShapeDtypeStruct((2, s_dim, 512), F32),
                   jax.ShapeDtypeStruct((s_dim, 1024), F32), jax.ShapeDtypeStruct((s_dim, 1024), F32),
                   jax.ShapeDtypeStruct((MLA_HEADS // 2, s_dim, LANE), F32)],
        scratch_shapes=[pltpu.VMEM((s_dim, 2 * LANE), BF16), pltpu.VMEM((s_dim, LANE), BF16)],
        compiler_params=pltpu.CompilerParams(dimension_semantics=("parallel",)),
    )(q, qr, kv, krt, kv, o, do)


def _all_gather(x, name):
    rows, width = x.shape

    def body(x_ref, out_ref, send_sems, recv_sems, local_sem):
        x_i, y_i, c_i = lax.axis_index("x"), lax.axis_index("y"), lax.axis_index("c")
        me, sibling = (x_i, y_i, c_i), (x_i, y_i, 1 - c_i)
        chips = [(1 - x_i, y_i), (x_i, 1 - y_i), (1 - x_i, 1 - y_i)]

        def slot(px, py, pc):
            return out_ref.at[4 * px + 2 * py + pc]

        def copy(k, block, to, src=None):
            return pltpu.make_async_remote_copy(
                src_ref=slot(*block) if src is None else src, dst_ref=slot(*block), send_sem=send_sems.at[k],
                recv_sem=recv_sems.at[k], device_id=to, device_id_type=pl.DeviceIdType.MESH)

        mine = pltpu.make_async_copy(x_ref, slot(*me), local_sem)
        mine.start()
        first = [copy(0, me, sibling, src=x_ref)]
        first += [copy(1 + j, me, (*chip, c_i), src=x_ref) for j, chip in enumerate(chips)]
        for cp in first:
            cp.start()
        passed = [copy(4 + j, (*chip, c_i), sibling) for j, chip in enumerate(chips)]
        for j, chip in enumerate(chips):
            copy(1 + j, (*chip, c_i), me).wait_recv()
            passed[j].start()
        copy(0, sibling, me).wait_recv()
        for j, chip in enumerate(chips):
            copy(4 + j, (*chip, 1 - c_i), me).wait_recv()
        for cp in first + passed:
            cp.wait_send()
        mine.wait()

    return pl.pallas_call(
        body, name=name, out_shape=jax.ShapeDtypeStruct((N_DEV, rows, width), x.dtype),
        in_specs=[pl.BlockSpec(memory_space=pl.ANY)], out_specs=pl.BlockSpec(memory_space=pl.ANY),
        scratch_shapes=[pltpu.SemaphoreType.DMA((7,)), pltpu.SemaphoreType.DMA((7,)), pltpu.SemaphoreType.DMA],
    )(x)


def _gather_many(shards, name):
    n_arr = len(shards)

    def body(*refs):
        x_refs, out_refs = refs[:n_arr], refs[n_arr:2 * n_arr]
        send_sems, recv_sems, local_sems = refs[2 * n_arr:]
        x_i, y_i, c_i = lax.axis_index("x"), lax.axis_index("y"), lax.axis_index("c")
        me, sibling = (x_i, y_i, c_i), (x_i, y_i, 1 - c_i)
        chips = [(1 - x_i, y_i), (x_i, 1 - y_i), (1 - x_i, 1 - y_i)]

        def copy(a, k, block, to, src=None):
            slot = out_refs[a].at[4 * block[0] + 2 * block[1] + block[2]]
            return pltpu.make_async_remote_copy(
                src_ref=slot if src is None else src, dst_ref=slot, send_sem=send_sems.at[a, k],
                recv_sem=recv_sems.at[a, k], device_id=to, device_id_type=pl.DeviceIdType.MESH)

        mine, first, passed = [], [], []
        for a in range(n_arr):
            mine.append(pltpu.make_async_copy(x_refs[a], out_refs[a].at[4 * x_i + 2 * y_i + c_i], local_sems.at[a]))
            mine[a].start()
            first.append([copy(a, 0, me, sibling, src=x_refs[a])]
                         + [copy(a, 1 + j, me, (*chip, c_i), src=x_refs[a]) for j, chip in enumerate(chips)])
            for cp in first[a]:
                cp.start()
            passed.append([copy(a, 4 + j, (*chip, c_i), sibling) for j, chip in enumerate(chips)])
        for j, chip in enumerate(chips):
            for a in range(n_arr):
                copy(a, 1 + j, (*chip, c_i), me).wait_recv()
                passed[a][j].start()
        for a in range(n_arr):
            copy(a, 0, sibling, me).wait_recv()
            for j, chip in enumerate(chips):
                copy(a, 4 + j, (*chip, 1 - c_i), me).wait_recv()
        for a in range(n_arr):
            for cp in first[a] + passed[a]:
                cp.wait_send()
            mine[a].wait()

    any_spec = pl.BlockSpec(memory_space=pl.ANY)
    return pl.pallas_call(
        body, name=name, out_shape=[jax.ShapeDtypeStruct((N_DEV,) + x.shape, x.dtype) for x in shards],
        in_specs=[any_spec] * n_arr, out_specs=[any_spec] * n_arr,
        scratch_shapes=[pltpu.SemaphoreType.DMA((n_arr, 7)), pltpu.SemaphoreType.DMA((n_arr, 7)),
                        pltpu.SemaphoreType.DMA((n_arr,))],
    )(*shards)


def _pair_exchange(grads, name):
    n_arr = len(grads)

    def body(*refs):
        g_refs, out_refs = refs[:n_arr], refs[n_arr:2 * n_arr]
        send_sems, recv_sems = refs[2 * n_arr:]
        x_i, y_i, c_i = lax.axis_index("x"), lax.axis_index("y"), lax.axis_index("c")
        copies = []
        for a in range(n_arr):
            for chip in range(4):
                copies.append(pltpu.make_async_remote_copy(
                    src_ref=g_refs[a].at[2 * chip + (1 - c_i)], dst_ref=out_refs[a].at[chip], send_sem=send_sems.at[a, chip],
                    recv_sem=recv_sems.at[a, chip], device_id=(x_i, y_i, 1 - c_i), device_id_type=pl.DeviceIdType.MESH))
        for cp in copies:
            cp.start()
        for cp in copies:
            cp.wait_recv()
        for cp in copies:
            cp.wait_send()

    any_spec = pl.BlockSpec(memory_space=pl.ANY)
    return pl.pallas_call(
        body, name=name, out_shape=[jax.ShapeDtypeStruct((4,) + g.shape[1:], g.dtype) for g in grads],
        in_specs=[any_spec] * n_arr, out_specs=[any_spec] * n_arr,
        scratch_shapes=[pltpu.SemaphoreType.DMA((n_arr, 4)), pltpu.SemaphoreType.DMA((n_arr, 4))],
    )(*grads)


def _chip_exchange(sums, name):
    n_arr = len(sums)

    def body(*refs):
        s_refs, out_refs = refs[:n_arr], refs[n_arr:2 * n_arr]
        send_sems, recv_sems, local_sems = refs[2 * n_arr:]
        x_i, y_i, c_i = lax.axis_index("x"), lax.axis_index("y"), lax.axis_index("c")
        my_chip = 2 * x_i + y_i
        copies, local = [], []
        for a in range(n_arr):
            local.append(pltpu.make_async_copy(s_refs[a].at[my_chip], out_refs[a].at[my_chip], local_sems.at[a]))
            local[a].start()
            for k in range(1, 4):
                px, py = x_i ^ (k >> 1), y_i ^ (k & 1)
                copies.append(pltpu.make_async_remote_copy(
                    src_ref=s_refs[a].at[2 * px + py], dst_ref=out_refs[a].at[my_chip], send_sem=send_sems.at[a, k - 1],
                    recv_sem=recv_sems.at[a, k - 1], device_id=(px, py, c_i), device_id_type=pl.DeviceIdType.MESH))
        for cp in copies:
            cp.start()
        for cp in copies:
            cp.wait_recv()
        for cp in copies:
            cp.wait_send()
        for cp in local:
            cp.wait()

    any_spec = pl.BlockSpec(memory_space=pl.ANY)
    return pl.pallas_call(
        body, name=name, out_shape=[jax.ShapeDtypeStruct(s.shape, s.dtype) for s in sums],
        in_specs=[any_spec] * n_arr, out_specs=[any_spec] * n_arr,
        scratch_shapes=[pltpu.SemaphoreType.DMA((n_arr, 3)), pltpu.SemaphoreType.DMA((n_arr, 3)),
                        pltpu.SemaphoreType.DMA((n_arr,))],
    )(*sums)


_HBM = pl.BlockSpec(memory_space=pltpu.HBM)
_SEM = pl.BlockSpec(memory_space=pltpu.SEMAPHORE)


def _plan_copies(plan, src_refs, land_refs, send_sems, recv_sems):
    copies = []
    for s_ref, l_ref in zip(src_refs, land_refs):
        for src, dst, peer in plan(s_ref, l_ref):
            k = len(copies)
            copies.append(pltpu.make_async_remote_copy(
                src_ref=src, dst_ref=dst, send_sem=send_sems.at[k], recv_sem=recv_sems.at[k], device_id=peer,
                device_id_type=pl.DeviceIdType.MESH))
    return copies


def _split_start(srcs, lands, plan, n_copy, name):
    n = len(srcs)

    def body(*refs):
        for cp in _plan_copies(plan, refs[:n], refs[n:2 * n], refs[2 * n], refs[2 * n + 1]):
            cp.start()
        refs[-1][...] = jnp.zeros_like(refs[-1])

    sems = pltpu.SemaphoreType.DMA((n * n_copy,))
    res = pl.pallas_call(
        body, name=name,
        out_shape=(sems, sems, *[pltpu.HBM(a.shape, a.dtype) for a in list(srcs) + list(lands)],
                   jax.ShapeDtypeStruct((8, LANE), F32)),
        in_specs=[_HBM] * (2 * n), out_specs=(_SEM, _SEM, *[_HBM] * (2 * n), pl.BlockSpec(memory_space=pltpu.VMEM)),
        input_output_aliases={i: 2 + i for i in range(2 * n)},
        compiler_params=pltpu.CompilerParams(has_side_effects=pltpu.SideEffectType.DATAFLOW_SIDE_EFFECTING),
    )(*[pltpu.with_memory_space_constraint(a, pltpu.HBM) for a in list(srcs) + list(lands)])
    return res[0], res[1], list(res[2:2 + n]), list(res[2 + n:2 + 2 * n]), res[-1]


def _split_wait(send_sems, recv_sems, srcs, lands, after, plan, name):
    n = len(srcs)

    def body(*refs):
        copies = _plan_copies(plan, refs[:n], refs[n:2 * n], refs[2 * n], refs[2 * n + 1])
        for cp in copies:
            cp.wait_send()
        for cp in copies:
            cp.wait_recv()

    res = pl.pallas_call(
        body, name=name, out_shape=tuple(pltpu.HBM(a.shape, a.dtype) for a in list(srcs) + list(lands)),
        in_specs=[_HBM] * (2 * n) + [_SEM, _SEM, pl.BlockSpec(memory_space=pl.ANY)], out_specs=tuple([_HBM] * (2 * n)),
        input_output_aliases={i: i for i in range(2 * n)},
        compiler_params=pltpu.CompilerParams(has_side_effects=pltpu.SideEffectType.DATAFLOW_SIDE_EFFECTING),
    )(*srcs, *lands, send_sems, recv_sems, after)
    return list(res[:n]), list(res[n:])


def _plan_broadcast(src, land):
    x_i, y_i, c_i = lax.axis_index("x"), lax.axis_index("y"), lax.axis_index("c")
    me = 4 * x_i + 2 * y_i + c_i
    return [(src, land.at[me], (x_i ^ (k >> 2), y_i ^ ((k >> 1) & 1), c_i ^ (k & 1))) for k in range(1, N_DEV)]


def _plan_scatter(src, land):
    x_i, y_i, c_i = lax.axis_index("x"), lax.axis_index("y"), lax.axis_index("c")
    me = 4 * x_i + 2 * y_i + c_i
    plan = []
    for k in range(1, N_DEV):
        px, py, pc = x_i ^ (k >> 2), y_i ^ ((k >> 1) & 1), c_i ^ (k & 1)
        plan.append((src.at[4 * px + 2 * py + pc], land.at[me], (px, py, pc)))
    return plan


def _pair_sum(g, recv, core, name):
    _, rows, cols = g.shape
    tr = ROW_TILE if rows % ROW_TILE == 0 else rows

    def body(core_ref, g_ref, r_ref, o_ref):
        o_ref[...] = (g_ref[...].astype(F32) + r_ref[...].astype(F32)).astype(o_ref.dtype)

    grid_spec = pltpu.PrefetchScalarGridSpec(
        num_scalar_prefetch=1, grid=(4, rows // tr),
        in_specs=[pl.BlockSpec((None, tr, cols), lambda k, i, core_ref: (2 * k + core_ref[0], i, 0)),
                  pl.BlockSpec((None, tr, cols), lambda k, i, core_ref: (k, i, 0))],
        out_specs=pl.BlockSpec((None, tr, cols), lambda k, i, core_ref: (k, i, 0)))
    return pl.pallas_call(body, name=name, grid_spec=grid_spec, out_shape=jax.ShapeDtypeStruct((4, rows, cols), g.dtype))(
        core, g, recv)


def _adam_math(g, w, m, v):
    m_new = ADAM_B1 * m + (1.0 - ADAM_B1) * g
    v_new = ADAM_B2 * v + (1.0 - ADAM_B2) * (g * g)
    m_hat = m_new / (1.0 - ADAM_B1 ** ADAM_STEP)
    v_hat = v_new / (1.0 - ADAM_B2 ** ADAM_STEP)
    return -ADAM_LR * (m_hat / (jnp.sqrt(v_hat) + ADAM_EPS) + ADAM_WD * w), m_new, v_new


def _adam(slots, w, m, v, name, own=None, own_idx=None):
    n_slot, rows, cols = slots.shape
    tr = ROW_TILE if rows % ROW_TILE == 0 else rows
    has_own = own is not None

    def body(*refs):
        if has_own:
            idx_ref, own_ref, refs = refs[0], refs[1], refs[2:]
        s_ref, w_ref, m_ref, v_ref, g_ref, d_ref, mo_ref, vo_ref = refs
        g = own_ref[...].astype(F32) if has_own else s_ref[0].astype(F32)
        for k in range(0 if has_own else 1, n_slot):
            part = s_ref[k].astype(F32)
            g = g + (jnp.where(idx_ref[0] == k, 0.0, part) if has_own else part)
        g_ref[...] = g
        d_ref[...], mo_ref[...], vo_ref[...] = _adam_math(g, w_ref[...], m_ref[...], v_ref[...])

    spec = pl.BlockSpec((tr, cols), lambda i, *_: (i, 0))
    in_specs = [pl.BlockSpec((n_slot, tr, cols), lambda i, *_: (0, i, 0)), spec, spec, spec]
    if has_own:
        in_specs = [pl.BlockSpec((None, tr, cols), lambda i, idx: (idx[0], i, 0))] + in_specs
    grid_spec = pltpu.PrefetchScalarGridSpec(num_scalar_prefetch=1 if has_own else 0, grid=(rows // tr,), in_specs=in_specs,
                                             out_specs=[spec] * 4)
    ins = ([own_idx, own] if has_own else []) + [slots, w, m, v]
    return pl.pallas_call(
        body, name=name, grid_spec=grid_spec, out_shape=[jax.ShapeDtypeStruct((rows, cols), F32)] * 4,
        compiler_params=pltpu.CompilerParams(dimension_semantics=("parallel",)),
    )(*ins)


PACK_ROWS, PACK_W = 24, 1536
REPL_W = (("ssd_conv_b", 1536), ("ssd_dt_bias", 16), ("ssd_A_log", 16), ("ssd_D", 16), ("ssd_norm_w", 1024),
          ("mla_q_norm_w", 384), ("mla_kv_norm_w", 256), ("mla_out_norm_w", 1024), ("ln_mix_g", 1024),
          ("ln_mix_b", 1024), ("ln_ffn_g", 1024), ("ln_ffn_b", 1024))
LOSS_ROW = 4 + len(REPL_W)


def _pack_small(conv_w_grad, grads, loss, name="pack_small"):
    def body(*refs):
        cw_ref, g_refs, loss_ref, o_ref = refs[0], refs[1:1 + len(REPL_W)], refs[1 + len(REPL_W)], refs[-1]
        o_ref[...] = jnp.zeros_like(o_ref)
        o_ref[0:4, :] = cw_ref[...]
        for i, g_ref in enumerate(g_refs):
            o_ref[4 + i:5 + i, 0:g_ref.shape[1]] = g_ref[...]
        o_ref[LOSS_ROW:LOSS_ROW + 1, 0:LANE] = loss_ref[...]

    return pl.pallas_call(body, name=name, out_shape=jax.ShapeDtypeStruct((PACK_ROWS, PACK_W), F32))(conv_w_grad, *grads, loss)


def _adam_small(gathered, wmv, name="adam_small"):
    def body(*refs):
        s_ref = refs[0]
        in_refs = refs[1:1 + 3 * len(REPL_W)]
        cw_ref, loss_ref = refs[1 + 3 * len(REPL_W)], refs[2 + 3 * len(REPL_W)]
        out_refs = refs[3 + 3 * len(REPL_W):-1]
        tot = refs[-1]
        acc = s_ref[0]
        for k in range(1, N_DEV):
            acc = acc + s_ref[k]
        tot[...] = acc
        cw_ref[...] = tot[0:4, :]
        loss_ref[...] = tot[LOSS_ROW:LOSS_ROW + 1, 0:LANE]
        for i, (_, width) in enumerate(REPL_W):
            g = tot[4 + i:5 + i, 0:width]
            w_ref, m_ref, v_ref = in_refs[3 * i:3 * i + 3]
            g_ref, d_ref, mo_ref, vo_ref = out_refs[4 * i:4 * i + 4]
            g_ref[...] = g
            d_ref[...], mo_ref[...], vo_ref[...] = _adam_math(g, w_ref[...], m_ref[...], v_ref[...])

    flat_in = [a for triple in wmv for a in triple]
    out_shape = [jax.ShapeDtypeStruct((4, PACK_W), F32), jax.ShapeDtypeStruct((1, LANE), F32)]
    for _, width in REPL_W:
        out_shape += [jax.ShapeDtypeStruct((1, width), F32)] * 4
    res = pl.pallas_call(body, name=name, out_shape=out_shape, scratch_shapes=[pltpu.VMEM((PACK_ROWS, PACK_W), F32)])(
        gathered, *flat_in)
    return res[0], res[1], [res[2 + 4 * i:6 + 4 * i] for i in range(len(REPL_W))]


def _cols_full(g):
    return jnp.transpose(g, (1, 0, 2)).reshape(g.shape[1], -1)


def _cols_split(full):
    k_dim, n_dim = full.shape
    return jnp.transpose(full.reshape(k_dim, N_DEV, n_dim // N_DEV), (1, 0, 2))


PROJ_BLOCK = {"z": (1024, 0), "dt": (LANE, 8), "q_c": (MLA_Q_RANK, 3), "xbc": (SSD_XBC, 1), "kv_c": (MLA_KV_RANK, 12),
              "k_rope": (LANE, 26)}


def _win_pad(wt):
    z = lambda n: jnp.zeros((n, wt.shape[1]), wt.dtype)
    return jnp.concatenate([wt[:1024], wt[2560:2576], z(112), wt[2576:2960], wt[1024:2560], wt[2960:3216], wt[3216:3248],
                            z(96)], axis=0)


def _win_unpad(wt):
    return jnp.concatenate([wt[:1024], wt[1536:3072], wt[1024:1040], wt[1152:1536], wt[3072:3328], wt[3328:3360]], axis=0)


def _heads_split_t(wt, a, b):
    w3 = wt.reshape(MLA_HEADS, a + b, wt.shape[1])
    return jnp.concatenate([w3[:, :a].reshape(-1, wt.shape[1]), w3[:, a:].reshape(-1, wt.shape[1])], axis=0)


def _heads_merge_t(wt, a, b):
    wa = wt[:MLA_HEADS * a].reshape(MLA_HEADS, a, wt.shape[1])
    wb = wt[MLA_HEADS * a:].reshape(MLA_HEADS, b, wt.shape[1])
    return jnp.concatenate([wa, wb], axis=1).reshape(-1, wt.shape[1])


def _heads_split(w, a, b):
    k_dim = w.shape[0]
    w3 = w.reshape(k_dim, MLA_HEADS, a + b)
    return jnp.concatenate([w3[:, :, :a].reshape(k_dim, -1), w3[:, :, a:].reshape(k_dim, -1)], axis=1)


def _heads_merge(w, a, b):
    k_dim = w.shape[0]
    wa = w[:, :MLA_HEADS * a].reshape(k_dim, MLA_HEADS, a)
    wb = w[:, MLA_HEADS * a:].reshape(k_dim, MLA_HEADS, b)
    return jnp.concatenate([wa, wb], axis=2).reshape(k_dim, -1)


def _pad_lanes(v, width=LANE):
    return jnp.concatenate([v, jnp.zeros((v.shape[0], width - v.shape[1]), v.dtype)], axis=1)


def _local_step(x, p, positions, tgt, W, P, comm=None):
    comm = comm or {}
    zero_tok = jnp.zeros((8, LANE), F32)
    s_dim = x.shape[0]
    inv_freq = 1.0 / (ROPE_BASE ** (jnp.arange(0, MLA_ROPE, 2, dtype=F32) / MLA_ROPE))
    ang = positions.astype(F32)[:, None] * inv_freq
    cos, sin = jnp.cos(ang), jnp.sin(ang)
    cos32 = jnp.concatenate([cos, cos], axis=1)
    sin32 = jnp.concatenate([-sin, sin], axis=1)
    cos512, sin512 = jnp.tile(cos32, (1, 16)), jnp.tile(sin32, (1, 16))
    cos128, sin128 = jnp.tile(cos32, (1, 4)), jnp.tile(sin32, (1, 4))
    bias_p, alog_p = _pad_lanes(P["ssd_dt_bias"]), _pad_lanes(P["ssd_A_log"])
    d_x = jnp.repeat(P["ssd_D"], SSD_HEAD_DIM, axis=1)

    xb, pb = x.astype(BF16), p.astype(BF16)
    if "token0" in comm:
        xb = (x + comm["token0"][0, 0]).astype(BF16)
    proj = _mm(xb, W["w_in"], tb=True, name="mm_in")
    z, qc, kvc, kr = [(proj,) + PROJ_BLOCK[n] for n in ("z", "q_c", "kv_c", "k_rope")]
    xbca = _conv_fwd(proj, PROJ_BLOCK["xbc"][1], P["ssd_conv_w"], P["ssd_conv_b"])
    y, states = _ssd_fwd(xbca, proj, PROJ_BLOCK["dt"][1], bias_p, alog_p, d_x)
    (yssd,) = _rowwise(_gate_rms, [y, z], [P["ssd_norm_w"]], [(1024, BF16)], name="ssd_gate_norm")
    (qn,) = _rowwise(_rms, [qc], [P["mla_q_norm_w"]], [(MLA_Q_RANK, BF16)], name="q_norm")
    (kvn,) = _rowwise(_rms, [kvc], [P["mla_kv_norm_w"]], [(MLA_KV_RANK, BF16)], name="kv_norm")
    q = _mm(qn, W["mla_w_q_b"], tb=True, name="mm_q")
    kv = _mm(kvn, W["mla_w_kv_b"], name="mm_kv")
    (qr,) = _rowwise(_rope_fwd_fn, [(q, 512, 2), cos512, sin512], [], [512], name="rope_q")
    (krt,) = _rowwise(lambda u, c, s: _spread4(_rope_fwd_fn(u, c, s)), [kr, cos128, sin128], [], [LANE], name="rope_k")
    att = _att_fwd(q, qr, kv, krt)
    (ymla,) = _rowwise(_rms, [att], [P["mla_out_norm_w"]], [(1024, BF16)], name="out_norm")
    ycat = jnp.concatenate([yssd, ymla], axis=1)
    if "late_weights" in comm:
        W = {**W, **comm["late_weights"](ycat)}
    mix = _mm(ycat, W["w_out"], name="mm_out")
    f_h1 = lambda xv, mv, g, b: _ln(ALPHA * xv + mv, g, b)
    h1, h1b = _rowwise(lambda *a: (f_h1(*a),) * 2, [x, mix], [P["ln_mix_g"], P["ln_mix_b"]], [1024, (1024, BF16)],
                       name="ln_mix")
    fb = D_FF // N_DEV
    hg = _mm(h1b, W["w_ffn_gate"], tb=True, b_blk="n", o_blk="n", out_dtype=BF16, name="mm_gate")
    hu = _mm(h1b, W["w_ffn_up"], tb=True, b_blk="n", o_blk="n", out_dtype=BF16, name="mm_up")
    pg = _mm(h1b, W["w_ple_gate"], name="mm_ple_gate")
    pp = _mm(pb, W["w_ple_proj"], name="mm_ple")
    hg2, hu2 = hg.reshape(N_DEV * s_dim, fb), hu.reshape(N_DEV * s_dim, fb)
    (act,) = _rowwise(lambda g, u: _silu(g.astype(F32)) * u.astype(F32), [hg2, hu2], [], [(fb, BF16)], name="swiglu",
                      tr=512)
    act3 = act.reshape(N_DEV, s_dim, fb)
    ffn = _mm(act3, W["w_ffn_down"], a_blk="k", b_blk="k", name="mm_down")

    f_h2 = lambda hv, fv, pg, ppv, g, b: _ln(ALPHA * hv + fv + _sigmoid(pg) * ppv, g, b)

    def final_fn(hv, fv, pg, ppv, tv, g, b):
        h2, pull = jax.vjp(f_h2, hv, fv, pg, ppv, g, b)
        diff = h2 - tv
        loss = 0.5 * jnp.sum(jnp.mean(diff * diff, axis=-1, keepdims=True), axis=0, keepdims=True)
        d_h, d_f, d_pg, d_pp, d_g, d_b = pull(diff * (1.0 / D_MODEL))
        return d_h, d_f, d_pg, d_pp, d_g, d_b, jnp.broadcast_to(loss, (1, LANE))

    dh1_a, dffn, dpg, dpp, g_ffn_g, g_ffn_b, loss = _rowwise(
        final_fn, [h1, ffn, pg, pp, tgt], [P["ln_ffn_g"], P["ln_ffn_b"]], [1024] + [(1024, BF16)] * 3,
        [1024, 1024, LANE], name="final")

    G = {}
    dact = _mm(dffn, W["w_ffn_down"], tb=True, b_blk="n", o_blk="n", name="mm_down_dx")
    G["w_ffn_down"] = _mm(act3, dffn, ta=True, a_blk="m", o_blk="m", out_dtype=GRAD_DT, name="mm_down_dw")

    def swiglu_bwd(g, u, d):
        g, u = g.astype(F32), u.astype(F32)
        sg = _sigmoid(g)
        return d * u * (sg * (1.0 + g * (1.0 - sg))), d * (g * sg)

    dg, du = _rowwise(swiglu_bwd, [hg2, hu2, dact.reshape(N_DEV * s_dim, fb)], [], [(fb, BF16)] * 2, name="swiglu_bwd",
                      tr=512)
    dg3, du3 = dg.reshape(N_DEV, s_dim, fb), du.reshape(N_DEV, s_dim, fb)
    dh1 = _mm(dg3, W["w_ffn_gate"], a_blk="k", b_blk="k", add=dh1_a, name="mm_gate_dx")
    dh1 = _mm(du3, W["w_ffn_up"], a_blk="k", b_blk="k", add=dh1, name="mm_up_dx")
    dh1 = _mm(dpg, W["w_ple_gate"], tb=True, add=dh1, name="mm_ple_gate_dx")
    G["w_ffn_gate"] = _mm(dg3, h1b, ta=True, a_blk="m", o_blk="m", out_dtype=GRAD_DT, name="mm_gate_dw")
    G["w_ffn_up"] = _mm(du3, h1b, ta=True, a_blk="m", o_blk="m", out_dtype=GRAD_DT, name="mm_up_dw")
    G["w_ple_gate"] = _mm(h1b, dpg, ta=True, out_dtype=GRAD_DT, name="mm_ple_gate_dw")
    G["w_ple_proj"] = _mm(pb, dpp, ta=True, out_dtype=GRAD_DT, name="mm_ple_dw")
    dx_a, dmix, g_mix_g, g_mix_b = _rowwise(
        lambda xv, mv, dv, g, b: _vjp_rows(f_h1)(xv, mv, g, b, dv), [x, mix, dh1], [P["ln_mix_g"], P["ln_mix_b"]],
        [1024, (1024, BF16)], [1024, 1024], name="ln_mix_bwd")
    dycat = _mm(dmix, W["w_out"], tb=True, name="mm_out_dx")
    G["w_out"] = _mm(ycat, dmix, ta=True, out_dtype=GRAD_DT, name="mm_out_dw")

    tok1 = comm["ffn_grads"](G) if "ffn_grads" in comm else zero_tok
    datt, g_out_norm = _rowwise(lambda a, dv, w, t: _vjp_rows(_rms)(a, w, dv + jnp.min(t)), [att, (dycat, 1024, 1)],
                                [P["mla_out_norm_w"], tok1], [1024], [1024], name="out_norm_bwd")
    dqn_nope, dqr, dkn, dv, dkrt = _att_bwd(q, qr, kv, krt, att, datt)
    dkv = jnp.concatenate([dkn, dv], axis=1)
    (dq_rope,) = _rowwise(lambda d0, d1, c, s: _rope_bwd_fn(d0 + d1, c, s), [(dqr, 512, 0), (dqr, 512, 1), cos512, sin512],
                          [], [(512, BF16)], name="rope_q_bwd")

    def rope_k_bwd(*a):
        d = _spread4(functools.reduce(lambda u, w: u + w, a[:-2]))
        lane = lax.broadcasted_iota(jnp.int32, d.shape, 1)
        return _rope_bwd_fn(jnp.where(lane < MLA_ROPE, d, 0.0), a[-2], a[-1])

    (dkr,) = _rowwise(rope_k_bwd, [(dkrt, LANE, k) for k in range(MLA_HEADS // 2)] + [cos128, sin128], [], [(LANE, BF16)],
                      name="rope_k_bwd")
    dq = jnp.concatenate([dqn_nope, dq_rope], axis=1)
    dqn = _mm(dq, W["mla_w_q_b"], name="mm_q_dx")
    G["mla_w_q_b"] = _mm(dq, qn, ta=True, out_dtype=GRAD_DT, name="mm_q_dw")
    dkvn = _mm(dkv, W["mla_w_kv_b"], tb=True, name="mm_kv_dx")
    G["mla_w_kv_b"] = _mm(kvn, dkv, ta=True, out_dtype=GRAD_DT, name="mm_kv_dw")
    dqc, g_q_norm = _rowwise(lambda a, dv, w: _vjp_rows(_rms)(a, w, dv), [qc, dqn], [P["mla_q_norm_w"]],
                             [(MLA_Q_RANK, BF16)], [MLA_Q_RANK], name="q_norm_bwd")
    dkvc, g_kv_norm = _rowwise(lambda a, dv, w: _vjp_rows(_rms)(a, w, dv), [kvc, dkvn], [P["mla_kv_norm_w"]],
                               [(MLA_KV_RANK, BF16)], [MLA_KV_RANK], name="kv_norm_bwd")

    dy, dz, g_ssd_norm = _rowwise(lambda yv, zv, dv, w, t: _vjp_rows(_gate_rms)(yv, zv, w, dv + jnp.min(t)),
                                  [y, z, (dycat, 1024, 0)], [P["ssd_norm_w"], tok1], [1024, (1024, BF16)], [1024],
                                  name="ssd_gate_norm_bwd")
    dxbca, ddtr, g_dt_bias, g_alog, g_d = _ssd_bwd(xbca, proj, PROJ_BLOCK["dt"][1], bias_p, alog_p, d_x, states, dy)
    da, g_conv_w, g_conv_b = _conv_bwd_pre(proj, PROJ_BLOCK["xbc"][1], P["ssd_conv_w"], P["ssd_conv_b"], dxbca)
    dxbc = _conv_bwd_in(da, P["ssd_conv_w"])

    dproj = jnp.concatenate([dz, ddtr, dqc, dxbc, dkvc, dkr], axis=1)
    grad_x = _mm(dproj, W["w_in"], add=dx_a, name="mm_in_dx")
    G["w_in"] = _mm(dproj, xb, ta=True, out_dtype=GRAD_DT, name="mm_in_dw")

    small = {
        "ssd_conv_b": g_conv_b, "ssd_dt_bias": g_dt_bias, "ssd_A_log": g_alog, "ssd_D": g_d, "ssd_norm_w": g_ssd_norm,
        "mla_q_norm_w": g_q_norm, "mla_kv_norm_w": g_kv_norm, "mla_out_norm_w": g_out_norm, "ln_mix_g": g_mix_g,
        "ln_mix_b": g_mix_b, "ln_ffn_g": g_ffn_g, "ln_ffn_b": g_ffn_b,
    }
    return grad_x, G, _pack_small(g_conv_w, [small[n] for n, _ in REPL_W], loss)


def kernel(x, p, positions, w_in, ssd_conv_w, ssd_conv_b, ssd_dt_bias, ssd_A_log, ssd_D, ssd_norm_w, mla_q_norm_w, mla_w_q_b, mla_kv_norm_w, mla_w_kv_b, mla_out_norm_w, w_out, ln_mix_g, ln_mix_b, w_ffn_gate, w_ffn_up, w_ffn_down, w_ple_gate, w_ple_proj, ln_ffn_g, ln_ffn_b, loss_target, m_w_in, m_ssd_conv_w, m_ssd_conv_b, m_ssd_dt_bias, m_ssd_A_log, m_ssd_D, m_ssd_norm_w, m_mla_q_norm_w, m_mla_w_q_b, m_mla_kv_norm_w, m_mla_w_kv_b, m_mla_out_norm_w, m_w_out, m_ln_mix_g, m_ln_mix_b, m_w_ffn_gate, m_w_ffn_up, m_w_ffn_down, m_w_ple_gate, m_w_ple_proj, m_ln_ffn_g, m_ln_ffn_b, v_w_in, v_ssd_conv_w, v_ssd_conv_b, v_ssd_dt_bias, v_ssd_A_log, v_ssd_D, v_ssd_norm_w, v_mla_q_norm_w, v_mla_w_q_b, v_mla_kv_norm_w, v_mla_w_kv_b, v_mla_out_norm_w, v_w_out, v_ln_mix_g, v_ln_mix_b, v_w_ffn_gate, v_w_ffn_up, v_w_ffn_down, v_w_ple_gate, v_w_ple_proj, v_ln_ffn_g, v_ln_ffn_b):
    args = dict(locals())
    core = lax.axis_index("c")
    me = 4 * lax.axis_index("x") + 2 * lax.axis_index("y") + core

    conv_sh = ssd_conv_w[0]
    conv_hi = conv_sh.astype(BF16)
    conv_lo = (conv_sh - conv_hi.astype(F32)).astype(BF16)
    stored = lambda n, pre="": jnp.transpose(args[pre + n][0]) if n in TRANSPOSED else args[pre + n][0]
    shards = {n: stored(n).astype(BF16) for n in BIG}
    rows_full = lambda g: g.reshape(-1, g.shape[2])
    core_arr = core.astype(jnp.int32).reshape(1)

    early = _gather_many([shards[n] for n in EARLY] + [jnp.concatenate([conv_hi, conv_lo], axis=0)], "gather_early")
    gw = dict(zip(EARLY, early[:-1]))
    conv_g = early[-1].astype(F32)
    W = {
        "w_in": _win_pad(rows_full(gw["w_in"])),
        "mla_w_q_b": _heads_split_t(rows_full(gw["mla_w_q_b"]), MLA_NOPE, MLA_ROPE),
        "mla_w_kv_b": _heads_split(_cols_full(gw["mla_w_kv_b"]), MLA_NOPE, MLA_V),
    }
    P = {n: args[n] for n, _ in REPL_W}
    P["ssd_conv_w"] = _cols_full(conv_g[:, :4] + conv_g[:, 4:])

    lands = [lax.dynamic_update_slice(lax.empty((N_DEV,) + shards[n].shape, BF16), shards[n][None], (me, 0, 0)) for n in LATE]
    late_sems = _split_start([shards[n] for n in LATE], lands, _plan_broadcast, N_DEV - 1, "gather_late_start")

    def late_weights(after):
        _, got = _split_wait(*late_sems[:4], after, _plan_broadcast, "gather_late_wait")
        lw = dict(zip(LATE, got))
        return {"w_out": rows_full(lw["w_out"]), "w_ple_gate": rows_full(lw["w_ple_gate"]),
                "w_ple_proj": _cols_full(lw["w_ple_proj"]), "w_ffn_gate": lw["w_ffn_gate"], "w_ffn_up": lw["w_ffn_up"],
                "w_ffn_down": lw["w_ffn_down"]}

    def to_blocks(n, g):
        if n in OWNER_BLOCKED:
            return g
        if n == "w_in":
            g = _win_unpad(g)
        elif n == "mla_w_q_b":
            g = _heads_merge_t(g, MLA_NOPE, MLA_ROPE)
        elif n == "mla_w_kv_b":
            g = _heads_merge(g, MLA_NOPE, MLA_V)
        if n in ROW_SHARDED or n in TRANSPOSED:
            return g.reshape(N_DEV, -1, g.shape[1])
        return _cols_split(g)

    flight = {}

    def ffn_grads(G):
        gl = [to_blocks(n, G[n]) for n in LATE_GRADS]
        flight["grads"] = _split_start(gl, [lax.empty(g.shape, g.dtype) for g in gl], _plan_scatter, N_DEV - 1, "grads_start")
        return flight["grads"][4]

    grad_x, G, packed = _local_step(x[0], p[0, 0], positions[0], loss_target[0], W, P,
                                    comm={"token0": late_sems[4], "late_weights": late_weights, "ffn_grads": ffn_grads})

    wmv = lambda n: (stored(n), stored(n, "m_"), stored(n, "v_"))
    mine, recv = _split_wait(*flight["grads"][:4], grad_x, _plan_scatter, "grads_wait")
    me_arr = me.astype(jnp.int32).reshape(1)
    big_out = {n: _adam(r, *wmv(n), "adam_" + n, own=g, own_idx=me_arr) for n, g, r in zip(LATE_GRADS, mine, recv)}

    glist = [to_blocks(n, G[n]) for n in LAST_GRADS]
    from_sibling = _pair_exchange(glist, "exchange_pairs")
    sums = [_pair_sum(g, r, core_arr, "pair_sum_" + n) for n, g, r in zip(LAST_GRADS, glist, from_sibling)]
    recv = _chip_exchange(sums, "exchange_chips")
    big_out.update({n: _adam(r, *wmv(n), "adam_" + n) for n, r in zip(LAST_GRADS, recv)})

    small_all = _all_gather(packed, "gather_small")
    conv_sum, loss_row, small_out = _adam_small(small_all, [(args[n], args["m_" + n], args["v_" + n]) for n, _ in REPL_W])
    conv_grad = lax.dynamic_slice_in_dim(conv_sum, me * 192, 192, axis=1)
    conv_out = _adam(conv_grad[None], conv_sh, m_ssd_conv_w[0], v_ssd_conv_w[0], "adam_conv")
    small_map = {n: small_out[i] for i, (n, _) in enumerate(REPL_W)}

    def outputs(idx):
        res = []
        for n in WEIGHT_ORDER:
            if n == "ssd_conv_w":
                res.append(conv_out[idx][None])
            elif n in big_out:
                res.append((jnp.transpose(big_out[n][idx]) if n in TRANSPOSED else big_out[n][idx])[None])
            else:
                res.append(small_map[n][idx])
        return res

    return (loss_row[0, 0], grad_x[None], *outputs(0), *outputs(1), *outputs(2), *outputs(3))
```

```python
import functools
import math

import numpy as np
import jax
import jax.numpy as jnp
from jax import lax
from jax.experimental import pallas as pl
from jax.experimental.pallas import tpu as pltpu

F32 = jnp.float32
BF16 = jnp.bfloat16
HI = lax.Precision.HIGHEST

N_DEV = 8
D_MODEL = 1024
PLE_DIM = 256
SSD_HEADS = 16
SSD_HEAD_DIM = 64
SSD_INNER = 1024
SSD_STATE = 128
SSD_XBC = 1536
SSD_CHUNK = 128
MLA_HEADS = 16
MLA_Q_RANK = 384
MLA_KV_RANK = 256
MLA_NOPE = 64
MLA_ROPE = 32
MLA_V = 64
ROPE_BASE = 10000.0
D_FF = 2816
IN_WIDTH = 3248
IN_PAD = 3456
ALPHA = 2.0 ** 0.25
EPS = 1e-6
LN_EPS = 1e-5
ATT_SCALE = 1.0 / math.sqrt(MLA_NOPE + MLA_ROPE)
ADAM_LR, ADAM_B1, ADAM_B2, ADAM_EPS, ADAM_WD, ADAM_STEP = 0.001, 0.9, 0.999, 1e-08, 0.01, 10

LANE = 128
MXU_DIM = 256
MM_TM, MM_TN, MM_TK = 1152, 1152, 2048
ROW_TILE = 256
ATT_TQ = 256

GRAD_DT = BF16

BIG = ("w_in", "mla_w_q_b", "mla_w_kv_b", "w_out", "w_ffn_gate", "w_ffn_up", "w_ffn_down", "w_ple_gate", "w_ple_proj")
EARLY = ("w_in", "mla_w_q_b", "mla_w_kv_b")
LATE = ("w_out", "w_ffn_gate", "w_ffn_up", "w_ffn_down", "w_ple_gate", "w_ple_proj")
LATE_GRADS = ("w_ffn_gate", "w_ffn_up", "w_ffn_down", "w_ple_gate", "w_ple_proj", "w_out")
LAST_GRADS = ("w_in", "mla_w_q_b", "mla_w_kv_b")
ROW_SHARDED = ("w_out", "w_ffn_down", "w_ple_gate")
TRANSPOSED = ("w_in", "mla_w_q_b", "w_ffn_gate", "w_ffn_up")
OWNER_BLOCKED = ("w_ffn_gate", "w_ffn_up", "w_ffn_down")
WEIGHT_ORDER = ("w_in", "ssd_conv_w", "ssd_conv_b", "ssd_dt_bias", "ssd_A_log", "ssd_D", "ssd_norm_w", "mla_q_norm_w",
                "mla_w_q_b", "mla_kv_norm_w", "mla_w_kv_b", "mla_out_norm_w", "w_out", "ln_mix_g", "ln_mix_b",
                "w_ffn_gate", "w_ffn_up", "w_ffn_down", "w_ple_gate", "w_ple_proj", "ln_ffn_g", "ln_ffn_b")


def _tile(dim, cap, prefer=None):
    cands = [t for t in range(LANE, min(cap, dim) + 1, LANE) if dim % t == 0]
    if not cands:
        return dim
    if prefer is None:
        return max(cands)
    fill = lambda t: t / (MXU_DIM * -(-t // MXU_DIM))
    best = max(fill(t) for t in cands)
    return min((t for t in cands if fill(t) == best), key=lambda t: abs(t - prefer))


def _dot(a, b, dims=(((1,), (0,)), ((), ())), precision=None):
    return lax.dot_general(a, b, dims, preferred_element_type=F32, precision=precision)


_NT = (((1,), (1,)), ((), ()))
_TN = (((0,), (0,)), ((), ()))


def _mm(a, b, *, ta=False, tb=False, a_blk=None, b_blk=None, o_blk=None, add=None, out_dtype=F32, name):
    ka, ma = a.shape[-2:] if ta else a.shape[-2:][::-1]
    nb, kb = b.shape[-2:] if tb else b.shape[-2:][::-1]
    m_dim = N_DEV * ma if a_blk == "m" else ma
    k_dim = N_DEV * ka if a_blk == "k" else ka
    n_dim = N_DEV * nb if b_blk == "n" else nb
    assert k_dim == (N_DEV * kb if b_blk == "k" else kb)
    tm = ma if a_blk == "m" else (m_dim // N_DEV if o_blk == "m" else _tile(m_dim, MM_TM))
    tn = nb if b_blk == "n" else (n_dim // N_DEV if o_blk == "n" else _tile(n_dim, MM_TN, prefer=1024 if a_blk == "m" else 512))
    tk = ka if a_blk == "k" else (kb if b_blk == "k" else _tile(k_dim, MM_TK, prefer=MM_TK))
    nk = k_dim // tk
    dims = (((0 if ta else 1,), (1 if tb else 0,)), ((), ()))
    has_add = add is not None

    def spec(tile, idx, lead):
        if lead is None:
            return pl.BlockSpec(tile, idx)
        return pl.BlockSpec((None,) + tile, lambda i, j, k: (lead(i, j, k),) + idx(i, j, k))

    def a_idx(i, j, k):
        ii, kk = (0 if a_blk == "m" else i), (0 if a_blk == "k" else k)
        return (kk, ii) if ta else (ii, kk)

    def b_idx(i, j, k):
        jj, kk = (0 if b_blk == "n" else j), (0 if b_blk == "k" else k)
        return (jj, kk) if tb else (kk, jj)

    def o_idx(i, j, k):
        return (0 if o_blk == "m" else i, 0 if o_blk == "n" else j)

    pick = {"m": lambda i, j, k: i, "n": lambda i, j, k: j, "k": lambda i, j, k: k, None: None}
    a_spec = spec((tk, tm) if ta else (tm, tk), a_idx, pick[a_blk])
    b_spec = spec((tn, tk) if tb else (tk, tn), b_idx, pick[b_blk])
    o_spec = spec((tm, tn), o_idx, pick[o_blk])

    def body(*refs):
        if has_add:
            a_ref, b_ref, add_ref, o_ref = refs[:4]
        else:
            a_ref, b_ref, o_ref = refs[:3]
        part = _dot(a_ref[...].astype(BF16), b_ref[...].astype(BF16), dims)
        if nk == 1:
            o_ref[...] = ((part + add_ref[...]) if has_add else part).astype(o_ref.dtype)
            return
        acc = refs[-1]
        k = pl.program_id(2)

        @pl.when(k == 0)
        def _():
            acc[...] = (part + add_ref[...]) if has_add else part

        @pl.when(k > 0)
        def _():
            acc[...] += part

        @pl.when(k == nk - 1)
        def _():
            o_ref[...] = acc[...].astype(o_ref.dtype)

    if o_blk == "m":
        out_shape = (N_DEV, tm, n_dim)
    elif o_blk == "n":
        out_shape = (N_DEV, m_dim, tn)
    else:
        out_shape = (m_dim, n_dim)
    ins = [a, b] + ([add] if has_add else [])
    specs = [a_spec, b_spec] + ([pl.BlockSpec((tm, tn), lambda i, j, k: (i, j))] if has_add else [])
    return pl.pallas_call(
        body, name=name, grid=(m_dim // tm, n_dim // tn, nk), in_specs=specs, out_specs=o_spec,
        out_shape=jax.ShapeDtypeStruct(out_shape, out_dtype),
        scratch_shapes=[pltpu.VMEM((tm, tn), F32)] if nk > 1 else [],
        compiler_params=pltpu.CompilerParams(dimension_semantics=("parallel", "parallel", "arbitrary")),
    )(*ins)


def _rowwise(fn, rows, consts, out_widths, acc_widths=(), *, name, tr=ROW_TILE):
    row_arrays, row_specs = [], []
    first_arr = rows[0][0] if isinstance(rows[0], tuple) else rows[0]
    s_dim = first_arr.shape[-2]
    tr = min(tr, s_dim)
    for r in rows:
        arr, width, cb = r if isinstance(r, tuple) else (r, r.shape[-1], 0)
        row_arrays.append(arr)
        if arr.ndim == 3:
            row_specs.append(pl.BlockSpec((None, tr, width), functools.partial(lambda i, k: (k, i, 0), k=cb)))
        else:
            row_specs.append(pl.BlockSpec((tr, width), functools.partial(lambda i, cb: (i, cb), cb=cb)))
    const_specs = [pl.BlockSpec(c.shape, lambda i: (0, 0)) for c in consts]
    nr, nc, no, na = len(rows), len(consts), len(out_widths), len(acc_widths)

    def body(*refs):
        ins = [r[...] for r in refs[:nr + nc]]
        res = fn(*ins)
        if not isinstance(res, (tuple, list)):
            res = (res,)
        out_refs = refs[nr + nc:nr + nc + no]
        acc_refs = refs[nr + nc + no:]
        for o_ref, val in zip(out_refs, res[:no]):
            o_ref[...] = val.astype(o_ref.dtype)
        first = pl.program_id(0) == 0
        for a_ref, val in zip(acc_refs, res[no:]):
            @pl.when(first)
            def _(a_ref=a_ref, val=val):
                a_ref[...] = val

            @pl.when(jnp.logical_not(first))
            def _(a_ref=a_ref, val=val):
                a_ref[...] += val

    outs = [w if isinstance(w, tuple) else (w, F32) for w in out_widths]
    out_shape = [jax.ShapeDtypeStruct((s_dim, w), dt) for w, dt in outs]
    out_shape += [jax.ShapeDtypeStruct((1, w), F32) for w in acc_widths]
    out_specs = [pl.BlockSpec((tr, w), lambda i: (i, 0)) for w, _ in outs]
    out_specs += [pl.BlockSpec((1, w), lambda i: (0, 0)) for w in acc_widths]
    res = pl.pallas_call(
        body, name=name, grid=(s_dim // tr,), in_specs=row_specs + const_specs, out_specs=out_specs, out_shape=out_shape,
        compiler_params=pltpu.CompilerParams(dimension_semantics=("arbitrary",)),
    )(*row_arrays, *consts)
    return res


def _colsum(v):
    return jnp.sum(v, axis=0, keepdims=True)


def _rms(u, g):
    return u * lax.rsqrt(jnp.mean(u * u, axis=-1, keepdims=True) + EPS) * g


def _ln(u, g, b):
    mu = jnp.mean(u, axis=-1, keepdims=True)
    d = u - mu
    var = jnp.mean(d * d, axis=-1, keepdims=True)
    return d * lax.rsqrt(var + LN_EPS) * g + b


def _sigmoid(v):
    return 1.0 / (1.0 + jnp.exp(-v))


def _silu(v):
    return v * _sigmoid(v)


def _softplus(v):
    y = jnp.exp(-jnp.abs(v))
    w = 1.0 + y
    log1p = jnp.where(w == 1.0, y, jnp.log(w) * y / jnp.where(w == 1.0, 1.0, w - 1.0))
    return jnp.maximum(v, 0.0) + log1p


def _gate_rms(y, z, w):
    return _rms(y * _silu(z), w)


def _vjp_rows(f):
    def fn(*args):
        prim, ct = args[:-1], args[-1]
        _, pull = jax.vjp(f, *prim)
        return pull(ct)
    return fn


def _conv_pre(cur, prev, w, b, first):
    row = lax.broadcasted_iota(jnp.int32, cur.shape, 0)
    acc = cur * w[3:4, :] + b
    for j in (1, 2, 3):
        tail = jnp.where(first, 0.0, pltpu.roll(prev, j, 0))
        acc = acc + jnp.where(row >= j, pltpu.roll(cur, j, 0), tail) * w[3 - j:4 - j, :]
    return acc


def _conv_fwd(u, ucb, w, b, name="conv_fwd"):
    s_dim, width = u.shape[0], w.shape[1]
    tr = min(ROW_TILE, s_dim)

    def body(cur_ref, prev_ref, w_ref, b_ref, o_ref):
        pre = _conv_pre(cur_ref[...], prev_ref[...], w_ref, b_ref[...], pl.program_id(0) == 0)
        o_ref[...] = _silu(pre)

    return pl.pallas_call(
        body, name=name, grid=(s_dim // tr,),
        in_specs=[pl.BlockSpec((tr, width), lambda i: (i, ucb)),
                  pl.BlockSpec((tr, width), lambda i: (jnp.maximum(i - 1, 0), ucb)),
                  pl.BlockSpec(w.shape, lambda i: (0, 0)), pl.BlockSpec(b.shape, lambda i: (0, 0))],
        out_specs=pl.BlockSpec((tr, width), lambda i: (i, 0)), out_shape=jax.ShapeDtypeStruct((s_dim, width), F32),
        compiler_params=pltpu.CompilerParams(dimension_semantics=("arbitrary",)),
    )(u, u, w, b)


def _conv_bwd_pre(u, ucb, w, b, dact, name="conv_bwd_pre"):
    s_dim, width = u.shape[0], w.shape[1]
    tr = min(ROW_TILE, s_dim)

    def body(cur_ref, prev_ref, w_ref, b_ref, d_ref, da_ref, dw_ref, db_ref):
        first = pl.program_id(0) == 0
        cur, prev = cur_ref[...], prev_ref[...]
        pre = _conv_pre(cur, prev, w_ref, b_ref[...], first)
        sg = _sigmoid(pre)
        da = d_ref[...] * (sg * (1.0 + pre * (1.0 - sg)))
        da_ref[...] = da
        row = lax.broadcasted_iota(jnp.int32, cur.shape, 0)

        @pl.when(first)
        def _():
            dw_ref[...] = jnp.zeros_like(dw_ref)
            db_ref[...] = jnp.zeros_like(db_ref)

        db_ref[...] += _colsum(da)
        dw_ref[3:4, :] += _colsum(da * cur)
        for j in (1, 2, 3):
            tail = jnp.where(first, 0.0, pltpu.roll(prev, j, 0))
            sh = jnp.where(row >= j, pltpu.roll(cur, j, 0), tail)
            dw_ref[3 - j:4 - j, :] += _colsum(da * sh)

    return pl.pallas_call(
        body, name=name, grid=(s_dim // tr,),
        in_specs=[pl.BlockSpec((tr, width), lambda i: (i, ucb)),
                  pl.BlockSpec((tr, width), lambda i: (jnp.maximum(i - 1, 0), ucb)),
                  pl.BlockSpec(w.shape, lambda i: (0, 0)), pl.BlockSpec(b.shape, lambda i: (0, 0)),
                  pl.BlockSpec((tr, width), lambda i: (i, 0))],
        out_specs=[pl.BlockSpec((tr, width), lambda i: (i, 0)), pl.BlockSpec(w.shape, lambda i: (0, 0)),
                   pl.BlockSpec(b.shape, lambda i: (0, 0))],
        out_shape=[jax.ShapeDtypeStruct((s_dim, width), F32), jax.ShapeDtypeStruct(w.shape, F32),
                   jax.ShapeDtypeStruct(b.shape, F32)],
        compiler_params=pltpu.CompilerParams(dimension_semantics=("arbitrary",)),
    )(u, u, w, b, dact)


def _conv_bwd_in(da, w, name="conv_bwd_in"):
    s_dim, width = da.shape
    tr = min(ROW_TILE, s_dim)
    n = s_dim // tr

    def body(cur_ref, nxt_ref, w_ref, o_ref):
        last = pl.program_id(0) == n - 1
        cur, nxt = cur_ref[...], nxt_ref[...]
        row = lax.broadcasted_iota(jnp.int32, cur.shape, 0)
        acc = cur * w_ref[3:4, :]
        for j in (1, 2, 3):
            head = jnp.where(last, 0.0, pltpu.roll(nxt, tr - j, 0))
            acc = acc + jnp.where(row < tr - j, pltpu.roll(cur, tr - j, 0), head) * w_ref[3 - j:4 - j, :]
        o_ref[...] = acc.astype(o_ref.dtype)

    return pl.pallas_call(
        body, name=name, grid=(n,),
        in_specs=[pl.BlockSpec((tr, width), lambda i: (i, 0)), pl.BlockSpec((tr, width), lambda i: (jnp.minimum(i + 1, n - 1), 0)),
                  pl.BlockSpec(w.shape, lambda i: (0, 0))],
        out_specs=pl.BlockSpec((tr, width), lambda i: (i, 0)), out_shape=jax.ShapeDtypeStruct((s_dim, width), BF16),
        compiler_params=pltpu.CompilerParams(dimension_semantics=("arbitrary",)),
    )(da, da, w)


def _sel_dot(a, sel, pieces, dims=(((1,), (0,)), ((), ())), sel_left=False):
    sel = sel.astype(BF16)
    acc, rest = None, a
    for _ in range(pieces):
        piece = rest.astype(BF16)
        rest = rest - piece.astype(F32)
        part = _dot(sel, piece, dims) if sel_left else _dot(piece, sel, dims)
        acc = part if acc is None else acc + part
    return acc


def _ssd_consts():
    L = SSD_CHUNK
    tri = np.tril(np.ones((L, L), np.float32))
    expand = np.zeros((LANE, SSD_INNER), np.float32)
    expand128 = np.zeros((LANE, SSD_HEADS * LANE), np.float32)
    for h in range(SSD_HEADS):
        expand[h, h * SSD_HEAD_DIM:(h + 1) * SSD_HEAD_DIM] = 1.0
        expand128[h, h * LANE:(h + 1) * LANE] = 1.0
    return jnp.asarray(tri), jnp.asarray(expand), jnp.asarray(expand128), jnp.asarray(expand.T.copy())


def _ssd_prep(dt_ref, bias_ref, alog_ref, tri_ref, exp_ref, exp128_ref, cs_s, cst_s, ex_s, csx_s):
    L = SSD_CHUNK
    dt = _softplus(dt_ref[...] + bias_ref[...])
    a = -jnp.exp(alog_ref[...])
    cs = _sel_dot(dt * a, tri_ref[...], 3, sel_left=True)
    cs_s[...] = cs
    cst_s[...] = cs.T
    last = cs_s[L - 1:L, :]
    expand = exp_ref[...]
    ex_s[...] = _sel_dot(jnp.exp(cs), expand, 2)
    f_x = _sel_dot(jnp.exp(last - cs), expand, 2)
    dt_x = _sel_dot(dt, expand, 2)
    csx_s[...] = _sel_dot(cs, exp128_ref[...], 3)
    t_x = ex_s[L - 1:L, :]
    return dt, a, dt_x, f_x, t_x


def _decay_matrix(csx_s, cst_s, h, tril):
    seg = csx_s[:, h * LANE:(h + 1) * LANE] - cst_s[h:h + 1, :]
    return jnp.exp(jnp.where(tril, seg, -jnp.inf))


def _ssd_fwd(xbca, dtr, dtcb, bias, alog, d_x, name="ssd_fwd"):
    s_dim = xbca.shape[0]
    L = SSD_CHUNK
    nc = s_dim // L
    tri, expand, expand128, _ = _ssd_consts()

    def body(xs_ref, b_ref, c_ref, dt_ref, bias_ref, alog_ref, dx_ref, tri_ref, exp_ref, exp128_ref,
             y_ref, st_ref, st_s, cs_s, cst_s, ex_s, csx_s):
        @pl.when(pl.program_id(0) == 0)
        def _():
            st_s[...] = jnp.zeros_like(st_s)

        dt, a, dt_x, f_x, t_x = _ssd_prep(dt_ref, bias_ref, alog_ref, tri_ref, exp_ref, exp128_ref, cs_s, cst_s, ex_s, csx_s)
        st_ref[0] = st_s[...]
        row = lax.broadcasted_iota(jnp.int32, (L, L), 0)
        col = lax.broadcasted_iota(jnp.int32, (L, L), 1)
        tril = row >= col
        low = col < SSD_HEAD_DIM
        for g in range(2):
            bg = b_ref[:, g * LANE:(g + 1) * LANE]
            cg = c_ref[:, g * LANE:(g + 1) * LANE].astype(BF16)
            gmat = _dot(cg, bg.astype(BF16), _NT)
            bgt = bg.T.astype(BF16)
            for jj in range(4):
                j = 4 * g + jj
                sl = slice(j * LANE, (j + 1) * LANE)
                xp = xs_ref[:, sl]
                x_dt = xp * dt_x[:, sl]
                xb = x_dt.astype(BF16)
                yd = []
                for e in range(2):
                    lm = _decay_matrix(csx_s, cst_s, 2 * j + e, tril)
                    yd.append(_dot((gmat * lm).astype(BF16), xb))
                stp = st_s[j]
                z = _dot(cg, stp.astype(BF16))
                y_ref[:, sl] = jnp.where(low, yd[0], yd[1]) + ex_s[:, sl] * z + dx_ref[:, sl] * xp
                xf = (x_dt * f_x[:, sl]).astype(BF16)
                st_s[j] = t_x[:, sl] * stp + _dot(bgt, xf)

    const = lambda shape: pl.BlockSpec(shape, lambda c: tuple(0 for _ in shape))
    return pl.pallas_call(
        body, name=name, grid=(nc,),
        in_specs=[pl.BlockSpec((L, 1024), lambda c: (c, 0)), pl.BlockSpec((L, 256), lambda c: (c, 4)),
                  pl.BlockSpec((L, 256), lambda c: (c, 5)), pl.BlockSpec((L, LANE), lambda c: (c, dtcb)),
                  const((1, LANE)), const((1, LANE)), const((1, 1024)), const((L, L)), const((LANE, 1024)),
                  const((LANE, 2048))],
        out_specs=[pl.BlockSpec((L, 1024), lambda c: (c, 0)), pl.BlockSpec((1, 8, LANE, LANE), lambda c: (c, 0, 0, 0))],
        out_shape=[jax.ShapeDtypeStruct((s_dim, 1024), F32), jax.ShapeDtypeStruct((nc, 8, LANE, LANE), F32)],
        scratch_shapes=[pltpu.VMEM((8, LANE, LANE), F32), pltpu.VMEM((L, LANE), F32), pltpu.VMEM((LANE, L), F32),
                        pltpu.VMEM((L, 1024), F32), pltpu.VMEM((L, 2048), F32)],
        compiler_params=pltpu.CompilerParams(dimension_semantics=("arbitrary",)),
    )(xbca, xbca, xbca, dtr, bias, alog, d_x, tri, expand, expand128)


def _ssd_bwd(xbca, dtr, dtcb, bias, alog, d_x, states, dy, name="ssd_bwd"):
    s_dim = xbca.shape[0]
    L = SSD_CHUNK
    nc = s_dim // L
    tri, expand, expand128, expand_t = _ssd_consts()

    def body(xs_ref, b_ref, c_ref, dt_ref, bias_ref, alog_ref, dx_ref, tri_ref, exp_ref, exp128_ref, expt_ref,
             st_ref, dy_ref, dxbc_ref, ddt_ref, dbias_ref, dalog_ref, dd_ref,
             dst_s, cs_s, cst_s, ex_s, csx_s, dcsx_s, ddtx_s, dcol_s, drow_s, dlast_s, dd_s):
        @pl.when(pl.program_id(0) == 0)
        def _():
            dst_s[...] = jnp.zeros_like(dst_s)
            dbias_ref[...] = jnp.zeros_like(dbias_ref)
            dalog_ref[...] = jnp.zeros_like(dalog_ref)
            dd_s[...] = jnp.zeros_like(dd_s)

        dt, a, dt_x, f_x, t_x = _ssd_prep(dt_ref, bias_ref, alog_ref, tri_ref, exp_ref, exp128_ref, cs_s, cst_s, ex_s, csx_s)
        row = lax.broadcasted_iota(jnp.int32, (L, L), 0)
        col = lax.broadcasted_iota(jnp.int32, (L, L), 1)
        tril = row >= col
        low = col < SSD_HEAD_DIM
        dcol_s[...] = jnp.zeros_like(dcol_s)
        drow_s[...] = jnp.zeros_like(drow_s)
        for g in range(2):
            bg = b_ref[:, g * LANE:(g + 1) * LANE]
            cg = c_ref[:, g * LANE:(g + 1) * LANE]
            bgb, cgb = bg.astype(BF16), cg.astype(BF16)
            gmat = _dot(cgb, bgb, _NT)
            d_g = jnp.zeros((L, L), F32)
            d_b = jnp.zeros((L, LANE), F32)
            d_c = jnp.zeros((L, LANE), F32)
            for jj in range(4):
                j = 4 * g + jj
                sl = slice(j * LANE, (j + 1) * LANE)
                xp = xs_ref[:, sl]
                dtp = dt_x[:, sl]
                x_dt = xp * dtp
                xb = x_dt.astype(BF16)
                dyp = dy_ref[:, sl]
                dd_s[:, sl] += _colsum(dyp * xp)
                d_xdt = jnp.zeros((L, LANE), F32)
                for e in range(2):
                    h = 2 * j + e
                    lm = _decay_matrix(csx_s, cst_s, h, tril)
                    m = gmat * lm
                    dye = jnp.where(low if e == 0 else jnp.logical_not(low), dyp, 0.0).astype(BF16)
                    d_m = jnp.where(tril, _dot(dye, xb, _NT), 0.0)
                    d_xdt = d_xdt + _dot(m.astype(BF16), dye, _TN)
                    d_g = d_g + d_m * lm
                    w = d_m * m
                    dcol_s[...] += jnp.where(col == h, jnp.sum(w, axis=1, keepdims=True), 0.0)
                    drow_s[...] += jnp.where(row == h, jnp.sum(w, axis=0, keepdims=True), 0.0)
                stp = st_ref[0, j]
                stb = stp.astype(BF16)
                dstn = dst_s[j]
                dstb = dstn.astype(BF16)
                e_p = ex_s[:, sl]
                f_p = f_x[:, sl]
                t_p = t_x[:, sl]
                z = _dot(cgb, stb)
                d_z = (e_p * dyp).astype(BF16)
                d_c = d_c + _dot(d_z, stb, _NT)
                d_xf = _dot(bgb, dstb)
                d_b = d_b + _dot((x_dt * f_p).astype(BF16), dstb, _NT)
                d_xdt = d_xdt + f_p * d_xf
                d_f = x_dt * d_xf * f_p
                dcsx_s[:, sl] = dyp * e_p * z - d_f
                dlast_s[:, sl] = _colsum(d_f) + _colsum(dstn * stp) * t_p
                dst_s[j] = _dot(cgb, d_z, _TN) + t_p * dstn
                dxbc_ref[:, sl] = dx_ref[:, sl] * dyp + d_xdt * dtp
                ddtx_s[:, sl] = d_xdt * xp
            d_gb = d_g.astype(BF16)
            dxbc_ref[:, 1024 + g * LANE:1024 + (g + 1) * LANE] = d_b + _dot(d_gb, cgb, _TN)
            dxbc_ref[:, 1280 + g * LANE:1280 + (g + 1) * LANE] = d_c + _dot(d_gb, bgb)

        expt = expt_ref[...]
        dlast = _sel_dot(jnp.broadcast_to(dlast_s[...], (8, 1024)), expt, 3)
        d_cs = dcol_s[...] - drow_s[...].T + _sel_dot(dcsx_s[...], expt, 3)
        rown = lax.broadcasted_iota(jnp.int32, (L, LANE), 0)
        d_cs = d_cs + jnp.where(rown == L - 1, jnp.sum(dlast, axis=0, keepdims=True) * 0.125, 0.0)
        d_da = _sel_dot(d_cs, tri_ref[...], 3, _TN, sel_left=True)
        d_dt = d_da * a + _sel_dot(ddtx_s[...], expt, 3)
        dalog_ref[...] += _colsum(d_da * dt) * a
        d_raw = d_dt * _sigmoid(dt_ref[...] + bias_ref[...])
        ddt_ref[...] = d_raw.astype(ddt_ref.dtype)
        dbias_ref[...] += _colsum(d_raw)
        dd8 = _sel_dot(jnp.broadcast_to(dd_s[...], (8, 1024)), expt, 3)
        dd_ref[...] = jnp.sum(dd8, axis=0, keepdims=True) * 0.125

    const = lambda shape: pl.BlockSpec(shape, lambda c: tuple(0 for _ in shape))
    rev = lambda cb: (lambda c: (nc - 1 - c, cb))
    return pl.pallas_call(
        body, name=name, grid=(nc,),
        in_specs=[pl.BlockSpec((L, 1024), rev(0)), pl.BlockSpec((L, 256), rev(4)), pl.BlockSpec((L, 256), rev(5)),
                  pl.BlockSpec((L, LANE), rev(dtcb)), const((1, LANE)), const((1, LANE)), const((1, 1024)), const((L, L)),
                  const((LANE, 1024)), const((LANE, 2048)), const((1024, LANE)),
                  pl.BlockSpec((1, 8, LANE, LANE), lambda c: (nc - 1 - c, 0, 0, 0)), pl.BlockSpec((L, 1024), rev(0))],
        out_specs=[pl.BlockSpec((L, SSD_XBC), rev(0)), pl.BlockSpec((L, LANE), rev(0)), const((1, LANE)), const((1, LANE)),
                   const((1, LANE))],
        out_shape=[jax.ShapeDtypeStruct((s_dim, SSD_XBC), F32), jax.ShapeDtypeStruct((s_dim, LANE), BF16),
                   jax.ShapeDtypeStruct((1, LANE), F32), jax.ShapeDtypeStruct((1, LANE), F32),
                   jax.ShapeDtypeStruct((1, LANE), F32)],
        scratch_shapes=[pltpu.VMEM((8, LANE, LANE), F32), pltpu.VMEM((L, LANE), F32), pltpu.VMEM((LANE, L), F32),
                        pltpu.VMEM((L, 1024), F32), pltpu.VMEM((L, 2048), F32), pltpu.VMEM((L, 1024), F32),
                        pltpu.VMEM((L, 1024), F32), pltpu.VMEM((L, LANE), F32), pltpu.VMEM((LANE, L), F32),
                        pltpu.VMEM((1, 1024), F32), pltpu.VMEM((1, 1024), F32)],
        compiler_params=pltpu.CompilerParams(dimension_semantics=("arbitrary",)),
    )(xbca, xbca, xbca, dtr, bias, alog, d_x, tri, expand, expand128, expand_t, states, dy)


def _swap_halves(u):
    width = u.shape[1]
    lane = lax.broadcasted_iota(jnp.int32, u.shape, 1)
    return jnp.where(lane % MLA_ROPE < MLA_ROPE // 2, pltpu.roll(u, width - MLA_ROPE // 2, 1), pltpu.roll(u, MLA_ROPE // 2, 1))


def _rope_fwd_fn(u, cos, sin):
    return u * cos + _swap_halves(u) * sin


def _rope_bwd_fn(d, cos, sin):
    return d * cos + _swap_halves(d * sin)


def _spread4(v):
    return v + pltpu.roll(v, 32, 1) + pltpu.roll(v, 64, 1) + pltpu.roll(v, 96, 1)


def _att_masks(tq):
    lane = lax.broadcasted_iota(jnp.int32, (tq, LANE), 1)
    return lane // MLA_NOPE, lane // MLA_ROPE


def _att_tile(i, tq):
    klen = (i + 1) * tq
    qpos = i * tq + lax.broadcasted_iota(jnp.int32, (tq, klen), 0)
    kpos = lax.broadcasted_iota(jnp.int32, (tq, klen), 1)
    return slice(i * tq, (i + 1) * tq), klen, qpos >= kpos


def _att_qcat(qn_t, qr_t, par, e, half_id, grp_id):
    return jnp.concatenate([jnp.where(half_id == par, qn_t * ATT_SCALE, 0.0), jnp.where(grp_id == e, qr_t * ATT_SCALE, 0.0)],
                           axis=1).astype(BF16)


def _att_exp(qcat, kcat, causal):
    s = jnp.where(causal, _dot(qcat, kcat, _NT), -jnp.inf)
    e = jnp.exp(s - jnp.max(s, axis=1, keepdims=True))
    return e, 1.0 / jnp.sum(e, axis=1, keepdims=True)


def _att_specs(s_dim):
    col = lambda f: pl.BlockSpec((s_dim, LANE), lambda j: (0, f(j)))
    return [col(lambda j: j), col(lambda j: j // 2), col(lambda j: j), col(lambda j: 0), col(lambda j: 8 + j)]


def _att_fwd(q, qr, kv, krt, name="att_fwd"):
    s_dim = q.shape[0]
    tq = min(ATT_TQ, s_dim)

    def body(qn_ref, qr_ref, kn_ref, krt_ref, v_ref, o_ref, kcat_s, vb_s):
        e0 = 2 * (pl.program_id(0) % 2)
        half_id, grp_id = _att_masks(tq)
        kcat_s[...] = jnp.concatenate([kn_ref[...], krt_ref[...]], axis=1).astype(BF16)
        vb_s[...] = v_ref[...].astype(BF16)
        for i in range(s_dim // tq):
            rows, klen, causal = _att_tile(i, tq)
            qn_t, qr_t = qn_ref[rows, :], qr_ref[rows, :]
            outs = []
            for par in range(2):
                qcat = _att_qcat(qn_t, qr_t, par, e0 + par, half_id, grp_id)
                e, inv_l = _att_exp(qcat, kcat_s[0:klen, :], causal)
                outs.append(_dot(e.astype(BF16), vb_s[0:klen, :]) * inv_l)
            o_ref[rows, :] = jnp.where(half_id == 0, outs[0], outs[1])

    return pl.pallas_call(
        body, name=name, grid=(MLA_HEADS // 2,), in_specs=_att_specs(s_dim),
        out_specs=pl.BlockSpec((s_dim, LANE), lambda j: (0, j)), out_shape=jax.ShapeDtypeStruct((s_dim, 1024), F32),
        scratch_shapes=[pltpu.VMEM((s_dim, 2 * LANE), BF16), pltpu.VMEM((s_dim, LANE), BF16)],
        compiler_params=pltpu.CompilerParams(dimension_semantics=("parallel",)),
    )(q, qr, kv, krt, kv)


def _att_bwd(q, qr, kv, krt, o, do, name="att_bwd"):
    s_dim = q.shape[0]
    tq = min(ATT_TQ, s_dim)

    def body(qn_ref, qr_ref, kn_ref, krt_ref, v_ref, o_ref, do_ref, dqn_ref, dqr_ref, dkn_ref, dv_ref, dkrt_ref,
             kcat_s, vb_s):
        e0 = 2 * (pl.program_id(0) % 2)
        half_id, grp_id = _att_masks(tq)
        kcat_s[...] = jnp.concatenate([kn_ref[...], krt_ref[...]], axis=1).astype(BF16)
        vb_s[...] = v_ref[...].astype(BF16)
        dkn_ref[...] = jnp.zeros_like(dkn_ref)
        dv_ref[...] = jnp.zeros_like(dv_ref)
        dkrt_ref[...] = jnp.zeros_like(dkrt_ref)
        for i in range(s_dim // tq):
            rows, klen, causal = _att_tile(i, tq)
            qn_t, qr_t, o_t, do_t = qn_ref[rows, :], qr_ref[rows, :], o_ref[rows, :], do_ref[rows, :]
            dqn = jnp.zeros((tq, LANE), F32)
            dqr = jnp.zeros((tq, LANE), F32)
            for par in range(2):
                qcat = _att_qcat(qn_t, qr_t, par, e0 + par, half_id, grp_id)
                e, inv_l = _att_exp(qcat, kcat_s[0:klen, :], causal)
                p = e * inv_l
                dom = jnp.where(half_id == par, do_t, 0.0)
                domb = dom.astype(BF16)
                d_p = _dot(domb, vb_s[0:klen, :], _NT)
                d_row = jnp.sum(dom * o_t, axis=1, keepdims=True)
                d_s = (p * (d_p - d_row)).astype(BF16)
                dqcat = _dot(d_s, kcat_s[0:klen, :]) * ATT_SCALE
                dqn = dqn + jnp.where(half_id == par, dqcat[:, :LANE], 0.0)
                dqr = dqr + jnp.where(grp_id == e0 + par, dqcat[:, LANE:], 0.0)
                dkcat = _dot(d_s, qcat, _TN)
                dkn_ref[0:klen, :] += dkcat[:, :LANE]
                dkrt_ref[0:klen, :] += dkcat[:, LANE:]
                dv_ref[0:klen, :] += _dot(p.astype(BF16), domb, _TN)
            dqn_ref[rows, :] = dqn.astype(dqn_ref.dtype)
            dqr_ref[rows, :] = dqr

    col = lambda f: pl.BlockSpec((s_dim, LANE), lambda j: (0, f(j)))
    return pl.pallas_call(
        body, name=name, grid=(MLA_HEADS // 2,), in_specs=_att_specs(s_dim) + [col(lambda j: j), col(lambda j: j)],
        out_specs=[col(lambda j: j), pl.BlockSpec((None, s_dim, LANE), lambda j: (j % 2, 0, j // 2)), col(lambda j: j),
                   col(lambda j: j), pl.BlockSpec((None, s_dim, LANE), lambda j: (j, 0, 0))],
        out_shape=[jax.ShapeDtypeStruct((s_dim, 1024), BF16), jax.ShapeDtypeStruct((2, s_dim, 512), F32),
                   jax.ShapeDtypeStruct((s_dim, 1024), F32), jax.ShapeDtypeStruct((s_dim, 1024), F32),
                   jax.ShapeDtypeStruct((MLA_HEADS // 2, s_dim, LANE), F32)],
        scratch_shapes=[pltpu.VMEM((s_dim, 2 * LANE), BF16), pltpu.VMEM((s_dim, LANE), BF16)],
        compiler_params=pltpu.CompilerParams(dimension_semantics=("parallel",)),
    )(q, qr, kv, krt, kv, o, do)


def _all_gather(x, name):
    rows, width = x.shape

    def body(x_ref, out_ref, send_sems, recv_sems, local_sem):
        x_i, y_i, c_i = lax.axis_index("x"), lax.axis_index("y"), lax.axis_index("c")
        me, sibling = (x_i, y_i, c_i), (x_i, y_i, 1 - c_i)
        chips = [(1 - x_i, y_i), (x_i, 1 - y_i), (1 - x_i, 1 - y_i)]

        def slot(px, py, pc):
            return out_ref.at[4 * px + 2 * py + pc]

        def copy(k, block, to, src=None):
            return pltpu.make_async_remote_copy(
                src_ref=slot(*block) if src is None else src, dst_ref=slot(*block), send_sem=send_sems.at[k],
                recv_sem=recv_sems.at[k], device_id=to, device_id_type=pl.DeviceIdType.MESH)

        mine = pltpu.make_async_copy(x_ref, slot(*me), local_sem)
        mine.start()
        first = [copy(0, me, sibling, src=x_ref)]
        first += [copy(1 + j, me, (*chip, c_i), src=x_ref) for j, chip in enumerate(chips)]
        for cp in first:
            cp.start()
        passed = [copy(4 + j, (*chip, c_i), sibling) for j, chip in enumerate(chips)]
        for j, chip in enumerate(chips):
            copy(1 + j, (*chip, c_i), me).wait_recv()
            passed[j].start()
        copy(0, sibling, me).wait_recv()
        for j, chip in enumerate(chips):
            copy(4 + j, (*chip, 1 - c_i), me).wait_recv()
        for cp in first + passed:
            cp.wait_send()
        mine.wait()

    return pl.pallas_call(
        body, name=name, out_shape=jax.ShapeDtypeStruct((N_DEV, rows, width), x.dtype),
        in_specs=[pl.BlockSpec(memory_space=pl.ANY)], out_specs=pl.BlockSpec(memory_space=pl.ANY),
        scratch_shapes=[pltpu.SemaphoreType.DMA((7,)), pltpu.SemaphoreType.DMA((7,)), pltpu.SemaphoreType.DMA],
    )(x)


def _gather_many(shards, name):
    n_arr = len(shards)

    def body(*refs):
        x_refs, out_refs = refs[:n_arr], refs[n_arr:2 * n_arr]
        send_sems, recv_sems, local_sems = refs[2 * n_arr:]
        x_i, y_i, c_i = lax.axis_index("x"), lax.axis_index("y"), lax.axis_index("c")
        me, sibling = (x_i, y_i, c_i), (x_i, y_i, 1 - c_i)
        chips = [(1 - x_i, y_i), (x_i, 1 - y_i), (1 - x_i, 1 - y_i)]

        def copy(a, k, block, to, src=None):
            slot = out_refs[a].at[4 * block[0] + 2 * block[1] + block[2]]
            return pltpu.make_async_remote_copy(
                src_ref=slot if src is None else src, dst_ref=slot, send_sem=send_sems.at[a, k],
                recv_sem=recv_sems.at[a, k], device_id=to, device_id_type=pl.DeviceIdType.MESH)

        mine, first, passed = [], [], []
        for a in range(n_arr):
            mine.append(pltpu.make_async_copy(x_refs[a], out_refs[a].at[4 * x_i + 2 * y_i + c_i], local_sems.at[a]))
            mine[a].start()
            first.append([copy(a, 0, me, sibling, src=x_refs[a])]
                         + [copy(a, 1 + j, me, (*chip, c_i), src=x_refs[a]) for j, chip in enumerate(chips)])
            for cp in first[a]:
                cp.start()
            passed.append([copy(a, 4 + j, (*chip, c_i), sibling) for j, chip in enumerate(chips)])
        for j, chip in enumerate(chips):
            for a in range(n_arr):
                copy(a, 1 + j, (*chip, c_i), me).wait_recv()
                passed[a][j].start()
        for a in range(n_arr):
            copy(a, 0, sibling, me).wait_recv()
            for j, chip in enumerate(chips):
                copy(a, 4 + j, (*chip, 1 - c_i), me).wait_recv()
        for a in range(n_arr):
            for cp in first[a] + passed[a]:
                cp.wait_send()
            mine[a].wait()

    any_spec = pl.BlockSpec(memory_space=pl.ANY)
    return pl.pallas_call(
        body, name=name, out_shape=[jax.ShapeDtypeStruct((N_DEV,) + x.shape, x.dtype) for x in shards],
        in_specs=[any_spec] * n_arr, out_specs=[any_spec] * n_arr,
        scratch_shapes=[pltpu.SemaphoreType.DMA((n_arr, 7)), pltpu.SemaphoreType.DMA((n_arr, 7)),
                        pltpu.SemaphoreType.DMA((n_arr,))],
    )(*shards)


def _pair_exchange(grads, name):
    n_arr = len(grads)

    def body(*refs):
        g_refs, out_refs = refs[:n_arr], refs[n_arr:2 * n_arr]
        send_sems, recv_sems = refs[2 * n_arr:]
        x_i, y_i, c_i = lax.axis_index("x"), lax.axis_index("y"), lax.axis_index("c")
        copies = []
        for a in range(n_arr):
            for chip in range(4):
                copies.append(pltpu.make_async_remote_copy(
                    src_ref=g_refs[a].at[2 * chip + (1 - c_i)], dst_ref=out_refs[a].at[chip], send_sem=send_sems.at[a, chip],
                    recv_sem=recv_sems.at[a, chip], device_id=(x_i, y_i, 1 - c_i), device_id_type=pl.DeviceIdType.MESH))
        for cp in copies:
            cp.start()
        for cp in copies:
            cp.wait_recv()
        for cp in copies:
            cp.wait_send()

    any_spec = pl.BlockSpec(memory_space=pl.ANY)
    return pl.pallas_call(
        body, name=name, out_shape=[jax.ShapeDtypeStruct((4,) + g.shape[1:], g.dtype) for g in grads],
        in_specs=[any_spec] * n_arr, out_specs=[any_spec] * n_arr,
        scratch_shapes=[pltpu.SemaphoreType.DMA((n_arr, 4)), pltpu.SemaphoreType.DMA((n_arr, 4))],
    )(*grads)


def _chip_exchange(sums, name):
    n_arr = len(sums)

    def body(*refs):
        s_refs, out_refs = refs[:n_arr], refs[n_arr:2 * n_arr]
        send_sems, recv_sems, local_sems = refs[2 * n_arr:]
        x_i, y_i, c_i = lax.axis_index("x"), lax.axis_index("y"), lax.axis_index("c")
        my_chip = 2 * x_i + y_i
        copies, local = [], []
        for a in range(n_arr):
            local.append(pltpu.make_async_copy(s_refs[a].at[my_chip], out_refs[a].at[my_chip], local_sems.at[a]))
            local[a].start()
            for k in range(1, 4):
                px, py = x_i ^ (k >> 1), y_i ^ (k & 1)
                copies.append(pltpu.make_async_remote_copy(
                    src_ref=s_refs[a].at[2 * px + py], dst_ref=out_refs[a].at[my_chip], send_sem=send_sems.at[a, k - 1],
                    recv_sem=recv_sems.at[a, k - 1], device_id=(px, py, c_i), device_id_type=pl.DeviceIdType.MESH))
        for cp in copies:
            cp.start()
        for cp in copies:
            cp.wait_recv()
        for cp in copies:
            cp.wait_send()
        for cp in local:
            cp.wait()

    any_spec = pl.BlockSpec(memory_space=pl.ANY)
    return pl.pallas_call(
        body, name=name, out_shape=[jax.ShapeDtypeStruct(s.shape, s.dtype) for s in sums],
        in_specs=[any_spec] * n_arr, out_specs=[any_spec] * n_arr,
        scratch_shapes=[pltpu.SemaphoreType.DMA((n_arr, 3)), pltpu.SemaphoreType.DMA((n_arr, 3)),
                        pltpu.SemaphoreType.DMA((n_arr,))],
    )(*sums)


_HBM = pl.BlockSpec(memory_space=pltpu.HBM)
_SEM = pl.BlockSpec(memory_space=pltpu.SEMAPHORE)


def _plan_copies(plan, src_refs, land_refs, send_sems, recv_sems):
    copies = []
    for s_ref, l_ref in zip(src_refs, land_refs):
        for src, dst, peer in plan(s_ref, l_ref):
            k = len(copies)
            copies.append(pltpu.make_async_remote_copy(
                src_ref=src, dst_ref=dst, send_sem=send_sems.at[k], recv_sem=recv_sems.at[k], device_id=peer,
                device_id_type=pl.DeviceIdType.MESH))
    return copies


def _split_start(srcs, lands, plan, n_copy, name, after=None):
    n = len(srcs)
    n_in = 2 * n + (after is not None)

    def body(*refs):
        for cp in _plan_copies(plan, refs[:n], refs[n:2 * n], refs[n_in], refs[n_in + 1]):
            cp.start()
        refs[-1][...] = jnp.zeros_like(refs[-1])

    sems = pltpu.SemaphoreType.DMA((n * n_copy,))
    res = pl.pallas_call(
        body, name=name,
        out_shape=(sems, sems, *[pltpu.HBM(a.shape, a.dtype) for a in list(srcs) + list(lands)],
                   jax.ShapeDtypeStruct((8, LANE), F32)),
        in_specs=[_HBM] * (2 * n) + [pl.BlockSpec(memory_space=pl.ANY)] * (after is not None),
        out_specs=(_SEM, _SEM, *[_HBM] * (2 * n), pl.BlockSpec(memory_space=pltpu.VMEM)),
        input_output_aliases={i: 2 + i for i in range(2 * n)},
        compiler_params=pltpu.CompilerParams(has_side_effects=pltpu.SideEffectType.DATAFLOW_SIDE_EFFECTING),
    )(*[pltpu.with_memory_space_constraint(a, pltpu.HBM) for a in list(srcs) + list(lands)],
      *([after] if after is not None else []))
    return res[0], res[1], list(res[2:2 + n]), list(res[2 + n:2 + 2 * n]), res[-1]


def _split_wait(send_sems, recv_sems, srcs, lands, after, plan, name):
    n = len(srcs)

    def body(*refs):
        copies = _plan_copies(plan, refs[:n], refs[n:2 * n], refs[2 * n], refs[2 * n + 1])
        for cp in copies:
            cp.wait_send()
        for cp in copies:
            cp.wait_recv()

    res = pl.pallas_call(
        body, name=name, out_shape=tuple(pltpu.HBM(a.shape, a.dtype) for a in list(srcs) + list(lands)),
        in_specs=[_HBM] * (2 * n) + [_SEM, _SEM, pl.BlockSpec(memory_space=pl.ANY)], out_specs=tuple([_HBM] * (2 * n)),
        input_output_aliases={i: i for i in range(2 * n)},
        compiler_params=pltpu.CompilerParams(has_side_effects=pltpu.SideEffectType.DATAFLOW_SIDE_EFFECTING),
    )(*srcs, *lands, send_sems, recv_sems, after)
    return list(res[:n]), list(res[n:])


def _plan_broadcast(src, land):
    x_i, y_i, c_i = lax.axis_index("x"), lax.axis_index("y"), lax.axis_index("c")
    me = 4 * x_i + 2 * y_i + c_i
    return [(src, land.at[me], (x_i ^ (k >> 2), y_i ^ ((k >> 1) & 1), c_i ^ (k & 1))) for k in range(1, N_DEV)]


def _plan_scatter(src, land):
    x_i, y_i, c_i = lax.axis_index("x"), lax.axis_index("y"), lax.axis_index("c")
    me = 4 * x_i + 2 * y_i + c_i
    plan = []
    for k in range(1, N_DEV):
        px, py, pc = x_i ^ (k >> 2), y_i ^ ((k >> 1) & 1), c_i ^ (k & 1)
        plan.append((src.at[4 * px + 2 * py + pc], land.at[me], (px, py, pc)))
    return plan


def _pair_sum(g, recv, core, name):
    _, rows, cols = g.shape
    tr = ROW_TILE if rows % ROW_TILE == 0 else rows

    def body(core_ref, g_ref, r_ref, o_ref):
        o_ref[...] = (g_ref[...].astype(F32) + r_ref[...].astype(F32)).astype(o_ref.dtype)

    grid_spec = pltpu.PrefetchScalarGridSpec(
        num_scalar_prefetch=1, grid=(4, rows // tr),
        in_specs=[pl.BlockSpec((None, tr, cols), lambda k, i, core_ref: (2 * k + core_ref[0], i, 0)),
                  pl.BlockSpec((None, tr, cols), lambda k, i, core_ref: (k, i, 0))],
        out_specs=pl.BlockSpec((None, tr, cols), lambda k, i, core_ref: (k, i, 0)))
    return pl.pallas_call(body, name=name, grid_spec=grid_spec, out_shape=jax.ShapeDtypeStruct((4, rows, cols), g.dtype))(
        core, g, recv)


def _adam_math(g, w, m, v):
    m_new = ADAM_B1 * m + (1.0 - ADAM_B1) * g
    v_new = ADAM_B2 * v + (1.0 - ADAM_B2) * (g * g)
    m_hat = m_new / (1.0 - ADAM_B1 ** ADAM_STEP)
    v_hat = v_new / (1.0 - ADAM_B2 ** ADAM_STEP)
    return -ADAM_LR * (m_hat / (jnp.sqrt(v_hat) + ADAM_EPS) + ADAM_WD * w), m_new, v_new


def _adam(slots, w, m, v, name, own=None, own_idx=None):
    n_slot, rows, cols = slots.shape
    tr = ROW_TILE if rows % ROW_TILE == 0 else rows
    has_own = own is not None

    def body(*refs):
        if has_own:
            idx_ref, own_ref, refs = refs[0], refs[1], refs[2:]
        s_ref, w_ref, m_ref, v_ref, g_ref, d_ref, mo_ref, vo_ref = refs
        g = own_ref[...].astype(F32) if has_own else s_ref[0].astype(F32)
        for k in range(0 if has_own else 1, n_slot):
            part = s_ref[k].astype(F32)
            g = g + (jnp.where(idx_ref[0] == k, 0.0, part) if has_own else part)
        g_ref[...] = g
        d_ref[...], mo_ref[...], vo_ref[...] = _adam_math(g, w_ref[...], m_ref[...], v_ref[...])

    spec = pl.BlockSpec((tr, cols), lambda i, *_: (i, 0))
    in_specs = [pl.BlockSpec((n_slot, tr, cols), lambda i, *_: (0, i, 0)), spec, spec, spec]
    if has_own:
        in_specs = [pl.BlockSpec((None, tr, cols), lambda i, idx: (idx[0], i, 0))] + in_specs
    grid_spec = pltpu.PrefetchScalarGridSpec(num_scalar_prefetch=1 if has_own else 0, grid=(rows // tr,), in_specs=in_specs,
                                             out_specs=[spec] * 4)
    ins = ([own_idx, own] if has_own else []) + [slots, w, m, v]
    return pl.pallas_call(
        body, name=name, grid_spec=grid_spec, out_shape=[jax.ShapeDtypeStruct((rows, cols), F32)] * 4,
        compiler_params=pltpu.CompilerParams(dimension_semantics=("parallel",)),
    )(*ins)


PACK_ROWS, PACK_W = 24, 1536
REPL_W = (("ssd_conv_b", 1536), ("ssd_dt_bias", 16), ("ssd_A_log", 16), ("ssd_D", 16), ("ssd_norm_w", 1024),
          ("mla_q_norm_w", 384), ("mla_kv_norm_w", 256), ("mla_out_norm_w", 1024), ("ln_mix_g", 1024),
          ("ln_mix_b", 1024), ("ln_ffn_g", 1024), ("ln_ffn_b", 1024))
LOSS_ROW = 4 + len(REPL_W)


def _pack_small(conv_w_grad, grads, loss, name="pack_small"):
    def body(*refs):
        cw_ref, g_refs, loss_ref, o_ref = refs[0], refs[1:1 + len(REPL_W)], refs[1 + len(REPL_W)], refs[-1]
        o_ref[...] = jnp.zeros_like(o_ref)
        o_ref[0:4, :] = cw_ref[...]
        for i, g_ref in enumerate(g_refs):
            o_ref[4 + i:5 + i, 0:g_ref.shape[1]] = g_ref[...]
        o_ref[LOSS_ROW:LOSS_ROW + 1, 0:LANE] = loss_ref[...]

    return pl.pallas_call(body, name=name, out_shape=jax.ShapeDtypeStruct((PACK_ROWS, PACK_W), F32))(conv_w_grad, *grads, loss)


def _adam_small(gathered, wmv, name="adam_small"):
    def body(*refs):
        s_ref = refs[0]
        in_refs = refs[1:1 + 3 * len(REPL_W)]
        cw_ref, loss_ref = refs[1 + 3 * len(REPL_W)], refs[2 + 3 * len(REPL_W)]
        out_refs = refs[3 + 3 * len(REPL_W):-1]
        tot = refs[-1]
        acc = s_ref[0]
        for k in range(1, N_DEV):
            acc = acc + s_ref[k]
        tot[...] = acc
        cw_ref[...] = tot[0:4, :]
        loss_ref[...] = tot[LOSS_ROW:LOSS_ROW + 1, 0:LANE]
        for i, (_, width) in enumerate(REPL_W):
            g = tot[4 + i:5 + i, 0:width]
            w_ref, m_ref, v_ref = in_refs[3 * i:3 * i + 3]
            g_ref, d_ref, mo_ref, vo_ref = out_refs[4 * i:4 * i + 4]
            g_ref[...] = g
            d_ref[...], mo_ref[...], vo_ref[...] = _adam_math(g, w_ref[...], m_ref[...], v_ref[...])

    flat_in = [a for triple in wmv for a in triple]
    out_shape = [jax.ShapeDtypeStruct((4, PACK_W), F32), jax.ShapeDtypeStruct((1, LANE), F32)]
    for _, width in REPL_W:
        out_shape += [jax.ShapeDtypeStruct((1, width), F32)] * 4
    res = pl.pallas_call(body, name=name, out_shape=out_shape, scratch_shapes=[pltpu.VMEM((PACK_ROWS, PACK_W), F32)])(
        gathered, *flat_in)
    return res[0], res[1], [res[2 + 4 * i:6 + 4 * i] for i in range(len(REPL_W))]


def _cols_full(g):
    return jnp.transpose(g, (1, 0, 2)).reshape(g.shape[1], -1)


def _cols_split(full):
    k_dim, n_dim = full.shape
    return jnp.transpose(full.reshape(k_dim, N_DEV, n_dim // N_DEV), (1, 0, 2))


PROJ_BLOCK = {"z": (1024, 0), "dt": (LANE, 8), "q_c": (MLA_Q_RANK, 3), "xbc": (SSD_XBC, 1), "kv_c": (MLA_KV_RANK, 12),
              "k_rope": (LANE, 26)}


def _win_pad(wt):
    z = lambda n: jnp.zeros((n, wt.shape[1]), wt.dtype)
    return jnp.concatenate([wt[:1024], wt[2560:2576], z(112), wt[2576:2960], wt[1024:2560], wt[2960:3216], wt[3216:3248],
                            z(96)], axis=0)


def _win_unpad(wt):
    return jnp.concatenate([wt[:1024], wt[1536:3072], wt[1024:1040], wt[1152:1536], wt[3072:3328], wt[3328:3360]], axis=0)


def _heads_split_t(wt, a, b):
    w3 = wt.reshape(MLA_HEADS, a + b, wt.shape[1])
    return jnp.concatenate([w3[:, :a].reshape(-1, wt.shape[1]), w3[:, a:].reshape(-1, wt.shape[1])], axis=0)


def _heads_merge_t(wt, a, b):
    wa = wt[:MLA_HEADS * a].reshape(MLA_HEADS, a, wt.shape[1])
    wb = wt[MLA_HEADS * a:].reshape(MLA_HEADS, b, wt.shape[1])
    return jnp.concatenate([wa, wb], axis=1).reshape(-1, wt.shape[1])


def _heads_split(w, a, b):
    k_dim = w.shape[0]
    w3 = w.reshape(k_dim, MLA_HEADS, a + b)
    return jnp.concatenate([w3[:, :, :a].reshape(k_dim, -1), w3[:, :, a:].reshape(k_dim, -1)], axis=1)


def _heads_merge(w, a, b):
    k_dim = w.shape[0]
    wa = w[:, :MLA_HEADS * a].reshape(k_dim, MLA_HEADS, a)
    wb = w[:, MLA_HEADS * a:].reshape(k_dim, MLA_HEADS, b)
    return jnp.concatenate([wa, wb], axis=2).reshape(k_dim, -1)


def _pad_lanes(v, width=LANE):
    return jnp.concatenate([v, jnp.zeros((v.shape[0], width - v.shape[1]), v.dtype)], axis=1)


def _local_step(x, p, positions, tgt, W, P, comm=None):
    comm = comm or {}
    zero_tok = jnp.zeros((8, LANE), F32)
    s_dim = x.shape[0]
    inv_freq = 1.0 / (ROPE_BASE ** (jnp.arange(0, MLA_ROPE, 2, dtype=F32) / MLA_ROPE))
    ang = positions.astype(F32)[:, None] * inv_freq
    cos, sin = jnp.cos(ang), jnp.sin(ang)
    cos32 = jnp.concatenate([cos, cos], axis=1)
    sin32 = jnp.concatenate([-sin, sin], axis=1)
    cos512, sin512 = jnp.tile(cos32, (1, 16)), jnp.tile(sin32, (1, 16))
    cos128, sin128 = jnp.tile(cos32, (1, 4)), jnp.tile(sin32, (1, 4))
    bias_p, alog_p = _pad_lanes(P["ssd_dt_bias"]), _pad_lanes(P["ssd_A_log"])
    d_x = jnp.repeat(P["ssd_D"], SSD_HEAD_DIM, axis=1)

    xb, pb = x.astype(BF16), p.astype(BF16)
    if "token0" in comm:
        xb = (x + comm["token0"][0, 0]).astype(BF16)
    proj = _mm(xb, W["w_in"], tb=True, name="mm_in")
    z, qc, kvc, kr = [(proj,) + PROJ_BLOCK[n] for n in ("z", "q_c", "kv_c", "k_rope")]
    xbca = _conv_fwd(proj, PROJ_BLOCK["xbc"][1], P["ssd_conv_w"], P["ssd_conv_b"])
    y, states = _ssd_fwd(xbca, proj, PROJ_BLOCK["dt"][1], bias_p, alog_p, d_x)
    (yssd,) = _rowwise(_gate_rms, [y, z], [P["ssd_norm_w"]], [(1024, BF16)], name="ssd_gate_norm")
    (qn,) = _rowwise(_rms, [qc], [P["mla_q_norm_w"]], [(MLA_Q_RANK, BF16)], name="q_norm")
    (kvn,) = _rowwise(_rms, [kvc], [P["mla_kv_norm_w"]], [(MLA_KV_RANK, BF16)], name="kv_norm")
    q = _mm(qn, W["mla_w_q_b"], tb=True, name="mm_q")
    kv = _mm(kvn, W["mla_w_kv_b"], name="mm_kv")
    (qr,) = _rowwise(_rope_fwd_fn, [(q, 512, 2), cos512, sin512], [], [512], name="rope_q")
    (krt,) = _rowwise(lambda u, c, s: _spread4(_rope_fwd_fn(u, c, s)), [kr, cos128, sin128], [], [LANE], name="rope_k")
    att = _att_fwd(q, qr, kv, krt)
    (ymla,) = _rowwise(_rms, [att], [P["mla_out_norm_w"]], [(1024, BF16)], name="out_norm")
    ycat = jnp.concatenate([yssd, ymla], axis=1)
    if "late_weights" in comm:
        W = {**W, **comm["late_weights"](ycat)}
    mix = _mm(ycat, W["w_out"], name="mm_out")
    f_h1 = lambda xv, mv, g, b: _ln(ALPHA * xv + mv, g, b)
    h1, h1b = _rowwise(lambda *a: (f_h1(*a),) * 2, [x, mix], [P["ln_mix_g"], P["ln_mix_b"]], [1024, (1024, BF16)],
                       name="ln_mix")
    fb = D_FF // N_DEV
    hg = _mm(h1b, W["w_ffn_gate"], tb=True, b_blk="n", o_blk="n", out_dtype=BF16, name="mm_gate")
    hu = _mm(h1b, W["w_ffn_up"], tb=True, b_blk="n", o_blk="n", out_dtype=BF16, name="mm_up")
    pg = _mm(h1b, W["w_ple_gate"], name="mm_ple_gate")
    pp = _mm(pb, W["w_ple_proj"], name="mm_ple")
    hg2, hu2 = hg.reshape(N_DEV * s_dim, fb), hu.reshape(N_DEV * s_dim, fb)
    (act,) = _rowwise(lambda g, u: _silu(g.astype(F32)) * u.astype(F32), [hg2, hu2], [], [(fb, BF16)], name="swiglu",
                      tr=512)
    act3 = act.reshape(N_DEV, s_dim, fb)
    ffn = _mm(act3, W["w_ffn_down"], a_blk="k", b_blk="k", name="mm_down")

    f_h2 = lambda hv, fv, pg, ppv, g, b: _ln(ALPHA * hv + fv + _sigmoid(pg) * ppv, g, b)

    def final_fn(hv, fv, pg, ppv, tv, g, b):
        h2, pull = jax.vjp(f_h2, hv, fv, pg, ppv, g, b)
        diff = h2 - tv
        loss = 0.5 * jnp.sum(jnp.mean(diff * diff, axis=-1, keepdims=True), axis=0, keepdims=True)
        d_h, d_f, d_pg, d_pp, d_g, d_b = pull(diff * (1.0 / D_MODEL))
        return d_h, d_f, d_pg, d_pp, d_g, d_b, jnp.broadcast_to(loss, (1, LANE))

    dh1_a, dffn, dpg, dpp, g_ffn_g, g_ffn_b, loss = _rowwise(
        final_fn, [h1, ffn, pg, pp, tgt], [P["ln_ffn_g"], P["ln_ffn_b"]], [1024] + [(1024, BF16)] * 3,
        [1024, 1024, LANE], name="final")

    G = {}
    dact = _mm(dffn, W["w_ffn_down"], tb=True, b_blk="n", o_blk="n", name="mm_down_dx")
    G["w_ffn_down"] = _mm(act3, dffn, ta=True, a_blk="m", o_blk="m", out_dtype=GRAD_DT, name="mm_down_dw")

    def swiglu_bwd(g, u, d):
        g, u = g.astype(F32), u.astype(F32)
        sg = _sigmoid(g)
        return d * u * (sg * (1.0 + g * (1.0 - sg))), d * (g * sg)

    dg, du = _rowwise(swiglu_bwd, [hg2, hu2, dact.reshape(N_DEV * s_dim, fb)], [], [(fb, BF16)] * 2, name="swiglu_bwd",
                      tr=512)
    dg3, du3 = dg.reshape(N_DEV, s_dim, fb), du.reshape(N_DEV, s_dim, fb)
    dh1 = _mm(dg3, W["w_ffn_gate"], a_blk="k", b_blk="k", add=dh1_a, name="mm_gate_dx")
    dh1 = _mm(du3, W["w_ffn_up"], a_blk="k", b_blk="k", add=dh1, name="mm_up_dx")
    dh1 = _mm(dpg, W["w_ple_gate"], tb=True, add=dh1, name="mm_ple_gate_dx")
    G["w_ffn_gate"] = _mm(dg3, h1b, ta=True, a_blk="m", o_blk="m", out_dtype=GRAD_DT, name="mm_gate_dw")
    G["w_ffn_up"] = _mm(du3, h1b, ta=True, a_blk="m", o_blk="m", out_dtype=GRAD_DT, name="mm_up_dw")
    G["w_ple_gate"] = _mm(h1b, dpg, ta=True, out_dtype=GRAD_DT, name="mm_ple_gate_dw")
    G["w_ple_proj"] = _mm(pb, dpp, ta=True, out_dtype=GRAD_DT, name="mm_ple_dw")
    dx_a, dmix, g_mix_g, g_mix_b = _rowwise(
        lambda xv, mv, dv, g, b: _vjp_rows(f_h1)(xv, mv, g, b, dv), [x, mix, dh1], [P["ln_mix_g"], P["ln_mix_b"]],
        [1024, (1024, BF16)], [1024, 1024], name="ln_mix_bwd")
    dycat = _mm(dmix, W["w_out"], tb=True, name="mm_out_dx")
    G["w_out"] = _mm(ycat, dmix, ta=True, out_dtype=GRAD_DT, name="mm_out_dw")

    tok1 = comm["ffn_grads"](G) if "ffn_grads" in comm else zero_tok
    datt, g_out_norm = _rowwise(lambda a, dv, w, t: _vjp_rows(_rms)(a, w, dv + jnp.min(t)), [att, (dycat, 1024, 1)],
                                [P["mla_out_norm_w"], tok1], [1024], [1024], name="out_norm_bwd")
    dqn_nope, dqr, dkn, dv, dkrt = _att_bwd(q, qr, kv, krt, att, datt)
    dkv = jnp.concatenate([dkn, dv], axis=1)
    (dq_rope,) = _rowwise(lambda d0, d1, c, s: _rope_bwd_fn(d0 + d1, c, s), [(dqr, 512, 0), (dqr, 512, 1), cos512, sin512],
                          [], [(512, BF16)], name="rope_q_bwd")

    def rope_k_bwd(*a):
        d = _spread4(functools.reduce(lambda u, w: u + w, a[:-2]))
        lane = lax.broadcasted_iota(jnp.int32, d.shape, 1)
        return _rope_bwd_fn(jnp.where(lane < MLA_ROPE, d, 0.0), a[-2], a[-1])

    (dkr,) = _rowwise(rope_k_bwd, [(dkrt, LANE, k) for k in range(MLA_HEADS // 2)] + [cos128, sin128], [], [(LANE, BF16)],
                      name="rope_k_bwd")
    dq = jnp.concatenate([dqn_nope, dq_rope], axis=1)
    dqn = _mm(dq, W["mla_w_q_b"], name="mm_q_dx")
    G["mla_w_q_b"] = _mm(dq, qn, ta=True, out_dtype=GRAD_DT, name="mm_q_dw")
    dkvn = _mm(dkv, W["mla_w_kv_b"], tb=True, name="mm_kv_dx")
    G["mla_w_kv_b"] = _mm(kvn, dkv, ta=True, out_dtype=GRAD_DT, name="mm_kv_dw")
    dqc, g_q_norm = _rowwise(lambda a, dv, w: _vjp_rows(_rms)(a, w, dv), [qc, dqn], [P["mla_q_norm_w"]],
                             [(MLA_Q_RANK, BF16)], [MLA_Q_RANK], name="q_norm_bwd")
    dkvc, g_kv_norm = _rowwise(lambda a, dv, w: _vjp_rows(_rms)(a, w, dv), [kvc, dkvn], [P["mla_kv_norm_w"]],
                               [(MLA_KV_RANK, BF16)], [MLA_KV_RANK], name="kv_norm_bwd")

    dy, dz, g_ssd_norm = _rowwise(lambda yv, zv, dv, w, t: _vjp_rows(_gate_rms)(yv, zv, w, dv + jnp.min(t)),
                                  [y, z, (dycat, 1024, 0)], [P["ssd_norm_w"], tok1], [1024, (1024, BF16)], [1024],
                                  name="ssd_gate_norm_bwd")
    dxbca, ddtr, g_dt_bias, g_alog, g_d = _ssd_bwd(xbca, proj, PROJ_BLOCK["dt"][1], bias_p, alog_p, d_x, states, dy)
    da, g_conv_w, g_conv_b = _conv_bwd_pre(proj, PROJ_BLOCK["xbc"][1], P["ssd_conv_w"], P["ssd_conv_b"], dxbca)
    dxbc = _conv_bwd_in(da, P["ssd_conv_w"])

    dproj = jnp.concatenate([dz, ddtr, dqc, dxbc, dkvc, dkr], axis=1)
    grad_x = _mm(dproj, W["w_in"], add=dx_a, name="mm_in_dx")
    G["w_in"] = _mm(dproj, xb, ta=True, out_dtype=GRAD_DT, name="mm_in_dw")

    small = {
        "ssd_conv_b": g_conv_b, "ssd_dt_bias": g_dt_bias, "ssd_A_log": g_alog, "ssd_D": g_d, "ssd_norm_w": g_ssd_norm,
        "mla_q_norm_w": g_q_norm, "mla_kv_norm_w": g_kv_norm, "mla_out_norm_w": g_out_norm, "ln_mix_g": g_mix_g,
        "ln_mix_b": g_mix_b, "ln_ffn_g": g_ffn_g, "ln_ffn_b": g_ffn_b,
    }
    return grad_x, G, _pack_small(g_conv_w, [small[n] for n, _ in REPL_W], loss)


def kernel(x, p, positions, w_in, ssd_conv_w, ssd_conv_b, ssd_dt_bias, ssd_A_log, ssd_D, ssd_norm_w, mla_q_norm_w, mla_w_q_b, mla_kv_norm_w, mla_w_kv_b, mla_out_norm_w, w_out, ln_mix_g, ln_mix_b, w_ffn_gate, w_ffn_up, w_ffn_down, w_ple_gate, w_ple_proj, ln_ffn_g, ln_ffn_b, loss_target, m_w_in, m_ssd_conv_w, m_ssd_conv_b, m_ssd_dt_bias, m_ssd_A_log, m_ssd_D, m_ssd_norm_w, m_mla_q_norm_w, m_mla_w_q_b, m_mla_kv_norm_w, m_mla_w_kv_b, m_mla_out_norm_w, m_w_out, m_ln_mix_g, m_ln_mix_b, m_w_ffn_gate, m_w_ffn_up, m_w_ffn_down, m_w_ple_gate, m_w_ple_proj, m_ln_ffn_g, m_ln_ffn_b, v_w_in, v_ssd_conv_w, v_ssd_conv_b, v_ssd_dt_bias, v_ssd_A_log, v_ssd_D, v_ssd_norm_w, v_mla_q_norm_w, v_mla_w_q_b, v_mla_kv_norm_w, v_mla_w_kv_b, v_mla_out_norm_w, v_w_out, v_ln_mix_g, v_ln_mix_b, v_w_ffn_gate, v_w_ffn_up, v_w_ffn_down, v_w_ple_gate, v_w_ple_proj, v_ln_ffn_g, v_ln_ffn_b):
    args = dict(locals())
    core = lax.axis_index("c")
    me = 4 * lax.axis_index("x") + 2 * lax.axis_index("y") + core

    conv_sh = ssd_conv_w[0]
    conv_hi = conv_sh.astype(BF16)
    conv_lo = (conv_sh - conv_hi.astype(F32)).astype(BF16)
    stored = lambda n, pre="": jnp.transpose(args[pre + n][0]) if n in TRANSPOSED else args[pre + n][0]
    shards = {n: stored(n).astype(BF16) for n in BIG}
    rows_full = lambda g: g.reshape(-1, g.shape[2])
    core_arr = core.astype(jnp.int32).reshape(1)

    early = _gather_many([shards[n] for n in EARLY] + [jnp.concatenate([conv_hi, conv_lo], axis=0)], "gather_early")
    gw = dict(zip(EARLY, early[:-1]))
    conv_g = early[-1].astype(F32)
    W = {
        "w_in": _win_pad(rows_full(gw["w_in"])),
        "mla_w_q_b": _heads_split_t(rows_full(gw["mla_w_q_b"]), MLA_NOPE, MLA_ROPE),
        "mla_w_kv_b": _heads_split(_cols_full(gw["mla_w_kv_b"]), MLA_NOPE, MLA_V),
    }
    P = {n: args[n] for n, _ in REPL_W}
    P["ssd_conv_w"] = _cols_full(conv_g[:, :4] + conv_g[:, 4:])

    lands = [lax.dynamic_update_slice(lax.empty((N_DEV,) + shards[n].shape, BF16), shards[n][None], (me, 0, 0)) for n in LATE]
    late_sems = _split_start([shards[n] for n in LATE], lands, _plan_broadcast, N_DEV - 1, "gather_late_start",
                             after=early[0])

    def late_weights(after):
        _, got = _split_wait(*late_sems[:4], after, _plan_broadcast, "gather_late_wait")
        lw = dict(zip(LATE, got))
        return {"w_out": rows_full(lw["w_out"]), "w_ple_gate": rows_full(lw["w_ple_gate"]),
                "w_ple_proj": _cols_full(lw["w_ple_proj"]), "w_ffn_gate": lw["w_ffn_gate"], "w_ffn_up": lw["w_ffn_up"],
                "w_ffn_down": lw["w_ffn_down"]}

    def to_blocks(n, g):
        if n in OWNER_BLOCKED:
            return g
        if n == "w_in":
            g = _win_unpad(g)
        elif n == "mla_w_q_b":
            g = _heads_merge_t(g, MLA_NOPE, MLA_ROPE)
        elif n == "mla_w_kv_b":
            g = _heads_merge(g, MLA_NOPE, MLA_V)
        if n in ROW_SHARDED or n in TRANSPOSED:
            return g.reshape(N_DEV, -1, g.shape[1])
        return _cols_split(g)

    flight = {}

    def ffn_grads(G):
        gl = [to_blocks(n, G[n]) for n in LATE_GRADS]
        flight["grads"] = _split_start(gl, [lax.empty(g.shape, g.dtype) for g in gl], _plan_scatter, N_DEV - 1, "grads_start")
        return flight["grads"][4]

    grad_x, G, packed = _local_step(x[0], p[0, 0], positions[0], loss_target[0], W, P,
                                    comm={"token0": late_sems[4], "late_weights": late_weights, "ffn_grads": ffn_grads})

    wmv = lambda n: (stored(n), stored(n, "m_"), stored(n, "v_"))
    mine, recv = _split_wait(*flight["grads"][:4], grad_x, _plan_scatter, "grads_wait")
    me_arr = me.astype(jnp.int32).reshape(1)
    big_out = {n: _adam(r, *wmv(n), "adam_" + n, own=g, own_idx=me_arr) for n, g, r in zip(LATE_GRADS, mine, recv)}

    glist = [to_blocks(n, G[n]) for n in LAST_GRADS]
    from_sibling = _pair_exchange(glist, "exchange_pairs")
    sums = [_pair_sum(g, r, core_arr, "pair_sum_" + n) for n, g, r in zip(LAST_GRADS, glist, from_sibling)]
    recv = _chip_exchange(sums, "exchange_chips")
    big_out.update({n: _adam(r, *wmv(n), "adam_" + n) for n, r in zip(LAST_GRADS, recv)})

    small_all = _all_gather(packed, "gather_small")
    conv_sum, loss_row, small_out = _adam_small(small_all, [(args[n], args["m_" + n], args["v_" + n]) for n, _ in REPL_W])
    conv_grad = lax.dynamic_slice_in_dim(conv_sum, me * 192, 192, axis=1)
    conv_out = _adam(conv_grad[None], conv_sh, m_ssd_conv_w[0], v_ssd_conv_w[0], "adam_conv")
    small_map = {n: small_out[i] for i, (n, _) in enumerate(REPL_W)}

    def outputs(idx):
        res = []
        for n in WEIGHT_ORDER:
            if n == "ssd_conv_w":
                res.append(conv_out[idx][None])
            elif n in big_out:
                res.append((jnp.transpose(big_out[n][idx]) if n in TRANSPOSED else big_out[n][idx])[None])
            else:
                res.append(small_map[n][idx])
        return res

    return (loss_row[0, 0], grad_x[None], *outputs(0), *outputs(1), *outputs(2), *outputs(3))
```

```python
import functools
import math

import numpy as np
import jax
import jax.numpy as jnp
from jax import lax
from jax.experimental import pallas as pl
from jax.experimental.pallas import tpu as pltpu

F32 = jnp.float32
BF16 = jnp.bfloat16
HI = lax.Precision.HIGHEST

N_DEV = 8
D_MODEL = 1024
PLE_DIM = 256
SSD_HEADS = 16
SSD_HEAD_DIM = 64
SSD_INNER = 1024
SSD_STATE = 128
SSD_XBC = 1536
SSD_CHUNK = 128
MLA_HEADS = 16
MLA_Q_RANK = 384
MLA_KV_RANK = 256
MLA_NOPE = 64
MLA_ROPE = 32
MLA_V = 64
ROPE_BASE = 10000.0
D_FF = 2816
IN_WIDTH = 3248
IN_PAD = 3456
ALPHA = 2.0 ** 0.25
EPS = 1e-6
LN_EPS = 1e-5
ATT_SCALE = 1.0 / math.sqrt(MLA_NOPE + MLA_ROPE)
ADAM_LR, ADAM_B1, ADAM_B2, ADAM_EPS, ADAM_WD, ADAM_STEP = 0.001, 0.9, 0.999, 1e-08, 0.01, 10

LANE = 128
MXU_DIM = 256
MM_TM, MM_TN, MM_TK = 1408, 1408, 2048
ROW_TILE = 256
ATT_TQ = 256

GRAD_DT = BF16

BIG = ("w_in", "mla_w_q_b", "mla_w_kv_b", "w_out", "w_ffn_gate", "w_ffn_up", "w_ffn_down", "w_ple_gate", "w_ple_proj")
EARLY = ("w_in", "mla_w_q_b", "mla_w_kv_b")
LATE = ("w_out", "w_ffn_gate", "w_ffn_up", "w_ffn_down", "w_ple_gate", "w_ple_proj")
LATE_GRADS = ("w_ffn_gate", "w_ffn_up", "w_ffn_down", "w_ple_gate", "w_ple_proj", "w_out")
LAST_GRADS = ("w_in", "mla_w_q_b", "mla_w_kv_b")
ROW_SHARDED = ("w_out", "w_ffn_down", "w_ple_gate")
TRANSPOSED = ("w_in", "mla_w_q_b", "w_ffn_gate", "w_ffn_up")
WEIGHT_ORDER = ("w_in", "ssd_conv_w", "ssd_conv_b", "ssd_dt_bias", "ssd_A_log", "ssd_D", "ssd_norm_w", "mla_q_norm_w",
                "mla_w_q_b", "mla_kv_norm_w", "mla_w_kv_b", "mla_out_norm_w", "w_out", "ln_mix_g", "ln_mix_b",
                "w_ffn_gate", "w_ffn_up", "w_ffn_down", "w_ple_gate", "w_ple_proj", "ln_ffn_g", "ln_ffn_b")


def _tile(dim, cap, prefer=None):
    cands = [t for t in range(LANE, min(cap, dim) + 1, LANE) if dim % t == 0]
    if not cands:
        return dim
    if prefer is None:
        return max(cands)
    fill = lambda t: t / (MXU_DIM * -(-t // MXU_DIM))
    good = min(0.9, max(fill(t) for t in cands))
    return min((t for t in cands if fill(t) >= good), key=lambda t: abs(t - prefer))


def _dot(a, b, dims=(((1,), (0,)), ((), ())), precision=None):
    return lax.dot_general(a, b, dims, preferred_element_type=F32, precision=precision)


_NT = (((1,), (1,)), ((), ()))
_TN = (((0,), (0,)), ((), ()))


def _mm(a, b, *, ta=False, tb=False, add=None, out_dtype=F32, name):
    k_dim, m_dim = a.shape if ta else a.shape[::-1]
    n_dim, kb = b.shape if tb else b.shape[::-1]
    assert k_dim == kb
    tm, tn, tk = _tile(m_dim, MM_TM), _tile(n_dim, MM_TN, prefer=1024), _tile(k_dim, MM_TK, prefer=MM_TK)
    nk = k_dim // tk
    dims = (((0 if ta else 1,), (1 if tb else 0,)), ((), ()))
    has_add = add is not None
    a_spec = pl.BlockSpec((tk, tm), lambda i, j, k: (k, i)) if ta else pl.BlockSpec((tm, tk), lambda i, j, k: (i, k))
    b_spec = pl.BlockSpec((tn, tk), lambda i, j, k: (j, k)) if tb else pl.BlockSpec((tk, tn), lambda i, j, k: (k, j))
    o_spec = pl.BlockSpec((tm, tn), lambda i, j, k: (i, j))

    def body(*refs):
        if has_add:
            a_ref, b_ref, add_ref, o_ref = refs[:4]
        else:
            a_ref, b_ref, o_ref = refs[:3]
        part = _dot(a_ref[...].astype(BF16), b_ref[...].astype(BF16), dims)
        if nk == 1:
            o_ref[...] = ((part + add_ref[...]) if has_add else part).astype(o_ref.dtype)
            return
        acc = refs[-1]
        k = pl.program_id(2)

        @pl.when(k == 0)
        def _():
            acc[...] = (part + add_ref[...]) if has_add else part

        @pl.when(k > 0)
        def _():
            acc[...] += part

        @pl.when(k == nk - 1)
        def _():
            o_ref[...] = acc[...].astype(o_ref.dtype)

    ins = [a, b] + ([add] if has_add else [])
    specs = [a_spec, b_spec] + ([o_spec] if has_add else [])
    return pl.pallas_call(
        body, name=name, grid=(m_dim // tm, n_dim // tn, nk), in_specs=specs, out_specs=o_spec,
        out_shape=jax.ShapeDtypeStruct((m_dim, n_dim), out_dtype),
        scratch_shapes=[pltpu.VMEM((tm, tn), F32)] if nk > 1 else [],
        compiler_params=pltpu.CompilerParams(dimension_semantics=("parallel", "parallel", "arbitrary")),
    )(*ins)


def _rowwise(fn, rows, consts, out_widths, acc_widths=(), *, name, tr=ROW_TILE):
    row_arrays, row_specs = [], []
    first_arr = rows[0][0] if isinstance(rows[0], tuple) else rows[0]
    s_dim = first_arr.shape[-2]
    tr = min(tr, s_dim)
    for r in rows:
        arr, width, cb = r if isinstance(r, tuple) else (r, r.shape[-1], 0)
        row_arrays.append(arr)
        if arr.ndim == 3:
            row_specs.append(pl.BlockSpec((None, tr, width), functools.partial(lambda i, k: (k, i, 0), k=cb)))
        else:
            row_specs.append(pl.BlockSpec((tr, width), functools.partial(lambda i, cb: (i, cb), cb=cb)))
    const_specs = [pl.BlockSpec(c.shape, lambda i: (0, 0)) for c in consts]
    nr, nc, no, na = len(rows), len(consts), len(out_widths), len(acc_widths)

    def body(*refs):
        ins = [r[...] for r in refs[:nr + nc]]
        res = fn(*ins)
        if not isinstance(res, (tuple, list)):
            res = (res,)
        out_refs = refs[nr + nc:nr + nc + no]
        acc_refs = refs[nr + nc + no:]
        for o_ref, val in zip(out_refs, res[:no]):
            o_ref[...] = val.astype(o_ref.dtype)
        first = pl.program_id(0) == 0
        for a_ref, val in zip(acc_refs, res[no:]):
            @pl.when(first)
            def _(a_ref=a_ref, val=val):
                a_ref[...] = val

            @pl.when(jnp.logical_not(first))
            def _(a_ref=a_ref, val=val):
                a_ref[...] += val

    outs = [w if isinstance(w, tuple) else (w, F32) for w in out_widths]
    out_shape = [jax.ShapeDtypeStruct((s_dim, w), dt) for w, dt in outs]
    out_shape += [jax.ShapeDtypeStruct((1, w), F32) for w in acc_widths]
    out_specs = [pl.BlockSpec((tr, w), lambda i: (i, 0)) for w, _ in outs]
    out_specs += [pl.BlockSpec((1, w), lambda i: (0, 0)) for w in acc_widths]
    res = pl.pallas_call(
        body, name=name, grid=(s_dim // tr,), in_specs=row_specs + const_specs, out_specs=out_specs, out_shape=out_shape,
        compiler_params=pltpu.CompilerParams(dimension_semantics=("arbitrary",)),
    )(*row_arrays, *consts)
    return res


def _colsum(v):
    return jnp.sum(v, axis=0, keepdims=True)


def _rms(u, g):
    return u * lax.rsqrt(jnp.mean(u * u, axis=-1, keepdims=True) + EPS) * g


def _ln(u, g, b):
    mu = jnp.mean(u, axis=-1, keepdims=True)
    d = u - mu
    var = jnp.mean(d * d, axis=-1, keepdims=True)
    return d * lax.rsqrt(var + LN_EPS) * g + b


def _sigmoid(v):
    return 1.0 / (1.0 + jnp.exp(-v))


def _silu(v):
    return v * _sigmoid(v)


def _softplus(v):
    y = jnp.exp(-jnp.abs(v))
    w = 1.0 + y
    log1p = jnp.where(w == 1.0, y, jnp.log(w) * y / jnp.where(w == 1.0, 1.0, w - 1.0))
    return jnp.maximum(v, 0.0) + log1p


def _gate_rms(y, z, w):
    return _rms(y * _silu(z), w)


def _vjp_rows(f):
    def fn(*args):
        prim, ct = args[:-1], args[-1]
        _, pull = jax.vjp(f, *prim)
        return pull(ct)
    return fn


def _conv_pre(cur, prev, w, b, first):
    row = lax.broadcasted_iota(jnp.int32, cur.shape, 0)
    acc = cur * w[3:4, :] + b
    for j in (1, 2, 3):
        tail = jnp.where(first, 0.0, pltpu.roll(prev, j, 0))
        acc = acc + jnp.where(row >= j, pltpu.roll(cur, j, 0), tail) * w[3 - j:4 - j, :]
    return acc


def _conv_fwd(u, ucb, w, b, name="conv_fwd"):
    s_dim, width = u.shape[0], w.shape[1]
    tr = min(ROW_TILE, s_dim)

    def body(cur_ref, prev_ref, w_ref, b_ref, o_ref):
        pre = _conv_pre(cur_ref[...], prev_ref[...], w_ref, b_ref[...], pl.program_id(0) == 0)
        o_ref[...] = _silu(pre)

    return pl.pallas_call(
        body, name=name, grid=(s_dim // tr,),
        in_specs=[pl.BlockSpec((tr, width), lambda i: (i, ucb)),
                  pl.BlockSpec((tr, width), lambda i: (jnp.maximum(i - 1, 0), ucb)),
                  pl.BlockSpec(w.shape, lambda i: (0, 0)), pl.BlockSpec(b.shape, lambda i: (0, 0))],
        out_specs=pl.BlockSpec((tr, width), lambda i: (i, 0)), out_shape=jax.ShapeDtypeStruct((s_dim, width), F32),
        compiler_params=pltpu.CompilerParams(dimension_semantics=("arbitrary",)),
    )(u, u, w, b)


def _conv_bwd_pre(u, ucb, w, b, dact, name="conv_bwd_pre"):
    s_dim, width = u.shape[0], w.shape[1]
    tr = min(ROW_TILE, s_dim)

    def body(cur_ref, prev_ref, w_ref, b_ref, d_ref, da_ref, dw_ref, db_ref):
        first = pl.program_id(0) == 0
        cur, prev = cur_ref[...], prev_ref[...]
        pre = _conv_pre(cur, prev, w_ref, b_ref[...], first)
        sg = _sigmoid(pre)
        da = d_ref[...] * (sg * (1.0 + pre * (1.0 - sg)))
        da_ref[...] = da
        row = lax.broadcasted_iota(jnp.int32, cur.shape, 0)

        @pl.when(first)
        def _():
            dw_ref[...] = jnp.zeros_like(dw_ref)
            db_ref[...] = jnp.zeros_like(db_ref)

        db_ref[...] += _colsum(da)
        dw_ref[3:4, :] += _colsum(da * cur)
        for j in (1, 2, 3):
            tail = jnp.where(first, 0.0, pltpu.roll(prev, j, 0))
            sh = jnp.where(row >= j, pltpu.roll(cur, j, 0), tail)
            dw_ref[3 - j:4 - j, :] += _colsum(da * sh)

    return pl.pallas_call(
        body, name=name, grid=(s_dim // tr,),
        in_specs=[pl.BlockSpec((tr, width), lambda i: (i, ucb)),
                  pl.BlockSpec((tr, width), lambda i: (jnp.maximum(i - 1, 0), ucb)),
                  pl.BlockSpec(w.shape, lambda i: (0, 0)), pl.BlockSpec(b.shape, lambda i: (0, 0)),
                  pl.BlockSpec((tr, width), lambda i: (i, 0))],
        out_specs=[pl.BlockSpec((tr, width), lambda i: (i, 0)), pl.BlockSpec(w.shape, lambda i: (0, 0)),
                   pl.BlockSpec(b.shape, lambda i: (0, 0))],
        out_shape=[jax.ShapeDtypeStruct((s_dim, width), F32), jax.ShapeDtypeStruct(w.shape, F32),
                   jax.ShapeDtypeStruct(b.shape, F32)],
        compiler_params=pltpu.CompilerParams(dimension_semantics=("arbitrary",)),
    )(u, u, w, b, dact)


def _conv_bwd_in(da, w, name="conv_bwd_in"):
    s_dim, width = da.shape
    tr = min(ROW_TILE, s_dim)
    n = s_dim // tr

    def body(cur_ref, nxt_ref, w_ref, o_ref):
        last = pl.program_id(0) == n - 1
        cur, nxt = cur_ref[...], nxt_ref[...]
        row = lax.broadcasted_iota(jnp.int32, cur.shape, 0)
        acc = cur * w_ref[3:4, :]
        for j in (1, 2, 3):
            head = jnp.where(last, 0.0, pltpu.roll(nxt, tr - j, 0))
            acc = acc + jnp.where(row < tr - j, pltpu.roll(cur, tr - j, 0), head) * w_ref[3 - j:4 - j, :]
        o_ref[...] = acc.astype(o_ref.dtype)

    return pl.pallas_call(
        body, name=name, grid=(n,),
        in_specs=[pl.BlockSpec((tr, width), lambda i: (i, 0)), pl.BlockSpec((tr, width), lambda i: (jnp.minimum(i + 1, n - 1), 0)),
                  pl.BlockSpec(w.shape, lambda i: (0, 0))],
        out_specs=pl.BlockSpec((tr, width), lambda i: (i, 0)), out_shape=jax.ShapeDtypeStruct((s_dim, width), BF16),
        compiler_params=pltpu.CompilerParams(dimension_semantics=("arbitrary",)),
    )(da, da, w)


def _sel_dot(a, sel, pieces, dims=(((1,), (0,)), ((), ())), sel_left=False):
    sel = sel.astype(BF16)
    acc, rest = None, a
    for _ in range(pieces):
        piece = rest.astype(BF16)
        rest = rest - piece.astype(F32)
        part = _dot(sel, piece, dims) if sel_left else _dot(piece, sel, dims)
        acc = part if acc is None else acc + part
    return acc


def _ssd_consts():
    L = SSD_CHUNK
    tri = np.tril(np.ones((L, L), np.float32))
    expand = np.zeros((LANE, SSD_INNER), np.float32)
    expand128 = np.zeros((LANE, SSD_HEADS * LANE), np.float32)
    for h in range(SSD_HEADS):
        expand[h, h * SSD_HEAD_DIM:(h + 1) * SSD_HEAD_DIM] = 1.0
        expand128[h, h * LANE:(h + 1) * LANE] = 1.0
    return jnp.asarray(tri), jnp.asarray(expand), jnp.asarray(expand128), jnp.asarray(expand.T.copy())


def _ssd_prep(dt_ref, bias_ref, alog_ref, tri_ref, exp_ref, exp128_ref, cs_s, cst_s, ex_s, csx_s):
    L = SSD_CHUNK
    dt = _softplus(dt_ref[...] + bias_ref[...])
    a = -jnp.exp(alog_ref[...])
    cs = _sel_dot(dt * a, tri_ref[...], 3, sel_left=True)
    cs_s[...] = cs
    cst_s[...] = cs.T
    last = cs_s[L - 1:L, :]
    expand = exp_ref[...]
    ex_s[...] = _sel_dot(jnp.exp(cs), expand, 2)
    f_x = _sel_dot(jnp.exp(last - cs), expand, 2)
    dt_x = _sel_dot(dt, expand, 2)
    csx_s[...] = _sel_dot(cs, exp128_ref[...], 3)
    t_x = ex_s[L - 1:L, :]
    return dt, a, dt_x, f_x, t_x


def _decay_matrix(csx_s, cst_s, h, tril):
    seg = csx_s[:, h * LANE:(h + 1) * LANE] - cst_s[h:h + 1, :]
    return jnp.exp(jnp.where(tril, seg, -jnp.inf))


def _ssd_fwd(xbca, dtr, dtcb, bias, alog, d_x, name="ssd_fwd"):
    s_dim = xbca.shape[0]
    L = SSD_CHUNK
    nc = s_dim // L
    tri, expand, expand128, _ = _ssd_consts()

    def body(xs_ref, b_ref, c_ref, dt_ref, bias_ref, alog_ref, dx_ref, tri_ref, exp_ref, exp128_ref,
             y_ref, st_ref, st_s, cs_s, cst_s, ex_s, csx_s):
        @pl.when(pl.program_id(0) == 0)
        def _():
            st_s[...] = jnp.zeros_like(st_s)

        dt, a, dt_x, f_x, t_x = _ssd_prep(dt_ref, bias_ref, alog_ref, tri_ref, exp_ref, exp128_ref, cs_s, cst_s, ex_s, csx_s)
        st_ref[0] = st_s[...]
        row = lax.broadcasted_iota(jnp.int32, (L, L), 0)
        col = lax.broadcasted_iota(jnp.int32, (L, L), 1)
        tril = row >= col
        low = col < SSD_HEAD_DIM
        for g in range(2):
            bg = b_ref[:, g * LANE:(g + 1) * LANE]
            cg = c_ref[:, g * LANE:(g + 1) * LANE].astype(BF16)
            gmat = _dot(cg, bg.astype(BF16), _NT)
            bgt = bg.T.astype(BF16)
            for jj in range(4):
                j = 4 * g + jj
                sl = slice(j * LANE, (j + 1) * LANE)
                xp = xs_ref[:, sl]
                x_dt = xp * dt_x[:, sl]
                xb = x_dt.astype(BF16)
                yd = []
                for e in range(2):
                    lm = _decay_matrix(csx_s, cst_s, 2 * j + e, tril)
                    yd.append(_dot((gmat * lm).astype(BF16), xb))
                stp = st_s[j]
                z = _dot(cg, stp.astype(BF16))
                y_ref[:, sl] = jnp.where(low, yd[0], yd[1]) + ex_s[:, sl] * z + dx_ref[:, sl] * xp
                xf = (x_dt * f_x[:, sl]).astype(BF16)
                st_s[j] = t_x[:, sl] * stp + _dot(bgt, xf)

    const = lambda shape: pl.BlockSpec(shape, lambda c: tuple(0 for _ in shape))
    return pl.pallas_call(
        body, name=name, grid=(nc,),
        in_specs=[pl.BlockSpec((L, 1024), lambda c: (c, 0)), pl.BlockSpec((L, 256), lambda c: (c, 4)),
                  pl.BlockSpec((L, 256), lambda c: (c, 5)), pl.BlockSpec((L, LANE), lambda c: (c, dtcb)),
                  const((1, LANE)), const((1, LANE)), const((1, 1024)), const((L, L)), const((LANE, 1024)),
                  const((LANE, 2048))],
        out_specs=[pl.BlockSpec((L, 1024), lambda c: (c, 0)), pl.BlockSpec((1, 8, LANE, LANE), lambda c: (c, 0, 0, 0))],
        out_shape=[jax.ShapeDtypeStruct((s_dim, 1024), F32), jax.ShapeDtypeStruct((nc, 8, LANE, LANE), F32)],
        scratch_shapes=[pltpu.VMEM((8, LANE, LANE), F32), pltpu.VMEM((L, LANE), F32), pltpu.VMEM((LANE, L), F32),
                        pltpu.VMEM((L, 1024), F32), pltpu.VMEM((L, 2048), F32)],
        compiler_params=pltpu.CompilerParams(dimension_semantics=("arbitrary",)),
    )(xbca, xbca, xbca, dtr, bias, alog, d_x, tri, expand, expand128)


def _ssd_bwd(xbca, dtr, dtcb, bias, alog, d_x, states, dy, name="ssd_bwd"):
    s_dim = xbca.shape[0]
    L = SSD_CHUNK
    nc = s_dim // L
    tri, expand, expand128, expand_t = _ssd_consts()

    def body(xs_ref, b_ref, c_ref, dt_ref, bias_ref, alog_ref, dx_ref, tri_ref, exp_ref, exp128_ref, expt_ref,
             st_ref, dy_ref, dxbc_ref, ddt_ref, dbias_ref, dalog_ref, dd_ref,
             dst_s, cs_s, cst_s, ex_s, csx_s, dcsx_s, ddtx_s, dcol_s, drow_s, dlast_s, dd_s):
        @pl.when(pl.program_id(0) == 0)
        def _():
            dst_s[...] = jnp.zeros_like(dst_s)
            dbias_ref[...] = jnp.zeros_like(dbias_ref)
            dalog_ref[...] = jnp.zeros_like(dalog_ref)
            dd_s[...] = jnp.zeros_like(dd_s)

        dt, a, dt_x, f_x, t_x = _ssd_prep(dt_ref, bias_ref, alog_ref, tri_ref, exp_ref, exp128_ref, cs_s, cst_s, ex_s, csx_s)
        row = lax.broadcasted_iota(jnp.int32, (L, L), 0)
        col = lax.broadcasted_iota(jnp.int32, (L, L), 1)
        tril = row >= col
        low = col < SSD_HEAD_DIM
        dcol_s[...] = jnp.zeros_like(dcol_s)
        drow_s[...] = jnp.zeros_like(drow_s)
        for g in range(2):
            bg = b_ref[:, g * LANE:(g + 1) * LANE]
            cg = c_ref[:, g * LANE:(g + 1) * LANE]
            bgb, cgb = bg.astype(BF16), cg.astype(BF16)
            gmat = _dot(cgb, bgb, _NT)
            d_g = jnp.zeros((L, L), F32)
            d_b = jnp.zeros((L, LANE), F32)
            d_c = jnp.zeros((L, LANE), F32)
            for jj in range(4):
                j = 4 * g + jj
                sl = slice(j * LANE, (j + 1) * LANE)
                xp = xs_ref[:, sl]
                dtp = dt_x[:, sl]
                x_dt = xp * dtp
                xb = x_dt.astype(BF16)
                dyp = dy_ref[:, sl]
                dd_s[:, sl] += _colsum(dyp * xp)
                d_xdt = jnp.zeros((L, LANE), F32)
                for e in range(2):
                    h = 2 * j + e
                    lm = _decay_matrix(csx_s, cst_s, h, tril)
                    m = gmat * lm
                    dye = jnp.where(low if e == 0 else jnp.logical_not(low), dyp, 0.0).astype(BF16)
                    d_m = jnp.where(tril, _dot(dye, xb, _NT), 0.0)
                    d_xdt = d_xdt + _dot(m.astype(BF16), dye, _TN)
                    d_g = d_g + d_m * lm
                    w = d_m * m
                    dcol_s[...] += jnp.where(col == h, jnp.sum(w, axis=1, keepdims=True), 0.0)
                    drow_s[...] += jnp.where(row == h, jnp.sum(w, axis=0, keepdims=True), 0.0)
                stp = st_ref[0, j]
                stb = stp.astype(BF16)
                dstn = dst_s[j]
                dstb = dstn.astype(BF16)
                e_p = ex_s[:, sl]
                f_p = f_x[:, sl]
                t_p = t_x[:, sl]
                z = _dot(cgb, stb)
                d_z = (e_p * dyp).astype(BF16)
                d_c = d_c + _dot(d_z, stb, _NT)
                d_xf = _dot(bgb, dstb)
                d_b = d_b + _dot((x_dt * f_p).astype(BF16), dstb, _NT)
                d_xdt = d_xdt + f_p * d_xf
                d_f = x_dt * d_xf * f_p
                dcsx_s[:, sl] = dyp * e_p * z - d_f
                dlast_s[:, sl] = _colsum(d_f) + _colsum(dstn * stp) * t_p
                dst_s[j] = _dot(cgb, d_z, _TN) + t_p * dstn
                dxbc_ref[:, sl] = dx_ref[:, sl] * dyp + d_xdt * dtp
                ddtx_s[:, sl] = d_xdt * xp
            d_gb = d_g.astype(BF16)
            dxbc_ref[:, 1024 + g * LANE:1024 + (g + 1) * LANE] = d_b + _dot(d_gb, cgb, _TN)
            dxbc_ref[:, 1280 + g * LANE:1280 + (g + 1) * LANE] = d_c + _dot(d_gb, bgb)

        expt = expt_ref[...]
        dlast = _sel_dot(jnp.broadcast_to(dlast_s[...], (8, 1024)), expt, 3)
        d_cs = dcol_s[...] - drow_s[...].T + _sel_dot(dcsx_s[...], expt, 3)
        rown = lax.broadcasted_iota(jnp.int32, (L, LANE), 0)
        d_cs = d_cs + jnp.where(rown == L - 1, jnp.sum(dlast, axis=0, keepdims=True) * 0.125, 0.0)
        d_da = _sel_dot(d_cs, tri_ref[...], 3, _TN, sel_left=True)
        d_dt = d_da * a + _sel_dot(ddtx_s[...], expt, 3)
        dalog_ref[...] += _colsum(d_da * dt) * a
        d_raw = d_dt * _sigmoid(dt_ref[...] + bias_ref[...])
        ddt_ref[...] = d_raw.astype(ddt_ref.dtype)
        dbias_ref[...] += _colsum(d_raw)
        dd8 = _sel_dot(jnp.broadcast_to(dd_s[...], (8, 1024)), expt, 3)
        dd_ref[...] = jnp.sum(dd8, axis=0, keepdims=True) * 0.125

    const = lambda shape: pl.BlockSpec(shape, lambda c: tuple(0 for _ in shape))
    rev = lambda cb: (lambda c: (nc - 1 - c, cb))
    return pl.pallas_call(
        body, name=name, grid=(nc,),
        in_specs=[pl.BlockSpec((L, 1024), rev(0)), pl.BlockSpec((L, 256), rev(4)), pl.BlockSpec((L, 256), rev(5)),
                  pl.BlockSpec((L, LANE), rev(dtcb)), const((1, LANE)), const((1, LANE)), const((1, 1024)), const((L, L)),
                  const((LANE, 1024)), const((LANE, 2048)), const((1024, LANE)),
                  pl.BlockSpec((1, 8, LANE, LANE), lambda c: (nc - 1 - c, 0, 0, 0)), pl.BlockSpec((L, 1024), rev(0))],
        out_specs=[pl.BlockSpec((L, SSD_XBC), rev(0)), pl.BlockSpec((L, LANE), rev(0)), const((1, LANE)), const((1, LANE)),
                   const((1, LANE))],
        out_shape=[jax.ShapeDtypeStruct((s_dim, SSD_XBC), F32), jax.ShapeDtypeStruct((s_dim, LANE), BF16),
                   jax.ShapeDtypeStruct((1, LANE), F32), jax.ShapeDtypeStruct((1, LANE), F32),
                   jax.ShapeDtypeStruct((1, LANE), F32)],
        scratch_shapes=[pltpu.VMEM((8, LANE, LANE), F32), pltpu.VMEM((L, LANE), F32), pltpu.VMEM((LANE, L), F32),
                        pltpu.VMEM((L, 1024), F32), pltpu.VMEM((L, 2048), F32), pltpu.VMEM((L, 1024), F32),
                        pltpu.VMEM((L, 1024), F32), pltpu.VMEM((L, LANE), F32), pltpu.VMEM((LANE, L), F32),
                        pltpu.VMEM((1, 1024), F32), pltpu.VMEM((1, 1024), F32)],
        compiler_params=pltpu.CompilerParams(dimension_semantics=("arbitrary",)),
    )(xbca, xbca, xbca, dtr, bias, alog, d_x, tri, expand, expand128, expand_t, states, dy)


def _swap_halves(u):
    width = u.shape[1]
    lane = lax.broadcasted_iota(jnp.int32, u.shape, 1)
    return jnp.where(lane % MLA_ROPE < MLA_ROPE // 2, pltpu.roll(u, width - MLA_ROPE // 2, 1), pltpu.roll(u, MLA_ROPE // 2, 1))


def _rope_fwd_fn(u, cos, sin):
    return u * cos + _swap_halves(u) * sin


def _rope_bwd_fn(d, cos, sin):
    return d * cos + _swap_halves(d * sin)


def _spread4(v):
    return v + pltpu.roll(v, 32, 1) + pltpu.roll(v, 64, 1) + pltpu.roll(v, 96, 1)


def _att_masks(tq):
    lane = lax.broadcasted_iota(jnp.int32, (tq, LANE), 1)
    return lane // MLA_NOPE, lane // MLA_ROPE


def _att_tile(i, tq):
    klen = (i + 1) * tq
    qpos = i * tq + lax.broadcasted_iota(jnp.int32, (tq, klen), 0)
    kpos = lax.broadcasted_iota(jnp.int32, (tq, klen), 1)
    return slice(i * tq, (i + 1) * tq), klen, qpos >= kpos


def _att_qcat(qn_t, qr_t, par, e, half_id, grp_id):
    return jnp.concatenate([jnp.where(half_id == par, qn_t * ATT_SCALE, 0.0), jnp.where(grp_id == e, qr_t * ATT_SCALE, 0.0)],
                           axis=1).astype(BF16)


def _att_exp(qcat, kcat, causal):
    s = jnp.where(causal, _dot(qcat, kcat, _NT), -jnp.inf)
    e = jnp.exp(s - jnp.max(s, axis=1, keepdims=True))
    return e, 1.0 / jnp.sum(e, axis=1, keepdims=True)


def _att_specs(s_dim):
    col = lambda f: pl.BlockSpec((s_dim, LANE), lambda j: (0, f(j)))
    return [col(lambda j: j), col(lambda j: j // 2), col(lambda j: j), col(lambda j: 0), col(lambda j: 8 + j)]


def _att_fwd(q, qr, kv, krt, name="att_fwd"):
    s_dim = q.shape[0]
    tq = min(ATT_TQ, s_dim)

    def body(qn_ref, qr_ref, kn_ref, krt_ref, v_ref, o_ref, kcat_s, vb_s):
        e0 = 2 * (pl.program_id(0) % 2)
        half_id, grp_id = _att_masks(tq)
        kcat_s[...] = jnp.concatenate([kn_ref[...], krt_ref[...]], axis=1).astype(BF16)
        vb_s[...] = v_ref[...].astype(BF16)
        for i in range(s_dim // tq):
            rows, klen, causal = _att_tile(i, tq)
            qn_t, qr_t = qn_ref[rows, :], qr_ref[rows, :]
            outs = []
            for par in range(2):
                qcat = _att_qcat(qn_t, qr_t, par, e0 + par, half_id, grp_id)
                e, inv_l = _att_exp(qcat, kcat_s[0:klen, :], causal)
                outs.append(_dot(e.astype(BF16), vb_s[0:klen, :]) * inv_l)
            o_ref[rows, :] = jnp.where(half_id == 0, outs[0], outs[1])

    return pl.pallas_call(
        body, name=name, grid=(MLA_HEADS // 2,), in_specs=_att_specs(s_dim),
        out_specs=pl.BlockSpec((s_dim, LANE), lambda j: (0, j)), out_shape=jax.ShapeDtypeStruct((s_dim, 1024), F32),
        scratch_shapes=[pltpu.VMEM((s_dim, 2 * LANE), BF16), pltpu.VMEM((s_dim, LANE), BF16)],
        compiler_params=pltpu.CompilerParams(dimension_semantics=("parallel",)),
    )(q, qr, kv, krt, kv)


def _att_bwd(q, qr, kv, krt, o, do, name="att_bwd"):
    s_dim = q.shape[0]
    tq = min(ATT_TQ, s_dim)

    def body(qn_ref, qr_ref, kn_ref, krt_ref, v_ref, o_ref, do_ref, dqn_ref, dqr_ref, dkn_ref, dv_ref, dkrt_ref,
             kcat_s, vb_s):
        e0 = 2 * (pl.program_id(0) % 2)
        half_id, grp_id = _att_masks(tq)
        kcat_s[...] = jnp.concatenate([kn_ref[...], krt_ref[...]], axis=1).astype(BF16)
        vb_s[...] = v_ref[...].astype(BF16)
        dkn_ref[...] = jnp.zeros_like(dkn_ref)
        dv_ref[...] = jnp.zeros_like(dv_ref)
        dkrt_ref[...] = jnp.zeros_like(dkrt_ref)
        for i in range(s_dim // tq):
            rows, klen, causal = _att_tile(i, tq)
            qn_t, qr_t, o_t, do_t = qn_ref[rows, :], qr_ref[rows, :], o_ref[rows, :], do_ref[rows, :]
            dqn = jnp.zeros((tq, LANE), F32)
            dqr = jnp.zeros((tq, LANE), F32)
            for par in range(2):
                qcat = _att_qcat(qn_t, qr_t, par, e0 + par, half_id, grp_id)
                e, inv_l = _att_exp(qcat, kcat_s[0:klen, :], causal)
                p = e * inv_l
                dom = jnp.where(half_id == par, do_t, 0.0)
                domb = dom.astype(BF16)
                d_p = _dot(domb, vb_s[0:klen, :], _NT)
                d_row = jnp.sum(dom * o_t, axis=1, keepdims=True)
                d_s = (p * (d_p - d_row)).astype(BF16)
                dqcat = _dot(d_s, kcat_s[0:klen, :]) * ATT_SCALE
                dqn = dqn + jnp.where(half_id == par, dqcat[:, :LANE], 0.0)
                dqr = dqr + jnp.where(grp_id == e0 + par, dqcat[:, LANE:], 0.0)
                dkcat = _dot(d_s, qcat, _TN)
                dkn_ref[0:klen, :] += dkcat[:, :LANE]
                dkrt_ref[0:klen, :] += dkcat[:, LANE:]
                dv_ref[0:klen, :] += _dot(p.astype(BF16), domb, _TN)
            dqn_ref[rows, :] = dqn.astype(dqn_ref.dtype)
            dqr_ref[rows, :] = dqr

    col = lambda f: pl.BlockSpec((s_dim, LANE), lambda j: (0, f(j)))
    return pl.pallas_call(
        body, name=name, grid=(MLA_HEADS // 2,), in_specs=_att_specs(s_dim) + [col(lambda j: j), col(lambda j: j)],
        out_specs=[col(lambda j: j), pl.BlockSpec((None, s_dim, LANE), lambda j: (j % 2, 0, j // 2)), col(lambda j: j),
                   col(lambda j: j), pl.BlockSpec((None, s_dim, LANE), lambda j: (j, 0, 0))],
        out_shape=[jax.ShapeDtypeStruct((s_dim, 1024), BF16), jax.ShapeDtypeStruct((2, s_dim, 512), F32),
                   jax.ShapeDtypeStruct((s_dim, 1024), F32), jax.ShapeDtypeStruct((s_dim, 1024), F32),
                   jax.ShapeDtypeStruct((MLA_HEADS // 2, s_dim, LANE), F32)],
        scratch_shapes=[pltpu.VMEM((s_dim, 2 * LANE), BF16), pltpu.VMEM((s_dim, LANE), BF16)],
        compiler_params=pltpu.CompilerParams(dimension_semantics=("parallel",)),
    )(q, qr, kv, krt, kv, o, do)


def _all_gather(x, name):
    rows, width = x.shape

    def body(x_ref, out_ref, send_sems, recv_sems, local_sem):
        x_i, y_i, c_i = lax.axis_index("x"), lax.axis_index("y"), lax.axis_index("c")
        me, sibling = (x_i, y_i, c_i), (x_i, y_i, 1 - c_i)
        chips = [(1 - x_i, y_i), (x_i, 1 - y_i), (1 - x_i, 1 - y_i)]

        def slot(px, py, pc):
            return out_ref.at[4 * px + 2 * py + pc]

        def copy(k, block, to, src=None):
            return pltpu.make_async_remote_copy(
                src_ref=slot(*block) if src is None else src, dst_ref=slot(*block), send_sem=send_sems.at[k],
                recv_sem=recv_sems.at[k], device_id=to, device_id_type=pl.DeviceIdType.MESH)

        mine = pltpu.make_async_copy(x_ref, slot(*me), local_sem)
        mine.start()
        first = [copy(0, me, sibling, src=x_ref)]
        first += [copy(1 + j, me, (*chip, c_i), src=x_ref) for j, chip in enumerate(chips)]
        for cp in first:
            cp.start()
        passed = [copy(4 + j, (*chip, c_i), sibling) for j, chip in enumerate(chips)]
        for j, chip in enumerate(chips):
            copy(1 + j, (*chip, c_i), me).wait_recv()
            passed[j].start()
        copy(0, sibling, me).wait_recv()
        for j, chip in enumerate(chips):
            copy(4 + j, (*chip, 1 - c_i), me).wait_recv()
        for cp in first + passed:
            cp.wait_send()
        mine.wait()

    return pl.pallas_call(
        body, name=name, out_shape=jax.ShapeDtypeStruct((N_DEV, rows, width), x.dtype),
        in_specs=[pl.BlockSpec(memory_space=pl.ANY)], out_specs=pl.BlockSpec(memory_space=pl.ANY),
        scratch_shapes=[pltpu.SemaphoreType.DMA((7,)), pltpu.SemaphoreType.DMA((7,)), pltpu.SemaphoreType.DMA],
    )(x)


def _gather_many(shards, name):
    n_arr = len(shards)

    def body(*refs):
        x_refs, out_refs = refs[:n_arr], refs[n_arr:2 * n_arr]
        send_sems, recv_sems, local_sems = refs[2 * n_arr:]
        x_i, y_i, c_i = lax.axis_index("x"), lax.axis_index("y"), lax.axis_index("c")
        me, sibling = (x_i, y_i, c_i), (x_i, y_i, 1 - c_i)
        chips = [(1 - x_i, y_i), (x_i, 1 - y_i), (1 - x_i, 1 - y_i)]

        def copy(a, k, block, to, src=None):
            slot = out_refs[a].at[4 * block[0] + 2 * block[1] + block[2]]
            return pltpu.make_async_remote_copy(
                src_ref=slot if src is None else src, dst_ref=slot, send_sem=send_sems.at[a, k],
                recv_sem=recv_sems.at[a, k], device_id=to, device_id_type=pl.DeviceIdType.MESH)

        mine, first, passed = [], [], []
        for a in range(n_arr):
            mine.append(pltpu.make_async_copy(x_refs[a], out_refs[a].at[4 * x_i + 2 * y_i + c_i], local_sems.at[a]))
            mine[a].start()
            first.append([copy(a, 0, me, sibling, src=x_refs[a])]
                         + [copy(a, 1 + j, me, (*chip, c_i), src=x_refs[a]) for j, chip in enumerate(chips)])
            for cp in first[a]:
                cp.start()
            passed.append([copy(a, 4 + j, (*chip, c_i), sibling) for j, chip in enumerate(chips)])
        for j, chip in enumerate(chips):
            for a in range(n_arr):
                copy(a, 1 + j, (*chip, c_i), me).wait_recv()
                passed[a][j].start()
        for a in range(n_arr):
            copy(a, 0, sibling, me).wait_recv()
            for j, chip in enumerate(chips):
                copy(a, 4 + j, (*chip, 1 - c_i), me).wait_recv()
        for a in range(n_arr):
            for cp in first[a] + passed[a]:
                cp.wait_send()
            mine[a].wait()

    any_spec = pl.BlockSpec(memory_space=pl.ANY)
    return pl.pallas_call(
        body, name=name, out_shape=[jax.ShapeDtypeStruct((N_DEV,) + x.shape, x.dtype) for x in shards],
        in_specs=[any_spec] * n_arr, out_specs=[any_spec] * n_arr,
        scratch_shapes=[pltpu.SemaphoreType.DMA((n_arr, 7)), pltpu.SemaphoreType.DMA((n_arr, 7)),
                        pltpu.SemaphoreType.DMA((n_arr,))],
    )(*shards)


def _pair_exchange(grads, name):
    n_arr = len(grads)

    def body(*refs):
        g_refs, out_refs = refs[:n_arr], refs[n_arr:2 * n_arr]
        send_sems, recv_sems = refs[2 * n_arr:]
        x_i, y_i, c_i = lax.axis_index("x"), lax.axis_index("y"), lax.axis_index("c")
        copies = []
        for a in range(n_arr):
            for chip in range(4):
                copies.append(pltpu.make_async_remote_copy(
                    src_ref=g_refs[a].at[2 * chip + (1 - c_i)], dst_ref=out_refs[a].at[chip], send_sem=send_sems.at[a, chip],
                    recv_sem=recv_sems.at[a, chip], device_id=(x_i, y_i, 1 - c_i), device_id_type=pl.DeviceIdType.MESH))
        for cp in copies:
            cp.start()
        for cp in copies:
            cp.wait_recv()
        for cp in copies:
            cp.wait_send()

    any_spec = pl.BlockSpec(memory_space=pl.ANY)
    return pl.pallas_call(
        body, name=name, out_shape=[jax.ShapeDtypeStruct((4,) + g.shape[1:], g.dtype) for g in grads],
        in_specs=[any_spec] * n_arr, out_specs=[any_spec] * n_arr,
        scratch_shapes=[pltpu.SemaphoreType.DMA((n_arr, 4)), pltpu.SemaphoreType.DMA((n_arr, 4))],
    )(*grads)


def _chip_exchange(sums, name):
    n_arr = len(sums)

    def body(*refs):
        s_refs, out_refs = refs[:n_arr], refs[n_arr:2 * n_arr]
        send_sems, recv_sems, local_sems = refs[2 * n_arr:]
        x_i, y_i, c_i = lax.axis_index("x"), lax.axis_index("y"), lax.axis_index("c")
        my_chip = 2 * x_i + y_i
        copies, local = [], []
        for a in range(n_arr):
            local.append(pltpu.make_async_copy(s_refs[a].at[my_chip], out_refs[a].at[my_chip], local_sems.at[a]))
            local[a].start()
            for k in range(1, 4):
                px, py = x_i ^ (k >> 1), y_i ^ (k & 1)
                copies.append(pltpu.make_async_remote_copy(
                    src_ref=s_refs[a].at[2 * px + py], dst_ref=out_refs[a].at[my_chip], send_sem=send_sems.at[a, k - 1],
                    recv_sem=recv_sems.at[a, k - 1], device_id=(px, py, c_i), device_id_type=pl.DeviceIdType.MESH))
        for cp in copies:
            cp.start()
        for cp in copies:
            cp.wait_recv()
        for cp in copies:
            cp.wait_send()
        for cp in local:
            cp.wait()

    any_spec = pl.BlockSpec(memory_space=pl.ANY)
    return pl.pallas_call(
        body, name=name, out_shape=[jax.ShapeDtypeStruct(s.shape, s.dtype) for s in sums],
        in_specs=[any_spec] * n_arr, out_specs=[any_spec] * n_arr,
        scratch_shapes=[pltpu.SemaphoreType.DMA((n_arr, 3)), pltpu.SemaphoreType.DMA((n_arr, 3)),
                        pltpu.SemaphoreType.DMA((n_arr,))],
    )(*sums)


_HBM = pl.BlockSpec(memory_space=pltpu.HBM)
_SEM = pl.BlockSpec(memory_space=pltpu.SEMAPHORE)


def _plan_copies(plan, src_refs, land_refs, send_sems, recv_sems):
    copies = []
    for s_ref, l_ref in zip(src_refs, land_refs):
        for src, dst, peer in plan(s_ref, l_ref):
            k = len(copies)
            copies.append(pltpu.make_async_remote_copy(
                src_ref=src, dst_ref=dst, send_sem=send_sems.at[k], recv_sem=recv_sems.at[k], device_id=peer,
                device_id_type=pl.DeviceIdType.MESH))
    return copies


def _split_start(srcs, lands, plan, n_copy, name, after=None):
    n = len(srcs)
    n_in = 2 * n + (after is not None)

    def body(*refs):
        for cp in _plan_copies(plan, refs[:n], refs[n:2 * n], refs[n_in], refs[n_in + 1]):
            cp.start()
        refs[-1][...] = jnp.zeros_like(refs[-1])

    sems = pltpu.SemaphoreType.DMA((n * n_copy,))
    res = pl.pallas_call(
        body, name=name,
        out_shape=(sems, sems, *[pltpu.HBM(a.shape, a.dtype) for a in list(srcs) + list(lands)],
                   jax.ShapeDtypeStruct((8, LANE), F32)),
        in_specs=[_HBM] * (2 * n) + [pl.BlockSpec(memory_space=pl.ANY)] * (after is not None),
        out_specs=(_SEM, _SEM, *[_HBM] * (2 * n), pl.BlockSpec(memory_space=pltpu.VMEM)),
        input_output_aliases={i: 2 + i for i in range(2 * n)},
        compiler_params=pltpu.CompilerParams(has_side_effects=pltpu.SideEffectType.DATAFLOW_SIDE_EFFECTING),
    )(*[pltpu.with_memory_space_constraint(a, pltpu.HBM) for a in list(srcs) + list(lands)],
      *([after] if after is not None else []))
    return res[0], res[1], list(res[2:2 + n]), list(res[2 + n:2 + 2 * n]), res[-1]


def _split_wait(send_sems, recv_sems, srcs, lands, after, plan, name):
    n = len(srcs)

    def body(*refs):
        copies = _plan_copies(plan, refs[:n], refs[n:2 * n], refs[2 * n], refs[2 * n + 1])
        for cp in copies:
            cp.wait_send()
        for cp in copies:
            cp.wait_recv()

    res = pl.pallas_call(
        body, name=name, out_shape=tuple(pltpu.HBM(a.shape, a.dtype) for a in list(srcs) + list(lands)),
        in_specs=[_HBM] * (2 * n) + [_SEM, _SEM, pl.BlockSpec(memory_space=pl.ANY)], out_specs=tuple([_HBM] * (2 * n)),
        input_output_aliases={i: i for i in range(2 * n)},
        compiler_params=pltpu.CompilerParams(has_side_effects=pltpu.SideEffectType.DATAFLOW_SIDE_EFFECTING),
    )(*srcs, *lands, send_sems, recv_sems, after)
    return list(res[:n]), list(res[n:])


def _plan_broadcast(src, land):
    x_i, y_i, c_i = lax.axis_index("x"), lax.axis_index("y"), lax.axis_index("c")
    me = 4 * x_i + 2 * y_i + c_i
    return [(src, land.at[me], (x_i ^ (k >> 2), y_i ^ ((k >> 1) & 1), c_i ^ (k & 1))) for k in range(1, N_DEV)]


def _plan_scatter(src, land):
    x_i, y_i, c_i = lax.axis_index("x"), lax.axis_index("y"), lax.axis_index("c")
    me = 4 * x_i + 2 * y_i + c_i
    plan = []
    for k in range(1, N_DEV):
        px, py, pc = x_i ^ (k >> 2), y_i ^ ((k >> 1) & 1), c_i ^ (k & 1)
        plan.append((src.at[4 * px + 2 * py + pc], land.at[me], (px, py, pc)))
    return plan


def _pair_sum(g, recv, core, name):
    _, rows, cols = g.shape
    tr = ROW_TILE if rows % ROW_TILE == 0 else rows

    def body(core_ref, g_ref, r_ref, o_ref):
        o_ref[...] = (g_ref[...].astype(F32) + r_ref[...].astype(F32)).astype(o_ref.dtype)

    grid_spec = pltpu.PrefetchScalarGridSpec(
        num_scalar_prefetch=1, grid=(4, rows // tr),
        in_specs=[pl.BlockSpec((None, tr, cols), lambda k, i, core_ref: (2 * k + core_ref[0], i, 0)),
                  pl.BlockSpec((None, tr, cols), lambda k, i, core_ref: (k, i, 0))],
        out_specs=pl.BlockSpec((None, tr, cols), lambda k, i, core_ref: (k, i, 0)))
    return pl.pallas_call(body, name=name, grid_spec=grid_spec, out_shape=jax.ShapeDtypeStruct((4, rows, cols), g.dtype))(
        core, g, recv)


def _adam_math(g, w, m, v):
    m_new = ADAM_B1 * m + (1.0 - ADAM_B1) * g
    v_new = ADAM_B2 * v + (1.0 - ADAM_B2) * (g * g)
    m_hat = m_new / (1.0 - ADAM_B1 ** ADAM_STEP)
    v_hat = v_new / (1.0 - ADAM_B2 ** ADAM_STEP)
    return -ADAM_LR * (m_hat / (jnp.sqrt(v_hat) + ADAM_EPS) + ADAM_WD * w), m_new, v_new


def _adam(slots, w, m, v, name, own=None, own_idx=None):
    n_slot, rows, cols = slots.shape
    tr = ROW_TILE if rows % ROW_TILE == 0 else rows
    has_own = own is not None

    def body(*refs):
        if has_own:
            idx_ref, own_ref, refs = refs[0], refs[1], refs[2:]
        s_ref, w_ref, m_ref, v_ref, g_ref, d_ref, mo_ref, vo_ref = refs
        g = own_ref[...].astype(F32) if has_own else s_ref[0].astype(F32)
        for k in range(0 if has_own else 1, n_slot):
            part = s_ref[k].astype(F32)
            g = g + (jnp.where(idx_ref[0] == k, 0.0, part) if has_own else part)
        g_ref[...] = g
        d_ref[...], mo_ref[...], vo_ref[...] = _adam_math(g, w_ref[...], m_ref[...], v_ref[...])

    spec = pl.BlockSpec((tr, cols), lambda i, *_: (i, 0))
    in_specs = [pl.BlockSpec((n_slot, tr, cols), lambda i, *_: (0, i, 0)), spec, spec, spec]
    if has_own:
        in_specs = [pl.BlockSpec((None, tr, cols), lambda i, idx: (idx[0], i, 0))] + in_specs
    grid_spec = pltpu.PrefetchScalarGridSpec(num_scalar_prefetch=1 if has_own else 0, grid=(rows // tr,), in_specs=in_specs,
                                             out_specs=[spec] * 4)
    ins = ([own_idx, own] if has_own else []) + [slots, w, m, v]
    return pl.pallas_call(
        body, name=name, grid_spec=grid_spec, out_shape=[jax.ShapeDtypeStruct((rows, cols), F32)] * 4,
        compiler_params=pltpu.CompilerParams(dimension_semantics=("parallel",)),
    )(*ins)


PACK_ROWS, PACK_W = 24, 1536
REPL_W = (("ssd_conv_b", 1536), ("ssd_dt_bias", 16), ("ssd_A_log", 16), ("ssd_D", 16), ("ssd_norm_w", 1024),
          ("mla_q_norm_w", 384), ("mla_kv_norm_w", 256), ("mla_out_norm_w", 1024), ("ln_mix_g", 1024),
          ("ln_mix_b", 1024), ("ln_ffn_g", 1024), ("ln_ffn_b", 1024))
LOSS_ROW = 4 + len(REPL_W)


def _pack_small(conv_w_grad, grads, loss, name="pack_small"):
    def body(*refs):
        cw_ref, g_refs, loss_ref, o_ref = refs[0], refs[1:1 + len(REPL_W)], refs[1 + len(REPL_W)], refs[-1]
        o_ref[...] = jnp.zeros_like(o_ref)
        o_ref[0:4, :] = cw_ref[...]
        for i, g_ref in enumerate(g_refs):
            o_ref[4 + i:5 + i, 0:g_ref.shape[1]] = g_ref[...]
        o_ref[LOSS_ROW:LOSS_ROW + 1, 0:LANE] = loss_ref[...]

    return pl.pallas_call(body, name=name, out_shape=jax.ShapeDtypeStruct((PACK_ROWS, PACK_W), F32))(conv_w_grad, *grads, loss)


def _adam_small(gathered, wmv, name="adam_small"):
    def body(*refs):
        s_ref = refs[0]
        in_refs = refs[1:1 + 3 * len(REPL_W)]
        cw_ref, loss_ref = refs[1 + 3 * len(REPL_W)], refs[2 + 3 * len(REPL_W)]
        out_refs = refs[3 + 3 * len(REPL_W):-1]
        tot = refs[-1]
        acc = s_ref[0]
        for k in range(1, N_DEV):
            acc = acc + s_ref[k]
        tot[...] = acc
        cw_ref[...] = tot[0:4, :]
        loss_ref[...] = tot[LOSS_ROW:LOSS_ROW + 1, 0:LANE]
        for i, (_, width) in enumerate(REPL_W):
            g = tot[4 + i:5 + i, 0:width]
            w_ref, m_ref, v_ref = in_refs[3 * i:3 * i + 3]
            g_ref, d_ref, mo_ref, vo_ref = out_refs[4 * i:4 * i + 4]
            g_ref[...] = g
            d_ref[...], mo_ref[...], vo_ref[...] = _adam_math(g, w_ref[...], m_ref[...], v_ref[...])

    flat_in = [a for triple in wmv for a in triple]
    out_shape = [jax.ShapeDtypeStruct((4, PACK_W), F32), jax.ShapeDtypeStruct((1, LANE), F32)]
    for _, width in REPL_W:
        out_shape += [jax.ShapeDtypeStruct((1, width), F32)] * 4
    res = pl.pallas_call(body, name=name, out_shape=out_shape, scratch_shapes=[pltpu.VMEM((PACK_ROWS, PACK_W), F32)])(
        gathered, *flat_in)
    return res[0], res[1], [res[2 + 4 * i:6 + 4 * i] for i in range(len(REPL_W))]


def _cols_full(g):
    return jnp.transpose(g, (1, 0, 2)).reshape(g.shape[1], -1)


def _cols_split(full):
    k_dim, n_dim = full.shape
    return jnp.transpose(full.reshape(k_dim, N_DEV, n_dim // N_DEV), (1, 0, 2))


PROJ_BLOCK = {"z": (1024, 0), "dt": (LANE, 8), "q_c": (MLA_Q_RANK, 3), "xbc": (SSD_XBC, 1), "kv_c": (MLA_KV_RANK, 12),
              "k_rope": (LANE, 26)}


def _win_pad(wt):
    z = lambda n: jnp.zeros((n, wt.shape[1]), wt.dtype)
    return jnp.concatenate([wt[:1024], wt[2560:2576], z(112), wt[2576:2960], wt[1024:2560], wt[2960:3216], wt[3216:3248],
                            z(96)], axis=0)


def _win_unpad(wt):
    return jnp.concatenate([wt[:1024], wt[1536:3072], wt[1024:1040], wt[1152:1536], wt[3072:3328], wt[3328:3360]], axis=0)


def _heads_split_t(wt, a, b):
    w3 = wt.reshape(MLA_HEADS, a + b, wt.shape[1])
    return jnp.concatenate([w3[:, :a].reshape(-1, wt.shape[1]), w3[:, a:].reshape(-1, wt.shape[1])], axis=0)


def _heads_merge_t(wt, a, b):
    wa = wt[:MLA_HEADS * a].reshape(MLA_HEADS, a, wt.shape[1])
    wb = wt[MLA_HEADS * a:].reshape(MLA_HEADS, b, wt.shape[1])
    return jnp.concatenate([wa, wb], axis=1).reshape(-1, wt.shape[1])


def _heads_split(w, a, b):
    k_dim = w.shape[0]
    w3 = w.reshape(k_dim, MLA_HEADS, a + b)
    return jnp.concatenate([w3[:, :, :a].reshape(k_dim, -1), w3[:, :, a:].reshape(k_dim, -1)], axis=1)


def _heads_merge(w, a, b):
    k_dim = w.shape[0]
    wa = w[:, :MLA_HEADS * a].reshape(k_dim, MLA_HEADS, a)
    wb = w[:, MLA_HEADS * a:].reshape(k_dim, MLA_HEADS, b)
    return jnp.concatenate([wa, wb], axis=2).reshape(k_dim, -1)


def _pad_lanes(v, width=LANE):
    return jnp.concatenate([v, jnp.zeros((v.shape[0], width - v.shape[1]), v.dtype)], axis=1)


def _local_step(x, p, positions, tgt, W, P, comm=None):
    comm = comm or {}
    zero_tok = jnp.zeros((8, LANE), F32)
    s_dim = x.shape[0]
    inv_freq = 1.0 / (ROPE_BASE ** (jnp.arange(0, MLA_ROPE, 2, dtype=F32) / MLA_ROPE))
    ang = positions.astype(F32)[:, None] * inv_freq
    cos, sin = jnp.cos(ang), jnp.sin(ang)
    cos32 = jnp.concatenate([cos, cos], axis=1)
    sin32 = jnp.concatenate([-sin, sin], axis=1)
    cos512, sin512 = jnp.tile(cos32, (1, 16)), jnp.tile(sin32, (1, 16))
    cos128, sin128 = jnp.tile(cos32, (1, 4)), jnp.tile(sin32, (1, 4))
    bias_p, alog_p = _pad_lanes(P["ssd_dt_bias"]), _pad_lanes(P["ssd_A_log"])
    d_x = jnp.repeat(P["ssd_D"], SSD_HEAD_DIM, axis=1)

    xb, pb = x.astype(BF16), p.astype(BF16)
    if "token0" in comm:
        xb = (x + comm["token0"][0, 0]).astype(BF16)
    proj = _mm(xb, W["w_in"], tb=True, name="mm_in")
    z, qc, kvc, kr = [(proj,) + PROJ_BLOCK[n] for n in ("z", "q_c", "kv_c", "k_rope")]
    xbca = _conv_fwd(proj, PROJ_BLOCK["xbc"][1], P["ssd_conv_w"], P["ssd_conv_b"])
    y, states = _ssd_fwd(xbca, proj, PROJ_BLOCK["dt"][1], bias_p, alog_p, d_x)
    (yssd,) = _rowwise(_gate_rms, [y, z], [P["ssd_norm_w"]], [(1024, BF16)], name="ssd_gate_norm")
    (qn,) = _rowwise(_rms, [qc], [P["mla_q_norm_w"]], [(MLA_Q_RANK, BF16)], name="q_norm")
    (kvn,) = _rowwise(_rms, [kvc], [P["mla_kv_norm_w"]], [(MLA_KV_RANK, BF16)], name="kv_norm")
    q = _mm(qn, W["mla_w_q_b"], tb=True, name="mm_q")
    kv = _mm(kvn, W["mla_w_kv_b"], name="mm_kv")
    (qr,) = _rowwise(_rope_fwd_fn, [(q, 512, 2), cos512, sin512], [], [512], name="rope_q")
    (krt,) = _rowwise(lambda u, c, s: _spread4(_rope_fwd_fn(u, c, s)), [kr, cos128, sin128], [], [LANE], name="rope_k")
    att = _att_fwd(q, qr, kv, krt)
    (ymla,) = _rowwise(_rms, [att], [P["mla_out_norm_w"]], [(1024, BF16)], name="out_norm")
    ycat = jnp.concatenate([yssd, ymla], axis=1)
    if "late_weights" in comm:
        W = {**W, **comm["late_weights"](ycat)}
    mix = _mm(ycat, W["w_out"], name="mm_out")
    f_h1 = lambda xv, mv, g, b: _ln(ALPHA * xv + mv, g, b)
    h1, h1b = _rowwise(lambda *a: (f_h1(*a),) * 2, [x, mix], [P["ln_mix_g"], P["ln_mix_b"]], [1024, (1024, BF16)],
                       name="ln_mix")
    hg = _mm(h1b, W["w_ffn_gate"], tb=True, out_dtype=BF16, name="mm_gate")
    hu = _mm(h1b, W["w_ffn_up"], tb=True, out_dtype=BF16, name="mm_up")
    pg = _mm(h1b, W["w_ple_gate"], name="mm_ple_gate")
    pp = _mm(pb, W["w_ple_proj"], name="mm_ple")
    (act,) = _rowwise(lambda g, u: _silu(g.astype(F32)) * u.astype(F32), [hg, hu], [], [(D_FF, BF16)], name="swiglu")
    ffn = _mm(act, W["w_ffn_down"], name="mm_down")

    f_h2 = lambda hv, fv, pg, ppv, g, b: _ln(ALPHA * hv + fv + _sigmoid(pg) * ppv, g, b)

    def final_fn(hv, fv, pg, ppv, tv, g, b):
        h2, pull = jax.vjp(f_h2, hv, fv, pg, ppv, g, b)
        diff = h2 - tv
        loss = 0.5 * jnp.sum(jnp.mean(diff * diff, axis=-1, keepdims=True), axis=0, keepdims=True)
        d_h, d_f, d_pg, d_pp, d_g, d_b = pull(diff * (1.0 / D_MODEL))
        return d_h, d_f, d_pg, d_pp, d_g, d_b, jnp.broadcast_to(loss, (1, LANE))

    dh1_a, dffn, dpg, dpp, g_ffn_g, g_ffn_b, loss = _rowwise(
        final_fn, [h1, ffn, pg, pp, tgt], [P["ln_ffn_g"], P["ln_ffn_b"]], [1024] + [(1024, BF16)] * 3,
        [1024, 1024, LANE], name="final")

    G = {}
    dact = _mm(dffn, W["w_ffn_down"], tb=True, name="mm_down_dx")
    G["w_ffn_down"] = _mm(act, dffn, ta=True, out_dtype=GRAD_DT, name="mm_down_dw")

    def swiglu_bwd(g, u, d):
        g, u = g.astype(F32), u.astype(F32)
        sg = _sigmoid(g)
        return d * u * (sg * (1.0 + g * (1.0 - sg))), d * (g * sg)

    dg, du = _rowwise(swiglu_bwd, [hg, hu, dact], [], [(D_FF, BF16)] * 2, name="swiglu_bwd")
    dh1 = _mm(dg, W["w_ffn_gate"], add=dh1_a, name="mm_gate_dx")
    dh1 = _mm(du, W["w_ffn_up"], add=dh1, name="mm_up_dx")
    dh1 = _mm(dpg, W["w_ple_gate"], tb=True, add=dh1, name="mm_ple_gate_dx")
    G["w_ffn_gate"] = _mm(dg, h1b, ta=True, out_dtype=GRAD_DT, name="mm_gate_dw")
    G["w_ffn_up"] = _mm(du, h1b, ta=True, out_dtype=GRAD_DT, name="mm_up_dw")
    G["w_ple_gate"] = _mm(h1b, dpg, ta=True, out_dtype=GRAD_DT, name="mm_ple_gate_dw")
    G["w_ple_proj"] = _mm(pb, dpp, ta=True, out_dtype=GRAD_DT, name="mm_ple_dw")
    dx_a, dmix, g_mix_g, g_mix_b = _rowwise(
        lambda xv, mv, dv, g, b: _vjp_rows(f_h1)(xv, mv, g, b, dv), [x, mix, dh1], [P["ln_mix_g"], P["ln_mix_b"]],
        [1024, (1024, BF16)], [1024, 1024], name="ln_mix_bwd")
    dycat = _mm(dmix, W["w_out"], tb=True, name="mm_out_dx")
    G["w_out"] = _mm(ycat, dmix, ta=True, out_dtype=GRAD_DT, name="mm_out_dw")

    tok1 = comm["ffn_grads"](G) if "ffn_grads" in comm else zero_tok
    datt, g_out_norm = _rowwise(lambda a, dv, w, t: _vjp_rows(_rms)(a, w, dv + jnp.min(t)), [att, (dycat, 1024, 1)],
                                [P["mla_out_norm_w"], tok1], [1024], [1024], name="out_norm_bwd")
    dqn_nope, dqr, dkn, dv, dkrt = _att_bwd(q, qr, kv, krt, att, datt)
    dkv = jnp.concatenate([dkn, dv], axis=1)
    (dq_rope,) = _rowwise(lambda d0, d1, c, s: _rope_bwd_fn(d0 + d1, c, s), [(dqr, 512, 0), (dqr, 512, 1), cos512, sin512],
                          [], [(512, BF16)], name="rope_q_bwd")

    def rope_k_bwd(*a):
        d = _spread4(functools.reduce(lambda u, w: u + w, a[:-2]))
        lane = lax.broadcasted_iota(jnp.int32, d.shape, 1)
        return _rope_bwd_fn(jnp.where(lane < MLA_ROPE, d, 0.0), a[-2], a[-1])

    (dkr,) = _rowwise(rope_k_bwd, [(dkrt, LANE, k) for k in range(MLA_HEADS // 2)] + [cos128, sin128], [], [(LANE, BF16)],
                      name="rope_k_bwd")
    dq = jnp.concatenate([dqn_nope, dq_rope], axis=1)
    dqn = _mm(dq, W["mla_w_q_b"], name="mm_q_dx")
    G["mla_w_q_b"] = _mm(dq, qn, ta=True, out_dtype=GRAD_DT, name="mm_q_dw")
    dkvn = _mm(dkv, W["mla_w_kv_b"], tb=True, name="mm_kv_dx")
    G["mla_w_kv_b"] = _mm(kvn, dkv, ta=True, out_dtype=GRAD_DT, name="mm_kv_dw")
    dqc, g_q_norm = _rowwise(lambda a, dv, w: _vjp_rows(_rms)(a, w, dv), [qc, dqn], [P["mla_q_norm_w"]],
                             [(MLA_Q_RANK, BF16)], [MLA_Q_RANK], name="q_norm_bwd")
    dkvc, g_kv_norm = _rowwise(lambda a, dv, w: _vjp_rows(_rms)(a, w, dv), [kvc, dkvn], [P["mla_kv_norm_w"]],
                               [(MLA_KV_RANK, BF16)], [MLA_KV_RANK], name="kv_norm_bwd")

    dy, dz, g_ssd_norm = _rowwise(lambda yv, zv, dv, w, t: _vjp_rows(_gate_rms)(yv, zv, w, dv + jnp.min(t)),
                                  [y, z, (dycat, 1024, 0)], [P["ssd_norm_w"], tok1], [1024, (1024, BF16)], [1024],
                                  name="ssd_gate_norm_bwd")
    dxbca, ddtr, g_dt_bias, g_alog, g_d = _ssd_bwd(xbca, proj, PROJ_BLOCK["dt"][1], bias_p, alog_p, d_x, states, dy)
    da, g_conv_w, g_conv_b = _conv_bwd_pre(proj, PROJ_BLOCK["xbc"][1], P["ssd_conv_w"], P["ssd_conv_b"], dxbca)
    dxbc = _conv_bwd_in(da, P["ssd_conv_w"])

    dproj = jnp.concatenate([dz, ddtr, dqc, dxbc, dkvc, dkr], axis=1)
    grad_x = _mm(dproj, W["w_in"], add=dx_a, name="mm_in_dx")
    G["w_in"] = _mm(dproj, xb, ta=True, out_dtype=GRAD_DT, name="mm_in_dw")

    small = {
        "ssd_conv_b": g_conv_b, "ssd_dt_bias": g_dt_bias, "ssd_A_log": g_alog, "ssd_D": g_d, "ssd_norm_w": g_ssd_norm,
        "mla_q_norm_w": g_q_norm, "mla_kv_norm_w": g_kv_norm, "mla_out_norm_w": g_out_norm, "ln_mix_g": g_mix_g,
        "ln_mix_b": g_mix_b, "ln_ffn_g": g_ffn_g, "ln_ffn_b": g_ffn_b,
    }
    return grad_x, G, _pack_small(g_conv_w, [small[n] for n, _ in REPL_W], loss)


def kernel(x, p, positions, w_in, ssd_conv_w, ssd_conv_b, ssd_dt_bias, ssd_A_log, ssd_D, ssd_norm_w, mla_q_norm_w, mla_w_q_b, mla_kv_norm_w, mla_w_kv_b, mla_out_norm_w, w_out, ln_mix_g, ln_mix_b, w_ffn_gate, w_ffn_up, w_ffn_down, w_ple_gate, w_ple_proj, ln_ffn_g, ln_ffn_b, loss_target, m_w_in, m_ssd_conv_w, m_ssd_conv_b, m_ssd_dt_bias, m_ssd_A_log, m_ssd_D, m_ssd_norm_w, m_mla_q_norm_w, m_mla_w_q_b, m_mla_kv_norm_w, m_mla_w_kv_b, m_mla_out_norm_w, m_w_out, m_ln_mix_g, m_ln_mix_b, m_w_ffn_gate, m_w_ffn_up, m_w_ffn_down, m_w_ple_gate, m_w_ple_proj, m_ln_ffn_g, m_ln_ffn_b, v_w_in, v_ssd_conv_w, v_ssd_conv_b, v_ssd_dt_bias, v_ssd_A_log, v_ssd_D, v_ssd_norm_w, v_mla_q_norm_w, v_mla_w_q_b, v_mla_kv_norm_w, v_mla_w_kv_b, v_mla_out_norm_w, v_w_out, v_ln_mix_g, v_ln_mix_b, v_w_ffn_gate, v_w_ffn_up, v_w_ffn_down, v_w_ple_gate, v_w_ple_proj, v_ln_ffn_g, v_ln_ffn_b):
    args = dict(locals())
    core = lax.axis_index("c")
    me = 4 * lax.axis_index("x") + 2 * lax.axis_index("y") + core

    conv_sh = ssd_conv_w[0]
    conv_hi = conv_sh.astype(BF16)
    conv_lo = (conv_sh - conv_hi.astype(F32)).astype(BF16)
    stored = lambda n, pre="": jnp.transpose(args[pre + n][0]) if n in TRANSPOSED else args[pre + n][0]
    shards = {n: stored(n).astype(BF16) for n in BIG}
    rows_full = lambda g: g.reshape(-1, g.shape[2])
    core_arr = core.astype(jnp.int32).reshape(1)

    early = _gather_many([shards[n] for n in EARLY] + [jnp.concatenate([conv_hi, conv_lo], axis=0)], "gather_early")
    gw = dict(zip(EARLY, early[:-1]))
    conv_g = early[-1].astype(F32)
    W = {
        "w_in": _win_pad(rows_full(gw["w_in"])),
        "mla_w_q_b": _heads_split_t(rows_full(gw["mla_w_q_b"]), MLA_NOPE, MLA_ROPE),
        "mla_w_kv_b": _heads_split(_cols_full(gw["mla_w_kv_b"]), MLA_NOPE, MLA_V),
    }
    P = {n: args[n] for n, _ in REPL_W}
    P["ssd_conv_w"] = _cols_full(conv_g[:, :4] + conv_g[:, 4:])

    lands = [lax.dynamic_update_slice(lax.empty((N_DEV,) + shards[n].shape, BF16), shards[n][None], (me, 0, 0)) for n in LATE]
    late_sems = _split_start([shards[n] for n in LATE], lands, _plan_broadcast, N_DEV - 1, "gather_late_start",
                             after=early[0])

    def late_weights(after):
        _, got = _split_wait(*late_sems[:4], after, _plan_broadcast, "gather_late_wait")
        lw = dict(zip(LATE, got))
        return {"w_out": rows_full(lw["w_out"]), "w_ple_gate": rows_full(lw["w_ple_gate"]),
                "w_ple_proj": _cols_full(lw["w_ple_proj"]), "w_ffn_gate": rows_full(lw["w_ffn_gate"]),
                "w_ffn_up": rows_full(lw["w_ffn_up"]), "w_ffn_down": rows_full(lw["w_ffn_down"])}

    def to_blocks(n, g):
        if n == "w_in":
            g = _win_unpad(g)
        elif n == "mla_w_q_b":
            g = _heads_merge_t(g, MLA_NOPE, MLA_ROPE)
        elif n == "mla_w_kv_b":
            g = _heads_merge(g, MLA_NOPE, MLA_V)
        if n in ROW_SHARDED or n in TRANSPOSED:
            return g.reshape(N_DEV, -1, g.shape[1])
        return _cols_split(g)

    flight = {}

    def ffn_grads(G):
        gl = [to_blocks(n, G[n]) for n in LATE_GRADS]
        flight["grads"] = _split_start(gl, [lax.empty(g.shape, g.dtype) for g in gl], _plan_scatter, N_DEV - 1, "grads_start")
        return flight["grads"][4]

    grad_x, G, packed = _local_step(x[0], p[0, 0], positions[0], loss_target[0], W, P,
                                    comm={"token0": late_sems[4], "late_weights": late_weights, "ffn_grads": ffn_grads})

    wmv = lambda n: (stored(n), stored(n, "m_"), stored(n, "v_"))
    mine, recv = _split_wait(*flight["grads"][:4], grad_x, _plan_scatter, "grads_wait")
    me_arr = me.astype(jnp.int32).reshape(1)
    big_out = {n: _adam(r, *wmv(n), "adam_" + n, own=g, own_idx=me_arr) for n, g, r in zip(LATE_GRADS, mine, recv)}

    glist = [to_blocks(n, G[n]) for n in LAST_GRADS]
    from_sibling = _pair_exchange(glist, "exchange_pairs")
    sums = [_pair_sum(g, r, core_arr, "pair_sum_" + n) for n, g, r in zip(LAST_GRADS, glist, from_sibling)]
    recv = _chip_exchange(sums, "exchange_chips")
    big_out.update({n: _adam(r, *wmv(n), "adam_" + n) for n, r in zip(LAST_GRADS, recv)})

    small_all = _all_gather(packed, "gather_small")
    conv_sum, loss_row, small_out = _adam_small(small_all, [(args[n], args["m_" + n], args["v_" + n]) for n, _ in REPL_W])
    conv_grad = lax.dynamic_slice_in_dim(conv_sum, me * 192, 192, axis=1)
    conv_out = _adam(conv_grad[None], conv_sh, m_ssd_conv_w[0], v_ssd_conv_w[0], "adam_conv")
    small_map = {n: small_out[i] for i, (n, _) in enumerate(REPL_W)}

    def outputs(idx):
        res = []
        for n in WEIGHT_ORDER:
            if n == "ssd_conv_w":
                res.append(conv_out[idx][None])
            elif n in big_out:
                res.append((jnp.transpose(big_out[n][idx]) if n in TRANSPOSED else big_out[n][idx])[None])
            else:
                res.append(small_map[n][idx])
        return res

    return (loss_row[0, 0], grad_x[None], *outputs(0), *outputs(1), *outputs(2), *outputs(3))
```

```python
import functools
import math

import numpy as np
import jax
import jax.numpy as jnp
from jax import lax
from jax.experimental import pallas as pl
from jax.experimental.pallas import tpu as pltpu

F32 = jnp.float32
BF16 = jnp.bfloat16
HI = lax.Precision.HIGHEST

N_DEV = 8
D_MODEL = 1024
PLE_DIM = 256
SSD_HEADS = 16
SSD_HEAD_DIM = 64
SSD_INNER = 1024
SSD_STATE = 128
SSD_XBC = 1536
SSD_CHUNK = 128
MLA_HEADS = 16
MLA_Q_RANK = 384
MLA_KV_RANK = 256
MLA_NOPE = 64
MLA_ROPE = 32
MLA_V = 64
ROPE_BASE = 10000.0
D_FF = 2816
IN_WIDTH = 3248
IN_PAD = 3456
ALPHA = 2.0 ** 0.25
EPS = 1e-6
LN_EPS = 1e-5
ATT_SCALE = 1.0 / math.sqrt(MLA_NOPE + MLA_ROPE)
ADAM_LR, ADAM_B1, ADAM_B2, ADAM_EPS, ADAM_WD, ADAM_STEP = 0.001, 0.9, 0.999, 1e-08, 0.01, 10

LANE = 128
MXU_DIM = 256
MM_TM, MM_TN, MM_TK = 1408, 1408, 2048
ROW_TILE = 256
ATT_TQ = 256

GRAD_DT = BF16

BIG = ("w_in", "mla_w_q_b", "mla_w_kv_b", "w_out", "w_ffn_gate", "w_ffn_up", "w_ffn_down", "w_ple_gate", "w_ple_proj")
EARLY = ("w_in", "mla_w_q_b", "mla_w_kv_b")
LATE = ("w_out", "w_ffn_gate", "w_ffn_up", "w_ffn_down", "w_ple_gate", "w_ple_proj")
GRAD_GROUPS = {"late": ("w_ffn_gate", "w_ffn_up", "w_ffn_down", "w_ple_gate", "w_ple_proj", "w_out"),
               "mla": ("mla_w_q_b", "mla_w_kv_b"), "in": ("w_in",)}
ROW_SHARDED = ("w_out", "w_ffn_down", "w_ple_gate")
TRANSPOSED = ("w_in", "mla_w_q_b", "w_ffn_gate", "w_ffn_up")
WEIGHT_ORDER = ("w_in", "ssd_conv_w", "ssd_conv_b", "ssd_dt_bias", "ssd_A_log", "ssd_D", "ssd_norm_w", "mla_q_norm_w",
                "mla_w_q_b", "mla_kv_norm_w", "mla_w_kv_b", "mla_out_norm_w", "w_out", "ln_mix_g", "ln_mix_b",
                "w_ffn_gate", "w_ffn_up", "w_ffn_down", "w_ple_gate", "w_ple_proj", "ln_ffn_g", "ln_ffn_b")


def _tile(dim, cap, prefer=None):
    cands = [t for t in range(LANE, min(cap, dim) + 1, LANE) if dim % t == 0]
    if not cands:
        return dim
    if prefer is None:
        return max(cands)
    fill = lambda t: t / (MXU_DIM * -(-t // MXU_DIM))
    good = min(0.9, max(fill(t) for t in cands))
    return min((t for t in cands if fill(t) >= good), key=lambda t: abs(t - prefer))


def _dot(a, b, dims=(((1,), (0,)), ((), ())), precision=None):
    return lax.dot_general(a, b, dims, preferred_element_type=F32, precision=precision)


_NT = (((1,), (1,)), ((), ()))
_TN = (((0,), (0,)), ((), ()))


def _mm(a, b, *, ta=False, tb=False, add=None, out_dtype=F32, after=None, name):
    k_dim, m_dim = a.shape if ta else a.shape[::-1]
    n_dim, kb = b.shape if tb else b.shape[::-1]
    assert k_dim == kb
    tm, tn, tk = _tile(m_dim, MM_TM), _tile(n_dim, MM_TN, prefer=1024), _tile(k_dim, MM_TK, prefer=MM_TK)
    nk = k_dim // tk
    dims = (((0 if ta else 1,), (1 if tb else 0,)), ((), ()))
    has_add = add is not None
    a_spec = pl.BlockSpec((tk, tm), lambda i, j, k: (k, i)) if ta else pl.BlockSpec((tm, tk), lambda i, j, k: (i, k))
    b_spec = pl.BlockSpec((tn, tk), lambda i, j, k: (j, k)) if tb else pl.BlockSpec((tk, tn), lambda i, j, k: (k, j))
    o_spec = pl.BlockSpec((tm, tn), lambda i, j, k: (i, j))

    def body(*refs):
        refs = [r for i, r in enumerate(refs) if i != 2 + has_add] if after is not None else refs
        if has_add:
            a_ref, b_ref, add_ref, o_ref = refs[:4]
        else:
            a_ref, b_ref, o_ref = refs[:3]
        part = _dot(a_ref[...].astype(BF16), b_ref[...].astype(BF16), dims)
        if nk == 1:
            o_ref[...] = ((part + add_ref[...]) if has_add else part).astype(o_ref.dtype)
            return
        acc = refs[-1]
        k = pl.program_id(2)

        @pl.when(k == 0)
        def _():
            acc[...] = (part + add_ref[...]) if has_add else part

        @pl.when(k > 0)
        def _():
            acc[...] += part

        @pl.when(k == nk - 1)
        def _():
            o_ref[...] = acc[...].astype(o_ref.dtype)

    ins = [a, b] + ([add] if has_add else []) + ([after] if after is not None else [])
    specs = [a_spec, b_spec] + ([o_spec] if has_add else []) + ([pl.BlockSpec(memory_space=pl.ANY)] if after is not None else [])
    return pl.pallas_call(
        body, name=name, grid=(m_dim // tm, n_dim // tn, nk), in_specs=specs, out_specs=o_spec,
        out_shape=jax.ShapeDtypeStruct((m_dim, n_dim), out_dtype),
        scratch_shapes=[pltpu.VMEM((tm, tn), F32)] if nk > 1 else [],
        compiler_params=pltpu.CompilerParams(dimension_semantics=("parallel", "parallel", "arbitrary")),
    )(*ins)


def _rowwise(fn, rows, consts, out_widths, acc_widths=(), *, name, tr=ROW_TILE):
    row_arrays, row_specs = [], []
    first_arr = rows[0][0] if isinstance(rows[0], tuple) else rows[0]
    s_dim = first_arr.shape[-2]
    tr = min(tr, s_dim)
    for r in rows:
        arr, width, cb = r if isinstance(r, tuple) else (r, r.shape[-1], 0)
        row_arrays.append(arr)
        if arr.ndim == 3:
            row_specs.append(pl.BlockSpec((None, tr, width), functools.partial(lambda i, k: (k, i, 0), k=cb)))
        else:
            row_specs.append(pl.BlockSpec((tr, width), functools.partial(lambda i, cb: (i, cb), cb=cb)))
    const_specs = [pl.BlockSpec(c.shape, lambda i: (0, 0)) for c in consts]
    nr, nc, no, na = len(rows), len(consts), len(out_widths), len(acc_widths)

    def body(*refs):
        ins = [r[...] for r in refs[:nr + nc]]
        res = fn(*ins)
        if not isinstance(res, (tuple, list)):
            res = (res,)
        out_refs = refs[nr + nc:nr + nc + no]
        acc_refs = refs[nr + nc + no:]
        for o_ref, val in zip(out_refs, res[:no]):
            o_ref[...] = val.astype(o_ref.dtype)
        first = pl.program_id(0) == 0
        for a_ref, val in zip(acc_refs, res[no:]):
            @pl.when(first)
            def _(a_ref=a_ref, val=val):
                a_ref[...] = val

            @pl.when(jnp.logical_not(first))
            def _(a_ref=a_ref, val=val):
                a_ref[...] += val

    outs = [w if isinstance(w, tuple) else (w, F32) for w in out_widths]
    out_shape = [jax.ShapeDtypeStruct((s_dim, w), dt) for w, dt in outs]
    out_shape += [jax.ShapeDtypeStruct((1, w), F32) for w in acc_widths]
    out_specs = [pl.BlockSpec((tr, w), lambda i: (i, 0)) for w, _ in outs]
    out_specs += [pl.BlockSpec((1, w), lambda i: (0, 0)) for w in acc_widths]
    res = pl.pallas_call(
        body, name=name, grid=(s_dim // tr,), in_specs=row_specs + const_specs, out_specs=out_specs, out_shape=out_shape,
        compiler_params=pltpu.CompilerParams(dimension_semantics=("arbitrary",)),
    )(*row_arrays, *consts)
    return res


def _colsum(v):
    return jnp.sum(v, axis=0, keepdims=True)


def _rms(u, g):
    return u * lax.rsqrt(jnp.mean(u * u, axis=-1, keepdims=True) + EPS) * g


def _ln(u, g, b):
    mu = jnp.mean(u, axis=-1, keepdims=True)
    d = u - mu
    var = jnp.mean(d * d, axis=-1, keepdims=True)
    return d * lax.rsqrt(var + LN_EPS) * g + b


def _sigmoid(v):
    return 1.0 / (1.0 + jnp.exp(-v))


def _silu(v):
    return v * _sigmoid(v)


def _softplus(v):
    y = jnp.exp(-jnp.abs(v))
    w = 1.0 + y
    log1p = jnp.where(w == 1.0, y, jnp.log(w) * y / jnp.where(w == 1.0, 1.0, w - 1.0))
    return jnp.maximum(v, 0.0) + log1p


def _gate_rms(y, z, w):
    return _rms(y * _silu(z), w)


def _vjp_rows(f):
    def fn(*args):
        prim, ct = args[:-1], args[-1]
        _, pull = jax.vjp(f, *prim)
        return pull(ct)
    return fn


def _conv_pre(cur, prev, w, b, first):
    row = lax.broadcasted_iota(jnp.int32, cur.shape, 0)
    acc = cur * w[3:4, :] + b
    for j in (1, 2, 3):
        tail = jnp.where(first, 0.0, pltpu.roll(prev, j, 0))
        acc = acc + jnp.where(row >= j, pltpu.roll(cur, j, 0), tail) * w[3 - j:4 - j, :]
    return acc


def _conv_fwd(u, ucb, w, b, name="conv_fwd"):
    s_dim, width = u.shape[0], w.shape[1]
    tr = min(ROW_TILE, s_dim)

    def body(cur_ref, prev_ref, w_ref, b_ref, o_ref):
        pre = _conv_pre(cur_ref[...], prev_ref[...], w_ref, b_ref[...], pl.program_id(0) == 0)
        o_ref[...] = _silu(pre)

    return pl.pallas_call(
        body, name=name, grid=(s_dim // tr,),
        in_specs=[pl.BlockSpec((tr, width), lambda i: (i, ucb)),
                  pl.BlockSpec((tr, width), lambda i: (jnp.maximum(i - 1, 0), ucb)),
                  pl.BlockSpec(w.shape, lambda i: (0, 0)), pl.BlockSpec(b.shape, lambda i: (0, 0))],
        out_specs=pl.BlockSpec((tr, width), lambda i: (i, 0)), out_shape=jax.ShapeDtypeStruct((s_dim, width), F32),
        compiler_params=pltpu.CompilerParams(dimension_semantics=("arbitrary",)),
    )(u, u, w, b)


def _conv_bwd_pre(u, ucb, w, b, dact, name="conv_bwd_pre"):
    s_dim, width = u.shape[0], w.shape[1]
    tr = min(ROW_TILE, s_dim)

    def body(cur_ref, prev_ref, w_ref, b_ref, d_ref, da_ref, dw_ref, db_ref):
        first = pl.program_id(0) == 0
        cur, prev = cur_ref[...], prev_ref[...]
        pre = _conv_pre(cur, prev, w_ref, b_ref[...], first)
        sg = _sigmoid(pre)
        da = d_ref[...] * (sg * (1.0 + pre * (1.0 - sg)))
        da_ref[...] = da
        row = lax.broadcasted_iota(jnp.int32, cur.shape, 0)

        @pl.when(first)
        def _():
            dw_ref[...] = jnp.zeros_like(dw_ref)
            db_ref[...] = jnp.zeros_like(db_ref)

        db_ref[...] += _colsum(da)
        dw_ref[3:4, :] += _colsum(da * cur)
        for j in (1, 2, 3):
            tail = jnp.where(first, 0.0, pltpu.roll(prev, j, 0))
            sh = jnp.where(row >= j, pltpu.roll(cur, j, 0), tail)
            dw_ref[3 - j:4 - j, :] += _colsum(da * sh)

    return pl.pallas_call(
        body, name=name, grid=(s_dim // tr,),
        in_specs=[pl.BlockSpec((tr, width), lambda i: (i, ucb)),
                  pl.BlockSpec((tr, width), lambda i: (jnp.maximum(i - 1, 0), ucb)),
                  pl.BlockSpec(w.shape, lambda i: (0, 0)), pl.BlockSpec(b.shape, lambda i: (0, 0)),
                  pl.BlockSpec((tr, width), lambda i: (i, 0))],
        out_specs=[pl.BlockSpec((tr, width), lambda i: (i, 0)), pl.BlockSpec(w.shape, lambda i: (0, 0)),
                   pl.BlockSpec(b.shape, lambda i: (0, 0))],
        out_shape=[jax.ShapeDtypeStruct((s_dim, width), F32), jax.ShapeDtypeStruct(w.shape, F32),
                   jax.ShapeDtypeStruct(b.shape, F32)],
        compiler_params=pltpu.CompilerParams(dimension_semantics=("arbitrary",)),
    )(u, u, w, b, dact)


def _conv_bwd_in(da, w, name="conv_bwd_in"):
    s_dim, width = da.shape
    tr = min(ROW_TILE, s_dim)
    n = s_dim // tr

    def body(cur_ref, nxt_ref, w_ref, o_ref):
        last = pl.program_id(0) == n - 1
        cur, nxt = cur_ref[...], nxt_ref[...]
        row = lax.broadcasted_iota(jnp.int32, cur.shape, 0)
        acc = cur * w_ref[3:4, :]
        for j in (1, 2, 3):
            head = jnp.where(last, 0.0, pltpu.roll(nxt, tr - j, 0))
            acc = acc + jnp.where(row < tr - j, pltpu.roll(cur, tr - j, 0), head) * w_ref[3 - j:4 - j, :]
        o_ref[...] = acc.astype(o_ref.dtype)

    return pl.pallas_call(
        body, name=name, grid=(n,),
        in_specs=[pl.BlockSpec((tr, width), lambda i: (i, 0)), pl.BlockSpec((tr, width), lambda i: (jnp.minimum(i + 1, n - 1), 0)),
                  pl.BlockSpec(w.shape, lambda i: (0, 0))],
        out_specs=pl.BlockSpec((tr, width), lambda i: (i, 0)), out_shape=jax.ShapeDtypeStruct((s_dim, width), BF16),
        compiler_params=pltpu.CompilerParams(dimension_semantics=("arbitrary",)),
    )(da, da, w)


def _sel_dot(a, sel, pieces, dims=(((1,), (0,)), ((), ())), sel_left=False):
    sel = sel.astype(BF16)
    acc, rest = None, a
    for _ in range(pieces):
        piece = rest.astype(BF16)
        rest = rest - piece.astype(F32)
        part = _dot(sel, piece, dims) if sel_left else _dot(piece, sel, dims)
        acc = part if acc is None else acc + part
    return acc


def _ssd_consts():
    L = SSD_CHUNK
    tri = np.tril(np.ones((L, L), np.float32))
    expand = np.zeros((LANE, SSD_INNER), np.float32)
    expand128 = np.zeros((LANE, SSD_HEADS * LANE), np.float32)
    for h in range(SSD_HEADS):
        expand[h, h * SSD_HEAD_DIM:(h + 1) * SSD_HEAD_DIM] = 1.0
        expand128[h, h * LANE:(h + 1) * LANE] = 1.0
    return jnp.asarray(tri), jnp.asarray(expand), jnp.asarray(expand128), jnp.asarray(expand.T.copy())


def _ssd_prep(dt_ref, bias_ref, alog_ref, tri_ref, exp_ref, exp128_ref, cs_s, cst_s, ex_s, csx_s):
    L = SSD_CHUNK
    dt = _softplus(dt_ref[...] + bias_ref[...])
    a = -jnp.exp(alog_ref[...])
    cs = _sel_dot(dt * a, tri_ref[...], 3, sel_left=True)
    cs_s[...] = cs
    cst_s[...] = cs.T
    last = cs_s[L - 1:L, :]
    expand = exp_ref[...]
    ex_s[...] = _sel_dot(jnp.exp(cs), expand, 2)
    f_x = _sel_dot(jnp.exp(last - cs), expand, 2)
    dt_x = _sel_dot(dt, expand, 2)
    csx_s[...] = _sel_dot(cs, exp128_ref[...], 3)
    t_x = ex_s[L - 1:L, :]
    return dt, a, dt_x, f_x, t_x


def _decay_matrix(csx_s, cst_s, h, tril):
    seg = csx_s[:, h * LANE:(h + 1) * LANE] - cst_s[h:h + 1, :]
    return jnp.exp(jnp.where(tril, seg, -jnp.inf))


def _ssd_fwd(xbca, dtr, dtcb, bias, alog, d_x, name="ssd_fwd"):
    s_dim = xbca.shape[0]
    L = SSD_CHUNK
    nc = s_dim // L
    tri, expand, expand128, _ = _ssd_consts()

    def body(xs_ref, b_ref, c_ref, dt_ref, bias_ref, alog_ref, dx_ref, tri_ref, exp_ref, exp128_ref,
             y_ref, st_ref, st_s, cs_s, cst_s, ex_s, csx_s):
        @pl.when(pl.program_id(0) == 0)
        def _():
            st_s[...] = jnp.zeros_like(st_s)

        dt, a, dt_x, f_x, t_x = _ssd_prep(dt_ref, bias_ref, alog_ref, tri_ref, exp_ref, exp128_ref, cs_s, cst_s, ex_s, csx_s)
        st_ref[0] = st_s[...]
        row = lax.broadcasted_iota(jnp.int32, (L, L), 0)
        col = lax.broadcasted_iota(jnp.int32, (L, L), 1)
        tril = row >= col
        low = col < SSD_HEAD_DIM
        for g in range(2):
            bg = b_ref[:, g * LANE:(g + 1) * LANE]
            cg = c_ref[:, g * LANE:(g + 1) * LANE].astype(BF16)
            gmat = _dot(cg, bg.astype(BF16), _NT)
            bgt = bg.T.astype(BF16)
            for jj in range(4):
                j = 4 * g + jj
                sl = slice(j * LANE, (j + 1) * LANE)
                xp = xs_ref[:, sl]
                x_dt = xp * dt_x[:, sl]
                xb = x_dt.astype(BF16)
                yd = []
                for e in range(2):
                    lm = _decay_matrix(csx_s, cst_s, 2 * j + e, tril)
                    yd.append(_dot((gmat * lm).astype(BF16), xb))
                stp = st_s[j]
                z = _dot(cg, stp.astype(BF16))
                y_ref[:, sl] = jnp.where(low, yd[0], yd[1]) + ex_s[:, sl] * z + dx_ref[:, sl] * xp
                xf = (x_dt * f_x[:, sl]).astype(BF16)
                st_s[j] = t_x[:, sl] * stp + _dot(bgt, xf)

    const = lambda shape: pl.BlockSpec(shape, lambda c: tuple(0 for _ in shape))
    return pl.pallas_call(
        body, name=name, grid=(nc,),
        in_specs=[pl.BlockSpec((L, 1024), lambda c: (c, 0)), pl.BlockSpec((L, 256), lambda c: (c, 4)),
                  pl.BlockSpec((L, 256), lambda c: (c, 5)), pl.BlockSpec((L, LANE), lambda c: (c, dtcb)),
                  const((1, LANE)), const((1, LANE)), const((1, 1024)), const((L, L)), const((LANE, 1024)),
                  const((LANE, 2048))],
        out_specs=[pl.BlockSpec((L, 1024), lambda c: (c, 0)), pl.BlockSpec((1, 8, LANE, LANE), lambda c: (c, 0, 0, 0))],
        out_shape=[jax.ShapeDtypeStruct((s_dim, 1024), F32), jax.ShapeDtypeStruct((nc, 8, LANE, LANE), F32)],
        scratch_shapes=[pltpu.VMEM((8, LANE, LANE), F32), pltpu.VMEM((L, LANE), F32), pltpu.VMEM((LANE, L), F32),
                        pltpu.VMEM((L, 1024), F32), pltpu.VMEM((L, 2048), F32)],
        compiler_params=pltpu.CompilerParams(dimension_semantics=("arbitrary",)),
    )(xbca, xbca, xbca, dtr, bias, alog, d_x, tri, expand, expand128)


def _ssd_bwd(xbca, dtr, dtcb, bias, alog, d_x, states, dy, name="ssd_bwd"):
    s_dim = xbca.shape[0]
    L = SSD_CHUNK
    nc = s_dim // L
    tri, expand, expand128, expand_t = _ssd_consts()

    def body(xs_ref, b_ref, c_ref, dt_ref, bias_ref, alog_ref, dx_ref, tri_ref, exp_ref, exp128_ref, expt_ref,
             st_ref, dy_ref, dxbc_ref, ddt_ref, dbias_ref, dalog_ref, dd_ref,
             dst_s, cs_s, cst_s, ex_s, csx_s, dcsx_s, ddtx_s, dcol_s, drow_s, dlast_s, dd_s):
        @pl.when(pl.program_id(0) == 0)
        def _():
            dst_s[...] = jnp.zeros_like(dst_s)
            dbias_ref[...] = jnp.zeros_like(dbias_ref)
            dalog_ref[...] = jnp.zeros_like(dalog_ref)
            dd_s[...] = jnp.zeros_like(dd_s)

        dt, a, dt_x, f_x, t_x = _ssd_prep(dt_ref, bias_ref, alog_ref, tri_ref, exp_ref, exp128_ref, cs_s, cst_s, ex_s, csx_s)
        row = lax.broadcasted_iota(jnp.int32, (L, L), 0)
        col = lax.broadcasted_iota(jnp.int32, (L, L), 1)
        tril = row >= col
        low = col < SSD_HEAD_DIM
        dcol_s[...] = jnp.zeros_like(dcol_s)
        drow_s[...] = jnp.zeros_like(drow_s)
        for g in range(2):
            bg = b_ref[:, g * LANE:(g + 1) * LANE]
            cg = c_ref[:, g * LANE:(g + 1) * LANE]
            bgb, cgb = bg.astype(BF16), cg.astype(BF16)
            gmat = _dot(cgb, bgb, _NT)
            d_g = jnp.zeros((L, L), F32)
            d_b = jnp.zeros((L, LANE), F32)
            d_c = jnp.zeros((L, LANE), F32)
            for jj in range(4):
                j = 4 * g + jj
                sl = slice(j * LANE, (j + 1) * LANE)
                xp = xs_ref[:, sl]
                dtp = dt_x[:, sl]
                x_dt = xp * dtp
                xb = x_dt.astype(BF16)
                dyp = dy_ref[:, sl]
                dd_s[:, sl] += _colsum(dyp * xp)
                d_xdt = jnp.zeros((L, LANE), F32)
                for e in range(2):
                    h = 2 * j + e
                    lm = _decay_matrix(csx_s, cst_s, h, tril)
                    m = gmat * lm
                    dye = jnp.where(low if e == 0 else jnp.logical_not(low), dyp, 0.0).astype(BF16)
                    d_m = jnp.where(tril, _dot(dye, xb, _NT), 0.0)
                    d_xdt = d_xdt + _dot(m.astype(BF16), dye, _TN)
                    d_g = d_g + d_m * lm
                    w = d_m * m
                    dcol_s[...] += jnp.where(col == h, jnp.sum(w, axis=1, keepdims=True), 0.0)
                    drow_s[...] += jnp.where(row == h, jnp.sum(w, axis=0, keepdims=True), 0.0)
                stp = st_ref[0, j]
                stb = stp.astype(BF16)
                dstn = dst_s[j]
                dstb = dstn.astype(BF16)
                e_p = ex_s[:, sl]
                f_p = f_x[:, sl]
                t_p = t_x[:, sl]
                z = _dot(cgb, stb)
                d_z = (e_p * dyp).astype(BF16)
                d_c = d_c + _dot(d_z, stb, _NT)
                d_xf = _dot(bgb, dstb)
                d_b = d_b + _dot((x_dt * f_p).astype(BF16), dstb, _NT)
                d_xdt = d_xdt + f_p * d_xf
                d_f = x_dt * d_xf * f_p
                dcsx_s[:, sl] = dyp * e_p * z - d_f
                dlast_s[:, sl] = _colsum(d_f) + _colsum(dstn * stp) * t_p
                dst_s[j] = _dot(cgb, d_z, _TN) + t_p * dstn
                dxbc_ref[:, sl] = dx_ref[:, sl] * dyp + d_xdt * dtp
                ddtx_s[:, sl] = d_xdt * xp
            d_gb = d_g.astype(BF16)
            dxbc_ref[:, 1024 + g * LANE:1024 + (g + 1) * LANE] = d_b + _dot(d_gb, cgb, _TN)
            dxbc_ref[:, 1280 + g * LANE:1280 + (g + 1) * LANE] = d_c + _dot(d_gb, bgb)

        expt = expt_ref[...]
        dlast = _sel_dot(jnp.broadcast_to(dlast_s[...], (8, 1024)), expt, 3)
        d_cs = dcol_s[...] - drow_s[...].T + _sel_dot(dcsx_s[...], expt, 3)
        rown = lax.broadcasted_iota(jnp.int32, (L, LANE), 0)
        d_cs = d_cs + jnp.where(rown == L - 1, jnp.sum(dlast, axis=0, keepdims=True) * 0.125, 0.0)
        d_da = _sel_dot(d_cs, tri_ref[...], 3, _TN, sel_left=True)
        d_dt = d_da * a + _sel_dot(ddtx_s[...], expt, 3)
        dalog_ref[...] += _colsum(d_da * dt) * a
        d_raw = d_dt * _sigmoid(dt_ref[...] + bias_ref[...])
        ddt_ref[...] = d_raw.astype(ddt_ref.dtype)
        dbias_ref[...] += _colsum(d_raw)
        dd8 = _sel_dot(jnp.broadcast_to(dd_s[...], (8, 1024)), expt, 3)
        dd_ref[...] = jnp.sum(dd8, axis=0, keepdims=True) * 0.125

    const = lambda shape: pl.BlockSpec(shape, lambda c: tuple(0 for _ in shape))
    rev = lambda cb: (lambda c: (nc - 1 - c, cb))
    return pl.pallas_call(
        body, name=name, grid=(nc,),
        in_specs=[pl.BlockSpec((L, 1024), rev(0)), pl.BlockSpec((L, 256), rev(4)), pl.BlockSpec((L, 256), rev(5)),
                  pl.BlockSpec((L, LANE), rev(dtcb)), const((1, LANE)), const((1, LANE)), const((1, 1024)), const((L, L)),
                  const((LANE, 1024)), const((LANE, 2048)), const((1024, LANE)),
                  pl.BlockSpec((1, 8, LANE, LANE), lambda c: (nc - 1 - c, 0, 0, 0)), pl.BlockSpec((L, 1024), rev(0))],
        out_specs=[pl.BlockSpec((L, SSD_XBC), rev(0)), pl.BlockSpec((L, LANE), rev(0)), const((1, LANE)), const((1, LANE)),
                   const((1, LANE))],
        out_shape=[jax.ShapeDtypeStruct((s_dim, SSD_XBC), F32), jax.ShapeDtypeStruct((s_dim, LANE), BF16),
                   jax.ShapeDtypeStruct((1, LANE), F32), jax.ShapeDtypeStruct((1, LANE), F32),
                   jax.ShapeDtypeStruct((1, LANE), F32)],
        scratch_shapes=[pltpu.VMEM((8, LANE, LANE), F32), pltpu.VMEM((L, LANE), F32), pltpu.VMEM((LANE, L), F32),
                        pltpu.VMEM((L, 1024), F32), pltpu.VMEM((L, 2048), F32), pltpu.VMEM((L, 1024), F32),
                        pltpu.VMEM((L, 1024), F32), pltpu.VMEM((L, LANE), F32), pltpu.VMEM((LANE, L), F32),
                        pltpu.VMEM((1, 1024), F32), pltpu.VMEM((1, 1024), F32)],
        compiler_params=pltpu.CompilerParams(dimension_semantics=("arbitrary",)),
    )(xbca, xbca, xbca, dtr, bias, alog, d_x, tri, expand, expand128, expand_t, states, dy)


def _swap_halves(u):
    width = u.shape[1]
    lane = lax.broadcasted_iota(jnp.int32, u.shape, 1)
    return jnp.where(lane % MLA_ROPE < MLA_ROPE // 2, pltpu.roll(u, width - MLA_ROPE // 2, 1), pltpu.roll(u, MLA_ROPE // 2, 1))


def _rope_fwd_fn(u, cos, sin):
    return u * cos + _swap_halves(u) * sin


def _rope_bwd_fn(d, cos, sin):
    return d * cos + _swap_halves(d * sin)


def _spread4(v):
    return v + pltpu.roll(v, 32, 1) + pltpu.roll(v, 64, 1) + pltpu.roll(v, 96, 1)


def _att_masks(tq):
    lane = lax.broadcasted_iota(jnp.int32, (tq, LANE), 1)
    return lane // MLA_NOPE, lane // MLA_ROPE


def _att_tile(i, tq):
    klen = (i + 1) * tq
    qpos = i * tq + lax.broadcasted_iota(jnp.int32, (tq, klen), 0)
    kpos = lax.broadcasted_iota(jnp.int32, (tq, klen), 1)
    return slice(i * tq, (i + 1) * tq), klen, qpos >= kpos


def _att_qcat(qn_t, qr_t, par, e, half_id, grp_id):
    return jnp.concatenate([jnp.where(half_id == par, qn_t * ATT_SCALE, 0.0), jnp.where(grp_id == e, qr_t * ATT_SCALE, 0.0)],
                           axis=1).astype(BF16)


def _att_exp(qcat, kcat, causal):
    s = jnp.where(causal, _dot(qcat, kcat, _NT), -jnp.inf)
    e = jnp.exp(s - jnp.max(s, axis=1, keepdims=True))
    return e, 1.0 / jnp.sum(e, axis=1, keepdims=True)


def _att_specs(s_dim):
    col = lambda f: pl.BlockSpec((s_dim, LANE), lambda j: (0, f(j)))
    return [col(lambda j: j), col(lambda j: j // 2), col(lambda j: j), col(lambda j: 0), col(lambda j: 8 + j)]


def _att_fwd(q, qr, kv, krt, name="att_fwd"):
    s_dim = q.shape[0]
    tq = min(ATT_TQ, s_dim)

    def body(qn_ref, qr_ref, kn_ref, krt_ref, v_ref, o_ref, kcat_s, vb_s):
        e0 = 2 * (pl.program_id(0) % 2)
        half_id, grp_id = _att_masks(tq)
        kcat_s[...] = jnp.concatenate([kn_ref[...], krt_ref[...]], axis=1).astype(BF16)
        vb_s[...] = v_ref[...].astype(BF16)
        for i in range(s_dim // tq):
            rows, klen, causal = _att_tile(i, tq)
            qn_t, qr_t = qn_ref[rows, :], qr_ref[rows, :]
            outs = []
            for par in range(2):
                qcat = _att_qcat(qn_t, qr_t, par, e0 + par, half_id, grp_id)
                e, inv_l = _att_exp(qcat, kcat_s[0:klen, :], causal)
                outs.append(_dot(e.astype(BF16), vb_s[0:klen, :]) * inv_l)
            o_ref[rows, :] = jnp.where(half_id == 0, outs[0], outs[1])

    return pl.pallas_call(
        body, name=name, grid=(MLA_HEADS // 2,), in_specs=_att_specs(s_dim),
        out_specs=pl.BlockSpec((s_dim, LANE), lambda j: (0, j)), out_shape=jax.ShapeDtypeStruct((s_dim, 1024), F32),
        scratch_shapes=[pltpu.VMEM((s_dim, 2 * LANE), BF16), pltpu.VMEM((s_dim, LANE), BF16)],
        compiler_params=pltpu.CompilerParams(dimension_semantics=("parallel",)),
    )(q, qr, kv, krt, kv)


def _att_bwd(q, qr, kv, krt, o, do, name="att_bwd"):
    s_dim = q.shape[0]
    tq = min(ATT_TQ, s_dim)

    def body(qn_ref, qr_ref, kn_ref, krt_ref, v_ref, o_ref, do_ref, dqn_ref, dqr_ref, dkn_ref, dv_ref, dkrt_ref,
             kcat_s, vb_s):
        e0 = 2 * (pl.program_id(0) % 2)
        half_id, grp_id = _att_masks(tq)
        kcat_s[...] = jnp.concatenate([kn_ref[...], krt_ref[...]], axis=1).astype(BF16)
        vb_s[...] = v_ref[...].astype(BF16)
        dkn_ref[...] = jnp.zeros_like(dkn_ref)
        dv_ref[...] = jnp.zeros_like(dv_ref)
        dkrt_ref[...] = jnp.zeros_like(dkrt_ref)
        for i in range(s_dim // tq):
            rows, klen, causal = _att_tile(i, tq)
            qn_t, qr_t, o_t, do_t = qn_ref[rows, :], qr_ref[rows, :], o_ref[rows, :], do_ref[rows, :]
            dqn = jnp.zeros((tq, LANE), F32)
            dqr = jnp.zeros((tq, LANE), F32)
            for par in range(2):
                qcat = _att_qcat(qn_t, qr_t, par, e0 + par, half_id, grp_id)
                e, inv_l = _att_exp(qcat, kcat_s[0:klen, :], causal)
                p = e * inv_l
                dom = jnp.where(half_id == par, do_t, 0.0)
                domb = dom.astype(BF16)
                d_p = _dot(domb, vb_s[0:klen, :], _NT)
                d_row = jnp.sum(dom * o_t, axis=1, keepdims=True)
                d_s = (p * (d_p - d_row)).astype(BF16)
                dqcat = _dot(d_s, kcat_s[0:klen, :]) * ATT_SCALE
                dqn = dqn + jnp.where(half_id == par, dqcat[:, :LANE], 0.0)
                dqr = dqr + jnp.where(grp_id == e0 + par, dqcat[:, LANE:], 0.0)
                dkcat = _dot(d_s, qcat, _TN)
                dkn_ref[0:klen, :] += dkcat[:, :LANE]
                dkrt_ref[0:klen, :] += dkcat[:, LANE:]
                dv_ref[0:klen, :] += _dot(p.astype(BF16), domb, _TN)
            dqn_ref[rows, :] = dqn.astype(dqn_ref.dtype)
            dqr_ref[rows, :] = dqr

    col = lambda f: pl.BlockSpec((s_dim, LANE), lambda j: (0, f(j)))
    return pl.pallas_call(
        body, name=name, grid=(MLA_HEADS // 2,), in_specs=_att_specs(s_dim) + [col(lambda j: j), col(lambda j: j)],
        out_specs=[col(lambda j: j), pl.BlockSpec((None, s_dim, LANE), lambda j: (j % 2, 0, j // 2)), col(lambda j: j),
                   col(lambda j: j), pl.BlockSpec((None, s_dim, LANE), lambda j: (j, 0, 0))],
        out_shape=[jax.ShapeDtypeStruct((s_dim, 1024), BF16), jax.ShapeDtypeStruct((2, s_dim, 512), F32),
                   jax.ShapeDtypeStruct((s_dim, 1024), F32), jax.ShapeDtypeStruct((s_dim, 1024), F32),
                   jax.ShapeDtypeStruct((MLA_HEADS // 2, s_dim, LANE), F32)],
        scratch_shapes=[pltpu.VMEM((s_dim, 2 * LANE), BF16), pltpu.VMEM((s_dim, LANE), BF16)],
        compiler_params=pltpu.CompilerParams(dimension_semantics=("parallel",)),
    )(q, qr, kv, krt, kv, o, do)


def _all_gather(x, name):
    rows, width = x.shape

    def body(x_ref, out_ref, send_sems, recv_sems, local_sem):
        x_i, y_i, c_i = lax.axis_index("x"), lax.axis_index("y"), lax.axis_index("c")
        me, sibling = (x_i, y_i, c_i), (x_i, y_i, 1 - c_i)
        chips = [(1 - x_i, y_i), (x_i, 1 - y_i), (1 - x_i, 1 - y_i)]

        def slot(px, py, pc):
            return out_ref.at[4 * px + 2 * py + pc]

        def copy(k, block, to, src=None):
            return pltpu.make_async_remote_copy(
                src_ref=slot(*block) if src is None else src, dst_ref=slot(*block), send_sem=send_sems.at[k],
                recv_sem=recv_sems.at[k], device_id=to, device_id_type=pl.DeviceIdType.MESH)

        mine = pltpu.make_async_copy(x_ref, slot(*me), local_sem)
        mine.start()
        first = [copy(0, me, sibling, src=x_ref)]
        first += [copy(1 + j, me, (*chip, c_i), src=x_ref) for j, chip in enumerate(chips)]
        for cp in first:
            cp.start()
        passed = [copy(4 + j, (*chip, c_i), sibling) for j, chip in enumerate(chips)]
        for j, chip in enumerate(chips):
            copy(1 + j, (*chip, c_i), me).wait_recv()
            passed[j].start()
        copy(0, sibling, me).wait_recv()
        for j, chip in enumerate(chips):
            copy(4 + j, (*chip, 1 - c_i), me).wait_recv()
        for cp in first + passed:
            cp.wait_send()
        mine.wait()

    return pl.pallas_call(
        body, name=name, out_shape=jax.ShapeDtypeStruct((N_DEV, rows, width), x.dtype),
        in_specs=[pl.BlockSpec(memory_space=pl.ANY)], out_specs=pl.BlockSpec(memory_space=pl.ANY),
        scratch_shapes=[pltpu.SemaphoreType.DMA((7,)), pltpu.SemaphoreType.DMA((7,)), pltpu.SemaphoreType.DMA],
    )(x)


def _gather_many(shards, name):
    n_arr = len(shards)

    def body(*refs):
        x_refs, out_refs = refs[:n_arr], refs[n_arr:2 * n_arr]
        send_sems, recv_sems, local_sems = refs[2 * n_arr:]
        x_i, y_i, c_i = lax.axis_index("x"), lax.axis_index("y"), lax.axis_index("c")
        me, sibling = (x_i, y_i, c_i), (x_i, y_i, 1 - c_i)
        chips = [(1 - x_i, y_i), (x_i, 1 - y_i), (1 - x_i, 1 - y_i)]

        def copy(a, k, block, to, src=None):
            slot = out_refs[a].at[4 * block[0] + 2 * block[1] + block[2]]
            return pltpu.make_async_remote_copy(
                src_ref=slot if src is None else src, dst_ref=slot, send_sem=send_sems.at[a, k],
                recv_sem=recv_sems.at[a, k], device_id=to, device_id_type=pl.DeviceIdType.MESH)

        mine, first, passed = [], [], []
        for a in range(n_arr):
            mine.append(pltpu.make_async_copy(x_refs[a], out_refs[a].at[4 * x_i + 2 * y_i + c_i], local_sems.at[a]))
            mine[a].start()
            first.append([copy(a, 0, me, sibling, src=x_refs[a])]
                         + [copy(a, 1 + j, me, (*chip, c_i), src=x_refs[a]) for j, chip in enumerate(chips)])
            for cp in first[a]:
                cp.start()
            passed.append([copy(a, 4 + j, (*chip, c_i), sibling) for j, chip in enumerate(chips)])
        for j, chip in enumerate(chips):
            for a in range(n_arr):
                copy(a, 1 + j, (*chip, c_i), me).wait_recv()
                passed[a][j].start()
        for a in range(n_arr):
            copy(a, 0, sibling, me).wait_recv()
            for j, chip in enumerate(chips):
                copy(a, 4 + j, (*chip, 1 - c_i), me).wait_recv()
        for a in range(n_arr):
            for cp in first[a] + passed[a]:
                cp.wait_send()
            mine[a].wait()

    any_spec = pl.BlockSpec(memory_space=pl.ANY)
    return pl.pallas_call(
        body, name=name, out_shape=[jax.ShapeDtypeStruct((N_DEV,) + x.shape, x.dtype) for x in shards],
        in_specs=[any_spec] * n_arr, out_specs=[any_spec] * n_arr,
        scratch_shapes=[pltpu.SemaphoreType.DMA((n_arr, 7)), pltpu.SemaphoreType.DMA((n_arr, 7)),
                        pltpu.SemaphoreType.DMA((n_arr,))],
    )(*shards)


_HBM = pl.BlockSpec(memory_space=pltpu.HBM)
_SEM = pl.BlockSpec(memory_space=pltpu.SEMAPHORE)


def _plan_copies(plan, src_refs, land_refs, send_sems, recv_sems):
    copies = []
    for s_ref, l_ref in zip(src_refs, land_refs):
        for src, dst, peer in plan(s_ref, l_ref):
            k = len(copies)
            copies.append(pltpu.make_async_remote_copy(
                src_ref=src, dst_ref=dst, send_sem=send_sems.at[k], recv_sem=recv_sems.at[k], device_id=peer,
                device_id_type=pl.DeviceIdType.MESH))
    return copies


def _split_start(srcs, lands, plan, n_copy, name, after=None):
    n = len(srcs)
    n_in = 2 * n + (after is not None)

    def body(*refs):
        for cp in _plan_copies(plan, refs[:n], refs[n:2 * n], refs[n_in], refs[n_in + 1]):
            cp.start()
        refs[-1][...] = jnp.zeros_like(refs[-1])

    sems = pltpu.SemaphoreType.DMA((n * n_copy,))
    res = pl.pallas_call(
        body, name=name,
        out_shape=(sems, sems, *[pltpu.HBM(a.shape, a.dtype) for a in list(srcs) + list(lands)],
                   jax.ShapeDtypeStruct((8, LANE), F32)),
        in_specs=[_HBM] * (2 * n) + [pl.BlockSpec(memory_space=pl.ANY)] * (after is not None),
        out_specs=(_SEM, _SEM, *[_HBM] * (2 * n), pl.BlockSpec(memory_space=pltpu.VMEM)),
        input_output_aliases={i: 2 + i for i in range(2 * n)},
        compiler_params=pltpu.CompilerParams(has_side_effects=pltpu.SideEffectType.DATAFLOW_SIDE_EFFECTING),
    )(*[pltpu.with_memory_space_constraint(a, pltpu.HBM) for a in list(srcs) + list(lands)],
      *([after] if after is not None else []))
    return res[0], res[1], list(res[2:2 + n]), list(res[2 + n:2 + 2 * n]), res[-1]


def _split_wait(send_sems, recv_sems, srcs, lands, after, plan, name):
    n = len(srcs)

    def body(*refs):
        copies = _plan_copies(plan, refs[:n], refs[n:2 * n], refs[2 * n], refs[2 * n + 1])
        for cp in copies:
            cp.wait_send()
        for cp in copies:
            cp.wait_recv()

    res = pl.pallas_call(
        body, name=name, out_shape=tuple(pltpu.HBM(a.shape, a.dtype) for a in list(srcs) + list(lands)),
        in_specs=[_HBM] * (2 * n) + [_SEM, _SEM, pl.BlockSpec(memory_space=pl.ANY)], out_specs=tuple([_HBM] * (2 * n)),
        input_output_aliases={i: i for i in range(2 * n)},
        compiler_params=pltpu.CompilerParams(has_side_effects=pltpu.SideEffectType.DATAFLOW_SIDE_EFFECTING),
    )(*srcs, *lands, send_sems, recv_sems, after)
    return list(res[:n]), list(res[n:])


def _plan_broadcast(src, land):
    x_i, y_i, c_i = lax.axis_index("x"), lax.axis_index("y"), lax.axis_index("c")
    me = 4 * x_i + 2 * y_i + c_i
    return [(src, land.at[me], (x_i ^ (k >> 2), y_i ^ ((k >> 1) & 1), c_i ^ (k & 1))) for k in range(1, N_DEV)]


def _plan_scatter(src, land):
    x_i, y_i, c_i = lax.axis_index("x"), lax.axis_index("y"), lax.axis_index("c")
    me = 4 * x_i + 2 * y_i + c_i
    plan = []
    for k in range(1, N_DEV):
        px, py, pc = x_i ^ (k >> 2), y_i ^ ((k >> 1) & 1), c_i ^ (k & 1)
        plan.append((src.at[4 * px + 2 * py + pc], land.at[me], (px, py, pc)))
    return plan


def _adam_math(g, w, m, v):
    m_new = ADAM_B1 * m + (1.0 - ADAM_B1) * g
    v_new = ADAM_B2 * v + (1.0 - ADAM_B2) * (g * g)
    m_hat = m_new / (1.0 - ADAM_B1 ** ADAM_STEP)
    v_hat = v_new / (1.0 - ADAM_B2 ** ADAM_STEP)
    return -ADAM_LR * (m_hat / (jnp.sqrt(v_hat) + ADAM_EPS) + ADAM_WD * w), m_new, v_new


def _adam(slots, w, m, v, name, own=None, own_idx=None):
    n_slot, rows, cols = slots.shape
    tr = ROW_TILE if rows % ROW_TILE == 0 else rows
    has_own = own is not None

    def body(*refs):
        if has_own:
            idx_ref, own_ref, refs = refs[0], refs[1], refs[2:]
        s_ref, w_ref, m_ref, v_ref, g_ref, d_ref, mo_ref, vo_ref = refs
        g = own_ref[...].astype(F32) if has_own else s_ref[0].astype(F32)
        for k in range(0 if has_own else 1, n_slot):
            part = s_ref[k].astype(F32)
            g = g + (jnp.where(idx_ref[0] == k, 0.0, part) if has_own else part)
        g_ref[...] = g
        d_ref[...], mo_ref[...], vo_ref[...] = _adam_math(g, w_ref[...], m_ref[...], v_ref[...])

    spec = pl.BlockSpec((tr, cols), lambda i, *_: (i, 0))
    in_specs = [pl.BlockSpec((n_slot, tr, cols), lambda i, *_: (0, i, 0)), spec, spec, spec]
    if has_own:
        in_specs = [pl.BlockSpec((None, tr, cols), lambda i, idx: (idx[0], i, 0))] + in_specs
    grid_spec = pltpu.PrefetchScalarGridSpec(num_scalar_prefetch=1 if has_own else 0, grid=(rows // tr,), in_specs=in_specs,
                                             out_specs=[spec] * 4)
    ins = ([own_idx, own] if has_own else []) + [slots, w, m, v]
    return pl.pallas_call(
        body, name=name, grid_spec=grid_spec, out_shape=[jax.ShapeDtypeStruct((rows, cols), F32)] * 4,
        compiler_params=pltpu.CompilerParams(dimension_semantics=("parallel",)),
    )(*ins)


PACK_ROWS, PACK_W = 24, 1536
REPL_W = (("ssd_conv_b", 1536), ("ssd_dt_bias", 16), ("ssd_A_log", 16), ("ssd_D", 16), ("ssd_norm_w", 1024),
          ("mla_q_norm_w", 384), ("mla_kv_norm_w", 256), ("mla_out_norm_w", 1024), ("ln_mix_g", 1024),
          ("ln_mix_b", 1024), ("ln_ffn_g", 1024), ("ln_ffn_b", 1024))
LOSS_ROW = 4 + len(REPL_W)


def _pack_small(conv_w_grad, grads, loss, name="pack_small"):
    def body(*refs):
        cw_ref, g_refs, loss_ref, o_ref = refs[0], refs[1:1 + len(REPL_W)], refs[1 + len(REPL_W)], refs[-1]
        o_ref[...] = jnp.zeros_like(o_ref)
        o_ref[0:4, :] = cw_ref[...]
        for i, g_ref in enumerate(g_refs):
            o_ref[4 + i:5 + i, 0:g_ref.shape[1]] = g_ref[...]
        o_ref[LOSS_ROW:LOSS_ROW + 1, 0:LANE] = loss_ref[...]

    return pl.pallas_call(body, name=name, out_shape=jax.ShapeDtypeStruct((PACK_ROWS, PACK_W), F32))(conv_w_grad, *grads, loss)


def _adam_small(gathered, wmv, name="adam_small"):
    def body(*refs):
        s_ref = refs[0]
        in_refs = refs[1:1 + 3 * len(REPL_W)]
        cw_ref, loss_ref = refs[1 + 3 * len(REPL_W)], refs[2 + 3 * len(REPL_W)]
        out_refs = refs[3 + 3 * len(REPL_W):-1]
        tot = refs[-1]
        acc = s_ref[0]
        for k in range(1, N_DEV):
            acc = acc + s_ref[k]
        tot[...] = acc
        cw_ref[...] = tot[0:4, :]
        loss_ref[...] = tot[LOSS_ROW:LOSS_ROW + 1, 0:LANE]
        for i, (_, width) in enumerate(REPL_W):
            g = tot[4 + i:5 + i, 0:width]
            w_ref, m_ref, v_ref = in_refs[3 * i:3 * i + 3]
            g_ref, d_ref, mo_ref, vo_ref = out_refs[4 * i:4 * i + 4]
            g_ref[...] = g
            d_ref[...], mo_ref[...], vo_ref[...] = _adam_math(g, w_ref[...], m_ref[...], v_ref[...])

    flat_in = [a for triple in wmv for a in triple]
    out_shape = [jax.ShapeDtypeStruct((4, PACK_W), F32), jax.ShapeDtypeStruct((1, LANE), F32)]
    for _, width in REPL_W:
        out_shape += [jax.ShapeDtypeStruct((1, width), F32)] * 4
    res = pl.pallas_call(body, name=name, out_shape=out_shape, scratch_shapes=[pltpu.VMEM((PACK_ROWS, PACK_W), F32)])(
        gathered, *flat_in)
    return res[0], res[1], [res[2 + 4 * i:6 + 4 * i] for i in range(len(REPL_W))]


def _cols_full(g):
    return jnp.transpose(g, (1, 0, 2)).reshape(g.shape[1], -1)


def _cols_split(full):
    k_dim, n_dim = full.shape
    return jnp.transpose(full.reshape(k_dim, N_DEV, n_dim // N_DEV), (1, 0, 2))


PROJ_BLOCK = {"z": (1024, 0), "dt": (LANE, 8), "q_c": (MLA_Q_RANK, 3), "xbc": (SSD_XBC, 1), "kv_c": (MLA_KV_RANK, 12),
              "k_rope": (LANE, 26)}


def _win_pad(wt):
    z = lambda n: jnp.zeros((n, wt.shape[1]), wt.dtype)
    return jnp.concatenate([wt[:1024], wt[2560:2576], z(112), wt[2576:2960], wt[1024:2560], wt[2960:3216], wt[3216:3248],
                            z(96)], axis=0)


def _win_unpad(wt):
    return jnp.concatenate([wt[:1024], wt[1536:3072], wt[1024:1040], wt[1152:1536], wt[3072:3328], wt[3328:3360]], axis=0)


def _heads_split_t(wt, a, b):
    w3 = wt.reshape(MLA_HEADS, a + b, wt.shape[1])
    return jnp.concatenate([w3[:, :a].reshape(-1, wt.shape[1]), w3[:, a:].reshape(-1, wt.shape[1])], axis=0)


def _heads_merge_t(wt, a, b):
    wa = wt[:MLA_HEADS * a].reshape(MLA_HEADS, a, wt.shape[1])
    wb = wt[MLA_HEADS * a:].reshape(MLA_HEADS, b, wt.shape[1])
    return jnp.concatenate([wa, wb], axis=1).reshape(-1, wt.shape[1])


def _heads_split(w, a, b):
    k_dim = w.shape[0]
    w3 = w.reshape(k_dim, MLA_HEADS, a + b)
    return jnp.concatenate([w3[:, :, :a].reshape(k_dim, -1), w3[:, :, a:].reshape(k_dim, -1)], axis=1)


def _heads_merge(w, a, b):
    k_dim = w.shape[0]
    wa = w[:, :MLA_HEADS * a].reshape(k_dim, MLA_HEADS, a)
    wb = w[:, MLA_HEADS * a:].reshape(k_dim, MLA_HEADS, b)
    return jnp.concatenate([wa, wb], axis=2).reshape(k_dim, -1)


def _pad_lanes(v, width=LANE):
    return jnp.concatenate([v, jnp.zeros((v.shape[0], width - v.shape[1]), v.dtype)], axis=1)


def _local_step(x, p, positions, tgt, W, P, comm=None):
    comm = comm or {}
    zero_tok = jnp.zeros((8, LANE), F32)
    s_dim = x.shape[0]
    inv_freq = 1.0 / (ROPE_BASE ** (jnp.arange(0, MLA_ROPE, 2, dtype=F32) / MLA_ROPE))
    ang = positions.astype(F32)[:, None] * inv_freq
    cos, sin = jnp.cos(ang), jnp.sin(ang)
    cos32 = jnp.concatenate([cos, cos], axis=1)
    sin32 = jnp.concatenate([-sin, sin], axis=1)
    cos512, sin512 = jnp.tile(cos32, (1, 16)), jnp.tile(sin32, (1, 16))
    cos128, sin128 = jnp.tile(cos32, (1, 4)), jnp.tile(sin32, (1, 4))
    bias_p, alog_p = _pad_lanes(P["ssd_dt_bias"]), _pad_lanes(P["ssd_A_log"])
    d_x = jnp.repeat(P["ssd_D"], SSD_HEAD_DIM, axis=1)

    xb, pb = x.astype(BF16), p.astype(BF16)
    if "token0" in comm:
        xb = (x + comm["token0"][0, 0]).astype(BF16)
    proj = _mm(xb, W["w_in"], tb=True, name="mm_in")
    z, qc, kvc, kr = [(proj,) + PROJ_BLOCK[n] for n in ("z", "q_c", "kv_c", "k_rope")]
    xbca = _conv_fwd(proj, PROJ_BLOCK["xbc"][1], P["ssd_conv_w"], P["ssd_conv_b"])
    y, states = _ssd_fwd(xbca, proj, PROJ_BLOCK["dt"][1], bias_p, alog_p, d_x)
    (yssd,) = _rowwise(_gate_rms, [y, z], [P["ssd_norm_w"]], [(1024, BF16)], name="ssd_gate_norm")
    (qn,) = _rowwise(_rms, [qc], [P["mla_q_norm_w"]], [(MLA_Q_RANK, BF16)], name="q_norm")
    (kvn,) = _rowwise(_rms, [kvc], [P["mla_kv_norm_w"]], [(MLA_KV_RANK, BF16)], name="kv_norm")
    q = _mm(qn, W["mla_w_q_b"], tb=True, name="mm_q")
    kv = _mm(kvn, W["mla_w_kv_b"], name="mm_kv")
    (qr,) = _rowwise(_rope_fwd_fn, [(q, 512, 2), cos512, sin512], [], [512], name="rope_q")
    (krt,) = _rowwise(lambda u, c, s: _spread4(_rope_fwd_fn(u, c, s)), [kr, cos128, sin128], [], [LANE], name="rope_k")
    att = _att_fwd(q, qr, kv, krt)
    (ymla,) = _rowwise(_rms, [att], [P["mla_out_norm_w"]], [(1024, BF16)], name="out_norm")
    ycat = jnp.concatenate([yssd, ymla], axis=1)
    if "late_weights" in comm:
        W = {**W, **comm["late_weights"](ycat)}
    mix = _mm(ycat, W["w_out"], name="mm_out")
    f_h1 = lambda xv, mv, g, b: _ln(ALPHA * xv + mv, g, b)
    h1, h1b = _rowwise(lambda *a: (f_h1(*a),) * 2, [x, mix], [P["ln_mix_g"], P["ln_mix_b"]], [1024, (1024, BF16)],
                       name="ln_mix")
    hg = _mm(h1b, W["w_ffn_gate"], tb=True, out_dtype=BF16, name="mm_gate")
    hu = _mm(h1b, W["w_ffn_up"], tb=True, out_dtype=BF16, name="mm_up")
    pg = _mm(h1b, W["w_ple_gate"], name="mm_ple_gate")
    pp = _mm(pb, W["w_ple_proj"], name="mm_ple")
    (act,) = _rowwise(lambda g, u: _silu(g.astype(F32)) * u.astype(F32), [hg, hu], [], [(D_FF, BF16)], name="swiglu")
    ffn = _mm(act, W["w_ffn_down"], name="mm_down")

    f_h2 = lambda hv, fv, pg, ppv, g, b: _ln(ALPHA * hv + fv + _sigmoid(pg) * ppv, g, b)

    def final_fn(hv, fv, pg, ppv, tv, g, b):
        h2, pull = jax.vjp(f_h2, hv, fv, pg, ppv, g, b)
        diff = h2 - tv
        loss = 0.5 * jnp.sum(jnp.mean(diff * diff, axis=-1, keepdims=True), axis=0, keepdims=True)
        d_h, d_f, d_pg, d_pp, d_g, d_b = pull(diff * (1.0 / D_MODEL))
        return d_h, d_f, d_pg, d_pp, d_g, d_b, jnp.broadcast_to(loss, (1, LANE))

    dh1_a, dffn, dpg, dpp, g_ffn_g, g_ffn_b, loss = _rowwise(
        final_fn, [h1, ffn, pg, pp, tgt], [P["ln_ffn_g"], P["ln_ffn_b"]], [1024] + [(1024, BF16)] * 3,
        [1024, 1024, LANE], name="final")

    G = {}
    dact = _mm(dffn, W["w_ffn_down"], tb=True, name="mm_down_dx")
    G["w_ffn_down"] = _mm(act, dffn, ta=True, out_dtype=GRAD_DT, name="mm_down_dw")

    def swiglu_bwd(g, u, d):
        g, u = g.astype(F32), u.astype(F32)
        sg = _sigmoid(g)
        return d * u * (sg * (1.0 + g * (1.0 - sg))), d * (g * sg)

    dg, du = _rowwise(swiglu_bwd, [hg, hu, dact], [], [(D_FF, BF16)] * 2, name="swiglu_bwd")
    dh1 = _mm(dg, W["w_ffn_gate"], add=dh1_a, name="mm_gate_dx")
    dh1 = _mm(du, W["w_ffn_up"], add=dh1, name="mm_up_dx")
    dh1 = _mm(dpg, W["w_ple_gate"], tb=True, add=dh1, name="mm_ple_gate_dx")
    G["w_ffn_gate"] = _mm(dg, h1b, ta=True, out_dtype=GRAD_DT, name="mm_gate_dw")
    G["w_ffn_up"] = _mm(du, h1b, ta=True, out_dtype=GRAD_DT, name="mm_up_dw")
    G["w_ple_gate"] = _mm(h1b, dpg, ta=True, out_dtype=GRAD_DT, name="mm_ple_gate_dw")
    G["w_ple_proj"] = _mm(pb, dpp, ta=True, out_dtype=GRAD_DT, name="mm_ple_dw")
    dx_a, dmix, g_mix_g, g_mix_b = _rowwise(
        lambda xv, mv, dv, g, b: _vjp_rows(f_h1)(xv, mv, g, b, dv), [x, mix, dh1], [P["ln_mix_g"], P["ln_mix_b"]],
        [1024, (1024, BF16)], [1024, 1024], name="ln_mix_bwd")
    dycat = _mm(dmix, W["w_out"], tb=True, name="mm_out_dx")
    G["w_out"] = _mm(ycat, dmix, ta=True, out_dtype=GRAD_DT, name="mm_out_dw")

    grads_done = comm.get("grads", lambda group, grads: zero_tok)
    tok1 = grads_done("late", G)
    datt, g_out_norm = _rowwise(lambda a, dv, w, t: _vjp_rows(_rms)(a, w, dv + jnp.min(t)), [att, (dycat, 1024, 1)],
                                [P["mla_out_norm_w"], tok1], [1024], [1024], name="out_norm_bwd")
    dqn_nope, dqr, dkn, dv, dkrt = _att_bwd(q, qr, kv, krt, att, datt)
    dkv = jnp.concatenate([dkn, dv], axis=1)
    (dq_rope,) = _rowwise(lambda d0, d1, c, s: _rope_bwd_fn(d0 + d1, c, s), [(dqr, 512, 0), (dqr, 512, 1), cos512, sin512],
                          [], [(512, BF16)], name="rope_q_bwd")

    def rope_k_bwd(*a):
        d = _spread4(functools.reduce(lambda u, w: u + w, a[:-2]))
        lane = lax.broadcasted_iota(jnp.int32, d.shape, 1)
        return _rope_bwd_fn(jnp.where(lane < MLA_ROPE, d, 0.0), a[-2], a[-1])

    (dkr,) = _rowwise(rope_k_bwd, [(dkrt, LANE, k) for k in range(MLA_HEADS // 2)] + [cos128, sin128], [], [(LANE, BF16)],
                      name="rope_k_bwd")
    dq = jnp.concatenate([dqn_nope, dq_rope], axis=1)
    dqn = _mm(dq, W["mla_w_q_b"], name="mm_q_dx")
    G["mla_w_q_b"] = _mm(dq, qn, ta=True, out_dtype=GRAD_DT, name="mm_q_dw")
    dkvn = _mm(dkv, W["mla_w_kv_b"], tb=True, name="mm_kv_dx")
    G["mla_w_kv_b"] = _mm(kvn, dkv, ta=True, out_dtype=GRAD_DT, name="mm_kv_dw")
    tok2 = grads_done("mla", G)
    dqc, g_q_norm = _rowwise(lambda a, dv, w, t: _vjp_rows(_rms)(a, w, dv + jnp.min(t)), [qc, dqn],
                             [P["mla_q_norm_w"], tok2], [(MLA_Q_RANK, BF16)], [MLA_Q_RANK], name="q_norm_bwd")
    dkvc, g_kv_norm = _rowwise(lambda a, dv, w: _vjp_rows(_rms)(a, w, dv), [kvc, dkvn], [P["mla_kv_norm_w"]],
                               [(MLA_KV_RANK, BF16)], [MLA_KV_RANK], name="kv_norm_bwd")

    dy, dz, g_ssd_norm = _rowwise(lambda yv, zv, dv, w, t: _vjp_rows(_gate_rms)(yv, zv, w, dv + jnp.min(t)),
                                  [y, z, (dycat, 1024, 0)], [P["ssd_norm_w"], tok1], [1024, (1024, BF16)], [1024],
                                  name="ssd_gate_norm_bwd")
    dxbca, ddtr, g_dt_bias, g_alog, g_d = _ssd_bwd(xbca, proj, PROJ_BLOCK["dt"][1], bias_p, alog_p, d_x, states, dy)
    da, g_conv_w, g_conv_b = _conv_bwd_pre(proj, PROJ_BLOCK["xbc"][1], P["ssd_conv_w"], P["ssd_conv_b"], dxbca)
    dxbc = _conv_bwd_in(da, P["ssd_conv_w"])

    dproj = jnp.concatenate([dz, ddtr, dqc, dxbc, dkvc, dkr], axis=1)
    G["w_in"] = _mm(dproj, xb, ta=True, out_dtype=GRAD_DT, name="mm_in_dw")
    grad_x = _mm(dproj, W["w_in"], add=dx_a, after=grads_done("in", G), name="mm_in_dx")

    small = {
        "ssd_conv_b": g_conv_b, "ssd_dt_bias": g_dt_bias, "ssd_A_log": g_alog, "ssd_D": g_d, "ssd_norm_w": g_ssd_norm,
        "mla_q_norm_w": g_q_norm, "mla_kv_norm_w": g_kv_norm, "mla_out_norm_w": g_out_norm, "ln_mix_g": g_mix_g,
        "ln_mix_b": g_mix_b, "ln_ffn_g": g_ffn_g, "ln_ffn_b": g_ffn_b,
    }
    return grad_x, G, _pack_small(g_conv_w, [small[n] for n, _ in REPL_W], loss)


def kernel(x, p, positions, w_in, ssd_conv_w, ssd_conv_b, ssd_dt_bias, ssd_A_log, ssd_D, ssd_norm_w, mla_q_norm_w, mla_w_q_b, mla_kv_norm_w, mla_w_kv_b, mla_out_norm_w, w_out, ln_mix_g, ln_mix_b, w_ffn_gate, w_ffn_up, w_ffn_down, w_ple_gate, w_ple_proj, ln_ffn_g, ln_ffn_b, loss_target, m_w_in, m_ssd_conv_w, m_ssd_conv_b, m_ssd_dt_bias, m_ssd_A_log, m_ssd_D, m_ssd_norm_w, m_mla_q_norm_w, m_mla_w_q_b, m_mla_kv_norm_w, m_mla_w_kv_b, m_mla_out_norm_w, m_w_out, m_ln_mix_g, m_ln_mix_b, m_w_ffn_gate, m_w_ffn_up, m_w_ffn_down, m_w_ple_gate, m_w_ple_proj, m_ln_ffn_g, m_ln_ffn_b, v_w_in, v_ssd_conv_w, v_ssd_conv_b, v_ssd_dt_bias, v_ssd_A_log, v_ssd_D, v_ssd_norm_w, v_mla_q_norm_w, v_mla_w_q_b, v_mla_kv_norm_w, v_mla_w_kv_b, v_mla_out_norm_w, v_w_out, v_ln_mix_g, v_ln_mix_b, v_w_ffn_gate, v_w_ffn_up, v_w_ffn_down, v_w_ple_gate, v_w_ple_proj, v_ln_ffn_g, v_ln_ffn_b):
    args = dict(locals())
    core = lax.axis_index("c")
    me = 4 * lax.axis_index("x") + 2 * lax.axis_index("y") + core

    conv_sh = ssd_conv_w[0]
    conv_hi = conv_sh.astype(BF16)
    conv_lo = (conv_sh - conv_hi.astype(F32)).astype(BF16)
    stored = lambda n, pre="": jnp.transpose(args[pre + n][0]) if n in TRANSPOSED else args[pre + n][0]
    shards = {n: stored(n).astype(BF16) for n in BIG}
    rows_full = lambda g: g.reshape(-1, g.shape[2])

    early = _gather_many([shards[n] for n in EARLY] + [jnp.concatenate([conv_hi, conv_lo], axis=0)], "gather_early")
    gw = dict(zip(EARLY, early[:-1]))
    conv_g = early[-1].astype(F32)
    W = {
        "w_in": _win_pad(rows_full(gw["w_in"])),
        "mla_w_q_b": _heads_split_t(rows_full(gw["mla_w_q_b"]), MLA_NOPE, MLA_ROPE),
        "mla_w_kv_b": _heads_split(_cols_full(gw["mla_w_kv_b"]), MLA_NOPE, MLA_V),
    }
    P = {n: args[n] for n, _ in REPL_W}
    P["ssd_conv_w"] = _cols_full(conv_g[:, :4] + conv_g[:, 4:])

    lands = [lax.dynamic_update_slice(lax.empty((N_DEV,) + shards[n].shape, BF16), shards[n][None], (me, 0, 0)) for n in LATE]
    late_sems = _split_start([shards[n] for n in LATE], lands, _plan_broadcast, N_DEV - 1, "gather_late_start",
                             after=early[0])

    def late_weights(after):
        _, got = _split_wait(*late_sems[:4], after, _plan_broadcast, "gather_late_wait")
        lw = dict(zip(LATE, got))
        return {"w_out": rows_full(lw["w_out"]), "w_ple_gate": rows_full(lw["w_ple_gate"]),
                "w_ple_proj": _cols_full(lw["w_ple_proj"]), "w_ffn_gate": rows_full(lw["w_ffn_gate"]),
                "w_ffn_up": rows_full(lw["w_ffn_up"]), "w_ffn_down": rows_full(lw["w_ffn_down"])}

    def to_blocks(n, g):
        if n == "w_in":
            g = _win_unpad(g)
        elif n == "mla_w_q_b":
            g = _heads_merge_t(g, MLA_NOPE, MLA_ROPE)
        elif n == "mla_w_kv_b":
            g = _heads_merge(g, MLA_NOPE, MLA_V)
        if n in ROW_SHARDED or n in TRANSPOSED:
            return g.reshape(N_DEV, -1, g.shape[1])
        return _cols_split(g)

    flight = {}

    def grads(group, G):
        gl = [to_blocks(n, G[n]) for n in GRAD_GROUPS[group]]
        flight[group] = _split_start(gl, [lax.empty(g.shape, g.dtype) for g in gl], _plan_scatter, N_DEV - 1,
                                     "grads_" + group + "_start")
        return flight[group][4]

    grad_x, G, packed = _local_step(x[0], p[0, 0], positions[0], loss_target[0], W, P,
                                    comm={"token0": late_sems[4], "late_weights": late_weights, "grads": grads})

    me_arr = me.astype(jnp.int32).reshape(1)
    big_out = {}

    def finish(group, after):
        mine, recv = _split_wait(*flight[group][:4], after, _plan_scatter, "grads_" + group + "_wait")
        for n, g, r in zip(GRAD_GROUPS[group], mine, recv):
            big_out[n] = _adam(r, stored(n), stored(n, "m_"), stored(n, "v_"), "adam_" + n, own=g, own_idx=me_arr)
        return big_out[GRAD_GROUPS[group][-1]][0]

    done = finish("late", grad_x)

    small_all = _all_gather(packed, "gather_small")
    conv_sum, loss_row, small_out = _adam_small(small_all, [(args[n], args["m_" + n], args["v_" + n]) for n, _ in REPL_W])
    finish("in", finish("mla", done))
    conv_grad = lax.dynamic_slice_in_dim(conv_sum, me * 192, 192, axis=1)
    conv_out = _adam(conv_grad[None], conv_sh, m_ssd_conv_w[0], v_ssd_conv_w[0], "adam_conv")
    small_map = {n: small_out[i] for i, (n, _) in enumerate(REPL_W)}

    def outputs(idx):
        res = []
        for n in WEIGHT_ORDER:
            if n == "ssd_conv_w":
                res.append(conv_out[idx][None])
            elif n in big_out:
                res.append((jnp.transpose(big_out[n][idx]) if n in TRANSPOSED else big_out[n][idx])[None])
            else:
                res.append(small_map[n][idx])
        return res

    return (loss_row[0, 0], grad_x[None], *outputs(0), *outputs(1), *outputs(2), *outputs(3))
```

```python
import functools
import math

import numpy as np
import jax
import jax.numpy as jnp
from jax import lax
from jax.experimental import pallas as pl
from jax.experimental.pallas import tpu as pltpu

F32 = jnp.float32
BF16 = jnp.bfloat16
HI = lax.Precision.HIGHEST

N_DEV = 8
D_MODEL = 1024
PLE_DIM = 256
SSD_HEADS = 16
SSD_HEAD_DIM = 64
SSD_INNER = 1024
SSD_STATE = 128
SSD_XBC = 1536
SSD_CHUNK = 128
MLA_HEADS = 16
MLA_Q_RANK = 384
MLA_KV_RANK = 256
MLA_NOPE = 64
MLA_ROPE = 32
MLA_V = 64
ROPE_BASE = 10000.0
D_FF = 2816
IN_WIDTH = 3248
IN_PAD = 3456
ALPHA = 2.0 ** 0.25
EPS = 1e-6
LN_EPS = 1e-5
ATT_SCALE = 1.0 / math.sqrt(MLA_NOPE + MLA_ROPE)
ADAM_LR, ADAM_B1, ADAM_B2, ADAM_EPS, ADAM_WD, ADAM_STEP = 0.001, 0.9, 0.999, 1e-08, 0.01, 10

LANE = 128
MXU_DIM = 256
MM_TM, MM_TN, MM_TK = 1408, 1408, 2048
ROW_TILE = 256
ATT_TQ = 256

GRAD_DT = BF16

BIG = ("w_in", "mla_w_q_b", "mla_w_kv_b", "w_out", "w_ffn_gate", "w_ffn_up", "w_ffn_down", "w_ple_gate", "w_ple_proj")
EARLY = ("w_in", "mla_w_q_b", "mla_w_kv_b")
LATE = ("w_out", "w_ffn_gate", "w_ffn_up", "w_ffn_down", "w_ple_gate", "w_ple_proj")
GRAD_GROUPS = {"ffn": ("w_ffn_gate", "w_ffn_up", "w_ffn_down", "w_ple_gate", "w_ple_proj"),
               "mla": ("w_out", "mla_w_q_b", "mla_w_kv_b"), "in": ("w_in",)}
ROW_SHARDED = ("w_out", "w_ffn_down", "w_ple_gate")
TRANSPOSED = ("w_in", "mla_w_q_b", "w_ffn_gate", "w_ffn_up")
WEIGHT_ORDER = ("w_in", "ssd_conv_w", "ssd_conv_b", "ssd_dt_bias", "ssd_A_log", "ssd_D", "ssd_norm_w", "mla_q_norm_w",
                "mla_w_q_b", "mla_kv_norm_w", "mla_w_kv_b", "mla_out_norm_w", "w_out", "ln_mix_g", "ln_mix_b",
                "w_ffn_gate", "w_ffn_up", "w_ffn_down", "w_ple_gate", "w_ple_proj", "ln_ffn_g", "ln_ffn_b")


def _tile(dim, cap, prefer=None):
    cands = [t for t in range(LANE, min(cap, dim) + 1, LANE) if dim % t == 0]
    if not cands:
        return dim
    if prefer is None:
        return max(cands)
    fill = lambda t: t / (MXU_DIM * -(-t // MXU_DIM))
    good = min(0.9, max(fill(t) for t in cands))
    return min((t for t in cands if fill(t) >= good), key=lambda t: abs(t - prefer))


def _dot(a, b, dims=(((1,), (0,)), ((), ())), precision=None):
    return lax.dot_general(a, b, dims, preferred_element_type=F32, precision=precision)


_NT = (((1,), (1,)), ((), ()))
_TN = (((0,), (0,)), ((), ()))


def _mm(a, b, *, ta=False, tb=False, add=None, out_dtype=F32, after=None, name):
    k_dim, m_dim = a.shape if ta else a.shape[::-1]
    n_dim, kb = b.shape if tb else b.shape[::-1]
    assert k_dim == kb
    tm, tn, tk = _tile(m_dim, MM_TM), _tile(n_dim, MM_TN, prefer=1024), _tile(k_dim, MM_TK, prefer=MM_TK)
    nk = k_dim // tk
    dims = (((0 if ta else 1,), (1 if tb else 0,)), ((), ()))
    has_add = add is not None
    a_spec = pl.BlockSpec((tk, tm), lambda i, j, k: (k, i)) if ta else pl.BlockSpec((tm, tk), lambda i, j, k: (i, k))
    b_spec = pl.BlockSpec((tn, tk), lambda i, j, k: (j, k)) if tb else pl.BlockSpec((tk, tn), lambda i, j, k: (k, j))
    o_spec = pl.BlockSpec((tm, tn), lambda i, j, k: (i, j))

    def body(*refs):
        refs = [r for i, r in enumerate(refs) if i != 2 + has_add] if after is not None else refs
        if has_add:
            a_ref, b_ref, add_ref, o_ref = refs[:4]
        else:
            a_ref, b_ref, o_ref = refs[:3]
        part = _dot(a_ref[...].astype(BF16), b_ref[...].astype(BF16), dims)
        if nk == 1:
            o_ref[...] = ((part + add_ref[...]) if has_add else part).astype(o_ref.dtype)
            return
        acc = refs[-1]
        k = pl.program_id(2)

        @pl.when(k == 0)
        def _():
            acc[...] = (part + add_ref[...]) if has_add else part

        @pl.when(k > 0)
        def _():
            acc[...] += part

        @pl.when(k == nk - 1)
        def _():
            o_ref[...] = acc[...].astype(o_ref.dtype)

    ins = [a, b] + ([add] if has_add else []) + ([after] if after is not None else [])
    specs = [a_spec, b_spec] + ([o_spec] if has_add else []) + ([pl.BlockSpec(memory_space=pl.ANY)] if after is not None else [])
    return pl.pallas_call(
        body, name=name, grid=(m_dim // tm, n_dim // tn, nk), in_specs=specs, out_specs=o_spec,
        out_shape=jax.ShapeDtypeStruct((m_dim, n_dim), out_dtype),
        scratch_shapes=[pltpu.VMEM((tm, tn), F32)] if nk > 1 else [],
        compiler_params=pltpu.CompilerParams(dimension_semantics=("parallel", "parallel", "arbitrary")),
    )(*ins)


def _rowwise(fn, rows, consts, out_widths, acc_widths=(), *, name, tr=ROW_TILE):
    row_arrays, row_specs = [], []
    first_arr = rows[0][0] if isinstance(rows[0], tuple) else rows[0]
    s_dim = first_arr.shape[-2]
    tr = min(tr, s_dim)
    for r in rows:
        arr, width, cb = r if isinstance(r, tuple) else (r, r.shape[-1], 0)
        row_arrays.append(arr)
        if arr.ndim == 3:
            row_specs.append(pl.BlockSpec((None, tr, width), functools.partial(lambda i, k: (k, i, 0), k=cb)))
        else:
            row_specs.append(pl.BlockSpec((tr, width), functools.partial(lambda i, cb: (i, cb), cb=cb)))
    const_specs = [pl.BlockSpec(c.shape, lambda i: (0, 0)) for c in consts]
    nr, nc, no, na = len(rows), len(consts), len(out_widths), len(acc_widths)

    def body(*refs):
        ins = [r[...] for r in refs[:nr + nc]]
        res = fn(*ins)
        if not isinstance(res, (tuple, list)):
            res = (res,)
        out_refs = refs[nr + nc:nr + nc + no]
        acc_refs = refs[nr + nc + no:]
        for o_ref, val in zip(out_refs, res[:no]):
            o_ref[...] = val.astype(o_ref.dtype)
        first = pl.program_id(0) == 0
        for a_ref, val in zip(acc_refs, res[no:]):
            @pl.when(first)
            def _(a_ref=a_ref, val=val):
                a_ref[...] = val

            @pl.when(jnp.logical_not(first))
            def _(a_ref=a_ref, val=val):
                a_ref[...] += val

    outs = [w if isinstance(w, tuple) else (w, F32) for w in out_widths]
    out_shape = [jax.ShapeDtypeStruct((s_dim, w), dt) for w, dt in outs]
    out_shape += [jax.ShapeDtypeStruct((1, w), F32) for w in acc_widths]
    out_specs = [pl.BlockSpec((tr, w), lambda i: (i, 0)) for w, _ in outs]
    out_specs += [pl.BlockSpec((1, w), lambda i: (0, 0)) for w in acc_widths]
    res = pl.pallas_call(
        body, name=name, grid=(s_dim // tr,), in_specs=row_specs + const_specs, out_specs=out_specs, out_shape=out_shape,
        compiler_params=pltpu.CompilerParams(dimension_semantics=("arbitrary",)),
    )(*row_arrays, *consts)
    return res


def _colsum(v):
    return jnp.sum(v, axis=0, keepdims=True)


def _rms(u, g):
    return u * lax.rsqrt(jnp.mean(u * u, axis=-1, keepdims=True) + EPS) * g


def _ln(u, g, b):
    mu = jnp.mean(u, axis=-1, keepdims=True)
    d = u - mu
    var = jnp.mean(d * d, axis=-1, keepdims=True)
    return d * lax.rsqrt(var + LN_EPS) * g + b


def _sigmoid(v):
    return 1.0 / (1.0 + jnp.exp(-v))


def _silu(v):
    return v * _sigmoid(v)


def _softplus(v):
    y = jnp.exp(-jnp.abs(v))
    w = 1.0 + y
    log1p = jnp.where(w == 1.0, y, jnp.log(w) * y / jnp.where(w == 1.0, 1.0, w - 1.0))
    return jnp.maximum(v, 0.0) + log1p


def _gate_rms(y, z, w):
    return _rms(y * _silu(z), w)


def _vjp_rows(f):
    def fn(*args):
        prim, ct = args[:-1], args[-1]
        _, pull = jax.vjp(f, *prim)
        return pull(ct)
    return fn


def _conv_pre(cur, prev, w, b, first):
    row = lax.broadcasted_iota(jnp.int32, cur.shape, 0)
    acc = cur * w[3:4, :] + b
    for j in (1, 2, 3):
        tail = jnp.where(first, 0.0, pltpu.roll(prev, j, 0))
        acc = acc + jnp.where(row >= j, pltpu.roll(cur, j, 0), tail) * w[3 - j:4 - j, :]
    return acc


def _conv_fwd(u, ucb, w, b, name="conv_fwd"):
    s_dim, width = u.shape[0], w.shape[1]
    tr = min(ROW_TILE, s_dim)

    def body(cur_ref, prev_ref, w_ref, b_ref, o_ref):
        pre = _conv_pre(cur_ref[...], prev_ref[...], w_ref, b_ref[...], pl.program_id(0) == 0)
        o_ref[...] = _silu(pre)

    return pl.pallas_call(
        body, name=name, grid=(s_dim // tr,),
        in_specs=[pl.BlockSpec((tr, width), lambda i: (i, ucb)),
                  pl.BlockSpec((tr, width), lambda i: (jnp.maximum(i - 1, 0), ucb)),
                  pl.BlockSpec(w.shape, lambda i: (0, 0)), pl.BlockSpec(b.shape, lambda i: (0, 0))],
        out_specs=pl.BlockSpec((tr, width), lambda i: (i, 0)), out_shape=jax.ShapeDtypeStruct((s_dim, width), F32),
        compiler_params=pltpu.CompilerParams(dimension_semantics=("arbitrary",)),
    )(u, u, w, b)


def _conv_bwd_pre(u, ucb, w, b, dact, name="conv_bwd_pre"):
    s_dim, width = u.shape[0], w.shape[1]
    tr = min(ROW_TILE, s_dim)

    def body(cur_ref, prev_ref, w_ref, b_ref, d_ref, da_ref, dw_ref, db_ref):
        first = pl.program_id(0) == 0
        cur, prev = cur_ref[...], prev_ref[...]
        pre = _conv_pre(cur, prev, w_ref, b_ref[...], first)
        sg = _sigmoid(pre)
        da = d_ref[...] * (sg * (1.0 + pre * (1.0 - sg)))
        da_ref[...] = da
        row = lax.broadcasted_iota(jnp.int32, cur.shape, 0)

        @pl.when(first)
        def _():
            dw_ref[...] = jnp.zeros_like(dw_ref)
            db_ref[...] = jnp.zeros_like(db_ref)

        db_ref[...] += _colsum(da)
        dw_ref[3:4, :] += _colsum(da * cur)
        for j in (1, 2, 3):
            tail = jnp.where(first, 0.0, pltpu.roll(prev, j, 0))
            sh = jnp.where(row >= j, pltpu.roll(cur, j, 0), tail)
            dw_ref[3 - j:4 - j, :] += _colsum(da * sh)

    return pl.pallas_call(
        body, name=name, grid=(s_dim // tr,),
        in_specs=[pl.BlockSpec((tr, width), lambda i: (i, ucb)),
                  pl.BlockSpec((tr, width), lambda i: (jnp.maximum(i - 1, 0), ucb)),
                  pl.BlockSpec(w.shape, lambda i: (0, 0)), pl.BlockSpec(b.shape, lambda i: (0, 0)),
                  pl.BlockSpec((tr, width), lambda i: (i, 0))],
        out_specs=[pl.BlockSpec((tr, width), lambda i: (i, 0)), pl.BlockSpec(w.shape, lambda i: (0, 0)),
                   pl.BlockSpec(b.shape, lambda i: (0, 0))],
        out_shape=[jax.ShapeDtypeStruct((s_dim, width), F32), jax.ShapeDtypeStruct(w.shape, F32),
                   jax.ShapeDtypeStruct(b.shape, F32)],
        compiler_params=pltpu.CompilerParams(dimension_semantics=("arbitrary",)),
    )(u, u, w, b, dact)


def _conv_bwd_in(da, w, name="conv_bwd_in"):
    s_dim, width = da.shape
    tr = min(ROW_TILE, s_dim)
    n = s_dim // tr

    def body(cur_ref, nxt_ref, w_ref, o_ref):
        last = pl.program_id(0) == n - 1
        cur, nxt = cur_ref[...], nxt_ref[...]
        row = lax.broadcasted_iota(jnp.int32, cur.shape, 0)
        acc = cur * w_ref[3:4, :]
        for j in (1, 2, 3):
            head = jnp.where(last, 0.0, pltpu.roll(nxt, tr - j, 0))
            acc = acc + jnp.where(row < tr - j, pltpu.roll(cur, tr - j, 0), head) * w_ref[3 - j:4 - j, :]
        o_ref[...] = acc.astype(o_ref.dtype)

    return pl.pallas_call(
        body, name=name, grid=(n,),
        in_specs=[pl.BlockSpec((tr, width), lambda i: (i, 0)), pl.BlockSpec((tr, width), lambda i: (jnp.minimum(i + 1, n - 1), 0)),
                  pl.BlockSpec(w.shape, lambda i: (0, 0))],
        out_specs=pl.BlockSpec((tr, width), lambda i: (i, 0)), out_shape=jax.ShapeDtypeStruct((s_dim, width), BF16),
        compiler_params=pltpu.CompilerParams(dimension_semantics=("arbitrary",)),
    )(da, da, w)


def _sel_dot(a, sel, pieces, dims=(((1,), (0,)), ((), ())), sel_left=False):
    sel = sel.astype(BF16)
    acc, rest = None, a
    for _ in range(pieces):
        piece = rest.astype(BF16)
        rest = rest - piece.astype(F32)
        part = _dot(sel, piece, dims) if sel_left else _dot(piece, sel, dims)
        acc = part if acc is None else acc + part
    return acc


def _ssd_consts():
    L = SSD_CHUNK
    tri = np.tril(np.ones((L, L), np.float32))
    expand = np.zeros((LANE, SSD_INNER), np.float32)
    expand128 = np.zeros((LANE, SSD_HEADS * LANE), np.float32)
    for h in range(SSD_HEADS):
        expand[h, h * SSD_HEAD_DIM:(h + 1) * SSD_HEAD_DIM] = 1.0
        expand128[h, h * LANE:(h + 1) * LANE] = 1.0
    return jnp.asarray(tri), jnp.asarray(expand), jnp.asarray(expand128), jnp.asarray(expand.T.copy())


def _ssd_prep(dt_ref, bias_ref, alog_ref, tri_ref, exp_ref, exp128_ref, cs_s, cst_s, ex_s, csx_s):
    L = SSD_CHUNK
    dt = _softplus(dt_ref[...] + bias_ref[...])
    a = -jnp.exp(alog_ref[...])
    cs = _sel_dot(dt * a, tri_ref[...], 3, sel_left=True)
    cs_s[...] = cs
    cst_s[...] = cs.T
    last = cs_s[L - 1:L, :]
    expand = exp_ref[...]
    ex_s[...] = _sel_dot(jnp.exp(cs), expand, 2)
    f_x = _sel_dot(jnp.exp(last - cs), expand, 2)
    dt_x = _sel_dot(dt, expand, 2)
    csx_s[...] = _sel_dot(cs, exp128_ref[...], 3)
    t_x = ex_s[L - 1:L, :]
    return dt, a, dt_x, f_x, t_x


def _decay_matrix(csx_s, cst_s, h, tril):
    seg = csx_s[:, h * LANE:(h + 1) * LANE] - cst_s[h:h + 1, :]
    return jnp.exp(jnp.where(tril, seg, -jnp.inf))


def _ssd_fwd(xbca, dtr, dtcb, bias, alog, d_x, name="ssd_fwd"):
    s_dim = xbca.shape[0]
    L = SSD_CHUNK
    nc = s_dim // L
    tri, expand, expand128, _ = _ssd_consts()

    def body(xs_ref, b_ref, c_ref, dt_ref, bias_ref, alog_ref, dx_ref, tri_ref, exp_ref, exp128_ref,
             y_ref, st_ref, st_s, cs_s, cst_s, ex_s, csx_s):
        @pl.when(pl.program_id(0) == 0)
        def _():
            st_s[...] = jnp.zeros_like(st_s)

        dt, a, dt_x, f_x, t_x = _ssd_prep(dt_ref, bias_ref, alog_ref, tri_ref, exp_ref, exp128_ref, cs_s, cst_s, ex_s, csx_s)
        st_ref[0] = st_s[...]
        row = lax.broadcasted_iota(jnp.int32, (L, L), 0)
        col = lax.broadcasted_iota(jnp.int32, (L, L), 1)
        tril = row >= col
        low = col < SSD_HEAD_DIM
        for g in range(2):
            bg = b_ref[:, g * LANE:(g + 1) * LANE]
            cg = c_ref[:, g * LANE:(g + 1) * LANE].astype(BF16)
            gmat = _dot(cg, bg.astype(BF16), _NT)
            bgt = bg.T.astype(BF16)
            for jj in range(4):
                j = 4 * g + jj
                sl = slice(j * LANE, (j + 1) * LANE)
                xp = xs_ref[:, sl]
                x_dt = xp * dt_x[:, sl]
                xb = x_dt.astype(BF16)
                yd = []
                for e in range(2):
                    lm = _decay_matrix(csx_s, cst_s, 2 * j + e, tril)
                    yd.append(_dot((gmat * lm).astype(BF16), xb))
                stp = st_s[j]
                z = _dot(cg, stp.astype(BF16))
                y_ref[:, sl] = jnp.where(low, yd[0], yd[1]) + ex_s[:, sl] * z + dx_ref[:, sl] * xp
                xf = (x_dt * f_x[:, sl]).astype(BF16)
                st_s[j] = t_x[:, sl] * stp + _dot(bgt, xf)

    const = lambda shape: pl.BlockSpec(shape, lambda c: tuple(0 for _ in shape))
    return pl.pallas_call(
        body, name=name, grid=(nc,),
        in_specs=[pl.BlockSpec((L, 1024), lambda c: (c, 0)), pl.BlockSpec((L, 256), lambda c: (c, 4)),
                  pl.BlockSpec((L, 256), lambda c: (c, 5)), pl.BlockSpec((L, LANE), lambda c: (c, dtcb)),
                  const((1, LANE)), const((1, LANE)), const((1, 1024)), const((L, L)), const((LANE, 1024)),
                  const((LANE, 2048))],
        out_specs=[pl.BlockSpec((L, 1024), lambda c: (c, 0)), pl.BlockSpec((1, 8, LANE, LANE), lambda c: (c, 0, 0, 0))],
        out_shape=[jax.ShapeDtypeStruct((s_dim, 1024), F32), jax.ShapeDtypeStruct((nc, 8, LANE, LANE), F32)],
        scratch_shapes=[pltpu.VMEM((8, LANE, LANE), F32), pltpu.VMEM((L, LANE), F32), pltpu.VMEM((LANE, L), F32),
                        pltpu.VMEM((L, 1024), F32), pltpu.VMEM((L, 2048), F32)],
        compiler_params=pltpu.CompilerParams(dimension_semantics=("arbitrary",)),
    )(xbca, xbca, xbca, dtr, bias, alog, d_x, tri, expand, expand128)


def _ssd_bwd(xbca, dtr, dtcb, bias, alog, d_x, states, dy, name="ssd_bwd"):
    s_dim = xbca.shape[0]
    L = SSD_CHUNK
    nc = s_dim // L
    tri, expand, expand128, expand_t = _ssd_consts()

    def body(xs_ref, b_ref, c_ref, dt_ref, bias_ref, alog_ref, dx_ref, tri_ref, exp_ref, exp128_ref, expt_ref,
             st_ref, dy_ref, dxbc_ref, ddt_ref, dbias_ref, dalog_ref, dd_ref,
             dst_s, cs_s, cst_s, ex_s, csx_s, dcsx_s, ddtx_s, dcol_s, drow_s, dlast_s, dd_s):
        @pl.when(pl.program_id(0) == 0)
        def _():
            dst_s[...] = jnp.zeros_like(dst_s)
            dbias_ref[...] = jnp.zeros_like(dbias_ref)
            dalog_ref[...] = jnp.zeros_like(dalog_ref)
            dd_s[...] = jnp.zeros_like(dd_s)

        dt, a, dt_x, f_x, t_x = _ssd_prep(dt_ref, bias_ref, alog_ref, tri_ref, exp_ref, exp128_ref, cs_s, cst_s, ex_s, csx_s)
        row = lax.broadcasted_iota(jnp.int32, (L, L), 0)
        col = lax.broadcasted_iota(jnp.int32, (L, L), 1)
        tril = row >= col
        low = col < SSD_HEAD_DIM
        dcol_s[...] = jnp.zeros_like(dcol_s)
        drow_s[...] = jnp.zeros_like(drow_s)
        for g in range(2):
            bg = b_ref[:, g * LANE:(g + 1) * LANE]
            cg = c_ref[:, g * LANE:(g + 1) * LANE]
            bgb, cgb = bg.astype(BF16), cg.astype(BF16)
            gmat = _dot(cgb, bgb, _NT)
            d_g = jnp.zeros((L, L), F32)
            d_b = jnp.zeros((L, LANE), F32)
            d_c = jnp.zeros((L, LANE), F32)
            for jj in range(4):
                j = 4 * g + jj
                sl = slice(j * LANE, (j + 1) * LANE)
                xp = xs_ref[:, sl]
                dtp = dt_x[:, sl]
                x_dt = xp * dtp
                xb = x_dt.astype(BF16)
                dyp = dy_ref[:, sl]
                dd_s[:, sl] += _colsum(dyp * xp)
                d_xdt = jnp.zeros((L, LANE), F32)
                for e in range(2):
                    h = 2 * j + e
                    lm = _decay_matrix(csx_s, cst_s, h, tril)
                    m = gmat * lm
                    dye = jnp.where(low if e == 0 else jnp.logical_not(low), dyp, 0.0).astype(BF16)
                    d_m = jnp.where(tril, _dot(dye, xb, _NT), 0.0)
                    d_xdt = d_xdt + _dot(m.astype(BF16), dye, _TN)
                    d_g = d_g + d_m * lm
                    w = d_m * m
                    dcol_s[...] += jnp.where(col == h, jnp.sum(w, axis=1, keepdims=True), 0.0)
                    drow_s[...] += jnp.where(row == h, jnp.sum(w, axis=0, keepdims=True), 0.0)
                stp = st_ref[0, j]
                stb = stp.astype(BF16)
                dstn = dst_s[j]
                dstb = dstn.astype(BF16)
                e_p = ex_s[:, sl]
                f_p = f_x[:, sl]
                t_p = t_x[:, sl]
                z = _dot(cgb, stb)
                d_z = (e_p * dyp).astype(BF16)
                d_c = d_c + _dot(d_z, stb, _NT)
                d_xf = _dot(bgb, dstb)
                d_b = d_b + _dot((x_dt * f_p).astype(BF16), dstb, _NT)
                d_xdt = d_xdt + f_p * d_xf
                d_f = x_dt * d_xf * f_p
                dcsx_s[:, sl] = dyp * e_p * z - d_f
                dlast_s[:, sl] = _colsum(d_f) + _colsum(dstn * stp) * t_p
                dst_s[j] = _dot(cgb, d_z, _TN) + t_p * dstn
                dxbc_ref[:, sl] = dx_ref[:, sl] * dyp + d_xdt * dtp
                ddtx_s[:, sl] = d_xdt * xp
            d_gb = d_g.astype(BF16)
            dxbc_ref[:, 1024 + g * LANE:1024 + (g + 1) * LANE] = d_b + _dot(d_gb, cgb, _TN)
            dxbc_ref[:, 1280 + g * LANE:1280 + (g + 1) * LANE] = d_c + _dot(d_gb, bgb)

        expt = expt_ref[...]
        dlast = _sel_dot(jnp.broadcast_to(dlast_s[...], (8, 1024)), expt, 3)
        d_cs = dcol_s[...] - drow_s[...].T + _sel_dot(dcsx_s[...], expt, 3)
        rown = lax.broadcasted_iota(jnp.int32, (L, LANE), 0)
        d_cs = d_cs + jnp.where(rown == L - 1, jnp.sum(dlast, axis=0, keepdims=True) * 0.125, 0.0)
        d_da = _sel_dot(d_cs, tri_ref[...], 3, _TN, sel_left=True)
        d_dt = d_da * a + _sel_dot(ddtx_s[...], expt, 3)
        dalog_ref[...] += _colsum(d_da * dt) * a
        d_raw = d_dt * _sigmoid(dt_ref[...] + bias_ref[...])
        ddt_ref[...] = d_raw.astype(ddt_ref.dtype)
        dbias_ref[...] += _colsum(d_raw)
        dd8 = _sel_dot(jnp.broadcast_to(dd_s[...], (8, 1024)), expt, 3)
        dd_ref[...] = jnp.sum(dd8, axis=0, keepdims=True) * 0.125

    const = lambda shape: pl.BlockSpec(shape, lambda c: tuple(0 for _ in shape))
    rev = lambda cb: (lambda c: (nc - 1 - c, cb))
    return pl.pallas_call(
        body, name=name, grid=(nc,),
        in_specs=[pl.BlockSpec((L, 1024), rev(0)), pl.BlockSpec((L, 256), rev(4)), pl.BlockSpec((L, 256), rev(5)),
                  pl.BlockSpec((L, LANE), rev(dtcb)), const((1, LANE)), const((1, LANE)), const((1, 1024)), const((L, L)),
                  const((LANE, 1024)), const((LANE, 2048)), const((1024, LANE)),
                  pl.BlockSpec((1, 8, LANE, LANE), lambda c: (nc - 1 - c, 0, 0, 0)), pl.BlockSpec((L, 1024), rev(0))],
        out_specs=[pl.BlockSpec((L, SSD_XBC), rev(0)), pl.BlockSpec((L, LANE), rev(0)), const((1, LANE)), const((1, LANE)),
                   const((1, LANE))],
        out_shape=[jax.ShapeDtypeStruct((s_dim, SSD_XBC), F32), jax.ShapeDtypeStruct((s_dim, LANE), BF16),
                   jax.ShapeDtypeStruct((1, LANE), F32), jax.ShapeDtypeStruct((1, LANE), F32),
                   jax.ShapeDtypeStruct((1, LANE), F32)],
        scratch_shapes=[pltpu.VMEM((8, LANE, LANE), F32), pltpu.VMEM((L, LANE), F32), pltpu.VMEM((LANE, L), F32),
                        pltpu.VMEM((L, 1024), F32), pltpu.VMEM((L, 2048), F32), pltpu.VMEM((L, 1024), F32),
                        pltpu.VMEM((L, 1024), F32), pltpu.VMEM((L, LANE), F32), pltpu.VMEM((LANE, L), F32),
                        pltpu.VMEM((1, 1024), F32), pltpu.VMEM((1, 1024), F32)],
        compiler_params=pltpu.CompilerParams(dimension_semantics=("arbitrary",)),
    )(xbca, xbca, xbca, dtr, bias, alog, d_x, tri, expand, expand128, expand_t, states, dy)


def _swap_halves(u):
    width = u.shape[1]
    lane = lax.broadcasted_iota(jnp.int32, u.shape, 1)
    return jnp.where(lane % MLA_ROPE < MLA_ROPE // 2, pltpu.roll(u, width - MLA_ROPE // 2, 1), pltpu.roll(u, MLA_ROPE // 2, 1))


def _rope_fwd_fn(u, cos, sin):
    return u * cos + _swap_halves(u) * sin


def _rope_bwd_fn(d, cos, sin):
    return d * cos + _swap_halves(d * sin)


def _spread4(v):
    return v + pltpu.roll(v, 32, 1) + pltpu.roll(v, 64, 1) + pltpu.roll(v, 96, 1)


def _att_masks(tq):
    lane = lax.broadcasted_iota(jnp.int32, (tq, LANE), 1)
    return lane // MLA_NOPE, lane // MLA_ROPE


def _att_tile(i, tq):
    klen = (i + 1) * tq
    qpos = i * tq + lax.broadcasted_iota(jnp.int32, (tq, klen), 0)
    kpos = lax.broadcasted_iota(jnp.int32, (tq, klen), 1)
    return slice(i * tq, (i + 1) * tq), klen, qpos >= kpos


def _att_qcat(qn_t, qr_t, par, e, half_id, grp_id):
    return jnp.concatenate([jnp.where(half_id == par, qn_t * ATT_SCALE, 0.0), jnp.where(grp_id == e, qr_t * ATT_SCALE, 0.0)],
                           axis=1).astype(BF16)


def _att_exp(qcat, kcat, causal):
    s = jnp.where(causal, _dot(qcat, kcat, _NT), -jnp.inf)
    e = jnp.exp(s - jnp.max(s, axis=1, keepdims=True))
    return e, 1.0 / jnp.sum(e, axis=1, keepdims=True)


def _att_specs(s_dim):
    col = lambda f: pl.BlockSpec((s_dim, LANE), lambda j: (0, f(j)))
    return [col(lambda j: j), col(lambda j: j // 2), col(lambda j: j), col(lambda j: 0), col(lambda j: 8 + j)]


def _att_fwd(q, qr, kv, krt, name="att_fwd"):
    s_dim = q.shape[0]
    tq = min(ATT_TQ, s_dim)

    def body(qn_ref, qr_ref, kn_ref, krt_ref, v_ref, o_ref, kcat_s, vb_s):
        e0 = 2 * (pl.program_id(0) % 2)
        half_id, grp_id = _att_masks(tq)
        kcat_s[...] = jnp.concatenate([kn_ref[...], krt_ref[...]], axis=1).astype(BF16)
        vb_s[...] = v_ref[...].astype(BF16)
        for i in range(s_dim // tq):
            rows, klen, causal = _att_tile(i, tq)
            qn_t, qr_t = qn_ref[rows, :], qr_ref[rows, :]
            outs = []
            for par in range(2):
                qcat = _att_qcat(qn_t, qr_t, par, e0 + par, half_id, grp_id)
                e, inv_l = _att_exp(qcat, kcat_s[0:klen, :], causal)
                outs.append(_dot(e.astype(BF16), vb_s[0:klen, :]) * inv_l)
            o_ref[rows, :] = jnp.where(half_id == 0, outs[0], outs[1])

    return pl.pallas_call(
        body, name=name, grid=(MLA_HEADS // 2,), in_specs=_att_specs(s_dim),
        out_specs=pl.BlockSpec((s_dim, LANE), lambda j: (0, j)), out_shape=jax.ShapeDtypeStruct((s_dim, 1024), F32),
        scratch_shapes=[pltpu.VMEM((s_dim, 2 * LANE), BF16), pltpu.VMEM((s_dim, LANE), BF16)],
        compiler_params=pltpu.CompilerParams(dimension_semantics=("parallel",)),
    )(q, qr, kv, krt, kv)


def _att_bwd(q, qr, kv, krt, o, do, name="att_bwd"):
    s_dim = q.shape[0]
    tq = min(ATT_TQ, s_dim)

    def body(qn_ref, qr_ref, kn_ref, krt_ref, v_ref, o_ref, do_ref, dqn_ref, dqr_ref, dkn_ref, dv_ref, dkrt_ref,
             kcat_s, vb_s):
        e0 = 2 * (pl.program_id(0) % 2)
        half_id, grp_id = _att_masks(tq)
        kcat_s[...] = jnp.concatenate([kn_ref[...], krt_ref[...]], axis=1).astype(BF16)
        vb_s[...] = v_ref[...].astype(BF16)
        dkn_ref[...] = jnp.zeros_like(dkn_ref)
        dv_ref[...] = jnp.zeros_like(dv_ref)
        dkrt_ref[...] = jnp.zeros_like(dkrt_ref)
        for i in range(s_dim // tq):
            rows, klen, causal = _att_tile(i, tq)
            qn_t, qr_t, o_t, do_t = qn_ref[rows, :], qr_ref[rows, :], o_ref[rows, :], do_ref[rows, :]
            dqn = jnp.zeros((tq, LANE), F32)
            dqr = jnp.zeros((tq, LANE), F32)
            for par in range(2):
                qcat = _att_qcat(qn_t, qr_t, par, e0 + par, half_id, grp_id)
                e, inv_l = _att_exp(qcat, kcat_s[0:klen, :], causal)
                p = e * inv_l
                dom = jnp.where(half_id == par, do_t, 0.0)
                domb = dom.astype(BF16)
                d_p = _dot(domb, vb_s[0:klen, :], _NT)
                d_row = jnp.sum(dom * o_t, axis=1, keepdims=True)
                d_s = (p * (d_p - d_row)).astype(BF16)
                dqcat = _dot(d_s, kcat_s[0:klen, :]) * ATT_SCALE
                dqn = dqn + jnp.where(half_id == par, dqcat[:, :LANE], 0.0)
                dqr = dqr + jnp.where(grp_id == e0 + par, dqcat[:, LANE:], 0.0)
                dkcat = _dot(d_s, qcat, _TN)
                dkn_ref[0:klen, :] += dkcat[:, :LANE]
                dkrt_ref[0:klen, :] += dkcat[:, LANE:]
                dv_ref[0:klen, :] += _dot(p.astype(BF16), domb, _TN)
            dqn_ref[rows, :] = dqn.astype(dqn_ref.dtype)
            dqr_ref[rows, :] = dqr

    col = lambda f: pl.BlockSpec((s_dim, LANE), lambda j: (0, f(j)))
    return pl.pallas_call(
        body, name=name, grid=(MLA_HEADS // 2,), in_specs=_att_specs(s_dim) + [col(lambda j: j), col(lambda j: j)],
        out_specs=[col(lambda j: j), pl.BlockSpec((None, s_dim, LANE), lambda j: (j % 2, 0, j // 2)), col(lambda j: j),
                   col(lambda j: j), pl.BlockSpec((None, s_dim, LANE), lambda j: (j, 0, 0))],
        out_shape=[jax.ShapeDtypeStruct((s_dim, 1024), BF16), jax.ShapeDtypeStruct((2, s_dim, 512), F32),
                   jax.ShapeDtypeStruct((s_dim, 1024), F32), jax.ShapeDtypeStruct((s_dim, 1024), F32),
                   jax.ShapeDtypeStruct((MLA_HEADS // 2, s_dim, LANE), F32)],
        scratch_shapes=[pltpu.VMEM((s_dim, 2 * LANE), BF16), pltpu.VMEM((s_dim, LANE), BF16)],
        compiler_params=pltpu.CompilerParams(dimension_semantics=("parallel",)),
    )(q, qr, kv, krt, kv, o, do)


def _all_gather(x, name):
    rows, width = x.shape

    def body(x_ref, out_ref, send_sems, recv_sems, local_sem):
        x_i, y_i, c_i = lax.axis_index("x"), lax.axis_index("y"), lax.axis_index("c")
        me, sibling = (x_i, y_i, c_i), (x_i, y_i, 1 - c_i)
        chips = [(1 - x_i, y_i), (x_i, 1 - y_i), (1 - x_i, 1 - y_i)]

        def slot(px, py, pc):
            return out_ref.at[4 * px + 2 * py + pc]

        def copy(k, block, to, src=None):
            return pltpu.make_async_remote_copy(
                src_ref=slot(*block) if src is None else src, dst_ref=slot(*block), send_sem=send_sems.at[k],
                recv_sem=recv_sems.at[k], device_id=to, device_id_type=pl.DeviceIdType.MESH)

        mine = pltpu.make_async_copy(x_ref, slot(*me), local_sem)
        mine.start()
        first = [copy(0, me, sibling, src=x_ref)]
        first += [copy(1 + j, me, (*chip, c_i), src=x_ref) for j, chip in enumerate(chips)]
        for cp in first:
            cp.start()
        passed = [copy(4 + j, (*chip, c_i), sibling) for j, chip in enumerate(chips)]
        for j, chip in enumerate(chips):
            copy(1 + j, (*chip, c_i), me).wait_recv()
            passed[j].start()
        copy(0, sibling, me).wait_recv()
        for j, chip in enumerate(chips):
            copy(4 + j, (*chip, 1 - c_i), me).wait_recv()
        for cp in first + passed:
            cp.wait_send()
        mine.wait()

    return pl.pallas_call(
        body, name=name, out_shape=jax.ShapeDtypeStruct((N_DEV, rows, width), x.dtype),
        in_specs=[pl.BlockSpec(memory_space=pl.ANY)], out_specs=pl.BlockSpec(memory_space=pl.ANY),
        scratch_shapes=[pltpu.SemaphoreType.DMA((7,)), pltpu.SemaphoreType.DMA((7,)), pltpu.SemaphoreType.DMA],
    )(x)


def _gather_many(shards, name):
    n_arr = len(shards)

    def body(*refs):
        x_refs, out_refs = refs[:n_arr], refs[n_arr:2 * n_arr]
        send_sems, recv_sems, local_sems = refs[2 * n_arr:]
        x_i, y_i, c_i = lax.axis_index("x"), lax.axis_index("y"), lax.axis_index("c")
        me, sibling = (x_i, y_i, c_i), (x_i, y_i, 1 - c_i)
        chips = [(1 - x_i, y_i), (x_i, 1 - y_i), (1 - x_i, 1 - y_i)]

        def copy(a, k, block, to, src=None):
            slot = out_refs[a].at[4 * block[0] + 2 * block[1] + block[2]]
            return pltpu.make_async_remote_copy(
                src_ref=slot if src is None else src, dst_ref=slot, send_sem=send_sems.at[a, k],
                recv_sem=recv_sems.at[a, k], device_id=to, device_id_type=pl.DeviceIdType.MESH)

        mine, first, passed = [], [], []
        for a in range(n_arr):
            mine.append(pltpu.make_async_copy(x_refs[a], out_refs[a].at[4 * x_i + 2 * y_i + c_i], local_sems.at[a]))
            mine[a].start()
            first.append([copy(a, 0, me, sibling, src=x_refs[a])]
                         + [copy(a, 1 + j, me, (*chip, c_i), src=x_refs[a]) for j, chip in enumerate(chips)])
            for cp in first[a]:
                cp.start()
            passed.append([copy(a, 4 + j, (*chip, c_i), sibling) for j, chip in enumerate(chips)])
        for j, chip in enumerate(chips):
            for a in range(n_arr):
                copy(a, 1 + j, (*chip, c_i), me).wait_recv()
                passed[a][j].start()
        for a in range(n_arr):
            copy(a, 0, sibling, me).wait_recv()
            for j, chip in enumerate(chips):
                copy(a, 4 + j, (*chip, 1 - c_i), me).wait_recv()
        for a in range(n_arr):
            for cp in first[a] + passed[a]:
                cp.wait_send()
            mine[a].wait()

    any_spec = pl.BlockSpec(memory_space=pl.ANY)
    return pl.pallas_call(
        body, name=name, out_shape=[jax.ShapeDtypeStruct((N_DEV,) + x.shape, x.dtype) for x in shards],
        in_specs=[any_spec] * n_arr, out_specs=[any_spec] * n_arr,
        scratch_shapes=[pltpu.SemaphoreType.DMA((n_arr, 7)), pltpu.SemaphoreType.DMA((n_arr, 7)),
                        pltpu.SemaphoreType.DMA((n_arr,))],
    )(*shards)


_HBM = pl.BlockSpec(memory_space=pltpu.HBM)
_SEM = pl.BlockSpec(memory_space=pltpu.SEMAPHORE)


def _plan_copies(plan, src_refs, land_refs, send_sems, recv_sems):
    copies = []
    for s_ref, l_ref in zip(src_refs, land_refs):
        for src, dst, peer in plan(s_ref, l_ref):
            k = len(copies)
            copies.append(pltpu.make_async_remote_copy(
                src_ref=src, dst_ref=dst, send_sem=send_sems.at[k], recv_sem=recv_sems.at[k], device_id=peer,
                device_id_type=pl.DeviceIdType.MESH))
    return copies


def _split_start(srcs, lands, plan, n_copy, name, after=None):
    n = len(srcs)
    n_in = 2 * n + (after is not None)

    def body(*refs):
        for cp in _plan_copies(plan, refs[:n], refs[n:2 * n], refs[n_in], refs[n_in + 1]):
            cp.start()
        refs[-1][...] = jnp.zeros_like(refs[-1])

    sems = pltpu.SemaphoreType.DMA((n * n_copy,))
    res = pl.pallas_call(
        body, name=name,
        out_shape=(sems, sems, *[pltpu.HBM(a.shape, a.dtype) for a in list(srcs) + list(lands)],
                   jax.ShapeDtypeStruct((8, LANE), F32)),
        in_specs=[_HBM] * (2 * n) + [pl.BlockSpec(memory_space=pl.ANY)] * (after is not None),
        out_specs=(_SEM, _SEM, *[_HBM] * (2 * n), pl.BlockSpec(memory_space=pltpu.VMEM)),
        input_output_aliases={i: 2 + i for i in range(2 * n)},
        compiler_params=pltpu.CompilerParams(has_side_effects=pltpu.SideEffectType.DATAFLOW_SIDE_EFFECTING),
    )(*[pltpu.with_memory_space_constraint(a, pltpu.HBM) for a in list(srcs) + list(lands)],
      *([after] if after is not None else []))
    return res[0], res[1], list(res[2:2 + n]), list(res[2 + n:2 + 2 * n]), res[-1]


def _split_wait(send_sems, recv_sems, srcs, lands, after, plan, name):
    n = len(srcs)

    def body(*refs):
        copies = _plan_copies(plan, refs[:n], refs[n:2 * n], refs[2 * n], refs[2 * n + 1])
        for cp in copies:
            cp.wait_send()
        for cp in copies:
            cp.wait_recv()

    res = pl.pallas_call(
        body, name=name, out_shape=tuple(pltpu.HBM(a.shape, a.dtype) for a in list(srcs) + list(lands)),
        in_specs=[_HBM] * (2 * n) + [_SEM, _SEM, pl.BlockSpec(memory_space=pl.ANY)], out_specs=tuple([_HBM] * (2 * n)),
        input_output_aliases={i: i for i in range(2 * n)},
        compiler_params=pltpu.CompilerParams(has_side_effects=pltpu.SideEffectType.DATAFLOW_SIDE_EFFECTING),
    )(*srcs, *lands, send_sems, recv_sems, after)
    return list(res[:n]), list(res[n:])


def _plan_broadcast(src, land):
    x_i, y_i, c_i = lax.axis_index("x"), lax.axis_index("y"), lax.axis_index("c")
    me = 4 * x_i + 2 * y_i + c_i
    return [(src, land.at[me], (x_i ^ (k >> 2), y_i ^ ((k >> 1) & 1), c_i ^ (k & 1))) for k in range(1, N_DEV)]


def _plan_scatter(src, land):
    x_i, y_i, c_i = lax.axis_index("x"), lax.axis_index("y"), lax.axis_index("c")
    me = 4 * x_i + 2 * y_i + c_i
    plan = []
    for k in range(1, N_DEV):
        px, py, pc = x_i ^ (k >> 2), y_i ^ ((k >> 1) & 1), c_i ^ (k & 1)
        plan.append((src.at[4 * px + 2 * py + pc], land.at[me], (px, py, pc)))
    return plan


def _adam_math(g, w, m, v):
    m_new = ADAM_B1 * m + (1.0 - ADAM_B1) * g
    v_new = ADAM_B2 * v + (1.0 - ADAM_B2) * (g * g)
    m_hat = m_new / (1.0 - ADAM_B1 ** ADAM_STEP)
    v_hat = v_new / (1.0 - ADAM_B2 ** ADAM_STEP)
    return -ADAM_LR * (m_hat / (jnp.sqrt(v_hat) + ADAM_EPS) + ADAM_WD * w), m_new, v_new


def _adam(slots, w, m, v, name, own=None, own_idx=None):
    n_slot, rows, cols = slots.shape
    tr = ROW_TILE if rows % ROW_TILE == 0 else rows
    has_own = own is not None

    def body(*refs):
        if has_own:
            idx_ref, own_ref, refs = refs[0], refs[1], refs[2:]
        s_ref, w_ref, m_ref, v_ref, g_ref, d_ref, mo_ref, vo_ref = refs
        g = own_ref[...].astype(F32) if has_own else s_ref[0].astype(F32)
        for k in range(0 if has_own else 1, n_slot):
            part = s_ref[k].astype(F32)
            g = g + (jnp.where(idx_ref[0] == k, 0.0, part) if has_own else part)
        g_ref[...] = g
        d_ref[...], mo_ref[...], vo_ref[...] = _adam_math(g, w_ref[...], m_ref[...], v_ref[...])

    spec = pl.BlockSpec((tr, cols), lambda i, *_: (i, 0))
    in_specs = [pl.BlockSpec((n_slot, tr, cols), lambda i, *_: (0, i, 0)), spec, spec, spec]
    if has_own:
        in_specs = [pl.BlockSpec((None, tr, cols), lambda i, idx: (idx[0], i, 0))] + in_specs
    grid_spec = pltpu.PrefetchScalarGridSpec(num_scalar_prefetch=1 if has_own else 0, grid=(rows // tr,), in_specs=in_specs,
                                             out_specs=[spec] * 4)
    ins = ([own_idx, own] if has_own else []) + [slots, w, m, v]
    return pl.pallas_call(
        body, name=name, grid_spec=grid_spec, out_shape=[jax.ShapeDtypeStruct((rows, cols), F32)] * 4,
        compiler_params=pltpu.CompilerParams(dimension_semantics=("parallel",)),
    )(*ins)


PACK_ROWS, PACK_W = 24, 1536
REPL_W = (("ssd_conv_b", 1536), ("ssd_dt_bias", 16), ("ssd_A_log", 16), ("ssd_D", 16), ("ssd_norm_w", 1024),
          ("mla_q_norm_w", 384), ("mla_kv_norm_w", 256), ("mla_out_norm_w", 1024), ("ln_mix_g", 1024),
          ("ln_mix_b", 1024), ("ln_ffn_g", 1024), ("ln_ffn_b", 1024))
LOSS_ROW = 4 + len(REPL_W)


def _pack_small(conv_w_grad, grads, loss, name="pack_small"):
    def body(*refs):
        cw_ref, g_refs, loss_ref, o_ref = refs[0], refs[1:1 + len(REPL_W)], refs[1 + len(REPL_W)], refs[-1]
        o_ref[...] = jnp.zeros_like(o_ref)
        o_ref[0:4, :] = cw_ref[...]
        for i, g_ref in enumerate(g_refs):
            o_ref[4 + i:5 + i, 0:g_ref.shape[1]] = g_ref[...]
        o_ref[LOSS_ROW:LOSS_ROW + 1, 0:LANE] = loss_ref[...]

    return pl.pallas_call(body, name=name, out_shape=jax.ShapeDtypeStruct((PACK_ROWS, PACK_W), F32))(conv_w_grad, *grads, loss)


def _adam_small(gathered, wmv, name="adam_small"):
    def body(*refs):
        s_ref = refs[0]
        in_refs = refs[1:1 + 3 * len(REPL_W)]
        cw_ref, loss_ref = refs[1 + 3 * len(REPL_W)], refs[2 + 3 * len(REPL_W)]
        out_refs = refs[3 + 3 * len(REPL_W):-1]
        tot = refs[-1]
        acc = s_ref[0]
        for k in range(1, N_DEV):
            acc = acc + s_ref[k]
        tot[...] = acc
        cw_ref[...] = tot[0:4, :]
        loss_ref[...] = tot[LOSS_ROW:LOSS_ROW + 1, 0:LANE]
        for i, (_, width) in enumerate(REPL_W):
            g = tot[4 + i:5 + i, 0:width]
            w_ref, m_ref, v_ref = in_refs[3 * i:3 * i + 3]
            g_ref, d_ref, mo_ref, vo_ref = out_refs[4 * i:4 * i + 4]
            g_ref[...] = g
            d_ref[...], mo_ref[...], vo_ref[...] = _adam_math(g, w_ref[...], m_ref[...], v_ref[...])

    flat_in = [a for triple in wmv for a in triple]
    out_shape = [jax.ShapeDtypeStruct((4, PACK_W), F32), jax.ShapeDtypeStruct((1, LANE), F32)]
    for _, width in REPL_W:
        out_shape += [jax.ShapeDtypeStruct((1, width), F32)] * 4
    res = pl.pallas_call(body, name=name, out_shape=out_shape, scratch_shapes=[pltpu.VMEM((PACK_ROWS, PACK_W), F32)])(
        gathered, *flat_in)
    return res[0], res[1], [res[2 + 4 * i:6 + 4 * i] for i in range(len(REPL_W))]


def _cols_full(g):
    return jnp.transpose(g, (1, 0, 2)).reshape(g.shape[1], -1)


def _cols_split(full):
    k_dim, n_dim = full.shape
    return jnp.transpose(full.reshape(k_dim, N_DEV, n_dim // N_DEV), (1, 0, 2))


PROJ_BLOCK = {"z": (1024, 0), "dt": (LANE, 8), "q_c": (MLA_Q_RANK, 3), "xbc": (SSD_XBC, 1), "kv_c": (MLA_KV_RANK, 12),
              "k_rope": (LANE, 26)}


def _win_pad(wt):
    z = lambda n: jnp.zeros((n, wt.shape[1]), wt.dtype)
    return jnp.concatenate([wt[:1024], wt[2560:2576], z(112), wt[2576:2960], wt[1024:2560], wt[2960:3216], wt[3216:3248],
                            z(96)], axis=0)


def _win_unpad(wt):
    return jnp.concatenate([wt[:1024], wt[1536:3072], wt[1024:1040], wt[1152:1536], wt[3072:3328], wt[3328:3360]], axis=0)


def _heads_split_t(wt, a, b):
    w3 = wt.reshape(MLA_HEADS, a + b, wt.shape[1])
    return jnp.concatenate([w3[:, :a].reshape(-1, wt.shape[1]), w3[:, a:].reshape(-1, wt.shape[1])], axis=0)


def _heads_merge_t(wt, a, b):
    wa = wt[:MLA_HEADS * a].reshape(MLA_HEADS, a, wt.shape[1])
    wb = wt[MLA_HEADS * a:].reshape(MLA_HEADS, b, wt.shape[1])
    return jnp.concatenate([wa, wb], axis=1).reshape(-1, wt.shape[1])


def _heads_split(w, a, b):
    k_dim = w.shape[0]
    w3 = w.reshape(k_dim, MLA_HEADS, a + b)
    return jnp.concatenate([w3[:, :, :a].reshape(k_dim, -1), w3[:, :, a:].reshape(k_dim, -1)], axis=1)


def _heads_merge(w, a, b):
    k_dim = w.shape[0]
    wa = w[:, :MLA_HEADS * a].reshape(k_dim, MLA_HEADS, a)
    wb = w[:, MLA_HEADS * a:].reshape(k_dim, MLA_HEADS, b)
    return jnp.concatenate([wa, wb], axis=2).reshape(k_dim, -1)


def _pad_lanes(v, width=LANE):
    return jnp.concatenate([v, jnp.zeros((v.shape[0], width - v.shape[1]), v.dtype)], axis=1)


def _local_step(x, p, positions, tgt, W, P, comm=None):
    comm = comm or {}
    zero_tok = jnp.zeros((8, LANE), F32)
    s_dim = x.shape[0]
    inv_freq = 1.0 / (ROPE_BASE ** (jnp.arange(0, MLA_ROPE, 2, dtype=F32) / MLA_ROPE))
    ang = positions.astype(F32)[:, None] * inv_freq
    cos, sin = jnp.cos(ang), jnp.sin(ang)
    cos32 = jnp.concatenate([cos, cos], axis=1)
    sin32 = jnp.concatenate([-sin, sin], axis=1)
    cos512, sin512 = jnp.tile(cos32, (1, 16)), jnp.tile(sin32, (1, 16))
    cos128, sin128 = jnp.tile(cos32, (1, 4)), jnp.tile(sin32, (1, 4))
    bias_p, alog_p = _pad_lanes(P["ssd_dt_bias"]), _pad_lanes(P["ssd_A_log"])
    d_x = jnp.repeat(P["ssd_D"], SSD_HEAD_DIM, axis=1)

    xb, pb = x.astype(BF16), p.astype(BF16)
    proj = _mm(xb, W["w_in"], tb=True, after=comm.get("token0", zero_tok), name="mm_in")
    z, qc, kvc, kr = [(proj,) + PROJ_BLOCK[n] for n in ("z", "q_c", "kv_c", "k_rope")]
    xbca = _conv_fwd(proj, PROJ_BLOCK["xbc"][1], P["ssd_conv_w"], P["ssd_conv_b"])
    y, states = _ssd_fwd(xbca, proj, PROJ_BLOCK["dt"][1], bias_p, alog_p, d_x)
    (yssd,) = _rowwise(_gate_rms, [y, z], [P["ssd_norm_w"]], [(1024, BF16)], name="ssd_gate_norm")
    (qn,) = _rowwise(_rms, [qc], [P["mla_q_norm_w"]], [(MLA_Q_RANK, BF16)], name="q_norm")
    (kvn,) = _rowwise(_rms, [kvc], [P["mla_kv_norm_w"]], [(MLA_KV_RANK, BF16)], name="kv_norm")
    q = _mm(qn, W["mla_w_q_b"], tb=True, name="mm_q")
    kv = _mm(kvn, W["mla_w_kv_b"], name="mm_kv")
    (qr,) = _rowwise(_rope_fwd_fn, [(q, 512, 2), cos512, sin512], [], [512], name="rope_q")
    (krt,) = _rowwise(lambda u, c, s: _spread4(_rope_fwd_fn(u, c, s)), [kr, cos128, sin128], [], [LANE], name="rope_k")
    att = _att_fwd(q, qr, kv, krt)
    (ymla,) = _rowwise(_rms, [att], [P["mla_out_norm_w"]], [(1024, BF16)], name="out_norm")
    ycat = jnp.concatenate([yssd, ymla], axis=1)
    if "late_weights" in comm:
        W = {**W, **comm["late_weights"](ycat)}
    mix = _mm(ycat, W["w_out"], name="mm_out")
    f_h1 = lambda xv, mv, g, b: _ln(ALPHA * xv + mv, g, b)
    h1, h1b = _rowwise(lambda *a: (f_h1(*a),) * 2, [x, mix], [P["ln_mix_g"], P["ln_mix_b"]], [1024, (1024, BF16)],
                       name="ln_mix")
    hg = _mm(h1b, W["w_ffn_gate"], tb=True, out_dtype=BF16, name="mm_gate")
    hu = _mm(h1b, W["w_ffn_up"], tb=True, out_dtype=BF16, name="mm_up")
    pg = _mm(h1b, W["w_ple_gate"], name="mm_ple_gate")
    pp = _mm(pb, W["w_ple_proj"], name="mm_ple")
    (act,) = _rowwise(lambda g, u: _silu(g.astype(F32)) * u.astype(F32), [hg, hu], [], [(D_FF, BF16)], name="swiglu")
    ffn = _mm(act, W["w_ffn_down"], name="mm_down")

    f_h2 = lambda hv, fv, pg, ppv, g, b: _ln(ALPHA * hv + fv + _sigmoid(pg) * ppv, g, b)

    def final_fn(hv, fv, pg, ppv, tv, g, b):
        h2, pull = jax.vjp(f_h2, hv, fv, pg, ppv, g, b)
        diff = h2 - tv
        loss = 0.5 * jnp.sum(jnp.mean(diff * diff, axis=-1, keepdims=True), axis=0, keepdims=True)
        d_h, d_f, d_pg, d_pp, d_g, d_b = pull(diff * (1.0 / D_MODEL))
        return d_h, d_f, d_pg, d_pp, d_g, d_b, jnp.broadcast_to(loss, (1, LANE))

    dh1_a, dffn, dpg, dpp, g_ffn_g, g_ffn_b, loss = _rowwise(
        final_fn, [h1, ffn, pg, pp, tgt], [P["ln_ffn_g"], P["ln_ffn_b"]], [1024] + [(1024, BF16)] * 3,
        [1024, 1024, LANE], name="final")

    G = {}
    dact = _mm(dffn, W["w_ffn_down"], tb=True, name="mm_down_dx")
    G["w_ffn_down"] = _mm(act, dffn, ta=True, out_dtype=GRAD_DT, name="mm_down_dw")

    def swiglu_bwd(g, u, d):
        g, u = g.astype(F32), u.astype(F32)
        sg = _sigmoid(g)
        return d * u * (sg * (1.0 + g * (1.0 - sg))), d * (g * sg)

    dg, du = _rowwise(swiglu_bwd, [hg, hu, dact], [], [(D_FF, BF16)] * 2, name="swiglu_bwd")
    dh1 = _mm(dg, W["w_ffn_gate"], add=dh1_a, name="mm_gate_dx")
    dh1 = _mm(du, W["w_ffn_up"], add=dh1, name="mm_up_dx")
    dh1 = _mm(dpg, W["w_ple_gate"], tb=True, add=dh1, name="mm_ple_gate_dx")
    G["w_ffn_gate"] = _mm(dg, h1b, ta=True, out_dtype=GRAD_DT, name="mm_gate_dw")
    G["w_ffn_up"] = _mm(du, h1b, ta=True, out_dtype=GRAD_DT, name="mm_up_dw")
    G["w_ple_gate"] = _mm(h1b, dpg, ta=True, out_dtype=GRAD_DT, name="mm_ple_gate_dw")
    G["w_ple_proj"] = _mm(pb, dpp, ta=True, out_dtype=GRAD_DT, name="mm_ple_dw")
    grads_done = comm.get("grads", lambda group, grads: zero_tok)
    tok1 = grads_done("ffn", G)
    dx_a, dmix, g_mix_g, g_mix_b = _rowwise(
        lambda xv, mv, dv, g, b, t: _vjp_rows(f_h1)(xv, mv, g, b, dv + jnp.min(t)), [x, mix, dh1],
        [P["ln_mix_g"], P["ln_mix_b"], tok1], [1024, (1024, BF16)], [1024, 1024], name="ln_mix_bwd")
    dycat = _mm(dmix, W["w_out"], tb=True, name="mm_out_dx")
    G["w_out"] = _mm(ycat, dmix, ta=True, out_dtype=GRAD_DT, name="mm_out_dw")

    datt, g_out_norm = _rowwise(lambda a, dv, w, t: _vjp_rows(_rms)(a, w, dv + jnp.min(t)), [att, (dycat, 1024, 1)],
                                [P["mla_out_norm_w"], tok1], [1024], [1024], name="out_norm_bwd")
    dqn_nope, dqr, dkn, dv, dkrt = _att_bwd(q, qr, kv, krt, att, datt)
    dkv = jnp.concatenate([dkn, dv], axis=1)
    (dq_rope,) = _rowwise(lambda d0, d1, c, s: _rope_bwd_fn(d0 + d1, c, s), [(dqr, 512, 0), (dqr, 512, 1), cos512, sin512],
                          [], [(512, BF16)], name="rope_q_bwd")

    def rope_k_bwd(*a):
        d = _spread4(functools.reduce(lambda u, w: u + w, a[:-2]))
        lane = lax.broadcasted_iota(jnp.int32, d.shape, 1)
        return _rope_bwd_fn(jnp.where(lane < MLA_ROPE, d, 0.0), a[-2], a[-1])

    (dkr,) = _rowwise(rope_k_bwd, [(dkrt, LANE, k) for k in range(MLA_HEADS // 2)] + [cos128, sin128], [], [(LANE, BF16)],
                      name="rope_k_bwd")
    dq = jnp.concatenate([dqn_nope, dq_rope], axis=1)
    dqn = _mm(dq, W["mla_w_q_b"], name="mm_q_dx")
    G["mla_w_q_b"] = _mm(dq, qn, ta=True, out_dtype=GRAD_DT, name="mm_q_dw")
    dkvn = _mm(dkv, W["mla_w_kv_b"], tb=True, name="mm_kv_dx")
    G["mla_w_kv_b"] = _mm(kvn, dkv, ta=True, out_dtype=GRAD_DT, name="mm_kv_dw")
    tok2 = grads_done("mla", G)
    dqc, g_q_norm = _rowwise(lambda a, dv, w, t: _vjp_rows(_rms)(a, w, dv + jnp.min(t)), [qc, dqn],
                             [P["mla_q_norm_w"], tok2], [(MLA_Q_RANK, BF16)], [MLA_Q_RANK], name="q_norm_bwd")
    dkvc, g_kv_norm = _rowwise(lambda a, dv, w: _vjp_rows(_rms)(a, w, dv), [kvc, dkvn], [P["mla_kv_norm_w"]],
                               [(MLA_KV_RANK, BF16)], [MLA_KV_RANK], name="kv_norm_bwd")

    dy, dz, g_ssd_norm = _rowwise(lambda yv, zv, dv, w, t: _vjp_rows(_gate_rms)(yv, zv, w, dv + jnp.min(t)),
                                  [y, z, (dycat, 1024, 0)], [P["ssd_norm_w"], tok1], [1024, (1024, BF16)], [1024],
                                  name="ssd_gate_norm_bwd")
    dxbca, ddtr, g_dt_bias, g_alog, g_d = _ssd_bwd(xbca, proj, PROJ_BLOCK["dt"][1], bias_p, alog_p, d_x, states, dy)
    da, g_conv_w, g_conv_b = _conv_bwd_pre(proj, PROJ_BLOCK["xbc"][1], P["ssd_conv_w"], P["ssd_conv_b"], dxbca)
    dxbc = _conv_bwd_in(da, P["ssd_conv_w"])

    dproj = jnp.concatenate([dz, ddtr, dqc, dxbc, dkvc, dkr], axis=1)
    G["w_in"] = _mm(dproj, xb, ta=True, out_dtype=GRAD_DT, name="mm_in_dw")
    grad_x = _mm(dproj, W["w_in"], add=dx_a, after=grads_done("in", G), name="mm_in_dx")

    small = {
        "ssd_conv_b": g_conv_b, "ssd_dt_bias": g_dt_bias, "ssd_A_log": g_alog, "ssd_D": g_d, "ssd_norm_w": g_ssd_norm,
        "mla_q_norm_w": g_q_norm, "mla_kv_norm_w": g_kv_norm, "mla_out_norm_w": g_out_norm, "ln_mix_g": g_mix_g,
        "ln_mix_b": g_mix_b, "ln_ffn_g": g_ffn_g, "ln_ffn_b": g_ffn_b,
    }
    return grad_x, G, _pack_small(g_conv_w, [small[n] for n, _ in REPL_W], loss)


def kernel(x, p, positions, w_in, ssd_conv_w, ssd_conv_b, ssd_dt_bias, ssd_A_log, ssd_D, ssd_norm_w, mla_q_norm_w, mla_w_q_b, mla_kv_norm_w, mla_w_kv_b, mla_out_norm_w, w_out, ln_mix_g, ln_mix_b, w_ffn_gate, w_ffn_up, w_ffn_down, w_ple_gate, w_ple_proj, ln_ffn_g, ln_ffn_b, loss_target, m_w_in, m_ssd_conv_w, m_ssd_conv_b, m_ssd_dt_bias, m_ssd_A_log, m_ssd_D, m_ssd_norm_w, m_mla_q_norm_w, m_mla_w_q_b, m_mla_kv_norm_w, m_mla_w_kv_b, m_mla_out_norm_w, m_w_out, m_ln_mix_g, m_ln_mix_b, m_w_ffn_gate, m_w_ffn_up, m_w_ffn_down, m_w_ple_gate, m_w_ple_proj, m_ln_ffn_g, m_ln_ffn_b, v_w_in, v_ssd_conv_w, v_ssd_conv_b, v_ssd_dt_bias, v_ssd_A_log, v_ssd_D, v_ssd_norm_w, v_mla_q_norm_w, v_mla_w_q_b, v_mla_kv_norm_w, v_mla_w_kv_b, v_mla_out_norm_w, v_w_out, v_ln_mix_g, v_ln_mix_b, v_w_ffn_gate, v_w_ffn_up, v_w_ffn_down, v_w_ple_gate, v_w_ple_proj, v_ln_ffn_g, v_ln_ffn_b):
    args = dict(locals())
    core = lax.axis_index("c")
    me = 4 * lax.axis_index("x") + 2 * lax.axis_index("y") + core

    conv_sh = ssd_conv_w[0]
    conv_hi = conv_sh.astype(BF16)
    conv_lo = (conv_sh - conv_hi.astype(F32)).astype(BF16)
    stored = lambda n, pre="": jnp.transpose(args[pre + n][0]) if n in TRANSPOSED else args[pre + n][0]
    shards = {n: stored(n).astype(BF16) for n in BIG}
    rows_full = lambda g: g.reshape(-1, g.shape[2])

    early = _gather_many([shards[n] for n in EARLY] + [jnp.concatenate([conv_hi, conv_lo], axis=0)], "gather_early")
    gw = dict(zip(EARLY, early[:-1]))
    conv_g = early[-1].astype(F32)
    W = {
        "w_in": _win_pad(rows_full(gw["w_in"])),
        "mla_w_q_b": _heads_split_t(rows_full(gw["mla_w_q_b"]), MLA_NOPE, MLA_ROPE),
        "mla_w_kv_b": _heads_split(_cols_full(gw["mla_w_kv_b"]), MLA_NOPE, MLA_V),
    }
    P = {n: args[n] for n, _ in REPL_W}
    P["ssd_conv_w"] = _cols_full(conv_g[:, :4] + conv_g[:, 4:])

    lands = [lax.dynamic_update_slice(lax.empty((N_DEV,) + shards[n].shape, BF16), shards[n][None], (me, 0, 0)) for n in LATE]
    late_sems = _split_start([shards[n] for n in LATE], lands, _plan_broadcast, N_DEV - 1, "gather_late_start",
                             after=early[0])

    def late_weights(after):
        _, got = _split_wait(*late_sems[:4], after, _plan_broadcast, "gather_late_wait")
        lw = dict(zip(LATE, got))
        return {"w_out": rows_full(lw["w_out"]), "w_ple_gate": rows_full(lw["w_ple_gate"]),
                "w_ple_proj": _cols_full(lw["w_ple_proj"]), "w_ffn_gate": rows_full(lw["w_ffn_gate"]),
                "w_ffn_up": rows_full(lw["w_ffn_up"]), "w_ffn_down": rows_full(lw["w_ffn_down"])}

    def to_blocks(n, g):
        if n == "w_in":
            g = _win_unpad(g)
        elif n == "mla_w_q_b":
            g = _heads_merge_t(g, MLA_NOPE, MLA_ROPE)
        elif n == "mla_w_kv_b":
            g = _heads_merge(g, MLA_NOPE, MLA_V)
        if n in ROW_SHARDED or n in TRANSPOSED:
            return g.reshape(N_DEV, -1, g.shape[1])
        return _cols_split(g)

    flight = {}

    def grads(group, G):
        gl = [to_blocks(n, G[n]) for n in GRAD_GROUPS[group]]
        flight[group] = _split_start(gl, [lax.empty(g.shape, g.dtype) for g in gl], _plan_scatter, N_DEV - 1,
                                     "grads_" + group + "_start")
        return flight[group][4]

    grad_x, G, packed = _local_step(x[0], p[0, 0], positions[0], loss_target[0], W, P,
                                    comm={"token0": late_sems[4], "late_weights": late_weights, "grads": grads})

    me_arr = me.astype(jnp.int32).reshape(1)
    big_out = {}

    def finish(group, after):
        mine, recv = _split_wait(*flight[group][:4], after, _plan_scatter, "grads_" + group + "_wait")
        for n, g, r in zip(GRAD_GROUPS[group], mine, recv):
            big_out[n] = _adam(r, stored(n), stored(n, "m_"), stored(n, "v_"), "adam_" + n, own=g, own_idx=me_arr)
        return big_out[GRAD_GROUPS[group][-1]][0]

    done = finish("ffn", grad_x)

    small_all = _all_gather(packed, "gather_small")
    conv_sum, loss_row, small_out = _adam_small(small_all, [(args[n], args["m_" + n], args["v_" + n]) for n, _ in REPL_W])
    finish("in", finish("mla", done))
    conv_grad = lax.dynamic_slice_in_dim(conv_sum, me * 192, 192, axis=1)
    conv_out = _adam(conv_grad[None], conv_sh, m_ssd_conv_w[0], v_ssd_conv_w[0], "adam_conv")
    small_map = {n: small_out[i] for i, (n, _) in enumerate(REPL_W)}

    def outputs(idx):
        res = []
        for n in WEIGHT_ORDER:
            if n == "ssd_conv_w":
                res.append(conv_out[idx][None])
            elif n in big_out:
                res.append((jnp.transpose(big_out[n][idx]) if n in TRANSPOSED else big_out[n][idx])[None])
            else:
                res.append(small_map[n][idx])
        return res

    return (loss_row[0, 0], grad_x[None], *outputs(0), *outputs(1), *outputs(2), *outputs(3))
```

```python
import functools
import math

import numpy as np
import jax
import jax.numpy as jnp
from jax import lax
from jax.experimental import pallas as pl
from jax.experimental.pallas import tpu as pltpu

F32 = jnp.float32
BF16 = jnp.bfloat16
HI = lax.Precision.HIGHEST

N_DEV = 8
D_MODEL = 1024
PLE_DIM = 256
SSD_HEADS = 16
SSD_HEAD_DIM = 64
SSD_INNER = 1024
SSD_STATE = 128
SSD_XBC = 1536
SSD_CHUNK = 128
MLA_HEADS = 16
MLA_Q_RANK = 384
MLA_KV_RANK = 256
MLA_NOPE = 64
MLA_ROPE = 32
MLA_V = 64
ROPE_BASE = 10000.0
D_FF = 2816
IN_WIDTH = 3248
IN_PAD = 3456
ALPHA = 2.0 ** 0.25
EPS = 1e-6
LN_EPS = 1e-5
ATT_SCALE = 1.0 / math.sqrt(MLA_NOPE + MLA_ROPE)
ADAM_LR, ADAM_B1, ADAM_B2, ADAM_EPS, ADAM_WD, ADAM_STEP = 0.001, 0.9, 0.999, 1e-08, 0.01, 10

LANE = 128
MXU_DIM = 256
MM_TM, MM_TN, MM_TK = 1408, 1408, 2048
ROW_TILE = 256
ATT_TQ = 256

GRAD_DT = BF16

BIG = ("w_in", "mla_w_q_b", "mla_w_kv_b", "w_out", "w_ffn_gate", "w_ffn_up", "w_ffn_down", "w_ple_gate", "w_ple_proj")
EARLY = ("w_in", "mla_w_q_b", "mla_w_kv_b")
LATE = ("w_out", "w_ffn_gate", "w_ffn_up", "w_ffn_down", "w_ple_gate", "w_ple_proj")
GRAD_GROUPS = {"ffn": ("w_ffn_gate", "w_ffn_up", "w_ffn_down", "w_ple_gate", "w_ple_proj"),
               "mla": ("w_out", "mla_w_q_b", "mla_w_kv_b"), "in": ("w_in",)}
ROW_SHARDED = ("w_out", "w_ffn_down", "w_ple_gate")
TRANSPOSED = ("w_in", "mla_w_q_b", "w_ffn_gate", "w_ffn_up")
WEIGHT_ORDER = ("w_in", "ssd_conv_w", "ssd_conv_b", "ssd_dt_bias", "ssd_A_log", "ssd_D", "ssd_norm_w", "mla_q_norm_w",
                "mla_w_q_b", "mla_kv_norm_w", "mla_w_kv_b", "mla_out_norm_w", "w_out", "ln_mix_g", "ln_mix_b",
                "w_ffn_gate", "w_ffn_up", "w_ffn_down", "w_ple_gate", "w_ple_proj", "ln_ffn_g", "ln_ffn_b")


def _tile(dim, cap, prefer=None):
    cands = [t for t in range(LANE, min(cap, dim) + 1, LANE) if dim % t == 0]
    if not cands:
        return dim
    if prefer is None:
        return max(cands)
    fill = lambda t: t / (MXU_DIM * -(-t // MXU_DIM))
    good = min(0.9, max(fill(t) for t in cands))
    return min((t for t in cands if fill(t) >= good), key=lambda t: abs(t - prefer))


def _dot(a, b, dims=(((1,), (0,)), ((), ())), precision=None):
    return lax.dot_general(a, b, dims, preferred_element_type=F32, precision=precision)


_NT = (((1,), (1,)), ((), ()))
_TN = (((0,), (0,)), ((), ()))


def _mm(a, b, *, ta=False, tb=False, add=None, out_dtype=F32, after=None, name):
    k_dim, m_dim = a.shape if ta else a.shape[::-1]
    n_dim, kb = b.shape if tb else b.shape[::-1]
    assert k_dim == kb
    tm, tn, tk = _tile(m_dim, MM_TM), _tile(n_dim, MM_TN, prefer=1024), _tile(k_dim, MM_TK, prefer=MM_TK)
    nk = k_dim // tk
    dims = (((0 if ta else 1,), (1 if tb else 0,)), ((), ()))
    has_add = add is not None
    a_spec = pl.BlockSpec((tk, tm), lambda i, j, k: (k, i)) if ta else pl.BlockSpec((tm, tk), lambda i, j, k: (i, k))
    b_spec = pl.BlockSpec((tn, tk), lambda i, j, k: (j, k)) if tb else pl.BlockSpec((tk, tn), lambda i, j, k: (k, j))
    o_spec = pl.BlockSpec((tm, tn), lambda i, j, k: (i, j))

    def body(*refs):
        refs = [r for i, r in enumerate(refs) if i != 2 + has_add] if after is not None else refs
        if has_add:
            a_ref, b_ref, add_ref, o_ref = refs[:4]
        else:
            a_ref, b_ref, o_ref = refs[:3]
        part = _dot(a_ref[...].astype(BF16), b_ref[...].astype(BF16), dims)
        if nk == 1:
            o_ref[...] = ((part + add_ref[...]) if has_add else part).astype(o_ref.dtype)
            return
        acc = refs[-1]
        k = pl.program_id(2)

        @pl.when(k == 0)
        def _():
            acc[...] = (part + add_ref[...]) if has_add else part

        @pl.when(k > 0)
        def _():
            acc[...] += part

        @pl.when(k == nk - 1)
        def _():
            o_ref[...] = acc[...].astype(o_ref.dtype)

    ins = [a, b] + ([add] if has_add else []) + ([after] if after is not None else [])
    specs = [a_spec, b_spec] + ([o_spec] if has_add else []) + ([pl.BlockSpec(memory_space=pl.ANY)] if after is not None else [])
    return pl.pallas_call(
        body, name=name, grid=(m_dim // tm, n_dim // tn, nk), in_specs=specs, out_specs=o_spec,
        out_shape=jax.ShapeDtypeStruct((m_dim, n_dim), out_dtype),
        scratch_shapes=[pltpu.VMEM((tm, tn), F32)] if nk > 1 else [],
        compiler_params=pltpu.CompilerParams(dimension_semantics=("parallel", "parallel", "arbitrary")),
    )(*ins)


def _rowwise(fn, rows, consts, out_widths, acc_widths=(), *, name, tr=ROW_TILE):
    row_arrays, row_specs = [], []
    first_arr = rows[0][0] if isinstance(rows[0], tuple) else rows[0]
    s_dim = first_arr.shape[-2]
    tr = min(tr, s_dim)
    for r in rows:
        arr, width, cb = r if isinstance(r, tuple) else (r, r.shape[-1], 0)
        row_arrays.append(arr)
        if arr.ndim == 3:
            row_specs.append(pl.BlockSpec((None, tr, width), functools.partial(lambda i, k: (k, i, 0), k=cb)))
        else:
            row_specs.append(pl.BlockSpec((tr, width), functools.partial(lambda i, cb: (i, cb), cb=cb)))
    const_specs = [pl.BlockSpec(c.shape, lambda i: (0, 0)) for c in consts]
    nr, nc, no, na = len(rows), len(consts), len(out_widths), len(acc_widths)

    def body(*refs):
        ins = [r[...] for r in refs[:nr + nc]]
        res = fn(*ins)
        if not isinstance(res, (tuple, list)):
            res = (res,)
        out_refs = refs[nr + nc:nr + nc + no]
        acc_refs = refs[nr + nc + no:]
        for o_ref, val in zip(out_refs, res[:no]):
            o_ref[...] = val.astype(o_ref.dtype)
        first = pl.program_id(0) == 0
        for a_ref, val in zip(acc_refs, res[no:]):
            @pl.when(first)
            def _(a_ref=a_ref, val=val):
                a_ref[...] = val

            @pl.when(jnp.logical_not(first))
            def _(a_ref=a_ref, val=val):
                a_ref[...] += val

    outs = [w if isinstance(w, tuple) else (w, F32) for w in out_widths]
    out_shape = [jax.ShapeDtypeStruct((s_dim, w), dt) for w, dt in outs]
    out_shape += [jax.ShapeDtypeStruct((1, w), F32) for w in acc_widths]
    out_specs = [pl.BlockSpec((tr, w), lambda i: (i, 0)) for w, _ in outs]
    out_specs += [pl.BlockSpec((1, w), lambda i: (0, 0)) for w in acc_widths]
    res = pl.pallas_call(
        body, name=name, grid=(s_dim // tr,), in_specs=row_specs + const_specs, out_specs=out_specs, out_shape=out_shape,
        compiler_params=pltpu.CompilerParams(dimension_semantics=("arbitrary",)),
    )(*row_arrays, *consts)
    return res


def _colsum(v):
    return jnp.sum(v, axis=0, keepdims=True)


def _rms(u, g):
    return u * lax.rsqrt(jnp.mean(u * u, axis=-1, keepdims=True) + EPS) * g


def _ln(u, g, b):
    mu = jnp.mean(u, axis=-1, keepdims=True)
    d = u - mu
    var = jnp.mean(d * d, axis=-1, keepdims=True)
    return d * lax.rsqrt(var + LN_EPS) * g + b


def _sigmoid(v):
    return 1.0 / (1.0 + jnp.exp(-v))


def _silu(v):
    return v * _sigmoid(v)


def _softplus(v):
    y = jnp.exp(-jnp.abs(v))
    w = 1.0 + y
    log1p = jnp.where(w == 1.0, y, jnp.log(w) * y / jnp.where(w == 1.0, 1.0, w - 1.0))
    return jnp.maximum(v, 0.0) + log1p


def _gate_rms(y, z, w):
    return _rms(y * _silu(z), w)


def _vjp_rows(f):
    def fn(*args):
        prim, ct = args[:-1], args[-1]
        _, pull = jax.vjp(f, *prim)
        return pull(ct)
    return fn


def _conv_pre(cur, prev, w, b, first):
    row = lax.broadcasted_iota(jnp.int32, cur.shape, 0)
    acc = cur * w[3:4, :] + b
    for j in (1, 2, 3):
        tail = jnp.where(first, 0.0, pltpu.roll(prev, j, 0))
        acc = acc + jnp.where(row >= j, pltpu.roll(cur, j, 0), tail) * w[3 - j:4 - j, :]
    return acc


def _conv_fwd(u, ucb, w, b, name="conv_fwd"):
    s_dim, width = u.shape[0], w.shape[1]
    tr = min(ROW_TILE, s_dim)

    def body(cur_ref, prev_ref, w_ref, b_ref, o_ref):
        pre = _conv_pre(cur_ref[...], prev_ref[...], w_ref, b_ref[...], pl.program_id(0) == 0)
        o_ref[...] = _silu(pre)

    return pl.pallas_call(
        body, name=name, grid=(s_dim // tr,),
        in_specs=[pl.BlockSpec((tr, width), lambda i: (i, ucb)),
                  pl.BlockSpec((tr, width), lambda i: (jnp.maximum(i - 1, 0), ucb)),
                  pl.BlockSpec(w.shape, lambda i: (0, 0)), pl.BlockSpec(b.shape, lambda i: (0, 0))],
        out_specs=pl.BlockSpec((tr, width), lambda i: (i, 0)), out_shape=jax.ShapeDtypeStruct((s_dim, width), F32),
        compiler_params=pltpu.CompilerParams(dimension_semantics=("arbitrary",)),
    )(u, u, w, b)


def _conv_bwd_pre(u, ucb, w, b, dact, name="conv_bwd_pre"):
    s_dim, width = u.shape[0], w.shape[1]
    tr = min(ROW_TILE, s_dim)

    def body(cur_ref, prev_ref, w_ref, b_ref, d_ref, da_ref, dw_ref, db_ref):
        first = pl.program_id(0) == 0
        cur, prev = cur_ref[...], prev_ref[...]
        pre = _conv_pre(cur, prev, w_ref, b_ref[...], first)
        sg = _sigmoid(pre)
        da = d_ref[...] * (sg * (1.0 + pre * (1.0 - sg)))
        da_ref[...] = da
        row = lax.broadcasted_iota(jnp.int32, cur.shape, 0)

        @pl.when(first)
        def _():
            dw_ref[...] = jnp.zeros_like(dw_ref)
            db_ref[...] = jnp.zeros_like(db_ref)

        db_ref[...] += _colsum(da)
        dw_ref[3:4, :] += _colsum(da * cur)
        for j in (1, 2, 3):
            tail = jnp.where(first, 0.0, pltpu.roll(prev, j, 0))
            sh = jnp.where(row >= j, pltpu.roll(cur, j, 0), tail)
            dw_ref[3 - j:4 - j, :] += _colsum(da * sh)

    return pl.pallas_call(
        body, name=name, grid=(s_dim // tr,),
        in_specs=[pl.BlockSpec((tr, width), lambda i: (i, ucb)),
                  pl.BlockSpec((tr, width), lambda i: (jnp.maximum(i - 1, 0), ucb)),
                  pl.BlockSpec(w.shape, lambda i: (0, 0)), pl.BlockSpec(b.shape, lambda i: (0, 0)),
                  pl.BlockSpec((tr, width), lambda i: (i, 0))],
        out_specs=[pl.BlockSpec((tr, width), lambda i: (i, 0)), pl.BlockSpec(w.shape, lambda i: (0, 0)),
                   pl.BlockSpec(b.shape, lambda i: (0, 0))],
        out_shape=[jax.ShapeDtypeStruct((s_dim, width), F32), jax.ShapeDtypeStruct(w.shape, F32),
                   jax.ShapeDtypeStruct(b.shape, F32)],
        compiler_params=pltpu.CompilerParams(dimension_semantics=("arbitrary",)),
    )(u, u, w, b, dact)


def _conv_bwd_in(da, w, name="conv_bwd_in"):
    s_dim, width = da.shape
    tr = min(ROW_TILE, s_dim)
    n = s_dim // tr

    def body(cur_ref, nxt_ref, w_ref, o_ref):
        last = pl.program_id(0) == n - 1
        cur, nxt = cur_ref[...], nxt_ref[...]
        row = lax.broadcasted_iota(jnp.int32, cur.shape, 0)
        acc = cur * w_ref[3:4, :]
        for j in (1, 2, 3):
            head = jnp.where(last, 0.0, pltpu.roll(nxt, tr - j, 0))
            acc = acc + jnp.where(row < tr - j, pltpu.roll(cur, tr - j, 0), head) * w_ref[3 - j:4 - j, :]
        o_ref[...] = acc.astype(o_ref.dtype)

    return pl.pallas_call(
        body, name=name, grid=(n,),
        in_specs=[pl.BlockSpec((tr, width), lambda i: (i, 0)), pl.BlockSpec((tr, width), lambda i: (jnp.minimum(i + 1, n - 1), 0)),
                  pl.BlockSpec(w.shape, lambda i: (0, 0))],
        out_specs=pl.BlockSpec((tr, width), lambda i: (i, 0)), out_shape=jax.ShapeDtypeStruct((s_dim, width), BF16),
        compiler_params=pltpu.CompilerParams(dimension_semantics=("arbitrary",)),
    )(da, da, w)


def _sel_dot(a, sel, pieces, dims=(((1,), (0,)), ((), ())), sel_left=False):
    sel = sel.astype(BF16)
    acc, rest = None, a
    for _ in range(pieces):
        piece = rest.astype(BF16)
        rest = rest - piece.astype(F32)
        part = _dot(sel, piece, dims) if sel_left else _dot(piece, sel, dims)
        acc = part if acc is None else acc + part
    return acc


def _ssd_consts():
    L = SSD_CHUNK
    tri = np.tril(np.ones((L, L), np.float32))
    expand = np.zeros((LANE, SSD_INNER), np.float32)
    expand128 = np.zeros((LANE, SSD_HEADS * LANE), np.float32)
    for h in range(SSD_HEADS):
        expand[h, h * SSD_HEAD_DIM:(h + 1) * SSD_HEAD_DIM] = 1.0
        expand128[h, h * LANE:(h + 1) * LANE] = 1.0
    return jnp.asarray(tri), jnp.asarray(expand), jnp.asarray(expand128), jnp.asarray(expand.T.copy())


def _ssd_prep(dt_ref, bias_ref, alog_ref, tri_ref, exp_ref, exp128_ref, cs_s, cst_s, ex_s, csx_s):
    L = SSD_CHUNK
    dt = _softplus(dt_ref[...] + bias_ref[...])
    a = -jnp.exp(alog_ref[...])
    cs = _sel_dot(dt * a, tri_ref[...], 3, sel_left=True)
    cs_s[...] = cs
    cst_s[...] = cs.T
    last = cs_s[L - 1:L, :]
    expand = exp_ref[...]
    ex_s[...] = _sel_dot(jnp.exp(cs), expand, 2)
    f_x = _sel_dot(jnp.exp(last - cs), expand, 2)
    dt_x = _sel_dot(dt, expand, 2)
    csx_s[...] = _sel_dot(cs, exp128_ref[...], 3)
    t_x = ex_s[L - 1:L, :]
    return dt, a, dt_x, f_x, t_x


def _decay_matrix(csx_s, cst_s, h, tril):
    seg = csx_s[:, h * LANE:(h + 1) * LANE] - cst_s[h:h + 1, :]
    return jnp.exp(jnp.where(tril, seg, -jnp.inf))


def _ssd_fwd(xbca, dtr, dtcb, bias, alog, d_x, name="ssd_fwd"):
    s_dim = xbca.shape[0]
    L = SSD_CHUNK
    nc = s_dim // L
    tri, expand, expand128, _ = _ssd_consts()

    def body(xs_ref, b_ref, c_ref, dt_ref, bias_ref, alog_ref, dx_ref, tri_ref, exp_ref, exp128_ref,
             y_ref, st_ref, st_s, cs_s, cst_s, ex_s, csx_s):
        @pl.when(pl.program_id(0) == 0)
        def _():
            st_s[...] = jnp.zeros_like(st_s)

        dt, a, dt_x, f_x, t_x = _ssd_prep(dt_ref, bias_ref, alog_ref, tri_ref, exp_ref, exp128_ref, cs_s, cst_s, ex_s, csx_s)
        st_ref[0] = st_s[...]
        row = lax.broadcasted_iota(jnp.int32, (L, L), 0)
        col = lax.broadcasted_iota(jnp.int32, (L, L), 1)
        tril = row >= col
        low = col < SSD_HEAD_DIM
        for g in range(2):
            bg = b_ref[:, g * LANE:(g + 1) * LANE]
            cg = c_ref[:, g * LANE:(g + 1) * LANE].astype(BF16)
            gmat = _dot(cg, bg.astype(BF16), _NT)
            bgt = bg.T.astype(BF16)
            for jj in range(4):
                j = 4 * g + jj
                sl = slice(j * LANE, (j + 1) * LANE)
                xp = xs_ref[:, sl]
                x_dt = xp * dt_x[:, sl]
                xb = x_dt.astype(BF16)
                yd = []
                for e in range(2):
                    lm = _decay_matrix(csx_s, cst_s, 2 * j + e, tril)
                    yd.append(_dot((gmat * lm).astype(BF16), xb))
                stp = st_s[j]
                z = _dot(cg, stp.astype(BF16))
                y_ref[:, sl] = jnp.where(low, yd[0], yd[1]) + ex_s[:, sl] * z + dx_ref[:, sl] * xp
                xf = (x_dt * f_x[:, sl]).astype(BF16)
                st_s[j] = t_x[:, sl] * stp + _dot(bgt, xf)

    const = lambda shape: pl.BlockSpec(shape, lambda c: tuple(0 for _ in shape))
    return pl.pallas_call(
        body, name=name, grid=(nc,),
        in_specs=[pl.BlockSpec((L, 1024), lambda c: (c, 0)), pl.BlockSpec((L, 256), lambda c: (c, 4)),
                  pl.BlockSpec((L, 256), lambda c: (c, 5)), pl.BlockSpec((L, LANE), lambda c: (c, dtcb)),
                  const((1, LANE)), const((1, LANE)), const((1, 1024)), const((L, L)), const((LANE, 1024)),
                  const((LANE, 2048))],
        out_specs=[pl.BlockSpec((L, 1024), lambda c: (c, 0)), pl.BlockSpec((1, 8, LANE, LANE), lambda c: (c, 0, 0, 0))],
        out_shape=[jax.ShapeDtypeStruct((s_dim, 1024), F32), jax.ShapeDtypeStruct((nc, 8, LANE, LANE), F32)],
        scratch_shapes=[pltpu.VMEM((8, LANE, LANE), F32), pltpu.VMEM((L, LANE), F32), pltpu.VMEM((LANE, L), F32),
                        pltpu.VMEM((L, 1024), F32), pltpu.VMEM((L, 2048), F32)],
        compiler_params=pltpu.CompilerParams(dimension_semantics=("arbitrary",)),
    )(xbca, xbca, xbca, dtr, bias, alog, d_x, tri, expand, expand128)


def _ssd_bwd(xbca, dtr, dtcb, bias, alog, d_x, states, dy, name="ssd_bwd"):
    s_dim = xbca.shape[0]
    L = SSD_CHUNK
    nc = s_dim // L
    tri, expand, expand128, expand_t = _ssd_consts()

    def body(xs_ref, b_ref, c_ref, dt_ref, bias_ref, alog_ref, dx_ref, tri_ref, exp_ref, exp128_ref, expt_ref,
             st_ref, dy_ref, dxbc_ref, ddt_ref, dbias_ref, dalog_ref, dd_ref,
             dst_s, cs_s, cst_s, ex_s, csx_s, dcsx_s, ddtx_s, dcol_s, drow_s, dlast_s, dd_s):
        @pl.when(pl.program_id(0) == 0)
        def _():
            dst_s[...] = jnp.zeros_like(dst_s)
            dbias_ref[...] = jnp.zeros_like(dbias_ref)
            dalog_ref[...] = jnp.zeros_like(dalog_ref)
            dd_s[...] = jnp.zeros_like(dd_s)

        dt, a, dt_x, f_x, t_x = _ssd_prep(dt_ref, bias_ref, alog_ref, tri_ref, exp_ref, exp128_ref, cs_s, cst_s, ex_s, csx_s)
        row = lax.broadcasted_iota(jnp.int32, (L, L), 0)
        col = lax.broadcasted_iota(jnp.int32, (L, L), 1)
        tril = row >= col
        low = col < SSD_HEAD_DIM
        dcol_s[...] = jnp.zeros_like(dcol_s)
        drow_s[...] = jnp.zeros_like(drow_s)
        for g in range(2):
            bg = b_ref[:, g * LANE:(g + 1) * LANE]
            cg = c_ref[:, g * LANE:(g + 1) * LANE]
            bgb, cgb = bg.astype(BF16), cg.astype(BF16)
            gmat = _dot(cgb, bgb, _NT)
            d_g = jnp.zeros((L, L), F32)
            d_b = jnp.zeros((L, LANE), F32)
            d_c = jnp.zeros((L, LANE), F32)
            for jj in range(4):
                j = 4 * g + jj
                sl = slice(j * LANE, (j + 1) * LANE)
                xp = xs_ref[:, sl]
                dtp = dt_x[:, sl]
                x_dt = xp * dtp
                xb = x_dt.astype(BF16)
                dyp = dy_ref[:, sl]
                dd_s[:, sl] += _colsum(dyp * xp)
                d_xdt = jnp.zeros((L, LANE), F32)
                for e in range(2):
                    h = 2 * j + e
                    lm = _decay_matrix(csx_s, cst_s, h, tril)
                    m = gmat * lm
                    dye = jnp.where(low if e == 0 else jnp.logical_not(low), dyp, 0.0).astype(BF16)
                    d_m = jnp.where(tril, _dot(dye, xb, _NT), 0.0)
                    d_xdt = d_xdt + _dot(m.astype(BF16), dye, _TN)
                    d_g = d_g + d_m * lm
                    w = d_m * m
                    dcol_s[...] += jnp.where(col == h, jnp.sum(w, axis=1, keepdims=True), 0.0)
                    drow_s[...] += jnp.where(row == h, jnp.sum(w, axis=0, keepdims=True), 0.0)
                stp = st_ref[0, j]
                stb = stp.astype(BF16)
                dstn = dst_s[j]
                dstb = dstn.astype(BF16)
                e_p = ex_s[:, sl]
                f_p = f_x[:, sl]
                t_p = t_x[:, sl]
                z = _dot(cgb, stb)
                d_z = (e_p * dyp).astype(BF16)
                d_c = d_c + _dot(d_z, stb, _NT)
                d_xf = _dot(bgb, dstb)
                d_b = d_b + _dot((x_dt * f_p).astype(BF16), dstb, _NT)
                d_xdt = d_xdt + f_p * d_xf
                d_f = x_dt * d_xf * f_p
                dcsx_s[:, sl] = dyp * e_p * z - d_f
                dlast_s[:, sl] = _colsum(d_f) + _colsum(dstn * stp) * t_p
                dst_s[j] = _dot(cgb, d_z, _TN) + t_p * dstn
                dxbc_ref[:, sl] = dx_ref[:, sl] * dyp + d_xdt * dtp
                ddtx_s[:, sl] = d_xdt * xp
            d_gb = d_g.astype(BF16)
            dxbc_ref[:, 1024 + g * LANE:1024 + (g + 1) * LANE] = d_b + _dot(d_gb, cgb, _TN)
            dxbc_ref[:, 1280 + g * LANE:1280 + (g + 1) * LANE] = d_c + _dot(d_gb, bgb)

        expt = expt_ref[...]
        dlast = _sel_dot(jnp.broadcast_to(dlast_s[...], (8, 1024)), expt, 3)
        d_cs = dcol_s[...] - drow_s[...].T + _sel_dot(dcsx_s[...], expt, 3)
        rown = lax.broadcasted_iota(jnp.int32, (L, LANE), 0)
        d_cs = d_cs + jnp.where(rown == L - 1, jnp.sum(dlast, axis=0, keepdims=True) * 0.125, 0.0)
        d_da = _sel_dot(d_cs, tri_ref[...], 3, _TN, sel_left=True)
        d_dt = d_da * a + _sel_dot(ddtx_s[...], expt, 3)
        dalog_ref[...] += _colsum(d_da * dt) * a
        d_raw = d_dt * _sigmoid(dt_ref[...] + bias_ref[...])
        ddt_ref[...] = d_raw.astype(ddt_ref.dtype)
        dbias_ref[...] += _colsum(d_raw)
        dd8 = _sel_dot(jnp.broadcast_to(dd_s[...], (8, 1024)), expt, 3)
        dd_ref[...] = jnp.sum(dd8, axis=0, keepdims=True) * 0.125

    const = lambda shape: pl.BlockSpec(shape, lambda c: tuple(0 for _ in shape))
    rev = lambda cb: (lambda c: (nc - 1 - c, cb))
    return pl.pallas_call(
        body, name=name, grid=(nc,),
        in_specs=[pl.BlockSpec((L, 1024), rev(0)), pl.BlockSpec((L, 256), rev(4)), pl.BlockSpec((L, 256), rev(5)),
                  pl.BlockSpec((L, LANE), rev(dtcb)), const((1, LANE)), const((1, LANE)), const((1, 1024)), const((L, L)),
                  const((LANE, 1024)), const((LANE, 2048)), const((1024, LANE)),
                  pl.BlockSpec((1, 8, LANE, LANE), lambda c: (nc - 1 - c, 0, 0, 0)), pl.BlockSpec((L, 1024), rev(0))],
        out_specs=[pl.BlockSpec((L, SSD_XBC), rev(0)), pl.BlockSpec((L, LANE), rev(0)), const((1, LANE)), const((1, LANE)),
                   const((1, LANE))],
        out_shape=[jax.ShapeDtypeStruct((s_dim, SSD_XBC), F32), jax.ShapeDtypeStruct((s_dim, LANE), BF16),
                   jax.ShapeDtypeStruct((1, LANE), F32), jax.ShapeDtypeStruct((1, LANE), F32),
                   jax.ShapeDtypeStruct((1, LANE), F32)],
        scratch_shapes=[pltpu.VMEM((8, LANE, LANE), F32), pltpu.VMEM((L, LANE), F32), pltpu.VMEM((LANE, L), F32),
                        pltpu.VMEM((L, 1024), F32), pltpu.VMEM((L, 2048), F32), pltpu.VMEM((L, 1024), F32),
                        pltpu.VMEM((L, 1024), F32), pltpu.VMEM((L, LANE), F32), pltpu.VMEM((LANE, L), F32),
                        pltpu.VMEM((1, 1024), F32), pltpu.VMEM((1, 1024), F32)],
        compiler_params=pltpu.CompilerParams(dimension_semantics=("arbitrary",)),
    )(xbca, xbca, xbca, dtr, bias, alog, d_x, tri, expand, expand128, expand_t, states, dy)


def _swap_halves(u):
    width = u.shape[1]
    lane = lax.broadcasted_iota(jnp.int32, u.shape, 1)
    return jnp.where(lane % MLA_ROPE < MLA_ROPE // 2, pltpu.roll(u, width - MLA_ROPE // 2, 1), pltpu.roll(u, MLA_ROPE // 2, 1))


def _rope_fwd_fn(u, cos, sin):
    return u * cos + _swap_halves(u) * sin


def _rope_bwd_fn(d, cos, sin):
    return d * cos + _swap_halves(d * sin)


def _spread4(v):
    return v + pltpu.roll(v, 32, 1) + pltpu.roll(v, 64, 1) + pltpu.roll(v, 96, 1)


def _att_masks(tq):
    lane = lax.broadcasted_iota(jnp.int32, (tq, LANE), 1)
    return lane // MLA_NOPE, lane // MLA_ROPE


def _att_tile(i, tq):
    klen = (i + 1) * tq
    qpos = i * tq + lax.broadcasted_iota(jnp.int32, (tq, klen), 0)
    kpos = lax.broadcasted_iota(jnp.int32, (tq, klen), 1)
    return slice(i * tq, (i + 1) * tq), klen, qpos >= kpos


def _att_qcat(qn_t, qr_t, par, e, half_id, grp_id):
    return jnp.concatenate([jnp.where(half_id == par, qn_t * ATT_SCALE, 0.0), jnp.where(grp_id == e, qr_t * ATT_SCALE, 0.0)],
                           axis=1).astype(BF16)


def _att_exp(qcat, kcat, causal):
    s = jnp.where(causal, _dot(qcat, kcat, _NT), -jnp.inf)
    e = jnp.exp(s - jnp.max(s, axis=1, keepdims=True))
    return e, 1.0 / jnp.sum(e, axis=1, keepdims=True)


def _att_specs(s_dim):
    col = lambda f: pl.BlockSpec((s_dim, LANE), lambda j: (0, f(j)))
    return [col(lambda j: j), col(lambda j: j // 2), col(lambda j: j), col(lambda j: 0), col(lambda j: 8 + j)]


def _att_fwd(q, qr, kv, krt, name="att_fwd"):
    s_dim = q.shape[0]
    tq = min(ATT_TQ, s_dim)

    def body(qn_ref, qr_ref, kn_ref, krt_ref, v_ref, o_ref, kcat_s, vb_s):
        e0 = 2 * (pl.program_id(0) % 2)
        half_id, grp_id = _att_masks(tq)
        kcat_s[...] = jnp.concatenate([kn_ref[...], krt_ref[...]], axis=1).astype(BF16)
        vb_s[...] = v_ref[...].astype(BF16)
        for i in range(s_dim // tq):
            rows, klen, causal = _att_tile(i, tq)
            qn_t, qr_t = qn_ref[rows, :], qr_ref[rows, :]
            outs = []
            for par in range(2):
                qcat = _att_qcat(qn_t, qr_t, par, e0 + par, half_id, grp_id)
                e, inv_l = _att_exp(qcat, kcat_s[0:klen, :], causal)
                outs.append(_dot(e.astype(BF16), vb_s[0:klen, :]) * inv_l)
            o_ref[rows, :] = jnp.where(half_id == 0, outs[0], outs[1])

    return pl.pallas_call(
        body, name=name, grid=(MLA_HEADS // 2,), in_specs=_att_specs(s_dim),
        out_specs=pl.BlockSpec((s_dim, LANE), lambda j: (0, j)), out_shape=jax.ShapeDtypeStruct((s_dim, 1024), F32),
        scratch_shapes=[pltpu.VMEM((s_dim, 2 * LANE), BF16), pltpu.VMEM((s_dim, LANE), BF16)],
        compiler_params=pltpu.CompilerParams(dimension_semantics=("parallel",)),
    )(q, qr, kv, krt, kv)


def _att_bwd(q, qr, kv, krt, o, do, name="att_bwd"):
    s_dim = q.shape[0]
    tq = min(ATT_TQ, s_dim)

    def body(qn_ref, qr_ref, kn_ref, krt_ref, v_ref, o_ref, do_ref, dqn_ref, dqr_ref, dkn_ref, dv_ref, dkrt_ref,
             kcat_s, vb_s):
        e0 = 2 * (pl.program_id(0) % 2)
        half_id, grp_id = _att_masks(tq)
        kcat_s[...] = jnp.concatenate([kn_ref[...], krt_ref[...]], axis=1).astype(BF16)
        vb_s[...] = v_ref[...].astype(BF16)
        dkn_ref[...] = jnp.zeros_like(dkn_ref)
        dv_ref[...] = jnp.zeros_like(dv_ref)
        dkrt_ref[...] = jnp.zeros_like(dkrt_ref)
        for i in range(s_dim // tq):
            rows, klen, causal = _att_tile(i, tq)
            qn_t, qr_t, o_t, do_t = qn_ref[rows, :], qr_ref[rows, :], o_ref[rows, :], do_ref[rows, :]
            dqn = jnp.zeros((tq, LANE), F32)
            dqr = jnp.zeros((tq, LANE), F32)
            for par in range(2):
                qcat = _att_qcat(qn_t, qr_t, par, e0 + par, half_id, grp_id)
                e, inv_l = _att_exp(qcat, kcat_s[0:klen, :], causal)
                p = e * inv_l
                dom = jnp.where(half_id == par, do_t, 0.0)
                domb = dom.astype(BF16)
                d_p = _dot(domb, vb_s[0:klen, :], _NT)
                d_row = jnp.sum(dom * o_t, axis=1, keepdims=True)
                d_s = (p * (d_p - d_row)).astype(BF16)
                dqcat = _dot(d_s, kcat_s[0:klen, :]) * ATT_SCALE
                dqn = dqn + jnp.where(half_id == par, dqcat[:, :LANE], 0.0)
                dqr = dqr + jnp.where(grp_id == e0 + par, dqcat[:, LANE:], 0.0)
                dkcat = _dot(d_s, qcat, _TN)
                dkn_ref[0:klen, :] += dkcat[:, :LANE]
                dkrt_ref[0:klen, :] += dkcat[:, LANE:]
                dv_ref[0:klen, :] += _dot(p.astype(BF16), domb, _TN)
            dqn_ref[rows, :] = dqn.astype(dqn_ref.dtype)
            dqr_ref[rows, :] = dqr

    col = lambda f: pl.BlockSpec((s_dim, LANE), lambda j: (0, f(j)))
    return pl.pallas_call(
        body, name=name, grid=(MLA_HEADS // 2,), in_specs=_att_specs(s_dim) + [col(lambda j: j), col(lambda j: j)],
        out_specs=[col(lambda j: j), pl.BlockSpec((None, s_dim, LANE), lambda j: (j % 2, 0, j // 2)), col(lambda j: j),
                   col(lambda j: j), pl.BlockSpec((None, s_dim, LANE), lambda j: (j, 0, 0))],
        out_shape=[jax.ShapeDtypeStruct((s_dim, 1024), BF16), jax.ShapeDtypeStruct((2, s_dim, 512), F32),
                   jax.ShapeDtypeStruct((s_dim, 1024), F32), jax.ShapeDtypeStruct((s_dim, 1024), F32),
                   jax.ShapeDtypeStruct((MLA_HEADS // 2, s_dim, LANE), F32)],
        scratch_shapes=[pltpu.VMEM((s_dim, 2 * LANE), BF16), pltpu.VMEM((s_dim, LANE), BF16)],
        compiler_params=pltpu.CompilerParams(dimension_semantics=("parallel",)),
    )(q, qr, kv, krt, kv, o, do)


def _all_gather(x, name):
    rows, width = x.shape

    def body(x_ref, out_ref, send_sems, recv_sems, local_sem):
        x_i, y_i, c_i = lax.axis_index("x"), lax.axis_index("y"), lax.axis_index("c")
        me, sibling = (x_i, y_i, c_i), (x_i, y_i, 1 - c_i)
        chips = [(1 - x_i, y_i), (x_i, 1 - y_i), (1 - x_i, 1 - y_i)]

        def slot(px, py, pc):
            return out_ref.at[4 * px + 2 * py + pc]

        def copy(k, block, to, src=None):
            return pltpu.make_async_remote_copy(
                src_ref=slot(*block) if src is None else src, dst_ref=slot(*block), send_sem=send_sems.at[k],
                recv_sem=recv_sems.at[k], device_id=to, device_id_type=pl.DeviceIdType.MESH)

        mine = pltpu.make_async_copy(x_ref, slot(*me), local_sem)
        mine.start()
        first = [copy(0, me, sibling, src=x_ref)]
        first += [copy(1 + j, me, (*chip, c_i), src=x_ref) for j, chip in enumerate(chips)]
        for cp in first:
            cp.start()
        passed = [copy(4 + j, (*chip, c_i), sibling) for j, chip in enumerate(chips)]
        for j, chip in enumerate(chips):
            copy(1 + j, (*chip, c_i), me).wait_recv()
            passed[j].start()
        copy(0, sibling, me).wait_recv()
        for j, chip in enumerate(chips):
            copy(4 + j, (*chip, 1 - c_i), me).wait_recv()
        for cp in first + passed:
            cp.wait_send()
        mine.wait()

    return pl.pallas_call(
        body, name=name, out_shape=jax.ShapeDtypeStruct((N_DEV, rows, width), x.dtype),
        in_specs=[pl.BlockSpec(memory_space=pl.ANY)], out_specs=pl.BlockSpec(memory_space=pl.ANY),
        scratch_shapes=[pltpu.SemaphoreType.DMA((7,)), pltpu.SemaphoreType.DMA((7,)), pltpu.SemaphoreType.DMA],
    )(x)


def _gather_many(shards, name):
    n_arr = len(shards)

    def body(*refs):
        x_refs, out_refs = refs[:n_arr], refs[n_arr:2 * n_arr]
        send_sems, recv_sems, local_sems = refs[2 * n_arr:]
        x_i, y_i, c_i = lax.axis_index("x"), lax.axis_index("y"), lax.axis_index("c")
        me, sibling = (x_i, y_i, c_i), (x_i, y_i, 1 - c_i)
        chips = [(1 - x_i, y_i), (x_i, 1 - y_i), (1 - x_i, 1 - y_i)]

        def copy(a, k, block, to, src=None):
            slot = out_refs[a].at[4 * block[0] + 2 * block[1] + block[2]]
            return pltpu.make_async_remote_copy(
                src_ref=slot if src is None else src, dst_ref=slot, send_sem=send_sems.at[a, k],
                recv_sem=recv_sems.at[a, k], device_id=to, device_id_type=pl.DeviceIdType.MESH)

        mine, first, passed = [], [], []
        for a in range(n_arr):
            mine.append(pltpu.make_async_copy(x_refs[a], out_refs[a].at[4 * x_i + 2 * y_i + c_i], local_sems.at[a]))
            mine[a].start()
            first.append([copy(a, 0, me, sibling, src=x_refs[a])]
                         + [copy(a, 1 + j, me, (*chip, c_i), src=x_refs[a]) for j, chip in enumerate(chips)])
            for cp in first[a]:
                cp.start()
            passed.append([copy(a, 4 + j, (*chip, c_i), sibling) for j, chip in enumerate(chips)])
        for j, chip in enumerate(chips):
            for a in range(n_arr):
                copy(a, 1 + j, (*chip, c_i), me).wait_recv()
                passed[a][j].start()
        for a in range(n_arr):
            copy(a, 0, sibling, me).wait_recv()
            for j, chip in enumerate(chips):
                copy(a, 4 + j, (*chip, 1 - c_i), me).wait_recv()
        for a in range(n_arr):
            for cp in first[a] + passed[a]:
                cp.wait_send()
            mine[a].wait()

    any_spec = pl.BlockSpec(memory_space=pl.ANY)
    return pl.pallas_call(
        body, name=name, out_shape=[jax.ShapeDtypeStruct((N_DEV,) + x.shape, x.dtype) for x in shards],
        in_specs=[any_spec] * n_arr, out_specs=[any_spec] * n_arr,
        scratch_shapes=[pltpu.SemaphoreType.DMA((n_arr, 7)), pltpu.SemaphoreType.DMA((n_arr, 7)),
                        pltpu.SemaphoreType.DMA((n_arr,))],
    )(*shards)


_HBM = pl.BlockSpec(memory_space=pltpu.HBM)
_SEM = pl.BlockSpec(memory_space=pltpu.SEMAPHORE)


def _plan_copies(plan, src_refs, land_refs, send_sems, recv_sems):
    copies = []
    for s_ref, l_ref in zip(src_refs, land_refs):
        for src, dst, peer in plan(s_ref, l_ref):
            k = len(copies)
            copies.append(pltpu.make_async_remote_copy(
                src_ref=src, dst_ref=dst, send_sem=send_sems.at[k], recv_sem=recv_sems.at[k], device_id=peer,
                device_id_type=pl.DeviceIdType.MESH))
    return copies


def _split_start(srcs, lands, plan, n_copy, name, after=None):
    n = len(srcs)
    n_in = 2 * n + (after is not None)

    def body(*refs):
        for cp in _plan_copies(plan, refs[:n], refs[n:2 * n], refs[n_in], refs[n_in + 1]):
            cp.start()
        refs[-1][...] = jnp.zeros_like(refs[-1])

    sems = pltpu.SemaphoreType.DMA((n * n_copy,))
    res = pl.pallas_call(
        body, name=name,
        out_shape=(sems, sems, *[pltpu.HBM(a.shape, a.dtype) for a in list(srcs) + list(lands)],
                   jax.ShapeDtypeStruct((8, LANE), F32)),
        in_specs=[_HBM] * (2 * n) + [pl.BlockSpec(memory_space=pl.ANY)] * (after is not None),
        out_specs=(_SEM, _SEM, *[_HBM] * (2 * n), pl.BlockSpec(memory_space=pltpu.VMEM)),
        input_output_aliases={i: 2 + i for i in range(2 * n)},
        compiler_params=pltpu.CompilerParams(has_side_effects=pltpu.SideEffectType.DATAFLOW_SIDE_EFFECTING),
    )(*[pltpu.with_memory_space_constraint(a, pltpu.HBM) for a in list(srcs) + list(lands)],
      *([after] if after is not None else []))
    return res[0], res[1], list(res[2:2 + n]), list(res[2 + n:2 + 2 * n]), res[-1]


def _split_wait(send_sems, recv_sems, srcs, lands, after, plan, name):
    n = len(srcs)

    def body(*refs):
        copies = _plan_copies(plan, refs[:n], refs[n:2 * n], refs[2 * n], refs[2 * n + 1])
        for cp in copies:
            cp.wait_send()
        for cp in copies:
            cp.wait_recv()

    res = pl.pallas_call(
        body, name=name, out_shape=tuple(pltpu.HBM(a.shape, a.dtype) for a in list(srcs) + list(lands)),
        in_specs=[_HBM] * (2 * n) + [_SEM, _SEM, pl.BlockSpec(memory_space=pl.ANY)], out_specs=tuple([_HBM] * (2 * n)),
        input_output_aliases={i: i for i in range(2 * n)},
        compiler_params=pltpu.CompilerParams(has_side_effects=pltpu.SideEffectType.DATAFLOW_SIDE_EFFECTING),
    )(*srcs, *lands, send_sems, recv_sems, after)
    return list(res[:n]), list(res[n:])


def _plan_broadcast(src, land):
    x_i, y_i, c_i = lax.axis_index("x"), lax.axis_index("y"), lax.axis_index("c")
    me = 4 * x_i + 2 * y_i + c_i
    return [(src, land.at[me], (x_i ^ (k >> 2), y_i ^ ((k >> 1) & 1), c_i ^ (k & 1))) for k in range(1, N_DEV)]


def _plan_scatter(src, land):
    x_i, y_i, c_i = lax.axis_index("x"), lax.axis_index("y"), lax.axis_index("c")
    me = 4 * x_i + 2 * y_i + c_i
    plan = []
    for k in range(1, N_DEV):
        px, py, pc = x_i ^ (k >> 2), y_i ^ ((k >> 1) & 1), c_i ^ (k & 1)
        plan.append((src.at[4 * px + 2 * py + pc], land.at[me], (px, py, pc)))
    return plan


def _adam_math(g, w, m, v):
    m_new = ADAM_B1 * m + (1.0 - ADAM_B1) * g
    v_new = ADAM_B2 * v + (1.0 - ADAM_B2) * (g * g)
    m_hat = m_new / (1.0 - ADAM_B1 ** ADAM_STEP)
    v_hat = v_new / (1.0 - ADAM_B2 ** ADAM_STEP)
    return -ADAM_LR * (m_hat / (jnp.sqrt(v_hat) + ADAM_EPS) + ADAM_WD * w), m_new, v_new


def _adam(slots, w, m, v, name, own=None, own_idx=None):
    n_slot, rows, cols = slots.shape
    tr = ROW_TILE if rows % ROW_TILE == 0 else rows
    has_own = own is not None

    def body(*refs):
        if has_own:
            idx_ref, own_ref, refs = refs[0], refs[1], refs[2:]
        s_ref, w_ref, m_ref, v_ref, g_ref, d_ref, mo_ref, vo_ref = refs
        g = own_ref[...].astype(F32) if has_own else s_ref[0].astype(F32)
        for k in range(0 if has_own else 1, n_slot):
            part = s_ref[k].astype(F32)
            g = g + (jnp.where(idx_ref[0] == k, 0.0, part) if has_own else part)
        g_ref[...] = g
        d_ref[...], mo_ref[...], vo_ref[...] = _adam_math(g, w_ref[...], m_ref[...], v_ref[...])

    spec = pl.BlockSpec((tr, cols), lambda i, *_: (i, 0))
    in_specs = [pl.BlockSpec((n_slot, tr, cols), lambda i, *_: (0, i, 0)), spec, spec, spec]
    if has_own:
        in_specs = [pl.BlockSpec((None, tr, cols), lambda i, idx: (idx[0], i, 0))] + in_specs
    grid_spec = pltpu.PrefetchScalarGridSpec(num_scalar_prefetch=1 if has_own else 0, grid=(rows // tr,), in_specs=in_specs,
                                             out_specs=[spec] * 4)
    ins = ([own_idx, own] if has_own else []) + [slots, w, m, v]
    return pl.pallas_call(
        body, name=name, grid_spec=grid_spec, out_shape=[jax.ShapeDtypeStruct((rows, cols), F32)] * 4,
        compiler_params=pltpu.CompilerParams(dimension_semantics=("parallel",)),
    )(*ins)


PACK_ROWS, PACK_W = 24, 1536
REPL_W = (("ssd_conv_b", 1536), ("ssd_dt_bias", 16), ("ssd_A_log", 16), ("ssd_D", 16), ("ssd_norm_w", 1024),
          ("mla_q_norm_w", 384), ("mla_kv_norm_w", 256), ("mla_out_norm_w", 1024), ("ln_mix_g", 1024),
          ("ln_mix_b", 1024), ("ln_ffn_g", 1024), ("ln_ffn_b", 1024))
LOSS_ROW = 4 + len(REPL_W)


def _pack_small(conv_w_grad, grads, loss, name="pack_small"):
    def body(*refs):
        cw_ref, g_refs, loss_ref, o_ref = refs[0], refs[1:1 + len(REPL_W)], refs[1 + len(REPL_W)], refs[-1]
        o_ref[...] = jnp.zeros_like(o_ref)
        o_ref[0:4, :] = cw_ref[...]
        for i, g_ref in enumerate(g_refs):
            o_ref[4 + i:5 + i, 0:g_ref.shape[1]] = g_ref[...]
        o_ref[LOSS_ROW:LOSS_ROW + 1, 0:LANE] = loss_ref[...]

    return pl.pallas_call(body, name=name, out_shape=jax.ShapeDtypeStruct((PACK_ROWS, PACK_W), F32))(conv_w_grad, *grads, loss)


def _adam_small(gathered, wmv, name="adam_small"):
    def body(*refs):
        s_ref = refs[0]
        in_refs = refs[1:1 + 3 * len(REPL_W)]
        cw_ref, loss_ref = refs[1 + 3 * len(REPL_W)], refs[2 + 3 * len(REPL_W)]
        out_refs = refs[3 + 3 * len(REPL_W):-1]
        tot = refs[-1]
        acc = s_ref[0]
        for k in range(1, N_DEV):
            acc = acc + s_ref[k]
        tot[...] = acc
        cw_ref[...] = tot[0:4, :]
        loss_ref[...] = tot[LOSS_ROW:LOSS_ROW + 1, 0:LANE]
        for i, (_, width) in enumerate(REPL_W):
            g = tot[4 + i:5 + i, 0:width]
            w_ref, m_ref, v_ref = in_refs[3 * i:3 * i + 3]
            g_ref, d_ref, mo_ref, vo_ref = out_refs[4 * i:4 * i + 4]
            g_ref[...] = g
            d_ref[...], mo_ref[...], vo_ref[...] = _adam_math(g, w_ref[...], m_ref[...], v_ref[...])

    flat_in = [a for triple in wmv for a in triple]
    out_shape = [jax.ShapeDtypeStruct((4, PACK_W), F32), jax.ShapeDtypeStruct((1, LANE), F32)]
    for _, width in REPL_W:
        out_shape += [jax.ShapeDtypeStruct((1, width), F32)] * 4
    res = pl.pallas_call(body, name=name, out_shape=out_shape, scratch_shapes=[pltpu.VMEM((PACK_ROWS, PACK_W), F32)])(
        gathered, *flat_in)
    return res[0], res[1], [res[2 + 4 * i:6 + 4 * i] for i in range(len(REPL_W))]


def _cols_full(g):
    return jnp.transpose(g, (1, 0, 2)).reshape(g.shape[1], -1)


def _cols_split(full):
    k_dim, n_dim = full.shape
    return jnp.transpose(full.reshape(k_dim, N_DEV, n_dim // N_DEV), (1, 0, 2))


PROJ_BLOCK = {"z": (1024, 0), "dt": (LANE, 8), "q_c": (MLA_Q_RANK, 3), "xbc": (SSD_XBC, 1), "kv_c": (MLA_KV_RANK, 12),
              "k_rope": (LANE, 26)}


def _win_pad(wt):
    z = lambda n: jnp.zeros((n, wt.shape[1]), wt.dtype)
    return jnp.concatenate([wt[:1024], wt[2560:2576], z(112), wt[2576:2960], wt[1024:2560], wt[2960:3216], wt[3216:3248],
                            z(96)], axis=0)


def _win_unpad(wt):
    return jnp.concatenate([wt[:1024], wt[1536:3072], wt[1024:1040], wt[1152:1536], wt[3072:3328], wt[3328:3360]], axis=0)


def _heads_split_t(wt, a, b):
    w3 = wt.reshape(MLA_HEADS, a + b, wt.shape[1])
    return jnp.concatenate([w3[:, :a].reshape(-1, wt.shape[1]), w3[:, a:].reshape(-1, wt.shape[1])], axis=0)


def _heads_merge_t(wt, a, b):
    wa = wt[:MLA_HEADS * a].reshape(MLA_HEADS, a, wt.shape[1])
    wb = wt[MLA_HEADS * a:].reshape(MLA_HEADS, b, wt.shape[1])
    return jnp.concatenate([wa, wb], axis=1).reshape(-1, wt.shape[1])


def _heads_split(w, a, b):
    k_dim = w.shape[0]
    w3 = w.reshape(k_dim, MLA_HEADS, a + b)
    return jnp.concatenate([w3[:, :, :a].reshape(k_dim, -1), w3[:, :, a:].reshape(k_dim, -1)], axis=1)


def _heads_merge(w, a, b):
    k_dim = w.shape[0]
    wa = w[:, :MLA_HEADS * a].reshape(k_dim, MLA_HEADS, a)
    wb = w[:, MLA_HEADS * a:].reshape(k_dim, MLA_HEADS, b)
    return jnp.concatenate([wa, wb], axis=2).reshape(k_dim, -1)


def _pad_lanes(v, width=LANE):
    return jnp.concatenate([v, jnp.zeros((v.shape[0], width - v.shape[1]), v.dtype)], axis=1)


def _local_step(x, p, positions, tgt, W, P, comm=None):
    comm = comm or {}
    zero_tok = jnp.zeros((8, LANE), F32)
    s_dim = x.shape[0]
    inv_freq = 1.0 / (ROPE_BASE ** (jnp.arange(0, MLA_ROPE, 2, dtype=F32) / MLA_ROPE))
    ang = positions.astype(F32)[:, None] * inv_freq
    cos, sin = jnp.cos(ang), jnp.sin(ang)
    cos32 = jnp.concatenate([cos, cos], axis=1)
    sin32 = jnp.concatenate([-sin, sin], axis=1)
    cos512, sin512 = jnp.tile(cos32, (1, 16)), jnp.tile(sin32, (1, 16))
    cos128, sin128 = jnp.tile(cos32, (1, 4)), jnp.tile(sin32, (1, 4))
    bias_p, alog_p = _pad_lanes(P["ssd_dt_bias"]), _pad_lanes(P["ssd_A_log"])
    d_x = jnp.repeat(P["ssd_D"], SSD_HEAD_DIM, axis=1)

    xb, pb = x.astype(BF16), p.astype(BF16)
    proj = _mm(xb, W["w_in"], tb=True, after=comm.get("token0", zero_tok), name="mm_in")
    z, qc, kvc, kr = [(proj,) + PROJ_BLOCK[n] for n in ("z", "q_c", "kv_c", "k_rope")]
    xbca = _conv_fwd(proj, PROJ_BLOCK["xbc"][1], P["ssd_conv_w"], P["ssd_conv_b"])
    y, states = _ssd_fwd(xbca, proj, PROJ_BLOCK["dt"][1], bias_p, alog_p, d_x)
    (yssd,) = _rowwise(_gate_rms, [y, z], [P["ssd_norm_w"]], [(1024, BF16)], name="ssd_gate_norm")
    (qn,) = _rowwise(_rms, [qc], [P["mla_q_norm_w"]], [(MLA_Q_RANK, BF16)], name="q_norm")
    (kvn,) = _rowwise(_rms, [kvc], [P["mla_kv_norm_w"]], [(MLA_KV_RANK, BF16)], name="kv_norm")
    q = _mm(qn, W["mla_w_q_b"], tb=True, name="mm_q")
    kv = _mm(kvn, W["mla_w_kv_b"], name="mm_kv")
    (qr,) = _rowwise(_rope_fwd_fn, [(q, 512, 2), cos512, sin512], [], [512], name="rope_q")
    (krt,) = _rowwise(lambda u, c, s: _spread4(_rope_fwd_fn(u, c, s)), [kr, cos128, sin128], [], [LANE], name="rope_k")
    att = _att_fwd(q, qr, kv, krt)
    (ymla,) = _rowwise(_rms, [att], [P["mla_out_norm_w"]], [(1024, BF16)], name="out_norm")
    ycat = jnp.concatenate([yssd, ymla], axis=1)
    if "late_weights" in comm:
        W = {**W, **comm["late_weights"](ycat)}
    mix = _mm(ycat, W["w_out"], name="mm_out")
    f_h1 = lambda xv, mv, g, b: _ln(ALPHA * xv + mv, g, b)
    h1, h1b = _rowwise(lambda *a: (f_h1(*a),) * 2, [x, mix], [P["ln_mix_g"], P["ln_mix_b"]], [1024, (1024, BF16)],
                       name="ln_mix")
    hg = _mm(h1b, W["w_ffn_gate"], tb=True, out_dtype=BF16, name="mm_gate")
    hu = _mm(h1b, W["w_ffn_up"], tb=True, out_dtype=BF16, name="mm_up")
    pg = _mm(h1b, W["w_ple_gate"], name="mm_ple_gate")
    pp = _mm(pb, W["w_ple_proj"], name="mm_ple")
    (act,) = _rowwise(lambda g, u: _silu(g.astype(F32)) * u.astype(F32), [hg, hu], [], [(D_FF, BF16)], name="swiglu")
    ffn = _mm(act, W["w_ffn_down"], name="mm_down")

    f_h2 = lambda hv, fv, pg, ppv, g, b: _ln(ALPHA * hv + fv + _sigmoid(pg) * ppv, g, b)

    def final_fn(hv, fv, pg, ppv, tv, g, b):
        h2, pull = jax.vjp(f_h2, hv, fv, pg, ppv, g, b)
        diff = h2 - tv
        loss = 0.5 * jnp.sum(jnp.mean(diff * diff, axis=-1, keepdims=True), axis=0, keepdims=True)
        d_h, d_f, d_pg, d_pp, d_g, d_b = pull(diff * (1.0 / D_MODEL))
        return d_h, d_f, d_pg, d_pp, d_g, d_b, jnp.broadcast_to(loss, (1, LANE))

    dh1_a, dffn, dpg, dpp, g_ffn_g, g_ffn_b, loss = _rowwise(
        final_fn, [h1, ffn, pg, pp, tgt], [P["ln_ffn_g"], P["ln_ffn_b"]], [1024] + [(1024, BF16)] * 3,
        [1024, 1024, LANE], name="final")

    G = {}
    dact = _mm(dffn, W["w_ffn_down"], tb=True, name="mm_down_dx")
    G["w_ffn_down"] = _mm(act, dffn, ta=True, out_dtype=GRAD_DT, name="mm_down_dw")

    def swiglu_bwd(g, u, d):
        g, u = g.astype(F32), u.astype(F32)
        sg = _sigmoid(g)
        return d * u * (sg * (1.0 + g * (1.0 - sg))), d * (g * sg)

    dg, du = _rowwise(swiglu_bwd, [hg, hu, dact], [], [(D_FF, BF16)] * 2, name="swiglu_bwd")
    dh1 = _mm(dg, W["w_ffn_gate"], add=dh1_a, name="mm_gate_dx")
    dh1 = _mm(du, W["w_ffn_up"], add=dh1, name="mm_up_dx")
    dh1 = _mm(dpg, W["w_ple_gate"], tb=True, add=dh1, name="mm_ple_gate_dx")
    G["w_ffn_gate"] = _mm(dg, h1b, ta=True, out_dtype=GRAD_DT, name="mm_gate_dw")
    G["w_ffn_up"] = _mm(du, h1b, ta=True, out_dtype=GRAD_DT, name="mm_up_dw")
    G["w_ple_gate"] = _mm(h1b, dpg, ta=True, out_dtype=GRAD_DT, name="mm_ple_gate_dw")
    G["w_ple_proj"] = _mm(pb, dpp, ta=True, out_dtype=GRAD_DT, name="mm_ple_dw")
    grads_done = comm.get("grads", lambda group, grads: zero_tok)
    tok1 = grads_done("ffn", G)
    dx_a, dmix, g_mix_g, g_mix_b = _rowwise(
        lambda xv, mv, dv, g, b, t: _vjp_rows(f_h1)(xv, mv, g, b, dv + jnp.min(t)), [x, mix, dh1],
        [P["ln_mix_g"], P["ln_mix_b"], tok1], [1024, (1024, BF16)], [1024, 1024], name="ln_mix_bwd")
    dycat = _mm(dmix, W["w_out"], tb=True, name="mm_out_dx")
    G["w_out"] = _mm(ycat, dmix, ta=True, out_dtype=GRAD_DT, name="mm_out_dw")

    datt, g_out_norm = _rowwise(lambda a, dv, w, t: _vjp_rows(_rms)(a, w, dv + jnp.min(t)), [att, (dycat, 1024, 1)],
                                [P["mla_out_norm_w"], tok1], [1024], [1024], name="out_norm_bwd")
    dqn_nope, dqr, dkn, dv, dkrt = _att_bwd(q, qr, kv, krt, att, datt)
    dkv = jnp.concatenate([dkn, dv], axis=1)
    (dq_rope,) = _rowwise(lambda d0, d1, c, s: _rope_bwd_fn(d0 + d1, c, s), [(dqr, 512, 0), (dqr, 512, 1), cos512, sin512],
                          [], [(512, BF16)], name="rope_q_bwd")

    def rope_k_bwd(*a):
        d = _spread4(functools.reduce(lambda u, w: u + w, a[:-2]))
        lane = lax.broadcasted_iota(jnp.int32, d.shape, 1)
        return _rope_bwd_fn(jnp.where(lane < MLA_ROPE, d, 0.0), a[-2], a[-1])

    (dkr,) = _rowwise(rope_k_bwd, [(dkrt, LANE, k) for k in range(MLA_HEADS // 2)] + [cos128, sin128], [], [(LANE, BF16)],
                      name="rope_k_bwd")
    dq = jnp.concatenate([dqn_nope, dq_rope], axis=1)
    dqn = _mm(dq, W["mla_w_q_b"], name="mm_q_dx")
    G["mla_w_q_b"] = _mm(dq, qn, ta=True, out_dtype=GRAD_DT, name="mm_q_dw")
    dkvn = _mm(dkv, W["mla_w_kv_b"], tb=True, name="mm_kv_dx")
    G["mla_w_kv_b"] = _mm(kvn, dkv, ta=True, out_dtype=GRAD_DT, name="mm_kv_dw")
    tok2 = grads_done("mla", G)
    dqc, g_q_norm = _rowwise(lambda a, dv, w, t: _vjp_rows(_rms)(a, w, dv + jnp.min(t)), [qc, dqn],
                             [P["mla_q_norm_w"], tok2], [(MLA_Q_RANK, BF16)], [MLA_Q_RANK], name="q_norm_bwd")
    dkvc, g_kv_norm = _rowwise(lambda a, dv, w: _vjp_rows(_rms)(a, w, dv), [kvc, dkvn], [P["mla_kv_norm_w"]],
                               [(MLA_KV_RANK, BF16)], [MLA_KV_RANK], name="kv_norm_bwd")

    dy, dz, g_ssd_norm = _rowwise(lambda yv, zv, dv, w, t: _vjp_rows(_gate_rms)(yv, zv, w, dv + jnp.min(t)),
                                  [y, z, (dycat, 1024, 0)], [P["ssd_norm_w"], tok1], [1024, (1024, BF16)], [1024],
                                  name="ssd_gate_norm_bwd")
    dxbca, ddtr, g_dt_bias, g_alog, g_d = _ssd_bwd(xbca, proj, PROJ_BLOCK["dt"][1], bias_p, alog_p, d_x, states, dy)
    da, g_conv_w, g_conv_b = _conv_bwd_pre(proj, PROJ_BLOCK["xbc"][1], P["ssd_conv_w"], P["ssd_conv_b"], dxbca)
    dxbc = _conv_bwd_in(da, P["ssd_conv_w"])

    small = {
        "ssd_conv_b": g_conv_b, "ssd_dt_bias": g_dt_bias, "ssd_A_log": g_alog, "ssd_D": g_d, "ssd_norm_w": g_ssd_norm,
        "mla_q_norm_w": g_q_norm, "mla_kv_norm_w": g_kv_norm, "mla_out_norm_w": g_out_norm, "ln_mix_g": g_mix_g,
        "ln_mix_b": g_mix_b, "ln_ffn_g": g_ffn_g, "ln_ffn_b": g_ffn_b,
    }
    packed = _pack_small(g_conv_w, [small[n] for n, _ in REPL_W], loss)
    if "small" in comm:
        comm["small"](packed)

    dproj = jnp.concatenate([dz, ddtr, dqc, dxbc, dkvc, dkr], axis=1)
    G["w_in"] = _mm(dproj, xb, ta=True, out_dtype=GRAD_DT, name="mm_in_dw")
    grad_x = _mm(dproj, W["w_in"], add=dx_a, after=grads_done("in", G), name="mm_in_dx")
    return grad_x, G, packed


def kernel(x, p, positions, w_in, ssd_conv_w, ssd_conv_b, ssd_dt_bias, ssd_A_log, ssd_D, ssd_norm_w, mla_q_norm_w, mla_w_q_b, mla_kv_norm_w, mla_w_kv_b, mla_out_norm_w, w_out, ln_mix_g, ln_mix_b, w_ffn_gate, w_ffn_up, w_ffn_down, w_ple_gate, w_ple_proj, ln_ffn_g, ln_ffn_b, loss_target, m_w_in, m_ssd_conv_w, m_ssd_conv_b, m_ssd_dt_bias, m_ssd_A_log, m_ssd_D, m_ssd_norm_w, m_mla_q_norm_w, m_mla_w_q_b, m_mla_kv_norm_w, m_mla_w_kv_b, m_mla_out_norm_w, m_w_out, m_ln_mix_g, m_ln_mix_b, m_w_ffn_gate, m_w_ffn_up, m_w_ffn_down, m_w_ple_gate, m_w_ple_proj, m_ln_ffn_g, m_ln_ffn_b, v_w_in, v_ssd_conv_w, v_ssd_conv_b, v_ssd_dt_bias, v_ssd_A_log, v_ssd_D, v_ssd_norm_w, v_mla_q_norm_w, v_mla_w_q_b, v_mla_kv_norm_w, v_mla_w_kv_b, v_mla_out_norm_w, v_w_out, v_ln_mix_g, v_ln_mix_b, v_w_ffn_gate, v_w_ffn_up, v_w_ffn_down, v_w_ple_gate, v_w_ple_proj, v_ln_ffn_g, v_ln_ffn_b):
    args = dict(locals())
    core = lax.axis_index("c")
    me = 4 * lax.axis_index("x") + 2 * lax.axis_index("y") + core

    conv_sh = ssd_conv_w[0]
    conv_hi = conv_sh.astype(BF16)
    conv_lo = (conv_sh - conv_hi.astype(F32)).astype(BF16)
    stored = lambda n, pre="": jnp.transpose(args[pre + n][0]) if n in TRANSPOSED else args[pre + n][0]
    shards = {n: stored(n).astype(BF16) for n in BIG}
    rows_full = lambda g: g.reshape(-1, g.shape[2])

    early = _gather_many([shards[n] for n in EARLY] + [jnp.concatenate([conv_hi, conv_lo], axis=0)], "gather_early")
    gw = dict(zip(EARLY, early[:-1]))
    conv_g = early[-1].astype(F32)
    W = {
        "w_in": _win_pad(rows_full(gw["w_in"])),
        "mla_w_q_b": _heads_split_t(rows_full(gw["mla_w_q_b"]), MLA_NOPE, MLA_ROPE),
        "mla_w_kv_b": _heads_split(_cols_full(gw["mla_w_kv_b"]), MLA_NOPE, MLA_V),
    }
    P = {n: args[n] for n, _ in REPL_W}
    P["ssd_conv_w"] = _cols_full(conv_g[:, :4] + conv_g[:, 4:])

    lands = [lax.dynamic_update_slice(lax.empty((N_DEV,) + shards[n].shape, BF16), shards[n][None], (me, 0, 0)) for n in LATE]
    late_sems = _split_start([shards[n] for n in LATE], lands, _plan_broadcast, N_DEV - 1, "gather_late_start",
                             after=early[0])

    def late_weights(after):
        _, got = _split_wait(*late_sems[:4], after, _plan_broadcast, "gather_late_wait")
        lw = dict(zip(LATE, got))
        return {"w_out": rows_full(lw["w_out"]), "w_ple_gate": rows_full(lw["w_ple_gate"]),
                "w_ple_proj": _cols_full(lw["w_ple_proj"]), "w_ffn_gate": rows_full(lw["w_ffn_gate"]),
                "w_ffn_up": rows_full(lw["w_ffn_up"]), "w_ffn_down": rows_full(lw["w_ffn_down"])}

    def to_blocks(n, g):
        if n == "w_in":
            g = _win_unpad(g)
        elif n == "mla_w_q_b":
            g = _heads_merge_t(g, MLA_NOPE, MLA_ROPE)
        elif n == "mla_w_kv_b":
            g = _heads_merge(g, MLA_NOPE, MLA_V)
        if n in ROW_SHARDED or n in TRANSPOSED:
            return g.reshape(N_DEV, -1, g.shape[1])
        return _cols_split(g)

    flight = {}

    def grads(group, G):
        gl = [to_blocks(n, G[n]) for n in GRAD_GROUPS[group]]
        flight[group] = _split_start(gl, [lax.empty(g.shape, g.dtype) for g in gl], _plan_scatter, N_DEV - 1,
                                     "grads_" + group + "_start", after=flight.get("small") if group == "in" else None)
        return flight[group][4]

    def small(packed):
        flight["small"] = _all_gather(packed, "gather_small")

    grad_x, G, packed = _local_step(x[0], p[0, 0], positions[0], loss_target[0], W, P,
                                    comm={"token0": late_sems[4], "late_weights": late_weights, "grads": grads, "small": small})

    me_arr = me.astype(jnp.int32).reshape(1)
    big_out = {}

    def finish(group, after):
        mine, recv = _split_wait(*flight[group][:4], after, _plan_scatter, "grads_" + group + "_wait")
        for n, g, r in zip(GRAD_GROUPS[group], mine, recv):
            big_out[n] = _adam(r, stored(n), stored(n, "m_"), stored(n, "v_"), "adam_" + n, own=g, own_idx=me_arr)
        return big_out[GRAD_GROUPS[group][-1]][0]

    done = finish("ffn", grad_x)
    conv_sum, loss_row, small_out = _adam_small(flight["small"], [(args[n], args["m_" + n], args["v_" + n]) for n, _ in REPL_W])
    finish("in", finish("mla", done))
    conv_grad = lax.dynamic_slice_in_dim(conv_sum, me * 192, 192, axis=1)
    conv_out = _adam(conv_grad[None], conv_sh, m_ssd_conv_w[0], v_ssd_conv_w[0], "adam_conv")
    small_map = {n: small_out[i] for i, (n, _) in enumerate(REPL_W)}

    def outputs(idx):
        res = []
        for n in WEIGHT_ORDER:
            if n == "ssd_conv_w":
                res.append(conv_out[idx][None])
            elif n in big_out:
                res.append((jnp.transpose(big_out[n][idx]) if n in TRANSPOSED else big_out[n][idx])[None])
            else:
                res.append(small_map[n][idx])
        return res

    return (loss_row[0, 0], grad_x[None], *outputs(0), *outputs(1), *outputs(2), *outputs(3))
```

```python
import functools
import math

import numpy as np
import jax
import jax.numpy as jnp
from jax import lax
from jax.experimental import pallas as pl
from jax.experimental.pallas import tpu as pltpu

F32 = jnp.float32
BF16 = jnp.bfloat16
HI = lax.Precision.HIGHEST

N_DEV = 8
D_MODEL = 1024
PLE_DIM = 256
SSD_HEADS = 16
SSD_HEAD_DIM = 64
SSD_INNER = 1024
SSD_STATE = 128
SSD_XBC = 1536
SSD_CHUNK = 128
MLA_HEADS = 16
MLA_Q_RANK = 384
MLA_KV_RANK = 256
MLA_NOPE = 64
MLA_ROPE = 32
MLA_V = 64
ROPE_BASE = 10000.0
D_FF = 2816
IN_WIDTH = 3248
IN_PAD = 3456
ALPHA = 2.0 ** 0.25
EPS = 1e-6
LN_EPS = 1e-5
ATT_SCALE = 1.0 / math.sqrt(MLA_NOPE + MLA_ROPE)
ADAM_LR, ADAM_B1, ADAM_B2, ADAM_EPS, ADAM_WD, ADAM_STEP = 0.001, 0.9, 0.999, 1e-08, 0.01, 10

LANE = 128
MXU_DIM = 256
MM_TM, MM_TN, MM_TK = 1408, 1408, 2048
ROW_TILE = 256
ATT_TQ = 256

GRAD_DT = BF16

BIG = ("w_in", "mla_w_q_b", "mla_w_kv_b", "w_out", "w_ffn_gate", "w_ffn_up", "w_ffn_down", "w_ple_gate", "w_ple_proj")
EARLY = ("w_in", "mla_w_q_b", "mla_w_kv_b")
LATE = ("w_out", "w_ffn_gate", "w_ffn_up", "w_ffn_down", "w_ple_gate", "w_ple_proj")
GRAD_GROUPS = {"ffn": ("w_ffn_gate", "w_ffn_up", "w_ffn_down", "w_ple_gate", "w_ple_proj", "w_out"),
               "mla": ("mla_w_q_b", "mla_w_kv_b"), "in": ("w_in",)}
ROW_SHARDED = ("w_out", "w_ffn_down", "w_ple_gate")
TRANSPOSED = ("w_in", "mla_w_q_b", "w_ffn_gate", "w_ffn_up")
WEIGHT_ORDER = ("w_in", "ssd_conv_w", "ssd_conv_b", "ssd_dt_bias", "ssd_A_log", "ssd_D", "ssd_norm_w", "mla_q_norm_w",
                "mla_w_q_b", "mla_kv_norm_w", "mla_w_kv_b", "mla_out_norm_w", "w_out", "ln_mix_g", "ln_mix_b",
                "w_ffn_gate", "w_ffn_up", "w_ffn_down", "w_ple_gate", "w_ple_proj", "ln_ffn_g", "ln_ffn_b")


def _tile(dim, cap, prefer=None):
    cands = [t for t in range(LANE, min(cap, dim) + 1, LANE) if dim % t == 0]
    if not cands:
        return dim
    if prefer is None:
        return max(cands)
    fill = lambda t: t / (MXU_DIM * -(-t // MXU_DIM))
    good = min(0.9, max(fill(t) for t in cands))
    return min((t for t in cands if fill(t) >= good), key=lambda t: abs(t - prefer))


def _dot(a, b, dims=(((1,), (0,)), ((), ())), precision=None):
    return lax.dot_general(a, b, dims, preferred_element_type=F32, precision=precision)


_NT = (((1,), (1,)), ((), ()))
_TN = (((0,), (0,)), ((), ()))


def _mm(a, b, *, ta=False, tb=False, add=None, out_dtype=F32, after=None, name):
    k_dim, m_dim = a.shape if ta else a.shape[::-1]
    n_dim, kb = b.shape if tb else b.shape[::-1]
    assert k_dim == kb
    tm, tn, tk = _tile(m_dim, MM_TM), _tile(n_dim, MM_TN, prefer=1024), _tile(k_dim, MM_TK, prefer=MM_TK)
    nk = k_dim // tk
    dims = (((0 if ta else 1,), (1 if tb else 0,)), ((), ()))
    has_add = add is not None
    a_spec = pl.BlockSpec((tk, tm), lambda i, j, k: (k, i)) if ta else pl.BlockSpec((tm, tk), lambda i, j, k: (i, k))
    b_spec = pl.BlockSpec((tn, tk), lambda i, j, k: (j, k)) if tb else pl.BlockSpec((tk, tn), lambda i, j, k: (k, j))
    o_spec = pl.BlockSpec((tm, tn), lambda i, j, k: (i, j))

    def body(*refs):
        refs = [r for i, r in enumerate(refs) if i != 2 + has_add] if after is not None else refs
        if has_add:
            a_ref, b_ref, add_ref, o_ref = refs[:4]
        else:
            a_ref, b_ref, o_ref = refs[:3]
        part = _dot(a_ref[...].astype(BF16), b_ref[...].astype(BF16), dims)
        if nk == 1:
            o_ref[...] = ((part + add_ref[...]) if has_add else part).astype(o_ref.dtype)
            return
        acc = refs[-1]
        k = pl.program_id(2)

        @pl.when(k == 0)
        def _():
            acc[...] = (part + add_ref[...]) if has_add else part

        @pl.when(k > 0)
        def _():
            acc[...] += part

        @pl.when(k == nk - 1)
        def _():
            o_ref[...] = acc[...].astype(o_ref.dtype)

    ins = [a, b] + ([add] if has_add else []) + ([after] if after is not None else [])
    specs = [a_spec, b_spec] + ([o_spec] if has_add else []) + ([pl.BlockSpec(memory_space=pl.ANY)] if after is not None else [])
    return pl.pallas_call(
        body, name=name, grid=(m_dim // tm, n_dim // tn, nk), in_specs=specs, out_specs=o_spec,
        out_shape=jax.ShapeDtypeStruct((m_dim, n_dim), out_dtype),
        scratch_shapes=[pltpu.VMEM((tm, tn), F32)] if nk > 1 else [],
        compiler_params=pltpu.CompilerParams(dimension_semantics=("parallel", "parallel", "arbitrary")),
    )(*ins)


def _rowwise(fn, rows, consts, out_widths, acc_widths=(), *, name, tr=ROW_TILE):
    row_arrays, row_specs = [], []
    first_arr = rows[0][0] if isinstance(rows[0], tuple) else rows[0]
    s_dim = first_arr.shape[-2]
    tr = min(tr, s_dim)
    for r in rows:
        arr, width, cb = r if isinstance(r, tuple) else (r, r.shape[-1], 0)
        row_arrays.append(arr)
        if arr.ndim == 3:
            row_specs.append(pl.BlockSpec((None, tr, width), functools.partial(lambda i, k: (k, i, 0), k=cb)))
        else:
            row_specs.append(pl.BlockSpec((tr, width), functools.partial(lambda i, cb: (i, cb), cb=cb)))
    const_specs = [pl.BlockSpec(c.shape, lambda i: (0, 0)) for c in consts]
    nr, nc, no, na = len(rows), len(consts), len(out_widths), len(acc_widths)

    def body(*refs):
        ins = [r[...] for r in refs[:nr + nc]]
        res = fn(*ins)
        if not isinstance(res, (tuple, list)):
            res = (res,)
        out_refs = refs[nr + nc:nr + nc + no]
        acc_refs = refs[nr + nc + no:]
        for o_ref, val in zip(out_refs, res[:no]):
            o_ref[...] = val.astype(o_ref.dtype)
        first = pl.program_id(0) == 0
        for a_ref, val in zip(acc_refs, res[no:]):
            @pl.when(first)
            def _(a_ref=a_ref, val=val):
                a_ref[...] = val

            @pl.when(jnp.logical_not(first))
            def _(a_ref=a_ref, val=val):
                a_ref[...] += val

    outs = [w if isinstance(w, tuple) else (w, F32) for w in out_widths]
    out_shape = [jax.ShapeDtypeStruct((s_dim, w), dt) for w, dt in outs]
    out_shape += [jax.ShapeDtypeStruct((1, w), F32) for w in acc_widths]
    out_specs = [pl.BlockSpec((tr, w), lambda i: (i, 0)) for w, _ in outs]
    out_specs += [pl.BlockSpec((1, w), lambda i: (0, 0)) for w in acc_widths]
    res = pl.pallas_call(
        body, name=name, grid=(s_dim // tr,), in_specs=row_specs + const_specs, out_specs=out_specs, out_shape=out_shape,
        compiler_params=pltpu.CompilerParams(dimension_semantics=("arbitrary",)),
    )(*row_arrays, *consts)
    return res


def _colsum(v):
    return jnp.sum(v, axis=0, keepdims=True)


def _rms(u, g):
    return u * lax.rsqrt(jnp.mean(u * u, axis=-1, keepdims=True) + EPS) * g


def _ln(u, g, b):
    mu = jnp.mean(u, axis=-1, keepdims=True)
    d = u - mu
    var = jnp.mean(d * d, axis=-1, keepdims=True)
    return d * lax.rsqrt(var + LN_EPS) * g + b


def _sigmoid(v):
    return 1.0 / (1.0 + jnp.exp(-v))


def _silu(v):
    return v * _sigmoid(v)


def _softplus(v):
    y = jnp.exp(-jnp.abs(v))
    w = 1.0 + y
    log1p = jnp.where(w == 1.0, y, jnp.log(w) * y / jnp.where(w == 1.0, 1.0, w - 1.0))
    return jnp.maximum(v, 0.0) + log1p


def _gate_rms(y, z, w):
    return _rms(y * _silu(z), w)


def _vjp_rows(f):
    def fn(*args):
        prim, ct = args[:-1], args[-1]
        _, pull = jax.vjp(f, *prim)
        return pull(ct)
    return fn


def _conv_pre(cur, prev, w, b, first):
    row = lax.broadcasted_iota(jnp.int32, cur.shape, 0)
    acc = cur * w[3:4, :] + b
    for j in (1, 2, 3):
        tail = jnp.where(first, 0.0, pltpu.roll(prev, j, 0))
        acc = acc + jnp.where(row >= j, pltpu.roll(cur, j, 0), tail) * w[3 - j:4 - j, :]
    return acc


def _conv_fwd(u, ucb, w, b, name="conv_fwd"):
    s_dim, width = u.shape[0], w.shape[1]
    tr = min(ROW_TILE, s_dim)

    def body(cur_ref, prev_ref, w_ref, b_ref, o_ref):
        pre = _conv_pre(cur_ref[...], prev_ref[...], w_ref, b_ref[...], pl.program_id(0) == 0)
        o_ref[...] = _silu(pre)

    return pl.pallas_call(
        body, name=name, grid=(s_dim // tr,),
        in_specs=[pl.BlockSpec((tr, width), lambda i: (i, ucb)),
                  pl.BlockSpec((tr, width), lambda i: (jnp.maximum(i - 1, 0), ucb)),
                  pl.BlockSpec(w.shape, lambda i: (0, 0)), pl.BlockSpec(b.shape, lambda i: (0, 0))],
        out_specs=pl.BlockSpec((tr, width), lambda i: (i, 0)), out_shape=jax.ShapeDtypeStruct((s_dim, width), F32),
        compiler_params=pltpu.CompilerParams(dimension_semantics=("arbitrary",)),
    )(u, u, w, b)


def _conv_bwd_pre(u, ucb, w, b, dact, name="conv_bwd_pre"):
    s_dim, width = u.shape[0], w.shape[1]
    tr = min(ROW_TILE, s_dim)

    def body(cur_ref, prev_ref, w_ref, b_ref, d_ref, da_ref, dw_ref, db_ref):
        first = pl.program_id(0) == 0
        cur, prev = cur_ref[...], prev_ref[...]
        pre = _conv_pre(cur, prev, w_ref, b_ref[...], first)
        sg = _sigmoid(pre)
        da = d_ref[...] * (sg * (1.0 + pre * (1.0 - sg)))
        da_ref[...] = da
        row = lax.broadcasted_iota(jnp.int32, cur.shape, 0)

        @pl.when(first)
        def _():
            dw_ref[...] = jnp.zeros_like(dw_ref)
            db_ref[...] = jnp.zeros_like(db_ref)

        db_ref[...] += _colsum(da)
        dw_ref[3:4, :] += _colsum(da * cur)
        for j in (1, 2, 3):
            tail = jnp.where(first, 0.0, pltpu.roll(prev, j, 0))
            sh = jnp.where(row >= j, pltpu.roll(cur, j, 0), tail)
            dw_ref[3 - j:4 - j, :] += _colsum(da * sh)

    return pl.pallas_call(
        body, name=name, grid=(s_dim // tr,),
        in_specs=[pl.BlockSpec((tr, width), lambda i: (i, ucb)),
                  pl.BlockSpec((tr, width), lambda i: (jnp.maximum(i - 1, 0), ucb)),
                  pl.BlockSpec(w.shape, lambda i: (0, 0)), pl.BlockSpec(b.shape, lambda i: (0, 0)),
                  pl.BlockSpec((tr, width), lambda i: (i, 0))],
        out_specs=[pl.BlockSpec((tr, width), lambda i: (i, 0)), pl.BlockSpec(w.shape, lambda i: (0, 0)),
                   pl.BlockSpec(b.shape, lambda i: (0, 0))],
        out_shape=[jax.ShapeDtypeStruct((s_dim, width), F32), jax.ShapeDtypeStruct(w.shape, F32),
                   jax.ShapeDtypeStruct(b.shape, F32)],
        compiler_params=pltpu.CompilerParams(dimension_semantics=("arbitrary",)),
    )(u, u, w, b, dact)


def _conv_bwd_in(da, w, name="conv_bwd_in"):
    s_dim, width = da.shape
    tr = min(ROW_TILE, s_dim)
    n = s_dim // tr

    def body(cur_ref, nxt_ref, w_ref, o_ref):
        last = pl.program_id(0) == n - 1
        cur, nxt = cur_ref[...], nxt_ref[...]
        row = lax.broadcasted_iota(jnp.int32, cur.shape, 0)
        acc = cur * w_ref[3:4, :]
        for j in (1, 2, 3):
            head = jnp.where(last, 0.0, pltpu.roll(nxt, tr - j, 0))
            acc = acc + jnp.where(row < tr - j, pltpu.roll(cur, tr - j, 0), head) * w_ref[3 - j:4 - j, :]
        o_ref[...] = acc.astype(o_ref.dtype)

    return pl.pallas_call(
        body, name=name, grid=(n,),
        in_specs=[pl.BlockSpec((tr, width), lambda i: (i, 0)), pl.BlockSpec((tr, width), lambda i: (jnp.minimum(i + 1, n - 1), 0)),
                  pl.BlockSpec(w.shape, lambda i: (0, 0))],
        out_specs=pl.BlockSpec((tr, width), lambda i: (i, 0)), out_shape=jax.ShapeDtypeStruct((s_dim, width), BF16),
        compiler_params=pltpu.CompilerParams(dimension_semantics=("arbitrary",)),
    )(da, da, w)


def _sel_dot(a, sel, pieces, dims=(((1,), (0,)), ((), ())), sel_left=False):
    sel = sel.astype(BF16)
    acc, rest = None, a
    for _ in range(pieces):
        piece = rest.astype(BF16)
        rest = rest - piece.astype(F32)
        part = _dot(sel, piece, dims) if sel_left else _dot(piece, sel, dims)
        acc = part if acc is None else acc + part
    return acc


def _ssd_consts():
    L = SSD_CHUNK
    tri = np.tril(np.ones((L, L), np.float32))
    expand = np.zeros((LANE, SSD_INNER), np.float32)
    expand128 = np.zeros((LANE, SSD_HEADS * LANE), np.float32)
    for h in range(SSD_HEADS):
        expand[h, h * SSD_HEAD_DIM:(h + 1) * SSD_HEAD_DIM] = 1.0
        expand128[h, h * LANE:(h + 1) * LANE] = 1.0
    return jnp.asarray(tri), jnp.asarray(expand), jnp.asarray(expand128), jnp.asarray(expand.T.copy())


def _ssd_prep(dt_ref, bias_ref, alog_ref, tri_ref, exp_ref, exp128_ref, cs_s, cst_s, ex_s, csx_s):
    L = SSD_CHUNK
    dt = _softplus(dt_ref[...] + bias_ref[...])
    a = -jnp.exp(alog_ref[...])
    cs = _sel_dot(dt * a, tri_ref[...], 3, sel_left=True)
    cs_s[...] = cs
    cst_s[...] = cs.T
    last = cs_s[L - 1:L, :]
    expand = exp_ref[...]
    ex_s[...] = _sel_dot(jnp.exp(cs), expand, 2)
    f_x = _sel_dot(jnp.exp(last - cs), expand, 2)
    dt_x = _sel_dot(dt, expand, 2)
    csx_s[...] = _sel_dot(cs, exp128_ref[...], 3)
    t_x = ex_s[L - 1:L, :]
    return dt, a, dt_x, f_x, t_x


def _decay_matrix(csx_s, cst_s, h, tril):
    seg = csx_s[:, h * LANE:(h + 1) * LANE] - cst_s[h:h + 1, :]
    return jnp.exp(jnp.where(tril, seg, -jnp.inf))


def _ssd_fwd(xbca, dtr, dtcb, bias, alog, d_x, name="ssd_fwd"):
    s_dim = xbca.shape[0]
    L = SSD_CHUNK
    nc = s_dim // L
    tri, expand, expand128, _ = _ssd_consts()

    def body(xs_ref, b_ref, c_ref, dt_ref, bias_ref, alog_ref, dx_ref, tri_ref, exp_ref, exp128_ref,
             y_ref, st_ref, st_s, cs_s, cst_s, ex_s, csx_s):
        @pl.when(pl.program_id(0) == 0)
        def _():
            st_s[...] = jnp.zeros_like(st_s)

        dt, a, dt_x, f_x, t_x = _ssd_prep(dt_ref, bias_ref, alog_ref, tri_ref, exp_ref, exp128_ref, cs_s, cst_s, ex_s, csx_s)
        st_ref[0] = st_s[...]
        row = lax.broadcasted_iota(jnp.int32, (L, L), 0)
        col = lax.broadcasted_iota(jnp.int32, (L, L), 1)
        tril = row >= col
        low = col < SSD_HEAD_DIM
        for g in range(2):
            bg = b_ref[:, g * LANE:(g + 1) * LANE]
            cg = c_ref[:, g * LANE:(g + 1) * LANE].astype(BF16)
            gmat = _dot(cg, bg.astype(BF16), _NT)
            bgt = bg.T.astype(BF16)
            for jj in range(4):
                j = 4 * g + jj
                sl = slice(j * LANE, (j + 1) * LANE)
                xp = xs_ref[:, sl]
                x_dt = xp * dt_x[:, sl]
                xb = x_dt.astype(BF16)
                yd = []
                for e in range(2):
                    lm = _decay_matrix(csx_s, cst_s, 2 * j + e, tril)
                    yd.append(_dot((gmat * lm).astype(BF16), xb))
                stp = st_s[j]
                z = _dot(cg, stp.astype(BF16))
                y_ref[:, sl] = jnp.where(low, yd[0], yd[1]) + ex_s[:, sl] * z + dx_ref[:, sl] * xp
                xf = (x_dt * f_x[:, sl]).astype(BF16)
                st_s[j] = t_x[:, sl] * stp + _dot(bgt, xf)

    const = lambda shape: pl.BlockSpec(shape, lambda c: tuple(0 for _ in shape))
    return pl.pallas_call(
        body, name=name, grid=(nc,),
        in_specs=[pl.BlockSpec((L, 1024), lambda c: (c, 0)), pl.BlockSpec((L, 256), lambda c: (c, 4)),
                  pl.BlockSpec((L, 256), lambda c: (c, 5)), pl.BlockSpec((L, LANE), lambda c: (c, dtcb)),
                  const((1, LANE)), const((1, LANE)), const((1, 1024)), const((L, L)), const((LANE, 1024)),
                  const((LANE, 2048))],
        out_specs=[pl.BlockSpec((L, 1024), lambda c: (c, 0)), pl.BlockSpec((1, 8, LANE, LANE), lambda c: (c, 0, 0, 0))],
        out_shape=[jax.ShapeDtypeStruct((s_dim, 1024), F32), jax.ShapeDtypeStruct((nc, 8, LANE, LANE), F32)],
        scratch_shapes=[pltpu.VMEM((8, LANE, LANE), F32), pltpu.VMEM((L, LANE), F32), pltpu.VMEM((LANE, L), F32),
                        pltpu.VMEM((L, 1024), F32), pltpu.VMEM((L, 2048), F32)],
        compiler_params=pltpu.CompilerParams(dimension_semantics=("arbitrary",)),
    )(xbca, xbca, xbca, dtr, bias, alog, d_x, tri, expand, expand128)


def _ssd_bwd(xbca, dtr, dtcb, bias, alog, d_x, states, dy, name="ssd_bwd"):
    s_dim = xbca.shape[0]
    L = SSD_CHUNK
    nc = s_dim // L
    tri, expand, expand128, expand_t = _ssd_consts()

    def body(xs_ref, b_ref, c_ref, dt_ref, bias_ref, alog_ref, dx_ref, tri_ref, exp_ref, exp128_ref, expt_ref,
             st_ref, dy_ref, dxbc_ref, ddt_ref, dbias_ref, dalog_ref, dd_ref,
             dst_s, cs_s, cst_s, ex_s, csx_s, dcsx_s, ddtx_s, dcol_s, drow_s, dlast_s, dd_s):
        @pl.when(pl.program_id(0) == 0)
        def _():
            dst_s[...] = jnp.zeros_like(dst_s)
            dbias_ref[...] = jnp.zeros_like(dbias_ref)
            dalog_ref[...] = jnp.zeros_like(dalog_ref)
            dd_s[...] = jnp.zeros_like(dd_s)

        dt, a, dt_x, f_x, t_x = _ssd_prep(dt_ref, bias_ref, alog_ref, tri_ref, exp_ref, exp128_ref, cs_s, cst_s, ex_s, csx_s)
        row = lax.broadcasted_iota(jnp.int32, (L, L), 0)
        col = lax.broadcasted_iota(jnp.int32, (L, L), 1)
        tril = row >= col
        low = col < SSD_HEAD_DIM
        dcol_s[...] = jnp.zeros_like(dcol_s)
        drow_s[...] = jnp.zeros_like(drow_s)
        for g in range(2):
            bg = b_ref[:, g * LANE:(g + 1) * LANE]
            cg = c_ref[:, g * LANE:(g + 1) * LANE]
            bgb, cgb = bg.astype(BF16), cg.astype(BF16)
            gmat = _dot(cgb, bgb, _NT)
            d_g = jnp.zeros((L, L), F32)
            d_b = jnp.zeros((L, LANE), F32)
            d_c = jnp.zeros((L, LANE), F32)
            for jj in range(4):
                j = 4 * g + jj
                sl = slice(j * LANE, (j + 1) * LANE)
                xp = xs_ref[:, sl]
                dtp = dt_x[:, sl]
                x_dt = xp * dtp
                xb = x_dt.astype(BF16)
                dyp = dy_ref[:, sl]
                dd_s[:, sl] += _colsum(dyp * xp)
                d_xdt = jnp.zeros((L, LANE), F32)
                for e in range(2):
                    h = 2 * j + e
                    lm = _decay_matrix(csx_s, cst_s, h, tril)
                    m = gmat * lm
                    dye = jnp.where(low if e == 0 else jnp.logical_not(low), dyp, 0.0).astype(BF16)
                    d_m = jnp.where(tril, _dot(dye, xb, _NT), 0.0)
                    d_xdt = d_xdt + _dot(m.astype(BF16), dye, _TN)
                    d_g = d_g + d_m * lm
                    w = d_m * m
                    dcol_s[...] += jnp.where(col == h, jnp.sum(w, axis=1, keepdims=True), 0.0)
                    drow_s[...] += jnp.where(row == h, jnp.sum(w, axis=0, keepdims=True), 0.0)
                stp = st_ref[0, j]
                stb = stp.astype(BF16)
                dstn = dst_s[j]
                dstb = dstn.astype(BF16)
                e_p = ex_s[:, sl]
                f_p = f_x[:, sl]
                t_p = t_x[:, sl]
                z = _dot(cgb, stb)
                d_z = (e_p * dyp).astype(BF16)
                d_c = d_c + _dot(d_z, stb, _NT)
                d_xf = _dot(bgb, dstb)
                d_b = d_b + _dot((x_dt * f_p).astype(BF16), dstb, _NT)
                d_xdt = d_xdt + f_p * d_xf
                d_f = x_dt * d_xf * f_p
                dcsx_s[:, sl] = dyp * e_p * z - d_f
                dlast_s[:, sl] = _colsum(d_f) + _colsum(dstn * stp) * t_p
                dst_s[j] = _dot(cgb, d_z, _TN) + t_p * dstn
                dxbc_ref[:, sl] = dx_ref[:, sl] * dyp + d_xdt * dtp
                ddtx_s[:, sl] = d_xdt * xp
            d_gb = d_g.astype(BF16)
            dxbc_ref[:, 1024 + g * LANE:1024 + (g + 1) * LANE] = d_b + _dot(d_gb, cgb, _TN)
            dxbc_ref[:, 1280 + g * LANE:1280 + (g + 1) * LANE] = d_c + _dot(d_gb, bgb)

        expt = expt_ref[...]
        dlast = _sel_dot(jnp.broadcast_to(dlast_s[...], (8, 1024)), expt, 3)
        d_cs = dcol_s[...] - drow_s[...].T + _sel_dot(dcsx_s[...], expt, 3)
        rown = lax.broadcasted_iota(jnp.int32, (L, LANE), 0)
        d_cs = d_cs + jnp.where(rown == L - 1, jnp.sum(dlast, axis=0, keepdims=True) * 0.125, 0.0)
        d_da = _sel_dot(d_cs, tri_ref[...], 3, _TN, sel_left=True)
        d_dt = d_da * a + _sel_dot(ddtx_s[...], expt, 3)
        dalog_ref[...] += _colsum(d_da * dt) * a
        d_raw = d_dt * _sigmoid(dt_ref[...] + bias_ref[...])
        ddt_ref[...] = d_raw.astype(ddt_ref.dtype)
        dbias_ref[...] += _colsum(d_raw)
        dd8 = _sel_dot(jnp.broadcast_to(dd_s[...], (8, 1024)), expt, 3)
        dd_ref[...] = jnp.sum(dd8, axis=0, keepdims=True) * 0.125

    const = lambda shape: pl.BlockSpec(shape, lambda c: tuple(0 for _ in shape))
    rev = lambda cb: (lambda c: (nc - 1 - c, cb))
    return pl.pallas_call(
        body, name=name, grid=(nc,),
        in_specs=[pl.BlockSpec((L, 1024), rev(0)), pl.BlockSpec((L, 256), rev(4)), pl.BlockSpec((L, 256), rev(5)),
                  pl.BlockSpec((L, LANE), rev(dtcb)), const((1, LANE)), const((1, LANE)), const((1, 1024)), const((L, L)),
                  const((LANE, 1024)), const((LANE, 2048)), const((1024, LANE)),
                  pl.BlockSpec((1, 8, LANE, LANE), lambda c: (nc - 1 - c, 0, 0, 0)), pl.BlockSpec((L, 1024), rev(0))],
        out_specs=[pl.BlockSpec((L, SSD_XBC), rev(0)), pl.BlockSpec((L, LANE), rev(0)), const((1, LANE)), const((1, LANE)),
                   const((1, LANE))],
        out_shape=[jax.ShapeDtypeStruct((s_dim, SSD_XBC), F32), jax.ShapeDtypeStruct((s_dim, LANE), BF16),
                   jax.ShapeDtypeStruct((1, LANE), F32), jax.ShapeDtypeStruct((1, LANE), F32),
                   jax.ShapeDtypeStruct((1, LANE), F32)],
        scratch_shapes=[pltpu.VMEM((8, LANE, LANE), F32), pltpu.VMEM((L, LANE), F32), pltpu.VMEM((LANE, L), F32),
                        pltpu.VMEM((L, 1024), F32), pltpu.VMEM((L, 2048), F32), pltpu.VMEM((L, 1024), F32),
                        pltpu.VMEM((L, 1024), F32), pltpu.VMEM((L, LANE), F32), pltpu.VMEM((LANE, L), F32),
                        pltpu.VMEM((1, 1024), F32), pltpu.VMEM((1, 1024), F32)],
        compiler_params=pltpu.CompilerParams(dimension_semantics=("arbitrary",)),
    )(xbca, xbca, xbca, dtr, bias, alog, d_x, tri, expand, expand128, expand_t, states, dy)


def _swap_halves(u):
    width = u.shape[1]
    lane = lax.broadcasted_iota(jnp.int32, u.shape, 1)
    return jnp.where(lane % MLA_ROPE < MLA_ROPE // 2, pltpu.roll(u, width - MLA_ROPE // 2, 1), pltpu.roll(u, MLA_ROPE // 2, 1))


def _rope_fwd_fn(u, cos, sin):
    return u * cos + _swap_halves(u) * sin


def _rope_bwd_fn(d, cos, sin):
    return d * cos + _swap_halves(d * sin)


def _spread4(v):
    return v + pltpu.roll(v, 32, 1) + pltpu.roll(v, 64, 1) + pltpu.roll(v, 96, 1)


def _att_masks(tq):
    lane = lax.broadcasted_iota(jnp.int32, (tq, LANE), 1)
    return lane // MLA_NOPE, lane // MLA_ROPE


def _att_tile(i, tq):
    klen = (i + 1) * tq
    qpos = i * tq + lax.broadcasted_iota(jnp.int32, (tq, klen), 0)
    kpos = lax.broadcasted_iota(jnp.int32, (tq, klen), 1)
    return slice(i * tq, (i + 1) * tq), klen, qpos >= kpos


def _att_qcat(qn_t, qr_t, par, e, half_id, grp_id):
    return jnp.concatenate([jnp.where(half_id == par, qn_t * ATT_SCALE, 0.0), jnp.where(grp_id == e, qr_t * ATT_SCALE, 0.0)],
                           axis=1).astype(BF16)


def _att_exp(qcat, kcat, causal):
    s = jnp.where(causal, _dot(qcat, kcat, _NT), -jnp.inf)
    e = jnp.exp(s - jnp.max(s, axis=1, keepdims=True))
    return e, 1.0 / jnp.sum(e, axis=1, keepdims=True)


def _att_specs(s_dim):
    col = lambda f: pl.BlockSpec((s_dim, LANE), lambda j: (0, f(j)))
    return [col(lambda j: j), col(lambda j: j // 2), col(lambda j: j), col(lambda j: 0), col(lambda j: 8 + j)]


def _att_fwd(q, qr, kv, krt, name="att_fwd"):
    s_dim = q.shape[0]
    tq = min(ATT_TQ, s_dim)

    def body(qn_ref, qr_ref, kn_ref, krt_ref, v_ref, o_ref, kcat_s, vb_s):
        e0 = 2 * (pl.program_id(0) % 2)
        half_id, grp_id = _att_masks(tq)
        kcat_s[...] = jnp.concatenate([kn_ref[...], krt_ref[...]], axis=1).astype(BF16)
        vb_s[...] = v_ref[...].astype(BF16)
        for i in range(s_dim // tq):
            rows, klen, causal = _att_tile(i, tq)
            qn_t, qr_t = qn_ref[rows, :], qr_ref[rows, :]
            outs = []
            for par in range(2):
                qcat = _att_qcat(qn_t, qr_t, par, e0 + par, half_id, grp_id)
                e, inv_l = _att_exp(qcat, kcat_s[0:klen, :], causal)
                outs.append(_dot(e.astype(BF16), vb_s[0:klen, :]) * inv_l)
            o_ref[rows, :] = jnp.where(half_id == 0, outs[0], outs[1])

    return pl.pallas_call(
        body, name=name, grid=(MLA_HEADS // 2,), in_specs=_att_specs(s_dim),
        out_specs=pl.BlockSpec((s_dim, LANE), lambda j: (0, j)), out_shape=jax.ShapeDtypeStruct((s_dim, 1024), F32),
        scratch_shapes=[pltpu.VMEM((s_dim, 2 * LANE), BF16), pltpu.VMEM((s_dim, LANE), BF16)],
        compiler_params=pltpu.CompilerParams(dimension_semantics=("parallel",)),
    )(q, qr, kv, krt, kv)


def _att_bwd(q, qr, kv, krt, o, do, name="att_bwd"):
    s_dim = q.shape[0]
    tq = min(ATT_TQ, s_dim)

    def body(qn_ref, qr_ref, kn_ref, krt_ref, v_ref, o_ref, do_ref, dqn_ref, dqr_ref, dkn_ref, dv_ref, dkrt_ref,
             kcat_s, vb_s):
        e0 = 2 * (pl.program_id(0) % 2)
        half_id, grp_id = _att_masks(tq)
        kcat_s[...] = jnp.concatenate([kn_ref[...], krt_ref[...]], axis=1).astype(BF16)
        vb_s[...] = v_ref[...].astype(BF16)
        dkn_ref[...] = jnp.zeros_like(dkn_ref)
        dv_ref[...] = jnp.zeros_like(dv_ref)
        dkrt_ref[...] = jnp.zeros_like(dkrt_ref)
        for i in range(s_dim // tq):
            rows, klen, causal = _att_tile(i, tq)
            qn_t, qr_t, o_t, do_t = qn_ref[rows, :], qr_ref[rows, :], o_ref[rows, :], do_ref[rows, :]
            dqn = jnp.zeros((tq, LANE), F32)
            dqr = jnp.zeros((tq, LANE), F32)
            for par in range(2):
                qcat = _att_qcat(qn_t, qr_t, par, e0 + par, half_id, grp_id)
                e, inv_l = _att_exp(qcat, kcat_s[0:klen, :], causal)
                p = e * inv_l
                dom = jnp.where(half_id == par, do_t, 0.0)
                domb = dom.astype(BF16)
                d_p = _dot(domb, vb_s[0:klen, :], _NT)
                d_row = jnp.sum(dom * o_t, axis=1, keepdims=True)
                d_s = (p * (d_p - d_row)).astype(BF16)
                dqcat = _dot(d_s, kcat_s[0:klen, :]) * ATT_SCALE
                dqn = dqn + jnp.where(half_id == par, dqcat[:, :LANE], 0.0)
                dqr = dqr + jnp.where(grp_id == e0 + par, dqcat[:, LANE:], 0.0)
                dkcat = _dot(d_s, qcat, _TN)
                dkn_ref[0:klen, :] += dkcat[:, :LANE]
                dkrt_ref[0:klen, :] += dkcat[:, LANE:]
                dv_ref[0:klen, :] += _dot(p.astype(BF16), domb, _TN)
            dqn_ref[rows, :] = dqn.astype(dqn_ref.dtype)
            dqr_ref[rows, :] = dqr

    col = lambda f: pl.BlockSpec((s_dim, LANE), lambda j: (0, f(j)))
    return pl.pallas_call(
        body, name=name, grid=(MLA_HEADS // 2,), in_specs=_att_specs(s_dim) + [col(lambda j: j), col(lambda j: j)],
        out_specs=[col(lambda j: j), pl.BlockSpec((None, s_dim, LANE), lambda j: (j % 2, 0, j // 2)), col(lambda j: j),
                   col(lambda j: j), pl.BlockSpec((None, s_dim, LANE), lambda j: (j, 0, 0))],
        out_shape=[jax.ShapeDtypeStruct((s_dim, 1024), BF16), jax.ShapeDtypeStruct((2, s_dim, 512), F32),
                   jax.ShapeDtypeStruct((s_dim, 1024), F32), jax.ShapeDtypeStruct((s_dim, 1024), F32),
                   jax.ShapeDtypeStruct((MLA_HEADS // 2, s_dim, LANE), F32)],
        scratch_shapes=[pltpu.VMEM((s_dim, 2 * LANE), BF16), pltpu.VMEM((s_dim, LANE), BF16)],
        compiler_params=pltpu.CompilerParams(dimension_semantics=("parallel",)),
    )(q, qr, kv, krt, kv, o, do)


def _all_gather(x, name):
    rows, width = x.shape

    def body(x_ref, out_ref, send_sems, recv_sems, local_sem):
        x_i, y_i, c_i = lax.axis_index("x"), lax.axis_index("y"), lax.axis_index("c")
        me, sibling = (x_i, y_i, c_i), (x_i, y_i, 1 - c_i)
        chips = [(1 - x_i, y_i), (x_i, 1 - y_i), (1 - x_i, 1 - y_i)]

        def slot(px, py, pc):
            return out_ref.at[4 * px + 2 * py + pc]

        def copy(k, block, to, src=None):
            return pltpu.make_async_remote_copy(
                src_ref=slot(*block) if src is None else src, dst_ref=slot(*block), send_sem=send_sems.at[k],
                recv_sem=recv_sems.at[k], device_id=to, device_id_type=pl.DeviceIdType.MESH)

        mine = pltpu.make_async_copy(x_ref, slot(*me), local_sem)
        mine.start()
        first = [copy(0, me, sibling, src=x_ref)]
        first += [copy(1 + j, me, (*chip, c_i), src=x_ref) for j, chip in enumerate(chips)]
        for cp in first:
            cp.start()
        passed = [copy(4 + j, (*chip, c_i), sibling) for j, chip in enumerate(chips)]
        for j, chip in enumerate(chips):
            copy(1 + j, (*chip, c_i), me).wait_recv()
            passed[j].start()
        copy(0, sibling, me).wait_recv()
        for j, chip in enumerate(chips):
            copy(4 + j, (*chip, 1 - c_i), me).wait_recv()
        for cp in first + passed:
            cp.wait_send()
        mine.wait()

    return pl.pallas_call(
        body, name=name, out_shape=jax.ShapeDtypeStruct((N_DEV, rows, width), x.dtype),
        in_specs=[pl.BlockSpec(memory_space=pl.ANY)], out_specs=pl.BlockSpec(memory_space=pl.ANY),
        scratch_shapes=[pltpu.SemaphoreType.DMA((7,)), pltpu.SemaphoreType.DMA((7,)), pltpu.SemaphoreType.DMA],
    )(x)


def _gather_many(shards, name):
    n_arr = len(shards)

    def body(*refs):
        x_refs, out_refs = refs[:n_arr], refs[n_arr:2 * n_arr]
        send_sems, recv_sems, local_sems = refs[2 * n_arr:]
        x_i, y_i, c_i = lax.axis_index("x"), lax.axis_index("y"), lax.axis_index("c")
        me, sibling = (x_i, y_i, c_i), (x_i, y_i, 1 - c_i)
        chips = [(1 - x_i, y_i), (x_i, 1 - y_i), (1 - x_i, 1 - y_i)]

        def copy(a, k, block, to, src=None):
            slot = out_refs[a].at[4 * block[0] + 2 * block[1] + block[2]]
            return pltpu.make_async_remote_copy(
                src_ref=slot if src is None else src, dst_ref=slot, send_sem=send_sems.at[a, k],
                recv_sem=recv_sems.at[a, k], device_id=to, device_id_type=pl.DeviceIdType.MESH)

        mine, first, passed = [], [], []
        for a in range(n_arr):
            mine.append(pltpu.make_async_copy(x_refs[a], out_refs[a].at[4 * x_i + 2 * y_i + c_i], local_sems.at[a]))
            mine[a].start()
            first.append([copy(a, 0, me, sibling, src=x_refs[a])]
                         + [copy(a, 1 + j, me, (*chip, c_i), src=x_refs[a]) for j, chip in enumerate(chips)])
            for cp in first[a]:
                cp.start()
            passed.append([copy(a, 4 + j, (*chip, c_i), sibling) for j, chip in enumerate(chips)])
        for j, chip in enumerate(chips):
            for a in range(n_arr):
                copy(a, 1 + j, (*chip, c_i), me).wait_recv()
                passed[a][j].start()
        for a in range(n_arr):
            copy(a, 0, sibling, me).wait_recv()
            for j, chip in enumerate(chips):
                copy(a, 4 + j, (*chip, 1 - c_i), me).wait_recv()
        for a in range(n_arr):
            for cp in first[a] + passed[a]:
                cp.wait_send()
            mine[a].wait()

    any_spec = pl.BlockSpec(memory_space=pl.ANY)
    return pl.pallas_call(
        body, name=name, out_shape=[jax.ShapeDtypeStruct((N_DEV,) + x.shape, x.dtype) for x in shards],
        in_specs=[any_spec] * n_arr, out_specs=[any_spec] * n_arr,
        scratch_shapes=[pltpu.SemaphoreType.DMA((n_arr, 7)), pltpu.SemaphoreType.DMA((n_arr, 7)),
                        pltpu.SemaphoreType.DMA((n_arr,))],
    )(*shards)


_HBM = pl.BlockSpec(memory_space=pltpu.HBM)
_SEM = pl.BlockSpec(memory_space=pltpu.SEMAPHORE)


def _plan_copies(plan, src_refs, land_refs, send_sems, recv_sems):
    copies = []
    for s_ref, l_ref in zip(src_refs, land_refs):
        for src, dst, peer in plan(s_ref, l_ref):
            k = len(copies)
            copies.append(pltpu.make_async_remote_copy(
                src_ref=src, dst_ref=dst, send_sem=send_sems.at[k], recv_sem=recv_sems.at[k], device_id=peer,
                device_id_type=pl.DeviceIdType.MESH))
    return copies


def _split_start(srcs, lands, plan, n_copy, name, after=None):
    n = len(srcs)
    n_in = 2 * n + (after is not None)

    def body(*refs):
        for cp in _plan_copies(plan, refs[:n], refs[n:2 * n], refs[n_in], refs[n_in + 1]):
            cp.start()
        refs[-1][...] = jnp.zeros_like(refs[-1])

    sems = pltpu.SemaphoreType.DMA((n * n_copy,))
    res = pl.pallas_call(
        body, name=name,
        out_shape=(sems, sems, *[pltpu.HBM(a.shape, a.dtype) for a in list(srcs) + list(lands)],
                   jax.ShapeDtypeStruct((8, LANE), F32)),
        in_specs=[_HBM] * (2 * n) + [pl.BlockSpec(memory_space=pl.ANY)] * (after is not None),
        out_specs=(_SEM, _SEM, *[_HBM] * (2 * n), pl.BlockSpec(memory_space=pltpu.VMEM)),
        input_output_aliases={i: 2 + i for i in range(2 * n)},
        compiler_params=pltpu.CompilerParams(has_side_effects=pltpu.SideEffectType.DATAFLOW_SIDE_EFFECTING),
    )(*[pltpu.with_memory_space_constraint(a, pltpu.HBM) for a in list(srcs) + list(lands)],
      *([after] if after is not None else []))
    return res[0], res[1], list(res[2:2 + n]), list(res[2 + n:2 + 2 * n]), res[-1]


def _split_wait(send_sems, recv_sems, srcs, lands, after, plan, name):
    n = len(srcs)

    def body(*refs):
        copies = _plan_copies(plan, refs[:n], refs[n:2 * n], refs[2 * n], refs[2 * n + 1])
        for cp in copies:
            cp.wait_send()
        for cp in copies:
            cp.wait_recv()

    res = pl.pallas_call(
        body, name=name, out_shape=tuple(pltpu.HBM(a.shape, a.dtype) for a in list(srcs) + list(lands)),
        in_specs=[_HBM] * (2 * n) + [_SEM, _SEM, pl.BlockSpec(memory_space=pl.ANY)], out_specs=tuple([_HBM] * (2 * n)),
        input_output_aliases={i: i for i in range(2 * n)},
        compiler_params=pltpu.CompilerParams(has_side_effects=pltpu.SideEffectType.DATAFLOW_SIDE_EFFECTING),
    )(*srcs, *lands, send_sems, recv_sems, after)
    return list(res[:n]), list(res[n:])


def _plan_broadcast(src, land):
    x_i, y_i, c_i = lax.axis_index("x"), lax.axis_index("y"), lax.axis_index("c")
    me = 4 * x_i + 2 * y_i + c_i
    return [(src, land.at[me], (x_i ^ (k >> 2), y_i ^ ((k >> 1) & 1), c_i ^ (k & 1))) for k in range(1, N_DEV)]


def _plan_scatter(src, land):
    x_i, y_i, c_i = lax.axis_index("x"), lax.axis_index("y"), lax.axis_index("c")
    me = 4 * x_i + 2 * y_i + c_i
    plan = []
    for k in range(1, N_DEV):
        px, py, pc = x_i ^ (k >> 2), y_i ^ ((k >> 1) & 1), c_i ^ (k & 1)
        plan.append((src.at[4 * px + 2 * py + pc], land.at[me], (px, py, pc)))
    return plan


def _adam_math(g, w, m, v):
    m_new = ADAM_B1 * m + (1.0 - ADAM_B1) * g
    v_new = ADAM_B2 * v + (1.0 - ADAM_B2) * (g * g)
    m_hat = m_new / (1.0 - ADAM_B1 ** ADAM_STEP)
    v_hat = v_new / (1.0 - ADAM_B2 ** ADAM_STEP)
    return -ADAM_LR * (m_hat / (jnp.sqrt(v_hat) + ADAM_EPS) + ADAM_WD * w), m_new, v_new


def _adam(slots, w, m, v, name, own=None, own_idx=None):
    n_slot, rows, cols = slots.shape
    tr = ROW_TILE if rows % ROW_TILE == 0 else rows
    has_own = own is not None

    def body(*refs):
        if has_own:
            idx_ref, own_ref, refs = refs[0], refs[1], refs[2:]
        s_ref, w_ref, m_ref, v_ref, g_ref, d_ref, mo_ref, vo_ref = refs
        g = own_ref[...].astype(F32) if has_own else s_ref[0].astype(F32)
        for k in range(0 if has_own else 1, n_slot):
            part = s_ref[k].astype(F32)
            g = g + (jnp.where(idx_ref[0] == k, 0.0, part) if has_own else part)
        g_ref[...] = g
        d_ref[...], mo_ref[...], vo_ref[...] = _adam_math(g, w_ref[...], m_ref[...], v_ref[...])

    spec = pl.BlockSpec((tr, cols), lambda i, *_: (i, 0))
    in_specs = [pl.BlockSpec((n_slot, tr, cols), lambda i, *_: (0, i, 0)), spec, spec, spec]
    if has_own:
        in_specs = [pl.BlockSpec((None, tr, cols), lambda i, idx: (idx[0], i, 0))] + in_specs
    grid_spec = pltpu.PrefetchScalarGridSpec(num_scalar_prefetch=1 if has_own else 0, grid=(rows // tr,), in_specs=in_specs,
                                             out_specs=[spec] * 4)
    ins = ([own_idx, own] if has_own else []) + [slots, w, m, v]
    return pl.pallas_call(
        body, name=name, grid_spec=grid_spec, out_shape=[jax.ShapeDtypeStruct((rows, cols), F32)] * 4,
        compiler_params=pltpu.CompilerParams(dimension_semantics=("parallel",)),
    )(*ins)


PACK_ROWS, PACK_W = 24, 1536
REPL_W = (("ssd_conv_b", 1536), ("ssd_dt_bias", 16), ("ssd_A_log", 16), ("ssd_D", 16), ("ssd_norm_w", 1024),
          ("mla_q_norm_w", 384), ("mla_kv_norm_w", 256), ("mla_out_norm_w", 1024), ("ln_mix_g", 1024),
          ("ln_mix_b", 1024), ("ln_ffn_g", 1024), ("ln_ffn_b", 1024))
LOSS_ROW = 4 + len(REPL_W)


def _pack_small(conv_w_grad, grads, loss, name="pack_small"):
    def body(*refs):
        cw_ref, g_refs, loss_ref, o_ref = refs[0], refs[1:1 + len(REPL_W)], refs[1 + len(REPL_W)], refs[-1]
        o_ref[...] = jnp.zeros_like(o_ref)
        o_ref[0:4, :] = cw_ref[...]
        for i, g_ref in enumerate(g_refs):
            o_ref[4 + i:5 + i, 0:g_ref.shape[1]] = g_ref[...]
        o_ref[LOSS_ROW:LOSS_ROW + 1, 0:LANE] = loss_ref[...]

    return pl.pallas_call(body, name=name, out_shape=jax.ShapeDtypeStruct((PACK_ROWS, PACK_W), F32))(conv_w_grad, *grads, loss)


def _adam_small(gathered, wmv, name="adam_small"):
    def body(*refs):
        s_ref = refs[0]
        in_refs = refs[1:1 + 3 * len(REPL_W)]
        cw_ref, loss_ref = refs[1 + 3 * len(REPL_W)], refs[2 + 3 * len(REPL_W)]
        out_refs = refs[3 + 3 * len(REPL_W):-1]
        tot = refs[-1]
        acc = s_ref[0]
        for k in range(1, N_DEV):
            acc = acc + s_ref[k]
        tot[...] = acc
        cw_ref[...] = tot[0:4, :]
        loss_ref[...] = tot[LOSS_ROW:LOSS_ROW + 1, 0:LANE]
        for i, (_, width) in enumerate(REPL_W):
            g = tot[4 + i:5 + i, 0:width]
            w_ref, m_ref, v_ref = in_refs[3 * i:3 * i + 3]
            g_ref, d_ref, mo_ref, vo_ref = out_refs[4 * i:4 * i + 4]
            g_ref[...] = g
            d_ref[...], mo_ref[...], vo_ref[...] = _adam_math(g, w_ref[...], m_ref[...], v_ref[...])

    flat_in = [a for triple in wmv for a in triple]
    out_shape = [jax.ShapeDtypeStruct((4, PACK_W), F32), jax.ShapeDtypeStruct((1, LANE), F32)]
    for _, width in REPL_W:
        out_shape += [jax.ShapeDtypeStruct((1, width), F32)] * 4
    res = pl.pallas_call(body, name=name, out_shape=out_shape, scratch_shapes=[pltpu.VMEM((PACK_ROWS, PACK_W), F32)])(
        gathered, *flat_in)
    return res[0], res[1], [res[2 + 4 * i:6 + 4 * i] for i in range(len(REPL_W))]


def _cols_full(g):
    return jnp.transpose(g, (1, 0, 2)).reshape(g.shape[1], -1)


def _cols_split(full):
    k_dim, n_dim = full.shape
    return jnp.transpose(full.reshape(k_dim, N_DEV, n_dim // N_DEV), (1, 0, 2))


PROJ_BLOCK = {"z": (1024, 0), "dt": (LANE, 8), "q_c": (MLA_Q_RANK, 3), "xbc": (SSD_XBC, 1), "kv_c": (MLA_KV_RANK, 12),
              "k_rope": (LANE, 26)}


def _win_pad(wt):
    z = lambda n: jnp.zeros((n, wt.shape[1]), wt.dtype)
    return jnp.concatenate([wt[:1024], wt[2560:2576], z(112), wt[2576:2960], wt[1024:2560], wt[2960:3216], wt[3216:3248],
                            z(96)], axis=0)


def _win_unpad(wt):
    return jnp.concatenate([wt[:1024], wt[1536:3072], wt[1024:1040], wt[1152:1536], wt[3072:3328], wt[3328:3360]], axis=0)


def _heads_split_t(wt, a, b):
    w3 = wt.reshape(MLA_HEADS, a + b, wt.shape[1])
    return jnp.concatenate([w3[:, :a].reshape(-1, wt.shape[1]), w3[:, a:].reshape(-1, wt.shape[1])], axis=0)


def _heads_merge_t(wt, a, b):
    wa = wt[:MLA_HEADS * a].reshape(MLA_HEADS, a, wt.shape[1])
    wb = wt[MLA_HEADS * a:].reshape(MLA_HEADS, b, wt.shape[1])
    return jnp.concatenate([wa, wb], axis=1).reshape(-1, wt.shape[1])


def _heads_split(w, a, b):
    k_dim = w.shape[0]
    w3 = w.reshape(k_dim, MLA_HEADS, a + b)
    return jnp.concatenate([w3[:, :, :a].reshape(k_dim, -1), w3[:, :, a:].reshape(k_dim, -1)], axis=1)


def _heads_merge(w, a, b):
    k_dim = w.shape[0]
    wa = w[:, :MLA_HEADS * a].reshape(k_dim, MLA_HEADS, a)
    wb = w[:, MLA_HEADS * a:].reshape(k_dim, MLA_HEADS, b)
    return jnp.concatenate([wa, wb], axis=2).reshape(k_dim, -1)


def _pad_lanes(v, width=LANE):
    return jnp.concatenate([v, jnp.zeros((v.shape[0], width - v.shape[1]), v.dtype)], axis=1)


def _local_step(x, p, positions, tgt, W, P, comm=None):
    comm = comm or {}
    zero_tok = jnp.zeros((8, LANE), F32)
    s_dim = x.shape[0]
    inv_freq = 1.0 / (ROPE_BASE ** (jnp.arange(0, MLA_ROPE, 2, dtype=F32) / MLA_ROPE))
    ang = positions.astype(F32)[:, None] * inv_freq
    cos, sin = jnp.cos(ang), jnp.sin(ang)
    cos32 = jnp.concatenate([cos, cos], axis=1)
    sin32 = jnp.concatenate([-sin, sin], axis=1)
    cos512, sin512 = jnp.tile(cos32, (1, 16)), jnp.tile(sin32, (1, 16))
    cos128, sin128 = jnp.tile(cos32, (1, 4)), jnp.tile(sin32, (1, 4))
    bias_p, alog_p = _pad_lanes(P["ssd_dt_bias"]), _pad_lanes(P["ssd_A_log"])
    d_x = jnp.repeat(P["ssd_D"], SSD_HEAD_DIM, axis=1)

    xb, pb = x.astype(BF16), p.astype(BF16)
    proj = _mm(xb, W["w_in"], tb=True, after=comm.get("token0", zero_tok), name="mm_in")
    z, qc, kvc, kr = [(proj,) + PROJ_BLOCK[n] for n in ("z", "q_c", "kv_c", "k_rope")]
    xbca = _conv_fwd(proj, PROJ_BLOCK["xbc"][1], P["ssd_conv_w"], P["ssd_conv_b"])
    y, states = _ssd_fwd(xbca, proj, PROJ_BLOCK["dt"][1], bias_p, alog_p, d_x)
    (yssd,) = _rowwise(_gate_rms, [y, z], [P["ssd_norm_w"]], [(1024, BF16)], name="ssd_gate_norm")
    (qn,) = _rowwise(_rms, [qc], [P["mla_q_norm_w"]], [(MLA_Q_RANK, BF16)], name="q_norm")
    (kvn,) = _rowwise(_rms, [kvc], [P["mla_kv_norm_w"]], [(MLA_KV_RANK, BF16)], name="kv_norm")
    q = _mm(qn, W["mla_w_q_b"], tb=True, name="mm_q")
    kv = _mm(kvn, W["mla_w_kv_b"], name="mm_kv")
    (qr,) = _rowwise(_rope_fwd_fn, [(q, 512, 2), cos512, sin512], [], [512], name="rope_q")
    (krt,) = _rowwise(lambda u, c, s: _spread4(_rope_fwd_fn(u, c, s)), [kr, cos128, sin128], [], [LANE], name="rope_k")
    att = _att_fwd(q, qr, kv, krt)
    (ymla,) = _rowwise(_rms, [att], [P["mla_out_norm_w"]], [(1024, BF16)], name="out_norm")
    ycat = jnp.concatenate([yssd, ymla], axis=1)
    if "late_weights" in comm:
        W = {**W, **comm["late_weights"](ycat)}
    mix = _mm(ycat, W["w_out"], name="mm_out")
    f_h1 = lambda xv, mv, g, b: _ln(ALPHA * xv + mv, g, b)
    h1, h1b = _rowwise(lambda *a: (f_h1(*a),) * 2, [x, mix], [P["ln_mix_g"], P["ln_mix_b"]], [1024, (1024, BF16)],
                       name="ln_mix")
    hg = _mm(h1b, W["w_ffn_gate"], tb=True, out_dtype=BF16, name="mm_gate")
    hu = _mm(h1b, W["w_ffn_up"], tb=True, out_dtype=BF16, name="mm_up")
    pg = _mm(h1b, W["w_ple_gate"], name="mm_ple_gate")
    pp = _mm(pb, W["w_ple_proj"], name="mm_ple")
    (act,) = _rowwise(lambda g, u: _silu(g.astype(F32)) * u.astype(F32), [hg, hu], [], [(D_FF, BF16)], name="swiglu")
    ffn = _mm(act, W["w_ffn_down"], name="mm_down")

    f_h2 = lambda hv, fv, pg, ppv, g, b: _ln(ALPHA * hv + fv + _sigmoid(pg) * ppv, g, b)

    def final_fn(hv, fv, pg, ppv, tv, g, b):
        h2, pull = jax.vjp(f_h2, hv, fv, pg, ppv, g, b)
        diff = h2 - tv
        loss = 0.5 * jnp.sum(jnp.mean(diff * diff, axis=-1, keepdims=True), axis=0, keepdims=True)
        d_h, d_f, d_pg, d_pp, d_g, d_b = pull(diff * (1.0 / D_MODEL))
        return d_h, d_f, d_pg, d_pp, d_g, d_b, jnp.broadcast_to(loss, (1, LANE))

    dh1_a, dffn, dpg, dpp, g_ffn_g, g_ffn_b, loss = _rowwise(
        final_fn, [h1, ffn, pg, pp, tgt], [P["ln_ffn_g"], P["ln_ffn_b"]], [1024] + [(1024, BF16)] * 3,
        [1024, 1024, LANE], name="final")

    G = {}
    dact = _mm(dffn, W["w_ffn_down"], tb=True, name="mm_down_dx")
    G["w_ffn_down"] = _mm(act, dffn, ta=True, out_dtype=GRAD_DT, name="mm_down_dw")

    def swiglu_bwd(g, u, d):
        g, u = g.astype(F32), u.astype(F32)
        sg = _sigmoid(g)
        return d * u * (sg * (1.0 + g * (1.0 - sg))), d * (g * sg)

    dg, du = _rowwise(swiglu_bwd, [hg, hu, dact], [], [(D_FF, BF16)] * 2, name="swiglu_bwd")
    dh1 = _mm(dg, W["w_ffn_gate"], add=dh1_a, name="mm_gate_dx")
    dh1 = _mm(du, W["w_ffn_up"], add=dh1, name="mm_up_dx")
    dh1 = _mm(dpg, W["w_ple_gate"], tb=True, add=dh1, name="mm_ple_gate_dx")
    G["w_ffn_gate"] = _mm(dg, h1b, ta=True, out_dtype=GRAD_DT, name="mm_gate_dw")
    G["w_ffn_up"] = _mm(du, h1b, ta=True, out_dtype=GRAD_DT, name="mm_up_dw")
    G["w_ple_gate"] = _mm(h1b, dpg, ta=True, out_dtype=GRAD_DT, name="mm_ple_gate_dw")
    G["w_ple_proj"] = _mm(pb, dpp, ta=True, out_dtype=GRAD_DT, name="mm_ple_dw")
    dx_a, dmix, g_mix_g, g_mix_b = _rowwise(
        lambda xv, mv, dv, g, b: _vjp_rows(f_h1)(xv, mv, g, b, dv), [x, mix, dh1], [P["ln_mix_g"], P["ln_mix_b"]],
        [1024, (1024, BF16)], [1024, 1024], name="ln_mix_bwd")
    dycat = _mm(dmix, W["w_out"], tb=True, name="mm_out_dx")
    G["w_out"] = _mm(ycat, dmix, ta=True, out_dtype=GRAD_DT, name="mm_out_dw")

    grads_done = comm.get("grads", lambda group, grads: zero_tok)
    tok1 = grads_done("ffn", G)
    datt, g_out_norm = _rowwise(lambda a, dv, w, t: _vjp_rows(_rms)(a, w, dv + jnp.min(t)), [att, (dycat, 1024, 1)],
                                [P["mla_out_norm_w"], tok1], [1024], [1024], name="out_norm_bwd")
    dqn_nope, dqr, dkn, dv, dkrt = _att_bwd(q, qr, kv, krt, att, datt)
    dkv = jnp.concatenate([dkn, dv], axis=1)
    (dq_rope,) = _rowwise(lambda d0, d1, c, s: _rope_bwd_fn(d0 + d1, c, s), [(dqr, 512, 0), (dqr, 512, 1), cos512, sin512],
                          [], [(512, BF16)], name="rope_q_bwd")

    def rope_k_bwd(*a):
        d = _spread4(functools.reduce(lambda u, w: u + w, a[:-2]))
        lane = lax.broadcasted_iota(jnp.int32, d.shape, 1)
        return _rope_bwd_fn(jnp.where(lane < MLA_ROPE, d, 0.0), a[-2], a[-1])

    (dkr,) = _rowwise(rope_k_bwd, [(dkrt, LANE, k) for k in range(MLA_HEADS // 2)] + [cos128, sin128], [], [(LANE, BF16)],
                      name="rope_k_bwd")
    dq = jnp.concatenate([dqn_nope, dq_rope], axis=1)
    dqn = _mm(dq, W["mla_w_q_b"], name="mm_q_dx")
    G["mla_w_q_b"] = _mm(dq, qn, ta=True, out_dtype=GRAD_DT, name="mm_q_dw")
    dkvn = _mm(dkv, W["mla_w_kv_b"], tb=True, name="mm_kv_dx")
    G["mla_w_kv_b"] = _mm(kvn, dkv, ta=True, out_dtype=GRAD_DT, name="mm_kv_dw")
    tok2 = grads_done("mla", G)
    dqc, g_q_norm = _rowwise(lambda a, dv, w, t: _vjp_rows(_rms)(a, w, dv + jnp.min(t)), [qc, dqn],
                             [P["mla_q_norm_w"], tok2], [(MLA_Q_RANK, BF16)], [MLA_Q_RANK], name="q_norm_bwd")
    dkvc, g_kv_norm = _rowwise(lambda a, dv, w: _vjp_rows(_rms)(a, w, dv), [kvc, dkvn], [P["mla_kv_norm_w"]],
                               [(MLA_KV_RANK, BF16)], [MLA_KV_RANK], name="kv_norm_bwd")

    dy, dz, g_ssd_norm = _rowwise(lambda yv, zv, dv, w, t: _vjp_rows(_gate_rms)(yv, zv, w, dv + jnp.min(t)),
                                  [y, z, (dycat, 1024, 0)], [P["ssd_norm_w"], tok1], [1024, (1024, BF16)], [1024],
                                  name="ssd_gate_norm_bwd")
    dxbca, ddtr, g_dt_bias, g_alog, g_d = _ssd_bwd(xbca, proj, PROJ_BLOCK["dt"][1], bias_p, alog_p, d_x, states, dy)
    da, g_conv_w, g_conv_b = _conv_bwd_pre(proj, PROJ_BLOCK["xbc"][1], P["ssd_conv_w"], P["ssd_conv_b"], dxbca)
    dxbc = _conv_bwd_in(da, P["ssd_conv_w"])

    small = {
        "ssd_conv_b": g_conv_b, "ssd_dt_bias": g_dt_bias, "ssd_A_log": g_alog, "ssd_D": g_d, "ssd_norm_w": g_ssd_norm,
        "mla_q_norm_w": g_q_norm, "mla_kv_norm_w": g_kv_norm, "mla_out_norm_w": g_out_norm, "ln_mix_g": g_mix_g,
        "ln_mix_b": g_mix_b, "ln_ffn_g": g_ffn_g, "ln_ffn_b": g_ffn_b,
    }
    packed = _pack_small(g_conv_w, [small[n] for n, _ in REPL_W], loss)
    if "small" in comm:
        comm["small"](packed)

    dproj = jnp.concatenate([dz, ddtr, dqc, dxbc, dkvc, dkr], axis=1)
    G["w_in"] = _mm(dproj, xb, ta=True, out_dtype=GRAD_DT, name="mm_in_dw")
    grad_x = _mm(dproj, W["w_in"], add=dx_a, after=grads_done("in", G), name="mm_in_dx")
    return grad_x, G, packed


def kernel(x, p, positions, w_in, ssd_conv_w, ssd_conv_b, ssd_dt_bias, ssd_A_log, ssd_D, ssd_norm_w, mla_q_norm_w, mla_w_q_b, mla_kv_norm_w, mla_w_kv_b, mla_out_norm_w, w_out, ln_mix_g, ln_mix_b, w_ffn_gate, w_ffn_up, w_ffn_down, w_ple_gate, w_ple_proj, ln_ffn_g, ln_ffn_b, loss_target, m_w_in, m_ssd_conv_w, m_ssd_conv_b, m_ssd_dt_bias, m_ssd_A_log, m_ssd_D, m_ssd_norm_w, m_mla_q_norm_w, m_mla_w_q_b, m_mla_kv_norm_w, m_mla_w_kv_b, m_mla_out_norm_w, m_w_out, m_ln_mix_g, m_ln_mix_b, m_w_ffn_gate, m_w_ffn_up, m_w_ffn_down, m_w_ple_gate, m_w_ple_proj, m_ln_ffn_g, m_ln_ffn_b, v_w_in, v_ssd_conv_w, v_ssd_conv_b, v_ssd_dt_bias, v_ssd_A_log, v_ssd_D, v_ssd_norm_w, v_mla_q_norm_w, v_mla_w_q_b, v_mla_kv_norm_w, v_mla_w_kv_b, v_mla_out_norm_w, v_w_out, v_ln_mix_g, v_ln_mix_b, v_w_ffn_gate, v_w_ffn_up, v_w_ffn_down, v_w_ple_gate, v_w_ple_proj, v_ln_ffn_g, v_ln_ffn_b):
    args = dict(locals())
    core = lax.axis_index("c")
    me = 4 * lax.axis_index("x") + 2 * lax.axis_index("y") + core

    conv_sh = ssd_conv_w[0]
    conv_hi = conv_sh.astype(BF16)
    conv_lo = (conv_sh - conv_hi.astype(F32)).astype(BF16)
    stored = lambda n, pre="": jnp.transpose(args[pre + n][0]) if n in TRANSPOSED else args[pre + n][0]
    shards = {n: stored(n).astype(BF16) for n in BIG}
    rows_full = lambda g: g.reshape(-1, g.shape[2])

    early = _gather_many([shards[n] for n in EARLY] + [jnp.concatenate([conv_hi, conv_lo], axis=0)], "gather_early")
    gw = dict(zip(EARLY, early[:-1]))
    conv_g = early[-1].astype(F32)
    W = {
        "w_in": _win_pad(rows_full(gw["w_in"])),
        "mla_w_q_b": _heads_split_t(rows_full(gw["mla_w_q_b"]), MLA_NOPE, MLA_ROPE),
        "mla_w_kv_b": _heads_split(_cols_full(gw["mla_w_kv_b"]), MLA_NOPE, MLA_V),
    }
    P = {n: args[n] for n, _ in REPL_W}
    P["ssd_conv_w"] = _cols_full(conv_g[:, :4] + conv_g[:, 4:])

    lands = [lax.dynamic_update_slice(lax.empty((N_DEV,) + shards[n].shape, BF16), shards[n][None], (me, 0, 0)) for n in LATE]
    late_sems = _split_start([shards[n] for n in LATE], lands, _plan_broadcast, N_DEV - 1, "gather_late_start",
                             after=early[0])

    def late_weights(after):
        _, got = _split_wait(*late_sems[:4], after, _plan_broadcast, "gather_late_wait")
        lw = dict(zip(LATE, got))
        return {"w_out": rows_full(lw["w_out"]), "w_ple_gate": rows_full(lw["w_ple_gate"]),
                "w_ple_proj": _cols_full(lw["w_ple_proj"]), "w_ffn_gate": rows_full(lw["w_ffn_gate"]),
                "w_ffn_up": rows_full(lw["w_ffn_up"]), "w_ffn_down": rows_full(lw["w_ffn_down"])}

    def to_blocks(n, g):
        if n == "w_in":
            g = _win_unpad(g)
        elif n == "mla_w_q_b":
            g = _heads_merge_t(g, MLA_NOPE, MLA_ROPE)
        elif n == "mla_w_kv_b":
            g = _heads_merge(g, MLA_NOPE, MLA_V)
        if n in ROW_SHARDED or n in TRANSPOSED:
            return g.reshape(N_DEV, -1, g.shape[1])
        return _cols_split(g)

    flight = {}

    def grads(group, G):
        gl = [to_blocks(n, G[n]) for n in GRAD_GROUPS[group]]
        flight[group] = _split_start(gl, [lax.empty(g.shape, g.dtype) for g in gl], _plan_scatter, N_DEV - 1,
                                     "grads_" + group + "_start", after=flight.get("small") if group == "in" else None)
        return flight[group][4]

    def small(packed):
        flight["small"] = _all_gather(packed, "gather_small")

    grad_x, G, packed = _local_step(x[0], p[0, 0], positions[0], loss_target[0], W, P,
                                    comm={"token0": late_sems[4], "late_weights": late_weights, "grads": grads, "small": small})

    me_arr = me.astype(jnp.int32).reshape(1)
    big_out = {}

    def finish(group, after):
        mine, recv = _split_wait(*flight[group][:4], after, _plan_scatter, "grads_" + group + "_wait")
        for n, g, r in zip(GRAD_GROUPS[group], mine, recv):
            big_out[n] = _adam(r, stored(n), stored(n, "m_"), stored(n, "v_"), "adam_" + n, own=g, own_idx=me_arr)
        return big_out[GRAD_GROUPS[group][-1]][0]

    done = finish("ffn", grad_x)
    conv_sum, loss_row, small_out = _adam_small(flight["small"], [(args[n], args["m_" + n], args["v_" + n]) for n, _ in REPL_W])
    finish("in", finish("mla", done))
    conv_grad = lax.dynamic_slice_in_dim(conv_sum, me * 192, 192, axis=1)
    conv_out = _adam(conv_grad[None], conv_sh, m_ssd_conv_w[0], v_ssd_conv_w[0], "adam_conv")
    small_map = {n: small_out[i] for i, (n, _) in enumerate(REPL_W)}

    def outputs(idx):
        res = []
        for n in WEIGHT_ORDER:
            if n == "ssd_conv_w":
                res.append(conv_out[idx][None])
            elif n in big_out:
                res.append((jnp.transpose(big_out[n][idx]) if n in TRANSPOSED else big_out[n][idx])[None])
            else:
                res.append(small_map[n][idx])
        return res

    return (loss_row[0, 0], grad_x[None], *outputs(0), *outputs(1), *outputs(2), *outputs(3))
```

```python
import functools
import math

import numpy as np
import jax
import jax.numpy as jnp
from jax import lax
from jax.experimental import pallas as pl
from jax.experimental.pallas import tpu as pltpu

F32 = jnp.float32
BF16 = jnp.bfloat16
HI = lax.Precision.HIGHEST

N_DEV = 8
D_MODEL = 1024
PLE_DIM = 256
SSD_HEADS = 16
SSD_HEAD_DIM = 64
SSD_INNER = 1024
SSD_STATE = 128
SSD_XBC = 1536
SSD_CHUNK = 128
MLA_HEADS = 16
MLA_Q_RANK = 384
MLA_KV_RANK = 256
MLA_NOPE = 64
MLA_ROPE = 32
MLA_V = 64
ROPE_BASE = 10000.0
D_FF = 2816
IN_WIDTH = 3248
IN_PAD = 3456
ALPHA = 2.0 ** 0.25
EPS = 1e-6
LN_EPS = 1e-5
ATT_SCALE = 1.0 / math.sqrt(MLA_NOPE + MLA_ROPE)
ADAM_LR, ADAM_B1, ADAM_B2, ADAM_EPS, ADAM_WD, ADAM_STEP = 0.001, 0.9, 0.999, 1e-08, 0.01, 10

LANE = 128
MXU_DIM = 256
MM_TM, MM_TN, MM_TK = 1408, 1408, 2048
ROW_TILE = 256
ATT_TQ = 256

GRAD_DT = BF16

BIG = ("w_in", "mla_w_q_b", "mla_w_kv_b", "w_out", "w_ffn_gate", "w_ffn_up", "w_ffn_down", "w_ple_gate", "w_ple_proj")
EARLY = ("w_in", "mla_w_q_b", "mla_w_kv_b")
LATE = ("w_out", "w_ffn_gate", "w_ffn_up", "w_ffn_down", "w_ple_gate", "w_ple_proj")
GRAD_GROUPS = {"ffn": ("w_ffn_gate", "w_ffn_up", "w_ffn_down", "w_ple_gate", "w_ple_proj", "w_out"),
               "mla": ("mla_w_q_b", "mla_w_kv_b"), "in": ("w_in",)}
ROW_SHARDED = ("w_out", "w_ffn_down", "w_ple_gate")
TRANSPOSED = ("w_in", "mla_w_q_b", "w_ffn_gate", "w_ffn_up")
WEIGHT_ORDER = ("w_in", "ssd_conv_w", "ssd_conv_b", "ssd_dt_bias", "ssd_A_log", "ssd_D", "ssd_norm_w", "mla_q_norm_w",
                "mla_w_q_b", "mla_kv_norm_w", "mla_w_kv_b", "mla_out_norm_w", "w_out", "ln_mix_g", "ln_mix_b",
                "w_ffn_gate", "w_ffn_up", "w_ffn_down", "w_ple_gate", "w_ple_proj", "ln_ffn_g", "ln_ffn_b")


def _tile(dim, cap, prefer=None):
    cands = [t for t in range(LANE, min(cap, dim) + 1, LANE) if dim % t == 0]
    if not cands:
        return dim
    if prefer is None:
        return max(cands)
    fill = lambda t: t / (MXU_DIM * -(-t // MXU_DIM))
    good = min(0.9, max(fill(t) for t in cands))
    return min((t for t in cands if fill(t) >= good), key=lambda t: abs(t - prefer))


def _dot(a, b, dims=(((1,), (0,)), ((), ())), precision=None):
    return lax.dot_general(a, b, dims, preferred_element_type=F32, precision=precision)


_NT = (((1,), (1,)), ((), ()))
_TN = (((0,), (0,)), ((), ()))


def _mm(a, b, *, ta=False, tb=False, add=None, out_dtype=F32, after=None, epilogue=None, name):
    k_dim, m_dim = a.shape if ta else a.shape[::-1]
    n_dim, kb = b.shape if tb else b.shape[::-1]
    assert k_dim == kb
    tm, tn, tk = _tile(m_dim, MM_TM), _tile(n_dim, MM_TN, prefer=1024), _tile(k_dim, MM_TK, prefer=MM_TK)
    nk = k_dim // tk
    dims = (((0 if ta else 1,), (1 if tb else 0,)), ((), ()))
    a_spec = pl.BlockSpec((tk, tm), lambda i, j, k: (k, i)) if ta else pl.BlockSpec((tm, tk), lambda i, j, k: (i, k))
    b_spec = pl.BlockSpec((tn, tk), lambda i, j, k: (j, k)) if tb else pl.BlockSpec((tk, tn), lambda i, j, k: (k, j))
    o_spec = pl.BlockSpec((tm, tn), lambda i, j, k: (i, j))
    epi_fn, epi_in, out_dtypes = epilogue if epilogue else (None, [], [out_dtype])
    tiles = ([add] if add is not None else []) + list(epi_in)
    n_out = len(out_dtypes)

    def body(*refs):
        a_ref, b_ref = refs[:2]
        tile_refs = refs[2:2 + len(tiles)]
        out_refs = refs[len(refs) - n_out - (nk > 1):len(refs) - (nk > 1)]
        part = _dot(a_ref[...].astype(BF16), b_ref[...].astype(BF16), dims)
        if add is not None:
            part_add = lambda v: v + tile_refs[0][...]
        else:
            part_add = lambda v: v

        def write(total):
            extra = [r[...] for r in tile_refs[add is not None:]]
            outs = epi_fn(total, *extra) if epi_fn else (total,)
            for o_ref, val in zip(out_refs, outs):
                o_ref[...] = val.astype(o_ref.dtype)

        if nk == 1:
            write(part_add(part))
            return
        acc = refs[-1]
        k = pl.program_id(2)

        @pl.when(k == 0)
        def _():
            acc[...] = part_add(part)

        @pl.when(k > 0)
        def _():
            acc[...] += part

        @pl.when(k == nk - 1)
        def _():
            write(acc[...])

    ins = [a, b] + tiles + ([after] if after is not None else [])
    specs = [a_spec, b_spec] + [o_spec] * len(tiles) + ([pl.BlockSpec(memory_space=pl.ANY)] if after is not None else [])
    res = pl.pallas_call(
        body, name=name, grid=(m_dim // tm, n_dim // tn, nk), in_specs=specs, out_specs=[o_spec] * n_out,
        out_shape=[jax.ShapeDtypeStruct((m_dim, n_dim), dt) for dt in out_dtypes],
        scratch_shapes=[pltpu.VMEM((tm, tn), F32)] if nk > 1 else [],
        compiler_params=pltpu.CompilerParams(dimension_semantics=("parallel", "parallel", "arbitrary")),
    )(*ins)
    return res if epilogue else res[0]


def _rowwise(fn, rows, consts, out_widths, acc_widths=(), *, name, tr=ROW_TILE):
    row_arrays, row_specs = [], []
    first_arr = rows[0][0] if isinstance(rows[0], tuple) else rows[0]
    s_dim = first_arr.shape[-2]
    tr = min(tr, s_dim)
    for r in rows:
        arr, width, cb = r if isinstance(r, tuple) else (r, r.shape[-1], 0)
        row_arrays.append(arr)
        if arr.ndim == 3:
            row_specs.append(pl.BlockSpec((None, tr, width), functools.partial(lambda i, k: (k, i, 0), k=cb)))
        else:
            row_specs.append(pl.BlockSpec((tr, width), functools.partial(lambda i, cb: (i, cb), cb=cb)))
    const_specs = [pl.BlockSpec(c.shape, lambda i: (0, 0)) for c in consts]
    nr, nc, no, na = len(rows), len(consts), len(out_widths), len(acc_widths)

    def body(*refs):
        ins = [r[...] for r in refs[:nr + nc]]
        res = fn(*ins)
        if not isinstance(res, (tuple, list)):
            res = (res,)
        out_refs = refs[nr + nc:nr + nc + no]
        acc_refs = refs[nr + nc + no:]
        for o_ref, val in zip(out_refs, res[:no]):
            o_ref[...] = val.astype(o_ref.dtype)
        first = pl.program_id(0) == 0
        for a_ref, val in zip(acc_refs, res[no:]):
            @pl.when(first)
            def _(a_ref=a_ref, val=val):
                a_ref[...] = val

            @pl.when(jnp.logical_not(first))
            def _(a_ref=a_ref, val=val):
                a_ref[...] += val

    outs = [w if isinstance(w, tuple) else (w, F32) for w in out_widths]
    out_shape = [jax.ShapeDtypeStruct((s_dim, w), dt) for w, dt in outs]
    out_shape += [jax.ShapeDtypeStruct((1, w), F32) for w in acc_widths]
    out_specs = [pl.BlockSpec((tr, w), lambda i: (i, 0)) for w, _ in outs]
    out_specs += [pl.BlockSpec((1, w), lambda i: (0, 0)) for w in acc_widths]
    res = pl.pallas_call(
        body, name=name, grid=(s_dim // tr,), in_specs=row_specs + const_specs, out_specs=out_specs, out_shape=out_shape,
        compiler_params=pltpu.CompilerParams(dimension_semantics=("arbitrary",)),
    )(*row_arrays, *consts)
    return res


def _colsum(v):
    return jnp.sum(v, axis=0, keepdims=True)


def _rms(u, g):
    return u * lax.rsqrt(jnp.mean(u * u, axis=-1, keepdims=True) + EPS) * g


def _ln(u, g, b):
    mu = jnp.mean(u, axis=-1, keepdims=True)
    d = u - mu
    var = jnp.mean(d * d, axis=-1, keepdims=True)
    return d * lax.rsqrt(var + LN_EPS) * g + b


def _sigmoid(v):
    return 1.0 / (1.0 + jnp.exp(-v))


def _silu(v):
    return v * _sigmoid(v)


def _softplus(v):
    y = jnp.exp(-jnp.abs(v))
    w = 1.0 + y
    log1p = jnp.where(w == 1.0, y, jnp.log(w) * y / jnp.where(w == 1.0, 1.0, w - 1.0))
    return jnp.maximum(v, 0.0) + log1p


def _gate_rms(y, z, w):
    return _rms(y * _silu(z), w)


def _vjp_rows(f):
    def fn(*args):
        prim, ct = args[:-1], args[-1]
        _, pull = jax.vjp(f, *prim)
        return pull(ct)
    return fn


def _conv_pre(cur, prev, w, b, first):
    row = lax.broadcasted_iota(jnp.int32, cur.shape, 0)
    acc = cur * w[3:4, :] + b
    for j in (1, 2, 3):
        tail = jnp.where(first, 0.0, pltpu.roll(prev, j, 0))
        acc = acc + jnp.where(row >= j, pltpu.roll(cur, j, 0), tail) * w[3 - j:4 - j, :]
    return acc


def _conv_fwd(u, ucb, w, b, name="conv_fwd"):
    s_dim, width = u.shape[0], w.shape[1]
    tr = min(ROW_TILE, s_dim)

    def body(cur_ref, prev_ref, w_ref, b_ref, o_ref):
        pre = _conv_pre(cur_ref[...], prev_ref[...], w_ref, b_ref[...], pl.program_id(0) == 0)
        o_ref[...] = _silu(pre)

    return pl.pallas_call(
        body, name=name, grid=(s_dim // tr,),
        in_specs=[pl.BlockSpec((tr, width), lambda i: (i, ucb)),
                  pl.BlockSpec((tr, width), lambda i: (jnp.maximum(i - 1, 0), ucb)),
                  pl.BlockSpec(w.shape, lambda i: (0, 0)), pl.BlockSpec(b.shape, lambda i: (0, 0))],
        out_specs=pl.BlockSpec((tr, width), lambda i: (i, 0)), out_shape=jax.ShapeDtypeStruct((s_dim, width), F32),
        compiler_params=pltpu.CompilerParams(dimension_semantics=("arbitrary",)),
    )(u, u, w, b)


def _conv_bwd_pre(u, ucb, w, b, dact, name="conv_bwd_pre"):
    s_dim, width = u.shape[0], w.shape[1]
    tr = min(ROW_TILE, s_dim)

    def body(cur_ref, prev_ref, w_ref, b_ref, d_ref, da_ref, dw_ref, db_ref):
        first = pl.program_id(0) == 0
        cur, prev = cur_ref[...], prev_ref[...]
        pre = _conv_pre(cur, prev, w_ref, b_ref[...], first)
        sg = _sigmoid(pre)
        da = d_ref[...] * (sg * (1.0 + pre * (1.0 - sg)))
        da_ref[...] = da
        row = lax.broadcasted_iota(jnp.int32, cur.shape, 0)

        @pl.when(first)
        def _():
            dw_ref[...] = jnp.zeros_like(dw_ref)
            db_ref[...] = jnp.zeros_like(db_ref)

        db_ref[...] += _colsum(da)
        dw_ref[3:4, :] += _colsum(da * cur)
        for j in (1, 2, 3):
            tail = jnp.where(first, 0.0, pltpu.roll(prev, j, 0))
            sh = jnp.where(row >= j, pltpu.roll(cur, j, 0), tail)
            dw_ref[3 - j:4 - j, :] += _colsum(da * sh)

    return pl.pallas_call(
        body, name=name, grid=(s_dim // tr,),
        in_specs=[pl.BlockSpec((tr, width), lambda i: (i, ucb)),
                  pl.BlockSpec((tr, width), lambda i: (jnp.maximum(i - 1, 0), ucb)),
                  pl.BlockSpec(w.shape, lambda i: (0, 0)), pl.BlockSpec(b.shape, lambda i: (0, 0)),
                  pl.BlockSpec((tr, width), lambda i: (i, 0))],
        out_specs=[pl.BlockSpec((tr, width), lambda i: (i, 0)), pl.BlockSpec(w.shape, lambda i: (0, 0)),
                   pl.BlockSpec(b.shape, lambda i: (0, 0))],
        out_shape=[jax.ShapeDtypeStruct((s_dim, width), F32), jax.ShapeDtypeStruct(w.shape, F32),
                   jax.ShapeDtypeStruct(b.shape, F32)],
        compiler_params=pltpu.CompilerParams(dimension_semantics=("arbitrary",)),
    )(u, u, w, b, dact)


def _conv_bwd_in(da, w, name="conv_bwd_in"):
    s_dim, width = da.shape
    tr = min(ROW_TILE, s_dim)
    n = s_dim // tr

    def body(cur_ref, nxt_ref, w_ref, o_ref):
        last = pl.program_id(0) == n - 1
        cur, nxt = cur_ref[...], nxt_ref[...]
        row = lax.broadcasted_iota(jnp.int32, cur.shape, 0)
        acc = cur * w_ref[3:4, :]
        for j in (1, 2, 3):
            head = jnp.where(last, 0.0, pltpu.roll(nxt, tr - j, 0))
            acc = acc + jnp.where(row < tr - j, pltpu.roll(cur, tr - j, 0), head) * w_ref[3 - j:4 - j, :]
        o_ref[...] = acc.astype(o_ref.dtype)

    return pl.pallas_call(
        body, name=name, grid=(n,),
        in_specs=[pl.BlockSpec((tr, width), lambda i: (i, 0)), pl.BlockSpec((tr, width), lambda i: (jnp.minimum(i + 1, n - 1), 0)),
                  pl.BlockSpec(w.shape, lambda i: (0, 0))],
        out_specs=pl.BlockSpec((tr, width), lambda i: (i, 0)), out_shape=jax.ShapeDtypeStruct((s_dim, width), BF16),
        compiler_params=pltpu.CompilerParams(dimension_semantics=("arbitrary",)),
    )(da, da, w)


def _sel_dot(a, sel, pieces, dims=(((1,), (0,)), ((), ())), sel_left=False):
    sel = sel.astype(BF16)
    acc, rest = None, a
    for _ in range(pieces):
        piece = rest.astype(BF16)
        rest = rest - piece.astype(F32)
        part = _dot(sel, piece, dims) if sel_left else _dot(piece, sel, dims)
        acc = part if acc is None else acc + part
    return acc


def _ssd_consts():
    L = SSD_CHUNK
    tri = np.tril(np.ones((L, L), np.float32))
    expand = np.zeros((LANE, SSD_INNER), np.float32)
    expand128 = np.zeros((LANE, SSD_HEADS * LANE), np.float32)
    for h in range(SSD_HEADS):
        expand[h, h * SSD_HEAD_DIM:(h + 1) * SSD_HEAD_DIM] = 1.0
        expand128[h, h * LANE:(h + 1) * LANE] = 1.0
    return jnp.asarray(tri), jnp.asarray(expand), jnp.asarray(expand128), jnp.asarray(expand.T.copy())


def _ssd_prep(dt_ref, bias_ref, alog_ref, tri_ref, exp_ref, exp128_ref, cs_s, cst_s, ex_s, csx_s):
    L = SSD_CHUNK
    dt = _softplus(dt_ref[...] + bias_ref[...])
    a = -jnp.exp(alog_ref[...])
    cs = _sel_dot(dt * a, tri_ref[...], 3, sel_left=True)
    cs_s[...] = cs
    cst_s[...] = cs.T
    last = cs_s[L - 1:L, :]
    expand = exp_ref[...]
    ex_s[...] = _sel_dot(jnp.exp(cs), expand, 2)
    f_x = _sel_dot(jnp.exp(last - cs), expand, 2)
    dt_x = _sel_dot(dt, expand, 2)
    csx_s[...] = _sel_dot(cs, exp128_ref[...], 3)
    t_x = ex_s[L - 1:L, :]
    return dt, a, dt_x, f_x, t_x


def _decay_matrix(csx_s, cst_s, h, tril):
    seg = csx_s[:, h * LANE:(h + 1) * LANE] - cst_s[h:h + 1, :]
    return jnp.exp(jnp.where(tril, seg, -jnp.inf))


def _ssd_fwd(xbca, dtr, dtcb, bias, alog, d_x, name="ssd_fwd"):
    s_dim = xbca.shape[0]
    L = SSD_CHUNK
    nc = s_dim // L
    tri, expand, expand128, _ = _ssd_consts()

    def body(xs_ref, b_ref, c_ref, dt_ref, bias_ref, alog_ref, dx_ref, tri_ref, exp_ref, exp128_ref,
             y_ref, st_ref, st_s, cs_s, cst_s, ex_s, csx_s):
        @pl.when(pl.program_id(0) == 0)
        def _():
            st_s[...] = jnp.zeros_like(st_s)

        dt, a, dt_x, f_x, t_x = _ssd_prep(dt_ref, bias_ref, alog_ref, tri_ref, exp_ref, exp128_ref, cs_s, cst_s, ex_s, csx_s)
        st_ref[0] = st_s[...]
        row = lax.broadcasted_iota(jnp.int32, (L, L), 0)
        col = lax.broadcasted_iota(jnp.int32, (L, L), 1)
        tril = row >= col
        low = col < SSD_HEAD_DIM
        for g in range(2):
            bg = b_ref[:, g * LANE:(g + 1) * LANE]
            cg = c_ref[:, g * LANE:(g + 1) * LANE].astype(BF16)
            gmat = _dot(cg, bg.astype(BF16), _NT)
            bgt = bg.T.astype(BF16)
            for jj in range(4):
                j = 4 * g + jj
                sl = slice(j * LANE, (j + 1) * LANE)
                xp = xs_ref[:, sl]
                x_dt = xp * dt_x[:, sl]
                xb = x_dt.astype(BF16)
                yd = []
                for e in range(2):
                    lm = _decay_matrix(csx_s, cst_s, 2 * j + e, tril)
                    yd.append(_dot((gmat * lm).astype(BF16), xb))
                stp = st_s[j]
                z = _dot(cg, stp.astype(BF16))
                y_ref[:, sl] = jnp.where(low, yd[0], yd[1]) + ex_s[:, sl] * z + dx_ref[:, sl] * xp
                xf = (x_dt * f_x[:, sl]).astype(BF16)
                st_s[j] = t_x[:, sl] * stp + _dot(bgt, xf)

    const = lambda shape: pl.BlockSpec(shape, lambda c: tuple(0 for _ in shape))
    return pl.pallas_call(
        body, name=name, grid=(nc,),
        in_specs=[pl.BlockSpec((L, 1024), lambda c: (c, 0)), pl.BlockSpec((L, 256), lambda c: (c, 4)),
                  pl.BlockSpec((L, 256), lambda c: (c, 5)), pl.BlockSpec((L, LANE), lambda c: (c, dtcb)),
                  const((1, LANE)), const((1, LANE)), const((1, 1024)), const((L, L)), const((LANE, 1024)),
                  const((LANE, 2048))],
        out_specs=[pl.BlockSpec((L, 1024), lambda c: (c, 0)), pl.BlockSpec((1, 8, LANE, LANE), lambda c: (c, 0, 0, 0))],
        out_shape=[jax.ShapeDtypeStruct((s_dim, 1024), F32), jax.ShapeDtypeStruct((nc, 8, LANE, LANE), F32)],
        scratch_shapes=[pltpu.VMEM((8, LANE, LANE), F32), pltpu.VMEM((L, LANE), F32), pltpu.VMEM((LANE, L), F32),
                        pltpu.VMEM((L, 1024), F32), pltpu.VMEM((L, 2048), F32)],
        compiler_params=pltpu.CompilerParams(dimension_semantics=("arbitrary",)),
    )(xbca, xbca, xbca, dtr, bias, alog, d_x, tri, expand, expand128)


def _ssd_bwd(xbca, dtr, dtcb, bias, alog, d_x, states, dy, name="ssd_bwd"):
    s_dim = xbca.shape[0]
    L = SSD_CHUNK
    nc = s_dim // L
    tri, expand, expand128, expand_t = _ssd_consts()

    def body(xs_ref, b_ref, c_ref, dt_ref, bias_ref, alog_ref, dx_ref, tri_ref, exp_ref, exp128_ref, expt_ref,
             st_ref, dy_ref, dxbc_ref, ddt_ref, dbias_ref, dalog_ref, dd_ref,
             dst_s, cs_s, cst_s, ex_s, csx_s, dcsx_s, ddtx_s, dcol_s, drow_s, dlast_s, dd_s):
        @pl.when(pl.program_id(0) == 0)
        def _():
            dst_s[...] = jnp.zeros_like(dst_s)
            dbias_ref[...] = jnp.zeros_like(dbias_ref)
            dalog_ref[...] = jnp.zeros_like(dalog_ref)
            dd_s[...] = jnp.zeros_like(dd_s)

        dt, a, dt_x, f_x, t_x = _ssd_prep(dt_ref, bias_ref, alog_ref, tri_ref, exp_ref, exp128_ref, cs_s, cst_s, ex_s, csx_s)
        row = lax.broadcasted_iota(jnp.int32, (L, L), 0)
        col = lax.broadcasted_iota(jnp.int32, (L, L), 1)
        tril = row >= col
        low = col < SSD_HEAD_DIM
        dcol_s[...] = jnp.zeros_like(dcol_s)
        drow_s[...] = jnp.zeros_like(drow_s)
        for g in range(2):
            bg = b_ref[:, g * LANE:(g + 1) * LANE]
            cg = c_ref[:, g * LANE:(g + 1) * LANE]
            bgb, cgb = bg.astype(BF16), cg.astype(BF16)
            gmat = _dot(cgb, bgb, _NT)
            d_g = jnp.zeros((L, L), F32)
            d_b = jnp.zeros((L, LANE), F32)
            d_c = jnp.zeros((L, LANE), F32)
            for jj in range(4):
                j = 4 * g + jj
                sl = slice(j * LANE, (j + 1) * LANE)
                xp = xs_ref[:, sl]
                dtp = dt_x[:, sl]
                x_dt = xp * dtp
                xb = x_dt.astype(BF16)
                dyp = dy_ref[:, sl]
                dd_s[:, sl] += _colsum(dyp * xp)
                d_xdt = jnp.zeros((L, LANE), F32)
                for e in range(2):
                    h = 2 * j + e
                    lm = _decay_matrix(csx_s, cst_s, h, tril)
                    m = gmat * lm
                    dye = jnp.where(low if e == 0 else jnp.logical_not(low), dyp, 0.0).astype(BF16)
                    d_m = jnp.where(tril, _dot(dye, xb, _NT), 0.0)
                    d_xdt = d_xdt + _dot(m.astype(BF16), dye, _TN)
                    d_g = d_g + d_m * lm
                    w = d_m * m
                    dcol_s[...] += jnp.where(col == h, jnp.sum(w, axis=1, keepdims=True), 0.0)
                    drow_s[...] += jnp.where(row == h, jnp.sum(w, axis=0, keepdims=True), 0.0)
                stp = st_ref[0, j]
                stb = stp.astype(BF16)
                dstn = dst_s[j]
                dstb = dstn.astype(BF16)
                e_p = ex_s[:, sl]
                f_p = f_x[:, sl]
                t_p = t_x[:, sl]
                z = _dot(cgb, stb)
                d_z = (e_p * dyp).astype(BF16)
                d_c = d_c + _dot(d_z, stb, _NT)
                d_xf = _dot(bgb, dstb)
                d_b = d_b + _dot((x_dt * f_p).astype(BF16), dstb, _NT)
                d_xdt = d_xdt + f_p * d_xf
                d_f = x_dt * d_xf * f_p
                dcsx_s[:, sl] = dyp * e_p * z - d_f
                dlast_s[:, sl] = _colsum(d_f) + _colsum(dstn * stp) * t_p
                dst_s[j] = _dot(cgb, d_z, _TN) + t_p * dstn
                dxbc_ref[:, sl] = dx_ref[:, sl] * dyp + d_xdt * dtp
                ddtx_s[:, sl] = d_xdt * xp
            d_gb = d_g.astype(BF16)
            dxbc_ref[:, 1024 + g * LANE:1024 + (g + 1) * LANE] = d_b + _dot(d_gb, cgb, _TN)
            dxbc_ref[:, 1280 + g * LANE:1280 + (g + 1) * LANE] = d_c + _dot(d_gb, bgb)

        expt = expt_ref[...]
        dlast = _sel_dot(jnp.broadcast_to(dlast_s[...], (8, 1024)), expt, 3)
        d_cs = dcol_s[...] - drow_s[...].T + _sel_dot(dcsx_s[...], expt, 3)
        rown = lax.broadcasted_iota(jnp.int32, (L, LANE), 0)
        d_cs = d_cs + jnp.where(rown == L - 1, jnp.sum(dlast, axis=0, keepdims=True) * 0.125, 0.0)
        d_da = _sel_dot(d_cs, tri_ref[...], 3, _TN, sel_left=True)
        d_dt = d_da * a + _sel_dot(ddtx_s[...], expt, 3)
        dalog_ref[...] += _colsum(d_da * dt) * a
        d_raw = d_dt * _sigmoid(dt_ref[...] + bias_ref[...])
        ddt_ref[...] = d_raw.astype(ddt_ref.dtype)
        dbias_ref[...] += _colsum(d_raw)
        dd8 = _sel_dot(jnp.broadcast_to(dd_s[...], (8, 1024)), expt, 3)
        dd_ref[...] = jnp.sum(dd8, axis=0, keepdims=True) * 0.125

    const = lambda shape: pl.BlockSpec(shape, lambda c: tuple(0 for _ in shape))
    rev = lambda cb: (lambda c: (nc - 1 - c, cb))
    return pl.pallas_call(
        body, name=name, grid=(nc,),
        in_specs=[pl.BlockSpec((L, 1024), rev(0)), pl.BlockSpec((L, 256), rev(4)), pl.BlockSpec((L, 256), rev(5)),
                  pl.BlockSpec((L, LANE), rev(dtcb)), const((1, LANE)), const((1, LANE)), const((1, 1024)), const((L, L)),
                  const((LANE, 1024)), const((LANE, 2048)), const((1024, LANE)),
                  pl.BlockSpec((1, 8, LANE, LANE), lambda c: (nc - 1 - c, 0, 0, 0)), pl.BlockSpec((L, 1024), rev(0))],
        out_specs=[pl.BlockSpec((L, SSD_XBC), rev(0)), pl.BlockSpec((L, LANE), rev(0)), const((1, LANE)), const((1, LANE)),
                   const((1, LANE))],
        out_shape=[jax.ShapeDtypeStruct((s_dim, SSD_XBC), F32), jax.ShapeDtypeStruct((s_dim, LANE), BF16),
                   jax.ShapeDtypeStruct((1, LANE), F32), jax.ShapeDtypeStruct((1, LANE), F32),
                   jax.ShapeDtypeStruct((1, LANE), F32)],
        scratch_shapes=[pltpu.VMEM((8, LANE, LANE), F32), pltpu.VMEM((L, LANE), F32), pltpu.VMEM((LANE, L), F32),
                        pltpu.VMEM((L, 1024), F32), pltpu.VMEM((L, 2048), F32), pltpu.VMEM((L, 1024), F32),
                        pltpu.VMEM((L, 1024), F32), pltpu.VMEM((L, LANE), F32), pltpu.VMEM((LANE, L), F32),
                        pltpu.VMEM((1, 1024), F32), pltpu.VMEM((1, 1024), F32)],
        compiler_params=pltpu.CompilerParams(dimension_semantics=("arbitrary",)),
    )(xbca, xbca, xbca, dtr, bias, alog, d_x, tri, expand, expand128, expand_t, states, dy)


def _swap_halves(u):
    width = u.shape[1]
    lane = lax.broadcasted_iota(jnp.int32, u.shape, 1)
    return jnp.where(lane % MLA_ROPE < MLA_ROPE // 2, pltpu.roll(u, width - MLA_ROPE // 2, 1), pltpu.roll(u, MLA_ROPE // 2, 1))


def _rope_fwd_fn(u, cos, sin):
    return u * cos + _swap_halves(u) * sin


def _rope_bwd_fn(d, cos, sin):
    return d * cos + _swap_halves(d * sin)


def _spread4(v):
    return v + pltpu.roll(v, 32, 1) + pltpu.roll(v, 64, 1) + pltpu.roll(v, 96, 1)


def _att_masks(tq):
    lane = lax.broadcasted_iota(jnp.int32, (tq, LANE), 1)
    return lane // MLA_NOPE, lane // MLA_ROPE


def _att_tile(i, tq):
    klen = (i + 1) * tq
    qpos = i * tq + lax.broadcasted_iota(jnp.int32, (tq, klen), 0)
    kpos = lax.broadcasted_iota(jnp.int32, (tq, klen), 1)
    return slice(i * tq, (i + 1) * tq), klen, qpos >= kpos


def _att_qcat(qn_t, qr_t, par, e, half_id, grp_id):
    return jnp.concatenate([jnp.where(half_id == par, qn_t * ATT_SCALE, 0.0), jnp.where(grp_id == e, qr_t * ATT_SCALE, 0.0)],
                           axis=1).astype(BF16)


def _att_exp(qcat, kcat, causal):
    s = jnp.where(causal, _dot(qcat, kcat, _NT), -jnp.inf)
    e = jnp.exp(s - jnp.max(s, axis=1, keepdims=True))
    return e, 1.0 / jnp.sum(e, axis=1, keepdims=True)


def _att_specs(s_dim):
    col = lambda f: pl.BlockSpec((s_dim, LANE), lambda j: (0, f(j)))
    return [col(lambda j: j), col(lambda j: j // 2), col(lambda j: j), col(lambda j: 0), col(lambda j: 8 + j)]


def _att_fwd(q, qr, kv, krt, name="att_fwd"):
    s_dim = q.shape[0]
    tq = min(ATT_TQ, s_dim)

    def body(qn_ref, qr_ref, kn_ref, krt_ref, v_ref, o_ref, kcat_s, vb_s):
        e0 = 2 * (pl.program_id(0) % 2)
        half_id, grp_id = _att_masks(tq)
        kcat_s[...] = jnp.concatenate([kn_ref[...], krt_ref[...]], axis=1).astype(BF16)
        vb_s[...] = v_ref[...].astype(BF16)
        for i in range(s_dim // tq):
            rows, klen, causal = _att_tile(i, tq)
            qn_t, qr_t = qn_ref[rows, :], qr_ref[rows, :]
            outs = []
            for par in range(2):
                qcat = _att_qcat(qn_t, qr_t, par, e0 + par, half_id, grp_id)
                e, inv_l = _att_exp(qcat, kcat_s[0:klen, :], causal)
                outs.append(_dot(e.astype(BF16), vb_s[0:klen, :]) * inv_l)
            o_ref[rows, :] = jnp.where(half_id == 0, outs[0], outs[1])

    return pl.pallas_call(
        body, name=name, grid=(MLA_HEADS // 2,), in_specs=_att_specs(s_dim),
        out_specs=pl.BlockSpec((s_dim, LANE), lambda j: (0, j)), out_shape=jax.ShapeDtypeStruct((s_dim, 1024), F32),
        scratch_shapes=[pltpu.VMEM((s_dim, 2 * LANE), BF16), pltpu.VMEM((s_dim, LANE), BF16)],
        compiler_params=pltpu.CompilerParams(dimension_semantics=("parallel",)),
    )(q, qr, kv, krt, kv)


def _att_bwd(q, qr, kv, krt, o, do, name="att_bwd"):
    s_dim = q.shape[0]
    tq = min(ATT_TQ, s_dim)

    def body(qn_ref, qr_ref, kn_ref, krt_ref, v_ref, o_ref, do_ref, dqn_ref, dqr_ref, dkn_ref, dv_ref, dkrt_ref,
             kcat_s, vb_s):
        e0 = 2 * (pl.program_id(0) % 2)
        half_id, grp_id = _att_masks(tq)
        kcat_s[...] = jnp.concatenate([kn_ref[...], krt_ref[...]], axis=1).astype(BF16)
        vb_s[...] = v_ref[...].astype(BF16)
        dkn_ref[...] = jnp.zeros_like(dkn_ref)
        dv_ref[...] = jnp.zeros_like(dv_ref)
        dkrt_ref[...] = jnp.zeros_like(dkrt_ref)
        for i in range(s_dim // tq):
            rows, klen, causal = _att_tile(i, tq)
            qn_t, qr_t, o_t, do_t = qn_ref[rows, :], qr_ref[rows, :], o_ref[rows, :], do_ref[rows, :]
            dqn = jnp.zeros((tq, LANE), F32)
            dqr = jnp.zeros((tq, LANE), F32)
            for par in range(2):
                qcat = _att_qcat(qn_t, qr_t, par, e0 + par, half_id, grp_id)
                e, inv_l = _att_exp(qcat, kcat_s[0:klen, :], causal)
                p = e * inv_l
                dom = jnp.where(half_id == par, do_t, 0.0)
                domb = dom.astype(BF16)
                d_p = _dot(domb, vb_s[0:klen, :], _NT)
                d_row = jnp.sum(dom * o_t, axis=1, keepdims=True)
                d_s = (p * (d_p - d_row)).astype(BF16)
                dqcat = _dot(d_s, kcat_s[0:klen, :]) * ATT_SCALE
                dqn = dqn + jnp.where(half_id == par, dqcat[:, :LANE], 0.0)
                dqr = dqr + jnp.where(grp_id == e0 + par, dqcat[:, LANE:], 0.0)
                dkcat = _dot(d_s, qcat, _TN)
                dkn_ref[0:klen, :] += dkcat[:, :LANE]
                dkrt_ref[0:klen, :] += dkcat[:, LANE:]
                dv_ref[0:klen, :] += _dot(p.astype(BF16), domb, _TN)
            dqn_ref[rows, :] = dqn.astype(dqn_ref.dtype)
            dqr_ref[rows, :] = dqr

    col = lambda f: pl.BlockSpec((s_dim, LANE), lambda j: (0, f(j)))
    return pl.pallas_call(
        body, name=name, grid=(MLA_HEADS // 2,), in_specs=_att_specs(s_dim) + [col(lambda j: j), col(lambda j: j)],
        out_specs=[col(lambda j: j), pl.BlockSpec((None, s_dim, LANE), lambda j: (j % 2, 0, j // 2)), col(lambda j: j),
                   col(lambda j: j), pl.BlockSpec((None, s_dim, LANE), lambda j: (j, 0, 0))],
        out_shape=[jax.ShapeDtypeStruct((s_dim, 1024), BF16), jax.ShapeDtypeStruct((2, s_dim, 512), F32),
                   jax.ShapeDtypeStruct((s_dim, 1024), F32), jax.ShapeDtypeStruct((s_dim, 1024), F32),
                   jax.ShapeDtypeStruct((MLA_HEADS // 2, s_dim, LANE), F32)],
        scratch_shapes=[pltpu.VMEM((s_dim, 2 * LANE), BF16), pltpu.VMEM((s_dim, LANE), BF16)],
        compiler_params=pltpu.CompilerParams(dimension_semantics=("parallel",)),
    )(q, qr, kv, krt, kv, o, do)


def _all_gather(x, name):
    rows, width = x.shape

    def body(x_ref, out_ref, send_sems, recv_sems, local_sem):
        x_i, y_i, c_i = lax.axis_index("x"), lax.axis_index("y"), lax.axis_index("c")
        me, sibling = (x_i, y_i, c_i), (x_i, y_i, 1 - c_i)
        chips = [(1 - x_i, y_i), (x_i, 1 - y_i), (1 - x_i, 1 - y_i)]

        def slot(px, py, pc):
            return out_ref.at[4 * px + 2 * py + pc]

        def copy(k, block, to, src=None):
            return pltpu.make_async_remote_copy(
                src_ref=slot(*block) if src is None else src, dst_ref=slot(*block), send_sem=send_sems.at[k],
                recv_sem=recv_sems.at[k], device_id=to, device_id_type=pl.DeviceIdType.MESH)

        mine = pltpu.make_async_copy(x_ref, slot(*me), local_sem)
        mine.start()
        first = [copy(0, me, sibling, src=x_ref)]
        first += [copy(1 + j, me, (*chip, c_i), src=x_ref) for j, chip in enumerate(chips)]
        for cp in first:
            cp.start()
        passed = [copy(4 + j, (*chip, c_i), sibling) for j, chip in enumerate(chips)]
        for j, chip in enumerate(chips):
            copy(1 + j, (*chip, c_i), me).wait_recv()
            passed[j].start()
        copy(0, sibling, me).wait_recv()
        for j, chip in enumerate(chips):
            copy(4 + j, (*chip, 1 - c_i), me).wait_recv()
        for cp in first + passed:
            cp.wait_send()
        mine.wait()

    return pl.pallas_call(
        body, name=name, out_shape=jax.ShapeDtypeStruct((N_DEV, rows, width), x.dtype),
        in_specs=[pl.BlockSpec(memory_space=pl.ANY)], out_specs=pl.BlockSpec(memory_space=pl.ANY),
        scratch_shapes=[pltpu.SemaphoreType.DMA((7,)), pltpu.SemaphoreType.DMA((7,)), pltpu.SemaphoreType.DMA],
    )(x)


def _gather_many(shards, name):
    n_arr = len(shards)

    def body(*refs):
        x_refs, out_refs = refs[:n_arr], refs[n_arr:2 * n_arr]
        send_sems, recv_sems, local_sems = refs[2 * n_arr:]
        x_i, y_i, c_i = lax.axis_index("x"), lax.axis_index("y"), lax.axis_index("c")
        me, sibling = (x_i, y_i, c_i), (x_i, y_i, 1 - c_i)
        chips = [(1 - x_i, y_i), (x_i, 1 - y_i), (1 - x_i, 1 - y_i)]

        def copy(a, k, block, to, src=None):
            slot = out_refs[a].at[4 * block[0] + 2 * block[1] + block[2]]
            return pltpu.make_async_remote_copy(
                src_ref=slot if src is None else src, dst_ref=slot, send_sem=send_sems.at[a, k],
                recv_sem=recv_sems.at[a, k], device_id=to, device_id_type=pl.DeviceIdType.MESH)

        mine, first, passed = [], [], []
        for a in range(n_arr):
            mine.append(pltpu.make_async_copy(x_refs[a], out_refs[a].at[4 * x_i + 2 * y_i + c_i], local_sems.at[a]))
            mine[a].start()
            first.append([copy(a, 0, me, sibling, src=x_refs[a])]
                         + [copy(a, 1 + j, me, (*chip, c_i), src=x_refs[a]) for j, chip in enumerate(chips)])
            for cp in first[a]:
                cp.start()
            passed.append([copy(a, 4 + j, (*chip, c_i), sibling) for j, chip in enumerate(chips)])
        for j, chip in enumerate(chips):
            for a in range(n_arr):
                copy(a, 1 + j, (*chip, c_i), me).wait_recv()
                passed[a][j].start()
        for a in range(n_arr):
            copy(a, 0, sibling, me).wait_recv()
            for j, chip in enumerate(chips):
                copy(a, 4 + j, (*chip, 1 - c_i), me).wait_recv()
        for a in range(n_arr):
            for cp in first[a] + passed[a]:
                cp.wait_send()
            mine[a].wait()

    any_spec = pl.BlockSpec(memory_space=pl.ANY)
    return pl.pallas_call(
        body, name=name, out_shape=[jax.ShapeDtypeStruct((N_DEV,) + x.shape, x.dtype) for x in shards],
        in_specs=[any_spec] * n_arr, out_specs=[any_spec] * n_arr,
        scratch_shapes=[pltpu.SemaphoreType.DMA((n_arr, 7)), pltpu.SemaphoreType.DMA((n_arr, 7)),
                        pltpu.SemaphoreType.DMA((n_arr,))],
    )(*shards)


_HBM = pl.BlockSpec(memory_space=pltpu.HBM)
_SEM = pl.BlockSpec(memory_space=pltpu.SEMAPHORE)


def _plan_copies(plan, src_refs, land_refs, send_sems, recv_sems):
    copies = []
    for s_ref, l_ref in zip(src_refs, land_refs):
        for src, dst, peer in plan(s_ref, l_ref):
            k = len(copies)
            copies.append(pltpu.make_async_remote_copy(
                src_ref=src, dst_ref=dst, send_sem=send_sems.at[k], recv_sem=recv_sems.at[k], device_id=peer,
                device_id_type=pl.DeviceIdType.MESH))
    return copies


def _split_start(srcs, lands, plan, n_copy, name, after=None):
    n = len(srcs)
    n_in = 2 * n + (after is not None)

    def body(*refs):
        for cp in _plan_copies(plan, refs[:n], refs[n:2 * n], refs[n_in], refs[n_in + 1]):
            cp.start()
        refs[-1][...] = jnp.zeros_like(refs[-1])

    sems = pltpu.SemaphoreType.DMA((n * n_copy,))
    res = pl.pallas_call(
        body, name=name,
        out_shape=(sems, sems, *[pltpu.HBM(a.shape, a.dtype) for a in list(srcs) + list(lands)],
                   jax.ShapeDtypeStruct((8, LANE), F32)),
        in_specs=[_HBM] * (2 * n) + [pl.BlockSpec(memory_space=pl.ANY)] * (after is not None),
        out_specs=(_SEM, _SEM, *[_HBM] * (2 * n), pl.BlockSpec(memory_space=pltpu.VMEM)),
        input_output_aliases={i: 2 + i for i in range(2 * n)},
        compiler_params=pltpu.CompilerParams(has_side_effects=pltpu.SideEffectType.DATAFLOW_SIDE_EFFECTING),
    )(*[pltpu.with_memory_space_constraint(a, pltpu.HBM) for a in list(srcs) + list(lands)],
      *([after] if after is not None else []))
    return res[0], res[1], list(res[2:2 + n]), list(res[2 + n:2 + 2 * n]), res[-1]


def _split_wait(send_sems, recv_sems, srcs, lands, after, plan, name):
    n = len(srcs)

    def body(*refs):
        copies = _plan_copies(plan, refs[:n], refs[n:2 * n], refs[2 * n], refs[2 * n + 1])
        for cp in copies:
            cp.wait_send()
        for cp in copies:
            cp.wait_recv()

    res = pl.pallas_call(
        body, name=name, out_shape=tuple(pltpu.HBM(a.shape, a.dtype) for a in list(srcs) + list(lands)),
        in_specs=[_HBM] * (2 * n) + [_SEM, _SEM, pl.BlockSpec(memory_space=pl.ANY)], out_specs=tuple([_HBM] * (2 * n)),
        input_output_aliases={i: i for i in range(2 * n)},
        compiler_params=pltpu.CompilerParams(has_side_effects=pltpu.SideEffectType.DATAFLOW_SIDE_EFFECTING),
    )(*srcs, *lands, send_sems, recv_sems, after)
    return list(res[:n]), list(res[n:])


def _plan_broadcast(src, land):
    x_i, y_i, c_i = lax.axis_index("x"), lax.axis_index("y"), lax.axis_index("c")
    me = 4 * x_i + 2 * y_i + c_i
    return [(src, land.at[me], (x_i ^ (k >> 2), y_i ^ ((k >> 1) & 1), c_i ^ (k & 1))) for k in range(1, N_DEV)]


def _plan_scatter(src, land):
    x_i, y_i, c_i = lax.axis_index("x"), lax.axis_index("y"), lax.axis_index("c")
    me = 4 * x_i + 2 * y_i + c_i
    plan = []
    for k in range(1, N_DEV):
        px, py, pc = x_i ^ (k >> 2), y_i ^ ((k >> 1) & 1), c_i ^ (k & 1)
        plan.append((src.at[4 * px + 2 * py + pc], land.at[me], (px, py, pc)))
    return plan


def _adam_math(g, w, m, v):
    m_new = ADAM_B1 * m + (1.0 - ADAM_B1) * g
    v_new = ADAM_B2 * v + (1.0 - ADAM_B2) * (g * g)
    m_hat = m_new / (1.0 - ADAM_B1 ** ADAM_STEP)
    v_hat = v_new / (1.0 - ADAM_B2 ** ADAM_STEP)
    return -ADAM_LR * (m_hat / (jnp.sqrt(v_hat) + ADAM_EPS) + ADAM_WD * w), m_new, v_new


def _adam(slots, w, m, v, name, own=None, own_idx=None):
    n_slot, rows, cols = slots.shape
    tr = ROW_TILE if rows % ROW_TILE == 0 else rows
    has_own = own is not None

    def body(*refs):
        if has_own:
            idx_ref, own_ref, refs = refs[0], refs[1], refs[2:]
        s_ref, w_ref, m_ref, v_ref, g_ref, d_ref, mo_ref, vo_ref = refs
        g = own_ref[...].astype(F32) if has_own else s_ref[0].astype(F32)
        for k in range(0 if has_own else 1, n_slot):
            part = s_ref[k].astype(F32)
            g = g + (jnp.where(idx_ref[0] == k, 0.0, part) if has_own else part)
        g_ref[...] = g
        d_ref[...], mo_ref[...], vo_ref[...] = _adam_math(g, w_ref[...], m_ref[...], v_ref[...])

    spec = pl.BlockSpec((tr, cols), lambda i, *_: (i, 0))
    in_specs = [pl.BlockSpec((n_slot, tr, cols), lambda i, *_: (0, i, 0)), spec, spec, spec]
    if has_own:
        in_specs = [pl.BlockSpec((None, tr, cols), lambda i, idx: (idx[0], i, 0))] + in_specs
    grid_spec = pltpu.PrefetchScalarGridSpec(num_scalar_prefetch=1 if has_own else 0, grid=(rows // tr,), in_specs=in_specs,
                                             out_specs=[spec] * 4)
    ins = ([own_idx, own] if has_own else []) + [slots, w, m, v]
    return pl.pallas_call(
        body, name=name, grid_spec=grid_spec, out_shape=[jax.ShapeDtypeStruct((rows, cols), F32)] * 4,
        compiler_params=pltpu.CompilerParams(dimension_semantics=("parallel",)),
    )(*ins)


PACK_ROWS, PACK_W = 24, 1536
REPL_W = (("ssd_conv_b", 1536), ("ssd_dt_bias", 16), ("ssd_A_log", 16), ("ssd_D", 16), ("ssd_norm_w", 1024),
          ("mla_q_norm_w", 384), ("mla_kv_norm_w", 256), ("mla_out_norm_w", 1024), ("ln_mix_g", 1024),
          ("ln_mix_b", 1024), ("ln_ffn_g", 1024), ("ln_ffn_b", 1024))
LOSS_ROW = 4 + len(REPL_W)


def _pack_small(conv_w_grad, grads, loss, name="pack_small"):
    def body(*refs):
        cw_ref, g_refs, loss_ref, o_ref = refs[0], refs[1:1 + len(REPL_W)], refs[1 + len(REPL_W)], refs[-1]
        o_ref[...] = jnp.zeros_like(o_ref)
        o_ref[0:4, :] = cw_ref[...]
        for i, g_ref in enumerate(g_refs):
            o_ref[4 + i:5 + i, 0:g_ref.shape[1]] = g_ref[...]
        o_ref[LOSS_ROW:LOSS_ROW + 1, 0:LANE] = loss_ref[...]

    return pl.pallas_call(body, name=name, out_shape=jax.ShapeDtypeStruct((PACK_ROWS, PACK_W), F32))(conv_w_grad, *grads, loss)


def _adam_small(gathered, wmv, name="adam_small"):
    def body(*refs):
        s_ref = refs[0]
        in_refs = refs[1:1 + 3 * len(REPL_W)]
        cw_ref, loss_ref = refs[1 + 3 * len(REPL_W)], refs[2 + 3 * len(REPL_W)]
        out_refs = refs[3 + 3 * len(REPL_W):-1]
        tot = refs[-1]
        acc = s_ref[0]
        for k in range(1, N_DEV):
            acc = acc + s_ref[k]
        tot[...] = acc
        cw_ref[...] = tot[0:4, :]
        loss_ref[...] = tot[LOSS_ROW:LOSS_ROW + 1, 0:LANE]
        for i, (_, width) in enumerate(REPL_W):
            g = tot[4 + i:5 + i, 0:width]
            w_ref, m_ref, v_ref = in_refs[3 * i:3 * i + 3]
            g_ref, d_ref, mo_ref, vo_ref = out_refs[4 * i:4 * i + 4]
            g_ref[...] = g
            d_ref[...], mo_ref[...], vo_ref[...] = _adam_math(g, w_ref[...], m_ref[...], v_ref[...])

    flat_in = [a for triple in wmv for a in triple]
    out_shape = [jax.ShapeDtypeStruct((4, PACK_W), F32), jax.ShapeDtypeStruct((1, LANE), F32)]
    for _, width in REPL_W:
        out_shape += [jax.ShapeDtypeStruct((1, width), F32)] * 4
    res = pl.pallas_call(body, name=name, out_shape=out_shape, scratch_shapes=[pltpu.VMEM((PACK_ROWS, PACK_W), F32)])(
        gathered, *flat_in)
    return res[0], res[1], [res[2 + 4 * i:6 + 4 * i] for i in range(len(REPL_W))]


def _cols_full(g):
    return jnp.transpose(g, (1, 0, 2)).reshape(g.shape[1], -1)


def _cols_split(full):
    k_dim, n_dim = full.shape
    return jnp.transpose(full.reshape(k_dim, N_DEV, n_dim // N_DEV), (1, 0, 2))


PROJ_BLOCK = {"z": (1024, 0), "dt": (LANE, 8), "q_c": (MLA_Q_RANK, 3), "xbc": (SSD_XBC, 1), "kv_c": (MLA_KV_RANK, 12),
              "k_rope": (LANE, 26)}


def _win_pad(wt):
    z = lambda n: jnp.zeros((n, wt.shape[1]), wt.dtype)
    return jnp.concatenate([wt[:1024], wt[2560:2576], z(112), wt[2576:2960], wt[1024:2560], wt[2960:3216], wt[3216:3248],
                            z(96)], axis=0)


def _win_unpad(wt):
    return jnp.concatenate([wt[:1024], wt[1536:3072], wt[1024:1040], wt[1152:1536], wt[3072:3328], wt[3328:3360]], axis=0)


def _heads_split_t(wt, a, b):
    w3 = wt.reshape(MLA_HEADS, a + b, wt.shape[1])
    return jnp.concatenate([w3[:, :a].reshape(-1, wt.shape[1]), w3[:, a:].reshape(-1, wt.shape[1])], axis=0)


def _heads_merge_t(wt, a, b):
    wa = wt[:MLA_HEADS * a].reshape(MLA_HEADS, a, wt.shape[1])
    wb = wt[MLA_HEADS * a:].reshape(MLA_HEADS, b, wt.shape[1])
    return jnp.concatenate([wa, wb], axis=1).reshape(-1, wt.shape[1])


def _heads_split(w, a, b):
    k_dim = w.shape[0]
    w3 = w.reshape(k_dim, MLA_HEADS, a + b)
    return jnp.concatenate([w3[:, :, :a].reshape(k_dim, -1), w3[:, :, a:].reshape(k_dim, -1)], axis=1)


def _heads_merge(w, a, b):
    k_dim = w.shape[0]
    wa = w[:, :MLA_HEADS * a].reshape(k_dim, MLA_HEADS, a)
    wb = w[:, MLA_HEADS * a:].reshape(k_dim, MLA_HEADS, b)
    return jnp.concatenate([wa, wb], axis=2).reshape(k_dim, -1)


def _pad_lanes(v, width=LANE):
    return jnp.concatenate([v, jnp.zeros((v.shape[0], width - v.shape[1]), v.dtype)], axis=1)


def _local_step(x, p, positions, tgt, W, P, comm=None):
    comm = comm or {}
    zero_tok = jnp.zeros((8, LANE), F32)
    s_dim = x.shape[0]
    inv_freq = 1.0 / (ROPE_BASE ** (jnp.arange(0, MLA_ROPE, 2, dtype=F32) / MLA_ROPE))
    ang = positions.astype(F32)[:, None] * inv_freq
    cos, sin = jnp.cos(ang), jnp.sin(ang)
    cos32 = jnp.concatenate([cos, cos], axis=1)
    sin32 = jnp.concatenate([-sin, sin], axis=1)
    cos512, sin512 = jnp.tile(cos32, (1, 16)), jnp.tile(sin32, (1, 16))
    cos128, sin128 = jnp.tile(cos32, (1, 4)), jnp.tile(sin32, (1, 4))
    bias_p, alog_p = _pad_lanes(P["ssd_dt_bias"]), _pad_lanes(P["ssd_A_log"])
    d_x = jnp.repeat(P["ssd_D"], SSD_HEAD_DIM, axis=1)

    xb, pb = x.astype(BF16), p.astype(BF16)
    proj = _mm(xb, W["w_in"], tb=True, after=comm.get("token0", zero_tok), name="mm_in")
    z, qc, kvc, kr = [(proj,) + PROJ_BLOCK[n] for n in ("z", "q_c", "kv_c", "k_rope")]
    xbca = _conv_fwd(proj, PROJ_BLOCK["xbc"][1], P["ssd_conv_w"], P["ssd_conv_b"])
    y, states = _ssd_fwd(xbca, proj, PROJ_BLOCK["dt"][1], bias_p, alog_p, d_x)
    (yssd,) = _rowwise(_gate_rms, [y, z], [P["ssd_norm_w"]], [(1024, BF16)], name="ssd_gate_norm")
    qn, kvn, krt = _rowwise(lambda a, c, u, cs, sn, wq, wkv: (_rms(a, wq), _rms(c, wkv), _spread4(_rope_fwd_fn(u, cs, sn))),
                            [qc, kvc, kr, cos128, sin128], [P["mla_q_norm_w"], P["mla_kv_norm_w"]],
                            [(MLA_Q_RANK, BF16), (MLA_KV_RANK, BF16), LANE], name="qkv_norm_rope_k")
    q = _mm(qn, W["mla_w_q_b"], tb=True, name="mm_q")
    kv = _mm(kvn, W["mla_w_kv_b"], name="mm_kv")
    (qr,) = _rowwise(_rope_fwd_fn, [(q, 512, 2), cos512, sin512], [], [512], name="rope_q")
    att = _att_fwd(q, qr, kv, krt)
    (ymla,) = _rowwise(_rms, [att], [P["mla_out_norm_w"]], [(1024, BF16)], name="out_norm")
    ycat = jnp.concatenate([yssd, ymla], axis=1)
    if "late_weights" in comm:
        W = {**W, **comm["late_weights"](ycat)}
    mix = _mm(ycat, W["w_out"], name="mm_out")
    f_h1 = lambda xv, mv, g, b: _ln(ALPHA * xv + mv, g, b)
    h1, h1b = _rowwise(lambda *a: (f_h1(*a),) * 2, [x, mix], [P["ln_mix_g"], P["ln_mix_b"]], [1024, (1024, BF16)],
                       name="ln_mix")
    hg = _mm(h1b, W["w_ffn_gate"], tb=True, out_dtype=BF16, name="mm_gate")
    hu, act = _mm(h1b, W["w_ffn_up"], tb=True, name="mm_up",
                  epilogue=(lambda u, g: (u, _silu(g.astype(F32)) * u), [hg], [BF16, BF16]))
    pg = _mm(h1b, W["w_ple_gate"], name="mm_ple_gate")
    pp = _mm(pb, W["w_ple_proj"], name="mm_ple")
    ffn = _mm(act, W["w_ffn_down"], name="mm_down")

    f_h2 = lambda hv, fv, pg, ppv, g, b: _ln(ALPHA * hv + fv + _sigmoid(pg) * ppv, g, b)

    def final_fn(hv, fv, pg, ppv, tv, g, b):
        h2, pull = jax.vjp(f_h2, hv, fv, pg, ppv, g, b)
        diff = h2 - tv
        loss = 0.5 * jnp.sum(jnp.mean(diff * diff, axis=-1, keepdims=True), axis=0, keepdims=True)
        d_h, d_f, d_pg, d_pp, d_g, d_b = pull(diff * (1.0 / D_MODEL))
        return d_h, d_f, d_pg, d_pp, d_g, d_b, jnp.broadcast_to(loss, (1, LANE))

    dh1_a, dffn, dpg, dpp, g_ffn_g, g_ffn_b, loss = _rowwise(
        final_fn, [h1, ffn, pg, pp, tgt], [P["ln_ffn_g"], P["ln_ffn_b"]], [1024] + [(1024, BF16)] * 3,
        [1024, 1024, LANE], name="final")

    G = {}
    def swiglu_bwd(d, g, u):
        g, u = g.astype(F32), u.astype(F32)
        sg = _sigmoid(g)
        return d * u * (sg * (1.0 + g * (1.0 - sg))), d * (g * sg)

    dg, du = _mm(dffn, W["w_ffn_down"], tb=True, name="mm_down_dx",
                 epilogue=(swiglu_bwd, [hg, hu], [BF16, BF16]))
    G["w_ffn_down"] = _mm(act, dffn, ta=True, out_dtype=GRAD_DT, name="mm_down_dw")
    dh1 = _mm(dg, W["w_ffn_gate"], add=dh1_a, name="mm_gate_dx")
    dh1 = _mm(du, W["w_ffn_up"], add=dh1, name="mm_up_dx")
    dh1 = _mm(dpg, W["w_ple_gate"], tb=True, add=dh1, name="mm_ple_gate_dx")
    G["w_ffn_gate"] = _mm(dg, h1b, ta=True, out_dtype=GRAD_DT, name="mm_gate_dw")
    G["w_ffn_up"] = _mm(du, h1b, ta=True, out_dtype=GRAD_DT, name="mm_up_dw")
    G["w_ple_gate"] = _mm(h1b, dpg, ta=True, out_dtype=GRAD_DT, name="mm_ple_gate_dw")
    G["w_ple_proj"] = _mm(pb, dpp, ta=True, out_dtype=GRAD_DT, name="mm_ple_dw")
    dx_a, dmix, g_mix_g, g_mix_b = _rowwise(
        lambda xv, mv, dv, g, b: _vjp_rows(f_h1)(xv, mv, g, b, dv), [x, mix, dh1], [P["ln_mix_g"], P["ln_mix_b"]],
        [1024, (1024, BF16)], [1024, 1024], name="ln_mix_bwd")
    dycat = _mm(dmix, W["w_out"], tb=True, name="mm_out_dx")
    G["w_out"] = _mm(ycat, dmix, ta=True, out_dtype=GRAD_DT, name="mm_out_dw")

    grads_done = comm.get("grads", lambda group, grads: zero_tok)
    tok1 = grads_done("ffn", G)
    datt, g_out_norm = _rowwise(lambda a, dv, w, t: _vjp_rows(_rms)(a, w, dv + jnp.min(t)), [att, (dycat, 1024, 1)],
                                [P["mla_out_norm_w"], tok1], [1024], [1024], name="out_norm_bwd")
    dqn_nope, dqr, dkn, dv, dkrt = _att_bwd(q, qr, kv, krt, att, datt)
    dkv = jnp.concatenate([dkn, dv], axis=1)
    (dq_rope,) = _rowwise(lambda d0, d1, c, s: _rope_bwd_fn(d0 + d1, c, s), [(dqr, 512, 0), (dqr, 512, 1), cos512, sin512],
                          [], [(512, BF16)], name="rope_q_bwd")

    def rope_k_bwd(*a):
        d = _spread4(functools.reduce(lambda u, w: u + w, a[:-2]))
        lane = lax.broadcasted_iota(jnp.int32, d.shape, 1)
        return _rope_bwd_fn(jnp.where(lane < MLA_ROPE, d, 0.0), a[-2], a[-1])

    (dkr,) = _rowwise(rope_k_bwd, [(dkrt, LANE, k) for k in range(MLA_HEADS // 2)] + [cos128, sin128], [], [(LANE, BF16)],
                      name="rope_k_bwd")
    dq = jnp.concatenate([dqn_nope, dq_rope], axis=1)
    dqn = _mm(dq, W["mla_w_q_b"], name="mm_q_dx")
    G["mla_w_q_b"] = _mm(dq, qn, ta=True, out_dtype=GRAD_DT, name="mm_q_dw")
    dkvn = _mm(dkv, W["mla_w_kv_b"], tb=True, name="mm_kv_dx")
    G["mla_w_kv_b"] = _mm(kvn, dkv, ta=True, out_dtype=GRAD_DT, name="mm_kv_dw")
    tok2 = grads_done("mla", G)
    def qkv_norm_bwd(a, da, c, dc, wq, wkv, t):
        (d_a, d_wq), (d_c, d_wkv) = _vjp_rows(_rms)(a, wq, da + jnp.min(t)), _vjp_rows(_rms)(c, wkv, dc)
        return d_a, d_c, d_wq, d_wkv

    dqc, dkvc, g_q_norm, g_kv_norm = _rowwise(
        qkv_norm_bwd, [qc, dqn, kvc, dkvn], [P["mla_q_norm_w"], P["mla_kv_norm_w"], tok2],
        [(MLA_Q_RANK, BF16), (MLA_KV_RANK, BF16)], [MLA_Q_RANK, MLA_KV_RANK], name="qkv_norm_bwd")

    dy, dz, g_ssd_norm = _rowwise(lambda yv, zv, dv, w, t: _vjp_rows(_gate_rms)(yv, zv, w, dv + jnp.min(t)),
                                  [y, z, (dycat, 1024, 0)], [P["ssd_norm_w"], tok1], [1024, (1024, BF16)], [1024],
                                  name="ssd_gate_norm_bwd")
    dxbca, ddtr, g_dt_bias, g_alog, g_d = _ssd_bwd(xbca, proj, PROJ_BLOCK["dt"][1], bias_p, alog_p, d_x, states, dy)
    da, g_conv_w, g_conv_b = _conv_bwd_pre(proj, PROJ_BLOCK["xbc"][1], P["ssd_conv_w"], P["ssd_conv_b"], dxbca)
    dxbc = _conv_bwd_in(da, P["ssd_conv_w"])

    small = {
        "ssd_conv_b": g_conv_b, "ssd_dt_bias": g_dt_bias, "ssd_A_log": g_alog, "ssd_D": g_d, "ssd_norm_w": g_ssd_norm,
        "mla_q_norm_w": g_q_norm, "mla_kv_norm_w": g_kv_norm, "mla_out_norm_w": g_out_norm, "ln_mix_g": g_mix_g,
        "ln_mix_b": g_mix_b, "ln_ffn_g": g_ffn_g, "ln_ffn_b": g_ffn_b,
    }
    packed = _pack_small(g_conv_w, [small[n] for n, _ in REPL_W], loss)
    if "small" in comm:
        comm["small"](packed)

    dproj = jnp.concatenate([dz, ddtr, dqc, dxbc, dkvc, dkr], axis=1)
    G["w_in"] = _mm(dproj, xb, ta=True, out_dtype=GRAD_DT, name="mm_in_dw")
    grad_x = _mm(dproj, W["w_in"], add=dx_a, after=grads_done("in", G), name="mm_in_dx")
    return grad_x, G, packed


def kernel(x, p, positions, w_in, ssd_conv_w, ssd_conv_b, ssd_dt_bias, ssd_A_log, ssd_D, ssd_norm_w, mla_q_norm_w, mla_w_q_b, mla_kv_norm_w, mla_w_kv_b, mla_out_norm_w, w_out, ln_mix_g, ln_mix_b, w_ffn_gate, w_ffn_up, w_ffn_down, w_ple_gate, w_ple_proj, ln_ffn_g, ln_ffn_b, loss_target, m_w_in, m_ssd_conv_w, m_ssd_conv_b, m_ssd_dt_bias, m_ssd_A_log, m_ssd_D, m_ssd_norm_w, m_mla_q_norm_w, m_mla_w_q_b, m_mla_kv_norm_w, m_mla_w_kv_b, m_mla_out_norm_w, m_w_out, m_ln_mix_g, m_ln_mix_b, m_w_ffn_gate, m_w_ffn_up, m_w_ffn_down, m_w_ple_gate, m_w_ple_proj, m_ln_ffn_g, m_ln_ffn_b, v_w_in, v_ssd_conv_w, v_ssd_conv_b, v_ssd_dt_bias, v_ssd_A_log, v_ssd_D, v_ssd_norm_w, v_mla_q_norm_w, v_mla_w_q_b, v_mla_kv_norm_w, v_mla_w_kv_b, v_mla_out_norm_w, v_w_out, v_ln_mix_g, v_ln_mix_b, v_w_ffn_gate, v_w_ffn_up, v_w_ffn_down, v_w_ple_gate, v_w_ple_proj, v_ln_ffn_g, v_ln_ffn_b):
    args = dict(locals())
    core = lax.axis_index("c")
    me = 4 * lax.axis_index("x") + 2 * lax.axis_index("y") + core

    conv_sh = ssd_conv_w[0]
    conv_hi = conv_sh.astype(BF16)
    conv_lo = (conv_sh - conv_hi.astype(F32)).astype(BF16)
    stored = lambda n, pre="": jnp.transpose(args[pre + n][0]) if n in TRANSPOSED else args[pre + n][0]
    shards = {n: stored(n).astype(BF16) for n in BIG}
    rows_full = lambda g: g.reshape(-1, g.shape[2])

    early = _gather_many([shards[n] for n in EARLY] + [jnp.concatenate([conv_hi, conv_lo], axis=0)], "gather_early")
    gw = dict(zip(EARLY, early[:-1]))
    conv_g = early[-1].astype(F32)
    W = {
        "w_in": _win_pad(rows_full(gw["w_in"])),
        "mla_w_q_b": _heads_split_t(rows_full(gw["mla_w_q_b"]), MLA_NOPE, MLA_ROPE),
        "mla_w_kv_b": _heads_split(_cols_full(gw["mla_w_kv_b"]), MLA_NOPE, MLA_V),
    }
    P = {n: args[n] for n, _ in REPL_W}
    P["ssd_conv_w"] = _cols_full(conv_g[:, :4] + conv_g[:, 4:])

    lands = [lax.dynamic_update_slice(lax.empty((N_DEV,) + shards[n].shape, BF16), shards[n][None], (me, 0, 0)) for n in LATE]
    late_sems = _split_start([shards[n] for n in LATE], lands, _plan_broadcast, N_DEV - 1, "gather_late_start",
                             after=early[0])

    def late_weights(after):
        _, got = _split_wait(*late_sems[:4], after, _plan_broadcast, "gather_late_wait")
        lw = dict(zip(LATE, got))
        return {"w_out": rows_full(lw["w_out"]), "w_ple_gate": rows_full(lw["w_ple_gate"]),
                "w_ple_proj": _cols_full(lw["w_ple_proj"]), "w_ffn_gate": rows_full(lw["w_ffn_gate"]),
                "w_ffn_up": rows_full(lw["w_ffn_up"]), "w_ffn_down": rows_full(lw["w_ffn_down"])}

    def to_blocks(n, g):
        if n == "w_in":
            g = _win_unpad(g)
        elif n == "mla_w_q_b":
            g = _heads_merge_t(g, MLA_NOPE, MLA_ROPE)
        elif n == "mla_w_kv_b":
            g = _heads_merge(g, MLA_NOPE, MLA_V)
        if n in ROW_SHARDED or n in TRANSPOSED:
            return g.reshape(N_DEV, -1, g.shape[1])
        return _cols_split(g)

    flight = {}

    def grads(group, G):
        gl = [to_blocks(n, G[n]) for n in GRAD_GROUPS[group]]
        flight[group] = _split_start(gl, [lax.empty(g.shape, g.dtype) for g in gl], _plan_scatter, N_DEV - 1,
                                     "grads_" + group + "_start", after=flight.get("small") if group == "in" else None)
        return flight[group][4]

    def small(packed):
        flight["small"] = _all_gather(packed, "gather_small")

    grad_x, G, packed = _local_step(x[0], p[0, 0], positions[0], loss_target[0], W, P,
                                    comm={"token0": late_sems[4], "late_weights": late_weights, "grads": grads, "small": small})

    me_arr = me.astype(jnp.int32).reshape(1)
    big_out = {}

    def finish(group, after):
        mine, recv = _split_wait(*flight[group][:4], after, _plan_scatter, "grads_" + group + "_wait")
        for n, g, r in zip(GRAD_GROUPS[group], mine, recv):
            big_out[n] = _adam(r, stored(n), stored(n, "m_"), stored(n, "v_"), "adam_" + n, own=g, own_idx=me_arr)
        return big_out[GRAD_GROUPS[group][-1]][0]

    done = finish("ffn", grad_x)
    conv_sum, loss_row, small_out = _adam_small(flight["small"], [(args[n], args["m_" + n], args["v_" + n]) for n, _ in REPL_W])
    finish("in", finish("mla", done))
    conv_grad = lax.dynamic_slice_in_dim(conv_sum, me * 192, 192, axis=1)
    conv_out = _adam(conv_grad[None], conv_sh, m_ssd_conv_w[0], v_ssd_conv_w[0], "adam_conv")
    small_map = {n: small_out[i] for i, (n, _) in enumerate(REPL_W)}

    def outputs(idx):
        res = []
        for n in WEIGHT_ORDER:
            if n == "ssd_conv_w":
                res.append(conv_out[idx][None])
            elif n in big_out:
                res.append((jnp.transpose(big_out[n][idx]) if n in TRANSPOSED else big_out[n][idx])[None])
            else:
                res.append(small_map[n][idx])
        return res

    return (loss_row[0, 0], grad_x[None], *outputs(0), *outputs(1), *outputs(2), *outputs(3))
```

```python
import functools
import math

import numpy as np
import jax
import jax.numpy as jnp
from jax import lax
from jax.experimental import pallas as pl
from jax.experimental.pallas import tpu as pltpu

F32 = jnp.float32
BF16 = jnp.bfloat16
HI = lax.Precision.HIGHEST

N_DEV = 8
D_MODEL = 1024
PLE_DIM = 256
SSD_HEADS = 16
SSD_HEAD_DIM = 64
SSD_INNER = 1024
SSD_STATE = 128
SSD_XBC = 1536
SSD_CHUNK = 128
MLA_HEADS = 16
MLA_Q_RANK = 384
MLA_KV_RANK = 256
MLA_NOPE = 64
MLA_ROPE = 32
MLA_V = 64
ROPE_BASE = 10000.0
D_FF = 2816
IN_WIDTH = 3248
IN_PAD = 3456
ALPHA = 2.0 ** 0.25
EPS = 1e-6
LN_EPS = 1e-5
ATT_SCALE = 1.0 / math.sqrt(MLA_NOPE + MLA_ROPE)
ADAM_LR, ADAM_B1, ADAM_B2, ADAM_EPS, ADAM_WD, ADAM_STEP = 0.001, 0.9, 0.999, 1e-08, 0.01, 10

LANE = 128
MXU_DIM = 256
MM_TM, MM_TN, MM_TK = 1408, 1408, 2048
ROW_TILE = 256
ATT_TQ = 256

GRAD_DT = BF16

BIG = ("w_in", "mla_w_q_b", "mla_w_kv_b", "w_out", "w_ffn_gate", "w_ffn_up", "w_ffn_down", "w_ple_gate", "w_ple_proj")
EARLY = ("w_in", "mla_w_q_b", "mla_w_kv_b")
LATE = {"out": ("w_out", "w_ple_gate", "w_ple_proj"), "ffn": ("w_ffn_gate", "w_ffn_up", "w_ffn_down")}
GRAD_GROUPS = {"ffn": ("w_ffn_gate", "w_ffn_up", "w_ffn_down", "w_ple_gate", "w_ple_proj", "w_out"),
               "mla": ("mla_w_q_b", "mla_w_kv_b"), "in": ("w_in",)}
ROW_SHARDED = ("w_out", "w_ffn_down", "w_ple_gate")
TRANSPOSED = ("w_in", "mla_w_q_b", "w_ffn_gate", "w_ffn_up")
WEIGHT_ORDER = ("w_in", "ssd_conv_w", "ssd_conv_b", "ssd_dt_bias", "ssd_A_log", "ssd_D", "ssd_norm_w", "mla_q_norm_w",
                "mla_w_q_b", "mla_kv_norm_w", "mla_w_kv_b", "mla_out_norm_w", "w_out", "ln_mix_g", "ln_mix_b",
                "w_ffn_gate", "w_ffn_up", "w_ffn_down", "w_ple_gate", "w_ple_proj", "ln_ffn_g", "ln_ffn_b")


def _tile(dim, cap, prefer=None):
    cands = [t for t in range(LANE, min(cap, dim) + 1, LANE) if dim % t == 0]
    if not cands:
        return dim
    if prefer is None:
        return max(cands)
    fill = lambda t: t / (MXU_DIM * -(-t // MXU_DIM))
    good = min(0.9, max(fill(t) for t in cands))
    return min((t for t in cands if fill(t) >= good), key=lambda t: abs(t - prefer))


def _dot(a, b, dims=(((1,), (0,)), ((), ())), precision=None):
    return lax.dot_general(a, b, dims, preferred_element_type=F32, precision=precision)


_NT = (((1,), (1,)), ((), ()))
_TN = (((0,), (0,)), ((), ()))


def _mm(a, b, *, ta=False, tb=False, add=None, out_dtype=F32, after=None, epilogue=None, name):
    k_dim, m_dim = a.shape if ta else a.shape[::-1]
    n_dim, kb = b.shape if tb else b.shape[::-1]
    assert k_dim == kb
    tm, tn, tk = _tile(m_dim, MM_TM), _tile(n_dim, MM_TN, prefer=1024), _tile(k_dim, MM_TK, prefer=MM_TK)
    nk = k_dim // tk
    dims = (((0 if ta else 1,), (1 if tb else 0,)), ((), ()))
    a_spec = pl.BlockSpec((tk, tm), lambda i, j, k: (k, i)) if ta else pl.BlockSpec((tm, tk), lambda i, j, k: (i, k))
    b_spec = pl.BlockSpec((tn, tk), lambda i, j, k: (j, k)) if tb else pl.BlockSpec((tk, tn), lambda i, j, k: (k, j))
    o_spec = pl.BlockSpec((tm, tn), lambda i, j, k: (i, j))
    epi_fn, epi_in, out_dtypes = epilogue if epilogue else (None, [], [out_dtype])
    tiles = ([add] if add is not None else []) + list(epi_in)
    n_out = len(out_dtypes)

    def body(*refs):
        a_ref, b_ref = refs[:2]
        tile_refs = refs[2:2 + len(tiles)]
        out_refs = refs[len(refs) - n_out - (nk > 1):len(refs) - (nk > 1)]
        part = _dot(a_ref[...].astype(BF16), b_ref[...].astype(BF16), dims)
        if add is not None:
            part_add = lambda v: v + tile_refs[0][...]
        else:
            part_add = lambda v: v

        def write(total):
            extra = [r[...] for r in tile_refs[add is not None:]]
            outs = epi_fn(total, *extra) if epi_fn else (total,)
            for o_ref, val in zip(out_refs, outs):
                o_ref[...] = val.astype(o_ref.dtype)

        if nk == 1:
            write(part_add(part))
            return
        acc = refs[-1]
        k = pl.program_id(2)

        @pl.when(k == 0)
        def _():
            acc[...] = part_add(part)

        @pl.when(k > 0)
        def _():
            acc[...] += part

        @pl.when(k == nk - 1)
        def _():
            write(acc[...])

    ins = [a, b] + tiles + ([after] if after is not None else [])
    specs = [a_spec, b_spec] + [o_spec] * len(tiles) + ([pl.BlockSpec(memory_space=pl.ANY)] if after is not None else [])
    res = pl.pallas_call(
        body, name=name, grid=(m_dim // tm, n_dim // tn, nk), in_specs=specs, out_specs=[o_spec] * n_out,
        out_shape=[jax.ShapeDtypeStruct((m_dim, n_dim), dt) for dt in out_dtypes],
        scratch_shapes=[pltpu.VMEM((tm, tn), F32)] if nk > 1 else [],
        compiler_params=pltpu.CompilerParams(dimension_semantics=("parallel", "parallel", "arbitrary")),
    )(*ins)
    return res if epilogue else res[0]


def _rowwise(fn, rows, consts, out_widths, acc_widths=(), *, name, tr=ROW_TILE):
    row_arrays, row_specs = [], []
    first_arr = rows[0][0] if isinstance(rows[0], tuple) else rows[0]
    s_dim = first_arr.shape[-2]
    tr = min(tr, s_dim)
    for r in rows:
        arr, width, cb = r if isinstance(r, tuple) else (r, r.shape[-1], 0)
        row_arrays.append(arr)
        if arr.ndim == 3:
            row_specs.append(pl.BlockSpec((None, tr, width), functools.partial(lambda i, k: (k, i, 0), k=cb)))
        else:
            row_specs.append(pl.BlockSpec((tr, width), functools.partial(lambda i, cb: (i, cb), cb=cb)))
    const_specs = [pl.BlockSpec(c.shape, lambda i: (0, 0)) for c in consts]
    nr, nc, no, na = len(rows), len(consts), len(out_widths), len(acc_widths)

    def body(*refs):
        ins = [r[...] for r in refs[:nr + nc]]
        res = fn(*ins)
        if not isinstance(res, (tuple, list)):
            res = (res,)
        out_refs = refs[nr + nc:nr + nc + no]
        acc_refs = refs[nr + nc + no:]
        for o_ref, val in zip(out_refs, res[:no]):
            o_ref[...] = val.astype(o_ref.dtype)
        first = pl.program_id(0) == 0
        for a_ref, val in zip(acc_refs, res[no:]):
            @pl.when(first)
            def _(a_ref=a_ref, val=val):
                a_ref[...] = val

            @pl.when(jnp.logical_not(first))
            def _(a_ref=a_ref, val=val):
                a_ref[...] += val

    outs = [w if isinstance(w, tuple) else (w, F32) for w in out_widths]
    out_shape = [jax.ShapeDtypeStruct((s_dim, w), dt) for w, dt in outs]
    out_shape += [jax.ShapeDtypeStruct((1, w), F32) for w in acc_widths]
    out_specs = [pl.BlockSpec((tr, w), lambda i: (i, 0)) for w, _ in outs]
    out_specs += [pl.BlockSpec((1, w), lambda i: (0, 0)) for w in acc_widths]
    res = pl.pallas_call(
        body, name=name, grid=(s_dim // tr,), in_specs=row_specs + const_specs, out_specs=out_specs, out_shape=out_shape,
        compiler_params=pltpu.CompilerParams(dimension_semantics=("arbitrary",)),
    )(*row_arrays, *consts)
    return res


def _colsum(v):
    return jnp.sum(v, axis=0, keepdims=True)


def _rms(u, g):
    return u * lax.rsqrt(jnp.mean(u * u, axis=-1, keepdims=True) + EPS) * g


def _ln(u, g, b):
    mu = jnp.mean(u, axis=-1, keepdims=True)
    d = u - mu
    var = jnp.mean(d * d, axis=-1, keepdims=True)
    return d * lax.rsqrt(var + LN_EPS) * g + b


def _sigmoid(v):
    return 1.0 / (1.0 + jnp.exp(-v))


def _silu(v):
    return v * _sigmoid(v)


def _softplus(v):
    y = jnp.exp(-jnp.abs(v))
    w = 1.0 + y
    log1p = jnp.where(w == 1.0, y, jnp.log(w) * y / jnp.where(w == 1.0, 1.0, w - 1.0))
    return jnp.maximum(v, 0.0) + log1p


def _gate_rms(y, z, w):
    return _rms(y * _silu(z), w)


def _vjp_rows(f):
    def fn(*args):
        prim, ct = args[:-1], args[-1]
        _, pull = jax.vjp(f, *prim)
        return pull(ct)
    return fn


def _conv_pre(cur, prev, w, b, first):
    row = lax.broadcasted_iota(jnp.int32, cur.shape, 0)
    acc = cur * w[3:4, :] + b
    for j in (1, 2, 3):
        tail = jnp.where(first, 0.0, pltpu.roll(prev, j, 0))
        acc = acc + jnp.where(row >= j, pltpu.roll(cur, j, 0), tail) * w[3 - j:4 - j, :]
    return acc


def _conv_fwd(u, ucb, w, b, name="conv_fwd"):
    s_dim, width = u.shape[0], w.shape[1]
    tr = min(ROW_TILE, s_dim)

    def body(cur_ref, prev_ref, w_ref, b_ref, o_ref):
        pre = _conv_pre(cur_ref[...], prev_ref[...], w_ref, b_ref[...], pl.program_id(0) == 0)
        o_ref[...] = _silu(pre)

    return pl.pallas_call(
        body, name=name, grid=(s_dim // tr,),
        in_specs=[pl.BlockSpec((tr, width), lambda i: (i, ucb)),
                  pl.BlockSpec((tr, width), lambda i: (jnp.maximum(i - 1, 0), ucb)),
                  pl.BlockSpec(w.shape, lambda i: (0, 0)), pl.BlockSpec(b.shape, lambda i: (0, 0))],
        out_specs=pl.BlockSpec((tr, width), lambda i: (i, 0)), out_shape=jax.ShapeDtypeStruct((s_dim, width), F32),
        compiler_params=pltpu.CompilerParams(dimension_semantics=("arbitrary",)),
    )(u, u, w, b)


def _conv_bwd_pre(u, ucb, w, b, dact, name="conv_bwd_pre"):
    s_dim, width = u.shape[0], w.shape[1]
    tr = min(ROW_TILE, s_dim)

    def body(cur_ref, prev_ref, w_ref, b_ref, d_ref, da_ref, dw_ref, db_ref):
        first = pl.program_id(0) == 0
        cur, prev = cur_ref[...], prev_ref[...]
        pre = _conv_pre(cur, prev, w_ref, b_ref[...], first)
        sg = _sigmoid(pre)
        da = d_ref[...] * (sg * (1.0 + pre * (1.0 - sg)))
        da_ref[...] = da
        row = lax.broadcasted_iota(jnp.int32, cur.shape, 0)

        @pl.when(first)
        def _():
            dw_ref[...] = jnp.zeros_like(dw_ref)
            db_ref[...] = jnp.zeros_like(db_ref)

        db_ref[...] += _colsum(da)
        dw_ref[3:4, :] += _colsum(da * cur)
        for j in (1, 2, 3):
            tail = jnp.where(first, 0.0, pltpu.roll(prev, j, 0))
            sh = jnp.where(row >= j, pltpu.roll(cur, j, 0), tail)
            dw_ref[3 - j:4 - j, :] += _colsum(da * sh)

    return pl.pallas_call(
        body, name=name, grid=(s_dim // tr,),
        in_specs=[pl.BlockSpec((tr, width), lambda i: (i, ucb)),
                  pl.BlockSpec((tr, width), lambda i: (jnp.maximum(i - 1, 0), ucb)),
                  pl.BlockSpec(w.shape, lambda i: (0, 0)), pl.BlockSpec(b.shape, lambda i: (0, 0)),
                  pl.BlockSpec((tr, width), lambda i: (i, 0))],
        out_specs=[pl.BlockSpec((tr, width), lambda i: (i, 0)), pl.BlockSpec(w.shape, lambda i: (0, 0)),
                   pl.BlockSpec(b.shape, lambda i: (0, 0))],
        out_shape=[jax.ShapeDtypeStruct((s_dim, width), F32), jax.ShapeDtypeStruct(w.shape, F32),
                   jax.ShapeDtypeStruct(b.shape, F32)],
        compiler_params=pltpu.CompilerParams(dimension_semantics=("arbitrary",)),
    )(u, u, w, b, dact)


def _conv_bwd_in(da, w, name="conv_bwd_in"):
    s_dim, width = da.shape
    tr = min(ROW_TILE, s_dim)
    n = s_dim // tr

    def body(cur_ref, nxt_ref, w_ref, o_ref):
        last = pl.program_id(0) == n - 1
        cur, nxt = cur_ref[...], nxt_ref[...]
        row = lax.broadcasted_iota(jnp.int32, cur.shape, 0)
        acc = cur * w_ref[3:4, :]
        for j in (1, 2, 3):
            head = jnp.where(last, 0.0, pltpu.roll(nxt, tr - j, 0))
            acc = acc + jnp.where(row < tr - j, pltpu.roll(cur, tr - j, 0), head) * w_ref[3 - j:4 - j, :]
        o_ref[...] = acc.astype(o_ref.dtype)

    return pl.pallas_call(
        body, name=name, grid=(n,),
        in_specs=[pl.BlockSpec((tr, width), lambda i: (i, 0)), pl.BlockSpec((tr, width), lambda i: (jnp.minimum(i + 1, n - 1), 0)),
                  pl.BlockSpec(w.shape, lambda i: (0, 0))],
        out_specs=pl.BlockSpec((tr, width), lambda i: (i, 0)), out_shape=jax.ShapeDtypeStruct((s_dim, width), BF16),
        compiler_params=pltpu.CompilerParams(dimension_semantics=("arbitrary",)),
    )(da, da, w)


def _sel_dot(a, sel, pieces, dims=(((1,), (0,)), ((), ())), sel_left=False):
    sel = sel.astype(BF16)
    acc, rest = None, a
    for _ in range(pieces):
        piece = rest.astype(BF16)
        rest = rest - piece.astype(F32)
        part = _dot(sel, piece, dims) if sel_left else _dot(piece, sel, dims)
        acc = part if acc is None else acc + part
    return acc


def _ssd_consts():
    L = SSD_CHUNK
    tri = np.tril(np.ones((L, L), np.float32))
    expand = np.zeros((LANE, SSD_INNER), np.float32)
    expand128 = np.zeros((LANE, SSD_HEADS * LANE), np.float32)
    for h in range(SSD_HEADS):
        expand[h, h * SSD_HEAD_DIM:(h + 1) * SSD_HEAD_DIM] = 1.0
        expand128[h, h * LANE:(h + 1) * LANE] = 1.0
    return jnp.asarray(tri), jnp.asarray(expand), jnp.asarray(expand128), jnp.asarray(expand.T.copy())


def _ssd_prep(dt_ref, bias_ref, alog_ref, tri_ref, exp_ref, exp128_ref, cs_s, cst_s, ex_s, csx_s):
    L = SSD_CHUNK
    dt = _softplus(dt_ref[...] + bias_ref[...])
    a = -jnp.exp(alog_ref[...])
    cs = _sel_dot(dt * a, tri_ref[...], 3, sel_left=True)
    cs_s[...] = cs
    cst_s[...] = cs.T
    last = cs_s[L - 1:L, :]
    expand = exp_ref[...]
    ex_s[...] = _sel_dot(jnp.exp(cs), expand, 2)
    f_x = _sel_dot(jnp.exp(last - cs), expand, 2)
    dt_x = _sel_dot(dt, expand, 2)
    csx_s[...] = _sel_dot(cs, exp128_ref[...], 3)
    t_x = ex_s[L - 1:L, :]
    return dt, a, dt_x, f_x, t_x


def _decay_matrix(csx_s, cst_s, h, tril):
    seg = csx_s[:, h * LANE:(h + 1) * LANE] - cst_s[h:h + 1, :]
    return jnp.exp(jnp.where(tril, seg, -jnp.inf))


def _ssd_fwd(xbca, dtr, dtcb, bias, alog, d_x, name="ssd_fwd"):
    s_dim = xbca.shape[0]
    L = SSD_CHUNK
    nc = s_dim // L
    tri, expand, expand128, _ = _ssd_consts()

    def body(xs_ref, b_ref, c_ref, dt_ref, bias_ref, alog_ref, dx_ref, tri_ref, exp_ref, exp128_ref,
             y_ref, st_ref, st_s, cs_s, cst_s, ex_s, csx_s):
        @pl.when(pl.program_id(0) == 0)
        def _():
            st_s[...] = jnp.zeros_like(st_s)

        dt, a, dt_x, f_x, t_x = _ssd_prep(dt_ref, bias_ref, alog_ref, tri_ref, exp_ref, exp128_ref, cs_s, cst_s, ex_s, csx_s)
        st_ref[0] = st_s[...]
        row = lax.broadcasted_iota(jnp.int32, (L, L), 0)
        col = lax.broadcasted_iota(jnp.int32, (L, L), 1)
        tril = row >= col
        low = col < SSD_HEAD_DIM
        for g in range(2):
            bg = b_ref[:, g * LANE:(g + 1) * LANE]
            cg = c_ref[:, g * LANE:(g + 1) * LANE].astype(BF16)
            gmat = _dot(cg, bg.astype(BF16), _NT)
            bgt = bg.T.astype(BF16)
            for jj in range(4):
                j = 4 * g + jj
                sl = slice(j * LANE, (j + 1) * LANE)
                xp = xs_ref[:, sl]
                x_dt = xp * dt_x[:, sl]
                xb = x_dt.astype(BF16)
                yd = []
                for e in range(2):
                    lm = _decay_matrix(csx_s, cst_s, 2 * j + e, tril)
                    yd.append(_dot((gmat * lm).astype(BF16), xb))
                stp = st_s[j]
                z = _dot(cg, stp.astype(BF16))
                y_ref[:, sl] = jnp.where(low, yd[0], yd[1]) + ex_s[:, sl] * z + dx_ref[:, sl] * xp
                xf = (x_dt * f_x[:, sl]).astype(BF16)
                st_s[j] = t_x[:, sl] * stp + _dot(bgt, xf)

    const = lambda shape: pl.BlockSpec(shape, lambda c: tuple(0 for _ in shape))
    return pl.pallas_call(
        body, name=name, grid=(nc,),
        in_specs=[pl.BlockSpec((L, 1024), lambda c: (c, 0)), pl.BlockSpec((L, 256), lambda c: (c, 4)),
                  pl.BlockSpec((L, 256), lambda c: (c, 5)), pl.BlockSpec((L, LANE), lambda c: (c, dtcb)),
                  const((1, LANE)), const((1, LANE)), const((1, 1024)), const((L, L)), const((LANE, 1024)),
                  const((LANE, 2048))],
        out_specs=[pl.BlockSpec((L, 1024), lambda c: (c, 0)), pl.BlockSpec((1, 8, LANE, LANE), lambda c: (c, 0, 0, 0))],
        out_shape=[jax.ShapeDtypeStruct((s_dim, 1024), F32), jax.ShapeDtypeStruct((nc, 8, LANE, LANE), F32)],
        scratch_shapes=[pltpu.VMEM((8, LANE, LANE), F32), pltpu.VMEM((L, LANE), F32), pltpu.VMEM((LANE, L), F32),
                        pltpu.VMEM((L, 1024), F32), pltpu.VMEM((L, 2048), F32)],
        compiler_params=pltpu.CompilerParams(dimension_semantics=("arbitrary",)),
    )(xbca, xbca, xbca, dtr, bias, alog, d_x, tri, expand, expand128)


def _ssd_bwd(xbca, dtr, dtcb, bias, alog, d_x, states, dy, name="ssd_bwd"):
    s_dim = xbca.shape[0]
    L = SSD_CHUNK
    nc = s_dim // L
    tri, expand, expand128, expand_t = _ssd_consts()

    def body(xs_ref, b_ref, c_ref, dt_ref, bias_ref, alog_ref, dx_ref, tri_ref, exp_ref, exp128_ref, expt_ref,
             st_ref, dy_ref, dxbc_ref, ddt_ref, dbias_ref, dalog_ref, dd_ref,
             dst_s, cs_s, cst_s, ex_s, csx_s, dcsx_s, ddtx_s, dcol_s, drow_s, dlast_s, dd_s):
        @pl.when(pl.program_id(0) == 0)
        def _():
            dst_s[...] = jnp.zeros_like(dst_s)
            dbias_ref[...] = jnp.zeros_like(dbias_ref)
            dalog_ref[...] = jnp.zeros_like(dalog_ref)
            dd_s[...] = jnp.zeros_like(dd_s)

        dt, a, dt_x, f_x, t_x = _ssd_prep(dt_ref, bias_ref, alog_ref, tri_ref, exp_ref, exp128_ref, cs_s, cst_s, ex_s, csx_s)
        row = lax.broadcasted_iota(jnp.int32, (L, L), 0)
        col = lax.broadcasted_iota(jnp.int32, (L, L), 1)
        tril = row >= col
        low = col < SSD_HEAD_DIM
        dcol_s[...] = jnp.zeros_like(dcol_s)
        drow_s[...] = jnp.zeros_like(drow_s)
        for g in range(2):
            bg = b_ref[:, g * LANE:(g + 1) * LANE]
            cg = c_ref[:, g * LANE:(g + 1) * LANE]
            bgb, cgb = bg.astype(BF16), cg.astype(BF16)
            gmat = _dot(cgb, bgb, _NT)
            d_g = jnp.zeros((L, L), F32)
            d_b = jnp.zeros((L, LANE), F32)
            d_c = jnp.zeros((L, LANE), F32)
            for jj in range(4):
                j = 4 * g + jj
                sl = slice(j * LANE, (j + 1) * LANE)
                xp = xs_ref[:, sl]
                dtp = dt_x[:, sl]
                x_dt = xp * dtp
                xb = x_dt.astype(BF16)
                dyp = dy_ref[:, sl]
                dd_s[:, sl] += _colsum(dyp * xp)
                d_xdt = jnp.zeros((L, LANE), F32)
                for e in range(2):
                    h = 2 * j + e
                    lm = _decay_matrix(csx_s, cst_s, h, tril)
                    m = gmat * lm
                    dye = jnp.where(low if e == 0 else jnp.logical_not(low), dyp, 0.0).astype(BF16)
                    d_m = jnp.where(tril, _dot(dye, xb, _NT), 0.0)
                    d_xdt = d_xdt + _dot(m.astype(BF16), dye, _TN)
                    d_g = d_g + d_m * lm
                    w = d_m * m
                    dcol_s[...] += jnp.where(col == h, jnp.sum(w, axis=1, keepdims=True), 0.0)
                    drow_s[...] += jnp.where(row == h, jnp.sum(w, axis=0, keepdims=True), 0.0)
                stp = st_ref[0, j]
                stb = stp.astype(BF16)
                dstn = dst_s[j]
                dstb = dstn.astype(BF16)
                e_p = ex_s[:, sl]
                f_p = f_x[:, sl]
                t_p = t_x[:, sl]
                z = _dot(cgb, stb)
                d_z = (e_p * dyp).astype(BF16)
                d_c = d_c + _dot(d_z, stb, _NT)
                d_xf = _dot(bgb, dstb)
                d_b = d_b + _dot((x_dt * f_p).astype(BF16), dstb, _NT)
                d_xdt = d_xdt + f_p * d_xf
                d_f = x_dt * d_xf * f_p
                dcsx_s[:, sl] = dyp * e_p * z - d_f
                dlast_s[:, sl] = _colsum(d_f) + _colsum(dstn * stp) * t_p
                dst_s[j] = _dot(cgb, d_z, _TN) + t_p * dstn
                dxbc_ref[:, sl] = dx_ref[:, sl] * dyp + d_xdt * dtp
                ddtx_s[:, sl] = d_xdt * xp
            d_gb = d_g.astype(BF16)
            dxbc_ref[:, 1024 + g * LANE:1024 + (g + 1) * LANE] = d_b + _dot(d_gb, cgb, _TN)
            dxbc_ref[:, 1280 + g * LANE:1280 + (g + 1) * LANE] = d_c + _dot(d_gb, bgb)

        expt = expt_ref[...]
        dlast = _sel_dot(jnp.broadcast_to(dlast_s[...], (8, 1024)), expt, 3)
        d_cs = dcol_s[...] - drow_s[...].T + _sel_dot(dcsx_s[...], expt, 3)
        rown = lax.broadcasted_iota(jnp.int32, (L, LANE), 0)
        d_cs = d_cs + jnp.where(rown == L - 1, jnp.sum(dlast, axis=0, keepdims=True) * 0.125, 0.0)
        d_da = _sel_dot(d_cs, tri_ref[...], 3, _TN, sel_left=True)
        d_dt = d_da * a + _sel_dot(ddtx_s[...], expt, 3)
        dalog_ref[...] += _colsum(d_da * dt) * a
        d_raw = d_dt * _sigmoid(dt_ref[...] + bias_ref[...])
        ddt_ref[...] = d_raw.astype(ddt_ref.dtype)
        dbias_ref[...] += _colsum(d_raw)
        dd8 = _sel_dot(jnp.broadcast_to(dd_s[...], (8, 1024)), expt, 3)
        dd_ref[...] = jnp.sum(dd8, axis=0, keepdims=True) * 0.125

    const = lambda shape: pl.BlockSpec(shape, lambda c: tuple(0 for _ in shape))
    rev = lambda cb: (lambda c: (nc - 1 - c, cb))
    return pl.pallas_call(
        body, name=name, grid=(nc,),
        in_specs=[pl.BlockSpec((L, 1024), rev(0)), pl.BlockSpec((L, 256), rev(4)), pl.BlockSpec((L, 256), rev(5)),
                  pl.BlockSpec((L, LANE), rev(dtcb)), const((1, LANE)), const((1, LANE)), const((1, 1024)), const((L, L)),
                  const((LANE, 1024)), const((LANE, 2048)), const((1024, LANE)),
                  pl.BlockSpec((1, 8, LANE, LANE), lambda c: (nc - 1 - c, 0, 0, 0)), pl.BlockSpec((L, 1024), rev(0))],
        out_specs=[pl.BlockSpec((L, SSD_XBC), rev(0)), pl.BlockSpec((L, LANE), rev(0)), const((1, LANE)), const((1, LANE)),
                   const((1, LANE))],
        out_shape=[jax.ShapeDtypeStruct((s_dim, SSD_XBC), F32), jax.ShapeDtypeStruct((s_dim, LANE), BF16),
                   jax.ShapeDtypeStruct((1, LANE), F32), jax.ShapeDtypeStruct((1, LANE), F32),
                   jax.ShapeDtypeStruct((1, LANE), F32)],
        scratch_shapes=[pltpu.VMEM((8, LANE, LANE), F32), pltpu.VMEM((L, LANE), F32), pltpu.VMEM((LANE, L), F32),
                        pltpu.VMEM((L, 1024), F32), pltpu.VMEM((L, 2048), F32), pltpu.VMEM((L, 1024), F32),
                        pltpu.VMEM((L, 1024), F32), pltpu.VMEM((L, LANE), F32), pltpu.VMEM((LANE, L), F32),
                        pltpu.VMEM((1, 1024), F32), pltpu.VMEM((1, 1024), F32)],
        compiler_params=pltpu.CompilerParams(dimension_semantics=("arbitrary",)),
    )(xbca, xbca, xbca, dtr, bias, alog, d_x, tri, expand, expand128, expand_t, states, dy)


def _swap_halves(u):
    width = u.shape[1]
    lane = lax.broadcasted_iota(jnp.int32, u.shape, 1)
    return jnp.where(lane % MLA_ROPE < MLA_ROPE // 2, pltpu.roll(u, width - MLA_ROPE // 2, 1), pltpu.roll(u, MLA_ROPE // 2, 1))


def _rope_fwd_fn(u, cos, sin):
    return u * cos + _swap_halves(u) * sin


def _rope_bwd_fn(d, cos, sin):
    return d * cos + _swap_halves(d * sin)


def _spread4(v):
    return v + pltpu.roll(v, 32, 1) + pltpu.roll(v, 64, 1) + pltpu.roll(v, 96, 1)


def _att_masks(tq):
    lane = lax.broadcasted_iota(jnp.int32, (tq, LANE), 1)
    return lane // MLA_NOPE, lane // MLA_ROPE


def _att_tile(i, tq):
    klen = (i + 1) * tq
    qpos = i * tq + lax.broadcasted_iota(jnp.int32, (tq, klen), 0)
    kpos = lax.broadcasted_iota(jnp.int32, (tq, klen), 1)
    return slice(i * tq, (i + 1) * tq), klen, qpos >= kpos


def _att_qcat(qn_t, qr_t, par, e, half_id, grp_id):
    return jnp.concatenate([jnp.where(half_id == par, qn_t * ATT_SCALE, 0.0), jnp.where(grp_id == e, qr_t * ATT_SCALE, 0.0)],
                           axis=1).astype(BF16)


def _att_exp(qcat, kcat, causal):
    s = jnp.where(causal, _dot(qcat, kcat, _NT), -jnp.inf)
    e = jnp.exp(s - jnp.max(s, axis=1, keepdims=True))
    return e, 1.0 / jnp.sum(e, axis=1, keepdims=True)


def _att_specs(s_dim):
    col = lambda f: pl.BlockSpec((s_dim, LANE), lambda j: (0, f(j)))
    return [col(lambda j: j), col(lambda j: j // 2), col(lambda j: j), col(lambda j: 0), col(lambda j: 8 + j)]


def _att_fwd(q, qr, kv, krt, name="att_fwd"):
    s_dim = q.shape[0]
    tq = min(ATT_TQ, s_dim)

    def body(qn_ref, qr_ref, kn_ref, krt_ref, v_ref, o_ref, kcat_s, vb_s):
        e0 = 2 * (pl.program_id(0) % 2)
        half_id, grp_id = _att_masks(tq)
        kcat_s[...] = jnp.concatenate([kn_ref[...], krt_ref[...]], axis=1).astype(BF16)
        vb_s[...] = v_ref[...].astype(BF16)
        for i in range(s_dim // tq):
            rows, klen, causal = _att_tile(i, tq)
            qn_t, qr_t = qn_ref[rows, :], qr_ref[rows, :]
            outs = []
            for par in range(2):
                qcat = _att_qcat(qn_t, qr_t, par, e0 + par, half_id, grp_id)
                e, inv_l = _att_exp(qcat, kcat_s[0:klen, :], causal)
                outs.append(_dot(e.astype(BF16), vb_s[0:klen, :]) * inv_l)
            o_ref[rows, :] = jnp.where(half_id == 0, outs[0], outs[1])

    return pl.pallas_call(
        body, name=name, grid=(MLA_HEADS // 2,), in_specs=_att_specs(s_dim),
        out_specs=pl.BlockSpec((s_dim, LANE), lambda j: (0, j)), out_shape=jax.ShapeDtypeStruct((s_dim, 1024), F32),
        scratch_shapes=[pltpu.VMEM((s_dim, 2 * LANE), BF16), pltpu.VMEM((s_dim, LANE), BF16)],
        compiler_params=pltpu.CompilerParams(dimension_semantics=("parallel",)),
    )(q, qr, kv, krt, kv)


def _att_bwd(q, qr, kv, krt, o, do, name="att_bwd"):
    s_dim = q.shape[0]
    tq = min(ATT_TQ, s_dim)

    def body(qn_ref, qr_ref, kn_ref, krt_ref, v_ref, o_ref, do_ref, dqn_ref, dqr_ref, dkn_ref, dv_ref, dkrt_ref,
             kcat_s, vb_s):
        e0 = 2 * (pl.program_id(0) % 2)
        half_id, grp_id = _att_masks(tq)
        kcat_s[...] = jnp.concatenate([kn_ref[...], krt_ref[...]], axis=1).astype(BF16)
        vb_s[...] = v_ref[...].astype(BF16)
        dkn_ref[...] = jnp.zeros_like(dkn_ref)
        dv_ref[...] = jnp.zeros_like(dv_ref)
        dkrt_ref[...] = jnp.zeros_like(dkrt_ref)
        for i in range(s_dim // tq):
            rows, klen, causal = _att_tile(i, tq)
            qn_t, qr_t, o_t, do_t = qn_ref[rows, :], qr_ref[rows, :], o_ref[rows, :], do_ref[rows, :]
            dqn = jnp.zeros((tq, LANE), F32)
            dqr = jnp.zeros((tq, LANE), F32)
            for par in range(2):
                qcat = _att_qcat(qn_t, qr_t, par, e0 + par, half_id, grp_id)
                e, inv_l = _att_exp(qcat, kcat_s[0:klen, :], causal)
                p = e * inv_l
                dom = jnp.where(half_id == par, do_t, 0.0)
                domb = dom.astype(BF16)
                d_p = _dot(domb, vb_s[0:klen, :], _NT)
                d_row = jnp.sum(dom * o_t, axis=1, keepdims=True)
                d_s = (p * (d_p - d_row)).astype(BF16)
                dqcat = _dot(d_s, kcat_s[0:klen, :]) * ATT_SCALE
                dqn = dqn + jnp.where(half_id == par, dqcat[:, :LANE], 0.0)
                dqr = dqr + jnp.where(grp_id == e0 + par, dqcat[:, LANE:], 0.0)
                dkcat = _dot(d_s, qcat, _TN)
                dkn_ref[0:klen, :] += dkcat[:, :LANE]
                dkrt_ref[0:klen, :] += dkcat[:, LANE:]
                dv_ref[0:klen, :] += _dot(p.astype(BF16), domb, _TN)
            dqn_ref[rows, :] = dqn.astype(dqn_ref.dtype)
            dqr_ref[rows, :] = dqr

    col = lambda f: pl.BlockSpec((s_dim, LANE), lambda j: (0, f(j)))
    return pl.pallas_call(
        body, name=name, grid=(MLA_HEADS // 2,), in_specs=_att_specs(s_dim) + [col(lambda j: j), col(lambda j: j)],
        out_specs=[col(lambda j: j), pl.BlockSpec((None, s_dim, LANE), lambda j: (j % 2, 0, j // 2)), col(lambda j: j),
                   col(lambda j: j), pl.BlockSpec((None, s_dim, LANE), lambda j: (j, 0, 0))],
        out_shape=[jax.ShapeDtypeStruct((s_dim, 1024), BF16), jax.ShapeDtypeStruct((2, s_dim, 512), F32),
                   jax.ShapeDtypeStruct((s_dim, 1024), F32), jax.ShapeDtypeStruct((s_dim, 1024), F32),
                   jax.ShapeDtypeStruct((MLA_HEADS // 2, s_dim, LANE), F32)],
        scratch_shapes=[pltpu.VMEM((s_dim, 2 * LANE), BF16), pltpu.VMEM((s_dim, LANE), BF16)],
        compiler_params=pltpu.CompilerParams(dimension_semantics=("parallel",)),
    )(q, qr, kv, krt, kv, o, do)


def _all_gather(x, name):
    rows, width = x.shape

    def body(x_ref, out_ref, send_sems, recv_sems, local_sem):
        x_i, y_i, c_i = lax.axis_index("x"), lax.axis_index("y"), lax.axis_index("c")
        me, sibling = (x_i, y_i, c_i), (x_i, y_i, 1 - c_i)
        chips = [(1 - x_i, y_i), (x_i, 1 - y_i), (1 - x_i, 1 - y_i)]

        def slot(px, py, pc):
            return out_ref.at[4 * px + 2 * py + pc]

        def copy(k, block, to, src=None):
            return pltpu.make_async_remote_copy(
                src_ref=slot(*block) if src is None else src, dst_ref=slot(*block), send_sem=send_sems.at[k],
                recv_sem=recv_sems.at[k], device_id=to, device_id_type=pl.DeviceIdType.MESH)

        mine = pltpu.make_async_copy(x_ref, slot(*me), local_sem)
        mine.start()
        first = [copy(0, me, sibling, src=x_ref)]
        first += [copy(1 + j, me, (*chip, c_i), src=x_ref) for j, chip in enumerate(chips)]
        for cp in first:
            cp.start()
        passed = [copy(4 + j, (*chip, c_i), sibling) for j, chip in enumerate(chips)]
        for j, chip in enumerate(chips):
            copy(1 + j, (*chip, c_i), me).wait_recv()
            passed[j].start()
        copy(0, sibling, me).wait_recv()
        for j, chip in enumerate(chips):
            copy(4 + j, (*chip, 1 - c_i), me).wait_recv()
        for cp in first + passed:
            cp.wait_send()
        mine.wait()

    return pl.pallas_call(
        body, name=name, out_shape=jax.ShapeDtypeStruct((N_DEV, rows, width), x.dtype),
        in_specs=[pl.BlockSpec(memory_space=pl.ANY)], out_specs=pl.BlockSpec(memory_space=pl.ANY),
        scratch_shapes=[pltpu.SemaphoreType.DMA((7,)), pltpu.SemaphoreType.DMA((7,)), pltpu.SemaphoreType.DMA],
    )(x)


def _gather_many(shards, name):
    n_arr = len(shards)

    def body(*refs):
        x_refs, out_refs = refs[:n_arr], refs[n_arr:2 * n_arr]
        send_sems, recv_sems, local_sems = refs[2 * n_arr:]
        x_i, y_i, c_i = lax.axis_index("x"), lax.axis_index("y"), lax.axis_index("c")
        me, sibling = (x_i, y_i, c_i), (x_i, y_i, 1 - c_i)
        chips = [(1 - x_i, y_i), (x_i, 1 - y_i), (1 - x_i, 1 - y_i)]

        def copy(a, k, block, to, src=None):
            slot = out_refs[a].at[4 * block[0] + 2 * block[1] + block[2]]
            return pltpu.make_async_remote_copy(
                src_ref=slot if src is None else src, dst_ref=slot, send_sem=send_sems.at[a, k],
                recv_sem=recv_sems.at[a, k], device_id=to, device_id_type=pl.DeviceIdType.MESH)

        mine, first, passed = [], [], []
        for a in range(n_arr):
            mine.append(pltpu.make_async_copy(x_refs[a], out_refs[a].at[4 * x_i + 2 * y_i + c_i], local_sems.at[a]))
            mine[a].start()
            first.append([copy(a, 0, me, sibling, src=x_refs[a])]
                         + [copy(a, 1 + j, me, (*chip, c_i), src=x_refs[a]) for j, chip in enumerate(chips)])
            for cp in first[a]:
                cp.start()
            passed.append([copy(a, 4 + j, (*chip, c_i), sibling) for j, chip in enumerate(chips)])
        for j, chip in enumerate(chips):
            for a in range(n_arr):
                copy(a, 1 + j, (*chip, c_i), me).wait_recv()
                passed[a][j].start()
        for a in range(n_arr):
            copy(a, 0, sibling, me).wait_recv()
            for j, chip in enumerate(chips):
                copy(a, 4 + j, (*chip, 1 - c_i), me).wait_recv()
        for a in range(n_arr):
            for cp in first[a] + passed[a]:
                cp.wait_send()
            mine[a].wait()

    any_spec = pl.BlockSpec(memory_space=pl.ANY)
    return pl.pallas_call(
        body, name=name, out_shape=[jax.ShapeDtypeStruct((N_DEV,) + x.shape, x.dtype) for x in shards],
        in_specs=[any_spec] * n_arr, out_specs=[any_spec] * n_arr,
        scratch_shapes=[pltpu.SemaphoreType.DMA((n_arr, 7)), pltpu.SemaphoreType.DMA((n_arr, 7)),
                        pltpu.SemaphoreType.DMA((n_arr,))],
    )(*shards)


_HBM = pl.BlockSpec(memory_space=pltpu.HBM)
_SEM = pl.BlockSpec(memory_space=pltpu.SEMAPHORE)


def _plan_copies(plan, src_refs, land_refs, send_sems, recv_sems):
    copies = []
    for s_ref, l_ref in zip(src_refs, land_refs):
        for src, dst, peer in plan(s_ref, l_ref):
            k = len(copies)
            copies.append(pltpu.make_async_remote_copy(
                src_ref=src, dst_ref=dst, send_sem=send_sems.at[k], recv_sem=recv_sems.at[k], device_id=peer,
                device_id_type=pl.DeviceIdType.MESH))
    return copies


def _split_start(srcs, lands, plan, n_copy, name, after=None):
    n = len(srcs)
    n_in = 2 * n + (after is not None)

    def body(*refs):
        for cp in _plan_copies(plan, refs[:n], refs[n:2 * n], refs[n_in], refs[n_in + 1]):
            cp.start()
        refs[-1][...] = jnp.zeros_like(refs[-1])

    sems = pltpu.SemaphoreType.DMA((n * n_copy,))
    res = pl.pallas_call(
        body, name=name,
        out_shape=(sems, sems, *[pltpu.HBM(a.shape, a.dtype) for a in list(srcs) + list(lands)],
                   jax.ShapeDtypeStruct((8, LANE), F32)),
        in_specs=[_HBM] * (2 * n) + [pl.BlockSpec(memory_space=pl.ANY)] * (after is not None),
        out_specs=(_SEM, _SEM, *[_HBM] * (2 * n), pl.BlockSpec(memory_space=pltpu.VMEM)),
        input_output_aliases={i: 2 + i for i in range(2 * n)},
        compiler_params=pltpu.CompilerParams(has_side_effects=pltpu.SideEffectType.DATAFLOW_SIDE_EFFECTING),
    )(*[pltpu.with_memory_space_constraint(a, pltpu.HBM) for a in list(srcs) + list(lands)],
      *([after] if after is not None else []))
    return res[0], res[1], list(res[2:2 + n]), list(res[2 + n:2 + 2 * n]), res[-1]


def _split_wait(send_sems, recv_sems, srcs, lands, after, plan, name):
    n = len(srcs)

    def body(*refs):
        copies = _plan_copies(plan, refs[:n], refs[n:2 * n], refs[2 * n], refs[2 * n + 1])
        for cp in copies:
            cp.wait_send()
        for cp in copies:
            cp.wait_recv()

    res = pl.pallas_call(
        body, name=name, out_shape=tuple(pltpu.HBM(a.shape, a.dtype) for a in list(srcs) + list(lands)),
        in_specs=[_HBM] * (2 * n) + [_SEM, _SEM, pl.BlockSpec(memory_space=pl.ANY)], out_specs=tuple([_HBM] * (2 * n)),
        input_output_aliases={i: i for i in range(2 * n)},
        compiler_params=pltpu.CompilerParams(has_side_effects=pltpu.SideEffectType.DATAFLOW_SIDE_EFFECTING),
    )(*srcs, *lands, send_sems, recv_sems, after)
    return list(res[:n]), list(res[n:])


def _plan_broadcast(src, land):
    x_i, y_i, c_i = lax.axis_index("x"), lax.axis_index("y"), lax.axis_index("c")
    me = 4 * x_i + 2 * y_i + c_i
    return [(src, land.at[me], (x_i ^ (k >> 2), y_i ^ ((k >> 1) & 1), c_i ^ (k & 1))) for k in range(1, N_DEV)]


def _plan_scatter(src, land):
    x_i, y_i, c_i = lax.axis_index("x"), lax.axis_index("y"), lax.axis_index("c")
    me = 4 * x_i + 2 * y_i + c_i
    plan = []
    for k in range(1, N_DEV):
        px, py, pc = x_i ^ (k >> 2), y_i ^ ((k >> 1) & 1), c_i ^ (k & 1)
        plan.append((src.at[4 * px + 2 * py + pc], land.at[me], (px, py, pc)))
    return plan


def _adam_math(g, w, m, v):
    m_new = ADAM_B1 * m + (1.0 - ADAM_B1) * g
    v_new = ADAM_B2 * v + (1.0 - ADAM_B2) * (g * g)
    m_hat = m_new / (1.0 - ADAM_B1 ** ADAM_STEP)
    v_hat = v_new / (1.0 - ADAM_B2 ** ADAM_STEP)
    return -ADAM_LR * (m_hat / (jnp.sqrt(v_hat) + ADAM_EPS) + ADAM_WD * w), m_new, v_new


def _adam(slots, w, m, v, name, own=None, own_idx=None):
    n_slot, rows, cols = slots.shape
    tr = ROW_TILE if rows % ROW_TILE == 0 else rows
    has_own = own is not None

    def body(*refs):
        if has_own:
            idx_ref, own_ref, refs = refs[0], refs[1], refs[2:]
        s_ref, w_ref, m_ref, v_ref, g_ref, d_ref, mo_ref, vo_ref = refs
        g = own_ref[...].astype(F32) if has_own else s_ref[0].astype(F32)
        for k in range(0 if has_own else 1, n_slot):
            part = s_ref[k].astype(F32)
            g = g + (jnp.where(idx_ref[0] == k, 0.0, part) if has_own else part)
        g_ref[...] = g
        d_ref[...], mo_ref[...], vo_ref[...] = _adam_math(g, w_ref[...], m_ref[...], v_ref[...])

    spec = pl.BlockSpec((tr, cols), lambda i, *_: (i, 0))
    in_specs = [pl.BlockSpec((n_slot, tr, cols), lambda i, *_: (0, i, 0)), spec, spec, spec]
    if has_own:
        in_specs = [pl.BlockSpec((None, tr, cols), lambda i, idx: (idx[0], i, 0))] + in_specs
    grid_spec = pltpu.PrefetchScalarGridSpec(num_scalar_prefetch=1 if has_own else 0, grid=(rows // tr,), in_specs=in_specs,
                                             out_specs=[spec] * 4)
    ins = ([own_idx, own] if has_own else []) + [slots, w, m, v]
    return pl.pallas_call(
        body, name=name, grid_spec=grid_spec, out_shape=[jax.ShapeDtypeStruct((rows, cols), F32)] * 4,
        compiler_params=pltpu.CompilerParams(dimension_semantics=("parallel",)),
    )(*ins)


PACK_ROWS, PACK_W = 24, 1536
REPL_W = (("ssd_conv_b", 1536), ("ssd_dt_bias", 16), ("ssd_A_log", 16), ("ssd_D", 16), ("ssd_norm_w", 1024),
          ("mla_q_norm_w", 384), ("mla_kv_norm_w", 256), ("mla_out_norm_w", 1024), ("ln_mix_g", 1024),
          ("ln_mix_b", 1024), ("ln_ffn_g", 1024), ("ln_ffn_b", 1024))
LOSS_ROW = 4 + len(REPL_W)


def _pack_small(conv_w_grad, grads, loss, name="pack_small"):
    def body(*refs):
        cw_ref, g_refs, loss_ref, o_ref = refs[0], refs[1:1 + len(REPL_W)], refs[1 + len(REPL_W)], refs[-1]
        o_ref[...] = jnp.zeros_like(o_ref)
        o_ref[0:4, :] = cw_ref[...]
        for i, g_ref in enumerate(g_refs):
            o_ref[4 + i:5 + i, 0:g_ref.shape[1]] = g_ref[...]
        o_ref[LOSS_ROW:LOSS_ROW + 1, 0:LANE] = loss_ref[...]

    return pl.pallas_call(body, name=name, out_shape=jax.ShapeDtypeStruct((PACK_ROWS, PACK_W), F32))(conv_w_grad, *grads, loss)


def _adam_small(gathered, wmv, name="adam_small"):
    def body(*refs):
        s_ref = refs[0]
        in_refs = refs[1:1 + 3 * len(REPL_W)]
        cw_ref, loss_ref = refs[1 + 3 * len(REPL_W)], refs[2 + 3 * len(REPL_W)]
        out_refs = refs[3 + 3 * len(REPL_W):-1]
        tot = refs[-1]
        acc = s_ref[0]
        for k in range(1, N_DEV):
            acc = acc + s_ref[k]
        tot[...] = acc
        cw_ref[...] = tot[0:4, :]
        loss_ref[...] = tot[LOSS_ROW:LOSS_ROW + 1, 0:LANE]
        for i, (_, width) in enumerate(REPL_W):
            g = tot[4 + i:5 + i, 0:width]
            w_ref, m_ref, v_ref = in_refs[3 * i:3 * i + 3]
            g_ref, d_ref, mo_ref, vo_ref = out_refs[4 * i:4 * i + 4]
            g_ref[...] = g
            d_ref[...], mo_ref[...], vo_ref[...] = _adam_math(g, w_ref[...], m_ref[...], v_ref[...])

    flat_in = [a for triple in wmv for a in triple]
    out_shape = [jax.ShapeDtypeStruct((4, PACK_W), F32), jax.ShapeDtypeStruct((1, LANE), F32)]
    for _, width in REPL_W:
        out_shape += [jax.ShapeDtypeStruct((1, width), F32)] * 4
    res = pl.pallas_call(body, name=name, out_shape=out_shape, scratch_shapes=[pltpu.VMEM((PACK_ROWS, PACK_W), F32)])(
        gathered, *flat_in)
    return res[0], res[1], [res[2 + 4 * i:6 + 4 * i] for i in range(len(REPL_W))]


def _cols_full(g):
    return jnp.transpose(g, (1, 0, 2)).reshape(g.shape[1], -1)


def _cols_split(full):
    k_dim, n_dim = full.shape
    return jnp.transpose(full.reshape(k_dim, N_DEV, n_dim // N_DEV), (1, 0, 2))


PROJ_BLOCK = {"z": (1024, 0), "dt": (LANE, 8), "q_c": (MLA_Q_RANK, 3), "xbc": (SSD_XBC, 1), "kv_c": (MLA_KV_RANK, 12),
              "k_rope": (LANE, 26)}


def _win_pad(wt):
    z = lambda n: jnp.zeros((n, wt.shape[1]), wt.dtype)
    return jnp.concatenate([wt[:1024], wt[2560:2576], z(112), wt[2576:2960], wt[1024:2560], wt[2960:3216], wt[3216:3248],
                            z(96)], axis=0)


def _win_unpad(wt):
    return jnp.concatenate([wt[:1024], wt[1536:3072], wt[1024:1040], wt[1152:1536], wt[3072:3328], wt[3328:3360]], axis=0)


def _heads_split_t(wt, a, b):
    w3 = wt.reshape(MLA_HEADS, a + b, wt.shape[1])
    return jnp.concatenate([w3[:, :a].reshape(-1, wt.shape[1]), w3[:, a:].reshape(-1, wt.shape[1])], axis=0)


def _heads_merge_t(wt, a, b):
    wa = wt[:MLA_HEADS * a].reshape(MLA_HEADS, a, wt.shape[1])
    wb = wt[MLA_HEADS * a:].reshape(MLA_HEADS, b, wt.shape[1])
    return jnp.concatenate([wa, wb], axis=1).reshape(-1, wt.shape[1])


def _heads_split(w, a, b):
    k_dim = w.shape[0]
    w3 = w.reshape(k_dim, MLA_HEADS, a + b)
    return jnp.concatenate([w3[:, :, :a].reshape(k_dim, -1), w3[:, :, a:].reshape(k_dim, -1)], axis=1)


def _heads_merge(w, a, b):
    k_dim = w.shape[0]
    wa = w[:, :MLA_HEADS * a].reshape(k_dim, MLA_HEADS, a)
    wb = w[:, MLA_HEADS * a:].reshape(k_dim, MLA_HEADS, b)
    return jnp.concatenate([wa, wb], axis=2).reshape(k_dim, -1)


def _pad_lanes(v, width=LANE):
    return jnp.concatenate([v, jnp.zeros((v.shape[0], width - v.shape[1]), v.dtype)], axis=1)


def _local_step(x, p, positions, tgt, W, P, comm=None):
    comm = comm or {}
    zero_tok = jnp.zeros((8, LANE), F32)
    s_dim = x.shape[0]
    inv_freq = 1.0 / (ROPE_BASE ** (jnp.arange(0, MLA_ROPE, 2, dtype=F32) / MLA_ROPE))
    ang = positions.astype(F32)[:, None] * inv_freq
    cos, sin = jnp.cos(ang), jnp.sin(ang)
    cos32 = jnp.concatenate([cos, cos], axis=1)
    sin32 = jnp.concatenate([-sin, sin], axis=1)
    cos512, sin512 = jnp.tile(cos32, (1, 16)), jnp.tile(sin32, (1, 16))
    cos128, sin128 = jnp.tile(cos32, (1, 4)), jnp.tile(sin32, (1, 4))
    bias_p, alog_p = _pad_lanes(P["ssd_dt_bias"]), _pad_lanes(P["ssd_A_log"])
    d_x = jnp.repeat(P["ssd_D"], SSD_HEAD_DIM, axis=1)

    xb, pb = x.astype(BF16), p.astype(BF16)
    proj = _mm(xb, W["w_in"], tb=True, after=comm.get("token0", zero_tok), name="mm_in")
    z, qc, kvc, kr = [(proj,) + PROJ_BLOCK[n] for n in ("z", "q_c", "kv_c", "k_rope")]
    xbca = _conv_fwd(proj, PROJ_BLOCK["xbc"][1], P["ssd_conv_w"], P["ssd_conv_b"])
    y, states = _ssd_fwd(xbca, proj, PROJ_BLOCK["dt"][1], bias_p, alog_p, d_x)
    (yssd,) = _rowwise(_gate_rms, [y, z], [P["ssd_norm_w"]], [(1024, BF16)], name="ssd_gate_norm")
    qn, kvn, krt = _rowwise(lambda a, c, u, cs, sn, wq, wkv: (_rms(a, wq), _rms(c, wkv), _spread4(_rope_fwd_fn(u, cs, sn))),
                            [qc, kvc, kr, cos128, sin128], [P["mla_q_norm_w"], P["mla_kv_norm_w"]],
                            [(MLA_Q_RANK, BF16), (MLA_KV_RANK, BF16), LANE], name="qkv_norm_rope_k")
    q = _mm(qn, W["mla_w_q_b"], tb=True, name="mm_q")
    kv = _mm(kvn, W["mla_w_kv_b"], name="mm_kv")
    (qr,) = _rowwise(_rope_fwd_fn, [(q, 512, 2), cos512, sin512], [], [512], name="rope_q")
    att = _att_fwd(q, qr, kv, krt)
    (ymla,) = _rowwise(_rms, [att], [P["mla_out_norm_w"]], [(1024, BF16)], name="out_norm")
    ycat = jnp.concatenate([yssd, ymla], axis=1)
    if "late_weights" in comm:
        W = {**W, **comm["late_weights"]("out", ycat)}
    mix = _mm(ycat, W["w_out"], name="mm_out")
    f_h1 = lambda xv, mv, g, b: _ln(ALPHA * xv + mv, g, b)
    h1, h1b = _rowwise(lambda *a: (f_h1(*a),) * 2, [x, mix], [P["ln_mix_g"], P["ln_mix_b"]], [1024, (1024, BF16)],
                       name="ln_mix")
    if "late_weights" in comm:
        W = {**W, **comm["late_weights"]("ffn", h1b)}
    hg = _mm(h1b, W["w_ffn_gate"], tb=True, out_dtype=BF16, name="mm_gate")
    hu, act = _mm(h1b, W["w_ffn_up"], tb=True, name="mm_up",
                  epilogue=(lambda u, g: (u, _silu(g.astype(F32)) * u), [hg], [BF16, BF16]))
    pg = _mm(h1b, W["w_ple_gate"], name="mm_ple_gate")
    pp = _mm(pb, W["w_ple_proj"], name="mm_ple")
    ffn = _mm(act, W["w_ffn_down"], name="mm_down")

    f_h2 = lambda hv, fv, pg, ppv, g, b: _ln(ALPHA * hv + fv + _sigmoid(pg) * ppv, g, b)

    def final_fn(hv, fv, pg, ppv, tv, g, b):
        h2, pull = jax.vjp(f_h2, hv, fv, pg, ppv, g, b)
        diff = h2 - tv
        loss = 0.5 * jnp.sum(jnp.mean(diff * diff, axis=-1, keepdims=True), axis=0, keepdims=True)
        d_h, d_f, d_pg, d_pp, d_g, d_b = pull(diff * (1.0 / D_MODEL))
        return d_h, d_f, d_pg, d_pp, d_g, d_b, jnp.broadcast_to(loss, (1, LANE))

    dh1_a, dffn, dpg, dpp, g_ffn_g, g_ffn_b, loss = _rowwise(
        final_fn, [h1, ffn, pg, pp, tgt], [P["ln_ffn_g"], P["ln_ffn_b"]], [1024] + [(1024, BF16)] * 3,
        [1024, 1024, LANE], name="final")

    G = {}
    def swiglu_bwd(d, g, u):
        g, u = g.astype(F32), u.astype(F32)
        sg = _sigmoid(g)
        return d * u * (sg * (1.0 + g * (1.0 - sg))), d * (g * sg)

    dg, du = _mm(dffn, W["w_ffn_down"], tb=True, name="mm_down_dx",
                 epilogue=(swiglu_bwd, [hg, hu], [BF16, BF16]))
    G["w_ffn_down"] = _mm(act, dffn, ta=True, out_dtype=GRAD_DT, name="mm_down_dw")
    dh1 = _mm(dg, W["w_ffn_gate"], add=dh1_a, name="mm_gate_dx")
    dh1 = _mm(du, W["w_ffn_up"], add=dh1, name="mm_up_dx")
    dh1 = _mm(dpg, W["w_ple_gate"], tb=True, add=dh1, name="mm_ple_gate_dx")
    G["w_ffn_gate"] = _mm(dg, h1b, ta=True, out_dtype=GRAD_DT, name="mm_gate_dw")
    G["w_ffn_up"] = _mm(du, h1b, ta=True, out_dtype=GRAD_DT, name="mm_up_dw")
    G["w_ple_gate"] = _mm(h1b, dpg, ta=True, out_dtype=GRAD_DT, name="mm_ple_gate_dw")
    G["w_ple_proj"] = _mm(pb, dpp, ta=True, out_dtype=GRAD_DT, name="mm_ple_dw")
    dx_a, dmix, g_mix_g, g_mix_b = _rowwise(
        lambda xv, mv, dv, g, b: _vjp_rows(f_h1)(xv, mv, g, b, dv), [x, mix, dh1], [P["ln_mix_g"], P["ln_mix_b"]],
        [1024, (1024, BF16)], [1024, 1024], name="ln_mix_bwd")
    dycat = _mm(dmix, W["w_out"], tb=True, name="mm_out_dx")
    G["w_out"] = _mm(ycat, dmix, ta=True, out_dtype=GRAD_DT, name="mm_out_dw")

    grads_done = comm.get("grads", lambda group, grads: zero_tok)
    tok1 = grads_done("ffn", G)
    datt, g_out_norm = _rowwise(lambda a, dv, w, t: _vjp_rows(_rms)(a, w, dv + jnp.min(t)), [att, (dycat, 1024, 1)],
                                [P["mla_out_norm_w"], tok1], [1024], [1024], name="out_norm_bwd")
    dqn_nope, dqr, dkn, dv, dkrt = _att_bwd(q, qr, kv, krt, att, datt)
    dkv = jnp.concatenate([dkn, dv], axis=1)
    (dq_rope,) = _rowwise(lambda d0, d1, c, s: _rope_bwd_fn(d0 + d1, c, s), [(dqr, 512, 0), (dqr, 512, 1), cos512, sin512],
                          [], [(512, BF16)], name="rope_q_bwd")

    def rope_k_bwd(*a):
        d = _spread4(functools.reduce(lambda u, w: u + w, a[:-2]))
        lane = lax.broadcasted_iota(jnp.int32, d.shape, 1)
        return _rope_bwd_fn(jnp.where(lane < MLA_ROPE, d, 0.0), a[-2], a[-1])

    (dkr,) = _rowwise(rope_k_bwd, [(dkrt, LANE, k) for k in range(MLA_HEADS // 2)] + [cos128, sin128], [], [(LANE, BF16)],
                      name="rope_k_bwd")
    dq = jnp.concatenate([dqn_nope, dq_rope], axis=1)
    dqn = _mm(dq, W["mla_w_q_b"], name="mm_q_dx")
    G["mla_w_q_b"] = _mm(dq, qn, ta=True, out_dtype=GRAD_DT, name="mm_q_dw")
    dkvn = _mm(dkv, W["mla_w_kv_b"], tb=True, name="mm_kv_dx")
    G["mla_w_kv_b"] = _mm(kvn, dkv, ta=True, out_dtype=GRAD_DT, name="mm_kv_dw")
    tok2 = grads_done("mla", G)
    def qkv_norm_bwd(a, da, c, dc, wq, wkv, t):
        (d_a, d_wq), (d_c, d_wkv) = _vjp_rows(_rms)(a, wq, da + jnp.min(t)), _vjp_rows(_rms)(c, wkv, dc)
        return d_a, d_c, d_wq, d_wkv

    dqc, dkvc, g_q_norm, g_kv_norm = _rowwise(
        qkv_norm_bwd, [qc, dqn, kvc, dkvn], [P["mla_q_norm_w"], P["mla_kv_norm_w"], tok2],
        [(MLA_Q_RANK, BF16), (MLA_KV_RANK, BF16)], [MLA_Q_RANK, MLA_KV_RANK], name="qkv_norm_bwd")

    dy, dz, g_ssd_norm = _rowwise(lambda yv, zv, dv, w, t: _vjp_rows(_gate_rms)(yv, zv, w, dv + jnp.min(t)),
                                  [y, z, (dycat, 1024, 0)], [P["ssd_norm_w"], tok1], [1024, (1024, BF16)], [1024],
                                  name="ssd_gate_norm_bwd")
    dxbca, ddtr, g_dt_bias, g_alog, g_d = _ssd_bwd(xbca, proj, PROJ_BLOCK["dt"][1], bias_p, alog_p, d_x, states, dy)
    da, g_conv_w, g_conv_b = _conv_bwd_pre(proj, PROJ_BLOCK["xbc"][1], P["ssd_conv_w"], P["ssd_conv_b"], dxbca)
    dxbc = _conv_bwd_in(da, P["ssd_conv_w"])

    small = {
        "ssd_conv_b": g_conv_b, "ssd_dt_bias": g_dt_bias, "ssd_A_log": g_alog, "ssd_D": g_d, "ssd_norm_w": g_ssd_norm,
        "mla_q_norm_w": g_q_norm, "mla_kv_norm_w": g_kv_norm, "mla_out_norm_w": g_out_norm, "ln_mix_g": g_mix_g,
        "ln_mix_b": g_mix_b, "ln_ffn_g": g_ffn_g, "ln_ffn_b": g_ffn_b,
    }
    packed = _pack_small(g_conv_w, [small[n] for n, _ in REPL_W], loss)
    if "small" in comm:
        comm["small"](packed)

    dproj = jnp.concatenate([dz, ddtr, dqc, dxbc, dkvc, dkr], axis=1)
    G["w_in"] = _mm(dproj, xb, ta=True, out_dtype=GRAD_DT, name="mm_in_dw")
    grad_x = _mm(dproj, W["w_in"], add=dx_a, after=grads_done("in", G), name="mm_in_dx")
    return grad_x, G, packed


def kernel(x, p, positions, w_in, ssd_conv_w, ssd_conv_b, ssd_dt_bias, ssd_A_log, ssd_D, ssd_norm_w, mla_q_norm_w, mla_w_q_b, mla_kv_norm_w, mla_w_kv_b, mla_out_norm_w, w_out, ln_mix_g, ln_mix_b, w_ffn_gate, w_ffn_up, w_ffn_down, w_ple_gate, w_ple_proj, ln_ffn_g, ln_ffn_b, loss_target, m_w_in, m_ssd_conv_w, m_ssd_conv_b, m_ssd_dt_bias, m_ssd_A_log, m_ssd_D, m_ssd_norm_w, m_mla_q_norm_w, m_mla_w_q_b, m_mla_kv_norm_w, m_mla_w_kv_b, m_mla_out_norm_w, m_w_out, m_ln_mix_g, m_ln_mix_b, m_w_ffn_gate, m_w_ffn_up, m_w_ffn_down, m_w_ple_gate, m_w_ple_proj, m_ln_ffn_g, m_ln_ffn_b, v_w_in, v_ssd_conv_w, v_ssd_conv_b, v_ssd_dt_bias, v_ssd_A_log, v_ssd_D, v_ssd_norm_w, v_mla_q_norm_w, v_mla_w_q_b, v_mla_kv_norm_w, v_mla_w_kv_b, v_mla_out_norm_w, v_w_out, v_ln_mix_g, v_ln_mix_b, v_w_ffn_gate, v_w_ffn_up, v_w_ffn_down, v_w_ple_gate, v_w_ple_proj, v_ln_ffn_g, v_ln_ffn_b):
    args = dict(locals())
    core = lax.axis_index("c")
    me = 4 * lax.axis_index("x") + 2 * lax.axis_index("y") + core

    conv_sh = ssd_conv_w[0]
    conv_hi = conv_sh.astype(BF16)
    conv_lo = (conv_sh - conv_hi.astype(F32)).astype(BF16)
    stored = lambda n, pre="": jnp.transpose(args[pre + n][0]) if n in TRANSPOSED else args[pre + n][0]
    shards = {n: stored(n).astype(BF16) for n in BIG}
    rows_full = lambda g: g.reshape(-1, g.shape[2])

    early = _gather_many([shards[n] for n in EARLY] + [jnp.concatenate([conv_hi, conv_lo], axis=0)], "gather_early")
    gw = dict(zip(EARLY, early[:-1]))
    conv_g = early[-1].astype(F32)
    W = {
        "w_in": _win_pad(rows_full(gw["w_in"])),
        "mla_w_q_b": _heads_split_t(rows_full(gw["mla_w_q_b"]), MLA_NOPE, MLA_ROPE),
        "mla_w_kv_b": _heads_split(_cols_full(gw["mla_w_kv_b"]), MLA_NOPE, MLA_V),
    }
    P = {n: args[n] for n, _ in REPL_W}
    P["ssd_conv_w"] = _cols_full(conv_g[:, :4] + conv_g[:, 4:])

    late, after = {}, early[0]
    for group, names in LATE.items():
        lands = [lax.dynamic_update_slice(lax.empty((N_DEV,) + shards[n].shape, BF16), shards[n][None], (me, 0, 0)) for n in names]
        late[group] = _split_start([shards[n] for n in names], lands, _plan_broadcast, N_DEV - 1,
                                   "gather_" + group + "_start", after=after)
        after = late[group][4]

    def late_weights(group, after):
        _, got = _split_wait(*late[group][:4], after, _plan_broadcast, "gather_" + group + "_wait")
        return {n: _cols_full(g) if n == "w_ple_proj" else rows_full(g) for n, g in zip(LATE[group], got)}

    def to_blocks(n, g):
        if n == "w_in":
            g = _win_unpad(g)
        elif n == "mla_w_q_b":
            g = _heads_merge_t(g, MLA_NOPE, MLA_ROPE)
        elif n == "mla_w_kv_b":
            g = _heads_merge(g, MLA_NOPE, MLA_V)
        if n in ROW_SHARDED or n in TRANSPOSED:
            return g.reshape(N_DEV, -1, g.shape[1])
        return _cols_split(g)

    flight = {}

    def grads(group, G):
        gl = [to_blocks(n, G[n]) for n in GRAD_GROUPS[group]]
        flight[group] = _split_start(gl, [lax.empty(g.shape, g.dtype) for g in gl], _plan_scatter, N_DEV - 1,
                                     "grads_" + group + "_start", after=flight.get("small") if group == "in" else None)
        return flight[group][4]

    def small(packed):
        flight["small"] = _all_gather(packed, "gather_small")

    grad_x, G, packed = _local_step(x[0], p[0, 0], positions[0], loss_target[0], W, P,
                                    comm={"token0": after, "late_weights": late_weights, "grads": grads, "small": small})

    me_arr = me.astype(jnp.int32).reshape(1)
    big_out = {}

    def finish(group, after):
        mine, recv = _split_wait(*flight[group][:4], after, _plan_scatter, "grads_" + group + "_wait")
        for n, g, r in zip(GRAD_GROUPS[group], mine, recv):
            big_out[n] = _adam(r, stored(n), stored(n, "m_"), stored(n, "v_"), "adam_" + n, own=g, own_idx=me_arr)
        return big_out[GRAD_GROUPS[group][-1]][0]

    done = finish("ffn", grad_x)
    conv_sum, loss_row, small_out = _adam_small(flight["small"], [(args[n], args["m_" + n], args["v_" + n]) for n, _ in REPL_W])
    finish("in", finish("mla", done))
    conv_grad = lax.dynamic_slice_in_dim(conv_sum, me * 192, 192, axis=1)
    conv_out = _adam(conv_grad[None], conv_sh, m_ssd_conv_w[0], v_ssd_conv_w[0], "adam_conv")
    small_map = {n: small_out[i] for i, (n, _) in enumerate(REPL_W)}

    def outputs(idx):
        res = []
        for n in WEIGHT_ORDER:
            if n == "ssd_conv_w":
                res.append(conv_out[idx][None])
            elif n in big_out:
                res.append((jnp.transpose(big_out[n][idx]) if n in TRANSPOSED else big_out[n][idx])[None])
            else:
                res.append(small_map[n][idx])
        return res

    return (loss_row[0, 0], grad_x[None], *outputs(0), *outputs(1), *outputs(2), *outputs(3))
```

```python
import functools
import math

import numpy as np
import jax
import jax.numpy as jnp
from jax import lax
from jax.experimental import pallas as pl
from jax.experimental.pallas import tpu as pltpu

F32 = jnp.float32
BF16 = jnp.bfloat16
HI = lax.Precision.HIGHEST

N_DEV = 8
D_MODEL = 1024
PLE_DIM = 256
SSD_HEADS = 16
SSD_HEAD_DIM = 64
SSD_INNER = 1024
SSD_STATE = 128
SSD_XBC = 1536
SSD_CHUNK = 128
MLA_HEADS = 16
MLA_Q_RANK = 384
MLA_KV_RANK = 256
MLA_NOPE = 64
MLA_ROPE = 32
MLA_V = 64
ROPE_BASE = 10000.0
D_FF = 2816
IN_WIDTH = 3248
IN_PAD = 3456
ALPHA = 2.0 ** 0.25
EPS = 1e-6
LN_EPS = 1e-5
ATT_SCALE = 1.0 / math.sqrt(MLA_NOPE + MLA_ROPE)
ADAM_LR, ADAM_B1, ADAM_B2, ADAM_EPS, ADAM_WD, ADAM_STEP = 0.001, 0.9, 0.999, 1e-08, 0.01, 10

LANE = 128
MXU_DIM = 256
MM_TM, MM_TN, MM_TK = 1408, 1408, 2048
ROW_TILE = 256
ATT_TQ = 256

GRAD_DT = BF16

BIG = ("w_in", "mla_w_q_b", "mla_w_kv_b", "w_out", "w_ffn_gate", "w_ffn_up", "w_ffn_down", "w_ple_gate", "w_ple_proj")
EARLY = ("w_in", "mla_w_q_b", "mla_w_kv_b")
LATE = {"out": ("w_out", "w_ple_gate", "w_ple_proj"), "ffn": ("w_ffn_gate", "w_ffn_up", "w_ffn_down")}
GRAD_GROUPS = {"ffn": ("w_ffn_gate", "w_ffn_up", "w_ffn_down", "w_ple_gate", "w_ple_proj", "w_out"),
               "mla": ("mla_w_q_b", "mla_w_kv_b"), "in": ("w_in",)}
ROW_SHARDED = ("w_out", "w_ffn_down", "w_ple_gate")
TRANSPOSED = ("w_in", "mla_w_q_b", "w_ffn_gate", "w_ffn_up")
WEIGHT_ORDER = ("w_in", "ssd_conv_w", "ssd_conv_b", "ssd_dt_bias", "ssd_A_log", "ssd_D", "ssd_norm_w", "mla_q_norm_w",
                "mla_w_q_b", "mla_kv_norm_w", "mla_w_kv_b", "mla_out_norm_w", "w_out", "ln_mix_g", "ln_mix_b",
                "w_ffn_gate", "w_ffn_up", "w_ffn_down", "w_ple_gate", "w_ple_proj", "ln_ffn_g", "ln_ffn_b")


def _tile(dim, cap, prefer=None):
    cands = [t for t in range(LANE, min(cap, dim) + 1, LANE) if dim % t == 0]
    if not cands:
        return dim
    if prefer is None:
        return max(cands)
    fill = lambda t: t / (MXU_DIM * -(-t // MXU_DIM))
    good = min(0.9, max(fill(t) for t in cands))
    return min((t for t in cands if fill(t) >= good), key=lambda t: abs(t - prefer))


def _dot(a, b, dims=(((1,), (0,)), ((), ())), precision=None):
    return lax.dot_general(a, b, dims, preferred_element_type=F32, precision=precision)


_NT = (((1,), (1,)), ((), ()))
_TN = (((0,), (0,)), ((), ()))


def _mm(a, b, *, ta=False, tb=False, add=None, out_dtype=F32, after=None, epilogue=None, name):
    k_dim, m_dim = a.shape if ta else a.shape[::-1]
    n_dim, kb = b.shape if tb else b.shape[::-1]
    assert k_dim == kb
    tm, tn, tk = _tile(m_dim, MM_TM), _tile(n_dim, MM_TN, prefer=1024), _tile(k_dim, MM_TK, prefer=MM_TK)
    nk = k_dim // tk
    dims = (((0 if ta else 1,), (1 if tb else 0,)), ((), ()))
    a_spec = pl.BlockSpec((tk, tm), lambda i, j, k: (k, i)) if ta else pl.BlockSpec((tm, tk), lambda i, j, k: (i, k))
    b_spec = pl.BlockSpec((tn, tk), lambda i, j, k: (j, k)) if tb else pl.BlockSpec((tk, tn), lambda i, j, k: (k, j))
    o_spec = pl.BlockSpec((tm, tn), lambda i, j, k: (i, j))
    epi_fn, epi_in, out_dtypes = epilogue if epilogue else (None, [], [out_dtype])
    tiles = ([add] if add is not None else []) + list(epi_in)
    n_out = len(out_dtypes)

    def body(*refs):
        a_ref, b_ref = refs[:2]
        tile_refs = refs[2:2 + len(tiles)]
        out_refs = refs[len(refs) - n_out - (nk > 1):len(refs) - (nk > 1)]
        part = _dot(a_ref[...].astype(BF16), b_ref[...].astype(BF16), dims)
        if add is not None:
            part_add = lambda v: v + tile_refs[0][...]
        else:
            part_add = lambda v: v

        def write(total):
            extra = [r[...] for r in tile_refs[add is not None:]]
            outs = epi_fn(total, *extra) if epi_fn else (total,)
            for o_ref, val in zip(out_refs, outs):
                o_ref[...] = val.astype(o_ref.dtype)

        if nk == 1:
            write(part_add(part))
            return
        acc = refs[-1]
        k = pl.program_id(2)

        @pl.when(k == 0)
        def _():
            acc[...] = part_add(part)

        @pl.when(k > 0)
        def _():
            acc[...] += part

        @pl.when(k == nk - 1)
        def _():
            write(acc[...])

    ins = [a, b] + tiles + ([after] if after is not None else [])
    specs = [a_spec, b_spec] + [o_spec] * len(tiles) + ([pl.BlockSpec(memory_space=pl.ANY)] if after is not None else [])
    res = pl.pallas_call(
        body, name=name, grid=(m_dim // tm, n_dim // tn, nk), in_specs=specs, out_specs=[o_spec] * n_out,
        out_shape=[jax.ShapeDtypeStruct((m_dim, n_dim), dt) for dt in out_dtypes],
        scratch_shapes=[pltpu.VMEM((tm, tn), F32)] if nk > 1 else [],
        compiler_params=pltpu.CompilerParams(dimension_semantics=("parallel", "parallel", "arbitrary")),
    )(*ins)
    return res if epilogue else res[0]


def _rowwise(fn, rows, consts, out_widths, acc_widths=(), *, name, tr=ROW_TILE):
    row_arrays, row_specs = [], []
    first_arr = rows[0][0] if isinstance(rows[0], tuple) else rows[0]
    s_dim = first_arr.shape[-2]
    tr = min(tr, s_dim)
    for r in rows:
        arr, width, cb = r if isinstance(r, tuple) else (r, r.shape[-1], 0)
        row_arrays.append(arr)
        if arr.ndim == 3:
            row_specs.append(pl.BlockSpec((None, tr, width), functools.partial(lambda i, k: (k, i, 0), k=cb)))
        else:
            row_specs.append(pl.BlockSpec((tr, width), functools.partial(lambda i, cb: (i, cb), cb=cb)))
    const_specs = [pl.BlockSpec(c.shape, lambda i: (0, 0)) for c in consts]
    nr, nc, no, na = len(rows), len(consts), len(out_widths), len(acc_widths)

    def body(*refs):
        ins = [r[...] for r in refs[:nr + nc]]
        res = fn(*ins)
        if not isinstance(res, (tuple, list)):
            res = (res,)
        out_refs = refs[nr + nc:nr + nc + no]
        acc_refs = refs[nr + nc + no:]
        for o_ref, val in zip(out_refs, res[:no]):
            o_ref[...] = val.astype(o_ref.dtype)
        first = pl.program_id(0) == 0
        for a_ref, val in zip(acc_refs, res[no:]):
            @pl.when(first)
            def _(a_ref=a_ref, val=val):
                a_ref[...] = val

            @pl.when(jnp.logical_not(first))
            def _(a_ref=a_ref, val=val):
                a_ref[...] += val

    outs = [w if isinstance(w, tuple) else (w, F32) for w in out_widths]
    out_shape = [jax.ShapeDtypeStruct((s_dim, w), dt) for w, dt in outs]
    out_shape += [jax.ShapeDtypeStruct((1, w), F32) for w in acc_widths]
    out_specs = [pl.BlockSpec((tr, w), lambda i: (i, 0)) for w, _ in outs]
    out_specs += [pl.BlockSpec((1, w), lambda i: (0, 0)) for w in acc_widths]
    res = pl.pallas_call(
        body, name=name, grid=(s_dim // tr,), in_specs=row_specs + const_specs, out_specs=out_specs, out_shape=out_shape,
        compiler_params=pltpu.CompilerParams(dimension_semantics=("arbitrary",)),
    )(*row_arrays, *consts)
    return res


def _colsum(v):
    return jnp.sum(v, axis=0, keepdims=True)


def _rms(u, g):
    return u * lax.rsqrt(jnp.mean(u * u, axis=-1, keepdims=True) + EPS) * g


def _ln(u, g, b):
    mu = jnp.mean(u, axis=-1, keepdims=True)
    d = u - mu
    var = jnp.mean(d * d, axis=-1, keepdims=True)
    return d * lax.rsqrt(var + LN_EPS) * g + b


def _sigmoid(v):
    return 1.0 / (1.0 + jnp.exp(-v))


def _silu(v):
    return v * _sigmoid(v)


def _softplus(v):
    y = jnp.exp(-jnp.abs(v))
    w = 1.0 + y
    log1p = jnp.where(w == 1.0, y, jnp.log(w) * y / jnp.where(w == 1.0, 1.0, w - 1.0))
    return jnp.maximum(v, 0.0) + log1p


def _gate_rms(y, z, w):
    return _rms(y * _silu(z), w)


def _vjp_rows(f):
    def fn(*args):
        prim, ct = args[:-1], args[-1]
        _, pull = jax.vjp(f, *prim)
        return pull(ct)
    return fn


def _conv_pre(cur, prev, w, b, first):
    row = lax.broadcasted_iota(jnp.int32, cur.shape, 0)
    acc = cur * w[3:4, :] + b
    for j in (1, 2, 3):
        tail = jnp.where(first, 0.0, pltpu.roll(prev, j, 0))
        acc = acc + jnp.where(row >= j, pltpu.roll(cur, j, 0), tail) * w[3 - j:4 - j, :]
    return acc


def _conv_fwd(u, ucb, w, b, name="conv_fwd"):
    s_dim, width = u.shape[0], w.shape[1]
    tr = min(ROW_TILE, s_dim)

    def body(cur_ref, prev_ref, w_ref, b_ref, o_ref):
        pre = _conv_pre(cur_ref[...], prev_ref[...], w_ref, b_ref[...], pl.program_id(0) == 0)
        o_ref[...] = _silu(pre)

    return pl.pallas_call(
        body, name=name, grid=(s_dim // tr,),
        in_specs=[pl.BlockSpec((tr, width), lambda i: (i, ucb)),
                  pl.BlockSpec((tr, width), lambda i: (jnp.maximum(i - 1, 0), ucb)),
                  pl.BlockSpec(w.shape, lambda i: (0, 0)), pl.BlockSpec(b.shape, lambda i: (0, 0))],
        out_specs=pl.BlockSpec((tr, width), lambda i: (i, 0)), out_shape=jax.ShapeDtypeStruct((s_dim, width), F32),
        compiler_params=pltpu.CompilerParams(dimension_semantics=("arbitrary",)),
    )(u, u, w, b)


def _conv_bwd_pre(u, ucb, w, b, dact, name="conv_bwd_pre"):
    s_dim, width = u.shape[0], w.shape[1]
    tr = min(ROW_TILE, s_dim)

    def body(cur_ref, prev_ref, w_ref, b_ref, d_ref, da_ref, dw_ref, db_ref):
        first = pl.program_id(0) == 0
        cur, prev = cur_ref[...], prev_ref[...]
        pre = _conv_pre(cur, prev, w_ref, b_ref[...], first)
        sg = _sigmoid(pre)
        da = d_ref[...] * (sg * (1.0 + pre * (1.0 - sg)))
        da_ref[...] = da
        row = lax.broadcasted_iota(jnp.int32, cur.shape, 0)

        @pl.when(first)
        def _():
            dw_ref[...] = jnp.zeros_like(dw_ref)
            db_ref[...] = jnp.zeros_like(db_ref)

        db_ref[...] += _colsum(da)
        dw_ref[3:4, :] += _colsum(da * cur)
        for j in (1, 2, 3):
            tail = jnp.where(first, 0.0, pltpu.roll(prev, j, 0))
            sh = jnp.where(row >= j, pltpu.roll(cur, j, 0), tail)
            dw_ref[3 - j:4 - j, :] += _colsum(da * sh)

    return pl.pallas_call(
        body, name=name, grid=(s_dim // tr,),
        in_specs=[pl.BlockSpec((tr, width), lambda i: (i, ucb)),
                  pl.BlockSpec((tr, width), lambda i: (jnp.maximum(i - 1, 0), ucb)),
                  pl.BlockSpec(w.shape, lambda i: (0, 0)), pl.BlockSpec(b.shape, lambda i: (0, 0)),
                  pl.BlockSpec((tr, width), lambda i: (i, 0))],
        out_specs=[pl.BlockSpec((tr, width), lambda i: (i, 0)), pl.BlockSpec(w.shape, lambda i: (0, 0)),
                   pl.BlockSpec(b.shape, lambda i: (0, 0))],
        out_shape=[jax.ShapeDtypeStruct((s_dim, width), F32), jax.ShapeDtypeStruct(w.shape, F32),
                   jax.ShapeDtypeStruct(b.shape, F32)],
        compiler_params=pltpu.CompilerParams(dimension_semantics=("arbitrary",)),
    )(u, u, w, b, dact)


def _conv_bwd_in(da, w, name="conv_bwd_in"):
    s_dim, width = da.shape
    tr = min(ROW_TILE, s_dim)
    n = s_dim // tr

    def body(cur_ref, nxt_ref, w_ref, o_ref):
        last = pl.program_id(0) == n - 1
        cur, nxt = cur_ref[...], nxt_ref[...]
        row = lax.broadcasted_iota(jnp.int32, cur.shape, 0)
        acc = cur * w_ref[3:4, :]
        for j in (1, 2, 3):
            head = jnp.where(last, 0.0, pltpu.roll(nxt, tr - j, 0))
            acc = acc + jnp.where(row < tr - j, pltpu.roll(cur, tr - j, 0), head) * w_ref[3 - j:4 - j, :]
        o_ref[...] = acc.astype(o_ref.dtype)

    return pl.pallas_call(
        body, name=name, grid=(n,),
        in_specs=[pl.BlockSpec((tr, width), lambda i: (i, 0)), pl.BlockSpec((tr, width), lambda i: (jnp.minimum(i + 1, n - 1), 0)),
                  pl.BlockSpec(w.shape, lambda i: (0, 0))],
        out_specs=pl.BlockSpec((tr, width), lambda i: (i, 0)), out_shape=jax.ShapeDtypeStruct((s_dim, width), BF16),
        compiler_params=pltpu.CompilerParams(dimension_semantics=("arbitrary",)),
    )(da, da, w)


def _sel_dot(a, sel, pieces, dims=(((1,), (0,)), ((), ())), sel_left=False):
    sel = sel.astype(BF16)
    acc, rest = None, a
    for _ in range(pieces):
        piece = rest.astype(BF16)
        rest = rest - piece.astype(F32)
        part = _dot(sel, piece, dims) if sel_left else _dot(piece, sel, dims)
        acc = part if acc is None else acc + part
    return acc


def _ssd_consts():
    L = SSD_CHUNK
    tri = np.tril(np.ones((L, L), np.float32))
    expand = np.zeros((LANE, SSD_INNER), np.float32)
    expand128 = np.zeros((LANE, SSD_HEADS * LANE), np.float32)
    for h in range(SSD_HEADS):
        expand[h, h * SSD_HEAD_DIM:(h + 1) * SSD_HEAD_DIM] = 1.0
        expand128[h, h * LANE:(h + 1) * LANE] = 1.0
    return jnp.asarray(tri), jnp.asarray(expand), jnp.asarray(expand128), jnp.asarray(expand.T.copy())


def _ssd_prep(dt_ref, bias_ref, alog_ref, tri_ref, exp_ref, exp128_ref, cs_s, cst_s, ex_s, csx_s):
    L = SSD_CHUNK
    dt = _softplus(dt_ref[...] + bias_ref[...])
    a = -jnp.exp(alog_ref[...])
    cs = _sel_dot(dt * a, tri_ref[...], 3, sel_left=True)
    cs_s[...] = cs
    cst_s[...] = cs.T
    last = cs_s[L - 1:L, :]
    expand = exp_ref[...]
    ex_s[...] = _sel_dot(jnp.exp(cs), expand, 2)
    f_x = _sel_dot(jnp.exp(last - cs), expand, 2)
    dt_x = _sel_dot(dt, expand, 2)
    csx_s[...] = _sel_dot(cs, exp128_ref[...], 3)
    t_x = ex_s[L - 1:L, :]
    return dt, a, dt_x, f_x, t_x


def _decay_matrix(csx_s, cst_s, h, tril):
    seg = csx_s[:, h * LANE:(h + 1) * LANE] - cst_s[h:h + 1, :]
    return jnp.exp(jnp.where(tril, seg, -jnp.inf))


def _ssd_fwd(xbca, dtr, dtcb, bias, alog, d_x, name="ssd_fwd"):
    s_dim = xbca.shape[0]
    L = SSD_CHUNK
    nc = s_dim // L
    tri, expand, expand128, _ = _ssd_consts()

    def body(xs_ref, b_ref, c_ref, dt_ref, bias_ref, alog_ref, dx_ref, tri_ref, exp_ref, exp128_ref,
             y_ref, st_ref, st_s, cs_s, cst_s, ex_s, csx_s):
        @pl.when(pl.program_id(0) == 0)
        def _():
            st_s[...] = jnp.zeros_like(st_s)

        dt, a, dt_x, f_x, t_x = _ssd_prep(dt_ref, bias_ref, alog_ref, tri_ref, exp_ref, exp128_ref, cs_s, cst_s, ex_s, csx_s)
        st_ref[0] = st_s[...]
        row = lax.broadcasted_iota(jnp.int32, (L, L), 0)
        col = lax.broadcasted_iota(jnp.int32, (L, L), 1)
        tril = row >= col
        low = col < SSD_HEAD_DIM
        for g in range(2):
            bg = b_ref[:, g * LANE:(g + 1) * LANE]
            cg = c_ref[:, g * LANE:(g + 1) * LANE].astype(BF16)
            gmat = _dot(cg, bg.astype(BF16), _NT)
            bgt = bg.T.astype(BF16)
            for jj in range(4):
                j = 4 * g + jj
                sl = slice(j * LANE, (j + 1) * LANE)
                xp = xs_ref[:, sl]
                x_dt = xp * dt_x[:, sl]
                xb = x_dt.astype(BF16)
                yd = []
                for e in range(2):
                    lm = _decay_matrix(csx_s, cst_s, 2 * j + e, tril)
                    yd.append(_dot((gmat * lm).astype(BF16), xb))
                stp = st_s[j]
                z = _dot(cg, stp.astype(BF16))
                y_ref[:, sl] = jnp.where(low, yd[0], yd[1]) + ex_s[:, sl] * z + dx_ref[:, sl] * xp
                xf = (x_dt * f_x[:, sl]).astype(BF16)
                st_s[j] = t_x[:, sl] * stp + _dot(bgt, xf)

    const = lambda shape: pl.BlockSpec(shape, lambda c: tuple(0 for _ in shape))
    return pl.pallas_call(
        body, name=name, grid=(nc,),
        in_specs=[pl.BlockSpec((L, 1024), lambda c: (c, 0)), pl.BlockSpec((L, 256), lambda c: (c, 4)),
                  pl.BlockSpec((L, 256), lambda c: (c, 5)), pl.BlockSpec((L, LANE), lambda c: (c, dtcb)),
                  const((1, LANE)), const((1, LANE)), const((1, 1024)), const((L, L)), const((LANE, 1024)),
                  const((LANE, 2048))],
        out_specs=[pl.BlockSpec((L, 1024), lambda c: (c, 0)), pl.BlockSpec((1, 8, LANE, LANE), lambda c: (c, 0, 0, 0))],
        out_shape=[jax.ShapeDtypeStruct((s_dim, 1024), F32), jax.ShapeDtypeStruct((nc, 8, LANE, LANE), F32)],
        scratch_shapes=[pltpu.VMEM((8, LANE, LANE), F32), pltpu.VMEM((L, LANE), F32), pltpu.VMEM((LANE, L), F32),
                        pltpu.VMEM((L, 1024), F32), pltpu.VMEM((L, 2048), F32)],
        compiler_params=pltpu.CompilerParams(dimension_semantics=("arbitrary",)),
    )(xbca, xbca, xbca, dtr, bias, alog, d_x, tri, expand, expand128)


def _ssd_bwd(xbca, dtr, dtcb, bias, alog, d_x, states, dy, name="ssd_bwd"):
    s_dim = xbca.shape[0]
    L = SSD_CHUNK
    nc = s_dim // L
    tri, expand, expand128, expand_t = _ssd_consts()

    def body(xs_ref, b_ref, c_ref, dt_ref, bias_ref, alog_ref, dx_ref, tri_ref, exp_ref, exp128_ref, expt_ref,
             st_ref, dy_ref, dxbc_ref, ddt_ref, dbias_ref, dalog_ref, dd_ref,
             dst_s, cs_s, cst_s, ex_s, csx_s, dcsx_s, ddtx_s, dcol_s, drow_s, dlast_s, dd_s):
        @pl.when(pl.program_id(0) == 0)
        def _():
            dst_s[...] = jnp.zeros_like(dst_s)
            dbias_ref[...] = jnp.zeros_like(dbias_ref)
            dalog_ref[...] = jnp.zeros_like(dalog_ref)
            dd_s[...] = jnp.zeros_like(dd_s)

        dt, a, dt_x, f_x, t_x = _ssd_prep(dt_ref, bias_ref, alog_ref, tri_ref, exp_ref, exp128_ref, cs_s, cst_s, ex_s, csx_s)
        row = lax.broadcasted_iota(jnp.int32, (L, L), 0)
        col = lax.broadcasted_iota(jnp.int32, (L, L), 1)
        tril = row >= col
        low = col < SSD_HEAD_DIM
        dcol_s[...] = jnp.zeros_like(dcol_s)
        drow_s[...] = jnp.zeros_like(drow_s)
        for g in range(2):
            bg = b_ref[:, g * LANE:(g + 1) * LANE]
            cg = c_ref[:, g * LANE:(g + 1) * LANE]
            bgb, cgb = bg.astype(BF16), cg.astype(BF16)
            gmat = _dot(cgb, bgb, _NT)
            d_g = jnp.zeros((L, L), F32)
            d_b = jnp.zeros((L, LANE), F32)
            d_c = jnp.zeros((L, LANE), F32)
            for jj in range(4):
                j = 4 * g + jj
                sl = slice(j * LANE, (j + 1) * LANE)
                xp = xs_ref[:, sl]
                dtp = dt_x[:, sl]
                x_dt = xp * dtp
                xb = x_dt.astype(BF16)
                dyp = dy_ref[:, sl]
                dd_s[:, sl] += _colsum(dyp * xp)
                d_xdt = jnp.zeros((L, LANE), F32)
                for e in range(2):
                    h = 2 * j + e
                    lm = _decay_matrix(csx_s, cst_s, h, tril)
                    m = gmat * lm
                    dye = jnp.where(low if e == 0 else jnp.logical_not(low), dyp, 0.0).astype(BF16)
                    d_m = jnp.where(tril, _dot(dye, xb, _NT), 0.0)
                    d_xdt = d_xdt + _dot(m.astype(BF16), dye, _TN)
                    d_g = d_g + d_m * lm
                    w = d_m * m
                    dcol_s[...] += jnp.where(col == h, jnp.sum(w, axis=1, keepdims=True), 0.0)
                    drow_s[...] += jnp.where(row == h, jnp.sum(w, axis=0, keepdims=True), 0.0)
                stp = st_ref[0, j]
                stb = stp.astype(BF16)
                dstn = dst_s[j]
                dstb = dstn.astype(BF16)
                e_p = ex_s[:, sl]
                f_p = f_x[:, sl]
                t_p = t_x[:, sl]
                z = _dot(cgb, stb)
                d_z = (e_p * dyp).astype(BF16)
                d_c = d_c + _dot(d_z, stb, _NT)
                d_xf = _dot(bgb, dstb)
                d_b = d_b + _dot((x_dt * f_p).astype(BF16), dstb, _NT)
                d_xdt = d_xdt + f_p * d_xf
                d_f = x_dt * d_xf * f_p
                dcsx_s[:, sl] = dyp * e_p * z - d_f
                dlast_s[:, sl] = _colsum(d_f) + _colsum(dstn * stp) * t_p
                dst_s[j] = _dot(cgb, d_z, _TN) + t_p * dstn
                dxbc_ref[:, sl] = dx_ref[:, sl] * dyp + d_xdt * dtp
                ddtx_s[:, sl] = d_xdt * xp
            d_gb = d_g.astype(BF16)
            dxbc_ref[:, 1024 + g * LANE:1024 + (g + 1) * LANE] = d_b + _dot(d_gb, cgb, _TN)
            dxbc_ref[:, 1280 + g * LANE:1280 + (g + 1) * LANE] = d_c + _dot(d_gb, bgb)

        expt = expt_ref[...]
        dlast = _sel_dot(jnp.broadcast_to(dlast_s[...], (8, 1024)), expt, 3)
        d_cs = dcol_s[...] - drow_s[...].T + _sel_dot(dcsx_s[...], expt, 3)
        rown = lax.broadcasted_iota(jnp.int32, (L, LANE), 0)
        d_cs = d_cs + jnp.where(rown == L - 1, jnp.sum(dlast, axis=0, keepdims=True) * 0.125, 0.0)
        d_da = _sel_dot(d_cs, tri_ref[...], 3, _TN, sel_left=True)
        d_dt = d_da * a + _sel_dot(ddtx_s[...], expt, 3)
        dalog_ref[...] += _colsum(d_da * dt) * a
        d_raw = d_dt * _sigmoid(dt_ref[...] + bias_ref[...])
        ddt_ref[...] = d_raw.astype(ddt_ref.dtype)
        dbias_ref[...] += _colsum(d_raw)
        dd8 = _sel_dot(jnp.broadcast_to(dd_s[...], (8, 1024)), expt, 3)
        dd_ref[...] = jnp.sum(dd8, axis=0, keepdims=True) * 0.125

    const = lambda shape: pl.BlockSpec(shape, lambda c: tuple(0 for _ in shape))
    rev = lambda cb: (lambda c: (nc - 1 - c, cb))
    return pl.pallas_call(
        body, name=name, grid=(nc,),
        in_specs=[pl.BlockSpec((L, 1024), rev(0)), pl.BlockSpec((L, 256), rev(4)), pl.BlockSpec((L, 256), rev(5)),
                  pl.BlockSpec((L, LANE), rev(dtcb)), const((1, LANE)), const((1, LANE)), const((1, 1024)), const((L, L)),
                  const((LANE, 1024)), const((LANE, 2048)), const((1024, LANE)),
                  pl.BlockSpec((1, 8, LANE, LANE), lambda c: (nc - 1 - c, 0, 0, 0)), pl.BlockSpec((L, 1024), rev(0))],
        out_specs=[pl.BlockSpec((L, SSD_XBC), rev(0)), pl.BlockSpec((L, LANE), rev(0)), const((1, LANE)), const((1, LANE)),
                   const((1, LANE))],
        out_shape=[jax.ShapeDtypeStruct((s_dim, SSD_XBC), F32), jax.ShapeDtypeStruct((s_dim, LANE), BF16),
                   jax.ShapeDtypeStruct((1, LANE), F32), jax.ShapeDtypeStruct((1, LANE), F32),
                   jax.ShapeDtypeStruct((1, LANE), F32)],
        scratch_shapes=[pltpu.VMEM((8, LANE, LANE), F32), pltpu.VMEM((L, LANE), F32), pltpu.VMEM((LANE, L), F32),
                        pltpu.VMEM((L, 1024), F32), pltpu.VMEM((L, 2048), F32), pltpu.VMEM((L, 1024), F32),
                        pltpu.VMEM((L, 1024), F32), pltpu.VMEM((L, LANE), F32), pltpu.VMEM((LANE, L), F32),
                        pltpu.VMEM((1, 1024), F32), pltpu.VMEM((1, 1024), F32)],
        compiler_params=pltpu.CompilerParams(dimension_semantics=("arbitrary",)),
    )(xbca, xbca, xbca, dtr, bias, alog, d_x, tri, expand, expand128, expand_t, states, dy)


def _swap_halves(u):
    width = u.shape[1]
    lane = lax.broadcasted_iota(jnp.int32, u.shape, 1)
    return jnp.where(lane % MLA_ROPE < MLA_ROPE // 2, pltpu.roll(u, width - MLA_ROPE // 2, 1), pltpu.roll(u, MLA_ROPE // 2, 1))


def _rope_fwd_fn(u, cos, sin):
    return u * cos + _swap_halves(u) * sin


def _rope_bwd_fn(d, cos, sin):
    return d * cos + _swap_halves(d * sin)


def _spread4(v):
    return v + pltpu.roll(v, 32, 1) + pltpu.roll(v, 64, 1) + pltpu.roll(v, 96, 1)


def _att_masks(tq):
    lane = lax.broadcasted_iota(jnp.int32, (tq, LANE), 1)
    return lane // MLA_NOPE, lane // MLA_ROPE


def _att_tile(i, tq):
    klen = (i + 1) * tq
    qpos = i * tq + lax.broadcasted_iota(jnp.int32, (tq, klen), 0)
    kpos = lax.broadcasted_iota(jnp.int32, (tq, klen), 1)
    return slice(i * tq, (i + 1) * tq), klen, qpos >= kpos


def _att_qcat(qn_t, qr_t, par, e, half_id, grp_id):
    return jnp.concatenate([jnp.where(half_id == par, qn_t * ATT_SCALE, 0.0), jnp.where(grp_id == e, qr_t * ATT_SCALE, 0.0)],
                           axis=1).astype(BF16)


def _att_exp(qcat, kcat, causal):
    s = jnp.where(causal, _dot(qcat, kcat, _NT), -jnp.inf)
    e = jnp.exp(s - jnp.max(s, axis=1, keepdims=True))
    return e, 1.0 / jnp.sum(e, axis=1, keepdims=True)


def _att_specs(s_dim):
    col = lambda f: pl.BlockSpec((s_dim, LANE), lambda j: (0, f(j)))
    return [col(lambda j: j), col(lambda j: j // 2), col(lambda j: j), col(lambda j: 0), col(lambda j: 8 + j)]


def _att_fwd(q, qr, kv, krt, name="att_fwd"):
    s_dim = q.shape[0]
    tq = min(ATT_TQ, s_dim)

    def body(qn_ref, qr_ref, kn_ref, krt_ref, v_ref, o_ref, kcat_s, vb_s):
        e0 = 2 * (pl.program_id(0) % 2)
        half_id, grp_id = _att_masks(tq)
        kcat_s[...] = jnp.concatenate([kn_ref[...], krt_ref[...]], axis=1).astype(BF16)
        vb_s[...] = v_ref[...].astype(BF16)
        for i in range(s_dim // tq):
            rows, klen, causal = _att_tile(i, tq)
            qn_t, qr_t = qn_ref[rows, :], qr_ref[rows, :]
            outs = []
            for par in range(2):
                qcat = _att_qcat(qn_t, qr_t, par, e0 + par, half_id, grp_id)
                e, inv_l = _att_exp(qcat, kcat_s[0:klen, :], causal)
                outs.append(_dot(e.astype(BF16), vb_s[0:klen, :]) * inv_l)
            o_ref[rows, :] = jnp.where(half_id == 0, outs[0], outs[1])

    return pl.pallas_call(
        body, name=name, grid=(MLA_HEADS // 2,), in_specs=_att_specs(s_dim),
        out_specs=pl.BlockSpec((s_dim, LANE), lambda j: (0, j)), out_shape=jax.ShapeDtypeStruct((s_dim, 1024), F32),
        scratch_shapes=[pltpu.VMEM((s_dim, 2 * LANE), BF16), pltpu.VMEM((s_dim, LANE), BF16)],
        compiler_params=pltpu.CompilerParams(dimension_semantics=("parallel",)),
    )(q, qr, kv, krt, kv)


def _att_bwd(q, qr, kv, krt, o, do, name="att_bwd"):
    s_dim = q.shape[0]
    tq = min(ATT_TQ, s_dim)

    def body(qn_ref, qr_ref, kn_ref, krt_ref, v_ref, o_ref, do_ref, dqn_ref, dqr_ref, dkn_ref, dv_ref, dkrt_ref,
             kcat_s, vb_s):
        e0 = 2 * (pl.program_id(0) % 2)
        half_id, grp_id = _att_masks(tq)
        kcat_s[...] = jnp.concatenate([kn_ref[...], krt_ref[...]], axis=1).astype(BF16)
        vb_s[...] = v_ref[...].astype(BF16)
        dkn_ref[...] = jnp.zeros_like(dkn_ref)
        dv_ref[...] = jnp.zeros_like(dv_ref)
        dkrt_ref[...] = jnp.zeros_like(dkrt_ref)
        for i in range(s_dim // tq):
            rows, klen, causal = _att_tile(i, tq)
            qn_t, qr_t, o_t, do_t = qn_ref[rows, :], qr_ref[rows, :], o_ref[rows, :], do_ref[rows, :]
            dqn = jnp.zeros((tq, LANE), F32)
            dqr = jnp.zeros((tq, LANE), F32)
            for par in range(2):
                qcat = _att_qcat(qn_t, qr_t, par, e0 + par, half_id, grp_id)
                e, inv_l = _att_exp(qcat, kcat_s[0:klen, :], causal)
                p = e * inv_l
                dom = jnp.where(half_id == par, do_t, 0.0)
                domb = dom.astype(BF16)
                d_p = _dot(domb, vb_s[0:klen, :], _NT)
                d_row = jnp.sum(dom * o_t, axis=1, keepdims=True)
                d_s = (p * (d_p - d_row)).astype(BF16)
                dqcat = _dot(d_s, kcat_s[0:klen, :]) * ATT_SCALE
                dqn = dqn + jnp.where(half_id == par, dqcat[:, :LANE], 0.0)
                dqr = dqr + jnp.where(grp_id == e0 + par, dqcat[:, LANE:], 0.0)
                dkcat = _dot(d_s, qcat, _TN)
                dkn_ref[0:klen, :] += dkcat[:, :LANE]
                dkrt_ref[0:klen, :] += dkcat[:, LANE:]
                dv_ref[0:klen, :] += _dot(p.astype(BF16), domb, _TN)
            dqn_ref[rows, :] = dqn.astype(dqn_ref.dtype)
            dqr_ref[rows, :] = dqr

    col = lambda f: pl.BlockSpec((s_dim, LANE), lambda j: (0, f(j)))
    return pl.pallas_call(
        body, name=name, grid=(MLA_HEADS // 2,), in_specs=_att_specs(s_dim) + [col(lambda j: j), col(lambda j: j)],
        out_specs=[col(lambda j: j), pl.BlockSpec((None, s_dim, LANE), lambda j: (j % 2, 0, j // 2)), col(lambda j: j),
                   col(lambda j: j), pl.BlockSpec((None, s_dim, LANE), lambda j: (j, 0, 0))],
        out_shape=[jax.ShapeDtypeStruct((s_dim, 1024), BF16), jax.ShapeDtypeStruct((2, s_dim, 512), F32),
                   jax.ShapeDtypeStruct((s_dim, 1024), F32), jax.ShapeDtypeStruct((s_dim, 1024), F32),
                   jax.ShapeDtypeStruct((MLA_HEADS // 2, s_dim, LANE), F32)],
        scratch_shapes=[pltpu.VMEM((s_dim, 2 * LANE), BF16), pltpu.VMEM((s_dim, LANE), BF16)],
        compiler_params=pltpu.CompilerParams(dimension_semantics=("parallel",)),
    )(q, qr, kv, krt, kv, o, do)


def _gather_many(shards, name):
    n_arr = len(shards)

    def body(*refs):
        x_refs, out_refs = refs[:n_arr], refs[n_arr:2 * n_arr]
        send_sems, recv_sems, local_sems = refs[2 * n_arr:]
        x_i, y_i, c_i = lax.axis_index("x"), lax.axis_index("y"), lax.axis_index("c")
        me, sibling = (x_i, y_i, c_i), (x_i, y_i, 1 - c_i)
        chips = [(1 - x_i, y_i), (x_i, 1 - y_i), (1 - x_i, 1 - y_i)]

        def copy(a, k, block, to, src=None):
            slot = out_refs[a].at[4 * block[0] + 2 * block[1] + block[2]]
            return pltpu.make_async_remote_copy(
                src_ref=slot if src is None else src, dst_ref=slot, send_sem=send_sems.at[a, k],
                recv_sem=recv_sems.at[a, k], device_id=to, device_id_type=pl.DeviceIdType.MESH)

        mine, first, passed = [], [], []
        for a in range(n_arr):
            mine.append(pltpu.make_async_copy(x_refs[a], out_refs[a].at[4 * x_i + 2 * y_i + c_i], local_sems.at[a]))
            mine[a].start()
            first.append([copy(a, 0, me, sibling, src=x_refs[a])]
                         + [copy(a, 1 + j, me, (*chip, c_i), src=x_refs[a]) for j, chip in enumerate(chips)])
            for cp in first[a]:
                cp.start()
            passed.append([copy(a, 4 + j, (*chip, c_i), sibling) for j, chip in enumerate(chips)])
        for j, chip in enumerate(chips):
            for a in range(n_arr):
                copy(a, 1 + j, (*chip, c_i), me).wait_recv()
                passed[a][j].start()
        for a in range(n_arr):
            copy(a, 0, sibling, me).wait_recv()
            for j, chip in enumerate(chips):
                copy(a, 4 + j, (*chip, 1 - c_i), me).wait_recv()
        for a in range(n_arr):
            for cp in first[a] + passed[a]:
                cp.wait_send()
            mine[a].wait()

    any_spec = pl.BlockSpec(memory_space=pl.ANY)
    return pl.pallas_call(
        body, name=name, out_shape=[jax.ShapeDtypeStruct((N_DEV,) + x.shape, x.dtype) for x in shards],
        in_specs=[any_spec] * n_arr, out_specs=[any_spec] * n_arr,
        scratch_shapes=[pltpu.SemaphoreType.DMA((n_arr, 7)), pltpu.SemaphoreType.DMA((n_arr, 7)),
                        pltpu.SemaphoreType.DMA((n_arr,))],
    )(*shards)


_HBM = pl.BlockSpec(memory_space=pltpu.HBM)
_SEM = pl.BlockSpec(memory_space=pltpu.SEMAPHORE)


def _plan_copies(plan, src_refs, land_refs, send_sems, recv_sems):
    copies = []
    for s_ref, l_ref in zip(src_refs, land_refs):
        for src, dst, peer in plan(s_ref, l_ref):
            k = len(copies)
            copies.append(pltpu.make_async_remote_copy(
                src_ref=src, dst_ref=dst, send_sem=send_sems.at[k], recv_sem=recv_sems.at[k], device_id=peer,
                device_id_type=pl.DeviceIdType.MESH))
    return copies


def _split_start(srcs, lands, plan, n_copy, name, after=None):
    n = len(srcs)
    n_in = 2 * n + (after is not None)

    def body(*refs):
        for cp in _plan_copies(plan, refs[:n], refs[n:2 * n], refs[n_in], refs[n_in + 1]):
            cp.start()
        refs[-1][...] = jnp.zeros_like(refs[-1])

    sems = pltpu.SemaphoreType.DMA((n * n_copy,))
    res = pl.pallas_call(
        body, name=name,
        out_shape=(sems, sems, *[pltpu.HBM(a.shape, a.dtype) for a in list(srcs) + list(lands)],
                   jax.ShapeDtypeStruct((8, LANE), F32)),
        in_specs=[_HBM] * (2 * n) + [pl.BlockSpec(memory_space=pl.ANY)] * (after is not None),
        out_specs=(_SEM, _SEM, *[_HBM] * (2 * n), pl.BlockSpec(memory_space=pltpu.VMEM)),
        input_output_aliases={i: 2 + i for i in range(2 * n)},
        compiler_params=pltpu.CompilerParams(has_side_effects=pltpu.SideEffectType.DATAFLOW_SIDE_EFFECTING),
    )(*[pltpu.with_memory_space_constraint(a, pltpu.HBM) for a in list(srcs) + list(lands)],
      *([after] if after is not None else []))
    return res[0], res[1], list(res[2:2 + n]), list(res[2 + n:2 + 2 * n]), res[-1]


def _split_wait(send_sems, recv_sems, srcs, lands, after, plan, name):
    n = len(srcs)

    def body(*refs):
        copies = _plan_copies(plan, refs[:n], refs[n:2 * n], refs[2 * n], refs[2 * n + 1])
        for cp in copies:
            cp.wait_send()
        for cp in copies:
            cp.wait_recv()

    res = pl.pallas_call(
        body, name=name, out_shape=tuple(pltpu.HBM(a.shape, a.dtype) for a in list(srcs) + list(lands)),
        in_specs=[_HBM] * (2 * n) + [_SEM, _SEM, pl.BlockSpec(memory_space=pl.ANY)], out_specs=tuple([_HBM] * (2 * n)),
        input_output_aliases={i: i for i in range(2 * n)},
        compiler_params=pltpu.CompilerParams(has_side_effects=pltpu.SideEffectType.DATAFLOW_SIDE_EFFECTING),
    )(*srcs, *lands, send_sems, recv_sems, after)
    return list(res[:n]), list(res[n:])


def _plan_broadcast(src, land):
    x_i, y_i, c_i = lax.axis_index("x"), lax.axis_index("y"), lax.axis_index("c")
    me = 4 * x_i + 2 * y_i + c_i
    return [(src, land.at[me], (x_i ^ (k >> 2), y_i ^ ((k >> 1) & 1), c_i ^ (k & 1))) for k in range(1, N_DEV)]


def _plan_scatter(src, land):
    x_i, y_i, c_i = lax.axis_index("x"), lax.axis_index("y"), lax.axis_index("c")
    me = 4 * x_i + 2 * y_i + c_i
    plan = []
    for k in range(1, N_DEV):
        px, py, pc = x_i ^ (k >> 2), y_i ^ ((k >> 1) & 1), c_i ^ (k & 1)
        plan.append((src.at[4 * px + 2 * py + pc], land.at[me], (px, py, pc)))
    return plan


def _adam_math(g, w, m, v):
    m_new = ADAM_B1 * m + (1.0 - ADAM_B1) * g
    v_new = ADAM_B2 * v + (1.0 - ADAM_B2) * (g * g)
    m_hat = m_new / (1.0 - ADAM_B1 ** ADAM_STEP)
    v_hat = v_new / (1.0 - ADAM_B2 ** ADAM_STEP)
    return -ADAM_LR * (m_hat / (jnp.sqrt(v_hat) + ADAM_EPS) + ADAM_WD * w), m_new, v_new


def _adam(slots, w, m, v, name, own=None, own_idx=None):
    n_slot, rows, cols = slots.shape
    tr = ROW_TILE if rows % ROW_TILE == 0 else rows
    has_own = own is not None

    def body(*refs):
        if has_own:
            idx_ref, own_ref, refs = refs[0], refs[1], refs[2:]
        s_ref, w_ref, m_ref, v_ref, g_ref, d_ref, mo_ref, vo_ref = refs
        g = own_ref[...].astype(F32) if has_own else s_ref[0].astype(F32)
        for k in range(0 if has_own else 1, n_slot):
            part = s_ref[k].astype(F32)
            g = g + (jnp.where(idx_ref[0] == k, 0.0, part) if has_own else part)
        g_ref[...] = g
        d_ref[...], mo_ref[...], vo_ref[...] = _adam_math(g, w_ref[...], m_ref[...], v_ref[...])

    spec = pl.BlockSpec((tr, cols), lambda i, *_: (i, 0))
    in_specs = [pl.BlockSpec((n_slot, tr, cols), lambda i, *_: (0, i, 0)), spec, spec, spec]
    if has_own:
        in_specs = [pl.BlockSpec((None, tr, cols), lambda i, idx: (idx[0], i, 0))] + in_specs
    grid_spec = pltpu.PrefetchScalarGridSpec(num_scalar_prefetch=1 if has_own else 0, grid=(rows // tr,), in_specs=in_specs,
                                             out_specs=[spec] * 4)
    ins = ([own_idx, own] if has_own else []) + [slots, w, m, v]
    return pl.pallas_call(
        body, name=name, grid_spec=grid_spec, out_shape=[jax.ShapeDtypeStruct((rows, cols), F32)] * 4,
        compiler_params=pltpu.CompilerParams(dimension_semantics=("parallel",)),
    )(*ins)


PACK_ROWS, PACK_W = 24, 1536
REPL_W = (("ssd_conv_b", 1536), ("ssd_dt_bias", 16), ("ssd_A_log", 16), ("ssd_D", 16), ("ssd_norm_w", 1024),
          ("mla_q_norm_w", 384), ("mla_kv_norm_w", 256), ("mla_out_norm_w", 1024), ("ln_mix_g", 1024),
          ("ln_mix_b", 1024), ("ln_ffn_g", 1024), ("ln_ffn_b", 1024))
LOSS_ROW = 4 + len(REPL_W)


def _pack_small(conv_w_grad, grads, loss, name="pack_small"):
    def body(*refs):
        cw_ref, g_refs, loss_ref, o_ref = refs[0], refs[1:1 + len(REPL_W)], refs[1 + len(REPL_W)], refs[-1]
        o_ref[...] = jnp.zeros_like(o_ref)
        o_ref[0:4, :] = cw_ref[...]
        for i, g_ref in enumerate(g_refs):
            o_ref[4 + i:5 + i, 0:g_ref.shape[1]] = g_ref[...]
        o_ref[LOSS_ROW:LOSS_ROW + 1, 0:LANE] = loss_ref[...]

    return pl.pallas_call(body, name=name, out_shape=jax.ShapeDtypeStruct((PACK_ROWS, PACK_W), F32))(conv_w_grad, *grads, loss)


def _adam_small(gathered, wmv, name="adam_small"):
    def body(*refs):
        s_ref = refs[0]
        in_refs = refs[1:1 + 3 * len(REPL_W)]
        cw_ref, loss_ref = refs[1 + 3 * len(REPL_W)], refs[2 + 3 * len(REPL_W)]
        out_refs = refs[3 + 3 * len(REPL_W):-1]
        tot = refs[-1]
        acc = s_ref[0]
        for k in range(1, N_DEV):
            acc = acc + s_ref[k]
        tot[...] = acc
        cw_ref[...] = tot[0:4, :]
        loss_ref[...] = tot[LOSS_ROW:LOSS_ROW + 1, 0:LANE]
        for i, (_, width) in enumerate(REPL_W):
            g = tot[4 + i:5 + i, 0:width]
            w_ref, m_ref, v_ref = in_refs[3 * i:3 * i + 3]
            g_ref, d_ref, mo_ref, vo_ref = out_refs[4 * i:4 * i + 4]
            g_ref[...] = g
            d_ref[...], mo_ref[...], vo_ref[...] = _adam_math(g, w_ref[...], m_ref[...], v_ref[...])

    flat_in = [a for triple in wmv for a in triple]
    out_shape = [jax.ShapeDtypeStruct((4, PACK_W), F32), jax.ShapeDtypeStruct((1, LANE), F32)]
    for _, width in REPL_W:
        out_shape += [jax.ShapeDtypeStruct((1, width), F32)] * 4
    res = pl.pallas_call(body, name=name, out_shape=out_shape, scratch_shapes=[pltpu.VMEM((PACK_ROWS, PACK_W), F32)])(
        gathered, *flat_in)
    return res[0], res[1], [res[2 + 4 * i:6 + 4 * i] for i in range(len(REPL_W))]


def _cols_full(g):
    return jnp.transpose(g, (1, 0, 2)).reshape(g.shape[1], -1)


def _cols_split(full):
    k_dim, n_dim = full.shape
    return jnp.transpose(full.reshape(k_dim, N_DEV, n_dim // N_DEV), (1, 0, 2))


PROJ_BLOCK = {"z": (1024, 0), "dt": (LANE, 8), "q_c": (MLA_Q_RANK, 3), "xbc": (SSD_XBC, 1), "kv_c": (MLA_KV_RANK, 12),
              "k_rope": (LANE, 26)}


def _win_pad(wt):
    z = lambda n: jnp.zeros((n, wt.shape[1]), wt.dtype)
    return jnp.concatenate([wt[:1024], wt[2560:2576], z(112), wt[2576:2960], wt[1024:2560], wt[2960:3216], wt[3216:3248],
                            z(96)], axis=0)


def _win_unpad(wt):
    return jnp.concatenate([wt[:1024], wt[1536:3072], wt[1024:1040], wt[1152:1536], wt[3072:3328], wt[3328:3360]], axis=0)


def _heads_split_t(wt, a, b):
    w3 = wt.reshape(MLA_HEADS, a + b, wt.shape[1])
    return jnp.concatenate([w3[:, :a].reshape(-1, wt.shape[1]), w3[:, a:].reshape(-1, wt.shape[1])], axis=0)


def _heads_merge_t(wt, a, b):
    wa = wt[:MLA_HEADS * a].reshape(MLA_HEADS, a, wt.shape[1])
    wb = wt[MLA_HEADS * a:].reshape(MLA_HEADS, b, wt.shape[1])
    return jnp.concatenate([wa, wb], axis=1).reshape(-1, wt.shape[1])


def _heads_split(w, a, b):
    k_dim = w.shape[0]
    w3 = w.reshape(k_dim, MLA_HEADS, a + b)
    return jnp.concatenate([w3[:, :, :a].reshape(k_dim, -1), w3[:, :, a:].reshape(k_dim, -1)], axis=1)


def _heads_merge(w, a, b):
    k_dim = w.shape[0]
    wa = w[:, :MLA_HEADS * a].reshape(k_dim, MLA_HEADS, a)
    wb = w[:, MLA_HEADS * a:].reshape(k_dim, MLA_HEADS, b)
    return jnp.concatenate([wa, wb], axis=2).reshape(k_dim, -1)


def _pad_lanes(v, width=LANE):
    return jnp.concatenate([v, jnp.zeros((v.shape[0], width - v.shape[1]), v.dtype)], axis=1)


def _local_step(x, p, positions, tgt, W, P, comm=None):
    comm = comm or {}
    zero_tok = jnp.zeros((8, LANE), F32)
    s_dim = x.shape[0]
    inv_freq = 1.0 / (ROPE_BASE ** (jnp.arange(0, MLA_ROPE, 2, dtype=F32) / MLA_ROPE))
    ang = positions.astype(F32)[:, None] * inv_freq
    cos, sin = jnp.cos(ang), jnp.sin(ang)
    cos32 = jnp.concatenate([cos, cos], axis=1)
    sin32 = jnp.concatenate([-sin, sin], axis=1)
    cos512, sin512 = jnp.tile(cos32, (1, 16)), jnp.tile(sin32, (1, 16))
    cos128, sin128 = jnp.tile(cos32, (1, 4)), jnp.tile(sin32, (1, 4))
    bias_p, alog_p = _pad_lanes(P["ssd_dt_bias"]), _pad_lanes(P["ssd_A_log"])
    d_x = jnp.repeat(P["ssd_D"], SSD_HEAD_DIM, axis=1)

    xb, pb = x.astype(BF16), p.astype(BF16)
    proj = _mm(xb, W["w_in"], tb=True, after=comm.get("token0", zero_tok), name="mm_in")
    z, qc, kvc, kr = [(proj,) + PROJ_BLOCK[n] for n in ("z", "q_c", "kv_c", "k_rope")]
    xbca = _conv_fwd(proj, PROJ_BLOCK["xbc"][1], P["ssd_conv_w"], P["ssd_conv_b"])
    y, states = _ssd_fwd(xbca, proj, PROJ_BLOCK["dt"][1], bias_p, alog_p, d_x)
    (yssd,) = _rowwise(_gate_rms, [y, z], [P["ssd_norm_w"]], [(1024, BF16)], name="ssd_gate_norm")
    qn, kvn, krt = _rowwise(lambda a, c, u, cs, sn, wq, wkv: (_rms(a, wq), _rms(c, wkv), _spread4(_rope_fwd_fn(u, cs, sn))),
                            [qc, kvc, kr, cos128, sin128], [P["mla_q_norm_w"], P["mla_kv_norm_w"]],
                            [(MLA_Q_RANK, BF16), (MLA_KV_RANK, BF16), LANE], name="qkv_norm_rope_k")
    q = _mm(qn, W["mla_w_q_b"], tb=True, name="mm_q")
    kv = _mm(kvn, W["mla_w_kv_b"], name="mm_kv")
    (qr,) = _rowwise(_rope_fwd_fn, [(q, 512, 2), cos512, sin512], [], [512], name="rope_q")
    att = _att_fwd(q, qr, kv, krt)
    (ymla,) = _rowwise(_rms, [att], [P["mla_out_norm_w"]], [(1024, BF16)], name="out_norm")
    ycat = jnp.concatenate([yssd, ymla], axis=1)
    if "late_weights" in comm:
        W = {**W, **comm["late_weights"]("out", ycat)}
    mix = _mm(ycat, W["w_out"], name="mm_out")
    f_h1 = lambda xv, mv, g, b: _ln(ALPHA * xv + mv, g, b)
    h1, h1b = _rowwise(lambda *a: (f_h1(*a),) * 2, [x, mix], [P["ln_mix_g"], P["ln_mix_b"]], [1024, (1024, BF16)],
                       name="ln_mix")
    if "late_weights" in comm:
        W = {**W, **comm["late_weights"]("ffn", h1b)}
    hg = _mm(h1b, W["w_ffn_gate"], tb=True, out_dtype=BF16, name="mm_gate")
    hu, act = _mm(h1b, W["w_ffn_up"], tb=True, name="mm_up",
                  epilogue=(lambda u, g: (u, _silu(g.astype(F32)) * u), [hg], [BF16, BF16]))
    pg = _mm(h1b, W["w_ple_gate"], name="mm_ple_gate")
    pp = _mm(pb, W["w_ple_proj"], name="mm_ple")
    ffn = _mm(act, W["w_ffn_down"], name="mm_down")

    f_h2 = lambda hv, fv, pg, ppv, g, b: _ln(ALPHA * hv + fv + _sigmoid(pg) * ppv, g, b)

    def final_fn(hv, fv, pg, ppv, tv, g, b):
        h2, pull = jax.vjp(f_h2, hv, fv, pg, ppv, g, b)
        diff = h2 - tv
        loss = 0.5 * jnp.sum(jnp.mean(diff * diff, axis=-1, keepdims=True), axis=0, keepdims=True)
        d_h, d_f, d_pg, d_pp, d_g, d_b = pull(diff * (1.0 / D_MODEL))
        return d_h, d_f, d_pg, d_pp, d_g, d_b, jnp.broadcast_to(loss, (1, LANE))

    dh1_a, dffn, dpg, dpp, g_ffn_g, g_ffn_b, loss = _rowwise(
        final_fn, [h1, ffn, pg, pp, tgt], [P["ln_ffn_g"], P["ln_ffn_b"]], [1024] + [(1024, BF16)] * 3,
        [1024, 1024, LANE], name="final")

    G = {}
    def swiglu_bwd(d, g, u):
        g, u = g.astype(F32), u.astype(F32)
        sg = _sigmoid(g)
        return d * u * (sg * (1.0 + g * (1.0 - sg))), d * (g * sg)

    dg, du = _mm(dffn, W["w_ffn_down"], tb=True, name="mm_down_dx",
                 epilogue=(swiglu_bwd, [hg, hu], [BF16, BF16]))
    G["w_ffn_down"] = _mm(act, dffn, ta=True, out_dtype=GRAD_DT, name="mm_down_dw")
    dh1 = _mm(dg, W["w_ffn_gate"], add=dh1_a, name="mm_gate_dx")
    dh1 = _mm(du, W["w_ffn_up"], add=dh1, name="mm_up_dx")
    dh1 = _mm(dpg, W["w_ple_gate"], tb=True, add=dh1, name="mm_ple_gate_dx")
    G["w_ffn_gate"] = _mm(dg, h1b, ta=True, out_dtype=GRAD_DT, name="mm_gate_dw")
    G["w_ffn_up"] = _mm(du, h1b, ta=True, out_dtype=GRAD_DT, name="mm_up_dw")
    G["w_ple_gate"] = _mm(h1b, dpg, ta=True, out_dtype=GRAD_DT, name="mm_ple_gate_dw")
    G["w_ple_proj"] = _mm(pb, dpp, ta=True, out_dtype=GRAD_DT, name="mm_ple_dw")
    dx_a, dmix, g_mix_g, g_mix_b = _rowwise(
        lambda xv, mv, dv, g, b: _vjp_rows(f_h1)(xv, mv, g, b, dv), [x, mix, dh1], [P["ln_mix_g"], P["ln_mix_b"]],
        [1024, (1024, BF16)], [1024, 1024], name="ln_mix_bwd")
    dycat = _mm(dmix, W["w_out"], tb=True, name="mm_out_dx")
    G["w_out"] = _mm(ycat, dmix, ta=True, out_dtype=GRAD_DT, name="mm_out_dw")

    grads_done = comm.get("grads", lambda group, grads: zero_tok)
    tok1 = grads_done("ffn", G)
    datt, g_out_norm = _rowwise(lambda a, dv, w, t: _vjp_rows(_rms)(a, w, dv + jnp.min(t)), [att, (dycat, 1024, 1)],
                                [P["mla_out_norm_w"], tok1], [1024], [1024], name="out_norm_bwd")
    dqn_nope, dqr, dkn, dv, dkrt = _att_bwd(q, qr, kv, krt, att, datt)
    dkv = jnp.concatenate([dkn, dv], axis=1)
    (dq_rope,) = _rowwise(lambda d0, d1, c, s: _rope_bwd_fn(d0 + d1, c, s), [(dqr, 512, 0), (dqr, 512, 1), cos512, sin512],
                          [], [(512, BF16)], name="rope_q_bwd")

    def rope_k_bwd(*a):
        d = _spread4(functools.reduce(lambda u, w: u + w, a[:-2]))
        lane = lax.broadcasted_iota(jnp.int32, d.shape, 1)
        return _rope_bwd_fn(jnp.where(lane < MLA_ROPE, d, 0.0), a[-2], a[-1])

    (dkr,) = _rowwise(rope_k_bwd, [(dkrt, LANE, k) for k in range(MLA_HEADS // 2)] + [cos128, sin128], [], [(LANE, BF16)],
                      name="rope_k_bwd")
    dq = jnp.concatenate([dqn_nope, dq_rope], axis=1)
    dqn = _mm(dq, W["mla_w_q_b"], name="mm_q_dx")
    G["mla_w_q_b"] = _mm(dq, qn, ta=True, out_dtype=GRAD_DT, name="mm_q_dw")
    dkvn = _mm(dkv, W["mla_w_kv_b"], tb=True, name="mm_kv_dx")
    G["mla_w_kv_b"] = _mm(kvn, dkv, ta=True, out_dtype=GRAD_DT, name="mm_kv_dw")
    tok2 = grads_done("mla", G)
    def qkv_norm_bwd(a, da, c, dc, wq, wkv, t):
        (d_a, d_wq), (d_c, d_wkv) = _vjp_rows(_rms)(a, wq, da + jnp.min(t)), _vjp_rows(_rms)(c, wkv, dc)
        return d_a, d_c, d_wq, d_wkv

    dqc, dkvc, g_q_norm, g_kv_norm = _rowwise(
        qkv_norm_bwd, [qc, dqn, kvc, dkvn], [P["mla_q_norm_w"], P["mla_kv_norm_w"], tok2],
        [(MLA_Q_RANK, BF16), (MLA_KV_RANK, BF16)], [MLA_Q_RANK, MLA_KV_RANK], name="qkv_norm_bwd")

    dy, dz, g_ssd_norm = _rowwise(lambda yv, zv, dv, w, t: _vjp_rows(_gate_rms)(yv, zv, w, dv + jnp.min(t)),
                                  [y, z, (dycat, 1024, 0)], [P["ssd_norm_w"], tok1], [1024, (1024, BF16)], [1024],
                                  name="ssd_gate_norm_bwd")
    dxbca, ddtr, g_dt_bias, g_alog, g_d = _ssd_bwd(xbca, proj, PROJ_BLOCK["dt"][1], bias_p, alog_p, d_x, states, dy)
    da, g_conv_w, g_conv_b = _conv_bwd_pre(proj, PROJ_BLOCK["xbc"][1], P["ssd_conv_w"], P["ssd_conv_b"], dxbca)
    dxbc = _conv_bwd_in(da, P["ssd_conv_w"])

    small = {
        "ssd_conv_b": g_conv_b, "ssd_dt_bias": g_dt_bias, "ssd_A_log": g_alog, "ssd_D": g_d, "ssd_norm_w": g_ssd_norm,
        "mla_q_norm_w": g_q_norm, "mla_kv_norm_w": g_kv_norm, "mla_out_norm_w": g_out_norm, "ln_mix_g": g_mix_g,
        "ln_mix_b": g_mix_b, "ln_ffn_g": g_ffn_g, "ln_ffn_b": g_ffn_b,
    }
    packed = _pack_small(g_conv_w, [small[n] for n, _ in REPL_W], loss)
    if "small" in comm:
        comm["small"](packed)

    dproj = jnp.concatenate([dz, ddtr, dqc, dxbc, dkvc, dkr], axis=1)
    G["w_in"] = _mm(dproj, xb, ta=True, out_dtype=GRAD_DT, name="mm_in_dw")
    grad_x = _mm(dproj, W["w_in"], add=dx_a, after=grads_done("in", G), name="mm_in_dx")
    return grad_x, G, packed


def kernel(x, p, positions, w_in, ssd_conv_w, ssd_conv_b, ssd_dt_bias, ssd_A_log, ssd_D, ssd_norm_w, mla_q_norm_w, mla_w_q_b, mla_kv_norm_w, mla_w_kv_b, mla_out_norm_w, w_out, ln_mix_g, ln_mix_b, w_ffn_gate, w_ffn_up, w_ffn_down, w_ple_gate, w_ple_proj, ln_ffn_g, ln_ffn_b, loss_target, m_w_in, m_ssd_conv_w, m_ssd_conv_b, m_ssd_dt_bias, m_ssd_A_log, m_ssd_D, m_ssd_norm_w, m_mla_q_norm_w, m_mla_w_q_b, m_mla_kv_norm_w, m_mla_w_kv_b, m_mla_out_norm_w, m_w_out, m_ln_mix_g, m_ln_mix_b, m_w_ffn_gate, m_w_ffn_up, m_w_ffn_down, m_w_ple_gate, m_w_ple_proj, m_ln_ffn_g, m_ln_ffn_b, v_w_in, v_ssd_conv_w, v_ssd_conv_b, v_ssd_dt_bias, v_ssd_A_log, v_ssd_D, v_ssd_norm_w, v_mla_q_norm_w, v_mla_w_q_b, v_mla_kv_norm_w, v_mla_w_kv_b, v_mla_out_norm_w, v_w_out, v_ln_mix_g, v_ln_mix_b, v_w_ffn_gate, v_w_ffn_up, v_w_ffn_down, v_w_ple_gate, v_w_ple_proj, v_ln_ffn_g, v_ln_ffn_b):
    args = dict(locals())
    core = lax.axis_index("c")
    me = 4 * lax.axis_index("x") + 2 * lax.axis_index("y") + core

    conv_sh = ssd_conv_w[0]
    conv_hi = conv_sh.astype(BF16)
    conv_lo = (conv_sh - conv_hi.astype(F32)).astype(BF16)
    stored = lambda n, pre="": jnp.transpose(args[pre + n][0]) if n in TRANSPOSED else args[pre + n][0]
    shards = {n: stored(n).astype(BF16) for n in BIG}
    rows_full = lambda g: g.reshape(-1, g.shape[2])

    early = _gather_many([shards[n] for n in EARLY] + [jnp.concatenate([conv_hi, conv_lo], axis=0)], "gather_early")
    gw = dict(zip(EARLY, early[:-1]))
    conv_g = early[-1].astype(F32)
    W = {
        "w_in": _win_pad(rows_full(gw["w_in"])),
        "mla_w_q_b": _heads_split_t(rows_full(gw["mla_w_q_b"]), MLA_NOPE, MLA_ROPE),
        "mla_w_kv_b": _heads_split(_cols_full(gw["mla_w_kv_b"]), MLA_NOPE, MLA_V),
    }
    P = {n: args[n] for n, _ in REPL_W}
    P["ssd_conv_w"] = _cols_full(conv_g[:, :4] + conv_g[:, 4:])

    late, after = {}, early[0]
    for group, names in LATE.items():
        lands = [lax.dynamic_update_slice(lax.empty((N_DEV,) + shards[n].shape, BF16), shards[n][None], (me, 0, 0)) for n in names]
        late[group] = _split_start([shards[n] for n in names], lands, _plan_broadcast, N_DEV - 1,
                                   "gather_" + group + "_start", after=after)
        after = late[group][4]

    def late_weights(group, after):
        _, got = _split_wait(*late[group][:4], after, _plan_broadcast, "gather_" + group + "_wait")
        return {n: _cols_full(g) if n == "w_ple_proj" else rows_full(g) for n, g in zip(LATE[group], got)}

    def to_blocks(n, g):
        if n == "w_in":
            g = _win_unpad(g)
        elif n == "mla_w_q_b":
            g = _heads_merge_t(g, MLA_NOPE, MLA_ROPE)
        elif n == "mla_w_kv_b":
            g = _heads_merge(g, MLA_NOPE, MLA_V)
        if n in ROW_SHARDED or n in TRANSPOSED:
            return g.reshape(N_DEV, -1, g.shape[1])
        return _cols_split(g)

    flight = {}

    def grads(group, G):
        gl = [to_blocks(n, G[n]) for n in GRAD_GROUPS[group]]
        flight[group] = _split_start(gl, [lax.empty(g.shape, g.dtype) for g in gl], _plan_scatter, N_DEV - 1,
                                     "grads_" + group + "_start", after=flight["small"][4] if group == "in" else None)
        return flight[group][4]

    def small(packed):
        land = lax.dynamic_update_slice(lax.empty((N_DEV,) + packed.shape, F32), packed[None], (me, 0, 0))
        flight["small"] = _split_start([packed], [land], _plan_broadcast, N_DEV - 1, "small_start")

    grad_x, G, packed = _local_step(x[0], p[0, 0], positions[0], loss_target[0], W, P,
                                    comm={"token0": after, "late_weights": late_weights, "grads": grads, "small": small})

    me_arr = me.astype(jnp.int32).reshape(1)
    big_out = {}

    def finish(group, after):
        mine, recv = _split_wait(*flight[group][:4], after, _plan_scatter, "grads_" + group + "_wait")
        for n, g, r in zip(GRAD_GROUPS[group], mine, recv):
            big_out[n] = _adam(r, stored(n), stored(n, "m_"), stored(n, "v_"), "adam_" + n, own=g, own_idx=me_arr)
        return big_out[GRAD_GROUPS[group][-1]][0]

    done = finish("ffn", grad_x)
    _, (small_all,) = _split_wait(*flight["small"][:4], done, _plan_broadcast, "small_wait")
    conv_sum, loss_row, small_out = _adam_small(small_all, [(args[n], args["m_" + n], args["v_" + n]) for n, _ in REPL_W])
    finish("in", finish("mla", done))
    conv_grad = lax.dynamic_slice_in_dim(conv_sum, me * 192, 192, axis=1)
    conv_out = _adam(conv_grad[None], conv_sh, m_ssd_conv_w[0], v_ssd_conv_w[0], "adam_conv")
    small_map = {n: small_out[i] for i, (n, _) in enumerate(REPL_W)}

    def outputs(idx):
        res = []
        for n in WEIGHT_ORDER:
            if n == "ssd_conv_w":
                res.append(conv_out[idx][None])
            elif n in big_out:
                res.append((jnp.transpose(big_out[n][idx]) if n in TRANSPOSED else big_out[n][idx])[None])
            else:
                res.append(small_map[n][idx])
        return res

    return (loss_row[0, 0], grad_x[None], *outputs(0), *outputs(1), *outputs(2), *outputs(3))
```

```python
import functools
import math

import numpy as np
import jax
import jax.numpy as jnp
from jax import lax
from jax.experimental import pallas as pl
from jax.experimental.pallas import tpu as pltpu

F32 = jnp.float32
BF16 = jnp.bfloat16
HI = lax.Precision.HIGHEST

N_DEV = 8
D_MODEL = 1024
PLE_DIM = 256
SSD_HEADS = 16
SSD_HEAD_DIM = 64
SSD_INNER = 1024
SSD_STATE = 128
SSD_XBC = 1536
SSD_CHUNK = 128
MLA_HEADS = 16
MLA_Q_RANK = 384
MLA_KV_RANK = 256
MLA_NOPE = 64
MLA_ROPE = 32
MLA_V = 64
ROPE_BASE = 10000.0
D_FF = 2816
IN_WIDTH = 3248
IN_PAD = 3456
ALPHA = 2.0 ** 0.25
EPS = 1e-6
LN_EPS = 1e-5
ATT_SCALE = 1.0 / math.sqrt(MLA_NOPE + MLA_ROPE)
ADAM_LR, ADAM_B1, ADAM_B2, ADAM_EPS, ADAM_WD, ADAM_STEP = 0.001, 0.9, 0.999, 1e-08, 0.01, 10

LANE = 128
MXU_DIM = 256
MM_TM, MM_TN, MM_TK = 1408, 1408, 2048
ROW_TILE = 256
ATT_TQ = 256

GRAD_DT = BF16

BIG = ("w_in", "mla_w_q_b", "mla_w_kv_b", "w_out", "w_ffn_gate", "w_ffn_up", "w_ffn_down", "w_ple_gate", "w_ple_proj")
EARLY = ("w_in",)
LATE = {"mla": ("mla_w_q_b", "mla_w_kv_b"), "out": ("w_out", "w_ple_gate", "w_ple_proj"),
        "ffn": ("w_ffn_gate", "w_ffn_up", "w_ffn_down")}
GRAD_GROUPS = {"ffn": ("w_ffn_gate", "w_ffn_up", "w_ffn_down", "w_ple_gate", "w_ple_proj", "w_out"),
               "mla": ("mla_w_q_b", "mla_w_kv_b"), "in": ("w_in",)}
ROW_SHARDED = ("w_out", "w_ffn_down", "w_ple_gate")
TRANSPOSED = ("w_in", "mla_w_q_b", "w_ffn_gate", "w_ffn_up")
WEIGHT_ORDER = ("w_in", "ssd_conv_w", "ssd_conv_b", "ssd_dt_bias", "ssd_A_log", "ssd_D", "ssd_norm_w", "mla_q_norm_w",
                "mla_w_q_b", "mla_kv_norm_w", "mla_w_kv_b", "mla_out_norm_w", "w_out", "ln_mix_g", "ln_mix_b",
                "w_ffn_gate", "w_ffn_up", "w_ffn_down", "w_ple_gate", "w_ple_proj", "ln_ffn_g", "ln_ffn_b")


def _tile(dim, cap, prefer=None):
    cands = [t for t in range(LANE, min(cap, dim) + 1, LANE) if dim % t == 0]
    if not cands:
        return dim
    if prefer is None:
        return max(cands)
    fill = lambda t: t / (MXU_DIM * -(-t // MXU_DIM))
    good = min(0.9, max(fill(t) for t in cands))
    return min((t for t in cands if fill(t) >= good), key=lambda t: abs(t - prefer))


def _dot(a, b, dims=(((1,), (0,)), ((), ())), precision=None):
    return lax.dot_general(a, b, dims, preferred_element_type=F32, precision=precision)


_NT = (((1,), (1,)), ((), ()))
_TN = (((0,), (0,)), ((), ()))


def _mm(a, b, *, ta=False, tb=False, add=None, out_dtype=F32, after=None, epilogue=None, name):
    k_dim, m_dim = a.shape if ta else a.shape[::-1]
    n_dim, kb = b.shape if tb else b.shape[::-1]
    assert k_dim == kb
    tm, tn, tk = _tile(m_dim, MM_TM), _tile(n_dim, MM_TN, prefer=1024), _tile(k_dim, MM_TK, prefer=MM_TK)
    nk = k_dim // tk
    dims = (((0 if ta else 1,), (1 if tb else 0,)), ((), ()))
    a_spec = pl.BlockSpec((tk, tm), lambda i, j, k: (k, i)) if ta else pl.BlockSpec((tm, tk), lambda i, j, k: (i, k))
    b_spec = pl.BlockSpec((tn, tk), lambda i, j, k: (j, k)) if tb else pl.BlockSpec((tk, tn), lambda i, j, k: (k, j))
    o_spec = pl.BlockSpec((tm, tn), lambda i, j, k: (i, j))
    epi_fn, epi_in, out_dtypes = epilogue if epilogue else (None, [], [out_dtype])
    tiles = ([add] if add is not None else []) + list(epi_in)
    n_out = len(out_dtypes)

    def body(*refs):
        a_ref, b_ref = refs[:2]
        tile_refs = refs[2:2 + len(tiles)]
        out_refs = refs[len(refs) - n_out - (nk > 1):len(refs) - (nk > 1)]
        part = _dot(a_ref[...].astype(BF16), b_ref[...].astype(BF16), dims)
        if add is not None:
            part_add = lambda v: v + tile_refs[0][...]
        else:
            part_add = lambda v: v

        def write(total):
            extra = [r[...] for r in tile_refs[add is not None:]]
            outs = epi_fn(total, *extra) if epi_fn else (total,)
            for o_ref, val in zip(out_refs, outs):
                o_ref[...] = val.astype(o_ref.dtype)

        if nk == 1:
            write(part_add(part))
            return
        acc = refs[-1]
        k = pl.program_id(2)

        @pl.when(k == 0)
        def _():
            acc[...] = part_add(part)

        @pl.when(k > 0)
        def _():
            acc[...] += part

        @pl.when(k == nk - 1)
        def _():
            write(acc[...])

    ins = [a, b] + tiles + ([after] if after is not None else [])
    specs = [a_spec, b_spec] + [o_spec] * len(tiles) + ([pl.BlockSpec(memory_space=pl.ANY)] if after is not None else [])
    res = pl.pallas_call(
        body, name=name, grid=(m_dim // tm, n_dim // tn, nk), in_specs=specs, out_specs=[o_spec] * n_out,
        out_shape=[jax.ShapeDtypeStruct((m_dim, n_dim), dt) for dt in out_dtypes],
        scratch_shapes=[pltpu.VMEM((tm, tn), F32)] if nk > 1 else [],
        compiler_params=pltpu.CompilerParams(dimension_semantics=("parallel", "parallel", "arbitrary")),
    )(*ins)
    return res if epilogue else res[0]


def _rowwise(fn, rows, consts, out_widths, acc_widths=(), *, name, tr=ROW_TILE):
    row_arrays, row_specs = [], []
    first_arr = rows[0][0] if isinstance(rows[0], tuple) else rows[0]
    s_dim = first_arr.shape[-2]
    tr = min(tr, s_dim)
    for r in rows:
        arr, width, cb = r if isinstance(r, tuple) else (r, r.shape[-1], 0)
        row_arrays.append(arr)
        if arr.ndim == 3:
            row_specs.append(pl.BlockSpec((None, tr, width), functools.partial(lambda i, k: (k, i, 0), k=cb)))
        else:
            row_specs.append(pl.BlockSpec((tr, width), functools.partial(lambda i, cb: (i, cb), cb=cb)))
    const_specs = [pl.BlockSpec(c.shape, lambda i: (0, 0)) for c in consts]
    nr, nc, no, na = len(rows), len(consts), len(out_widths), len(acc_widths)

    def body(*refs):
        ins = [r[...] for r in refs[:nr + nc]]
        res = fn(*ins)
        if not isinstance(res, (tuple, list)):
            res = (res,)
        out_refs = refs[nr + nc:nr + nc + no]
        acc_refs = refs[nr + nc + no:]
        for o_ref, val in zip(out_refs, res[:no]):
            o_ref[...] = val.astype(o_ref.dtype)
        first = pl.program_id(0) == 0
        for a_ref, val in zip(acc_refs, res[no:]):
            @pl.when(first)
            def _(a_ref=a_ref, val=val):
                a_ref[...] = val

            @pl.when(jnp.logical_not(first))
            def _(a_ref=a_ref, val=val):
                a_ref[...] += val

    outs = [w if isinstance(w, tuple) else (w, F32) for w in out_widths]
    out_shape = [jax.ShapeDtypeStruct((s_dim, w), dt) for w, dt in outs]
    out_shape += [jax.ShapeDtypeStruct((1, w), F32) for w in acc_widths]
    out_specs = [pl.BlockSpec((tr, w), lambda i: (i, 0)) for w, _ in outs]
    out_specs += [pl.BlockSpec((1, w), lambda i: (0, 0)) for w in acc_widths]
    res = pl.pallas_call(
        body, name=name, grid=(s_dim // tr,), in_specs=row_specs + const_specs, out_specs=out_specs, out_shape=out_shape,
        compiler_params=pltpu.CompilerParams(dimension_semantics=("arbitrary",)),
    )(*row_arrays, *consts)
    return res


def _colsum(v):
    return jnp.sum(v, axis=0, keepdims=True)


def _rms(u, g):
    return u * lax.rsqrt(jnp.mean(u * u, axis=-1, keepdims=True) + EPS) * g


def _ln(u, g, b):
    mu = jnp.mean(u, axis=-1, keepdims=True)
    d = u - mu
    var = jnp.mean(d * d, axis=-1, keepdims=True)
    return d * lax.rsqrt(var + LN_EPS) * g + b


def _sigmoid(v):
    return 1.0 / (1.0 + jnp.exp(-v))


def _silu(v):
    return v * _sigmoid(v)


def _softplus(v):
    y = jnp.exp(-jnp.abs(v))
    w = 1.0 + y
    log1p = jnp.where(w == 1.0, y, jnp.log(w) * y / jnp.where(w == 1.0, 1.0, w - 1.0))
    return jnp.maximum(v, 0.0) + log1p


def _gate_rms(y, z, w):
    return _rms(y * _silu(z), w)


def _vjp_rows(f):
    def fn(*args):
        prim, ct = args[:-1], args[-1]
        _, pull = jax.vjp(f, *prim)
        return pull(ct)
    return fn


def _conv_pre(cur, prev, w, b, first):
    row = lax.broadcasted_iota(jnp.int32, cur.shape, 0)
    acc = cur * w[3:4, :] + b
    for j in (1, 2, 3):
        tail = jnp.where(first, 0.0, pltpu.roll(prev, j, 0))
        acc = acc + jnp.where(row >= j, pltpu.roll(cur, j, 0), tail) * w[3 - j:4 - j, :]
    return acc


def _conv_fwd(u, ucb, w, b, name="conv_fwd"):
    s_dim, width = u.shape[0], w.shape[1]
    tr = min(ROW_TILE, s_dim)

    def body(cur_ref, prev_ref, w_ref, b_ref, o_ref):
        pre = _conv_pre(cur_ref[...], prev_ref[...], w_ref, b_ref[...], pl.program_id(0) == 0)
        o_ref[...] = _silu(pre)

    return pl.pallas_call(
        body, name=name, grid=(s_dim // tr,),
        in_specs=[pl.BlockSpec((tr, width), lambda i: (i, ucb)),
                  pl.BlockSpec((tr, width), lambda i: (jnp.maximum(i - 1, 0), ucb)),
                  pl.BlockSpec(w.shape, lambda i: (0, 0)), pl.BlockSpec(b.shape, lambda i: (0, 0))],
        out_specs=pl.BlockSpec((tr, width), lambda i: (i, 0)), out_shape=jax.ShapeDtypeStruct((s_dim, width), F32),
        compiler_params=pltpu.CompilerParams(dimension_semantics=("arbitrary",)),
    )(u, u, w, b)


def _conv_bwd_pre(u, ucb, w, b, dact, name="conv_bwd_pre"):
    s_dim, width = u.shape[0], w.shape[1]
    tr = min(ROW_TILE, s_dim)

    def body(cur_ref, prev_ref, w_ref, b_ref, d_ref, da_ref, dw_ref, db_ref):
        first = pl.program_id(0) == 0
        cur, prev = cur_ref[...], prev_ref[...]
        pre = _conv_pre(cur, prev, w_ref, b_ref[...], first)
        sg = _sigmoid(pre)
        da = d_ref[...] * (sg * (1.0 + pre * (1.0 - sg)))
        da_ref[...] = da
        row = lax.broadcasted_iota(jnp.int32, cur.shape, 0)

        @pl.when(first)
        def _():
            dw_ref[...] = jnp.zeros_like(dw_ref)
            db_ref[...] = jnp.zeros_like(db_ref)

        db_ref[...] += _colsum(da)
        dw_ref[3:4, :] += _colsum(da * cur)
        for j in (1, 2, 3):
            tail = jnp.where(first, 0.0, pltpu.roll(prev, j, 0))
            sh = jnp.where(row >= j, pltpu.roll(cur, j, 0), tail)
            dw_ref[3 - j:4 - j, :] += _colsum(da * sh)

    return pl.pallas_call(
        body, name=name, grid=(s_dim // tr,),
        in_specs=[pl.BlockSpec((tr, width), lambda i: (i, ucb)),
                  pl.BlockSpec((tr, width), lambda i: (jnp.maximum(i - 1, 0), ucb)),
                  pl.BlockSpec(w.shape, lambda i: (0, 0)), pl.BlockSpec(b.shape, lambda i: (0, 0)),
                  pl.BlockSpec((tr, width), lambda i: (i, 0))],
        out_specs=[pl.BlockSpec((tr, width), lambda i: (i, 0)), pl.BlockSpec(w.shape, lambda i: (0, 0)),
                   pl.BlockSpec(b.shape, lambda i: (0, 0))],
        out_shape=[jax.ShapeDtypeStruct((s_dim, width), F32), jax.ShapeDtypeStruct(w.shape, F32),
                   jax.ShapeDtypeStruct(b.shape, F32)],
        compiler_params=pltpu.CompilerParams(dimension_semantics=("arbitrary",)),
    )(u, u, w, b, dact)


def _conv_bwd_in(da, w, name="conv_bwd_in"):
    s_dim, width = da.shape
    tr = min(ROW_TILE, s_dim)
    n = s_dim // tr

    def body(cur_ref, nxt_ref, w_ref, o_ref):
        last = pl.program_id(0) == n - 1
        cur, nxt = cur_ref[...], nxt_ref[...]
        row = lax.broadcasted_iota(jnp.int32, cur.shape, 0)
        acc = cur * w_ref[3:4, :]
        for j in (1, 2, 3):
            head = jnp.where(last, 0.0, pltpu.roll(nxt, tr - j, 0))
            acc = acc + jnp.where(row < tr - j, pltpu.roll(cur, tr - j, 0), head) * w_ref[3 - j:4 - j, :]
        o_ref[...] = acc.astype(o_ref.dtype)

    return pl.pallas_call(
        body, name=name, grid=(n,),
        in_specs=[pl.BlockSpec((tr, width), lambda i: (i, 0)), pl.BlockSpec((tr, width), lambda i: (jnp.minimum(i + 1, n - 1), 0)),
                  pl.BlockSpec(w.shape, lambda i: (0, 0))],
        out_specs=pl.BlockSpec((tr, width), lambda i: (i, 0)), out_shape=jax.ShapeDtypeStruct((s_dim, width), BF16),
        compiler_params=pltpu.CompilerParams(dimension_semantics=("arbitrary",)),
    )(da, da, w)


def _sel_dot(a, sel, pieces, dims=(((1,), (0,)), ((), ())), sel_left=False):
    sel = sel.astype(BF16)
    acc, rest = None, a
    for _ in range(pieces):
        piece = rest.astype(BF16)
        rest = rest - piece.astype(F32)
        part = _dot(sel, piece, dims) if sel_left else _dot(piece, sel, dims)
        acc = part if acc is None else acc + part
    return acc


def _ssd_consts():
    L = SSD_CHUNK
    tri = np.tril(np.ones((L, L), np.float32))
    expand = np.zeros((LANE, SSD_INNER), np.float32)
    expand128 = np.zeros((LANE, SSD_HEADS * LANE), np.float32)
    for h in range(SSD_HEADS):
        expand[h, h * SSD_HEAD_DIM:(h + 1) * SSD_HEAD_DIM] = 1.0
        expand128[h, h * LANE:(h + 1) * LANE] = 1.0
    return jnp.asarray(tri), jnp.asarray(expand), jnp.asarray(expand128), jnp.asarray(expand.T.copy())


def _ssd_prep(dt_ref, bias_ref, alog_ref, tri_ref, exp_ref, exp128_ref, cs_s, cst_s, ex_s, csx_s):
    L = SSD_CHUNK
    dt = _softplus(dt_ref[...] + bias_ref[...])
    a = -jnp.exp(alog_ref[...])
    cs = _sel_dot(dt * a, tri_ref[...], 3, sel_left=True)
    cs_s[...] = cs
    cst_s[...] = cs.T
    last = cs_s[L - 1:L, :]
    expand = exp_ref[...]
    ex_s[...] = _sel_dot(jnp.exp(cs), expand, 2)
    f_x = _sel_dot(jnp.exp(last - cs), expand, 2)
    dt_x = _sel_dot(dt, expand, 2)
    csx_s[...] = _sel_dot(cs, exp128_ref[...], 3)
    t_x = ex_s[L - 1:L, :]
    return dt, a, dt_x, f_x, t_x


def _decay_matrix(csx_s, cst_s, h, tril):
    seg = csx_s[:, h * LANE:(h + 1) * LANE] - cst_s[h:h + 1, :]
    return jnp.exp(jnp.where(tril, seg, -jnp.inf))


def _ssd_fwd(xbca, dtr, dtcb, bias, alog, d_x, name="ssd_fwd"):
    s_dim = xbca.shape[0]
    L = SSD_CHUNK
    nc = s_dim // L
    tri, expand, expand128, _ = _ssd_consts()

    def body(xs_ref, b_ref, c_ref, dt_ref, bias_ref, alog_ref, dx_ref, tri_ref, exp_ref, exp128_ref,
             y_ref, st_ref, st_s, cs_s, cst_s, ex_s, csx_s):
        @pl.when(pl.program_id(0) == 0)
        def _():
            st_s[...] = jnp.zeros_like(st_s)

        dt, a, dt_x, f_x, t_x = _ssd_prep(dt_ref, bias_ref, alog_ref, tri_ref, exp_ref, exp128_ref, cs_s, cst_s, ex_s, csx_s)
        st_ref[0] = st_s[...]
        row = lax.broadcasted_iota(jnp.int32, (L, L), 0)
        col = lax.broadcasted_iota(jnp.int32, (L, L), 1)
        tril = row >= col
        low = col < SSD_HEAD_DIM
        for g in range(2):
            bg = b_ref[:, g * LANE:(g + 1) * LANE]
            cg = c_ref[:, g * LANE:(g + 1) * LANE].astype(BF16)
            gmat = _dot(cg, bg.astype(BF16), _NT)
            bgt = bg.T.astype(BF16)
            for jj in range(4):
                j = 4 * g + jj
                sl = slice(j * LANE, (j + 1) * LANE)
                xp = xs_ref[:, sl]
                x_dt = xp * dt_x[:, sl]
                xb = x_dt.astype(BF16)
                yd = []
                for e in range(2):
                    lm = _decay_matrix(csx_s, cst_s, 2 * j + e, tril)
                    yd.append(_dot((gmat * lm).astype(BF16), xb))
                stp = st_s[j]
                z = _dot(cg, stp.astype(BF16))
                y_ref[:, sl] = jnp.where(low, yd[0], yd[1]) + ex_s[:, sl] * z + dx_ref[:, sl] * xp
                xf = (x_dt * f_x[:, sl]).astype(BF16)
                st_s[j] = t_x[:, sl] * stp + _dot(bgt, xf)

    const = lambda shape: pl.BlockSpec(shape, lambda c: tuple(0 for _ in shape))
    return pl.pallas_call(
        body, name=name, grid=(nc,),
        in_specs=[pl.BlockSpec((L, 1024), lambda c: (c, 0)), pl.BlockSpec((L, 256), lambda c: (c, 4)),
                  pl.BlockSpec((L, 256), lambda c: (c, 5)), pl.BlockSpec((L, LANE), lambda c: (c, dtcb)),
                  const((1, LANE)), const((1, LANE)), const((1, 1024)), const((L, L)), const((LANE, 1024)),
                  const((LANE, 2048))],
        out_specs=[pl.BlockSpec((L, 1024), lambda c: (c, 0)), pl.BlockSpec((1, 8, LANE, LANE), lambda c: (c, 0, 0, 0))],
        out_shape=[jax.ShapeDtypeStruct((s_dim, 1024), F32), jax.ShapeDtypeStruct((nc, 8, LANE, LANE), F32)],
        scratch_shapes=[pltpu.VMEM((8, LANE, LANE), F32), pltpu.VMEM((L, LANE), F32), pltpu.VMEM((LANE, L), F32),
                        pltpu.VMEM((L, 1024), F32), pltpu.VMEM((L, 2048), F32)],
        compiler_params=pltpu.CompilerParams(dimension_semantics=("arbitrary",)),
    )(xbca, xbca, xbca, dtr, bias, alog, d_x, tri, expand, expand128)


def _ssd_bwd(xbca, dtr, dtcb, bias, alog, d_x, states, dy, name="ssd_bwd"):
    s_dim = xbca.shape[0]
    L = SSD_CHUNK
    nc = s_dim // L
    tri, expand, expand128, expand_t = _ssd_consts()

    def body(xs_ref, b_ref, c_ref, dt_ref, bias_ref, alog_ref, dx_ref, tri_ref, exp_ref, exp128_ref, expt_ref,
             st_ref, dy_ref, dxbc_ref, ddt_ref, dbias_ref, dalog_ref, dd_ref,
             dst_s, cs_s, cst_s, ex_s, csx_s, dcsx_s, ddtx_s, dcol_s, drow_s, dlast_s, dd_s):
        @pl.when(pl.program_id(0) == 0)
        def _():
            dst_s[...] = jnp.zeros_like(dst_s)
            dbias_ref[...] = jnp.zeros_like(dbias_ref)
            dalog_ref[...] = jnp.zeros_like(dalog_ref)
            dd_s[...] = jnp.zeros_like(dd_s)

        dt, a, dt_x, f_x, t_x = _ssd_prep(dt_ref, bias_ref, alog_ref, tri_ref, exp_ref, exp128_ref, cs_s, cst_s, ex_s, csx_s)
        row = lax.broadcasted_iota(jnp.int32, (L, L), 0)
        col = lax.broadcasted_iota(jnp.int32, (L, L), 1)
        tril = row >= col
        low = col < SSD_HEAD_DIM
        dcol_s[...] = jnp.zeros_like(dcol_s)
        drow_s[...] = jnp.zeros_like(drow_s)
        for g in range(2):
            bg = b_ref[:, g * LANE:(g + 1) * LANE]
            cg = c_ref[:, g * LANE:(g + 1) * LANE]
            bgb, cgb = bg.astype(BF16), cg.astype(BF16)
            gmat = _dot(cgb, bgb, _NT)
            d_g = jnp.zeros((L, L), F32)
            d_b = jnp.zeros((L, LANE), F32)
            d_c = jnp.zeros((L, LANE), F32)
            for jj in range(4):
                j = 4 * g + jj
                sl = slice(j * LANE, (j + 1) * LANE)
                xp = xs_ref[:, sl]
                dtp = dt_x[:, sl]
                x_dt = xp * dtp
                xb = x_dt.astype(BF16)
                dyp = dy_ref[:, sl]
                dd_s[:, sl] += _colsum(dyp * xp)
                d_xdt = jnp.zeros((L, LANE), F32)
                for e in range(2):
                    h = 2 * j + e
                    lm = _decay_matrix(csx_s, cst_s, h, tril)
                    m = gmat * lm
                    dye = jnp.where(low if e == 0 else jnp.logical_not(low), dyp, 0.0).astype(BF16)
                    d_m = jnp.where(tril, _dot(dye, xb, _NT), 0.0)
                    d_xdt = d_xdt + _dot(m.astype(BF16), dye, _TN)
                    d_g = d_g + d_m * lm
                    w = d_m * m
                    dcol_s[...] += jnp.where(col == h, jnp.sum(w, axis=1, keepdims=True), 0.0)
                    drow_s[...] += jnp.where(row == h, jnp.sum(w, axis=0, keepdims=True), 0.0)
                stp = st_ref[0, j]
                stb = stp.astype(BF16)
                dstn = dst_s[j]
                dstb = dstn.astype(BF16)
                e_p = ex_s[:, sl]
                f_p = f_x[:, sl]
                t_p = t_x[:, sl]
                z = _dot(cgb, stb)
                d_z = (e_p * dyp).astype(BF16)
                d_c = d_c + _dot(d_z, stb, _NT)
                d_xf = _dot(bgb, dstb)
                d_b = d_b + _dot((x_dt * f_p).astype(BF16), dstb, _NT)
                d_xdt = d_xdt + f_p * d_xf
                d_f = x_dt * d_xf * f_p
                dcsx_s[:, sl] = dyp * e_p * z - d_f
                dlast_s[:, sl] = _colsum(d_f) + _colsum(dstn * stp) * t_p
                dst_s[j] = _dot(cgb, d_z, _TN) + t_p * dstn
                dxbc_ref[:, sl] = dx_ref[:, sl] * dyp + d_xdt * dtp
                ddtx_s[:, sl] = d_xdt * xp
            d_gb = d_g.astype(BF16)
            dxbc_ref[:, 1024 + g * LANE:1024 + (g + 1) * LANE] = d_b + _dot(d_gb, cgb, _TN)
            dxbc_ref[:, 1280 + g * LANE:1280 + (g + 1) * LANE] = d_c + _dot(d_gb, bgb)

        expt = expt_ref[...]
        dlast = _sel_dot(jnp.broadcast_to(dlast_s[...], (8, 1024)), expt, 3)
        d_cs = dcol_s[...] - drow_s[...].T + _sel_dot(dcsx_s[...], expt, 3)
        rown = lax.broadcasted_iota(jnp.int32, (L, LANE), 0)
        d_cs = d_cs + jnp.where(rown == L - 1, jnp.sum(dlast, axis=0, keepdims=True) * 0.125, 0.0)
        d_da = _sel_dot(d_cs, tri_ref[...], 3, _TN, sel_left=True)
        d_dt = d_da * a + _sel_dot(ddtx_s[...], expt, 3)
        dalog_ref[...] += _colsum(d_da * dt) * a
        d_raw = d_dt * _sigmoid(dt_ref[...] + bias_ref[...])
        ddt_ref[...] = d_raw.astype(ddt_ref.dtype)
        dbias_ref[...] += _colsum(d_raw)
        dd8 = _sel_dot(jnp.broadcast_to(dd_s[...], (8, 1024)), expt, 3)
        dd_ref[...] = jnp.sum(dd8, axis=0, keepdims=True) * 0.125

    const = lambda shape: pl.BlockSpec(shape, lambda c: tuple(0 for _ in shape))
    rev = lambda cb: (lambda c: (nc - 1 - c, cb))
    return pl.pallas_call(
        body, name=name, grid=(nc,),
        in_specs=[pl.BlockSpec((L, 1024), rev(0)), pl.BlockSpec((L, 256), rev(4)), pl.BlockSpec((L, 256), rev(5)),
                  pl.BlockSpec((L, LANE), rev(dtcb)), const((1, LANE)), const((1, LANE)), const((1, 1024)), const((L, L)),
                  const((LANE, 1024)), const((LANE, 2048)), const((1024, LANE)),
                  pl.BlockSpec((1, 8, LANE, LANE), lambda c: (nc - 1 - c, 0, 0, 0)), pl.BlockSpec((L, 1024), rev(0))],
        out_specs=[pl.BlockSpec((L, SSD_XBC), rev(0)), pl.BlockSpec((L, LANE), rev(0)), const((1, LANE)), const((1, LANE)),
                   const((1, LANE))],
        out_shape=[jax.ShapeDtypeStruct((s_dim, SSD_XBC), F32), jax.ShapeDtypeStruct((s_dim, LANE), BF16),
                   jax.ShapeDtypeStruct((1, LANE), F32), jax.ShapeDtypeStruct((1, LANE), F32),
                   jax.ShapeDtypeStruct((1, LANE), F32)],
        scratch_shapes=[pltpu.VMEM((8, LANE, LANE), F32), pltpu.VMEM((L, LANE), F32), pltpu.VMEM((LANE, L), F32),
                        pltpu.VMEM((L, 1024), F32), pltpu.VMEM((L, 2048), F32), pltpu.VMEM((L, 1024), F32),
                        pltpu.VMEM((L, 1024), F32), pltpu.VMEM((L, LANE), F32), pltpu.VMEM((LANE, L), F32),
                        pltpu.VMEM((1, 1024), F32), pltpu.VMEM((1, 1024), F32)],
        compiler_params=pltpu.CompilerParams(dimension_semantics=("arbitrary",)),
    )(xbca, xbca, xbca, dtr, bias, alog, d_x, tri, expand, expand128, expand_t, states, dy)


def _swap_halves(u):
    width = u.shape[1]
    lane = lax.broadcasted_iota(jnp.int32, u.shape, 1)
    return jnp.where(lane % MLA_ROPE < MLA_ROPE // 2, pltpu.roll(u, width - MLA_ROPE // 2, 1), pltpu.roll(u, MLA_ROPE // 2, 1))


def _rope_fwd_fn(u, cos, sin):
    return u * cos + _swap_halves(u) * sin


def _rope_bwd_fn(d, cos, sin):
    return d * cos + _swap_halves(d * sin)


def _spread4(v):
    return v + pltpu.roll(v, 32, 1) + pltpu.roll(v, 64, 1) + pltpu.roll(v, 96, 1)


def _att_masks(tq):
    lane = lax.broadcasted_iota(jnp.int32, (tq, LANE), 1)
    return lane // MLA_NOPE, lane // MLA_ROPE


def _att_tile(i, tq):
    klen = (i + 1) * tq
    qpos = i * tq + lax.broadcasted_iota(jnp.int32, (tq, klen), 0)
    kpos = lax.broadcasted_iota(jnp.int32, (tq, klen), 1)
    return slice(i * tq, (i + 1) * tq), klen, qpos >= kpos


def _att_qcat(qn_t, qr_t, par, e, half_id, grp_id):
    return jnp.concatenate([jnp.where(half_id == par, qn_t * ATT_SCALE, 0.0), jnp.where(grp_id == e, qr_t * ATT_SCALE, 0.0)],
                           axis=1).astype(BF16)


def _att_exp(qcat, kcat, causal):
    s = jnp.where(causal, _dot(qcat, kcat, _NT), -jnp.inf)
    e = jnp.exp(s - jnp.max(s, axis=1, keepdims=True))
    return e, 1.0 / jnp.sum(e, axis=1, keepdims=True)


def _att_specs(s_dim):
    col = lambda f: pl.BlockSpec((s_dim, LANE), lambda j: (0, f(j)))
    return [col(lambda j: j), col(lambda j: j // 2), col(lambda j: j), col(lambda j: 0), col(lambda j: 8 + j)]


def _att_fwd(q, qr, kv, krt, name="att_fwd"):
    s_dim = q.shape[0]
    tq = min(ATT_TQ, s_dim)

    def body(qn_ref, qr_ref, kn_ref, krt_ref, v_ref, o_ref, kcat_s, vb_s):
        e0 = 2 * (pl.program_id(0) % 2)
        half_id, grp_id = _att_masks(tq)
        kcat_s[...] = jnp.concatenate([kn_ref[...], krt_ref[...]], axis=1).astype(BF16)
        vb_s[...] = v_ref[...].astype(BF16)
        for i in range(s_dim // tq):
            rows, klen, causal = _att_tile(i, tq)
            qn_t, qr_t = qn_ref[rows, :], qr_ref[rows, :]
            outs = []
            for par in range(2):
                qcat = _att_qcat(qn_t, qr_t, par, e0 + par, half_id, grp_id)
                e, inv_l = _att_exp(qcat, kcat_s[0:klen, :], causal)
                outs.append(_dot(e.astype(BF16), vb_s[0:klen, :]) * inv_l)
            o_ref[rows, :] = jnp.where(half_id == 0, outs[0], outs[1])

    return pl.pallas_call(
        body, name=name, grid=(MLA_HEADS // 2,), in_specs=_att_specs(s_dim),
        out_specs=pl.BlockSpec((s_dim, LANE), lambda j: (0, j)), out_shape=jax.ShapeDtypeStruct((s_dim, 1024), F32),
        scratch_shapes=[pltpu.VMEM((s_dim, 2 * LANE), BF16), pltpu.VMEM((s_dim, LANE), BF16)],
        compiler_params=pltpu.CompilerParams(dimension_semantics=("parallel",)),
    )(q, qr, kv, krt, kv)


def _att_bwd(q, qr, kv, krt, o, do, name="att_bwd"):
    s_dim = q.shape[0]
    tq = min(ATT_TQ, s_dim)

    def body(qn_ref, qr_ref, kn_ref, krt_ref, v_ref, o_ref, do_ref, dqn_ref, dqr_ref, dkn_ref, dv_ref, dkrt_ref,
             kcat_s, vb_s):
        e0 = 2 * (pl.program_id(0) % 2)
        half_id, grp_id = _att_masks(tq)
        kcat_s[...] = jnp.concatenate([kn_ref[...], krt_ref[...]], axis=1).astype(BF16)
        vb_s[...] = v_ref[...].astype(BF16)
        dkn_ref[...] = jnp.zeros_like(dkn_ref)
        dv_ref[...] = jnp.zeros_like(dv_ref)
        dkrt_ref[...] = jnp.zeros_like(dkrt_ref)
        for i in range(s_dim // tq):
            rows, klen, causal = _att_tile(i, tq)
            qn_t, qr_t, o_t, do_t = qn_ref[rows, :], qr_ref[rows, :], o_ref[rows, :], do_ref[rows, :]
            dqn = jnp.zeros((tq, LANE), F32)
            dqr = jnp.zeros((tq, LANE), F32)
            for par in range(2):
                qcat = _att_qcat(qn_t, qr_t, par, e0 + par, half_id, grp_id)
                e, inv_l = _att_exp(qcat, kcat_s[0:klen, :], causal)
                p = e * inv_l
                dom = jnp.where(half_id == par, do_t, 0.0)
                domb = dom.astype(BF16)
                d_p = _dot(domb, vb_s[0:klen, :], _NT)
                d_row = jnp.sum(dom * o_t, axis=1, keepdims=True)
                d_s = (p * (d_p - d_row)).astype(BF16)
                dqcat = _dot(d_s, kcat_s[0:klen, :]) * ATT_SCALE
                dqn = dqn + jnp.where(half_id == par, dqcat[:, :LANE], 0.0)
                dqr = dqr + jnp.where(grp_id == e0 + par, dqcat[:, LANE:], 0.0)
                dkcat = _dot(d_s, qcat, _TN)
                dkn_ref[0:klen, :] += dkcat[:, :LANE]
                dkrt_ref[0:klen, :] += dkcat[:, LANE:]
                dv_ref[0:klen, :] += _dot(p.astype(BF16), domb, _TN)
            dqn_ref[rows, :] = dqn.astype(dqn_ref.dtype)
            dqr_ref[rows, :] = dqr

    col = lambda f: pl.BlockSpec((s_dim, LANE), lambda j: (0, f(j)))
    return pl.pallas_call(
        body, name=name, grid=(MLA_HEADS // 2,), in_specs=_att_specs(s_dim) + [col(lambda j: j), col(lambda j: j)],
        out_specs=[col(lambda j: j), pl.BlockSpec((None, s_dim, LANE), lambda j: (j % 2, 0, j // 2)), col(lambda j: j),
                   col(lambda j: j), pl.BlockSpec((None, s_dim, LANE), lambda j: (j, 0, 0))],
        out_shape=[jax.ShapeDtypeStruct((s_dim, 1024), BF16), jax.ShapeDtypeStruct((2, s_dim, 512), F32),
                   jax.ShapeDtypeStruct((s_dim, 1024), F32), jax.ShapeDtypeStruct((s_dim, 1024), F32),
                   jax.ShapeDtypeStruct((MLA_HEADS // 2, s_dim, LANE), F32)],
        scratch_shapes=[pltpu.VMEM((s_dim, 2 * LANE), BF16), pltpu.VMEM((s_dim, LANE), BF16)],
        compiler_params=pltpu.CompilerParams(dimension_semantics=("parallel",)),
    )(q, qr, kv, krt, kv, o, do)


def _gather_many(shards, name):
    n_arr = len(shards)

    def body(*refs):
        x_refs, out_refs = refs[:n_arr], refs[n_arr:2 * n_arr]
        send_sems, recv_sems, local_sems = refs[2 * n_arr:]
        x_i, y_i, c_i = lax.axis_index("x"), lax.axis_index("y"), lax.axis_index("c")
        me, sibling = (x_i, y_i, c_i), (x_i, y_i, 1 - c_i)
        chips = [(1 - x_i, y_i), (x_i, 1 - y_i), (1 - x_i, 1 - y_i)]

        def copy(a, k, block, to, src=None):
            slot = out_refs[a].at[4 * block[0] + 2 * block[1] + block[2]]
            return pltpu.make_async_remote_copy(
                src_ref=slot if src is None else src, dst_ref=slot, send_sem=send_sems.at[a, k],
                recv_sem=recv_sems.at[a, k], device_id=to, device_id_type=pl.DeviceIdType.MESH)

        mine, first, passed = [], [], []
        for a in range(n_arr):
            mine.append(pltpu.make_async_copy(x_refs[a], out_refs[a].at[4 * x_i + 2 * y_i + c_i], local_sems.at[a]))
            mine[a].start()
            first.append([copy(a, 0, me, sibling, src=x_refs[a])]
                         + [copy(a, 1 + j, me, (*chip, c_i), src=x_refs[a]) for j, chip in enumerate(chips)])
            for cp in first[a]:
                cp.start()
            passed.append([copy(a, 4 + j, (*chip, c_i), sibling) for j, chip in enumerate(chips)])
        for j, chip in enumerate(chips):
            for a in range(n_arr):
                copy(a, 1 + j, (*chip, c_i), me).wait_recv()
                passed[a][j].start()
        for a in range(n_arr):
            copy(a, 0, sibling, me).wait_recv()
            for j, chip in enumerate(chips):
                copy(a, 4 + j, (*chip, 1 - c_i), me).wait_recv()
        for a in range(n_arr):
            for cp in first[a] + passed[a]:
                cp.wait_send()
            mine[a].wait()

    any_spec = pl.BlockSpec(memory_space=pl.ANY)
    return pl.pallas_call(
        body, name=name, out_shape=[jax.ShapeDtypeStruct((N_DEV,) + x.shape, x.dtype) for x in shards],
        in_specs=[any_spec] * n_arr, out_specs=[any_spec] * n_arr,
        scratch_shapes=[pltpu.SemaphoreType.DMA((n_arr, 7)), pltpu.SemaphoreType.DMA((n_arr, 7)),
                        pltpu.SemaphoreType.DMA((n_arr,))],
    )(*shards)


_HBM = pl.BlockSpec(memory_space=pltpu.HBM)
_SEM = pl.BlockSpec(memory_space=pltpu.SEMAPHORE)


def _plan_copies(plan, src_refs, land_refs, send_sems, recv_sems):
    copies = []
    for s_ref, l_ref in zip(src_refs, land_refs):
        for src, dst, peer in plan(s_ref, l_ref):
            k = len(copies)
            copies.append(pltpu.make_async_remote_copy(
                src_ref=src, dst_ref=dst, send_sem=send_sems.at[k], recv_sem=recv_sems.at[k], device_id=peer,
                device_id_type=pl.DeviceIdType.MESH))
    return copies


def _split_start(srcs, lands, plan, n_copy, name, after=None):
    n = len(srcs)
    n_in = 2 * n + (after is not None)

    def body(*refs):
        for cp in _plan_copies(plan, refs[:n], refs[n:2 * n], refs[n_in], refs[n_in + 1]):
            cp.start()
        refs[-1][...] = jnp.zeros_like(refs[-1])

    sems = pltpu.SemaphoreType.DMA((n * n_copy,))
    res = pl.pallas_call(
        body, name=name,
        out_shape=(sems, sems, *[pltpu.HBM(a.shape, a.dtype) for a in list(srcs) + list(lands)],
                   jax.ShapeDtypeStruct((8, LANE), F32)),
        in_specs=[_HBM] * (2 * n) + [pl.BlockSpec(memory_space=pl.ANY)] * (after is not None),
        out_specs=(_SEM, _SEM, *[_HBM] * (2 * n), pl.BlockSpec(memory_space=pltpu.VMEM)),
        input_output_aliases={i: 2 + i for i in range(2 * n)},
        compiler_params=pltpu.CompilerParams(has_side_effects=pltpu.SideEffectType.DATAFLOW_SIDE_EFFECTING),
    )(*[pltpu.with_memory_space_constraint(a, pltpu.HBM) for a in list(srcs) + list(lands)],
      *([after] if after is not None else []))
    return res[0], res[1], list(res[2:2 + n]), list(res[2 + n:2 + 2 * n]), res[-1]


def _split_wait(send_sems, recv_sems, srcs, lands, after, plan, name):
    n = len(srcs)

    def body(*refs):
        copies = _plan_copies(plan, refs[:n], refs[n:2 * n], refs[2 * n], refs[2 * n + 1])
        for cp in copies:
            cp.wait_send()
        for cp in copies:
            cp.wait_recv()

    res = pl.pallas_call(
        body, name=name, out_shape=tuple(pltpu.HBM(a.shape, a.dtype) for a in list(srcs) + list(lands)),
        in_specs=[_HBM] * (2 * n) + [_SEM, _SEM, pl.BlockSpec(memory_space=pl.ANY)], out_specs=tuple([_HBM] * (2 * n)),
        input_output_aliases={i: i for i in range(2 * n)},
        compiler_params=pltpu.CompilerParams(has_side_effects=pltpu.SideEffectType.DATAFLOW_SIDE_EFFECTING),
    )(*srcs, *lands, send_sems, recv_sems, after)
    return list(res[:n]), list(res[n:])


def _plan_broadcast(src, land):
    x_i, y_i, c_i = lax.axis_index("x"), lax.axis_index("y"), lax.axis_index("c")
    me = 4 * x_i + 2 * y_i + c_i
    return [(src, land.at[me], (x_i ^ (k >> 2), y_i ^ ((k >> 1) & 1), c_i ^ (k & 1))) for k in range(1, N_DEV)]


def _plan_scatter(src, land):
    x_i, y_i, c_i = lax.axis_index("x"), lax.axis_index("y"), lax.axis_index("c")
    me = 4 * x_i + 2 * y_i + c_i
    plan = []
    for k in range(1, N_DEV):
        px, py, pc = x_i ^ (k >> 2), y_i ^ ((k >> 1) & 1), c_i ^ (k & 1)
        plan.append((src.at[4 * px + 2 * py + pc], land.at[me], (px, py, pc)))
    return plan


def _adam_math(g, w, m, v):
    m_new = ADAM_B1 * m + (1.0 - ADAM_B1) * g
    v_new = ADAM_B2 * v + (1.0 - ADAM_B2) * (g * g)
    m_hat = m_new / (1.0 - ADAM_B1 ** ADAM_STEP)
    v_hat = v_new / (1.0 - ADAM_B2 ** ADAM_STEP)
    return -ADAM_LR * (m_hat / (jnp.sqrt(v_hat) + ADAM_EPS) + ADAM_WD * w), m_new, v_new


def _adam(slots, w, m, v, name, own=None, own_idx=None):
    n_slot, rows, cols = slots.shape
    tr = ROW_TILE if rows % ROW_TILE == 0 else rows
    has_own = own is not None

    def body(*refs):
        if has_own:
            idx_ref, own_ref, refs = refs[0], refs[1], refs[2:]
        s_ref, w_ref, m_ref, v_ref, g_ref, d_ref, mo_ref, vo_ref = refs
        g = own_ref[...].astype(F32) if has_own else s_ref[0].astype(F32)
        for k in range(0 if has_own else 1, n_slot):
            part = s_ref[k].astype(F32)
            g = g + (jnp.where(idx_ref[0] == k, 0.0, part) if has_own else part)
        g_ref[...] = g
        d_ref[...], mo_ref[...], vo_ref[...] = _adam_math(g, w_ref[...], m_ref[...], v_ref[...])

    spec = pl.BlockSpec((tr, cols), lambda i, *_: (i, 0))
    in_specs = [pl.BlockSpec((n_slot, tr, cols), lambda i, *_: (0, i, 0)), spec, spec, spec]
    if has_own:
        in_specs = [pl.BlockSpec((None, tr, cols), lambda i, idx: (idx[0], i, 0))] + in_specs
    grid_spec = pltpu.PrefetchScalarGridSpec(num_scalar_prefetch=1 if has_own else 0, grid=(rows // tr,), in_specs=in_specs,
                                             out_specs=[spec] * 4)
    ins = ([own_idx, own] if has_own else []) + [slots, w, m, v]
    return pl.pallas_call(
        body, name=name, grid_spec=grid_spec, out_shape=[jax.ShapeDtypeStruct((rows, cols), F32)] * 4,
        compiler_params=pltpu.CompilerParams(dimension_semantics=("parallel",)),
    )(*ins)


PACK_ROWS, PACK_W = 24, 1536
REPL_W = (("ssd_conv_b", 1536), ("ssd_dt_bias", 16), ("ssd_A_log", 16), ("ssd_D", 16), ("ssd_norm_w", 1024),
          ("mla_q_norm_w", 384), ("mla_kv_norm_w", 256), ("mla_out_norm_w", 1024), ("ln_mix_g", 1024),
          ("ln_mix_b", 1024), ("ln_ffn_g", 1024), ("ln_ffn_b", 1024))
LOSS_ROW = 4 + len(REPL_W)


def _pack_small(conv_w_grad, grads, loss, name="pack_small"):
    def body(*refs):
        cw_ref, g_refs, loss_ref, o_ref = refs[0], refs[1:1 + len(REPL_W)], refs[1 + len(REPL_W)], refs[-1]
        o_ref[...] = jnp.zeros_like(o_ref)
        o_ref[0:4, :] = cw_ref[...]
        for i, g_ref in enumerate(g_refs):
            o_ref[4 + i:5 + i, 0:g_ref.shape[1]] = g_ref[...]
        o_ref[LOSS_ROW:LOSS_ROW + 1, 0:LANE] = loss_ref[...]

    return pl.pallas_call(body, name=name, out_shape=jax.ShapeDtypeStruct((PACK_ROWS, PACK_W), F32))(conv_w_grad, *grads, loss)


def _adam_small(gathered, wmv, name="adam_small"):
    def body(*refs):
        s_ref = refs[0]
        in_refs = refs[1:1 + 3 * len(REPL_W)]
        cw_ref, loss_ref = refs[1 + 3 * len(REPL_W)], refs[2 + 3 * len(REPL_W)]
        out_refs = refs[3 + 3 * len(REPL_W):-1]
        tot = refs[-1]
        acc = s_ref[0]
        for k in range(1, N_DEV):
            acc = acc + s_ref[k]
        tot[...] = acc
        cw_ref[...] = tot[0:4, :]
        loss_ref[...] = tot[LOSS_ROW:LOSS_ROW + 1, 0:LANE]
        for i, (_, width) in enumerate(REPL_W):
            g = tot[4 + i:5 + i, 0:width]
            w_ref, m_ref, v_ref = in_refs[3 * i:3 * i + 3]
            g_ref, d_ref, mo_ref, vo_ref = out_refs[4 * i:4 * i + 4]
            g_ref[...] = g
            d_ref[...], mo_ref[...], vo_ref[...] = _adam_math(g, w_ref[...], m_ref[...], v_ref[...])

    flat_in = [a for triple in wmv for a in triple]
    out_shape = [jax.ShapeDtypeStruct((4, PACK_W), F32), jax.ShapeDtypeStruct((1, LANE), F32)]
    for _, width in REPL_W:
        out_shape += [jax.ShapeDtypeStruct((1, width), F32)] * 4
    res = pl.pallas_call(body, name=name, out_shape=out_shape, scratch_shapes=[pltpu.VMEM((PACK_ROWS, PACK_W), F32)])(
        gathered, *flat_in)
    return res[0], res[1], [res[2 + 4 * i:6 + 4 * i] for i in range(len(REPL_W))]


def _cols_full(g):
    return jnp.transpose(g, (1, 0, 2)).reshape(g.shape[1], -1)


def _cols_split(full):
    k_dim, n_dim = full.shape
    return jnp.transpose(full.reshape(k_dim, N_DEV, n_dim // N_DEV), (1, 0, 2))


PROJ_BLOCK = {"z": (1024, 0), "dt": (LANE, 8), "q_c": (MLA_Q_RANK, 3), "xbc": (SSD_XBC, 1), "kv_c": (MLA_KV_RANK, 12),
              "k_rope": (LANE, 26)}


def _win_pad(wt):
    z = lambda n: jnp.zeros((n, wt.shape[1]), wt.dtype)
    return jnp.concatenate([wt[:1024], wt[2560:2576], z(112), wt[2576:2960], wt[1024:2560], wt[2960:3216], wt[3216:3248],
                            z(96)], axis=0)


def _win_unpad(wt):
    return jnp.concatenate([wt[:1024], wt[1536:3072], wt[1024:1040], wt[1152:1536], wt[3072:3328], wt[3328:3360]], axis=0)


def _heads_split_t(wt, a, b):
    w3 = wt.reshape(MLA_HEADS, a + b, wt.shape[1])
    return jnp.concatenate([w3[:, :a].reshape(-1, wt.shape[1]), w3[:, a:].reshape(-1, wt.shape[1])], axis=0)


def _heads_merge_t(wt, a, b):
    wa = wt[:MLA_HEADS * a].reshape(MLA_HEADS, a, wt.shape[1])
    wb = wt[MLA_HEADS * a:].reshape(MLA_HEADS, b, wt.shape[1])
    return jnp.concatenate([wa, wb], axis=1).reshape(-1, wt.shape[1])


def _heads_split(w, a, b):
    k_dim = w.shape[0]
    w3 = w.reshape(k_dim, MLA_HEADS, a + b)
    return jnp.concatenate([w3[:, :, :a].reshape(k_dim, -1), w3[:, :, a:].reshape(k_dim, -1)], axis=1)


def _heads_merge(w, a, b):
    k_dim = w.shape[0]
    wa = w[:, :MLA_HEADS * a].reshape(k_dim, MLA_HEADS, a)
    wb = w[:, MLA_HEADS * a:].reshape(k_dim, MLA_HEADS, b)
    return jnp.concatenate([wa, wb], axis=2).reshape(k_dim, -1)


def _pad_lanes(v, width=LANE):
    return jnp.concatenate([v, jnp.zeros((v.shape[0], width - v.shape[1]), v.dtype)], axis=1)


def _local_step(x, p, positions, tgt, W, P, comm=None):
    comm = comm or {}
    zero_tok = jnp.zeros((8, LANE), F32)
    s_dim = x.shape[0]
    inv_freq = 1.0 / (ROPE_BASE ** (jnp.arange(0, MLA_ROPE, 2, dtype=F32) / MLA_ROPE))
    ang = positions.astype(F32)[:, None] * inv_freq
    cos, sin = jnp.cos(ang), jnp.sin(ang)
    cos32 = jnp.concatenate([cos, cos], axis=1)
    sin32 = jnp.concatenate([-sin, sin], axis=1)
    cos512, sin512 = jnp.tile(cos32, (1, 16)), jnp.tile(sin32, (1, 16))
    cos128, sin128 = jnp.tile(cos32, (1, 4)), jnp.tile(sin32, (1, 4))
    bias_p, alog_p = _pad_lanes(P["ssd_dt_bias"]), _pad_lanes(P["ssd_A_log"])
    d_x = jnp.repeat(P["ssd_D"], SSD_HEAD_DIM, axis=1)

    xb, pb = x.astype(BF16), p.astype(BF16)
    proj = _mm(xb, W["w_in"], tb=True, after=comm.get("token0", zero_tok), name="mm_in")
    z, qc, kvc, kr = [(proj,) + PROJ_BLOCK[n] for n in ("z", "q_c", "kv_c", "k_rope")]
    xbca = _conv_fwd(proj, PROJ_BLOCK["xbc"][1], P["ssd_conv_w"], P["ssd_conv_b"])
    y, states = _ssd_fwd(xbca, proj, PROJ_BLOCK["dt"][1], bias_p, alog_p, d_x)
    (yssd,) = _rowwise(_gate_rms, [y, z], [P["ssd_norm_w"]], [(1024, BF16)], name="ssd_gate_norm")
    qn, kvn, krt = _rowwise(lambda a, c, u, cs, sn, wq, wkv: (_rms(a, wq), _rms(c, wkv), _spread4(_rope_fwd_fn(u, cs, sn))),
                            [qc, kvc, kr, cos128, sin128], [P["mla_q_norm_w"], P["mla_kv_norm_w"]],
                            [(MLA_Q_RANK, BF16), (MLA_KV_RANK, BF16), LANE], name="qkv_norm_rope_k")
    if "late_weights" in comm:
        W = {**W, **comm["late_weights"]("mla", qn)}
    q = _mm(qn, W["mla_w_q_b"], tb=True, name="mm_q")
    kv = _mm(kvn, W["mla_w_kv_b"], name="mm_kv")
    (qr,) = _rowwise(_rope_fwd_fn, [(q, 512, 2), cos512, sin512], [], [512], name="rope_q")
    att = _att_fwd(q, qr, kv, krt)
    (ymla,) = _rowwise(_rms, [att], [P["mla_out_norm_w"]], [(1024, BF16)], name="out_norm")
    ycat = jnp.concatenate([yssd, ymla], axis=1)
    if "late_weights" in comm:
        W = {**W, **comm["late_weights"]("out", ycat)}
    mix = _mm(ycat, W["w_out"], name="mm_out")
    f_h1 = lambda xv, mv, g, b: _ln(ALPHA * xv + mv, g, b)
    h1, h1b = _rowwise(lambda *a: (f_h1(*a),) * 2, [x, mix], [P["ln_mix_g"], P["ln_mix_b"]], [1024, (1024, BF16)],
                       name="ln_mix")
    if "late_weights" in comm:
        W = {**W, **comm["late_weights"]("ffn", h1b)}
    hg = _mm(h1b, W["w_ffn_gate"], tb=True, out_dtype=BF16, name="mm_gate")
    hu, act = _mm(h1b, W["w_ffn_up"], tb=True, name="mm_up",
                  epilogue=(lambda u, g: (u, _silu(g.astype(F32)) * u), [hg], [BF16, BF16]))
    pg = _mm(h1b, W["w_ple_gate"], name="mm_ple_gate")
    pp = _mm(pb, W["w_ple_proj"], name="mm_ple")
    ffn = _mm(act, W["w_ffn_down"], name="mm_down")

    f_h2 = lambda hv, fv, pg, ppv, g, b: _ln(ALPHA * hv + fv + _sigmoid(pg) * ppv, g, b)

    def final_fn(hv, fv, pg, ppv, tv, g, b):
        h2, pull = jax.vjp(f_h2, hv, fv, pg, ppv, g, b)
        diff = h2 - tv
        loss = 0.5 * jnp.sum(jnp.mean(diff * diff, axis=-1, keepdims=True), axis=0, keepdims=True)
        d_h, d_f, d_pg, d_pp, d_g, d_b = pull(diff * (1.0 / D_MODEL))
        return d_h, d_f, d_pg, d_pp, d_g, d_b, jnp.broadcast_to(loss, (1, LANE))

    dh1_a, dffn, dpg, dpp, g_ffn_g, g_ffn_b, loss = _rowwise(
        final_fn, [h1, ffn, pg, pp, tgt], [P["ln_ffn_g"], P["ln_ffn_b"]], [1024] + [(1024, BF16)] * 3,
        [1024, 1024, LANE], name="final")

    G = {}
    def swiglu_bwd(d, g, u):
        g, u = g.astype(F32), u.astype(F32)
        sg = _sigmoid(g)
        return d * u * (sg * (1.0 + g * (1.0 - sg))), d * (g * sg)

    dg, du = _mm(dffn, W["w_ffn_down"], tb=True, name="mm_down_dx",
                 epilogue=(swiglu_bwd, [hg, hu], [BF16, BF16]))
    G["w_ffn_down"] = _mm(act, dffn, ta=True, out_dtype=GRAD_DT, name="mm_down_dw")
    dh1 = _mm(dg, W["w_ffn_gate"], add=dh1_a, name="mm_gate_dx")
    dh1 = _mm(du, W["w_ffn_up"], add=dh1, name="mm_up_dx")
    dh1 = _mm(dpg, W["w_ple_gate"], tb=True, add=dh1, name="mm_ple_gate_dx")
    G["w_ffn_gate"] = _mm(dg, h1b, ta=True, out_dtype=GRAD_DT, name="mm_gate_dw")
    G["w_ffn_up"] = _mm(du, h1b, ta=True, out_dtype=GRAD_DT, name="mm_up_dw")
    G["w_ple_gate"] = _mm(h1b, dpg, ta=True, out_dtype=GRAD_DT, name="mm_ple_gate_dw")
    G["w_ple_proj"] = _mm(pb, dpp, ta=True, out_dtype=GRAD_DT, name="mm_ple_dw")
    dx_a, dmix, g_mix_g, g_mix_b = _rowwise(
        lambda xv, mv, dv, g, b: _vjp_rows(f_h1)(xv, mv, g, b, dv), [x, mix, dh1], [P["ln_mix_g"], P["ln_mix_b"]],
        [1024, (1024, BF16)], [1024, 1024], name="ln_mix_bwd")
    dycat = _mm(dmix, W["w_out"], tb=True, name="mm_out_dx")
    G["w_out"] = _mm(ycat, dmix, ta=True, out_dtype=GRAD_DT, name="mm_out_dw")

    grads_done = comm.get("grads", lambda group, grads: zero_tok)
    tok1 = grads_done("ffn", G)
    datt, g_out_norm = _rowwise(lambda a, dv, w, t: _vjp_rows(_rms)(a, w, dv + jnp.min(t)), [att, (dycat, 1024, 1)],
                                [P["mla_out_norm_w"], tok1], [1024], [1024], name="out_norm_bwd")
    dqn_nope, dqr, dkn, dv, dkrt = _att_bwd(q, qr, kv, krt, att, datt)
    dkv = jnp.concatenate([dkn, dv], axis=1)
    (dq_rope,) = _rowwise(lambda d0, d1, c, s: _rope_bwd_fn(d0 + d1, c, s), [(dqr, 512, 0), (dqr, 512, 1), cos512, sin512],
                          [], [(512, BF16)], name="rope_q_bwd")

    def rope_k_bwd(*a):
        d = _spread4(functools.reduce(lambda u, w: u + w, a[:-2]))
        lane = lax.broadcasted_iota(jnp.int32, d.shape, 1)
        return _rope_bwd_fn(jnp.where(lane < MLA_ROPE, d, 0.0), a[-2], a[-1])

    (dkr,) = _rowwise(rope_k_bwd, [(dkrt, LANE, k) for k in range(MLA_HEADS // 2)] + [cos128, sin128], [], [(LANE, BF16)],
                      name="rope_k_bwd")
    dq = jnp.concatenate([dqn_nope, dq_rope], axis=1)
    dqn = _mm(dq, W["mla_w_q_b"], name="mm_q_dx")
    G["mla_w_q_b"] = _mm(dq, qn, ta=True, out_dtype=GRAD_DT, name="mm_q_dw")
    dkvn = _mm(dkv, W["mla_w_kv_b"], tb=True, name="mm_kv_dx")
    G["mla_w_kv_b"] = _mm(kvn, dkv, ta=True, out_dtype=GRAD_DT, name="mm_kv_dw")
    tok2 = grads_done("mla", G)
    def qkv_norm_bwd(a, da, c, dc, wq, wkv, t):
        (d_a, d_wq), (d_c, d_wkv) = _vjp_rows(_rms)(a, wq, da + jnp.min(t)), _vjp_rows(_rms)(c, wkv, dc)
        return d_a, d_c, d_wq, d_wkv

    dqc, dkvc, g_q_norm, g_kv_norm = _rowwise(
        qkv_norm_bwd, [qc, dqn, kvc, dkvn], [P["mla_q_norm_w"], P["mla_kv_norm_w"], tok2],
        [(MLA_Q_RANK, BF16), (MLA_KV_RANK, BF16)], [MLA_Q_RANK, MLA_KV_RANK], name="qkv_norm_bwd")

    dy, dz, g_ssd_norm = _rowwise(lambda yv, zv, dv, w, t: _vjp_rows(_gate_rms)(yv, zv, w, dv + jnp.min(t)),
                                  [y, z, (dycat, 1024, 0)], [P["ssd_norm_w"], tok1], [1024, (1024, BF16)], [1024],
                                  name="ssd_gate_norm_bwd")
    dxbca, ddtr, g_dt_bias, g_alog, g_d = _ssd_bwd(xbca, proj, PROJ_BLOCK["dt"][1], bias_p, alog_p, d_x, states, dy)
    da, g_conv_w, g_conv_b = _conv_bwd_pre(proj, PROJ_BLOCK["xbc"][1], P["ssd_conv_w"], P["ssd_conv_b"], dxbca)
    dxbc = _conv_bwd_in(da, P["ssd_conv_w"])

    small = {
        "ssd_conv_b": g_conv_b, "ssd_dt_bias": g_dt_bias, "ssd_A_log": g_alog, "ssd_D": g_d, "ssd_norm_w": g_ssd_norm,
        "mla_q_norm_w": g_q_norm, "mla_kv_norm_w": g_kv_norm, "mla_out_norm_w": g_out_norm, "ln_mix_g": g_mix_g,
        "ln_mix_b": g_mix_b, "ln_ffn_g": g_ffn_g, "ln_ffn_b": g_ffn_b,
    }
    packed = _pack_small(g_conv_w, [small[n] for n, _ in REPL_W], loss)
    if "small" in comm:
        comm["small"](packed)

    dproj = jnp.concatenate([dz, ddtr, dqc, dxbc, dkvc, dkr], axis=1)
    G["w_in"] = _mm(dproj, xb, ta=True, out_dtype=GRAD_DT, name="mm_in_dw")
    grad_x = _mm(dproj, W["w_in"], add=dx_a, after=grads_done("in", G), name="mm_in_dx")
    return grad_x, G, packed


def kernel(x, p, positions, w_in, ssd_conv_w, ssd_conv_b, ssd_dt_bias, ssd_A_log, ssd_D, ssd_norm_w, mla_q_norm_w, mla_w_q_b, mla_kv_norm_w, mla_w_kv_b, mla_out_norm_w, w_out, ln_mix_g, ln_mix_b, w_ffn_gate, w_ffn_up, w_ffn_down, w_ple_gate, w_ple_proj, ln_ffn_g, ln_ffn_b, loss_target, m_w_in, m_ssd_conv_w, m_ssd_conv_b, m_ssd_dt_bias, m_ssd_A_log, m_ssd_D, m_ssd_norm_w, m_mla_q_norm_w, m_mla_w_q_b, m_mla_kv_norm_w, m_mla_w_kv_b, m_mla_out_norm_w, m_w_out, m_ln_mix_g, m_ln_mix_b, m_w_ffn_gate, m_w_ffn_up, m_w_ffn_down, m_w_ple_gate, m_w_ple_proj, m_ln_ffn_g, m_ln_ffn_b, v_w_in, v_ssd_conv_w, v_ssd_conv_b, v_ssd_dt_bias, v_ssd_A_log, v_ssd_D, v_ssd_norm_w, v_mla_q_norm_w, v_mla_w_q_b, v_mla_kv_norm_w, v_mla_w_kv_b, v_mla_out_norm_w, v_w_out, v_ln_mix_g, v_ln_mix_b, v_w_ffn_gate, v_w_ffn_up, v_w_ffn_down, v_w_ple_gate, v_w_ple_proj, v_ln_ffn_g, v_ln_ffn_b):
    args = dict(locals())
    core = lax.axis_index("c")
    me = 4 * lax.axis_index("x") + 2 * lax.axis_index("y") + core

    conv_sh = ssd_conv_w[0]
    conv_hi = conv_sh.astype(BF16)
    conv_lo = (conv_sh - conv_hi.astype(F32)).astype(BF16)
    stored = lambda n, pre="": jnp.transpose(args[pre + n][0]) if n in TRANSPOSED else args[pre + n][0]
    shards = {n: stored(n).astype(BF16) for n in BIG}
    rows_full = lambda g: g.reshape(-1, g.shape[2])

    early = _gather_many([shards[n] for n in EARLY] + [jnp.concatenate([conv_hi, conv_lo], axis=0)], "gather_early")
    gw = dict(zip(EARLY, early[:-1]))
    conv_g = early[-1].astype(F32)
    W = {"w_in": _win_pad(rows_full(gw["w_in"]))}
    P = {n: args[n] for n, _ in REPL_W}
    P["ssd_conv_w"] = _cols_full(conv_g[:, :4] + conv_g[:, 4:])

    late, after = {}, early[0]
    for group, names in LATE.items():
        lands = [lax.dynamic_update_slice(lax.empty((N_DEV,) + shards[n].shape, BF16), shards[n][None], (me, 0, 0)) for n in names]
        late[group] = _split_start([shards[n] for n in names], lands, _plan_broadcast, N_DEV - 1,
                                   "gather_" + group + "_start", after=after)
        after = late[group][4]

    def late_weights(group, after):
        _, got = _split_wait(*late[group][:4], after, _plan_broadcast, "gather_" + group + "_wait")
        full = {n: rows_full(g) if n in ROW_SHARDED or n in TRANSPOSED else _cols_full(g) for n, g in zip(LATE[group], got)}
        if group == "mla":
            full = {"mla_w_q_b": _heads_split_t(full["mla_w_q_b"], MLA_NOPE, MLA_ROPE),
                    "mla_w_kv_b": _heads_split(full["mla_w_kv_b"], MLA_NOPE, MLA_V)}
        return full

    def to_blocks(n, g):
        if n == "w_in":
            g = _win_unpad(g)
        elif n == "mla_w_q_b":
            g = _heads_merge_t(g, MLA_NOPE, MLA_ROPE)
        elif n == "mla_w_kv_b":
            g = _heads_merge(g, MLA_NOPE, MLA_V)
        if n in ROW_SHARDED or n in TRANSPOSED:
            return g.reshape(N_DEV, -1, g.shape[1])
        return _cols_split(g)

    flight = {}

    def grads(group, G):
        gl = [to_blocks(n, G[n]) for n in GRAD_GROUPS[group]]
        flight[group] = _split_start(gl, [lax.empty(g.shape, g.dtype) for g in gl], _plan_scatter, N_DEV - 1,
                                     "grads_" + group + "_start", after=flight["small"][4] if group == "in" else None)
        return flight[group][4]

    def small(packed):
        land = lax.dynamic_update_slice(lax.empty((N_DEV,) + packed.shape, F32), packed[None], (me, 0, 0))
        flight["small"] = _split_start([packed], [land], _plan_broadcast, N_DEV - 1, "small_start")

    grad_x, G, packed = _local_step(x[0], p[0, 0], positions[0], loss_target[0], W, P,
                                    comm={"token0": after, "late_weights": late_weights, "grads": grads, "small": small})

    me_arr = me.astype(jnp.int32).reshape(1)
    big_out = {}

    def finish(group, after):
        mine, recv = _split_wait(*flight[group][:4], after, _plan_scatter, "grads_" + group + "_wait")
        for n, g, r in zip(GRAD_GROUPS[group], mine, recv):
            big_out[n] = _adam(r, stored(n), stored(n, "m_"), stored(n, "v_"), "adam_" + n, own=g, own_idx=me_arr)
        return big_out[GRAD_GROUPS[group][-1]][0]

    done = finish("ffn", grad_x)
    _, (small_all,) = _split_wait(*flight["small"][:4], done, _plan_broadcast, "small_wait")
    conv_sum, loss_row, small_out = _adam_small(small_all, [(args[n], args["m_" + n], args["v_" + n]) for n, _ in REPL_W])
    finish("in", finish("mla", done))
    conv_grad = lax.dynamic_slice_in_dim(conv_sum, me * 192, 192, axis=1)
    conv_out = _adam(conv_grad[None], conv_sh, m_ssd_conv_w[0], v_ssd_conv_w[0], "adam_conv")
    small_map = {n: small_out[i] for i, (n, _) in enumerate(REPL_W)}

    def outputs(idx):
        res = []
        for n in WEIGHT_ORDER:
            if n == "ssd_conv_w":
                res.append(conv_out[idx][None])
            elif n in big_out:
                res.append((jnp.transpose(big_out[n][idx]) if n in TRANSPOSED else big_out[n][idx])[None])
            else:
                res.append(small_map[n][idx])
        return res

    return (loss_row[0, 0], grad_x[None], *outputs(0), *outputs(1), *outputs(2), *outputs(3))
```

```python
import functools
import math

import numpy as np
import jax
import jax.numpy as jnp
from jax import lax
from jax.experimental import pallas as pl
from jax.experimental.pallas import tpu as pltpu

F32 = jnp.float32
BF16 = jnp.bfloat16
HI = lax.Precision.HIGHEST

N_DEV = 8
D_MODEL = 1024
PLE_DIM = 256
SSD_HEADS = 16
SSD_HEAD_DIM = 64
SSD_INNER = 1024
SSD_STATE = 128
SSD_XBC = 1536
SSD_CHUNK = 128
MLA_HEADS = 16
MLA_Q_RANK = 384
MLA_KV_RANK = 256
MLA_NOPE = 64
MLA_ROPE = 32
MLA_V = 64
ROPE_BASE = 10000.0
D_FF = 2816
IN_WIDTH = 3248
IN_PAD = 3456
ALPHA = 2.0 ** 0.25
EPS = 1e-6
LN_EPS = 1e-5
ATT_SCALE = 1.0 / math.sqrt(MLA_NOPE + MLA_ROPE)
ADAM_LR, ADAM_B1, ADAM_B2, ADAM_EPS, ADAM_WD, ADAM_STEP = 0.001, 0.9, 0.999, 1e-08, 0.01, 10

LANE = 128
MXU_DIM = 256
MM_TM, MM_TN, MM_TK = 1408, 1408, 2048
ROW_TILE = 256
ATT_TQ = 256

GRAD_DT = BF16

BIG = ("w_in", "mla_w_q_b", "mla_w_kv_b", "w_out", "w_ffn_gate", "w_ffn_up", "w_ffn_down", "w_ple_gate", "w_ple_proj")
EARLY = ("w_in", "mla_w_q_b", "mla_w_kv_b")
LATE = {"out": ("w_out", "w_ple_gate", "w_ple_proj"), "ffn": ("w_ffn_gate", "w_ffn_up", "w_ffn_down")}
GRAD_GROUPS = {"ffn": ("w_ffn_gate", "w_ffn_up", "w_ffn_down", "w_ple_gate", "w_ple_proj", "w_out"),
               "mla": ("mla_w_q_b", "mla_w_kv_b"), "in": ("w_in",)}
ROW_SHARDED = ("w_out", "w_ffn_down", "w_ple_gate")
TRANSPOSED = ("w_in", "mla_w_q_b", "w_ffn_gate", "w_ffn_up")
WEIGHT_ORDER = ("w_in", "ssd_conv_w", "ssd_conv_b", "ssd_dt_bias", "ssd_A_log", "ssd_D", "ssd_norm_w", "mla_q_norm_w",
                "mla_w_q_b", "mla_kv_norm_w", "mla_w_kv_b", "mla_out_norm_w", "w_out", "ln_mix_g", "ln_mix_b",
                "w_ffn_gate", "w_ffn_up", "w_ffn_down", "w_ple_gate", "w_ple_proj", "ln_ffn_g", "ln_ffn_b")


def _tile(dim, cap, prefer=None):
    cands = [t for t in range(LANE, min(cap, dim) + 1, LANE) if dim % t == 0]
    if not cands:
        return dim
    if prefer is None:
        return max(cands)
    fill = lambda t: t / (MXU_DIM * -(-t // MXU_DIM))
    good = min(0.9, max(fill(t) for t in cands))
    return min((t for t in cands if fill(t) >= good), key=lambda t: abs(t - prefer))


def _dot(a, b, dims=(((1,), (0,)), ((), ())), precision=None):
    return lax.dot_general(a, b, dims, preferred_element_type=F32, precision=precision)


_NT = (((1,), (1,)), ((), ()))
_TN = (((0,), (0,)), ((), ()))


def _mm(a, b, *, ta=False, tb=False, add=None, out_dtype=F32, after=None, epilogue=None, name):
    k_dim, m_dim = a.shape if ta else a.shape[::-1]
    n_dim, kb = b.shape if tb else b.shape[::-1]
    assert k_dim == kb
    tm, tn, tk = _tile(m_dim, MM_TM), _tile(n_dim, MM_TN, prefer=1024), _tile(k_dim, MM_TK, prefer=MM_TK)
    nk = k_dim // tk
    dims = (((0 if ta else 1,), (1 if tb else 0,)), ((), ()))
    a_spec = pl.BlockSpec((tk, tm), lambda i, j, k: (k, i)) if ta else pl.BlockSpec((tm, tk), lambda i, j, k: (i, k))
    b_spec = pl.BlockSpec((tn, tk), lambda i, j, k: (j, k)) if tb else pl.BlockSpec((tk, tn), lambda i, j, k: (k, j))
    o_spec = pl.BlockSpec((tm, tn), lambda i, j, k: (i, j))
    epi_fn, epi_in, out_dtypes = epilogue if epilogue else (None, [], [out_dtype])
    tiles = ([add] if add is not None else []) + list(epi_in)
    n_out = len(out_dtypes)

    def body(*refs):
        a_ref, b_ref = refs[:2]
        tile_refs = refs[2:2 + len(tiles)]
        out_refs = refs[len(refs) - n_out - (nk > 1):len(refs) - (nk > 1)]
        part = _dot(a_ref[...].astype(BF16), b_ref[...].astype(BF16), dims)
        if add is not None:
            part_add = lambda v: v + tile_refs[0][...]
        else:
            part_add = lambda v: v

        def write(total):
            extra = [r[...] for r in tile_refs[add is not None:]]
            outs = epi_fn(total, *extra) if epi_fn else (total,)
            for o_ref, val in zip(out_refs, outs):
                o_ref[...] = val.astype(o_ref.dtype)

        if nk == 1:
            write(part_add(part))
            return
        acc = refs[-1]
        k = pl.program_id(2)

        @pl.when(k == 0)
        def _():
            acc[...] = part_add(part)

        @pl.when(k > 0)
        def _():
            acc[...] += part

        @pl.when(k == nk - 1)
        def _():
            write(acc[...])

    ins = [a, b] + tiles + ([after] if after is not None else [])
    specs = [a_spec, b_spec] + [o_spec] * len(tiles) + ([pl.BlockSpec(memory_space=pl.ANY)] if after is not None else [])
    res = pl.pallas_call(
        body, name=name, grid=(m_dim // tm, n_dim // tn, nk), in_specs=specs, out_specs=[o_spec] * n_out,
        out_shape=[jax.ShapeDtypeStruct((m_dim, n_dim), dt) for dt in out_dtypes],
        scratch_shapes=[pltpu.VMEM((tm, tn), F32)] if nk > 1 else [],
        compiler_params=pltpu.CompilerParams(dimension_semantics=("parallel", "parallel", "arbitrary")),
    )(*ins)
    return res if epilogue else res[0]


def _rowwise(fn, rows, consts, out_widths, acc_widths=(), *, name, tr=ROW_TILE):
    row_arrays, row_specs = [], []
    first_arr = rows[0][0] if isinstance(rows[0], tuple) else rows[0]
    s_dim = first_arr.shape[-2]
    tr = min(tr, s_dim)
    for r in rows:
        arr, width, cb = r if isinstance(r, tuple) else (r, r.shape[-1], 0)
        row_arrays.append(arr)
        if arr.ndim == 3:
            row_specs.append(pl.BlockSpec((None, tr, width), functools.partial(lambda i, k: (k, i, 0), k=cb)))
        else:
            row_specs.append(pl.BlockSpec((tr, width), functools.partial(lambda i, cb: (i, cb), cb=cb)))
    const_specs = [pl.BlockSpec(c.shape, lambda i: (0, 0)) for c in consts]
    nr, nc, no, na = len(rows), len(consts), len(out_widths), len(acc_widths)

    def body(*refs):
        ins = [r[...] for r in refs[:nr + nc]]
        res = fn(*ins)
        if not isinstance(res, (tuple, list)):
            res = (res,)
        out_refs = refs[nr + nc:nr + nc + no]
        acc_refs = refs[nr + nc + no:]
        for o_ref, val in zip(out_refs, res[:no]):
            o_ref[...] = val.astype(o_ref.dtype)
        first = pl.program_id(0) == 0
        for a_ref, val in zip(acc_refs, res[no:]):
            @pl.when(first)
            def _(a_ref=a_ref, val=val):
                a_ref[...] = val

            @pl.when(jnp.logical_not(first))
            def _(a_ref=a_ref, val=val):
                a_ref[...] += val

    outs = [w if isinstance(w, tuple) else (w, F32) for w in out_widths]
    out_shape = [jax.ShapeDtypeStruct((s_dim, w), dt) for w, dt in outs]
    out_shape += [jax.ShapeDtypeStruct((1, w), F32) for w in acc_widths]
    out_specs = [pl.BlockSpec((tr, w), lambda i: (i, 0)) for w, _ in outs]
    out_specs += [pl.BlockSpec((1, w), lambda i: (0, 0)) for w in acc_widths]
    res = pl.pallas_call(
        body, name=name, grid=(s_dim // tr,), in_specs=row_specs + const_specs, out_specs=out_specs, out_shape=out_shape,
        compiler_params=pltpu.CompilerParams(dimension_semantics=("arbitrary",)),
    )(*row_arrays, *consts)
    return res


def _colsum(v):
    return jnp.sum(v, axis=0, keepdims=True)


def _rms(u, g):
    return u * lax.rsqrt(jnp.mean(u * u, axis=-1, keepdims=True) + EPS) * g


def _ln(u, g, b):
    mu = jnp.mean(u, axis=-1, keepdims=True)
    d = u - mu
    var = jnp.mean(d * d, axis=-1, keepdims=True)
    return d * lax.rsqrt(var + LN_EPS) * g + b


def _sigmoid(v):
    return 1.0 / (1.0 + jnp.exp(-v))


def _silu(v):
    return v * _sigmoid(v)


def _softplus(v):
    y = jnp.exp(-jnp.abs(v))
    w = 1.0 + y
    log1p = jnp.where(w == 1.0, y, jnp.log(w) * y / jnp.where(w == 1.0, 1.0, w - 1.0))
    return jnp.maximum(v, 0.0) + log1p


def _gate_rms(y, z, w):
    return _rms(y * _silu(z), w)


def _vjp_rows(f):
    def fn(*args):
        prim, ct = args[:-1], args[-1]
        _, pull = jax.vjp(f, *prim)
        return pull(ct)
    return fn


def _conv_pre(cur, prev, w, b, first):
    row = lax.broadcasted_iota(jnp.int32, cur.shape, 0)
    acc = cur * w[3:4, :] + b
    for j in (1, 2, 3):
        tail = jnp.where(first, 0.0, pltpu.roll(prev, j, 0))
        acc = acc + jnp.where(row >= j, pltpu.roll(cur, j, 0), tail) * w[3 - j:4 - j, :]
    return acc


def _conv_fwd(u, ucb, w, b, name="conv_fwd"):
    s_dim, width = u.shape[0], w.shape[1]
    tr = min(ROW_TILE, s_dim)

    def body(cur_ref, prev_ref, w_ref, b_ref, o_ref):
        pre = _conv_pre(cur_ref[...], prev_ref[...], w_ref, b_ref[...], pl.program_id(0) == 0)
        o_ref[...] = _silu(pre)

    return pl.pallas_call(
        body, name=name, grid=(s_dim // tr,),
        in_specs=[pl.BlockSpec((tr, width), lambda i: (i, ucb)),
                  pl.BlockSpec((tr, width), lambda i: (jnp.maximum(i - 1, 0), ucb)),
                  pl.BlockSpec(w.shape, lambda i: (0, 0)), pl.BlockSpec(b.shape, lambda i: (0, 0))],
        out_specs=pl.BlockSpec((tr, width), lambda i: (i, 0)), out_shape=jax.ShapeDtypeStruct((s_dim, width), F32),
        compiler_params=pltpu.CompilerParams(dimension_semantics=("arbitrary",)),
    )(u, u, w, b)


def _conv_bwd_pre(u, ucb, w, b, dact, name="conv_bwd_pre"):
    s_dim, width = u.shape[0], w.shape[1]
    tr = min(ROW_TILE, s_dim)

    def body(cur_ref, prev_ref, w_ref, b_ref, d_ref, da_ref, dw_ref, db_ref):
        first = pl.program_id(0) == 0
        cur, prev = cur_ref[...], prev_ref[...]
        pre = _conv_pre(cur, prev, w_ref, b_ref[...], first)
        sg = _sigmoid(pre)
        da = d_ref[...] * (sg * (1.0 + pre * (1.0 - sg)))
        da_ref[...] = da
        row = lax.broadcasted_iota(jnp.int32, cur.shape, 0)

        @pl.when(first)
        def _():
            dw_ref[...] = jnp.zeros_like(dw_ref)
            db_ref[...] = jnp.zeros_like(db_ref)

        db_ref[...] += _colsum(da)
        dw_ref[3:4, :] += _colsum(da * cur)
        for j in (1, 2, 3):
            tail = jnp.where(first, 0.0, pltpu.roll(prev, j, 0))
            sh = jnp.where(row >= j, pltpu.roll(cur, j, 0), tail)
            dw_ref[3 - j:4 - j, :] += _colsum(da * sh)

    return pl.pallas_call(
        body, name=name, grid=(s_dim // tr,),
        in_specs=[pl.BlockSpec((tr, width), lambda i: (i, ucb)),
                  pl.BlockSpec((tr, width), lambda i: (jnp.maximum(i - 1, 0), ucb)),
                  pl.BlockSpec(w.shape, lambda i: (0, 0)), pl.BlockSpec(b.shape, lambda i: (0, 0)),
                  pl.BlockSpec((tr, width), lambda i: (i, 0))],
        out_specs=[pl.BlockSpec((tr, width), lambda i: (i, 0)), pl.BlockSpec(w.shape, lambda i: (0, 0)),
                   pl.BlockSpec(b.shape, lambda i: (0, 0))],
        out_shape=[jax.ShapeDtypeStruct((s_dim, width), F32), jax.ShapeDtypeStruct(w.shape, F32),
                   jax.ShapeDtypeStruct(b.shape, F32)],
        compiler_params=pltpu.CompilerParams(dimension_semantics=("arbitrary",)),
    )(u, u, w, b, dact)


def _conv_bwd_in(da, w, name="conv_bwd_in"):
    s_dim, width = da.shape
    tr = min(ROW_TILE, s_dim)
    n = s_dim // tr

    def body(cur_ref, nxt_ref, w_ref, o_ref):
        last = pl.program_id(0) == n - 1
        cur, nxt = cur_ref[...], nxt_ref[...]
        row = lax.broadcasted_iota(jnp.int32, cur.shape, 0)
        acc = cur * w_ref[3:4, :]
        for j in (1, 2, 3):
            head = jnp.where(last, 0.0, pltpu.roll(nxt, tr - j, 0))
            acc = acc + jnp.where(row < tr - j, pltpu.roll(cur, tr - j, 0), head) * w_ref[3 - j:4 - j, :]
        o_ref[...] = acc.astype(o_ref.dtype)

    return pl.pallas_call(
        body, name=name, grid=(n,),
        in_specs=[pl.BlockSpec((tr, width), lambda i: (i, 0)), pl.BlockSpec((tr, width), lambda i: (jnp.minimum(i + 1, n - 1), 0)),
                  pl.BlockSpec(w.shape, lambda i: (0, 0))],
        out_specs=pl.BlockSpec((tr, width), lambda i: (i, 0)), out_shape=jax.ShapeDtypeStruct((s_dim, width), BF16),
        compiler_params=pltpu.CompilerParams(dimension_semantics=("arbitrary",)),
    )(da, da, w)


def _sel_dot(a, sel, pieces, dims=(((1,), (0,)), ((), ())), sel_left=False):
    sel = sel.astype(BF16)
    acc, rest = None, a
    for _ in range(pieces):
        piece = rest.astype(BF16)
        rest = rest - piece.astype(F32)
        part = _dot(sel, piece, dims) if sel_left else _dot(piece, sel, dims)
        acc = part if acc is None else acc + part
    return acc


def _ssd_consts():
    L = SSD_CHUNK
    tri = np.tril(np.ones((L, L), np.float32))
    expand = np.zeros((LANE, SSD_INNER), np.float32)
    expand128 = np.zeros((LANE, SSD_HEADS * LANE), np.float32)
    for h in range(SSD_HEADS):
        expand[h, h * SSD_HEAD_DIM:(h + 1) * SSD_HEAD_DIM] = 1.0
        expand128[h, h * LANE:(h + 1) * LANE] = 1.0
    return jnp.asarray(tri), jnp.asarray(expand), jnp.asarray(expand128), jnp.asarray(expand.T.copy())


def _ssd_prep(dt_ref, bias_ref, alog_ref, tri_ref, exp_ref, exp128_ref, cs_s, cst_s, ex_s, csx_s):
    L = SSD_CHUNK
    dt = _softplus(dt_ref[...] + bias_ref[...])
    a = -jnp.exp(alog_ref[...])
    cs = _sel_dot(dt * a, tri_ref[...], 3, sel_left=True)
    cs_s[...] = cs
    cst_s[...] = cs.T
    last = cs_s[L - 1:L, :]
    expand = exp_ref[...]
    ex_s[...] = _sel_dot(jnp.exp(cs), expand, 2)
    f_x = _sel_dot(jnp.exp(last - cs), expand, 2)
    dt_x = _sel_dot(dt, expand, 2)
    csx_s[...] = _sel_dot(cs, exp128_ref[...], 3)
    t_x = ex_s[L - 1:L, :]
    return dt, a, dt_x, f_x, t_x


def _decay_matrix(csx_s, cst_s, h, tril):
    seg = csx_s[:, h * LANE:(h + 1) * LANE] - cst_s[h:h + 1, :]
    return jnp.exp(jnp.where(tril, seg, -jnp.inf))


def _ssd_fwd(xbca, dtr, dtcb, bias, alog, d_x, name="ssd_fwd"):
    s_dim = xbca.shape[0]
    L = SSD_CHUNK
    nc = s_dim // L
    tri, expand, expand128, _ = _ssd_consts()

    def body(xs_ref, b_ref, c_ref, dt_ref, bias_ref, alog_ref, dx_ref, tri_ref, exp_ref, exp128_ref,
             y_ref, st_ref, st_s, cs_s, cst_s, ex_s, csx_s):
        @pl.when(pl.program_id(0) == 0)
        def _():
            st_s[...] = jnp.zeros_like(st_s)

        dt, a, dt_x, f_x, t_x = _ssd_prep(dt_ref, bias_ref, alog_ref, tri_ref, exp_ref, exp128_ref, cs_s, cst_s, ex_s, csx_s)
        st_ref[0] = st_s[...]
        row = lax.broadcasted_iota(jnp.int32, (L, L), 0)
        col = lax.broadcasted_iota(jnp.int32, (L, L), 1)
        tril = row >= col
        low = col < SSD_HEAD_DIM
        for g in range(2):
            bg = b_ref[:, g * LANE:(g + 1) * LANE]
            cg = c_ref[:, g * LANE:(g + 1) * LANE].astype(BF16)
            gmat = _dot(cg, bg.astype(BF16), _NT)
            bgt = bg.T.astype(BF16)
            for jj in range(4):
                j = 4 * g + jj
                sl = slice(j * LANE, (j + 1) * LANE)
                xp = xs_ref[:, sl]
                x_dt = xp * dt_x[:, sl]
                xb = x_dt.astype(BF16)
                yd = []
                for e in range(2):
                    lm = _decay_matrix(csx_s, cst_s, 2 * j + e, tril)
                    yd.append(_dot((gmat * lm).astype(BF16), xb))
                stp = st_s[j]
                z = _dot(cg, stp.astype(BF16))
                y_ref[:, sl] = jnp.where(low, yd[0], yd[1]) + ex_s[:, sl] * z + dx_ref[:, sl] * xp
                xf = (x_dt * f_x[:, sl]).astype(BF16)
                st_s[j] = t_x[:, sl] * stp + _dot(bgt, xf)

    const = lambda shape: pl.BlockSpec(shape, lambda c: tuple(0 for _ in shape))
    return pl.pallas_call(
        body, name=name, grid=(nc,),
        in_specs=[pl.BlockSpec((L, 1024), lambda c: (c, 0)), pl.BlockSpec((L, 256), lambda c: (c, 4)),
                  pl.BlockSpec((L, 256), lambda c: (c, 5)), pl.BlockSpec((L, LANE), lambda c: (c, dtcb)),
                  const((1, LANE)), const((1, LANE)), const((1, 1024)), const((L, L)), const((LANE, 1024)),
                  const((LANE, 2048))],
        out_specs=[pl.BlockSpec((L, 1024), lambda c: (c, 0)), pl.BlockSpec((1, 8, LANE, LANE), lambda c: (c, 0, 0, 0))],
        out_shape=[jax.ShapeDtypeStruct((s_dim, 1024), F32), jax.ShapeDtypeStruct((nc, 8, LANE, LANE), F32)],
        scratch_shapes=[pltpu.VMEM((8, LANE, LANE), F32), pltpu.VMEM((L, LANE), F32), pltpu.VMEM((LANE, L), F32),
                        pltpu.VMEM((L, 1024), F32), pltpu.VMEM((L, 2048), F32)],
        compiler_params=pltpu.CompilerParams(dimension_semantics=("arbitrary",)),
    )(xbca, xbca, xbca, dtr, bias, alog, d_x, tri, expand, expand128)


def _ssd_bwd(xbca, dtr, dtcb, bias, alog, d_x, states, dy, name="ssd_bwd"):
    s_dim = xbca.shape[0]
    L = SSD_CHUNK
    nc = s_dim // L
    tri, expand, expand128, expand_t = _ssd_consts()

    def body(xs_ref, b_ref, c_ref, dt_ref, bias_ref, alog_ref, dx_ref, tri_ref, exp_ref, exp128_ref, expt_ref,
             st_ref, dy_ref, dxbc_ref, ddt_ref, dbias_ref, dalog_ref, dd_ref,
             dst_s, cs_s, cst_s, ex_s, csx_s, dcsx_s, ddtx_s, dcol_s, drow_s, dlast_s, dd_s):
        @pl.when(pl.program_id(0) == 0)
        def _():
            dst_s[...] = jnp.zeros_like(dst_s)
            dbias_ref[...] = jnp.zeros_like(dbias_ref)
            dalog_ref[...] = jnp.zeros_like(dalog_ref)
            dd_s[...] = jnp.zeros_like(dd_s)

        dt, a, dt_x, f_x, t_x = _ssd_prep(dt_ref, bias_ref, alog_ref, tri_ref, exp_ref, exp128_ref, cs_s, cst_s, ex_s, csx_s)
        row = lax.broadcasted_iota(jnp.int32, (L, L), 0)
        col = lax.broadcasted_iota(jnp.int32, (L, L), 1)
        tril = row >= col
        low = col < SSD_HEAD_DIM
        dcol_s[...] = jnp.zeros_like(dcol_s)
        drow_s[...] = jnp.zeros_like(drow_s)
        for g in range(2):
            bg = b_ref[:, g * LANE:(g + 1) * LANE]
            cg = c_ref[:, g * LANE:(g + 1) * LANE]
            bgb, cgb = bg.astype(BF16), cg.astype(BF16)
            gmat = _dot(cgb, bgb, _NT)
            d_g = jnp.zeros((L, L), F32)
            d_b = jnp.zeros((L, LANE), F32)
            d_c = jnp.zeros((L, LANE), F32)
            for jj in range(4):
                j = 4 * g + jj
                sl = slice(j * LANE, (j + 1) * LANE)
                xp = xs_ref[:, sl]
                dtp = dt_x[:, sl]
                x_dt = xp * dtp
                xb = x_dt.astype(BF16)
                dyp = dy_ref[:, sl]
                dd_s[:, sl] += _colsum(dyp * xp)
                d_xdt = jnp.zeros((L, LANE), F32)
                for e in range(2):
                    h = 2 * j + e
                    lm = _decay_matrix(csx_s, cst_s, h, tril)
                    m = gmat * lm
                    dye = jnp.where(low if e == 0 else jnp.logical_not(low), dyp, 0.0).astype(BF16)
                    d_m = jnp.where(tril, _dot(dye, xb, _NT), 0.0)
                    d_xdt = d_xdt + _dot(m.astype(BF16), dye, _TN)
                    d_g = d_g + d_m * lm
                    w = d_m * m
                    dcol_s[...] += jnp.where(col == h, jnp.sum(w, axis=1, keepdims=True), 0.0)
                    drow_s[...] += jnp.where(row == h, jnp.sum(w, axis=0, keepdims=True), 0.0)
                stp = st_ref[0, j]
                stb = stp.astype(BF16)
                dstn = dst_s[j]
                dstb = dstn.astype(BF16)
                e_p = ex_s[:, sl]
                f_p = f_x[:, sl]
                t_p = t_x[:, sl]
                z = _dot(cgb, stb)
                d_z = (e_p * dyp).astype(BF16)
                d_c = d_c + _dot(d_z, stb, _NT)
                d_xf = _dot(bgb, dstb)
                d_b = d_b + _dot((x_dt * f_p).astype(BF16), dstb, _NT)
                d_xdt = d_xdt + f_p * d_xf
                d_f = x_dt * d_xf * f_p
                dcsx_s[:, sl] = dyp * e_p * z - d_f
                dlast_s[:, sl] = _colsum(d_f) + _colsum(dstn * stp) * t_p
                dst_s[j] = _dot(cgb, d_z, _TN) + t_p * dstn
                dxbc_ref[:, sl] = dx_ref[:, sl] * dyp + d_xdt * dtp
                ddtx_s[:, sl] = d_xdt * xp
            d_gb = d_g.astype(BF16)
            dxbc_ref[:, 1024 + g * LANE:1024 + (g + 1) * LANE] = d_b + _dot(d_gb, cgb, _TN)
            dxbc_ref[:, 1280 + g * LANE:1280 + (g + 1) * LANE] = d_c + _dot(d_gb, bgb)

        expt = expt_ref[...]
        dlast = _sel_dot(jnp.broadcast_to(dlast_s[...], (8, 1024)), expt, 3)
        d_cs = dcol_s[...] - drow_s[...].T + _sel_dot(dcsx_s[...], expt, 3)
        rown = lax.broadcasted_iota(jnp.int32, (L, LANE), 0)
        d_cs = d_cs + jnp.where(rown == L - 1, jnp.sum(dlast, axis=0, keepdims=True) * 0.125, 0.0)
        d_da = _sel_dot(d_cs, tri_ref[...], 3, _TN, sel_left=True)
        d_dt = d_da * a + _sel_dot(ddtx_s[...], expt, 3)
        dalog_ref[...] += _colsum(d_da * dt) * a
        d_raw = d_dt * _sigmoid(dt_ref[...] + bias_ref[...])
        ddt_ref[...] = d_raw.astype(ddt_ref.dtype)
        dbias_ref[...] += _colsum(d_raw)
        dd8 = _sel_dot(jnp.broadcast_to(dd_s[...], (8, 1024)), expt, 3)
        dd_ref[...] = jnp.sum(dd8, axis=0, keepdims=True) * 0.125

    const = lambda shape: pl.BlockSpec(shape, lambda c: tuple(0 for _ in shape))
    rev = lambda cb: (lambda c: (nc - 1 - c, cb))
    return pl.pallas_call(
        body, name=name, grid=(nc,),
        in_specs=[pl.BlockSpec((L, 1024), rev(0)), pl.BlockSpec((L, 256), rev(4)), pl.BlockSpec((L, 256), rev(5)),
                  pl.BlockSpec((L, LANE), rev(dtcb)), const((1, LANE)), const((1, LANE)), const((1, 1024)), const((L, L)),
                  const((LANE, 1024)), const((LANE, 2048)), const((1024, LANE)),
                  pl.BlockSpec((1, 8, LANE, LANE), lambda c: (nc - 1 - c, 0, 0, 0)), pl.BlockSpec((L, 1024), rev(0))],
        out_specs=[pl.BlockSpec((L, SSD_XBC), rev(0)), pl.BlockSpec((L, LANE), rev(0)), const((1, LANE)), const((1, LANE)),
                   const((1, LANE))],
        out_shape=[jax.ShapeDtypeStruct((s_dim, SSD_XBC), F32), jax.ShapeDtypeStruct((s_dim, LANE), BF16),
                   jax.ShapeDtypeStruct((1, LANE), F32), jax.ShapeDtypeStruct((1, LANE), F32),
                   jax.ShapeDtypeStruct((1, LANE), F32)],
        scratch_shapes=[pltpu.VMEM((8, LANE, LANE), F32), pltpu.VMEM((L, LANE), F32), pltpu.VMEM((LANE, L), F32),
                        pltpu.VMEM((L, 1024), F32), pltpu.VMEM((L, 2048), F32), pltpu.VMEM((L, 1024), F32),
                        pltpu.VMEM((L, 1024), F32), pltpu.VMEM((L, LANE), F32), pltpu.VMEM((LANE, L), F32),
                        pltpu.VMEM((1, 1024), F32), pltpu.VMEM((1, 1024), F32)],
        compiler_params=pltpu.CompilerParams(dimension_semantics=("arbitrary",)),
    )(xbca, xbca, xbca, dtr, bias, alog, d_x, tri, expand, expand128, expand_t, states, dy)


def _swap_halves(u):
    width = u.shape[1]
    lane = lax.broadcasted_iota(jnp.int32, u.shape, 1)
    return jnp.where(lane % MLA_ROPE < MLA_ROPE // 2, pltpu.roll(u, width - MLA_ROPE // 2, 1), pltpu.roll(u, MLA_ROPE // 2, 1))


def _rope_fwd_fn(u, cos, sin):
    return u * cos + _swap_halves(u) * sin


def _rope_bwd_fn(d, cos, sin):
    return d * cos + _swap_halves(d * sin)


def _spread4(v):
    return v + pltpu.roll(v, 32, 1) + pltpu.roll(v, 64, 1) + pltpu.roll(v, 96, 1)


def _att_masks(tq):
    lane = lax.broadcasted_iota(jnp.int32, (tq, LANE), 1)
    return lane // MLA_NOPE, lane // MLA_ROPE


def _att_tile(i, tq):
    klen = (i + 1) * tq
    qpos = i * tq + lax.broadcasted_iota(jnp.int32, (tq, klen), 0)
    kpos = lax.broadcasted_iota(jnp.int32, (tq, klen), 1)
    return slice(i * tq, (i + 1) * tq), klen, qpos >= kpos


def _att_qcat(qn_t, qr_t, par, e, half_id, grp_id):
    return jnp.concatenate([jnp.where(half_id == par, qn_t * ATT_SCALE, 0.0), jnp.where(grp_id == e, qr_t * ATT_SCALE, 0.0)],
                           axis=1).astype(BF16)


def _att_softmax(scores, causal):
    s = jnp.where(causal, scores, -jnp.inf)
    e = jnp.exp(s - jnp.max(s, axis=1, keepdims=True))
    return e, 1.0 / jnp.sum(e, axis=1, keepdims=True)


def _att_specs(s_dim):
    col = lambda f: pl.BlockSpec((s_dim, LANE), lambda j: (0, f(j)))
    return [col(lambda j: j), col(lambda j: j // 2), col(lambda j: j), col(lambda j: 0), col(lambda j: 8 + j)]


def _att_fwd(q, qr, kv, krt, name="att_fwd"):
    s_dim = q.shape[0]
    tq = min(ATT_TQ, s_dim)

    def body(qn_ref, qr_ref, kn_ref, krt_ref, v_ref, o_ref, kcat_s, vb_s):
        e0 = 2 * (pl.program_id(0) % 2)
        half_id, grp_id = _att_masks(tq)
        kcat_s[...] = jnp.concatenate([kn_ref[...], krt_ref[...]], axis=1).astype(BF16)
        vb_s[...] = v_ref[...].astype(BF16)
        for i in range(s_dim // tq):
            rows, klen, causal = _att_tile(i, tq)
            qn_t, qr_t = qn_ref[rows, :], qr_ref[rows, :]
            scores = [_dot(_att_qcat(qn_t, qr_t, par, e0 + par, half_id, grp_id), kcat_s[0:klen, :], _NT) for par in range(2)]
            probs = [_att_softmax(s, causal) for s in scores]
            outs = [_dot(e.astype(BF16), vb_s[0:klen, :]) * inv_l for e, inv_l in probs]
            o_ref[rows, :] = jnp.where(half_id == 0, outs[0], outs[1])

    return pl.pallas_call(
        body, name=name, grid=(MLA_HEADS // 2,), in_specs=_att_specs(s_dim),
        out_specs=pl.BlockSpec((s_dim, LANE), lambda j: (0, j)), out_shape=jax.ShapeDtypeStruct((s_dim, 1024), F32),
        scratch_shapes=[pltpu.VMEM((s_dim, 2 * LANE), BF16), pltpu.VMEM((s_dim, LANE), BF16)],
        compiler_params=pltpu.CompilerParams(dimension_semantics=("parallel",)),
    )(q, qr, kv, krt, kv)


def _att_bwd(q, qr, kv, krt, o, do, name="att_bwd"):
    s_dim = q.shape[0]
    tq = min(ATT_TQ, s_dim)

    def body(qn_ref, qr_ref, kn_ref, krt_ref, v_ref, o_ref, do_ref, dqn_ref, dqr_ref, dkn_ref, dv_ref, dkrt_ref,
             kcat_s, vb_s):
        e0 = 2 * (pl.program_id(0) % 2)
        half_id, grp_id = _att_masks(tq)
        kcat_s[...] = jnp.concatenate([kn_ref[...], krt_ref[...]], axis=1).astype(BF16)
        vb_s[...] = v_ref[...].astype(BF16)
        dkn_ref[...] = jnp.zeros_like(dkn_ref)
        dv_ref[...] = jnp.zeros_like(dv_ref)
        dkrt_ref[...] = jnp.zeros_like(dkrt_ref)
        for i in range(s_dim // tq):
            rows, klen, causal = _att_tile(i, tq)
            qn_t, qr_t, o_t, do_t = qn_ref[rows, :], qr_ref[rows, :], o_ref[rows, :], do_ref[rows, :]
            heads = range(2)
            qcats = [_att_qcat(qn_t, qr_t, par, e0 + par, half_id, grp_id) for par in heads]
            scores = [_dot(qcats[par], kcat_s[0:klen, :], _NT) for par in heads]
            doms = [jnp.where(half_id == par, do_t, 0.0) for par in heads]
            dombs = [d.astype(BF16) for d in doms]
            d_ps = [_dot(dombs[par], vb_s[0:klen, :], _NT) for par in heads]
            probs = []
            for par in heads:
                e, inv_l = _att_softmax(scores[par], causal)
                probs.append(e * inv_l)
            d_ss = []
            for par in heads:
                d_row = jnp.sum(doms[par] * o_t, axis=1, keepdims=True)
                d_ss.append((probs[par] * (d_ps[par] - d_row)).astype(BF16))
            dqcats = [_dot(d_ss[par], kcat_s[0:klen, :]) * ATT_SCALE for par in heads]
            dkcats = [_dot(d_ss[par], qcats[par], _TN) for par in heads]
            dvs = [_dot(probs[par].astype(BF16), dombs[par], _TN) for par in heads]
            dqn_ref[rows, :] = jnp.where(half_id == 0, dqcats[0][:, :LANE], dqcats[1][:, :LANE]).astype(dqn_ref.dtype)
            dqr_ref[rows, :] = (jnp.where(grp_id == e0, dqcats[0][:, LANE:], 0.0)
                                + jnp.where(grp_id == e0 + 1, dqcats[1][:, LANE:], 0.0))
            dkn_ref[0:klen, :] += dkcats[0][:, :LANE] + dkcats[1][:, :LANE]
            dkrt_ref[0:klen, :] += dkcats[0][:, LANE:] + dkcats[1][:, LANE:]
            dv_ref[0:klen, :] += dvs[0] + dvs[1]

    col = lambda f: pl.BlockSpec((s_dim, LANE), lambda j: (0, f(j)))
    return pl.pallas_call(
        body, name=name, grid=(MLA_HEADS // 2,), in_specs=_att_specs(s_dim) + [col(lambda j: j), col(lambda j: j)],
        out_specs=[col(lambda j: j), pl.BlockSpec((None, s_dim, LANE), lambda j: (j % 2, 0, j // 2)), col(lambda j: j),
                   col(lambda j: j), pl.BlockSpec((None, s_dim, LANE), lambda j: (j, 0, 0))],
        out_shape=[jax.ShapeDtypeStruct((s_dim, 1024), BF16), jax.ShapeDtypeStruct((2, s_dim, 512), F32),
                   jax.ShapeDtypeStruct((s_dim, 1024), F32), jax.ShapeDtypeStruct((s_dim, 1024), F32),
                   jax.ShapeDtypeStruct((MLA_HEADS // 2, s_dim, LANE), F32)],
        scratch_shapes=[pltpu.VMEM((s_dim, 2 * LANE), BF16), pltpu.VMEM((s_dim, LANE), BF16)],
        compiler_params=pltpu.CompilerParams(dimension_semantics=("parallel",)),
    )(q, qr, kv, krt, kv, o, do)


def _gather_many(shards, name):
    n_arr = len(shards)

    def body(*refs):
        x_refs, out_refs = refs[:n_arr], refs[n_arr:2 * n_arr]
        send_sems, recv_sems, local_sems = refs[2 * n_arr:]
        x_i, y_i, c_i = lax.axis_index("x"), lax.axis_index("y"), lax.axis_index("c")
        me, sibling = (x_i, y_i, c_i), (x_i, y_i, 1 - c_i)
        chips = [(1 - x_i, y_i), (x_i, 1 - y_i), (1 - x_i, 1 - y_i)]

        def copy(a, k, block, to, src=None):
            slot = out_refs[a].at[4 * block[0] + 2 * block[1] + block[2]]
            return pltpu.make_async_remote_copy(
                src_ref=slot if src is None else src, dst_ref=slot, send_sem=send_sems.at[a, k],
                recv_sem=recv_sems.at[a, k], device_id=to, device_id_type=pl.DeviceIdType.MESH)

        mine, first, passed = [], [], []
        for a in range(n_arr):
            mine.append(pltpu.make_async_copy(x_refs[a], out_refs[a].at[4 * x_i + 2 * y_i + c_i], local_sems.at[a]))
            mine[a].start()
            first.append([copy(a, 0, me, sibling, src=x_refs[a])]
                         + [copy(a, 1 + j, me, (*chip, c_i), src=x_refs[a]) for j, chip in enumerate(chips)])
            for cp in first[a]:
                cp.start()
            passed.append([copy(a, 4 + j, (*chip, c_i), sibling) for j, chip in enumerate(chips)])
        for j, chip in enumerate(chips):
            for a in range(n_arr):
                copy(a, 1 + j, (*chip, c_i), me).wait_recv()
                passed[a][j].start()
        for a in range(n_arr):
            copy(a, 0, sibling, me).wait_recv()
            for j, chip in enumerate(chips):
                copy(a, 4 + j, (*chip, 1 - c_i), me).wait_recv()
        for a in range(n_arr):
            for cp in first[a] + passed[a]:
                cp.wait_send()
            mine[a].wait()

    any_spec = pl.BlockSpec(memory_space=pl.ANY)
    return pl.pallas_call(
        body, name=name, out_shape=[jax.ShapeDtypeStruct((N_DEV,) + x.shape, x.dtype) for x in shards],
        in_specs=[any_spec] * n_arr, out_specs=[any_spec] * n_arr,
        scratch_shapes=[pltpu.SemaphoreType.DMA((n_arr, 7)), pltpu.SemaphoreType.DMA((n_arr, 7)),
                        pltpu.SemaphoreType.DMA((n_arr,))],
    )(*shards)


_HBM = pl.BlockSpec(memory_space=pltpu.HBM)
_SEM = pl.BlockSpec(memory_space=pltpu.SEMAPHORE)


def _plan_copies(plan, src_refs, land_refs, send_sems, recv_sems):
    copies = []
    for s_ref, l_ref in zip(src_refs, land_refs):
        for src, dst, peer in plan(s_ref, l_ref):
            k = len(copies)
            copies.append(pltpu.make_async_remote_copy(
                src_ref=src, dst_ref=dst, send_sem=send_sems.at[k], recv_sem=recv_sems.at[k], device_id=peer,
                device_id_type=pl.DeviceIdType.MESH))
    return copies


def _split_start(srcs, lands, plan, n_copy, name, after=None):
    n = len(srcs)
    n_in = 2 * n + (after is not None)

    def body(*refs):
        for cp in _plan_copies(plan, refs[:n], refs[n:2 * n], refs[n_in], refs[n_in + 1]):
            cp.start()
        refs[-1][...] = jnp.zeros_like(refs[-1])

    sems = pltpu.SemaphoreType.DMA((n * n_copy,))
    res = pl.pallas_call(
        body, name=name,
        out_shape=(sems, sems, *[pltpu.HBM(a.shape, a.dtype) for a in list(srcs) + list(lands)],
                   jax.ShapeDtypeStruct((8, LANE), F32)),
        in_specs=[_HBM] * (2 * n) + [pl.BlockSpec(memory_space=pl.ANY)] * (after is not None),
        out_specs=(_SEM, _SEM, *[_HBM] * (2 * n), pl.BlockSpec(memory_space=pltpu.VMEM)),
        input_output_aliases={i: 2 + i for i in range(2 * n)},
        compiler_params=pltpu.CompilerParams(has_side_effects=pltpu.SideEffectType.DATAFLOW_SIDE_EFFECTING),
    )(*[pltpu.with_memory_space_constraint(a, pltpu.HBM) for a in list(srcs) + list(lands)],
      *([after] if after is not None else []))
    return res[0], res[1], list(res[2:2 + n]), list(res[2 + n:2 + 2 * n]), res[-1]


def _split_wait(send_sems, recv_sems, srcs, lands, after, plan, name):
    n = len(srcs)

    def body(*refs):
        copies = _plan_copies(plan, refs[:n], refs[n:2 * n], refs[2 * n], refs[2 * n + 1])
        for cp in copies:
            cp.wait_send()
        for cp in copies:
            cp.wait_recv()

    res = pl.pallas_call(
        body, name=name, out_shape=tuple(pltpu.HBM(a.shape, a.dtype) for a in list(srcs) + list(lands)),
        in_specs=[_HBM] * (2 * n) + [_SEM, _SEM, pl.BlockSpec(memory_space=pl.ANY)], out_specs=tuple([_HBM] * (2 * n)),
        input_output_aliases={i: i for i in range(2 * n)},
        compiler_params=pltpu.CompilerParams(has_side_effects=pltpu.SideEffectType.DATAFLOW_SIDE_EFFECTING),
    )(*srcs, *lands, send_sems, recv_sems, after)
    return list(res[:n]), list(res[n:])


def _plan_broadcast(src, land):
    x_i, y_i, c_i = lax.axis_index("x"), lax.axis_index("y"), lax.axis_index("c")
    me = 4 * x_i + 2 * y_i + c_i
    return [(src, land.at[me], (x_i ^ (k >> 2), y_i ^ ((k >> 1) & 1), c_i ^ (k & 1))) for k in range(1, N_DEV)]


def _plan_scatter(src, land):
    x_i, y_i, c_i = lax.axis_index("x"), lax.axis_index("y"), lax.axis_index("c")
    me = 4 * x_i + 2 * y_i + c_i
    plan = []
    for k in range(1, N_DEV):
        px, py, pc = x_i ^ (k >> 2), y_i ^ ((k >> 1) & 1), c_i ^ (k & 1)
        plan.append((src.at[4 * px + 2 * py + pc], land.at[me], (px, py, pc)))
    return plan


def _adam_math(g, w, m, v):
    m_new = ADAM_B1 * m + (1.0 - ADAM_B1) * g
    v_new = ADAM_B2 * v + (1.0 - ADAM_B2) * (g * g)
    m_hat = m_new / (1.0 - ADAM_B1 ** ADAM_STEP)
    v_hat = v_new / (1.0 - ADAM_B2 ** ADAM_STEP)
    return -ADAM_LR * (m_hat / (jnp.sqrt(v_hat) + ADAM_EPS) + ADAM_WD * w), m_new, v_new


def _adam(slots, w, m, v, name, own=None, own_idx=None):
    n_slot, rows, cols = slots.shape
    tr = ROW_TILE if rows % ROW_TILE == 0 else rows
    has_own = own is not None

    def body(*refs):
        if has_own:
            idx_ref, own_ref, refs = refs[0], refs[1], refs[2:]
        s_ref, w_ref, m_ref, v_ref, g_ref, d_ref, mo_ref, vo_ref = refs
        g = own_ref[...].astype(F32) if has_own else s_ref[0].astype(F32)
        for k in range(0 if has_own else 1, n_slot):
            part = s_ref[k].astype(F32)
            g = g + (jnp.where(idx_ref[0] == k, 0.0, part) if has_own else part)
        g_ref[...] = g
        d_ref[...], mo_ref[...], vo_ref[...] = _adam_math(g, w_ref[...], m_ref[...], v_ref[...])

    spec = pl.BlockSpec((tr, cols), lambda i, *_: (i, 0))
    in_specs = [pl.BlockSpec((n_slot, tr, cols), lambda i, *_: (0, i, 0)), spec, spec, spec]
    if has_own:
        in_specs = [pl.BlockSpec((None, tr, cols), lambda i, idx: (idx[0], i, 0))] + in_specs
    grid_spec = pltpu.PrefetchScalarGridSpec(num_scalar_prefetch=1 if has_own else 0, grid=(rows // tr,), in_specs=in_specs,
                                             out_specs=[spec] * 4)
    ins = ([own_idx, own] if has_own else []) + [slots, w, m, v]
    return pl.pallas_call(
        body, name=name, grid_spec=grid_spec, out_shape=[jax.ShapeDtypeStruct((rows, cols), F32)] * 4,
        compiler_params=pltpu.CompilerParams(dimension_semantics=("parallel",)),
    )(*ins)


PACK_ROWS, PACK_W = 24, 1536
REPL_W = (("ssd_conv_b", 1536), ("ssd_dt_bias", 16), ("ssd_A_log", 16), ("ssd_D", 16), ("ssd_norm_w", 1024),
          ("mla_q_norm_w", 384), ("mla_kv_norm_w", 256), ("mla_out_norm_w", 1024), ("ln_mix_g", 1024),
          ("ln_mix_b", 1024), ("ln_ffn_g", 1024), ("ln_ffn_b", 1024))
LOSS_ROW = 4 + len(REPL_W)


def _pack_small(conv_w_grad, grads, loss, name="pack_small"):
    def body(*refs):
        cw_ref, g_refs, loss_ref, o_ref = refs[0], refs[1:1 + len(REPL_W)], refs[1 + len(REPL_W)], refs[-1]
        o_ref[...] = jnp.zeros_like(o_ref)
        o_ref[0:4, :] = cw_ref[...]
        for i, g_ref in enumerate(g_refs):
            o_ref[4 + i:5 + i, 0:g_ref.shape[1]] = g_ref[...]
        o_ref[LOSS_ROW:LOSS_ROW + 1, 0:LANE] = loss_ref[...]

    return pl.pallas_call(body, name=name, out_shape=jax.ShapeDtypeStruct((PACK_ROWS, PACK_W), F32))(conv_w_grad, *grads, loss)


def _adam_small(gathered, wmv, name="adam_small"):
    def body(*refs):
        s_ref = refs[0]
        in_refs = refs[1:1 + 3 * len(REPL_W)]
        cw_ref, loss_ref = refs[1 + 3 * len(REPL_W)], refs[2 + 3 * len(REPL_W)]
        out_refs = refs[3 + 3 * len(REPL_W):-1]
        tot = refs[-1]
        acc = s_ref[0]
        for k in range(1, N_DEV):
            acc = acc + s_ref[k]
        tot[...] = acc
        cw_ref[...] = tot[0:4, :]
        loss_ref[...] = tot[LOSS_ROW:LOSS_ROW + 1, 0:LANE]
        for i, (_, width) in enumerate(REPL_W):
            g = tot[4 + i:5 + i, 0:width]
            w_ref, m_ref, v_ref = in_refs[3 * i:3 * i + 3]
            g_ref, d_ref, mo_ref, vo_ref = out_refs[4 * i:4 * i + 4]
            g_ref[...] = g
            d_ref[...], mo_ref[...], vo_ref[...] = _adam_math(g, w_ref[...], m_ref[...], v_ref[...])

    flat_in = [a for triple in wmv for a in triple]
    out_shape = [jax.ShapeDtypeStruct((4, PACK_W), F32), jax.ShapeDtypeStruct((1, LANE), F32)]
    for _, width in REPL_W:
        out_shape += [jax.ShapeDtypeStruct((1, width), F32)] * 4
    res = pl.pallas_call(body, name=name, out_shape=out_shape, scratch_shapes=[pltpu.VMEM((PACK_ROWS, PACK_W), F32)])(
        gathered, *flat_in)
    return res[0], res[1], [res[2 + 4 * i:6 + 4 * i] for i in range(len(REPL_W))]


def _cols_full(g):
    return jnp.transpose(g, (1, 0, 2)).reshape(g.shape[1], -1)


def _cols_split(full):
    k_dim, n_dim = full.shape
    return jnp.transpose(full.reshape(k_dim, N_DEV, n_dim // N_DEV), (1, 0, 2))


PROJ_BLOCK = {"z": (1024, 0), "dt": (LANE, 8), "q_c": (MLA_Q_RANK, 3), "xbc": (SSD_XBC, 1), "kv_c": (MLA_KV_RANK, 12),
              "k_rope": (LANE, 26)}


def _win_pad(wt):
    z = lambda n: jnp.zeros((n, wt.shape[1]), wt.dtype)
    return jnp.concatenate([wt[:1024], wt[2560:2576], z(112), wt[2576:2960], wt[1024:2560], wt[2960:3216], wt[3216:3248],
                            z(96)], axis=0)


def _win_unpad(wt):
    return jnp.concatenate([wt[:1024], wt[1536:3072], wt[1024:1040], wt[1152:1536], wt[3072:3328], wt[3328:3360]], axis=0)


def _heads_split_t(wt, a, b):
    w3 = wt.reshape(MLA_HEADS, a + b, wt.shape[1])
    return jnp.concatenate([w3[:, :a].reshape(-1, wt.shape[1]), w3[:, a:].reshape(-1, wt.shape[1])], axis=0)


def _heads_merge_t(wt, a, b):
    wa = wt[:MLA_HEADS * a].reshape(MLA_HEADS, a, wt.shape[1])
    wb = wt[MLA_HEADS * a:].reshape(MLA_HEADS, b, wt.shape[1])
    return jnp.concatenate([wa, wb], axis=1).reshape(-1, wt.shape[1])


def _heads_split(w, a, b):
    k_dim = w.shape[0]
    w3 = w.reshape(k_dim, MLA_HEADS, a + b)
    return jnp.concatenate([w3[:, :, :a].reshape(k_dim, -1), w3[:, :, a:].reshape(k_dim, -1)], axis=1)


def _heads_merge(w, a, b):
    k_dim = w.shape[0]
    wa = w[:, :MLA_HEADS * a].reshape(k_dim, MLA_HEADS, a)
    wb = w[:, MLA_HEADS * a:].reshape(k_dim, MLA_HEADS, b)
    return jnp.concatenate([wa, wb], axis=2).reshape(k_dim, -1)


def _pad_lanes(v, width=LANE):
    return jnp.concatenate([v, jnp.zeros((v.shape[0], width - v.shape[1]), v.dtype)], axis=1)


def _local_step(x, p, positions, tgt, W, P, comm=None):
    comm = comm or {}
    zero_tok = jnp.zeros((8, LANE), F32)
    s_dim = x.shape[0]
    inv_freq = 1.0 / (ROPE_BASE ** (jnp.arange(0, MLA_ROPE, 2, dtype=F32) / MLA_ROPE))
    ang = positions.astype(F32)[:, None] * inv_freq
    cos, sin = jnp.cos(ang), jnp.sin(ang)
    cos32 = jnp.concatenate([cos, cos], axis=1)
    sin32 = jnp.concatenate([-sin, sin], axis=1)
    cos512, sin512 = jnp.tile(cos32, (1, 16)), jnp.tile(sin32, (1, 16))
    cos128, sin128 = jnp.tile(cos32, (1, 4)), jnp.tile(sin32, (1, 4))
    bias_p, alog_p = _pad_lanes(P["ssd_dt_bias"]), _pad_lanes(P["ssd_A_log"])
    d_x = jnp.repeat(P["ssd_D"], SSD_HEAD_DIM, axis=1)

    xb, pb = x.astype(BF16), p.astype(BF16)
    proj = _mm(xb, W["w_in"], tb=True, after=comm.get("token0", zero_tok), name="mm_in")
    z, qc, kvc, kr = [(proj,) + PROJ_BLOCK[n] for n in ("z", "q_c", "kv_c", "k_rope")]
    xbca = _conv_fwd(proj, PROJ_BLOCK["xbc"][1], P["ssd_conv_w"], P["ssd_conv_b"])
    y, states = _ssd_fwd(xbca, proj, PROJ_BLOCK["dt"][1], bias_p, alog_p, d_x)
    (yssd,) = _rowwise(_gate_rms, [y, z], [P["ssd_norm_w"]], [(1024, BF16)], name="ssd_gate_norm")
    qn, kvn, krt = _rowwise(lambda a, c, u, cs, sn, wq, wkv: (_rms(a, wq), _rms(c, wkv), _spread4(_rope_fwd_fn(u, cs, sn))),
                            [qc, kvc, kr, cos128, sin128], [P["mla_q_norm_w"], P["mla_kv_norm_w"]],
                            [(MLA_Q_RANK, BF16), (MLA_KV_RANK, BF16), LANE], name="qkv_norm_rope_k")
    q = _mm(qn, W["mla_w_q_b"], tb=True, name="mm_q")
    kv = _mm(kvn, W["mla_w_kv_b"], name="mm_kv")
    (qr,) = _rowwise(_rope_fwd_fn, [(q, 512, 2), cos512, sin512], [], [512], name="rope_q")
    att = _att_fwd(q, qr, kv, krt)
    (ymla,) = _rowwise(_rms, [att], [P["mla_out_norm_w"]], [(1024, BF16)], name="out_norm")
    ycat = jnp.concatenate([yssd, ymla], axis=1)
    if "late_weights" in comm:
        W = {**W, **comm["late_weights"]("out", ycat)}
    mix = _mm(ycat, W["w_out"], name="mm_out")
    f_h1 = lambda xv, mv, g, b: _ln(ALPHA * xv + mv, g, b)
    h1, h1b = _rowwise(lambda *a: (f_h1(*a),) * 2, [x, mix], [P["ln_mix_g"], P["ln_mix_b"]], [1024, (1024, BF16)],
                       name="ln_mix")
    if "late_weights" in comm:
        W = {**W, **comm["late_weights"]("ffn", h1b)}
    hg = _mm(h1b, W["w_ffn_gate"], tb=True, out_dtype=BF16, name="mm_gate")
    hu, act = _mm(h1b, W["w_ffn_up"], tb=True, name="mm_up",
                  epilogue=(lambda u, g: (u, _silu(g.astype(F32)) * u), [hg], [BF16, BF16]))
    pg = _mm(h1b, W["w_ple_gate"], name="mm_ple_gate")
    pp = _mm(pb, W["w_ple_proj"], name="mm_ple")
    ffn = _mm(act, W["w_ffn_down"], name="mm_down")

    f_h2 = lambda hv, fv, pg, ppv, g, b: _ln(ALPHA * hv + fv + _sigmoid(pg) * ppv, g, b)

    def final_fn(hv, fv, pg, ppv, tv, g, b):
        h2, pull = jax.vjp(f_h2, hv, fv, pg, ppv, g, b)
        diff = h2 - tv
        loss = 0.5 * jnp.sum(jnp.mean(diff * diff, axis=-1, keepdims=True), axis=0, keepdims=True)
        d_h, d_f, d_pg, d_pp, d_g, d_b = pull(diff * (1.0 / D_MODEL))
        return d_h, d_f, d_pg, d_pp, d_g, d_b, jnp.broadcast_to(loss, (1, LANE))

    dh1_a, dffn, dpg, dpp, g_ffn_g, g_ffn_b, loss = _rowwise(
        final_fn, [h1, ffn, pg, pp, tgt], [P["ln_ffn_g"], P["ln_ffn_b"]], [1024] + [(1024, BF16)] * 3,
        [1024, 1024, LANE], name="final")

    G = {}
    def swiglu_bwd(d, g, u):
        g, u = g.astype(F32), u.astype(F32)
        sg = _sigmoid(g)
        return d * u * (sg * (1.0 + g * (1.0 - sg))), d * (g * sg)

    dg, du = _mm(dffn, W["w_ffn_down"], tb=True, name="mm_down_dx",
                 epilogue=(swiglu_bwd, [hg, hu], [BF16, BF16]))
    G["w_ffn_down"] = _mm(act, dffn, ta=True, out_dtype=GRAD_DT, name="mm_down_dw")
    dh1 = _mm(dg, W["w_ffn_gate"], add=dh1_a, name="mm_gate_dx")
    dh1 = _mm(du, W["w_ffn_up"], add=dh1, name="mm_up_dx")
    dh1 = _mm(dpg, W["w_ple_gate"], tb=True, add=dh1, name="mm_ple_gate_dx")
    G["w_ffn_gate"] = _mm(dg, h1b, ta=True, out_dtype=GRAD_DT, name="mm_gate_dw")
    G["w_ffn_up"] = _mm(du, h1b, ta=True, out_dtype=GRAD_DT, name="mm_up_dw")
    G["w_ple_gate"] = _mm(h1b, dpg, ta=True, out_dtype=GRAD_DT, name="mm_ple_gate_dw")
    G["w_ple_proj"] = _mm(pb, dpp, ta=True, out_dtype=GRAD_DT, name="mm_ple_dw")
    dx_a, dmix, g_mix_g, g_mix_b = _rowwise(
        lambda xv, mv, dv, g, b: _vjp_rows(f_h1)(xv, mv, g, b, dv), [x, mix, dh1], [P["ln_mix_g"], P["ln_mix_b"]],
        [1024, (1024, BF16)], [1024, 1024], name="ln_mix_bwd")
    dycat = _mm(dmix, W["w_out"], tb=True, name="mm_out_dx")
    G["w_out"] = _mm(ycat, dmix, ta=True, out_dtype=GRAD_DT, name="mm_out_dw")

    grads_done = comm.get("grads", lambda group, grads: zero_tok)
    tok1 = grads_done("ffn", G)
    datt, g_out_norm = _rowwise(lambda a, dv, w, t: _vjp_rows(_rms)(a, w, dv + jnp.min(t)), [att, (dycat, 1024, 1)],
                                [P["mla_out_norm_w"], tok1], [1024], [1024], name="out_norm_bwd")
    dqn_nope, dqr, dkn, dv, dkrt = _att_bwd(q, qr, kv, krt, att, datt)
    dkv = jnp.concatenate([dkn, dv], axis=1)
    (dq_rope,) = _rowwise(lambda d0, d1, c, s: _rope_bwd_fn(d0 + d1, c, s), [(dqr, 512, 0), (dqr, 512, 1), cos512, sin512],
                          [], [(512, BF16)], name="rope_q_bwd")

    def rope_k_bwd(*a):
        d = _spread4(functools.reduce(lambda u, w: u + w, a[:-2]))
        lane = lax.broadcasted_iota(jnp.int32, d.shape, 1)
        return _rope_bwd_fn(jnp.where(lane < MLA_ROPE, d, 0.0), a[-2], a[-1])

    (dkr,) = _rowwise(rope_k_bwd, [(dkrt, LANE, k) for k in range(MLA_HEADS // 2)] + [cos128, sin128], [], [(LANE, BF16)],
                      name="rope_k_bwd")
    dq = jnp.concatenate([dqn_nope, dq_rope], axis=1)
    dqn = _mm(dq, W["mla_w_q_b"], name="mm_q_dx")
    G["mla_w_q_b"] = _mm(dq, qn, ta=True, out_dtype=GRAD_DT, name="mm_q_dw")
    dkvn = _mm(dkv, W["mla_w_kv_b"], tb=True, name="mm_kv_dx")
    G["mla_w_kv_b"] = _mm(kvn, dkv, ta=True, out_dtype=GRAD_DT, name="mm_kv_dw")
    tok2 = grads_done("mla", G)
    def qkv_norm_bwd(a, da, c, dc, wq, wkv, t):
        (d_a, d_wq), (d_c, d_wkv) = _vjp_rows(_rms)(a, wq, da + jnp.min(t)), _vjp_rows(_rms)(c, wkv, dc)
        return d_a, d_c, d_wq, d_wkv

    dqc, dkvc, g_q_norm, g_kv_norm = _rowwise(
        qkv_norm_bwd, [qc, dqn, kvc, dkvn], [P["mla_q_norm_w"], P["mla_kv_norm_w"], tok2],
        [(MLA_Q_RANK, BF16), (MLA_KV_RANK, BF16)], [MLA_Q_RANK, MLA_KV_RANK], name="qkv_norm_bwd")

    dy, dz, g_ssd_norm = _rowwise(lambda yv, zv, dv, w, t: _vjp_rows(_gate_rms)(yv, zv, w, dv + jnp.min(t)),
                                  [y, z, (dycat, 1024, 0)], [P["ssd_norm_w"], tok1], [1024, (1024, BF16)], [1024],
                                  name="ssd_gate_norm_bwd")
    dxbca, ddtr, g_dt_bias, g_alog, g_d = _ssd_bwd(xbca, proj, PROJ_BLOCK["dt"][1], bias_p, alog_p, d_x, states, dy)
    da, g_conv_w, g_conv_b = _conv_bwd_pre(proj, PROJ_BLOCK["xbc"][1], P["ssd_conv_w"], P["ssd_conv_b"], dxbca)
    dxbc = _conv_bwd_in(da, P["ssd_conv_w"])

    small = {
        "ssd_conv_b": g_conv_b, "ssd_dt_bias": g_dt_bias, "ssd_A_log": g_alog, "ssd_D": g_d, "ssd_norm_w": g_ssd_norm,
        "mla_q_norm_w": g_q_norm, "mla_kv_norm_w": g_kv_norm, "mla_out_norm_w": g_out_norm, "ln_mix_g": g_mix_g,
        "ln_mix_b": g_mix_b, "ln_ffn_g": g_ffn_g, "ln_ffn_b": g_ffn_b,
    }
    packed = _pack_small(g_conv_w, [small[n] for n, _ in REPL_W], loss)
    if "small" in comm:
        comm["small"](packed)

    dproj = jnp.concatenate([dz, ddtr, dqc, dxbc, dkvc, dkr], axis=1)
    G["w_in"] = _mm(dproj, xb, ta=True, out_dtype=GRAD_DT, name="mm_in_dw")
    grad_x = _mm(dproj, W["w_in"], add=dx_a, after=grads_done("in", G), name="mm_in_dx")
    return grad_x, G, packed


def kernel(x, p, positions, w_in, ssd_conv_w, ssd_conv_b, ssd_dt_bias, ssd_A_log, ssd_D, ssd_norm_w, mla_q_norm_w, mla_w_q_b, mla_kv_norm_w, mla_w_kv_b, mla_out_norm_w, w_out, ln_mix_g, ln_mix_b, w_ffn_gate, w_ffn_up, w_ffn_down, w_ple_gate, w_ple_proj, ln_ffn_g, ln_ffn_b, loss_target, m_w_in, m_ssd_conv_w, m_ssd_conv_b, m_ssd_dt_bias, m_ssd_A_log, m_ssd_D, m_ssd_norm_w, m_mla_q_norm_w, m_mla_w_q_b, m_mla_kv_norm_w, m_mla_w_kv_b, m_mla_out_norm_w, m_w_out, m_ln_mix_g, m_ln_mix_b, m_w_ffn_gate, m_w_ffn_up, m_w_ffn_down, m_w_ple_gate, m_w_ple_proj, m_ln_ffn_g, m_ln_ffn_b, v_w_in, v_ssd_conv_w, v_ssd_conv_b, v_ssd_dt_bias, v_ssd_A_log, v_ssd_D, v_ssd_norm_w, v_mla_q_norm_w, v_mla_w_q_b, v_mla_kv_norm_w, v_mla_w_kv_b, v_mla_out_norm_w, v_w_out, v_ln_mix_g, v_ln_mix_b, v_w_ffn_gate, v_w_ffn_up, v_w_ffn_down, v_w_ple_gate, v_w_ple_proj, v_ln_ffn_g, v_ln_ffn_b):
    args = dict(locals())
    core = lax.axis_index("c")
    me = 4 * lax.axis_index("x") + 2 * lax.axis_index("y") + core

    conv_sh = ssd_conv_w[0]
    conv_hi = conv_sh.astype(BF16)
    conv_lo = (conv_sh - conv_hi.astype(F32)).astype(BF16)
    stored = lambda n, pre="": jnp.transpose(args[pre + n][0]) if n in TRANSPOSED else args[pre + n][0]
    shards = {n: stored(n).astype(BF16) for n in BIG}
    rows_full = lambda g: g.reshape(-1, g.shape[2])

    early = _gather_many([shards[n] for n in EARLY] + [jnp.concatenate([conv_hi, conv_lo], axis=0)], "gather_early")
    gw = dict(zip(EARLY, early[:-1]))
    conv_g = early[-1].astype(F32)
    W = {
        "w_in": _win_pad(rows_full(gw["w_in"])),
        "mla_w_q_b": _heads_split_t(rows_full(gw["mla_w_q_b"]), MLA_NOPE, MLA_ROPE),
        "mla_w_kv_b": _heads_split(_cols_full(gw["mla_w_kv_b"]), MLA_NOPE, MLA_V),
    }
    P = {n: args[n] for n, _ in REPL_W}
    P["ssd_conv_w"] = _cols_full(conv_g[:, :4] + conv_g[:, 4:])

    late, after = {}, early[0]
    for group, names in LATE.items():
        lands = [lax.dynamic_update_slice(lax.empty((N_DEV,) + shards[n].shape, BF16), shards[n][None], (me, 0, 0)) for n in names]
        late[group] = _split_start([shards[n] for n in names], lands, _plan_broadcast, N_DEV - 1,
                                   "gather_" + group + "_start", after=after)
        after = late[group][4]

    def late_weights(group, after):
        _, got = _split_wait(*late[group][:4], after, _plan_broadcast, "gather_" + group + "_wait")
        return {n: _cols_full(g) if n == "w_ple_proj" else rows_full(g) for n, g in zip(LATE[group], got)}

    def to_blocks(n, g):
        if n == "w_in":
            g = _win_unpad(g)
        elif n == "mla_w_q_b":
            g = _heads_merge_t(g, MLA_NOPE, MLA_ROPE)
        elif n == "mla_w_kv_b":
            g = _heads_merge(g, MLA_NOPE, MLA_V)
        if n in ROW_SHARDED or n in TRANSPOSED:
            return g.reshape(N_DEV, -1, g.shape[1])
        return _cols_split(g)

    flight = {}

    def grads(group, G):
        gl = [to_blocks(n, G[n]) for n in GRAD_GROUPS[group]]
        flight[group] = _split_start(gl, [lax.empty(g.shape, g.dtype) for g in gl], _plan_scatter, N_DEV - 1,
                                     "grads_" + group + "_start", after=flight["small"][4] if group == "in" else None)
        return flight[group][4]

    def small(packed):
        land = lax.dynamic_update_slice(lax.empty((N_DEV,) + packed.shape, F32), packed[None], (me, 0, 0))
        flight["small"] = _split_start([packed], [land], _plan_broadcast, N_DEV - 1, "small_start")

    grad_x, G, packed = _local_step(x[0], p[0, 0], positions[0], loss_target[0], W, P,
                                    comm={"token0": after, "late_weights": late_weights, "grads": grads, "small": small})

    me_arr = me.astype(jnp.int32).reshape(1)
    big_out = {}

    def finish(group, after):
        mine, recv = _split_wait(*flight[group][:4], after, _plan_scatter, "grads_" + group + "_wait")
        for n, g, r in zip(GRAD_GROUPS[group], mine, recv):
            big_out[n] = _adam(r, stored(n), stored(n, "m_"), stored(n, "v_"), "adam_" + n, own=g, own_idx=me_arr)
        return big_out[GRAD_GROUPS[group][-1]][0]

    done = finish("ffn", grad_x)
    _, (small_all,) = _split_wait(*flight["small"][:4], done, _plan_broadcast, "small_wait")
    conv_sum, loss_row, small_out = _adam_small(small_all, [(args[n], args["m_" + n], args["v_" + n]) for n, _ in REPL_W])
    finish("in", finish("mla", done))
    conv_grad = lax.dynamic_slice_in_dim(conv_sum, me * 192, 192, axis=1)
    conv_out = _adam(conv_grad[None], conv_sh, m_ssd_conv_w[0], v_ssd_conv_w[0], "adam_conv")
    small_map = {n: small_out[i] for i, (n, _) in enumerate(REPL_W)}

    def outputs(idx):
        res = []
        for n in WEIGHT_ORDER:
            if n == "ssd_conv_w":
                res.append(conv_out[idx][None])
            elif n in big_out:
                res.append((jnp.transpose(big_out[n][idx]) if n in TRANSPOSED else big_out[n][idx])[None])
            else:
                res.append(small_map[n][idx])
        return res

    return (loss_row[0, 0], grad_x[None], *outputs(0), *outputs(1), *outputs(2), *outputs(3))
```

```python
import functools
import math

import numpy as np
import jax
import jax.numpy as jnp
from jax import lax
from jax.experimental import pallas as pl
from jax.experimental.pallas import tpu as pltpu

F32 = jnp.float32
BF16 = jnp.bfloat16
HI = lax.Precision.HIGHEST

N_DEV = 8
D_MODEL = 1024
PLE_DIM = 256
SSD_HEADS = 16
SSD_HEAD_DIM = 64
SSD_INNER = 1024
SSD_STATE = 128
SSD_XBC = 1536
SSD_CHUNK = 128
MLA_HEADS = 16
MLA_Q_RANK = 384
MLA_KV_RANK = 256
MLA_NOPE = 64
MLA_ROPE = 32
MLA_V = 64
ROPE_BASE = 10000.0
D_FF = 2816
IN_WIDTH = 3248
IN_PAD = 3456
ALPHA = 2.0 ** 0.25
EPS = 1e-6
LN_EPS = 1e-5
ATT_SCALE = 1.0 / math.sqrt(MLA_NOPE + MLA_ROPE)
ADAM_LR, ADAM_B1, ADAM_B2, ADAM_EPS, ADAM_WD, ADAM_STEP = 0.001, 0.9, 0.999, 1e-08, 0.01, 10

LANE = 128
MXU_DIM = 256
MM_TM, MM_TN, MM_TK = 1408, 1408, 2048
ROW_TILE = 256
ATT_TQ = 256

GRAD_DT = BF16

BIG = ("w_in", "mla_w_q_b", "mla_w_kv_b", "w_out", "w_ffn_gate", "w_ffn_up", "w_ffn_down", "w_ple_gate", "w_ple_proj")
EARLY = ("w_in", "mla_w_q_b", "mla_w_kv_b")
LATE = {"out": ("w_out", "w_ple_gate", "w_ple_proj"), "ffn": ("w_ffn_gate", "w_ffn_up", "w_ffn_down")}
GRAD_GROUPS = {"ffn": ("w_ffn_gate", "w_ffn_up", "w_ffn_down", "w_ple_gate", "w_ple_proj", "w_out"),
               "mla": ("mla_w_q_b", "mla_w_kv_b"), "in": ("w_in",)}
ROW_SHARDED = ("w_out", "w_ffn_down", "w_ple_gate")
TRANSPOSED = ("w_in", "mla_w_q_b", "w_ffn_gate", "w_ffn_up")
WEIGHT_ORDER = ("w_in", "ssd_conv_w", "ssd_conv_b", "ssd_dt_bias", "ssd_A_log", "ssd_D", "ssd_norm_w", "mla_q_norm_w",
                "mla_w_q_b", "mla_kv_norm_w", "mla_w_kv_b", "mla_out_norm_w", "w_out", "ln_mix_g", "ln_mix_b",
                "w_ffn_gate", "w_ffn_up", "w_ffn_down", "w_ple_gate", "w_ple_proj", "ln_ffn_g", "ln_ffn_b")


def _tile(dim, cap, prefer=None):
    cands = [t for t in range(LANE, min(cap, dim) + 1, LANE) if dim % t == 0]
    if not cands:
        return dim
    if prefer is None:
        return max(cands)
    fill = lambda t: t / (MXU_DIM * -(-t // MXU_DIM))
    good = min(0.9, max(fill(t) for t in cands))
    return min((t for t in cands if fill(t) >= good), key=lambda t: abs(t - prefer))


def _dot(a, b, dims=(((1,), (0,)), ((), ())), precision=None):
    return lax.dot_general(a, b, dims, preferred_element_type=F32, precision=precision)


_NT = (((1,), (1,)), ((), ()))
_TN = (((0,), (0,)), ((), ()))


def _mm(a, b, *, ta=False, tb=False, add=None, out_dtype=F32, after=None, epilogue=None, name):
    k_dim, m_dim = a.shape if ta else a.shape[::-1]
    n_dim, kb = b.shape if tb else b.shape[::-1]
    assert k_dim == kb
    tm, tn, tk = _tile(m_dim, MM_TM), _tile(n_dim, MM_TN, prefer=1024), _tile(k_dim, MM_TK, prefer=MM_TK)
    nk = k_dim // tk
    dims = (((0 if ta else 1,), (1 if tb else 0,)), ((), ()))
    a_spec = pl.BlockSpec((tk, tm), lambda i, j, k: (k, i)) if ta else pl.BlockSpec((tm, tk), lambda i, j, k: (i, k))
    b_spec = pl.BlockSpec((tn, tk), lambda i, j, k: (j, k)) if tb else pl.BlockSpec((tk, tn), lambda i, j, k: (k, j))
    o_spec = pl.BlockSpec((tm, tn), lambda i, j, k: (i, j))
    epi_fn, epi_in, out_dtypes = epilogue if epilogue else (None, [], [out_dtype])
    tiles = ([add] if add is not None else []) + list(epi_in)
    n_out = len(out_dtypes)

    def body(*refs):
        a_ref, b_ref = refs[:2]
        tile_refs = refs[2:2 + len(tiles)]
        out_refs = refs[len(refs) - n_out - (nk > 1):len(refs) - (nk > 1)]
        part = _dot(a_ref[...].astype(BF16), b_ref[...].astype(BF16), dims)
        if add is not None:
            part_add = lambda v: v + tile_refs[0][...]
        else:
            part_add = lambda v: v

        def write(total):
            extra = [r[...] for r in tile_refs[add is not None:]]
            outs = epi_fn(total, *extra) if epi_fn else (total,)
            for o_ref, val in zip(out_refs, outs):
                o_ref[...] = val.astype(o_ref.dtype)

        if nk == 1:
            write(part_add(part))
            return
        acc = refs[-1]
        k = pl.program_id(2)

        @pl.when(k == 0)
        def _():
            acc[...] = part_add(part)

        @pl.when(k > 0)
        def _():
            acc[...] += part

        @pl.when(k == nk - 1)
        def _():
            write(acc[...])

    ins = [a, b] + tiles + ([after] if after is not None else [])
    specs = [a_spec, b_spec] + [o_spec] * len(tiles) + ([pl.BlockSpec(memory_space=pl.ANY)] if after is not None else [])
    res = pl.pallas_call(
        body, name=name, grid=(m_dim // tm, n_dim // tn, nk), in_specs=specs, out_specs=[o_spec] * n_out,
        out_shape=[jax.ShapeDtypeStruct((m_dim, n_dim), dt) for dt in out_dtypes],
        scratch_shapes=[pltpu.VMEM((tm, tn), F32)] if nk > 1 else [],
        compiler_params=pltpu.CompilerParams(dimension_semantics=("parallel", "parallel", "arbitrary")),
    )(*ins)
    return res if epilogue else res[0]


def _rowwise(fn, rows, consts, out_widths, acc_widths=(), *, name, tr=ROW_TILE):
    row_arrays, row_specs = [], []
    first_arr = rows[0][0] if isinstance(rows[0], tuple) else rows[0]
    s_dim = first_arr.shape[-2]
    tr = min(tr, s_dim)
    for r in rows:
        arr, width, cb = r if isinstance(r, tuple) else (r, r.shape[-1], 0)
        row_arrays.append(arr)
        if arr.ndim == 3:
            row_specs.append(pl.BlockSpec((None, tr, width), functools.partial(lambda i, k: (k, i, 0), k=cb)))
        else:
            row_specs.append(pl.BlockSpec((tr, width), functools.partial(lambda i, cb: (i, cb), cb=cb)))
    const_specs = [pl.BlockSpec(c.shape, lambda i: (0, 0)) for c in consts]
    nr, nc, no, na = len(rows), len(consts), len(out_widths), len(acc_widths)

    def body(*refs):
        ins = [r[...] for r in refs[:nr + nc]]
        res = fn(*ins)
        if not isinstance(res, (tuple, list)):
            res = (res,)
        out_refs = refs[nr + nc:nr + nc + no]
        acc_refs = refs[nr + nc + no:]
        for o_ref, val in zip(out_refs, res[:no]):
            o_ref[...] = val.astype(o_ref.dtype)
        first = pl.program_id(0) == 0
        for a_ref, val in zip(acc_refs, res[no:]):
            @pl.when(first)
            def _(a_ref=a_ref, val=val):
                a_ref[...] = val

            @pl.when(jnp.logical_not(first))
            def _(a_ref=a_ref, val=val):
                a_ref[...] += val

    outs = [w if isinstance(w, tuple) else (w, F32) for w in out_widths]
    out_shape = [jax.ShapeDtypeStruct((s_dim, w), dt) for w, dt in outs]
    out_shape += [jax.ShapeDtypeStruct((1, w), F32) for w in acc_widths]
    out_specs = [pl.BlockSpec((tr, w), lambda i: (i, 0)) for w, _ in outs]
    out_specs += [pl.BlockSpec((1, w), lambda i: (0, 0)) for w in acc_widths]
    res = pl.pallas_call(
        body, name=name, grid=(s_dim // tr,), in_specs=row_specs + const_specs, out_specs=out_specs, out_shape=out_shape,
        compiler_params=pltpu.CompilerParams(dimension_semantics=("arbitrary",)),
    )(*row_arrays, *consts)
    return res


def _colsum(v):
    return jnp.sum(v, axis=0, keepdims=True)


def _rms(u, g):
    return u * lax.rsqrt(jnp.mean(u * u, axis=-1, keepdims=True) + EPS) * g


def _ln(u, g, b):
    mu = jnp.mean(u, axis=-1, keepdims=True)
    d = u - mu
    var = jnp.mean(d * d, axis=-1, keepdims=True)
    return d * lax.rsqrt(var + LN_EPS) * g + b


def _sigmoid(v):
    return 1.0 / (1.0 + jnp.exp(-v))


def _silu(v):
    return v * _sigmoid(v)


def _softplus(v):
    y = jnp.exp(-jnp.abs(v))
    w = 1.0 + y
    log1p = jnp.where(w == 1.0, y, jnp.log(w) * y / jnp.where(w == 1.0, 1.0, w - 1.0))
    return jnp.maximum(v, 0.0) + log1p


def _gate_rms(y, z, w):
    return _rms(y * _silu(z), w)


def _vjp_rows(f):
    def fn(*args):
        prim, ct = args[:-1], args[-1]
        _, pull = jax.vjp(f, *prim)
        return pull(ct)
    return fn


def _conv_pre(cur, prev, w, b, first):
    row = lax.broadcasted_iota(jnp.int32, cur.shape, 0)
    acc = cur * w[3:4, :] + b
    for j in (1, 2, 3):
        tail = jnp.where(first, 0.0, pltpu.roll(prev, j, 0))
        acc = acc + jnp.where(row >= j, pltpu.roll(cur, j, 0), tail) * w[3 - j:4 - j, :]
    return acc


def _conv_fwd(u, ucb, w, b, name="conv_fwd"):
    s_dim, width = u.shape[0], w.shape[1]
    tr = min(ROW_TILE, s_dim)

    def body(cur_ref, prev_ref, w_ref, b_ref, o_ref):
        pre = _conv_pre(cur_ref[...], prev_ref[...], w_ref, b_ref[...], pl.program_id(0) == 0)
        o_ref[...] = _silu(pre)

    return pl.pallas_call(
        body, name=name, grid=(s_dim // tr,),
        in_specs=[pl.BlockSpec((tr, width), lambda i: (i, ucb)),
                  pl.BlockSpec((tr, width), lambda i: (jnp.maximum(i - 1, 0), ucb)),
                  pl.BlockSpec(w.shape, lambda i: (0, 0)), pl.BlockSpec(b.shape, lambda i: (0, 0))],
        out_specs=pl.BlockSpec((tr, width), lambda i: (i, 0)), out_shape=jax.ShapeDtypeStruct((s_dim, width), F32),
        compiler_params=pltpu.CompilerParams(dimension_semantics=("arbitrary",)),
    )(u, u, w, b)


def _conv_bwd_pre(u, ucb, w, b, dact, name="conv_bwd_pre"):
    s_dim, width = u.shape[0], w.shape[1]
    tr = min(ROW_TILE, s_dim)

    def body(cur_ref, prev_ref, w_ref, b_ref, d_ref, da_ref, dw_ref, db_ref):
        first = pl.program_id(0) == 0
        cur, prev = cur_ref[...], prev_ref[...]
        pre = _conv_pre(cur, prev, w_ref, b_ref[...], first)
        sg = _sigmoid(pre)
        da = d_ref[...] * (sg * (1.0 + pre * (1.0 - sg)))
        da_ref[...] = da
        row = lax.broadcasted_iota(jnp.int32, cur.shape, 0)

        @pl.when(first)
        def _():
            dw_ref[...] = jnp.zeros_like(dw_ref)
            db_ref[...] = jnp.zeros_like(db_ref)

        db_ref[...] += _colsum(da)
        dw_ref[3:4, :] += _colsum(da * cur)
        for j in (1, 2, 3):
            tail = jnp.where(first, 0.0, pltpu.roll(prev, j, 0))
            sh = jnp.where(row >= j, pltpu.roll(cur, j, 0), tail)
            dw_ref[3 - j:4 - j, :] += _colsum(da * sh)

    return pl.pallas_call(
        body, name=name, grid=(s_dim // tr,),
        in_specs=[pl.BlockSpec((tr, width), lambda i: (i, ucb)),
                  pl.BlockSpec((tr, width), lambda i: (jnp.maximum(i - 1, 0), ucb)),
                  pl.BlockSpec(w.shape, lambda i: (0, 0)), pl.BlockSpec(b.shape, lambda i: (0, 0)),
                  pl.BlockSpec((tr, width), lambda i: (i, 0))],
        out_specs=[pl.BlockSpec((tr, width), lambda i: (i, 0)), pl.BlockSpec(w.shape, lambda i: (0, 0)),
                   pl.BlockSpec(b.shape, lambda i: (0, 0))],
        out_shape=[jax.ShapeDtypeStruct((s_dim, width), F32), jax.ShapeDtypeStruct(w.shape, F32),
                   jax.ShapeDtypeStruct(b.shape, F32)],
        compiler_params=pltpu.CompilerParams(dimension_semantics=("arbitrary",)),
    )(u, u, w, b, dact)


def _conv_bwd_in(da, w, name="conv_bwd_in"):
    s_dim, width = da.shape
    tr = min(ROW_TILE, s_dim)
    n = s_dim // tr

    def body(cur_ref, nxt_ref, w_ref, o_ref):
        last = pl.program_id(0) == n - 1
        cur, nxt = cur_ref[...], nxt_ref[...]
        row = lax.broadcasted_iota(jnp.int32, cur.shape, 0)
        acc = cur * w_ref[3:4, :]
        for j in (1, 2, 3):
            head = jnp.where(last, 0.0, pltpu.roll(nxt, tr - j, 0))
            acc = acc + jnp.where(row < tr - j, pltpu.roll(cur, tr - j, 0), head) * w_ref[3 - j:4 - j, :]
        o_ref[...] = acc.astype(o_ref.dtype)

    return pl.pallas_call(
        body, name=name, grid=(n,),
        in_specs=[pl.BlockSpec((tr, width), lambda i: (i, 0)), pl.BlockSpec((tr, width), lambda i: (jnp.minimum(i + 1, n - 1), 0)),
                  pl.BlockSpec(w.shape, lambda i: (0, 0))],
        out_specs=pl.BlockSpec((tr, width), lambda i: (i, 0)), out_shape=jax.ShapeDtypeStruct((s_dim, width), BF16),
        compiler_params=pltpu.CompilerParams(dimension_semantics=("arbitrary",)),
    )(da, da, w)


def _sel_dot(a, sel, pieces, dims=(((1,), (0,)), ((), ())), sel_left=False):
    sel = sel.astype(BF16)
    acc, rest = None, a
    for _ in range(pieces):
        piece = rest.astype(BF16)
        rest = rest - piece.astype(F32)
        part = _dot(sel, piece, dims) if sel_left else _dot(piece, sel, dims)
        acc = part if acc is None else acc + part
    return acc


def _ssd_consts():
    L = SSD_CHUNK
    tri = np.tril(np.ones((L, L), np.float32))
    expand = np.zeros((LANE, SSD_INNER), np.float32)
    for h in range(SSD_HEADS):
        expand[h, h * SSD_HEAD_DIM:(h + 1) * SSD_HEAD_DIM] = 1.0
    return jnp.asarray(tri), jnp.asarray(expand), jnp.asarray(expand.T.copy())


def _ssd_prep(dt_ref, bias_ref, alog_ref, tri_ref, exp_ref, cs_s, cst_s, ex_s):
    L = SSD_CHUNK
    dt = _softplus(dt_ref[...] + bias_ref[...])
    a = -jnp.exp(alog_ref[...])
    cs = _sel_dot(dt * a, tri_ref[...], 3, sel_left=True)
    cs_s[...] = cs
    cst_s[...] = cs.T
    last = cs_s[L - 1:L, :]
    expand = exp_ref[...]
    ex_s[...] = _sel_dot(jnp.exp(cs), expand, 2)
    f_x = _sel_dot(jnp.exp(last - cs), expand, 2)
    dt_x = _sel_dot(dt, expand, 2)
    t_x = ex_s[L - 1:L, :]
    return dt, a, dt_x, f_x, t_x


def _decay_matrix(cs_s, cst_s, h, tril):
    seg = cs_s[:, h:h + 1] - cst_s[h:h + 1, :]
    return jnp.exp(jnp.where(tril, seg, -jnp.inf))


def _ssd_fwd(xbca, dtr, dtcb, bias, alog, d_x, name="ssd_fwd"):
    s_dim = xbca.shape[0]
    L = SSD_CHUNK
    nc = s_dim // L
    tri, expand, _ = _ssd_consts()

    def body(xs_ref, b_ref, c_ref, dt_ref, bias_ref, alog_ref, dx_ref, tri_ref, exp_ref,
             y_ref, st_ref, st_s, cs_s, cst_s, ex_s):
        @pl.when(pl.program_id(0) == 0)
        def _():
            st_s[...] = jnp.zeros_like(st_s)

        dt, a, dt_x, f_x, t_x = _ssd_prep(dt_ref, bias_ref, alog_ref, tri_ref, exp_ref, cs_s, cst_s, ex_s)
        st_ref[0] = st_s[...]
        row = lax.broadcasted_iota(jnp.int32, (L, L), 0)
        col = lax.broadcasted_iota(jnp.int32, (L, L), 1)
        tril = row >= col
        low = col < SSD_HEAD_DIM
        for g in range(2):
            bg = b_ref[:, g * LANE:(g + 1) * LANE]
            cg = c_ref[:, g * LANE:(g + 1) * LANE].astype(BF16)
            gmat = _dot(cg, bg.astype(BF16), _NT)
            bgt = bg.T.astype(BF16)
            for jj in range(4):
                j = 4 * g + jj
                sl = slice(j * LANE, (j + 1) * LANE)
                xp = xs_ref[:, sl]
                x_dt = xp * dt_x[:, sl]
                xb = x_dt.astype(BF16)
                yd = []
                for e in range(2):
                    lm = _decay_matrix(cs_s, cst_s, 2 * j + e, tril)
                    yd.append(_dot((gmat * lm).astype(BF16), xb))
                stp = st_s[j]
                z = _dot(cg, stp.astype(BF16))
                y_ref[:, sl] = jnp.where(low, yd[0], yd[1]) + ex_s[:, sl] * z + dx_ref[:, sl] * xp
                xf = (x_dt * f_x[:, sl]).astype(BF16)
                st_s[j] = t_x[:, sl] * stp + _dot(bgt, xf)

    const = lambda shape: pl.BlockSpec(shape, lambda c: tuple(0 for _ in shape))
    return pl.pallas_call(
        body, name=name, grid=(nc,),
        in_specs=[pl.BlockSpec((L, 1024), lambda c: (c, 0)), pl.BlockSpec((L, 256), lambda c: (c, 4)),
                  pl.BlockSpec((L, 256), lambda c: (c, 5)), pl.BlockSpec((L, LANE), lambda c: (c, dtcb)),
                  const((1, LANE)), const((1, LANE)), const((1, 1024)), const((L, L)), const((LANE, 1024))],
        out_specs=[pl.BlockSpec((L, 1024), lambda c: (c, 0)), pl.BlockSpec((1, 8, LANE, LANE), lambda c: (c, 0, 0, 0))],
        out_shape=[jax.ShapeDtypeStruct((s_dim, 1024), F32), jax.ShapeDtypeStruct((nc, 8, LANE, LANE), F32)],
        scratch_shapes=[pltpu.VMEM((8, LANE, LANE), F32), pltpu.VMEM((L, LANE), F32), pltpu.VMEM((LANE, L), F32),
                        pltpu.VMEM((L, 1024), F32)],
        compiler_params=pltpu.CompilerParams(dimension_semantics=("arbitrary",)),
    )(xbca, xbca, xbca, dtr, bias, alog, d_x, tri, expand)


def _ssd_bwd(xbca, dtr, dtcb, bias, alog, d_x, states, dy, name="ssd_bwd"):
    s_dim = xbca.shape[0]
    L = SSD_CHUNK
    nc = s_dim // L
    tri, expand, expand_t = _ssd_consts()

    def body(xs_ref, b_ref, c_ref, dt_ref, bias_ref, alog_ref, dx_ref, tri_ref, exp_ref, expt_ref,
             st_ref, dy_ref, dxbc_ref, ddt_ref, dbias_ref, dalog_ref, dd_ref,
             dst_s, cs_s, cst_s, ex_s, dcsx_s, ddtx_s, dcol_s, drow_s, dlast_s, dd_s):
        @pl.when(pl.program_id(0) == 0)
        def _():
            dst_s[...] = jnp.zeros_like(dst_s)
            dbias_ref[...] = jnp.zeros_like(dbias_ref)
            dalog_ref[...] = jnp.zeros_like(dalog_ref)
            dd_s[...] = jnp.zeros_like(dd_s)

        dt, a, dt_x, f_x, t_x = _ssd_prep(dt_ref, bias_ref, alog_ref, tri_ref, exp_ref, cs_s, cst_s, ex_s)
        row = lax.broadcasted_iota(jnp.int32, (L, L), 0)
        col = lax.broadcasted_iota(jnp.int32, (L, L), 1)
        tril = row >= col
        low = col < SSD_HEAD_DIM
        dcol_s[...] = jnp.zeros_like(dcol_s)
        drow_s[...] = jnp.zeros_like(drow_s)
        for g in range(2):
            bg = b_ref[:, g * LANE:(g + 1) * LANE]
            cg = c_ref[:, g * LANE:(g + 1) * LANE]
            bgb, cgb = bg.astype(BF16), cg.astype(BF16)
            gmat = _dot(cgb, bgb, _NT)
            d_g = jnp.zeros((L, L), F32)
            d_b = jnp.zeros((L, LANE), F32)
            d_c = jnp.zeros((L, LANE), F32)
            for jj in range(4):
                j = 4 * g + jj
                sl = slice(j * LANE, (j + 1) * LANE)
                xp = xs_ref[:, sl]
                dtp = dt_x[:, sl]
                x_dt = xp * dtp
                xb = x_dt.astype(BF16)
                dyp = dy_ref[:, sl]
                dd_s[:, sl] += _colsum(dyp * xp)
                d_xdt = jnp.zeros((L, LANE), F32)
                for e in range(2):
                    h = 2 * j + e
                    lm = _decay_matrix(cs_s, cst_s, h, tril)
                    m = gmat * lm
                    dye = jnp.where(low if e == 0 else jnp.logical_not(low), dyp, 0.0).astype(BF16)
                    d_m = jnp.where(tril, _dot(dye, xb, _NT), 0.0)
                    d_xdt = d_xdt + _dot(m.astype(BF16), dye, _TN)
                    d_g = d_g + d_m * lm
                    w = d_m * m
                    dcol_s[...] += jnp.where(col == h, jnp.sum(w, axis=1, keepdims=True), 0.0)
                    drow_s[...] += jnp.where(row == h, jnp.sum(w, axis=0, keepdims=True), 0.0)
                stp = st_ref[0, j]
                stb = stp.astype(BF16)
                dstn = dst_s[j]
                dstb = dstn.astype(BF16)
                e_p = ex_s[:, sl]
                f_p = f_x[:, sl]
                t_p = t_x[:, sl]
                z = _dot(cgb, stb)
                d_z = (e_p * dyp).astype(BF16)
                d_c = d_c + _dot(d_z, stb, _NT)
                d_xf = _dot(bgb, dstb)
                d_b = d_b + _dot((x_dt * f_p).astype(BF16), dstb, _NT)
                d_xdt = d_xdt + f_p * d_xf
                d_f = x_dt * d_xf * f_p
                dcsx_s[:, sl] = dyp * e_p * z - d_f
                dlast_s[:, sl] = _colsum(d_f) + _colsum(dstn * stp) * t_p
                dst_s[j] = _dot(cgb, d_z, _TN) + t_p * dstn
                dxbc_ref[:, sl] = dx_ref[:, sl] * dyp + d_xdt * dtp
                ddtx_s[:, sl] = d_xdt * xp
            d_gb = d_g.astype(BF16)
            dxbc_ref[:, 1024 + g * LANE:1024 + (g + 1) * LANE] = d_b + _dot(d_gb, cgb, _TN)
            dxbc_ref[:, 1280 + g * LANE:1280 + (g + 1) * LANE] = d_c + _dot(d_gb, bgb)

        expt = expt_ref[...]
        dlast = _sel_dot(jnp.broadcast_to(dlast_s[...], (8, 1024)), expt, 3)
        d_cs = dcol_s[...] - drow_s[...].T + _sel_dot(dcsx_s[...], expt, 3)
        rown = lax.broadcasted_iota(jnp.int32, (L, LANE), 0)
        d_cs = d_cs + jnp.where(rown == L - 1, jnp.sum(dlast, axis=0, keepdims=True) * 0.125, 0.0)
        d_da = _sel_dot(d_cs, tri_ref[...], 3, _TN, sel_left=True)
        d_dt = d_da * a + _sel_dot(ddtx_s[...], expt, 3)
        dalog_ref[...] += _colsum(d_da * dt) * a
        d_raw = d_dt * _sigmoid(dt_ref[...] + bias_ref[...])
        ddt_ref[...] = d_raw.astype(ddt_ref.dtype)
        dbias_ref[...] += _colsum(d_raw)
        dd8 = _sel_dot(jnp.broadcast_to(dd_s[...], (8, 1024)), expt, 3)
        dd_ref[...] = jnp.sum(dd8, axis=0, keepdims=True) * 0.125

    const = lambda shape: pl.BlockSpec(shape, lambda c: tuple(0 for _ in shape))
    rev = lambda cb: (lambda c: (nc - 1 - c, cb))
    return pl.pallas_call(
        body, name=name, grid=(nc,),
        in_specs=[pl.BlockSpec((L, 1024), rev(0)), pl.BlockSpec((L, 256), rev(4)), pl.BlockSpec((L, 256), rev(5)),
                  pl.BlockSpec((L, LANE), rev(dtcb)), const((1, LANE)), const((1, LANE)), const((1, 1024)), const((L, L)),
                  const((LANE, 1024)), const((1024, LANE)),
                  pl.BlockSpec((1, 8, LANE, LANE), lambda c: (nc - 1 - c, 0, 0, 0)), pl.BlockSpec((L, 1024), rev(0))],
        out_specs=[pl.BlockSpec((L, SSD_XBC), rev(0)), pl.BlockSpec((L, LANE), rev(0)), const((1, LANE)), const((1, LANE)),
                   const((1, LANE))],
        out_shape=[jax.ShapeDtypeStruct((s_dim, SSD_XBC), F32), jax.ShapeDtypeStruct((s_dim, LANE), BF16),
                   jax.ShapeDtypeStruct((1, LANE), F32), jax.ShapeDtypeStruct((1, LANE), F32),
                   jax.ShapeDtypeStruct((1, LANE), F32)],
        scratch_shapes=[pltpu.VMEM((8, LANE, LANE), F32), pltpu.VMEM((L, LANE), F32), pltpu.VMEM((LANE, L), F32),
                        pltpu.VMEM((L, 1024), F32), pltpu.VMEM((L, 1024), F32), pltpu.VMEM((L, 1024), F32),
                        pltpu.VMEM((L, LANE), F32), pltpu.VMEM((LANE, L), F32), pltpu.VMEM((1, 1024), F32),
                        pltpu.VMEM((1, 1024), F32)],
        compiler_params=pltpu.CompilerParams(dimension_semantics=("arbitrary",)),
    )(xbca, xbca, xbca, dtr, bias, alog, d_x, tri, expand, expand_t, states, dy)


def _swap_halves(u):
    width = u.shape[1]
    lane = lax.broadcasted_iota(jnp.int32, u.shape, 1)
    return jnp.where(lane % MLA_ROPE < MLA_ROPE // 2, pltpu.roll(u, width - MLA_ROPE // 2, 1), pltpu.roll(u, MLA_ROPE // 2, 1))


def _rope_fwd_fn(u, cos, sin):
    return u * cos + _swap_halves(u) * sin


def _rope_bwd_fn(d, cos, sin):
    return d * cos + _swap_halves(d * sin)


def _spread4(v):
    return v + pltpu.roll(v, 32, 1) + pltpu.roll(v, 64, 1) + pltpu.roll(v, 96, 1)


def _att_masks(tq):
    lane = lax.broadcasted_iota(jnp.int32, (tq, LANE), 1)
    return lane // MLA_NOPE, lane // MLA_ROPE


def _att_tile(i, tq):
    klen = (i + 1) * tq
    qpos = i * tq + lax.broadcasted_iota(jnp.int32, (tq, klen), 0)
    kpos = lax.broadcasted_iota(jnp.int32, (tq, klen), 1)
    return slice(i * tq, (i + 1) * tq), klen, qpos >= kpos


def _att_qcat(qn_t, qr_t, par, e, half_id, grp_id):
    return jnp.concatenate([jnp.where(half_id == par, qn_t * ATT_SCALE, 0.0), jnp.where(grp_id == e, qr_t * ATT_SCALE, 0.0)],
                           axis=1).astype(BF16)


def _att_softmax(scores, causal):
    s = jnp.where(causal, scores, -jnp.inf)
    e = jnp.exp(s - jnp.max(s, axis=1, keepdims=True))
    return e, 1.0 / jnp.sum(e, axis=1, keepdims=True)


def _att_specs(s_dim):
    col = lambda f: pl.BlockSpec((s_dim, LANE), lambda j: (0, f(j)))
    return [col(lambda j: j), col(lambda j: j // 2), col(lambda j: j), col(lambda j: 0), col(lambda j: 8 + j)]


def _att_fwd(q, qr, kv, krt, name="att_fwd"):
    s_dim = q.shape[0]
    tq = min(ATT_TQ, s_dim)

    def body(qn_ref, qr_ref, kn_ref, krt_ref, v_ref, o_ref, kcat_s, vb_s):
        e0 = 2 * (pl.program_id(0) % 2)
        half_id, grp_id = _att_masks(tq)
        kcat_s[...] = jnp.concatenate([kn_ref[...], krt_ref[...]], axis=1).astype(BF16)
        vb_s[...] = v_ref[...].astype(BF16)
        for i in range(s_dim // tq):
            rows, klen, causal = _att_tile(i, tq)
            qn_t, qr_t = qn_ref[rows, :], qr_ref[rows, :]
            scores = [_dot(_att_qcat(qn_t, qr_t, par, e0 + par, half_id, grp_id), kcat_s[0:klen, :], _NT) for par in range(2)]
            probs = [_att_softmax(s, causal) for s in scores]
            outs = [_dot(e.astype(BF16), vb_s[0:klen, :]) * inv_l for e, inv_l in probs]
            o_ref[rows, :] = jnp.where(half_id == 0, outs[0], outs[1])

    return pl.pallas_call(
        body, name=name, grid=(MLA_HEADS // 2,), in_specs=_att_specs(s_dim),
        out_specs=pl.BlockSpec((s_dim, LANE), lambda j: (0, j)), out_shape=jax.ShapeDtypeStruct((s_dim, 1024), F32),
        scratch_shapes=[pltpu.VMEM((s_dim, 2 * LANE), BF16), pltpu.VMEM((s_dim, LANE), BF16)],
        compiler_params=pltpu.CompilerParams(dimension_semantics=("parallel",)),
    )(q, qr, kv, krt, kv)


def _att_bwd(q, qr, kv, krt, o, do, name="att_bwd"):
    s_dim = q.shape[0]
    tq = min(ATT_TQ, s_dim)

    def body(qn_ref, qr_ref, kn_ref, krt_ref, v_ref, o_ref, do_ref, dqn_ref, dqr_ref, dkn_ref, dv_ref, dkrt_ref,
             kcat_s, vb_s):
        e0 = 2 * (pl.program_id(0) % 2)
        half_id, grp_id = _att_masks(tq)
        kcat_s[...] = jnp.concatenate([kn_ref[...], krt_ref[...]], axis=1).astype(BF16)
        vb_s[...] = v_ref[...].astype(BF16)
        dkn_ref[...] = jnp.zeros_like(dkn_ref)
        dv_ref[...] = jnp.zeros_like(dv_ref)
        dkrt_ref[...] = jnp.zeros_like(dkrt_ref)
        for i in range(s_dim // tq):
            rows, klen, causal = _att_tile(i, tq)
            qn_t, qr_t, o_t, do_t = qn_ref[rows, :], qr_ref[rows, :], o_ref[rows, :], do_ref[rows, :]
            heads = range(2)
            qcats = [_att_qcat(qn_t, qr_t, par, e0 + par, half_id, grp_id) for par in heads]
            scores = [_dot(qcats[par], kcat_s[0:klen, :], _NT) for par in heads]
            doms = [jnp.where(half_id == par, do_t, 0.0) for par in heads]
            dombs = [d.astype(BF16) for d in doms]
            d_ps = [_dot(dombs[par], vb_s[0:klen, :], _NT) for par in heads]
            probs = []
            for par in heads:
                e, inv_l = _att_softmax(scores[par], causal)
                probs.append(e * inv_l)
            d_ss = []
            for par in heads:
                d_row = jnp.sum(doms[par] * o_t, axis=1, keepdims=True)
                d_ss.append((probs[par] * (d_ps[par] - d_row)).astype(BF16))
            dqcats = [_dot(d_ss[par], kcat_s[0:klen, :]) * ATT_SCALE for par in heads]
            dkcats = [_dot(d_ss[par], qcats[par], _TN) for par in heads]
            dvs = [_dot(probs[par].astype(BF16), dombs[par], _TN) for par in heads]
            dqn_ref[rows, :] = jnp.where(half_id == 0, dqcats[0][:, :LANE], dqcats[1][:, :LANE]).astype(dqn_ref.dtype)
            dqr_ref[rows, :] = (jnp.where(grp_id == e0, dqcats[0][:, LANE:], 0.0)
                                + jnp.where(grp_id == e0 + 1, dqcats[1][:, LANE:], 0.0))
            dkn_ref[0:klen, :] += dkcats[0][:, :LANE] + dkcats[1][:, :LANE]
            dkrt_ref[0:klen, :] += dkcats[0][:, LANE:] + dkcats[1][:, LANE:]
            dv_ref[0:klen, :] += dvs[0] + dvs[1]

    col = lambda f: pl.BlockSpec((s_dim, LANE), lambda j: (0, f(j)))
    return pl.pallas_call(
        body, name=name, grid=(MLA_HEADS // 2,), in_specs=_att_specs(s_dim) + [col(lambda j: j), col(lambda j: j)],
        out_specs=[col(lambda j: j), pl.BlockSpec((None, s_dim, LANE), lambda j: (j % 2, 0, j // 2)), col(lambda j: j),
                   col(lambda j: j), pl.BlockSpec((None, s_dim, LANE), lambda j: (j, 0, 0))],
        out_shape=[jax.ShapeDtypeStruct((s_dim, 1024), BF16), jax.ShapeDtypeStruct((2, s_dim, 512), F32),
                   jax.ShapeDtypeStruct((s_dim, 1024), F32), jax.ShapeDtypeStruct((s_dim, 1024), F32),
                   jax.ShapeDtypeStruct((MLA_HEADS // 2, s_dim, LANE), F32)],
        scratch_shapes=[pltpu.VMEM((s_dim, 2 * LANE), BF16), pltpu.VMEM((s_dim, LANE), BF16)],
        compiler_params=pltpu.CompilerParams(dimension_semantics=("parallel",)),
    )(q, qr, kv, krt, kv, o, do)


def _gather_many(shards, name):
    n_arr = len(shards)

    def body(*refs):
        x_refs, out_refs = refs[:n_arr], refs[n_arr:2 * n_arr]
        send_sems, recv_sems, local_sems = refs[2 * n_arr:]
        x_i, y_i, c_i = lax.axis_index("x"), lax.axis_index("y"), lax.axis_index("c")
        me, sibling = (x_i, y_i, c_i), (x_i, y_i, 1 - c_i)
        chips = [(1 - x_i, y_i), (x_i, 1 - y_i), (1 - x_i, 1 - y_i)]

        def copy(a, k, block, to, src=None):
            slot = out_refs[a].at[4 * block[0] + 2 * block[1] + block[2]]
            return pltpu.make_async_remote_copy(
                src_ref=slot if src is None else src, dst_ref=slot, send_sem=send_sems.at[a, k],
                recv_sem=recv_sems.at[a, k], device_id=to, device_id_type=pl.DeviceIdType.MESH)

        mine, first, passed = [], [], []
        for a in range(n_arr):
            mine.append(pltpu.make_async_copy(x_refs[a], out_refs[a].at[4 * x_i + 2 * y_i + c_i], local_sems.at[a]))
            mine[a].start()
            first.append([copy(a, 0, me, sibling, src=x_refs[a])]
                         + [copy(a, 1 + j, me, (*chip, c_i), src=x_refs[a]) for j, chip in enumerate(chips)])
            for cp in first[a]:
                cp.start()
            passed.append([copy(a, 4 + j, (*chip, c_i), sibling) for j, chip in enumerate(chips)])
        for j, chip in enumerate(chips):
            for a in range(n_arr):
                copy(a, 1 + j, (*chip, c_i), me).wait_recv()
                passed[a][j].start()
        for a in range(n_arr):
            copy(a, 0, sibling, me).wait_recv()
            for j, chip in enumerate(chips):
                copy(a, 4 + j, (*chip, 1 - c_i), me).wait_recv()
        for a in range(n_arr):
            for cp in first[a] + passed[a]:
                cp.wait_send()
            mine[a].wait()

    any_spec = pl.BlockSpec(memory_space=pl.ANY)
    return pl.pallas_call(
        body, name=name, out_shape=[jax.ShapeDtypeStruct((N_DEV,) + x.shape, x.dtype) for x in shards],
        in_specs=[any_spec] * n_arr, out_specs=[any_spec] * n_arr,
        scratch_shapes=[pltpu.SemaphoreType.DMA((n_arr, 7)), pltpu.SemaphoreType.DMA((n_arr, 7)),
                        pltpu.SemaphoreType.DMA((n_arr,))],
    )(*shards)


_HBM = pl.BlockSpec(memory_space=pltpu.HBM)
_SEM = pl.BlockSpec(memory_space=pltpu.SEMAPHORE)


def _plan_copies(plan, src_refs, land_refs, send_sems, recv_sems):
    copies = []
    for s_ref, l_ref in zip(src_refs, land_refs):
        for src, dst, peer in plan(s_ref, l_ref):
            k = len(copies)
            copies.append(pltpu.make_async_remote_copy(
                src_ref=src, dst_ref=dst, send_sem=send_sems.at[k], recv_sem=recv_sems.at[k], device_id=peer,
                device_id_type=pl.DeviceIdType.MESH))
    return copies


def _split_start(srcs, lands, plan, n_copy, name, after=None):
    n = len(srcs)
    n_in = 2 * n + (after is not None)

    def body(*refs):
        for cp in _plan_copies(plan, refs[:n], refs[n:2 * n], refs[n_in], refs[n_in + 1]):
            cp.start()
        refs[-1][...] = jnp.zeros_like(refs[-1])

    sems = pltpu.SemaphoreType.DMA((n * n_copy,))
    res = pl.pallas_call(
        body, name=name,
        out_shape=(sems, sems, *[pltpu.HBM(a.shape, a.dtype) for a in list(srcs) + list(lands)],
                   jax.ShapeDtypeStruct((8, LANE), F32)),
        in_specs=[_HBM] * (2 * n) + [pl.BlockSpec(memory_space=pl.ANY)] * (after is not None),
        out_specs=(_SEM, _SEM, *[_HBM] * (2 * n), pl.BlockSpec(memory_space=pltpu.VMEM)),
        input_output_aliases={i: 2 + i for i in range(2 * n)},
        compiler_params=pltpu.CompilerParams(has_side_effects=pltpu.SideEffectType.DATAFLOW_SIDE_EFFECTING),
    )(*[pltpu.with_memory_space_constraint(a, pltpu.HBM) for a in list(srcs) + list(lands)],
      *([after] if after is not None else []))
    return res[0], res[1], list(res[2:2 + n]), list(res[2 + n:2 + 2 * n]), res[-1]


def _split_wait(send_sems, recv_sems, srcs, lands, after, plan, name):
    n = len(srcs)

    def body(*refs):
        copies = _plan_copies(plan, refs[:n], refs[n:2 * n], refs[2 * n], refs[2 * n + 1])
        for cp in copies:
            cp.wait_send()
        for cp in copies:
            cp.wait_recv()

    res = pl.pallas_call(
        body, name=name, out_shape=tuple(pltpu.HBM(a.shape, a.dtype) for a in list(srcs) + list(lands)),
        in_specs=[_HBM] * (2 * n) + [_SEM, _SEM, pl.BlockSpec(memory_space=pl.ANY)], out_specs=tuple([_HBM] * (2 * n)),
        input_output_aliases={i: i for i in range(2 * n)},
        compiler_params=pltpu.CompilerParams(has_side_effects=pltpu.SideEffectType.DATAFLOW_SIDE_EFFECTING),
    )(*srcs, *lands, send_sems, recv_sems, after)
    return list(res[:n]), list(res[n:])


def _plan_broadcast(src, land):
    x_i, y_i, c_i = lax.axis_index("x"), lax.axis_index("y"), lax.axis_index("c")
    me = 4 * x_i + 2 * y_i + c_i
    return [(src, land.at[me], (x_i ^ (k >> 2), y_i ^ ((k >> 1) & 1), c_i ^ (k & 1))) for k in range(1, N_DEV)]


def _plan_scatter(src, land):
    x_i, y_i, c_i = lax.axis_index("x"), lax.axis_index("y"), lax.axis_index("c")
    me = 4 * x_i + 2 * y_i + c_i
    plan = []
    for k in range(1, N_DEV):
        px, py, pc = x_i ^ (k >> 2), y_i ^ ((k >> 1) & 1), c_i ^ (k & 1)
        plan.append((src.at[4 * px + 2 * py + pc], land.at[me], (px, py, pc)))
    return plan


def _adam_math(g, w, m, v):
    m_new = ADAM_B1 * m + (1.0 - ADAM_B1) * g
    v_new = ADAM_B2 * v + (1.0 - ADAM_B2) * (g * g)
    m_hat = m_new / (1.0 - ADAM_B1 ** ADAM_STEP)
    v_hat = v_new / (1.0 - ADAM_B2 ** ADAM_STEP)
    return -ADAM_LR * (m_hat / (jnp.sqrt(v_hat) + ADAM_EPS) + ADAM_WD * w), m_new, v_new


def _adam(slots, w, m, v, name, own=None, own_idx=None):
    n_slot, rows, cols = slots.shape
    tr = ROW_TILE if rows % ROW_TILE == 0 else rows
    has_own = own is not None

    def body(*refs):
        if has_own:
            idx_ref, own_ref, refs = refs[0], refs[1], refs[2:]
        s_ref, w_ref, m_ref, v_ref, g_ref, d_ref, mo_ref, vo_ref = refs
        g = own_ref[...].astype(F32) if has_own else s_ref[0].astype(F32)
        for k in range(0 if has_own else 1, n_slot):
            part = s_ref[k].astype(F32)
            g = g + (jnp.where(idx_ref[0] == k, 0.0, part) if has_own else part)
        g_ref[...] = g
        d_ref[...], mo_ref[...], vo_ref[...] = _adam_math(g, w_ref[...], m_ref[...], v_ref[...])

    spec = pl.BlockSpec((tr, cols), lambda i, *_: (i, 0))
    in_specs = [pl.BlockSpec((n_slot, tr, cols), lambda i, *_: (0, i, 0)), spec, spec, spec]
    if has_own:
        in_specs = [pl.BlockSpec((None, tr, cols), lambda i, idx: (idx[0], i, 0))] + in_specs
    grid_spec = pltpu.PrefetchScalarGridSpec(num_scalar_prefetch=1 if has_own else 0, grid=(rows // tr,), in_specs=in_specs,
                                             out_specs=[spec] * 4)
    ins = ([own_idx, own] if has_own else []) + [slots, w, m, v]
    return pl.pallas_call(
        body, name=name, grid_spec=grid_spec, out_shape=[jax.ShapeDtypeStruct((rows, cols), F32)] * 4,
        compiler_params=pltpu.CompilerParams(dimension_semantics=("parallel",)),
    )(*ins)


PACK_ROWS, PACK_W = 24, 1536
REPL_W = (("ssd_conv_b", 1536), ("ssd_dt_bias", 16), ("ssd_A_log", 16), ("ssd_D", 16), ("ssd_norm_w", 1024),
          ("mla_q_norm_w", 384), ("mla_kv_norm_w", 256), ("mla_out_norm_w", 1024), ("ln_mix_g", 1024),
          ("ln_mix_b", 1024), ("ln_ffn_g", 1024), ("ln_ffn_b", 1024))
LOSS_ROW = 4 + len(REPL_W)


def _pack_small(conv_w_grad, grads, loss, name="pack_small"):
    def body(*refs):
        cw_ref, g_refs, loss_ref, o_ref = refs[0], refs[1:1 + len(REPL_W)], refs[1 + len(REPL_W)], refs[-1]
        o_ref[...] = jnp.zeros_like(o_ref)
        o_ref[0:4, :] = cw_ref[...]
        for i, g_ref in enumerate(g_refs):
            o_ref[4 + i:5 + i, 0:g_ref.shape[1]] = g_ref[...]
        o_ref[LOSS_ROW:LOSS_ROW + 1, 0:LANE] = loss_ref[...]

    return pl.pallas_call(body, name=name, out_shape=jax.ShapeDtypeStruct((PACK_ROWS, PACK_W), F32))(conv_w_grad, *grads, loss)


def _adam_small(gathered, wmv, name="adam_small"):
    def body(*refs):
        s_ref = refs[0]
        in_refs = refs[1:1 + 3 * len(REPL_W)]
        cw_ref, loss_ref = refs[1 + 3 * len(REPL_W)], refs[2 + 3 * len(REPL_W)]
        out_refs = refs[3 + 3 * len(REPL_W):-1]
        tot = refs[-1]
        acc = s_ref[0]
        for k in range(1, N_DEV):
            acc = acc + s_ref[k]
        tot[...] = acc
        cw_ref[...] = tot[0:4, :]
        loss_ref[...] = tot[LOSS_ROW:LOSS_ROW + 1, 0:LANE]
        for i, (_, width) in enumerate(REPL_W):
            g = tot[4 + i:5 + i, 0:width]
            w_ref, m_ref, v_ref = in_refs[3 * i:3 * i + 3]
            g_ref, d_ref, mo_ref, vo_ref = out_refs[4 * i:4 * i + 4]
            g_ref[...] = g
            d_ref[...], mo_ref[...], vo_ref[...] = _adam_math(g, w_ref[...], m_ref[...], v_ref[...])

    flat_in = [a for triple in wmv for a in triple]
    out_shape = [jax.ShapeDtypeStruct((4, PACK_W), F32), jax.ShapeDtypeStruct((1, LANE), F32)]
    for _, width in REPL_W:
        out_shape += [jax.ShapeDtypeStruct((1, width), F32)] * 4
    res = pl.pallas_call(body, name=name, out_shape=out_shape, scratch_shapes=[pltpu.VMEM((PACK_ROWS, PACK_W), F32)])(
        gathered, *flat_in)
    return res[0], res[1], [res[2 + 4 * i:6 + 4 * i] for i in range(len(REPL_W))]


def _cols_full(g):
    return jnp.transpose(g, (1, 0, 2)).reshape(g.shape[1], -1)


def _cols_split(full):
    k_dim, n_dim = full.shape
    return jnp.transpose(full.reshape(k_dim, N_DEV, n_dim // N_DEV), (1, 0, 2))


PROJ_BLOCK = {"z": (1024, 0), "dt": (LANE, 8), "q_c": (MLA_Q_RANK, 3), "xbc": (SSD_XBC, 1), "kv_c": (MLA_KV_RANK, 12),
              "k_rope": (LANE, 26)}


def _win_pad(wt):
    z = lambda n: jnp.zeros((n, wt.shape[1]), wt.dtype)
    return jnp.concatenate([wt[:1024], wt[2560:2576], z(112), wt[2576:2960], wt[1024:2560], wt[2960:3216], wt[3216:3248],
                            z(96)], axis=0)


def _win_unpad(wt):
    return jnp.concatenate([wt[:1024], wt[1536:3072], wt[1024:1040], wt[1152:1536], wt[3072:3328], wt[3328:3360]], axis=0)


def _heads_split_t(wt, a, b):
    w3 = wt.reshape(MLA_HEADS, a + b, wt.shape[1])
    return jnp.concatenate([w3[:, :a].reshape(-1, wt.shape[1]), w3[:, a:].reshape(-1, wt.shape[1])], axis=0)


def _heads_merge_t(wt, a, b):
    wa = wt[:MLA_HEADS * a].reshape(MLA_HEADS, a, wt.shape[1])
    wb = wt[MLA_HEADS * a:].reshape(MLA_HEADS, b, wt.shape[1])
    return jnp.concatenate([wa, wb], axis=1).reshape(-1, wt.shape[1])


def _heads_split(w, a, b):
    k_dim = w.shape[0]
    w3 = w.reshape(k_dim, MLA_HEADS, a + b)
    return jnp.concatenate([w3[:, :, :a].reshape(k_dim, -1), w3[:, :, a:].reshape(k_dim, -1)], axis=1)


def _heads_merge(w, a, b):
    k_dim = w.shape[0]
    wa = w[:, :MLA_HEADS * a].reshape(k_dim, MLA_HEADS, a)
    wb = w[:, MLA_HEADS * a:].reshape(k_dim, MLA_HEADS, b)
    return jnp.concatenate([wa, wb], axis=2).reshape(k_dim, -1)


def _pad_lanes(v, width=LANE):
    return jnp.concatenate([v, jnp.zeros((v.shape[0], width - v.shape[1]), v.dtype)], axis=1)


def _local_step(x, p, positions, tgt, W, P, comm=None):
    comm = comm or {}
    zero_tok = jnp.zeros((8, LANE), F32)
    s_dim = x.shape[0]
    inv_freq = 1.0 / (ROPE_BASE ** (jnp.arange(0, MLA_ROPE, 2, dtype=F32) / MLA_ROPE))
    ang = positions.astype(F32)[:, None] * inv_freq
    cos, sin = jnp.cos(ang), jnp.sin(ang)
    cos32 = jnp.concatenate([cos, cos], axis=1)
    sin32 = jnp.concatenate([-sin, sin], axis=1)
    cos512, sin512 = jnp.tile(cos32, (1, 16)), jnp.tile(sin32, (1, 16))
    cos128, sin128 = jnp.tile(cos32, (1, 4)), jnp.tile(sin32, (1, 4))
    bias_p, alog_p = _pad_lanes(P["ssd_dt_bias"]), _pad_lanes(P["ssd_A_log"])
    d_x = jnp.repeat(P["ssd_D"], SSD_HEAD_DIM, axis=1)

    xb, pb = x.astype(BF16), p.astype(BF16)
    proj = _mm(xb, W["w_in"], tb=True, after=comm.get("token0", zero_tok), name="mm_in")
    z, qc, kvc, kr = [(proj,) + PROJ_BLOCK[n] for n in ("z", "q_c", "kv_c", "k_rope")]
    xbca = _conv_fwd(proj, PROJ_BLOCK["xbc"][1], P["ssd_conv_w"], P["ssd_conv_b"])
    y, states = _ssd_fwd(xbca, proj, PROJ_BLOCK["dt"][1], bias_p, alog_p, d_x)
    (yssd,) = _rowwise(_gate_rms, [y, z], [P["ssd_norm_w"]], [(1024, BF16)], name="ssd_gate_norm")
    qn, kvn, krt = _rowwise(lambda a, c, u, cs, sn, wq, wkv: (_rms(a, wq), _rms(c, wkv), _spread4(_rope_fwd_fn(u, cs, sn))),
                            [qc, kvc, kr, cos128, sin128], [P["mla_q_norm_w"], P["mla_kv_norm_w"]],
                            [(MLA_Q_RANK, BF16), (MLA_KV_RANK, BF16), LANE], name="qkv_norm_rope_k")
    q = _mm(qn, W["mla_w_q_b"], tb=True, name="mm_q")
    kv = _mm(kvn, W["mla_w_kv_b"], name="mm_kv")
    (qr,) = _rowwise(_rope_fwd_fn, [(q, 512, 2), cos512, sin512], [], [512], name="rope_q")
    att = _att_fwd(q, qr, kv, krt)
    (ymla,) = _rowwise(_rms, [att], [P["mla_out_norm_w"]], [(1024, BF16)], name="out_norm")
    ycat = jnp.concatenate([yssd, ymla], axis=1)
    if "late_weights" in comm:
        W = {**W, **comm["late_weights"]("out", ycat)}
    mix = _mm(ycat, W["w_out"], name="mm_out")
    f_h1 = lambda xv, mv, g, b: _ln(ALPHA * xv + mv, g, b)
    h1, h1b = _rowwise(lambda *a: (f_h1(*a),) * 2, [x, mix], [P["ln_mix_g"], P["ln_mix_b"]], [1024, (1024, BF16)],
                       name="ln_mix")
    if "late_weights" in comm:
        W = {**W, **comm["late_weights"]("ffn", h1b)}
    hg = _mm(h1b, W["w_ffn_gate"], tb=True, out_dtype=BF16, name="mm_gate")
    hu, act = _mm(h1b, W["w_ffn_up"], tb=True, name="mm_up",
                  epilogue=(lambda u, g: (u, _silu(g.astype(F32)) * u), [hg], [BF16, BF16]))
    pg = _mm(h1b, W["w_ple_gate"], name="mm_ple_gate")
    pp = _mm(pb, W["w_ple_proj"], name="mm_ple")
    ffn = _mm(act, W["w_ffn_down"], name="mm_down")

    f_h2 = lambda hv, fv, pg, ppv, g, b: _ln(ALPHA * hv + fv + _sigmoid(pg) * ppv, g, b)

    def final_fn(hv, fv, pg, ppv, tv, g, b):
        h2, pull = jax.vjp(f_h2, hv, fv, pg, ppv, g, b)
        diff = h2 - tv
        loss = 0.5 * jnp.sum(jnp.mean(diff * diff, axis=-1, keepdims=True), axis=0, keepdims=True)
        d_h, d_f, d_pg, d_pp, d_g, d_b = pull(diff * (1.0 / D_MODEL))
        return d_h, d_f, d_pg, d_pp, d_g, d_b, jnp.broadcast_to(loss, (1, LANE))

    dh1_a, dffn, dpg, dpp, g_ffn_g, g_ffn_b, loss = _rowwise(
        final_fn, [h1, ffn, pg, pp, tgt], [P["ln_ffn_g"], P["ln_ffn_b"]], [1024] + [(1024, BF16)] * 3,
        [1024, 1024, LANE], name="final")

    G = {}
    def swiglu_bwd(d, g, u):
        g, u = g.astype(F32), u.astype(F32)
        sg = _sigmoid(g)
        return d * u * (sg * (1.0 + g * (1.0 - sg))), d * (g * sg)

    dg, du = _mm(dffn, W["w_ffn_down"], tb=True, name="mm_down_dx",
                 epilogue=(swiglu_bwd, [hg, hu], [BF16, BF16]))
    G["w_ffn_down"] = _mm(act, dffn, ta=True, out_dtype=GRAD_DT, name="mm_down_dw")
    dh1 = _mm(dg, W["w_ffn_gate"], add=dh1_a, name="mm_gate_dx")
    dh1 = _mm(du, W["w_ffn_up"], add=dh1, name="mm_up_dx")
    dh1 = _mm(dpg, W["w_ple_gate"], tb=True, add=dh1, name="mm_ple_gate_dx")
    G["w_ffn_gate"] = _mm(dg, h1b, ta=True, out_dtype=GRAD_DT, name="mm_gate_dw")
    G["w_ffn_up"] = _mm(du, h1b, ta=True, out_dtype=GRAD_DT, name="mm_up_dw")
    G["w_ple_gate"] = _mm(h1b, dpg, ta=True, out_dtype=GRAD_DT, name="mm_ple_gate_dw")
    G["w_ple_proj"] = _mm(pb, dpp, ta=True, out_dtype=GRAD_DT, name="mm_ple_dw")
    dx_a, dmix, g_mix_g, g_mix_b = _rowwise(
        lambda xv, mv, dv, g, b: _vjp_rows(f_h1)(xv, mv, g, b, dv), [x, mix, dh1], [P["ln_mix_g"], P["ln_mix_b"]],
        [1024, (1024, BF16)], [1024, 1024], name="ln_mix_bwd")
    dycat = _mm(dmix, W["w_out"], tb=True, name="mm_out_dx")
    G["w_out"] = _mm(ycat, dmix, ta=True, out_dtype=GRAD_DT, name="mm_out_dw")

    grads_done = comm.get("grads", lambda group, grads: zero_tok)
    tok1 = grads_done("ffn", G)
    datt, g_out_norm = _rowwise(lambda a, dv, w, t: _vjp_rows(_rms)(a, w, dv + jnp.min(t)), [att, (dycat, 1024, 1)],
                                [P["mla_out_norm_w"], tok1], [1024], [1024], name="out_norm_bwd")
    dqn_nope, dqr, dkn, dv, dkrt = _att_bwd(q, qr, kv, krt, att, datt)
    dkv = jnp.concatenate([dkn, dv], axis=1)
    (dq_rope,) = _rowwise(lambda d0, d1, c, s: _rope_bwd_fn(d0 + d1, c, s), [(dqr, 512, 0), (dqr, 512, 1), cos512, sin512],
                          [], [(512, BF16)], name="rope_q_bwd")

    def rope_k_bwd(*a):
        d = _spread4(functools.reduce(lambda u, w: u + w, a[:-2]))
        lane = lax.broadcasted_iota(jnp.int32, d.shape, 1)
        return _rope_bwd_fn(jnp.where(lane < MLA_ROPE, d, 0.0), a[-2], a[-1])

    (dkr,) = _rowwise(rope_k_bwd, [(dkrt, LANE, k) for k in range(MLA_HEADS // 2)] + [cos128, sin128], [], [(LANE, BF16)],
                      name="rope_k_bwd")
    dq = jnp.concatenate([dqn_nope, dq_rope], axis=1)
    dqn = _mm(dq, W["mla_w_q_b"], name="mm_q_dx")
    G["mla_w_q_b"] = _mm(dq, qn, ta=True, out_dtype=GRAD_DT, name="mm_q_dw")
    dkvn = _mm(dkv, W["mla_w_kv_b"], tb=True, name="mm_kv_dx")
    G["mla_w_kv_b"] = _mm(kvn, dkv, ta=True, out_dtype=GRAD_DT, name="mm_kv_dw")
    tok2 = grads_done("mla", G)
    def qkv_norm_bwd(a, da, c, dc, wq, wkv, t):
        (d_a, d_wq), (d_c, d_wkv) = _vjp_rows(_rms)(a, wq, da + jnp.min(t)), _vjp_rows(_rms)(c, wkv, dc)
        return d_a, d_c, d_wq, d_wkv

    dqc, dkvc, g_q_norm, g_kv_norm = _rowwise(
        qkv_norm_bwd, [qc, dqn, kvc, dkvn], [P["mla_q_norm_w"], P["mla_kv_norm_w"], tok2],
        [(MLA_Q_RANK, BF16), (MLA_KV_RANK, BF16)], [MLA_Q_RANK, MLA_KV_RANK], name="qkv_norm_bwd")

    dy, dz, g_ssd_norm = _rowwise(lambda yv, zv, dv, w, t: _vjp_rows(_gate_rms)(yv, zv, w, dv + jnp.min(t)),
                                  [y, z, (dycat, 1024, 0)], [P["ssd_norm_w"], tok1], [1024, (1024, BF16)], [1024],
                                  name="ssd_gate_norm_bwd")
    dxbca, ddtr, g_dt_bias, g_alog, g_d = _ssd_bwd(xbca, proj, PROJ_BLOCK["dt"][1], bias_p, alog_p, d_x, states, dy)
    da, g_conv_w, g_conv_b = _conv_bwd_pre(proj, PROJ_BLOCK["xbc"][1], P["ssd_conv_w"], P["ssd_conv_b"], dxbca)
    dxbc = _conv_bwd_in(da, P["ssd_conv_w"])

    small = {
        "ssd_conv_b": g_conv_b, "ssd_dt_bias": g_dt_bias, "ssd_A_log": g_alog, "ssd_D": g_d, "ssd_norm_w": g_ssd_norm,
        "mla_q_norm_w": g_q_norm, "mla_kv_norm_w": g_kv_norm, "mla_out_norm_w": g_out_norm, "ln_mix_g": g_mix_g,
        "ln_mix_b": g_mix_b, "ln_ffn_g": g_ffn_g, "ln_ffn_b": g_ffn_b,
    }
    packed = _pack_small(g_conv_w, [small[n] for n, _ in REPL_W], loss)
    if "small" in comm:
        comm["small"](packed)

    dproj = jnp.concatenate([dz, ddtr, dqc, dxbc, dkvc, dkr], axis=1)
    G["w_in"] = _mm(dproj, xb, ta=True, out_dtype=GRAD_DT, name="mm_in_dw")
    grad_x = _mm(dproj, W["w_in"], add=dx_a, after=grads_done("in", G), name="mm_in_dx")
    return grad_x, G, packed


def kernel(x, p, positions, w_in, ssd_conv_w, ssd_conv_b, ssd_dt_bias, ssd_A_log, ssd_D, ssd_norm_w, mla_q_norm_w, mla_w_q_b, mla_kv_norm_w, mla_w_kv_b, mla_out_norm_w, w_out, ln_mix_g, ln_mix_b, w_ffn_gate, w_ffn_up, w_ffn_down, w_ple_gate, w_ple_proj, ln_ffn_g, ln_ffn_b, loss_target, m_w_in, m_ssd_conv_w, m_ssd_conv_b, m_ssd_dt_bias, m_ssd_A_log, m_ssd_D, m_ssd_norm_w, m_mla_q_norm_w, m_mla_w_q_b, m_mla_kv_norm_w, m_mla_w_kv_b, m_mla_out_norm_w, m_w_out, m_ln_mix_g, m_ln_mix_b, m_w_ffn_gate, m_w_ffn_up, m_w_ffn_down, m_w_ple_gate, m_w_ple_proj, m_ln_ffn_g, m_ln_ffn_b, v_w_in, v_ssd_conv_w, v_ssd_conv_b, v_ssd_dt_bias, v_ssd_A_log, v_ssd_D, v_ssd_norm_w, v_mla_q_norm_w, v_mla_w_q_b, v_mla_kv_norm_w, v_mla_w_kv_b, v_mla_out_norm_w, v_w_out, v_ln_mix_g, v_ln_mix_b, v_w_ffn_gate, v_w_ffn_up, v_w_ffn_down, v_w_ple_gate, v_w_ple_proj, v_ln_ffn_g, v_ln_ffn_b):
    args = dict(locals())
    core = lax.axis_index("c")
    me = 4 * lax.axis_index("x") + 2 * lax.axis_index("y") + core

    conv_sh = ssd_conv_w[0]
    conv_hi = conv_sh.astype(BF16)
    conv_lo = (conv_sh - conv_hi.astype(F32)).astype(BF16)
    stored = lambda n, pre="": jnp.transpose(args[pre + n][0]) if n in TRANSPOSED else args[pre + n][0]
    shards = {n: stored(n).astype(BF16) for n in BIG}
    rows_full = lambda g: g.reshape(-1, g.shape[2])

    early = _gather_many([shards[n] for n in EARLY] + [jnp.concatenate([conv_hi, conv_lo], axis=0)], "gather_early")
    gw = dict(zip(EARLY, early[:-1]))
    conv_g = early[-1].astype(F32)
    W = {
        "w_in": _win_pad(rows_full(gw["w_in"])),
        "mla_w_q_b": _heads_split_t(rows_full(gw["mla_w_q_b"]), MLA_NOPE, MLA_ROPE),
        "mla_w_kv_b": _heads_split(_cols_full(gw["mla_w_kv_b"]), MLA_NOPE, MLA_V),
    }
    P = {n: args[n] for n, _ in REPL_W}
    P["ssd_conv_w"] = _cols_full(conv_g[:, :4] + conv_g[:, 4:])

    late, after = {}, early[0]
    for group, names in LATE.items():
        lands = [lax.dynamic_update_slice(lax.empty((N_DEV,) + shards[n].shape, BF16), shards[n][None], (me, 0, 0)) for n in names]
        late[group] = _split_start([shards[n] for n in names], lands, _plan_broadcast, N_DEV - 1,
                                   "gather_" + group + "_start", after=after)
        after = late[group][4]

    def late_weights(group, after):
        _, got = _split_wait(*late[group][:4], after, _plan_broadcast, "gather_" + group + "_wait")
        return {n: _cols_full(g) if n == "w_ple_proj" else rows_full(g) for n, g in zip(LATE[group], got)}

    def to_blocks(n, g):
        if n == "w_in":
            g = _win_unpad(g)
        elif n == "mla_w_q_b":
            g = _heads_merge_t(g, MLA_NOPE, MLA_ROPE)
        elif n == "mla_w_kv_b":
            g = _heads_merge(g, MLA_NOPE, MLA_V)
        if n in ROW_SHARDED or n in TRANSPOSED:
            return g.reshape(N_DEV, -1, g.shape[1])
        return _cols_split(g)

    flight = {}

    def grads(group, G):
        gl = [to_blocks(n, G[n]) for n in GRAD_GROUPS[group]]
        flight[group] = _split_start(gl, [lax.empty(g.shape, g.dtype) for g in gl], _plan_scatter, N_DEV - 1,
                                     "grads_" + group + "_start", after=flight["small"][4] if group == "in" else None)
        return flight[group][4]

    def small(packed):
        land = lax.dynamic_update_slice(lax.empty((N_DEV,) + packed.shape, F32), packed[None], (me, 0, 0))
        flight["small"] = _split_start([packed], [land], _plan_broadcast, N_DEV - 1, "small_start")

    grad_x, G, packed = _local_step(x[0], p[0, 0], positions[0], loss_target[0], W, P,
                                    comm={"token0": after, "late_weights": late_weights, "grads": grads, "small": small})

    me_arr = me.astype(jnp.int32).reshape(1)
    big_out = {}

    def finish(group, after):
        mine, recv = _split_wait(*flight[group][:4], after, _plan_scatter, "grads_" + group + "_wait")
        for n, g, r in zip(GRAD_GROUPS[group], mine, recv):
            big_out[n] = _adam(r, stored(n), stored(n, "m_"), stored(n, "v_"), "adam_" + n, own=g, own_idx=me_arr)
        return big_out[GRAD_GROUPS[group][-1]][0]

    done = finish("ffn", grad_x)
    _, (small_all,) = _split_wait(*flight["small"][:4], done, _plan_broadcast, "small_wait")
    conv_sum, loss_row, small_out = _adam_small(small_all, [(args[n], args["m_" + n], args["v_" + n]) for n, _ in REPL_W])
    finish("in", finish("mla", done))
    conv_grad = lax.dynamic_slice_in_dim(conv_sum, me * 192, 192, axis=1)
    conv_out = _adam(conv_grad[None], conv_sh, m_ssd_conv_w[0], v_ssd_conv_w[0], "adam_conv")
    small_map = {n: small_out[i] for i, (n, _) in enumerate(REPL_W)}

    def outputs(idx):
        res = []
        for n in WEIGHT_ORDER:
            if n == "ssd_conv_w":
                res.append(conv_out[idx][None])
            elif n in big_out:
                res.append((jnp.transpose(big_out[n][idx]) if n in TRANSPOSED else big_out[n][idx])[None])
            else:
                res.append(small_map[n][idx])
        return res

    return (loss_row[0, 0], grad_x[None], *outputs(0), *outputs(1), *outputs(2), *outputs(3))
```

```python
import functools
import math

import numpy as np
import jax
import jax.numpy as jnp
from jax import lax
from jax.experimental import pallas as pl
from jax.experimental.pallas import tpu as pltpu

F32 = jnp.float32
BF16 = jnp.bfloat16
HI = lax.Precision.HIGHEST

N_DEV = 8
D_MODEL = 1024
PLE_DIM = 256
SSD_HEADS = 16
SSD_HEAD_DIM = 64
SSD_INNER = 1024
SSD_STATE = 128
SSD_XBC = 1536
SSD_CHUNK = 128
MLA_HEADS = 16
MLA_Q_RANK = 384
MLA_KV_RANK = 256
MLA_NOPE = 64
MLA_ROPE = 32
MLA_V = 64
ROPE_BASE = 10000.0
D_FF = 2816
IN_WIDTH = 3248
IN_PAD = 3456
ALPHA = 2.0 ** 0.25
EPS = 1e-6
LN_EPS = 1e-5
ATT_SCALE = 1.0 / math.sqrt(MLA_NOPE + MLA_ROPE)
ADAM_LR, ADAM_B1, ADAM_B2, ADAM_EPS, ADAM_WD, ADAM_STEP = 0.001, 0.9, 0.999, 1e-08, 0.01, 10

LANE = 128
MXU_DIM = 256
MM_TM, MM_TN, MM_TK = 1408, 1408, 2048
ROW_TILE = 256
ATT_TQ = 256

GRAD_DT = BF16

BIG = ("w_in", "mla_w_q_b", "mla_w_kv_b", "w_out", "w_ffn_gate", "w_ffn_up", "w_ffn_down", "w_ple_gate", "w_ple_proj")
EARLY = ("w_in", "mla_w_q_b", "mla_w_kv_b")
LATE = {"out": ("w_out", "w_ple_gate", "w_ple_proj"), "ffn": ("w_ffn_gate", "w_ffn_up", "w_ffn_down")}
GRAD_GROUPS = {"ffn": ("w_ffn_gate", "w_ffn_up", "w_ffn_down", "w_ple_gate", "w_ple_proj", "w_out"),
               "mla": ("mla_w_q_b", "mla_w_kv_b"), "in": ("w_in",)}
ROW_SHARDED = ("w_out", "w_ffn_down", "w_ple_gate")
TRANSPOSED = ("w_in", "mla_w_q_b", "w_ffn_gate", "w_ffn_up")
WEIGHT_ORDER = ("w_in", "ssd_conv_w", "ssd_conv_b", "ssd_dt_bias", "ssd_A_log", "ssd_D", "ssd_norm_w", "mla_q_norm_w",
                "mla_w_q_b", "mla_kv_norm_w", "mla_w_kv_b", "mla_out_norm_w", "w_out", "ln_mix_g", "ln_mix_b",
                "w_ffn_gate", "w_ffn_up", "w_ffn_down", "w_ple_gate", "w_ple_proj", "ln_ffn_g", "ln_ffn_b")


def _tile(dim, cap, prefer=None):
    cands = [t for t in range(LANE, min(cap, dim) + 1, LANE) if dim % t == 0]
    if not cands:
        return dim
    if prefer is None:
        return max(cands)
    fill = lambda t: t / (MXU_DIM * -(-t // MXU_DIM))
    good = min(0.9, max(fill(t) for t in cands))
    return min((t for t in cands if fill(t) >= good), key=lambda t: abs(t - prefer))


def _dot(a, b, dims=(((1,), (0,)), ((), ())), precision=None):
    return lax.dot_general(a, b, dims, preferred_element_type=F32, precision=precision)


_NT = (((1,), (1,)), ((), ()))
_TN = (((0,), (0,)), ((), ()))


def _mm(a, b, *, ta=False, tb=False, add=None, out_dtype=F32, after=None, epilogue=None, name):
    k_dim, m_dim = a.shape if ta else a.shape[::-1]
    n_dim, kb = b.shape if tb else b.shape[::-1]
    assert k_dim == kb
    tm, tn, tk = _tile(m_dim, MM_TM), _tile(n_dim, MM_TN, prefer=1024), _tile(k_dim, MM_TK, prefer=MM_TK)
    nk = k_dim // tk
    dims = (((0 if ta else 1,), (1 if tb else 0,)), ((), ()))
    a_spec = pl.BlockSpec((tk, tm), lambda i, j, k: (k, i)) if ta else pl.BlockSpec((tm, tk), lambda i, j, k: (i, k))
    b_spec = pl.BlockSpec((tn, tk), lambda i, j, k: (j, k)) if tb else pl.BlockSpec((tk, tn), lambda i, j, k: (k, j))
    o_spec = pl.BlockSpec((tm, tn), lambda i, j, k: (i, j))
    epi_fn, epi_in, out_dtypes = epilogue if epilogue else (None, [], [out_dtype])
    tiles = ([add] if add is not None else []) + list(epi_in)
    n_out = len(out_dtypes)

    def body(*refs):
        a_ref, b_ref = refs[:2]
        tile_refs = refs[2:2 + len(tiles)]
        out_refs = refs[len(refs) - n_out - (nk > 1):len(refs) - (nk > 1)]
        part = _dot(a_ref[...].astype(BF16), b_ref[...].astype(BF16), dims)
        if add is not None:
            part_add = lambda v: v + tile_refs[0][...]
        else:
            part_add = lambda v: v

        def write(total):
            extra = [r[...] for r in tile_refs[add is not None:]]
            outs = epi_fn(total, *extra) if epi_fn else (total,)
            for o_ref, val in zip(out_refs, outs):
                o_ref[...] = val.astype(o_ref.dtype)

        if nk == 1:
            write(part_add(part))
            return
        acc = refs[-1]
        k = pl.program_id(2)

        @pl.when(k == 0)
        def _():
            acc[...] = part_add(part)

        @pl.when(k > 0)
        def _():
            acc[...] += part

        @pl.when(k == nk - 1)
        def _():
            write(acc[...])

    ins = [a, b] + tiles + ([after] if after is not None else [])
    specs = [a_spec, b_spec] + [o_spec] * len(tiles) + ([pl.BlockSpec(memory_space=pl.ANY)] if after is not None else [])
    res = pl.pallas_call(
        body, name=name, grid=(m_dim // tm, n_dim // tn, nk), in_specs=specs, out_specs=[o_spec] * n_out,
        out_shape=[jax.ShapeDtypeStruct((m_dim, n_dim), dt) for dt in out_dtypes],
        scratch_shapes=[pltpu.VMEM((tm, tn), F32)] if nk > 1 else [],
        compiler_params=pltpu.CompilerParams(dimension_semantics=("parallel", "parallel", "arbitrary")),
    )(*ins)
    return res if epilogue else res[0]


def _rowwise(fn, rows, consts, out_widths, acc_widths=(), *, name, tr=ROW_TILE):
    row_arrays, row_specs = [], []
    first_arr = rows[0][0] if isinstance(rows[0], tuple) else rows[0]
    s_dim = first_arr.shape[-2]
    tr = min(tr, s_dim)
    for r in rows:
        arr, width, cb = r if isinstance(r, tuple) else (r, r.shape[-1], 0)
        row_arrays.append(arr)
        if arr.ndim == 3:
            row_specs.append(pl.BlockSpec((None, tr, width), functools.partial(lambda i, k: (k, i, 0), k=cb)))
        else:
            row_specs.append(pl.BlockSpec((tr, width), functools.partial(lambda i, cb: (i, cb), cb=cb)))
    const_specs = [pl.BlockSpec(c.shape, lambda i: (0, 0)) for c in consts]
    nr, nc, no, na = len(rows), len(consts), len(out_widths), len(acc_widths)

    def body(*refs):
        ins = [r[...] for r in refs[:nr + nc]]
        res = fn(*ins)
        if not isinstance(res, (tuple, list)):
            res = (res,)
        out_refs = refs[nr + nc:nr + nc + no]
        acc_refs = refs[nr + nc + no:]
        for o_ref, val in zip(out_refs, res[:no]):
            o_ref[...] = val.astype(o_ref.dtype)
        first = pl.program_id(0) == 0
        for a_ref, val in zip(acc_refs, res[no:]):
            @pl.when(first)
            def _(a_ref=a_ref, val=val):
                a_ref[...] = val

            @pl.when(jnp.logical_not(first))
            def _(a_ref=a_ref, val=val):
                a_ref[...] += val

    outs = [w if isinstance(w, tuple) else (w, F32) for w in out_widths]
    out_shape = [jax.ShapeDtypeStruct((s_dim, w), dt) for w, dt in outs]
    out_shape += [jax.ShapeDtypeStruct((1, w), F32) for w in acc_widths]
    out_specs = [pl.BlockSpec((tr, w), lambda i: (i, 0)) for w, _ in outs]
    out_specs += [pl.BlockSpec((1, w), lambda i: (0, 0)) for w in acc_widths]
    res = pl.pallas_call(
        body, name=name, grid=(s_dim // tr,), in_specs=row_specs + const_specs, out_specs=out_specs, out_shape=out_shape,
        compiler_params=pltpu.CompilerParams(dimension_semantics=("arbitrary",)),
    )(*row_arrays, *consts)
    return res


def _colsum(v):
    return jnp.sum(v, axis=0, keepdims=True)


def _rms(u, g):
    return u * lax.rsqrt(jnp.mean(u * u, axis=-1, keepdims=True) + EPS) * g


def _ln(u, g, b):
    mu = jnp.mean(u, axis=-1, keepdims=True)
    d = u - mu
    var = jnp.mean(d * d, axis=-1, keepdims=True)
    return d * lax.rsqrt(var + LN_EPS) * g + b


def _sigmoid(v):
    return 1.0 / (1.0 + jnp.exp(-v))


def _silu(v):
    return v * _sigmoid(v)


def _softplus(v):
    y = jnp.exp(-jnp.abs(v))
    w = 1.0 + y
    log1p = jnp.where(w == 1.0, y, jnp.log(w) * y / jnp.where(w == 1.0, 1.0, w - 1.0))
    return jnp.maximum(v, 0.0) + log1p


def _gate_rms(y, z, w):
    return _rms(y * _silu(z), w)


def _rms_bwd(u, g, d_out):
    r = lax.rsqrt(jnp.mean(u * u, axis=-1, keepdims=True) + EPS)
    n = u * r
    gd = d_out * g
    return r * (gd - n * jnp.mean(gd * n, axis=-1, keepdims=True)), _colsum(d_out * n)


def _ln_parts(u):
    d = u - jnp.mean(u, axis=-1, keepdims=True)
    r = lax.rsqrt(jnp.mean(d * d, axis=-1, keepdims=True) + LN_EPS)
    return d * r, r


def _ln_bwd(n, r, g, d_out):
    gd = d_out * g
    d_u = r * (gd - jnp.mean(gd, axis=-1, keepdims=True) - n * jnp.mean(gd * n, axis=-1, keepdims=True))
    return d_u, _colsum(d_out * n), _colsum(d_out)


def _conv_pre(cur, prev, w, b, first):
    row = lax.broadcasted_iota(jnp.int32, cur.shape, 0)
    acc = cur * w[3:4, :] + b
    for j in (1, 2, 3):
        tail = jnp.where(first, 0.0, pltpu.roll(prev, j, 0))
        acc = acc + jnp.where(row >= j, pltpu.roll(cur, j, 0), tail) * w[3 - j:4 - j, :]
    return acc


def _conv_fwd(u, ucb, w, b, name="conv_fwd"):
    s_dim, width = u.shape[0], w.shape[1]
    tr = min(ROW_TILE, s_dim)

    def body(cur_ref, prev_ref, w_ref, b_ref, o_ref):
        pre = _conv_pre(cur_ref[...], prev_ref[...], w_ref, b_ref[...], pl.program_id(0) == 0)
        o_ref[...] = _silu(pre)

    return pl.pallas_call(
        body, name=name, grid=(s_dim // tr,),
        in_specs=[pl.BlockSpec((tr, width), lambda i: (i, ucb)),
                  pl.BlockSpec((tr, width), lambda i: (jnp.maximum(i - 1, 0), ucb)),
                  pl.BlockSpec(w.shape, lambda i: (0, 0)), pl.BlockSpec(b.shape, lambda i: (0, 0))],
        out_specs=pl.BlockSpec((tr, width), lambda i: (i, 0)), out_shape=jax.ShapeDtypeStruct((s_dim, width), F32),
        compiler_params=pltpu.CompilerParams(dimension_semantics=("arbitrary",)),
    )(u, u, w, b)


def _conv_bwd_pre(u, ucb, w, b, dact, name="conv_bwd_pre"):
    s_dim, width = u.shape[0], w.shape[1]
    tr = min(ROW_TILE, s_dim)

    def body(cur_ref, prev_ref, w_ref, b_ref, d_ref, da_ref, dw_ref, db_ref):
        first = pl.program_id(0) == 0
        cur, prev = cur_ref[...], prev_ref[...]
        pre = _conv_pre(cur, prev, w_ref, b_ref[...], first)
        sg = _sigmoid(pre)
        da = d_ref[...] * (sg * (1.0 + pre * (1.0 - sg)))
        da_ref[...] = da
        row = lax.broadcasted_iota(jnp.int32, cur.shape, 0)

        @pl.when(first)
        def _():
            dw_ref[...] = jnp.zeros_like(dw_ref)
            db_ref[...] = jnp.zeros_like(db_ref)

        db_ref[...] += _colsum(da)
        dw_ref[3:4, :] += _colsum(da * cur)
        for j in (1, 2, 3):
            tail = jnp.where(first, 0.0, pltpu.roll(prev, j, 0))
            sh = jnp.where(row >= j, pltpu.roll(cur, j, 0), tail)
            dw_ref[3 - j:4 - j, :] += _colsum(da * sh)

    return pl.pallas_call(
        body, name=name, grid=(s_dim // tr,),
        in_specs=[pl.BlockSpec((tr, width), lambda i: (i, ucb)),
                  pl.BlockSpec((tr, width), lambda i: (jnp.maximum(i - 1, 0), ucb)),
                  pl.BlockSpec(w.shape, lambda i: (0, 0)), pl.BlockSpec(b.shape, lambda i: (0, 0)),
                  pl.BlockSpec((tr, width), lambda i: (i, 0))],
        out_specs=[pl.BlockSpec((tr, width), lambda i: (i, 0)), pl.BlockSpec(w.shape, lambda i: (0, 0)),
                   pl.BlockSpec(b.shape, lambda i: (0, 0))],
        out_shape=[jax.ShapeDtypeStruct((s_dim, width), F32), jax.ShapeDtypeStruct(w.shape, F32),
                   jax.ShapeDtypeStruct(b.shape, F32)],
        compiler_params=pltpu.CompilerParams(dimension_semantics=("arbitrary",)),
    )(u, u, w, b, dact)


def _conv_bwd_in(da, w, name="conv_bwd_in"):
    s_dim, width = da.shape
    tr = min(ROW_TILE, s_dim)
    n = s_dim // tr

    def body(cur_ref, nxt_ref, w_ref, o_ref):
        last = pl.program_id(0) == n - 1
        cur, nxt = cur_ref[...], nxt_ref[...]
        row = lax.broadcasted_iota(jnp.int32, cur.shape, 0)
        acc = cur * w_ref[3:4, :]
        for j in (1, 2, 3):
            head = jnp.where(last, 0.0, pltpu.roll(nxt, tr - j, 0))
            acc = acc + jnp.where(row < tr - j, pltpu.roll(cur, tr - j, 0), head) * w_ref[3 - j:4 - j, :]
        o_ref[...] = acc.astype(o_ref.dtype)

    return pl.pallas_call(
        body, name=name, grid=(n,),
        in_specs=[pl.BlockSpec((tr, width), lambda i: (i, 0)), pl.BlockSpec((tr, width), lambda i: (jnp.minimum(i + 1, n - 1), 0)),
                  pl.BlockSpec(w.shape, lambda i: (0, 0))],
        out_specs=pl.BlockSpec((tr, width), lambda i: (i, 0)), out_shape=jax.ShapeDtypeStruct((s_dim, width), BF16),
        compiler_params=pltpu.CompilerParams(dimension_semantics=("arbitrary",)),
    )(da, da, w)


def _sel_dot(a, sel, pieces, dims=(((1,), (0,)), ((), ())), sel_left=False):
    sel = sel.astype(BF16)
    acc, rest = None, a
    for _ in range(pieces):
        piece = rest.astype(BF16)
        rest = rest - piece.astype(F32)
        part = _dot(sel, piece, dims) if sel_left else _dot(piece, sel, dims)
        acc = part if acc is None else acc + part
    return acc


def _ssd_consts():
    L = SSD_CHUNK
    tri = np.tril(np.ones((L, L), np.float32))
    expand = np.zeros((LANE, SSD_INNER), np.float32)
    for h in range(SSD_HEADS):
        expand[h, h * SSD_HEAD_DIM:(h + 1) * SSD_HEAD_DIM] = 1.0
    return jnp.asarray(tri), jnp.asarray(expand), jnp.asarray(expand.T.copy())


def _ssd_prep(dt_ref, bias_ref, alog_ref, tri_ref, exp_ref, cs_s, cst_s, ex_s):
    L = SSD_CHUNK
    dt = _softplus(dt_ref[...] + bias_ref[...])
    a = -jnp.exp(alog_ref[...])
    cs = _sel_dot(dt * a, tri_ref[...], 3, sel_left=True)
    cs_s[...] = cs
    cst_s[...] = cs.T
    last = cs_s[L - 1:L, :]
    expand = exp_ref[...]
    ex_s[...] = _sel_dot(jnp.exp(cs), expand, 2)
    f_x = _sel_dot(jnp.exp(last - cs), expand, 2)
    dt_x = _sel_dot(dt, expand, 2)
    t_x = ex_s[L - 1:L, :]
    return dt, a, dt_x, f_x, t_x


def _decay_matrix(cs_s, cst_s, h, tril):
    seg = cs_s[:, h:h + 1] - cst_s[h:h + 1, :]
    return jnp.exp(jnp.where(tril, seg, -jnp.inf))


def _ssd_fwd(xbca, dtr, dtcb, bias, alog, d_x, name="ssd_fwd"):
    s_dim = xbca.shape[0]
    L = SSD_CHUNK
    nc = s_dim // L
    tri, expand, _ = _ssd_consts()

    def body(xs_ref, b_ref, c_ref, dt_ref, bias_ref, alog_ref, dx_ref, tri_ref, exp_ref,
             y_ref, st_ref, st_s, cs_s, cst_s, ex_s):
        @pl.when(pl.program_id(0) == 0)
        def _():
            st_s[...] = jnp.zeros_like(st_s)

        dt, a, dt_x, f_x, t_x = _ssd_prep(dt_ref, bias_ref, alog_ref, tri_ref, exp_ref, cs_s, cst_s, ex_s)
        st_ref[0] = st_s[...]
        row = lax.broadcasted_iota(jnp.int32, (L, L), 0)
        col = lax.broadcasted_iota(jnp.int32, (L, L), 1)
        tril = row >= col
        low = col < SSD_HEAD_DIM
        for g in range(2):
            bg = b_ref[:, g * LANE:(g + 1) * LANE]
            cg = c_ref[:, g * LANE:(g + 1) * LANE].astype(BF16)
            gmat = _dot(cg, bg.astype(BF16), _NT)
            bgt = bg.T.astype(BF16)
            for jj in range(4):
                j = 4 * g + jj
                sl = slice(j * LANE, (j + 1) * LANE)
                xp = xs_ref[:, sl]
                x_dt = xp * dt_x[:, sl]
                xb = x_dt.astype(BF16)
                yd = []
                for e in range(2):
                    lm = _decay_matrix(cs_s, cst_s, 2 * j + e, tril)
                    yd.append(_dot((gmat * lm).astype(BF16), xb))
                stp = st_s[j]
                z = _dot(cg, stp.astype(BF16))
                y_ref[:, sl] = jnp.where(low, yd[0], yd[1]) + ex_s[:, sl] * z + dx_ref[:, sl] * xp
                xf = (x_dt * f_x[:, sl]).astype(BF16)
                st_s[j] = t_x[:, sl] * stp + _dot(bgt, xf)

    const = lambda shape: pl.BlockSpec(shape, lambda c: tuple(0 for _ in shape))
    return pl.pallas_call(
        body, name=name, grid=(nc,),
        in_specs=[pl.BlockSpec((L, 1024), lambda c: (c, 0)), pl.BlockSpec((L, 256), lambda c: (c, 4)),
                  pl.BlockSpec((L, 256), lambda c: (c, 5)), pl.BlockSpec((L, LANE), lambda c: (c, dtcb)),
                  const((1, LANE)), const((1, LANE)), const((1, 1024)), const((L, L)), const((LANE, 1024))],
        out_specs=[pl.BlockSpec((L, 1024), lambda c: (c, 0)), pl.BlockSpec((1, 8, LANE, LANE), lambda c: (c, 0, 0, 0))],
        out_shape=[jax.ShapeDtypeStruct((s_dim, 1024), F32), jax.ShapeDtypeStruct((nc, 8, LANE, LANE), F32)],
        scratch_shapes=[pltpu.VMEM((8, LANE, LANE), F32), pltpu.VMEM((L, LANE), F32), pltpu.VMEM((LANE, L), F32),
                        pltpu.VMEM((L, 1024), F32)],
        compiler_params=pltpu.CompilerParams(dimension_semantics=("arbitrary",)),
    )(xbca, xbca, xbca, dtr, bias, alog, d_x, tri, expand)


def _ssd_bwd(xbca, dtr, dtcb, bias, alog, d_x, states, dy, name="ssd_bwd"):
    s_dim = xbca.shape[0]
    L = SSD_CHUNK
    nc = s_dim // L
    tri, expand, expand_t = _ssd_consts()

    def body(xs_ref, b_ref, c_ref, dt_ref, bias_ref, alog_ref, dx_ref, tri_ref, exp_ref, expt_ref,
             st_ref, dy_ref, dxbc_ref, ddt_ref, dbias_ref, dalog_ref, dd_ref,
             dst_s, cs_s, cst_s, ex_s, dcsx_s, ddtx_s, dcol_s, drow_s, dlast_s, dd_s):
        @pl.when(pl.program_id(0) == 0)
        def _():
            dst_s[...] = jnp.zeros_like(dst_s)
            dbias_ref[...] = jnp.zeros_like(dbias_ref)
            dalog_ref[...] = jnp.zeros_like(dalog_ref)
            dd_s[...] = jnp.zeros_like(dd_s)

        dt, a, dt_x, f_x, t_x = _ssd_prep(dt_ref, bias_ref, alog_ref, tri_ref, exp_ref, cs_s, cst_s, ex_s)
        row = lax.broadcasted_iota(jnp.int32, (L, L), 0)
        col = lax.broadcasted_iota(jnp.int32, (L, L), 1)
        tril = row >= col
        low = col < SSD_HEAD_DIM
        dcol_s[...] = jnp.zeros_like(dcol_s)
        drow_s[...] = jnp.zeros_like(drow_s)
        for g in range(2):
            bg = b_ref[:, g * LANE:(g + 1) * LANE]
            cg = c_ref[:, g * LANE:(g + 1) * LANE]
            bgb, cgb = bg.astype(BF16), cg.astype(BF16)
            gmat = _dot(cgb, bgb, _NT)
            d_g = jnp.zeros((L, L), F32)
            d_b = jnp.zeros((L, LANE), F32)
            d_c = jnp.zeros((L, LANE), F32)
            for jj in range(4):
                j = 4 * g + jj
                sl = slice(j * LANE, (j + 1) * LANE)
                xp = xs_ref[:, sl]
                dtp = dt_x[:, sl]
                x_dt = xp * dtp
                xb = x_dt.astype(BF16)
                dyp = dy_ref[:, sl]
                dd_s[:, sl] += _colsum(dyp * xp)
                d_xdt = jnp.zeros((L, LANE), F32)
                for e in range(2):
                    h = 2 * j + e
                    lm = _decay_matrix(cs_s, cst_s, h, tril)
                    m = gmat * lm
                    dye = jnp.where(low if e == 0 else jnp.logical_not(low), dyp, 0.0).astype(BF16)
                    d_m = jnp.where(tril, _dot(dye, xb, _NT), 0.0)
                    d_xdt = d_xdt + _dot(m.astype(BF16), dye, _TN)
                    d_g = d_g + d_m * lm
                    w = d_m * m
                    dcol_s[...] += jnp.where(col == h, jnp.sum(w, axis=1, keepdims=True), 0.0)
                    drow_s[...] += jnp.where(row == h, jnp.sum(w, axis=0, keepdims=True), 0.0)
                stp = st_ref[0, j]
                stb = stp.astype(BF16)
                dstn = dst_s[j]
                dstb = dstn.astype(BF16)
                e_p = ex_s[:, sl]
                f_p = f_x[:, sl]
                t_p = t_x[:, sl]
                z = _dot(cgb, stb)
                d_z = (e_p * dyp).astype(BF16)
                d_c = d_c + _dot(d_z, stb, _NT)
                d_xf = _dot(bgb, dstb)
                d_b = d_b + _dot((x_dt * f_p).astype(BF16), dstb, _NT)
                d_xdt = d_xdt + f_p * d_xf
                d_f = x_dt * d_xf * f_p
                dcsx_s[:, sl] = dyp * e_p * z - d_f
                dlast_s[:, sl] = _colsum(d_f) + _colsum(dstn * stp) * t_p
                dst_s[j] = _dot(cgb, d_z, _TN) + t_p * dstn
                dxbc_ref[:, sl] = dx_ref[:, sl] * dyp + d_xdt * dtp
                ddtx_s[:, sl] = d_xdt * xp
            d_gb = d_g.astype(BF16)
            dxbc_ref[:, 1024 + g * LANE:1024 + (g + 1) * LANE] = d_b + _dot(d_gb, cgb, _TN)
            dxbc_ref[:, 1280 + g * LANE:1280 + (g + 1) * LANE] = d_c + _dot(d_gb, bgb)

        expt = expt_ref[...]
        dlast = _sel_dot(jnp.broadcast_to(dlast_s[...], (8, 1024)), expt, 3)
        d_cs = dcol_s[...] - drow_s[...].T + _sel_dot(dcsx_s[...], expt, 3)
        rown = lax.broadcasted_iota(jnp.int32, (L, LANE), 0)
        d_cs = d_cs + jnp.where(rown == L - 1, jnp.sum(dlast, axis=0, keepdims=True) * 0.125, 0.0)
        d_da = _sel_dot(d_cs, tri_ref[...], 3, _TN, sel_left=True)
        d_dt = d_da * a + _sel_dot(ddtx_s[...], expt, 3)
        dalog_ref[...] += _colsum(d_da * dt) * a
        d_raw = d_dt * _sigmoid(dt_ref[...] + bias_ref[...])
        ddt_ref[...] = d_raw.astype(ddt_ref.dtype)
        dbias_ref[...] += _colsum(d_raw)
        dd8 = _sel_dot(jnp.broadcast_to(dd_s[...], (8, 1024)), expt, 3)
        dd_ref[...] = jnp.sum(dd8, axis=0, keepdims=True) * 0.125

    const = lambda shape: pl.BlockSpec(shape, lambda c: tuple(0 for _ in shape))
    rev = lambda cb: (lambda c: (nc - 1 - c, cb))
    return pl.pallas_call(
        body, name=name, grid=(nc,),
        in_specs=[pl.BlockSpec((L, 1024), rev(0)), pl.BlockSpec((L, 256), rev(4)), pl.BlockSpec((L, 256), rev(5)),
                  pl.BlockSpec((L, LANE), rev(dtcb)), const((1, LANE)), const((1, LANE)), const((1, 1024)), const((L, L)),
                  const((LANE, 1024)), const((1024, LANE)),
                  pl.BlockSpec((1, 8, LANE, LANE), lambda c: (nc - 1 - c, 0, 0, 0)), pl.BlockSpec((L, 1024), rev(0))],
        out_specs=[pl.BlockSpec((L, SSD_XBC), rev(0)), pl.BlockSpec((L, LANE), rev(0)), const((1, LANE)), const((1, LANE)),
                   const((1, LANE))],
        out_shape=[jax.ShapeDtypeStruct((s_dim, SSD_XBC), F32), jax.ShapeDtypeStruct((s_dim, LANE), BF16),
                   jax.ShapeDtypeStruct((1, LANE), F32), jax.ShapeDtypeStruct((1, LANE), F32),
                   jax.ShapeDtypeStruct((1, LANE), F32)],
        scratch_shapes=[pltpu.VMEM((8, LANE, LANE), F32), pltpu.VMEM((L, LANE), F32), pltpu.VMEM((LANE, L), F32),
                        pltpu.VMEM((L, 1024), F32), pltpu.VMEM((L, 1024), F32), pltpu.VMEM((L, 1024), F32),
                        pltpu.VMEM((L, LANE), F32), pltpu.VMEM((LANE, L), F32), pltpu.VMEM((1, 1024), F32),
                        pltpu.VMEM((1, 1024), F32)],
        compiler_params=pltpu.CompilerParams(dimension_semantics=("arbitrary",)),
    )(xbca, xbca, xbca, dtr, bias, alog, d_x, tri, expand, expand_t, states, dy)


def _swap_halves(u):
    width = u.shape[1]
    lane = lax.broadcasted_iota(jnp.int32, u.shape, 1)
    return jnp.where(lane % MLA_ROPE < MLA_ROPE // 2, pltpu.roll(u, width - MLA_ROPE // 2, 1), pltpu.roll(u, MLA_ROPE // 2, 1))


def _rope_fwd_fn(u, cos, sin):
    return u * cos + _swap_halves(u) * sin


def _rope_bwd_fn(d, cos, sin):
    return d * cos + _swap_halves(d * sin)


def _spread4(v):
    return v + pltpu.roll(v, 32, 1) + pltpu.roll(v, 64, 1) + pltpu.roll(v, 96, 1)


def _att_masks(tq):
    lane = lax.broadcasted_iota(jnp.int32, (tq, LANE), 1)
    return lane // MLA_NOPE, lane // MLA_ROPE


def _att_tile(i, tq):
    klen = (i + 1) * tq
    qpos = i * tq + lax.broadcasted_iota(jnp.int32, (tq, klen), 0)
    kpos = lax.broadcasted_iota(jnp.int32, (tq, klen), 1)
    return slice(i * tq, (i + 1) * tq), klen, qpos >= kpos


def _att_qcat(qn_t, qr_t, par, e, half_id, grp_id):
    return jnp.concatenate([jnp.where(half_id == par, qn_t * ATT_SCALE, 0.0), jnp.where(grp_id == e, qr_t * ATT_SCALE, 0.0)],
                           axis=1).astype(BF16)


def _att_softmax(scores, causal):
    s = jnp.where(causal, scores, -jnp.inf)
    e = jnp.exp(s - jnp.max(s, axis=1, keepdims=True))
    return e, 1.0 / jnp.sum(e, axis=1, keepdims=True)


def _att_specs(s_dim):
    col = lambda f: pl.BlockSpec((s_dim, LANE), lambda j: (0, f(j)))
    return [col(lambda j: j), col(lambda j: j // 2), col(lambda j: j), col(lambda j: 0), col(lambda j: 8 + j)]


def _att_fwd(q, qr, kv, krt, name="att_fwd"):
    s_dim = q.shape[0]
    tq = min(ATT_TQ, s_dim)

    def body(qn_ref, qr_ref, kn_ref, krt_ref, v_ref, o_ref, kcat_s, vb_s):
        e0 = 2 * (pl.program_id(0) % 2)
        half_id, grp_id = _att_masks(tq)
        kcat_s[...] = jnp.concatenate([kn_ref[...], krt_ref[...]], axis=1).astype(BF16)
        vb_s[...] = v_ref[...].astype(BF16)
        for i in range(s_dim // tq):
            rows, klen, causal = _att_tile(i, tq)
            qn_t, qr_t = qn_ref[rows, :], qr_ref[rows, :]
            scores = [_dot(_att_qcat(qn_t, qr_t, par, e0 + par, half_id, grp_id), kcat_s[0:klen, :], _NT) for par in range(2)]
            probs = [_att_softmax(s, causal) for s in scores]
            outs = [_dot(e.astype(BF16), vb_s[0:klen, :]) * inv_l for e, inv_l in probs]
            o_ref[rows, :] = jnp.where(half_id == 0, outs[0], outs[1])

    return pl.pallas_call(
        body, name=name, grid=(MLA_HEADS // 2,), in_specs=_att_specs(s_dim),
        out_specs=pl.BlockSpec((s_dim, LANE), lambda j: (0, j)), out_shape=jax.ShapeDtypeStruct((s_dim, 1024), F32),
        scratch_shapes=[pltpu.VMEM((s_dim, 2 * LANE), BF16), pltpu.VMEM((s_dim, LANE), BF16)],
        compiler_params=pltpu.CompilerParams(dimension_semantics=("parallel",)),
    )(q, qr, kv, krt, kv)


def _att_bwd(q, qr, kv, krt, o, do, name="att_bwd"):
    s_dim = q.shape[0]
    tq = min(ATT_TQ, s_dim)

    def body(qn_ref, qr_ref, kn_ref, krt_ref, v_ref, o_ref, do_ref, dqn_ref, dqr_ref, dkn_ref, dv_ref, dkrt_ref,
             kcat_s, vb_s):
        e0 = 2 * (pl.program_id(0) % 2)
        half_id, grp_id = _att_masks(tq)
        kcat_s[...] = jnp.concatenate([kn_ref[...], krt_ref[...]], axis=1).astype(BF16)
        vb_s[...] = v_ref[...].astype(BF16)
        dkn_ref[...] = jnp.zeros_like(dkn_ref)
        dv_ref[...] = jnp.zeros_like(dv_ref)
        dkrt_ref[...] = jnp.zeros_like(dkrt_ref)
        for i in range(s_dim // tq):
            rows, klen, causal = _att_tile(i, tq)
            qn_t, qr_t, o_t, do_t = qn_ref[rows, :], qr_ref[rows, :], o_ref[rows, :], do_ref[rows, :]
            heads = range(2)
            qcats = [_att_qcat(qn_t, qr_t, par, e0 + par, half_id, grp_id) for par in heads]
            scores = [_dot(qcats[par], kcat_s[0:klen, :], _NT) for par in heads]
            doms = [jnp.where(half_id == par, do_t, 0.0) for par in heads]
            dombs = [d.astype(BF16) for d in doms]
            d_ps = [_dot(dombs[par], vb_s[0:klen, :], _NT) for par in heads]
            probs = []
            for par in heads:
                e, inv_l = _att_softmax(scores[par], causal)
                probs.append(e * inv_l)
            d_ss = []
            for par in heads:
                d_row = jnp.sum(doms[par] * o_t, axis=1, keepdims=True)
                d_ss.append((probs[par] * (d_ps[par] - d_row)).astype(BF16))
            dqcats = [_dot(d_ss[par], kcat_s[0:klen, :]) * ATT_SCALE for par in heads]
            dkcats = [_dot(d_ss[par], qcats[par], _TN) for par in heads]
            dvs = [_dot(probs[par].astype(BF16), dombs[par], _TN) for par in heads]
            dqn_ref[rows, :] = jnp.where(half_id == 0, dqcats[0][:, :LANE], dqcats[1][:, :LANE]).astype(dqn_ref.dtype)
            dqr_ref[rows, :] = (jnp.where(grp_id == e0, dqcats[0][:, LANE:], 0.0)
                                + jnp.where(grp_id == e0 + 1, dqcats[1][:, LANE:], 0.0))
            dkn_ref[0:klen, :] += dkcats[0][:, :LANE] + dkcats[1][:, :LANE]
            dkrt_ref[0:klen, :] += dkcats[0][:, LANE:] + dkcats[1][:, LANE:]
            dv_ref[0:klen, :] += dvs[0] + dvs[1]

    col = lambda f: pl.BlockSpec((s_dim, LANE), lambda j: (0, f(j)))
    return pl.pallas_call(
        body, name=name, grid=(MLA_HEADS // 2,), in_specs=_att_specs(s_dim) + [col(lambda j: j), col(lambda j: j)],
        out_specs=[col(lambda j: j), pl.BlockSpec((None, s_dim, LANE), lambda j: (j % 2, 0, j // 2)), col(lambda j: j),
                   col(lambda j: j), pl.BlockSpec((None, s_dim, LANE), lambda j: (j, 0, 0))],
        out_shape=[jax.ShapeDtypeStruct((s_dim, 1024), BF16), jax.ShapeDtypeStruct((2, s_dim, 512), F32),
                   jax.ShapeDtypeStruct((s_dim, 1024), F32), jax.ShapeDtypeStruct((s_dim, 1024), F32),
                   jax.ShapeDtypeStruct((MLA_HEADS // 2, s_dim, LANE), F32)],
        scratch_shapes=[pltpu.VMEM((s_dim, 2 * LANE), BF16), pltpu.VMEM((s_dim, LANE), BF16)],
        compiler_params=pltpu.CompilerParams(dimension_semantics=("parallel",)),
    )(q, qr, kv, krt, kv, o, do)


def _gather_many(shards, name):
    n_arr = len(shards)

    def body(*refs):
        x_refs, out_refs = refs[:n_arr], refs[n_arr:2 * n_arr]
        send_sems, recv_sems, local_sems = refs[2 * n_arr:]
        x_i, y_i, c_i = lax.axis_index("x"), lax.axis_index("y"), lax.axis_index("c")
        me, sibling = (x_i, y_i, c_i), (x_i, y_i, 1 - c_i)
        chips = [(1 - x_i, y_i), (x_i, 1 - y_i), (1 - x_i, 1 - y_i)]

        def copy(a, k, block, to, src=None):
            slot = out_refs[a].at[4 * block[0] + 2 * block[1] + block[2]]
            return pltpu.make_async_remote_copy(
                src_ref=slot if src is None else src, dst_ref=slot, send_sem=send_sems.at[a, k],
                recv_sem=recv_sems.at[a, k], device_id=to, device_id_type=pl.DeviceIdType.MESH)

        mine, first, passed = [], [], []
        for a in range(n_arr):
            mine.append(pltpu.make_async_copy(x_refs[a], out_refs[a].at[4 * x_i + 2 * y_i + c_i], local_sems.at[a]))
            mine[a].start()
            first.append([copy(a, 0, me, sibling, src=x_refs[a])]
                         + [copy(a, 1 + j, me, (*chip, c_i), src=x_refs[a]) for j, chip in enumerate(chips)])
            for cp in first[a]:
                cp.start()
            passed.append([copy(a, 4 + j, (*chip, c_i), sibling) for j, chip in enumerate(chips)])
        for j, chip in enumerate(chips):
            for a in range(n_arr):
                copy(a, 1 + j, (*chip, c_i), me).wait_recv()
                passed[a][j].start()
        for a in range(n_arr):
            copy(a, 0, sibling, me).wait_recv()
            for j, chip in enumerate(chips):
                copy(a, 4 + j, (*chip, 1 - c_i), me).wait_recv()
        for a in range(n_arr):
            for cp in first[a] + passed[a]:
                cp.wait_send()
            mine[a].wait()

    any_spec = pl.BlockSpec(memory_space=pl.ANY)
    return pl.pallas_call(
        body, name=name, out_shape=[jax.ShapeDtypeStruct((N_DEV,) + x.shape, x.dtype) for x in shards],
        in_specs=[any_spec] * n_arr, out_specs=[any_spec] * n_arr,
        scratch_shapes=[pltpu.SemaphoreType.DMA((n_arr, 7)), pltpu.SemaphoreType.DMA((n_arr, 7)),
                        pltpu.SemaphoreType.DMA((n_arr,))],
    )(*shards)


_HBM = pl.BlockSpec(memory_space=pltpu.HBM)
_SEM = pl.BlockSpec(memory_space=pltpu.SEMAPHORE)


def _plan_copies(plan, src_refs, land_refs, send_sems, recv_sems):
    copies = []
    for s_ref, l_ref in zip(src_refs, land_refs):
        for src, dst, peer in plan(s_ref, l_ref):
            k = len(copies)
            copies.append(pltpu.make_async_remote_copy(
                src_ref=src, dst_ref=dst, send_sem=send_sems.at[k], recv_sem=recv_sems.at[k], device_id=peer,
                device_id_type=pl.DeviceIdType.MESH))
    return copies


def _split_start(srcs, lands, plan, n_copy, name, after=None):
    n = len(srcs)
    n_in = 2 * n + (after is not None)

    def body(*refs):
        for cp in _plan_copies(plan, refs[:n], refs[n:2 * n], refs[n_in], refs[n_in + 1]):
            cp.start()
        refs[-1][...] = jnp.zeros_like(refs[-1])

    sems = pltpu.SemaphoreType.DMA((n * n_copy,))
    res = pl.pallas_call(
        body, name=name,
        out_shape=(sems, sems, *[pltpu.HBM(a.shape, a.dtype) for a in list(srcs) + list(lands)],
                   jax.ShapeDtypeStruct((8, LANE), F32)),
        in_specs=[_HBM] * (2 * n) + [pl.BlockSpec(memory_space=pl.ANY)] * (after is not None),
        out_specs=(_SEM, _SEM, *[_HBM] * (2 * n), pl.BlockSpec(memory_space=pltpu.VMEM)),
        input_output_aliases={i: 2 + i for i in range(2 * n)},
        compiler_params=pltpu.CompilerParams(has_side_effects=pltpu.SideEffectType.DATAFLOW_SIDE_EFFECTING),
    )(*[pltpu.with_memory_space_constraint(a, pltpu.HBM) for a in list(srcs) + list(lands)],
      *([after] if after is not None else []))
    return res[0], res[1], list(res[2:2 + n]), list(res[2 + n:2 + 2 * n]), res[-1]


def _split_wait(send_sems, recv_sems, srcs, lands, after, plan, name):
    n = len(srcs)

    def body(*refs):
        copies = _plan_copies(plan, refs[:n], refs[n:2 * n], refs[2 * n], refs[2 * n + 1])
        for cp in copies:
            cp.wait_send()
        for cp in copies:
            cp.wait_recv()

    res = pl.pallas_call(
        body, name=name, out_shape=tuple(pltpu.HBM(a.shape, a.dtype) for a in list(srcs) + list(lands)),
        in_specs=[_HBM] * (2 * n) + [_SEM, _SEM, pl.BlockSpec(memory_space=pl.ANY)], out_specs=tuple([_HBM] * (2 * n)),
        input_output_aliases={i: i for i in range(2 * n)},
        compiler_params=pltpu.CompilerParams(has_side_effects=pltpu.SideEffectType.DATAFLOW_SIDE_EFFECTING),
    )(*srcs, *lands, send_sems, recv_sems, after)
    return list(res[:n]), list(res[n:])


def _plan_broadcast(src, land):
    x_i, y_i, c_i = lax.axis_index("x"), lax.axis_index("y"), lax.axis_index("c")
    me = 4 * x_i + 2 * y_i + c_i
    return [(src, land.at[me], (x_i ^ (k >> 2), y_i ^ ((k >> 1) & 1), c_i ^ (k & 1))) for k in range(1, N_DEV)]


def _plan_scatter(src, land):
    x_i, y_i, c_i = lax.axis_index("x"), lax.axis_index("y"), lax.axis_index("c")
    me = 4 * x_i + 2 * y_i + c_i
    plan = []
    for k in range(1, N_DEV):
        px, py, pc = x_i ^ (k >> 2), y_i ^ ((k >> 1) & 1), c_i ^ (k & 1)
        plan.append((src.at[4 * px + 2 * py + pc], land.at[me], (px, py, pc)))
    return plan


def _adam_math(g, w, m, v):
    m_new = ADAM_B1 * m + (1.0 - ADAM_B1) * g
    v_new = ADAM_B2 * v + (1.0 - ADAM_B2) * (g * g)
    m_hat = m_new / (1.0 - ADAM_B1 ** ADAM_STEP)
    v_hat = v_new / (1.0 - ADAM_B2 ** ADAM_STEP)
    return -ADAM_LR * (m_hat / (jnp.sqrt(v_hat) + ADAM_EPS) + ADAM_WD * w), m_new, v_new


def _adam(slots, w, m, v, name, own=None, own_idx=None):
    n_slot, rows, cols = slots.shape
    tr = ROW_TILE if rows % ROW_TILE == 0 else rows
    has_own = own is not None

    def body(*refs):
        if has_own:
            idx_ref, own_ref, refs = refs[0], refs[1], refs[2:]
        s_ref, w_ref, m_ref, v_ref, g_ref, d_ref, mo_ref, vo_ref = refs
        g = own_ref[...].astype(F32) if has_own else s_ref[0].astype(F32)
        for k in range(0 if has_own else 1, n_slot):
            part = s_ref[k].astype(F32)
            g = g + (jnp.where(idx_ref[0] == k, 0.0, part) if has_own else part)
        g_ref[...] = g
        d_ref[...], mo_ref[...], vo_ref[...] = _adam_math(g, w_ref[...], m_ref[...], v_ref[...])

    spec = pl.BlockSpec((tr, cols), lambda i, *_: (i, 0))
    in_specs = [pl.BlockSpec((n_slot, tr, cols), lambda i, *_: (0, i, 0)), spec, spec, spec]
    if has_own:
        in_specs = [pl.BlockSpec((None, tr, cols), lambda i, idx: (idx[0], i, 0))] + in_specs
    grid_spec = pltpu.PrefetchScalarGridSpec(num_scalar_prefetch=1 if has_own else 0, grid=(rows // tr,), in_specs=in_specs,
                                             out_specs=[spec] * 4)
    ins = ([own_idx, own] if has_own else []) + [slots, w, m, v]
    return pl.pallas_call(
        body, name=name, grid_spec=grid_spec, out_shape=[jax.ShapeDtypeStruct((rows, cols), F32)] * 4,
        compiler_params=pltpu.CompilerParams(dimension_semantics=("parallel",)),
    )(*ins)


PACK_ROWS, PACK_W = 24, 1536
REPL_W = (("ssd_conv_b", 1536), ("ssd_dt_bias", 16), ("ssd_A_log", 16), ("ssd_D", 16), ("ssd_norm_w", 1024),
          ("mla_q_norm_w", 384), ("mla_kv_norm_w", 256), ("mla_out_norm_w", 1024), ("ln_mix_g", 1024),
          ("ln_mix_b", 1024), ("ln_ffn_g", 1024), ("ln_ffn_b", 1024))
LOSS_ROW = 4 + len(REPL_W)


def _pack_small(conv_w_grad, grads, loss, name="pack_small"):
    def body(*refs):
        cw_ref, g_refs, loss_ref, o_ref = refs[0], refs[1:1 + len(REPL_W)], refs[1 + len(REPL_W)], refs[-1]
        o_ref[...] = jnp.zeros_like(o_ref)
        o_ref[0:4, :] = cw_ref[...]
        for i, g_ref in enumerate(g_refs):
            o_ref[4 + i:5 + i, 0:g_ref.shape[1]] = g_ref[...]
        o_ref[LOSS_ROW:LOSS_ROW + 1, 0:LANE] = loss_ref[...]

    return pl.pallas_call(body, name=name, out_shape=jax.ShapeDtypeStruct((PACK_ROWS, PACK_W), F32))(conv_w_grad, *grads, loss)


def _adam_small(gathered, wmv, name="adam_small"):
    def body(*refs):
        s_ref = refs[0]
        in_refs = refs[1:1 + 3 * len(REPL_W)]
        cw_ref, loss_ref = refs[1 + 3 * len(REPL_W)], refs[2 + 3 * len(REPL_W)]
        out_refs = refs[3 + 3 * len(REPL_W):-1]
        tot = refs[-1]
        acc = s_ref[0]
        for k in range(1, N_DEV):
            acc = acc + s_ref[k]
        tot[...] = acc
        cw_ref[...] = tot[0:4, :]
        loss_ref[...] = tot[LOSS_ROW:LOSS_ROW + 1, 0:LANE]
        for i, (_, width) in enumerate(REPL_W):
            g = tot[4 + i:5 + i, 0:width]
            w_ref, m_ref, v_ref = in_refs[3 * i:3 * i + 3]
            g_ref, d_ref, mo_ref, vo_ref = out_refs[4 * i:4 * i + 4]
            g_ref[...] = g
            d_ref[...], mo_ref[...], vo_ref[...] = _adam_math(g, w_ref[...], m_ref[...], v_ref[...])

    flat_in = [a for triple in wmv for a in triple]
    out_shape = [jax.ShapeDtypeStruct((4, PACK_W), F32), jax.ShapeDtypeStruct((1, LANE), F32)]
    for _, width in REPL_W:
        out_shape += [jax.ShapeDtypeStruct((1, width), F32)] * 4
    res = pl.pallas_call(body, name=name, out_shape=out_shape, scratch_shapes=[pltpu.VMEM((PACK_ROWS, PACK_W), F32)])(
        gathered, *flat_in)
    return res[0], res[1], [res[2 + 4 * i:6 + 4 * i] for i in range(len(REPL_W))]


def _cols_full(g):
    return jnp.transpose(g, (1, 0, 2)).reshape(g.shape[1], -1)


def _cols_split(full):
    k_dim, n_dim = full.shape
    return jnp.transpose(full.reshape(k_dim, N_DEV, n_dim // N_DEV), (1, 0, 2))


PROJ_BLOCK = {"z": (1024, 0), "dt": (LANE, 8), "q_c": (MLA_Q_RANK, 3), "xbc": (SSD_XBC, 1), "kv_c": (MLA_KV_RANK, 12),
              "k_rope": (LANE, 26)}


def _win_pad(wt):
    z = lambda n: jnp.zeros((n, wt.shape[1]), wt.dtype)
    return jnp.concatenate([wt[:1024], wt[2560:2576], z(112), wt[2576:2960], wt[1024:2560], wt[2960:3216], wt[3216:3248],
                            z(96)], axis=0)


def _win_unpad(wt):
    return jnp.concatenate([wt[:1024], wt[1536:3072], wt[1024:1040], wt[1152:1536], wt[3072:3328], wt[3328:3360]], axis=0)


def _heads_split_t(wt, a, b):
    w3 = wt.reshape(MLA_HEADS, a + b, wt.shape[1])
    return jnp.concatenate([w3[:, :a].reshape(-1, wt.shape[1]), w3[:, a:].reshape(-1, wt.shape[1])], axis=0)


def _heads_merge_t(wt, a, b):
    wa = wt[:MLA_HEADS * a].reshape(MLA_HEADS, a, wt.shape[1])
    wb = wt[MLA_HEADS * a:].reshape(MLA_HEADS, b, wt.shape[1])
    return jnp.concatenate([wa, wb], axis=1).reshape(-1, wt.shape[1])


def _heads_split(w, a, b):
    k_dim = w.shape[0]
    w3 = w.reshape(k_dim, MLA_HEADS, a + b)
    return jnp.concatenate([w3[:, :, :a].reshape(k_dim, -1), w3[:, :, a:].reshape(k_dim, -1)], axis=1)


def _heads_merge(w, a, b):
    k_dim = w.shape[0]
    wa = w[:, :MLA_HEADS * a].reshape(k_dim, MLA_HEADS, a)
    wb = w[:, MLA_HEADS * a:].reshape(k_dim, MLA_HEADS, b)
    return jnp.concatenate([wa, wb], axis=2).reshape(k_dim, -1)


def _pad_lanes(v, width=LANE):
    return jnp.concatenate([v, jnp.zeros((v.shape[0], width - v.shape[1]), v.dtype)], axis=1)


def _local_step(x, p, positions, tgt, W, P, comm=None):
    comm = comm or {}
    zero_tok = jnp.zeros((8, LANE), F32)
    s_dim = x.shape[0]
    inv_freq = 1.0 / (ROPE_BASE ** (jnp.arange(0, MLA_ROPE, 2, dtype=F32) / MLA_ROPE))
    ang = positions.astype(F32)[:, None] * inv_freq
    cos, sin = jnp.cos(ang), jnp.sin(ang)
    cos32 = jnp.concatenate([cos, cos], axis=1)
    sin32 = jnp.concatenate([-sin, sin], axis=1)
    cos512, sin512 = jnp.tile(cos32, (1, 16)), jnp.tile(sin32, (1, 16))
    cos128, sin128 = jnp.tile(cos32, (1, 4)), jnp.tile(sin32, (1, 4))
    bias_p, alog_p = _pad_lanes(P["ssd_dt_bias"]), _pad_lanes(P["ssd_A_log"])
    d_x = jnp.repeat(P["ssd_D"], SSD_HEAD_DIM, axis=1)

    xb, pb = x.astype(BF16), p.astype(BF16)
    proj = _mm(xb, W["w_in"], tb=True, after=comm.get("token0", zero_tok), name="mm_in")
    z, qc, kvc, kr = [(proj,) + PROJ_BLOCK[n] for n in ("z", "q_c", "kv_c", "k_rope")]
    xbca = _conv_fwd(proj, PROJ_BLOCK["xbc"][1], P["ssd_conv_w"], P["ssd_conv_b"])
    y, states = _ssd_fwd(xbca, proj, PROJ_BLOCK["dt"][1], bias_p, alog_p, d_x)
    (yssd,) = _rowwise(_gate_rms, [y, z], [P["ssd_norm_w"]], [(1024, BF16)], name="ssd_gate_norm")
    qn, kvn, krt = _rowwise(lambda a, c, u, cs, sn, wq, wkv: (_rms(a, wq), _rms(c, wkv), _spread4(_rope_fwd_fn(u, cs, sn))),
                            [qc, kvc, kr, cos128, sin128], [P["mla_q_norm_w"], P["mla_kv_norm_w"]],
                            [(MLA_Q_RANK, BF16), (MLA_KV_RANK, BF16), LANE], name="qkv_norm_rope_k")
    q = _mm(qn, W["mla_w_q_b"], tb=True, name="mm_q")
    kv = _mm(kvn, W["mla_w_kv_b"], name="mm_kv")
    (qr,) = _rowwise(_rope_fwd_fn, [(q, 512, 2), cos512, sin512], [], [512], name="rope_q")
    att = _att_fwd(q, qr, kv, krt)
    (ymla,) = _rowwise(_rms, [att], [P["mla_out_norm_w"]], [(1024, BF16)], name="out_norm")
    ycat = jnp.concatenate([yssd, ymla], axis=1)
    if "late_weights" in comm:
        W = {**W, **comm["late_weights"]("out", ycat)}
    mix = _mm(ycat, W["w_out"], name="mm_out")
    f_h1 = lambda xv, mv, g, b: _ln(ALPHA * xv + mv, g, b)
    h1, h1b = _rowwise(lambda *a: (f_h1(*a),) * 2, [x, mix], [P["ln_mix_g"], P["ln_mix_b"]], [1024, (1024, BF16)],
                       name="ln_mix")
    if "late_weights" in comm:
        W = {**W, **comm["late_weights"]("ffn", h1b)}
    hg = _mm(h1b, W["w_ffn_gate"], tb=True, out_dtype=BF16, name="mm_gate")
    hu, act = _mm(h1b, W["w_ffn_up"], tb=True, name="mm_up",
                  epilogue=(lambda u, g: (u, _silu(g.astype(F32)) * u), [hg], [BF16, BF16]))
    pg = _mm(h1b, W["w_ple_gate"], name="mm_ple_gate")
    pp = _mm(pb, W["w_ple_proj"], name="mm_ple")
    ffn = _mm(act, W["w_ffn_down"], name="mm_down")

    def final_fn(hv, fv, pg, ppv, tv, g, b):
        sg = _sigmoid(pg)
        n, r = _ln_parts(ALPHA * hv + fv + sg * ppv)
        diff = n * g + b - tv
        loss = 0.5 * jnp.sum(jnp.mean(diff * diff, axis=-1, keepdims=True), axis=0, keepdims=True)
        d_pre, d_g, d_b = _ln_bwd(n, r, g, diff * (1.0 / D_MODEL))
        return (ALPHA * d_pre, d_pre, d_pre * ppv * (sg * (1.0 - sg)), d_pre * sg, d_g, d_b,
                jnp.broadcast_to(loss, (1, LANE)))

    dh1_a, dffn, dpg, dpp, g_ffn_g, g_ffn_b, loss = _rowwise(
        final_fn, [h1, ffn, pg, pp, tgt], [P["ln_ffn_g"], P["ln_ffn_b"]], [1024] + [(1024, BF16)] * 3,
        [1024, 1024, LANE], name="final")

    G = {}
    def swiglu_bwd(d, g, u):
        g, u = g.astype(F32), u.astype(F32)
        sg = _sigmoid(g)
        return d * u * (sg * (1.0 + g * (1.0 - sg))), d * (g * sg)

    dg, du = _mm(dffn, W["w_ffn_down"], tb=True, name="mm_down_dx",
                 epilogue=(swiglu_bwd, [hg, hu], [BF16, BF16]))
    G["w_ffn_down"] = _mm(act, dffn, ta=True, out_dtype=GRAD_DT, name="mm_down_dw")
    dh1 = _mm(dg, W["w_ffn_gate"], add=dh1_a, name="mm_gate_dx")
    dh1 = _mm(du, W["w_ffn_up"], add=dh1, name="mm_up_dx")
    dh1 = _mm(dpg, W["w_ple_gate"], tb=True, add=dh1, name="mm_ple_gate_dx")
    G["w_ffn_gate"] = _mm(dg, h1b, ta=True, out_dtype=GRAD_DT, name="mm_gate_dw")
    G["w_ffn_up"] = _mm(du, h1b, ta=True, out_dtype=GRAD_DT, name="mm_up_dw")
    G["w_ple_gate"] = _mm(h1b, dpg, ta=True, out_dtype=GRAD_DT, name="mm_ple_gate_dw")
    G["w_ple_proj"] = _mm(pb, dpp, ta=True, out_dtype=GRAD_DT, name="mm_ple_dw")
    def ln_mix_bwd(xv, mv, dv, g, b):
        n, r = _ln_parts(ALPHA * xv + mv)
        d_pre, d_g, d_b = _ln_bwd(n, r, g, dv)
        return ALPHA * d_pre, d_pre, d_g, d_b

    dx_a, dmix, g_mix_g, g_mix_b = _rowwise(ln_mix_bwd, [x, mix, dh1], [P["ln_mix_g"], P["ln_mix_b"]],
                                            [1024, (1024, BF16)], [1024, 1024], name="ln_mix_bwd")
    dycat = _mm(dmix, W["w_out"], tb=True, name="mm_out_dx")
    G["w_out"] = _mm(ycat, dmix, ta=True, out_dtype=GRAD_DT, name="mm_out_dw")

    grads_done = comm.get("grads", lambda group, grads: zero_tok)
    tok1 = grads_done("ffn", G)
    datt, g_out_norm = _rowwise(lambda a, dv, w, t: _rms_bwd(a, w, dv + jnp.min(t)), [att, (dycat, 1024, 1)],
                                [P["mla_out_norm_w"], tok1], [1024], [1024], name="out_norm_bwd")
    dqn_nope, dqr, dkn, dv, dkrt = _att_bwd(q, qr, kv, krt, att, datt)
    dkv = jnp.concatenate([dkn, dv], axis=1)
    (dq_rope,) = _rowwise(lambda d0, d1, c, s: _rope_bwd_fn(d0 + d1, c, s), [(dqr, 512, 0), (dqr, 512, 1), cos512, sin512],
                          [], [(512, BF16)], name="rope_q_bwd")

    def rope_k_bwd(*a):
        d = _spread4(functools.reduce(lambda u, w: u + w, a[:-2]))
        lane = lax.broadcasted_iota(jnp.int32, d.shape, 1)
        return _rope_bwd_fn(jnp.where(lane < MLA_ROPE, d, 0.0), a[-2], a[-1])

    (dkr,) = _rowwise(rope_k_bwd, [(dkrt, LANE, k) for k in range(MLA_HEADS // 2)] + [cos128, sin128], [], [(LANE, BF16)],
                      name="rope_k_bwd")
    dq = jnp.concatenate([dqn_nope, dq_rope], axis=1)
    dqn = _mm(dq, W["mla_w_q_b"], name="mm_q_dx")
    G["mla_w_q_b"] = _mm(dq, qn, ta=True, out_dtype=GRAD_DT, name="mm_q_dw")
    dkvn = _mm(dkv, W["mla_w_kv_b"], tb=True, name="mm_kv_dx")
    G["mla_w_kv_b"] = _mm(kvn, dkv, ta=True, out_dtype=GRAD_DT, name="mm_kv_dw")
    tok2 = grads_done("mla", G)
    def qkv_norm_bwd(a, da, c, dc, wq, wkv, t):
        (d_a, d_wq), (d_c, d_wkv) = _rms_bwd(a, wq, da + jnp.min(t)), _rms_bwd(c, wkv, dc)
        return d_a, d_c, d_wq, d_wkv

    dqc, dkvc, g_q_norm, g_kv_norm = _rowwise(
        qkv_norm_bwd, [qc, dqn, kvc, dkvn], [P["mla_q_norm_w"], P["mla_kv_norm_w"], tok2],
        [(MLA_Q_RANK, BF16), (MLA_KV_RANK, BF16)], [MLA_Q_RANK, MLA_KV_RANK], name="qkv_norm_bwd")

    def gate_rms_bwd(yv, zv, dv, w, t):
        sg = _sigmoid(zv)
        silu = zv * sg
        gated = yv * silu
        r = lax.rsqrt(jnp.mean(gated * gated, axis=-1, keepdims=True) + EPS)
        n = gated * r
        dv = dv + jnp.min(t)
        g = dv * w
        d_gated = r * (g - n * jnp.mean(g * n, axis=-1, keepdims=True))
        return d_gated * silu, d_gated * yv * (sg * (1.0 + zv * (1.0 - sg))), _colsum(dv * n)

    dy, dz, g_ssd_norm = _rowwise(gate_rms_bwd, [y, z, (dycat, 1024, 0)], [P["ssd_norm_w"], tok1], [1024, (1024, BF16)],
                                  [1024], name="ssd_gate_norm_bwd")
    dxbca, ddtr, g_dt_bias, g_alog, g_d = _ssd_bwd(xbca, proj, PROJ_BLOCK["dt"][1], bias_p, alog_p, d_x, states, dy)
    da, g_conv_w, g_conv_b = _conv_bwd_pre(proj, PROJ_BLOCK["xbc"][1], P["ssd_conv_w"], P["ssd_conv_b"], dxbca)
    dxbc = _conv_bwd_in(da, P["ssd_conv_w"])

    small = {
        "ssd_conv_b": g_conv_b, "ssd_dt_bias": g_dt_bias, "ssd_A_log": g_alog, "ssd_D": g_d, "ssd_norm_w": g_ssd_norm,
        "mla_q_norm_w": g_q_norm, "mla_kv_norm_w": g_kv_norm, "mla_out_norm_w": g_out_norm, "ln_mix_g": g_mix_g,
        "ln_mix_b": g_mix_b, "ln_ffn_g": g_ffn_g, "ln_ffn_b": g_ffn_b,
    }
    packed = _pack_small(g_conv_w, [small[n] for n, _ in REPL_W], loss)
    if "small" in comm:
        comm["small"](packed)

    dproj = jnp.concatenate([dz, ddtr, dqc, dxbc, dkvc, dkr], axis=1)
    G["w_in"] = _mm(dproj, xb, ta=True, out_dtype=GRAD_DT, name="mm_in_dw")
    grad_x = _mm(dproj, W["w_in"], add=dx_a, after=grads_done("in", G), name="mm_in_dx")
    return grad_x, G, packed


def kernel(x, p, positions, w_in, ssd_conv_w, ssd_conv_b, ssd_dt_bias, ssd_A_log, ssd_D, ssd_norm_w, mla_q_norm_w, mla_w_q_b, mla_kv_norm_w, mla_w_kv_b, mla_out_norm_w, w_out, ln_mix_g, ln_mix_b, w_ffn_gate, w_ffn_up, w_ffn_down, w_ple_gate, w_ple_proj, ln_ffn_g, ln_ffn_b, loss_target, m_w_in, m_ssd_conv_w, m_ssd_conv_b, m_ssd_dt_bias, m_ssd_A_log, m_ssd_D, m_ssd_norm_w, m_mla_q_norm_w, m_mla_w_q_b, m_mla_kv_norm_w, m_mla_w_kv_b, m_mla_out_norm_w, m_w_out, m_ln_mix_g, m_ln_mix_b, m_w_ffn_gate, m_w_ffn_up, m_w_ffn_down, m_w_ple_gate, m_w_ple_proj, m_ln_ffn_g, m_ln_ffn_b, v_w_in, v_ssd_conv_w, v_ssd_conv_b, v_ssd_dt_bias, v_ssd_A_log, v_ssd_D, v_ssd_norm_w, v_mla_q_norm_w, v_mla_w_q_b, v_mla_kv_norm_w, v_mla_w_kv_b, v_mla_out_norm_w, v_w_out, v_ln_mix_g, v_ln_mix_b, v_w_ffn_gate, v_w_ffn_up, v_w_ffn_down, v_w_ple_gate, v_w_ple_proj, v_ln_ffn_g, v_ln_ffn_b):
    args = dict(locals())
    core = lax.axis_index("c")
    me = 4 * lax.axis_index("x") + 2 * lax.axis_index("y") + core

    conv_sh = ssd_conv_w[0]
    conv_hi = conv_sh.astype(BF16)
    conv_lo = (conv_sh - conv_hi.astype(F32)).astype(BF16)
    stored = lambda n, pre="": jnp.transpose(args[pre + n][0]) if n in TRANSPOSED else args[pre + n][0]
    shards = {n: stored(n).astype(BF16) for n in BIG}
    rows_full = lambda g: g.reshape(-1, g.shape[2])

    early = _gather_many([shards[n] for n in EARLY] + [jnp.concatenate([conv_hi, conv_lo], axis=0)], "gather_early")
    gw = dict(zip(EARLY, early[:-1]))
    conv_g = early[-1].astype(F32)
    W = {
        "w_in": _win_pad(rows_full(gw["w_in"])),
        "mla_w_q_b": _heads_split_t(rows_full(gw["mla_w_q_b"]), MLA_NOPE, MLA_ROPE),
        "mla_w_kv_b": _heads_split(_cols_full(gw["mla_w_kv_b"]), MLA_NOPE, MLA_V),
    }
    P = {n: args[n] for n, _ in REPL_W}
    P["ssd_conv_w"] = _cols_full(conv_g[:, :4] + conv_g[:, 4:])

    late, after = {}, early[0]
    for group, names in LATE.items():
        lands = [lax.dynamic_update_slice(lax.empty((N_DEV,) + shards[n].shape, BF16), shards[n][None], (me, 0, 0)) for n in names]
        late[group] = _split_start([shards[n] for n in names], lands, _plan_broadcast, N_DEV - 1,
                                   "gather_" + group + "_start", after=after)
        after = late[group][4]

    def late_weights(group, after):
        _, got = _split_wait(*late[group][:4], after, _plan_broadcast, "gather_" + group + "_wait")
        return {n: _cols_full(g) if n == "w_ple_proj" else rows_full(g) for n, g in zip(LATE[group], got)}

    def to_blocks(n, g):
        if n == "w_in":
            g = _win_unpad(g)
        elif n == "mla_w_q_b":
            g = _heads_merge_t(g, MLA_NOPE, MLA_ROPE)
        elif n == "mla_w_kv_b":
            g = _heads_merge(g, MLA_NOPE, MLA_V)
        if n in ROW_SHARDED or n in TRANSPOSED:
            return g.reshape(N_DEV, -1, g.shape[1])
        return _cols_split(g)

    flight = {}

    def grads(group, G):
        gl = [to_blocks(n, G[n]) for n in GRAD_GROUPS[group]]
        flight[group] = _split_start(gl, [lax.empty(g.shape, g.dtype) for g in gl], _plan_scatter, N_DEV - 1,
                                     "grads_" + group + "_start", after=flight["small"][4] if group == "in" else None)
        return flight[group][4]

    def small(packed):
        land = lax.dynamic_update_slice(lax.empty((N_DEV,) + packed.shape, F32), packed[None], (me, 0, 0))
        flight["small"] = _split_start([packed], [land], _plan_broadcast, N_DEV - 1, "small_start")

    grad_x, G, packed = _local_step(x[0], p[0, 0], positions[0], loss_target[0], W, P,
                                    comm={"token0": after, "late_weights": late_weights, "grads": grads, "small": small})

    me_arr = me.astype(jnp.int32).reshape(1)
    big_out = {}

    def finish(group, after):
        mine, recv = _split_wait(*flight[group][:4], after, _plan_scatter, "grads_" + group + "_wait")
        for n, g, r in zip(GRAD_GROUPS[group], mine, recv):
            big_out[n] = _adam(r, stored(n), stored(n, "m_"), stored(n, "v_"), "adam_" + n, own=g, own_idx=me_arr)
        return big_out[GRAD_GROUPS[group][-1]][0]

    done = finish("ffn", grad_x)
    _, (small_all,) = _split_wait(*flight["small"][:4], done, _plan_broadcast, "small_wait")
    conv_sum, loss_row, small_out = _adam_small(small_all, [(args[n], args["m_" + n], args["v_" + n]) for n, _ in REPL_W])
    finish("in", finish("mla", done))
    conv_grad = lax.dynamic_slice_in_dim(conv_sum, me * 192, 192, axis=1)
    conv_out = _adam(conv_grad[None], conv_sh, m_ssd_conv_w[0], v_ssd_conv_w[0], "adam_conv")
    small_map = {n: small_out[i] for i, (n, _) in enumerate(REPL_W)}

    def outputs(idx):
        res = []
        for n in WEIGHT_ORDER:
            if n == "ssd_conv_w":
                res.append(conv_out[idx][None])
            elif n in big_out:
                res.append((jnp.transpose(big_out[n][idx]) if n in TRANSPOSED else big_out[n][idx])[None])
            else:
                res.append(small_map[n][idx])
        return res

    return (loss_row[0, 0], grad_x[None], *outputs(0), *outputs(1), *outputs(2), *outputs(3))
```

```python
import functools
import math

import numpy as np
import jax
import jax.numpy as jnp
from jax import lax
from jax.experimental import pallas as pl
from jax.experimental.pallas import tpu as pltpu

F32 = jnp.float32
BF16 = jnp.bfloat16
HI = lax.Precision.HIGHEST

N_DEV = 8
D_MODEL = 1024
PLE_DIM = 256
SSD_HEADS = 16
SSD_HEAD_DIM = 64
SSD_INNER = 1024
SSD_STATE = 128
SSD_XBC = 1536
SSD_CHUNK = 128
MLA_HEADS = 16
MLA_Q_RANK = 384
MLA_KV_RANK = 256
MLA_NOPE = 64
MLA_ROPE = 32
MLA_V = 64
ROPE_BASE = 10000.0
D_FF = 2816
IN_WIDTH = 3248
IN_PAD = 3456
ALPHA = 2.0 ** 0.25
EPS = 1e-6
LN_EPS = 1e-5
ATT_SCALE = 1.0 / math.sqrt(MLA_NOPE + MLA_ROPE)
ADAM_LR, ADAM_B1, ADAM_B2, ADAM_EPS, ADAM_WD, ADAM_STEP = 0.001, 0.9, 0.999, 1e-08, 0.01, 10

LANE = 128
MXU_DIM = 256
MM_TM, MM_TN, MM_TK = 1408, 1408, 2048
ROW_TILE = 512
ATT_TQ = 256

GRAD_DT = BF16

BIG = ("w_in", "mla_w_q_b", "mla_w_kv_b", "w_out", "w_ffn_gate", "w_ffn_up", "w_ffn_down", "w_ple_gate", "w_ple_proj")
EARLY = ("w_in", "mla_w_q_b", "mla_w_kv_b")
LATE = {"out": ("w_out", "w_ple_gate", "w_ple_proj"), "ffn": ("w_ffn_gate", "w_ffn_up", "w_ffn_down")}
GRAD_GROUPS = {"ffn": ("w_ffn_gate", "w_ffn_up", "w_ffn_down", "w_ple_gate", "w_ple_proj", "w_out"),
               "mla": ("mla_w_q_b", "mla_w_kv_b"), "in": ("w_in",)}
ROW_SHARDED = ("w_out", "w_ffn_down", "w_ple_gate")
TRANSPOSED = ("w_in", "mla_w_q_b", "w_ffn_gate", "w_ffn_up")
WEIGHT_ORDER = ("w_in", "ssd_conv_w", "ssd_conv_b", "ssd_dt_bias", "ssd_A_log", "ssd_D", "ssd_norm_w", "mla_q_norm_w",
                "mla_w_q_b", "mla_kv_norm_w", "mla_w_kv_b", "mla_out_norm_w", "w_out", "ln_mix_g", "ln_mix_b",
                "w_ffn_gate", "w_ffn_up", "w_ffn_down", "w_ple_gate", "w_ple_proj", "ln_ffn_g", "ln_ffn_b")


def _tile(dim, cap, prefer=None):
    cands = [t for t in range(LANE, min(cap, dim) + 1, LANE) if dim % t == 0]
    if not cands:
        return dim
    if prefer is None:
        return max(cands)
    fill = lambda t: t / (MXU_DIM * -(-t // MXU_DIM))
    good = min(0.9, max(fill(t) for t in cands))
    return min((t for t in cands if fill(t) >= good), key=lambda t: abs(t - prefer))


def _dot(a, b, dims=(((1,), (0,)), ((), ())), precision=None):
    return lax.dot_general(a, b, dims, preferred_element_type=F32, precision=precision)


_NT = (((1,), (1,)), ((), ()))
_TN = (((0,), (0,)), ((), ()))


def _mm(a, b, *, ta=False, tb=False, add=None, out_dtype=F32, after=None, epilogue=None, name):
    k_dim, m_dim = a.shape if ta else a.shape[::-1]
    n_dim, kb = b.shape if tb else b.shape[::-1]
    assert k_dim == kb
    tm, tn, tk = _tile(m_dim, MM_TM), _tile(n_dim, MM_TN, prefer=1024), _tile(k_dim, MM_TK, prefer=MM_TK)
    nk = k_dim // tk
    dims = (((0 if ta else 1,), (1 if tb else 0,)), ((), ()))
    a_spec = pl.BlockSpec((tk, tm), lambda i, j, k: (k, i)) if ta else pl.BlockSpec((tm, tk), lambda i, j, k: (i, k))
    b_spec = pl.BlockSpec((tn, tk), lambda i, j, k: (j, k)) if tb else pl.BlockSpec((tk, tn), lambda i, j, k: (k, j))
    o_spec = pl.BlockSpec((tm, tn), lambda i, j, k: (i, j))
    epi_fn, epi_in, out_dtypes = epilogue if epilogue else (None, [], [out_dtype])
    tiles = ([add] if add is not None else []) + list(epi_in)
    n_out = len(out_dtypes)

    def body(*refs):
        a_ref, b_ref = refs[:2]
        tile_refs = refs[2:2 + len(tiles)]
        out_refs = refs[len(refs) - n_out - (nk > 1):len(refs) - (nk > 1)]
        part = _dot(a_ref[...].astype(BF16), b_ref[...].astype(BF16), dims)
        if add is not None:
            part_add = lambda v: v + tile_refs[0][...]
        else:
            part_add = lambda v: v

        def write(total):
            extra = [r[...] for r in tile_refs[add is not None:]]
            outs = epi_fn(total, *extra) if epi_fn else (total,)
            for o_ref, val in zip(out_refs, outs):
                o_ref[...] = val.astype(o_ref.dtype)

        if nk == 1:
            write(part_add(part))
            return
        acc = refs[-1]
        k = pl.program_id(2)

        @pl.when(k == 0)
        def _():
            acc[...] = part_add(part)

        @pl.when(k > 0)
        def _():
            acc[...] += part

        @pl.when(k == nk - 1)
        def _():
            write(acc[...])

    ins = [a, b] + tiles + ([after] if after is not None else [])
    specs = [a_spec, b_spec] + [o_spec] * len(tiles) + ([pl.BlockSpec(memory_space=pl.ANY)] if after is not None else [])
    res = pl.pallas_call(
        body, name=name, grid=(m_dim // tm, n_dim // tn, nk), in_specs=specs, out_specs=[o_spec] * n_out,
        out_shape=[jax.ShapeDtypeStruct((m_dim, n_dim), dt) for dt in out_dtypes],
        scratch_shapes=[pltpu.VMEM((tm, tn), F32)] if nk > 1 else [],
        compiler_params=pltpu.CompilerParams(dimension_semantics=("parallel", "parallel", "arbitrary")),
    )(*ins)
    return res if epilogue else res[0]


def _rowwise(fn, rows, consts, out_widths, acc_widths=(), *, name, tr=ROW_TILE):
    row_arrays, row_specs = [], []
    first_arr = rows[0][0] if isinstance(rows[0], tuple) else rows[0]
    s_dim = first_arr.shape[-2]
    tr = min(tr, s_dim)
    for r in rows:
        arr, width, cb = r if isinstance(r, tuple) else (r, r.shape[-1], 0)
        row_arrays.append(arr)
        if arr.ndim == 3:
            row_specs.append(pl.BlockSpec((None, tr, width), functools.partial(lambda i, k: (k, i, 0), k=cb)))
        else:
            row_specs.append(pl.BlockSpec((tr, width), functools.partial(lambda i, cb: (i, cb), cb=cb)))
    const_specs = [pl.BlockSpec(c.shape, lambda i: (0, 0)) for c in consts]
    nr, nc, no, na = len(rows), len(consts), len(out_widths), len(acc_widths)

    def body(*refs):
        ins = [r[...] for r in refs[:nr + nc]]
        res = fn(*ins)
        if not isinstance(res, (tuple, list)):
            res = (res,)
        out_refs = refs[nr + nc:nr + nc + no]
        acc_refs = refs[nr + nc + no:]
        for o_ref, val in zip(out_refs, res[:no]):
            o_ref[...] = val.astype(o_ref.dtype)
        first = pl.program_id(0) == 0
        for a_ref, val in zip(acc_refs, res[no:]):
            @pl.when(first)
            def _(a_ref=a_ref, val=val):
                a_ref[...] = val

            @pl.when(jnp.logical_not(first))
            def _(a_ref=a_ref, val=val):
                a_ref[...] += val

    outs = [w if isinstance(w, tuple) else (w, F32) for w in out_widths]
    out_shape = [jax.ShapeDtypeStruct((s_dim, w), dt) for w, dt in outs]
    out_shape += [jax.ShapeDtypeStruct((1, w), F32) for w in acc_widths]
    out_specs = [pl.BlockSpec((tr, w), lambda i: (i, 0)) for w, _ in outs]
    out_specs += [pl.BlockSpec((1, w), lambda i: (0, 0)) for w in acc_widths]
    res = pl.pallas_call(
        body, name=name, grid=(s_dim // tr,), in_specs=row_specs + const_specs, out_specs=out_specs, out_shape=out_shape,
        compiler_params=pltpu.CompilerParams(dimension_semantics=("arbitrary",)),
    )(*row_arrays, *consts)
    return res


def _colsum(v):
    return jnp.sum(v, axis=0, keepdims=True)


def _rms(u, g):
    return u * lax.rsqrt(jnp.mean(u * u, axis=-1, keepdims=True) + EPS) * g


def _ln(u, g, b):
    mu = jnp.mean(u, axis=-1, keepdims=True)
    d = u - mu
    var = jnp.mean(d * d, axis=-1, keepdims=True)
    return d * lax.rsqrt(var + LN_EPS) * g + b


def _sigmoid(v):
    return 1.0 / (1.0 + jnp.exp(-v))


def _silu(v):
    return v * _sigmoid(v)


def _softplus(v):
    y = jnp.exp(-jnp.abs(v))
    w = 1.0 + y
    log1p = jnp.where(w == 1.0, y, jnp.log(w) * y / jnp.where(w == 1.0, 1.0, w - 1.0))
    return jnp.maximum(v, 0.0) + log1p


def _gate_rms(y, z, w):
    return _rms(y * _silu(z), w)


def _rms_bwd(u, g, d_out):
    r = lax.rsqrt(jnp.mean(u * u, axis=-1, keepdims=True) + EPS)
    n = u * r
    gd = d_out * g
    return r * (gd - n * jnp.mean(gd * n, axis=-1, keepdims=True)), _colsum(d_out * n)


def _ln_parts(u):
    d = u - jnp.mean(u, axis=-1, keepdims=True)
    r = lax.rsqrt(jnp.mean(d * d, axis=-1, keepdims=True) + LN_EPS)
    return d * r, r


def _ln_bwd(n, r, g, d_out):
    gd = d_out * g
    d_u = r * (gd - jnp.mean(gd, axis=-1, keepdims=True) - n * jnp.mean(gd * n, axis=-1, keepdims=True))
    return d_u, _colsum(d_out * n), _colsum(d_out)


def _conv_pre(cur, prev, w, b, first):
    row = lax.broadcasted_iota(jnp.int32, cur.shape, 0)
    acc = cur * w[3:4, :] + b
    for j in (1, 2, 3):
        tail = jnp.where(first, 0.0, pltpu.roll(prev, j, 0))
        acc = acc + jnp.where(row >= j, pltpu.roll(cur, j, 0), tail) * w[3 - j:4 - j, :]
    return acc


def _conv_fwd(u, ucb, w, b, name="conv_fwd"):
    s_dim, width = u.shape[0], w.shape[1]
    tr = min(ROW_TILE, s_dim)

    def body(cur_ref, prev_ref, w_ref, b_ref, o_ref):
        pre = _conv_pre(cur_ref[...], prev_ref[...], w_ref, b_ref[...], pl.program_id(0) == 0)
        o_ref[...] = _silu(pre)

    return pl.pallas_call(
        body, name=name, grid=(s_dim // tr,),
        in_specs=[pl.BlockSpec((tr, width), lambda i: (i, ucb)),
                  pl.BlockSpec((tr, width), lambda i: (jnp.maximum(i - 1, 0), ucb)),
                  pl.BlockSpec(w.shape, lambda i: (0, 0)), pl.BlockSpec(b.shape, lambda i: (0, 0))],
        out_specs=pl.BlockSpec((tr, width), lambda i: (i, 0)), out_shape=jax.ShapeDtypeStruct((s_dim, width), F32),
        compiler_params=pltpu.CompilerParams(dimension_semantics=("arbitrary",)),
    )(u, u, w, b)


def _conv_bwd_pre(u, ucb, w, b, dact, name="conv_bwd_pre"):
    s_dim, width = u.shape[0], w.shape[1]
    tr = min(ROW_TILE, s_dim)

    def body(cur_ref, prev_ref, w_ref, b_ref, d_ref, da_ref, dw_ref, db_ref):
        first = pl.program_id(0) == 0
        cur, prev = cur_ref[...], prev_ref[...]
        pre = _conv_pre(cur, prev, w_ref, b_ref[...], first)
        sg = _sigmoid(pre)
        da = d_ref[...] * (sg * (1.0 + pre * (1.0 - sg)))
        da_ref[...] = da
        row = lax.broadcasted_iota(jnp.int32, cur.shape, 0)

        @pl.when(first)
        def _():
            dw_ref[...] = jnp.zeros_like(dw_ref)
            db_ref[...] = jnp.zeros_like(db_ref)

        db_ref[...] += _colsum(da)
        dw_ref[3:4, :] += _colsum(da * cur)
        for j in (1, 2, 3):
            tail = jnp.where(first, 0.0, pltpu.roll(prev, j, 0))
            sh = jnp.where(row >= j, pltpu.roll(cur, j, 0), tail)
            dw_ref[3 - j:4 - j, :] += _colsum(da * sh)

    return pl.pallas_call(
        body, name=name, grid=(s_dim // tr,),
        in_specs=[pl.BlockSpec((tr, width), lambda i: (i, ucb)),
                  pl.BlockSpec((tr, width), lambda i: (jnp.maximum(i - 1, 0), ucb)),
                  pl.BlockSpec(w.shape, lambda i: (0, 0)), pl.BlockSpec(b.shape, lambda i: (0, 0)),
                  pl.BlockSpec((tr, width), lambda i: (i, 0))],
        out_specs=[pl.BlockSpec((tr, width), lambda i: (i, 0)), pl.BlockSpec(w.shape, lambda i: (0, 0)),
                   pl.BlockSpec(b.shape, lambda i: (0, 0))],
        out_shape=[jax.ShapeDtypeStruct((s_dim, width), F32), jax.ShapeDtypeStruct(w.shape, F32),
                   jax.ShapeDtypeStruct(b.shape, F32)],
        compiler_params=pltpu.CompilerParams(dimension_semantics=("arbitrary",)),
    )(u, u, w, b, dact)


def _conv_bwd_in(da, w, name="conv_bwd_in"):
    s_dim, width = da.shape
    tr = min(ROW_TILE, s_dim)
    n = s_dim // tr

    def body(cur_ref, nxt_ref, w_ref, o_ref):
        last = pl.program_id(0) == n - 1
        cur, nxt = cur_ref[...], nxt_ref[...]
        row = lax.broadcasted_iota(jnp.int32, cur.shape, 0)
        acc = cur * w_ref[3:4, :]
        for j in (1, 2, 3):
            head = jnp.where(last, 0.0, pltpu.roll(nxt, tr - j, 0))
            acc = acc + jnp.where(row < tr - j, pltpu.roll(cur, tr - j, 0), head) * w_ref[3 - j:4 - j, :]
        o_ref[...] = acc.astype(o_ref.dtype)

    return pl.pallas_call(
        body, name=name, grid=(n,),
        in_specs=[pl.BlockSpec((tr, width), lambda i: (i, 0)), pl.BlockSpec((tr, width), lambda i: (jnp.minimum(i + 1, n - 1), 0)),
                  pl.BlockSpec(w.shape, lambda i: (0, 0))],
        out_specs=pl.BlockSpec((tr, width), lambda i: (i, 0)), out_shape=jax.ShapeDtypeStruct((s_dim, width), BF16),
        compiler_params=pltpu.CompilerParams(dimension_semantics=("arbitrary",)),
    )(da, da, w)


def _sel_dot(a, sel, pieces, dims=(((1,), (0,)), ((), ())), sel_left=False):
    sel = sel.astype(BF16)
    acc, rest = None, a
    for _ in range(pieces):
        piece = rest.astype(BF16)
        rest = rest - piece.astype(F32)
        part = _dot(sel, piece, dims) if sel_left else _dot(piece, sel, dims)
        acc = part if acc is None else acc + part
    return acc


def _ssd_consts():
    L = SSD_CHUNK
    tri = np.tril(np.ones((L, L), np.float32))
    expand = np.zeros((LANE, SSD_INNER), np.float32)
    for h in range(SSD_HEADS):
        expand[h, h * SSD_HEAD_DIM:(h + 1) * SSD_HEAD_DIM] = 1.0
    return jnp.asarray(tri), jnp.asarray(expand), jnp.asarray(expand.T.copy())


def _ssd_prep(dt_ref, bias_ref, alog_ref, tri_ref, exp_ref, cs_s, cst_s, ex_s):
    L = SSD_CHUNK
    dt = _softplus(dt_ref[...] + bias_ref[...])
    a = -jnp.exp(alog_ref[...])
    cs = _sel_dot(dt * a, tri_ref[...], 3, sel_left=True)
    cs_s[...] = cs
    cst_s[...] = cs.T
    last = cs_s[L - 1:L, :]
    expand = exp_ref[...]
    ex_s[...] = _sel_dot(jnp.exp(cs), expand, 2)
    f_x = _sel_dot(jnp.exp(last - cs), expand, 2)
    dt_x = _sel_dot(dt, expand, 2)
    t_x = ex_s[L - 1:L, :]
    return dt, a, dt_x, f_x, t_x


def _decay_matrix(cs_s, cst_s, h, tril):
    seg = cs_s[:, h:h + 1] - cst_s[h:h + 1, :]
    return jnp.exp(jnp.where(tril, seg, -jnp.inf))


def _ssd_fwd(xbca, dtr, dtcb, bias, alog, d_x, name="ssd_fwd"):
    s_dim = xbca.shape[0]
    L = SSD_CHUNK
    nc = s_dim // L
    tri, expand, _ = _ssd_consts()

    def body(xs_ref, b_ref, c_ref, dt_ref, bias_ref, alog_ref, dx_ref, tri_ref, exp_ref,
             y_ref, st_ref, st_s, cs_s, cst_s, ex_s):
        @pl.when(pl.program_id(0) == 0)
        def _():
            st_s[...] = jnp.zeros_like(st_s)

        dt, a, dt_x, f_x, t_x = _ssd_prep(dt_ref, bias_ref, alog_ref, tri_ref, exp_ref, cs_s, cst_s, ex_s)
        st_ref[0] = st_s[...]
        row = lax.broadcasted_iota(jnp.int32, (L, L), 0)
        col = lax.broadcasted_iota(jnp.int32, (L, L), 1)
        tril = row >= col
        low = col < SSD_HEAD_DIM
        for g in range(2):
            bg = b_ref[:, g * LANE:(g + 1) * LANE]
            cg = c_ref[:, g * LANE:(g + 1) * LANE].astype(BF16)
            gmat = _dot(cg, bg.astype(BF16), _NT)
            bgt = bg.T.astype(BF16)
            for jj in range(4):
                j = 4 * g + jj
                sl = slice(j * LANE, (j + 1) * LANE)
                xp = xs_ref[:, sl]
                x_dt = xp * dt_x[:, sl]
                xb = x_dt.astype(BF16)
                yd = []
                for e in range(2):
                    lm = _decay_matrix(cs_s, cst_s, 2 * j + e, tril)
                    yd.append(_dot((gmat * lm).astype(BF16), xb))
                stp = st_s[j]
                z = _dot(cg, stp.astype(BF16))
                y_ref[:, sl] = jnp.where(low, yd[0], yd[1]) + ex_s[:, sl] * z + dx_ref[:, sl] * xp
                xf = (x_dt * f_x[:, sl]).astype(BF16)
                st_s[j] = t_x[:, sl] * stp + _dot(bgt, xf)

    const = lambda shape: pl.BlockSpec(shape, lambda c: tuple(0 for _ in shape))
    return pl.pallas_call(
        body, name=name, grid=(nc,),
        in_specs=[pl.BlockSpec((L, 1024), lambda c: (c, 0)), pl.BlockSpec((L, 256), lambda c: (c, 4)),
                  pl.BlockSpec((L, 256), lambda c: (c, 5)), pl.BlockSpec((L, LANE), lambda c: (c, dtcb)),
                  const((1, LANE)), const((1, LANE)), const((1, 1024)), const((L, L)), const((LANE, 1024))],
        out_specs=[pl.BlockSpec((L, 1024), lambda c: (c, 0)), pl.BlockSpec((1, 8, LANE, LANE), lambda c: (c, 0, 0, 0))],
        out_shape=[jax.ShapeDtypeStruct((s_dim, 1024), F32), jax.ShapeDtypeStruct((nc, 8, LANE, LANE), F32)],
        scratch_shapes=[pltpu.VMEM((8, LANE, LANE), F32), pltpu.VMEM((L, LANE), F32), pltpu.VMEM((LANE, L), F32),
                        pltpu.VMEM((L, 1024), F32)],
        compiler_params=pltpu.CompilerParams(dimension_semantics=("arbitrary",)),
    )(xbca, xbca, xbca, dtr, bias, alog, d_x, tri, expand)


def _ssd_bwd(xbca, dtr, dtcb, bias, alog, d_x, states, dy, name="ssd_bwd"):
    s_dim = xbca.shape[0]
    L = SSD_CHUNK
    nc = s_dim // L
    tri, expand, expand_t = _ssd_consts()

    def body(xs_ref, b_ref, c_ref, dt_ref, bias_ref, alog_ref, dx_ref, tri_ref, exp_ref, expt_ref,
             st_ref, dy_ref, dxbc_ref, ddt_ref, dbias_ref, dalog_ref, dd_ref,
             dst_s, cs_s, cst_s, ex_s, dcsx_s, ddtx_s, dcol_s, drow_s, dlast_s, dd_s):
        @pl.when(pl.program_id(0) == 0)
        def _():
            dst_s[...] = jnp.zeros_like(dst_s)
            dbias_ref[...] = jnp.zeros_like(dbias_ref)
            dalog_ref[...] = jnp.zeros_like(dalog_ref)
            dd_s[...] = jnp.zeros_like(dd_s)

        dt, a, dt_x, f_x, t_x = _ssd_prep(dt_ref, bias_ref, alog_ref, tri_ref, exp_ref, cs_s, cst_s, ex_s)
        row = lax.broadcasted_iota(jnp.int32, (L, L), 0)
        col = lax.broadcasted_iota(jnp.int32, (L, L), 1)
        tril = row >= col
        low = col < SSD_HEAD_DIM
        dcol_s[...] = jnp.zeros_like(dcol_s)
        drow_s[...] = jnp.zeros_like(drow_s)
        for g in range(2):
            bg = b_ref[:, g * LANE:(g + 1) * LANE]
            cg = c_ref[:, g * LANE:(g + 1) * LANE]
            bgb, cgb = bg.astype(BF16), cg.astype(BF16)
            gmat = _dot(cgb, bgb, _NT)
            d_g = jnp.zeros((L, L), F32)
            d_b = jnp.zeros((L, LANE), F32)
            d_c = jnp.zeros((L, LANE), F32)
            for jj in range(4):
                j = 4 * g + jj
                sl = slice(j * LANE, (j + 1) * LANE)
                xp = xs_ref[:, sl]
                dtp = dt_x[:, sl]
                x_dt = xp * dtp
                xb = x_dt.astype(BF16)
                dyp = dy_ref[:, sl]
                dd_s[:, sl] += _colsum(dyp * xp)
                d_xdt = jnp.zeros((L, LANE), F32)
                for e in range(2):
                    h = 2 * j + e
                    lm = _decay_matrix(cs_s, cst_s, h, tril)
                    m = gmat * lm
                    dye = jnp.where(low if e == 0 else jnp.logical_not(low), dyp, 0.0).astype(BF16)
                    d_m = jnp.where(tril, _dot(dye, xb, _NT), 0.0)
                    d_xdt = d_xdt + _dot(m.astype(BF16), dye, _TN)
                    d_g = d_g + d_m * lm
                    w = d_m * m
                    dcol_s[...] += jnp.where(col == h, jnp.sum(w, axis=1, keepdims=True), 0.0)
                    drow_s[...] += jnp.where(row == h, jnp.sum(w, axis=0, keepdims=True), 0.0)
                stp = st_ref[0, j]
                stb = stp.astype(BF16)
                dstn = dst_s[j]
                dstb = dstn.astype(BF16)
                e_p = ex_s[:, sl]
                f_p = f_x[:, sl]
                t_p = t_x[:, sl]
                z = _dot(cgb, stb)
                d_z = (e_p * dyp).astype(BF16)
                d_c = d_c + _dot(d_z, stb, _NT)
                d_xf = _dot(bgb, dstb)
                d_b = d_b + _dot((x_dt * f_p).astype(BF16), dstb, _NT)
                d_xdt = d_xdt + f_p * d_xf
                d_f = x_dt * d_xf * f_p
                dcsx_s[:, sl] = dyp * e_p * z - d_f
                dlast_s[:, sl] = _colsum(d_f) + _colsum(dstn * stp) * t_p
                dst_s[j] = _dot(cgb, d_z, _TN) + t_p * dstn
                dxbc_ref[:, sl] = dx_ref[:, sl] * dyp + d_xdt * dtp
                ddtx_s[:, sl] = d_xdt * xp
            d_gb = d_g.astype(BF16)
            dxbc_ref[:, 1024 + g * LANE:1024 + (g + 1) * LANE] = d_b + _dot(d_gb, cgb, _TN)
            dxbc_ref[:, 1280 + g * LANE:1280 + (g + 1) * LANE] = d_c + _dot(d_gb, bgb)

        expt = expt_ref[...]
        dlast = _sel_dot(jnp.broadcast_to(dlast_s[...], (8, 1024)), expt, 3)
        d_cs = dcol_s[...] - drow_s[...].T + _sel_dot(dcsx_s[...], expt, 3)
        rown = lax.broadcasted_iota(jnp.int32, (L, LANE), 0)
        d_cs = d_cs + jnp.where(rown == L - 1, jnp.sum(dlast, axis=0, keepdims=True) * 0.125, 0.0)
        d_da = _sel_dot(d_cs, tri_ref[...], 3, _TN, sel_left=True)
        d_dt = d_da * a + _sel_dot(ddtx_s[...], expt, 3)
        dalog_ref[...] += _colsum(d_da * dt) * a
        d_raw = d_dt * _sigmoid(dt_ref[...] + bias_ref[...])
        ddt_ref[...] = d_raw.astype(ddt_ref.dtype)
        dbias_ref[...] += _colsum(d_raw)
        dd8 = _sel_dot(jnp.broadcast_to(dd_s[...], (8, 1024)), expt, 3)
        dd_ref[...] = jnp.sum(dd8, axis=0, keepdims=True) * 0.125

    const = lambda shape: pl.BlockSpec(shape, lambda c: tuple(0 for _ in shape))
    rev = lambda cb: (lambda c: (nc - 1 - c, cb))
    return pl.pallas_call(
        body, name=name, grid=(nc,),
        in_specs=[pl.BlockSpec((L, 1024), rev(0)), pl.BlockSpec((L, 256), rev(4)), pl.BlockSpec((L, 256), rev(5)),
                  pl.BlockSpec((L, LANE), rev(dtcb)), const((1, LANE)), const((1, LANE)), const((1, 1024)), const((L, L)),
                  const((LANE, 1024)), const((1024, LANE)),
                  pl.BlockSpec((1, 8, LANE, LANE), lambda c: (nc - 1 - c, 0, 0, 0)), pl.BlockSpec((L, 1024), rev(0))],
        out_specs=[pl.BlockSpec((L, SSD_XBC), rev(0)), pl.BlockSpec((L, LANE), rev(0)), const((1, LANE)), const((1, LANE)),
                   const((1, LANE))],
        out_shape=[jax.ShapeDtypeStruct((s_dim, SSD_XBC), F32), jax.ShapeDtypeStruct((s_dim, LANE), BF16),
                   jax.ShapeDtypeStruct((1, LANE), F32), jax.ShapeDtypeStruct((1, LANE), F32),
                   jax.ShapeDtypeStruct((1, LANE), F32)],
        scratch_shapes=[pltpu.VMEM((8, LANE, LANE), F32), pltpu.VMEM((L, LANE), F32), pltpu.VMEM((LANE, L), F32),
                        pltpu.VMEM((L, 1024), F32), pltpu.VMEM((L, 1024), F32), pltpu.VMEM((L, 1024), F32),
                        pltpu.VMEM((L, LANE), F32), pltpu.VMEM((LANE, L), F32), pltpu.VMEM((1, 1024), F32),
                        pltpu.VMEM((1, 1024), F32)],
        compiler_params=pltpu.CompilerParams(dimension_semantics=("arbitrary",)),
    )(xbca, xbca, xbca, dtr, bias, alog, d_x, tri, expand, expand_t, states, dy)


def _swap_halves(u):
    width = u.shape[1]
    lane = lax.broadcasted_iota(jnp.int32, u.shape, 1)
    return jnp.where(lane % MLA_ROPE < MLA_ROPE // 2, pltpu.roll(u, width - MLA_ROPE // 2, 1), pltpu.roll(u, MLA_ROPE // 2, 1))


def _rope_fwd_fn(u, cos, sin):
    return u * cos + _swap_halves(u) * sin


def _rope_bwd_fn(d, cos, sin):
    return d * cos + _swap_halves(d * sin)


def _spread4(v):
    return v + pltpu.roll(v, 32, 1) + pltpu.roll(v, 64, 1) + pltpu.roll(v, 96, 1)


def _att_masks(tq):
    lane = lax.broadcasted_iota(jnp.int32, (tq, LANE), 1)
    return lane // MLA_NOPE, lane // MLA_ROPE


def _att_tile(i, tq):
    klen = (i + 1) * tq
    qpos = i * tq + lax.broadcasted_iota(jnp.int32, (tq, klen), 0)
    kpos = lax.broadcasted_iota(jnp.int32, (tq, klen), 1)
    return slice(i * tq, (i + 1) * tq), klen, qpos >= kpos


def _att_qcat(qn_t, qr_t, par, e, half_id, grp_id):
    return jnp.concatenate([jnp.where(half_id == par, qn_t * ATT_SCALE, 0.0), jnp.where(grp_id == e, qr_t * ATT_SCALE, 0.0)],
                           axis=1).astype(BF16)


def _att_softmax(scores, causal):
    s = jnp.where(causal, scores, -jnp.inf)
    e = jnp.exp(s - jnp.max(s, axis=1, keepdims=True))
    return e, 1.0 / jnp.sum(e, axis=1, keepdims=True)


def _att_specs(s_dim):
    col = lambda f: pl.BlockSpec((s_dim, LANE), lambda j: (0, f(j)))
    return [col(lambda j: j), col(lambda j: j // 2), col(lambda j: j), col(lambda j: 0), col(lambda j: 8 + j)]


def _att_fwd(q, qr, kv, krt, name="att_fwd"):
    s_dim = q.shape[0]
    tq = min(ATT_TQ, s_dim)

    def body(qn_ref, qr_ref, kn_ref, krt_ref, v_ref, o_ref, kcat_s, vb_s):
        e0 = 2 * (pl.program_id(0) % 2)
        half_id, grp_id = _att_masks(tq)
        kcat_s[...] = jnp.concatenate([kn_ref[...], krt_ref[...]], axis=1).astype(BF16)
        vb_s[...] = v_ref[...].astype(BF16)
        for i in range(s_dim // tq):
            rows, klen, causal = _att_tile(i, tq)
            qn_t, qr_t = qn_ref[rows, :], qr_ref[rows, :]
            scores = [_dot(_att_qcat(qn_t, qr_t, par, e0 + par, half_id, grp_id), kcat_s[0:klen, :], _NT) for par in range(2)]
            probs = [_att_softmax(s, causal) for s in scores]
            outs = [_dot(e.astype(BF16), vb_s[0:klen, :]) * inv_l for e, inv_l in probs]
            o_ref[rows, :] = jnp.where(half_id == 0, outs[0], outs[1])

    return pl.pallas_call(
        body, name=name, grid=(MLA_HEADS // 2,), in_specs=_att_specs(s_dim),
        out_specs=pl.BlockSpec((s_dim, LANE), lambda j: (0, j)), out_shape=jax.ShapeDtypeStruct((s_dim, 1024), F32),
        scratch_shapes=[pltpu.VMEM((s_dim, 2 * LANE), BF16), pltpu.VMEM((s_dim, LANE), BF16)],
        compiler_params=pltpu.CompilerParams(dimension_semantics=("parallel",)),
    )(q, qr, kv, krt, kv)


def _att_bwd(q, qr, kv, krt, o, do, name="att_bwd"):
    s_dim = q.shape[0]
    tq = min(ATT_TQ, s_dim)

    def body(qn_ref, qr_ref, kn_ref, krt_ref, v_ref, o_ref, do_ref, dqn_ref, dqr_ref, dkn_ref, dv_ref, dkrt_ref,
             kcat_s, vb_s):
        e0 = 2 * (pl.program_id(0) % 2)
        half_id, grp_id = _att_masks(tq)
        kcat_s[...] = jnp.concatenate([kn_ref[...], krt_ref[...]], axis=1).astype(BF16)
        vb_s[...] = v_ref[...].astype(BF16)
        dkn_ref[...] = jnp.zeros_like(dkn_ref)
        dv_ref[...] = jnp.zeros_like(dv_ref)
        dkrt_ref[...] = jnp.zeros_like(dkrt_ref)
        for i in range(s_dim // tq):
            rows, klen, causal = _att_tile(i, tq)
            qn_t, qr_t, o_t, do_t = qn_ref[rows, :], qr_ref[rows, :], o_ref[rows, :], do_ref[rows, :]
            heads = range(2)
            qcats = [_att_qcat(qn_t, qr_t, par, e0 + par, half_id, grp_id) for par in heads]
            scores = [_dot(qcats[par], kcat_s[0:klen, :], _NT) for par in heads]
            doms = [jnp.where(half_id == par, do_t, 0.0) for par in heads]
            dombs = [d.astype(BF16) for d in doms]
            d_ps = [_dot(dombs[par], vb_s[0:klen, :], _NT) for par in heads]
            probs = []
            for par in heads:
                e, inv_l = _att_softmax(scores[par], causal)
                probs.append(e * inv_l)
            d_ss = []
            for par in heads:
                d_row = jnp.sum(doms[par] * o_t, axis=1, keepdims=True)
                d_ss.append((probs[par] * (d_ps[par] - d_row)).astype(BF16))
            dqcats = [_dot(d_ss[par], kcat_s[0:klen, :]) * ATT_SCALE for par in heads]
            dkcats = [_dot(d_ss[par], qcats[par], _TN) for par in heads]
            dvs = [_dot(probs[par].astype(BF16), dombs[par], _TN) for par in heads]
            dqn_ref[rows, :] = jnp.where(half_id == 0, dqcats[0][:, :LANE], dqcats[1][:, :LANE]).astype(dqn_ref.dtype)
            dqr_ref[rows, :] = (jnp.where(grp_id == e0, dqcats[0][:, LANE:], 0.0)
                                + jnp.where(grp_id == e0 + 1, dqcats[1][:, LANE:], 0.0))
            dkn_ref[0:klen, :] += dkcats[0][:, :LANE] + dkcats[1][:, :LANE]
            dkrt_ref[0:klen, :] += dkcats[0][:, LANE:] + dkcats[1][:, LANE:]
            dv_ref[0:klen, :] += dvs[0] + dvs[1]

    col = lambda f: pl.BlockSpec((s_dim, LANE), lambda j: (0, f(j)))
    return pl.pallas_call(
        body, name=name, grid=(MLA_HEADS // 2,), in_specs=_att_specs(s_dim) + [col(lambda j: j), col(lambda j: j)],
        out_specs=[col(lambda j: j), pl.BlockSpec((None, s_dim, LANE), lambda j: (j % 2, 0, j // 2)), col(lambda j: j),
                   col(lambda j: j), pl.BlockSpec((None, s_dim, LANE), lambda j: (j, 0, 0))],
        out_shape=[jax.ShapeDtypeStruct((s_dim, 1024), BF16), jax.ShapeDtypeStruct((2, s_dim, 512), F32),
                   jax.ShapeDtypeStruct((s_dim, 1024), F32), jax.ShapeDtypeStruct((s_dim, 1024), F32),
                   jax.ShapeDtypeStruct((MLA_HEADS // 2, s_dim, LANE), F32)],
        scratch_shapes=[pltpu.VMEM((s_dim, 2 * LANE), BF16), pltpu.VMEM((s_dim, LANE), BF16)],
        compiler_params=pltpu.CompilerParams(dimension_semantics=("parallel",)),
    )(q, qr, kv, krt, kv, o, do)


def _gather_many(shards, name):
    n_arr = len(shards)

    def body(*refs):
        x_refs, out_refs = refs[:n_arr], refs[n_arr:2 * n_arr]
        send_sems, recv_sems, local_sems = refs[2 * n_arr:]
        x_i, y_i, c_i = lax.axis_index("x"), lax.axis_index("y"), lax.axis_index("c")
        me, sibling = (x_i, y_i, c_i), (x_i, y_i, 1 - c_i)
        chips = [(1 - x_i, y_i), (x_i, 1 - y_i), (1 - x_i, 1 - y_i)]

        def copy(a, k, block, to, src=None):
            slot = out_refs[a].at[4 * block[0] + 2 * block[1] + block[2]]
            return pltpu.make_async_remote_copy(
                src_ref=slot if src is None else src, dst_ref=slot, send_sem=send_sems.at[a, k],
                recv_sem=recv_sems.at[a, k], device_id=to, device_id_type=pl.DeviceIdType.MESH)

        mine, first, passed = [], [], []
        for a in range(n_arr):
            mine.append(pltpu.make_async_copy(x_refs[a], out_refs[a].at[4 * x_i + 2 * y_i + c_i], local_sems.at[a]))
            mine[a].start()
            first.append([copy(a, 0, me, sibling, src=x_refs[a])]
                         + [copy(a, 1 + j, me, (*chip, c_i), src=x_refs[a]) for j, chip in enumerate(chips)])
            for cp in first[a]:
                cp.start()
            passed.append([copy(a, 4 + j, (*chip, c_i), sibling) for j, chip in enumerate(chips)])
        for j, chip in enumerate(chips):
            for a in range(n_arr):
                copy(a, 1 + j, (*chip, c_i), me).wait_recv()
                passed[a][j].start()
        for a in range(n_arr):
            copy(a, 0, sibling, me).wait_recv()
            for j, chip in enumerate(chips):
                copy(a, 4 + j, (*chip, 1 - c_i), me).wait_recv()
        for a in range(n_arr):
            for cp in first[a] + passed[a]:
                cp.wait_send()
            mine[a].wait()

    any_spec = pl.BlockSpec(memory_space=pl.ANY)
    return pl.pallas_call(
        body, name=name, out_shape=[jax.ShapeDtypeStruct((N_DEV,) + x.shape, x.dtype) for x in shards],
        in_specs=[any_spec] * n_arr, out_specs=[any_spec] * n_arr,
        scratch_shapes=[pltpu.SemaphoreType.DMA((n_arr, 7)), pltpu.SemaphoreType.DMA((n_arr, 7)),
                        pltpu.SemaphoreType.DMA((n_arr,))],
    )(*shards)


_HBM = pl.BlockSpec(memory_space=pltpu.HBM)
_SEM = pl.BlockSpec(memory_space=pltpu.SEMAPHORE)


def _plan_copies(plan, src_refs, land_refs, send_sems, recv_sems):
    copies = []
    for s_ref, l_ref in zip(src_refs, land_refs):
        for src, dst, peer in plan(s_ref, l_ref):
            k = len(copies)
            copies.append(pltpu.make_async_remote_copy(
                src_ref=src, dst_ref=dst, send_sem=send_sems.at[k], recv_sem=recv_sems.at[k], device_id=peer,
                device_id_type=pl.DeviceIdType.MESH))
    return copies


def _split_start(srcs, lands, plan, n_copy, name, after=None):
    n = len(srcs)
    n_in = 2 * n + (after is not None)

    def body(*refs):
        for cp in _plan_copies(plan, refs[:n], refs[n:2 * n], refs[n_in], refs[n_in + 1]):
            cp.start()
        refs[-1][...] = jnp.zeros_like(refs[-1])

    sems = pltpu.SemaphoreType.DMA((n * n_copy,))
    res = pl.pallas_call(
        body, name=name,
        out_shape=(sems, sems, *[pltpu.HBM(a.shape, a.dtype) for a in list(srcs) + list(lands)],
                   jax.ShapeDtypeStruct((8, LANE), F32)),
        in_specs=[_HBM] * (2 * n) + [pl.BlockSpec(memory_space=pl.ANY)] * (after is not None),
        out_specs=(_SEM, _SEM, *[_HBM] * (2 * n), pl.BlockSpec(memory_space=pltpu.VMEM)),
        input_output_aliases={i: 2 + i for i in range(2 * n)},
        compiler_params=pltpu.CompilerParams(has_side_effects=pltpu.SideEffectType.DATAFLOW_SIDE_EFFECTING),
    )(*[pltpu.with_memory_space_constraint(a, pltpu.HBM) for a in list(srcs) + list(lands)],
      *([after] if after is not None else []))
    return res[0], res[1], list(res[2:2 + n]), list(res[2 + n:2 + 2 * n]), res[-1]


def _split_wait(send_sems, recv_sems, srcs, lands, after, plan, name):
    n = len(srcs)

    def body(*refs):
        copies = _plan_copies(plan, refs[:n], refs[n:2 * n], refs[2 * n], refs[2 * n + 1])
        for cp in copies:
            cp.wait_send()
        for cp in copies:
            cp.wait_recv()

    res = pl.pallas_call(
        body, name=name, out_shape=tuple(pltpu.HBM(a.shape, a.dtype) for a in list(srcs) + list(lands)),
        in_specs=[_HBM] * (2 * n) + [_SEM, _SEM, pl.BlockSpec(memory_space=pl.ANY)], out_specs=tuple([_HBM] * (2 * n)),
        input_output_aliases={i: i for i in range(2 * n)},
        compiler_params=pltpu.CompilerParams(has_side_effects=pltpu.SideEffectType.DATAFLOW_SIDE_EFFECTING),
    )(*srcs, *lands, send_sems, recv_sems, after)
    return list(res[:n]), list(res[n:])


def _plan_broadcast(src, land):
    x_i, y_i, c_i = lax.axis_index("x"), lax.axis_index("y"), lax.axis_index("c")
    me = 4 * x_i + 2 * y_i + c_i
    return [(src, land.at[me], (x_i ^ (k >> 2), y_i ^ ((k >> 1) & 1), c_i ^ (k & 1))) for k in range(1, N_DEV)]


def _plan_scatter(src, land):
    x_i, y_i, c_i = lax.axis_index("x"), lax.axis_index("y"), lax.axis_index("c")
    me = 4 * x_i + 2 * y_i + c_i
    plan = []
    for k in range(1, N_DEV):
        px, py, pc = x_i ^ (k >> 2), y_i ^ ((k >> 1) & 1), c_i ^ (k & 1)
        plan.append((src.at[4 * px + 2 * py + pc], land.at[me], (px, py, pc)))
    return plan


def _adam_math(g, w, m, v):
    m_new = ADAM_B1 * m + (1.0 - ADAM_B1) * g
    v_new = ADAM_B2 * v + (1.0 - ADAM_B2) * (g * g)
    m_hat = m_new / (1.0 - ADAM_B1 ** ADAM_STEP)
    v_hat = v_new / (1.0 - ADAM_B2 ** ADAM_STEP)
    return -ADAM_LR * (m_hat / (jnp.sqrt(v_hat) + ADAM_EPS) + ADAM_WD * w), m_new, v_new


def _adam(slots, w, m, v, name, own=None, own_idx=None):
    n_slot, rows, cols = slots.shape
    tr = ROW_TILE if rows % ROW_TILE == 0 else rows
    has_own = own is not None

    def body(*refs):
        if has_own:
            idx_ref, own_ref, refs = refs[0], refs[1], refs[2:]
        s_ref, w_ref, m_ref, v_ref, g_ref, d_ref, mo_ref, vo_ref = refs
        g = own_ref[...].astype(F32) if has_own else s_ref[0].astype(F32)
        for k in range(0 if has_own else 1, n_slot):
            part = s_ref[k].astype(F32)
            g = g + (jnp.where(idx_ref[0] == k, 0.0, part) if has_own else part)
        g_ref[...] = g
        d_ref[...], mo_ref[...], vo_ref[...] = _adam_math(g, w_ref[...], m_ref[...], v_ref[...])

    spec = pl.BlockSpec((tr, cols), lambda i, *_: (i, 0))
    in_specs = [pl.BlockSpec((n_slot, tr, cols), lambda i, *_: (0, i, 0)), spec, spec, spec]
    if has_own:
        in_specs = [pl.BlockSpec((None, tr, cols), lambda i, idx: (idx[0], i, 0))] + in_specs
    grid_spec = pltpu.PrefetchScalarGridSpec(num_scalar_prefetch=1 if has_own else 0, grid=(rows // tr,), in_specs=in_specs,
                                             out_specs=[spec] * 4)
    ins = ([own_idx, own] if has_own else []) + [slots, w, m, v]
    return pl.pallas_call(
        body, name=name, grid_spec=grid_spec, out_shape=[jax.ShapeDtypeStruct((rows, cols), F32)] * 4,
        compiler_params=pltpu.CompilerParams(dimension_semantics=("parallel",)),
    )(*ins)


PACK_ROWS, PACK_W = 24, 1536
REPL_W = (("ssd_conv_b", 1536), ("ssd_dt_bias", 16), ("ssd_A_log", 16), ("ssd_D", 16), ("ssd_norm_w", 1024),
          ("mla_q_norm_w", 384), ("mla_kv_norm_w", 256), ("mla_out_norm_w", 1024), ("ln_mix_g", 1024),
          ("ln_mix_b", 1024), ("ln_ffn_g", 1024), ("ln_ffn_b", 1024))
LOSS_ROW = 4 + len(REPL_W)


def _pack_small(conv_w_grad, grads, loss, name="pack_small"):
    def body(*refs):
        cw_ref, g_refs, loss_ref, o_ref = refs[0], refs[1:1 + len(REPL_W)], refs[1 + len(REPL_W)], refs[-1]
        o_ref[...] = jnp.zeros_like(o_ref)
        o_ref[0:4, :] = cw_ref[...]
        for i, g_ref in enumerate(g_refs):
            o_ref[4 + i:5 + i, 0:g_ref.shape[1]] = g_ref[...]
        o_ref[LOSS_ROW:LOSS_ROW + 1, 0:LANE] = loss_ref[...]

    return pl.pallas_call(body, name=name, out_shape=jax.ShapeDtypeStruct((PACK_ROWS, PACK_W), F32))(conv_w_grad, *grads, loss)


def _adam_small(gathered, wmv, name="adam_small"):
    def body(*refs):
        s_ref = refs[0]
        in_refs = refs[1:1 + 3 * len(REPL_W)]
        cw_ref, loss_ref = refs[1 + 3 * len(REPL_W)], refs[2 + 3 * len(REPL_W)]
        out_refs = refs[3 + 3 * len(REPL_W):-1]
        tot = refs[-1]
        acc = s_ref[0]
        for k in range(1, N_DEV):
            acc = acc + s_ref[k]
        tot[...] = acc
        cw_ref[...] = tot[0:4, :]
        loss_ref[...] = tot[LOSS_ROW:LOSS_ROW + 1, 0:LANE]
        for i, (_, width) in enumerate(REPL_W):
            g = tot[4 + i:5 + i, 0:width]
            w_ref, m_ref, v_ref = in_refs[3 * i:3 * i + 3]
            g_ref, d_ref, mo_ref, vo_ref = out_refs[4 * i:4 * i + 4]
            g_ref[...] = g
            d_ref[...], mo_ref[...], vo_ref[...] = _adam_math(g, w_ref[...], m_ref[...], v_ref[...])

    flat_in = [a for triple in wmv for a in triple]
    out_shape = [jax.ShapeDtypeStruct((4, PACK_W), F32), jax.ShapeDtypeStruct((1, LANE), F32)]
    for _, width in REPL_W:
        out_shape += [jax.ShapeDtypeStruct((1, width), F32)] * 4
    res = pl.pallas_call(body, name=name, out_shape=out_shape, scratch_shapes=[pltpu.VMEM((PACK_ROWS, PACK_W), F32)])(
        gathered, *flat_in)
    return res[0], res[1], [res[2 + 4 * i:6 + 4 * i] for i in range(len(REPL_W))]


def _cols_full(g):
    return jnp.transpose(g, (1, 0, 2)).reshape(g.shape[1], -1)


def _cols_split(full):
    k_dim, n_dim = full.shape
    return jnp.transpose(full.reshape(k_dim, N_DEV, n_dim // N_DEV), (1, 0, 2))


PROJ_BLOCK = {"z": (1024, 0), "dt": (LANE, 8), "q_c": (MLA_Q_RANK, 3), "xbc": (SSD_XBC, 1), "kv_c": (MLA_KV_RANK, 12),
              "k_rope": (LANE, 26)}


def _win_pad(wt):
    z = lambda n: jnp.zeros((n, wt.shape[1]), wt.dtype)
    return jnp.concatenate([wt[:1024], wt[2560:2576], z(112), wt[2576:2960], wt[1024:2560], wt[2960:3216], wt[3216:3248],
                            z(96)], axis=0)


def _win_unpad(wt):
    return jnp.concatenate([wt[:1024], wt[1536:3072], wt[1024:1040], wt[1152:1536], wt[3072:3328], wt[3328:3360]], axis=0)


def _heads_split_t(wt, a, b):
    w3 = wt.reshape(MLA_HEADS, a + b, wt.shape[1])
    return jnp.concatenate([w3[:, :a].reshape(-1, wt.shape[1]), w3[:, a:].reshape(-1, wt.shape[1])], axis=0)


def _heads_merge_t(wt, a, b):
    wa = wt[:MLA_HEADS * a].reshape(MLA_HEADS, a, wt.shape[1])
    wb = wt[MLA_HEADS * a:].reshape(MLA_HEADS, b, wt.shape[1])
    return jnp.concatenate([wa, wb], axis=1).reshape(-1, wt.shape[1])


def _heads_split(w, a, b):
    k_dim = w.shape[0]
    w3 = w.reshape(k_dim, MLA_HEADS, a + b)
    return jnp.concatenate([w3[:, :, :a].reshape(k_dim, -1), w3[:, :, a:].reshape(k_dim, -1)], axis=1)


def _heads_merge(w, a, b):
    k_dim = w.shape[0]
    wa = w[:, :MLA_HEADS * a].reshape(k_dim, MLA_HEADS, a)
    wb = w[:, MLA_HEADS * a:].reshape(k_dim, MLA_HEADS, b)
    return jnp.concatenate([wa, wb], axis=2).reshape(k_dim, -1)


def _pad_lanes(v, width=LANE):
    return jnp.concatenate([v, jnp.zeros((v.shape[0], width - v.shape[1]), v.dtype)], axis=1)


def _local_step(x, p, positions, tgt, W, P, comm=None):
    comm = comm or {}
    zero_tok = jnp.zeros((8, LANE), F32)
    s_dim = x.shape[0]
    inv_freq = 1.0 / (ROPE_BASE ** (jnp.arange(0, MLA_ROPE, 2, dtype=F32) / MLA_ROPE))
    ang = positions.astype(F32)[:, None] * inv_freq
    cos, sin = jnp.cos(ang), jnp.sin(ang)
    cos32 = jnp.concatenate([cos, cos], axis=1)
    sin32 = jnp.concatenate([-sin, sin], axis=1)
    cos512, sin512 = jnp.tile(cos32, (1, 16)), jnp.tile(sin32, (1, 16))
    cos128, sin128 = jnp.tile(cos32, (1, 4)), jnp.tile(sin32, (1, 4))
    bias_p, alog_p = _pad_lanes(P["ssd_dt_bias"]), _pad_lanes(P["ssd_A_log"])
    d_x = jnp.repeat(P["ssd_D"], SSD_HEAD_DIM, axis=1)

    xb, pb = x.astype(BF16), p.astype(BF16)
    proj = _mm(xb, W["w_in"], tb=True, after=comm.get("token0", zero_tok), name="mm_in")
    z, qc, kvc, kr = [(proj,) + PROJ_BLOCK[n] for n in ("z", "q_c", "kv_c", "k_rope")]
    xbca = _conv_fwd(proj, PROJ_BLOCK["xbc"][1], P["ssd_conv_w"], P["ssd_conv_b"])
    y, states = _ssd_fwd(xbca, proj, PROJ_BLOCK["dt"][1], bias_p, alog_p, d_x)
    (yssd,) = _rowwise(_gate_rms, [y, z], [P["ssd_norm_w"]], [(1024, BF16)], name="ssd_gate_norm")
    qn, kvn, krt = _rowwise(lambda a, c, u, cs, sn, wq, wkv: (_rms(a, wq), _rms(c, wkv), _spread4(_rope_fwd_fn(u, cs, sn))),
                            [qc, kvc, kr, cos128, sin128], [P["mla_q_norm_w"], P["mla_kv_norm_w"]],
                            [(MLA_Q_RANK, BF16), (MLA_KV_RANK, BF16), LANE], name="qkv_norm_rope_k")
    q = _mm(qn, W["mla_w_q_b"], tb=True, name="mm_q")
    kv = _mm(kvn, W["mla_w_kv_b"], name="mm_kv")
    (qr,) = _rowwise(_rope_fwd_fn, [(q, 512, 2), cos512, sin512], [], [512], name="rope_q")
    att = _att_fwd(q, qr, kv, krt)
    (ymla,) = _rowwise(_rms, [att], [P["mla_out_norm_w"]], [(1024, BF16)], name="out_norm")
    ycat = jnp.concatenate([yssd, ymla], axis=1)
    if "late_weights" in comm:
        W = {**W, **comm["late_weights"]("out", ycat)}
    mix = _mm(ycat, W["w_out"], name="mm_out")
    f_h1 = lambda xv, mv, g, b: _ln(ALPHA * xv + mv, g, b)
    h1, h1b = _rowwise(lambda *a: (f_h1(*a),) * 2, [x, mix], [P["ln_mix_g"], P["ln_mix_b"]], [1024, (1024, BF16)],
                       name="ln_mix")
    if "late_weights" in comm:
        W = {**W, **comm["late_weights"]("ffn", h1b)}
    hg = _mm(h1b, W["w_ffn_gate"], tb=True, out_dtype=BF16, name="mm_gate")
    hu, act = _mm(h1b, W["w_ffn_up"], tb=True, name="mm_up",
                  epilogue=(lambda u, g: (u, _silu(g.astype(F32)) * u), [hg], [BF16, BF16]))
    pg = _mm(h1b, W["w_ple_gate"], name="mm_ple_gate")
    pp = _mm(pb, W["w_ple_proj"], name="mm_ple")
    ffn = _mm(act, W["w_ffn_down"], name="mm_down")

    def final_fn(hv, fv, pg, ppv, tv, g, b):
        sg = _sigmoid(pg)
        n, r = _ln_parts(ALPHA * hv + fv + sg * ppv)
        diff = n * g + b - tv
        loss = 0.5 * jnp.sum(jnp.mean(diff * diff, axis=-1, keepdims=True), axis=0, keepdims=True)
        d_pre, d_g, d_b = _ln_bwd(n, r, g, diff * (1.0 / D_MODEL))
        return (ALPHA * d_pre, d_pre, d_pre * ppv * (sg * (1.0 - sg)), d_pre * sg, d_g, d_b,
                jnp.broadcast_to(loss, (1, LANE)))

    dh1_a, dffn, dpg, dpp, g_ffn_g, g_ffn_b, loss = _rowwise(
        final_fn, [h1, ffn, pg, pp, tgt], [P["ln_ffn_g"], P["ln_ffn_b"]], [1024] + [(1024, BF16)] * 3,
        [1024, 1024, LANE], name="final")

    G = {}
    def swiglu_bwd(d, g, u):
        g, u = g.astype(F32), u.astype(F32)
        sg = _sigmoid(g)
        return d * u * (sg * (1.0 + g * (1.0 - sg))), d * (g * sg)

    dg, du = _mm(dffn, W["w_ffn_down"], tb=True, name="mm_down_dx",
                 epilogue=(swiglu_bwd, [hg, hu], [BF16, BF16]))
    G["w_ffn_down"] = _mm(act, dffn, ta=True, out_dtype=GRAD_DT, name="mm_down_dw")
    dh1 = _mm(dg, W["w_ffn_gate"], add=dh1_a, name="mm_gate_dx")
    dh1 = _mm(du, W["w_ffn_up"], add=dh1, name="mm_up_dx")
    dh1 = _mm(dpg, W["w_ple_gate"], tb=True, add=dh1, name="mm_ple_gate_dx")
    G["w_ffn_gate"] = _mm(dg, h1b, ta=True, out_dtype=GRAD_DT, name="mm_gate_dw")
    G["w_ffn_up"] = _mm(du, h1b, ta=True, out_dtype=GRAD_DT, name="mm_up_dw")
    G["w_ple_gate"] = _mm(h1b, dpg, ta=True, out_dtype=GRAD_DT, name="mm_ple_gate_dw")
    G["w_ple_proj"] = _mm(pb, dpp, ta=True, out_dtype=GRAD_DT, name="mm_ple_dw")
    def ln_mix_bwd(xv, mv, dv, g, b):
        n, r = _ln_parts(ALPHA * xv + mv)
        d_pre, d_g, d_b = _ln_bwd(n, r, g, dv)
        return ALPHA * d_pre, d_pre, d_g, d_b

    dx_a, dmix, g_mix_g, g_mix_b = _rowwise(ln_mix_bwd, [x, mix, dh1], [P["ln_mix_g"], P["ln_mix_b"]],
                                            [1024, (1024, BF16)], [1024, 1024], name="ln_mix_bwd")
    dycat = _mm(dmix, W["w_out"], tb=True, name="mm_out_dx")
    G["w_out"] = _mm(ycat, dmix, ta=True, out_dtype=GRAD_DT, name="mm_out_dw")

    grads_done = comm.get("grads", lambda group, grads: zero_tok)
    tok1 = grads_done("ffn", G)
    datt, g_out_norm = _rowwise(lambda a, dv, w, t: _rms_bwd(a, w, dv + jnp.min(t)), [att, (dycat, 1024, 1)],
                                [P["mla_out_norm_w"], tok1], [1024], [1024], name="out_norm_bwd")
    dqn_nope, dqr, dkn, dv, dkrt = _att_bwd(q, qr, kv, krt, att, datt)
    dkv = jnp.concatenate([dkn, dv], axis=1)
    (dq_rope,) = _rowwise(lambda d0, d1, c, s: _rope_bwd_fn(d0 + d1, c, s), [(dqr, 512, 0), (dqr, 512, 1), cos512, sin512],
                          [], [(512, BF16)], name="rope_q_bwd")

    def rope_k_bwd(*a):
        d = _spread4(functools.reduce(lambda u, w: u + w, a[:-2]))
        lane = lax.broadcasted_iota(jnp.int32, d.shape, 1)
        return _rope_bwd_fn(jnp.where(lane < MLA_ROPE, d, 0.0), a[-2], a[-1])

    (dkr,) = _rowwise(rope_k_bwd, [(dkrt, LANE, k) for k in range(MLA_HEADS // 2)] + [cos128, sin128], [], [(LANE, BF16)],
                      name="rope_k_bwd")
    dq = jnp.concatenate([dqn_nope, dq_rope], axis=1)
    dqn = _mm(dq, W["mla_w_q_b"], name="mm_q_dx")
    G["mla_w_q_b"] = _mm(dq, qn, ta=True, out_dtype=GRAD_DT, name="mm_q_dw")
    dkvn = _mm(dkv, W["mla_w_kv_b"], tb=True, name="mm_kv_dx")
    G["mla_w_kv_b"] = _mm(kvn, dkv, ta=True, out_dtype=GRAD_DT, name="mm_kv_dw")
    tok2 = grads_done("mla", G)
    def qkv_norm_bwd(a, da, c, dc, wq, wkv, t):
        (d_a, d_wq), (d_c, d_wkv) = _rms_bwd(a, wq, da + jnp.min(t)), _rms_bwd(c, wkv, dc)
        return d_a, d_c, d_wq, d_wkv

    dqc, dkvc, g_q_norm, g_kv_norm = _rowwise(
        qkv_norm_bwd, [qc, dqn, kvc, dkvn], [P["mla_q_norm_w"], P["mla_kv_norm_w"], tok2],
        [(MLA_Q_RANK, BF16), (MLA_KV_RANK, BF16)], [MLA_Q_RANK, MLA_KV_RANK], name="qkv_norm_bwd")

    def gate_rms_bwd(yv, zv, dv, w, t):
        sg = _sigmoid(zv)
        silu = zv * sg
        gated = yv * silu
        r = lax.rsqrt(jnp.mean(gated * gated, axis=-1, keepdims=True) + EPS)
        n = gated * r
        dv = dv + jnp.min(t)
        g = dv * w
        d_gated = r * (g - n * jnp.mean(g * n, axis=-1, keepdims=True))
        return d_gated * silu, d_gated * yv * (sg * (1.0 + zv * (1.0 - sg))), _colsum(dv * n)

    dy, dz, g_ssd_norm = _rowwise(gate_rms_bwd, [y, z, (dycat, 1024, 0)], [P["ssd_norm_w"], tok1], [1024, (1024, BF16)],
                                  [1024], name="ssd_gate_norm_bwd")
    dxbca, ddtr, g_dt_bias, g_alog, g_d = _ssd_bwd(xbca, proj, PROJ_BLOCK["dt"][1], bias_p, alog_p, d_x, states, dy)
    da, g_conv_w, g_conv_b = _conv_bwd_pre(proj, PROJ_BLOCK["xbc"][1], P["ssd_conv_w"], P["ssd_conv_b"], dxbca)
    dxbc = _conv_bwd_in(da, P["ssd_conv_w"])

    small = {
        "ssd_conv_b": g_conv_b, "ssd_dt_bias": g_dt_bias, "ssd_A_log": g_alog, "ssd_D": g_d, "ssd_norm_w": g_ssd_norm,
        "mla_q_norm_w": g_q_norm, "mla_kv_norm_w": g_kv_norm, "mla_out_norm_w": g_out_norm, "ln_mix_g": g_mix_g,
        "ln_mix_b": g_mix_b, "ln_ffn_g": g_ffn_g, "ln_ffn_b": g_ffn_b,
    }
    packed = _pack_small(g_conv_w, [small[n] for n, _ in REPL_W], loss)
    if "small" in comm:
        comm["small"](packed)

    dproj = jnp.concatenate([dz, ddtr, dqc, dxbc, dkvc, dkr], axis=1)
    G["w_in"] = _mm(dproj, xb, ta=True, out_dtype=GRAD_DT, name="mm_in_dw")
    grad_x = _mm(dproj, W["w_in"], add=dx_a, after=grads_done("in", G), name="mm_in_dx")
    return grad_x, G, packed


def kernel(x, p, positions, w_in, ssd_conv_w, ssd_conv_b, ssd_dt_bias, ssd_A_log, ssd_D, ssd_norm_w, mla_q_norm_w, mla_w_q_b, mla_kv_norm_w, mla_w_kv_b, mla_out_norm_w, w_out, ln_mix_g, ln_mix_b, w_ffn_gate, w_ffn_up, w_ffn_down, w_ple_gate, w_ple_proj, ln_ffn_g, ln_ffn_b, loss_target, m_w_in, m_ssd_conv_w, m_ssd_conv_b, m_ssd_dt_bias, m_ssd_A_log, m_ssd_D, m_ssd_norm_w, m_mla_q_norm_w, m_mla_w_q_b, m_mla_kv_norm_w, m_mla_w_kv_b, m_mla_out_norm_w, m_w_out, m_ln_mix_g, m_ln_mix_b, m_w_ffn_gate, m_w_ffn_up, m_w_ffn_down, m_w_ple_gate, m_w_ple_proj, m_ln_ffn_g, m_ln_ffn_b, v_w_in, v_ssd_conv_w, v_ssd_conv_b, v_ssd_dt_bias, v_ssd_A_log, v_ssd_D, v_ssd_norm_w, v_mla_q_norm_w, v_mla_w_q_b, v_mla_kv_norm_w, v_mla_w_kv_b, v_mla_out_norm_w, v_w_out, v_ln_mix_g, v_ln_mix_b, v_w_ffn_gate, v_w_ffn_up, v_w_ffn_down, v_w_ple_gate, v_w_ple_proj, v_ln_ffn_g, v_ln_ffn_b):
    args = dict(locals())
    core = lax.axis_index("c")
    me = 4 * lax.axis_index("x") + 2 * lax.axis_index("y") + core

    conv_sh = ssd_conv_w[0]
    conv_hi = conv_sh.astype(BF16)
    conv_lo = (conv_sh - conv_hi.astype(F32)).astype(BF16)
    stored = lambda n, pre="": jnp.transpose(args[pre + n][0]) if n in TRANSPOSED else args[pre + n][0]
    shards = {n: stored(n).astype(BF16) for n in BIG}
    rows_full = lambda g: g.reshape(-1, g.shape[2])

    early = _gather_many([shards[n] for n in EARLY] + [jnp.concatenate([conv_hi, conv_lo], axis=0)], "gather_early")
    gw = dict(zip(EARLY, early[:-1]))
    conv_g = early[-1].astype(F32)
    W = {
        "w_in": _win_pad(rows_full(gw["w_in"])),
        "mla_w_q_b": _heads_split_t(rows_full(gw["mla_w_q_b"]), MLA_NOPE, MLA_ROPE),
        "mla_w_kv_b": _heads_split(_cols_full(gw["mla_w_kv_b"]), MLA_NOPE, MLA_V),
    }
    P = {n: args[n] for n, _ in REPL_W}
    P["ssd_conv_w"] = _cols_full(conv_g[:, :4] + conv_g[:, 4:])

    late, after = {}, early[0]
    for group, names in LATE.items():
        lands = [lax.dynamic_update_slice(lax.empty((N_DEV,) + shards[n].shape, BF16), shards[n][None], (me, 0, 0)) for n in names]
        late[group] = _split_start([shards[n] for n in names], lands, _plan_broadcast, N_DEV - 1,
                                   "gather_" + group + "_start", after=after)
        after = late[group][4]

    def late_weights(group, after):
        _, got = _split_wait(*late[group][:4], after, _plan_broadcast, "gather_" + group + "_wait")
        return {n: _cols_full(g) if n == "w_ple_proj" else rows_full(g) for n, g in zip(LATE[group], got)}

    def to_blocks(n, g):
        if n == "w_in":
            g = _win_unpad(g)
        elif n == "mla_w_q_b":
            g = _heads_merge_t(g, MLA_NOPE, MLA_ROPE)
        elif n == "mla_w_kv_b":
            g = _heads_merge(g, MLA_NOPE, MLA_V)
        if n in ROW_SHARDED or n in TRANSPOSED:
            return g.reshape(N_DEV, -1, g.shape[1])
        return _cols_split(g)

    flight = {}

    def grads(group, G):
        gl = [to_blocks(n, G[n]) for n in GRAD_GROUPS[group]]
        flight[group] = _split_start(gl, [lax.empty(g.shape, g.dtype) for g in gl], _plan_scatter, N_DEV - 1,
                                     "grads_" + group + "_start", after=flight["small"][4] if group == "in" else None)
        return flight[group][4]

    def small(packed):
        land = lax.dynamic_update_slice(lax.empty((N_DEV,) + packed.shape, F32), packed[None], (me, 0, 0))
        flight["small"] = _split_start([packed], [land], _plan_broadcast, N_DEV - 1, "small_start")

    grad_x, G, packed = _local_step(x[0], p[0, 0], positions[0], loss_target[0], W, P,
                                    comm={"token0": after, "late_weights": late_weights, "grads": grads, "small": small})

    me_arr = me.astype(jnp.int32).reshape(1)
    big_out = {}

    def finish(group, after):
        mine, recv = _split_wait(*flight[group][:4], after, _plan_scatter, "grads_" + group + "_wait")
        for n, g, r in zip(GRAD_GROUPS[group], mine, recv):
            big_out[n] = _adam(r, stored(n), stored(n, "m_"), stored(n, "v_"), "adam_" + n, own=g, own_idx=me_arr)
        return big_out[GRAD_GROUPS[group][-1]][0]

    done = finish("ffn", grad_x)
    _, (small_all,) = _split_wait(*flight["small"][:4], done, _plan_broadcast, "small_wait")
    conv_sum, loss_row, small_out = _adam_small(small_all, [(args[n], args["m_" + n], args["v_" + n]) for n, _ in REPL_W])
    finish("in", finish("mla", done))
    conv_grad = lax.dynamic_slice_in_dim(conv_sum, me * 192, 192, axis=1)
    conv_out = _adam(conv_grad[None], conv_sh, m_ssd_conv_w[0], v_ssd_conv_w[0], "adam_conv")
    small_map = {n: small_out[i] for i, (n, _) in enumerate(REPL_W)}

    def outputs(idx):
        res = []
        for n in WEIGHT_ORDER:
            if n == "ssd_conv_w":
                res.append(conv_out[idx][None])
            elif n in big_out:
                res.append((jnp.transpose(big_out[n][idx]) if n in TRANSPOSED else big_out[n][idx])[None])
            else:
                res.append(small_map[n][idx])
        return res

    return (loss_row[0, 0], grad_x[None], *outputs(0), *outputs(1), *outputs(2), *outputs(3))
```

```python
import functools
import math

import numpy as np
import jax
import jax.numpy as jnp
from jax import lax
from jax.experimental import pallas as pl
from jax.experimental.pallas import tpu as pltpu

F32 = jnp.float32
BF16 = jnp.bfloat16

N_DEV = 8
D_MODEL = 1024
PLE_DIM = 256
SSD_HEADS = 16
SSD_HEAD_DIM = 64
SSD_INNER = 1024
SSD_STATE = 128
SSD_XBC = 1536
SSD_CHUNK = 128
MLA_HEADS = 16
MLA_Q_RANK = 384
MLA_KV_RANK = 256
MLA_NOPE = 64
MLA_ROPE = 32
MLA_V = 64
ROPE_BASE = 10000.0
D_FF = 2816
IN_WIDTH = 3248
IN_PAD = 3456
ALPHA = 2.0 ** 0.25
EPS = 1e-6
LN_EPS = 1e-5
ATT_SCALE = 1.0 / math.sqrt(MLA_NOPE + MLA_ROPE)
ADAM_LR, ADAM_B1, ADAM_B2, ADAM_EPS, ADAM_WD, ADAM_STEP = 0.001, 0.9, 0.999, 1e-08, 0.01, 10

LANE = 128
MXU_DIM = 256
MM_TM, MM_TN, MM_TK = 1408, 1408, 3456
MM_TM_DEEP_K, MM_TM_DEEP = 2048, 512
ROW_TILE = 512
ATT_TQ = 256

GRAD_DT = BF16

BIG = ("w_in", "mla_w_q_b", "mla_w_kv_b", "w_out", "w_ffn_gate", "w_ffn_up", "w_ffn_down", "w_ple_gate", "w_ple_proj")
EARLY = ("w_in", "mla_w_q_b", "mla_w_kv_b")
LATE = {"out": ("w_out", "w_ple_gate", "w_ple_proj"), "ffn": ("w_ffn_gate", "w_ffn_up", "w_ffn_down")}
GRAD_GROUPS = {"ffn": ("w_ffn_gate", "w_ffn_up", "w_ffn_down", "w_ple_gate", "w_ple_proj", "w_out"),
               "mla": ("mla_w_q_b", "mla_w_kv_b"), "in": ("w_in",)}
ROW_SHARDED = ("w_out", "w_ffn_down", "w_ple_gate")
TRANSPOSED = ("w_in", "mla_w_q_b", "w_ffn_gate", "w_ffn_up")
WEIGHT_ORDER = ("w_in", "ssd_conv_w", "ssd_conv_b", "ssd_dt_bias", "ssd_A_log", "ssd_D", "ssd_norm_w", "mla_q_norm_w",
                "mla_w_q_b", "mla_kv_norm_w", "mla_w_kv_b", "mla_out_norm_w", "w_out", "ln_mix_g", "ln_mix_b",
                "w_ffn_gate", "w_ffn_up", "w_ffn_down", "w_ple_gate", "w_ple_proj", "ln_ffn_g", "ln_ffn_b")


def _tile(dim, cap, prefer=None):
    cands = [t for t in range(LANE, min(cap, dim) + 1, LANE) if dim % t == 0]
    if not cands:
        return dim
    if prefer is None:
        return max(cands)
    fill = lambda t: t / (MXU_DIM * -(-t // MXU_DIM))
    good = min(0.9, max(fill(t) for t in cands))
    return min((t for t in cands if fill(t) >= good), key=lambda t: abs(t - prefer))


def _dot(a, b, dims=(((1,), (0,)), ((), ())), precision=None):
    return lax.dot_general(a, b, dims, preferred_element_type=F32, precision=precision)


_NT = (((1,), (1,)), ((), ()))
_TN = (((0,), (0,)), ((), ()))


def _mm(a, b, *, ta=False, tb=False, add=None, out_dtype=F32, after=None, epilogue=None, name):
    k_dim, m_dim = a.shape if ta else a.shape[::-1]
    n_dim, kb = b.shape if tb else b.shape[::-1]
    assert k_dim == kb
    tn, tk = _tile(n_dim, MM_TN, prefer=1024), _tile(k_dim, MM_TK, prefer=MM_TK)
    tm = _tile(m_dim, MM_TM if tk <= MM_TM_DEEP_K else MM_TM_DEEP)
    nk = k_dim // tk
    dims = (((0 if ta else 1,), (1 if tb else 0,)), ((), ()))
    a_spec = pl.BlockSpec((tk, tm), lambda i, j, k: (k, i)) if ta else pl.BlockSpec((tm, tk), lambda i, j, k: (i, k))
    b_spec = pl.BlockSpec((tn, tk), lambda i, j, k: (j, k)) if tb else pl.BlockSpec((tk, tn), lambda i, j, k: (k, j))
    o_spec = pl.BlockSpec((tm, tn), lambda i, j, k: (i, j))
    epi_fn, epi_in, out_dtypes = epilogue if epilogue else (None, [], [out_dtype])
    tiles = ([add] if add is not None else []) + list(epi_in)
    n_out = len(out_dtypes)

    def body(*refs):
        a_ref, b_ref = refs[:2]
        tile_refs = refs[2:2 + len(tiles)]
        out_refs = refs[len(refs) - n_out - (nk > 1):len(refs) - (nk > 1)]
        part = _dot(a_ref[...].astype(BF16), b_ref[...].astype(BF16), dims)
        if add is not None:
            part_add = lambda v: v + tile_refs[0][...]
        else:
            part_add = lambda v: v

        def write(total):
            extra = [r[...] for r in tile_refs[add is not None:]]
            outs = epi_fn(total, *extra) if epi_fn else (total,)
            for o_ref, val in zip(out_refs, outs):
                o_ref[...] = val.astype(o_ref.dtype)

        if nk == 1:
            write(part_add(part))
            return
        acc = refs[-1]
        k = pl.program_id(2)

        @pl.when(k == 0)
        def _():
            acc[...] = part_add(part)

        @pl.when(k > 0)
        def _():
            acc[...] += part

        @pl.when(k == nk - 1)
        def _():
            write(acc[...])

    ins = [a, b] + tiles + ([after] if after is not None else [])
    specs = [a_spec, b_spec] + [o_spec] * len(tiles) + ([pl.BlockSpec(memory_space=pl.ANY)] if after is not None else [])
    res = pl.pallas_call(
        body, name=name, grid=(m_dim // tm, n_dim // tn, nk), in_specs=specs, out_specs=[o_spec] * n_out,
        out_shape=[jax.ShapeDtypeStruct((m_dim, n_dim), dt) for dt in out_dtypes],
        scratch_shapes=[pltpu.VMEM((tm, tn), F32)] if nk > 1 else [],
        compiler_params=pltpu.CompilerParams(dimension_semantics=("parallel", "parallel", "arbitrary")),
    )(*ins)
    return res if epilogue else res[0]


def _rowwise(fn, rows, consts, out_widths, acc_widths=(), *, name, tr=ROW_TILE):
    row_arrays, row_specs = [], []
    first_arr = rows[0][0] if isinstance(rows[0], tuple) else rows[0]
    s_dim = first_arr.shape[-2]
    tr = min(tr, s_dim)
    for r in rows:
        arr, width, cb = r if isinstance(r, tuple) else (r, r.shape[-1], 0)
        row_arrays.append(arr)
        if arr.ndim == 3:
            row_specs.append(pl.BlockSpec((None, tr, width), functools.partial(lambda i, k: (k, i, 0), k=cb)))
        else:
            row_specs.append(pl.BlockSpec((tr, width), functools.partial(lambda i, cb: (i, cb), cb=cb)))
    const_specs = [pl.BlockSpec(c.shape, lambda i: (0, 0)) for c in consts]
    nr, nc, no, na = len(rows), len(consts), len(out_widths), len(acc_widths)

    def body(*refs):
        ins = [r[...] for r in refs[:nr + nc]]
        res = fn(*ins)
        if not isinstance(res, (tuple, list)):
            res = (res,)
        out_refs = refs[nr + nc:nr + nc + no]
        acc_refs = refs[nr + nc + no:]
        for o_ref, val in zip(out_refs, res[:no]):
            o_ref[...] = val.astype(o_ref.dtype)
        first = pl.program_id(0) == 0
        for a_ref, val in zip(acc_refs, res[no:]):
            @pl.when(first)
            def _(a_ref=a_ref, val=val):
                a_ref[...] = val

            @pl.when(jnp.logical_not(first))
            def _(a_ref=a_ref, val=val):
                a_ref[...] += val

    outs = [w if isinstance(w, tuple) else (w, F32) for w in out_widths]
    out_shape = [jax.ShapeDtypeStruct((s_dim, w), dt) for w, dt in outs]
    out_shape += [jax.ShapeDtypeStruct((1, w), F32) for w in acc_widths]
    out_specs = [pl.BlockSpec((tr, w), lambda i: (i, 0)) for w, _ in outs]
    out_specs += [pl.BlockSpec((1, w), lambda i: (0, 0)) for w in acc_widths]
    res = pl.pallas_call(
        body, name=name, grid=(s_dim // tr,), in_specs=row_specs + const_specs, out_specs=out_specs, out_shape=out_shape,
        compiler_params=pltpu.CompilerParams(dimension_semantics=("arbitrary",)),
    )(*row_arrays, *consts)
    return res


def _colsum(v):
    return jnp.sum(v, axis=0, keepdims=True)


def _rms(u, g):
    return u * lax.rsqrt(jnp.mean(u * u, axis=-1, keepdims=True) + EPS) * g


def _ln(u, g, b):
    mu = jnp.mean(u, axis=-1, keepdims=True)
    d = u - mu
    var = jnp.mean(d * d, axis=-1, keepdims=True)
    return d * lax.rsqrt(var + LN_EPS) * g + b


def _sigmoid(v):
    return 1.0 / (1.0 + jnp.exp(-v))


def _silu(v):
    return v * _sigmoid(v)


def _softplus(v):
    y = jnp.exp(-jnp.abs(v))
    w = 1.0 + y
    log1p = jnp.where(w == 1.0, y, jnp.log(w) * y / jnp.where(w == 1.0, 1.0, w - 1.0))
    return jnp.maximum(v, 0.0) + log1p


def _gate_rms(y, z, w):
    return _rms(y * _silu(z), w)


def _rms_bwd(u, g, d_out):
    r = lax.rsqrt(jnp.mean(u * u, axis=-1, keepdims=True) + EPS)
    n = u * r
    gd = d_out * g
    return r * (gd - n * jnp.mean(gd * n, axis=-1, keepdims=True)), _colsum(d_out * n)


def _ln_parts(u):
    d = u - jnp.mean(u, axis=-1, keepdims=True)
    r = lax.rsqrt(jnp.mean(d * d, axis=-1, keepdims=True) + LN_EPS)
    return d * r, r


def _ln_bwd(n, r, g, d_out):
    gd = d_out * g
    d_u = r * (gd - jnp.mean(gd, axis=-1, keepdims=True) - n * jnp.mean(gd * n, axis=-1, keepdims=True))
    return d_u, _colsum(d_out * n), _colsum(d_out)


def _conv_pre(cur, prev, w, b, first):
    row = lax.broadcasted_iota(jnp.int32, cur.shape, 0)
    acc = cur * w[3:4, :] + b
    for j in (1, 2, 3):
        tail = jnp.where(first, 0.0, pltpu.roll(prev, j, 0))
        acc = acc + jnp.where(row >= j, pltpu.roll(cur, j, 0), tail) * w[3 - j:4 - j, :]
    return acc


def _conv_fwd(u, ucb, w, b, name="conv_fwd"):
    s_dim, width = u.shape[0], w.shape[1]
    tr = min(ROW_TILE, s_dim)

    def body(cur_ref, prev_ref, w_ref, b_ref, o_ref):
        pre = _conv_pre(cur_ref[...], prev_ref[...], w_ref, b_ref[...], pl.program_id(0) == 0)
        o_ref[...] = _silu(pre)

    return pl.pallas_call(
        body, name=name, grid=(s_dim // tr,),
        in_specs=[pl.BlockSpec((tr, width), lambda i: (i, ucb)),
                  pl.BlockSpec((tr, width), lambda i: (jnp.maximum(i - 1, 0), ucb)),
                  pl.BlockSpec(w.shape, lambda i: (0, 0)), pl.BlockSpec(b.shape, lambda i: (0, 0))],
        out_specs=pl.BlockSpec((tr, width), lambda i: (i, 0)), out_shape=jax.ShapeDtypeStruct((s_dim, width), F32),
        compiler_params=pltpu.CompilerParams(dimension_semantics=("arbitrary",)),
    )(u, u, w, b)


def _conv_bwd_pre(u, ucb, w, b, dact, name="conv_bwd_pre"):
    s_dim, width = u.shape[0], w.shape[1]
    tr = min(ROW_TILE, s_dim)

    def body(cur_ref, prev_ref, w_ref, b_ref, d_ref, da_ref, dw_ref, db_ref):
        first = pl.program_id(0) == 0
        cur, prev = cur_ref[...], prev_ref[...]
        pre = _conv_pre(cur, prev, w_ref, b_ref[...], first)
        sg = _sigmoid(pre)
        da = d_ref[...] * (sg * (1.0 + pre * (1.0 - sg)))
        da_ref[...] = da
        row = lax.broadcasted_iota(jnp.int32, cur.shape, 0)

        @pl.when(first)
        def _():
            dw_ref[...] = jnp.zeros_like(dw_ref)
            db_ref[...] = jnp.zeros_like(db_ref)

        db_ref[...] += _colsum(da)
        dw_ref[3:4, :] += _colsum(da * cur)
        for j in (1, 2, 3):
            tail = jnp.where(first, 0.0, pltpu.roll(prev, j, 0))
            sh = jnp.where(row >= j, pltpu.roll(cur, j, 0), tail)
            dw_ref[3 - j:4 - j, :] += _colsum(da * sh)

    return pl.pallas_call(
        body, name=name, grid=(s_dim // tr,),
        in_specs=[pl.BlockSpec((tr, width), lambda i: (i, ucb)),
                  pl.BlockSpec((tr, width), lambda i: (jnp.maximum(i - 1, 0), ucb)),
                  pl.BlockSpec(w.shape, lambda i: (0, 0)), pl.BlockSpec(b.shape, lambda i: (0, 0)),
                  pl.BlockSpec((tr, width), lambda i: (i, 0))],
        out_specs=[pl.BlockSpec((tr, width), lambda i: (i, 0)), pl.BlockSpec(w.shape, lambda i: (0, 0)),
                   pl.BlockSpec(b.shape, lambda i: (0, 0))],
        out_shape=[jax.ShapeDtypeStruct((s_dim, width), F32), jax.ShapeDtypeStruct(w.shape, F32),
                   jax.ShapeDtypeStruct(b.shape, F32)],
        compiler_params=pltpu.CompilerParams(dimension_semantics=("arbitrary",)),
    )(u, u, w, b, dact)


def _conv_bwd_in(da, w, name="conv_bwd_in"):
    s_dim, width = da.shape
    tr = min(ROW_TILE, s_dim)
    n = s_dim // tr

    def body(cur_ref, nxt_ref, w_ref, o_ref):
        last = pl.program_id(0) == n - 1
        cur, nxt = cur_ref[...], nxt_ref[...]
        row = lax.broadcasted_iota(jnp.int32, cur.shape, 0)
        acc = cur * w_ref[3:4, :]
        for j in (1, 2, 3):
            head = jnp.where(last, 0.0, pltpu.roll(nxt, tr - j, 0))
            acc = acc + jnp.where(row < tr - j, pltpu.roll(cur, tr - j, 0), head) * w_ref[3 - j:4 - j, :]
        o_ref[...] = acc.astype(o_ref.dtype)

    return pl.pallas_call(
        body, name=name, grid=(n,),
        in_specs=[pl.BlockSpec((tr, width), lambda i: (i, 0)), pl.BlockSpec((tr, width), lambda i: (jnp.minimum(i + 1, n - 1), 0)),
                  pl.BlockSpec(w.shape, lambda i: (0, 0))],
        out_specs=pl.BlockSpec((tr, width), lambda i: (i, 0)), out_shape=jax.ShapeDtypeStruct((s_dim, width), BF16),
        compiler_params=pltpu.CompilerParams(dimension_semantics=("arbitrary",)),
    )(da, da, w)


def _sel_dot(a, sel, pieces, dims=(((1,), (0,)), ((), ())), sel_left=False):
    sel = sel.astype(BF16)
    acc, rest = None, a
    for _ in range(pieces):
        piece = rest.astype(BF16)
        rest = rest - piece.astype(F32)
        part = _dot(sel, piece, dims) if sel_left else _dot(piece, sel, dims)
        acc = part if acc is None else acc + part
    return acc


def _ssd_consts():
    L = SSD_CHUNK
    tri = np.tril(np.ones((L, L), np.float32))
    expand = np.zeros((LANE, SSD_INNER), np.float32)
    for h in range(SSD_HEADS):
        expand[h, h * SSD_HEAD_DIM:(h + 1) * SSD_HEAD_DIM] = 1.0
    return jnp.asarray(tri), jnp.asarray(expand), jnp.asarray(expand.T.copy())


def _ssd_prep(dt_ref, bias_ref, alog_ref, tri_ref, exp_ref, cs_s, cst_s, ex_s):
    L = SSD_CHUNK
    dt = _softplus(dt_ref[...] + bias_ref[...])
    a = -jnp.exp(alog_ref[...])
    cs = _sel_dot(dt * a, tri_ref[...], 3, sel_left=True)
    cs_s[...] = cs
    cst_s[...] = cs.T
    last = cs_s[L - 1:L, :]
    expand = exp_ref[...]
    ex_s[...] = _sel_dot(jnp.exp(cs), expand, 2)
    f_x = _sel_dot(jnp.exp(last - cs), expand, 2)
    dt_x = _sel_dot(dt, expand, 2)
    t_x = ex_s[L - 1:L, :]
    return dt, a, dt_x, f_x, t_x


def _decay_matrix(cs_s, cst_s, h, tril):
    seg = cs_s[:, h:h + 1] - cst_s[h:h + 1, :]
    return jnp.exp(jnp.where(tril, seg, -jnp.inf))


def _ssd_fwd(xbca, dtr, dtcb, bias, alog, d_x, name="ssd_fwd"):
    s_dim = xbca.shape[0]
    L = SSD_CHUNK
    nc = s_dim // L
    tri, expand, _ = _ssd_consts()

    def body(xs_ref, b_ref, c_ref, dt_ref, bias_ref, alog_ref, dx_ref, tri_ref, exp_ref,
             y_ref, st_ref, st_s, cs_s, cst_s, ex_s):
        @pl.when(pl.program_id(0) == 0)
        def _():
            st_s[...] = jnp.zeros_like(st_s)

        dt, a, dt_x, f_x, t_x = _ssd_prep(dt_ref, bias_ref, alog_ref, tri_ref, exp_ref, cs_s, cst_s, ex_s)
        st_ref[0] = st_s[...]
        row = lax.broadcasted_iota(jnp.int32, (L, L), 0)
        col = lax.broadcasted_iota(jnp.int32, (L, L), 1)
        tril = row >= col
        low = col < SSD_HEAD_DIM
        for g in range(2):
            bg = b_ref[:, g * LANE:(g + 1) * LANE]
            cg = c_ref[:, g * LANE:(g + 1) * LANE].astype(BF16)
            gmat = _dot(cg, bg.astype(BF16), _NT)
            bgt = bg.T.astype(BF16)
            for jj in range(4):
                j = 4 * g + jj
                sl = slice(j * LANE, (j + 1) * LANE)
                xp = xs_ref[:, sl]
                x_dt = xp * dt_x[:, sl]
                xb = x_dt.astype(BF16)
                yd = []
                for e in range(2):
                    lm = _decay_matrix(cs_s, cst_s, 2 * j + e, tril)
                    yd.append(_dot((gmat * lm).astype(BF16), xb))
                stp = st_s[j]
                z = _dot(cg, stp.astype(BF16))
                y_ref[:, sl] = jnp.where(low, yd[0], yd[1]) + ex_s[:, sl] * z + dx_ref[:, sl] * xp
                xf = (x_dt * f_x[:, sl]).astype(BF16)
                st_s[j] = t_x[:, sl] * stp + _dot(bgt, xf)

    const = lambda shape: pl.BlockSpec(shape, lambda c: tuple(0 for _ in shape))
    return pl.pallas_call(
        body, name=name, grid=(nc,),
        in_specs=[pl.BlockSpec((L, 1024), lambda c: (c, 0)), pl.BlockSpec((L, 256), lambda c: (c, 4)),
                  pl.BlockSpec((L, 256), lambda c: (c, 5)), pl.BlockSpec((L, LANE), lambda c: (c, dtcb)),
                  const((1, LANE)), const((1, LANE)), const((1, 1024)), const((L, L)), const((LANE, 1024))],
        out_specs=[pl.BlockSpec((L, 1024), lambda c: (c, 0)), pl.BlockSpec((1, 8, LANE, LANE), lambda c: (c, 0, 0, 0))],
        out_shape=[jax.ShapeDtypeStruct((s_dim, 1024), F32), jax.ShapeDtypeStruct((nc, 8, LANE, LANE), F32)],
        scratch_shapes=[pltpu.VMEM((8, LANE, LANE), F32), pltpu.VMEM((L, LANE), F32), pltpu.VMEM((LANE, L), F32),
                        pltpu.VMEM((L, 1024), F32)],
        compiler_params=pltpu.CompilerParams(dimension_semantics=("arbitrary",)),
    )(xbca, xbca, xbca, dtr, bias, alog, d_x, tri, expand)


def _ssd_bwd(xbca, dtr, dtcb, bias, alog, d_x, states, dy, name="ssd_bwd"):
    s_dim = xbca.shape[0]
    L = SSD_CHUNK
    nc = s_dim // L
    tri, expand, expand_t = _ssd_consts()

    def body(xs_ref, b_ref, c_ref, dt_ref, bias_ref, alog_ref, dx_ref, tri_ref, exp_ref, expt_ref,
             st_ref, dy_ref, dxbc_ref, ddt_ref, dbias_ref, dalog_ref, dd_ref,
             dst_s, cs_s, cst_s, ex_s, dcsx_s, ddtx_s, dcol_s, drow_s, dlast_s, dd_s):
        @pl.when(pl.program_id(0) == 0)
        def _():
            dst_s[...] = jnp.zeros_like(dst_s)
            dbias_ref[...] = jnp.zeros_like(dbias_ref)
            dalog_ref[...] = jnp.zeros_like(dalog_ref)
            dd_s[...] = jnp.zeros_like(dd_s)

        dt, a, dt_x, f_x, t_x = _ssd_prep(dt_ref, bias_ref, alog_ref, tri_ref, exp_ref, cs_s, cst_s, ex_s)
        row = lax.broadcasted_iota(jnp.int32, (L, L), 0)
        col = lax.broadcasted_iota(jnp.int32, (L, L), 1)
        tril = row >= col
        low = col < SSD_HEAD_DIM
        dcol_s[...] = jnp.zeros_like(dcol_s)
        drow_s[...] = jnp.zeros_like(drow_s)
        for g in range(2):
            bg = b_ref[:, g * LANE:(g + 1) * LANE]
            cg = c_ref[:, g * LANE:(g + 1) * LANE]
            bgb, cgb = bg.astype(BF16), cg.astype(BF16)
            gmat = _dot(cgb, bgb, _NT)
            d_g = jnp.zeros((L, L), F32)
            d_b = jnp.zeros((L, LANE), F32)
            d_c = jnp.zeros((L, LANE), F32)
            for jj in range(4):
                j = 4 * g + jj
                sl = slice(j * LANE, (j + 1) * LANE)
                xp = xs_ref[:, sl]
                dtp = dt_x[:, sl]
                x_dt = xp * dtp
                xb = x_dt.astype(BF16)
                dyp = dy_ref[:, sl]
                dd_s[:, sl] += _colsum(dyp * xp)
                d_xdt = jnp.zeros((L, LANE), F32)
                for e in range(2):
                    h = 2 * j + e
                    lm = _decay_matrix(cs_s, cst_s, h, tril)
                    m = gmat * lm
                    dye = jnp.where(low if e == 0 else jnp.logical_not(low), dyp, 0.0).astype(BF16)
                    d_m = jnp.where(tril, _dot(dye, xb, _NT), 0.0)
                    d_xdt = d_xdt + _dot(m.astype(BF16), dye, _TN)
                    d_g = d_g + d_m * lm
                    w = d_m * m
                    dcol_s[...] += jnp.where(col == h, jnp.sum(w, axis=1, keepdims=True), 0.0)
                    drow_s[...] += jnp.where(row == h, jnp.sum(w, axis=0, keepdims=True), 0.0)
                stp = st_ref[0, j]
                stb = stp.astype(BF16)
                dstn = dst_s[j]
                dstb = dstn.astype(BF16)
                e_p = ex_s[:, sl]
                f_p = f_x[:, sl]
                t_p = t_x[:, sl]
                z = _dot(cgb, stb)
                d_z = (e_p * dyp).astype(BF16)
                d_c = d_c + _dot(d_z, stb, _NT)
                d_xf = _dot(bgb, dstb)
                d_b = d_b + _dot((x_dt * f_p).astype(BF16), dstb, _NT)
                d_xdt = d_xdt + f_p * d_xf
                d_f = x_dt * d_xf * f_p
                dcsx_s[:, sl] = dyp * e_p * z - d_f
                dlast_s[:, sl] = _colsum(d_f) + _colsum(dstn * stp) * t_p
                dst_s[j] = _dot(cgb, d_z, _TN) + t_p * dstn
                dxbc_ref[:, sl] = dx_ref[:, sl] * dyp + d_xdt * dtp
                ddtx_s[:, sl] = d_xdt * xp
            d_gb = d_g.astype(BF16)
            dxbc_ref[:, 1024 + g * LANE:1024 + (g + 1) * LANE] = d_b + _dot(d_gb, cgb, _TN)
            dxbc_ref[:, 1280 + g * LANE:1280 + (g + 1) * LANE] = d_c + _dot(d_gb, bgb)

        expt = expt_ref[...]
        dlast = _sel_dot(jnp.broadcast_to(dlast_s[...], (8, 1024)), expt, 3)
        d_cs = dcol_s[...] - drow_s[...].T + _sel_dot(dcsx_s[...], expt, 3)
        rown = lax.broadcasted_iota(jnp.int32, (L, LANE), 0)
        d_cs = d_cs + jnp.where(rown == L - 1, jnp.sum(dlast, axis=0, keepdims=True) * 0.125, 0.0)
        d_da = _sel_dot(d_cs, tri_ref[...], 3, _TN, sel_left=True)
        d_dt = d_da * a + _sel_dot(ddtx_s[...], expt, 3)
        dalog_ref[...] += _colsum(d_da * dt) * a
        d_raw = d_dt * _sigmoid(dt_ref[...] + bias_ref[...])
        ddt_ref[...] = d_raw.astype(ddt_ref.dtype)
        dbias_ref[...] += _colsum(d_raw)
        dd8 = _sel_dot(jnp.broadcast_to(dd_s[...], (8, 1024)), expt, 3)
        dd_ref[...] = jnp.sum(dd8, axis=0, keepdims=True) * 0.125

    const = lambda shape: pl.BlockSpec(shape, lambda c: tuple(0 for _ in shape))
    rev = lambda cb: (lambda c: (nc - 1 - c, cb))
    return pl.pallas_call(
        body, name=name, grid=(nc,),
        in_specs=[pl.BlockSpec((L, 1024), rev(0)), pl.BlockSpec((L, 256), rev(4)), pl.BlockSpec((L, 256), rev(5)),
                  pl.BlockSpec((L, LANE), rev(dtcb)), const((1, LANE)), const((1, LANE)), const((1, 1024)), const((L, L)),
                  const((LANE, 1024)), const((1024, LANE)),
                  pl.BlockSpec((1, 8, LANE, LANE), lambda c: (nc - 1 - c, 0, 0, 0)), pl.BlockSpec((L, 1024), rev(0))],
        out_specs=[pl.BlockSpec((L, SSD_XBC), rev(0)), pl.BlockSpec((L, LANE), rev(0)), const((1, LANE)), const((1, LANE)),
                   const((1, LANE))],
        out_shape=[jax.ShapeDtypeStruct((s_dim, SSD_XBC), F32), jax.ShapeDtypeStruct((s_dim, LANE), BF16),
                   jax.ShapeDtypeStruct((1, LANE), F32), jax.ShapeDtypeStruct((1, LANE), F32),
                   jax.ShapeDtypeStruct((1, LANE), F32)],
        scratch_shapes=[pltpu.VMEM((8, LANE, LANE), F32), pltpu.VMEM((L, LANE), F32), pltpu.VMEM((LANE, L), F32),
                        pltpu.VMEM((L, 1024), F32), pltpu.VMEM((L, 1024), F32), pltpu.VMEM((L, 1024), F32),
                        pltpu.VMEM((L, LANE), F32), pltpu.VMEM((LANE, L), F32), pltpu.VMEM((1, 1024), F32),
                        pltpu.VMEM((1, 1024), F32)],
        compiler_params=pltpu.CompilerParams(dimension_semantics=("arbitrary",)),
    )(xbca, xbca, xbca, dtr, bias, alog, d_x, tri, expand, expand_t, states, dy)


def _swap_halves(u):
    width = u.shape[1]
    lane = lax.broadcasted_iota(jnp.int32, u.shape, 1)
    return jnp.where(lane % MLA_ROPE < MLA_ROPE // 2, pltpu.roll(u, width - MLA_ROPE // 2, 1), pltpu.roll(u, MLA_ROPE // 2, 1))


def _rope_fwd_fn(u, cos, sin):
    return u * cos + _swap_halves(u) * sin


def _rope_bwd_fn(d, cos, sin):
    return d * cos + _swap_halves(d * sin)


def _spread4(v):
    return v + pltpu.roll(v, 32, 1) + pltpu.roll(v, 64, 1) + pltpu.roll(v, 96, 1)


def _att_masks(tq):
    lane = lax.broadcasted_iota(jnp.int32, (tq, LANE), 1)
    return lane // MLA_NOPE, lane // MLA_ROPE


def _att_tile(i, tq):
    klen = (i + 1) * tq
    qpos = i * tq + lax.broadcasted_iota(jnp.int32, (tq, klen), 0)
    kpos = lax.broadcasted_iota(jnp.int32, (tq, klen), 1)
    return slice(i * tq, (i + 1) * tq), klen, qpos >= kpos


def _att_qcat(qn_t, qr_t, par, e, half_id, grp_id):
    return jnp.concatenate([jnp.where(half_id == par, qn_t * ATT_SCALE, 0.0), jnp.where(grp_id == e, qr_t * ATT_SCALE, 0.0)],
                           axis=1).astype(BF16)


def _att_softmax(scores, causal):
    s = jnp.where(causal, scores, -jnp.inf)
    e = jnp.exp(s - jnp.max(s, axis=1, keepdims=True))
    return e, 1.0 / jnp.sum(e, axis=1, keepdims=True)


def _att_specs(s_dim):
    col = lambda f: pl.BlockSpec((s_dim, LANE), lambda j: (0, f(j)))
    return [col(lambda j: j), col(lambda j: j // 2), col(lambda j: j), col(lambda j: 0), col(lambda j: 8 + j)]


def _att_fwd(q, qr, kv, krt, name="att_fwd"):
    s_dim = q.shape[0]
    tq = min(ATT_TQ, s_dim)

    def body(qn_ref, qr_ref, kn_ref, krt_ref, v_ref, o_ref, kcat_s, vb_s):
        e0 = 2 * (pl.program_id(0) % 2)
        half_id, grp_id = _att_masks(tq)
        kcat_s[...] = jnp.concatenate([kn_ref[...], krt_ref[...]], axis=1).astype(BF16)
        vb_s[...] = v_ref[...].astype(BF16)
        for i in range(s_dim // tq):
            rows, klen, causal = _att_tile(i, tq)
            qn_t, qr_t = qn_ref[rows, :], qr_ref[rows, :]
            scores = [_dot(_att_qcat(qn_t, qr_t, par, e0 + par, half_id, grp_id), kcat_s[0:klen, :], _NT) for par in range(2)]
            probs = [_att_softmax(s, causal) for s in scores]
            outs = [_dot(e.astype(BF16), vb_s[0:klen, :]) * inv_l for e, inv_l in probs]
            o_ref[rows, :] = jnp.where(half_id == 0, outs[0], outs[1])

    return pl.pallas_call(
        body, name=name, grid=(MLA_HEADS // 2,), in_specs=_att_specs(s_dim),
        out_specs=pl.BlockSpec((s_dim, LANE), lambda j: (0, j)), out_shape=jax.ShapeDtypeStruct((s_dim, 1024), F32),
        scratch_shapes=[pltpu.VMEM((s_dim, 2 * LANE), BF16), pltpu.VMEM((s_dim, LANE), BF16)],
        compiler_params=pltpu.CompilerParams(dimension_semantics=("parallel",)),
    )(q, qr, kv, krt, kv)


def _att_bwd(q, qr, kv, krt, o, do, name="att_bwd"):
    s_dim = q.shape[0]
    tq = min(ATT_TQ, s_dim)

    def body(qn_ref, qr_ref, kn_ref, krt_ref, v_ref, o_ref, do_ref, dqn_ref, dqr_ref, dkn_ref, dv_ref, dkrt_ref,
             kcat_s, vb_s):
        e0 = 2 * (pl.program_id(0) % 2)
        half_id, grp_id = _att_masks(tq)
        kcat_s[...] = jnp.concatenate([kn_ref[...], krt_ref[...]], axis=1).astype(BF16)
        vb_s[...] = v_ref[...].astype(BF16)
        dkn_ref[...] = jnp.zeros_like(dkn_ref)
        dv_ref[...] = jnp.zeros_like(dv_ref)
        dkrt_ref[...] = jnp.zeros_like(dkrt_ref)
        for i in range(s_dim // tq):
            rows, klen, causal = _att_tile(i, tq)
            qn_t, qr_t, o_t, do_t = qn_ref[rows, :], qr_ref[rows, :], o_ref[rows, :], do_ref[rows, :]
            heads = range(2)
            qcats = [_att_qcat(qn_t, qr_t, par, e0 + par, half_id, grp_id) for par in heads]
            scores = [_dot(qcats[par], kcat_s[0:klen, :], _NT) for par in heads]
            doms = [jnp.where(half_id == par, do_t, 0.0) for par in heads]
            dombs = [d.astype(BF16) for d in doms]
            d_ps = [_dot(dombs[par], vb_s[0:klen, :], _NT) for par in heads]
            probs = []
            for par in heads:
                e, inv_l = _att_softmax(scores[par], causal)
                probs.append(e * inv_l)
            d_ss = []
            for par in heads:
                d_row = jnp.sum(doms[par] * o_t, axis=1, keepdims=True)
                d_ss.append((probs[par] * (d_ps[par] - d_row)).astype(BF16))
            dqcats = [_dot(d_ss[par], kcat_s[0:klen, :]) * ATT_SCALE for par in heads]
            dkcats = [_dot(d_ss[par], qcats[par], _TN) for par in heads]
            dvs = [_dot(probs[par].astype(BF16), dombs[par], _TN) for par in heads]
            dqn_ref[rows, :] = jnp.where(half_id == 0, dqcats[0][:, :LANE], dqcats[1][:, :LANE]).astype(dqn_ref.dtype)
            dqr_ref[rows, :] = (jnp.where(grp_id == e0, dqcats[0][:, LANE:], 0.0)
                                + jnp.where(grp_id == e0 + 1, dqcats[1][:, LANE:], 0.0))
            dkn_ref[0:klen, :] += dkcats[0][:, :LANE] + dkcats[1][:, :LANE]
            dkrt_ref[0:klen, :] += dkcats[0][:, LANE:] + dkcats[1][:, LANE:]
            dv_ref[0:klen, :] += dvs[0] + dvs[1]

    col = lambda f: pl.BlockSpec((s_dim, LANE), lambda j: (0, f(j)))
    return pl.pallas_call(
        body, name=name, grid=(MLA_HEADS // 2,), in_specs=_att_specs(s_dim) + [col(lambda j: j), col(lambda j: j)],
        out_specs=[col(lambda j: j), pl.BlockSpec((None, s_dim, LANE), lambda j: (j % 2, 0, j // 2)), col(lambda j: j),
                   col(lambda j: j), pl.BlockSpec((None, s_dim, LANE), lambda j: (j, 0, 0))],
        out_shape=[jax.ShapeDtypeStruct((s_dim, 1024), BF16), jax.ShapeDtypeStruct((2, s_dim, 512), F32),
                   jax.ShapeDtypeStruct((s_dim, 1024), F32), jax.ShapeDtypeStruct((s_dim, 1024), F32),
                   jax.ShapeDtypeStruct((MLA_HEADS // 2, s_dim, LANE), F32)],
        scratch_shapes=[pltpu.VMEM((s_dim, 2 * LANE), BF16), pltpu.VMEM((s_dim, LANE), BF16)],
        compiler_params=pltpu.CompilerParams(dimension_semantics=("parallel",)),
    )(q, qr, kv, krt, kv, o, do)


def _gather_many(shards, name):
    n_arr = len(shards)

    def body(*refs):
        x_refs, out_refs = refs[:n_arr], refs[n_arr:2 * n_arr]
        send_sems, recv_sems, local_sems = refs[2 * n_arr:]
        x_i, y_i, c_i = lax.axis_index("x"), lax.axis_index("y"), lax.axis_index("c")
        me, sibling = (x_i, y_i, c_i), (x_i, y_i, 1 - c_i)
        chips = [(1 - x_i, y_i), (x_i, 1 - y_i), (1 - x_i, 1 - y_i)]

        def copy(a, k, block, to, src=None):
            slot = out_refs[a].at[4 * block[0] + 2 * block[1] + block[2]]
            return pltpu.make_async_remote_copy(
                src_ref=slot if src is None else src, dst_ref=slot, send_sem=send_sems.at[a, k],
                recv_sem=recv_sems.at[a, k], device_id=to, device_id_type=pl.DeviceIdType.MESH)

        mine, first, passed = [], [], []
        for a in range(n_arr):
            mine.append(pltpu.make_async_copy(x_refs[a], out_refs[a].at[4 * x_i + 2 * y_i + c_i], local_sems.at[a]))
            mine[a].start()
            first.append([copy(a, 0, me, sibling, src=x_refs[a])]
                         + [copy(a, 1 + j, me, (*chip, c_i), src=x_refs[a]) for j, chip in enumerate(chips)])
            for cp in first[a]:
                cp.start()
            passed.append([copy(a, 4 + j, (*chip, c_i), sibling) for j, chip in enumerate(chips)])
        for j, chip in enumerate(chips):
            for a in range(n_arr):
                copy(a, 1 + j, (*chip, c_i), me).wait_recv()
                passed[a][j].start()
        for a in range(n_arr):
            copy(a, 0, sibling, me).wait_recv()
            for j, chip in enumerate(chips):
                copy(a, 4 + j, (*chip, 1 - c_i), me).wait_recv()
        for a in range(n_arr):
            for cp in first[a] + passed[a]:
                cp.wait_send()
            mine[a].wait()

    any_spec = pl.BlockSpec(memory_space=pl.ANY)
    return pl.pallas_call(
        body, name=name, out_shape=[jax.ShapeDtypeStruct((N_DEV,) + x.shape, x.dtype) for x in shards],
        in_specs=[any_spec] * n_arr, out_specs=[any_spec] * n_arr,
        scratch_shapes=[pltpu.SemaphoreType.DMA((n_arr, 7)), pltpu.SemaphoreType.DMA((n_arr, 7)),
                        pltpu.SemaphoreType.DMA((n_arr,))],
    )(*shards)


_HBM = pl.BlockSpec(memory_space=pltpu.HBM)
_SEM = pl.BlockSpec(memory_space=pltpu.SEMAPHORE)


def _plan_copies(plan, src_refs, land_refs, send_sems, recv_sems):
    copies = []
    for s_ref, l_ref in zip(src_refs, land_refs):
        for src, dst, peer in plan(s_ref, l_ref):
            k = len(copies)
            copies.append(pltpu.make_async_remote_copy(
                src_ref=src, dst_ref=dst, send_sem=send_sems.at[k], recv_sem=recv_sems.at[k], device_id=peer,
                device_id_type=pl.DeviceIdType.MESH))
    return copies


def _split_start(srcs, lands, plan, n_copy, name, after=None):
    n = len(srcs)
    n_in = 2 * n + (after is not None)

    def body(*refs):
        for cp in _plan_copies(plan, refs[:n], refs[n:2 * n], refs[n_in], refs[n_in + 1]):
            cp.start()
        refs[-1][...] = jnp.zeros_like(refs[-1])

    sems = pltpu.SemaphoreType.DMA((n * n_copy,))
    res = pl.pallas_call(
        body, name=name,
        out_shape=(sems, sems, *[pltpu.HBM(a.shape, a.dtype) for a in list(srcs) + list(lands)],
                   jax.ShapeDtypeStruct((8, LANE), F32)),
        in_specs=[_HBM] * (2 * n) + [pl.BlockSpec(memory_space=pl.ANY)] * (after is not None),
        out_specs=(_SEM, _SEM, *[_HBM] * (2 * n), pl.BlockSpec(memory_space=pltpu.VMEM)),
        input_output_aliases={i: 2 + i for i in range(2 * n)},
        compiler_params=pltpu.CompilerParams(has_side_effects=pltpu.SideEffectType.DATAFLOW_SIDE_EFFECTING),
    )(*[pltpu.with_memory_space_constraint(a, pltpu.HBM) for a in list(srcs) + list(lands)],
      *([after] if after is not None else []))
    return res[0], res[1], list(res[2:2 + n]), list(res[2 + n:2 + 2 * n]), res[-1]


def _split_wait(send_sems, recv_sems, srcs, lands, after, plan, name):
    n = len(srcs)

    def body(*refs):
        copies = _plan_copies(plan, refs[:n], refs[n:2 * n], refs[2 * n], refs[2 * n + 1])
        for cp in copies:
            cp.wait_send()
        for cp in copies:
            cp.wait_recv()

    res = pl.pallas_call(
        body, name=name, out_shape=tuple(pltpu.HBM(a.shape, a.dtype) for a in list(srcs) + list(lands)),
        in_specs=[_HBM] * (2 * n) + [_SEM, _SEM, pl.BlockSpec(memory_space=pl.ANY)], out_specs=tuple([_HBM] * (2 * n)),
        input_output_aliases={i: i for i in range(2 * n)},
        compiler_params=pltpu.CompilerParams(has_side_effects=pltpu.SideEffectType.DATAFLOW_SIDE_EFFECTING),
    )(*srcs, *lands, send_sems, recv_sems, after)
    return list(res[:n]), list(res[n:])


def _plan_broadcast(src, land):
    x_i, y_i, c_i = lax.axis_index("x"), lax.axis_index("y"), lax.axis_index("c")
    me = 4 * x_i + 2 * y_i + c_i
    return [(src, land.at[me], (x_i ^ (k >> 2), y_i ^ ((k >> 1) & 1), c_i ^ (k & 1))) for k in range(1, N_DEV)]


def _plan_scatter(src, land):
    x_i, y_i, c_i = lax.axis_index("x"), lax.axis_index("y"), lax.axis_index("c")
    me = 4 * x_i + 2 * y_i + c_i
    plan = []
    for k in range(1, N_DEV):
        px, py, pc = x_i ^ (k >> 2), y_i ^ ((k >> 1) & 1), c_i ^ (k & 1)
        plan.append((src.at[4 * px + 2 * py + pc], land.at[me], (px, py, pc)))
    return plan


def _adam_math(g, w, m, v):
    m_new = ADAM_B1 * m + (1.0 - ADAM_B1) * g
    v_new = ADAM_B2 * v + (1.0 - ADAM_B2) * (g * g)
    m_hat = m_new / (1.0 - ADAM_B1 ** ADAM_STEP)
    v_hat = v_new / (1.0 - ADAM_B2 ** ADAM_STEP)
    return -ADAM_LR * (m_hat / (jnp.sqrt(v_hat) + ADAM_EPS) + ADAM_WD * w), m_new, v_new


def _adam(slots, w, m, v, name, own=None, own_idx=None):
    n_slot, rows, cols = slots.shape
    tr = ROW_TILE if rows % ROW_TILE == 0 else rows
    has_own = own is not None

    def body(*refs):
        if has_own:
            idx_ref, own_ref, refs = refs[0], refs[1], refs[2:]
        s_ref, w_ref, m_ref, v_ref, g_ref, d_ref, mo_ref, vo_ref = refs
        g = own_ref[...].astype(F32) if has_own else s_ref[0].astype(F32)
        for k in range(0 if has_own else 1, n_slot):
            part = s_ref[k].astype(F32)
            g = g + (jnp.where(idx_ref[0] == k, 0.0, part) if has_own else part)
        g_ref[...] = g
        d_ref[...], mo_ref[...], vo_ref[...] = _adam_math(g, w_ref[...], m_ref[...], v_ref[...])

    spec = pl.BlockSpec((tr, cols), lambda i, *_: (i, 0))
    in_specs = [pl.BlockSpec((n_slot, tr, cols), lambda i, *_: (0, i, 0)), spec, spec, spec]
    if has_own:
        in_specs = [pl.BlockSpec((None, tr, cols), lambda i, idx: (idx[0], i, 0))] + in_specs
    grid_spec = pltpu.PrefetchScalarGridSpec(num_scalar_prefetch=1 if has_own else 0, grid=(rows // tr,), in_specs=in_specs,
                                             out_specs=[spec] * 4)
    ins = ([own_idx, own] if has_own else []) + [slots, w, m, v]
    return pl.pallas_call(
        body, name=name, grid_spec=grid_spec, out_shape=[jax.ShapeDtypeStruct((rows, cols), F32)] * 4,
        compiler_params=pltpu.CompilerParams(dimension_semantics=("parallel",)),
    )(*ins)


PACK_ROWS, PACK_W = 24, 1536
REPL_W = (("ssd_conv_b", 1536), ("ssd_dt_bias", 16), ("ssd_A_log", 16), ("ssd_D", 16), ("ssd_norm_w", 1024),
          ("mla_q_norm_w", 384), ("mla_kv_norm_w", 256), ("mla_out_norm_w", 1024), ("ln_mix_g", 1024),
          ("ln_mix_b", 1024), ("ln_ffn_g", 1024), ("ln_ffn_b", 1024))
LOSS_ROW = 4 + len(REPL_W)


def _pack_small(conv_w_grad, grads, loss, name="pack_small"):
    def body(*refs):
        cw_ref, g_refs, loss_ref, o_ref = refs[0], refs[1:1 + len(REPL_W)], refs[1 + len(REPL_W)], refs[-1]
        o_ref[...] = jnp.zeros_like(o_ref)
        o_ref[0:4, :] = cw_ref[...]
        for i, g_ref in enumerate(g_refs):
            o_ref[4 + i:5 + i, 0:g_ref.shape[1]] = g_ref[...]
        o_ref[LOSS_ROW:LOSS_ROW + 1, 0:LANE] = loss_ref[...]

    return pl.pallas_call(body, name=name, out_shape=jax.ShapeDtypeStruct((PACK_ROWS, PACK_W), F32))(conv_w_grad, *grads, loss)


def _adam_small(gathered, wmv, name="adam_small"):
    def body(*refs):
        s_ref = refs[0]
        in_refs = refs[1:1 + 3 * len(REPL_W)]
        cw_ref, loss_ref = refs[1 + 3 * len(REPL_W)], refs[2 + 3 * len(REPL_W)]
        out_refs = refs[3 + 3 * len(REPL_W):-1]
        tot = refs[-1]
        acc = s_ref[0]
        for k in range(1, N_DEV):
            acc = acc + s_ref[k]
        tot[...] = acc
        cw_ref[...] = tot[0:4, :]
        loss_ref[...] = tot[LOSS_ROW:LOSS_ROW + 1, 0:LANE]
        for i, (_, width) in enumerate(REPL_W):
            g = tot[4 + i:5 + i, 0:width]
            w_ref, m_ref, v_ref = in_refs[3 * i:3 * i + 3]
            g_ref, d_ref, mo_ref, vo_ref = out_refs[4 * i:4 * i + 4]
            g_ref[...] = g
            d_ref[...], mo_ref[...], vo_ref[...] = _adam_math(g, w_ref[...], m_ref[...], v_ref[...])

    flat_in = [a for triple in wmv for a in triple]
    out_shape = [jax.ShapeDtypeStruct((4, PACK_W), F32), jax.ShapeDtypeStruct((1, LANE), F32)]
    for _, width in REPL_W:
        out_shape += [jax.ShapeDtypeStruct((1, width), F32)] * 4
    res = pl.pallas_call(body, name=name, out_shape=out_shape, scratch_shapes=[pltpu.VMEM((PACK_ROWS, PACK_W), F32)])(
        gathered, *flat_in)
    return res[0], res[1], [res[2 + 4 * i:6 + 4 * i] for i in range(len(REPL_W))]


def _cols_full(g):
    return jnp.transpose(g, (1, 0, 2)).reshape(g.shape[1], -1)


def _cols_split(full):
    k_dim, n_dim = full.shape
    return jnp.transpose(full.reshape(k_dim, N_DEV, n_dim // N_DEV), (1, 0, 2))


PROJ_BLOCK = {"z": (1024, 0), "dt": (LANE, 8), "q_c": (MLA_Q_RANK, 3), "xbc": (SSD_XBC, 1), "kv_c": (MLA_KV_RANK, 12),
              "k_rope": (LANE, 26)}


def _win_pad(wt):
    z = lambda n: jnp.zeros((n, wt.shape[1]), wt.dtype)
    return jnp.concatenate([wt[:1024], wt[2560:2576], z(112), wt[2576:2960], wt[1024:2560], wt[2960:3216], wt[3216:3248],
                            z(96)], axis=0)


def _win_unpad(wt):
    return jnp.concatenate([wt[:1024], wt[1536:3072], wt[1024:1040], wt[1152:1536], wt[3072:3328], wt[3328:3360]], axis=0)


def _heads_split_t(wt, a, b):
    w3 = wt.reshape(MLA_HEADS, a + b, wt.shape[1])
    return jnp.concatenate([w3[:, :a].reshape(-1, wt.shape[1]), w3[:, a:].reshape(-1, wt.shape[1])], axis=0)


def _heads_merge_t(wt, a, b):
    wa = wt[:MLA_HEADS * a].reshape(MLA_HEADS, a, wt.shape[1])
    wb = wt[MLA_HEADS * a:].reshape(MLA_HEADS, b, wt.shape[1])
    return jnp.concatenate([wa, wb], axis=1).reshape(-1, wt.shape[1])


def _heads_split(w, a, b):
    k_dim = w.shape[0]
    w3 = w.reshape(k_dim, MLA_HEADS, a + b)
    return jnp.concatenate([w3[:, :, :a].reshape(k_dim, -1), w3[:, :, a:].reshape(k_dim, -1)], axis=1)


def _heads_merge(w, a, b):
    k_dim = w.shape[0]
    wa = w[:, :MLA_HEADS * a].reshape(k_dim, MLA_HEADS, a)
    wb = w[:, MLA_HEADS * a:].reshape(k_dim, MLA_HEADS, b)
    return jnp.concatenate([wa, wb], axis=2).reshape(k_dim, -1)


def _pad_lanes(v, width=LANE):
    return jnp.concatenate([v, jnp.zeros((v.shape[0], width - v.shape[1]), v.dtype)], axis=1)


def _local_step(x, p, positions, tgt, W, P, comm=None):
    comm = comm or {}
    zero_tok = jnp.zeros((8, LANE), F32)
    s_dim = x.shape[0]
    inv_freq = 1.0 / (ROPE_BASE ** (jnp.arange(0, MLA_ROPE, 2, dtype=F32) / MLA_ROPE))
    ang = positions.astype(F32)[:, None] * inv_freq
    cos, sin = jnp.cos(ang), jnp.sin(ang)
    cos32 = jnp.concatenate([cos, cos], axis=1)
    sin32 = jnp.concatenate([-sin, sin], axis=1)
    cos512, sin512 = jnp.tile(cos32, (1, 16)), jnp.tile(sin32, (1, 16))
    cos128, sin128 = jnp.tile(cos32, (1, 4)), jnp.tile(sin32, (1, 4))
    bias_p, alog_p = _pad_lanes(P["ssd_dt_bias"]), _pad_lanes(P["ssd_A_log"])
    d_x = jnp.repeat(P["ssd_D"], SSD_HEAD_DIM, axis=1)

    xb, pb = x.astype(BF16), p.astype(BF16)
    proj = _mm(xb, W["w_in"], tb=True, after=comm.get("token0", zero_tok), name="mm_in")
    z, qc, kvc, kr = [(proj,) + PROJ_BLOCK[n] for n in ("z", "q_c", "kv_c", "k_rope")]
    xbca = _conv_fwd(proj, PROJ_BLOCK["xbc"][1], P["ssd_conv_w"], P["ssd_conv_b"])
    y, states = _ssd_fwd(xbca, proj, PROJ_BLOCK["dt"][1], bias_p, alog_p, d_x)
    (yssd,) = _rowwise(_gate_rms, [y, z], [P["ssd_norm_w"]], [(1024, BF16)], name="ssd_gate_norm")
    qn, kvn, krt = _rowwise(lambda a, c, u, cs, sn, wq, wkv: (_rms(a, wq), _rms(c, wkv), _spread4(_rope_fwd_fn(u, cs, sn))),
                            [qc, kvc, kr, cos128, sin128], [P["mla_q_norm_w"], P["mla_kv_norm_w"]],
                            [(MLA_Q_RANK, BF16), (MLA_KV_RANK, BF16), LANE], name="qkv_norm_rope_k")
    q = _mm(qn, W["mla_w_q_b"], tb=True, name="mm_q")
    kv = _mm(kvn, W["mla_w_kv_b"], name="mm_kv")
    (qr,) = _rowwise(_rope_fwd_fn, [(q, 512, 2), cos512, sin512], [], [512], name="rope_q")
    att = _att_fwd(q, qr, kv, krt)
    (ymla,) = _rowwise(_rms, [att], [P["mla_out_norm_w"]], [(1024, BF16)], name="out_norm")
    ycat = jnp.concatenate([yssd, ymla], axis=1)
    if "late_weights" in comm:
        W = {**W, **comm["late_weights"]("out", ycat)}
    mix = _mm(ycat, W["w_out"], name="mm_out")
    f_h1 = lambda xv, mv, g, b: _ln(ALPHA * xv + mv, g, b)
    h1, h1b = _rowwise(lambda *a: (f_h1(*a),) * 2, [x, mix], [P["ln_mix_g"], P["ln_mix_b"]], [1024, (1024, BF16)],
                       name="ln_mix")
    if "late_weights" in comm:
        W = {**W, **comm["late_weights"]("ffn", h1b)}
    hg = _mm(h1b, W["w_ffn_gate"], tb=True, out_dtype=BF16, name="mm_gate")
    hu, act = _mm(h1b, W["w_ffn_up"], tb=True, name="mm_up",
                  epilogue=(lambda u, g: (u, _silu(g.astype(F32)) * u), [hg], [BF16, BF16]))
    pg = _mm(h1b, W["w_ple_gate"], name="mm_ple_gate")
    pp = _mm(pb, W["w_ple_proj"], name="mm_ple")
    ffn = _mm(act, W["w_ffn_down"], name="mm_down")

    def final_fn(hv, fv, pg, ppv, tv, g, b):
        sg = _sigmoid(pg)
        n, r = _ln_parts(ALPHA * hv + fv + sg * ppv)
        diff = n * g + b - tv
        loss = 0.5 * jnp.sum(jnp.mean(diff * diff, axis=-1, keepdims=True), axis=0, keepdims=True)
        d_pre, d_g, d_b = _ln_bwd(n, r, g, diff * (1.0 / D_MODEL))
        return (ALPHA * d_pre, d_pre, d_pre * ppv * (sg * (1.0 - sg)), d_pre * sg, d_g, d_b,
                jnp.broadcast_to(loss, (1, LANE)))

    dh1_a, dffn, dpg, dpp, g_ffn_g, g_ffn_b, loss = _rowwise(
        final_fn, [h1, ffn, pg, pp, tgt], [P["ln_ffn_g"], P["ln_ffn_b"]], [1024] + [(1024, BF16)] * 3,
        [1024, 1024, LANE], name="final")

    G = {}
    def swiglu_bwd(d, g, u):
        g, u = g.astype(F32), u.astype(F32)
        sg = _sigmoid(g)
        return d * u * (sg * (1.0 + g * (1.0 - sg))), d * (g * sg)

    dg, du = _mm(dffn, W["w_ffn_down"], tb=True, name="mm_down_dx",
                 epilogue=(swiglu_bwd, [hg, hu], [BF16, BF16]))
    G["w_ffn_down"] = _mm(act, dffn, ta=True, out_dtype=GRAD_DT, name="mm_down_dw")
    dh1 = _mm(dg, W["w_ffn_gate"], add=dh1_a, name="mm_gate_dx")
    dh1 = _mm(du, W["w_ffn_up"], add=dh1, name="mm_up_dx")
    dh1 = _mm(dpg, W["w_ple_gate"], tb=True, add=dh1, name="mm_ple_gate_dx")
    G["w_ffn_gate"] = _mm(dg, h1b, ta=True, out_dtype=GRAD_DT, name="mm_gate_dw")
    G["w_ffn_up"] = _mm(du, h1b, ta=True, out_dtype=GRAD_DT, name="mm_up_dw")
    G["w_ple_gate"] = _mm(h1b, dpg, ta=True, out_dtype=GRAD_DT, name="mm_ple_gate_dw")
    G["w_ple_proj"] = _mm(pb, dpp, ta=True, out_dtype=GRAD_DT, name="mm_ple_dw")
    def ln_mix_bwd(xv, mv, dv, g, b):
        n, r = _ln_parts(ALPHA * xv + mv)
        d_pre, d_g, d_b = _ln_bwd(n, r, g, dv)
        return ALPHA * d_pre, d_pre, d_g, d_b

    dx_a, dmix, g_mix_g, g_mix_b = _rowwise(ln_mix_bwd, [x, mix, dh1], [P["ln_mix_g"], P["ln_mix_b"]],
                                            [1024, (1024, BF16)], [1024, 1024], name="ln_mix_bwd")
    dycat = _mm(dmix, W["w_out"], tb=True, name="mm_out_dx")
    G["w_out"] = _mm(ycat, dmix, ta=True, out_dtype=GRAD_DT, name="mm_out_dw")

    grads_done = comm.get("grads", lambda group, grads: zero_tok)
    tok1 = grads_done("ffn", G)
    datt, g_out_norm = _rowwise(lambda a, dv, w, t: _rms_bwd(a, w, dv + jnp.min(t)), [att, (dycat, 1024, 1)],
                                [P["mla_out_norm_w"], tok1], [1024], [1024], name="out_norm_bwd")
    dqn_nope, dqr, dkn, dv, dkrt = _att_bwd(q, qr, kv, krt, att, datt)
    dkv = jnp.concatenate([dkn, dv], axis=1)
    (dq_rope,) = _rowwise(lambda d0, d1, c, s: _rope_bwd_fn(d0 + d1, c, s), [(dqr, 512, 0), (dqr, 512, 1), cos512, sin512],
                          [], [(512, BF16)], name="rope_q_bwd")

    def rope_k_bwd(*a):
        d = _spread4(functools.reduce(lambda u, w: u + w, a[:-2]))
        lane = lax.broadcasted_iota(jnp.int32, d.shape, 1)
        return _rope_bwd_fn(jnp.where(lane < MLA_ROPE, d, 0.0), a[-2], a[-1])

    (dkr,) = _rowwise(rope_k_bwd, [(dkrt, LANE, k) for k in range(MLA_HEADS // 2)] + [cos128, sin128], [], [(LANE, BF16)],
                      name="rope_k_bwd")
    dq = jnp.concatenate([dqn_nope, dq_rope], axis=1)
    dqn = _mm(dq, W["mla_w_q_b"], name="mm_q_dx")
    G["mla_w_q_b"] = _mm(dq, qn, ta=True, out_dtype=GRAD_DT, name="mm_q_dw")
    dkvn = _mm(dkv, W["mla_w_kv_b"], tb=True, name="mm_kv_dx")
    G["mla_w_kv_b"] = _mm(kvn, dkv, ta=True, out_dtype=GRAD_DT, name="mm_kv_dw")
    tok2 = grads_done("mla", G)
    def qkv_norm_bwd(a, da, c, dc, wq, wkv, t):
        (d_a, d_wq), (d_c, d_wkv) = _rms_bwd(a, wq, da + jnp.min(t)), _rms_bwd(c, wkv, dc)
        return d_a, d_c, d_wq, d_wkv

    dqc, dkvc, g_q_norm, g_kv_norm = _rowwise(
        qkv_norm_bwd, [qc, dqn, kvc, dkvn], [P["mla_q_norm_w"], P["mla_kv_norm_w"], tok2],
        [(MLA_Q_RANK, BF16), (MLA_KV_RANK, BF16)], [MLA_Q_RANK, MLA_KV_RANK], name="qkv_norm_bwd")

    def gate_rms_bwd(yv, zv, dv, w, t):
        sg = _sigmoid(zv)
        silu = zv * sg
        gated = yv * silu
        r = lax.rsqrt(jnp.mean(gated * gated, axis=-1, keepdims=True) + EPS)
        n = gated * r
        dv = dv + jnp.min(t)
        g = dv * w
        d_gated = r * (g - n * jnp.mean(g * n, axis=-1, keepdims=True))
        return d_gated * silu, d_gated * yv * (sg * (1.0 + zv * (1.0 - sg))), _colsum(dv * n)

    dy, dz, g_ssd_norm = _rowwise(gate_rms_bwd, [y, z, (dycat, 1024, 0)], [P["ssd_norm_w"], tok1], [1024, (1024, BF16)],
                                  [1024], name="ssd_gate_norm_bwd")
    dxbca, ddtr, g_dt_bias, g_alog, g_d = _ssd_bwd(xbca, proj, PROJ_BLOCK["dt"][1], bias_p, alog_p, d_x, states, dy)
    da, g_conv_w, g_conv_b = _conv_bwd_pre(proj, PROJ_BLOCK["xbc"][1], P["ssd_conv_w"], P["ssd_conv_b"], dxbca)
    dxbc = _conv_bwd_in(da, P["ssd_conv_w"])

    small = {
        "ssd_conv_b": g_conv_b, "ssd_dt_bias": g_dt_bias, "ssd_A_log": g_alog, "ssd_D": g_d, "ssd_norm_w": g_ssd_norm,
        "mla_q_norm_w": g_q_norm, "mla_kv_norm_w": g_kv_norm, "mla_out_norm_w": g_out_norm, "ln_mix_g": g_mix_g,
        "ln_mix_b": g_mix_b, "ln_ffn_g": g_ffn_g, "ln_ffn_b": g_ffn_b,
    }
    packed = _pack_small(g_conv_w, [small[n] for n, _ in REPL_W], loss)
    if "small" in comm:
        comm["small"](packed)

    dproj = jnp.concatenate([dz, ddtr, dqc, dxbc, dkvc, dkr], axis=1)
    G["w_in"] = _mm(dproj, xb, ta=True, out_dtype=GRAD_DT, name="mm_in_dw")
    grad_x = _mm(dproj, W["w_in"], add=dx_a, after=grads_done("in", G), name="mm_in_dx")
    return grad_x, G, packed


def kernel(x, p, positions, w_in, ssd_conv_w, ssd_conv_b, ssd_dt_bias, ssd_A_log, ssd_D, ssd_norm_w, mla_q_norm_w, mla_w_q_b, mla_kv_norm_w, mla_w_kv_b, mla_out_norm_w, w_out, ln_mix_g, ln_mix_b, w_ffn_gate, w_ffn_up, w_ffn_down, w_ple_gate, w_ple_proj, ln_ffn_g, ln_ffn_b, loss_target, m_w_in, m_ssd_conv_w, m_ssd_conv_b, m_ssd_dt_bias, m_ssd_A_log, m_ssd_D, m_ssd_norm_w, m_mla_q_norm_w, m_mla_w_q_b, m_mla_kv_norm_w, m_mla_w_kv_b, m_mla_out_norm_w, m_w_out, m_ln_mix_g, m_ln_mix_b, m_w_ffn_gate, m_w_ffn_up, m_w_ffn_down, m_w_ple_gate, m_w_ple_proj, m_ln_ffn_g, m_ln_ffn_b, v_w_in, v_ssd_conv_w, v_ssd_conv_b, v_ssd_dt_bias, v_ssd_A_log, v_ssd_D, v_ssd_norm_w, v_mla_q_norm_w, v_mla_w_q_b, v_mla_kv_norm_w, v_mla_w_kv_b, v_mla_out_norm_w, v_w_out, v_ln_mix_g, v_ln_mix_b, v_w_ffn_gate, v_w_ffn_up, v_w_ffn_down, v_w_ple_gate, v_w_ple_proj, v_ln_ffn_g, v_ln_ffn_b):
    args = dict(locals())
    core = lax.axis_index("c")
    me = 4 * lax.axis_index("x") + 2 * lax.axis_index("y") + core

    conv_sh = ssd_conv_w[0]
    conv_hi = conv_sh.astype(BF16)
    conv_lo = (conv_sh - conv_hi.astype(F32)).astype(BF16)
    stored = lambda n, pre="": jnp.transpose(args[pre + n][0]) if n in TRANSPOSED else args[pre + n][0]
    shards = {n: stored(n).astype(BF16) for n in BIG}
    rows_full = lambda g: g.reshape(-1, g.shape[2])

    early = _gather_many([shards[n] for n in EARLY] + [jnp.concatenate([conv_hi, conv_lo], axis=0)], "gather_early")
    gw = dict(zip(EARLY, early[:-1]))
    conv_g = early[-1].astype(F32)
    W = {
        "w_in": _win_pad(rows_full(gw["w_in"])),
        "mla_w_q_b": _heads_split_t(rows_full(gw["mla_w_q_b"]), MLA_NOPE, MLA_ROPE),
        "mla_w_kv_b": _heads_split(_cols_full(gw["mla_w_kv_b"]), MLA_NOPE, MLA_V),
    }
    P = {n: args[n] for n, _ in REPL_W}
    P["ssd_conv_w"] = _cols_full(conv_g[:, :4] + conv_g[:, 4:])

    late, after = {}, early[0]
    for group, names in LATE.items():
        lands = [lax.dynamic_update_slice(lax.empty((N_DEV,) + shards[n].shape, BF16), shards[n][None], (me, 0, 0)) for n in names]
        late[group] = _split_start([shards[n] for n in names], lands, _plan_broadcast, N_DEV - 1,
                                   "gather_" + group + "_start", after=after)
        after = late[group][4]

    def late_weights(group, after):
        _, got = _split_wait(*late[group][:4], after, _plan_broadcast, "gather_" + group + "_wait")
        return {n: _cols_full(g) if n == "w_ple_proj" else rows_full(g) for n, g in zip(LATE[group], got)}

    def to_blocks(n, g):
        if n == "w_in":
            g = _win_unpad(g)
        elif n == "mla_w_q_b":
            g = _heads_merge_t(g, MLA_NOPE, MLA_ROPE)
        elif n == "mla_w_kv_b":
            g = _heads_merge(g, MLA_NOPE, MLA_V)
        if n in ROW_SHARDED or n in TRANSPOSED:
            return g.reshape(N_DEV, -1, g.shape[1])
        return _cols_split(g)

    flight = {}

    def grads(group, G):
        gl = [to_blocks(n, G[n]) for n in GRAD_GROUPS[group]]
        flight[group] = _split_start(gl, [lax.empty(g.shape, g.dtype) for g in gl], _plan_scatter, N_DEV - 1,
                                     "grads_" + group + "_start", after=flight["small"][4] if group == "in" else None)
        return flight[group][4]

    def small(packed):
        land = lax.dynamic_update_slice(lax.empty((N_DEV,) + packed.shape, F32), packed[None], (me, 0, 0))
        flight["small"] = _split_start([packed], [land], _plan_broadcast, N_DEV - 1, "small_start")

    grad_x, G, packed = _local_step(x[0], p[0, 0], positions[0], loss_target[0], W, P,
                                    comm={"token0": after, "late_weights": late_weights, "grads": grads, "small": small})

    me_arr = me.astype(jnp.int32).reshape(1)
    big_out = {}

    def finish(group, after):
        mine, recv = _split_wait(*flight[group][:4], after, _plan_scatter, "grads_" + group + "_wait")
        for n, g, r in zip(GRAD_GROUPS[group], mine, recv):
            big_out[n] = _adam(r, stored(n), stored(n, "m_"), stored(n, "v_"), "adam_" + n, own=g, own_idx=me_arr)
        return big_out[GRAD_GROUPS[group][-1]][0]

    done = finish("ffn", grad_x)
    _, (small_all,) = _split_wait(*flight["small"][:4], done, _plan_broadcast, "small_wait")
    conv_sum, loss_row, small_out = _adam_small(small_all, [(args[n], args["m_" + n], args["v_" + n]) for n, _ in REPL_W])
    finish("in", finish("mla", done))
    conv_grad = lax.dynamic_slice_in_dim(conv_sum, me * 192, 192, axis=1)
    conv_out = _adam(conv_grad[None], conv_sh, m_ssd_conv_w[0], v_ssd_conv_w[0], "adam_conv")
    small_map = {n: small_out[i] for i, (n, _) in enumerate(REPL_W)}

    def outputs(idx):
        res = []
        for n in WEIGHT_ORDER:
            if n == "ssd_conv_w":
                res.append(conv_out[idx][None])
            elif n in big_out:
                res.append((jnp.transpose(big_out[n][idx]) if n in TRANSPOSED else big_out[n][idx])[None])
            else:
                res.append(small_map[n][idx])
        return res

    return (loss_row[0, 0], grad_x[None], *outputs(0), *outputs(1), *outputs(2), *outputs(3))
```

```python
import functools
import math

import numpy as np
import jax
import jax.numpy as jnp
from jax import lax
from jax.experimental import pallas as pl
from jax.experimental.pallas import tpu as pltpu

F32 = jnp.float32
BF16 = jnp.bfloat16

N_DEV = 8
D_MODEL = 1024
PLE_DIM = 256
SSD_HEADS = 16
SSD_HEAD_DIM = 64
SSD_INNER = 1024
SSD_STATE = 128
SSD_XBC = 1536
SSD_CHUNK = 128
MLA_HEADS = 16
MLA_Q_RANK = 384
MLA_KV_RANK = 256
MLA_NOPE = 64
MLA_ROPE = 32
MLA_V = 64
ROPE_BASE = 10000.0
D_FF = 2816
IN_WIDTH = 3248
IN_PAD = 3456
ALPHA = 2.0 ** 0.25
EPS = 1e-6
LN_EPS = 1e-5
ATT_SCALE = 1.0 / math.sqrt(MLA_NOPE + MLA_ROPE)
ADAM_LR, ADAM_B1, ADAM_B2, ADAM_EPS, ADAM_WD, ADAM_STEP = 0.001, 0.9, 0.999, 1e-08, 0.01, 10

LANE = 128
MXU_DIM = 256
MM_TM, MM_TN, MM_TK = 1408, 1408, 3456
MM_TM_DEEP_K, MM_TM_DEEP = 2048, 512
ROW_TILE = 512
ATT_TQ = 256

GRAD_DT = BF16

BIG = ("w_in", "mla_w_q_b", "mla_w_kv_b", "w_out", "w_ffn_gate", "w_ffn_up", "w_ffn_down", "w_ple_gate", "w_ple_proj")
EARLY = ("w_in", "mla_w_q_b", "mla_w_kv_b")
LATE = {"out": ("w_out", "w_ple_gate", "w_ple_proj"), "ffn": ("w_ffn_gate", "w_ffn_up", "w_ffn_down")}
GRAD_GROUPS = {"ffn": ("w_ffn_gate", "w_ffn_up", "w_ffn_down", "w_ple_gate", "w_ple_proj", "w_out"),
               "mla": ("mla_w_q_b", "mla_w_kv_b"), "in": ("w_in",)}
ROW_SHARDED = ("w_out", "w_ffn_down", "w_ple_gate")
TRANSPOSED = ("w_in", "mla_w_q_b", "w_ffn_gate", "w_ffn_up")
WEIGHT_ORDER = ("w_in", "ssd_conv_w", "ssd_conv_b", "ssd_dt_bias", "ssd_A_log", "ssd_D", "ssd_norm_w", "mla_q_norm_w",
                "mla_w_q_b", "mla_kv_norm_w", "mla_w_kv_b", "mla_out_norm_w", "w_out", "ln_mix_g", "ln_mix_b",
                "w_ffn_gate", "w_ffn_up", "w_ffn_down", "w_ple_gate", "w_ple_proj", "ln_ffn_g", "ln_ffn_b")


def _tile(dim, cap, prefer=None):
    cands = [t for t in range(LANE, min(cap, dim) + 1, LANE) if dim % t == 0]
    if not cands:
        return dim
    if prefer is None:
        return max(cands)
    fill = lambda t: t / (MXU_DIM * -(-t // MXU_DIM))
    good = min(0.9, max(fill(t) for t in cands))
    return min((t for t in cands if fill(t) >= good), key=lambda t: abs(t - prefer))


def _dot(a, b, dims=(((1,), (0,)), ((), ())), precision=None):
    return lax.dot_general(a, b, dims, preferred_element_type=F32, precision=precision)


_NT = (((1,), (1,)), ((), ()))
_TN = (((0,), (0,)), ((), ()))


def _mm(a, b, *, ta=False, tb=False, add=None, out_dtype=F32, after=None, epilogue=None, name):
    k_dim, m_dim = a.shape if ta else a.shape[::-1]
    n_dim, kb = b.shape if tb else b.shape[::-1]
    assert k_dim == kb
    tn, tk = _tile(n_dim, MM_TN, prefer=1024), _tile(k_dim, MM_TK, prefer=MM_TK)
    tm = _tile(m_dim, MM_TM if tk <= MM_TM_DEEP_K else MM_TM_DEEP)
    nk = k_dim // tk
    dims = (((0 if ta else 1,), (1 if tb else 0,)), ((), ()))
    a_spec = pl.BlockSpec((tk, tm), lambda i, j, k: (k, i)) if ta else pl.BlockSpec((tm, tk), lambda i, j, k: (i, k))
    b_spec = pl.BlockSpec((tn, tk), lambda i, j, k: (j, k)) if tb else pl.BlockSpec((tk, tn), lambda i, j, k: (k, j))
    o_spec = pl.BlockSpec((tm, tn), lambda i, j, k: (i, j))
    epi_fn, epi_in, out_dtypes = epilogue if epilogue else (None, [], [out_dtype])
    tiles = ([add] if add is not None else []) + list(epi_in)
    n_out = len(out_dtypes)

    def body(*refs):
        a_ref, b_ref = refs[:2]
        tile_refs = refs[2:2 + len(tiles)]
        out_refs = refs[len(refs) - n_out - (nk > 1):len(refs) - (nk > 1)]
        part = _dot(a_ref[...].astype(BF16), b_ref[...].astype(BF16), dims)
        if add is not None:
            part_add = lambda v: v + tile_refs[0][...]
        else:
            part_add = lambda v: v

        def write(total):
            extra = [r[...] for r in tile_refs[add is not None:]]
            outs = epi_fn(total, *extra) if epi_fn else (total,)
            for o_ref, val in zip(out_refs, outs):
                o_ref[...] = val.astype(o_ref.dtype)

        if nk == 1:
            write(part_add(part))
            return
        acc = refs[-1]
        k = pl.program_id(2)

        @pl.when(k == 0)
        def _():
            acc[...] = part_add(part)

        @pl.when(k > 0)
        def _():
            acc[...] += part

        @pl.when(k == nk - 1)
        def _():
            write(acc[...])

    ins = [a, b] + tiles + ([after] if after is not None else [])
    specs = [a_spec, b_spec] + [o_spec] * len(tiles) + ([pl.BlockSpec(memory_space=pl.ANY)] if after is not None else [])
    res = pl.pallas_call(
        body, name=name, grid=(m_dim // tm, n_dim // tn, nk), in_specs=specs, out_specs=[o_spec] * n_out,
        out_shape=[jax.ShapeDtypeStruct((m_dim, n_dim), dt) for dt in out_dtypes],
        scratch_shapes=[pltpu.VMEM((tm, tn), F32)] if nk > 1 else [],
        compiler_params=pltpu.CompilerParams(dimension_semantics=("parallel", "parallel", "arbitrary")),
    )(*ins)
    return res if epilogue else res[0]


def _rowwise(fn, rows, consts, out_widths, acc_widths=(), *, name, tr=ROW_TILE):
    row_arrays, row_specs = [], []
    first_arr = rows[0][0] if isinstance(rows[0], tuple) else rows[0]
    s_dim = first_arr.shape[-2]
    tr = min(tr, s_dim)
    for r in rows:
        arr, width, cb = r if isinstance(r, tuple) else (r, r.shape[-1], 0)
        row_arrays.append(arr)
        if arr.ndim == 3:
            row_specs.append(pl.BlockSpec((None, tr, width), functools.partial(lambda i, k: (k, i, 0), k=cb)))
        else:
            row_specs.append(pl.BlockSpec((tr, width), functools.partial(lambda i, cb: (i, cb), cb=cb)))
    const_specs = [pl.BlockSpec(c.shape, lambda i: (0, 0)) for c in consts]
    nr, nc, no, na = len(rows), len(consts), len(out_widths), len(acc_widths)

    def body(*refs):
        ins = [r[...] for r in refs[:nr + nc]]
        res = fn(*ins)
        if not isinstance(res, (tuple, list)):
            res = (res,)
        out_refs = refs[nr + nc:nr + nc + no]
        acc_refs = refs[nr + nc + no:]
        for o_ref, val in zip(out_refs, res[:no]):
            o_ref[...] = val.astype(o_ref.dtype)
        first = pl.program_id(0) == 0
        for a_ref, val in zip(acc_refs, res[no:]):
            @pl.when(first)
            def _(a_ref=a_ref, val=val):
                a_ref[...] = val

            @pl.when(jnp.logical_not(first))
            def _(a_ref=a_ref, val=val):
                a_ref[...] += val

    outs = [w if isinstance(w, tuple) else (w, F32) for w in out_widths]
    out_shape = [jax.ShapeDtypeStruct((s_dim, w), dt) for w, dt in outs]
    out_shape += [jax.ShapeDtypeStruct((1, w), F32) for w in acc_widths]
    out_specs = [pl.BlockSpec((tr, w), lambda i: (i, 0)) for w, _ in outs]
    out_specs += [pl.BlockSpec((1, w), lambda i: (0, 0)) for w in acc_widths]
    res = pl.pallas_call(
        body, name=name, grid=(s_dim // tr,), in_specs=row_specs + const_specs, out_specs=out_specs, out_shape=out_shape,
        compiler_params=pltpu.CompilerParams(dimension_semantics=("arbitrary",)),
    )(*row_arrays, *consts)
    return res


def _colsum(v):
    return jnp.sum(v, axis=0, keepdims=True)


def _rms(u, g):
    return u * lax.rsqrt(jnp.mean(u * u, axis=-1, keepdims=True) + EPS) * g


def _ln(u, g, b):
    mu = jnp.mean(u, axis=-1, keepdims=True)
    d = u - mu
    var = jnp.mean(d * d, axis=-1, keepdims=True)
    return d * lax.rsqrt(var + LN_EPS) * g + b


def _sigmoid(v):
    return 1.0 / (1.0 + jnp.exp(-v))


def _silu(v):
    return v * _sigmoid(v)


def _softplus(v):
    y = jnp.exp(-jnp.abs(v))
    w = 1.0 + y
    log1p = jnp.where(w == 1.0, y, jnp.log(w) * y / jnp.where(w == 1.0, 1.0, w - 1.0))
    return jnp.maximum(v, 0.0) + log1p


def _gate_rms(y, z, w):
    return _rms(y * _silu(z), w)


def _rms_bwd(u, g, d_out):
    r = lax.rsqrt(jnp.mean(u * u, axis=-1, keepdims=True) + EPS)
    n = u * r
    gd = d_out * g
    return r * (gd - n * jnp.mean(gd * n, axis=-1, keepdims=True)), _colsum(d_out * n)


def _ln_parts(u):
    d = u - jnp.mean(u, axis=-1, keepdims=True)
    r = lax.rsqrt(jnp.mean(d * d, axis=-1, keepdims=True) + LN_EPS)
    return d * r, r


def _ln_bwd(n, r, g, d_out):
    gd = d_out * g
    d_u = r * (gd - jnp.mean(gd, axis=-1, keepdims=True) - n * jnp.mean(gd * n, axis=-1, keepdims=True))
    return d_u, _colsum(d_out * n), _colsum(d_out)


def _conv_pre(cur, prev, w, b, first):
    row = lax.broadcasted_iota(jnp.int32, cur.shape, 0)
    acc = cur * w[3:4, :] + b
    for j in (1, 2, 3):
        tail = jnp.where(first, 0.0, pltpu.roll(prev, j, 0))
        acc = acc + jnp.where(row >= j, pltpu.roll(cur, j, 0), tail) * w[3 - j:4 - j, :]
    return acc


def _conv_fwd(u, ucb, w, b, name="conv_fwd"):
    s_dim, width = u.shape[0], w.shape[1]
    tr = min(ROW_TILE, s_dim)

    def body(cur_ref, prev_ref, w_ref, b_ref, o_ref):
        pre = _conv_pre(cur_ref[...], prev_ref[...], w_ref, b_ref[...], pl.program_id(0) == 0)
        o_ref[...] = _silu(pre)

    return pl.pallas_call(
        body, name=name, grid=(s_dim // tr,),
        in_specs=[pl.BlockSpec((tr, width), lambda i: (i, ucb)),
                  pl.BlockSpec((tr, width), lambda i: (jnp.maximum(i - 1, 0), ucb)),
                  pl.BlockSpec(w.shape, lambda i: (0, 0)), pl.BlockSpec(b.shape, lambda i: (0, 0))],
        out_specs=pl.BlockSpec((tr, width), lambda i: (i, 0)), out_shape=jax.ShapeDtypeStruct((s_dim, width), F32),
        compiler_params=pltpu.CompilerParams(dimension_semantics=("arbitrary",)),
    )(u, u, w, b)


def _conv_bwd_pre(u, ucb, w, b, dact, name="conv_bwd_pre"):
    s_dim, width = u.shape[0], w.shape[1]
    tr = min(ROW_TILE, s_dim)

    def body(cur_ref, prev_ref, w_ref, b_ref, d_ref, da_ref, dw_ref, db_ref):
        first = pl.program_id(0) == 0
        cur, prev = cur_ref[...], prev_ref[...]
        pre = _conv_pre(cur, prev, w_ref, b_ref[...], first)
        sg = _sigmoid(pre)
        da = d_ref[...] * (sg * (1.0 + pre * (1.0 - sg)))
        da_ref[...] = da
        row = lax.broadcasted_iota(jnp.int32, cur.shape, 0)

        @pl.when(first)
        def _():
            dw_ref[...] = jnp.zeros_like(dw_ref)
            db_ref[...] = jnp.zeros_like(db_ref)

        db_ref[...] += _colsum(da)
        dw_ref[3:4, :] += _colsum(da * cur)
        for j in (1, 2, 3):
            tail = jnp.where(first, 0.0, pltpu.roll(prev, j, 0))
            sh = jnp.where(row >= j, pltpu.roll(cur, j, 0), tail)
            dw_ref[3 - j:4 - j, :] += _colsum(da * sh)

    return pl.pallas_call(
        body, name=name, grid=(s_dim // tr,),
        in_specs=[pl.BlockSpec((tr, width), lambda i: (i, ucb)),
                  pl.BlockSpec((tr, width), lambda i: (jnp.maximum(i - 1, 0), ucb)),
                  pl.BlockSpec(w.shape, lambda i: (0, 0)), pl.BlockSpec(b.shape, lambda i: (0, 0)),
                  pl.BlockSpec((tr, width), lambda i: (i, 0))],
        out_specs=[pl.BlockSpec((tr, width), lambda i: (i, 0)), pl.BlockSpec(w.shape, lambda i: (0, 0)),
                   pl.BlockSpec(b.shape, lambda i: (0, 0))],
        out_shape=[jax.ShapeDtypeStruct((s_dim, width), F32), jax.ShapeDtypeStruct(w.shape, F32),
                   jax.ShapeDtypeStruct(b.shape, F32)],
        compiler_params=pltpu.CompilerParams(dimension_semantics=("arbitrary",)),
    )(u, u, w, b, dact)


def _conv_bwd_in(da, w, name="conv_bwd_in"):
    s_dim, width = da.shape
    tr = min(ROW_TILE, s_dim)
    n = s_dim // tr

    def body(cur_ref, nxt_ref, w_ref, o_ref):
        last = pl.program_id(0) == n - 1
        cur, nxt = cur_ref[...], nxt_ref[...]
        row = lax.broadcasted_iota(jnp.int32, cur.shape, 0)
        acc = cur * w_ref[3:4, :]
        for j in (1, 2, 3):
            head = jnp.where(last, 0.0, pltpu.roll(nxt, tr - j, 0))
            acc = acc + jnp.where(row < tr - j, pltpu.roll(cur, tr - j, 0), head) * w_ref[3 - j:4 - j, :]
        o_ref[...] = acc.astype(o_ref.dtype)

    return pl.pallas_call(
        body, name=name, grid=(n,),
        in_specs=[pl.BlockSpec((tr, width), lambda i: (i, 0)), pl.BlockSpec((tr, width), lambda i: (jnp.minimum(i + 1, n - 1), 0)),
                  pl.BlockSpec(w.shape, lambda i: (0, 0))],
        out_specs=pl.BlockSpec((tr, width), lambda i: (i, 0)), out_shape=jax.ShapeDtypeStruct((s_dim, width), BF16),
        compiler_params=pltpu.CompilerParams(dimension_semantics=("arbitrary",)),
    )(da, da, w)


def _sel_dot(a, sel, pieces, dims=(((1,), (0,)), ((), ())), sel_left=False):
    sel = sel.astype(BF16)
    acc, rest = None, a
    for _ in range(pieces):
        piece = rest.astype(BF16)
        rest = rest - piece.astype(F32)
        part = _dot(sel, piece, dims) if sel_left else _dot(piece, sel, dims)
        acc = part if acc is None else acc + part
    return acc


def _ssd_consts():
    L = SSD_CHUNK
    tri = np.tril(np.ones((L, L), np.float32))
    expand = np.zeros((LANE, SSD_INNER), np.float32)
    for h in range(SSD_HEADS):
        expand[h, h * SSD_HEAD_DIM:(h + 1) * SSD_HEAD_DIM] = 1.0
    return jnp.asarray(tri), jnp.asarray(expand), jnp.asarray(expand.T.copy())


def _ssd_prep(dt_ref, bias_ref, alog_ref, tri_ref, exp_ref, cs_s, cst_s, ex_s):
    L = SSD_CHUNK
    dt = _softplus(dt_ref[...] + bias_ref[...])
    a = -jnp.exp(alog_ref[...])
    cs = _sel_dot(dt * a, tri_ref[...], 3, sel_left=True)
    cs_s[...] = cs
    cst_s[...] = cs.T
    last = cs_s[L - 1:L, :]
    expand = exp_ref[...]
    ex_s[...] = _sel_dot(jnp.exp(cs), expand, 2)
    f_x = _sel_dot(jnp.exp(last - cs), expand, 2)
    dt_x = _sel_dot(dt, expand, 2)
    t_x = ex_s[L - 1:L, :]
    return dt, a, dt_x, f_x, t_x


def _decay_matrix(cs_s, cst_s, h, tril):
    seg = cs_s[:, h:h + 1] - cst_s[h:h + 1, :]
    return jnp.exp(jnp.where(tril, seg, -jnp.inf))


def _ssd_fwd(xbca, dtr, dtcb, bias, alog, d_x, name="ssd_fwd"):
    s_dim = xbca.shape[0]
    L = SSD_CHUNK
    nc = s_dim // L
    tri, expand, _ = _ssd_consts()

    def body(xs_ref, b_ref, c_ref, dt_ref, bias_ref, alog_ref, dx_ref, tri_ref, exp_ref,
             y_ref, st_ref, st_s, cs_s, cst_s, ex_s):
        @pl.when(pl.program_id(0) == 0)
        def _():
            st_s[...] = jnp.zeros_like(st_s)

        dt, a, dt_x, f_x, t_x = _ssd_prep(dt_ref, bias_ref, alog_ref, tri_ref, exp_ref, cs_s, cst_s, ex_s)
        st_ref[0] = st_s[...]
        row = lax.broadcasted_iota(jnp.int32, (L, L), 0)
        col = lax.broadcasted_iota(jnp.int32, (L, L), 1)
        tril = row >= col
        low = col < SSD_HEAD_DIM
        for g in range(2):
            bg = b_ref[:, g * LANE:(g + 1) * LANE]
            cg = c_ref[:, g * LANE:(g + 1) * LANE].astype(BF16)
            gmat = _dot(cg, bg.astype(BF16), _NT)
            bgt = bg.T.astype(BF16)
            for jj in range(4):
                j = 4 * g + jj
                sl = slice(j * LANE, (j + 1) * LANE)
                xp = xs_ref[:, sl]
                x_dt = xp * dt_x[:, sl]
                xb = x_dt.astype(BF16)
                yd = []
                for e in range(2):
                    lm = _decay_matrix(cs_s, cst_s, 2 * j + e, tril)
                    yd.append(_dot((gmat * lm).astype(BF16), xb))
                stp = st_s[j]
                z = _dot(cg, stp.astype(BF16))
                y_ref[:, sl] = jnp.where(low, yd[0], yd[1]) + ex_s[:, sl] * z + dx_ref[:, sl] * xp
                xf = (x_dt * f_x[:, sl]).astype(BF16)
                st_s[j] = t_x[:, sl] * stp + _dot(bgt, xf)

    const = lambda shape: pl.BlockSpec(shape, lambda c: tuple(0 for _ in shape))
    return pl.pallas_call(
        body, name=name, grid=(nc,),
        in_specs=[pl.BlockSpec((L, 1024), lambda c: (c, 0)), pl.BlockSpec((L, 256), lambda c: (c, 4)),
                  pl.BlockSpec((L, 256), lambda c: (c, 5)), pl.BlockSpec((L, LANE), lambda c: (c, dtcb)),
                  const((1, LANE)), const((1, LANE)), const((1, 1024)), const((L, L)), const((LANE, 1024))],
        out_specs=[pl.BlockSpec((L, 1024), lambda c: (c, 0)), pl.BlockSpec((1, 8, LANE, LANE), lambda c: (c, 0, 0, 0))],
        out_shape=[jax.ShapeDtypeStruct((s_dim, 1024), F32), jax.ShapeDtypeStruct((nc, 8, LANE, LANE), F32)],
        scratch_shapes=[pltpu.VMEM((8, LANE, LANE), F32), pltpu.VMEM((L, LANE), F32), pltpu.VMEM((LANE, L), F32),
                        pltpu.VMEM((L, 1024), F32)],
        compiler_params=pltpu.CompilerParams(dimension_semantics=("arbitrary",)),
    )(xbca, xbca, xbca, dtr, bias, alog, d_x, tri, expand)


def _ssd_bwd(xbca, dtr, dtcb, bias, alog, d_x, states, dy, name="ssd_bwd"):
    s_dim = xbca.shape[0]
    L = SSD_CHUNK
    nc = s_dim // L
    tri, expand, expand_t = _ssd_consts()

    def body(xs_ref, b_ref, c_ref, dt_ref, bias_ref, alog_ref, dx_ref, tri_ref, exp_ref, expt_ref,
             st_ref, dy_ref, dxbc_ref, ddt_ref, dbias_ref, dalog_ref, dd_ref,
             dst_s, cs_s, cst_s, ex_s, dcsx_s, ddtx_s, dcol_s, drow_s, dlast_s, dd_s):
        @pl.when(pl.program_id(0) == 0)
        def _():
            dst_s[...] = jnp.zeros_like(dst_s)
            dbias_ref[...] = jnp.zeros_like(dbias_ref)
            dalog_ref[...] = jnp.zeros_like(dalog_ref)
            dd_s[...] = jnp.zeros_like(dd_s)

        dt, a, dt_x, f_x, t_x = _ssd_prep(dt_ref, bias_ref, alog_ref, tri_ref, exp_ref, cs_s, cst_s, ex_s)
        row = lax.broadcasted_iota(jnp.int32, (L, L), 0)
        col = lax.broadcasted_iota(jnp.int32, (L, L), 1)
        tril = row >= col
        low = col < SSD_HEAD_DIM
        dcol_s[...] = jnp.zeros_like(dcol_s)
        drow_s[...] = jnp.zeros_like(drow_s)
        for g in range(2):
            bg = b_ref[:, g * LANE:(g + 1) * LANE]
            cg = c_ref[:, g * LANE:(g + 1) * LANE]
            bgb, cgb = bg.astype(BF16), cg.astype(BF16)
            gmat = _dot(cgb, bgb, _NT)
            d_g = jnp.zeros((L, L), F32)
            d_b = jnp.zeros((L, LANE), F32)
            d_c = jnp.zeros((L, LANE), F32)
            for jj in range(4):
                j = 4 * g + jj
                sl = slice(j * LANE, (j + 1) * LANE)
                xp = xs_ref[:, sl]
                dtp = dt_x[:, sl]
                x_dt = xp * dtp
                xb = x_dt.astype(BF16)
                dyp = dy_ref[:, sl]
                dd_s[:, sl] += _colsum(dyp * xp)
                d_xdt = jnp.zeros((L, LANE), F32)
                for e in range(2):
                    h = 2 * j + e
                    lm = _decay_matrix(cs_s, cst_s, h, tril)
                    m = gmat * lm
                    dye = jnp.where(low if e == 0 else jnp.logical_not(low), dyp, 0.0).astype(BF16)
                    d_m = jnp.where(tril, _dot(dye, xb, _NT), 0.0)
                    d_xdt = d_xdt + _dot(m.astype(BF16), dye, _TN)
                    d_g = d_g + d_m * lm
                    w = d_m * m
                    dcol_s[...] += jnp.where(col == h, jnp.sum(w, axis=1, keepdims=True), 0.0)
                    drow_s[...] += jnp.where(row == h, jnp.sum(w, axis=0, keepdims=True), 0.0)
                stp = st_ref[0, j]
                stb = stp.astype(BF16)
                dstn = dst_s[j]
                dstb = dstn.astype(BF16)
                e_p = ex_s[:, sl]
                f_p = f_x[:, sl]
                t_p = t_x[:, sl]
                z = _dot(cgb, stb)
                d_z = (e_p * dyp).astype(BF16)
                d_c = d_c + _dot(d_z, stb, _NT)
                d_xf = _dot(bgb, dstb)
                d_b = d_b + _dot((x_dt * f_p).astype(BF16), dstb, _NT)
                d_xdt = d_xdt + f_p * d_xf
                d_f = x_dt * d_xf * f_p
                dcsx_s[:, sl] = dyp * e_p * z - d_f
                dlast_s[:, sl] = _colsum(d_f) + _colsum(dstn * stp) * t_p
                dst_s[j] = _dot(cgb, d_z, _TN) + t_p * dstn
                dxbc_ref[:, sl] = dx_ref[:, sl] * dyp + d_xdt * dtp
                ddtx_s[:, sl] = d_xdt * xp
            d_gb = d_g.astype(BF16)
            dxbc_ref[:, 1024 + g * LANE:1024 + (g + 1) * LANE] = d_b + _dot(d_gb, cgb, _TN)
            dxbc_ref[:, 1280 + g * LANE:1280 + (g + 1) * LANE] = d_c + _dot(d_gb, bgb)

        expt = expt_ref[...]
        dlast = _sel_dot(jnp.broadcast_to(dlast_s[...], (8, 1024)), expt, 3)
        d_cs = dcol_s[...] - drow_s[...].T + _sel_dot(dcsx_s[...], expt, 3)
        rown = lax.broadcasted_iota(jnp.int32, (L, LANE), 0)
        d_cs = d_cs + jnp.where(rown == L - 1, jnp.sum(dlast, axis=0, keepdims=True) * 0.125, 0.0)
        d_da = _sel_dot(d_cs, tri_ref[...], 3, _TN, sel_left=True)
        d_dt = d_da * a + _sel_dot(ddtx_s[...], expt, 3)
        dalog_ref[...] += _colsum(d_da * dt) * a
        d_raw = d_dt * _sigmoid(dt_ref[...] + bias_ref[...])
        ddt_ref[...] = d_raw.astype(ddt_ref.dtype)
        dbias_ref[...] += _colsum(d_raw)
        dd8 = _sel_dot(jnp.broadcast_to(dd_s[...], (8, 1024)), expt, 3)
        dd_ref[...] = jnp.sum(dd8, axis=0, keepdims=True) * 0.125

    const = lambda shape: pl.BlockSpec(shape, lambda c: tuple(0 for _ in shape))
    rev = lambda cb: (lambda c: (nc - 1 - c, cb))
    return pl.pallas_call(
        body, name=name, grid=(nc,),
        in_specs=[pl.BlockSpec((L, 1024), rev(0)), pl.BlockSpec((L, 256), rev(4)), pl.BlockSpec((L, 256), rev(5)),
                  pl.BlockSpec((L, LANE), rev(dtcb)), const((1, LANE)), const((1, LANE)), const((1, 1024)), const((L, L)),
                  const((LANE, 1024)), const((1024, LANE)),
                  pl.BlockSpec((1, 8, LANE, LANE), lambda c: (nc - 1 - c, 0, 0, 0)), pl.BlockSpec((L, 1024), rev(0))],
        out_specs=[pl.BlockSpec((L, SSD_XBC), rev(0)), pl.BlockSpec((L, LANE), rev(0)), const((1, LANE)), const((1, LANE)),
                   const((1, LANE))],
        out_shape=[jax.ShapeDtypeStruct((s_dim, SSD_XBC), F32), jax.ShapeDtypeStruct((s_dim, LANE), BF16),
                   jax.ShapeDtypeStruct((1, LANE), F32), jax.ShapeDtypeStruct((1, LANE), F32),
                   jax.ShapeDtypeStruct((1, LANE), F32)],
        scratch_shapes=[pltpu.VMEM((8, LANE, LANE), F32), pltpu.VMEM((L, LANE), F32), pltpu.VMEM((LANE, L), F32),
                        pltpu.VMEM((L, 1024), F32), pltpu.VMEM((L, 1024), F32), pltpu.VMEM((L, 1024), F32),
                        pltpu.VMEM((L, LANE), F32), pltpu.VMEM((LANE, L), F32), pltpu.VMEM((1, 1024), F32),
                        pltpu.VMEM((1, 1024), F32)],
        compiler_params=pltpu.CompilerParams(dimension_semantics=("arbitrary",)),
    )(xbca, xbca, xbca, dtr, bias, alog, d_x, tri, expand, expand_t, states, dy)


def _swap_halves(u):
    width = u.shape[1]
    lane = lax.broadcasted_iota(jnp.int32, u.shape, 1)
    return jnp.where(lane % MLA_ROPE < MLA_ROPE // 2, pltpu.roll(u, width - MLA_ROPE // 2, 1), pltpu.roll(u, MLA_ROPE // 2, 1))


def _rope_fwd_fn(u, cos, sin):
    return u * cos + _swap_halves(u) * sin


def _rope_bwd_fn(d, cos, sin):
    return d * cos + _swap_halves(d * sin)


def _spread4(v):
    return v + pltpu.roll(v, 32, 1) + pltpu.roll(v, 64, 1) + pltpu.roll(v, 96, 1)


def _att_masks(tq):
    lane = lax.broadcasted_iota(jnp.int32, (tq, LANE), 1)
    return lane // MLA_NOPE, lane // MLA_ROPE


def _att_tile(i, tq):
    klen = (i + 1) * tq
    qpos = i * tq + lax.broadcasted_iota(jnp.int32, (tq, klen), 0)
    kpos = lax.broadcasted_iota(jnp.int32, (tq, klen), 1)
    return slice(i * tq, (i + 1) * tq), klen, qpos >= kpos


def _att_qcat(qn_t, qr_t, par, e, half_id, grp_id):
    return jnp.concatenate([jnp.where(half_id == par, qn_t * ATT_SCALE, 0.0), jnp.where(grp_id == e, qr_t * ATT_SCALE, 0.0)],
                           axis=1).astype(BF16)


def _att_softmax(scores, causal):
    s = jnp.where(causal, scores, -jnp.inf)
    e = jnp.exp(s - jnp.max(s, axis=1, keepdims=True))
    return e, 1.0 / jnp.sum(e, axis=1, keepdims=True)


def _att_specs(s_dim):
    col = lambda f: pl.BlockSpec((s_dim, LANE), lambda j: (0, f(j)))
    return [col(lambda j: j), col(lambda j: j // 2), col(lambda j: j), col(lambda j: 0), col(lambda j: 8 + j)]


def _att_fwd(q, qr, kv, krt, name="att_fwd"):
    s_dim = q.shape[0]
    tq = min(ATT_TQ, s_dim)

    def body(qn_ref, qr_ref, kn_ref, krt_ref, v_ref, o_ref, kcat_s, vb_s):
        e0 = 2 * (pl.program_id(0) % 2)
        half_id, grp_id = _att_masks(tq)
        kcat_s[...] = jnp.concatenate([kn_ref[...], krt_ref[...]], axis=1).astype(BF16)
        vb_s[...] = v_ref[...].astype(BF16)
        for i in range(s_dim // tq):
            rows, klen, causal = _att_tile(i, tq)
            qn_t, qr_t = qn_ref[rows, :], qr_ref[rows, :]
            scores = [_dot(_att_qcat(qn_t, qr_t, par, e0 + par, half_id, grp_id), kcat_s[0:klen, :], _NT) for par in range(2)]
            probs = [_att_softmax(s, causal) for s in scores]
            outs = [_dot(e.astype(BF16), vb_s[0:klen, :]) * inv_l for e, inv_l in probs]
            o_ref[rows, :] = jnp.where(half_id == 0, outs[0], outs[1])

    return pl.pallas_call(
        body, name=name, grid=(MLA_HEADS // 2,), in_specs=_att_specs(s_dim),
        out_specs=pl.BlockSpec((s_dim, LANE), lambda j: (0, j)), out_shape=jax.ShapeDtypeStruct((s_dim, 1024), F32),
        scratch_shapes=[pltpu.VMEM((s_dim, 2 * LANE), BF16), pltpu.VMEM((s_dim, LANE), BF16)],
        compiler_params=pltpu.CompilerParams(dimension_semantics=("parallel",)),
    )(q, qr, kv, krt, kv)


def _att_bwd(q, qr, kv, krt, o, do, name="att_bwd"):
    s_dim = q.shape[0]
    tq = min(ATT_TQ, s_dim)

    def body(qn_ref, qr_ref, kn_ref, krt_ref, v_ref, o_ref, do_ref, dqn_ref, dqr_ref, dkn_ref, dv_ref, dkrt_ref,
             kcat_s, vb_s):
        e0 = 2 * (pl.program_id(0) % 2)
        half_id, grp_id = _att_masks(tq)
        kcat_s[...] = jnp.concatenate([kn_ref[...], krt_ref[...]], axis=1).astype(BF16)
        vb_s[...] = v_ref[...].astype(BF16)
        dkn_ref[...] = jnp.zeros_like(dkn_ref)
        dv_ref[...] = jnp.zeros_like(dv_ref)
        dkrt_ref[...] = jnp.zeros_like(dkrt_ref)
        for i in range(s_dim // tq):
            rows, klen, causal = _att_tile(i, tq)
            qn_t, qr_t, o_t, do_t = qn_ref[rows, :], qr_ref[rows, :], o_ref[rows, :], do_ref[rows, :]
            heads = range(2)
            qcats = [_att_qcat(qn_t, qr_t, par, e0 + par, half_id, grp_id) for par in heads]
            scores = [_dot(qcats[par], kcat_s[0:klen, :], _NT) for par in heads]
            doms = [jnp.where(half_id == par, do_t, 0.0) for par in heads]
            dombs = [d.astype(BF16) for d in doms]
            d_ps = [_dot(dombs[par], vb_s[0:klen, :], _NT) for par in heads]
            probs = []
            for par in heads:
                e, inv_l = _att_softmax(scores[par], causal)
                probs.append(e * inv_l)
            d_ss = []
            for par in heads:
                d_row = jnp.sum(doms[par] * o_t, axis=1, keepdims=True)
                d_ss.append((probs[par] * (d_ps[par] - d_row)).astype(BF16))
            dqcats = [_dot(d_ss[par], kcat_s[0:klen, :]) * ATT_SCALE for par in heads]
            dkcats = [_dot(d_ss[par], qcats[par], _TN) for par in heads]
            dvs = [_dot(probs[par].astype(BF16), dombs[par], _TN) for par in heads]
            dqn_ref[rows, :] = jnp.where(half_id == 0, dqcats[0][:, :LANE], dqcats[1][:, :LANE]).astype(dqn_ref.dtype)
            dqr_ref[rows, :] = (jnp.where(grp_id == e0, dqcats[0][:, LANE:], 0.0)
                                + jnp.where(grp_id == e0 + 1, dqcats[1][:, LANE:], 0.0))
            dkn_ref[0:klen, :] += dkcats[0][:, :LANE] + dkcats[1][:, :LANE]
            dkrt_ref[0:klen, :] += dkcats[0][:, LANE:] + dkcats[1][:, LANE:]
            dv_ref[0:klen, :] += dvs[0] + dvs[1]

    col = lambda f: pl.BlockSpec((s_dim, LANE), lambda j: (0, f(j)))
    return pl.pallas_call(
        body, name=name, grid=(MLA_HEADS // 2,), in_specs=_att_specs(s_dim) + [col(lambda j: j), col(lambda j: j)],
        out_specs=[col(lambda j: j), pl.BlockSpec((None, s_dim, LANE), lambda j: (j % 2, 0, j // 2)), col(lambda j: j),
                   col(lambda j: j), pl.BlockSpec((None, s_dim, LANE), lambda j: (j, 0, 0))],
        out_shape=[jax.ShapeDtypeStruct((s_dim, 1024), BF16), jax.ShapeDtypeStruct((2, s_dim, 512), F32),
                   jax.ShapeDtypeStruct((s_dim, 1024), F32), jax.ShapeDtypeStruct((s_dim, 1024), F32),
                   jax.ShapeDtypeStruct((MLA_HEADS // 2, s_dim, LANE), F32)],
        scratch_shapes=[pltpu.VMEM((s_dim, 2 * LANE), BF16), pltpu.VMEM((s_dim, LANE), BF16)],
        compiler_params=pltpu.CompilerParams(dimension_semantics=("parallel",)),
    )(q, qr, kv, krt, kv, o, do)


def _gather_many(shards, name):
    n_arr = len(shards)

    def body(*refs):
        x_refs, out_refs = refs[:n_arr], refs[n_arr:2 * n_arr]
        send_sems, recv_sems, local_sems = refs[2 * n_arr:]
        x_i, y_i, c_i = lax.axis_index("x"), lax.axis_index("y"), lax.axis_index("c")
        me, sibling = (x_i, y_i, c_i), (x_i, y_i, 1 - c_i)
        chips = [(1 - x_i, y_i), (x_i, 1 - y_i), (1 - x_i, 1 - y_i)]

        def copy(a, k, block, to, src=None):
            slot = out_refs[a].at[4 * block[0] + 2 * block[1] + block[2]]
            return pltpu.make_async_remote_copy(
                src_ref=slot if src is None else src, dst_ref=slot, send_sem=send_sems.at[a, k],
                recv_sem=recv_sems.at[a, k], device_id=to, device_id_type=pl.DeviceIdType.MESH)

        mine, first, passed = [], [], []
        for a in range(n_arr):
            mine.append(pltpu.make_async_copy(x_refs[a], out_refs[a].at[4 * x_i + 2 * y_i + c_i], local_sems.at[a]))
            mine[a].start()
            first.append([copy(a, 0, me, sibling, src=x_refs[a])]
                         + [copy(a, 1 + j, me, (*chip, c_i), src=x_refs[a]) for j, chip in enumerate(chips)])
            for cp in first[a]:
                cp.start()
            passed.append([copy(a, 4 + j, (*chip, c_i), sibling) for j, chip in enumerate(chips)])
        for j, chip in enumerate(chips):
            for a in range(n_arr):
                copy(a, 1 + j, (*chip, c_i), me).wait_recv()
                passed[a][j].start()
        for a in range(n_arr):
            copy(a, 0, sibling, me).wait_recv()
            for j, chip in enumerate(chips):
                copy(a, 4 + j, (*chip, 1 - c_i), me).wait_recv()
        for a in range(n_arr):
            for cp in first[a] + passed[a]:
                cp.wait_send()
            mine[a].wait()

    any_spec = pl.BlockSpec(memory_space=pl.ANY)
    return pl.pallas_call(
        body, name=name, out_shape=[jax.ShapeDtypeStruct((N_DEV,) + x.shape, x.dtype) for x in shards],
        in_specs=[any_spec] * n_arr, out_specs=[any_spec] * n_arr,
        scratch_shapes=[pltpu.SemaphoreType.DMA((n_arr, 7)), pltpu.SemaphoreType.DMA((n_arr, 7)),
                        pltpu.SemaphoreType.DMA((n_arr,))],
    )(*shards)


_HBM = pl.BlockSpec(memory_space=pltpu.HBM)
_SEM = pl.BlockSpec(memory_space=pltpu.SEMAPHORE)


def _plan_copies(plan, src_refs, land_refs, send_sems, recv_sems):
    copies = []
    for s_ref, l_ref in zip(src_refs, land_refs):
        for src, dst, peer in plan(s_ref, l_ref):
            k = len(copies)
            copies.append(pltpu.make_async_remote_copy(
                src_ref=src, dst_ref=dst, send_sem=send_sems.at[k], recv_sem=recv_sems.at[k], device_id=peer,
                device_id_type=pl.DeviceIdType.MESH))
    return copies


def _split_start(srcs, lands, plan, n_copy, name, after=None):
    n = len(srcs)
    n_in = 2 * n + (after is not None)

    def body(*refs):
        for cp in _plan_copies(plan, refs[:n], refs[n:2 * n], refs[n_in], refs[n_in + 1]):
            cp.start()
        refs[-1][...] = jnp.zeros_like(refs[-1])

    sems = pltpu.SemaphoreType.DMA((n * n_copy,))
    res = pl.pallas_call(
        body, name=name,
        out_shape=(sems, sems, *[pltpu.HBM(a.shape, a.dtype) for a in list(srcs) + list(lands)],
                   jax.ShapeDtypeStruct((8, LANE), F32)),
        in_specs=[_HBM] * (2 * n) + [pl.BlockSpec(memory_space=pl.ANY)] * (after is not None),
        out_specs=(_SEM, _SEM, *[_HBM] * (2 * n), pl.BlockSpec(memory_space=pltpu.VMEM)),
        input_output_aliases={i: 2 + i for i in range(2 * n)},
        compiler_params=pltpu.CompilerParams(has_side_effects=pltpu.SideEffectType.DATAFLOW_SIDE_EFFECTING),
    )(*[pltpu.with_memory_space_constraint(a, pltpu.HBM) for a in list(srcs) + list(lands)],
      *([after] if after is not None else []))
    return res[0], res[1], list(res[2:2 + n]), list(res[2 + n:2 + 2 * n]), res[-1]


def _split_wait(send_sems, recv_sems, srcs, lands, after, plan, name):
    n = len(srcs)

    def body(*refs):
        copies = _plan_copies(plan, refs[:n], refs[n:2 * n], refs[2 * n], refs[2 * n + 1])
        for cp in copies:
            cp.wait_send()
        for cp in copies:
            cp.wait_recv()

    res = pl.pallas_call(
        body, name=name, out_shape=tuple(pltpu.HBM(a.shape, a.dtype) for a in list(srcs) + list(lands)),
        in_specs=[_HBM] * (2 * n) + [_SEM, _SEM, pl.BlockSpec(memory_space=pl.ANY)], out_specs=tuple([_HBM] * (2 * n)),
        input_output_aliases={i: i for i in range(2 * n)},
        compiler_params=pltpu.CompilerParams(has_side_effects=pltpu.SideEffectType.DATAFLOW_SIDE_EFFECTING),
    )(*srcs, *lands, send_sems, recv_sems, after)
    return list(res[:n]), list(res[n:])


def _plan_broadcast(src, land):
    x_i, y_i, c_i = lax.axis_index("x"), lax.axis_index("y"), lax.axis_index("c")
    me = 4 * x_i + 2 * y_i + c_i
    return [(src, land.at[me], (x_i ^ (k >> 2), y_i ^ ((k >> 1) & 1), c_i ^ (k & 1))) for k in range(1, N_DEV)]


def _plan_scatter(src, land):
    x_i, y_i, c_i = lax.axis_index("x"), lax.axis_index("y"), lax.axis_index("c")
    me = 4 * x_i + 2 * y_i + c_i
    plan = []
    for k in range(1, N_DEV):
        px, py, pc = x_i ^ (k >> 2), y_i ^ ((k >> 1) & 1), c_i ^ (k & 1)
        plan.append((src.at[4 * px + 2 * py + pc], land.at[me], (px, py, pc)))
    return plan


def _adam_math(g, w, m, v):
    m_new = ADAM_B1 * m + (1.0 - ADAM_B1) * g
    v_new = ADAM_B2 * v + (1.0 - ADAM_B2) * (g * g)
    m_hat = m_new / (1.0 - ADAM_B1 ** ADAM_STEP)
    v_hat = v_new / (1.0 - ADAM_B2 ** ADAM_STEP)
    return -ADAM_LR * (m_hat / (jnp.sqrt(v_hat) + ADAM_EPS) + ADAM_WD * w), m_new, v_new


def _adam(slots, w, m, v, name, own=None, own_idx=None):
    n_slot, rows, cols = slots.shape
    tr = ROW_TILE if rows % ROW_TILE == 0 else rows
    has_own = own is not None

    def body(*refs):
        if has_own:
            idx_ref, own_ref, refs = refs[0], refs[1], refs[2:]
        s_ref, w_ref, m_ref, v_ref, g_ref, d_ref, mo_ref, vo_ref = refs
        g = own_ref[...].astype(F32) if has_own else s_ref[0].astype(F32)
        for k in range(0 if has_own else 1, n_slot):
            part = s_ref[k].astype(F32)
            g = g + (jnp.where(idx_ref[0] == k, 0.0, part) if has_own else part)
        g_ref[...] = g
        d_ref[...], mo_ref[...], vo_ref[...] = _adam_math(g, w_ref[...], m_ref[...], v_ref[...])

    spec = pl.BlockSpec((tr, cols), lambda i, *_: (i, 0))
    in_specs = [pl.BlockSpec((n_slot, tr, cols), lambda i, *_: (0, i, 0)), spec, spec, spec]
    if has_own:
        in_specs = [pl.BlockSpec((None, tr, cols), lambda i, idx: (idx[0], i, 0))] + in_specs
    grid_spec = pltpu.PrefetchScalarGridSpec(num_scalar_prefetch=1 if has_own else 0, grid=(rows // tr,), in_specs=in_specs,
                                             out_specs=[spec] * 4)
    ins = ([own_idx, own] if has_own else []) + [slots, w, m, v]
    return pl.pallas_call(
        body, name=name, grid_spec=grid_spec, out_shape=[jax.ShapeDtypeStruct((rows, cols), F32)] * 4,
        compiler_params=pltpu.CompilerParams(dimension_semantics=("parallel",)),
    )(*ins)


PACK_ROWS, PACK_W = 24, 1536
REPL_W = (("ssd_conv_b", 1536), ("ssd_dt_bias", 16), ("ssd_A_log", 16), ("ssd_D", 16), ("ssd_norm_w", 1024),
          ("mla_q_norm_w", 384), ("mla_kv_norm_w", 256), ("mla_out_norm_w", 1024), ("ln_mix_g", 1024),
          ("ln_mix_b", 1024), ("ln_ffn_g", 1024), ("ln_ffn_b", 1024))
LOSS_ROW = 4 + len(REPL_W)


def _pack_small(conv_w_grad, grads, loss, name="pack_small"):
    def body(*refs):
        cw_ref, g_refs, loss_ref, o_ref = refs[0], refs[1:1 + len(REPL_W)], refs[1 + len(REPL_W)], refs[-1]
        o_ref[...] = jnp.zeros_like(o_ref)
        o_ref[0:4, :] = cw_ref[...]
        for i, g_ref in enumerate(g_refs):
            o_ref[4 + i:5 + i, 0:g_ref.shape[1]] = g_ref[...]
        o_ref[LOSS_ROW:LOSS_ROW + 1, 0:LANE] = loss_ref[...]

    return pl.pallas_call(body, name=name, out_shape=jax.ShapeDtypeStruct((PACK_ROWS, PACK_W), F32))(conv_w_grad, *grads, loss)


def _adam_small(gathered, wmv, name="adam_small"):
    def body(*refs):
        s_ref = refs[0]
        in_refs = refs[1:1 + 3 * len(REPL_W)]
        cw_ref, loss_ref = refs[1 + 3 * len(REPL_W)], refs[2 + 3 * len(REPL_W)]
        out_refs = refs[3 + 3 * len(REPL_W):-1]
        tot = refs[-1]
        acc = s_ref[0]
        for k in range(1, N_DEV):
            acc = acc + s_ref[k]
        tot[...] = acc
        cw_ref[...] = tot[0:4, :]
        loss_ref[...] = tot[LOSS_ROW:LOSS_ROW + 1, 0:LANE]
        for i, (_, width) in enumerate(REPL_W):
            g = tot[4 + i:5 + i, 0:width]
            w_ref, m_ref, v_ref = in_refs[3 * i:3 * i + 3]
            g_ref, d_ref, mo_ref, vo_ref = out_refs[4 * i:4 * i + 4]
            g_ref[...] = g
            d_ref[...], mo_ref[...], vo_ref[...] = _adam_math(g, w_ref[...], m_ref[...], v_ref[...])

    flat_in = [a for triple in wmv for a in triple]
    out_shape = [jax.ShapeDtypeStruct((4, PACK_W), F32), jax.ShapeDtypeStruct((1, LANE), F32)]
    for _, width in REPL_W:
        out_shape += [jax.ShapeDtypeStruct((1, width), F32)] * 4
    res = pl.pallas_call(body, name=name, out_shape=out_shape, scratch_shapes=[pltpu.VMEM((PACK_ROWS, PACK_W), F32)])(
        gathered, *flat_in)
    return res[0], res[1], [res[2 + 4 * i:6 + 4 * i] for i in range(len(REPL_W))]


def _cols_full(g):
    return jnp.transpose(g, (1, 0, 2)).reshape(g.shape[1], -1)


def _cols_split(full):
    k_dim, n_dim = full.shape
    return jnp.transpose(full.reshape(k_dim, N_DEV, n_dim // N_DEV), (1, 0, 2))


PROJ_BLOCK = {"z": (1024, 0), "dt": (LANE, 8), "q_c": (MLA_Q_RANK, 3), "xbc": (SSD_XBC, 1), "kv_c": (MLA_KV_RANK, 12),
              "k_rope": (LANE, 26)}


def _win_pad(wt):
    z = lambda n: jnp.zeros((n, wt.shape[1]), wt.dtype)
    return jnp.concatenate([wt[:1024], wt[2560:2576], z(112), wt[2576:2960], wt[1024:2560], wt[2960:3216], wt[3216:3248],
                            z(96)], axis=0)


def _win_unpad(wt):
    return jnp.concatenate([wt[:1024], wt[1536:3072], wt[1024:1040], wt[1152:1536], wt[3072:3328], wt[3328:3360]], axis=0)


def _heads_split_t(wt, a, b):
    w3 = wt.reshape(MLA_HEADS, a + b, wt.shape[1])
    return jnp.concatenate([w3[:, :a].reshape(-1, wt.shape[1]), w3[:, a:].reshape(-1, wt.shape[1])], axis=0)


def _heads_merge_t(wt, a, b):
    wa = wt[:MLA_HEADS * a].reshape(MLA_HEADS, a, wt.shape[1])
    wb = wt[MLA_HEADS * a:].reshape(MLA_HEADS, b, wt.shape[1])
    return jnp.concatenate([wa, wb], axis=1).reshape(-1, wt.shape[1])


def _heads_split(w, a, b):
    k_dim = w.shape[0]
    w3 = w.reshape(k_dim, MLA_HEADS, a + b)
    return jnp.concatenate([w3[:, :, :a].reshape(k_dim, -1), w3[:, :, a:].reshape(k_dim, -1)], axis=1)


def _heads_merge(w, a, b):
    k_dim = w.shape[0]
    wa = w[:, :MLA_HEADS * a].reshape(k_dim, MLA_HEADS, a)
    wb = w[:, MLA_HEADS * a:].reshape(k_dim, MLA_HEADS, b)
    return jnp.concatenate([wa, wb], axis=2).reshape(k_dim, -1)


def _pad_lanes(v, width=LANE):
    return jnp.concatenate([v, jnp.zeros((v.shape[0], width - v.shape[1]), v.dtype)], axis=1)


def _local_step(x, p, positions, tgt, W, P, comm=None):
    comm = comm or {}
    zero_tok = jnp.zeros((8, LANE), F32)
    s_dim = x.shape[0]
    inv_freq = 1.0 / (ROPE_BASE ** (jnp.arange(0, MLA_ROPE, 2, dtype=F32) / MLA_ROPE))
    ang = positions.astype(F32)[:, None] * inv_freq
    cos, sin = jnp.cos(ang), jnp.sin(ang)
    cos32 = jnp.concatenate([cos, cos], axis=1)
    sin32 = jnp.concatenate([-sin, sin], axis=1)
    cos512, sin512 = jnp.tile(cos32, (1, 16)), jnp.tile(sin32, (1, 16))
    cos128, sin128 = jnp.tile(cos32, (1, 4)), jnp.tile(sin32, (1, 4))
    bias_p, alog_p = _pad_lanes(P["ssd_dt_bias"]), _pad_lanes(P["ssd_A_log"])
    d_x = jnp.repeat(P["ssd_D"], SSD_HEAD_DIM, axis=1)

    xb, pb = x.astype(BF16), p.astype(BF16)
    proj = _mm(xb, W["w_in"], tb=True, after=comm.get("token0", zero_tok), name="mm_in")
    z, qc, kvc, kr = [(proj,) + PROJ_BLOCK[n] for n in ("z", "q_c", "kv_c", "k_rope")]
    xbca = _conv_fwd(proj, PROJ_BLOCK["xbc"][1], P["ssd_conv_w"], P["ssd_conv_b"])
    y, states = _ssd_fwd(xbca, proj, PROJ_BLOCK["dt"][1], bias_p, alog_p, d_x)
    (yssd,) = _rowwise(_gate_rms, [y, z], [P["ssd_norm_w"]], [(1024, BF16)], name="ssd_gate_norm")
    qn, kvn, krt = _rowwise(lambda a, c, u, cs, sn, wq, wkv: (_rms(a, wq), _rms(c, wkv), _spread4(_rope_fwd_fn(u, cs, sn))),
                            [qc, kvc, kr, cos128, sin128], [P["mla_q_norm_w"], P["mla_kv_norm_w"]],
                            [(MLA_Q_RANK, BF16), (MLA_KV_RANK, BF16), LANE], name="qkv_norm_rope_k")
    q = _mm(qn, W["mla_w_q_b"], tb=True, name="mm_q")
    kv = _mm(kvn, W["mla_w_kv_b"], name="mm_kv")
    (qr,) = _rowwise(_rope_fwd_fn, [(q, 512, 2), cos512, sin512], [], [512], name="rope_q")
    att = _att_fwd(q, qr, kv, krt)
    (ymla,) = _rowwise(_rms, [att], [P["mla_out_norm_w"]], [(1024, BF16)], name="out_norm")
    ycat = jnp.concatenate([yssd, ymla], axis=1)
    if "late_weights" in comm:
        W = {**W, **comm["late_weights"]("out", ycat)}
    mix = _mm(ycat, W["w_out"], name="mm_out")
    f_h1 = lambda xv, mv, g, b: _ln(ALPHA * xv + mv, g, b)
    h1, h1b = _rowwise(lambda *a: (f_h1(*a),) * 2, [x, mix], [P["ln_mix_g"], P["ln_mix_b"]], [1024, (1024, BF16)],
                       name="ln_mix")
    if "late_weights" in comm:
        W = {**W, **comm["late_weights"]("ffn", h1b)}
    hg = _mm(h1b, W["w_ffn_gate"], tb=True, out_dtype=BF16, name="mm_gate")
    hu, act = _mm(h1b, W["w_ffn_up"], tb=True, name="mm_up",
                  epilogue=(lambda u, g: (u, _silu(g.astype(F32)) * u), [hg], [BF16, BF16]))
    pg = _mm(h1b, W["w_ple_gate"], name="mm_ple_gate")
    pp = _mm(pb, W["w_ple_proj"], name="mm_ple")
    ffn = _mm(act, W["w_ffn_down"], name="mm_down")

    def final_fn(hv, fv, pg, ppv, tv, g, b):
        sg = _sigmoid(pg)
        n, r = _ln_parts(ALPHA * hv + fv + sg * ppv)
        diff = n * g + b - tv
        loss = 0.5 * jnp.sum(jnp.mean(diff * diff, axis=-1, keepdims=True), axis=0, keepdims=True)
        d_pre, d_g, d_b = _ln_bwd(n, r, g, diff * (1.0 / D_MODEL))
        return (ALPHA * d_pre, d_pre, d_pre * ppv * (sg * (1.0 - sg)), d_pre * sg, d_g, d_b,
                jnp.broadcast_to(loss, (1, LANE)))

    dh1_a, dffn, dpg, dpp, g_ffn_g, g_ffn_b, loss = _rowwise(
        final_fn, [h1, ffn, pg, pp, tgt], [P["ln_ffn_g"], P["ln_ffn_b"]], [1024] + [(1024, BF16)] * 3,
        [1024, 1024, LANE], name="final")

    G = {}
    def swiglu_bwd(d, g, u):
        g, u = g.astype(F32), u.astype(F32)
        sg = _sigmoid(g)
        return d * u * (sg * (1.0 + g * (1.0 - sg))), d * (g * sg)

    dg, du = _mm(dffn, W["w_ffn_down"], tb=True, name="mm_down_dx",
                 epilogue=(swiglu_bwd, [hg, hu], [BF16, BF16]))
    G["w_ffn_down"] = _mm(act, dffn, ta=True, out_dtype=GRAD_DT, name="mm_down_dw")
    dh1 = _mm(dg, W["w_ffn_gate"], add=dh1_a, name="mm_gate_dx")
    dh1 = _mm(du, W["w_ffn_up"], add=dh1, name="mm_up_dx")
    dh1 = _mm(dpg, W["w_ple_gate"], tb=True, add=dh1, name="mm_ple_gate_dx")
    G["w_ffn_gate"] = _mm(dg, h1b, ta=True, out_dtype=GRAD_DT, name="mm_gate_dw")
    G["w_ffn_up"] = _mm(du, h1b, ta=True, out_dtype=GRAD_DT, name="mm_up_dw")
    G["w_ple_gate"] = _mm(h1b, dpg, ta=True, out_dtype=GRAD_DT, name="mm_ple_gate_dw")
    G["w_ple_proj"] = _mm(pb, dpp, ta=True, out_dtype=GRAD_DT, name="mm_ple_dw")
    def ln_mix_bwd(xv, mv, dv, g, b):
        n, r = _ln_parts(ALPHA * xv + mv)
        d_pre, d_g, d_b = _ln_bwd(n, r, g, dv)
        return ALPHA * d_pre, d_pre, d_g, d_b

    dx_a, dmix, g_mix_g, g_mix_b = _rowwise(ln_mix_bwd, [x, mix, dh1], [P["ln_mix_g"], P["ln_mix_b"]],
                                            [1024, (1024, BF16)], [1024, 1024], name="ln_mix_bwd")
    dycat = _mm(dmix, W["w_out"], tb=True, name="mm_out_dx")
    G["w_out"] = _mm(ycat, dmix, ta=True, out_dtype=GRAD_DT, name="mm_out_dw")

    grads_done = comm.get("grads", lambda group, grads: zero_tok)
    tok1 = grads_done("ffn", G)
    datt, g_out_norm = _rowwise(lambda a, dv, w, t: _rms_bwd(a, w, dv + jnp.min(t)), [att, (dycat, 1024, 1)],
                                [P["mla_out_norm_w"], tok1], [1024], [1024], name="out_norm_bwd")
    dqn_nope, dqr, dkn, dv, dkrt = _att_bwd(q, qr, kv, krt, att, datt)
    dkv = jnp.concatenate([dkn, dv], axis=1)
    (dq_rope,) = _rowwise(lambda d0, d1, c, s: _rope_bwd_fn(d0 + d1, c, s), [(dqr, 512, 0), (dqr, 512, 1), cos512, sin512],
                          [], [(512, BF16)], name="rope_q_bwd")

    def rope_k_bwd(*a):
        d = _spread4(functools.reduce(lambda u, w: u + w, a[:-2]))
        lane = lax.broadcasted_iota(jnp.int32, d.shape, 1)
        return _rope_bwd_fn(jnp.where(lane < MLA_ROPE, d, 0.0), a[-2], a[-1])

    (dkr,) = _rowwise(rope_k_bwd, [(dkrt, LANE, k) for k in range(MLA_HEADS // 2)] + [cos128, sin128], [], [(LANE, BF16)],
                      name="rope_k_bwd")
    dq = jnp.concatenate([dqn_nope, dq_rope], axis=1)
    dqn = _mm(dq, W["mla_w_q_b"], name="mm_q_dx")
    G["mla_w_q_b"] = _mm(dq, qn, ta=True, out_dtype=GRAD_DT, name="mm_q_dw")
    dkvn = _mm(dkv, W["mla_w_kv_b"], tb=True, name="mm_kv_dx")
    G["mla_w_kv_b"] = _mm(kvn, dkv, ta=True, out_dtype=GRAD_DT, name="mm_kv_dw")
    tok2 = grads_done("mla", G)
    def qkv_norm_bwd(a, da, c, dc, wq, wkv, t):
        (d_a, d_wq), (d_c, d_wkv) = _rms_bwd(a, wq, da + jnp.min(t)), _rms_bwd(c, wkv, dc)
        return d_a, d_c, d_wq, d_wkv

    dqc, dkvc, g_q_norm, g_kv_norm = _rowwise(
        qkv_norm_bwd, [qc, dqn, kvc, dkvn], [P["mla_q_norm_w"], P["mla_kv_norm_w"], tok2],
        [(MLA_Q_RANK, BF16), (MLA_KV_RANK, BF16)], [MLA_Q_RANK, MLA_KV_RANK], name="qkv_norm_bwd")

    def gate_rms_bwd(yv, zv, dv, w, t):
        sg = _sigmoid(zv)
        silu = zv * sg
        gated = yv * silu
        r = lax.rsqrt(jnp.mean(gated * gated, axis=-1, keepdims=True) + EPS)
        n = gated * r
        dv = dv + jnp.min(t)
        g = dv * w
        d_gated = r * (g - n * jnp.mean(g * n, axis=-1, keepdims=True))
        return d_gated * silu, d_gated * yv * (sg * (1.0 + zv * (1.0 - sg))), _colsum(dv * n)

    dy, dz, g_ssd_norm = _rowwise(gate_rms_bwd, [y, z, (dycat, 1024, 0)], [P["ssd_norm_w"], tok1], [1024, (1024, BF16)],
                                  [1024], name="ssd_gate_norm_bwd")
    dxbca, ddtr, g_dt_bias, g_alog, g_d = _ssd_bwd(xbca, proj, PROJ_BLOCK["dt"][1], bias_p, alog_p, d_x, states, dy)
    da, g_conv_w, g_conv_b = _conv_bwd_pre(proj, PROJ_BLOCK["xbc"][1], P["ssd_conv_w"], P["ssd_conv_b"], dxbca)
    dxbc = _conv_bwd_in(da, P["ssd_conv_w"])

    small = {
        "ssd_conv_b": g_conv_b, "ssd_dt_bias": g_dt_bias, "ssd_A_log": g_alog, "ssd_D": g_d, "ssd_norm_w": g_ssd_norm,
        "mla_q_norm_w": g_q_norm, "mla_kv_norm_w": g_kv_norm, "mla_out_norm_w": g_out_norm, "ln_mix_g": g_mix_g,
        "ln_mix_b": g_mix_b, "ln_ffn_g": g_ffn_g, "ln_ffn_b": g_ffn_b,
    }
    dproj = jnp.concatenate([dz, ddtr, dqc, dxbc, dkvc, dkr], axis=1)
    G["w_in"] = _mm(dproj, xb, ta=True, out_dtype=GRAD_DT, name="mm_in_dw")
    sent = grads_done("in", G)
    packed = _pack_small(g_conv_w, [small[n] for n, _ in REPL_W], loss)
    if "small" in comm:
        comm["small"](packed, sent)
    grad_x = _mm(dproj, W["w_in"], add=dx_a, after=sent, name="mm_in_dx")
    return grad_x, G, packed


def kernel(x, p, positions, w_in, ssd_conv_w, ssd_conv_b, ssd_dt_bias, ssd_A_log, ssd_D, ssd_norm_w, mla_q_norm_w, mla_w_q_b, mla_kv_norm_w, mla_w_kv_b, mla_out_norm_w, w_out, ln_mix_g, ln_mix_b, w_ffn_gate, w_ffn_up, w_ffn_down, w_ple_gate, w_ple_proj, ln_ffn_g, ln_ffn_b, loss_target, m_w_in, m_ssd_conv_w, m_ssd_conv_b, m_ssd_dt_bias, m_ssd_A_log, m_ssd_D, m_ssd_norm_w, m_mla_q_norm_w, m_mla_w_q_b, m_mla_kv_norm_w, m_mla_w_kv_b, m_mla_out_norm_w, m_w_out, m_ln_mix_g, m_ln_mix_b, m_w_ffn_gate, m_w_ffn_up, m_w_ffn_down, m_w_ple_gate, m_w_ple_proj, m_ln_ffn_g, m_ln_ffn_b, v_w_in, v_ssd_conv_w, v_ssd_conv_b, v_ssd_dt_bias, v_ssd_A_log, v_ssd_D, v_ssd_norm_w, v_mla_q_norm_w, v_mla_w_q_b, v_mla_kv_norm_w, v_mla_w_kv_b, v_mla_out_norm_w, v_w_out, v_ln_mix_g, v_ln_mix_b, v_w_ffn_gate, v_w_ffn_up, v_w_ffn_down, v_w_ple_gate, v_w_ple_proj, v_ln_ffn_g, v_ln_ffn_b):
    args = dict(locals())
    core = lax.axis_index("c")
    me = 4 * lax.axis_index("x") + 2 * lax.axis_index("y") + core

    conv_sh = ssd_conv_w[0]
    conv_hi = conv_sh.astype(BF16)
    conv_lo = (conv_sh - conv_hi.astype(F32)).astype(BF16)
    stored = lambda n, pre="": jnp.transpose(args[pre + n][0]) if n in TRANSPOSED else args[pre + n][0]
    shards = {n: stored(n).astype(BF16) for n in BIG}
    rows_full = lambda g: g.reshape(-1, g.shape[2])

    early = _gather_many([shards[n] for n in EARLY] + [jnp.concatenate([conv_hi, conv_lo], axis=0)], "gather_early")
    gw = dict(zip(EARLY, early[:-1]))
    conv_g = early[-1].astype(F32)
    W = {
        "w_in": _win_pad(rows_full(gw["w_in"])),
        "mla_w_q_b": _heads_split_t(rows_full(gw["mla_w_q_b"]), MLA_NOPE, MLA_ROPE),
        "mla_w_kv_b": _heads_split(_cols_full(gw["mla_w_kv_b"]), MLA_NOPE, MLA_V),
    }
    P = {n: args[n] for n, _ in REPL_W}
    P["ssd_conv_w"] = _cols_full(conv_g[:, :4] + conv_g[:, 4:])

    late, after = {}, early[0]
    for group, names in LATE.items():
        lands = [lax.dynamic_update_slice(lax.empty((N_DEV,) + shards[n].shape, BF16), shards[n][None], (me, 0, 0)) for n in names]
        late[group] = _split_start([shards[n] for n in names], lands, _plan_broadcast, N_DEV - 1,
                                   "gather_" + group + "_start", after=after)
        after = late[group][4]

    def late_weights(group, after):
        _, got = _split_wait(*late[group][:4], after, _plan_broadcast, "gather_" + group + "_wait")
        return {n: _cols_full(g) if n == "w_ple_proj" else rows_full(g) for n, g in zip(LATE[group], got)}

    def to_blocks(n, g):
        if n == "w_in":
            g = _win_unpad(g)
        elif n == "mla_w_q_b":
            g = _heads_merge_t(g, MLA_NOPE, MLA_ROPE)
        elif n == "mla_w_kv_b":
            g = _heads_merge(g, MLA_NOPE, MLA_V)
        if n in ROW_SHARDED or n in TRANSPOSED:
            return g.reshape(N_DEV, -1, g.shape[1])
        return _cols_split(g)

    flight = {}

    def grads(group, G):
        gl = [to_blocks(n, G[n]) for n in GRAD_GROUPS[group]]
        flight[group] = _split_start(gl, [lax.empty(g.shape, g.dtype) for g in gl], _plan_scatter, N_DEV - 1,
                                     "grads_" + group + "_start")
        return flight[group][4]

    def small(packed, after):
        land = lax.dynamic_update_slice(lax.empty((N_DEV,) + packed.shape, F32), packed[None], (me, 0, 0))
        flight["small"] = _split_start([packed], [land], _plan_broadcast, N_DEV - 1, "small_start", after=after)

    grad_x, G, packed = _local_step(x[0], p[0, 0], positions[0], loss_target[0], W, P,
                                    comm={"token0": after, "late_weights": late_weights, "grads": grads, "small": small})

    me_arr = me.astype(jnp.int32).reshape(1)
    big_out = {}

    def finish(group, after):
        mine, recv = _split_wait(*flight[group][:4], after, _plan_scatter, "grads_" + group + "_wait")
        for n, g, r in zip(GRAD_GROUPS[group], mine, recv):
            big_out[n] = _adam(r, stored(n), stored(n, "m_"), stored(n, "v_"), "adam_" + n, own=g, own_idx=me_arr)
        return big_out[GRAD_GROUPS[group][-1]][0]

    done = finish("ffn", grad_x)
    _, (small_all,) = _split_wait(*flight["small"][:4], done, _plan_broadcast, "small_wait")
    conv_sum, loss_row, small_out = _adam_small(small_all, [(args[n], args["m_" + n], args["v_" + n]) for n, _ in REPL_W])
    finish("in", finish("mla", done))
    conv_grad = lax.dynamic_slice_in_dim(conv_sum, me * 192, 192, axis=1)
    conv_out = _adam(conv_grad[None], conv_sh, m_ssd_conv_w[0], v_ssd_conv_w[0], "adam_conv")
    small_map = {n: small_out[i] for i, (n, _) in enumerate(REPL_W)}

    def outputs(idx):
        res = []
        for n in WEIGHT_ORDER:
            if n == "ssd_conv_w":
                res.append(conv_out[idx][None])
            elif n in big_out:
                res.append((jnp.transpose(big_out[n][idx]) if n in TRANSPOSED else big_out[n][idx])[None])
            else:
                res.append(small_map[n][idx])
        return res

    return (loss_row[0, 0], grad_x[None], *outputs(0), *outputs(1), *outputs(2), *outputs(3))
```
